```python
import jax, jax.numpy as jnp
from jax import lax
import numpy as np

D_MODEL = 1024
BATCH = 8
SEQ = 4096
DEPTH = 2

CHUNK = 64
LN_EPS = 1e-5
GMLP_HEADS = 4
GMLP_WIDTH = D_MODEL
GMLP_HEAD_DIM = GMLP_WIDTH // GMLP_HEADS
GMLP_BLOCK = 128
POOL_WINDOWS = (2, 4, 8, 16)
POOL_GROUPS = len(POOL_WINDOWS)
POOL_WIDTH = D_MODEL
POOL_GROUP_DIM = POOL_WIDTH // POOL_GROUPS
EVEN_IN = 3 * GMLP_WIDTH + 2 * POOL_WIDTH
EVEN_MIX = GMLP_WIDTH + POOL_WIDTH
MLA_HEADS = 16
MLA_NOPE = 128
MLA_ROPE = 64
MLA_V = 128
MLA_Q_RANK = 256
MLA_KV_RANK = 128
MLA_WIDTH = MLA_HEADS * MLA_V
ODD_IN = MLA_Q_RANK + MLA_KV_RANK + MLA_ROPE + MLA_WIDTH
ROPE_THETA = 10000.0
Q_BLOCK = 128
ATTN_SCALE = (MLA_NOPE + MLA_ROPE) ** -0.5
DEEPNORM_ALPHA = (2.0 * DEPTH) ** 0.25
DEEPNORM_BETA = (8.0 * DEPTH) ** -0.25
N_EVEN = (DEPTH + 1) // 2
N_ODD = DEPTH // 2

kernel_name = "hybrid_gmlp_pool_mla_deepnorm_adaln"


def layer_norm(x, g, b):
    xf = x.astype(jnp.float32)
    mu = jnp.mean(xf, axis=-1, keepdims=True)
    var = jnp.mean(jnp.square(xf - mu), axis=-1, keepdims=True)
    return ((xf - mu) * lax.rsqrt(var + LN_EPS) * g + b).astype(x.dtype)


def rms_norm(x, g):
    xf = x.astype(jnp.float32)
    ms = jnp.mean(jnp.square(xf), axis=-1, keepdims=True)
    return (xf * lax.rsqrt(ms + LN_EPS) * g).astype(x.dtype)


def rope_cos_sin(positions):
    inv = 1.0 / (ROPE_THETA ** (jnp.arange(0, MLA_ROPE, 2, dtype=jnp.float32) / MLA_ROPE))
    ang = positions.astype(jnp.float32)[..., None] * inv
    return jnp.cos(ang), jnp.sin(ang)


def apply_rope(x, cos, sin):
    half = x.shape[-1] // 2
    x1 = x[..., :half].astype(jnp.float32)
    x2 = x[..., half:].astype(jnp.float32)
    return jnp.concatenate([x1 * cos - x2 * sin, x2 * cos + x1 * sin], axis=-1).astype(x.dtype)


def gmlp_spatial_unit(u, v, norm_g, norm_b, ws, bs):
    B, S, _ = u.shape
    nb = S // GMLP_BLOCK
    v = layer_norm(v.reshape(B, S, GMLP_HEADS, GMLP_HEAD_DIM), norm_g, norm_b)
    v = v.reshape(B, nb, GMLP_BLOCK, GMLP_HEADS, GMLP_HEAD_DIM)
    pos_chunk = jnp.arange(GMLP_BLOCK) // CHUNK
    mask = pos_chunk[None, :] <= pos_chunk[:, None]
    w = jnp.where(mask[None], ws, jnp.zeros_like(ws))
    sv = jnp.einsum('hts,bnshd->bnthd', w, v) + bs.T[:, :, None]
    return u * sv.reshape(B, S, GMLP_WIDTH)


def multiscale_pool(xb, pool_w, pool_b, pool_scale):
    B, S, _ = xb.shape
    xg = xb.reshape(B, S, POOL_GROUPS, POOL_GROUP_DIM).astype(jnp.float32)
    cs = jnp.cumsum(xg, axis=1)
    t = jnp.arange(S)
    means = []
    for g, win in enumerate(POOL_WINDOWS):
        csg = cs[:, :, g]
        lagged = jnp.concatenate([jnp.zeros((B, win, POOL_GROUP_DIM), csg.dtype), csg[:, :S - win]], axis=1)
        cnt = jnp.minimum(t + 1, win).astype(jnp.float32)
        means.append((csg - lagged) / cnt[None, :, None])
    pooled = jnp.stack(means, axis=2) - xg
    y = jnp.einsum('bsgd,gde->bsge', pooled.astype(xb.dtype), pool_w).reshape(B, S, POOL_WIDTH)
    return (y + pool_b) * pool_scale


def even_mixer(h, w_in, gmlp_norm_g, gmlp_norm_b, gmlp_ws, gmlp_bs, pool_w, pool_b, pool_scale, w_out):
    proj = h @ w_in
    u, v, z_a, x_b, z_b = jnp.split(proj, [GMLP_WIDTH, 2 * GMLP_WIDTH, 3 * GMLP_WIDTH,
                                           3 * GMLP_WIDTH + POOL_WIDTH], axis=-1)
    a = gmlp_spatial_unit(u, v, gmlp_norm_g, gmlp_norm_b, gmlp_ws, gmlp_bs) * jax.nn.silu(z_a)
    b = multiscale_pool(x_b, pool_w, pool_b, pool_scale) * jax.nn.silu(z_b)
    return jnp.concatenate([a, b], axis=-1) @ w_out


def mla_mixer(h, positions, w_in, q_norm_g, kv_norm_g, w_uq, w_uk, w_uv, w_out):
    B, S, _ = h.shape
    proj = h @ w_in
    q_c, kv_c, k_r, z = jnp.split(proj, [MLA_Q_RANK, MLA_Q_RANK + MLA_KV_RANK,
                                         MLA_Q_RANK + MLA_KV_RANK + MLA_ROPE], axis=-1)
    q_c = rms_norm(q_c, q_norm_g)
    kv_c = rms_norm(kv_c, kv_norm_g)
    q = jnp.einsum('bsr,rhd->bshd', q_c, w_uq)
    q_nope, q_rope = q[..., :MLA_NOPE], q[..., MLA_NOPE:]
    cos, sin = rope_cos_sin(positions)
    q_rope = apply_rope(q_rope, cos[:, :, None, :], sin[:, :, None, :])
    k_rope = apply_rope(k_r, cos, sin)
    q_lat = jnp.einsum('bshd,rhd->bshr', q_nope, w_uk)
    nb = S // Q_BLOCK
    q_lat_b = q_lat.reshape(B, nb, Q_BLOCK, MLA_HEADS, MLA_KV_RANK).transpose(1, 0, 2, 3, 4)
    q_rope_b = q_rope.reshape(B, nb, Q_BLOCK, MLA_HEADS, MLA_ROPE).transpose(1, 0, 2, 3, 4)
    key_chunk = jnp.arange(S) // CHUNK

    def attend_block(args):
        ql, qr, i = args
        s = jnp.einsum('bqhr,bkr->bhqk', ql, kv_c) + jnp.einsum('bqhd,bkd->bhqk', qr, k_rope)
        s = s.astype(jnp.float32) * ATTN_SCALE
        q_chunk = (i * Q_BLOCK + jnp.arange(Q_BLOCK)) // CHUNK
        mask = key_chunk[None, :] <= q_chunk[:, None]
        p = jax.nn.softmax(jnp.where(mask, s, -jnp.inf), axis=-1).astype(kv_c.dtype)
        return jnp.einsum('bhqk,bkr->bqhr', p, kv_c)

    o_lat = lax.map(attend_block, (q_lat_b, q_rope_b, jnp.arange(nb)))
    o_lat = o_lat.transpose(1, 0, 2, 3, 4).reshape(B, S, MLA_HEADS, MLA_KV_RANK)
    o = jnp.einsum('bshr,rhd->bshd', o_lat, w_uv).reshape(B, S, MLA_WIDTH)
    return (o * jax.nn.silu(z)) @ w_out


def _fwd_setup_inputs(seed: int = 0) -> dict:
    key = jax.random.key(seed)
    ks = jax.random.split(key, 24)
    f32 = jnp.float32

    def nrm(k, shape, s):
        return s * jax.random.normal(k, shape, f32)

    x = nrm(ks[0], (BATCH, SEQ, D_MODEL), 1.0)
    c = nrm(ks[1], (BATCH, D_MODEL), 1.0)
    offs = jax.random.randint(ks[2], (BATCH, 1), 0, 4096, dtype=jnp.int32)
    positions = offs + jnp.arange(SEQ, dtype=jnp.int32)[None, :]
    ada_w = nrm(ks[3], (DEPTH, D_MODEL, 3 * D_MODEL), 0.1 * D_MODEL ** -0.5)
    ada_b = nrm(ks[4], (DEPTH, 3 * D_MODEL), 0.01)
    ln_g = 1.0 + nrm(ks[5], (DEPTH, D_MODEL), 0.02)
    ln_b = nrm(ks[6], (DEPTH, D_MODEL), 0.02)
    e_w_in = nrm(ks[7], (N_EVEN, D_MODEL, EVEN_IN), D_MODEL ** -0.5)
    gmlp_norm_g = 1.0 + nrm(ks[8], (N_EVEN, GMLP_HEAD_DIM), 0.02)
    gmlp_norm_b = nrm(ks[9], (N_EVEN, GMLP_HEAD_DIM), 0.02)
    gmlp_ws = nrm(ks[10], (N_EVEN, GMLP_HEADS, GMLP_BLOCK, GMLP_BLOCK), 0.5 * GMLP_BLOCK ** -0.5)
    gmlp_bs = 1.0 + nrm(ks[11], (N_EVEN, GMLP_HEADS, GMLP_BLOCK), 0.02)
    pool_w = nrm(ks[12], (N_EVEN, POOL_GROUPS, POOL_GROUP_DIM, POOL_GROUP_DIM), POOL_GROUP_DIM ** -0.5)
    pool_b = nrm(ks[13], (N_EVEN, POOL_WIDTH), 0.01)
    pool_scale = 1.0 + nrm(ks[14], (N_EVEN, POOL_WIDTH), 0.1)
    e_w_out = nrm(ks[15], (N_EVEN, EVEN_MIX, D_MODEL), DEEPNORM_BETA * EVEN_MIX ** -0.5)
    o_w_in = nrm(ks[16], (N_ODD, D_MODEL, ODD_IN), D_MODEL ** -0.5)
    mla_q_norm_g = 1.0 + nrm(ks[17], (N_ODD, MLA_Q_RANK), 0.02)
    mla_kv_norm_g = 1.0 + nrm(ks[18], (N_ODD, MLA_KV_RANK), 0.02)
    mla_w_uq = nrm(ks[19], (N_ODD, MLA_Q_RANK, MLA_HEADS, MLA_NOPE + MLA_ROPE), MLA_Q_RANK ** -0.5)
    mla_w_uk = nrm(ks[20], (N_ODD, MLA_KV_RANK, MLA_HEADS, MLA_NOPE), MLA_KV_RANK ** -0.5)
    mla_w_uv = nrm(ks[21], (N_ODD, MLA_KV_RANK, MLA_HEADS, MLA_V), MLA_KV_RANK ** -0.5)
    o_w_out = nrm(ks[22], (N_ODD, MLA_WIDTH, D_MODEL), DEEPNORM_BETA * MLA_WIDTH ** -0.5)
    return {"x": x, "c": c, "positions": positions, "ada_w": ada_w, "ada_b": ada_b,
            "ln_g": ln_g, "ln_b": ln_b, "e_w_in": e_w_in, "gmlp_norm_g": gmlp_norm_g,
            "gmlp_norm_b": gmlp_norm_b, "gmlp_ws": gmlp_ws, "gmlp_bs": gmlp_bs,
            "pool_w": pool_w, "pool_b": pool_b, "pool_scale": pool_scale, "e_w_out": e_w_out,
            "o_w_in": o_w_in, "mla_q_norm_g": mla_q_norm_g, "mla_kv_norm_g": mla_kv_norm_g,
            "mla_w_uq": mla_w_uq, "mla_w_uk": mla_w_uk, "mla_w_uv": mla_w_uv, "o_w_out": o_w_out}


def _fwd_reference(x, c, positions, ada_w, ada_b, ln_g, ln_b, e_w_in, gmlp_norm_g, gmlp_norm_b,
              gmlp_ws, gmlp_bs, pool_w, pool_b, pool_scale, e_w_out, o_w_in, mla_q_norm_g,
              mla_kv_norm_g, mla_w_uq, mla_w_uk, mla_w_uv, o_w_out):
    cond = jax.nn.silu(c)
    for l in range(DEPTH):
        mod = cond @ ada_w[l] + ada_b[l]
        shift, scale, gate = jnp.split(mod, 3, axis=-1)
        h = x * (1.0 + scale[:, None, :]) + shift[:, None, :]
        if l % 2 == 0:
            e = l // 2
            y = even_mixer(h, e_w_in[e], gmlp_norm_g[e], gmlp_norm_b[e], gmlp_ws[e], gmlp_bs[e],
                           pool_w[e], pool_b[e], pool_scale[e], e_w_out[e])
        else:
            o = l // 2
            y = mla_mixer(h, positions, o_w_in[o], mla_q_norm_g[o], mla_kv_norm_g[o],
                          mla_w_uq[o], mla_w_uk[o], mla_w_uv[o], o_w_out[o])
        x = layer_norm(DEEPNORM_ALPHA * x + (1.0 + gate[:, None, :]) * y, ln_g[l], ln_b[l])
    return x


import jax as _jax
import jax.numpy as _jnp

TWIN_FORMAT = 'train_step'
FWD_PARAMS = ['x', 'c', 'positions', 'ada_w', 'ada_b', 'ln_g', 'ln_b', 'e_w_in', 'gmlp_norm_g', 'gmlp_norm_b', 'gmlp_ws', 'gmlp_bs', 'pool_w', 'pool_b', 'pool_scale', 'e_w_out', 'o_w_in', 'mla_q_norm_g', 'mla_kv_norm_g', 'mla_w_uq', 'mla_w_uk', 'mla_w_uv', 'o_w_out']
TWIN_WEIGHTS = ['ada_w', 'ada_b', 'ln_g', 'ln_b', 'e_w_in', 'gmlp_norm_g', 'gmlp_norm_b', 'gmlp_ws', 'gmlp_bs', 'pool_w', 'pool_b', 'pool_scale', 'e_w_out', 'o_w_in', 'mla_q_norm_g', 'mla_kv_norm_g', 'mla_w_uq', 'mla_w_uk', 'mla_w_uv', 'o_w_out']
TWIN_DIFF_INPUT = 'x'
TWIN_INPUTS = ['x', 'c', 'positions', 'ada_w', 'ada_b', 'ln_g', 'ln_b', 'e_w_in', 'gmlp_norm_g', 'gmlp_norm_b', 'gmlp_ws', 'gmlp_bs', 'pool_w', 'pool_b', 'pool_scale', 'e_w_out', 'o_w_in', 'mla_q_norm_g', 'mla_kv_norm_g', 'mla_w_uq', 'mla_w_uk', 'mla_w_uv', 'o_w_out', 'loss_target', 'm_ada_w', 'm_ada_b', 'm_ln_g', 'm_ln_b', 'm_e_w_in', 'm_gmlp_norm_g', 'm_gmlp_norm_b', 'm_gmlp_ws', 'm_gmlp_bs', 'm_pool_w', 'm_pool_b', 'm_pool_scale', 'm_e_w_out', 'm_o_w_in', 'm_mla_q_norm_g', 'm_mla_kv_norm_g', 'm_mla_w_uq', 'm_mla_w_uk', 'm_mla_w_uv', 'm_o_w_out', 'v_ada_w', 'v_ada_b', 'v_ln_g', 'v_ln_b', 'v_e_w_in', 'v_gmlp_norm_g', 'v_gmlp_norm_b', 'v_gmlp_ws', 'v_gmlp_bs', 'v_pool_w', 'v_pool_b', 'v_pool_scale', 'v_e_w_out', 'v_o_w_in', 'v_mla_q_norm_g', 'v_mla_kv_norm_g', 'v_mla_w_uq', 'v_mla_w_uk', 'v_mla_w_uv', 'v_o_w_out']
TWIN_OUTPUTS = ['loss', 'grad_x', 'grad_ada_w', 'grad_ada_b', 'grad_ln_g', 'grad_ln_b', 'grad_e_w_in', 'grad_gmlp_norm_g', 'grad_gmlp_norm_b', 'grad_gmlp_ws', 'grad_gmlp_bs', 'grad_pool_w', 'grad_pool_b', 'grad_pool_scale', 'grad_e_w_out', 'grad_o_w_in', 'grad_mla_q_norm_g', 'grad_mla_kv_norm_g', 'grad_mla_w_uq', 'grad_mla_w_uk', 'grad_mla_w_uv', 'grad_o_w_out', 'delta_ada_w', 'delta_ada_b', 'delta_ln_g', 'delta_ln_b', 'delta_e_w_in', 'delta_gmlp_norm_g', 'delta_gmlp_norm_b', 'delta_gmlp_ws', 'delta_gmlp_bs', 'delta_pool_w', 'delta_pool_b', 'delta_pool_scale', 'delta_e_w_out', 'delta_o_w_in', 'delta_mla_q_norm_g', 'delta_mla_kv_norm_g', 'delta_mla_w_uq', 'delta_mla_w_uk', 'delta_mla_w_uv', 'delta_o_w_out', 'new_m_ada_w', 'new_m_ada_b', 'new_m_ln_g', 'new_m_ln_b', 'new_m_e_w_in', 'new_m_gmlp_norm_g', 'new_m_gmlp_norm_b', 'new_m_gmlp_ws', 'new_m_gmlp_bs', 'new_m_pool_w', 'new_m_pool_b', 'new_m_pool_scale', 'new_m_e_w_out', 'new_m_o_w_in', 'new_m_mla_q_norm_g', 'new_m_mla_kv_norm_g', 'new_m_mla_w_uq', 'new_m_mla_w_uk', 'new_m_mla_w_uv', 'new_m_o_w_out', 'new_v_ada_w', 'new_v_ada_b', 'new_v_ln_g', 'new_v_ln_b', 'new_v_e_w_in', 'new_v_gmlp_norm_g', 'new_v_gmlp_norm_b', 'new_v_gmlp_ws', 'new_v_gmlp_bs', 'new_v_pool_w', 'new_v_pool_b', 'new_v_pool_scale', 'new_v_e_w_out', 'new_v_o_w_in', 'new_v_mla_q_norm_g', 'new_v_mla_kv_norm_g', 'new_v_mla_w_uq', 'new_v_mla_w_uk', 'new_v_mla_w_uv', 'new_v_o_w_out']
TWIN_LEAF_KINDS = {'loss': 'loss', 'grad_x': 'grad_x', 'grad_ada_w': 'grad_w', 'grad_ada_b': 'grad_w', 'grad_ln_g': 'grad_w', 'grad_ln_b': 'grad_w', 'grad_e_w_in': 'grad_w', 'grad_gmlp_norm_g': 'grad_w', 'grad_gmlp_norm_b': 'grad_w', 'grad_gmlp_ws': 'grad_w', 'grad_gmlp_bs': 'grad_w', 'grad_pool_w': 'grad_w', 'grad_pool_b': 'grad_w', 'grad_pool_scale': 'grad_w', 'grad_e_w_out': 'grad_w', 'grad_o_w_in': 'grad_w', 'grad_mla_q_norm_g': 'grad_w', 'grad_mla_kv_norm_g': 'grad_w', 'grad_mla_w_uq': 'grad_w', 'grad_mla_w_uk': 'grad_w', 'grad_mla_w_uv': 'grad_w', 'grad_o_w_out': 'grad_w', 'delta_ada_w': 'delta_w', 'delta_ada_b': 'delta_w', 'delta_ln_g': 'delta_w', 'delta_ln_b': 'delta_w', 'delta_e_w_in': 'delta_w', 'delta_gmlp_norm_g': 'delta_w', 'delta_gmlp_norm_b': 'delta_w', 'delta_gmlp_ws': 'delta_w', 'delta_gmlp_bs': 'delta_w', 'delta_pool_w': 'delta_w', 'delta_pool_b': 'delta_w', 'delta_pool_scale': 'delta_w', 'delta_e_w_out': 'delta_w', 'delta_o_w_in': 'delta_w', 'delta_mla_q_norm_g': 'delta_w', 'delta_mla_kv_norm_g': 'delta_w', 'delta_mla_w_uq': 'delta_w', 'delta_mla_w_uk': 'delta_w', 'delta_mla_w_uv': 'delta_w', 'delta_o_w_out': 'delta_w', 'new_m_ada_w': 'new_m', 'new_m_ada_b': 'new_m', 'new_m_ln_g': 'new_m', 'new_m_ln_b': 'new_m', 'new_m_e_w_in': 'new_m', 'new_m_gmlp_norm_g': 'new_m', 'new_m_gmlp_norm_b': 'new_m', 'new_m_gmlp_ws': 'new_m', 'new_m_gmlp_bs': 'new_m', 'new_m_pool_w': 'new_m', 'new_m_pool_b': 'new_m', 'new_m_pool_scale': 'new_m', 'new_m_e_w_out': 'new_m', 'new_m_o_w_in': 'new_m', 'new_m_mla_q_norm_g': 'new_m', 'new_m_mla_kv_norm_g': 'new_m', 'new_m_mla_w_uq': 'new_m', 'new_m_mla_w_uk': 'new_m', 'new_m_mla_w_uv': 'new_m', 'new_m_o_w_out': 'new_m', 'new_v_ada_w': 'new_v', 'new_v_ada_b': 'new_v', 'new_v_ln_g': 'new_v', 'new_v_ln_b': 'new_v', 'new_v_e_w_in': 'new_v', 'new_v_gmlp_norm_g': 'new_v', 'new_v_gmlp_norm_b': 'new_v', 'new_v_gmlp_ws': 'new_v', 'new_v_gmlp_bs': 'new_v', 'new_v_pool_w': 'new_v', 'new_v_pool_b': 'new_v', 'new_v_pool_scale': 'new_v', 'new_v_e_w_out': 'new_v', 'new_v_o_w_in': 'new_v', 'new_v_mla_q_norm_g': 'new_v', 'new_v_mla_kv_norm_g': 'new_v', 'new_v_mla_w_uq': 'new_v', 'new_v_mla_w_uk': 'new_v', 'new_v_mla_w_uv': 'new_v', 'new_v_o_w_out': 'new_v'}


def _forward(args):
    return _fwd_reference(*[args[k] for k in FWD_PARAMS])


def _output_shape():
    out = _jax.eval_shape(lambda: _forward(_fwd_setup_inputs(0)))
    return out.shape, out.dtype

N_MICROBATCH = 1
ADAM_LR = 0.001
ADAM_B1 = 0.9
ADAM_B2 = 0.999
ADAM_EPS = 1e-08
ADAM_WD = 0.01
ADAM_STEP = 10
PER_EXAMPLE_BATCH_AXIS = {'x': 0, 'c': 0, 'positions': 0, 'loss_target': 0}
SHARED_INPUTS = []
_WEIGHT_DTYPES = {'ada_w': _jnp.float32, 'ada_b': _jnp.float32, 'ln_g': _jnp.float32, 'ln_b': _jnp.float32, 'e_w_in': _jnp.float32, 'gmlp_norm_g': _jnp.float32, 'gmlp_norm_b': _jnp.float32, 'gmlp_ws': _jnp.float32, 'gmlp_bs': _jnp.float32, 'pool_w': _jnp.float32, 'pool_b': _jnp.float32, 'pool_scale': _jnp.float32, 'e_w_out': _jnp.float32, 'o_w_in': _jnp.float32, 'mla_q_norm_g': _jnp.float32, 'mla_kv_norm_g': _jnp.float32, 'mla_w_uq': _jnp.float32, 'mla_w_uk': _jnp.float32, 'mla_w_uv': _jnp.float32, 'o_w_out': _jnp.float32}
MOMENT_SCALE = {'ada_w': 2.311381e-02, 'ada_b': 3.879304e-02, 'ln_g': 2.263326e+01, 'ln_b': 5.658812e-01, 'e_w_in': 2.417253e-02, 'gmlp_norm_g': 2.209332e-02, 'gmlp_norm_b': 2.368640e-02, 'gmlp_ws': 3.219711e-02, 'gmlp_bs': 3.770264e-02, 'pool_w': 2.317198e-02, 'pool_b': 3.200564e-02, 'pool_scale': 2.347387e-02, 'e_w_out': 7.346128e-02, 'o_w_in': 7.689986e-03, 'mla_q_norm_g': 1.113173e-02, 'mla_kv_norm_g': 2.540193e-02, 'mla_w_uq': 3.360774e-03, 'mla_w_uk': 3.527925e-03, 'mla_w_uv': 4.596099e-03, 'o_w_out': 1.295443e-02}


def _to_microbatches(a, axis):
    t = _jnp.moveaxis(a, axis, 0)
    t = t.reshape((N_MICROBATCH, t.shape[0] // N_MICROBATCH) + t.shape[1:])
    return _jnp.moveaxis(t, 1, axis + 1)


def setup_inputs(seed: int = 0) -> dict:
    inp = _fwd_setup_inputs(seed)
    key = _jax.random.fold_in(_jax.random.key(seed), 7919)
    shape, _ = _output_shape()
    out = dict(inp)
    out["loss_target"] = _jax.random.normal(_jax.random.fold_in(key, 0), shape, _jnp.float32)
    for i, name in enumerate(TWIN_WEIGHTS):
        w = inp[name].astype(_jnp.float32)
        if MOMENT_SCALE is None:
            s = _jnp.sqrt(_jnp.mean(_jnp.square(w)) + 1e-30)
        else:
            s = MOMENT_SCALE[name]
        km, kv = _jax.random.split(_jax.random.fold_in(key, i + 1))
        out[name] = w
        out["m_" + name] = s * _jax.random.normal(km, w.shape, _jnp.float32)
        out["v_" + name] = (s * s) * _jax.random.uniform(kv, w.shape, _jnp.float32, 0.5, 1.5)
    if N_MICROBATCH > 1:
        for name, axis in PER_EXAMPLE_BATCH_AXIS.items():
            out[name] = _to_microbatches(out[name], axis)
    return {'x': out['x'], 'c': out['c'], 'positions': out['positions'], 'ada_w': out['ada_w'], 'ada_b': out['ada_b'], 'ln_g': out['ln_g'], 'ln_b': out['ln_b'], 'e_w_in': out['e_w_in'], 'gmlp_norm_g': out['gmlp_norm_g'], 'gmlp_norm_b': out['gmlp_norm_b'], 'gmlp_ws': out['gmlp_ws'], 'gmlp_bs': out['gmlp_bs'], 'pool_w': out['pool_w'], 'pool_b': out['pool_b'], 'pool_scale': out['pool_scale'], 'e_w_out': out['e_w_out'], 'o_w_in': out['o_w_in'], 'mla_q_norm_g': out['mla_q_norm_g'], 'mla_kv_norm_g': out['mla_kv_norm_g'], 'mla_w_uq': out['mla_w_uq'], 'mla_w_uk': out['mla_w_uk'], 'mla_w_uv': out['mla_w_uv'], 'o_w_out': out['o_w_out'], 'loss_target': out['loss_target'], 'm_ada_w': out['m_ada_w'], 'm_ada_b': out['m_ada_b'], 'm_ln_g': out['m_ln_g'], 'm_ln_b': out['m_ln_b'], 'm_e_w_in': out['m_e_w_in'], 'm_gmlp_norm_g': out['m_gmlp_norm_g'], 'm_gmlp_norm_b': out['m_gmlp_norm_b'], 'm_gmlp_ws': out['m_gmlp_ws'], 'm_gmlp_bs': out['m_gmlp_bs'], 'm_pool_w': out['m_pool_w'], 'm_pool_b': out['m_pool_b'], 'm_pool_scale': out['m_pool_scale'], 'm_e_w_out': out['m_e_w_out'], 'm_o_w_in': out['m_o_w_in'], 'm_mla_q_norm_g': out['m_mla_q_norm_g'], 'm_mla_kv_norm_g': out['m_mla_kv_norm_g'], 'm_mla_w_uq': out['m_mla_w_uq'], 'm_mla_w_uk': out['m_mla_w_uk'], 'm_mla_w_uv': out['m_mla_w_uv'], 'm_o_w_out': out['m_o_w_out'], 'v_ada_w': out['v_ada_w'], 'v_ada_b': out['v_ada_b'], 'v_ln_g': out['v_ln_g'], 'v_ln_b': out['v_ln_b'], 'v_e_w_in': out['v_e_w_in'], 'v_gmlp_norm_g': out['v_gmlp_norm_g'], 'v_gmlp_norm_b': out['v_gmlp_norm_b'], 'v_gmlp_ws': out['v_gmlp_ws'], 'v_gmlp_bs': out['v_gmlp_bs'], 'v_pool_w': out['v_pool_w'], 'v_pool_b': out['v_pool_b'], 'v_pool_scale': out['v_pool_scale'], 'v_e_w_out': out['v_e_w_out'], 'v_o_w_in': out['v_o_w_in'], 'v_mla_q_norm_g': out['v_mla_q_norm_g'], 'v_mla_kv_norm_g': out['v_mla_kv_norm_g'], 'v_mla_w_uq': out['v_mla_w_uq'], 'v_mla_w_uk': out['v_mla_w_uk'], 'v_mla_w_uv': out['v_mla_w_uv'], 'v_o_w_out': out['v_o_w_out']}


def _loss(weights, diff, rest, loss_target):
    with _jax.named_scope("forward"):
        args = {**rest, TWIN_DIFF_INPUT: diff, **{k: w.astype(_WEIGHT_DTYPES[k]) for k, w in weights.items()}}
        y = _forward(args)
    with _jax.named_scope("loss_head"):
        err = _jnp.square(y.astype(_jnp.float32) - loss_target)
        return 0.5 * _jnp.sum(_jnp.mean(err, axis=-1)) if err.ndim else 0.5 * err


def _adamw(w, g, m, v):
    m = ADAM_B1 * m + (1.0 - ADAM_B1) * g
    v = ADAM_B2 * v + (1.0 - ADAM_B2) * _jnp.square(g)
    m_hat = m / (1.0 - ADAM_B1 ** ADAM_STEP)
    v_hat = v / (1.0 - ADAM_B2 ** ADAM_STEP)
    delta = -ADAM_LR * (m_hat / (_jnp.sqrt(v_hat) + ADAM_EPS) + ADAM_WD * w)
    return delta, m, v


def reference(x, c, positions, ada_w, ada_b, ln_g, ln_b, e_w_in, gmlp_norm_g, gmlp_norm_b, gmlp_ws, gmlp_bs, pool_w, pool_b, pool_scale, e_w_out, o_w_in, mla_q_norm_g, mla_kv_norm_g, mla_w_uq, mla_w_uk, mla_w_uv, o_w_out, loss_target, m_ada_w, m_ada_b, m_ln_g, m_ln_b, m_e_w_in, m_gmlp_norm_g, m_gmlp_norm_b, m_gmlp_ws, m_gmlp_bs, m_pool_w, m_pool_b, m_pool_scale, m_e_w_out, m_o_w_in, m_mla_q_norm_g, m_mla_kv_norm_g, m_mla_w_uq, m_mla_w_uk, m_mla_w_uv, m_o_w_out, v_ada_w, v_ada_b, v_ln_g, v_ln_b, v_e_w_in, v_gmlp_norm_g, v_gmlp_norm_b, v_gmlp_ws, v_gmlp_bs, v_pool_w, v_pool_b, v_pool_scale, v_e_w_out, v_o_w_in, v_mla_q_norm_g, v_mla_kv_norm_g, v_mla_w_uq, v_mla_w_uk, v_mla_w_uv, v_o_w_out):
    given = dict(x=x, c=c, positions=positions, ada_w=ada_w, ada_b=ada_b, ln_g=ln_g, ln_b=ln_b, e_w_in=e_w_in, gmlp_norm_g=gmlp_norm_g, gmlp_norm_b=gmlp_norm_b, gmlp_ws=gmlp_ws, gmlp_bs=gmlp_bs, pool_w=pool_w, pool_b=pool_b, pool_scale=pool_scale, e_w_out=e_w_out, o_w_in=o_w_in, mla_q_norm_g=mla_q_norm_g, mla_kv_norm_g=mla_kv_norm_g, mla_w_uq=mla_w_uq, mla_w_uk=mla_w_uk, mla_w_uv=mla_w_uv, o_w_out=o_w_out, loss_target=loss_target, m_ada_w=m_ada_w, m_ada_b=m_ada_b, m_ln_g=m_ln_g, m_ln_b=m_ln_b, m_e_w_in=m_e_w_in, m_gmlp_norm_g=m_gmlp_norm_g, m_gmlp_norm_b=m_gmlp_norm_b, m_gmlp_ws=m_gmlp_ws, m_gmlp_bs=m_gmlp_bs, m_pool_w=m_pool_w, m_pool_b=m_pool_b, m_pool_scale=m_pool_scale, m_e_w_out=m_e_w_out, m_o_w_in=m_o_w_in, m_mla_q_norm_g=m_mla_q_norm_g, m_mla_kv_norm_g=m_mla_kv_norm_g, m_mla_w_uq=m_mla_w_uq, m_mla_w_uk=m_mla_w_uk, m_mla_w_uv=m_mla_w_uv, m_o_w_out=m_o_w_out, v_ada_w=v_ada_w, v_ada_b=v_ada_b, v_ln_g=v_ln_g, v_ln_b=v_ln_b, v_e_w_in=v_e_w_in, v_gmlp_norm_g=v_gmlp_norm_g, v_gmlp_norm_b=v_gmlp_norm_b, v_gmlp_ws=v_gmlp_ws, v_gmlp_bs=v_gmlp_bs, v_pool_w=v_pool_w, v_pool_b=v_pool_b, v_pool_scale=v_pool_scale, v_e_w_out=v_e_w_out, v_o_w_in=v_o_w_in, v_mla_q_norm_g=v_mla_q_norm_g, v_mla_kv_norm_g=v_mla_kv_norm_g, v_mla_w_uq=v_mla_w_uq, v_mla_w_uk=v_mla_w_uk, v_mla_w_uv=v_mla_w_uv, v_o_w_out=v_o_w_out)
    weights = {n: given[n] for n in TWIN_WEIGHTS}
    shared = {n: given[n] for n in SHARED_INPUTS}
    per_example = {n: given[n] for n in ['x', 'c', 'positions']}
    grad_fn = _jax.value_and_grad(_loss, argnums=(0, 1))

    def one_microbatch(ex, loss_target):
        ex = dict(ex)
        diff = ex.pop(TWIN_DIFF_INPUT)
        return grad_fn(weights, diff, {**shared, **ex}, loss_target)

    if N_MICROBATCH == 1:
        loss, (grad_w, grad_x) = one_microbatch(per_example, given["loss_target"])
    else:
        def body(carry, xs):
            loss_sum, grad_sum = carry
            l_k, (gw_k, gx_k) = one_microbatch(xs[0], xs[1])
            with _jax.named_scope("update"):
                return (loss_sum + l_k, _jax.tree.map(_jnp.add, grad_sum, gw_k)), gx_k

        init = (_jnp.zeros((), _jnp.float32), _jax.tree.map(_jnp.zeros_like, weights))
        (loss, grad_w), grad_x = _jax.lax.scan(body, init, (per_example, given["loss_target"]))
    with _jax.named_scope("update"):
        delta_w, new_m, new_v = {}, {}, {}
        for n in TWIN_WEIGHTS:
            delta_w[n], new_m[n], new_v[n] = _adamw(weights[n], grad_w[n], given["m_" + n], given["v_" + n])
    return (loss, grad_x, *[grad_w[n] for n in TWIN_WEIGHTS], *[delta_w[n] for n in TWIN_WEIGHTS],
            *[new_m[n] for n in TWIN_WEIGHTS], *[new_v[n] for n in TWIN_WEIGHTS])
```

```python
import jax
import jax.numpy as jnp
from jax import lax
from jax.experimental import pallas as pl
from jax.experimental.pallas import tpu as pltpu

F32 = jnp.float32
BF16 = jnp.bfloat16
MESH = pl.DeviceIdType.MESH

D_MODEL = 1024
CHUNK = 64
LN_EPS = 1e-5
GMLP_HEADS = 4
GMLP_HEAD_DIM = 256
GMLP_BLOCK = 128
POOL_WINDOWS = (2, 4, 8, 16)
POOL_GROUPS = 4
POOL_GROUP_DIM = 256
POOL_HALO = 16
EVEN_IN = 5120
MLA_HEADS = 16
MLA_NOPE = 128
MLA_ROPE = 64
MLA_V = 128
MLA_Q_RANK = 256
MLA_KV_RANK = 128
MLA_WIDTH = MLA_HEADS * MLA_V
ODD_IN = 2496
ODD_SMALL = MLA_Q_RANK + MLA_KV_RANK + MLA_ROPE
ODD_SMALL_PAD = 512
QK_PAD = 256
ROPE_THETA = 10000.0
ATTN_SCALE = (MLA_NOPE + MLA_ROPE) ** -0.5
DEEPNORM_ALPHA = (2.0 * 2) ** 0.25
ADAM_LR = 0.001
ADAM_B1 = 0.9
ADAM_B2 = 0.999
ADAM_EPS = 1e-08
ADAM_WD = 0.01
ADAM_STEP = 10
NEG = -1e30
LANES = 128
VMEM_LIMIT_BYTES = 56 * 1024 * 1024
HBM = pl.BlockSpec(memory_space=pltpu.HBM)


def _params(*sem):
    return pltpu.CompilerParams(dimension_semantics=sem if sem else None, vmem_limit_bytes=VMEM_LIMIT_BYTES)


def _tile(dim, pref):
    for t in (pref, 512, 256, 128):
        if t <= pref and dim % t == 0:
            return t
    return dim


def _row_tile(rows, cap):
    for t in range(cap - cap % 16, 0, -16):
        if rows % t == 0:
            return t
    return rows


def _sigmoid(z):
    return 1.0 / (1.0 + jnp.exp(-z))


def _dot(a, b, dims):
    return lax.dot_general(a, b, (dims, ((), ())), preferred_element_type=F32)


NN = ((1,), (0,))
NT = ((1,), (1,))
TN = ((0,), (0,))


def _matmul(a, b, *, name, trans_a=False, trans_b=False, out_dtype=F32):
    k, m = a.shape if trans_a else a.shape[::-1]
    n, kb = b.shape if trans_b else b.shape[::-1]
    assert k == kb, (a.shape, b.shape)
    tm, tn, tk = _tile(m, 512), _tile(n, 512), _tile(k, 1024)
    nk = k // tk
    dims = ((0 if trans_a else 1,), (1 if trans_b else 0,))

    def body(a_ref, b_ref, o_ref, acc_ref):
        kk = pl.program_id(2)

        @pl.when(kk == 0)
        def _():
            acc_ref[...] = jnp.zeros_like(acc_ref)

        acc_ref[...] += _dot(a_ref[...].astype(BF16), b_ref[...].astype(BF16), dims)

        @pl.when(kk == nk - 1)
        def _():
            o_ref[...] = acc_ref[...].astype(out_dtype)

    a_spec = pl.BlockSpec((tk, tm), lambda i, j, kk: (kk, i)) if trans_a else pl.BlockSpec((tm, tk), lambda i, j, kk: (i, kk))
    b_spec = pl.BlockSpec((tn, tk), lambda i, j, kk: (j, kk)) if trans_b else pl.BlockSpec((tk, tn), lambda i, j, kk: (kk, j))
    return pl.pallas_call(
        body, name=name, grid=(m // tm, n // tn, nk), in_specs=[a_spec, b_spec],
        out_specs=pl.BlockSpec((tm, tn), lambda i, j, kk: (i, j)),
        out_shape=jax.ShapeDtypeStruct((m, n), out_dtype),
        scratch_shapes=[pltpu.VMEM((tm, tn), F32)],
        compiler_params=_params("parallel", "parallel", "arbitrary"),
    )(a, b)


def _row_spec(ts, d):
    return pl.BlockSpec((ts, d), lambda i: (i, 0))


def _vec_spec(d):
    return pl.BlockSpec((1, d), lambda i: (0, 0))


def _modulate(x, scale, shift, *, name):
    s, d = x.shape
    ts = _tile(s, 512)

    def body(x_ref, sc_ref, sh_ref, h_ref):
        h_ref[...] = (x_ref[...] * (1.0 + sc_ref[...]) + sh_ref[...]).astype(BF16)

    return pl.pallas_call(
        body, name=name, grid=(s // ts,), in_specs=[_row_spec(ts, d), _vec_spec(d), _vec_spec(d)],
        out_specs=_row_spec(ts, d), out_shape=jax.ShapeDtypeStruct((s, d), BF16), compiler_params=_params("parallel"),
    )(x, scale, shift)


def _ln_stats(pre):
    mu = jnp.mean(pre, axis=-1, keepdims=True)
    xc = pre - mu
    var = jnp.mean(xc * xc, axis=-1, keepdims=True)
    rstd = lax.rsqrt(var + LN_EPS)
    return xc * rstd, rstd


def _ln_bwd_rows(dout, xhat, rstd, g):
    dxh = dout * g
    m1 = jnp.mean(dxh, axis=-1, keepdims=True)
    m2 = jnp.mean(dxh * xhat, axis=-1, keepdims=True)
    return rstd * (dxh - m1 - xhat * m2)


def _colsum(v):
    return jnp.sum(v, axis=0, keepdims=True)


def _resid_ln_modulate(x, y, gate, g, b, scale_next, shift_next, *, name):
    s, d = x.shape
    ts = _tile(s, 256)

    def body(x_ref, y_ref, gate_ref, g_ref, b_ref, sc_ref, sh_ref, xn_ref, h_ref):
        pre = DEEPNORM_ALPHA * x_ref[...] + (1.0 + gate_ref[...]) * y_ref[...]
        xhat, _ = _ln_stats(pre)
        xn = xhat * g_ref[...] + b_ref[...]
        xn_ref[...] = xn
        h_ref[...] = (xn * (1.0 + sc_ref[...]) + sh_ref[...]).astype(BF16)

    return pl.pallas_call(
        body, name=name, grid=(s // ts,),
        in_specs=[_row_spec(ts, d), _row_spec(ts, d)] + [_vec_spec(d)] * 5,
        out_specs=[_row_spec(ts, d), _row_spec(ts, d)],
        out_shape=[jax.ShapeDtypeStruct((s, d), F32), jax.ShapeDtypeStruct((s, d), BF16)],
        compiler_params=_params("parallel"),
    )(x, y, gate, g, b, scale_next, shift_next)


def _loss_ln_bwd(x, y, gate, g, b, target, *, name):
    s, d = x.shape
    ts = _tile(s, 256)

    def body(x_ref, y_ref, gate_ref, g_ref, b_ref, t_ref, dy_ref, dres_ref, dg_ref, db_ref, dgate_ref, loss_ref):
        @pl.when(pl.program_id(0) == 0)
        def _():
            for r in (dg_ref, db_ref, dgate_ref, loss_ref):
                r[...] = jnp.zeros_like(r)

        yv = y_ref[...]
        pre = DEEPNORM_ALPHA * x_ref[...] + (1.0 + gate_ref[...]) * yv
        xhat, rstd = _ln_stats(pre)
        diff = xhat * g_ref[...] + b_ref[...] - t_ref[...]
        loss_ref[...] += (0.5 / d) * jnp.sum(jnp.sum(diff * diff, axis=1, keepdims=True), axis=0, keepdims=True)
        dout = diff * (1.0 / d)
        dpre = _ln_bwd_rows(dout, xhat, rstd, g_ref[...])
        dy_ref[...] = (dpre * (1.0 + gate_ref[...])).astype(BF16)
        dres_ref[...] = DEEPNORM_ALPHA * dpre
        dg_ref[...] += _colsum(dout * xhat)
        db_ref[...] += _colsum(dout)
        dgate_ref[...] += _colsum(dpre * yv)

    vec = jax.ShapeDtypeStruct((1, d), F32)
    return pl.pallas_call(
        body, name=name, grid=(s // ts,),
        in_specs=[_row_spec(ts, d), _row_spec(ts, d), _vec_spec(d), _vec_spec(d), _vec_spec(d), _row_spec(ts, d)],
        out_specs=[_row_spec(ts, d), _row_spec(ts, d), _vec_spec(d), _vec_spec(d), _vec_spec(d), _vec_spec(1)],
        out_shape=[jax.ShapeDtypeStruct((s, d), BF16), jax.ShapeDtypeStruct((s, d), F32), vec, vec, vec,
                   jax.ShapeDtypeStruct((1, 1), F32)],
        compiler_params=_params("arbitrary"),
    )(x, y, gate, g, b, target)


def _mid_ln_bwd(x, y, gate, g, b, dh_next, dres_next, scale_next, x_next, *, name):
    s, d = x.shape
    ts = _tile(s, 256)

    def body(x_ref, y_ref, gate_ref, g_ref, b_ref, dh_ref, dr_ref, sc_ref, xn_ref,
             dy_ref, dres_ref, dg_ref, db_ref, dgate_ref, dscale_ref, dshift_ref):
        @pl.when(pl.program_id(0) == 0)
        def _():
            for r in (dg_ref, db_ref, dgate_ref, dscale_ref, dshift_ref):
                r[...] = jnp.zeros_like(r)

        dh = dh_ref[...]
        dout = dr_ref[...] + dh * (1.0 + sc_ref[...])
        dscale_ref[...] += _colsum(dh * xn_ref[...])
        dshift_ref[...] += _colsum(dh)
        yv = y_ref[...]
        pre = DEEPNORM_ALPHA * x_ref[...] + (1.0 + gate_ref[...]) * yv
        xhat, rstd = _ln_stats(pre)
        dpre = _ln_bwd_rows(dout, xhat, rstd, g_ref[...])
        dy_ref[...] = (dpre * (1.0 + gate_ref[...])).astype(BF16)
        dres_ref[...] = DEEPNORM_ALPHA * dpre
        dg_ref[...] += _colsum(dout * xhat)
        db_ref[...] += _colsum(dout)
        dgate_ref[...] += _colsum(dpre * yv)

    vec = jax.ShapeDtypeStruct((1, d), F32)
    return pl.pallas_call(
        body, name=name, grid=(s // ts,),
        in_specs=[_row_spec(ts, d), _row_spec(ts, d), _vec_spec(d), _vec_spec(d), _vec_spec(d),
                  _row_spec(ts, d), _row_spec(ts, d), _vec_spec(d), _row_spec(ts, d)],
        out_specs=[_row_spec(ts, d), _row_spec(ts, d)] + [_vec_spec(d)] * 5,
        out_shape=[jax.ShapeDtypeStruct((s, d), BF16), jax.ShapeDtypeStruct((s, d), F32)] + [vec] * 5,
        compiler_params=_params("arbitrary"),
    )(x, y, gate, g, b, dh_next, dres_next, scale_next, x_next)


def _input_bwd(x, dh, dres, scale, *, name):
    s, d = x.shape
    ts = _tile(s, 512)

    def body(x_ref, dh_ref, dr_ref, sc_ref, dx_ref, dscale_ref, dshift_ref):
        @pl.when(pl.program_id(0) == 0)
        def _():
            dscale_ref[...] = jnp.zeros_like(dscale_ref)
            dshift_ref[...] = jnp.zeros_like(dshift_ref)

        dh = dh_ref[...]
        dx_ref[...] = dr_ref[...] + dh * (1.0 + sc_ref[...])
        dscale_ref[...] += _colsum(dh * x_ref[...])
        dshift_ref[...] += _colsum(dh)

    vec = jax.ShapeDtypeStruct((1, d), F32)
    return pl.pallas_call(
        body, name=name, grid=(s // ts,),
        in_specs=[_row_spec(ts, d), _row_spec(ts, d), _row_spec(ts, d), _vec_spec(d)],
        out_specs=[_row_spec(ts, d), _vec_spec(d), _vec_spec(d)],
        out_shape=[jax.ShapeDtypeStruct((s, d), F32), vec, vec],
        compiler_params=_params("arbitrary"),
    )(x, dh, dres, scale)


def _chunk_mask(transposed=False):
    r = lax.broadcasted_iota(jnp.int32, (GMLP_BLOCK, GMLP_BLOCK), 0) // CHUNK
    c = lax.broadcasted_iota(jnp.int32, (GMLP_BLOCK, GMLP_BLOCK), 1) // CHUNK
    return (r <= c) if transposed else (c <= r)


def _window_sum(ext, steps, forward):
    rows = ext.shape[0]
    acc = ext
    for k in range(steps):
        shift = 1 << k
        acc = acc + pltpu.roll(acc, (rows - shift) if forward else shift, 0)
    return acc


def _pool_counts(first_row, rows, win):
    t = first_row + lax.broadcasted_iota(jnp.int32, (rows, 1), 0)
    return jnp.minimum(t + 1, win).astype(F32)


def _even_specs(t):
    col = lambda j: pl.BlockSpec((t, D_MODEL), lambda n: (n, j))
    per = t // POOL_HALO
    prev = pl.BlockSpec((POOL_HALO, D_MODEL), lambda n: (jnp.maximum(n * per - 1, 0), 3))
    return col, per, prev


def _full(shape):
    return pl.BlockSpec(shape, lambda n: (0,) * len(shape))


def _gmlp_head(v_h, ng, nb, w_bf):
    xhat, rstd = _ln_stats(v_h)
    vn = (xhat * ng + nb).astype(BF16)
    return xhat, rstd, vn, _dot(w_bf, vn, NN)


def _pool_group(xb_g, prev_g, first_row, grp):
    t = xb_g.shape[0]
    ext = jnp.concatenate([prev_g, xb_g], axis=0)
    tot = _window_sum(ext, grp + 1, False)[POOL_HALO:, :]
    cnt = _pool_counts(first_row, t, POOL_WINDOWS[grp])
    return tot / cnt - xb_g, cnt


def _even_fwd(proj, ws, bs_t, ng, nb, pool_w, pool_b, pool_scale, *, name):
    s = proj.shape[0]
    t = GMLP_BLOCK
    col, per, prev = _even_specs(t)

    def body(u_ref, v_ref, za_ref, xb_ref, zb_ref, xp_ref, ws_ref, bs_ref, ng_ref, nb_ref, pw_ref, pb_ref, ps_ref, o_ref):
        n = pl.program_id(0)
        mask = _chunk_mask()
        for h in range(GMLP_HEADS):
            c0 = h * GMLP_HEAD_DIM
            cs = slice(c0, c0 + GMLP_HEAD_DIM)
            w_bf = jnp.where(mask, ws_ref[h], 0.0).astype(BF16)
            _, _, _, sv = _gmlp_head(v_ref[:, cs], ng_ref[...], nb_ref[...], w_bf)
            sv = sv + bs_ref[:, h:h + 1]
            za = za_ref[:, cs]
            o_ref[:, cs] = (u_ref[:, cs] * sv * (za * _sigmoid(za))).astype(BF16)
        live = (n > 0).astype(F32)
        for grp in range(POOL_GROUPS):
            c0 = grp * POOL_GROUP_DIM
            cs = slice(c0, c0 + POOL_GROUP_DIM)
            pooled, _ = _pool_group(xb_ref[:, cs], xp_ref[:, cs] * live, n * t, grp)
            yb = _dot(pooled.astype(BF16), pw_ref[grp], NN) + pb_ref[:, cs]
            zb = zb_ref[:, cs]
            o_ref[:, D_MODEL + c0:D_MODEL + c0 + POOL_GROUP_DIM] = (yb * ps_ref[:, cs] * (zb * _sigmoid(zb))).astype(BF16)

    return pl.pallas_call(
        body, name=name, grid=(s // t,),
        in_specs=[col(0), col(1), col(2), col(3), col(4), prev,
                  _full((GMLP_HEADS, t, t)), _full((t, LANES)), _full((1, GMLP_HEAD_DIM)), _full((1, GMLP_HEAD_DIM)),
                  _full((POOL_GROUPS, POOL_GROUP_DIM, POOL_GROUP_DIM)), _full((1, D_MODEL)), _full((1, D_MODEL))],
        out_specs=pl.BlockSpec((t, 2 * D_MODEL), lambda n: (n, 0)),
        out_shape=jax.ShapeDtypeStruct((s, 2 * D_MODEL), BF16),
        compiler_params=_params("parallel"),
    )(proj, proj, proj, proj, proj, proj, ws, bs_t, ng, nb, pool_w, pool_b, pool_scale)


def _even_bwd(proj, dmix, ws, ws_t, bs_t, ng, nb, pool_w, pool_b, pool_scale, *, name):
    s = proj.shape[0]
    t = GMLP_BLOCK
    nblk = s // t
    col, per, prev = _even_specs(t)
    nxt = lambda j: pl.BlockSpec((POOL_HALO, D_MODEL), lambda n: (jnp.minimum((n + 1) * per, nblk * per - 1), j))

    def body(u_ref, v_ref, za_ref, xb_ref, zb_ref, xp_ref, zn_ref, da_ref, db_ref, dbn_ref,
             ws_ref, wst_ref, bs_ref, ng_ref, nb_ref, pw_ref, pb_ref, ps_ref,
             dp_ref, gws_ref, gbs_ref, gng_ref, gnb_ref, gpw_ref, gpb_ref, gps_ref):
        n = pl.program_id(0)

        @pl.when(n == 0)
        def _():
            for r in (gws_ref, gbs_ref, gng_ref, gnb_ref, gpw_ref, gpb_ref, gps_ref):
                r[...] = jnp.zeros_like(r)

        mask, mask_t = _chunk_mask(), _chunk_mask(True)
        lane = lax.broadcasted_iota(jnp.int32, (t, LANES), 1)
        ngv, nbv = ng_ref[...], nb_ref[...]
        for h in range(GMLP_HEADS):
            c0 = h * GMLP_HEAD_DIM
            cs = slice(c0, c0 + GMLP_HEAD_DIM)
            w_bf = jnp.where(mask, ws_ref[h], 0.0).astype(BF16)
            wt_bf = jnp.where(mask_t, wst_ref[h], 0.0).astype(BF16)
            xhat, rstd, vn, sv = _gmlp_head(v_ref[:, cs], ngv, nbv, w_bf)
            sv = sv + bs_ref[:, h:h + 1]
            za, u, da = za_ref[:, cs], u_ref[:, cs], da_ref[:, cs]
            sg = _sigmoid(za)
            sl = za * sg
            dp_ref[:, cs] = (da * sv * sl).astype(BF16)
            dp_ref[:, 2 * D_MODEL + c0:2 * D_MODEL + c0 + GMLP_HEAD_DIM] = (
                da * u * sv * (sg * (1.0 + za * (1.0 - sg)))).astype(BF16)
            dsv = da * u * sl
            gbs_ref[...] += jnp.where(lane == h, jnp.sum(dsv, axis=1, keepdims=True), 0.0)
            dsv_bf = dsv.astype(BF16)
            gws_ref[h] += jnp.where(mask, _dot(dsv_bf, vn, NT), 0.0)
            dvn = _dot(wt_bf, dsv_bf, NN)
            dp_ref[:, D_MODEL + c0:D_MODEL + c0 + GMLP_HEAD_DIM] = _ln_bwd_rows(dvn, xhat, rstd, ngv).astype(BF16)
            gng_ref[...] += _colsum(dvn * xhat)
            gnb_ref[...] += _colsum(dvn)
        live_prev = (n > 0).astype(F32)
        live_next = (n < nblk - 1).astype(F32)
        for grp in range(POOL_GROUPS):
            c0 = grp * POOL_GROUP_DIM
            cs = slice(c0, c0 + POOL_GROUP_DIM)
            xb = xb_ref[:, cs]
            pooled, cnt = _pool_group(xb, xp_ref[:, cs] * live_prev, n * t, grp)
            pooled_bf = pooled.astype(BF16)
            pw = pw_ref[grp]
            yb = _dot(pooled_bf, pw, NN) + pb_ref[:, cs]
            ps = ps_ref[:, cs]
            zb, db = zb_ref[:, cs], db_ref[:, cs]
            sg = _sigmoid(zb)
            sl = zb * sg
            dp_ref[:, 4 * D_MODEL + c0:4 * D_MODEL + c0 + POOL_GROUP_DIM] = (
                db * yb * ps * (sg * (1.0 + zb * (1.0 - sg)))).astype(BF16)
            dsl = db * sl
            dy = dsl * ps
            gps_ref[:, cs] += _colsum(dsl * yb)
            gpb_ref[:, cs] += _colsum(dy)
            dy_bf = dy.astype(BF16)
            gpw_ref[grp] += _dot(pooled_bf, dy_bf, TN)
            r = _dot(dy_bf, pw, NT)
            zn = zn_ref[:, cs]
            dyn = (dbn_ref[:, cs] * (zn * _sigmoid(zn)) * ps * live_next).astype(BF16)
            rn = _dot(dyn, pw, NT) / _pool_counts((n + 1) * t, POOL_HALO, POOL_WINDOWS[grp])
            ext = jnp.concatenate([r / cnt, rn], axis=0)
            dxb = _window_sum(ext, grp + 1, True)[:t, :] - r
            dp_ref[:, 3 * D_MODEL + c0:3 * D_MODEL + c0 + POOL_GROUP_DIM] = dxb.astype(BF16)

    out_shape = [
        jax.ShapeDtypeStruct((s, EVEN_IN), BF16),
        jax.ShapeDtypeStruct((GMLP_HEADS, t, t), F32), jax.ShapeDtypeStruct((t, LANES), F32),
        jax.ShapeDtypeStruct((1, GMLP_HEAD_DIM), F32), jax.ShapeDtypeStruct((1, GMLP_HEAD_DIM), F32),
        jax.ShapeDtypeStruct((POOL_GROUPS, POOL_GROUP_DIM, POOL_GROUP_DIM), F32),
        jax.ShapeDtypeStruct((1, D_MODEL), F32), jax.ShapeDtypeStruct((1, D_MODEL), F32),
    ]
    return pl.pallas_call(
        body, name=name, grid=(nblk,),
        in_specs=[col(0), col(1), col(2), col(3), col(4), prev, nxt(4),
                  pl.BlockSpec((t, D_MODEL), lambda n: (n, 0)), pl.BlockSpec((t, D_MODEL), lambda n: (n, 1)), nxt(1),
                  _full((GMLP_HEADS, t, t)), _full((GMLP_HEADS, t, t)), _full((t, LANES)),
                  _full((1, GMLP_HEAD_DIM)), _full((1, GMLP_HEAD_DIM)),
                  _full((POOL_GROUPS, POOL_GROUP_DIM, POOL_GROUP_DIM)), _full((1, D_MODEL)), _full((1, D_MODEL))],
        out_specs=[pl.BlockSpec((t, EVEN_IN), lambda n: (n, 0))] + [_full(o.shape) for o in out_shape[1:]],
        out_shape=out_shape,
        compiler_params=_params("arbitrary"),
    )(proj, proj, proj, proj, proj, proj, proj, dmix, dmix, dmix, ws, ws_t, bs_t, ng, nb, pool_w, pool_b, pool_scale)


def _half_swap(v):
    lane = lax.broadcasted_iota(jnp.int32, v.shape, 1)
    return jnp.where(lane % MLA_ROPE < MLA_ROPE // 2, pltpu.roll(v, LANES - MLA_ROPE // 2, 1), pltpu.roll(v, MLA_ROPE // 2, 1))


def _rope(v, cos, sin_signed):
    return v * cos + _half_swap(v) * sin_signed


def _rope_bwd(d, cos, sin_signed):
    return d * cos + _half_swap(d * sin_signed)


def _rms(v, g):
    r = lax.rsqrt(jnp.mean(v * v, axis=-1, keepdims=True) + LN_EPS)
    return v * r * g, r


def _rms_bwd(dy, v, r, g):
    u = dy * g
    return r * u - v * (r * r * r) * jnp.mean(u * v, axis=-1, keepdims=True)


def _mla_prep(proj, gq, gkv, cos, sin_signed, *, name):
    s = proj.shape[0]
    ts = _tile(s, 256)

    def body(p_ref, gq_ref, gkv_ref, c_ref, s_ref, q_ref, k_ref):
        qcn, _ = _rms(p_ref[:, :MLA_Q_RANK], gq_ref[...])
        kvn, _ = _rms(p_ref[:, MLA_Q_RANK:MLA_Q_RANK + MLA_KV_RANK], gkv_ref[...])
        kr = _rope(p_ref[:, MLA_Q_RANK + MLA_KV_RANK:], c_ref[...], s_ref[...])
        q_ref[...] = qcn.astype(BF16)
        k_ref[...] = jnp.concatenate([kvn, kr], axis=1).astype(BF16)

    return pl.pallas_call(
        body, name=name, grid=(s // ts,),
        in_specs=[_row_spec(ts, ODD_SMALL_PAD), _vec_spec(MLA_Q_RANK), _vec_spec(MLA_KV_RANK), _row_spec(ts, LANES), _row_spec(ts, LANES)],
        out_specs=[_row_spec(ts, MLA_Q_RANK), _row_spec(ts, QK_PAD)],
        out_shape=[jax.ShapeDtypeStruct((s, MLA_Q_RANK), BF16), jax.ShapeDtypeStruct((s, QK_PAD), BF16)],
        compiler_params=_params("parallel"),
    )(proj, gq, gkv, cos, sin_signed)


def _mla_prep_bwd(proj, dqcn, dkv, gq, gkv, cos, sin_signed, *, name):
    s = proj.shape[0]
    ts = _tile(s, 256)

    def body(p_ref, dq_ref, dkv_ref, gq_ref, gkv_ref, c_ref, s_ref, ds_ref, ggq_ref, ggkv_ref):
        @pl.when(pl.program_id(0) == 0)
        def _():
            ggq_ref[...] = jnp.zeros_like(ggq_ref)
            ggkv_ref[...] = jnp.zeros_like(ggkv_ref)

        qc = p_ref[:, :MLA_Q_RANK]
        kvc = p_ref[:, MLA_Q_RANK:MLA_Q_RANK + MLA_KV_RANK]
        _, rq = _rms(qc, gq_ref[...])
        _, rkv = _rms(kvc, gkv_ref[...])
        dq = dq_ref[...]
        dkvn = dkv_ref[:, :MLA_KV_RANK]
        ggq_ref[...] += _colsum(dq * qc * rq)
        ggkv_ref[...] += _colsum(dkvn * kvc * rkv)
        dkr = _rope_bwd(dkv_ref[:, MLA_KV_RANK:], c_ref[...], s_ref[...])
        ds_ref[...] = jnp.concatenate(
            [_rms_bwd(dq, qc, rq, gq_ref[...]), _rms_bwd(dkvn, kvc, rkv, gkv_ref[...]), dkr], axis=1).astype(BF16)

    return pl.pallas_call(
        body, name=name, grid=(s // ts,),
        in_specs=[_row_spec(ts, ODD_SMALL_PAD), _row_spec(ts, MLA_Q_RANK), _row_spec(ts, QK_PAD),
                  _vec_spec(MLA_Q_RANK), _vec_spec(MLA_KV_RANK), _row_spec(ts, LANES), _row_spec(ts, LANES)],
        out_specs=[_row_spec(ts, ODD_SMALL_PAD), _vec_spec(MLA_Q_RANK), _vec_spec(MLA_KV_RANK)],
        out_shape=[jax.ShapeDtypeStruct((s, ODD_SMALL_PAD), BF16), jax.ShapeDtypeStruct((1, MLA_Q_RANK), F32),
                   jax.ShapeDtypeStruct((1, MLA_KV_RANK), F32)],
        compiler_params=_params("arbitrary"),
    )(proj, dqcn, dkv, gq, gkv, cos, sin_signed)


def _q_build(q_nope, q_rope_pre, wuk_hdr, cos, sin_signed, *, name):
    s = q_nope.shape[0]
    ts = _tile(s, 256)

    def body(qn_ref, qr_ref, w_ref, c_ref, s_ref, o_ref):
        r = _rope(qr_ref[...], c_ref[...], s_ref[...])
        lane = lax.broadcasted_iota(jnp.int32, (ts, LANES), 1)
        for j in range(2):
            ql = _dot(qn_ref[:, j * MLA_NOPE:(j + 1) * MLA_NOPE], w_ref[j], NN)
            rr = r if j == 0 else pltpu.roll(r, MLA_ROPE, 1)
            o_ref[j] = jnp.concatenate([ql, jnp.where(lane < MLA_ROPE, rr, 0.0)], axis=1).astype(BF16)

    return pl.pallas_call(
        body, name=name, grid=(s // ts, MLA_HEADS // 2),
        in_specs=[pl.BlockSpec((ts, 2 * MLA_NOPE), lambda i, p: (i, p)), pl.BlockSpec((ts, LANES), lambda i, p: (i, p)),
                  pl.BlockSpec((2, MLA_NOPE, MLA_KV_RANK), lambda i, p: (p, 0, 0)),
                  pl.BlockSpec((ts, LANES), lambda i, p: (i, 0)), pl.BlockSpec((ts, LANES), lambda i, p: (i, 0))],
        out_specs=pl.BlockSpec((2, ts, QK_PAD), lambda i, p: (p, i, 0)),
        out_shape=jax.ShapeDtypeStruct((MLA_HEADS, s, QK_PAD), BF16),
        compiler_params=_params("parallel", "parallel"),
    )(q_nope, q_rope_pre, wuk_hdr, cos, sin_signed)


def _q_bwd(dq, q_nope, wuk_hrd, cos, sin_signed, *, name):
    s = q_nope.shape[0]
    ts = _tile(s, 256)

    def body(dq_ref, qn_ref, w_ref, c_ref, s_ref, dn_ref, dr_ref, gw_ref):
        @pl.when(pl.program_id(1) == 0)
        def _():
            gw_ref[...] = jnp.zeros_like(gw_ref)

        lane = lax.broadcasted_iota(jnp.int32, (ts, LANES), 1)
        for j in range(2):
            dql = dq_ref[j, :, :MLA_KV_RANK]
            dn_ref[:, j * MLA_NOPE:(j + 1) * MLA_NOPE] = _dot(dql, w_ref[j], NN).astype(BF16)
            gw_ref[j] += _dot(dql, qn_ref[:, j * MLA_NOPE:(j + 1) * MLA_NOPE], TN)
        hi0 = dq_ref[0, :, MLA_KV_RANK:].astype(F32)
        hi1 = dq_ref[1, :, MLA_KV_RANK:].astype(F32)
        d = jnp.where(lane < MLA_ROPE, hi0, pltpu.roll(hi1, MLA_ROPE, 1))
        dr_ref[...] = _rope_bwd(d, c_ref[...], s_ref[...]).astype(BF16)

    return pl.pallas_call(
        body, name=name, grid=(MLA_HEADS // 2, s // ts),
        in_specs=[pl.BlockSpec((2, ts, QK_PAD), lambda p, i: (p, i, 0)), pl.BlockSpec((ts, 2 * MLA_NOPE), lambda p, i: (i, p)),
                  pl.BlockSpec((2, MLA_KV_RANK, MLA_NOPE), lambda p, i: (p, 0, 0)),
                  pl.BlockSpec((ts, LANES), lambda p, i: (i, 0)), pl.BlockSpec((ts, LANES), lambda p, i: (i, 0))],
        out_specs=[pl.BlockSpec((ts, 2 * MLA_NOPE), lambda p, i: (i, p)), pl.BlockSpec((ts, LANES), lambda p, i: (i, p)),
                   pl.BlockSpec((2, MLA_KV_RANK, MLA_NOPE), lambda p, i: (p, 0, 0))],
        out_shape=[jax.ShapeDtypeStruct((s, MLA_HEADS * MLA_NOPE), BF16), jax.ShapeDtypeStruct((s, MLA_HEADS * MLA_ROPE), BF16),
                   jax.ShapeDtypeStruct((MLA_HEADS, MLA_KV_RANK, MLA_NOPE), F32)],
        compiler_params=_params("parallel", "arbitrary"),
    )(dq, q_nope, wuk_hrd, cos, sin_signed)


def _diag_mask(rows, bq):
    qi = (lax.broadcasted_iota(jnp.int32, (rows, bq), 0) % bq) // CHUNK
    kj = lax.broadcasted_iota(jnp.int32, (rows, bq), 1) // CHUNK
    return kj <= qi


def _attn_fwd(q, k, *, name):
    nh, s, dk = q.shape
    bq = _tile(s, 128)
    rows = nh * bq

    def body(q_ref, k_ref, o_ref, lse_ref):
        i = pl.program_id(0)
        qb = q_ref[...].reshape(rows, dk)

        def step(j, carry, masked):
            m, l, acc = carry
            kb = k_ref[pl.ds(pl.multiple_of(j * bq, bq), bq), :]
            sc = _dot(qb, kb, NT) * ATTN_SCALE
            if masked:
                sc = jnp.where(_diag_mask(rows, bq), sc, NEG)
            m_new = jnp.maximum(m, jnp.max(sc, axis=1, keepdims=True))
            p = jnp.exp(sc - m_new)
            a = jnp.exp(m - m_new)
            l = a * l + jnp.sum(p, axis=1, keepdims=True)
            acc = a * acc + _dot(p.astype(BF16), kb[:, :MLA_KV_RANK], NN)
            return m_new, l, acc

        init = (jnp.full((rows, 1), NEG, F32), jnp.zeros((rows, 1), F32), jnp.zeros((rows, MLA_KV_RANK), F32))
        carry = lax.fori_loop(0, i, lambda j, c: step(j, c, False), init)
        m, l, acc = step(i, carry, True)
        o_ref[...] = (acc / l).astype(BF16).reshape(nh, bq, MLA_KV_RANK)
        lse_ref[...] = jnp.broadcast_to(m + jnp.log(l), (rows, LANES)).reshape(nh, bq, LANES)

    return pl.pallas_call(
        body, name=name, grid=(s // bq,),
        in_specs=[pl.BlockSpec((nh, bq, dk), lambda i: (0, i, 0)), pl.BlockSpec((s, dk), lambda i: (0, 0))],
        out_specs=[pl.BlockSpec((nh, bq, MLA_KV_RANK), lambda i: (0, i, 0)), pl.BlockSpec((nh, bq, LANES), lambda i: (0, i, 0))],
        out_shape=[jax.ShapeDtypeStruct((nh, s, MLA_KV_RANK), BF16), jax.ShapeDtypeStruct((nh, s, LANES), F32)],
        compiler_params=_params("parallel"),
    )(q, k)


def _attn_bwd(q, k, do, o, lse, *, name):
    nh, s, dk = q.shape
    bq = _tile(s, 128)
    rows = nh * bq

    def body(q_ref, k_ref, do_ref, o_ref, lse_ref, dq_ref, dkv_ref):
        i = pl.program_id(0)

        @pl.when(i == 0)
        def _():
            dkv_ref[...] = jnp.zeros_like(dkv_ref)

        qb = q_ref[...].reshape(rows, dk)
        dob = do_ref[...].reshape(rows, MLA_KV_RANK)
        lse_b = lse_ref[...].reshape(rows, LANES)[:, :1]
        delta = jnp.sum(dob.astype(F32) * o_ref[...].reshape(rows, MLA_KV_RANK).astype(F32), axis=1, keepdims=True)

        def step(j, dq, masked):
            j0 = pl.multiple_of(j * bq, bq)
            kb = k_ref[pl.ds(j0, bq), :]
            sc = _dot(qb, kb, NT) * ATTN_SCALE
            if masked:
                sc = jnp.where(_diag_mask(rows, bq), sc, NEG)
            p = jnp.exp(sc - lse_b)
            dp = _dot(dob, kb[:, :MLA_KV_RANK], NT)
            ds_bf = (p * (dp - delta) * ATTN_SCALE).astype(BF16)
            dkv_ref[pl.ds(j0, bq), :] += _dot(ds_bf, qb, TN)
            dkv_ref[pl.ds(j0, bq), :MLA_KV_RANK] += _dot(p.astype(BF16), dob, TN)
            return dq + _dot(ds_bf, kb, NN)

        dq = lax.fori_loop(0, i, lambda j, c: step(j, c, False), jnp.zeros((rows, dk), F32))
        dq = step(i, dq, True)
        dq_ref[...] = dq.astype(BF16).reshape(nh, bq, dk)

    blk = lambda w: pl.BlockSpec((nh, bq, w), lambda i: (0, i, 0))
    return pl.pallas_call(
        body, name=name, grid=(s // bq,),
        in_specs=[blk(dk), pl.BlockSpec((s, dk), lambda i: (0, 0)), blk(MLA_KV_RANK), blk(MLA_KV_RANK), blk(LANES)],
        out_specs=[blk(dk), pl.BlockSpec((s, dk), lambda i: (0, 0))],
        out_shape=[jax.ShapeDtypeStruct((nh, s, dk), BF16), jax.ShapeDtypeStruct((s, dk), F32)],
        compiler_params=_params("arbitrary"),
    )(q, k, do, o, lse)


HEAD_GROUP = 4


def _o_build(o_lat, wuv_hrv, proj, *, name):
    s = proj.shape[0]
    ts = _tile(s, 256)
    w = HEAD_GROUP * MLA_V

    def body(ol_ref, w_ref, z_ref, og_ref):
        for j in range(HEAD_GROUP):
            cs = slice(j * MLA_V, (j + 1) * MLA_V)
            z = z_ref[:, cs]
            og_ref[:, cs] = (_dot(ol_ref[j], w_ref[j], NN) * (z * _sigmoid(z))).astype(BF16)

    return pl.pallas_call(
        body, name=name, grid=(s // ts, MLA_HEADS // HEAD_GROUP),
        in_specs=[pl.BlockSpec((HEAD_GROUP, ts, MLA_KV_RANK), lambda i, g: (g, i, 0)),
                  pl.BlockSpec((HEAD_GROUP, MLA_KV_RANK, MLA_V), lambda i, g: (g, 0, 0)),
                  pl.BlockSpec((ts, w), lambda i, g: (i, g + 1))],
        out_specs=pl.BlockSpec((ts, w), lambda i, g: (i, g)),
        out_shape=jax.ShapeDtypeStruct((s, MLA_WIDTH), BF16),
        compiler_params=_params("parallel", "parallel"),
    )(o_lat, wuv_hrv, proj)


def _o_bwd(dg, proj, o_lat, wuv_hrv, wuv_hvr, *, name):
    s = proj.shape[0]
    ts = _tile(s, 256)
    w = HEAD_GROUP * MLA_V

    def body(dg_ref, z_ref, ol_ref, w_ref, wt_ref, dol_ref, dz_ref, gw_ref):
        @pl.when(pl.program_id(1) == 0)
        def _():
            gw_ref[...] = jnp.zeros_like(gw_ref)

        for j in range(HEAD_GROUP):
            cs = slice(j * MLA_V, (j + 1) * MLA_V)
            z, dgj, ol = z_ref[:, cs], dg_ref[:, cs], ol_ref[j]
            sg = _sigmoid(z)
            o = _dot(ol, w_ref[j], NN)
            dz_ref[:, cs] = (dgj * o * (sg * (1.0 + z * (1.0 - sg)))).astype(BF16)
            do_bf = (dgj * (z * sg)).astype(BF16)
            dol_ref[j] = _dot(do_bf, wt_ref[j], NN).astype(BF16)
            gw_ref[j] += _dot(ol, do_bf, TN)

    hs = lambda a, b: pl.BlockSpec((HEAD_GROUP, a, b), lambda g, i: (g, 0, 0))
    return pl.pallas_call(
        body, name=name, grid=(MLA_HEADS // HEAD_GROUP, s // ts),
        in_specs=[pl.BlockSpec((ts, w), lambda g, i: (i, g)), pl.BlockSpec((ts, w), lambda g, i: (i, g + 1)),
                  pl.BlockSpec((HEAD_GROUP, ts, MLA_KV_RANK), lambda g, i: (g, i, 0)),
                  hs(MLA_KV_RANK, MLA_V), hs(MLA_V, MLA_KV_RANK)],
        out_specs=[pl.BlockSpec((HEAD_GROUP, ts, MLA_KV_RANK), lambda g, i: (g, i, 0)),
                   pl.BlockSpec((ts, w), lambda g, i: (i, g)), hs(MLA_KV_RANK, MLA_V)],
        out_shape=[jax.ShapeDtypeStruct((MLA_HEADS, s, MLA_KV_RANK), BF16), jax.ShapeDtypeStruct((s, MLA_WIDTH), BF16),
                   jax.ShapeDtypeStruct((MLA_HEADS, MLA_KV_RANK, MLA_V), F32)],
        compiler_params=_params("parallel", "arbitrary"),
    )(dg, proj, o_lat, wuv_hrv, wuv_hvr)


def _ada_mod(c_all, ada_w, ada_b_sh, *, name):
    nl, _, cols = ada_w.shape

    def body(c_ref, w_ref, b_ref, o_ref):
        c = c_ref[...]
        cond = (c * _sigmoid(c)).astype(BF16)
        for l in range(nl):
            o_ref[l] = _dot(cond, w_ref[l].astype(BF16), NN) + b_ref[l]

    return pl.pallas_call(
        body, name=name, out_shape=jax.ShapeDtypeStruct((nl, c_all.shape[0], cols), F32),
        compiler_params=_params(),
    )(c_all, ada_w, ada_b_sh)


def _ada_grad(c_all_t, dmod_sh, dmod_all, *, name):
    nl, _, cols = dmod_sh.shape
    d = c_all_t.shape[0]

    def body(c_ref, dm_ref, da_ref, gw_ref, gb_ref):
        c = c_ref[...]
        cond_t = c * _sigmoid(c)
        for l in range(nl):
            gw_ref[l] = lax.dot_general(cond_t, dm_ref[l], (NN, ((), ())), precision=lax.Precision.HIGHEST,
                                        preferred_element_type=F32)
        gb_ref[...] = jnp.sum(da_ref[...], axis=0)

    return pl.pallas_call(
        body, name=name,
        out_shape=[jax.ShapeDtypeStruct((nl, d, cols), F32), jax.ShapeDtypeStruct(dmod_all.shape[1:], F32)],
        compiler_params=_params(),
    )(c_all_t, dmod_sh, dmod_all)


def _sum_pair(a, b, *, name):
    ns, r, w = a.shape
    tr = _row_tile(r, 512)

    def body(a_ref, b_ref, o_ref):
        o_ref[...] = (a_ref[...].astype(F32) + b_ref[...].astype(F32)).astype(BF16)

    spec = pl.BlockSpec((1, tr, w), lambda s, i: (s, i, 0))
    return pl.pallas_call(
        body, name=name, grid=(ns, r // tr), in_specs=[spec, spec], out_specs=spec,
        out_shape=jax.ShapeDtypeStruct(a.shape, BF16), compiler_params=_params("parallel", "parallel"),
    )(a, b)


def _sum_four(a, *, name):
    ns, r, w = a.shape
    tr = _row_tile(r, 256)

    def body(a_ref, o_ref):
        acc = a_ref[0].astype(F32)
        for k in range(1, ns):
            acc = acc + a_ref[k].astype(F32)
        o_ref[...] = acc

    return pl.pallas_call(
        body, name=name, grid=(r // tr,), in_specs=[pl.BlockSpec((ns, tr, w), lambda i: (0, i, 0))],
        out_specs=pl.BlockSpec((tr, w), lambda i: (i, 0)),
        out_shape=jax.ShapeDtypeStruct((r, w), F32), compiler_params=_params("parallel"),
    )(a)


def _adamw(w, g, m, v, *, name):
    r, c = w.shape
    tr = _row_tile(r, 256)
    c1 = 1.0 - ADAM_B1 ** ADAM_STEP
    c2 = 1.0 - ADAM_B2 ** ADAM_STEP

    def body(w_ref, g_ref, m_ref, v_ref, d_ref, nm_ref, nv_ref):
        gv = g_ref[...]
        nm = ADAM_B1 * m_ref[...] + (1.0 - ADAM_B1) * gv
        nv = ADAM_B2 * v_ref[...] + (1.0 - ADAM_B2) * (gv * gv)
        nm_ref[...] = nm
        nv_ref[...] = nv
        d_ref[...] = -ADAM_LR * ((nm / c1) / (jnp.sqrt(nv / c2) + ADAM_EPS) + ADAM_WD * w_ref[...])

    spec = pl.BlockSpec((tr, c), lambda i: (i, 0))
    out = jax.ShapeDtypeStruct((r, c), F32)
    return pl.pallas_call(
        body, name=name, grid=(r // tr,), in_specs=[spec] * 4, out_specs=[spec] * 3, out_shape=[out] * 3,
        compiler_params=_params("parallel"),
    )(w, g, m, v)


def _flip(v, bit):
    return 1 - v if bit else v


CHIP_DELTAS = ((1, 0), (0, 1), (1, 1))


def _all_gather_chips(shard, *, name):
    def body(x_ref, o_ref, send_sems, recv_sems, local_sem):
        x, y, c = lax.axis_index("x"), lax.axis_index("y"), lax.axis_index("c")
        mine = pltpu.make_async_copy(x_ref, o_ref.at[2 * x + y], local_sem)
        mine.start()

        def copy(k):
            tx, ty = _flip(x, CHIP_DELTAS[k][0]), _flip(y, CHIP_DELTAS[k][1])
            send = pltpu.make_async_remote_copy(src_ref=x_ref, dst_ref=o_ref.at[2 * x + y], send_sem=send_sems.at[k],
                                                recv_sem=recv_sems.at[k], device_id=(tx, ty, c), device_id_type=MESH)
            recv = pltpu.make_async_remote_copy(src_ref=x_ref, dst_ref=o_ref.at[2 * tx + ty], send_sem=send_sems.at[k],
                                                recv_sem=recv_sems.at[k], device_id=(tx, ty, c), device_id_type=MESH)
            return send, recv

        pairs = [copy(k) for k in range(3)]
        for send, _ in pairs:
            send.start()
        for _, recv in pairs:
            recv.wait_recv()
        for send, _ in pairs:
            send.wait_send()
        mine.wait()

    return pl.pallas_call(
        body, name=name, out_shape=jax.ShapeDtypeStruct((4,) + shard.shape, shard.dtype),
        in_specs=[HBM], out_specs=HBM,
        scratch_shapes=[pltpu.SemaphoreType.DMA((3,)), pltpu.SemaphoreType.DMA((3,)), pltpu.SemaphoreType.DMA(())],
    )(shard)


def _scatter_chips(parts, *, name):
    def body(p_ref, o_ref, send_sems, recv_sems, local_sem):
        x, y, c = lax.axis_index("x"), lax.axis_index("y"), lax.axis_index("c")
        me = 2 * x + y
        mine = pltpu.make_async_copy(p_ref.at[me], o_ref.at[me], local_sem)
        mine.start()

        def copy(k):
            tx, ty = _flip(x, CHIP_DELTAS[k][0]), _flip(y, CHIP_DELTAS[k][1])
            peer = 2 * tx + ty
            send = pltpu.make_async_remote_copy(src_ref=p_ref.at[peer], dst_ref=o_ref.at[me], send_sem=send_sems.at[k],
                                                recv_sem=recv_sems.at[k], device_id=(tx, ty, c), device_id_type=MESH)
            recv = pltpu.make_async_remote_copy(src_ref=p_ref.at[peer], dst_ref=o_ref.at[peer], send_sem=send_sems.at[k],
                                                recv_sem=recv_sems.at[k], device_id=(tx, ty, c), device_id_type=MESH)
            return send, recv

        pairs = [copy(k) for k in range(3)]
        for send, _ in pairs:
            send.start()
        for _, recv in pairs:
            recv.wait_recv()
        for send, _ in pairs:
            send.wait_send()
        mine.wait()

    return pl.pallas_call(
        body, name=name, out_shape=jax.ShapeDtypeStruct(parts.shape, parts.dtype),
        in_specs=[HBM], out_specs=HBM,
        scratch_shapes=[pltpu.SemaphoreType.DMA((3,)), pltpu.SemaphoreType.DMA((3,)), pltpu.SemaphoreType.DMA(())],
    )(parts)


def _sibling_swap(halves, *, name):
    ns = halves.shape[0]

    def body(h_ref, o_ref, send_sems, recv_sems):
        x, y, c = lax.axis_index("x"), lax.axis_index("y"), lax.axis_index("c")
        copies = [pltpu.make_async_remote_copy(src_ref=h_ref.at[s, 1 - c], dst_ref=o_ref.at[s], send_sem=send_sems.at[s],
                                               recv_sem=recv_sems.at[s], device_id=(x, y, 1 - c), device_id_type=MESH)
                  for s in range(ns)]
        for cp in copies:
            cp.start()
        for cp in copies:
            cp.wait_recv()
        for cp in copies:
            cp.wait_send()

    return pl.pallas_call(
        body, name=name, out_shape=jax.ShapeDtypeStruct((ns,) + halves.shape[2:], halves.dtype),
        in_specs=[HBM], out_specs=HBM,
        scratch_shapes=[pltpu.SemaphoreType.DMA((ns,)), pltpu.SemaphoreType.DMA((ns,))],
    )(halves)


def _sibling_gather(half, *, name):
    def body(h_ref, o_ref, send_sem, recv_sem, local_sem):
        x, y, c = lax.axis_index("x"), lax.axis_index("y"), lax.axis_index("c")
        mine = pltpu.make_async_copy(h_ref, o_ref.at[c], local_sem)
        mine.start()
        send = pltpu.make_async_remote_copy(src_ref=h_ref, dst_ref=o_ref.at[c], send_sem=send_sem, recv_sem=recv_sem,
                                            device_id=(x, y, 1 - c), device_id_type=MESH)
        recv = pltpu.make_async_remote_copy(src_ref=h_ref, dst_ref=o_ref.at[1 - c], send_sem=send_sem, recv_sem=recv_sem,
                                            device_id=(x, y, 1 - c), device_id_type=MESH)
        send.start()
        recv.wait_recv()
        send.wait_send()
        mine.wait()

    return pl.pallas_call(
        body, name=name, out_shape=jax.ShapeDtypeStruct((2,) + half.shape, half.dtype),
        in_specs=[HBM], out_specs=HBM,
        scratch_shapes=[pltpu.SemaphoreType.DMA(()), pltpu.SemaphoreType.DMA(()), pltpu.SemaphoreType.DMA(())],
    )(half)


N_DEV = 8


def _all_gather_devices(rows, *, name):
    deltas = [(dx, dy, dc) for dx in (0, 1) for dy in (0, 1) for dc in (0, 1)][1:]

    def body(x_ref, o_ref, send_sems, recv_sems):
        x, y, c = lax.axis_index("x"), lax.axis_index("y"), lax.axis_index("c")
        me = 4 * x + 2 * y + c
        o_ref[me] = x_ref[...]
        sends, recvs = [], []
        for k, (dx, dy, dc) in enumerate(deltas):
            tx, ty, tc = _flip(x, dx), _flip(y, dy), _flip(c, dc)
            sends.append(pltpu.make_async_remote_copy(src_ref=x_ref, dst_ref=o_ref.at[me], send_sem=send_sems.at[k],
                                                      recv_sem=recv_sems.at[k], device_id=(tx, ty, tc), device_id_type=MESH))
            recvs.append(pltpu.make_async_remote_copy(src_ref=x_ref, dst_ref=o_ref.at[4 * tx + 2 * ty + tc],
                                                      send_sem=send_sems.at[k], recv_sem=recv_sems.at[k],
                                                      device_id=(tx, ty, tc), device_id_type=MESH))
        for cp in sends:
            cp.start()
        for cp in recvs:
            cp.wait_recv()
        for cp in sends:
            cp.wait_send()

    vmem = pl.BlockSpec(memory_space=pltpu.VMEM)
    return pl.pallas_call(
        body, name=name, out_shape=jax.ShapeDtypeStruct((N_DEV,) + rows.shape, rows.dtype),
        in_specs=[vmem], out_specs=vmem,
        scratch_shapes=[pltpu.SemaphoreType.DMA((N_DEV - 1,)), pltpu.SemaphoreType.DMA((N_DEV - 1,))],
    )(rows)


def _cols_from_shards(g):
    return jnp.transpose(g, (1, 0, 2)).reshape(g.shape[1], -1)


def _cols_to_shards(w):
    k = w.shape[0]
    return jnp.transpose(w.reshape(k, 4, -1), (1, 0, 2)).reshape(4, -1)


def _rows_to_shards(w):
    return w.reshape(4, -1)


SHARDED = ("e_w_in", "pool_w", "e_w_out", "o_w_in", "mla_q_norm_g", "mla_w_uq", "o_w_out")
REPLICATED = ("ln_g", "ln_b", "gmlp_norm_g", "gmlp_norm_b", "gmlp_ws", "gmlp_bs", "pool_b", "pool_scale",
              "mla_kv_norm_g", "mla_w_uk", "mla_w_uv")
WEIGHTS = ("ada_w", "ada_b", "ln_g", "ln_b", "e_w_in", "gmlp_norm_g", "gmlp_norm_b", "gmlp_ws", "gmlp_bs", "pool_w",
           "pool_b", "pool_scale", "e_w_out", "o_w_in", "mla_q_norm_g", "mla_kv_norm_g", "mla_w_uq", "mla_w_uk",
           "mla_w_uv", "o_w_out")
PACK_ROW = 1024
PACK_ALIGN = 2 * 64 * PACK_ROW


def _pad_to(v, n):
    return jnp.concatenate([v, jnp.zeros((n - v.shape[0],), v.dtype)]) if n > v.shape[0] else v


def _pad_cols(v, n):
    return jnp.concatenate([v, jnp.zeros((v.shape[0], n - v.shape[1]), v.dtype)], axis=1) if n > v.shape[1] else v


def _round_up(n, m):
    return -(-n // m) * m


def kernel(x, c, positions, ada_w, ada_b, ln_g, ln_b, e_w_in, gmlp_norm_g, gmlp_norm_b, gmlp_ws, gmlp_bs, pool_w, pool_b, pool_scale, e_w_out, o_w_in, mla_q_norm_g, mla_kv_norm_g, mla_w_uq, mla_w_uk, mla_w_uv, o_w_out, loss_target, m_ada_w, m_ada_b, m_ln_g, m_ln_b, m_e_w_in, m_gmlp_norm_g, m_gmlp_norm_b, m_gmlp_ws, m_gmlp_bs, m_pool_w, m_pool_b, m_pool_scale, m_e_w_out, m_o_w_in, m_mla_q_norm_g, m_mla_kv_norm_g, m_mla_w_uq, m_mla_w_uk, m_mla_w_uv, m_o_w_out, v_ada_w, v_ada_b, v_ln_g, v_ln_b, v_e_w_in, v_gmlp_norm_g, v_gmlp_norm_b, v_gmlp_ws, v_gmlp_bs, v_pool_w, v_pool_b, v_pool_scale, v_e_w_out, v_o_w_in, v_mla_q_norm_g, v_mla_kv_norm_g, v_mla_w_uq, v_mla_w_uk, v_mla_w_uv, v_o_w_out):
    args = dict(locals())
    weights = {n: args[n] for n in WEIGHTS}
    mom = {n: args["m_" + n] for n in WEIGHTS}
    var = {n: args["v_" + n] for n in WEIGHTS}
    ax, ay, ac = lax.axis_index("x"), lax.axis_index("y"), lax.axis_index("c")
    chip = 2 * ax + ay
    dev = 2 * chip + ac
    d = D_MODEL
    x2 = x[0]
    target = loss_target[0]

    local = [weights[n].reshape(-1) for n in SHARDED]
    sizes = [v.shape[0] for v in local]
    n_sh = sum(sizes)
    n_sh_pad = _round_up(n_sh, 8 * PACK_ROW)
    packed = _pad_to(jnp.concatenate(local).astype(BF16), n_sh_pad).reshape(-1, PACK_ROW)
    gathered = _all_gather_chips(packed, name="gather_weights").reshape(4, -1)
    full, off = {}, 0
    for n, sz in zip(SHARDED, sizes):
        full[n] = gathered[:, off:off + sz]
        off += sz
    w_in0 = _cols_from_shards(full["e_w_in"].reshape(4, d, -1))
    pool_w_bf = jnp.transpose(full["pool_w"].reshape(4, POOL_GROUPS, -1, POOL_GROUP_DIM), (1, 0, 2, 3)).reshape(
        POOL_GROUPS, POOL_GROUP_DIM, POOL_GROUP_DIM)
    w_out0 = full["e_w_out"].reshape(-1, d)
    w_in1 = _cols_from_shards(full["o_w_in"].reshape(4, d, -1))
    w_in1 = jnp.concatenate([_pad_cols(w_in1[:, :ODD_SMALL], ODD_SMALL_PAD), w_in1[:, ODD_SMALL:]], axis=1)
    w_uq = full["mla_w_uq"].reshape(MLA_Q_RANK, MLA_HEADS, MLA_NOPE + MLA_ROPE)
    w_uq_nope = w_uq[:, :, :MLA_NOPE].reshape(MLA_Q_RANK, -1)
    w_uq_rope = w_uq[:, :, MLA_NOPE:].reshape(MLA_Q_RANK, -1)
    w_out1 = full["o_w_out"].reshape(-1, d)
    wuk_hrd = jnp.transpose(mla_w_uk[0], (1, 0, 2)).astype(BF16)
    wuk_hdr = jnp.transpose(mla_w_uk[0], (1, 2, 0)).astype(BF16)
    wuv_hrv = jnp.transpose(mla_w_uv[0], (1, 0, 2)).astype(BF16)
    wuv_hvr = jnp.transpose(mla_w_uv[0], (1, 2, 0)).astype(BF16)
    ws = gmlp_ws[0]
    ws_t = jnp.transpose(ws, (0, 2, 1))
    bs_t = _pad_cols(gmlp_bs[0].T, LANES)

    inv = 1.0 / (ROPE_THETA ** (jnp.arange(0, MLA_ROPE, 2, dtype=F32) / MLA_ROPE))
    ang = positions[0].astype(F32)[:, None] * inv
    cos_t = jnp.tile(jnp.cos(ang), (1, 4))
    sin_t = jnp.tile(jnp.concatenate([-jnp.sin(ang), jnp.sin(ang)], axis=1), (1, 2))

    c_all = _all_gather_devices(c.reshape(8, LANES), name="gather_c").reshape(N_DEV, d)
    cols = ada_w.shape[2]
    ada_b_mine = lax.dynamic_slice_in_dim(ada_b, chip * cols, cols, axis=1)[:, None, :]
    mod_sh = _ada_mod(c_all, ada_w, ada_b_mine, name="ada_mod")
    q_norm_rows = jnp.zeros((8, cols), F32).at[0, :mla_q_norm_g.shape[1]].set(mla_q_norm_g[0])
    mod_all = _all_gather_chips(jnp.concatenate([mod_sh.reshape(2 * N_DEV, cols), q_norm_rows]), name="gather_mod")
    q_norm_g = mod_all[:, 2 * N_DEV, :mla_q_norm_g.shape[1]].reshape(1, -1)
    mod_all = jnp.transpose(mod_all[:, :2 * N_DEV].reshape(4, 2, N_DEV, cols), (1, 2, 0, 3)).reshape(2, N_DEV, 3 * d)
    mod = lax.dynamic_index_in_dim(mod_all, dev, axis=1, keepdims=False)
    shift = [mod[l:l + 1, :d] for l in range(2)]
    scale = [mod[l:l + 1, d:2 * d] for l in range(2)]
    gate = [mod[l:l + 1, 2 * d:] for l in range(2)]

    h0 = _modulate(x2, scale[0], shift[0], name="modulate0")
    proj0 = _matmul(h0, w_in0, name="proj0")
    mix0 = _even_fwd(proj0, ws, bs_t, gmlp_norm_g, gmlp_norm_b, pool_w_bf, pool_b, pool_scale, name="even_fwd")
    y0 = _matmul(mix0, w_out0, name="out0")
    x1, h1 = _resid_ln_modulate(x2, y0, gate[0], ln_g[0:1], ln_b[0:1], scale[1], shift[1], name="resid_ln0")

    proj1 = _matmul(h1, w_in1, name="proj1")
    q_cn, keys = _mla_prep(proj1, q_norm_g, mla_kv_norm_g, cos_t, sin_t, name="mla_prep")
    q_nope = _matmul(q_cn, w_uq_nope, name="q_nope", out_dtype=BF16)
    q_rope_pre = _matmul(q_cn, w_uq_rope, name="q_rope")
    q = _q_build(q_nope, q_rope_pre, wuk_hdr, cos_t, sin_t, name="q_build")
    o_lat, lse = _attn_fwd(q, keys, name="attn_fwd")
    og = _o_build(o_lat, wuv_hrv, proj1, name="o_build")
    y1 = _matmul(og, w_out1, name="out1")

    dy1, dres1, g_ln_g1, g_ln_b1, dgate1, loss = _loss_ln_bwd(x1, y1, gate[1], ln_g[1:2], ln_b[1:2], target, name="loss_ln1")
    dg1 = _matmul(dy1, w_out1, trans_b=True, name="d_og")
    g_w_out1 = _matmul(og, dy1, trans_a=True, name="g_out1")
    do_lat, dz, g_uv = _o_bwd(dg1, proj1, o_lat, wuv_hrv, wuv_hvr, name="o_bwd")
    dq, dkeys = _attn_bwd(q, keys, do_lat, o_lat, lse, name="attn_bwd")
    dq_nope, dq_rope, g_uk = _q_bwd(dq, q_nope, wuk_hrd, cos_t, sin_t, name="q_bwd")
    dq_cn = _matmul(dq_nope, w_uq_nope, trans_b=True, name="d_qcn_nope") + _matmul(dq_rope, w_uq_rope, trans_b=True, name="d_qcn_rope")
    g_uq_nope = _matmul(q_cn, dq_nope, trans_a=True, name="g_uq_nope")
    g_uq_rope = _matmul(q_cn, dq_rope, trans_a=True, name="g_uq_rope")
    dsmall, g_qg, g_kvg = _mla_prep_bwd(proj1, dq_cn, dkeys, q_norm_g, mla_kv_norm_g, cos_t, sin_t, name="mla_prep_bwd")
    dproj1 = jnp.concatenate([dsmall, dz], axis=1)
    dh1 = _matmul(dproj1, w_in1, trans_b=True, name="d_h1")
    g_w_in1 = _matmul(h1, dproj1, trans_a=True, name="g_in1")

    dy0, dres0, g_ln_g0, g_ln_b0, dgate0, dscale1, dshift1 = _mid_ln_bwd(
        x2, y0, gate[0], ln_g[0:1], ln_b[0:1], dh1, dres1, scale[1], x1, name="mid_ln0")
    dmix0 = _matmul(dy0, w_out0, trans_b=True, name="d_mix0")
    g_w_out0 = _matmul(mix0, dy0, trans_a=True, name="g_out0")
    dproj0, g_ws, g_bs_t, g_ng, g_nb, g_pw, g_pb, g_ps = _even_bwd(
        proj0, dmix0, ws, ws_t, bs_t, gmlp_norm_g, gmlp_norm_b, pool_w_bf, pool_b, pool_scale, name="even_bwd")
    dh0 = _matmul(dproj0, w_in0, trans_b=True, name="d_h0")
    g_w_in0 = _matmul(h0, dproj0, trans_a=True, name="g_in0")
    grad_x, dscale0, dshift0 = _input_bwd(x2, dh0, dres0, scale[0], name="input_bwd")

    dmod = jnp.concatenate([dshift0, dscale0, dgate0, dshift1, dscale1, dgate1], axis=1)
    dmod_all = _all_gather_devices(dmod.reshape(-1, LANES), name="gather_dmod").reshape(N_DEV, 2, 3 * d)
    dmod_sh = jnp.transpose(lax.dynamic_slice_in_dim(dmod_all, chip * cols, cols, axis=2), (1, 0, 2))
    dmod_sh = jnp.concatenate([dmod_sh, jnp.zeros((2, LANES - N_DEV, cols), F32)], axis=1)
    c_all_t = _pad_cols(c_all.T, LANES)
    g_ada_w, g_ada_b = _ada_grad(c_all_t, dmod_sh, dmod_all, name="ada_grad")

    g_uq = jnp.concatenate([g_uq_nope.reshape(MLA_Q_RANK, MLA_HEADS, MLA_NOPE), g_uq_rope.reshape(MLA_Q_RANK, MLA_HEADS, MLA_ROPE)], axis=2)
    g_w_in1 = jnp.concatenate([g_w_in1[:, :ODD_SMALL], g_w_in1[:, ODD_SMALL_PAD:]], axis=1)
    g_full = {
        "e_w_in": _cols_to_shards(g_w_in0),
        "pool_w": jnp.transpose(g_pw.reshape(POOL_GROUPS, 4, -1, POOL_GROUP_DIM), (1, 0, 2, 3)).reshape(4, -1),
        "e_w_out": _rows_to_shards(g_w_out0),
        "o_w_in": _cols_to_shards(g_w_in1),
        "mla_q_norm_g": _rows_to_shards(g_qg.reshape(-1)),
        "mla_w_uq": _rows_to_shards(g_uq),
        "o_w_out": _rows_to_shards(g_w_out1),
    }
    g_rep = {
        "ln_g": jnp.concatenate([g_ln_g0, g_ln_g1]), "ln_b": jnp.concatenate([g_ln_b0, g_ln_b1]),
        "gmlp_norm_g": g_ng, "gmlp_norm_b": g_nb, "gmlp_ws": g_ws, "gmlp_bs": g_bs_t[:, :GMLP_HEADS].T,
        "pool_b": g_pb, "pool_scale": g_ps, "mla_kv_norm_g": g_kvg,
        "mla_w_uk": jnp.transpose(g_uk, (1, 0, 2)), "mla_w_uv": jnp.transpose(g_uv, (1, 0, 2)),
    }
    rep_flat = jnp.concatenate([g_rep[n].reshape(-1) for n in REPLICATED])
    n_rep = rep_flat.shape[0]
    rep_part = _round_up(-(-n_rep // 4), 8 * PACK_ROW)
    per_chip = _round_up(n_sh + rep_part, PACK_ALIGN)
    half_rows = per_chip // (2 * PACK_ROW)
    g_pack = jnp.concatenate([g_full[n] for n in SHARDED] + [_pad_to(rep_flat, 4 * rep_part).reshape(4, rep_part)], axis=1)
    g_pack = _pad_cols(g_pack, per_chip).astype(BF16).reshape(4, 2, half_rows, PACK_ROW)
    from_sibling = _sibling_swap(g_pack, name="reduce_sibling")
    mine = lax.dynamic_index_in_dim(g_pack, ac, axis=1, keepdims=False)
    chip_sum = _sum_pair(mine, from_sibling, name="sum_sibling")
    from_chips = _scatter_chips(chip_sum, name="reduce_chips")
    total_half = _sum_four(from_chips, name="sum_chips")
    total = _sibling_gather(total_half, name="share_sibling").reshape(-1)
    g_sh_flat = total[:n_sh]
    rep_mine = total[n_sh:n_sh + rep_part].reshape(-1, PACK_ROW)
    rep_total = _all_gather_chips(rep_mine, name="gather_rep").reshape(-1)[:n_rep]

    grads, off = {}, 0
    for n, sz in zip(SHARDED, sizes):
        grads[n] = g_sh_flat[off:off + sz].reshape(weights[n].shape)
        off += sz
    off = 0
    for n in REPLICATED:
        sz = weights[n].size
        grads[n] = rep_total[off:off + sz].reshape(weights[n].shape)
        off += sz
    grads["ada_w"] = g_ada_w
    grads["ada_b"] = g_ada_b

    def pack(tree):
        flat = jnp.concatenate([tree[n].reshape(-1) for n in WEIGHTS])
        return _pad_to(flat, _round_up(flat.shape[0], 256 * PACK_ROW)).reshape(-1, PACK_ROW)

    delta_p, new_m_p, new_v_p = _adamw(pack(weights), pack(grads), pack(mom), pack(var), name="adamw")

    def unpack(p):
        flat, out, o = p.reshape(-1), {}, 0
        for n in WEIGHTS:
            sz = weights[n].size
            out[n] = flat[o:o + sz].reshape(weights[n].shape)
            o += sz
        return out

    delta, new_m, new_v = unpack(delta_p), unpack(new_m_p), unpack(new_v_p)
    loss_total = lax.psum(loss[0, 0], ("x", "y", "c"))
    return (loss_total, grad_x[None], *[grads[n] for n in WEIGHTS], *[delta[n] for n in WEIGHTS],
            *[new_m[n] for n in WEIGHTS], *[new_v[n] for n in WEIGHTS])
```

```python
import jax
import jax.numpy as jnp
from jax import lax
from jax.experimental import pallas as pl
from jax.experimental.pallas import tpu as pltpu

F32 = jnp.float32
BF16 = jnp.bfloat16
MESH = pl.DeviceIdType.MESH

D_MODEL = 1024
CHUNK = 64
LN_EPS = 1e-5
GMLP_HEADS = 4
GMLP_HEAD_DIM = 256
GMLP_BLOCK = 128
POOL_WINDOWS = (2, 4, 8, 16)
POOL_GROUPS = 4
POOL_GROUP_DIM = 256
POOL_HALO = 16
EVEN_IN = 5120
MLA_HEADS = 16
MLA_NOPE = 128
MLA_ROPE = 64
MLA_V = 128
MLA_Q_RANK = 256
MLA_KV_RANK = 128
MLA_WIDTH = MLA_HEADS * MLA_V
ODD_IN = 2496
ODD_SMALL = MLA_Q_RANK + MLA_KV_RANK + MLA_ROPE
ODD_SMALL_PAD = 512
QK_PAD = 256
ROPE_THETA = 10000.0
ATTN_SCALE = (MLA_NOPE + MLA_ROPE) ** -0.5
DEEPNORM_ALPHA = (2.0 * 2) ** 0.25
ADAM_LR = 0.001
ADAM_B1 = 0.9
ADAM_B2 = 0.999
ADAM_EPS = 1e-08
ADAM_WD = 0.01
ADAM_STEP = 10
NEG = -1e30
LANES = 128
N_DEV = 8
N_CHIPS = 4
VMEM_LIMIT_BYTES = 56 * 1024 * 1024
HBM = pl.BlockSpec(memory_space=pltpu.HBM)
VMEM = pl.BlockSpec(memory_space=pltpu.VMEM)


def _params(*sem):
    return pltpu.CompilerParams(dimension_semantics=sem if sem else None, vmem_limit_bytes=VMEM_LIMIT_BYTES)


def _tile(dim, pref):
    for t in (pref, 512, 256, 128):
        if t <= pref and dim % t == 0:
            return t
    return dim


def _sigmoid(z):
    return 1.0 / (1.0 + jnp.exp(-z))


def _dot(a, b, dims):
    return lax.dot_general(a, b, (dims, ((), ())), preferred_element_type=F32)


NN = ((1,), (0,))
NT = ((1,), (1,))
TN = ((0,), (0,))


def _matmul(a, b, *, name, trans_a=False, trans_b=False, out_dtype=F32, b_stacked=False, out_stacked=False, tm=512, tn=512):
    k, m = a.shape if trans_a else a.shape[::-1]
    if b_stacked:
        ns, kb, n_sh = b.shape
        kb, n = (ns * n_sh, kb) if trans_b else (kb, ns * n_sh)
    else:
        n, kb = b.shape if trans_b else b.shape[::-1]
    assert k == kb, (a.shape, b.shape)
    tm = _tile(m, tm)
    if b_stacked and trans_b:
        tn, tk = _tile(n, tn), n_sh
    elif b_stacked or out_stacked:
        tn, tk = _tile(n // N_CHIPS, 256), _tile(k, 1024)
    else:
        tn, tk = _tile(n, tn), _tile(k, 1024)
    nk = k // tk
    per = (n // N_CHIPS) // tn
    dims = ((0 if trans_a else 1,), (1 if trans_b else 0,))

    def body(a_ref, b_ref, o_ref, acc_ref):
        kk = pl.program_id(2)

        @pl.when(kk == 0)
        def _():
            acc_ref[...] = jnp.zeros_like(acc_ref)

        acc_ref[...] += _dot(a_ref[...].astype(BF16), b_ref[...].astype(BF16), dims)

        @pl.when(kk == nk - 1)
        def _():
            o_ref[...] = acc_ref[...].astype(out_dtype)

    a_spec = pl.BlockSpec((tk, tm), lambda i, j, kk: (kk, i)) if trans_a else pl.BlockSpec((tm, tk), lambda i, j, kk: (i, kk))
    if b_stacked and trans_b:
        b_spec = pl.BlockSpec((None, tn, tk), lambda i, j, kk: (kk, j, 0))
    elif b_stacked:
        b_spec = pl.BlockSpec((None, tk, tn), lambda i, j, kk: (j // per, kk, j % per))
    elif trans_b:
        b_spec = pl.BlockSpec((tn, tk), lambda i, j, kk: (j, kk))
    else:
        b_spec = pl.BlockSpec((tk, tn), lambda i, j, kk: (kk, j))
    if out_stacked:
        o_spec = pl.BlockSpec((None, tm, tn), lambda i, j, kk: (j // per, i, j % per))
        o_shape = jax.ShapeDtypeStruct((N_CHIPS, m, n // N_CHIPS), out_dtype)
    else:
        o_spec = pl.BlockSpec((tm, tn), lambda i, j, kk: (i, j))
        o_shape = jax.ShapeDtypeStruct((m, n), out_dtype)
    return pl.pallas_call(
        body, name=name, grid=(m // tm, n // tn, nk), in_specs=[a_spec, b_spec], out_specs=o_spec, out_shape=o_shape,
        scratch_shapes=[pltpu.VMEM((tm, tn), F32)],
        compiler_params=_params("parallel", "parallel", "arbitrary"),
    )(a, b)


def _row_spec(ts, d):
    return pl.BlockSpec((ts, d), lambda i: (i, 0))


def _vec_spec(d):
    return pl.BlockSpec((1, d), lambda i: (0, 0))


def _modulate(x, scale, shift, *, name):
    s, d = x.shape
    ts = _tile(s, 512)

    def body(x_ref, sc_ref, sh_ref, h_ref):
        h_ref[...] = (x_ref[...] * (1.0 + sc_ref[...]) + sh_ref[...]).astype(BF16)

    return pl.pallas_call(
        body, name=name, grid=(s // ts,), in_specs=[_row_spec(ts, d), _vec_spec(d), _vec_spec(d)],
        out_specs=_row_spec(ts, d), out_shape=jax.ShapeDtypeStruct((s, d), BF16), compiler_params=_params("parallel"),
    )(x, scale, shift)


def _ln_stats(pre):
    mu = jnp.mean(pre, axis=-1, keepdims=True)
    xc = pre - mu
    var = jnp.mean(xc * xc, axis=-1, keepdims=True)
    rstd = lax.rsqrt(var + LN_EPS)
    return xc * rstd, rstd


def _ln_bwd_rows(dout, xhat, rstd, g):
    dxh = dout * g
    m1 = jnp.mean(dxh, axis=-1, keepdims=True)
    m2 = jnp.mean(dxh * xhat, axis=-1, keepdims=True)
    return rstd * (dxh - m1 - xhat * m2)


def _colsum(v):
    return jnp.sum(v, axis=0, keepdims=True)


def _resid_ln_modulate(x, y, gate, g, b, scale_next, shift_next, *, name):
    s, d = x.shape
    ts = _tile(s, 256)

    def body(x_ref, y_ref, gate_ref, g_ref, b_ref, sc_ref, sh_ref, xn_ref, h_ref):
        pre = DEEPNORM_ALPHA * x_ref[...] + (1.0 + gate_ref[...]) * y_ref[...]
        xhat, _ = _ln_stats(pre)
        xn = xhat * g_ref[...] + b_ref[...]
        xn_ref[...] = xn
        h_ref[...] = (xn * (1.0 + sc_ref[...]) + sh_ref[...]).astype(BF16)

    return pl.pallas_call(
        body, name=name, grid=(s // ts,),
        in_specs=[_row_spec(ts, d), _row_spec(ts, d)] + [_vec_spec(d)] * 5,
        out_specs=[_row_spec(ts, d), _row_spec(ts, d)],
        out_shape=[jax.ShapeDtypeStruct((s, d), F32), jax.ShapeDtypeStruct((s, d), BF16)],
        compiler_params=_params("parallel"),
    )(x, y, gate, g, b, scale_next, shift_next)


def _loss_ln_bwd(x, y, gate, g, b, target, *, name):
    s, d = x.shape
    ts = _tile(s, 256)

    def body(x_ref, y_ref, gate_ref, g_ref, b_ref, t_ref, dy_ref, dres_ref, dg_ref, db_ref, dgate_ref, loss_ref):
        @pl.when(pl.program_id(0) == 0)
        def _():
            for r in (dg_ref, db_ref, dgate_ref, loss_ref):
                r[...] = jnp.zeros_like(r)

        yv = y_ref[...]
        pre = DEEPNORM_ALPHA * x_ref[...] + (1.0 + gate_ref[...]) * yv
        xhat, rstd = _ln_stats(pre)
        diff = xhat * g_ref[...] + b_ref[...] - t_ref[...]
        loss_ref[...] += (0.5 / d) * jnp.sum(jnp.sum(diff * diff, axis=1, keepdims=True), axis=0, keepdims=True)
        dout = diff * (1.0 / d)
        dpre = _ln_bwd_rows(dout, xhat, rstd, g_ref[...])
        dy_ref[...] = (dpre * (1.0 + gate_ref[...])).astype(BF16)
        dres_ref[...] = DEEPNORM_ALPHA * dpre
        dg_ref[...] += _colsum(dout * xhat)
        db_ref[...] += _colsum(dout)
        dgate_ref[...] += _colsum(dpre * yv)

    vec = jax.ShapeDtypeStruct((1, d), F32)
    return pl.pallas_call(
        body, name=name, grid=(s // ts,),
        in_specs=[_row_spec(ts, d), _row_spec(ts, d), _vec_spec(d), _vec_spec(d), _vec_spec(d), _row_spec(ts, d)],
        out_specs=[_row_spec(ts, d), _row_spec(ts, d), _vec_spec(d), _vec_spec(d), _vec_spec(d), _vec_spec(1)],
        out_shape=[jax.ShapeDtypeStruct((s, d), BF16), jax.ShapeDtypeStruct((s, d), F32), vec, vec, vec,
                   jax.ShapeDtypeStruct((1, 1), F32)],
        compiler_params=_params("arbitrary"),
    )(x, y, gate, g, b, target)


def _mid_ln_bwd(x, y, gate, g, b, dh_next, dres_next, scale_next, x_next, *, name):
    s, d = x.shape
    ts = _tile(s, 256)

    def body(x_ref, y_ref, gate_ref, g_ref, b_ref, dh_ref, dr_ref, sc_ref, xn_ref,
             dy_ref, dres_ref, dg_ref, db_ref, dgate_ref, dscale_ref, dshift_ref):
        @pl.when(pl.program_id(0) == 0)
        def _():
            for r in (dg_ref, db_ref, dgate_ref, dscale_ref, dshift_ref):
                r[...] = jnp.zeros_like(r)

        dh = dh_ref[...]
        dout = dr_ref[...] + dh * (1.0 + sc_ref[...])
        dscale_ref[...] += _colsum(dh * xn_ref[...])
        dshift_ref[...] += _colsum(dh)
        yv = y_ref[...]
        pre = DEEPNORM_ALPHA * x_ref[...] + (1.0 + gate_ref[...]) * yv
        xhat, rstd = _ln_stats(pre)
        dpre = _ln_bwd_rows(dout, xhat, rstd, g_ref[...])
        dy_ref[...] = (dpre * (1.0 + gate_ref[...])).astype(BF16)
        dres_ref[...] = DEEPNORM_ALPHA * dpre
        dg_ref[...] += _colsum(dout * xhat)
        db_ref[...] += _colsum(dout)
        dgate_ref[...] += _colsum(dpre * yv)

    vec = jax.ShapeDtypeStruct((1, d), F32)
    return pl.pallas_call(
        body, name=name, grid=(s // ts,),
        in_specs=[_row_spec(ts, d), _row_spec(ts, d), _vec_spec(d), _vec_spec(d), _vec_spec(d),
                  _row_spec(ts, d), _row_spec(ts, d), _vec_spec(d), _row_spec(ts, d)],
        out_specs=[_row_spec(ts, d), _row_spec(ts, d)] + [_vec_spec(d)] * 5,
        out_shape=[jax.ShapeDtypeStruct((s, d), BF16), jax.ShapeDtypeStruct((s, d), F32)] + [vec] * 5,
        compiler_params=_params("arbitrary"),
    )(x, y, gate, g, b, dh_next, dres_next, scale_next, x_next)


def _input_bwd(x, dh, dres, scale, *, name):
    s, d = x.shape
    ts = _tile(s, 512)

    def body(x_ref, dh_ref, dr_ref, sc_ref, dx_ref, dscale_ref, dshift_ref):
        @pl.when(pl.program_id(0) == 0)
        def _():
            dscale_ref[...] = jnp.zeros_like(dscale_ref)
            dshift_ref[...] = jnp.zeros_like(dshift_ref)

        dh = dh_ref[...]
        dx_ref[...] = dr_ref[...] + dh * (1.0 + sc_ref[...])
        dscale_ref[...] += _colsum(dh * x_ref[...])
        dshift_ref[...] += _colsum(dh)

    vec = jax.ShapeDtypeStruct((1, d), F32)
    return pl.pallas_call(
        body, name=name, grid=(s // ts,),
        in_specs=[_row_spec(ts, d), _row_spec(ts, d), _row_spec(ts, d), _vec_spec(d)],
        out_specs=[_row_spec(ts, d), _vec_spec(d), _vec_spec(d)],
        out_shape=[jax.ShapeDtypeStruct((s, d), F32), vec, vec],
        compiler_params=_params("arbitrary"),
    )(x, dh, dres, scale)


def _chunk_mask(transposed=False):
    r = lax.broadcasted_iota(jnp.int32, (GMLP_BLOCK, GMLP_BLOCK), 0) // CHUNK
    c = lax.broadcasted_iota(jnp.int32, (GMLP_BLOCK, GMLP_BLOCK), 1) // CHUNK
    return (r <= c) if transposed else (c <= r)


def _window_sum(ext, steps, forward):
    rows = ext.shape[0]
    acc = ext
    for k in range(steps):
        shift = 1 << k
        acc = acc + pltpu.roll(acc, (rows - shift) if forward else shift, 0)
    return acc


def _pool_counts(first_row, rows, win):
    t = first_row + lax.broadcasted_iota(jnp.int32, (rows, 1), 0)
    return jnp.minimum(t + 1, win).astype(F32)


def _even_specs(t):
    col = lambda j: pl.BlockSpec((t, D_MODEL), lambda n: (n, j))
    per = t // POOL_HALO
    prev = pl.BlockSpec((POOL_HALO, D_MODEL), lambda n: (jnp.maximum(n * per - 1, 0), 3))
    return col, per, prev


def _full(shape):
    return pl.BlockSpec(shape, lambda n: (0,) * len(shape))


def _gmlp_head(v_h, ng, nb, w_bf):
    xhat, rstd = _ln_stats(v_h)
    vn = (xhat * ng + nb).astype(BF16)
    return xhat, rstd, vn, _dot(w_bf, vn, NN)


def _pool_group(xb_g, prev_g, first_row, grp):
    t = xb_g.shape[0]
    ext = jnp.concatenate([prev_g, xb_g], axis=0)
    tot = _window_sum(ext, grp + 1, False)[POOL_HALO:, :]
    cnt = _pool_counts(first_row, t, POOL_WINDOWS[grp])
    return tot / cnt - xb_g, cnt


def _even_fwd(proj, ws, bs_t, ng, nb, pool_w, pool_b, pool_scale, *, name):
    s = proj.shape[0]
    t = GMLP_BLOCK
    col, per, prev = _even_specs(t)

    def body(u_ref, v_ref, za_ref, xb_ref, zb_ref, xp_ref, ws_ref, bs_ref, ng_ref, nb_ref, pw_ref, pb_ref, ps_ref, o_ref):
        n = pl.program_id(0)
        mask = _chunk_mask()
        for h in range(GMLP_HEADS):
            c0 = h * GMLP_HEAD_DIM
            cs = slice(c0, c0 + GMLP_HEAD_DIM)
            w_bf = jnp.where(mask, ws_ref[h], 0.0).astype(BF16)
            _, _, _, sv = _gmlp_head(v_ref[:, cs], ng_ref[...], nb_ref[...], w_bf)
            sv = sv + bs_ref[:, h:h + 1]
            za = za_ref[:, cs]
            o_ref[:, cs] = (u_ref[:, cs] * sv * (za * _sigmoid(za))).astype(BF16)
        live = (n > 0).astype(F32)
        for grp in range(POOL_GROUPS):
            c0 = grp * POOL_GROUP_DIM
            cs = slice(c0, c0 + POOL_GROUP_DIM)
            pooled, _ = _pool_group(xb_ref[:, cs], xp_ref[:, cs] * live, n * t, grp)
            yb = _dot(pooled.astype(BF16), pw_ref[grp], NN) + pb_ref[:, cs]
            zb = zb_ref[:, cs]
            o_ref[:, D_MODEL + c0:D_MODEL + c0 + POOL_GROUP_DIM] = (yb * ps_ref[:, cs] * (zb * _sigmoid(zb))).astype(BF16)

    return pl.pallas_call(
        body, name=name, grid=(s // t,),
        in_specs=[col(0), col(1), col(2), col(3), col(4), prev,
                  _full((GMLP_HEADS, t, t)), _full((t, LANES)), _full((1, GMLP_HEAD_DIM)), _full((1, GMLP_HEAD_DIM)),
                  _full((POOL_GROUPS, POOL_GROUP_DIM, POOL_GROUP_DIM)), _full((1, D_MODEL)), _full((1, D_MODEL))],
        out_specs=pl.BlockSpec((t, 2 * D_MODEL), lambda n: (n, 0)),
        out_shape=jax.ShapeDtypeStruct((s, 2 * D_MODEL), BF16),
        compiler_params=_params("parallel"),
    )(proj, proj, proj, proj, proj, proj, ws, bs_t, ng, nb, pool_w, pool_b, pool_scale)


def _even_bwd(proj, dmix, ws, ws_t, bs_t, ng, nb, pool_w, pool_b, pool_scale, *, name):
    s = proj.shape[0]
    t = GMLP_BLOCK
    nblk = s // t
    col, per, prev = _even_specs(t)
    nxt = lambda j: pl.BlockSpec((POOL_HALO, D_MODEL), lambda n: (jnp.minimum((n + 1) * per, nblk * per - 1), j))

    def body(u_ref, v_ref, za_ref, xb_ref, zb_ref, xp_ref, zn_ref, da_ref, db_ref, dbn_ref,
             ws_ref, wst_ref, bs_ref, ng_ref, nb_ref, pw_ref, pb_ref, ps_ref,
             dp_ref, gws_ref, gbs_ref, gng_ref, gnb_ref, gpw_ref, gpb_ref, gps_ref):
        n = pl.program_id(0)

        @pl.when(n == 0)
        def _():
            for r in (gws_ref, gbs_ref, gng_ref, gnb_ref, gpw_ref, gpb_ref, gps_ref):
                r[...] = jnp.zeros_like(r)

        mask, mask_t = _chunk_mask(), _chunk_mask(True)
        lane = lax.broadcasted_iota(jnp.int32, (t, LANES), 1)
        ngv, nbv = ng_ref[...], nb_ref[...]
        for h in range(GMLP_HEADS):
            c0 = h * GMLP_HEAD_DIM
            cs = slice(c0, c0 + GMLP_HEAD_DIM)
            w_bf = jnp.where(mask, ws_ref[h], 0.0).astype(BF16)
            wt_bf = jnp.where(mask_t, wst_ref[h], 0.0).astype(BF16)
            xhat, rstd, vn, sv = _gmlp_head(v_ref[:, cs], ngv, nbv, w_bf)
            sv = sv + bs_ref[:, h:h + 1]
            za, u, da = za_ref[:, cs], u_ref[:, cs], da_ref[:, cs]
            sg = _sigmoid(za)
            sl = za * sg
            dp_ref[:, cs] = (da * sv * sl).astype(BF16)
            dp_ref[:, 2 * D_MODEL + c0:2 * D_MODEL + c0 + GMLP_HEAD_DIM] = (
                da * u * sv * (sg * (1.0 + za * (1.0 - sg)))).astype(BF16)
            dsv = da * u * sl
            gbs_ref[...] += jnp.where(lane == h, jnp.sum(dsv, axis=1, keepdims=True), 0.0)
            dsv_bf = dsv.astype(BF16)
            gws_ref[h] += jnp.where(mask, _dot(dsv_bf, vn, NT), 0.0)
            dvn = _dot(wt_bf, dsv_bf, NN)
            dp_ref[:, D_MODEL + c0:D_MODEL + c0 + GMLP_HEAD_DIM] = _ln_bwd_rows(dvn, xhat, rstd, ngv).astype(BF16)
            gng_ref[...] += _colsum(dvn * xhat)
            gnb_ref[...] += _colsum(dvn)
        live_prev = (n > 0).astype(F32)
        live_next = (n < nblk - 1).astype(F32)
        for grp in range(POOL_GROUPS):
            c0 = grp * POOL_GROUP_DIM
            cs = slice(c0, c0 + POOL_GROUP_DIM)
            xb = xb_ref[:, cs]
            pooled, cnt = _pool_group(xb, xp_ref[:, cs] * live_prev, n * t, grp)
            pooled_bf = pooled.astype(BF16)
            pw = pw_ref[grp]
            yb = _dot(pooled_bf, pw, NN) + pb_ref[:, cs]
            ps = ps_ref[:, cs]
            zb, db = zb_ref[:, cs], db_ref[:, cs]
            sg = _sigmoid(zb)
            sl = zb * sg
            dp_ref[:, 4 * D_MODEL + c0:4 * D_MODEL + c0 + POOL_GROUP_DIM] = (
                db * yb * ps * (sg * (1.0 + zb * (1.0 - sg)))).astype(BF16)
            dsl = db * sl
            dy = dsl * ps
            gps_ref[:, cs] += _colsum(dsl * yb)
            gpb_ref[:, cs] += _colsum(dy)
            dy_bf = dy.astype(BF16)
            gpw_ref[grp] += _dot(pooled_bf, dy_bf, TN)
            r = _dot(dy_bf, pw, NT)
            zn = zn_ref[:, cs]
            dyn = (dbn_ref[:, cs] * (zn * _sigmoid(zn)) * ps * live_next).astype(BF16)
            rn = _dot(dyn, pw, NT) / _pool_counts((n + 1) * t, POOL_HALO, POOL_WINDOWS[grp])
            ext = jnp.concatenate([r / cnt, rn], axis=0)
            dxb = _window_sum(ext, grp + 1, True)[:t, :] - r
            dp_ref[:, 3 * D_MODEL + c0:3 * D_MODEL + c0 + POOL_GROUP_DIM] = dxb.astype(BF16)

    out_shape = [
        jax.ShapeDtypeStruct((s, EVEN_IN), BF16),
        jax.ShapeDtypeStruct((GMLP_HEADS, t, t), F32), jax.ShapeDtypeStruct((t, LANES), F32),
        jax.ShapeDtypeStruct((1, GMLP_HEAD_DIM), F32), jax.ShapeDtypeStruct((1, GMLP_HEAD_DIM), F32),
        jax.ShapeDtypeStruct((POOL_GROUPS, POOL_GROUP_DIM, POOL_GROUP_DIM), F32),
        jax.ShapeDtypeStruct((1, D_MODEL), F32), jax.ShapeDtypeStruct((1, D_MODEL), F32),
    ]
    return pl.pallas_call(
        body, name=name, grid=(nblk,),
        in_specs=[col(0), col(1), col(2), col(3), col(4), prev, nxt(4),
                  pl.BlockSpec((t, D_MODEL), lambda n: (n, 0)), pl.BlockSpec((t, D_MODEL), lambda n: (n, 1)), nxt(1),
                  _full((GMLP_HEADS, t, t)), _full((GMLP_HEADS, t, t)), _full((t, LANES)),
                  _full((1, GMLP_HEAD_DIM)), _full((1, GMLP_HEAD_DIM)),
                  _full((POOL_GROUPS, POOL_GROUP_DIM, POOL_GROUP_DIM)), _full((1, D_MODEL)), _full((1, D_MODEL))],
        out_specs=[pl.BlockSpec((t, EVEN_IN), lambda n: (n, 0))] + [_full(o.shape) for o in out_shape[1:]],
        out_shape=out_shape,
        compiler_params=_params("arbitrary"),
    )(proj, proj, proj, proj, proj, proj, proj, dmix, dmix, dmix, ws, ws_t, bs_t, ng, nb, pool_w, pool_b, pool_scale)


def _half_swap(v):
    lane = lax.broadcasted_iota(jnp.int32, v.shape, 1)
    return jnp.where(lane % MLA_ROPE < MLA_ROPE // 2, pltpu.roll(v, LANES - MLA_ROPE // 2, 1), pltpu.roll(v, MLA_ROPE // 2, 1))


def _rope(v, cos, sin_signed):
    return v * cos + _half_swap(v) * sin_signed


def _rope_bwd(d, cos, sin_signed):
    return d * cos + _half_swap(d * sin_signed)


def _rms(v, g):
    r = lax.rsqrt(jnp.mean(v * v, axis=-1, keepdims=True) + LN_EPS)
    return v * r * g, r


def _rms_bwd(dy, v, r, g):
    u = dy * g
    return r * u - v * (r * r * r) * jnp.mean(u * v, axis=-1, keepdims=True)


def _mla_prep(proj, gq, gkv, cos, sin_signed, *, name):
    s = proj.shape[0]
    ts = _tile(s, 256)

    def body(p_ref, gq_ref, gkv_ref, c_ref, s_ref, q_ref, k_ref):
        qcn, _ = _rms(p_ref[:, :MLA_Q_RANK], gq_ref[...])
        kvn, _ = _rms(p_ref[:, MLA_Q_RANK:MLA_Q_RANK + MLA_KV_RANK], gkv_ref[...])
        kr = _rope(p_ref[:, MLA_Q_RANK + MLA_KV_RANK:], c_ref[...], s_ref[...])
        q_ref[...] = qcn.astype(BF16)
        k_ref[...] = jnp.concatenate([kvn, kr], axis=1).astype(BF16)

    return pl.pallas_call(
        body, name=name, grid=(s // ts,),
        in_specs=[_row_spec(ts, ODD_SMALL_PAD), _vec_spec(MLA_Q_RANK), _vec_spec(MLA_KV_RANK), _row_spec(ts, LANES), _row_spec(ts, LANES)],
        out_specs=[_row_spec(ts, MLA_Q_RANK), _row_spec(ts, QK_PAD)],
        out_shape=[jax.ShapeDtypeStruct((s, MLA_Q_RANK), BF16), jax.ShapeDtypeStruct((s, QK_PAD), BF16)],
        compiler_params=_params("parallel"),
    )(proj, gq, gkv, cos, sin_signed)


def _mla_prep_bwd(proj, dqcn, dkv, gq, gkv, cos, sin_signed, *, name):
    s = proj.shape[0]
    ts = _tile(s, 256)

    def body(p_ref, dq_ref, dkv_ref, gq_ref, gkv_ref, c_ref, s_ref, ds_ref, ggq_ref, ggkv_ref):
        @pl.when(pl.program_id(0) == 0)
        def _():
            ggq_ref[...] = jnp.zeros_like(ggq_ref)
            ggkv_ref[...] = jnp.zeros_like(ggkv_ref)

        qc = p_ref[:, :MLA_Q_RANK]
        kvc = p_ref[:, MLA_Q_RANK:MLA_Q_RANK + MLA_KV_RANK]
        _, rq = _rms(qc, gq_ref[...])
        _, rkv = _rms(kvc, gkv_ref[...])
        dq = dq_ref[...]
        dkvn = dkv_ref[:, :MLA_KV_RANK]
        ggq_ref[...] += _colsum(dq * qc * rq)
        ggkv_ref[...] += _colsum(dkvn * kvc * rkv)
        dkr = _rope_bwd(dkv_ref[:, MLA_KV_RANK:], c_ref[...], s_ref[...])
        ds_ref[...] = jnp.concatenate(
            [_rms_bwd(dq, qc, rq, gq_ref[...]), _rms_bwd(dkvn, kvc, rkv, gkv_ref[...]), dkr], axis=1).astype(BF16)

    return pl.pallas_call(
        body, name=name, grid=(s // ts,),
        in_specs=[_row_spec(ts, ODD_SMALL_PAD), _row_spec(ts, MLA_Q_RANK), _row_spec(ts, QK_PAD),
                  _vec_spec(MLA_Q_RANK), _vec_spec(MLA_KV_RANK), _row_spec(ts, LANES), _row_spec(ts, LANES)],
        out_specs=[_row_spec(ts, ODD_SMALL_PAD), _vec_spec(MLA_Q_RANK), _vec_spec(MLA_KV_RANK)],
        out_shape=[jax.ShapeDtypeStruct((s, ODD_SMALL_PAD), BF16), jax.ShapeDtypeStruct((1, MLA_Q_RANK), F32),
                   jax.ShapeDtypeStruct((1, MLA_KV_RANK), F32)],
        compiler_params=_params("arbitrary"),
    )(proj, dqcn, dkv, gq, gkv, cos, sin_signed)


def _q_build(q_nope, q_rope_pre, wuk_hdr, cos, sin_signed, *, name):
    s = q_nope.shape[0]
    ts = _tile(s, 256)

    def body(qn_ref, qr_ref, w_ref, c_ref, s_ref, o_ref):
        r = _rope(qr_ref[...], c_ref[...], s_ref[...])
        lane = lax.broadcasted_iota(jnp.int32, (ts, LANES), 1)
        for j in range(2):
            ql = _dot(qn_ref[:, j * MLA_NOPE:(j + 1) * MLA_NOPE], w_ref[j], NN)
            rr = r if j == 0 else pltpu.roll(r, MLA_ROPE, 1)
            o_ref[j] = jnp.concatenate([ql, jnp.where(lane < MLA_ROPE, rr, 0.0)], axis=1).astype(BF16)

    return pl.pallas_call(
        body, name=name, grid=(s // ts, MLA_HEADS // 2),
        in_specs=[pl.BlockSpec((ts, 2 * MLA_NOPE), lambda i, p: (i, p)), pl.BlockSpec((ts, LANES), lambda i, p: (i, p)),
                  pl.BlockSpec((2, MLA_NOPE, MLA_KV_RANK), lambda i, p: (p, 0, 0)),
                  pl.BlockSpec((ts, LANES), lambda i, p: (i, 0)), pl.BlockSpec((ts, LANES), lambda i, p: (i, 0))],
        out_specs=pl.BlockSpec((2, ts, QK_PAD), lambda i, p: (p, i, 0)),
        out_shape=jax.ShapeDtypeStruct((MLA_HEADS, s, QK_PAD), BF16),
        compiler_params=_params("parallel", "parallel"),
    )(q_nope, q_rope_pre, wuk_hdr, cos, sin_signed)


def _q_bwd(dq, q_nope, wuk_hrd, cos, sin_signed, *, name):
    s = q_nope.shape[0]
    ts = _tile(s, 256)

    def body(dq_ref, qn_ref, w_ref, c_ref, s_ref, dn_ref, dr_ref, gw_ref):
        @pl.when(pl.program_id(1) == 0)
        def _():
            gw_ref[...] = jnp.zeros_like(gw_ref)

        lane = lax.broadcasted_iota(jnp.int32, (ts, LANES), 1)
        for j in range(2):
            dql = dq_ref[j, :, :MLA_KV_RANK]
            dn_ref[:, j * MLA_NOPE:(j + 1) * MLA_NOPE] = _dot(dql, w_ref[j], NN).astype(BF16)
            gw_ref[j] += _dot(dql, qn_ref[:, j * MLA_NOPE:(j + 1) * MLA_NOPE], TN)
        hi0 = dq_ref[0, :, MLA_KV_RANK:].astype(F32)
        hi1 = dq_ref[1, :, MLA_KV_RANK:].astype(F32)
        d = jnp.where(lane < MLA_ROPE, hi0, pltpu.roll(hi1, MLA_ROPE, 1))
        dr_ref[...] = _rope_bwd(d, c_ref[...], s_ref[...]).astype(BF16)

    return pl.pallas_call(
        body, name=name, grid=(MLA_HEADS // 2, s // ts),
        in_specs=[pl.BlockSpec((2, ts, QK_PAD), lambda p, i: (p, i, 0)), pl.BlockSpec((ts, 2 * MLA_NOPE), lambda p, i: (i, p)),
                  pl.BlockSpec((2, MLA_KV_RANK, MLA_NOPE), lambda p, i: (p, 0, 0)),
                  pl.BlockSpec((ts, LANES), lambda p, i: (i, 0)), pl.BlockSpec((ts, LANES), lambda p, i: (i, 0))],
        out_specs=[pl.BlockSpec((ts, 2 * MLA_NOPE), lambda p, i: (i, p)), pl.BlockSpec((ts, LANES), lambda p, i: (i, p)),
                   pl.BlockSpec((2, MLA_KV_RANK, MLA_NOPE), lambda p, i: (p, 0, 0))],
        out_shape=[jax.ShapeDtypeStruct((s, MLA_HEADS * MLA_NOPE), BF16), jax.ShapeDtypeStruct((s, MLA_HEADS * MLA_ROPE), BF16),
                   jax.ShapeDtypeStruct((MLA_HEADS, MLA_KV_RANK, MLA_NOPE), F32)],
        compiler_params=_params("parallel", "arbitrary"),
    )(dq, q_nope, wuk_hrd, cos, sin_signed)


ATTN_BQ = 128
ATTN_BK = 512


def _diag_mask(rows, bq, bk, q0, k0):
    qc = (q0 + lax.broadcasted_iota(jnp.int32, (rows, bk), 0) % bq) // CHUNK
    kc = (k0 + lax.broadcasted_iota(jnp.int32, (rows, bk), 1)) // CHUNK
    return kc <= qc


def _attn_fwd(q, k, *, name):
    nh, s, dk = q.shape
    bq, bk = _tile(s, ATTN_BQ), _tile(s, ATTN_BK)
    rows = nh * bq

    def body(q_ref, k_ref, o_ref, lse_ref):
        i = pl.program_id(0)
        qb = q_ref[...].reshape(rows, dk)
        n_before = (i * bq) // bk

        def step(j, carry, masked):
            m, l, acc = carry
            k0 = pl.multiple_of(j * bk, bk)
            kb = k_ref[pl.ds(k0, bk), :]
            sc = _dot(qb, kb, NT) * ATTN_SCALE
            if masked:
                sc = jnp.where(_diag_mask(rows, bq, bk, i * bq, k0), sc, NEG)
            m_new = jnp.maximum(m, jnp.max(sc, axis=1, keepdims=True))
            p = jnp.exp(sc - m_new)
            a = jnp.exp(m - m_new)
            l = a * l + jnp.sum(p, axis=1, keepdims=True)
            acc = a * acc + _dot(p.astype(BF16), kb[:, :MLA_KV_RANK], NN)
            return m_new, l, acc

        init = (jnp.full((rows, 1), NEG, F32), jnp.zeros((rows, 1), F32), jnp.zeros((rows, MLA_KV_RANK), F32))
        carry = lax.fori_loop(0, n_before, lambda j, c: step(j, c, False), init)
        m, l, acc = step(n_before, carry, True)
        o_ref[...] = (acc / l).astype(BF16).reshape(nh, bq, MLA_KV_RANK)
        lse_ref[...] = jnp.broadcast_to(m + jnp.log(l), (rows, LANES)).reshape(nh, bq, LANES)

    return pl.pallas_call(
        body, name=name, grid=(s // bq,),
        in_specs=[pl.BlockSpec((nh, bq, dk), lambda i: (0, i, 0)), pl.BlockSpec((s, dk), lambda i: (0, 0))],
        out_specs=[pl.BlockSpec((nh, bq, MLA_KV_RANK), lambda i: (0, i, 0)), pl.BlockSpec((nh, bq, LANES), lambda i: (0, i, 0))],
        out_shape=[jax.ShapeDtypeStruct((nh, s, MLA_KV_RANK), BF16), jax.ShapeDtypeStruct((nh, s, LANES), F32)],
        compiler_params=_params("parallel"),
    )(q, k)


def _attn_bwd(q, k, do, o, lse, *, name):
    nh, s, dk = q.shape
    bq, bk = _tile(s, ATTN_BQ), _tile(s, ATTN_BK)
    rows = nh * bq

    def body(q_ref, k_ref, do_ref, o_ref, lse_ref, dq_ref, dkv_ref):
        i = pl.program_id(0)
        n_before = (i * bq) // bk

        @pl.when(i == 0)
        def _():
            dkv_ref[...] = jnp.zeros_like(dkv_ref)

        qb = q_ref[...].reshape(rows, dk)
        dob = do_ref[...].reshape(rows, MLA_KV_RANK)
        lse_b = lse_ref[...].reshape(rows, LANES)[:, :1]
        delta = jnp.sum(dob.astype(F32) * o_ref[...].reshape(rows, MLA_KV_RANK).astype(F32), axis=1, keepdims=True)

        def step(j, dq, masked):
            j0 = pl.multiple_of(j * bk, bk)
            kb = k_ref[pl.ds(j0, bk), :]
            sc = _dot(qb, kb, NT) * ATTN_SCALE
            if masked:
                sc = jnp.where(_diag_mask(rows, bq, bk, i * bq, j0), sc, NEG)
            p = jnp.exp(sc - lse_b)
            dp = _dot(dob, kb[:, :MLA_KV_RANK], NT)
            ds_bf = (p * (dp - delta) * ATTN_SCALE).astype(BF16)
            dkv_ref[pl.ds(j0, bk), :] += _dot(ds_bf, qb, TN)
            dkv_ref[pl.ds(j0, bk), :MLA_KV_RANK] += _dot(p.astype(BF16), dob, TN)
            return dq + _dot(ds_bf, kb, NN)

        dq = lax.fori_loop(0, n_before, lambda j, c: step(j, c, False), jnp.zeros((rows, dk), F32))
        dq = step(n_before, dq, True)
        dq_ref[...] = dq.astype(BF16).reshape(nh, bq, dk)

    blk = lambda w: pl.BlockSpec((nh, bq, w), lambda i: (0, i, 0))
    return pl.pallas_call(
        body, name=name, grid=(s // bq,),
        in_specs=[blk(dk), pl.BlockSpec((s, dk), lambda i: (0, 0)), blk(MLA_KV_RANK), blk(MLA_KV_RANK), blk(LANES)],
        out_specs=[blk(dk), pl.BlockSpec((s, dk), lambda i: (0, 0))],
        out_shape=[jax.ShapeDtypeStruct((nh, s, dk), BF16), jax.ShapeDtypeStruct((s, dk), F32)],
        compiler_params=_params("arbitrary"),
    )(q, k, do, o, lse)


HEAD_GROUP = 4


def _o_build(o_lat, wuv_hrv, proj, *, name):
    s = proj.shape[0]
    ts = _tile(s, 256)
    w = HEAD_GROUP * MLA_V

    def body(ol_ref, w_ref, z_ref, og_ref):
        for j in range(HEAD_GROUP):
            cs = slice(j * MLA_V, (j + 1) * MLA_V)
            z = z_ref[:, cs]
            og_ref[:, cs] = (_dot(ol_ref[j], w_ref[j], NN) * (z * _sigmoid(z))).astype(BF16)

    return pl.pallas_call(
        body, name=name, grid=(s // ts, MLA_HEADS // HEAD_GROUP),
        in_specs=[pl.BlockSpec((HEAD_GROUP, ts, MLA_KV_RANK), lambda i, g: (g, i, 0)),
                  pl.BlockSpec((HEAD_GROUP, MLA_KV_RANK, MLA_V), lambda i, g: (g, 0, 0)),
                  pl.BlockSpec((ts, w), lambda i, g: (i, g + 1))],
        out_specs=pl.BlockSpec((ts, w), lambda i, g: (i, g)),
        out_shape=jax.ShapeDtypeStruct((s, MLA_WIDTH), BF16),
        compiler_params=_params("parallel", "parallel"),
    )(o_lat, wuv_hrv, proj)


def _o_bwd(dg, proj, o_lat, wuv_hrv, wuv_hvr, *, name):
    s = proj.shape[0]
    ts = _tile(s, 256)
    w = HEAD_GROUP * MLA_V

    def body(dg_ref, z_ref, ol_ref, w_ref, wt_ref, dol_ref, dz_ref, gw_ref):
        @pl.when(pl.program_id(1) == 0)
        def _():
            gw_ref[...] = jnp.zeros_like(gw_ref)

        for j in range(HEAD_GROUP):
            cs = slice(j * MLA_V, (j + 1) * MLA_V)
            z, dgj, ol = z_ref[:, cs], dg_ref[:, cs], ol_ref[j]
            sg = _sigmoid(z)
            o = _dot(ol, w_ref[j], NN)
            dz_ref[:, cs] = (dgj * o * (sg * (1.0 + z * (1.0 - sg)))).astype(BF16)
            do_bf = (dgj * (z * sg)).astype(BF16)
            dol_ref[j] = _dot(do_bf, wt_ref[j], NN).astype(BF16)
            gw_ref[j] += _dot(ol, do_bf, TN)

    hs = lambda a, b: pl.BlockSpec((HEAD_GROUP, a, b), lambda g, i: (g, 0, 0))
    return pl.pallas_call(
        body, name=name, grid=(MLA_HEADS // HEAD_GROUP, s // ts),
        in_specs=[pl.BlockSpec((ts, w), lambda g, i: (i, g)), pl.BlockSpec((ts, w), lambda g, i: (i, g + 1)),
                  pl.BlockSpec((HEAD_GROUP, ts, MLA_KV_RANK), lambda g, i: (g, i, 0)),
                  hs(MLA_KV_RANK, MLA_V), hs(MLA_V, MLA_KV_RANK)],
        out_specs=[pl.BlockSpec((HEAD_GROUP, ts, MLA_KV_RANK), lambda g, i: (g, i, 0)),
                   pl.BlockSpec((ts, w), lambda g, i: (i, g)), hs(MLA_KV_RANK, MLA_V)],
        out_shape=[jax.ShapeDtypeStruct((MLA_HEADS, s, MLA_KV_RANK), BF16), jax.ShapeDtypeStruct((s, MLA_WIDTH), BF16),
                   jax.ShapeDtypeStruct((MLA_HEADS, MLA_KV_RANK, MLA_V), F32)],
        compiler_params=_params("parallel", "arbitrary"),
    )(dg, proj, o_lat, wuv_hrv, wuv_hvr)


def _ada_mod(c_all, ada_w, ada_b_sh, *, name):
    nl, _, cols = ada_w.shape

    def body(c_ref, w_ref, b_ref, o_ref):
        c = c_ref[...]
        cond = (c * _sigmoid(c)).astype(BF16)
        for l in range(nl):
            o_ref[l] = _dot(cond, w_ref[l].astype(BF16), NN) + b_ref[l]

    return pl.pallas_call(
        body, name=name, out_shape=jax.ShapeDtypeStruct((nl, c_all.shape[0], cols), F32),
        compiler_params=_params(),
    )(c_all, ada_w, ada_b_sh)


def _ada_grad(c_all_t, dmod_sh, *, name):
    nl, _, cols = dmod_sh.shape
    d = c_all_t.shape[0]

    def body(c_ref, dm_ref, gw_ref):
        c = c_ref[...]
        cond_t = c * _sigmoid(c)
        for l in range(nl):
            gw_ref[l] = lax.dot_general(cond_t, dm_ref[l], (NN, ((), ())), precision=lax.Precision.HIGHEST,
                                        preferred_element_type=F32)

    return pl.pallas_call(
        body, name=name, out_shape=jax.ShapeDtypeStruct((nl, d, cols), F32), compiler_params=_params(),
    )(c_all_t, dmod_sh)


def _sum_devices(parts, *, name):
    def body(p_ref, o_ref):
        acc = p_ref[0]
        for k in range(1, parts.shape[0]):
            acc = acc + p_ref[k]
        o_ref[...] = acc

    return pl.pallas_call(body, name=name, out_shape=jax.ShapeDtypeStruct(parts.shape[1:], F32), compiler_params=_params())(parts)


def _adamw_math(w, g, m, v):
    c1 = 1.0 - ADAM_B1 ** ADAM_STEP
    c2 = 1.0 - ADAM_B2 ** ADAM_STEP
    nm = ADAM_B1 * m + (1.0 - ADAM_B1) * g
    nv = ADAM_B2 * v + (1.0 - ADAM_B2) * (g * g)
    return -ADAM_LR * ((nm / c1) / (jnp.sqrt(nv / c2) + ADAM_EPS) + ADAM_WD * w), nm, nv


ADAMW_BLOCK_BYTES = 1 << 20


def _adamw(w, g, m, v, *, name):
    shape = w.shape
    a, b = shape[-2], shape[-1]
    lead = 1
    for dim in shape[:-2]:
        lead *= dim
    row_bytes = 4 * b
    if a * row_bytes <= ADAMW_BLOCK_BYTES:
        ta = a
        tl = max(1, min(lead, ADAMW_BLOCK_BYTES // (a * row_bytes)))
        while lead % tl:
            tl -= 1
    else:
        tl = 1
        ta = _tile(a, 256)
    to3 = lambda t: t.reshape(lead, a, b)

    def body(w_ref, g_ref, m_ref, v_ref, d_ref, nm_ref, nv_ref):
        d_ref[...], nm_ref[...], nv_ref[...] = _adamw_math(w_ref[...], g_ref[...], m_ref[...], v_ref[...])

    spec = pl.BlockSpec((tl, ta, b), lambda i, j: (i, j, 0))
    out = jax.ShapeDtypeStruct((lead, a, b), F32)
    res = pl.pallas_call(
        body, name=name, grid=(lead // tl, a // ta), in_specs=[spec] * 4, out_specs=[spec] * 3, out_shape=[out] * 3,
        compiler_params=_params("parallel", "parallel"),
    )(to3(w), to3(g), to3(m), to3(v))
    return [r.reshape(shape) for r in res]


def _adamw_small(ws, gs, ms, vs, *, name):
    n = len(ws)

    def body(*refs):
        for k in range(n):
            w_ref, g_ref, m_ref, v_ref = (refs[j * n + k] for j in range(4))
            d_ref, nm_ref, nv_ref = (refs[(4 + j) * n + k] for j in range(3))
            d_ref[...], nm_ref[...], nv_ref[...] = _adamw_math(w_ref[...], g_ref[...], m_ref[...], v_ref[...])

    outs = [jax.ShapeDtypeStruct(w.shape, F32) for w in ws]
    res = pl.pallas_call(body, name=name, out_shape=outs * 3, compiler_params=_params())(*ws, *gs, *ms, *vs)
    return res[:n], res[n:2 * n], res[2 * n:]


def _flip(v, bit):
    return 1 - v if bit else v


CHIP_DELTAS = ((1, 0), (0, 1), (1, 1))
SUM_ROWS = 32


def _all_gather_chips(shard, *, name):
    def body(x_ref, o_ref, send_sems, recv_sems, local_sem):
        x, y, c = lax.axis_index("x"), lax.axis_index("y"), lax.axis_index("c")
        mine = pltpu.make_async_copy(x_ref, o_ref.at[2 * x + y], local_sem)
        mine.start()

        def copy(k):
            tx, ty = _flip(x, CHIP_DELTAS[k][0]), _flip(y, CHIP_DELTAS[k][1])
            send = pltpu.make_async_remote_copy(src_ref=x_ref, dst_ref=o_ref.at[2 * x + y], send_sem=send_sems.at[k],
                                                recv_sem=recv_sems.at[k], device_id=(tx, ty, c), device_id_type=MESH)
            recv = pltpu.make_async_remote_copy(src_ref=x_ref, dst_ref=o_ref.at[2 * tx + ty], send_sem=send_sems.at[k],
                                                recv_sem=recv_sems.at[k], device_id=(tx, ty, c), device_id_type=MESH)
            return send, recv

        pairs = [copy(k) for k in range(3)]
        for send, _ in pairs:
            send.start()
        for _, recv in pairs:
            recv.wait_recv()
        for send, _ in pairs:
            send.wait_send()
        mine.wait()

    return pl.pallas_call(
        body, name=name, out_shape=jax.ShapeDtypeStruct((N_CHIPS,) + shard.shape, shard.dtype),
        in_specs=[HBM], out_specs=HBM,
        scratch_shapes=[pltpu.SemaphoreType.DMA((3,)), pltpu.SemaphoreType.DMA((3,)), pltpu.SemaphoreType.DMA(())],
    )(shard)


def _gather_weights(shards, *, name):
    n = len(shards)

    def body(*refs):
        w_refs, o_refs = refs[:n], refs[n:2 * n]
        ici_send, ici_recv, d2d_send, d2d_recv, local_sems = refs[2 * n:]
        x, y, c = lax.axis_index("x"), lax.axis_index("y"), lax.axis_index("c")
        me = 2 * x + y
        peers = [(_flip(x, dx), _flip(y, dy)) for dx, dy in CHIP_DELTAS]
        locals_ = [pltpu.make_async_copy(w_refs[k], o_refs[k].at[me], local_sems.at[k]) for k in range(n)]
        for cp in locals_:
            cp.start()

        def rows(k, which):
            half = shards[k].shape[0] // 2
            return pl.ds(pl.multiple_of(which * half, half), half)

        def over_chips(k, d, slot):
            tx, ty = peers[d]
            return pltpu.make_async_remote_copy(
                src_ref=w_refs[k].at[rows(k, c)], dst_ref=o_refs[k].at[slot, rows(k, c)], send_sem=ici_send.at[k, d],
                recv_sem=ici_recv.at[k, d], device_id=(tx, ty, c), device_id_type=MESH)

        def to_sibling(k, d, which):
            tx, ty = peers[d]
            at = o_refs[k].at[2 * tx + ty, rows(k, which)]
            return pltpu.make_async_remote_copy(src_ref=at, dst_ref=at, send_sem=d2d_send.at[k, d], recv_sem=d2d_recv.at[k, d],
                                                device_id=(x, y, 1 - c), device_id_type=MESH)

        sends = [over_chips(k, d, me) for k in range(n) for d in range(3)]
        for cp in sends:
            cp.start()
        passed = []
        for k in range(n):
            for d in range(3):
                over_chips(k, d, 2 * peers[d][0] + peers[d][1]).wait_recv()
                passed.append(to_sibling(k, d, c))
                passed[-1].start()
        for k in range(n):
            for d in range(3):
                to_sibling(k, d, 1 - c).wait_recv()
        for cp in sends + passed:
            cp.wait_send()
        for cp in locals_:
            cp.wait()

    return pl.pallas_call(
        body, name=name, out_shape=[jax.ShapeDtypeStruct((N_CHIPS,) + w.shape, w.dtype) for w in shards],
        in_specs=[HBM] * n, out_specs=[HBM] * n,
        scratch_shapes=[pltpu.SemaphoreType.DMA((n, 3))] * 4 + [pltpu.SemaphoreType.DMA((n,))],
    )(*shards)


def _add_into(dst_ref, src_ref):
    ns, r, _ = dst_ref.shape
    step = SUM_ROWS if r % SUM_ROWS == 0 else r
    for s in range(ns):
        def tile(t, carry):
            at = pl.ds(pl.multiple_of(t * step, step), step)
            dst_ref[s, at, :] = (dst_ref[s, at, :].astype(F32) + src_ref[s, at, :].astype(F32)).astype(dst_ref.dtype)
            return carry
        lax.fori_loop(0, r // step, tile, 0)


def _reduce_sibling(grads, *, name):
    n = len(grads)

    def body(*refs):
        g_refs, o_refs = refs[:n], refs[n:2 * n]
        mine, got = refs[2 * n:3 * n], refs[3 * n:4 * n]
        send_sems, recv_sems, load_sems, store_sems = refs[4 * n:]
        x, y, c = lax.axis_index("x"), lax.axis_index("y"), lax.axis_index("c")
        loads = [pltpu.make_async_copy(g_refs[k].at[:, c], mine[k], load_sems.at[k]) for k in range(n)]
        swaps = [pltpu.make_async_remote_copy(src_ref=g_refs[k].at[:, 1 - c], dst_ref=got[k], send_sem=send_sems.at[k],
                                              recv_sem=recv_sems.at[k], device_id=(x, y, 1 - c), device_id_type=MESH)
                 for k in range(n)]
        for cp in loads + swaps:
            cp.start()
        stores = []
        for k in range(n):
            loads[k].wait()
            swaps[k].wait_recv()
            _add_into(mine[k], got[k])
            stores.append(pltpu.make_async_copy(mine[k], o_refs[k], store_sems.at[k]))
            stores[-1].start()
        for k in range(n):
            swaps[k].wait_send()
            stores[k].wait()

    half = [jax.ShapeDtypeStruct((g.shape[0],) + g.shape[2:], g.dtype) for g in grads]
    return pl.pallas_call(
        body, name=name, out_shape=half, in_specs=[HBM] * n, out_specs=[HBM] * n,
        scratch_shapes=[pltpu.VMEM(h.shape, h.dtype) for h in half] * 2 + [pltpu.SemaphoreType.DMA((n,))] * 4,
        compiler_params=_params(),
    )(*grads)


def _reduce_chips(parts, *, name):
    n = len(parts)

    def body(*refs):
        p_refs, o_refs = refs[:n], refs[n:2 * n]
        got, total = refs[2 * n:3 * n], refs[3 * n:4 * n]
        send_sems, recv_sems, load_sems, share_send, share_recv, store_sems = refs[4 * n:]
        x, y, c = lax.axis_index("x"), lax.axis_index("y"), lax.axis_index("c")
        me = 2 * x + y
        peers = [(_flip(x, dx), _flip(y, dy)) for dx, dy in CHIP_DELTAS]

        def over_chips(k, d, src_slot, dst_slot):
            tx, ty = peers[d]
            return pltpu.make_async_remote_copy(
                src_ref=p_refs[k].at[src_slot], dst_ref=got[k].at[dst_slot], send_sem=send_sems.at[k, d],
                recv_sem=recv_sems.at[k, d], device_id=(tx, ty, c), device_id_type=MESH)

        loads = [pltpu.make_async_copy(p_refs[k].at[me], got[k].at[me], load_sems.at[k]) for k in range(n)]
        sends = [over_chips(k, d, 2 * peers[d][0] + peers[d][1], me) for k in range(n) for d in range(3)]
        for cp in loads + sends:
            cp.start()
        shares, stores = [], []
        for k in range(n):
            loads[k].wait()
            for d in range(3):
                slot = 2 * peers[d][0] + peers[d][1]
                over_chips(k, d, slot, slot).wait_recv()
            r = total[k].shape[0]
            step = SUM_ROWS if r % SUM_ROWS == 0 else r

            def tile(t, carry, k=k, step=step):
                at = pl.ds(pl.multiple_of(t * step, step), step)
                acc = got[k][0, at, :].astype(F32)
                for s in range(1, N_CHIPS):
                    acc = acc + got[k][s, at, :].astype(F32)
                total[k][at, :] = acc
                return carry

            lax.fori_loop(0, r // step, tile, 0)
            stores.append(pltpu.make_async_copy(total[k], o_refs[k].at[c], store_sems.at[k]))
            shares.append(pltpu.make_async_remote_copy(
                src_ref=total[k], dst_ref=o_refs[k].at[c], send_sem=share_send.at[k], recv_sem=share_recv.at[k],
                device_id=(x, y, 1 - c), device_id_type=MESH))
            stores[-1].start()
            shares[-1].start()
        for k in range(n):
            pltpu.make_async_remote_copy(
                src_ref=total[k], dst_ref=o_refs[k].at[1 - c], send_sem=share_send.at[k], recv_sem=share_recv.at[k],
                device_id=(x, y, 1 - c), device_id_type=MESH).wait_recv()
        for cp in sends + shares:
            cp.wait_send()
        for cp in stores:
            cp.wait()

    return pl.pallas_call(
        body, name=name, out_shape=[jax.ShapeDtypeStruct((2,) + p.shape[1:], F32) for p in parts],
        in_specs=[HBM] * n, out_specs=[HBM] * n,
        scratch_shapes=[pltpu.VMEM(p.shape, p.dtype) for p in parts] + [pltpu.VMEM(p.shape[1:], F32) for p in parts]
        + [pltpu.SemaphoreType.DMA((n, 3))] * 2 + [pltpu.SemaphoreType.DMA((n,))] * 4,
        compiler_params=_params(),
    )(*parts)


def _all_gather_devices(rows, *, name):
    deltas = [(dx, dy, dc) for dx in (0, 1) for dy in (0, 1) for dc in (0, 1)][1:]

    def body(x_ref, o_ref, send_sems, recv_sems):
        x, y, c = lax.axis_index("x"), lax.axis_index("y"), lax.axis_index("c")
        me = 4 * x + 2 * y + c
        o_ref[me] = x_ref[...]
        sends, recvs = [], []
        for k, (dx, dy, dc) in enumerate(deltas):
            tx, ty, tc = _flip(x, dx), _flip(y, dy), _flip(c, dc)
            sends.append(pltpu.make_async_remote_copy(src_ref=x_ref, dst_ref=o_ref.at[me], send_sem=send_sems.at[k],
                                                      recv_sem=recv_sems.at[k], device_id=(tx, ty, tc), device_id_type=MESH))
            recvs.append(pltpu.make_async_remote_copy(src_ref=x_ref, dst_ref=o_ref.at[4 * tx + 2 * ty + tc],
                                                      send_sem=send_sems.at[k], recv_sem=recv_sems.at[k],
                                                      device_id=(tx, ty, tc), device_id_type=MESH))
        for cp in sends:
            cp.start()
        for cp in recvs:
            cp.wait_recv()
        for cp in sends:
            cp.wait_send()

    return pl.pallas_call(
        body, name=name, out_shape=jax.ShapeDtypeStruct((N_DEV,) + rows.shape, rows.dtype),
        in_specs=[VMEM], out_specs=VMEM,
        scratch_shapes=[pltpu.SemaphoreType.DMA((N_DEV - 1,)), pltpu.SemaphoreType.DMA((N_DEV - 1,))],
    )(rows)


WEIGHTS = ("ada_w", "ada_b", "ln_g", "ln_b", "e_w_in", "gmlp_norm_g", "gmlp_norm_b", "gmlp_ws", "gmlp_bs", "pool_w",
           "pool_b", "pool_scale", "e_w_out", "o_w_in", "mla_q_norm_g", "mla_kv_norm_g", "mla_w_uq", "mla_w_uk",
           "mla_w_uv", "o_w_out")
SMALL = ("ln_g", "ln_b", "gmlp_norm_g", "gmlp_norm_b", "gmlp_bs", "pool_b", "pool_scale", "mla_kv_norm_g", "mla_q_norm_g")


def _pad_cols(v, n):
    return jnp.concatenate([v, jnp.zeros((v.shape[0], n - v.shape[1]), v.dtype)], axis=1) if n > v.shape[1] else v


def _halves(g):
    return g.reshape(g.shape[0], 2, g.shape[1] // 2, g.shape[2])


def kernel(x, c, positions, ada_w, ada_b, ln_g, ln_b, e_w_in, gmlp_norm_g, gmlp_norm_b, gmlp_ws, gmlp_bs, pool_w, pool_b, pool_scale, e_w_out, o_w_in, mla_q_norm_g, mla_kv_norm_g, mla_w_uq, mla_w_uk, mla_w_uv, o_w_out, loss_target, m_ada_w, m_ada_b, m_ln_g, m_ln_b, m_e_w_in, m_gmlp_norm_g, m_gmlp_norm_b, m_gmlp_ws, m_gmlp_bs, m_pool_w, m_pool_b, m_pool_scale, m_e_w_out, m_o_w_in, m_mla_q_norm_g, m_mla_kv_norm_g, m_mla_w_uq, m_mla_w_uk, m_mla_w_uv, m_o_w_out, v_ada_w, v_ada_b, v_ln_g, v_ln_b, v_e_w_in, v_gmlp_norm_g, v_gmlp_norm_b, v_gmlp_ws, v_gmlp_bs, v_pool_w, v_pool_b, v_pool_scale, v_e_w_out, v_o_w_in, v_mla_q_norm_g, v_mla_kv_norm_g, v_mla_w_uq, v_mla_w_uk, v_mla_w_uv, v_o_w_out):
    args = dict(locals())
    weights = {n: args[n] for n in WEIGHTS}
    mom = {n: args["m_" + n] for n in WEIGHTS}
    var = {n: args["v_" + n] for n in WEIGHTS}
    ax, ay, ac = lax.axis_index("x"), lax.axis_index("y"), lax.axis_index("c")
    chip = 2 * ax + ay
    dev = 2 * chip + ac
    d = D_MODEL
    x2 = x[0]
    target = loss_target[0]
    q_rank_sh = mla_q_norm_g.shape[1]

    shards = [e_w_in[0], pool_w[0].reshape(-1, POOL_GROUP_DIM), e_w_out[0], o_w_in[0],
              mla_w_uq[0].reshape(q_rank_sh, -1), o_w_out[0]]
    w_in0, pool_w_g, w_out0, w_in1_g, w_uq_g, w_out1 = _gather_weights([w.astype(BF16) for w in shards], name="gather_weights")
    pool_w_bf = jnp.transpose(pool_w_g.reshape(N_CHIPS, POOL_GROUPS, -1, POOL_GROUP_DIM), (1, 0, 2, 3)).reshape(
        POOL_GROUPS, POOL_GROUP_DIM, POOL_GROUP_DIM)
    w_out0 = w_out0.reshape(-1, d)
    w_out1 = w_out1.reshape(-1, d)
    w_in1 = jnp.transpose(w_in1_g, (1, 0, 2)).reshape(d, ODD_IN)
    w_in1 = jnp.concatenate([_pad_cols(w_in1[:, :ODD_SMALL], ODD_SMALL_PAD), w_in1[:, ODD_SMALL:]], axis=1)
    w_uq = w_uq_g.reshape(MLA_Q_RANK, MLA_HEADS, MLA_NOPE + MLA_ROPE)
    w_uq_nope = w_uq[:, :, :MLA_NOPE].reshape(MLA_Q_RANK, -1)
    w_uq_rope = w_uq[:, :, MLA_NOPE:].reshape(MLA_Q_RANK, -1)
    wuk_hrd = jnp.transpose(mla_w_uk[0], (1, 0, 2)).astype(BF16)
    wuk_hdr = jnp.transpose(mla_w_uk[0], (1, 2, 0)).astype(BF16)
    wuv_hrv = jnp.transpose(mla_w_uv[0], (1, 0, 2)).astype(BF16)
    wuv_hvr = jnp.transpose(mla_w_uv[0], (1, 2, 0)).astype(BF16)
    ws = gmlp_ws[0]
    ws_t = jnp.transpose(ws, (0, 2, 1))
    bs_t = _pad_cols(gmlp_bs[0].T, LANES)

    inv = 1.0 / (ROPE_THETA ** (jnp.arange(0, MLA_ROPE, 2, dtype=F32) / MLA_ROPE))
    ang = positions[0].astype(F32)[:, None] * inv
    cos_t = jnp.tile(jnp.cos(ang), (1, 4))
    sin_t = jnp.tile(jnp.concatenate([-jnp.sin(ang), jnp.sin(ang)], axis=1), (1, 2))

    c_all = _all_gather_devices(c.reshape(8, LANES), name="gather_c").reshape(N_DEV, d)
    cols = ada_w.shape[2]
    ada_b_mine = lax.dynamic_slice_in_dim(ada_b, chip * cols, cols, axis=1)[:, None, :]
    mod_sh = _ada_mod(c_all, ada_w, ada_b_mine, name="ada_mod")
    q_norm_rows = jnp.zeros((8, cols), F32).at[0, :q_rank_sh].set(mla_q_norm_g[0])
    mod_all = _all_gather_chips(jnp.concatenate([mod_sh.reshape(2 * N_DEV, cols), q_norm_rows]), name="gather_mod")
    q_norm_g = mod_all[:, 2 * N_DEV, :q_rank_sh].reshape(1, -1)
    mod_all = jnp.transpose(mod_all[:, :2 * N_DEV].reshape(N_CHIPS, 2, N_DEV, cols), (1, 2, 0, 3)).reshape(2, N_DEV, 3 * d)
    mod = lax.dynamic_index_in_dim(mod_all, dev, axis=1, keepdims=False)
    shift = [mod[l:l + 1, :d] for l in range(2)]
    scale = [mod[l:l + 1, d:2 * d] for l in range(2)]
    gate = [mod[l:l + 1, 2 * d:] for l in range(2)]

    h0 = _modulate(x2, scale[0], shift[0], name="modulate0")
    proj0 = _matmul(h0, w_in0, b_stacked=True, tm=1024, name="proj0")
    mix0 = _even_fwd(proj0, ws, bs_t, gmlp_norm_g, gmlp_norm_b, pool_w_bf, pool_b, pool_scale, name="even_fwd")
    y0 = _matmul(mix0, w_out0, name="out0")
    x1, h1 = _resid_ln_modulate(x2, y0, gate[0], ln_g[0:1], ln_b[0:1], scale[1], shift[1], name="resid_ln0")

    proj1 = _matmul(h1, w_in1, name="proj1")
    q_cn, keys = _mla_prep(proj1, q_norm_g, mla_kv_norm_g, cos_t, sin_t, name="mla_prep")
    q_nope = _matmul(q_cn, w_uq_nope, name="q_nope", out_dtype=BF16)
    q_rope_pre = _matmul(q_cn, w_uq_rope, name="q_rope")
    q = _q_build(q_nope, q_rope_pre, wuk_hdr, cos_t, sin_t, name="q_build")
    o_lat, lse = _attn_fwd(q, keys, name="attn_fwd")
    og = _o_build(o_lat, wuv_hrv, proj1, name="o_build")
    y1 = _matmul(og, w_out1, name="out1")

    dy1, dres1, g_ln_g1, g_ln_b1, dgate1, loss = _loss_ln_bwd(x1, y1, gate[1], ln_g[1:2], ln_b[1:2], target, name="loss_ln1")
    dg1 = _matmul(dy1, w_out1, trans_b=True, name="d_og")
    g_w_out1 = _matmul(og, dy1, trans_a=True, out_dtype=BF16, name="g_out1")
    do_lat, dz, g_uv = _o_bwd(dg1, proj1, o_lat, wuv_hrv, wuv_hvr, name="o_bwd")
    dq, dkeys = _attn_bwd(q, keys, do_lat, o_lat, lse, name="attn_bwd")
    dq_nope, dq_rope, g_uk = _q_bwd(dq, q_nope, wuk_hrd, cos_t, sin_t, name="q_bwd")
    dq_cn = _matmul(dq_nope, w_uq_nope, trans_b=True, name="d_qcn_nope") + _matmul(dq_rope, w_uq_rope, trans_b=True, name="d_qcn_rope")
    g_uq_nope = _matmul(q_cn, dq_nope, trans_a=True, out_dtype=BF16, name="g_uq_nope")
    g_uq_rope = _matmul(q_cn, dq_rope, trans_a=True, out_dtype=BF16, name="g_uq_rope")
    dsmall, g_qg, g_kvg = _mla_prep_bwd(proj1, dq_cn, dkeys, q_norm_g, mla_kv_norm_g, cos_t, sin_t, name="mla_prep_bwd")
    dproj1 = jnp.concatenate([dsmall, dz], axis=1)
    dh1 = _matmul(dproj1, w_in1, trans_b=True, name="d_h1")
    g_w_in1 = _matmul(h1, dproj1, trans_a=True, out_dtype=BF16, name="g_in1")

    dy0, dres0, g_ln_g0, g_ln_b0, dgate0, dscale1, dshift1 = _mid_ln_bwd(
        x2, y0, gate[0], ln_g[0:1], ln_b[0:1], dh1, dres1, scale[1], x1, name="mid_ln0")
    dmix0 = _matmul(dy0, w_out0, trans_b=True, name="d_mix0")
    g_w_out0 = _matmul(mix0, dy0, trans_a=True, out_dtype=BF16, name="g_out0")
    dproj0, g_ws, g_bs_t, g_ng, g_nb, g_pw, g_pb, g_ps = _even_bwd(
        proj0, dmix0, ws, ws_t, bs_t, gmlp_norm_g, gmlp_norm_b, pool_w_bf, pool_b, pool_scale, name="even_bwd")
    dh0 = _matmul(dproj0, w_in0, trans_b=True, b_stacked=True, name="d_h0")
    g_w_in0 = _matmul(h0, dproj0, trans_a=True, out_dtype=BF16, out_stacked=True, tm=1024, name="g_in0")
    grad_x, dscale0, dshift0 = _input_bwd(x2, dh0, dres0, scale[0], name="input_bwd")

    small_local = {
        "ln_g": jnp.concatenate([g_ln_g0, g_ln_g1]), "ln_b": jnp.concatenate([g_ln_b0, g_ln_b1]),
        "gmlp_norm_g": g_ng, "gmlp_norm_b": g_nb, "gmlp_bs": g_bs_t[:, :GMLP_HEADS].T, "pool_b": g_pb, "pool_scale": g_ps,
        "mla_kv_norm_g": g_kvg, "mla_q_norm_g": g_qg,
    }
    n_mod = 2 * 3 * d
    vec = jnp.concatenate([dshift0, dscale0, dgate0, dshift1, dscale1, dgate1]
                          + [small_local[n].reshape(1, -1) for n in SMALL], axis=1)
    n_vec = vec.shape[1]
    vec = _pad_cols(vec, -(-n_vec // (8 * LANES)) * 8 * LANES).reshape(-1, LANES)
    vec_all = _all_gather_devices(vec, name="gather_small")
    vec_sum = _sum_devices(vec_all, name="sum_small").reshape(-1)
    dmod_all = vec_all.reshape(N_DEV, -1)[:, :n_mod].reshape(N_DEV, 2, 3 * d)
    dmod_sh = jnp.transpose(lax.dynamic_slice_in_dim(dmod_all, chip * cols, cols, axis=2), (1, 0, 2))
    dmod_sh = jnp.concatenate([dmod_sh, jnp.zeros((2, LANES - N_DEV, cols), F32)], axis=1)
    grads = {"ada_w": _ada_grad(_pad_cols(c_all.T, LANES), dmod_sh, name="ada_grad"), "ada_b": vec_sum[:n_mod].reshape(2, 3 * d)}
    off = n_mod
    for n in SMALL:
        sz = small_local[n].size
        grads[n] = vec_sum[off:off + sz]
        off += sz
    grads["mla_q_norm_g"] = lax.dynamic_slice_in_dim(grads["mla_q_norm_g"], chip * q_rank_sh, q_rank_sh)
    for n in SMALL:
        grads[n] = grads[n].reshape(weights[n].shape)

    g_uq = jnp.concatenate([g_uq_nope.reshape(MLA_Q_RANK, MLA_HEADS, MLA_NOPE), g_uq_rope.reshape(MLA_Q_RANK, MLA_HEADS, MLA_ROPE)], axis=2)
    g_w_in1 = jnp.concatenate([g_w_in1[:, :ODD_SMALL], g_w_in1[:, ODD_SMALL_PAD:]], axis=1)
    g_w_in1 = jnp.transpose(g_w_in1.reshape(d, N_CHIPS, -1), (1, 0, 2))
    g_pw = jnp.transpose(g_pw.astype(BF16).reshape(POOL_GROUPS, N_CHIPS, -1, POOL_GROUP_DIM), (1, 0, 2, 3))
    big = [
        _halves(g_w_in0),
        _halves(g_pw.reshape(N_CHIPS, -1, POOL_GROUP_DIM)),
        _halves(g_w_out0.reshape(N_CHIPS, -1, d)),
        _halves(g_w_in1),
        _halves(g_uq.reshape(N_CHIPS, q_rank_sh, -1)),
        _halves(g_w_out1.reshape(N_CHIPS, -1, d)),
        _halves(g_ws.astype(BF16)),
        _halves(g_uk.astype(BF16).reshape(N_CHIPS, -1, MLA_NOPE)),
        _halves(g_uv.astype(BF16).reshape(N_CHIPS, -1, MLA_V)),
    ]
    totals = _reduce_chips(_reduce_sibling(big, name="reduce_sibling"), name="reduce_chips")
    for n, t in zip(("e_w_in", "pool_w", "e_w_out", "o_w_in", "mla_w_uq", "o_w_out"), totals):
        grads[n] = t.reshape(weights[n].shape)
    rep = jnp.concatenate([t.reshape(-1, LANES) for t in totals[6:]])
    rep = _all_gather_chips(rep, name="gather_rep")
    r_ws, r_uk = GMLP_BLOCK, 4 * MLA_KV_RANK
    grads["gmlp_ws"] = rep[:, :r_ws].reshape(weights["gmlp_ws"].shape)
    grads["mla_w_uk"] = jnp.transpose(rep[:, r_ws:r_ws + r_uk].reshape(MLA_HEADS, MLA_KV_RANK, MLA_NOPE), (1, 0, 2))[None]
    grads["mla_w_uv"] = jnp.transpose(rep[:, r_ws + r_uk:].reshape(MLA_HEADS, MLA_KV_RANK, MLA_V), (1, 0, 2))[None]

    delta, new_m, new_v = {}, {}, {}
    large = [n for n in WEIGHTS if n not in SMALL and n != "ada_b"]
    for n in large:
        delta[n], new_m[n], new_v[n] = _adamw(weights[n], grads[n], mom[n], var[n], name="adamw_" + n)
    small = [n for n in WEIGHTS if n not in large]
    ds, ms, vs = _adamw_small([weights[n] for n in small], [grads[n] for n in small], [mom[n] for n in small],
                              [var[n] for n in small], name="adamw_small")
    for n, dn, mn, vn in zip(small, ds, ms, vs):
        delta[n], new_m[n], new_v[n] = dn, mn, vn

    loss_total = lax.psum(loss[0, 0], ("x", "y", "c"))
    return (loss_total, grad_x[None], *[grads[n] for n in WEIGHTS], *[delta[n] for n in WEIGHTS],
            *[new_m[n] for n in WEIGHTS], *[new_v[n] for n in WEIGHTS])
```

```python
import jax
import jax.numpy as jnp
from jax import lax
from jax.experimental import pallas as pl
from jax.experimental.pallas import tpu as pltpu

F32 = jnp.float32
BF16 = jnp.bfloat16
MESH = pl.DeviceIdType.MESH

D_MODEL = 1024
CHUNK = 64
LN_EPS = 1e-5
GMLP_HEADS = 4
GMLP_HEAD_DIM = 256
GMLP_BLOCK = 128
POOL_WINDOWS = (2, 4, 8, 16)
POOL_GROUPS = 4
POOL_GROUP_DIM = 256
POOL_HALO = 16
EVEN_IN = 5120
MLA_HEADS = 16
MLA_NOPE = 128
MLA_ROPE = 64
MLA_V = 128
MLA_Q_RANK = 256
MLA_KV_RANK = 128
MLA_WIDTH = MLA_HEADS * MLA_V
ODD_IN = 2496
ODD_SMALL = MLA_Q_RANK + MLA_KV_RANK + MLA_ROPE
ODD_SMALL_PAD = 512
QK_PAD = 256
ROPE_THETA = 10000.0
ATTN_SCALE = (MLA_NOPE + MLA_ROPE) ** -0.5
DEEPNORM_ALPHA = (2.0 * 2) ** 0.25
ADAM_LR = 0.001
ADAM_B1 = 0.9
ADAM_B2 = 0.999
ADAM_EPS = 1e-08
ADAM_WD = 0.01
ADAM_STEP = 10
NEG = -1e30
LANES = 128
N_DEV = 8
N_CHIPS = 4
VMEM_LIMIT_BYTES = 56 * 1024 * 1024
HBM = pl.BlockSpec(memory_space=pltpu.HBM)
VMEM = pl.BlockSpec(memory_space=pltpu.VMEM)


def _params(*sem):
    return pltpu.CompilerParams(dimension_semantics=sem if sem else None, vmem_limit_bytes=VMEM_LIMIT_BYTES)


def _tile(dim, pref):
    for t in (pref, 2048, 1280, 1024, 512, 256, 128):
        if t <= min(pref, dim) and dim % t == 0:
            return t
    return dim


def _sigmoid(z):
    return 1.0 / (1.0 + jnp.exp(-z))


def _dot(a, b, dims):
    return lax.dot_general(a, b, (dims, ((), ())), preferred_element_type=F32)


NN = ((1,), (0,))
NT = ((1,), (1,))
TN = ((0,), (0,))


def _matmul(a, b, *, name, trans_a=False, trans_b=False, out_dtype=F32, b_stacked=False, out_stacked=False,
            tm=512, tn=1024, tk=2048):
    k, m = a.shape if trans_a else a.shape[::-1]
    if b_stacked:
        ns, kb, n_sh = b.shape
        kb, n = (ns * n_sh, kb) if trans_b else (kb, ns * n_sh)
    else:
        n, kb = b.shape if trans_b else b.shape[::-1]
    assert k == kb, (a.shape, b.shape)
    tm = _tile(m, tm)
    if b_stacked and trans_b:
        tn, tk = _tile(n, tn), n_sh
    elif b_stacked or out_stacked:
        tn, tk = _tile(n // N_CHIPS, tn), _tile(k, tk)
    else:
        tn, tk = _tile(n, tn), _tile(k, tk)
    nk = k // tk
    per = max((n // N_CHIPS) // tn, 1)
    dims = ((0 if trans_a else 1,), (1 if trans_b else 0,))

    def body_one(a_ref, b_ref, o_ref):
        o_ref[...] = _dot(a_ref[...].astype(BF16), b_ref[...].astype(BF16), dims).astype(out_dtype)

    def body_acc(a_ref, b_ref, o_ref, acc_ref):
        kk = pl.program_id(2)

        @pl.when(kk == 0)
        def _():
            acc_ref[...] = jnp.zeros_like(acc_ref)

        acc_ref[...] += _dot(a_ref[...].astype(BF16), b_ref[...].astype(BF16), dims)

        @pl.when(kk == nk - 1)
        def _():
            o_ref[...] = acc_ref[...].astype(out_dtype)

    a_spec = pl.BlockSpec((tk, tm), lambda i, j, kk: (kk, i)) if trans_a else pl.BlockSpec((tm, tk), lambda i, j, kk: (i, kk))
    if b_stacked and trans_b:
        b_spec = pl.BlockSpec((None, tn, tk), lambda i, j, kk: (kk, j, 0))
    elif b_stacked:
        b_spec = pl.BlockSpec((None, tk, tn), lambda i, j, kk: (j // per, kk, j % per))
    elif trans_b:
        b_spec = pl.BlockSpec((tn, tk), lambda i, j, kk: (j, kk))
    else:
        b_spec = pl.BlockSpec((tk, tn), lambda i, j, kk: (kk, j))
    if out_stacked:
        o_spec = pl.BlockSpec((None, tm, tn), lambda i, j, kk: (j // per, i, j % per))
        o_shape = jax.ShapeDtypeStruct((N_CHIPS, m, n // N_CHIPS), out_dtype)
    else:
        o_spec = pl.BlockSpec((tm, tn), lambda i, j, kk: (i, j))
        o_shape = jax.ShapeDtypeStruct((m, n), out_dtype)
    return pl.pallas_call(
        body_one if nk == 1 else body_acc, name=name, grid=(m // tm, n // tn, nk), in_specs=[a_spec, b_spec],
        out_specs=o_spec, out_shape=o_shape, scratch_shapes=[] if nk == 1 else [pltpu.VMEM((tm, tn), F32)],
        compiler_params=_params("parallel", "parallel", "arbitrary"),
    )(a, b)


def _row_spec(ts, d):
    return pl.BlockSpec((ts, d), lambda i: (i, 0))


def _vec_spec(d):
    return pl.BlockSpec((1, d), lambda i: (0, 0))


def _modulate(x, scale, shift, *, name):
    s, d = x.shape
    ts = _tile(s, 512)

    def body(x_ref, sc_ref, sh_ref, h_ref):
        h_ref[...] = (x_ref[...] * (1.0 + sc_ref[...]) + sh_ref[...]).astype(BF16)

    return pl.pallas_call(
        body, name=name, grid=(s // ts,), in_specs=[_row_spec(ts, d), _vec_spec(d), _vec_spec(d)],
        out_specs=_row_spec(ts, d), out_shape=jax.ShapeDtypeStruct((s, d), BF16), compiler_params=_params("parallel"),
    )(x, scale, shift)


def _ln_stats(pre):
    mu = jnp.mean(pre, axis=-1, keepdims=True)
    xc = pre - mu
    var = jnp.mean(xc * xc, axis=-1, keepdims=True)
    rstd = lax.rsqrt(var + LN_EPS)
    return xc * rstd, rstd


def _ln_bwd_rows(dout, xhat, rstd, g):
    dxh = dout * g
    m1 = jnp.mean(dxh, axis=-1, keepdims=True)
    m2 = jnp.mean(dxh * xhat, axis=-1, keepdims=True)
    return rstd * (dxh - m1 - xhat * m2)


def _colsum(v):
    return jnp.sum(v, axis=0, keepdims=True)


def _resid_ln_modulate(x, y, gate, g, b, scale_next, shift_next, *, name):
    s, d = x.shape
    ts = _tile(s, 512)

    def body(x_ref, y_ref, gate_ref, g_ref, b_ref, sc_ref, sh_ref, xn_ref, h_ref):
        pre = DEEPNORM_ALPHA * x_ref[...] + (1.0 + gate_ref[...]) * y_ref[...]
        xhat, _ = _ln_stats(pre)
        xn = xhat * g_ref[...] + b_ref[...]
        xn_ref[...] = xn
        h_ref[...] = (xn * (1.0 + sc_ref[...]) + sh_ref[...]).astype(BF16)

    return pl.pallas_call(
        body, name=name, grid=(s // ts,),
        in_specs=[_row_spec(ts, d), _row_spec(ts, d)] + [_vec_spec(d)] * 5,
        out_specs=[_row_spec(ts, d), _row_spec(ts, d)],
        out_shape=[jax.ShapeDtypeStruct((s, d), F32), jax.ShapeDtypeStruct((s, d), BF16)],
        compiler_params=_params("parallel"),
    )(x, y, gate, g, b, scale_next, shift_next)


def _loss_ln_bwd(x, y, gate, g, b, target, *, name):
    s, d = x.shape
    ts = _tile(s, 512)

    def body(x_ref, y_ref, gate_ref, g_ref, b_ref, t_ref, dy_ref, dres_ref, dg_ref, db_ref, dgate_ref, loss_ref):
        @pl.when(pl.program_id(0) == 0)
        def _():
            for r in (dg_ref, db_ref, dgate_ref, loss_ref):
                r[...] = jnp.zeros_like(r)

        yv = y_ref[...]
        pre = DEEPNORM_ALPHA * x_ref[...] + (1.0 + gate_ref[...]) * yv
        xhat, rstd = _ln_stats(pre)
        diff = xhat * g_ref[...] + b_ref[...] - t_ref[...]
        loss_ref[...] += (0.5 / d) * jnp.sum(jnp.sum(diff * diff, axis=1, keepdims=True), axis=0, keepdims=True)
        dout = diff * (1.0 / d)
        dpre = _ln_bwd_rows(dout, xhat, rstd, g_ref[...])
        dy_ref[...] = (dpre * (1.0 + gate_ref[...])).astype(BF16)
        dres_ref[...] = DEEPNORM_ALPHA * dpre
        dg_ref[...] += _colsum(dout * xhat)
        db_ref[...] += _colsum(dout)
        dgate_ref[...] += _colsum(dpre * yv)

    vec = jax.ShapeDtypeStruct((1, d), F32)
    return pl.pallas_call(
        body, name=name, grid=(s // ts,),
        in_specs=[_row_spec(ts, d), _row_spec(ts, d), _vec_spec(d), _vec_spec(d), _vec_spec(d), _row_spec(ts, d)],
        out_specs=[_row_spec(ts, d), _row_spec(ts, d), _vec_spec(d), _vec_spec(d), _vec_spec(d), _vec_spec(1)],
        out_shape=[jax.ShapeDtypeStruct((s, d), BF16), jax.ShapeDtypeStruct((s, d), F32), vec, vec, vec,
                   jax.ShapeDtypeStruct((1, 1), F32)],
        compiler_params=_params("arbitrary"),
    )(x, y, gate, g, b, target)


def _mid_ln_bwd(x, y, gate, g, b, dh_next, dres_next, scale_next, x_next, *, name):
    s, d = x.shape
    ts = _tile(s, 512)

    def body(x_ref, y_ref, gate_ref, g_ref, b_ref, dh_ref, dr_ref, sc_ref, xn_ref,
             dy_ref, dres_ref, dg_ref, db_ref, dgate_ref, dscale_ref, dshift_ref):
        @pl.when(pl.program_id(0) == 0)
        def _():
            for r in (dg_ref, db_ref, dgate_ref, dscale_ref, dshift_ref):
                r[...] = jnp.zeros_like(r)

        dh = dh_ref[...]
        dout = dr_ref[...] + dh * (1.0 + sc_ref[...])
        dscale_ref[...] += _colsum(dh * xn_ref[...])
        dshift_ref[...] += _colsum(dh)
        yv = y_ref[...]
        pre = DEEPNORM_ALPHA * x_ref[...] + (1.0 + gate_ref[...]) * yv
        xhat, rstd = _ln_stats(pre)
        dpre = _ln_bwd_rows(dout, xhat, rstd, g_ref[...])
        dy_ref[...] = (dpre * (1.0 + gate_ref[...])).astype(BF16)
        dres_ref[...] = DEEPNORM_ALPHA * dpre
        dg_ref[...] += _colsum(dout * xhat)
        db_ref[...] += _colsum(dout)
        dgate_ref[...] += _colsum(dpre * yv)

    vec = jax.ShapeDtypeStruct((1, d), F32)
    return pl.pallas_call(
        body, name=name, grid=(s // ts,),
        in_specs=[_row_spec(ts, d), _row_spec(ts, d), _vec_spec(d), _vec_spec(d), _vec_spec(d),
                  _row_spec(ts, d), _row_spec(ts, d), _vec_spec(d), _row_spec(ts, d)],
        out_specs=[_row_spec(ts, d), _row_spec(ts, d)] + [_vec_spec(d)] * 5,
        out_shape=[jax.ShapeDtypeStruct((s, d), BF16), jax.ShapeDtypeStruct((s, d), F32)] + [vec] * 5,
        compiler_params=_params("arbitrary"),
    )(x, y, gate, g, b, dh_next, dres_next, scale_next, x_next)


def _input_bwd(x, dh, dres, scale, *, name):
    s, d = x.shape
    ts = _tile(s, 512)

    def body(x_ref, dh_ref, dr_ref, sc_ref, dx_ref, dscale_ref, dshift_ref):
        @pl.when(pl.program_id(0) == 0)
        def _():
            dscale_ref[...] = jnp.zeros_like(dscale_ref)
            dshift_ref[...] = jnp.zeros_like(dshift_ref)

        dh = dh_ref[...]
        dx_ref[...] = dr_ref[...] + dh * (1.0 + sc_ref[...])
        dscale_ref[...] += _colsum(dh * x_ref[...])
        dshift_ref[...] += _colsum(dh)

    vec = jax.ShapeDtypeStruct((1, d), F32)
    return pl.pallas_call(
        body, name=name, grid=(s // ts,),
        in_specs=[_row_spec(ts, d), _row_spec(ts, d), _row_spec(ts, d), _vec_spec(d)],
        out_specs=[_row_spec(ts, d), _vec_spec(d), _vec_spec(d)],
        out_shape=[jax.ShapeDtypeStruct((s, d), F32), vec, vec],
        compiler_params=_params("arbitrary"),
    )(x, dh, dres, scale)


def _chunk_mask(transposed=False):
    r = lax.broadcasted_iota(jnp.int32, (GMLP_BLOCK, GMLP_BLOCK), 0) // CHUNK
    c = lax.broadcasted_iota(jnp.int32, (GMLP_BLOCK, GMLP_BLOCK), 1) // CHUNK
    return (r <= c) if transposed else (c <= r)


def _window_sum(ext, steps, forward):
    rows = ext.shape[0]
    acc = ext
    for k in range(steps):
        shift = 1 << k
        acc = acc + pltpu.roll(acc, (rows - shift) if forward else shift, 0)
    return acc


def _pool_counts(first_row, rows, win):
    t = first_row + lax.broadcasted_iota(jnp.int32, (rows, 1), 0)
    return jnp.minimum(t + 1, win).astype(F32)


def _even_specs(t):
    col = lambda j: pl.BlockSpec((t, D_MODEL), lambda n: (n, j))
    per = t // POOL_HALO
    prev = pl.BlockSpec((POOL_HALO, D_MODEL), lambda n: (jnp.maximum(n * per - 1, 0), 3))
    return col, per, prev


def _full(shape):
    return pl.BlockSpec(shape, lambda n: (0,) * len(shape))


def _gmlp_head(v_h, ng, nb, w_bf):
    xhat, rstd = _ln_stats(v_h)
    vn = (xhat * ng + nb).astype(BF16)
    return xhat, rstd, vn, _dot(w_bf, vn, NN)


def _pool_group(xb_g, prev_g, first_row, grp):
    t = xb_g.shape[0]
    ext = jnp.concatenate([prev_g, xb_g], axis=0)
    tot = _window_sum(ext, grp + 1, False)[POOL_HALO:, :]
    cnt = _pool_counts(first_row, t, POOL_WINDOWS[grp])
    return tot / cnt - xb_g, cnt


def _even_fwd(proj, ws, bs_t, ng, nb, pool_w, pool_b, pool_scale, *, name):
    s = proj.shape[0]
    t = GMLP_BLOCK
    col, per, prev = _even_specs(t)

    def body(u_ref, v_ref, za_ref, xb_ref, zb_ref, xp_ref, ws_ref, bs_ref, ng_ref, nb_ref, pw_ref, pb_ref, ps_ref, o_ref):
        n = pl.program_id(0)
        mask = _chunk_mask()
        for h in range(GMLP_HEADS):
            c0 = h * GMLP_HEAD_DIM
            cs = slice(c0, c0 + GMLP_HEAD_DIM)
            w_bf = jnp.where(mask, ws_ref[h], 0.0).astype(BF16)
            _, _, _, sv = _gmlp_head(v_ref[:, cs], ng_ref[...], nb_ref[...], w_bf)
            sv = sv + bs_ref[:, h:h + 1]
            za = za_ref[:, cs]
            o_ref[:, cs] = (u_ref[:, cs] * sv * (za * _sigmoid(za))).astype(BF16)
        live = (n > 0).astype(F32)
        for grp in range(POOL_GROUPS):
            c0 = grp * POOL_GROUP_DIM
            cs = slice(c0, c0 + POOL_GROUP_DIM)
            pooled, _ = _pool_group(xb_ref[:, cs], xp_ref[:, cs] * live, n * t, grp)
            yb = _dot(pooled.astype(BF16), pw_ref[grp], NN) + pb_ref[:, cs]
            zb = zb_ref[:, cs]
            o_ref[:, D_MODEL + c0:D_MODEL + c0 + POOL_GROUP_DIM] = (yb * ps_ref[:, cs] * (zb * _sigmoid(zb))).astype(BF16)

    return pl.pallas_call(
        body, name=name, grid=(s // t,),
        in_specs=[col(0), col(1), col(2), col(3), col(4), prev,
                  _full((GMLP_HEADS, t, t)), _full((t, LANES)), _full((1, GMLP_HEAD_DIM)), _full((1, GMLP_HEAD_DIM)),
                  _full((POOL_GROUPS, POOL_GROUP_DIM, POOL_GROUP_DIM)), _full((1, D_MODEL)), _full((1, D_MODEL))],
        out_specs=pl.BlockSpec((t, 2 * D_MODEL), lambda n: (n, 0)),
        out_shape=jax.ShapeDtypeStruct((s, 2 * D_MODEL), BF16),
        compiler_params=_params("parallel"),
    )(proj, proj, proj, proj, proj, proj, ws, bs_t, ng, nb, pool_w, pool_b, pool_scale)


def _even_bwd(proj, dmix, ws, ws_t, bs_t, ng, nb, pool_w, pool_b, pool_scale, *, name):
    s = proj.shape[0]
    t = GMLP_BLOCK
    nblk = s // t
    col, per, prev = _even_specs(t)
    nxt = lambda j: pl.BlockSpec((POOL_HALO, D_MODEL), lambda n: (jnp.minimum((n + 1) * per, nblk * per - 1), j))

    def body(u_ref, v_ref, za_ref, xb_ref, zb_ref, xp_ref, zn_ref, da_ref, db_ref, dbn_ref,
             ws_ref, wst_ref, bs_ref, ng_ref, nb_ref, pw_ref, pb_ref, ps_ref,
             dp_ref, gws_ref, gbs_ref, gng_ref, gnb_ref, gpw_ref, gpb_ref, gps_ref):
        n = pl.program_id(0)

        @pl.when(n == 0)
        def _():
            for r in (gws_ref, gbs_ref, gng_ref, gnb_ref, gpw_ref, gpb_ref, gps_ref):
                r[...] = jnp.zeros_like(r)

        mask, mask_t = _chunk_mask(), _chunk_mask(True)
        lane = lax.broadcasted_iota(jnp.int32, (t, LANES), 1)
        ngv, nbv = ng_ref[...], nb_ref[...]
        for h in range(GMLP_HEADS):
            c0 = h * GMLP_HEAD_DIM
            cs = slice(c0, c0 + GMLP_HEAD_DIM)
            w_bf = jnp.where(mask, ws_ref[h], 0.0).astype(BF16)
            wt_bf = jnp.where(mask_t, wst_ref[h], 0.0).astype(BF16)
            xhat, rstd, vn, sv = _gmlp_head(v_ref[:, cs], ngv, nbv, w_bf)
            sv = sv + bs_ref[:, h:h + 1]
            za, u, da = za_ref[:, cs], u_ref[:, cs], da_ref[:, cs]
            sg = _sigmoid(za)
            sl = za * sg
            dp_ref[:, cs] = (da * sv * sl).astype(BF16)
            dp_ref[:, 2 * D_MODEL + c0:2 * D_MODEL + c0 + GMLP_HEAD_DIM] = (
                da * u * sv * (sg * (1.0 + za * (1.0 - sg)))).astype(BF16)
            dsv = da * u * sl
            gbs_ref[...] += jnp.where(lane == h, jnp.sum(dsv, axis=1, keepdims=True), 0.0)
            dsv_bf = dsv.astype(BF16)
            gws_ref[h] += jnp.where(mask, _dot(dsv_bf, vn, NT), 0.0)
            dvn = _dot(wt_bf, dsv_bf, NN)
            dp_ref[:, D_MODEL + c0:D_MODEL + c0 + GMLP_HEAD_DIM] = _ln_bwd_rows(dvn, xhat, rstd, ngv).astype(BF16)
            gng_ref[...] += _colsum(dvn * xhat)
            gnb_ref[...] += _colsum(dvn)
        live_prev = (n > 0).astype(F32)
        live_next = (n < nblk - 1).astype(F32)
        for grp in range(POOL_GROUPS):
            c0 = grp * POOL_GROUP_DIM
            cs = slice(c0, c0 + POOL_GROUP_DIM)
            xb = xb_ref[:, cs]
            pooled, cnt = _pool_group(xb, xp_ref[:, cs] * live_prev, n * t, grp)
            pooled_bf = pooled.astype(BF16)
            pw = pw_ref[grp]
            yb = _dot(pooled_bf, pw, NN) + pb_ref[:, cs]
            ps = ps_ref[:, cs]
            zb, db = zb_ref[:, cs], db_ref[:, cs]
            sg = _sigmoid(zb)
            sl = zb * sg
            dp_ref[:, 4 * D_MODEL + c0:4 * D_MODEL + c0 + POOL_GROUP_DIM] = (
                db * yb * ps * (sg * (1.0 + zb * (1.0 - sg)))).astype(BF16)
            dsl = db * sl
            dy = dsl * ps
            gps_ref[:, cs] += _colsum(dsl * yb)
            gpb_ref[:, cs] += _colsum(dy)
            dy_bf = dy.astype(BF16)
            gpw_ref[grp] += _dot(pooled_bf, dy_bf, TN)
            r = _dot(dy_bf, pw, NT)
            zn = zn_ref[:, cs]
            dyn = (dbn_ref[:, cs] * (zn * _sigmoid(zn)) * ps * live_next).astype(BF16)
            rn = _dot(dyn, pw, NT) / _pool_counts((n + 1) * t, POOL_HALO, POOL_WINDOWS[grp])
            ext = jnp.concatenate([r / cnt, rn], axis=0)
            dxb = _window_sum(ext, grp + 1, True)[:t, :] - r
            dp_ref[:, 3 * D_MODEL + c0:3 * D_MODEL + c0 + POOL_GROUP_DIM] = dxb.astype(BF16)

    out_shape = [
        jax.ShapeDtypeStruct((s, EVEN_IN), BF16),
        jax.ShapeDtypeStruct((GMLP_HEADS, t, t), F32), jax.ShapeDtypeStruct((t, LANES), F32),
        jax.ShapeDtypeStruct((1, GMLP_HEAD_DIM), F32), jax.ShapeDtypeStruct((1, GMLP_HEAD_DIM), F32),
        jax.ShapeDtypeStruct((POOL_GROUPS, POOL_GROUP_DIM, POOL_GROUP_DIM), F32),
        jax.ShapeDtypeStruct((1, D_MODEL), F32), jax.ShapeDtypeStruct((1, D_MODEL), F32),
    ]
    return pl.pallas_call(
        body, name=name, grid=(nblk,),
        in_specs=[col(0), col(1), col(2), col(3), col(4), prev, nxt(4),
                  pl.BlockSpec((t, D_MODEL), lambda n: (n, 0)), pl.BlockSpec((t, D_MODEL), lambda n: (n, 1)), nxt(1),
                  _full((GMLP_HEADS, t, t)), _full((GMLP_HEADS, t, t)), _full((t, LANES)),
                  _full((1, GMLP_HEAD_DIM)), _full((1, GMLP_HEAD_DIM)),
                  _full((POOL_GROUPS, POOL_GROUP_DIM, POOL_GROUP_DIM)), _full((1, D_MODEL)), _full((1, D_MODEL))],
        out_specs=[pl.BlockSpec((t, EVEN_IN), lambda n: (n, 0))] + [_full(o.shape) for o in out_shape[1:]],
        out_shape=out_shape,
        compiler_params=_params("arbitrary"),
    )(proj, proj, proj, proj, proj, proj, proj, dmix, dmix, dmix, ws, ws_t, bs_t, ng, nb, pool_w, pool_b, pool_scale)


def _half_swap(v):
    lane = lax.broadcasted_iota(jnp.int32, v.shape, 1)
    return jnp.where(lane % MLA_ROPE < MLA_ROPE // 2, pltpu.roll(v, LANES - MLA_ROPE // 2, 1), pltpu.roll(v, MLA_ROPE // 2, 1))


def _rope(v, cos, sin_signed):
    return v * cos + _half_swap(v) * sin_signed


def _rope_bwd(d, cos, sin_signed):
    return d * cos + _half_swap(d * sin_signed)


def _rms(v, g):
    r = lax.rsqrt(jnp.mean(v * v, axis=-1, keepdims=True) + LN_EPS)
    return v * r * g, r


def _rms_bwd(dy, v, r, g):
    u = dy * g
    return r * u - v * (r * r * r) * jnp.mean(u * v, axis=-1, keepdims=True)


def _mla_prep(proj, gq, gkv, cos, sin_signed, *, name):
    s = proj.shape[0]
    ts = _tile(s, 512)

    def body(p_ref, gq_ref, gkv_ref, c_ref, s_ref, q_ref, k_ref):
        qcn, _ = _rms(p_ref[:, :MLA_Q_RANK], gq_ref[...])
        kvn, _ = _rms(p_ref[:, MLA_Q_RANK:MLA_Q_RANK + MLA_KV_RANK], gkv_ref[...])
        kr = _rope(p_ref[:, MLA_Q_RANK + MLA_KV_RANK:], c_ref[...], s_ref[...])
        q_ref[...] = qcn.astype(BF16)
        k_ref[...] = jnp.concatenate([kvn, kr], axis=1).astype(BF16)

    return pl.pallas_call(
        body, name=name, grid=(s // ts,),
        in_specs=[_row_spec(ts, ODD_SMALL_PAD), _vec_spec(MLA_Q_RANK), _vec_spec(MLA_KV_RANK), _row_spec(ts, LANES), _row_spec(ts, LANES)],
        out_specs=[_row_spec(ts, MLA_Q_RANK), _row_spec(ts, QK_PAD)],
        out_shape=[jax.ShapeDtypeStruct((s, MLA_Q_RANK), BF16), jax.ShapeDtypeStruct((s, QK_PAD), BF16)],
        compiler_params=_params("parallel"),
    )(proj, gq, gkv, cos, sin_signed)


def _mla_prep_bwd(proj, dqcn, dkv, gq, gkv, cos, sin_signed, *, name):
    s = proj.shape[0]
    ts = _tile(s, 512)

    def body(p_ref, dq_ref, dkv_ref, gq_ref, gkv_ref, c_ref, s_ref, ds_ref, ggq_ref, ggkv_ref):
        @pl.when(pl.program_id(0) == 0)
        def _():
            ggq_ref[...] = jnp.zeros_like(ggq_ref)
            ggkv_ref[...] = jnp.zeros_like(ggkv_ref)

        qc = p_ref[:, :MLA_Q_RANK]
        kvc = p_ref[:, MLA_Q_RANK:MLA_Q_RANK + MLA_KV_RANK]
        _, rq = _rms(qc, gq_ref[...])
        _, rkv = _rms(kvc, gkv_ref[...])
        dq = dq_ref[...]
        dkvn = dkv_ref[:, :MLA_KV_RANK]
        ggq_ref[...] += _colsum(dq * qc * rq)
        ggkv_ref[...] += _colsum(dkvn * kvc * rkv)
        dkr = _rope_bwd(dkv_ref[:, MLA_KV_RANK:], c_ref[...], s_ref[...])
        ds_ref[...] = jnp.concatenate(
            [_rms_bwd(dq, qc, rq, gq_ref[...]), _rms_bwd(dkvn, kvc, rkv, gkv_ref[...]), dkr], axis=1).astype(BF16)

    return pl.pallas_call(
        body, name=name, grid=(s // ts,),
        in_specs=[_row_spec(ts, ODD_SMALL_PAD), _row_spec(ts, MLA_Q_RANK), _row_spec(ts, QK_PAD),
                  _vec_spec(MLA_Q_RANK), _vec_spec(MLA_KV_RANK), _row_spec(ts, LANES), _row_spec(ts, LANES)],
        out_specs=[_row_spec(ts, ODD_SMALL_PAD), _vec_spec(MLA_Q_RANK), _vec_spec(MLA_KV_RANK)],
        out_shape=[jax.ShapeDtypeStruct((s, ODD_SMALL_PAD), BF16), jax.ShapeDtypeStruct((1, MLA_Q_RANK), F32),
                   jax.ShapeDtypeStruct((1, MLA_KV_RANK), F32)],
        compiler_params=_params("arbitrary"),
    )(proj, dqcn, dkv, gq, gkv, cos, sin_signed)


LOG2_E = 1.4426950408889634
Q_PRESCALE = ATTN_SCALE * LOG2_E


def _q_build(q_nope, q_rope_pre, wuk_hdr, cos, sin_signed, *, name):
    s = q_nope.shape[0]
    ts = _tile(s, 1024)

    def body(qn_ref, qr_ref, w_ref, c_ref, s_ref, o_ref):
        r = _rope(qr_ref[...], c_ref[...], s_ref[...])
        lane = lax.broadcasted_iota(jnp.int32, (ts, LANES), 1)
        for j in range(2):
            ql = _dot(qn_ref[:, j * MLA_NOPE:(j + 1) * MLA_NOPE], w_ref[j], NN)
            rr = r if j == 0 else pltpu.roll(r, MLA_ROPE, 1)
            o_ref[j] = (jnp.concatenate([ql, jnp.where(lane < MLA_ROPE, rr, 0.0)], axis=1) * Q_PRESCALE).astype(BF16)

    return pl.pallas_call(
        body, name=name, grid=(s // ts, MLA_HEADS // 2),
        in_specs=[pl.BlockSpec((ts, 2 * MLA_NOPE), lambda i, p: (i, p)), pl.BlockSpec((ts, LANES), lambda i, p: (i, p)),
                  pl.BlockSpec((2, MLA_NOPE, MLA_KV_RANK), lambda i, p: (p, 0, 0)),
                  pl.BlockSpec((ts, LANES), lambda i, p: (i, 0)), pl.BlockSpec((ts, LANES), lambda i, p: (i, 0))],
        out_specs=pl.BlockSpec((2, ts, QK_PAD), lambda i, p: (p, i, 0)),
        out_shape=jax.ShapeDtypeStruct((MLA_HEADS, s, QK_PAD), BF16),
        compiler_params=_params("parallel", "parallel"),
    )(q_nope, q_rope_pre, wuk_hdr, cos, sin_signed)


def _q_bwd(dq, q_nope, wuk_hrd, cos, sin_signed, *, name):
    s = q_nope.shape[0]
    ts = _tile(s, 1024)

    def body(dq_ref, qn_ref, w_ref, c_ref, s_ref, dn_ref, dr_ref, gw_ref):
        @pl.when(pl.program_id(1) == 0)
        def _():
            gw_ref[...] = jnp.zeros_like(gw_ref)

        lane = lax.broadcasted_iota(jnp.int32, (ts, LANES), 1)
        for j in range(2):
            dql = dq_ref[j, :, :MLA_KV_RANK]
            dn_ref[:, j * MLA_NOPE:(j + 1) * MLA_NOPE] = _dot(dql, w_ref[j], NN).astype(BF16)
            gw_ref[j] += _dot(dql, qn_ref[:, j * MLA_NOPE:(j + 1) * MLA_NOPE], TN)
        hi0 = dq_ref[0, :, MLA_KV_RANK:].astype(F32)
        hi1 = dq_ref[1, :, MLA_KV_RANK:].astype(F32)
        d = jnp.where(lane < MLA_ROPE, hi0, pltpu.roll(hi1, MLA_ROPE, 1))
        dr_ref[...] = _rope_bwd(d, c_ref[...], s_ref[...]).astype(BF16)

    return pl.pallas_call(
        body, name=name, grid=(MLA_HEADS // 2, s // ts),
        in_specs=[pl.BlockSpec((2, ts, QK_PAD), lambda p, i: (p, i, 0)), pl.BlockSpec((ts, 2 * MLA_NOPE), lambda p, i: (i, p)),
                  pl.BlockSpec((2, MLA_KV_RANK, MLA_NOPE), lambda p, i: (p, 0, 0)),
                  pl.BlockSpec((ts, LANES), lambda p, i: (i, 0)), pl.BlockSpec((ts, LANES), lambda p, i: (i, 0))],
        out_specs=[pl.BlockSpec((ts, 2 * MLA_NOPE), lambda p, i: (i, p)), pl.BlockSpec((ts, LANES), lambda p, i: (i, p)),
                   pl.BlockSpec((2, MLA_KV_RANK, MLA_NOPE), lambda p, i: (p, 0, 0))],
        out_shape=[jax.ShapeDtypeStruct((s, MLA_HEADS * MLA_NOPE), BF16), jax.ShapeDtypeStruct((s, MLA_HEADS * MLA_ROPE), BF16),
                   jax.ShapeDtypeStruct((MLA_HEADS, MLA_KV_RANK, MLA_NOPE), F32)],
        compiler_params=_params("parallel", "arbitrary"),
    )(dq, q_nope, wuk_hrd, cos, sin_signed)


ATTN_BQ = 128
ATTN_BK = 512


def _diag_mask(rows, bq, bk, q0, k0):
    qc = (q0 + lax.broadcasted_iota(jnp.int32, (rows, bk), 0) % bq) // CHUNK
    kc = (k0 + lax.broadcasted_iota(jnp.int32, (rows, bk), 1)) // CHUNK
    return kc <= qc


def _attn_fwd(q, k, *, name):
    nh, s, dk = q.shape
    bq, bk = _tile(s, ATTN_BQ), _tile(s, ATTN_BK)
    rows = nh * bq

    def body(q_ref, k_ref, o_ref, lse_ref):
        i = pl.program_id(0)
        qb = q_ref[...].reshape(rows, dk)
        n_before = (i * bq) // bk

        def step(j, carry, masked):
            m, l, acc = carry
            k0 = pl.multiple_of(j * bk, bk)
            kb = k_ref[pl.ds(k0, bk), :]
            sc = _dot(qb, kb, NT)
            if masked:
                sc = jnp.where(_diag_mask(rows, bq, bk, i * bq, k0), sc, NEG)
            m_new = jnp.maximum(m, jnp.max(sc, axis=1, keepdims=True))
            p = jnp.exp2(sc - m_new)
            a = jnp.exp2(m - m_new)
            l = a * l + jnp.sum(p, axis=1, keepdims=True)
            acc = a * acc + _dot(p.astype(BF16), kb[:, :MLA_KV_RANK], NN)
            return m_new, l, acc

        init = (jnp.full((rows, 1), NEG, F32), jnp.zeros((rows, 1), F32), jnp.zeros((rows, MLA_KV_RANK), F32))
        carry = lax.fori_loop(0, n_before, lambda j, c: step(j, c, False), init)
        m, l, acc = step(n_before, carry, True)
        o_ref[...] = (acc / l).astype(BF16).reshape(nh, bq, MLA_KV_RANK)
        lse_ref[...] = jnp.broadcast_to(m + jnp.log2(l), (rows, LANES)).reshape(nh, bq, LANES)

    return pl.pallas_call(
        body, name=name, grid=(s // bq,),
        in_specs=[pl.BlockSpec((nh, bq, dk), lambda i: (0, i, 0)), pl.BlockSpec((s, dk), lambda i: (0, 0))],
        out_specs=[pl.BlockSpec((nh, bq, MLA_KV_RANK), lambda i: (0, i, 0)), pl.BlockSpec((nh, bq, LANES), lambda i: (0, i, 0))],
        out_shape=[jax.ShapeDtypeStruct((nh, s, MLA_KV_RANK), BF16), jax.ShapeDtypeStruct((nh, s, LANES), F32)],
        compiler_params=_params("parallel"),
    )(q, k)


def _attn_bwd(q, k, do, o, lse, *, name):
    nh, s, dk = q.shape
    bq, bk = _tile(s, ATTN_BQ), _tile(s, ATTN_BK)
    rows = nh * bq

    def body(q_ref, k_ref, do_ref, o_ref, lse_ref, dq_ref, dkv_ref):
        i = pl.program_id(0)
        n_before = (i * bq) // bk

        @pl.when(i == 0)
        def _():
            dkv_ref[...] = jnp.zeros_like(dkv_ref)

        qb = q_ref[...].reshape(rows, dk)
        dob = do_ref[...].reshape(rows, MLA_KV_RANK)
        lse_b = lse_ref[...].reshape(rows, LANES)[:, :1]
        delta = jnp.sum(dob.astype(F32) * o_ref[...].reshape(rows, MLA_KV_RANK).astype(F32), axis=1, keepdims=True)

        def step(j, dq, masked):
            j0 = pl.multiple_of(j * bk, bk)
            kb = k_ref[pl.ds(j0, bk), :]
            sc = _dot(qb, kb, NT)
            if masked:
                sc = jnp.where(_diag_mask(rows, bq, bk, i * bq, j0), sc, NEG)
            p = jnp.exp2(sc - lse_b)
            dp = _dot(dob, kb[:, :MLA_KV_RANK], NT)
            ds_bf = (p * (dp - delta)).astype(BF16)
            dkv_ref[pl.ds(j0, bk), :] += _dot(ds_bf, qb, TN) * (1.0 / LOG2_E)
            dkv_ref[pl.ds(j0, bk), :MLA_KV_RANK] += _dot(p.astype(BF16), dob, TN)
            return dq + _dot(ds_bf, kb, NN)

        dq = lax.fori_loop(0, n_before, lambda j, c: step(j, c, False), jnp.zeros((rows, dk), F32))
        dq = step(n_before, dq, True) * ATTN_SCALE
        dq_ref[...] = dq.astype(BF16).reshape(nh, bq, dk)

    blk = lambda w: pl.BlockSpec((nh, bq, w), lambda i: (0, i, 0))
    return pl.pallas_call(
        body, name=name, grid=(s // bq,),
        in_specs=[blk(dk), pl.BlockSpec((s, dk), lambda i: (0, 0)), blk(MLA_KV_RANK), blk(MLA_KV_RANK), blk(LANES)],
        out_specs=[blk(dk), pl.BlockSpec((s, dk), lambda i: (0, 0))],
        out_shape=[jax.ShapeDtypeStruct((nh, s, dk), BF16), jax.ShapeDtypeStruct((s, dk), F32)],
        compiler_params=_params("arbitrary"),
    )(q, k, do, o, lse)


HEAD_GROUP = 4


def _o_build(o_lat, wuv_hrv, proj, *, name):
    s = proj.shape[0]
    ts = _tile(s, 1024)
    w = HEAD_GROUP * MLA_V

    def body(ol_ref, w_ref, z_ref, og_ref):
        for j in range(HEAD_GROUP):
            cs = slice(j * MLA_V, (j + 1) * MLA_V)
            z = z_ref[:, cs]
            og_ref[:, cs] = (_dot(ol_ref[j], w_ref[j], NN) * (z * _sigmoid(z))).astype(BF16)

    return pl.pallas_call(
        body, name=name, grid=(s // ts, MLA_HEADS // HEAD_GROUP),
        in_specs=[pl.BlockSpec((HEAD_GROUP, ts, MLA_KV_RANK), lambda i, g: (g, i, 0)),
                  pl.BlockSpec((HEAD_GROUP, MLA_KV_RANK, MLA_V), lambda i, g: (g, 0, 0)),
                  pl.BlockSpec((ts, w), lambda i, g: (i, g + 1))],
        out_specs=pl.BlockSpec((ts, w), lambda i, g: (i, g)),
        out_shape=jax.ShapeDtypeStruct((s, MLA_WIDTH), BF16),
        compiler_params=_params("parallel", "parallel"),
    )(o_lat, wuv_hrv, proj)


def _o_bwd(dg, proj, o_lat, wuv_hrv, wuv_hvr, *, name):
    s = proj.shape[0]
    ts = _tile(s, 1024)
    w = HEAD_GROUP * MLA_V

    def body(dg_ref, z_ref, ol_ref, w_ref, wt_ref, dol_ref, dz_ref, gw_ref):
        @pl.when(pl.program_id(1) == 0)
        def _():
            gw_ref[...] = jnp.zeros_like(gw_ref)

        for j in range(HEAD_GROUP):
            cs = slice(j * MLA_V, (j + 1) * MLA_V)
            z, dgj, ol = z_ref[:, cs], dg_ref[:, cs], ol_ref[j]
            sg = _sigmoid(z)
            o = _dot(ol, w_ref[j], NN)
            dz_ref[:, cs] = (dgj * o * (sg * (1.0 + z * (1.0 - sg)))).astype(BF16)
            do_bf = (dgj * (z * sg)).astype(BF16)
            dol_ref[j] = _dot(do_bf, wt_ref[j], NN).astype(BF16)
            gw_ref[j] += _dot(ol, do_bf, TN)

    hs = lambda a, b: pl.BlockSpec((HEAD_GROUP, a, b), lambda g, i: (g, 0, 0))
    return pl.pallas_call(
        body, name=name, grid=(MLA_HEADS // HEAD_GROUP, s // ts),
        in_specs=[pl.BlockSpec((ts, w), lambda g, i: (i, g)), pl.BlockSpec((ts, w), lambda g, i: (i, g + 1)),
                  pl.BlockSpec((HEAD_GROUP, ts, MLA_KV_RANK), lambda g, i: (g, i, 0)),
                  hs(MLA_KV_RANK, MLA_V), hs(MLA_V, MLA_KV_RANK)],
        out_specs=[pl.BlockSpec((HEAD_GROUP, ts, MLA_KV_RANK), lambda g, i: (g, i, 0)),
                   pl.BlockSpec((ts, w), lambda g, i: (i, g)), hs(MLA_KV_RANK, MLA_V)],
        out_shape=[jax.ShapeDtypeStruct((MLA_HEADS, s, MLA_KV_RANK), BF16), jax.ShapeDtypeStruct((s, MLA_WIDTH), BF16),
                   jax.ShapeDtypeStruct((MLA_HEADS, MLA_KV_RANK, MLA_V), F32)],
        compiler_params=_params("parallel", "arbitrary"),
    )(dg, proj, o_lat, wuv_hrv, wuv_hvr)


def _ada_mod(c_all, ada_w, ada_b_sh, *, name):
    nl, _, cols = ada_w.shape

    def body(c_ref, w_ref, b_ref, o_ref):
        c = c_ref[...]
        cond = (c * _sigmoid(c)).astype(BF16)
        for l in range(nl):
            o_ref[l] = _dot(cond, w_ref[l].astype(BF16), NN) + b_ref[l]

    return pl.pallas_call(
        body, name=name, out_shape=jax.ShapeDtypeStruct((nl, c_all.shape[0], cols), F32),
        compiler_params=_params(),
    )(c_all, ada_w, ada_b_sh)


def _ada_grad(c_all_t, dmod_sh, *, name):
    nl, _, cols = dmod_sh.shape
    d = c_all_t.shape[0]

    def body(c_ref, dm_ref, gw_ref):
        c = c_ref[...]
        cond_t = c * _sigmoid(c)
        for l in range(nl):
            gw_ref[l] = lax.dot_general(cond_t, dm_ref[l], (NN, ((), ())), precision=lax.Precision.HIGHEST,
                                        preferred_element_type=F32)

    return pl.pallas_call(
        body, name=name, out_shape=jax.ShapeDtypeStruct((nl, d, cols), F32), compiler_params=_params(),
    )(c_all_t, dmod_sh)


def _sum_devices(parts, *, name):
    def body(p_ref, o_ref):
        acc = p_ref[0]
        for k in range(1, parts.shape[0]):
            acc = acc + p_ref[k]
        o_ref[...] = acc

    return pl.pallas_call(body, name=name, out_shape=jax.ShapeDtypeStruct(parts.shape[1:], F32), compiler_params=_params())(parts)


def _adamw_math(w, g, m, v):
    c1 = 1.0 - ADAM_B1 ** ADAM_STEP
    c2 = 1.0 - ADAM_B2 ** ADAM_STEP
    nm = ADAM_B1 * m + (1.0 - ADAM_B1) * g
    nv = ADAM_B2 * v + (1.0 - ADAM_B2) * (g * g)
    return -ADAM_LR * ((nm / c1) / (jnp.sqrt(nv / c2) + ADAM_EPS) + ADAM_WD * w), nm, nv


ADAMW_BLOCK_BYTES = 1 << 20


def _adamw(w, g, m, v, *, name):
    shape = w.shape
    a, b = shape[-2], shape[-1]
    lead = 1
    for dim in shape[:-2]:
        lead *= dim
    row_bytes = 4 * b
    if a * row_bytes <= ADAMW_BLOCK_BYTES:
        ta = a
        tl = max(1, min(lead, ADAMW_BLOCK_BYTES // (a * row_bytes)))
        while lead % tl:
            tl -= 1
    else:
        tl = 1
        ta = _tile(a, 256)
    to3 = lambda t: t.reshape(lead, a, b)

    def body(w_ref, g_ref, m_ref, v_ref, d_ref, nm_ref, nv_ref):
        d_ref[...], nm_ref[...], nv_ref[...] = _adamw_math(w_ref[...], g_ref[...], m_ref[...], v_ref[...])

    spec = pl.BlockSpec((tl, ta, b), lambda i, j: (i, j, 0))
    out = jax.ShapeDtypeStruct((lead, a, b), F32)
    res = pl.pallas_call(
        body, name=name, grid=(lead // tl, a // ta), in_specs=[spec] * 4, out_specs=[spec] * 3, out_shape=[out] * 3,
        compiler_params=_params("parallel", "parallel"),
    )(to3(w), to3(g), to3(m), to3(v))
    return [r.reshape(shape) for r in res]


def _adamw_small(ws, gs, ms, vs, *, name):
    n = len(ws)

    def body(*refs):
        for k in range(n):
            w_ref, g_ref, m_ref, v_ref = (refs[j * n + k] for j in range(4))
            d_ref, nm_ref, nv_ref = (refs[(4 + j) * n + k] for j in range(3))
            d_ref[...], nm_ref[...], nv_ref[...] = _adamw_math(w_ref[...], g_ref[...], m_ref[...], v_ref[...])

    outs = [jax.ShapeDtypeStruct(w.shape, F32) for w in ws]
    res = pl.pallas_call(body, name=name, out_shape=outs * 3, compiler_params=_params())(*ws, *gs, *ms, *vs)
    return res[:n], res[n:2 * n], res[2 * n:]


def _flip(v, bit):
    return 1 - v if bit else v


CHIP_DELTAS = ((1, 0), (0, 1), (1, 1))
SUM_ROWS = 32


def _all_gather_chips(shard, *, name):
    def body(x_ref, o_ref, send_sems, recv_sems, local_sem):
        x, y, c = lax.axis_index("x"), lax.axis_index("y"), lax.axis_index("c")
        mine = pltpu.make_async_copy(x_ref, o_ref.at[2 * x + y], local_sem)
        mine.start()

        def copy(k):
            tx, ty = _flip(x, CHIP_DELTAS[k][0]), _flip(y, CHIP_DELTAS[k][1])
            send = pltpu.make_async_remote_copy(src_ref=x_ref, dst_ref=o_ref.at[2 * x + y], send_sem=send_sems.at[k],
                                                recv_sem=recv_sems.at[k], device_id=(tx, ty, c), device_id_type=MESH)
            recv = pltpu.make_async_remote_copy(src_ref=x_ref, dst_ref=o_ref.at[2 * tx + ty], send_sem=send_sems.at[k],
                                                recv_sem=recv_sems.at[k], device_id=(tx, ty, c), device_id_type=MESH)
            return send, recv

        pairs = [copy(k) for k in range(3)]
        for send, _ in pairs:
            send.start()
        for _, recv in pairs:
            recv.wait_recv()
        for send, _ in pairs:
            send.wait_send()
        mine.wait()

    return pl.pallas_call(
        body, name=name, out_shape=jax.ShapeDtypeStruct((N_CHIPS,) + shard.shape, shard.dtype),
        in_specs=[HBM], out_specs=HBM,
        scratch_shapes=[pltpu.SemaphoreType.DMA((3,)), pltpu.SemaphoreType.DMA((3,)), pltpu.SemaphoreType.DMA(())],
    )(shard)


def _gather_weights(shards, *, name):
    n = len(shards)

    def body(*refs):
        w_refs, o_refs = refs[:n], refs[n:2 * n]
        ici_send, ici_recv, d2d_send, d2d_recv, local_sems = refs[2 * n:]
        x, y, c = lax.axis_index("x"), lax.axis_index("y"), lax.axis_index("c")
        me = 2 * x + y
        peers = [(_flip(x, dx), _flip(y, dy)) for dx, dy in CHIP_DELTAS]
        locals_ = [pltpu.make_async_copy(w_refs[k], o_refs[k].at[me], local_sems.at[k]) for k in range(n)]
        for cp in locals_:
            cp.start()

        def rows(k, which):
            half = shards[k].shape[0] // 2
            return pl.ds(pl.multiple_of(which * half, half), half)

        def over_chips(k, d, slot):
            tx, ty = peers[d]
            return pltpu.make_async_remote_copy(
                src_ref=w_refs[k].at[rows(k, c)], dst_ref=o_refs[k].at[slot, rows(k, c)], send_sem=ici_send.at[k, d],
                recv_sem=ici_recv.at[k, d], device_id=(tx, ty, c), device_id_type=MESH)

        def to_sibling(k, d, which):
            tx, ty = peers[d]
            at = o_refs[k].at[2 * tx + ty, rows(k, which)]
            return pltpu.make_async_remote_copy(src_ref=at, dst_ref=at, send_sem=d2d_send.at[k, d], recv_sem=d2d_recv.at[k, d],
                                                device_id=(x, y, 1 - c), device_id_type=MESH)

        sends = [over_chips(k, d, me) for k in range(n) for d in range(3)]
        for cp in sends:
            cp.start()
        passed = []
        for k in range(n):
            for d in range(3):
                over_chips(k, d, 2 * peers[d][0] + peers[d][1]).wait_recv()
                passed.append(to_sibling(k, d, c))
                passed[-1].start()
        for k in range(n):
            for d in range(3):
                to_sibling(k, d, 1 - c).wait_recv()
        for cp in sends + passed:
            cp.wait_send()
        for cp in locals_:
            cp.wait()

    return pl.pallas_call(
        body, name=name, out_shape=[jax.ShapeDtypeStruct((N_CHIPS,) + w.shape, w.dtype) for w in shards],
        in_specs=[HBM] * n, out_specs=[HBM] * n,
        scratch_shapes=[pltpu.SemaphoreType.DMA((n, 3))] * 4 + [pltpu.SemaphoreType.DMA((n,))],
    )(*shards)


def _add_into(dst_ref, src_ref):
    ns, r, _ = dst_ref.shape
    step = SUM_ROWS if r % SUM_ROWS == 0 else r
    for s in range(ns):
        def tile(t, carry):
            at = pl.ds(pl.multiple_of(t * step, step), step)
            dst_ref[s, at, :] = (dst_ref[s, at, :].astype(F32) + src_ref[s, at, :].astype(F32)).astype(dst_ref.dtype)
            return carry
        lax.fori_loop(0, r // step, tile, 0)


def _reduce_sibling(grads, *, name):
    n = len(grads)

    def body(*refs):
        g_refs, o_refs = refs[:n], refs[n:2 * n]
        mine, got = refs[2 * n:3 * n], refs[3 * n:4 * n]
        send_sems, recv_sems, load_sems, store_sems = refs[4 * n:]
        x, y, c = lax.axis_index("x"), lax.axis_index("y"), lax.axis_index("c")
        loads = [pltpu.make_async_copy(g_refs[k].at[:, c], mine[k], load_sems.at[k]) for k in range(n)]
        swaps = [pltpu.make_async_remote_copy(src_ref=g_refs[k].at[:, 1 - c], dst_ref=got[k], send_sem=send_sems.at[k],
                                              recv_sem=recv_sems.at[k], device_id=(x, y, 1 - c), device_id_type=MESH)
                 for k in range(n)]
        for cp in loads + swaps:
            cp.start()
        stores = []
        for k in range(n):
            loads[k].wait()
            swaps[k].wait_recv()
            _add_into(mine[k], got[k])
            stores.append(pltpu.make_async_copy(mine[k], o_refs[k], store_sems.at[k]))
            stores[-1].start()
        for k in range(n):
            swaps[k].wait_send()
            stores[k].wait()

    half = [jax.ShapeDtypeStruct((g.shape[0],) + g.shape[2:], g.dtype) for g in grads]
    return pl.pallas_call(
        body, name=name, out_shape=half, in_specs=[HBM] * n, out_specs=[HBM] * n,
        scratch_shapes=[pltpu.VMEM(h.shape, h.dtype) for h in half] * 2 + [pltpu.SemaphoreType.DMA((n,))] * 4,
        compiler_params=_params(),
    )(*grads)


def _reduce_chips(parts, *, name):
    n = len(parts)

    def body(*refs):
        p_refs, o_refs = refs[:n], refs[n:2 * n]
        got, total = refs[2 * n:3 * n], refs[3 * n:4 * n]
        send_sems, recv_sems, load_sems, share_send, share_recv, store_sems = refs[4 * n:]
        x, y, c = lax.axis_index("x"), lax.axis_index("y"), lax.axis_index("c")
        me = 2 * x + y
        peers = [(_flip(x, dx), _flip(y, dy)) for dx, dy in CHIP_DELTAS]

        def over_chips(k, d, src_slot, dst_slot):
            tx, ty = peers[d]
            return pltpu.make_async_remote_copy(
                src_ref=p_refs[k].at[src_slot], dst_ref=got[k].at[dst_slot], send_sem=send_sems.at[k, d],
                recv_sem=recv_sems.at[k, d], device_id=(tx, ty, c), device_id_type=MESH)

        loads = [pltpu.make_async_copy(p_refs[k].at[me], got[k].at[me], load_sems.at[k]) for k in range(n)]
        sends = [over_chips(k, d, 2 * peers[d][0] + peers[d][1], me) for k in range(n) for d in range(3)]
        for cp in loads + sends:
            cp.start()
        shares, stores = [], []
        for k in range(n):
            loads[k].wait()
            for d in range(3):
                slot = 2 * peers[d][0] + peers[d][1]
                over_chips(k, d, slot, slot).wait_recv()
            r = total[k].shape[0]
            step = SUM_ROWS if r % SUM_ROWS == 0 else r

            def tile(t, carry, k=k, step=step):
                at = pl.ds(pl.multiple_of(t * step, step), step)
                acc = got[k][0, at, :].astype(F32)
                for s in range(1, N_CHIPS):
                    acc = acc + got[k][s, at, :].astype(F32)
                total[k][at, :] = acc
                return carry

            lax.fori_loop(0, r // step, tile, 0)
            stores.append(pltpu.make_async_copy(total[k], o_refs[k].at[c], store_sems.at[k]))
            shares.append(pltpu.make_async_remote_copy(
                src_ref=total[k], dst_ref=o_refs[k].at[c], send_sem=share_send.at[k], recv_sem=share_recv.at[k],
                device_id=(x, y, 1 - c), device_id_type=MESH))
            stores[-1].start()
            shares[-1].start()
        for k in range(n):
            pltpu.make_async_remote_copy(
                src_ref=total[k], dst_ref=o_refs[k].at[1 - c], send_sem=share_send.at[k], recv_sem=share_recv.at[k],
                device_id=(x, y, 1 - c), device_id_type=MESH).wait_recv()
        for cp in sends + shares:
            cp.wait_send()
        for cp in stores:
            cp.wait()

    return pl.pallas_call(
        body, name=name, out_shape=[jax.ShapeDtypeStruct((2,) + p.shape[1:], F32) for p in parts],
        in_specs=[HBM] * n, out_specs=[HBM] * n,
        scratch_shapes=[pltpu.VMEM(p.shape, p.dtype) for p in parts] + [pltpu.VMEM(p.shape[1:], F32) for p in parts]
        + [pltpu.SemaphoreType.DMA((n, 3))] * 2 + [pltpu.SemaphoreType.DMA((n,))] * 4,
        compiler_params=_params(),
    )(*parts)


def _all_gather_devices(rows, *, name):
    deltas = [(dx, dy, dc) for dx in (0, 1) for dy in (0, 1) for dc in (0, 1)][1:]

    def body(x_ref, o_ref, send_sems, recv_sems):
        x, y, c = lax.axis_index("x"), lax.axis_index("y"), lax.axis_index("c")
        me = 4 * x + 2 * y + c
        o_ref[me] = x_ref[...]
        sends, recvs = [], []
        for k, (dx, dy, dc) in enumerate(deltas):
            tx, ty, tc = _flip(x, dx), _flip(y, dy), _flip(c, dc)
            sends.append(pltpu.make_async_remote_copy(src_ref=x_ref, dst_ref=o_ref.at[me], send_sem=send_sems.at[k],
                                                      recv_sem=recv_sems.at[k], device_id=(tx, ty, tc), device_id_type=MESH))
            recvs.append(pltpu.make_async_remote_copy(src_ref=x_ref, dst_ref=o_ref.at[4 * tx + 2 * ty + tc],
                                                      send_sem=send_sems.at[k], recv_sem=recv_sems.at[k],
                                                      device_id=(tx, ty, tc), device_id_type=MESH))
        for cp in sends:
            cp.start()
        for cp in recvs:
            cp.wait_recv()
        for cp in sends:
            cp.wait_send()

    return pl.pallas_call(
        body, name=name, out_shape=jax.ShapeDtypeStruct((N_DEV,) + rows.shape, rows.dtype),
        in_specs=[VMEM], out_specs=VMEM,
        scratch_shapes=[pltpu.SemaphoreType.DMA((N_DEV - 1,)), pltpu.SemaphoreType.DMA((N_DEV - 1,))],
    )(rows)


WEIGHTS = ("ada_w", "ada_b", "ln_g", "ln_b", "e_w_in", "gmlp_norm_g", "gmlp_norm_b", "gmlp_ws", "gmlp_bs", "pool_w",
           "pool_b", "pool_scale", "e_w_out", "o_w_in", "mla_q_norm_g", "mla_kv_norm_g", "mla_w_uq", "mla_w_uk",
           "mla_w_uv", "o_w_out")
SMALL = ("ln_g", "ln_b", "gmlp_norm_g", "gmlp_norm_b", "gmlp_bs", "pool_b", "pool_scale", "mla_kv_norm_g", "mla_q_norm_g")


def _pad_cols(v, n):
    return jnp.concatenate([v, jnp.zeros((v.shape[0], n - v.shape[1]), v.dtype)], axis=1) if n > v.shape[1] else v


def _halves(g):
    return g.reshape(g.shape[0], 2, g.shape[1] // 2, g.shape[2])


def kernel(x, c, positions, ada_w, ada_b, ln_g, ln_b, e_w_in, gmlp_norm_g, gmlp_norm_b, gmlp_ws, gmlp_bs, pool_w, pool_b, pool_scale, e_w_out, o_w_in, mla_q_norm_g, mla_kv_norm_g, mla_w_uq, mla_w_uk, mla_w_uv, o_w_out, loss_target, m_ada_w, m_ada_b, m_ln_g, m_ln_b, m_e_w_in, m_gmlp_norm_g, m_gmlp_norm_b, m_gmlp_ws, m_gmlp_bs, m_pool_w, m_pool_b, m_pool_scale, m_e_w_out, m_o_w_in, m_mla_q_norm_g, m_mla_kv_norm_g, m_mla_w_uq, m_mla_w_uk, m_mla_w_uv, m_o_w_out, v_ada_w, v_ada_b, v_ln_g, v_ln_b, v_e_w_in, v_gmlp_norm_g, v_gmlp_norm_b, v_gmlp_ws, v_gmlp_bs, v_pool_w, v_pool_b, v_pool_scale, v_e_w_out, v_o_w_in, v_mla_q_norm_g, v_mla_kv_norm_g, v_mla_w_uq, v_mla_w_uk, v_mla_w_uv, v_o_w_out):
    args = dict(locals())
    weights = {n: args[n] for n in WEIGHTS}
    mom = {n: args["m_" + n] for n in WEIGHTS}
    var = {n: args["v_" + n] for n in WEIGHTS}
    ax, ay, ac = lax.axis_index("x"), lax.axis_index("y"), lax.axis_index("c")
    chip = 2 * ax + ay
    dev = 2 * chip + ac
    d = D_MODEL
    x2 = x[0]
    target = loss_target[0]
    q_rank_sh = mla_q_norm_g.shape[1]

    shards = [e_w_in[0], pool_w[0].reshape(-1, POOL_GROUP_DIM), e_w_out[0], o_w_in[0],
              mla_w_uq[0].reshape(q_rank_sh, -1), o_w_out[0]]
    w_in0, pool_w_g, w_out0, w_in1_g, w_uq_g, w_out1 = _gather_weights([w.astype(BF16) for w in shards], name="gather_weights")
    pool_w_bf = jnp.transpose(pool_w_g.reshape(N_CHIPS, POOL_GROUPS, -1, POOL_GROUP_DIM), (1, 0, 2, 3)).reshape(
        POOL_GROUPS, POOL_GROUP_DIM, POOL_GROUP_DIM)
    w_out0 = w_out0.reshape(-1, d)
    w_out1 = w_out1.reshape(-1, d)
    w_in1 = jnp.transpose(w_in1_g, (1, 0, 2)).reshape(d, ODD_IN)
    w_in1 = jnp.concatenate([_pad_cols(w_in1[:, :ODD_SMALL], ODD_SMALL_PAD), w_in1[:, ODD_SMALL:]], axis=1)
    w_uq = w_uq_g.reshape(MLA_Q_RANK, MLA_HEADS, MLA_NOPE + MLA_ROPE)
    w_uq_nope = w_uq[:, :, :MLA_NOPE].reshape(MLA_Q_RANK, -1)
    w_uq_rope = w_uq[:, :, MLA_NOPE:].reshape(MLA_Q_RANK, -1)
    wuk_hrd = jnp.transpose(mla_w_uk[0], (1, 0, 2)).astype(BF16)
    wuk_hdr = jnp.transpose(mla_w_uk[0], (1, 2, 0)).astype(BF16)
    wuv_hrv = jnp.transpose(mla_w_uv[0], (1, 0, 2)).astype(BF16)
    wuv_hvr = jnp.transpose(mla_w_uv[0], (1, 2, 0)).astype(BF16)
    ws = gmlp_ws[0]
    ws_t = jnp.transpose(ws, (0, 2, 1))
    bs_t = _pad_cols(gmlp_bs[0].T, LANES)

    inv = 1.0 / (ROPE_THETA ** (jnp.arange(0, MLA_ROPE, 2, dtype=F32) / MLA_ROPE))
    ang = positions[0].astype(F32)[:, None] * inv
    cos_t = jnp.tile(jnp.cos(ang), (1, 4))
    sin_t = jnp.tile(jnp.concatenate([-jnp.sin(ang), jnp.sin(ang)], axis=1), (1, 2))

    c_all = _all_gather_devices(c.reshape(8, LANES), name="gather_c").reshape(N_DEV, d)
    cols = ada_w.shape[2]
    ada_b_mine = lax.dynamic_slice_in_dim(ada_b, chip * cols, cols, axis=1)[:, None, :]
    mod_sh = _ada_mod(c_all, ada_w, ada_b_mine, name="ada_mod")
    q_norm_rows = jnp.zeros((8, cols), F32).at[0, :q_rank_sh].set(mla_q_norm_g[0])
    mod_all = _all_gather_chips(jnp.concatenate([mod_sh.reshape(2 * N_DEV, cols), q_norm_rows]), name="gather_mod")
    q_norm_g = mod_all[:, 2 * N_DEV, :q_rank_sh].reshape(1, -1)
    mod_all = jnp.transpose(mod_all[:, :2 * N_DEV].reshape(N_CHIPS, 2, N_DEV, cols), (1, 2, 0, 3)).reshape(2, N_DEV, 3 * d)
    mod = lax.dynamic_index_in_dim(mod_all, dev, axis=1, keepdims=False)
    shift = [mod[l:l + 1, :d] for l in range(2)]
    scale = [mod[l:l + 1, d:2 * d] for l in range(2)]
    gate = [mod[l:l + 1, 2 * d:] for l in range(2)]

    h0 = _modulate(x2, scale[0], shift[0], name="modulate0")
    proj0 = _matmul(h0, w_in0, b_stacked=True, tm=1024, tn=1280, name="proj0")
    mix0 = _even_fwd(proj0, ws, bs_t, gmlp_norm_g, gmlp_norm_b, pool_w_bf, pool_b, pool_scale, name="even_fwd")
    y0 = _matmul(mix0, w_out0, name="out0")
    x1, h1 = _resid_ln_modulate(x2, y0, gate[0], ln_g[0:1], ln_b[0:1], scale[1], shift[1], name="resid_ln0")

    proj1 = _matmul(h1, w_in1, tm=1024, tn=1280, name="proj1")
    q_cn, keys = _mla_prep(proj1, q_norm_g, mla_kv_norm_g, cos_t, sin_t, name="mla_prep")
    q_nope = _matmul(q_cn, w_uq_nope, tm=1024, tn=2048, name="q_nope", out_dtype=BF16)
    q_rope_pre = _matmul(q_cn, w_uq_rope, tm=1024, name="q_rope")
    q = _q_build(q_nope, q_rope_pre, wuk_hdr, cos_t, sin_t, name="q_build")
    o_lat, lse = _attn_fwd(q, keys, name="attn_fwd")
    og = _o_build(o_lat, wuv_hrv, proj1, name="o_build")
    y1 = _matmul(og, w_out1, name="out1")

    dy1, dres1, g_ln_g1, g_ln_b1, dgate1, loss = _loss_ln_bwd(x1, y1, gate[1], ln_g[1:2], ln_b[1:2], target, name="loss_ln1")
    dg1 = _matmul(dy1, w_out1, trans_b=True, tn=2048, name="d_og")
    g_w_out1 = _matmul(og, dy1, trans_a=True, out_dtype=BF16, tm=1024, name="g_out1")
    do_lat, dz, g_uv = _o_bwd(dg1, proj1, o_lat, wuv_hrv, wuv_hvr, name="o_bwd")
    dq, dkeys = _attn_bwd(q, keys, do_lat, o_lat, lse, name="attn_bwd")
    dq_nope, dq_rope, g_uk = _q_bwd(dq, q_nope, wuk_hrd, cos_t, sin_t, name="q_bwd")
    dq_cn = (_matmul(dq_nope, w_uq_nope, trans_b=True, tm=1024, name="d_qcn_nope")
             + _matmul(dq_rope, w_uq_rope, trans_b=True, tm=1024, name="d_qcn_rope"))
    g_uq_nope = _matmul(q_cn, dq_nope, trans_a=True, out_dtype=BF16, tn=2048, name="g_uq_nope")
    g_uq_rope = _matmul(q_cn, dq_rope, trans_a=True, out_dtype=BF16, name="g_uq_rope")
    dsmall, g_qg, g_kvg = _mla_prep_bwd(proj1, dq_cn, dkeys, q_norm_g, mla_kv_norm_g, cos_t, sin_t, name="mla_prep_bwd")
    dproj1 = jnp.concatenate([dsmall, dz], axis=1)
    dh1 = _matmul(dproj1, w_in1, trans_b=True, name="d_h1")
    g_w_in1 = _matmul(h1, dproj1, trans_a=True, out_dtype=BF16, tm=1024, tn=1280, name="g_in1")

    dy0, dres0, g_ln_g0, g_ln_b0, dgate0, dscale1, dshift1 = _mid_ln_bwd(
        x2, y0, gate[0], ln_g[0:1], ln_b[0:1], dh1, dres1, scale[1], x1, name="mid_ln0")
    dmix0 = _matmul(dy0, w_out0, trans_b=True, tn=2048, name="d_mix0")
    g_w_out0 = _matmul(mix0, dy0, trans_a=True, out_dtype=BF16, tm=1024, name="g_out0")
    dproj0, g_ws, g_bs_t, g_ng, g_nb, g_pw, g_pb, g_ps = _even_bwd(
        proj0, dmix0, ws, ws_t, bs_t, gmlp_norm_g, gmlp_norm_b, pool_w_bf, pool_b, pool_scale, name="even_bwd")
    dh0 = _matmul(dproj0, w_in0, trans_b=True, b_stacked=True, tm=1024, name="d_h0")
    g_w_in0 = _matmul(h0, dproj0, trans_a=True, out_dtype=BF16, out_stacked=True, tm=1024, tn=1280, name="g_in0")
    grad_x, dscale0, dshift0 = _input_bwd(x2, dh0, dres0, scale[0], name="input_bwd")

    small_local = {
        "ln_g": jnp.concatenate([g_ln_g0, g_ln_g1]), "ln_b": jnp.concatenate([g_ln_b0, g_ln_b1]),
        "gmlp_norm_g": g_ng, "gmlp_norm_b": g_nb, "gmlp_bs": g_bs_t[:, :GMLP_HEADS].T, "pool_b": g_pb, "pool_scale": g_ps,
        "mla_kv_norm_g": g_kvg, "mla_q_norm_g": g_qg,
    }
    n_mod = 2 * 3 * d
    vec = jnp.concatenate([dshift0, dscale0, dgate0, dshift1, dscale1, dgate1]
                          + [small_local[n].reshape(1, -1) for n in SMALL], axis=1)
    n_vec = vec.shape[1]
    vec = _pad_cols(vec, -(-n_vec // (8 * LANES)) * 8 * LANES).reshape(-1, LANES)
    vec_all = _all_gather_devices(vec, name="gather_small")
    vec_sum = _sum_devices(vec_all, name="sum_small").reshape(-1)
    dmod_all = vec_all.reshape(N_DEV, -1)[:, :n_mod].reshape(N_DEV, 2, 3 * d)
    dmod_sh = jnp.transpose(lax.dynamic_slice_in_dim(dmod_all, chip * cols, cols, axis=2), (1, 0, 2))
    dmod_sh = jnp.concatenate([dmod_sh, jnp.zeros((2, LANES - N_DEV, cols), F32)], axis=1)
    grads = {"ada_w": _ada_grad(_pad_cols(c_all.T, LANES), dmod_sh, name="ada_grad"), "ada_b": vec_sum[:n_mod].reshape(2, 3 * d)}
    off = n_mod
    for n in SMALL:
        sz = small_local[n].size
        grads[n] = vec_sum[off:off + sz]
        off += sz
    grads["mla_q_norm_g"] = lax.dynamic_slice_in_dim(grads["mla_q_norm_g"], chip * q_rank_sh, q_rank_sh)
    for n in SMALL:
        grads[n] = grads[n].reshape(weights[n].shape)

    g_uq = jnp.concatenate([g_uq_nope.reshape(MLA_Q_RANK, MLA_HEADS, MLA_NOPE), g_uq_rope.reshape(MLA_Q_RANK, MLA_HEADS, MLA_ROPE)], axis=2)
    g_w_in1 = jnp.concatenate([g_w_in1[:, :ODD_SMALL], g_w_in1[:, ODD_SMALL_PAD:]], axis=1)
    g_w_in1 = jnp.transpose(g_w_in1.reshape(d, N_CHIPS, -1), (1, 0, 2))
    g_pw = jnp.transpose(g_pw.astype(BF16).reshape(POOL_GROUPS, N_CHIPS, -1, POOL_GROUP_DIM), (1, 0, 2, 3))
    big = [
        _halves(g_w_in0),
        _halves(g_pw.reshape(N_CHIPS, -1, POOL_GROUP_DIM)),
        _halves(g_w_out0.reshape(N_CHIPS, -1, d)),
        _halves(g_w_in1),
        _halves(g_uq.reshape(N_CHIPS, q_rank_sh, -1)),
        _halves(g_w_out1.reshape(N_CHIPS, -1, d)),
        _halves(g_ws.astype(BF16)),
        _halves(g_uk.astype(BF16).reshape(N_CHIPS, -1, MLA_NOPE)),
        _halves(g_uv.astype(BF16).reshape(N_CHIPS, -1, MLA_V)),
    ]
    totals = _reduce_chips(_reduce_sibling(big, name="reduce_sibling"), name="reduce_chips")
    for n, t in zip(("e_w_in", "pool_w", "e_w_out", "o_w_in", "mla_w_uq", "o_w_out"), totals):
        grads[n] = t.reshape(weights[n].shape)
    rep = jnp.concatenate([t.reshape(-1, LANES) for t in totals[6:]])
    rep = _all_gather_chips(rep, name="gather_rep")
    r_ws, r_uk = GMLP_BLOCK, 4 * MLA_KV_RANK
    grads["gmlp_ws"] = rep[:, :r_ws].reshape(weights["gmlp_ws"].shape)
    grads["mla_w_uk"] = jnp.transpose(rep[:, r_ws:r_ws + r_uk].reshape(MLA_HEADS, MLA_KV_RANK, MLA_NOPE), (1, 0, 2))[None]
    grads["mla_w_uv"] = jnp.transpose(rep[:, r_ws + r_uk:].reshape(MLA_HEADS, MLA_KV_RANK, MLA_V), (1, 0, 2))[None]

    delta, new_m, new_v = {}, {}, {}
    large = [n for n in WEIGHTS if n not in SMALL and n != "ada_b"]
    for n in large:
        delta[n], new_m[n], new_v[n] = _adamw(weights[n], grads[n], mom[n], var[n], name="adamw_" + n)
    small = [n for n in WEIGHTS if n not in large]
    ds, ms, vs = _adamw_small([weights[n] for n in small], [grads[n] for n in small], [mom[n] for n in small],
                              [var[n] for n in small], name="adamw_small")
    for n, dn, mn, vn in zip(small, ds, ms, vs):
        delta[n], new_m[n], new_v[n] = dn, mn, vn

    loss_total = lax.psum(loss[0, 0], ("x", "y", "c"))
    return (loss_total, grad_x[None], *[grads[n] for n in WEIGHTS], *[delta[n] for n in WEIGHTS],
            *[new_m[n] for n in WEIGHTS], *[new_v[n] for n in WEIGHTS])
```

```python
import jax
import jax.numpy as jnp
from jax import lax
from jax.experimental import pallas as pl
from jax.experimental.pallas import tpu as pltpu

F32 = jnp.float32
BF16 = jnp.bfloat16
MESH = pl.DeviceIdType.MESH

D_MODEL = 1024
CHUNK = 64
LN_EPS = 1e-5
GMLP_HEADS = 4
GMLP_HEAD_DIM = 256
GMLP_BLOCK = 128
POOL_WINDOWS = (2, 4, 8, 16)
POOL_GROUPS = 4
POOL_GROUP_DIM = 256
POOL_HALO = 16
EVEN_IN = 5120
MLA_HEADS = 16
MLA_NOPE = 128
MLA_ROPE = 64
MLA_V = 128
MLA_Q_RANK = 256
MLA_KV_RANK = 128
MLA_WIDTH = MLA_HEADS * MLA_V
ODD_IN = 2496
ODD_SMALL = MLA_Q_RANK + MLA_KV_RANK + MLA_ROPE
ODD_SMALL_PAD = 512
QK_PAD = 256
ROPE_THETA = 10000.0
ATTN_SCALE = (MLA_NOPE + MLA_ROPE) ** -0.5
DEEPNORM_ALPHA = (2.0 * 2) ** 0.25
ADAM_LR = 0.001
ADAM_B1 = 0.9
ADAM_B2 = 0.999
ADAM_EPS = 1e-08
ADAM_WD = 0.01
ADAM_STEP = 10
NEG = -1e30
LANES = 128
N_DEV = 8
N_CHIPS = 4
VMEM_LIMIT_BYTES = 56 * 1024 * 1024
HBM = pl.BlockSpec(memory_space=pltpu.HBM)
VMEM = pl.BlockSpec(memory_space=pltpu.VMEM)


def _params(*sem):
    return pltpu.CompilerParams(dimension_semantics=sem if sem else None, vmem_limit_bytes=VMEM_LIMIT_BYTES)


def _tile(dim, pref):
    for t in (pref, 2048, 1280, 1024, 512, 256, 128):
        if t <= min(pref, dim) and dim % t == 0:
            return t
    return dim


def _sigmoid(z):
    return 1.0 / (1.0 + jnp.exp(-z))


def _dot(a, b, dims):
    return lax.dot_general(a, b, (dims, ((), ())), preferred_element_type=F32)


NN = ((1,), (0,))
NT = ((1,), (1,))
TN = ((0,), (0,))


def _matmul(a, b, *, name, trans_a=False, trans_b=False, out_dtype=F32, b_stacked=False, out_stacked=False,
            tm=512, tn=1024, tk=2048):
    k, m = a.shape if trans_a else a.shape[::-1]
    if b_stacked:
        ns, kb, n_sh = b.shape
        kb, n = (ns * n_sh, kb) if trans_b else (kb, ns * n_sh)
    else:
        n, kb = b.shape if trans_b else b.shape[::-1]
    assert k == kb, (a.shape, b.shape)
    tm = _tile(m, tm)
    if b_stacked and trans_b:
        tn, tk = _tile(n, tn), n_sh
    elif b_stacked or out_stacked:
        tn, tk = _tile(n // N_CHIPS, tn), _tile(k, tk)
    else:
        tn, tk = _tile(n, tn), _tile(k, tk)
    nk = k // tk
    per = max((n // N_CHIPS) // tn, 1)
    dims = ((0 if trans_a else 1,), (1 if trans_b else 0,))

    def body_one(a_ref, b_ref, o_ref):
        o_ref[...] = _dot(a_ref[...].astype(BF16), b_ref[...].astype(BF16), dims).astype(out_dtype)

    def body_acc(a_ref, b_ref, o_ref, acc_ref):
        kk = pl.program_id(2)

        @pl.when(kk == 0)
        def _():
            acc_ref[...] = jnp.zeros_like(acc_ref)

        acc_ref[...] += _dot(a_ref[...].astype(BF16), b_ref[...].astype(BF16), dims)

        @pl.when(kk == nk - 1)
        def _():
            o_ref[...] = acc_ref[...].astype(out_dtype)

    a_spec = pl.BlockSpec((tk, tm), lambda i, j, kk: (kk, i)) if trans_a else pl.BlockSpec((tm, tk), lambda i, j, kk: (i, kk))
    if b_stacked and trans_b:
        b_spec = pl.BlockSpec((None, tn, tk), lambda i, j, kk: (kk, j, 0))
    elif b_stacked:
        b_spec = pl.BlockSpec((None, tk, tn), lambda i, j, kk: (j // per, kk, j % per))
    elif trans_b:
        b_spec = pl.BlockSpec((tn, tk), lambda i, j, kk: (j, kk))
    else:
        b_spec = pl.BlockSpec((tk, tn), lambda i, j, kk: (kk, j))
    if out_stacked:
        o_spec = pl.BlockSpec((None, tm, tn), lambda i, j, kk: (j // per, i, j % per))
        o_shape = jax.ShapeDtypeStruct((N_CHIPS, m, n // N_CHIPS), out_dtype)
    else:
        o_spec = pl.BlockSpec((tm, tn), lambda i, j, kk: (i, j))
        o_shape = jax.ShapeDtypeStruct((m, n), out_dtype)
    return pl.pallas_call(
        body_one if nk == 1 else body_acc, name=name, grid=(m // tm, n // tn, nk), in_specs=[a_spec, b_spec],
        out_specs=o_spec, out_shape=o_shape, scratch_shapes=[] if nk == 1 else [pltpu.VMEM((tm, tn), F32)],
        compiler_params=_params("parallel", "parallel", "arbitrary"),
    )(a, b)


def _row_spec(ts, d):
    return pl.BlockSpec((ts, d), lambda i: (i, 0))


def _vec_spec(d):
    return pl.BlockSpec((1, d), lambda i: (0, 0))


def _modulate(x, scale, shift, *, name):
    s, d = x.shape
    ts = _tile(s, 512)

    def body(x_ref, sc_ref, sh_ref, h_ref):
        h_ref[...] = (x_ref[...] * (1.0 + sc_ref[...]) + sh_ref[...]).astype(BF16)

    return pl.pallas_call(
        body, name=name, grid=(s // ts,), in_specs=[_row_spec(ts, d), _vec_spec(d), _vec_spec(d)],
        out_specs=_row_spec(ts, d), out_shape=jax.ShapeDtypeStruct((s, d), BF16), compiler_params=_params("parallel"),
    )(x, scale, shift)


def _ln_stats(pre):
    mu = jnp.mean(pre, axis=-1, keepdims=True)
    xc = pre - mu
    var = jnp.mean(xc * xc, axis=-1, keepdims=True)
    rstd = lax.rsqrt(var + LN_EPS)
    return xc * rstd, rstd


def _ln_bwd_rows(dout, xhat, rstd, g):
    dxh = dout * g
    m1 = jnp.mean(dxh, axis=-1, keepdims=True)
    m2 = jnp.mean(dxh * xhat, axis=-1, keepdims=True)
    return rstd * (dxh - m1 - xhat * m2)


def _colsum(v):
    return jnp.sum(v, axis=0, keepdims=True)


def _resid_ln_modulate(x, y, gate, g, b, scale_next, shift_next, *, name):
    s, d = x.shape
    ts = _tile(s, 512)

    def body(x_ref, y_ref, gate_ref, g_ref, b_ref, sc_ref, sh_ref, xn_ref, h_ref):
        pre = DEEPNORM_ALPHA * x_ref[...] + (1.0 + gate_ref[...]) * y_ref[...]
        xhat, _ = _ln_stats(pre)
        xn = xhat * g_ref[...] + b_ref[...]
        xn_ref[...] = xn
        h_ref[...] = (xn * (1.0 + sc_ref[...]) + sh_ref[...]).astype(BF16)

    return pl.pallas_call(
        body, name=name, grid=(s // ts,),
        in_specs=[_row_spec(ts, d), _row_spec(ts, d)] + [_vec_spec(d)] * 5,
        out_specs=[_row_spec(ts, d), _row_spec(ts, d)],
        out_shape=[jax.ShapeDtypeStruct((s, d), F32), jax.ShapeDtypeStruct((s, d), BF16)],
        compiler_params=_params("parallel"),
    )(x, y, gate, g, b, scale_next, shift_next)


def _loss_ln_bwd(x, y, gate, g, b, target, *, name):
    s, d = x.shape
    ts = _tile(s, 512)

    def body(x_ref, y_ref, gate_ref, g_ref, b_ref, t_ref, dy_ref, dres_ref, dg_ref, db_ref, dgate_ref, loss_ref):
        @pl.when(pl.program_id(0) == 0)
        def _():
            for r in (dg_ref, db_ref, dgate_ref, loss_ref):
                r[...] = jnp.zeros_like(r)

        yv = y_ref[...]
        pre = DEEPNORM_ALPHA * x_ref[...] + (1.0 + gate_ref[...]) * yv
        xhat, rstd = _ln_stats(pre)
        diff = xhat * g_ref[...] + b_ref[...] - t_ref[...]
        loss_ref[...] += (0.5 / d) * jnp.sum(jnp.sum(diff * diff, axis=1, keepdims=True), axis=0, keepdims=True)
        dout = diff * (1.0 / d)
        dpre = _ln_bwd_rows(dout, xhat, rstd, g_ref[...])
        dy_ref[...] = (dpre * (1.0 + gate_ref[...])).astype(BF16)
        dres_ref[...] = DEEPNORM_ALPHA * dpre
        dg_ref[...] += _colsum(dout * xhat)
        db_ref[...] += _colsum(dout)
        dgate_ref[...] += _colsum(dpre * yv)

    vec = jax.ShapeDtypeStruct((1, d), F32)
    return pl.pallas_call(
        body, name=name, grid=(s // ts,),
        in_specs=[_row_spec(ts, d), _row_spec(ts, d), _vec_spec(d), _vec_spec(d), _vec_spec(d), _row_spec(ts, d)],
        out_specs=[_row_spec(ts, d), _row_spec(ts, d), _vec_spec(d), _vec_spec(d), _vec_spec(d), _vec_spec(1)],
        out_shape=[jax.ShapeDtypeStruct((s, d), BF16), jax.ShapeDtypeStruct((s, d), F32), vec, vec, vec,
                   jax.ShapeDtypeStruct((1, 1), F32)],
        compiler_params=_params("arbitrary"),
    )(x, y, gate, g, b, target)


def _mid_ln_bwd(x, y, gate, g, b, dh_next, dres_next, scale_next, x_next, *, name):
    s, d = x.shape
    ts = _tile(s, 512)

    def body(x_ref, y_ref, gate_ref, g_ref, b_ref, dh_ref, dr_ref, sc_ref, xn_ref,
             dy_ref, dres_ref, dg_ref, db_ref, dgate_ref, dscale_ref, dshift_ref):
        @pl.when(pl.program_id(0) == 0)
        def _():
            for r in (dg_ref, db_ref, dgate_ref, dscale_ref, dshift_ref):
                r[...] = jnp.zeros_like(r)

        dh = dh_ref[...]
        dout = dr_ref[...] + dh * (1.0 + sc_ref[...])
        dscale_ref[...] += _colsum(dh * xn_ref[...])
        dshift_ref[...] += _colsum(dh)
        yv = y_ref[...]
        pre = DEEPNORM_ALPHA * x_ref[...] + (1.0 + gate_ref[...]) * yv
        xhat, rstd = _ln_stats(pre)
        dpre = _ln_bwd_rows(dout, xhat, rstd, g_ref[...])
        dy_ref[...] = (dpre * (1.0 + gate_ref[...])).astype(BF16)
        dres_ref[...] = DEEPNORM_ALPHA * dpre
        dg_ref[...] += _colsum(dout * xhat)
        db_ref[...] += _colsum(dout)
        dgate_ref[...] += _colsum(dpre * yv)

    vec = jax.ShapeDtypeStruct((1, d), F32)
    return pl.pallas_call(
        body, name=name, grid=(s // ts,),
        in_specs=[_row_spec(ts, d), _row_spec(ts, d), _vec_spec(d), _vec_spec(d), _vec_spec(d),
                  _row_spec(ts, d), _row_spec(ts, d), _vec_spec(d), _row_spec(ts, d)],
        out_specs=[_row_spec(ts, d), _row_spec(ts, d)] + [_vec_spec(d)] * 5,
        out_shape=[jax.ShapeDtypeStruct((s, d), BF16), jax.ShapeDtypeStruct((s, d), F32)] + [vec] * 5,
        compiler_params=_params("arbitrary"),
    )(x, y, gate, g, b, dh_next, dres_next, scale_next, x_next)


def _input_bwd(x, dh, dres, scale, *, name):
    s, d = x.shape
    ts = _tile(s, 512)

    def body(x_ref, dh_ref, dr_ref, sc_ref, dx_ref, dscale_ref, dshift_ref):
        @pl.when(pl.program_id(0) == 0)
        def _():
            dscale_ref[...] = jnp.zeros_like(dscale_ref)
            dshift_ref[...] = jnp.zeros_like(dshift_ref)

        dh = dh_ref[...]
        dx_ref[...] = dr_ref[...] + dh * (1.0 + sc_ref[...])
        dscale_ref[...] += _colsum(dh * x_ref[...])
        dshift_ref[...] += _colsum(dh)

    vec = jax.ShapeDtypeStruct((1, d), F32)
    return pl.pallas_call(
        body, name=name, grid=(s // ts,),
        in_specs=[_row_spec(ts, d), _row_spec(ts, d), _row_spec(ts, d), _vec_spec(d)],
        out_specs=[_row_spec(ts, d), _vec_spec(d), _vec_spec(d)],
        out_shape=[jax.ShapeDtypeStruct((s, d), F32), vec, vec],
        compiler_params=_params("arbitrary"),
    )(x, dh, dres, scale)


def _chunk_mask(transposed=False):
    r = lax.broadcasted_iota(jnp.int32, (GMLP_BLOCK, GMLP_BLOCK), 0) // CHUNK
    c = lax.broadcasted_iota(jnp.int32, (GMLP_BLOCK, GMLP_BLOCK), 1) // CHUNK
    return (r <= c) if transposed else (c <= r)


def _window_sum(ext, steps, forward):
    rows = ext.shape[0]
    acc = ext
    for k in range(steps):
        shift = 1 << k
        acc = acc + pltpu.roll(acc, (rows - shift) if forward else shift, 0)
    return acc


def _pool_counts(first_row, rows, win):
    t = first_row + lax.broadcasted_iota(jnp.int32, (rows, 1), 0)
    return jnp.minimum(t + 1, win).astype(F32)


def _even_specs(t):
    col = lambda j: pl.BlockSpec((t, D_MODEL), lambda n: (n, j))
    per = t // POOL_HALO
    prev = pl.BlockSpec((POOL_HALO, D_MODEL), lambda n: (jnp.maximum(n * per - 1, 0), 3))
    return col, per, prev


def _full(shape):
    return pl.BlockSpec(shape, lambda n: (0,) * len(shape))


def _gmlp_head(v_h, ng, nb, w_bf):
    xhat, rstd = _ln_stats(v_h)
    vn = (xhat * ng + nb).astype(BF16)
    return xhat, rstd, vn, _dot(w_bf, vn, NN)


def _pool_group(xb_g, prev_g, first_row, grp):
    t = xb_g.shape[0]
    ext = jnp.concatenate([prev_g, xb_g], axis=0)
    tot = _window_sum(ext, grp + 1, False)[POOL_HALO:, :]
    cnt = _pool_counts(first_row, t, POOL_WINDOWS[grp])
    return tot / cnt - xb_g, cnt


def _even_fwd(proj, ws, bs_t, ng, nb, pool_w, pool_b, pool_scale, *, name):
    s = proj.shape[0]
    t = GMLP_BLOCK
    col, per, prev = _even_specs(t)

    def body(u_ref, v_ref, za_ref, xb_ref, zb_ref, xp_ref, ws_ref, bs_ref, ng_ref, nb_ref, pw_ref, pb_ref, ps_ref, o_ref):
        n = pl.program_id(0)
        mask = _chunk_mask()
        for h in range(GMLP_HEADS):
            c0 = h * GMLP_HEAD_DIM
            cs = slice(c0, c0 + GMLP_HEAD_DIM)
            w_bf = jnp.where(mask, ws_ref[h], 0.0).astype(BF16)
            _, _, _, sv = _gmlp_head(v_ref[:, cs], ng_ref[...], nb_ref[...], w_bf)
            sv = sv + bs_ref[:, h:h + 1]
            za = za_ref[:, cs]
            o_ref[:, cs] = (u_ref[:, cs] * sv * (za * _sigmoid(za))).astype(BF16)
        live = (n > 0).astype(F32)
        for grp in range(POOL_GROUPS):
            c0 = grp * POOL_GROUP_DIM
            cs = slice(c0, c0 + POOL_GROUP_DIM)
            pooled, _ = _pool_group(xb_ref[:, cs], xp_ref[:, cs] * live, n * t, grp)
            yb = _dot(pooled.astype(BF16), pw_ref[grp], NN) + pb_ref[:, cs]
            zb = zb_ref[:, cs]
            o_ref[:, D_MODEL + c0:D_MODEL + c0 + POOL_GROUP_DIM] = (yb * ps_ref[:, cs] * (zb * _sigmoid(zb))).astype(BF16)

    return pl.pallas_call(
        body, name=name, grid=(s // t,),
        in_specs=[col(0), col(1), col(2), col(3), col(4), prev,
                  _full((GMLP_HEADS, t, t)), _full((t, LANES)), _full((1, GMLP_HEAD_DIM)), _full((1, GMLP_HEAD_DIM)),
                  _full((POOL_GROUPS, POOL_GROUP_DIM, POOL_GROUP_DIM)), _full((1, D_MODEL)), _full((1, D_MODEL))],
        out_specs=pl.BlockSpec((t, 2 * D_MODEL), lambda n: (n, 0)),
        out_shape=jax.ShapeDtypeStruct((s, 2 * D_MODEL), BF16),
        compiler_params=_params("parallel"),
    )(proj, proj, proj, proj, proj, proj, ws, bs_t, ng, nb, pool_w, pool_b, pool_scale)


def _even_bwd(proj, dmix, ws, ws_t, bs_t, ng, nb, pool_w, pool_b, pool_scale, *, name):
    s = proj.shape[0]
    t = GMLP_BLOCK
    nblk = s // t
    col, per, prev = _even_specs(t)
    nxt = lambda j: pl.BlockSpec((POOL_HALO, D_MODEL), lambda n: (jnp.minimum((n + 1) * per, nblk * per - 1), j))

    def body(u_ref, v_ref, za_ref, xb_ref, zb_ref, xp_ref, zn_ref, da_ref, db_ref, dbn_ref,
             ws_ref, wst_ref, bs_ref, ng_ref, nb_ref, pw_ref, pb_ref, ps_ref,
             dp_ref, gws_ref, gbs_ref, gng_ref, gnb_ref, gpw_ref, gpb_ref, gps_ref):
        n = pl.program_id(0)

        @pl.when(n == 0)
        def _():
            for r in (gws_ref, gbs_ref, gng_ref, gnb_ref, gpw_ref, gpb_ref, gps_ref):
                r[...] = jnp.zeros_like(r)

        mask, mask_t = _chunk_mask(), _chunk_mask(True)
        lane = lax.broadcasted_iota(jnp.int32, (t, LANES), 1)
        ngv, nbv = ng_ref[...], nb_ref[...]
        for h in range(GMLP_HEADS):
            c0 = h * GMLP_HEAD_DIM
            cs = slice(c0, c0 + GMLP_HEAD_DIM)
            w_bf = jnp.where(mask, ws_ref[h], 0.0).astype(BF16)
            wt_bf = jnp.where(mask_t, wst_ref[h], 0.0).astype(BF16)
            xhat, rstd, vn, sv = _gmlp_head(v_ref[:, cs], ngv, nbv, w_bf)
            sv = sv + bs_ref[:, h:h + 1]
            za, u, da = za_ref[:, cs], u_ref[:, cs], da_ref[:, cs]
            sg = _sigmoid(za)
            sl = za * sg
            dp_ref[:, cs] = (da * sv * sl).astype(BF16)
            dp_ref[:, 2 * D_MODEL + c0:2 * D_MODEL + c0 + GMLP_HEAD_DIM] = (
                da * u * sv * (sg * (1.0 + za * (1.0 - sg)))).astype(BF16)
            dsv = da * u * sl
            gbs_ref[...] += jnp.where(lane == h, jnp.sum(dsv, axis=1, keepdims=True), 0.0)
            dsv_bf = dsv.astype(BF16)
            gws_ref[h] += jnp.where(mask, _dot(dsv_bf, vn, NT), 0.0)
            dvn = _dot(wt_bf, dsv_bf, NN)
            dp_ref[:, D_MODEL + c0:D_MODEL + c0 + GMLP_HEAD_DIM] = _ln_bwd_rows(dvn, xhat, rstd, ngv).astype(BF16)
            gng_ref[...] += _colsum(dvn * xhat)
            gnb_ref[...] += _colsum(dvn)
        live_prev = (n > 0).astype(F32)
        live_next = (n < nblk - 1).astype(F32)
        for grp in range(POOL_GROUPS):
            c0 = grp * POOL_GROUP_DIM
            cs = slice(c0, c0 + POOL_GROUP_DIM)
            xb = xb_ref[:, cs]
            pooled, cnt = _pool_group(xb, xp_ref[:, cs] * live_prev, n * t, grp)
            pooled_bf = pooled.astype(BF16)
            pw = pw_ref[grp]
            yb = _dot(pooled_bf, pw, NN) + pb_ref[:, cs]
            ps = ps_ref[:, cs]
            zb, db = zb_ref[:, cs], db_ref[:, cs]
            sg = _sigmoid(zb)
            sl = zb * sg
            dp_ref[:, 4 * D_MODEL + c0:4 * D_MODEL + c0 + POOL_GROUP_DIM] = (
                db * yb * ps * (sg * (1.0 + zb * (1.0 - sg)))).astype(BF16)
            dsl = db * sl
            dy = dsl * ps
            gps_ref[:, cs] += _colsum(dsl * yb)
            gpb_ref[:, cs] += _colsum(dy)
            dy_bf = dy.astype(BF16)
            gpw_ref[grp] += _dot(pooled_bf, dy_bf, TN)
            r = _dot(dy_bf, pw, NT)
            zn = zn_ref[:, cs]
            dyn = (dbn_ref[:, cs] * (zn * _sigmoid(zn)) * ps * live_next).astype(BF16)
            rn = _dot(dyn, pw, NT) / _pool_counts((n + 1) * t, POOL_HALO, POOL_WINDOWS[grp])
            ext = jnp.concatenate([r / cnt, rn], axis=0)
            dxb = _window_sum(ext, grp + 1, True)[:t, :] - r
            dp_ref[:, 3 * D_MODEL + c0:3 * D_MODEL + c0 + POOL_GROUP_DIM] = dxb.astype(BF16)

    out_shape = [
        jax.ShapeDtypeStruct((s, EVEN_IN), BF16),
        jax.ShapeDtypeStruct((GMLP_HEADS, t, t), F32), jax.ShapeDtypeStruct((t, LANES), F32),
        jax.ShapeDtypeStruct((1, GMLP_HEAD_DIM), F32), jax.ShapeDtypeStruct((1, GMLP_HEAD_DIM), F32),
        jax.ShapeDtypeStruct((POOL_GROUPS, POOL_GROUP_DIM, POOL_GROUP_DIM), F32),
        jax.ShapeDtypeStruct((1, D_MODEL), F32), jax.ShapeDtypeStruct((1, D_MODEL), F32),
    ]
    return pl.pallas_call(
        body, name=name, grid=(nblk,),
        in_specs=[col(0), col(1), col(2), col(3), col(4), prev, nxt(4),
                  pl.BlockSpec((t, D_MODEL), lambda n: (n, 0)), pl.BlockSpec((t, D_MODEL), lambda n: (n, 1)), nxt(1),
                  _full((GMLP_HEADS, t, t)), _full((GMLP_HEADS, t, t)), _full((t, LANES)),
                  _full((1, GMLP_HEAD_DIM)), _full((1, GMLP_HEAD_DIM)),
                  _full((POOL_GROUPS, POOL_GROUP_DIM, POOL_GROUP_DIM)), _full((1, D_MODEL)), _full((1, D_MODEL))],
        out_specs=[pl.BlockSpec((t, EVEN_IN), lambda n: (n, 0))] + [_full(o.shape) for o in out_shape[1:]],
        out_shape=out_shape,
        compiler_params=_params("arbitrary"),
    )(proj, proj, proj, proj, proj, proj, proj, dmix, dmix, dmix, ws, ws_t, bs_t, ng, nb, pool_w, pool_b, pool_scale)


def _half_swap(v):
    lane = lax.broadcasted_iota(jnp.int32, v.shape, 1)
    return jnp.where(lane % MLA_ROPE < MLA_ROPE // 2, pltpu.roll(v, LANES - MLA_ROPE // 2, 1), pltpu.roll(v, MLA_ROPE // 2, 1))


def _rope(v, cos, sin_signed):
    return v * cos + _half_swap(v) * sin_signed


def _rope_bwd(d, cos, sin_signed):
    return d * cos + _half_swap(d * sin_signed)


def _rms(v, g):
    r = lax.rsqrt(jnp.mean(v * v, axis=-1, keepdims=True) + LN_EPS)
    return v * r * g, r


def _rms_bwd(dy, v, r, g):
    u = dy * g
    return r * u - v * (r * r * r) * jnp.mean(u * v, axis=-1, keepdims=True)


def _mla_prep(proj, gq, gkv, cos, sin_signed, *, name):
    s = proj.shape[0]
    ts = _tile(s, 512)

    def body(p_ref, gq_ref, gkv_ref, c_ref, s_ref, q_ref, k_ref):
        qcn, _ = _rms(p_ref[:, :MLA_Q_RANK], gq_ref[...])
        kvn, _ = _rms(p_ref[:, MLA_Q_RANK:MLA_Q_RANK + MLA_KV_RANK], gkv_ref[...])
        kr = _rope(p_ref[:, MLA_Q_RANK + MLA_KV_RANK:], c_ref[...], s_ref[...])
        q_ref[...] = qcn.astype(BF16)
        k_ref[...] = jnp.concatenate([kvn, kr], axis=1).astype(BF16)

    return pl.pallas_call(
        body, name=name, grid=(s // ts,),
        in_specs=[_row_spec(ts, ODD_SMALL_PAD), _vec_spec(MLA_Q_RANK), _vec_spec(MLA_KV_RANK), _row_spec(ts, LANES), _row_spec(ts, LANES)],
        out_specs=[_row_spec(ts, MLA_Q_RANK), _row_spec(ts, QK_PAD)],
        out_shape=[jax.ShapeDtypeStruct((s, MLA_Q_RANK), BF16), jax.ShapeDtypeStruct((s, QK_PAD), BF16)],
        compiler_params=_params("parallel"),
    )(proj, gq, gkv, cos, sin_signed)


def _mla_prep_bwd(proj, dqcn, dkv, gq, gkv, cos, sin_signed, *, name):
    s = proj.shape[0]
    ts = _tile(s, 512)

    def body(p_ref, dq_ref, dkv_ref, gq_ref, gkv_ref, c_ref, s_ref, ds_ref, ggq_ref, ggkv_ref):
        @pl.when(pl.program_id(0) == 0)
        def _():
            ggq_ref[...] = jnp.zeros_like(ggq_ref)
            ggkv_ref[...] = jnp.zeros_like(ggkv_ref)

        qc = p_ref[:, :MLA_Q_RANK]
        kvc = p_ref[:, MLA_Q_RANK:MLA_Q_RANK + MLA_KV_RANK]
        _, rq = _rms(qc, gq_ref[...])
        _, rkv = _rms(kvc, gkv_ref[...])
        dq = dq_ref[...]
        dkvn = dkv_ref[:, :MLA_KV_RANK]
        ggq_ref[...] += _colsum(dq * qc * rq)
        ggkv_ref[...] += _colsum(dkvn * kvc * rkv)
        dkr = _rope_bwd(dkv_ref[:, MLA_KV_RANK:], c_ref[...], s_ref[...])
        ds_ref[...] = jnp.concatenate(
            [_rms_bwd(dq, qc, rq, gq_ref[...]), _rms_bwd(dkvn, kvc, rkv, gkv_ref[...]), dkr], axis=1).astype(BF16)

    return pl.pallas_call(
        body, name=name, grid=(s // ts,),
        in_specs=[_row_spec(ts, ODD_SMALL_PAD), _row_spec(ts, MLA_Q_RANK), _row_spec(ts, QK_PAD),
                  _vec_spec(MLA_Q_RANK), _vec_spec(MLA_KV_RANK), _row_spec(ts, LANES), _row_spec(ts, LANES)],
        out_specs=[_row_spec(ts, ODD_SMALL_PAD), _vec_spec(MLA_Q_RANK), _vec_spec(MLA_KV_RANK)],
        out_shape=[jax.ShapeDtypeStruct((s, ODD_SMALL_PAD), BF16), jax.ShapeDtypeStruct((1, MLA_Q_RANK), F32),
                   jax.ShapeDtypeStruct((1, MLA_KV_RANK), F32)],
        compiler_params=_params("arbitrary"),
    )(proj, dqcn, dkv, gq, gkv, cos, sin_signed)


LOG2_E = 1.4426950408889634
Q_PRESCALE = ATTN_SCALE * LOG2_E


def _q_build(q_nope, q_rope_pre, wuk_hdr, cos, sin_signed, *, name):
    s = q_nope.shape[0]
    ts = _tile(s, 1024)

    def body(qn_ref, qr_ref, w_ref, c_ref, s_ref, o_ref):
        r = _rope(qr_ref[...], c_ref[...], s_ref[...])
        lane = lax.broadcasted_iota(jnp.int32, (ts, LANES), 1)
        for j in range(2):
            ql = _dot(qn_ref[:, j * MLA_NOPE:(j + 1) * MLA_NOPE], w_ref[j], NN)
            rr = r if j == 0 else pltpu.roll(r, MLA_ROPE, 1)
            o_ref[j] = (jnp.concatenate([ql, jnp.where(lane < MLA_ROPE, rr, 0.0)], axis=1) * Q_PRESCALE).astype(BF16)

    return pl.pallas_call(
        body, name=name, grid=(s // ts, MLA_HEADS // 2),
        in_specs=[pl.BlockSpec((ts, 2 * MLA_NOPE), lambda i, p: (i, p)), pl.BlockSpec((ts, LANES), lambda i, p: (i, p)),
                  pl.BlockSpec((2, MLA_NOPE, MLA_KV_RANK), lambda i, p: (p, 0, 0)),
                  pl.BlockSpec((ts, LANES), lambda i, p: (i, 0)), pl.BlockSpec((ts, LANES), lambda i, p: (i, 0))],
        out_specs=pl.BlockSpec((2, ts, QK_PAD), lambda i, p: (p, i, 0)),
        out_shape=jax.ShapeDtypeStruct((MLA_HEADS, s, QK_PAD), BF16),
        compiler_params=_params("parallel", "parallel"),
    )(q_nope, q_rope_pre, wuk_hdr, cos, sin_signed)


def _q_bwd(dq, q_nope, wuk_hrd, cos, sin_signed, *, name):
    s = q_nope.shape[0]
    ts = _tile(s, 1024)

    def body(dq_ref, qn_ref, w_ref, c_ref, s_ref, dn_ref, dr_ref, gw_ref):
        @pl.when(pl.program_id(1) == 0)
        def _():
            gw_ref[...] = jnp.zeros_like(gw_ref)

        lane = lax.broadcasted_iota(jnp.int32, (ts, LANES), 1)
        for j in range(2):
            dql = dq_ref[j, :, :MLA_KV_RANK]
            dn_ref[:, j * MLA_NOPE:(j + 1) * MLA_NOPE] = _dot(dql, w_ref[j], NN).astype(BF16)
            gw_ref[j] += _dot(dql, qn_ref[:, j * MLA_NOPE:(j + 1) * MLA_NOPE], TN)
        hi0 = dq_ref[0, :, MLA_KV_RANK:].astype(F32)
        hi1 = dq_ref[1, :, MLA_KV_RANK:].astype(F32)
        d = jnp.where(lane < MLA_ROPE, hi0, pltpu.roll(hi1, MLA_ROPE, 1))
        dr_ref[...] = _rope_bwd(d, c_ref[...], s_ref[...]).astype(BF16)

    return pl.pallas_call(
        body, name=name, grid=(MLA_HEADS // 2, s // ts),
        in_specs=[pl.BlockSpec((2, ts, QK_PAD), lambda p, i: (p, i, 0)), pl.BlockSpec((ts, 2 * MLA_NOPE), lambda p, i: (i, p)),
                  pl.BlockSpec((2, MLA_KV_RANK, MLA_NOPE), lambda p, i: (p, 0, 0)),
                  pl.BlockSpec((ts, LANES), lambda p, i: (i, 0)), pl.BlockSpec((ts, LANES), lambda p, i: (i, 0))],
        out_specs=[pl.BlockSpec((ts, 2 * MLA_NOPE), lambda p, i: (i, p)), pl.BlockSpec((ts, LANES), lambda p, i: (i, p)),
                   pl.BlockSpec((2, MLA_KV_RANK, MLA_NOPE), lambda p, i: (p, 0, 0))],
        out_shape=[jax.ShapeDtypeStruct((s, MLA_HEADS * MLA_NOPE), BF16), jax.ShapeDtypeStruct((s, MLA_HEADS * MLA_ROPE), BF16),
                   jax.ShapeDtypeStruct((MLA_HEADS, MLA_KV_RANK, MLA_NOPE), F32)],
        compiler_params=_params("parallel", "arbitrary"),
    )(dq, q_nope, wuk_hrd, cos, sin_signed)


ATTN_BQ = 128
ATTN_BK = 512


def _diag_mask(rows, bq, bk, q0, k0):
    qc = (q0 + lax.broadcasted_iota(jnp.int32, (rows, bk), 0) % bq) // CHUNK
    kc = (k0 + lax.broadcasted_iota(jnp.int32, (rows, bk), 1)) // CHUNK
    return kc <= qc


def _attn_fwd(q, k, *, name):
    nh, s, dk = q.shape
    bq, bk = _tile(s, ATTN_BQ), _tile(s, ATTN_BK)
    rows = nh * bq

    def body(q_ref, k_ref, o_ref, lse_ref):
        i = pl.program_id(0)
        qb = q_ref[...].reshape(rows, dk)
        n_before = (i * bq) // bk

        def step(j, carry, masked):
            m, l, acc = carry
            k0 = pl.multiple_of(j * bk, bk)
            kb = k_ref[pl.ds(k0, bk), :]
            sc = _dot(qb, kb, NT)
            if masked:
                sc = jnp.where(_diag_mask(rows, bq, bk, i * bq, k0), sc, NEG)
            m_new = jnp.maximum(m, jnp.max(sc, axis=1, keepdims=True))
            p = jnp.exp2(sc - m_new)
            a = jnp.exp2(m - m_new)
            l = a * l + jnp.sum(p, axis=1, keepdims=True)
            acc = a * acc + _dot(p.astype(BF16), kb[:, :MLA_KV_RANK], NN)
            return m_new, l, acc

        init = (jnp.full((rows, 1), NEG, F32), jnp.zeros((rows, 1), F32), jnp.zeros((rows, MLA_KV_RANK), F32))
        carry = lax.fori_loop(0, n_before, lambda j, c: step(j, c, False), init)
        m, l, acc = step(n_before, carry, True)
        o_ref[...] = (acc / l).astype(BF16).reshape(nh, bq, MLA_KV_RANK)
        lse_ref[...] = jnp.broadcast_to(m + jnp.log2(l), (rows, LANES)).reshape(nh, bq, LANES)

    return pl.pallas_call(
        body, name=name, grid=(s // bq,),
        in_specs=[pl.BlockSpec((nh, bq, dk), lambda i: (0, i, 0)), pl.BlockSpec((s, dk), lambda i: (0, 0))],
        out_specs=[pl.BlockSpec((nh, bq, MLA_KV_RANK), lambda i: (0, i, 0)), pl.BlockSpec((nh, bq, LANES), lambda i: (0, i, 0))],
        out_shape=[jax.ShapeDtypeStruct((nh, s, MLA_KV_RANK), BF16), jax.ShapeDtypeStruct((nh, s, LANES), F32)],
        compiler_params=_params("parallel"),
    )(q, k)


def _attn_bwd(q, k, do, o, lse, *, name):
    nh, s, dk = q.shape
    bq, bk = _tile(s, ATTN_BQ), _tile(s, ATTN_BK)
    rows = nh * bq

    def body(q_ref, k_ref, do_ref, o_ref, lse_ref, dq_ref, dkv_ref):
        i = pl.program_id(0)
        n_before = (i * bq) // bk

        @pl.when(i == 0)
        def _():
            dkv_ref[...] = jnp.zeros_like(dkv_ref)

        qb = q_ref[...].reshape(rows, dk)
        dob = do_ref[...].reshape(rows, MLA_KV_RANK)
        lse_b = lse_ref[...].reshape(rows, LANES)[:, :1]
        delta = jnp.sum(dob.astype(F32) * o_ref[...].reshape(rows, MLA_KV_RANK).astype(F32), axis=1, keepdims=True)

        def step(j, dq, masked):
            j0 = pl.multiple_of(j * bk, bk)
            kb = k_ref[pl.ds(j0, bk), :]
            sc = _dot(qb, kb, NT)
            if masked:
                sc = jnp.where(_diag_mask(rows, bq, bk, i * bq, j0), sc, NEG)
            p = jnp.exp2(sc - lse_b)
            dp = _dot(dob, kb[:, :MLA_KV_RANK], NT)
            ds_bf = (p * (dp - delta)).astype(BF16)
            dkv_ref[pl.ds(j0, bk), :] += _dot(ds_bf, qb, TN) * (1.0 / LOG2_E)
            dkv_ref[pl.ds(j0, bk), :MLA_KV_RANK] += _dot(p.astype(BF16), dob, TN)
            return dq + _dot(ds_bf, kb, NN)

        dq = lax.fori_loop(0, n_before, lambda j, c: step(j, c, False), jnp.zeros((rows, dk), F32))
        dq = step(n_before, dq, True) * ATTN_SCALE
        dq_ref[...] = dq.astype(BF16).reshape(nh, bq, dk)

    blk = lambda w: pl.BlockSpec((nh, bq, w), lambda i: (0, i, 0))
    return pl.pallas_call(
        body, name=name, grid=(s // bq,),
        in_specs=[blk(dk), pl.BlockSpec((s, dk), lambda i: (0, 0)), blk(MLA_KV_RANK), blk(MLA_KV_RANK), blk(LANES)],
        out_specs=[blk(dk), pl.BlockSpec((s, dk), lambda i: (0, 0))],
        out_shape=[jax.ShapeDtypeStruct((nh, s, dk), BF16), jax.ShapeDtypeStruct((s, dk), F32)],
        compiler_params=_params("arbitrary"),
    )(q, k, do, o, lse)


HEAD_GROUP = 4


def _o_build(o_lat, wuv_hrv, proj, *, name):
    s = proj.shape[0]
    ts = _tile(s, 1024)
    w = HEAD_GROUP * MLA_V

    def body(ol_ref, w_ref, z_ref, og_ref):
        for j in range(HEAD_GROUP):
            cs = slice(j * MLA_V, (j + 1) * MLA_V)
            z = z_ref[:, cs]
            og_ref[:, cs] = (_dot(ol_ref[j], w_ref[j], NN) * (z * _sigmoid(z))).astype(BF16)

    return pl.pallas_call(
        body, name=name, grid=(s // ts, MLA_HEADS // HEAD_GROUP),
        in_specs=[pl.BlockSpec((HEAD_GROUP, ts, MLA_KV_RANK), lambda i, g: (g, i, 0)),
                  pl.BlockSpec((HEAD_GROUP, MLA_KV_RANK, MLA_V), lambda i, g: (g, 0, 0)),
                  pl.BlockSpec((ts, w), lambda i, g: (i, g + 1))],
        out_specs=pl.BlockSpec((ts, w), lambda i, g: (i, g)),
        out_shape=jax.ShapeDtypeStruct((s, MLA_WIDTH), BF16),
        compiler_params=_params("parallel", "parallel"),
    )(o_lat, wuv_hrv, proj)


def _o_bwd(dg, proj, o_lat, wuv_hrv, wuv_hvr, *, name):
    s = proj.shape[0]
    ts = _tile(s, 1024)
    w = HEAD_GROUP * MLA_V

    def body(dg_ref, z_ref, ol_ref, w_ref, wt_ref, dol_ref, dz_ref, gw_ref):
        @pl.when(pl.program_id(1) == 0)
        def _():
            gw_ref[...] = jnp.zeros_like(gw_ref)

        for j in range(HEAD_GROUP):
            cs = slice(j * MLA_V, (j + 1) * MLA_V)
            z, dgj, ol = z_ref[:, cs], dg_ref[:, cs], ol_ref[j]
            sg = _sigmoid(z)
            o = _dot(ol, w_ref[j], NN)
            dz_ref[:, cs] = (dgj * o * (sg * (1.0 + z * (1.0 - sg)))).astype(BF16)
            do_bf = (dgj * (z * sg)).astype(BF16)
            dol_ref[j] = _dot(do_bf, wt_ref[j], NN).astype(BF16)
            gw_ref[j] += _dot(ol, do_bf, TN)

    hs = lambda a, b: pl.BlockSpec((HEAD_GROUP, a, b), lambda g, i: (g, 0, 0))
    return pl.pallas_call(
        body, name=name, grid=(MLA_HEADS // HEAD_GROUP, s // ts),
        in_specs=[pl.BlockSpec((ts, w), lambda g, i: (i, g)), pl.BlockSpec((ts, w), lambda g, i: (i, g + 1)),
                  pl.BlockSpec((HEAD_GROUP, ts, MLA_KV_RANK), lambda g, i: (g, i, 0)),
                  hs(MLA_KV_RANK, MLA_V), hs(MLA_V, MLA_KV_RANK)],
        out_specs=[pl.BlockSpec((HEAD_GROUP, ts, MLA_KV_RANK), lambda g, i: (g, i, 0)),
                   pl.BlockSpec((ts, w), lambda g, i: (i, g)), hs(MLA_KV_RANK, MLA_V)],
        out_shape=[jax.ShapeDtypeStruct((MLA_HEADS, s, MLA_KV_RANK), BF16), jax.ShapeDtypeStruct((s, MLA_WIDTH), BF16),
                   jax.ShapeDtypeStruct((MLA_HEADS, MLA_KV_RANK, MLA_V), F32)],
        compiler_params=_params("parallel", "arbitrary"),
    )(dg, proj, o_lat, wuv_hrv, wuv_hvr)


def _ada_mod(c_all, ada_w, ada_b_sh, *, name):
    nl, _, cols = ada_w.shape

    def body(c_ref, w_ref, b_ref, o_ref):
        c = c_ref[...]
        cond = (c * _sigmoid(c)).astype(BF16)
        for l in range(nl):
            o_ref[l] = _dot(cond, w_ref[l].astype(BF16), NN) + b_ref[l]

    return pl.pallas_call(
        body, name=name, out_shape=jax.ShapeDtypeStruct((nl, c_all.shape[0], cols), F32),
        compiler_params=_params(),
    )(c_all, ada_w, ada_b_sh)


def _ada_grad(c_all_t, dmod_sh, *, name):
    nl, _, cols = dmod_sh.shape
    d = c_all_t.shape[0]

    def body(c_ref, dm_ref, gw_ref):
        c = c_ref[...]
        cond_t = c * _sigmoid(c)
        for l in range(nl):
            gw_ref[l] = lax.dot_general(cond_t, dm_ref[l], (NN, ((), ())), precision=lax.Precision.HIGHEST,
                                        preferred_element_type=F32)

    return pl.pallas_call(
        body, name=name, out_shape=jax.ShapeDtypeStruct((nl, d, cols), F32), compiler_params=_params(),
    )(c_all_t, dmod_sh)


def _sum_devices(parts, *, name):
    def body(p_ref, o_ref):
        acc = p_ref[0]
        for k in range(1, parts.shape[0]):
            acc = acc + p_ref[k]
        o_ref[...] = acc

    return pl.pallas_call(body, name=name, out_shape=jax.ShapeDtypeStruct(parts.shape[1:], F32), compiler_params=_params())(parts)


def _adamw_math(w, g, m, v):
    c1 = 1.0 - ADAM_B1 ** ADAM_STEP
    c2 = 1.0 - ADAM_B2 ** ADAM_STEP
    nm = ADAM_B1 * m + (1.0 - ADAM_B1) * g
    nv = ADAM_B2 * v + (1.0 - ADAM_B2) * (g * g)
    return -ADAM_LR * ((nm / c1) / (jnp.sqrt(nv / c2) + ADAM_EPS) + ADAM_WD * w), nm, nv


ADAMW_BLOCK_BYTES = 1 << 20


def _adamw(w, g, m, v, *, name):
    shape = w.shape
    a, b = shape[-2], shape[-1]
    lead = 1
    for dim in shape[:-2]:
        lead *= dim
    row_bytes = 4 * b
    if a * row_bytes <= ADAMW_BLOCK_BYTES:
        ta = a
        tl = max(1, min(lead, ADAMW_BLOCK_BYTES // (a * row_bytes)))
        while lead % tl:
            tl -= 1
    else:
        tl = 1
        ta = _tile(a, 256)
    to3 = lambda t: t.reshape(lead, a, b)

    def body(w_ref, g_ref, m_ref, v_ref, d_ref, nm_ref, nv_ref):
        d_ref[...], nm_ref[...], nv_ref[...] = _adamw_math(w_ref[...], g_ref[...], m_ref[...], v_ref[...])

    spec = pl.BlockSpec((tl, ta, b), lambda i, j: (i, j, 0))
    out = jax.ShapeDtypeStruct((lead, a, b), F32)
    res = pl.pallas_call(
        body, name=name, grid=(lead // tl, a // ta), in_specs=[spec] * 4, out_specs=[spec] * 3, out_shape=[out] * 3,
        compiler_params=_params("parallel", "parallel"),
    )(to3(w), to3(g), to3(m), to3(v))
    return [r.reshape(shape) for r in res]


def _adamw_small(ws, gs, ms, vs, *, name):
    n = len(ws)

    def body(*refs):
        for k in range(n):
            w_ref, g_ref, m_ref, v_ref = (refs[j * n + k] for j in range(4))
            d_ref, nm_ref, nv_ref = (refs[(4 + j) * n + k] for j in range(3))
            d_ref[...], nm_ref[...], nv_ref[...] = _adamw_math(w_ref[...], g_ref[...], m_ref[...], v_ref[...])

    outs = [jax.ShapeDtypeStruct(w.shape, F32) for w in ws]
    res = pl.pallas_call(body, name=name, out_shape=outs * 3, compiler_params=_params())(*ws, *gs, *ms, *vs)
    return res[:n], res[n:2 * n], res[2 * n:]


def _flip(v, bit):
    return 1 - v if bit else v


CHIP_DELTAS = ((1, 0), (0, 1), (1, 1))
SUM_ROWS = 32


def _all_gather_chips(shard, *, name):
    def body(x_ref, o_ref, send_sems, recv_sems, local_sem):
        x, y, c = lax.axis_index("x"), lax.axis_index("y"), lax.axis_index("c")
        mine = pltpu.make_async_copy(x_ref, o_ref.at[2 * x + y], local_sem)
        mine.start()

        def copy(k):
            tx, ty = _flip(x, CHIP_DELTAS[k][0]), _flip(y, CHIP_DELTAS[k][1])
            send = pltpu.make_async_remote_copy(src_ref=x_ref, dst_ref=o_ref.at[2 * x + y], send_sem=send_sems.at[k],
                                                recv_sem=recv_sems.at[k], device_id=(tx, ty, c), device_id_type=MESH)
            recv = pltpu.make_async_remote_copy(src_ref=x_ref, dst_ref=o_ref.at[2 * tx + ty], send_sem=send_sems.at[k],
                                                recv_sem=recv_sems.at[k], device_id=(tx, ty, c), device_id_type=MESH)
            return send, recv

        pairs = [copy(k) for k in range(3)]
        for send, _ in pairs:
            send.start()
        for _, recv in pairs:
            recv.wait_recv()
        for send, _ in pairs:
            send.wait_send()
        mine.wait()

    return pl.pallas_call(
        body, name=name, out_shape=jax.ShapeDtypeStruct((N_CHIPS,) + shard.shape, shard.dtype),
        in_specs=[HBM], out_specs=HBM,
        scratch_shapes=[pltpu.SemaphoreType.DMA((3,)), pltpu.SemaphoreType.DMA((3,)), pltpu.SemaphoreType.DMA(())],
    )(shard)


def _gather_weights(shards, *, name):
    n = len(shards)

    def body(*refs):
        w_refs, o_refs = refs[:n], refs[n:2 * n]
        ici_send, ici_recv, d2d_send, d2d_recv, local_sems = refs[2 * n:]
        x, y, c = lax.axis_index("x"), lax.axis_index("y"), lax.axis_index("c")
        me = 2 * x + y
        peers = [(_flip(x, dx), _flip(y, dy)) for dx, dy in CHIP_DELTAS]
        locals_ = [pltpu.make_async_copy(w_refs[k], o_refs[k].at[me], local_sems.at[k]) for k in range(n)]
        for cp in locals_:
            cp.start()

        def rows(k, which):
            half = shards[k].shape[0] // 2
            return pl.ds(pl.multiple_of(which * half, half), half)

        def over_chips(k, d, slot):
            tx, ty = peers[d]
            return pltpu.make_async_remote_copy(
                src_ref=w_refs[k].at[rows(k, c)], dst_ref=o_refs[k].at[slot, rows(k, c)], send_sem=ici_send.at[k, d],
                recv_sem=ici_recv.at[k, d], device_id=(tx, ty, c), device_id_type=MESH)

        def to_sibling(k, d, which):
            tx, ty = peers[d]
            at = o_refs[k].at[2 * tx + ty, rows(k, which)]
            return pltpu.make_async_remote_copy(src_ref=at, dst_ref=at, send_sem=d2d_send.at[k, d], recv_sem=d2d_recv.at[k, d],
                                                device_id=(x, y, 1 - c), device_id_type=MESH)

        sends = [over_chips(k, d, me) for k in range(n) for d in range(3)]
        for cp in sends:
            cp.start()
        passed = []
        for k in range(n):
            for d in range(3):
                over_chips(k, d, 2 * peers[d][0] + peers[d][1]).wait_recv()
                passed.append(to_sibling(k, d, c))
                passed[-1].start()
        for k in range(n):
            for d in range(3):
                to_sibling(k, d, 1 - c).wait_recv()
        for cp in sends + passed:
            cp.wait_send()
        for cp in locals_:
            cp.wait()

    return pl.pallas_call(
        body, name=name, out_shape=[jax.ShapeDtypeStruct((N_CHIPS,) + w.shape, w.dtype) for w in shards],
        in_specs=[HBM] * n, out_specs=[HBM] * n,
        scratch_shapes=[pltpu.SemaphoreType.DMA((n, 3))] * 4 + [pltpu.SemaphoreType.DMA((n,))],
    )(*shards)


def _add_into(dst_ref, src_ref):
    ns, r, _ = dst_ref.shape
    step = SUM_ROWS if r % SUM_ROWS == 0 else r
    for s in range(ns):
        def tile(t, carry):
            at = pl.ds(pl.multiple_of(t * step, step), step)
            dst_ref[s, at, :] = (dst_ref[s, at, :].astype(F32) + src_ref[s, at, :].astype(F32)).astype(dst_ref.dtype)
            return carry
        lax.fori_loop(0, r // step, tile, 0)


def _reduce_sibling(grads, *, name):
    n = len(grads)

    def body(*refs):
        g_refs, o_refs = refs[:n], refs[n:2 * n]
        mine, got = refs[2 * n:3 * n], refs[3 * n:4 * n]
        send_sems, recv_sems, load_sems, store_sems = refs[4 * n:]
        x, y, c = lax.axis_index("x"), lax.axis_index("y"), lax.axis_index("c")
        loads = [pltpu.make_async_copy(g_refs[k].at[:, c], mine[k], load_sems.at[k]) for k in range(n)]
        swaps = [pltpu.make_async_remote_copy(src_ref=g_refs[k].at[:, 1 - c], dst_ref=got[k], send_sem=send_sems.at[k],
                                              recv_sem=recv_sems.at[k], device_id=(x, y, 1 - c), device_id_type=MESH)
                 for k in range(n)]
        for cp in loads + swaps:
            cp.start()
        stores = []
        for k in range(n):
            loads[k].wait()
            swaps[k].wait_recv()
            _add_into(mine[k], got[k])
            stores.append(pltpu.make_async_copy(mine[k], o_refs[k], store_sems.at[k]))
            stores[-1].start()
        for k in range(n):
            swaps[k].wait_send()
            stores[k].wait()

    half = [jax.ShapeDtypeStruct((g.shape[0],) + g.shape[2:], g.dtype) for g in grads]
    return pl.pallas_call(
        body, name=name, out_shape=half, in_specs=[HBM] * n, out_specs=[HBM] * n,
        scratch_shapes=[pltpu.VMEM(h.shape, h.dtype) for h in half] * 2 + [pltpu.SemaphoreType.DMA((n,))] * 4,
        compiler_params=_params(),
    )(*grads)


def _reduce_chips(parts, landed, *, name):
    n_send = len(parts)
    n = n_send + len(landed)

    def body(*refs):
        p_refs, o_refs = refs[:n], refs[n:2 * n]
        got, total = refs[2 * n:3 * n], refs[3 * n:4 * n]
        send_sems, recv_sems, load_sems, share_send, share_recv, store_sems = refs[4 * n:]
        x, y, c = lax.axis_index("x"), lax.axis_index("y"), lax.axis_index("c")
        me = 2 * x + y
        peers = [(_flip(x, dx), _flip(y, dy)) for dx, dy in CHIP_DELTAS]

        def over_chips(k, d, src_slot, dst_slot):
            tx, ty = peers[d]
            return pltpu.make_async_remote_copy(
                src_ref=p_refs[k].at[src_slot], dst_ref=got[k].at[dst_slot], send_sem=send_sems.at[k, d],
                recv_sem=recv_sems.at[k, d], device_id=(tx, ty, c), device_id_type=MESH)

        loads = [pltpu.make_async_copy(p_refs[k].at[me], got[k].at[me], load_sems.at[k]) for k in range(n_send)]
        loads += [pltpu.make_async_copy(p_refs[k], got[k], load_sems.at[k]) for k in range(n_send, n)]
        sends = [over_chips(k, d, 2 * peers[d][0] + peers[d][1], me) for k in range(n_send) for d in range(3)]
        for cp in loads + sends:
            cp.start()
        shares, stores = [], []
        for k in range(n):
            loads[k].wait()
            for d in range(3 if k < n_send else 0):
                slot = 2 * peers[d][0] + peers[d][1]
                over_chips(k, d, slot, slot).wait_recv()
            r = total[k].shape[0]
            step = SUM_ROWS if r % SUM_ROWS == 0 else r

            def tile(t, carry, k=k, step=step):
                at = pl.ds(pl.multiple_of(t * step, step), step)
                acc = got[k][0, at, :].astype(F32)
                for s in range(1, N_CHIPS):
                    acc = acc + got[k][s, at, :].astype(F32)
                total[k][at, :] = acc
                return carry

            lax.fori_loop(0, r // step, tile, 0)
            stores.append(pltpu.make_async_copy(total[k], o_refs[k].at[c], store_sems.at[k]))
            shares.append(pltpu.make_async_remote_copy(
                src_ref=total[k], dst_ref=o_refs[k].at[c], send_sem=share_send.at[k], recv_sem=share_recv.at[k],
                device_id=(x, y, 1 - c), device_id_type=MESH))
            stores[-1].start()
            shares[-1].start()
        for k in range(n):
            pltpu.make_async_remote_copy(
                src_ref=total[k], dst_ref=o_refs[k].at[1 - c], send_sem=share_send.at[k], recv_sem=share_recv.at[k],
                device_id=(x, y, 1 - c), device_id_type=MESH).wait_recv()
        for cp in sends + shares:
            cp.wait_send()
        for cp in stores:
            cp.wait()

    both = list(parts) + list(landed)
    return pl.pallas_call(
        body, name=name, out_shape=[jax.ShapeDtypeStruct((2,) + p.shape[1:], F32) for p in both],
        in_specs=[HBM] * n, out_specs=[HBM] * n,
        scratch_shapes=[pltpu.VMEM(p.shape, p.dtype) for p in both] + [pltpu.VMEM(p.shape[1:], F32) for p in both]
        + [pltpu.SemaphoreType.DMA((n, 3))] * 2 + [pltpu.SemaphoreType.DMA((n,))] * 4,
        compiler_params=_params(),
    )(*both)


SEM = pl.BlockSpec(memory_space=pltpu.SEMAPHORE)
IN_FLIGHT = pltpu.SideEffectType.DATAFLOW_SIDE_EFFECTING


def _chip_copies(s_refs, l_refs, sems, scatter, theirs):
    x, y, c = lax.axis_index("x"), lax.axis_index("y"), lax.axis_index("c")
    me = 2 * x + y
    copies = []
    for k in range(len(s_refs)):
        for d, (dx, dy) in enumerate(CHIP_DELTAS):
            tx, ty = _flip(x, dx), _flip(y, dy)
            peer = 2 * tx + ty
            send_sem, recv_sem = sems[2 * (3 * k + d)], sems[2 * (3 * k + d) + 1]
            copies.append(pltpu.make_async_remote_copy(
                src_ref=s_refs[k].at[peer] if scatter else s_refs[k], dst_ref=l_refs[k].at[peer if theirs else me],
                send_sem=send_sem, recv_sem=recv_sem, device_id=(tx, ty, c), device_id_type=MESH))
    return copies


def _chips_start(srcs, lands, after, *, scatter, name):
    n = len(srcs)
    n_sem = 2 * 3 * n

    def body(*refs):
        s_refs, l_refs = refs[:n], refs[n:2 * n]
        sems = refs[2 * n + 1:2 * n + 1 + n_sem]
        token = refs[-1]
        for cp in _chip_copies(s_refs, l_refs, sems, scatter, False):
            cp.start()
        token[...] = jnp.zeros_like(token)

    hbm = lambda a: pltpu.HBM(a.shape, a.dtype)
    res = pl.pallas_call(
        body, name=name,
        out_shape=(*[pltpu.SemaphoreType.DMA(())] * n_sem, *[hbm(a) for a in srcs], *[hbm(a) for a in lands],
                   jax.ShapeDtypeStruct((8, LANES), F32)),
        in_specs=[HBM] * (2 * n) + [pl.BlockSpec(memory_space=pl.ANY)],
        out_specs=(*[SEM] * n_sem, *[HBM] * (2 * n), VMEM),
        input_output_aliases={k: n_sem + k for k in range(2 * n)},
        compiler_params=pltpu.CompilerParams(has_side_effects=IN_FLIGHT),
    )(*[pltpu.with_memory_space_constraint(a, pltpu.HBM) for a in list(srcs) + list(lands)], after)
    return res[:n_sem], res[n_sem:n_sem + n], res[n_sem + n:n_sem + 2 * n], res[-1]


def _chips_wait(sems, srcs, lands, after, *, scatter, name):
    n = len(srcs)
    n_sem = len(sems)

    def body(*refs):
        s_refs, l_refs = refs[:n], refs[n:2 * n]
        sem_refs = refs[2 * n:2 * n + n_sem]
        for cp in _chip_copies(s_refs, l_refs, sem_refs, scatter, False):
            cp.wait_send()
        for cp in _chip_copies(s_refs, l_refs, sem_refs, scatter, True):
            cp.wait_recv()

    hbm = lambda a: pltpu.HBM(a.shape, a.dtype)
    res = pl.pallas_call(
        body, name=name, out_shape=tuple(hbm(a) for a in list(srcs) + list(lands)),
        in_specs=[HBM] * (2 * n) + [SEM] * n_sem + [pl.BlockSpec(memory_space=pl.ANY)], out_specs=tuple([HBM] * (2 * n)),
        input_output_aliases={k: k for k in range(2 * n)},
        compiler_params=pltpu.CompilerParams(has_side_effects=IN_FLIGHT),
    )(*srcs, *lands, *sems, after)
    return res[n:]


def _all_gather_devices(rows, *, name):
    deltas = [(dx, dy, dc) for dx in (0, 1) for dy in (0, 1) for dc in (0, 1)][1:]

    def body(x_ref, o_ref, send_sems, recv_sems):
        x, y, c = lax.axis_index("x"), lax.axis_index("y"), lax.axis_index("c")
        me = 4 * x + 2 * y + c
        o_ref[me] = x_ref[...]
        sends, recvs = [], []
        for k, (dx, dy, dc) in enumerate(deltas):
            tx, ty, tc = _flip(x, dx), _flip(y, dy), _flip(c, dc)
            sends.append(pltpu.make_async_remote_copy(src_ref=x_ref, dst_ref=o_ref.at[me], send_sem=send_sems.at[k],
                                                      recv_sem=recv_sems.at[k], device_id=(tx, ty, tc), device_id_type=MESH))
            recvs.append(pltpu.make_async_remote_copy(src_ref=x_ref, dst_ref=o_ref.at[4 * tx + 2 * ty + tc],
                                                      send_sem=send_sems.at[k], recv_sem=recv_sems.at[k],
                                                      device_id=(tx, ty, tc), device_id_type=MESH))
        for cp in sends:
            cp.start()
        for cp in recvs:
            cp.wait_recv()
        for cp in sends:
            cp.wait_send()

    return pl.pallas_call(
        body, name=name, out_shape=jax.ShapeDtypeStruct((N_DEV,) + rows.shape, rows.dtype),
        in_specs=[VMEM], out_specs=VMEM,
        scratch_shapes=[pltpu.SemaphoreType.DMA((N_DEV - 1,)), pltpu.SemaphoreType.DMA((N_DEV - 1,))],
    )(rows)


WEIGHTS = ("ada_w", "ada_b", "ln_g", "ln_b", "e_w_in", "gmlp_norm_g", "gmlp_norm_b", "gmlp_ws", "gmlp_bs", "pool_w",
           "pool_b", "pool_scale", "e_w_out", "o_w_in", "mla_q_norm_g", "mla_kv_norm_g", "mla_w_uq", "mla_w_uk",
           "mla_w_uv", "o_w_out")
SMALL = ("ln_g", "ln_b", "gmlp_norm_g", "gmlp_norm_b", "gmlp_bs", "pool_b", "pool_scale", "mla_kv_norm_g", "mla_q_norm_g")


def _pad_cols(v, n):
    return jnp.concatenate([v, jnp.zeros((v.shape[0], n - v.shape[1]), v.dtype)], axis=1) if n > v.shape[1] else v


def _halves(g):
    return g.reshape(g.shape[0], 2, g.shape[1] // 2, g.shape[2])


def kernel(x, c, positions, ada_w, ada_b, ln_g, ln_b, e_w_in, gmlp_norm_g, gmlp_norm_b, gmlp_ws, gmlp_bs, pool_w, pool_b, pool_scale, e_w_out, o_w_in, mla_q_norm_g, mla_kv_norm_g, mla_w_uq, mla_w_uk, mla_w_uv, o_w_out, loss_target, m_ada_w, m_ada_b, m_ln_g, m_ln_b, m_e_w_in, m_gmlp_norm_g, m_gmlp_norm_b, m_gmlp_ws, m_gmlp_bs, m_pool_w, m_pool_b, m_pool_scale, m_e_w_out, m_o_w_in, m_mla_q_norm_g, m_mla_kv_norm_g, m_mla_w_uq, m_mla_w_uk, m_mla_w_uv, m_o_w_out, v_ada_w, v_ada_b, v_ln_g, v_ln_b, v_e_w_in, v_gmlp_norm_g, v_gmlp_norm_b, v_gmlp_ws, v_gmlp_bs, v_pool_w, v_pool_b, v_pool_scale, v_e_w_out, v_o_w_in, v_mla_q_norm_g, v_mla_kv_norm_g, v_mla_w_uq, v_mla_w_uk, v_mla_w_uv, v_o_w_out):
    args = dict(locals())
    weights = {n: args[n] for n in WEIGHTS}
    mom = {n: args["m_" + n] for n in WEIGHTS}
    var = {n: args["v_" + n] for n in WEIGHTS}
    ax, ay, ac = lax.axis_index("x"), lax.axis_index("y"), lax.axis_index("c")
    chip = 2 * ax + ay
    dev = 2 * chip + ac
    d = D_MODEL
    x2 = x[0]
    target = loss_target[0]
    q_rank_sh = mla_q_norm_g.shape[1]

    shards0 = [w.astype(BF16) for w in (e_w_in[0], pool_w[0].reshape(-1, POOL_GROUP_DIM), e_w_out[0])]
    shards1 = [w.astype(BF16) for w in (o_w_in[0], mla_w_uq[0].reshape(q_rank_sh, -1), o_w_out[0])]
    w_in0, pool_w_g, w_out0 = _gather_weights(shards0, name="gather_weights")
    lands1 = [lax.dynamic_update_slice(lax.empty((N_CHIPS,) + w.shape, BF16), w[None], (chip, 0, 0)) for w in shards1]
    flight1 = _chips_start(shards1, lands1, w_in0, scatter=False, name="gather1_start")
    pool_w_bf = jnp.transpose(pool_w_g.reshape(N_CHIPS, POOL_GROUPS, -1, POOL_GROUP_DIM), (1, 0, 2, 3)).reshape(
        POOL_GROUPS, POOL_GROUP_DIM, POOL_GROUP_DIM)
    w_out0 = w_out0.reshape(-1, d)
    wuk_hrd = jnp.transpose(mla_w_uk[0], (1, 0, 2)).astype(BF16)
    wuk_hdr = jnp.transpose(mla_w_uk[0], (1, 2, 0)).astype(BF16)
    wuv_hrv = jnp.transpose(mla_w_uv[0], (1, 0, 2)).astype(BF16)
    wuv_hvr = jnp.transpose(mla_w_uv[0], (1, 2, 0)).astype(BF16)
    ws = gmlp_ws[0]
    ws_t = jnp.transpose(ws, (0, 2, 1))
    bs_t = _pad_cols(gmlp_bs[0].T, LANES)

    inv = 1.0 / (ROPE_THETA ** (jnp.arange(0, MLA_ROPE, 2, dtype=F32) / MLA_ROPE))
    ang = positions[0].astype(F32)[:, None] * inv
    cos_t = jnp.tile(jnp.cos(ang), (1, 4))
    sin_t = jnp.tile(jnp.concatenate([-jnp.sin(ang), jnp.sin(ang)], axis=1), (1, 2))

    c_all = _all_gather_devices(c.reshape(8, LANES), name="gather_c").reshape(N_DEV, d)
    cols = ada_w.shape[2]
    ada_b_mine = lax.dynamic_slice_in_dim(ada_b, chip * cols, cols, axis=1)[:, None, :]
    mod_sh = _ada_mod(c_all, ada_w, ada_b_mine, name="ada_mod")
    q_norm_rows = jnp.zeros((8, cols), F32).at[0, :q_rank_sh].set(mla_q_norm_g[0])
    mod_all = _all_gather_chips(jnp.concatenate([mod_sh.reshape(2 * N_DEV, cols), q_norm_rows]), name="gather_mod")
    q_norm_g = mod_all[:, 2 * N_DEV, :q_rank_sh].reshape(1, -1)
    mod_all = jnp.transpose(mod_all[:, :2 * N_DEV].reshape(N_CHIPS, 2, N_DEV, cols), (1, 2, 0, 3)).reshape(2, N_DEV, 3 * d)
    mod = lax.dynamic_index_in_dim(mod_all, dev, axis=1, keepdims=False)
    shift = [mod[l:l + 1, :d] for l in range(2)]
    scale = [mod[l:l + 1, d:2 * d] for l in range(2)]
    gate = [mod[l:l + 1, 2 * d:] for l in range(2)]

    scale[0] = scale[0] + flight1[3][:1, :1]
    h0 = _modulate(x2, scale[0], shift[0], name="modulate0")
    proj0 = _matmul(h0, w_in0, b_stacked=True, tm=1024, tn=1280, name="proj0")
    mix0 = _even_fwd(proj0, ws, bs_t, gmlp_norm_g, gmlp_norm_b, pool_w_bf, pool_b, pool_scale, name="even_fwd")
    y0 = _matmul(mix0, w_out0, name="out0")
    x1, h1 = _resid_ln_modulate(x2, y0, gate[0], ln_g[0:1], ln_b[0:1], scale[1], shift[1], name="resid_ln0")

    w_in1_g, w_uq_g, w_out1 = _chips_wait(*flight1[:3], h1, scatter=False, name="gather1_wait")
    w_out1 = w_out1.reshape(-1, d)
    w_in1 = jnp.transpose(w_in1_g, (1, 0, 2)).reshape(d, ODD_IN)
    w_in1 = jnp.concatenate([_pad_cols(w_in1[:, :ODD_SMALL], ODD_SMALL_PAD), w_in1[:, ODD_SMALL:]], axis=1)
    w_uq = w_uq_g.reshape(MLA_Q_RANK, MLA_HEADS, MLA_NOPE + MLA_ROPE)
    w_uq_nope = w_uq[:, :, :MLA_NOPE].reshape(MLA_Q_RANK, -1)
    w_uq_rope = w_uq[:, :, MLA_NOPE:].reshape(MLA_Q_RANK, -1)
    proj1 = _matmul(h1, w_in1, tm=1024, tn=1280, name="proj1")
    q_cn, keys = _mla_prep(proj1, q_norm_g, mla_kv_norm_g, cos_t, sin_t, name="mla_prep")
    q_nope = _matmul(q_cn, w_uq_nope, tm=1024, tn=2048, name="q_nope", out_dtype=BF16)
    q_rope_pre = _matmul(q_cn, w_uq_rope, tm=1024, name="q_rope")
    q = _q_build(q_nope, q_rope_pre, wuk_hdr, cos_t, sin_t, name="q_build")
    o_lat, lse = _attn_fwd(q, keys, name="attn_fwd")
    og = _o_build(o_lat, wuv_hrv, proj1, name="o_build")
    y1 = _matmul(og, w_out1, name="out1")

    dy1, dres1, g_ln_g1, g_ln_b1, dgate1, loss = _loss_ln_bwd(x1, y1, gate[1], ln_g[1:2], ln_b[1:2], target, name="loss_ln1")
    dg1 = _matmul(dy1, w_out1, trans_b=True, tn=2048, name="d_og")
    g_w_out1 = _matmul(og, dy1, trans_a=True, out_dtype=BF16, tm=1024, name="g_out1")
    do_lat, dz, g_uv = _o_bwd(dg1, proj1, o_lat, wuv_hrv, wuv_hvr, name="o_bwd")
    dq, dkeys = _attn_bwd(q, keys, do_lat, o_lat, lse, name="attn_bwd")
    dq_nope, dq_rope, g_uk = _q_bwd(dq, q_nope, wuk_hrd, cos_t, sin_t, name="q_bwd")
    dq_cn = (_matmul(dq_nope, w_uq_nope, trans_b=True, tm=1024, name="d_qcn_nope")
             + _matmul(dq_rope, w_uq_rope, trans_b=True, tm=1024, name="d_qcn_rope"))
    g_uq_nope = _matmul(q_cn, dq_nope, trans_a=True, out_dtype=BF16, tn=2048, name="g_uq_nope")
    g_uq_rope = _matmul(q_cn, dq_rope, trans_a=True, out_dtype=BF16, name="g_uq_rope")
    dsmall, g_qg, g_kvg = _mla_prep_bwd(proj1, dq_cn, dkeys, q_norm_g, mla_kv_norm_g, cos_t, sin_t, name="mla_prep_bwd")
    dproj1 = jnp.concatenate([dsmall, dz], axis=1)
    dh1 = _matmul(dproj1, w_in1, trans_b=True, name="d_h1")
    g_w_in1 = _matmul(h1, dproj1, trans_a=True, out_dtype=BF16, tm=1024, tn=1280, name="g_in1")

    g_uq = jnp.concatenate([g_uq_nope.reshape(MLA_Q_RANK, MLA_HEADS, MLA_NOPE), g_uq_rope.reshape(MLA_Q_RANK, MLA_HEADS, MLA_ROPE)], axis=2)
    g_w_in1 = jnp.concatenate([g_w_in1[:, :ODD_SMALL], g_w_in1[:, ODD_SMALL_PAD:]], axis=1)
    g_w_in1 = jnp.transpose(g_w_in1.reshape(d, N_CHIPS, -1), (1, 0, 2))
    big1 = [
        _halves(g_w_in1),
        _halves(g_uq.reshape(N_CHIPS, q_rank_sh, -1)),
        _halves(g_w_out1.reshape(N_CHIPS, -1, d)),
        _halves(g_uk.astype(BF16).reshape(N_CHIPS, -1, MLA_NOPE)),
        _halves(g_uv.astype(BF16).reshape(N_CHIPS, -1, MLA_V)),
    ]
    parts1 = _reduce_sibling(big1, name="reduce_sibling1")
    lands2 = [lax.dynamic_update_slice(lax.empty(p.shape, BF16), lax.dynamic_slice_in_dim(p, chip, 1, axis=0), (chip, 0, 0))
              for p in parts1]
    flight2 = _chips_start(parts1, lands2, loss, scatter=True, name="reduce1_start")

    gate[0] = gate[0] + flight2[3][:1, :1]
    dy0, dres0, g_ln_g0, g_ln_b0, dgate0, dscale1, dshift1 = _mid_ln_bwd(
        x2, y0, gate[0], ln_g[0:1], ln_b[0:1], dh1, dres1, scale[1], x1, name="mid_ln0")
    dmix0 = _matmul(dy0, w_out0, trans_b=True, tn=2048, name="d_mix0")
    g_w_out0 = _matmul(mix0, dy0, trans_a=True, out_dtype=BF16, tm=1024, name="g_out0")
    dproj0, g_ws, g_bs_t, g_ng, g_nb, g_pw, g_pb, g_ps = _even_bwd(
        proj0, dmix0, ws, ws_t, bs_t, gmlp_norm_g, gmlp_norm_b, pool_w_bf, pool_b, pool_scale, name="even_bwd")
    dh0 = _matmul(dproj0, w_in0, trans_b=True, b_stacked=True, tm=1024, name="d_h0")
    g_w_in0 = _matmul(h0, dproj0, trans_a=True, out_dtype=BF16, out_stacked=True, tm=1024, tn=1280, name="g_in0")
    grad_x, dscale0, dshift0 = _input_bwd(x2, dh0, dres0, scale[0], name="input_bwd")

    small_local = {
        "ln_g": jnp.concatenate([g_ln_g0, g_ln_g1]), "ln_b": jnp.concatenate([g_ln_b0, g_ln_b1]),
        "gmlp_norm_g": g_ng, "gmlp_norm_b": g_nb, "gmlp_bs": g_bs_t[:, :GMLP_HEADS].T, "pool_b": g_pb, "pool_scale": g_ps,
        "mla_kv_norm_g": g_kvg, "mla_q_norm_g": g_qg,
    }
    n_mod = 2 * 3 * d
    vec = jnp.concatenate([dshift0, dscale0, dgate0, dshift1, dscale1, dgate1]
                          + [small_local[n].reshape(1, -1) for n in SMALL], axis=1)
    n_vec = vec.shape[1]
    vec = _pad_cols(vec, -(-n_vec // (8 * LANES)) * 8 * LANES).reshape(-1, LANES)
    vec_all = _all_gather_devices(vec, name="gather_small")
    vec_sum = _sum_devices(vec_all, name="sum_small").reshape(-1)
    dmod_all = vec_all.reshape(N_DEV, -1)[:, :n_mod].reshape(N_DEV, 2, 3 * d)
    dmod_sh = jnp.transpose(lax.dynamic_slice_in_dim(dmod_all, chip * cols, cols, axis=2), (1, 0, 2))
    dmod_sh = jnp.concatenate([dmod_sh, jnp.zeros((2, LANES - N_DEV, cols), F32)], axis=1)
    grads = {"ada_w": _ada_grad(_pad_cols(c_all.T, LANES), dmod_sh, name="ada_grad"), "ada_b": vec_sum[:n_mod].reshape(2, 3 * d)}
    off = n_mod
    for n in SMALL:
        sz = small_local[n].size
        grads[n] = vec_sum[off:off + sz]
        off += sz
    grads["mla_q_norm_g"] = lax.dynamic_slice_in_dim(grads["mla_q_norm_g"], chip * q_rank_sh, q_rank_sh)
    for n in SMALL:
        grads[n] = grads[n].reshape(weights[n].shape)

    g_pw = jnp.transpose(g_pw.astype(BF16).reshape(POOL_GROUPS, N_CHIPS, -1, POOL_GROUP_DIM), (1, 0, 2, 3))
    big0 = [
        _halves(g_w_in0),
        _halves(g_pw.reshape(N_CHIPS, -1, POOL_GROUP_DIM)),
        _halves(g_w_out0.reshape(N_CHIPS, -1, d)),
        _halves(g_ws.astype(BF16)),
    ]
    parts0 = _reduce_sibling(big0, name="reduce_sibling0")
    landed1 = _chips_wait(*flight2[:3], parts0[0], scatter=True, name="reduce1_wait")
    totals = _reduce_chips(parts0, landed1, name="reduce_chips")
    for n, t in zip(("e_w_in", "pool_w", "e_w_out", "gmlp_ws", "o_w_in", "mla_w_uq", "o_w_out"), totals):
        if n != "gmlp_ws":
            grads[n] = t.reshape(weights[n].shape)
    rep = jnp.concatenate([t.reshape(-1, LANES) for t in (totals[3], totals[7], totals[8])])
    rep = _all_gather_chips(rep, name="gather_rep")
    r_ws, r_uk = GMLP_BLOCK, 4 * MLA_KV_RANK
    grads["gmlp_ws"] = rep[:, :r_ws].reshape(weights["gmlp_ws"].shape)
    grads["mla_w_uk"] = jnp.transpose(rep[:, r_ws:r_ws + r_uk].reshape(MLA_HEADS, MLA_KV_RANK, MLA_NOPE), (1, 0, 2))[None]
    grads["mla_w_uv"] = jnp.transpose(rep[:, r_ws + r_uk:].reshape(MLA_HEADS, MLA_KV_RANK, MLA_V), (1, 0, 2))[None]

    delta, new_m, new_v = {}, {}, {}
    large = [n for n in WEIGHTS if n not in SMALL and n != "ada_b"]
    for n in large:
        delta[n], new_m[n], new_v[n] = _adamw(weights[n], grads[n], mom[n], var[n], name="adamw_" + n)
    small = [n for n in WEIGHTS if n not in large]
    ds, ms, vs = _adamw_small([weights[n] for n in small], [grads[n] for n in small], [mom[n] for n in small],
                              [var[n] for n in small], name="adamw_small")
    for n, dn, mn, vn in zip(small, ds, ms, vs):
        delta[n], new_m[n], new_v[n] = dn, mn, vn

    loss_total = lax.psum(loss[0, 0], ("x", "y", "c"))
    return (loss_total, grad_x[None], *[grads[n] for n in WEIGHTS], *[delta[n] for n in WEIGHTS],
            *[new_m[n] for n in WEIGHTS], *[new_v[n] for n in WEIGHTS])
```

```python
import jax
import jax.numpy as jnp
from jax import lax
from jax.experimental import pallas as pl
from jax.experimental.pallas import tpu as pltpu

F32 = jnp.float32
BF16 = jnp.bfloat16
MESH = pl.DeviceIdType.MESH

D_MODEL = 1024
CHUNK = 64
LN_EPS = 1e-5
GMLP_HEADS = 4
GMLP_HEAD_DIM = 256
GMLP_BLOCK = 128
POOL_WINDOWS = (2, 4, 8, 16)
POOL_GROUPS = 4
POOL_GROUP_DIM = 256
POOL_HALO = 16
EVEN_IN = 5120
MLA_HEADS = 16
MLA_NOPE = 128
MLA_ROPE = 64
MLA_V = 128
MLA_Q_RANK = 256
MLA_KV_RANK = 128
MLA_WIDTH = MLA_HEADS * MLA_V
ODD_IN = 2496
ODD_SMALL = MLA_Q_RANK + MLA_KV_RANK + MLA_ROPE
ODD_SMALL_PAD = 512
QK_PAD = 256
ROPE_THETA = 10000.0
ATTN_SCALE = (MLA_NOPE + MLA_ROPE) ** -0.5
DEEPNORM_ALPHA = (2.0 * 2) ** 0.25
ADAM_LR = 0.001
ADAM_B1 = 0.9
ADAM_B2 = 0.999
ADAM_EPS = 1e-08
ADAM_WD = 0.01
ADAM_STEP = 10
NEG = -1e30
LANES = 128
N_DEV = 8
N_CHIPS = 4
VMEM_LIMIT_BYTES = 56 * 1024 * 1024
HBM = pl.BlockSpec(memory_space=pltpu.HBM)
VMEM = pl.BlockSpec(memory_space=pltpu.VMEM)


def _params(*sem):
    return pltpu.CompilerParams(dimension_semantics=sem if sem else None, vmem_limit_bytes=VMEM_LIMIT_BYTES)


def _tile(dim, pref):
    for t in (pref, 2048, 1280, 1024, 512, 256, 128):
        if t <= min(pref, dim) and dim % t == 0:
            return t
    return dim


def _sigmoid(z):
    return 1.0 / (1.0 + jnp.exp(-z))


def _dot(a, b, dims):
    return lax.dot_general(a, b, (dims, ((), ())), preferred_element_type=F32)


NN = ((1,), (0,))
NT = ((1,), (1,))
TN = ((0,), (0,))


def _matmul(a, b, *, name, trans_a=False, trans_b=False, out_dtype=F32, b_stacked=False, out_stacked=False,
            tm=512, tn=1024, tk=2048, after=None):
    k, m = a.shape if trans_a else a.shape[::-1]
    if b_stacked:
        ns, kb, n_sh = b.shape
        kb, n = (ns * n_sh, kb) if trans_b else (kb, ns * n_sh)
    else:
        n, kb = b.shape if trans_b else b.shape[::-1]
    assert k == kb, (a.shape, b.shape)
    tm = _tile(m, tm)
    if b_stacked and trans_b:
        tn, tk = _tile(n, tn), n_sh
    elif b_stacked or out_stacked:
        tn, tk = _tile(n // N_CHIPS, tn), _tile(k, tk)
    else:
        tn, tk = _tile(n, tn), _tile(k, tk)
    nk = k // tk
    per = max((n // N_CHIPS) // tn, 1)
    dims = ((0 if trans_a else 1,), (1 if trans_b else 0,))

    def body_one(a_ref, b_ref, *rest):
        o_ref = rest[-1]
        o_ref[...] = _dot(a_ref[...].astype(BF16), b_ref[...].astype(BF16), dims).astype(out_dtype)

    def body_acc(a_ref, b_ref, *rest):
        o_ref, acc_ref = rest[-2:]
        kk = pl.program_id(2)

        @pl.when(kk == 0)
        def _():
            acc_ref[...] = jnp.zeros_like(acc_ref)

        acc_ref[...] += _dot(a_ref[...].astype(BF16), b_ref[...].astype(BF16), dims)

        @pl.when(kk == nk - 1)
        def _():
            o_ref[...] = acc_ref[...].astype(out_dtype)

    a_spec = pl.BlockSpec((tk, tm), lambda i, j, kk: (kk, i)) if trans_a else pl.BlockSpec((tm, tk), lambda i, j, kk: (i, kk))
    if b_stacked and trans_b:
        b_spec = pl.BlockSpec((None, tn, tk), lambda i, j, kk: (kk, j, 0))
    elif b_stacked:
        b_spec = pl.BlockSpec((None, tk, tn), lambda i, j, kk: (j // per, kk, j % per))
    elif trans_b:
        b_spec = pl.BlockSpec((tn, tk), lambda i, j, kk: (j, kk))
    else:
        b_spec = pl.BlockSpec((tk, tn), lambda i, j, kk: (kk, j))
    if out_stacked:
        o_spec = pl.BlockSpec((None, tm, tn), lambda i, j, kk: (j // per, i, j % per))
        o_shape = jax.ShapeDtypeStruct((N_CHIPS, m, n // N_CHIPS), out_dtype)
    else:
        o_spec = pl.BlockSpec((tm, tn), lambda i, j, kk: (i, j))
        o_shape = jax.ShapeDtypeStruct((m, n), out_dtype)
    order = [] if after is None else [after]
    return pl.pallas_call(
        body_one if nk == 1 else body_acc, name=name, grid=(m // tm, n // tn, nk),
        in_specs=[a_spec, b_spec] + [pl.BlockSpec(memory_space=pl.ANY)] * len(order),
        out_specs=o_spec, out_shape=o_shape, scratch_shapes=[] if nk == 1 else [pltpu.VMEM((tm, tn), F32)],
        compiler_params=_params("parallel", "parallel", "arbitrary"),
    )(a, b, *order)


def _row_spec(ts, d):
    return pl.BlockSpec((ts, d), lambda i: (i, 0))


def _vec_spec(d):
    return pl.BlockSpec((1, d), lambda i: (0, 0))


def _modulate(x, scale, shift, *, name):
    s, d = x.shape
    ts = _tile(s, 512)

    def body(x_ref, sc_ref, sh_ref, h_ref):
        h_ref[...] = (x_ref[...] * (1.0 + sc_ref[...]) + sh_ref[...]).astype(BF16)

    return pl.pallas_call(
        body, name=name, grid=(s // ts,), in_specs=[_row_spec(ts, d), _vec_spec(d), _vec_spec(d)],
        out_specs=_row_spec(ts, d), out_shape=jax.ShapeDtypeStruct((s, d), BF16), compiler_params=_params("parallel"),
    )(x, scale, shift)


def _ln_stats(pre):
    mu = jnp.mean(pre, axis=-1, keepdims=True)
    xc = pre - mu
    var = jnp.mean(xc * xc, axis=-1, keepdims=True)
    rstd = lax.rsqrt(var + LN_EPS)
    return xc * rstd, rstd


def _ln_bwd_rows(dout, xhat, rstd, g):
    dxh = dout * g
    m1 = jnp.mean(dxh, axis=-1, keepdims=True)
    m2 = jnp.mean(dxh * xhat, axis=-1, keepdims=True)
    return rstd * (dxh - m1 - xhat * m2)


def _colsum(v):
    return jnp.sum(v, axis=0, keepdims=True)


def _resid_ln_modulate(x, y, gate, g, b, scale_next, shift_next, *, name):
    s, d = x.shape
    ts = _tile(s, 512)

    def body(x_ref, y_ref, gate_ref, g_ref, b_ref, sc_ref, sh_ref, xn_ref, h_ref):
        pre = DEEPNORM_ALPHA * x_ref[...] + (1.0 + gate_ref[...]) * y_ref[...]
        xhat, _ = _ln_stats(pre)
        xn = xhat * g_ref[...] + b_ref[...]
        xn_ref[...] = xn
        h_ref[...] = (xn * (1.0 + sc_ref[...]) + sh_ref[...]).astype(BF16)

    return pl.pallas_call(
        body, name=name, grid=(s // ts,),
        in_specs=[_row_spec(ts, d), _row_spec(ts, d)] + [_vec_spec(d)] * 5,
        out_specs=[_row_spec(ts, d), _row_spec(ts, d)],
        out_shape=[jax.ShapeDtypeStruct((s, d), F32), jax.ShapeDtypeStruct((s, d), BF16)],
        compiler_params=_params("parallel"),
    )(x, y, gate, g, b, scale_next, shift_next)


def _loss_ln_bwd(x, y, gate, g, b, target, *, name):
    s, d = x.shape
    ts = _tile(s, 512)

    def body(x_ref, y_ref, gate_ref, g_ref, b_ref, t_ref, dy_ref, dres_ref, dg_ref, db_ref, dgate_ref, loss_ref):
        @pl.when(pl.program_id(0) == 0)
        def _():
            for r in (dg_ref, db_ref, dgate_ref, loss_ref):
                r[...] = jnp.zeros_like(r)

        yv = y_ref[...]
        pre = DEEPNORM_ALPHA * x_ref[...] + (1.0 + gate_ref[...]) * yv
        xhat, rstd = _ln_stats(pre)
        diff = xhat * g_ref[...] + b_ref[...] - t_ref[...]
        loss_ref[...] += (0.5 / d) * jnp.sum(jnp.sum(diff * diff, axis=1, keepdims=True), axis=0, keepdims=True)
        dout = diff * (1.0 / d)
        dpre = _ln_bwd_rows(dout, xhat, rstd, g_ref[...])
        dy_ref[...] = (dpre * (1.0 + gate_ref[...])).astype(BF16)
        dres_ref[...] = DEEPNORM_ALPHA * dpre
        dg_ref[...] += _colsum(dout * xhat)
        db_ref[...] += _colsum(dout)
        dgate_ref[...] += _colsum(dpre * yv)

    vec = jax.ShapeDtypeStruct((1, d), F32)
    return pl.pallas_call(
        body, name=name, grid=(s // ts,),
        in_specs=[_row_spec(ts, d), _row_spec(ts, d), _vec_spec(d), _vec_spec(d), _vec_spec(d), _row_spec(ts, d)],
        out_specs=[_row_spec(ts, d), _row_spec(ts, d), _vec_spec(d), _vec_spec(d), _vec_spec(d), _vec_spec(1)],
        out_shape=[jax.ShapeDtypeStruct((s, d), BF16), jax.ShapeDtypeStruct((s, d), F32), vec, vec, vec,
                   jax.ShapeDtypeStruct((1, 1), F32)],
        compiler_params=_params("arbitrary"),
    )(x, y, gate, g, b, target)


def _mid_ln_bwd(x, y, gate, g, b, dh_next, dres_next, scale_next, x_next, *, name):
    s, d = x.shape
    ts = _tile(s, 512)

    def body(x_ref, y_ref, gate_ref, g_ref, b_ref, dh_ref, dr_ref, sc_ref, xn_ref,
             dy_ref, dres_ref, dg_ref, db_ref, dgate_ref, dscale_ref, dshift_ref):
        @pl.when(pl.program_id(0) == 0)
        def _():
            for r in (dg_ref, db_ref, dgate_ref, dscale_ref, dshift_ref):
                r[...] = jnp.zeros_like(r)

        dh = dh_ref[...]
        dout = dr_ref[...] + dh * (1.0 + sc_ref[...])
        dscale_ref[...] += _colsum(dh * xn_ref[...])
        dshift_ref[...] += _colsum(dh)
        yv = y_ref[...]
        pre = DEEPNORM_ALPHA * x_ref[...] + (1.0 + gate_ref[...]) * yv
        xhat, rstd = _ln_stats(pre)
        dpre = _ln_bwd_rows(dout, xhat, rstd, g_ref[...])
        dy_ref[...] = (dpre * (1.0 + gate_ref[...])).astype(BF16)
        dres_ref[...] = DEEPNORM_ALPHA * dpre
        dg_ref[...] += _colsum(dout * xhat)
        db_ref[...] += _colsum(dout)
        dgate_ref[...] += _colsum(dpre * yv)

    vec = jax.ShapeDtypeStruct((1, d), F32)
    return pl.pallas_call(
        body, name=name, grid=(s // ts,),
        in_specs=[_row_spec(ts, d), _row_spec(ts, d), _vec_spec(d), _vec_spec(d), _vec_spec(d),
                  _row_spec(ts, d), _row_spec(ts, d), _vec_spec(d), _row_spec(ts, d)],
        out_specs=[_row_spec(ts, d), _row_spec(ts, d)] + [_vec_spec(d)] * 5,
        out_shape=[jax.ShapeDtypeStruct((s, d), BF16), jax.ShapeDtypeStruct((s, d), F32)] + [vec] * 5,
        compiler_params=_params("arbitrary"),
    )(x, y, gate, g, b, dh_next, dres_next, scale_next, x_next)


def _input_bwd(x, dh, dres, scale, *, name):
    s, d = x.shape
    ts = _tile(s, 512)

    def body(x_ref, dh_ref, dr_ref, sc_ref, dx_ref, dscale_ref, dshift_ref):
        @pl.when(pl.program_id(0) == 0)
        def _():
            dscale_ref[...] = jnp.zeros_like(dscale_ref)
            dshift_ref[...] = jnp.zeros_like(dshift_ref)

        dh = dh_ref[...]
        dx_ref[...] = dr_ref[...] + dh * (1.0 + sc_ref[...])
        dscale_ref[...] += _colsum(dh * x_ref[...])
        dshift_ref[...] += _colsum(dh)

    vec = jax.ShapeDtypeStruct((1, d), F32)
    return pl.pallas_call(
        body, name=name, grid=(s // ts,),
        in_specs=[_row_spec(ts, d), _row_spec(ts, d), _row_spec(ts, d), _vec_spec(d)],
        out_specs=[_row_spec(ts, d), _vec_spec(d), _vec_spec(d)],
        out_shape=[jax.ShapeDtypeStruct((s, d), F32), vec, vec],
        compiler_params=_params("arbitrary"),
    )(x, dh, dres, scale)


def _chunk_mask(transposed=False):
    r = lax.broadcasted_iota(jnp.int32, (GMLP_BLOCK, GMLP_BLOCK), 0) // CHUNK
    c = lax.broadcasted_iota(jnp.int32, (GMLP_BLOCK, GMLP_BLOCK), 1) // CHUNK
    return (r <= c) if transposed else (c <= r)


def _window_sum(ext, steps, forward):
    rows = ext.shape[0]
    acc = ext
    for k in range(steps):
        shift = 1 << k
        acc = acc + pltpu.roll(acc, (rows - shift) if forward else shift, 0)
    return acc


def _pool_counts(first_row, rows, win):
    t = first_row + lax.broadcasted_iota(jnp.int32, (rows, 1), 0)
    return jnp.minimum(t + 1, win).astype(F32)


def _even_specs(t):
    col = lambda j: pl.BlockSpec((t, D_MODEL), lambda n: (n, j))
    per = t // POOL_HALO
    prev = pl.BlockSpec((POOL_HALO, D_MODEL), lambda n: (jnp.maximum(n * per - 1, 0), 3))
    return col, per, prev


def _full(shape):
    return pl.BlockSpec(shape, lambda n: (0,) * len(shape))


def _gmlp_head(v_h, ng, nb, w_bf):
    xhat, rstd = _ln_stats(v_h)
    vn = (xhat * ng + nb).astype(BF16)
    return xhat, rstd, vn, _dot(w_bf, vn, NN)


def _pool_group(xb_g, prev_g, first_row, grp):
    t = xb_g.shape[0]
    ext = jnp.concatenate([prev_g, xb_g], axis=0)
    tot = _window_sum(ext, grp + 1, False)[POOL_HALO:, :]
    cnt = _pool_counts(first_row, t, POOL_WINDOWS[grp])
    return tot / cnt - xb_g, cnt


def _even_fwd(proj, ws, bs_t, ng, nb, pool_w, pool_b, pool_scale, *, name):
    s = proj.shape[0]
    t = GMLP_BLOCK
    col, per, prev = _even_specs(t)

    def body(u_ref, v_ref, za_ref, xb_ref, zb_ref, xp_ref, ws_ref, bs_ref, ng_ref, nb_ref, pw_ref, pb_ref, ps_ref, o_ref):
        n = pl.program_id(0)
        mask = _chunk_mask()
        for h in range(GMLP_HEADS):
            c0 = h * GMLP_HEAD_DIM
            cs = slice(c0, c0 + GMLP_HEAD_DIM)
            w_bf = jnp.where(mask, ws_ref[h], 0.0).astype(BF16)
            _, _, _, sv = _gmlp_head(v_ref[:, cs], ng_ref[...], nb_ref[...], w_bf)
            sv = sv + bs_ref[:, h:h + 1]
            za = za_ref[:, cs]
            o_ref[:, cs] = (u_ref[:, cs] * sv * (za * _sigmoid(za))).astype(BF16)
        live = (n > 0).astype(F32)
        for grp in range(POOL_GROUPS):
            c0 = grp * POOL_GROUP_DIM
            cs = slice(c0, c0 + POOL_GROUP_DIM)
            pooled, _ = _pool_group(xb_ref[:, cs], xp_ref[:, cs] * live, n * t, grp)
            yb = _dot(pooled.astype(BF16), pw_ref[grp], NN) + pb_ref[:, cs]
            zb = zb_ref[:, cs]
            o_ref[:, D_MODEL + c0:D_MODEL + c0 + POOL_GROUP_DIM] = (yb * ps_ref[:, cs] * (zb * _sigmoid(zb))).astype(BF16)

    return pl.pallas_call(
        body, name=name, grid=(s // t,),
        in_specs=[col(0), col(1), col(2), col(3), col(4), prev,
                  _full((GMLP_HEADS, t, t)), _full((t, LANES)), _full((1, GMLP_HEAD_DIM)), _full((1, GMLP_HEAD_DIM)),
                  _full((POOL_GROUPS, POOL_GROUP_DIM, POOL_GROUP_DIM)), _full((1, D_MODEL)), _full((1, D_MODEL))],
        out_specs=pl.BlockSpec((t, 2 * D_MODEL), lambda n: (n, 0)),
        out_shape=jax.ShapeDtypeStruct((s, 2 * D_MODEL), BF16),
        compiler_params=_params("parallel"),
    )(proj, proj, proj, proj, proj, proj, ws, bs_t, ng, nb, pool_w, pool_b, pool_scale)


def _even_bwd(proj, dmix, ws, ws_t, bs_t, ng, nb, pool_w, pool_b, pool_scale, *, name):
    s = proj.shape[0]
    t = GMLP_BLOCK
    nblk = s // t
    col, per, prev = _even_specs(t)
    nxt = lambda j: pl.BlockSpec((POOL_HALO, D_MODEL), lambda n: (jnp.minimum((n + 1) * per, nblk * per - 1), j))

    def body(u_ref, v_ref, za_ref, xb_ref, zb_ref, xp_ref, zn_ref, da_ref, db_ref, dbn_ref,
             ws_ref, wst_ref, bs_ref, ng_ref, nb_ref, pw_ref, pb_ref, ps_ref,
             dp_ref, gws_ref, gbs_ref, gng_ref, gnb_ref, gpw_ref, gpb_ref, gps_ref):
        n = pl.program_id(0)

        @pl.when(n == 0)
        def _():
            for r in (gws_ref, gbs_ref, gng_ref, gnb_ref, gpw_ref, gpb_ref, gps_ref):
                r[...] = jnp.zeros_like(r)

        mask, mask_t = _chunk_mask(), _chunk_mask(True)
        lane = lax.broadcasted_iota(jnp.int32, (t, LANES), 1)
        ngv, nbv = ng_ref[...], nb_ref[...]
        for h in range(GMLP_HEADS):
            c0 = h * GMLP_HEAD_DIM
            cs = slice(c0, c0 + GMLP_HEAD_DIM)
            w_bf = jnp.where(mask, ws_ref[h], 0.0).astype(BF16)
            wt_bf = jnp.where(mask_t, wst_ref[h], 0.0).astype(BF16)
            xhat, rstd, vn, sv = _gmlp_head(v_ref[:, cs], ngv, nbv, w_bf)
            sv = sv + bs_ref[:, h:h + 1]
            za, u, da = za_ref[:, cs], u_ref[:, cs], da_ref[:, cs]
            sg = _sigmoid(za)
            sl = za * sg
            dp_ref[:, cs] = (da * sv * sl).astype(BF16)
            dp_ref[:, 2 * D_MODEL + c0:2 * D_MODEL + c0 + GMLP_HEAD_DIM] = (
                da * u * sv * (sg * (1.0 + za * (1.0 - sg)))).astype(BF16)
            dsv = da * u * sl
            gbs_ref[...] += jnp.where(lane == h, jnp.sum(dsv, axis=1, keepdims=True), 0.0)
            dsv_bf = dsv.astype(BF16)
            gws_ref[h] += jnp.where(mask, _dot(dsv_bf, vn, NT), 0.0)
            dvn = _dot(wt_bf, dsv_bf, NN)
            dp_ref[:, D_MODEL + c0:D_MODEL + c0 + GMLP_HEAD_DIM] = _ln_bwd_rows(dvn, xhat, rstd, ngv).astype(BF16)
            gng_ref[...] += _colsum(dvn * xhat)
            gnb_ref[...] += _colsum(dvn)
        live_prev = (n > 0).astype(F32)
        live_next = (n < nblk - 1).astype(F32)
        for grp in range(POOL_GROUPS):
            c0 = grp * POOL_GROUP_DIM
            cs = slice(c0, c0 + POOL_GROUP_DIM)
            xb = xb_ref[:, cs]
            pooled, cnt = _pool_group(xb, xp_ref[:, cs] * live_prev, n * t, grp)
            pooled_bf = pooled.astype(BF16)
            pw = pw_ref[grp]
            yb = _dot(pooled_bf, pw, NN) + pb_ref[:, cs]
            ps = ps_ref[:, cs]
            zb, db = zb_ref[:, cs], db_ref[:, cs]
            sg = _sigmoid(zb)
            sl = zb * sg
            dp_ref[:, 4 * D_MODEL + c0:4 * D_MODEL + c0 + POOL_GROUP_DIM] = (
                db * yb * ps * (sg * (1.0 + zb * (1.0 - sg)))).astype(BF16)
            dsl = db * sl
            dy = dsl * ps
            gps_ref[:, cs] += _colsum(dsl * yb)
            gpb_ref[:, cs] += _colsum(dy)
            dy_bf = dy.astype(BF16)
            gpw_ref[grp] += _dot(pooled_bf, dy_bf, TN)
            r = _dot(dy_bf, pw, NT)
            zn = zn_ref[:, cs]
            dyn = (dbn_ref[:, cs] * (zn * _sigmoid(zn)) * ps * live_next).astype(BF16)
            rn = _dot(dyn, pw, NT) / _pool_counts((n + 1) * t, POOL_HALO, POOL_WINDOWS[grp])
            ext = jnp.concatenate([r / cnt, rn], axis=0)
            dxb = _window_sum(ext, grp + 1, True)[:t, :] - r
            dp_ref[:, 3 * D_MODEL + c0:3 * D_MODEL + c0 + POOL_GROUP_DIM] = dxb.astype(BF16)

    out_shape = [
        jax.ShapeDtypeStruct((s, EVEN_IN), BF16),
        jax.ShapeDtypeStruct((GMLP_HEADS, t, t), F32), jax.ShapeDtypeStruct((t, LANES), F32),
        jax.ShapeDtypeStruct((1, GMLP_HEAD_DIM), F32), jax.ShapeDtypeStruct((1, GMLP_HEAD_DIM), F32),
        jax.ShapeDtypeStruct((POOL_GROUPS, POOL_GROUP_DIM, POOL_GROUP_DIM), F32),
        jax.ShapeDtypeStruct((1, D_MODEL), F32), jax.ShapeDtypeStruct((1, D_MODEL), F32),
    ]
    return pl.pallas_call(
        body, name=name, grid=(nblk,),
        in_specs=[col(0), col(1), col(2), col(3), col(4), prev, nxt(4),
                  pl.BlockSpec((t, D_MODEL), lambda n: (n, 0)), pl.BlockSpec((t, D_MODEL), lambda n: (n, 1)), nxt(1),
                  _full((GMLP_HEADS, t, t)), _full((GMLP_HEADS, t, t)), _full((t, LANES)),
                  _full((1, GMLP_HEAD_DIM)), _full((1, GMLP_HEAD_DIM)),
                  _full((POOL_GROUPS, POOL_GROUP_DIM, POOL_GROUP_DIM)), _full((1, D_MODEL)), _full((1, D_MODEL))],
        out_specs=[pl.BlockSpec((t, EVEN_IN), lambda n: (n, 0))] + [_full(o.shape) for o in out_shape[1:]],
        out_shape=out_shape,
        compiler_params=_params("arbitrary"),
    )(proj, proj, proj, proj, proj, proj, proj, dmix, dmix, dmix, ws, ws_t, bs_t, ng, nb, pool_w, pool_b, pool_scale)


def _half_swap(v):
    lane = lax.broadcasted_iota(jnp.int32, v.shape, 1)
    return jnp.where(lane % MLA_ROPE < MLA_ROPE // 2, pltpu.roll(v, LANES - MLA_ROPE // 2, 1), pltpu.roll(v, MLA_ROPE // 2, 1))


def _rope(v, cos, sin_signed):
    return v * cos + _half_swap(v) * sin_signed


def _rope_bwd(d, cos, sin_signed):
    return d * cos + _half_swap(d * sin_signed)


def _rms(v, g):
    r = lax.rsqrt(jnp.mean(v * v, axis=-1, keepdims=True) + LN_EPS)
    return v * r * g, r


def _rms_bwd(dy, v, r, g):
    u = dy * g
    return r * u - v * (r * r * r) * jnp.mean(u * v, axis=-1, keepdims=True)


def _mla_prep(proj, gq, gkv, cos, sin_signed, *, name):
    s = proj.shape[0]
    ts = _tile(s, 512)

    def body(p_ref, gq_ref, gkv_ref, c_ref, s_ref, q_ref, k_ref):
        qcn, _ = _rms(p_ref[:, :MLA_Q_RANK], gq_ref[...])
        kvn, _ = _rms(p_ref[:, MLA_Q_RANK:MLA_Q_RANK + MLA_KV_RANK], gkv_ref[...])
        kr = _rope(p_ref[:, MLA_Q_RANK + MLA_KV_RANK:], c_ref[...], s_ref[...])
        q_ref[...] = qcn.astype(BF16)
        k_ref[...] = jnp.concatenate([kvn, kr], axis=1).astype(BF16)

    return pl.pallas_call(
        body, name=name, grid=(s // ts,),
        in_specs=[_row_spec(ts, ODD_SMALL_PAD), _vec_spec(MLA_Q_RANK), _vec_spec(MLA_KV_RANK), _row_spec(ts, LANES), _row_spec(ts, LANES)],
        out_specs=[_row_spec(ts, MLA_Q_RANK), _row_spec(ts, QK_PAD)],
        out_shape=[jax.ShapeDtypeStruct((s, MLA_Q_RANK), BF16), jax.ShapeDtypeStruct((s, QK_PAD), BF16)],
        compiler_params=_params("parallel"),
    )(proj, gq, gkv, cos, sin_signed)


def _mla_prep_bwd(proj, dqcn, dkv, gq, gkv, cos, sin_signed, *, name):
    s = proj.shape[0]
    ts = _tile(s, 512)

    def body(p_ref, dq_ref, dkv_ref, gq_ref, gkv_ref, c_ref, s_ref, ds_ref, ggq_ref, ggkv_ref):
        @pl.when(pl.program_id(0) == 0)
        def _():
            ggq_ref[...] = jnp.zeros_like(ggq_ref)
            ggkv_ref[...] = jnp.zeros_like(ggkv_ref)

        qc = p_ref[:, :MLA_Q_RANK]
        kvc = p_ref[:, MLA_Q_RANK:MLA_Q_RANK + MLA_KV_RANK]
        _, rq = _rms(qc, gq_ref[...])
        _, rkv = _rms(kvc, gkv_ref[...])
        dq = dq_ref[...]
        dkvn = dkv_ref[:, :MLA_KV_RANK]
        ggq_ref[...] += _colsum(dq * qc * rq)
        ggkv_ref[...] += _colsum(dkvn * kvc * rkv)
        dkr = _rope_bwd(dkv_ref[:, MLA_KV_RANK:], c_ref[...], s_ref[...])
        ds_ref[...] = jnp.concatenate(
            [_rms_bwd(dq, qc, rq, gq_ref[...]), _rms_bwd(dkvn, kvc, rkv, gkv_ref[...]), dkr], axis=1).astype(BF16)

    return pl.pallas_call(
        body, name=name, grid=(s // ts,),
        in_specs=[_row_spec(ts, ODD_SMALL_PAD), _row_spec(ts, MLA_Q_RANK), _row_spec(ts, QK_PAD),
                  _vec_spec(MLA_Q_RANK), _vec_spec(MLA_KV_RANK), _row_spec(ts, LANES), _row_spec(ts, LANES)],
        out_specs=[_row_spec(ts, ODD_SMALL_PAD), _vec_spec(MLA_Q_RANK), _vec_spec(MLA_KV_RANK)],
        out_shape=[jax.ShapeDtypeStruct((s, ODD_SMALL_PAD), BF16), jax.ShapeDtypeStruct((1, MLA_Q_RANK), F32),
                   jax.ShapeDtypeStruct((1, MLA_KV_RANK), F32)],
        compiler_params=_params("arbitrary"),
    )(proj, dqcn, dkv, gq, gkv, cos, sin_signed)


LOG2_E = 1.4426950408889634
Q_PRESCALE = ATTN_SCALE * LOG2_E


def _q_build(q_nope, q_rope_pre, wuk_hdr, cos, sin_signed, *, name):
    s = q_nope.shape[0]
    ts = _tile(s, 1024)

    def body(qn_ref, qr_ref, w_ref, c_ref, s_ref, o_ref):
        r = _rope(qr_ref[...], c_ref[...], s_ref[...])
        lane = lax.broadcasted_iota(jnp.int32, (ts, LANES), 1)
        for j in range(2):
            ql = _dot(qn_ref[:, j * MLA_NOPE:(j + 1) * MLA_NOPE], w_ref[j], NN)
            rr = r if j == 0 else pltpu.roll(r, MLA_ROPE, 1)
            o_ref[j] = (jnp.concatenate([ql, jnp.where(lane < MLA_ROPE, rr, 0.0)], axis=1) * Q_PRESCALE).astype(BF16)

    return pl.pallas_call(
        body, name=name, grid=(s // ts, MLA_HEADS // 2),
        in_specs=[pl.BlockSpec((ts, 2 * MLA_NOPE), lambda i, p: (i, p)), pl.BlockSpec((ts, LANES), lambda i, p: (i, p)),
                  pl.BlockSpec((2, MLA_NOPE, MLA_KV_RANK), lambda i, p: (p, 0, 0)),
                  pl.BlockSpec((ts, LANES), lambda i, p: (i, 0)), pl.BlockSpec((ts, LANES), lambda i, p: (i, 0))],
        out_specs=pl.BlockSpec((2, ts, QK_PAD), lambda i, p: (p, i, 0)),
        out_shape=jax.ShapeDtypeStruct((MLA_HEADS, s, QK_PAD), BF16),
        compiler_params=_params("parallel", "parallel"),
    )(q_nope, q_rope_pre, wuk_hdr, cos, sin_signed)


def _q_bwd(dq, q_nope, wuk_hrd, cos, sin_signed, *, name):
    s = q_nope.shape[0]
    ts = _tile(s, 1024)

    def body(dq_ref, qn_ref, w_ref, c_ref, s_ref, dn_ref, dr_ref, gw_ref):
        @pl.when(pl.program_id(1) == 0)
        def _():
            gw_ref[...] = jnp.zeros_like(gw_ref)

        lane = lax.broadcasted_iota(jnp.int32, (ts, LANES), 1)
        for j in range(2):
            dql = dq_ref[j, :, :MLA_KV_RANK]
            dn_ref[:, j * MLA_NOPE:(j + 1) * MLA_NOPE] = _dot(dql, w_ref[j], NN).astype(BF16)
            gw_ref[j] += _dot(dql, qn_ref[:, j * MLA_NOPE:(j + 1) * MLA_NOPE], TN)
        hi0 = dq_ref[0, :, MLA_KV_RANK:].astype(F32)
        hi1 = dq_ref[1, :, MLA_KV_RANK:].astype(F32)
        d = jnp.where(lane < MLA_ROPE, hi0, pltpu.roll(hi1, MLA_ROPE, 1))
        dr_ref[...] = _rope_bwd(d, c_ref[...], s_ref[...]).astype(BF16)

    return pl.pallas_call(
        body, name=name, grid=(MLA_HEADS // 2, s // ts),
        in_specs=[pl.BlockSpec((2, ts, QK_PAD), lambda p, i: (p, i, 0)), pl.BlockSpec((ts, 2 * MLA_NOPE), lambda p, i: (i, p)),
                  pl.BlockSpec((2, MLA_KV_RANK, MLA_NOPE), lambda p, i: (p, 0, 0)),
                  pl.BlockSpec((ts, LANES), lambda p, i: (i, 0)), pl.BlockSpec((ts, LANES), lambda p, i: (i, 0))],
        out_specs=[pl.BlockSpec((ts, 2 * MLA_NOPE), lambda p, i: (i, p)), pl.BlockSpec((ts, LANES), lambda p, i: (i, p)),
                   pl.BlockSpec((2, MLA_KV_RANK, MLA_NOPE), lambda p, i: (p, 0, 0))],
        out_shape=[jax.ShapeDtypeStruct((s, MLA_HEADS * MLA_NOPE), BF16), jax.ShapeDtypeStruct((s, MLA_HEADS * MLA_ROPE), BF16),
                   jax.ShapeDtypeStruct((MLA_HEADS, MLA_KV_RANK, MLA_NOPE), F32)],
        compiler_params=_params("parallel", "arbitrary"),
    )(dq, q_nope, wuk_hrd, cos, sin_signed)


ATTN_BQ = 128
ATTN_BK = 512


def _diag_mask(rows, bq, bk, q0, k0):
    qc = (q0 + lax.broadcasted_iota(jnp.int32, (rows, bk), 0) % bq) // CHUNK
    kc = (k0 + lax.broadcasted_iota(jnp.int32, (rows, bk), 1)) // CHUNK
    return kc <= qc


def _attn_fwd(q, k, *, name):
    nh, s, dk = q.shape
    bq, bk = _tile(s, ATTN_BQ), _tile(s, ATTN_BK)
    rows = nh * bq

    def body(q_ref, k_ref, o_ref, lse_ref):
        i = pl.program_id(0)
        qb = q_ref[...].reshape(rows, dk)
        n_before = (i * bq) // bk

        def step(j, carry, masked):
            m, l, acc = carry
            k0 = pl.multiple_of(j * bk, bk)
            kb = k_ref[pl.ds(k0, bk), :]
            sc = _dot(qb, kb, NT)
            if masked:
                sc = jnp.where(_diag_mask(rows, bq, bk, i * bq, k0), sc, NEG)
            m_new = jnp.maximum(m, jnp.max(sc, axis=1, keepdims=True))
            p = jnp.exp2(sc - m_new)
            a = jnp.exp2(m - m_new)
            l = a * l + jnp.sum(p, axis=1, keepdims=True)
            acc = a * acc + _dot(p.astype(BF16), kb[:, :MLA_KV_RANK], NN)
            return m_new, l, acc

        init = (jnp.full((rows, 1), NEG, F32), jnp.zeros((rows, 1), F32), jnp.zeros((rows, MLA_KV_RANK), F32))
        carry = lax.fori_loop(0, n_before, lambda j, c: step(j, c, False), init)
        m, l, acc = step(n_before, carry, True)
        o_ref[...] = (acc / l).astype(BF16).reshape(nh, bq, MLA_KV_RANK)
        lse_ref[...] = jnp.broadcast_to(m + jnp.log2(l), (rows, LANES)).reshape(nh, bq, LANES)

    return pl.pallas_call(
        body, name=name, grid=(s // bq,),
        in_specs=[pl.BlockSpec((nh, bq, dk), lambda i: (0, i, 0)), pl.BlockSpec((s, dk), lambda i: (0, 0))],
        out_specs=[pl.BlockSpec((nh, bq, MLA_KV_RANK), lambda i: (0, i, 0)), pl.BlockSpec((nh, bq, LANES), lambda i: (0, i, 0))],
        out_shape=[jax.ShapeDtypeStruct((nh, s, MLA_KV_RANK), BF16), jax.ShapeDtypeStruct((nh, s, LANES), F32)],
        compiler_params=_params("parallel"),
    )(q, k)


def _attn_bwd(q, k, do, o, lse, *, name):
    nh, s, dk = q.shape
    bq, bk = _tile(s, ATTN_BQ), _tile(s, ATTN_BK)
    rows = nh * bq

    def body(q_ref, k_ref, do_ref, o_ref, lse_ref, dq_ref, dkv_ref):
        i = pl.program_id(0)
        n_before = (i * bq) // bk

        @pl.when(i == 0)
        def _():
            dkv_ref[...] = jnp.zeros_like(dkv_ref)

        qb = q_ref[...].reshape(rows, dk)
        dob = do_ref[...].reshape(rows, MLA_KV_RANK)
        lse_b = lse_ref[...].reshape(rows, LANES)[:, :1]
        delta = jnp.sum(dob.astype(F32) * o_ref[...].reshape(rows, MLA_KV_RANK).astype(F32), axis=1, keepdims=True)

        def step(j, dq, masked):
            j0 = pl.multiple_of(j * bk, bk)
            kb = k_ref[pl.ds(j0, bk), :]
            sc = _dot(qb, kb, NT)
            if masked:
                sc = jnp.where(_diag_mask(rows, bq, bk, i * bq, j0), sc, NEG)
            p = jnp.exp2(sc - lse_b)
            dp = _dot(dob, kb[:, :MLA_KV_RANK], NT)
            ds_bf = (p * (dp - delta)).astype(BF16)
            dkv_ref[pl.ds(j0, bk), :] += _dot(ds_bf, qb, TN) * (1.0 / LOG2_E)
            dkv_ref[pl.ds(j0, bk), :MLA_KV_RANK] += _dot(p.astype(BF16), dob, TN)
            return dq + _dot(ds_bf, kb, NN)

        dq = lax.fori_loop(0, n_before, lambda j, c: step(j, c, False), jnp.zeros((rows, dk), F32))
        dq = step(n_before, dq, True) * ATTN_SCALE
        dq_ref[...] = dq.astype(BF16).reshape(nh, bq, dk)

    blk = lambda w: pl.BlockSpec((nh, bq, w), lambda i: (0, i, 0))
    return pl.pallas_call(
        body, name=name, grid=(s // bq,),
        in_specs=[blk(dk), pl.BlockSpec((s, dk), lambda i: (0, 0)), blk(MLA_KV_RANK), blk(MLA_KV_RANK), blk(LANES)],
        out_specs=[blk(dk), pl.BlockSpec((s, dk), lambda i: (0, 0))],
        out_shape=[jax.ShapeDtypeStruct((nh, s, dk), BF16), jax.ShapeDtypeStruct((s, dk), F32)],
        compiler_params=_params("arbitrary"),
    )(q, k, do, o, lse)


HEAD_GROUP = 4


def _o_build(o_lat, wuv_hrv, proj, *, name):
    s = proj.shape[0]
    ts = _tile(s, 1024)
    w = HEAD_GROUP * MLA_V

    def body(ol_ref, w_ref, z_ref, og_ref):
        for j in range(HEAD_GROUP):
            cs = slice(j * MLA_V, (j + 1) * MLA_V)
            z = z_ref[:, cs]
            og_ref[:, cs] = (_dot(ol_ref[j], w_ref[j], NN) * (z * _sigmoid(z))).astype(BF16)

    return pl.pallas_call(
        body, name=name, grid=(s // ts, MLA_HEADS // HEAD_GROUP),
        in_specs=[pl.BlockSpec((HEAD_GROUP, ts, MLA_KV_RANK), lambda i, g: (g, i, 0)),
                  pl.BlockSpec((HEAD_GROUP, MLA_KV_RANK, MLA_V), lambda i, g: (g, 0, 0)),
                  pl.BlockSpec((ts, w), lambda i, g: (i, g + 1))],
        out_specs=pl.BlockSpec((ts, w), lambda i, g: (i, g)),
        out_shape=jax.ShapeDtypeStruct((s, MLA_WIDTH), BF16),
        compiler_params=_params("parallel", "parallel"),
    )(o_lat, wuv_hrv, proj)


def _o_bwd(dg, proj, o_lat, wuv_hrv, wuv_hvr, *, name):
    s = proj.shape[0]
    ts = _tile(s, 1024)
    w = HEAD_GROUP * MLA_V

    def body(dg_ref, z_ref, ol_ref, w_ref, wt_ref, dol_ref, dz_ref, gw_ref):
        @pl.when(pl.program_id(1) == 0)
        def _():
            gw_ref[...] = jnp.zeros_like(gw_ref)

        for j in range(HEAD_GROUP):
            cs = slice(j * MLA_V, (j + 1) * MLA_V)
            z, dgj, ol = z_ref[:, cs], dg_ref[:, cs], ol_ref[j]
            sg = _sigmoid(z)
            o = _dot(ol, w_ref[j], NN)
            dz_ref[:, cs] = (dgj * o * (sg * (1.0 + z * (1.0 - sg)))).astype(BF16)
            do_bf = (dgj * (z * sg)).astype(BF16)
            dol_ref[j] = _dot(do_bf, wt_ref[j], NN).astype(BF16)
            gw_ref[j] += _dot(ol, do_bf, TN)

    hs = lambda a, b: pl.BlockSpec((HEAD_GROUP, a, b), lambda g, i: (g, 0, 0))
    return pl.pallas_call(
        body, name=name, grid=(MLA_HEADS // HEAD_GROUP, s // ts),
        in_specs=[pl.BlockSpec((ts, w), lambda g, i: (i, g)), pl.BlockSpec((ts, w), lambda g, i: (i, g + 1)),
                  pl.BlockSpec((HEAD_GROUP, ts, MLA_KV_RANK), lambda g, i: (g, i, 0)),
                  hs(MLA_KV_RANK, MLA_V), hs(MLA_V, MLA_KV_RANK)],
        out_specs=[pl.BlockSpec((HEAD_GROUP, ts, MLA_KV_RANK), lambda g, i: (g, i, 0)),
                   pl.BlockSpec((ts, w), lambda g, i: (i, g)), hs(MLA_KV_RANK, MLA_V)],
        out_shape=[jax.ShapeDtypeStruct((MLA_HEADS, s, MLA_KV_RANK), BF16), jax.ShapeDtypeStruct((s, MLA_WIDTH), BF16),
                   jax.ShapeDtypeStruct((MLA_HEADS, MLA_KV_RANK, MLA_V), F32)],
        compiler_params=_params("parallel", "arbitrary"),
    )(dg, proj, o_lat, wuv_hrv, wuv_hvr)


def _ada_mod(c_all, ada_w, ada_b_sh, *, name):
    nl, _, cols = ada_w.shape

    def body(c_ref, w_ref, b_ref, o_ref):
        c = c_ref[...]
        cond = (c * _sigmoid(c)).astype(BF16)
        for l in range(nl):
            o_ref[l] = _dot(cond, w_ref[l].astype(BF16), NN) + b_ref[l]

    return pl.pallas_call(
        body, name=name, out_shape=jax.ShapeDtypeStruct((nl, c_all.shape[0], cols), F32),
        compiler_params=_params(),
    )(c_all, ada_w, ada_b_sh)


def _ada_grad(c_all_t, dmod_sh, *, name):
    nl, _, cols = dmod_sh.shape
    d = c_all_t.shape[0]

    def body(c_ref, dm_ref, gw_ref):
        c = c_ref[...]
        cond_t = c * _sigmoid(c)
        for l in range(nl):
            gw_ref[l] = lax.dot_general(cond_t, dm_ref[l], (NN, ((), ())), precision=lax.Precision.HIGHEST,
                                        preferred_element_type=F32)

    return pl.pallas_call(
        body, name=name, out_shape=jax.ShapeDtypeStruct((nl, d, cols), F32), compiler_params=_params(),
    )(c_all_t, dmod_sh)


def _sum_devices(parts, *, name):
    def body(p_ref, o_ref):
        acc = p_ref[0]
        for k in range(1, parts.shape[0]):
            acc = acc + p_ref[k]
        o_ref[...] = acc

    return pl.pallas_call(body, name=name, out_shape=jax.ShapeDtypeStruct(parts.shape[1:], F32), compiler_params=_params())(parts)


def _adamw_math(w, g, m, v):
    c1 = 1.0 - ADAM_B1 ** ADAM_STEP
    c2 = 1.0 - ADAM_B2 ** ADAM_STEP
    nm = ADAM_B1 * m + (1.0 - ADAM_B1) * g
    nv = ADAM_B2 * v + (1.0 - ADAM_B2) * (g * g)
    return -ADAM_LR * ((nm / c1) / (jnp.sqrt(nv / c2) + ADAM_EPS) + ADAM_WD * w), nm, nv


ADAMW_BLOCK_BYTES = 1 << 20


def _adamw(w, g, m, v, *, name, after=None):
    shape = w.shape
    a, b = shape[-2], shape[-1]
    lead = 1
    for dim in shape[:-2]:
        lead *= dim
    row_bytes = 4 * b
    if a * row_bytes <= ADAMW_BLOCK_BYTES:
        ta = a
        tl = max(1, min(lead, ADAMW_BLOCK_BYTES // (a * row_bytes)))
        while lead % tl:
            tl -= 1
    else:
        tl = 1
        ta = _tile(a, 256)
    to3 = lambda t: t.reshape(lead, a, b)

    def body(w_ref, g_ref, m_ref, v_ref, *rest):
        d_ref, nm_ref, nv_ref = rest[-3:]
        d_ref[...], nm_ref[...], nv_ref[...] = _adamw_math(w_ref[...], g_ref[...], m_ref[...], v_ref[...])

    spec = pl.BlockSpec((tl, ta, b), lambda i, j: (i, j, 0))
    out = jax.ShapeDtypeStruct((lead, a, b), F32)
    order = [] if after is None else [after]
    res = pl.pallas_call(
        body, name=name, grid=(lead // tl, a // ta), in_specs=[spec] * 4 + [pl.BlockSpec(memory_space=pl.ANY)] * len(order),
        out_specs=[spec] * 3, out_shape=[out] * 3, compiler_params=_params("parallel", "parallel"),
    )(to3(w), to3(g), to3(m), to3(v), *order)
    return [r.reshape(shape) for r in res]


def _adamw_small(ws, gs, ms, vs, *, name):
    n = len(ws)

    def body(*refs):
        for k in range(n):
            w_ref, g_ref, m_ref, v_ref = (refs[j * n + k] for j in range(4))
            d_ref, nm_ref, nv_ref = (refs[(4 + j) * n + k] for j in range(3))
            d_ref[...], nm_ref[...], nv_ref[...] = _adamw_math(w_ref[...], g_ref[...], m_ref[...], v_ref[...])

    outs = [jax.ShapeDtypeStruct(w.shape, F32) for w in ws]
    res = pl.pallas_call(body, name=name, out_shape=outs * 3, compiler_params=_params())(*ws, *gs, *ms, *vs)
    return res[:n], res[n:2 * n], res[2 * n:]


def _flip(v, bit):
    return 1 - v if bit else v


CHIP_DELTAS = ((1, 0), (0, 1), (1, 1))
SUM_ROWS = 32


def _all_gather_chips(shard, *, name):
    def body(x_ref, o_ref, send_sems, recv_sems, local_sem):
        x, y, c = lax.axis_index("x"), lax.axis_index("y"), lax.axis_index("c")
        mine = pltpu.make_async_copy(x_ref, o_ref.at[2 * x + y], local_sem)
        mine.start()

        def copy(k):
            tx, ty = _flip(x, CHIP_DELTAS[k][0]), _flip(y, CHIP_DELTAS[k][1])
            send = pltpu.make_async_remote_copy(src_ref=x_ref, dst_ref=o_ref.at[2 * x + y], send_sem=send_sems.at[k],
                                                recv_sem=recv_sems.at[k], device_id=(tx, ty, c), device_id_type=MESH)
            recv = pltpu.make_async_remote_copy(src_ref=x_ref, dst_ref=o_ref.at[2 * tx + ty], send_sem=send_sems.at[k],
                                                recv_sem=recv_sems.at[k], device_id=(tx, ty, c), device_id_type=MESH)
            return send, recv

        pairs = [copy(k) for k in range(3)]
        for send, _ in pairs:
            send.start()
        for _, recv in pairs:
            recv.wait_recv()
        for send, _ in pairs:
            send.wait_send()
        mine.wait()

    return pl.pallas_call(
        body, name=name, out_shape=jax.ShapeDtypeStruct((N_CHIPS,) + shard.shape, shard.dtype),
        in_specs=[HBM], out_specs=HBM,
        scratch_shapes=[pltpu.SemaphoreType.DMA((3,)), pltpu.SemaphoreType.DMA((3,)), pltpu.SemaphoreType.DMA(())],
    )(shard)


def _gather_weights(shards, *, name):
    n = len(shards)

    def body(*refs):
        w_refs, o_refs = refs[:n], refs[n:2 * n]
        ici_send, ici_recv, d2d_send, d2d_recv, local_sems = refs[2 * n:]
        x, y, c = lax.axis_index("x"), lax.axis_index("y"), lax.axis_index("c")
        me = 2 * x + y
        peers = [(_flip(x, dx), _flip(y, dy)) for dx, dy in CHIP_DELTAS]
        locals_ = [pltpu.make_async_copy(w_refs[k], o_refs[k].at[me], local_sems.at[k]) for k in range(n)]
        for cp in locals_:
            cp.start()

        def rows(k, which):
            half = shards[k].shape[0] // 2
            return pl.ds(pl.multiple_of(which * half, half), half)

        def over_chips(k, d, slot):
            tx, ty = peers[d]
            return pltpu.make_async_remote_copy(
                src_ref=w_refs[k].at[rows(k, c)], dst_ref=o_refs[k].at[slot, rows(k, c)], send_sem=ici_send.at[k, d],
                recv_sem=ici_recv.at[k, d], device_id=(tx, ty, c), device_id_type=MESH)

        def to_sibling(k, d, which):
            tx, ty = peers[d]
            at = o_refs[k].at[2 * tx + ty, rows(k, which)]
            return pltpu.make_async_remote_copy(src_ref=at, dst_ref=at, send_sem=d2d_send.at[k, d], recv_sem=d2d_recv.at[k, d],
                                                device_id=(x, y, 1 - c), device_id_type=MESH)

        sends = [over_chips(k, d, me) for k in range(n) for d in range(3)]
        for cp in sends:
            cp.start()
        passed = []
        for k in range(n):
            for d in range(3):
                over_chips(k, d, 2 * peers[d][0] + peers[d][1]).wait_recv()
                passed.append(to_sibling(k, d, c))
                passed[-1].start()
        for k in range(n):
            for d in range(3):
                to_sibling(k, d, 1 - c).wait_recv()
        for cp in sends + passed:
            cp.wait_send()
        for cp in locals_:
            cp.wait()

    return pl.pallas_call(
        body, name=name, out_shape=[jax.ShapeDtypeStruct((N_CHIPS,) + w.shape, w.dtype) for w in shards],
        in_specs=[HBM] * n, out_specs=[HBM] * n,
        scratch_shapes=[pltpu.SemaphoreType.DMA((n, 3))] * 4 + [pltpu.SemaphoreType.DMA((n,))],
    )(*shards)


def _add_into(dst_ref, src_ref):
    ns, r, _ = dst_ref.shape
    step = SUM_ROWS if r % SUM_ROWS == 0 else r
    for s in range(ns):
        def tile(t, carry):
            at = pl.ds(pl.multiple_of(t * step, step), step)
            dst_ref[s, at, :] = (dst_ref[s, at, :].astype(F32) + src_ref[s, at, :].astype(F32)).astype(dst_ref.dtype)
            return carry
        lax.fori_loop(0, r // step, tile, 0)


def _reduce_sibling(grads, *, name):
    n = len(grads)

    def body(*refs):
        g_refs, o_refs = refs[:n], refs[n:2 * n]
        mine, got = refs[2 * n:3 * n], refs[3 * n:4 * n]
        send_sems, recv_sems, load_sems, store_sems = refs[4 * n:]
        x, y, c = lax.axis_index("x"), lax.axis_index("y"), lax.axis_index("c")
        loads = [pltpu.make_async_copy(g_refs[k].at[:, c], mine[k], load_sems.at[k]) for k in range(n)]
        swaps = [pltpu.make_async_remote_copy(src_ref=g_refs[k].at[:, 1 - c], dst_ref=got[k], send_sem=send_sems.at[k],
                                              recv_sem=recv_sems.at[k], device_id=(x, y, 1 - c), device_id_type=MESH)
                 for k in range(n)]
        for cp in loads + swaps:
            cp.start()
        stores = []
        for k in range(n):
            loads[k].wait()
            swaps[k].wait_recv()
            _add_into(mine[k], got[k])
            stores.append(pltpu.make_async_copy(mine[k], o_refs[k], store_sems.at[k]))
            stores[-1].start()
        for k in range(n):
            swaps[k].wait_send()
            stores[k].wait()

    half = [jax.ShapeDtypeStruct((g.shape[0],) + g.shape[2:], g.dtype) for g in grads]
    return pl.pallas_call(
        body, name=name, out_shape=half, in_specs=[HBM] * n, out_specs=[HBM] * n,
        scratch_shapes=[pltpu.VMEM(h.shape, h.dtype) for h in half] * 2 + [pltpu.SemaphoreType.DMA((n,))] * 4,
        compiler_params=_params(),
    )(*grads)


def _reduce_chips(parts, landed, *, name):
    n_send = len(parts)
    n = n_send + len(landed)

    def body(*refs):
        p_refs, o_refs = refs[:n], refs[n:2 * n]
        got, total = refs[2 * n:3 * n], refs[3 * n:4 * n]
        send_sems, recv_sems, load_sems, share_send, share_recv, store_sems = refs[4 * n:]
        x, y, c = lax.axis_index("x"), lax.axis_index("y"), lax.axis_index("c")
        me = 2 * x + y
        peers = [(_flip(x, dx), _flip(y, dy)) for dx, dy in CHIP_DELTAS]

        def over_chips(k, d, src_slot, dst_slot):
            tx, ty = peers[d]
            return pltpu.make_async_remote_copy(
                src_ref=p_refs[k].at[src_slot], dst_ref=got[k].at[dst_slot], send_sem=send_sems.at[k, d],
                recv_sem=recv_sems.at[k, d], device_id=(tx, ty, c), device_id_type=MESH)

        loads = [pltpu.make_async_copy(p_refs[k].at[me], got[k].at[me], load_sems.at[k]) for k in range(n_send)]
        loads += [pltpu.make_async_copy(p_refs[k], got[k], load_sems.at[k]) for k in range(n_send, n)]
        sends = [over_chips(k, d, 2 * peers[d][0] + peers[d][1], me) for k in range(n_send) for d in range(3)]
        for cp in loads + sends:
            cp.start()
        shares, stores = [], []
        for k in range(n):
            loads[k].wait()
            for d in range(3 if k < n_send else 0):
                slot = 2 * peers[d][0] + peers[d][1]
                over_chips(k, d, slot, slot).wait_recv()
            r = total[k].shape[0]
            step = SUM_ROWS if r % SUM_ROWS == 0 else r

            def tile(t, carry, k=k, step=step):
                at = pl.ds(pl.multiple_of(t * step, step), step)
                acc = got[k][0, at, :].astype(F32)
                for s in range(1, N_CHIPS):
                    acc = acc + got[k][s, at, :].astype(F32)
                total[k][at, :] = acc
                return carry

            lax.fori_loop(0, r // step, tile, 0)
            stores.append(pltpu.make_async_copy(total[k], o_refs[k].at[c], store_sems.at[k]))
            shares.append(pltpu.make_async_remote_copy(
                src_ref=total[k], dst_ref=o_refs[k].at[c], send_sem=share_send.at[k], recv_sem=share_recv.at[k],
                device_id=(x, y, 1 - c), device_id_type=MESH))
            stores[-1].start()
            shares[-1].start()
        for k in range(n):
            pltpu.make_async_remote_copy(
                src_ref=total[k], dst_ref=o_refs[k].at[1 - c], send_sem=share_send.at[k], recv_sem=share_recv.at[k],
                device_id=(x, y, 1 - c), device_id_type=MESH).wait_recv()
        for cp in sends + shares:
            cp.wait_send()
        for cp in stores:
            cp.wait()

    both = list(parts) + list(landed)
    return pl.pallas_call(
        body, name=name, out_shape=[jax.ShapeDtypeStruct((2,) + p.shape[1:], F32) for p in both],
        in_specs=[HBM] * n, out_specs=[HBM] * n,
        scratch_shapes=[pltpu.VMEM(p.shape, p.dtype) for p in both] + [pltpu.VMEM(p.shape[1:], F32) for p in both]
        + [pltpu.SemaphoreType.DMA((n, 3))] * 2 + [pltpu.SemaphoreType.DMA((n,))] * 4,
        compiler_params=_params(),
    )(*both)


SEM = pl.BlockSpec(memory_space=pltpu.SEMAPHORE)
IN_FLIGHT = pltpu.SideEffectType.DATAFLOW_SIDE_EFFECTING


def _chip_copies(s_refs, l_refs, sems, scatter, theirs):
    x, y, c = lax.axis_index("x"), lax.axis_index("y"), lax.axis_index("c")
    me = 2 * x + y
    copies = []
    for k in range(len(s_refs)):
        for d, (dx, dy) in enumerate(CHIP_DELTAS):
            tx, ty = _flip(x, dx), _flip(y, dy)
            peer = 2 * tx + ty
            send_sem, recv_sem = sems[2 * (3 * k + d)], sems[2 * (3 * k + d) + 1]
            copies.append(pltpu.make_async_remote_copy(
                src_ref=s_refs[k].at[peer] if scatter else s_refs[k], dst_ref=l_refs[k].at[peer if theirs else me],
                send_sem=send_sem, recv_sem=recv_sem, device_id=(tx, ty, c), device_id_type=MESH))
    return copies


def _chips_start(srcs, lands, after, *, scatter, name):
    n = len(srcs)
    n_sem = 2 * 3 * n

    def body(*refs):
        s_refs, l_refs = refs[:n], refs[n:2 * n]
        sems = refs[2 * n + 1:2 * n + 1 + n_sem]
        token = refs[-1]
        for cp in _chip_copies(s_refs, l_refs, sems, scatter, False):
            cp.start()
        token[...] = jnp.zeros_like(token)

    hbm = lambda a: pltpu.HBM(a.shape, a.dtype)
    res = pl.pallas_call(
        body, name=name,
        out_shape=(*[pltpu.SemaphoreType.DMA(())] * n_sem, *[hbm(a) for a in srcs], *[hbm(a) for a in lands],
                   jax.ShapeDtypeStruct((8, LANES), F32)),
        in_specs=[HBM] * (2 * n) + [pl.BlockSpec(memory_space=pl.ANY)],
        out_specs=(*[SEM] * n_sem, *[HBM] * (2 * n), VMEM),
        input_output_aliases={k: n_sem + k for k in range(2 * n)},
        compiler_params=pltpu.CompilerParams(has_side_effects=IN_FLIGHT),
    )(*[pltpu.with_memory_space_constraint(a, pltpu.HBM) for a in list(srcs) + list(lands)], after)
    return res[:n_sem], res[n_sem:n_sem + n], res[n_sem + n:n_sem + 2 * n], res[-1]


def _chips_wait(sems, srcs, lands, after, *, scatter, name):
    n = len(srcs)
    n_sem = len(sems)

    def body(*refs):
        s_refs, l_refs = refs[:n], refs[n:2 * n]
        sem_refs = refs[2 * n:2 * n + n_sem]
        for cp in _chip_copies(s_refs, l_refs, sem_refs, scatter, False):
            cp.wait_send()
        for cp in _chip_copies(s_refs, l_refs, sem_refs, scatter, True):
            cp.wait_recv()

    hbm = lambda a: pltpu.HBM(a.shape, a.dtype)
    res = pl.pallas_call(
        body, name=name, out_shape=tuple(hbm(a) for a in list(srcs) + list(lands)),
        in_specs=[HBM] * (2 * n) + [SEM] * n_sem + [pl.BlockSpec(memory_space=pl.ANY)], out_specs=tuple([HBM] * (2 * n)),
        input_output_aliases={k: k for k in range(2 * n)},
        compiler_params=pltpu.CompilerParams(has_side_effects=IN_FLIGHT),
    )(*srcs, *lands, *sems, after)
    return res[n:]


def _all_gather_devices(rows, *, name):
    deltas = [(dx, dy, dc) for dx in (0, 1) for dy in (0, 1) for dc in (0, 1)][1:]

    def body(x_ref, o_ref, send_sems, recv_sems):
        x, y, c = lax.axis_index("x"), lax.axis_index("y"), lax.axis_index("c")
        me = 4 * x + 2 * y + c
        o_ref[me] = x_ref[...]
        sends, recvs = [], []
        for k, (dx, dy, dc) in enumerate(deltas):
            tx, ty, tc = _flip(x, dx), _flip(y, dy), _flip(c, dc)
            sends.append(pltpu.make_async_remote_copy(src_ref=x_ref, dst_ref=o_ref.at[me], send_sem=send_sems.at[k],
                                                      recv_sem=recv_sems.at[k], device_id=(tx, ty, tc), device_id_type=MESH))
            recvs.append(pltpu.make_async_remote_copy(src_ref=x_ref, dst_ref=o_ref.at[4 * tx + 2 * ty + tc],
                                                      send_sem=send_sems.at[k], recv_sem=recv_sems.at[k],
                                                      device_id=(tx, ty, tc), device_id_type=MESH))
        for cp in sends:
            cp.start()
        for cp in recvs:
            cp.wait_recv()
        for cp in sends:
            cp.wait_send()

    return pl.pallas_call(
        body, name=name, out_shape=jax.ShapeDtypeStruct((N_DEV,) + rows.shape, rows.dtype),
        in_specs=[VMEM], out_specs=VMEM,
        scratch_shapes=[pltpu.SemaphoreType.DMA((N_DEV - 1,)), pltpu.SemaphoreType.DMA((N_DEV - 1,))],
    )(rows)


WEIGHTS = ("ada_w", "ada_b", "ln_g", "ln_b", "e_w_in", "gmlp_norm_g", "gmlp_norm_b", "gmlp_ws", "gmlp_bs", "pool_w",
           "pool_b", "pool_scale", "e_w_out", "o_w_in", "mla_q_norm_g", "mla_kv_norm_g", "mla_w_uq", "mla_w_uk",
           "mla_w_uv", "o_w_out")
SMALL = ("ln_g", "ln_b", "gmlp_norm_g", "gmlp_norm_b", "gmlp_bs", "pool_b", "pool_scale", "mla_kv_norm_g", "mla_q_norm_g")


def _pad_cols(v, n):
    return jnp.concatenate([v, jnp.zeros((v.shape[0], n - v.shape[1]), v.dtype)], axis=1) if n > v.shape[1] else v


def _halves(g):
    return g.reshape(g.shape[0], 2, g.shape[1] // 2, g.shape[2])


def kernel(x, c, positions, ada_w, ada_b, ln_g, ln_b, e_w_in, gmlp_norm_g, gmlp_norm_b, gmlp_ws, gmlp_bs, pool_w, pool_b, pool_scale, e_w_out, o_w_in, mla_q_norm_g, mla_kv_norm_g, mla_w_uq, mla_w_uk, mla_w_uv, o_w_out, loss_target, m_ada_w, m_ada_b, m_ln_g, m_ln_b, m_e_w_in, m_gmlp_norm_g, m_gmlp_norm_b, m_gmlp_ws, m_gmlp_bs, m_pool_w, m_pool_b, m_pool_scale, m_e_w_out, m_o_w_in, m_mla_q_norm_g, m_mla_kv_norm_g, m_mla_w_uq, m_mla_w_uk, m_mla_w_uv, m_o_w_out, v_ada_w, v_ada_b, v_ln_g, v_ln_b, v_e_w_in, v_gmlp_norm_g, v_gmlp_norm_b, v_gmlp_ws, v_gmlp_bs, v_pool_w, v_pool_b, v_pool_scale, v_e_w_out, v_o_w_in, v_mla_q_norm_g, v_mla_kv_norm_g, v_mla_w_uq, v_mla_w_uk, v_mla_w_uv, v_o_w_out):
    args = dict(locals())
    weights = {n: args[n] for n in WEIGHTS}
    mom = {n: args["m_" + n] for n in WEIGHTS}
    var = {n: args["v_" + n] for n in WEIGHTS}
    ax, ay, ac = lax.axis_index("x"), lax.axis_index("y"), lax.axis_index("c")
    chip = 2 * ax + ay
    dev = 2 * chip + ac
    d = D_MODEL
    x2 = x[0]
    target = loss_target[0]
    q_rank_sh = mla_q_norm_g.shape[1]

    shards0 = [w.astype(BF16) for w in (e_w_in[0], pool_w[0].reshape(-1, POOL_GROUP_DIM), e_w_out[0])]
    shards1 = [w.astype(BF16) for w in (o_w_in[0], mla_w_uq[0].reshape(q_rank_sh, -1), o_w_out[0])]
    w_in0, pool_w_g, w_out0 = _gather_weights(shards0, name="gather_weights")
    lands1 = [lax.dynamic_update_slice(lax.empty((N_CHIPS,) + w.shape, BF16), w[None], (chip, 0, 0)) for w in shards1]
    flight1 = _chips_start(shards1, lands1, w_in0, scatter=False, name="gather1_start")
    pool_w_bf = jnp.transpose(pool_w_g.reshape(N_CHIPS, POOL_GROUPS, -1, POOL_GROUP_DIM), (1, 0, 2, 3)).reshape(
        POOL_GROUPS, POOL_GROUP_DIM, POOL_GROUP_DIM)
    w_out0 = w_out0.reshape(-1, d)
    wuk_hrd = jnp.transpose(mla_w_uk[0], (1, 0, 2)).astype(BF16)
    wuk_hdr = jnp.transpose(mla_w_uk[0], (1, 2, 0)).astype(BF16)
    wuv_hrv = jnp.transpose(mla_w_uv[0], (1, 0, 2)).astype(BF16)
    wuv_hvr = jnp.transpose(mla_w_uv[0], (1, 2, 0)).astype(BF16)
    ws = gmlp_ws[0]
    ws_t = jnp.transpose(ws, (0, 2, 1))
    bs_t = _pad_cols(gmlp_bs[0].T, LANES)

    inv = 1.0 / (ROPE_THETA ** (jnp.arange(0, MLA_ROPE, 2, dtype=F32) / MLA_ROPE))
    ang = positions[0].astype(F32)[:, None] * inv
    cos_t = jnp.tile(jnp.cos(ang), (1, 4))
    sin_t = jnp.tile(jnp.concatenate([-jnp.sin(ang), jnp.sin(ang)], axis=1), (1, 2))

    c_all = _all_gather_devices(c.reshape(8, LANES), name="gather_c").reshape(N_DEV, d)
    cols = ada_w.shape[2]
    ada_b_mine = lax.dynamic_slice_in_dim(ada_b, chip * cols, cols, axis=1)[:, None, :]
    mod_sh = _ada_mod(c_all, ada_w, ada_b_mine, name="ada_mod")
    q_norm_rows = jnp.zeros((8, cols), F32).at[0, :q_rank_sh].set(mla_q_norm_g[0])
    mod_all = _all_gather_chips(jnp.concatenate([mod_sh.reshape(2 * N_DEV, cols), q_norm_rows]), name="gather_mod")
    q_norm_g = mod_all[:, 2 * N_DEV, :q_rank_sh].reshape(1, -1)
    mod_all = jnp.transpose(mod_all[:, :2 * N_DEV].reshape(N_CHIPS, 2, N_DEV, cols), (1, 2, 0, 3)).reshape(2, N_DEV, 3 * d)
    mod = lax.dynamic_index_in_dim(mod_all, dev, axis=1, keepdims=False)
    shift = [mod[l:l + 1, :d] for l in range(2)]
    scale = [mod[l:l + 1, d:2 * d] for l in range(2)]
    gate = [mod[l:l + 1, 2 * d:] for l in range(2)]

    scale[0] = scale[0] + flight1[3][:1, :1]
    h0 = _modulate(x2, scale[0], shift[0], name="modulate0")
    proj0 = _matmul(h0, w_in0, b_stacked=True, tm=1024, tn=1280, name="proj0")
    mix0 = _even_fwd(proj0, ws, bs_t, gmlp_norm_g, gmlp_norm_b, pool_w_bf, pool_b, pool_scale, name="even_fwd")
    y0 = _matmul(mix0, w_out0, name="out0")
    x1, h1 = _resid_ln_modulate(x2, y0, gate[0], ln_g[0:1], ln_b[0:1], scale[1], shift[1], name="resid_ln0")

    w_in1_g, w_uq_g, w_out1 = _chips_wait(*flight1[:3], h1, scatter=False, name="gather1_wait")
    w_out1 = w_out1.reshape(-1, d)
    w_in1 = jnp.transpose(w_in1_g, (1, 0, 2)).reshape(d, ODD_IN)
    w_in1 = jnp.concatenate([_pad_cols(w_in1[:, :ODD_SMALL], ODD_SMALL_PAD), w_in1[:, ODD_SMALL:]], axis=1)
    w_uq = w_uq_g.reshape(MLA_Q_RANK, MLA_HEADS, MLA_NOPE + MLA_ROPE)
    w_uq_nope = w_uq[:, :, :MLA_NOPE].reshape(MLA_Q_RANK, -1)
    w_uq_rope = w_uq[:, :, MLA_NOPE:].reshape(MLA_Q_RANK, -1)
    proj1 = _matmul(h1, w_in1, tm=1024, tn=1280, name="proj1")
    q_cn, keys = _mla_prep(proj1, q_norm_g, mla_kv_norm_g, cos_t, sin_t, name="mla_prep")
    q_nope = _matmul(q_cn, w_uq_nope, tm=1024, tn=2048, name="q_nope", out_dtype=BF16)
    q_rope_pre = _matmul(q_cn, w_uq_rope, tm=1024, name="q_rope")
    q = _q_build(q_nope, q_rope_pre, wuk_hdr, cos_t, sin_t, name="q_build")
    o_lat, lse = _attn_fwd(q, keys, name="attn_fwd")
    og = _o_build(o_lat, wuv_hrv, proj1, name="o_build")
    y1 = _matmul(og, w_out1, name="out1")

    dy1, dres1, g_ln_g1, g_ln_b1, dgate1, loss = _loss_ln_bwd(x1, y1, gate[1], ln_g[1:2], ln_b[1:2], target, name="loss_ln1")
    dg1 = _matmul(dy1, w_out1, trans_b=True, tn=2048, name="d_og")
    g_w_out1 = _matmul(og, dy1, trans_a=True, out_dtype=BF16, tm=1024, name="g_out1")
    do_lat, dz, g_uv = _o_bwd(dg1, proj1, o_lat, wuv_hrv, wuv_hvr, name="o_bwd")
    dq, dkeys = _attn_bwd(q, keys, do_lat, o_lat, lse, name="attn_bwd")
    dq_nope, dq_rope, g_uk = _q_bwd(dq, q_nope, wuk_hrd, cos_t, sin_t, name="q_bwd")
    dq_cn = (_matmul(dq_nope, w_uq_nope, trans_b=True, tm=1024, name="d_qcn_nope")
             + _matmul(dq_rope, w_uq_rope, trans_b=True, tm=1024, name="d_qcn_rope"))
    g_uq_nope = _matmul(q_cn, dq_nope, trans_a=True, out_dtype=BF16, tn=2048, name="g_uq_nope")
    g_uq_rope = _matmul(q_cn, dq_rope, trans_a=True, out_dtype=BF16, name="g_uq_rope")
    dsmall, g_qg, g_kvg = _mla_prep_bwd(proj1, dq_cn, dkeys, q_norm_g, mla_kv_norm_g, cos_t, sin_t, name="mla_prep_bwd")
    dproj1 = jnp.concatenate([dsmall, dz], axis=1)
    dh1 = _matmul(dproj1, w_in1, trans_b=True, name="d_h1")
    g_w_in1 = _matmul(h1, dproj1, trans_a=True, out_dtype=BF16, tm=1024, tn=1280, name="g_in1")

    g_uq = jnp.concatenate([g_uq_nope.reshape(MLA_Q_RANK, MLA_HEADS, MLA_NOPE), g_uq_rope.reshape(MLA_Q_RANK, MLA_HEADS, MLA_ROPE)], axis=2)
    g_w_in1 = jnp.concatenate([g_w_in1[:, :ODD_SMALL], g_w_in1[:, ODD_SMALL_PAD:]], axis=1)
    g_w_in1 = jnp.transpose(g_w_in1.reshape(d, N_CHIPS, -1), (1, 0, 2))
    big1 = [
        _halves(g_w_in1),
        _halves(g_uq.reshape(N_CHIPS, q_rank_sh, -1)),
        _halves(g_w_out1.reshape(N_CHIPS, -1, d)),
        _halves(g_uk.astype(BF16).reshape(N_CHIPS, -1, MLA_NOPE)),
        _halves(g_uv.astype(BF16).reshape(N_CHIPS, -1, MLA_V)),
    ]
    parts1 = _reduce_sibling(big1, name="reduce_sibling1")
    lands2 = [lax.dynamic_update_slice(lax.empty(p.shape, BF16), lax.dynamic_slice_in_dim(p, chip, 1, axis=0), (chip, 0, 0))
              for p in parts1]
    flight2 = _chips_start(parts1, lands2, loss, scatter=True, name="reduce1_start")

    gate[0] = gate[0] + flight2[3][:1, :1]
    dy0, dres0, g_ln_g0, g_ln_b0, dgate0, dscale1, dshift1 = _mid_ln_bwd(
        x2, y0, gate[0], ln_g[0:1], ln_b[0:1], dh1, dres1, scale[1], x1, name="mid_ln0")
    dmix0 = _matmul(dy0, w_out0, trans_b=True, tn=2048, name="d_mix0")
    g_w_out0 = _matmul(mix0, dy0, trans_a=True, out_dtype=BF16, tm=1024, name="g_out0")
    dproj0, g_ws, g_bs_t, g_ng, g_nb, g_pw, g_pb, g_ps = _even_bwd(
        proj0, dmix0, ws, ws_t, bs_t, gmlp_norm_g, gmlp_norm_b, pool_w_bf, pool_b, pool_scale, name="even_bwd")
    g_w_in0 = _matmul(h0, dproj0, trans_a=True, out_dtype=BF16, out_stacked=True, tm=1024, tn=1280, name="g_in0")

    g_pw = jnp.transpose(g_pw.astype(BF16).reshape(POOL_GROUPS, N_CHIPS, -1, POOL_GROUP_DIM), (1, 0, 2, 3))
    big0 = [
        _halves(g_w_in0),
        _halves(g_pw.reshape(N_CHIPS, -1, POOL_GROUP_DIM)),
        _halves(g_w_out0.reshape(N_CHIPS, -1, d)),
        _halves(g_ws.astype(BF16)),
    ]
    parts0 = _reduce_sibling(big0, name="reduce_sibling0")
    landed1 = _chips_wait(*flight2[:3], parts0[0], scatter=True, name="reduce1_wait")
    lands3 = [lax.dynamic_update_slice(lax.empty(p.shape, BF16), lax.dynamic_slice_in_dim(p, chip, 1, axis=0), (chip, 0, 0))
              for p in parts0]
    flight3 = _chips_start(parts0, lands3, landed1[0], scatter=True, name="reduce0_start")
    dh0 = _matmul(dproj0, w_in0, trans_b=True, b_stacked=True, tm=1024, after=flight3[3], name="d_h0")
    grad_x, dscale0, dshift0 = _input_bwd(x2, dh0, dres0, scale[0], name="input_bwd")

    small_local = {
        "ln_g": jnp.concatenate([g_ln_g0, g_ln_g1]), "ln_b": jnp.concatenate([g_ln_b0, g_ln_b1]),
        "gmlp_norm_g": g_ng, "gmlp_norm_b": g_nb, "gmlp_bs": g_bs_t[:, :GMLP_HEADS].T, "pool_b": g_pb, "pool_scale": g_ps,
        "mla_kv_norm_g": g_kvg, "mla_q_norm_g": g_qg,
    }
    n_mod = 2 * 3 * d
    vec = jnp.concatenate([dshift0, dscale0, dgate0, dshift1, dscale1, dgate1]
                          + [small_local[n].reshape(1, -1) for n in SMALL], axis=1)
    n_vec = vec.shape[1]
    vec = _pad_cols(vec, -(-n_vec // (8 * LANES)) * 8 * LANES).reshape(-1, LANES)
    vec_all = _all_gather_devices(vec, name="gather_small")
    vec_sum = _sum_devices(vec_all, name="sum_small").reshape(-1)
    dmod_all = vec_all.reshape(N_DEV, -1)[:, :n_mod].reshape(N_DEV, 2, 3 * d)
    dmod_sh = jnp.transpose(lax.dynamic_slice_in_dim(dmod_all, chip * cols, cols, axis=2), (1, 0, 2))
    dmod_sh = jnp.concatenate([dmod_sh, jnp.zeros((2, LANES - N_DEV, cols), F32)], axis=1)
    grads = {"ada_w": _ada_grad(_pad_cols(c_all.T, LANES), dmod_sh, name="ada_grad"), "ada_b": vec_sum[:n_mod].reshape(2, 3 * d)}
    off = n_mod
    for n in SMALL:
        sz = small_local[n].size
        grads[n] = vec_sum[off:off + sz]
        off += sz
    grads["mla_q_norm_g"] = lax.dynamic_slice_in_dim(grads["mla_q_norm_g"], chip * q_rank_sh, q_rank_sh)
    for n in SMALL:
        grads[n] = grads[n].reshape(weights[n].shape)

    landed0 = _chips_wait(*flight3[:3], grads["ada_w"], scatter=True, name="reduce0_wait")
    totals = _reduce_chips([], list(landed0) + list(landed1), name="reduce_chips")
    for n, t in zip(("e_w_in", "pool_w", "e_w_out", "gmlp_ws", "o_w_in", "mla_w_uq", "o_w_out"), totals):
        if n != "gmlp_ws":
            grads[n] = t.reshape(weights[n].shape)
    rep = jnp.concatenate([t.reshape(-1, LANES) for t in (totals[3], totals[7], totals[8])])
    rep_land = lax.dynamic_update_slice(lax.empty((N_CHIPS,) + rep.shape, F32), rep[None], (chip, 0, 0))
    flight4 = _chips_start([rep], [rep_land], totals[0], scatter=False, name="gather_rep_start")

    delta, new_m, new_v = {}, {}, {}
    replicated = ("gmlp_ws", "mla_w_uk", "mla_w_uv")
    large = [n for n in WEIGHTS if n not in SMALL and n != "ada_b"]
    for n in large:
        if n not in replicated:
            delta[n], new_m[n], new_v[n] = _adamw(weights[n], grads[n], mom[n], var[n], after=flight4[3], name="adamw_" + n)
    rep = _chips_wait(*flight4[:3], delta["e_w_in"], scatter=False, name="gather_rep_wait")[0]
    r_ws, r_uk = GMLP_BLOCK, 4 * MLA_KV_RANK
    grads["gmlp_ws"] = rep[:, :r_ws].reshape(weights["gmlp_ws"].shape)
    grads["mla_w_uk"] = jnp.transpose(rep[:, r_ws:r_ws + r_uk].reshape(MLA_HEADS, MLA_KV_RANK, MLA_NOPE), (1, 0, 2))[None]
    grads["mla_w_uv"] = jnp.transpose(rep[:, r_ws + r_uk:].reshape(MLA_HEADS, MLA_KV_RANK, MLA_V), (1, 0, 2))[None]
    for n in replicated:
        delta[n], new_m[n], new_v[n] = _adamw(weights[n], grads[n], mom[n], var[n], name="adamw_" + n)
    small = [n for n in WEIGHTS if n not in large]
    ds, ms, vs = _adamw_small([weights[n] for n in small], [grads[n] for n in small], [mom[n] for n in small],
                              [var[n] for n in small], name="adamw_small")
    for n, dn, mn, vn in zip(small, ds, ms, vs):
        delta[n], new_m[n], new_v[n] = dn, mn, vn

    loss_total = lax.psum(loss[0, 0], ("x", "y", "c"))
    return (loss_total, grad_x[None], *[grads[n] for n in WEIGHTS], *[delta[n] for n in WEIGHTS],
            *[new_m[n] for n in WEIGHTS], *[new_v[n] for n in WEIGHTS])
```

```python
import jax
import jax.numpy as jnp
from jax import lax
from jax.experimental import pallas as pl
from jax.experimental.pallas import tpu as pltpu

F32 = jnp.float32
BF16 = jnp.bfloat16
MESH = pl.DeviceIdType.MESH

D_MODEL = 1024
CHUNK = 64
LN_EPS = 1e-5
GMLP_HEADS = 4
GMLP_HEAD_DIM = 256
GMLP_BLOCK = 128
POOL_WINDOWS = (2, 4, 8, 16)
POOL_GROUPS = 4
POOL_GROUP_DIM = 256
POOL_HALO = 16
EVEN_IN = 5120
MLA_HEADS = 16
MLA_NOPE = 128
MLA_ROPE = 64
MLA_V = 128
MLA_Q_RANK = 256
MLA_KV_RANK = 128
MLA_WIDTH = MLA_HEADS * MLA_V
ODD_IN = 2496
ODD_SMALL = MLA_Q_RANK + MLA_KV_RANK + MLA_ROPE
ODD_SMALL_PAD = 512
QK_PAD = 256
ROPE_THETA = 10000.0
ATTN_SCALE = (MLA_NOPE + MLA_ROPE) ** -0.5
DEEPNORM_ALPHA = (2.0 * 2) ** 0.25
ADAM_LR = 0.001
ADAM_B1 = 0.9
ADAM_B2 = 0.999
ADAM_EPS = 1e-08
ADAM_WD = 0.01
ADAM_STEP = 10
NEG = -1e30
LANES = 128
N_DEV = 8
N_CHIPS = 4
VMEM_LIMIT_BYTES = 56 * 1024 * 1024
HBM = pl.BlockSpec(memory_space=pltpu.HBM)
VMEM = pl.BlockSpec(memory_space=pltpu.VMEM)


def _params(*sem):
    return pltpu.CompilerParams(dimension_semantics=sem if sem else None, vmem_limit_bytes=VMEM_LIMIT_BYTES)


def _tile(dim, pref):
    for t in (pref, 2048, 1280, 1024, 512, 256, 128):
        if t <= min(pref, dim) and dim % t == 0:
            return t
    return dim


def _sigmoid(z):
    return 1.0 / (1.0 + jnp.exp(-z))


def _dot(a, b, dims):
    return lax.dot_general(a, b, (dims, ((), ())), preferred_element_type=F32)


NN = ((1,), (0,))
NT = ((1,), (1,))
TN = ((0,), (0,))


def _matmul(a, b, *, name, trans_a=False, trans_b=False, out_dtype=F32, b_stacked=False, out_stacked=False,
            tm=512, tn=1024, tk=2048, after=None):
    k, m = a.shape if trans_a else a.shape[::-1]
    if b_stacked:
        ns, kb, n_sh = b.shape
        kb, n = (ns * n_sh, kb) if trans_b else (kb, ns * n_sh)
    else:
        n, kb = b.shape if trans_b else b.shape[::-1]
    assert k == kb, (a.shape, b.shape)
    tm = _tile(m, tm)
    if b_stacked and trans_b:
        tn, tk = _tile(n, tn), n_sh
    elif b_stacked or out_stacked:
        tn, tk = _tile(n // N_CHIPS, tn), _tile(k, tk)
    else:
        tn, tk = _tile(n, tn), _tile(k, tk)
    nk = k // tk
    per = max((n // N_CHIPS) // tn, 1)
    dims = ((0 if trans_a else 1,), (1 if trans_b else 0,))

    def body_one(a_ref, b_ref, *rest):
        o_ref = rest[-1]
        o_ref[...] = _dot(a_ref[...].astype(BF16), b_ref[...].astype(BF16), dims).astype(out_dtype)

    def body_acc(a_ref, b_ref, *rest):
        o_ref, acc_ref = rest[-2:]
        kk = pl.program_id(2)

        @pl.when(kk == 0)
        def _():
            acc_ref[...] = jnp.zeros_like(acc_ref)

        acc_ref[...] += _dot(a_ref[...].astype(BF16), b_ref[...].astype(BF16), dims)

        @pl.when(kk == nk - 1)
        def _():
            o_ref[...] = acc_ref[...].astype(out_dtype)

    a_spec = pl.BlockSpec((tk, tm), lambda i, j, kk: (kk, i)) if trans_a else pl.BlockSpec((tm, tk), lambda i, j, kk: (i, kk))
    if b_stacked and trans_b:
        b_spec = pl.BlockSpec((None, tn, tk), lambda i, j, kk: (kk, j, 0))
    elif b_stacked:
        b_spec = pl.BlockSpec((None, tk, tn), lambda i, j, kk: (j // per, kk, j % per))
    elif trans_b:
        b_spec = pl.BlockSpec((tn, tk), lambda i, j, kk: (j, kk))
    else:
        b_spec = pl.BlockSpec((tk, tn), lambda i, j, kk: (kk, j))
    if out_stacked:
        o_spec = pl.BlockSpec((None, tm, tn), lambda i, j, kk: (j // per, i, j % per))
        o_shape = jax.ShapeDtypeStruct((N_CHIPS, m, n // N_CHIPS), out_dtype)
    else:
        o_spec = pl.BlockSpec((tm, tn), lambda i, j, kk: (i, j))
        o_shape = jax.ShapeDtypeStruct((m, n), out_dtype)
    order = [] if after is None else [after]
    return pl.pallas_call(
        body_one if nk == 1 else body_acc, name=name, grid=(m // tm, n // tn, nk),
        in_specs=[a_spec, b_spec] + [pl.BlockSpec(memory_space=pl.ANY)] * len(order),
        out_specs=o_spec, out_shape=o_shape, scratch_shapes=[] if nk == 1 else [pltpu.VMEM((tm, tn), F32)],
        compiler_params=_params("parallel", "parallel", "arbitrary"),
    )(a, b, *order)


def _row_spec(ts, d):
    return pl.BlockSpec((ts, d), lambda i: (i, 0))


def _vec_spec(d):
    return pl.BlockSpec((1, d), lambda i: (0, 0))


def _modulate(x, scale, shift, *, name):
    s, d = x.shape
    ts = _tile(s, 512)

    def body(x_ref, sc_ref, sh_ref, h_ref):
        h_ref[...] = (x_ref[...] * (1.0 + sc_ref[...]) + sh_ref[...]).astype(BF16)

    return pl.pallas_call(
        body, name=name, grid=(s // ts,), in_specs=[_row_spec(ts, d), _vec_spec(d), _vec_spec(d)],
        out_specs=_row_spec(ts, d), out_shape=jax.ShapeDtypeStruct((s, d), BF16), compiler_params=_params("parallel"),
    )(x, scale, shift)


def _ln_stats(pre):
    mu = jnp.mean(pre, axis=-1, keepdims=True)
    xc = pre - mu
    var = jnp.mean(xc * xc, axis=-1, keepdims=True)
    rstd = lax.rsqrt(var + LN_EPS)
    return xc * rstd, rstd


def _ln_bwd_rows(dout, xhat, rstd, g):
    dxh = dout * g
    m1 = jnp.mean(dxh, axis=-1, keepdims=True)
    m2 = jnp.mean(dxh * xhat, axis=-1, keepdims=True)
    return rstd * (dxh - m1 - xhat * m2)


def _colsum(v):
    return jnp.sum(v, axis=0, keepdims=True)


def _resid_ln_modulate(x, y, gate, g, b, scale_next, shift_next, *, name):
    s, d = x.shape
    ts = _tile(s, 512)

    def body(x_ref, y_ref, gate_ref, g_ref, b_ref, sc_ref, sh_ref, xn_ref, h_ref):
        pre = DEEPNORM_ALPHA * x_ref[...] + (1.0 + gate_ref[...]) * y_ref[...]
        xhat, _ = _ln_stats(pre)
        xn = xhat * g_ref[...] + b_ref[...]
        xn_ref[...] = xn
        h_ref[...] = (xn * (1.0 + sc_ref[...]) + sh_ref[...]).astype(BF16)

    return pl.pallas_call(
        body, name=name, grid=(s // ts,),
        in_specs=[_row_spec(ts, d), _row_spec(ts, d)] + [_vec_spec(d)] * 5,
        out_specs=[_row_spec(ts, d), _row_spec(ts, d)],
        out_shape=[jax.ShapeDtypeStruct((s, d), F32), jax.ShapeDtypeStruct((s, d), BF16)],
        compiler_params=_params("parallel"),
    )(x, y, gate, g, b, scale_next, shift_next)


def _loss_ln_bwd(x, y, gate, g, b, target, *, name):
    s, d = x.shape
    ts = _tile(s, 512)

    def body(x_ref, y_ref, gate_ref, g_ref, b_ref, t_ref, dy_ref, dres_ref, dg_ref, db_ref, dgate_ref, loss_ref):
        @pl.when(pl.program_id(0) == 0)
        def _():
            for r in (dg_ref, db_ref, dgate_ref, loss_ref):
                r[...] = jnp.zeros_like(r)

        yv = y_ref[...]
        pre = DEEPNORM_ALPHA * x_ref[...] + (1.0 + gate_ref[...]) * yv
        xhat, rstd = _ln_stats(pre)
        diff = xhat * g_ref[...] + b_ref[...] - t_ref[...]
        loss_ref[...] += (0.5 / d) * jnp.sum(jnp.sum(diff * diff, axis=1, keepdims=True), axis=0, keepdims=True)
        dout = diff * (1.0 / d)
        dpre = _ln_bwd_rows(dout, xhat, rstd, g_ref[...])
        dy_ref[...] = (dpre * (1.0 + gate_ref[...])).astype(BF16)
        dres_ref[...] = DEEPNORM_ALPHA * dpre
        dg_ref[...] += _colsum(dout * xhat)
        db_ref[...] += _colsum(dout)
        dgate_ref[...] += _colsum(dpre * yv)

    vec = jax.ShapeDtypeStruct((1, d), F32)
    return pl.pallas_call(
        body, name=name, grid=(s // ts,),
        in_specs=[_row_spec(ts, d), _row_spec(ts, d), _vec_spec(d), _vec_spec(d), _vec_spec(d), _row_spec(ts, d)],
        out_specs=[_row_spec(ts, d), _row_spec(ts, d), _vec_spec(d), _vec_spec(d), _vec_spec(d), _vec_spec(1)],
        out_shape=[jax.ShapeDtypeStruct((s, d), BF16), jax.ShapeDtypeStruct((s, d), F32), vec, vec, vec,
                   jax.ShapeDtypeStruct((1, 1), F32)],
        compiler_params=_params("arbitrary"),
    )(x, y, gate, g, b, target)


def _mid_ln_bwd(x, y, gate, g, b, dh_next, dres_next, scale_next, x_next, *, name):
    s, d = x.shape
    ts = _tile(s, 512)

    def body(x_ref, y_ref, gate_ref, g_ref, b_ref, dh_ref, dr_ref, sc_ref, xn_ref,
             dy_ref, dres_ref, dg_ref, db_ref, dgate_ref, dscale_ref, dshift_ref):
        @pl.when(pl.program_id(0) == 0)
        def _():
            for r in (dg_ref, db_ref, dgate_ref, dscale_ref, dshift_ref):
                r[...] = jnp.zeros_like(r)

        dh = dh_ref[...]
        dout = dr_ref[...] + dh * (1.0 + sc_ref[...])
        dscale_ref[...] += _colsum(dh * xn_ref[...])
        dshift_ref[...] += _colsum(dh)
        yv = y_ref[...]
        pre = DEEPNORM_ALPHA * x_ref[...] + (1.0 + gate_ref[...]) * yv
        xhat, rstd = _ln_stats(pre)
        dpre = _ln_bwd_rows(dout, xhat, rstd, g_ref[...])
        dy_ref[...] = (dpre * (1.0 + gate_ref[...])).astype(BF16)
        dres_ref[...] = DEEPNORM_ALPHA * dpre
        dg_ref[...] += _colsum(dout * xhat)
        db_ref[...] += _colsum(dout)
        dgate_ref[...] += _colsum(dpre * yv)

    vec = jax.ShapeDtypeStruct((1, d), F32)
    return pl.pallas_call(
        body, name=name, grid=(s // ts,),
        in_specs=[_row_spec(ts, d), _row_spec(ts, d), _vec_spec(d), _vec_spec(d), _vec_spec(d),
                  _row_spec(ts, d), _row_spec(ts, d), _vec_spec(d), _row_spec(ts, d)],
        out_specs=[_row_spec(ts, d), _row_spec(ts, d)] + [_vec_spec(d)] * 5,
        out_shape=[jax.ShapeDtypeStruct((s, d), BF16), jax.ShapeDtypeStruct((s, d), F32)] + [vec] * 5,
        compiler_params=_params("arbitrary"),
    )(x, y, gate, g, b, dh_next, dres_next, scale_next, x_next)


def _input_bwd(x, dh, dres, scale, *, name):
    s, d = x.shape
    ts = _tile(s, 512)

    def body(x_ref, dh_ref, dr_ref, sc_ref, dx_ref, dscale_ref, dshift_ref):
        @pl.when(pl.program_id(0) == 0)
        def _():
            dscale_ref[...] = jnp.zeros_like(dscale_ref)
            dshift_ref[...] = jnp.zeros_like(dshift_ref)

        dh = dh_ref[...]
        dx_ref[...] = dr_ref[...] + dh * (1.0 + sc_ref[...])
        dscale_ref[...] += _colsum(dh * x_ref[...])
        dshift_ref[...] += _colsum(dh)

    vec = jax.ShapeDtypeStruct((1, d), F32)
    return pl.pallas_call(
        body, name=name, grid=(s // ts,),
        in_specs=[_row_spec(ts, d), _row_spec(ts, d), _row_spec(ts, d), _vec_spec(d)],
        out_specs=[_row_spec(ts, d), _vec_spec(d), _vec_spec(d)],
        out_shape=[jax.ShapeDtypeStruct((s, d), F32), vec, vec],
        compiler_params=_params("arbitrary"),
    )(x, dh, dres, scale)


def _chunk_mask(transposed=False):
    r = lax.broadcasted_iota(jnp.int32, (GMLP_BLOCK, GMLP_BLOCK), 0) // CHUNK
    c = lax.broadcasted_iota(jnp.int32, (GMLP_BLOCK, GMLP_BLOCK), 1) // CHUNK
    return (r <= c) if transposed else (c <= r)


def _window_sum(ext, steps, forward):
    rows = ext.shape[0]
    acc = ext
    for k in range(steps):
        shift = 1 << k
        acc = acc + pltpu.roll(acc, (rows - shift) if forward else shift, 0)
    return acc


def _pool_counts(first_row, rows, win):
    t = first_row + lax.broadcasted_iota(jnp.int32, (rows, 1), 0)
    return jnp.minimum(t + 1, win).astype(F32)


def _even_specs(t):
    col = lambda j: pl.BlockSpec((t, D_MODEL), lambda n: (n, j))
    per = t // POOL_HALO
    prev = pl.BlockSpec((POOL_HALO, D_MODEL), lambda n: (jnp.maximum(n * per - 1, 0), 3))
    return col, per, prev


def _full(shape):
    return pl.BlockSpec(shape, lambda n: (0,) * len(shape))


def _gmlp_head(v_h, ng, nb, w_bf):
    xhat, rstd = _ln_stats(v_h)
    vn = (xhat * ng + nb).astype(BF16)
    return xhat, rstd, vn, _dot(w_bf, vn, NN)


def _pool_group(xb_g, prev_g, first_row, grp):
    t = xb_g.shape[0]
    ext = jnp.concatenate([prev_g, xb_g], axis=0)
    tot = _window_sum(ext, grp + 1, False)[POOL_HALO:, :]
    cnt = _pool_counts(first_row, t, POOL_WINDOWS[grp])
    return tot / cnt - xb_g, cnt


def _even_fwd(proj, ws, bs_t, ng, nb, pool_w, pool_b, pool_scale, *, name):
    s = proj.shape[0]
    t = GMLP_BLOCK
    col, per, prev = _even_specs(t)

    def body(u_ref, v_ref, za_ref, xb_ref, zb_ref, xp_ref, ws_ref, bs_ref, ng_ref, nb_ref, pw_ref, pb_ref, ps_ref, o_ref):
        n = pl.program_id(0)
        mask = _chunk_mask()
        for h in range(GMLP_HEADS):
            c0 = h * GMLP_HEAD_DIM
            cs = slice(c0, c0 + GMLP_HEAD_DIM)
            w_bf = jnp.where(mask, ws_ref[h], 0.0).astype(BF16)
            _, _, _, sv = _gmlp_head(v_ref[:, cs], ng_ref[...], nb_ref[...], w_bf)
            sv = sv + bs_ref[:, h:h + 1]
            za = za_ref[:, cs]
            o_ref[:, cs] = (u_ref[:, cs] * sv * (za * _sigmoid(za))).astype(BF16)
        live = (n > 0).astype(F32)
        for grp in range(POOL_GROUPS):
            c0 = grp * POOL_GROUP_DIM
            cs = slice(c0, c0 + POOL_GROUP_DIM)
            pooled, _ = _pool_group(xb_ref[:, cs], xp_ref[:, cs] * live, n * t, grp)
            yb = _dot(pooled.astype(BF16), pw_ref[grp], NN) + pb_ref[:, cs]
            zb = zb_ref[:, cs]
            o_ref[:, D_MODEL + c0:D_MODEL + c0 + POOL_GROUP_DIM] = (yb * ps_ref[:, cs] * (zb * _sigmoid(zb))).astype(BF16)

    return pl.pallas_call(
        body, name=name, grid=(s // t,),
        in_specs=[col(0), col(1), col(2), col(3), col(4), prev,
                  _full((GMLP_HEADS, t, t)), _full((t, LANES)), _full((1, GMLP_HEAD_DIM)), _full((1, GMLP_HEAD_DIM)),
                  _full((POOL_GROUPS, POOL_GROUP_DIM, POOL_GROUP_DIM)), _full((1, D_MODEL)), _full((1, D_MODEL))],
        out_specs=pl.BlockSpec((t, 2 * D_MODEL), lambda n: (n, 0)),
        out_shape=jax.ShapeDtypeStruct((s, 2 * D_MODEL), BF16),
        compiler_params=_params("parallel"),
    )(proj, proj, proj, proj, proj, proj, ws, bs_t, ng, nb, pool_w, pool_b, pool_scale)


def _even_bwd(proj, dmix, ws, ws_t, bs_t, ng, nb, pool_w, pool_b, pool_scale, *, name):
    s = proj.shape[0]
    t = GMLP_BLOCK
    nblk = s // t
    col, per, prev = _even_specs(t)
    nxt = lambda j: pl.BlockSpec((POOL_HALO, D_MODEL), lambda n: (jnp.minimum((n + 1) * per, nblk * per - 1), j))

    def body(u_ref, v_ref, za_ref, xb_ref, zb_ref, xp_ref, zn_ref, da_ref, db_ref, dbn_ref,
             ws_ref, wst_ref, bs_ref, ng_ref, nb_ref, pw_ref, pb_ref, ps_ref,
             dp_ref, gws_ref, gbs_ref, gng_ref, gnb_ref, gpw_ref, gpb_ref, gps_ref):
        n = pl.program_id(0)

        @pl.when(n == 0)
        def _():
            for r in (gws_ref, gbs_ref, gng_ref, gnb_ref, gpw_ref, gpb_ref, gps_ref):
                r[...] = jnp.zeros_like(r)

        mask, mask_t = _chunk_mask(), _chunk_mask(True)
        lane = lax.broadcasted_iota(jnp.int32, (t, LANES), 1)
        ngv, nbv = ng_ref[...], nb_ref[...]
        for h in range(GMLP_HEADS):
            c0 = h * GMLP_HEAD_DIM
            cs = slice(c0, c0 + GMLP_HEAD_DIM)
            w_bf = jnp.where(mask, ws_ref[h], 0.0).astype(BF16)
            wt_bf = jnp.where(mask_t, wst_ref[h], 0.0).astype(BF16)
            xhat, rstd, vn, sv = _gmlp_head(v_ref[:, cs], ngv, nbv, w_bf)
            sv = sv + bs_ref[:, h:h + 1]
            za, u, da = za_ref[:, cs], u_ref[:, cs], da_ref[:, cs]
            sg = _sigmoid(za)
            sl = za * sg
            dp_ref[:, cs] = (da * sv * sl).astype(BF16)
            dp_ref[:, 2 * D_MODEL + c0:2 * D_MODEL + c0 + GMLP_HEAD_DIM] = (
                da * u * sv * (sg * (1.0 + za * (1.0 - sg)))).astype(BF16)
            dsv = da * u * sl
            gbs_ref[...] += jnp.where(lane == h, jnp.sum(dsv, axis=1, keepdims=True), 0.0)
            dsv_bf = dsv.astype(BF16)
            gws_ref[h] += jnp.where(mask, _dot(dsv_bf, vn, NT), 0.0)
            dvn = _dot(wt_bf, dsv_bf, NN)
            dp_ref[:, D_MODEL + c0:D_MODEL + c0 + GMLP_HEAD_DIM] = _ln_bwd_rows(dvn, xhat, rstd, ngv).astype(BF16)
            gng_ref[...] += _colsum(dvn * xhat)
            gnb_ref[...] += _colsum(dvn)
        live_prev = (n > 0).astype(F32)
        live_next = (n < nblk - 1).astype(F32)
        for grp in range(POOL_GROUPS):
            c0 = grp * POOL_GROUP_DIM
            cs = slice(c0, c0 + POOL_GROUP_DIM)
            xb = xb_ref[:, cs]
            pooled, cnt = _pool_group(xb, xp_ref[:, cs] * live_prev, n * t, grp)
            pooled_bf = pooled.astype(BF16)
            pw = pw_ref[grp]
            yb = _dot(pooled_bf, pw, NN) + pb_ref[:, cs]
            ps = ps_ref[:, cs]
            zb, db = zb_ref[:, cs], db_ref[:, cs]
            sg = _sigmoid(zb)
            sl = zb * sg
            dp_ref[:, 4 * D_MODEL + c0:4 * D_MODEL + c0 + POOL_GROUP_DIM] = (
                db * yb * ps * (sg * (1.0 + zb * (1.0 - sg)))).astype(BF16)
            dsl = db * sl
            dy = dsl * ps
            gps_ref[:, cs] += _colsum(dsl * yb)
            gpb_ref[:, cs] += _colsum(dy)
            dy_bf = dy.astype(BF16)
            gpw_ref[grp] += _dot(pooled_bf, dy_bf, TN)
            r = _dot(dy_bf, pw, NT)
            zn = zn_ref[:, cs]
            dyn = (dbn_ref[:, cs] * (zn * _sigmoid(zn)) * ps * live_next).astype(BF16)
            rn = _dot(dyn, pw, NT) / _pool_counts((n + 1) * t, POOL_HALO, POOL_WINDOWS[grp])
            ext = jnp.concatenate([r / cnt, rn], axis=0)
            dxb = _window_sum(ext, grp + 1, True)[:t, :] - r
            dp_ref[:, 3 * D_MODEL + c0:3 * D_MODEL + c0 + POOL_GROUP_DIM] = dxb.astype(BF16)

    out_shape = [
        jax.ShapeDtypeStruct((s, EVEN_IN), BF16),
        jax.ShapeDtypeStruct((GMLP_HEADS, t, t), F32), jax.ShapeDtypeStruct((t, LANES), F32),
        jax.ShapeDtypeStruct((1, GMLP_HEAD_DIM), F32), jax.ShapeDtypeStruct((1, GMLP_HEAD_DIM), F32),
        jax.ShapeDtypeStruct((POOL_GROUPS, POOL_GROUP_DIM, POOL_GROUP_DIM), F32),
        jax.ShapeDtypeStruct((1, D_MODEL), F32), jax.ShapeDtypeStruct((1, D_MODEL), F32),
    ]
    return pl.pallas_call(
        body, name=name, grid=(nblk,),
        in_specs=[col(0), col(1), col(2), col(3), col(4), prev, nxt(4),
                  pl.BlockSpec((t, D_MODEL), lambda n: (n, 0)), pl.BlockSpec((t, D_MODEL), lambda n: (n, 1)), nxt(1),
                  _full((GMLP_HEADS, t, t)), _full((GMLP_HEADS, t, t)), _full((t, LANES)),
                  _full((1, GMLP_HEAD_DIM)), _full((1, GMLP_HEAD_DIM)),
                  _full((POOL_GROUPS, POOL_GROUP_DIM, POOL_GROUP_DIM)), _full((1, D_MODEL)), _full((1, D_MODEL))],
        out_specs=[pl.BlockSpec((t, EVEN_IN), lambda n: (n, 0))] + [_full(o.shape) for o in out_shape[1:]],
        out_shape=out_shape,
        compiler_params=_params("arbitrary"),
    )(proj, proj, proj, proj, proj, proj, proj, dmix, dmix, dmix, ws, ws_t, bs_t, ng, nb, pool_w, pool_b, pool_scale)


def _half_swap(v):
    lane = lax.broadcasted_iota(jnp.int32, v.shape, 1)
    return jnp.where(lane % MLA_ROPE < MLA_ROPE // 2, pltpu.roll(v, LANES - MLA_ROPE // 2, 1), pltpu.roll(v, MLA_ROPE // 2, 1))


def _rope(v, cos, sin_signed):
    return v * cos + _half_swap(v) * sin_signed


def _rope_bwd(d, cos, sin_signed):
    return d * cos + _half_swap(d * sin_signed)


def _rms(v, g):
    r = lax.rsqrt(jnp.mean(v * v, axis=-1, keepdims=True) + LN_EPS)
    return v * r * g, r


def _rms_bwd(dy, v, r, g):
    u = dy * g
    return r * u - v * (r * r * r) * jnp.mean(u * v, axis=-1, keepdims=True)


def _mla_prep(proj, gq, gkv, cos, sin_signed, *, name):
    s = proj.shape[0]
    ts = _tile(s, 512)

    def body(p_ref, gq_ref, gkv_ref, c_ref, s_ref, q_ref, k_ref):
        qcn, _ = _rms(p_ref[:, :MLA_Q_RANK], gq_ref[...])
        kvn, _ = _rms(p_ref[:, MLA_Q_RANK:MLA_Q_RANK + MLA_KV_RANK], gkv_ref[...])
        kr = _rope(p_ref[:, MLA_Q_RANK + MLA_KV_RANK:], c_ref[...], s_ref[...])
        q_ref[...] = qcn.astype(BF16)
        k_ref[...] = jnp.concatenate([kvn, kr], axis=1).astype(BF16)

    return pl.pallas_call(
        body, name=name, grid=(s // ts,),
        in_specs=[_row_spec(ts, ODD_SMALL_PAD), _vec_spec(MLA_Q_RANK), _vec_spec(MLA_KV_RANK), _row_spec(ts, LANES), _row_spec(ts, LANES)],
        out_specs=[_row_spec(ts, MLA_Q_RANK), _row_spec(ts, QK_PAD)],
        out_shape=[jax.ShapeDtypeStruct((s, MLA_Q_RANK), BF16), jax.ShapeDtypeStruct((s, QK_PAD), BF16)],
        compiler_params=_params("parallel"),
    )(proj, gq, gkv, cos, sin_signed)


def _mla_prep_bwd(proj, dqcn, dkv, gq, gkv, cos, sin_signed, *, name):
    s = proj.shape[0]
    ts = _tile(s, 512)

    def body(p_ref, dq_ref, dkv_ref, gq_ref, gkv_ref, c_ref, s_ref, ds_ref, ggq_ref, ggkv_ref):
        @pl.when(pl.program_id(0) == 0)
        def _():
            ggq_ref[...] = jnp.zeros_like(ggq_ref)
            ggkv_ref[...] = jnp.zeros_like(ggkv_ref)

        qc = p_ref[:, :MLA_Q_RANK]
        kvc = p_ref[:, MLA_Q_RANK:MLA_Q_RANK + MLA_KV_RANK]
        _, rq = _rms(qc, gq_ref[...])
        _, rkv = _rms(kvc, gkv_ref[...])
        dq = dq_ref[...]
        dkvn = dkv_ref[:, :MLA_KV_RANK]
        ggq_ref[...] += _colsum(dq * qc * rq)
        ggkv_ref[...] += _colsum(dkvn * kvc * rkv)
        dkr = _rope_bwd(dkv_ref[:, MLA_KV_RANK:], c_ref[...], s_ref[...])
        ds_ref[...] = jnp.concatenate(
            [_rms_bwd(dq, qc, rq, gq_ref[...]), _rms_bwd(dkvn, kvc, rkv, gkv_ref[...]), dkr], axis=1).astype(BF16)

    return pl.pallas_call(
        body, name=name, grid=(s // ts,),
        in_specs=[_row_spec(ts, ODD_SMALL_PAD), _row_spec(ts, MLA_Q_RANK), _row_spec(ts, QK_PAD),
                  _vec_spec(MLA_Q_RANK), _vec_spec(MLA_KV_RANK), _row_spec(ts, LANES), _row_spec(ts, LANES)],
        out_specs=[_row_spec(ts, ODD_SMALL_PAD), _vec_spec(MLA_Q_RANK), _vec_spec(MLA_KV_RANK)],
        out_shape=[jax.ShapeDtypeStruct((s, ODD_SMALL_PAD), BF16), jax.ShapeDtypeStruct((1, MLA_Q_RANK), F32),
                   jax.ShapeDtypeStruct((1, MLA_KV_RANK), F32)],
        compiler_params=_params("arbitrary"),
    )(proj, dqcn, dkv, gq, gkv, cos, sin_signed)


LOG2_E = 1.4426950408889634
Q_PRESCALE = ATTN_SCALE * LOG2_E


def _q_build(q_nope, q_rope_pre, wuk_hdr, cos, sin_signed, *, name):
    s = q_nope.shape[0]
    ts = _tile(s, 1024)

    def body(qn_ref, qr_ref, w_ref, c_ref, s_ref, o_ref):
        r = _rope(qr_ref[...], c_ref[...], s_ref[...])
        lane = lax.broadcasted_iota(jnp.int32, (ts, LANES), 1)
        for j in range(2):
            ql = _dot(qn_ref[:, j * MLA_NOPE:(j + 1) * MLA_NOPE], w_ref[j], NN)
            rr = r if j == 0 else pltpu.roll(r, MLA_ROPE, 1)
            o_ref[j] = (jnp.concatenate([ql, jnp.where(lane < MLA_ROPE, rr, 0.0)], axis=1) * Q_PRESCALE).astype(BF16)

    return pl.pallas_call(
        body, name=name, grid=(s // ts, MLA_HEADS // 2),
        in_specs=[pl.BlockSpec((ts, 2 * MLA_NOPE), lambda i, p: (i, p)), pl.BlockSpec((ts, LANES), lambda i, p: (i, p)),
                  pl.BlockSpec((2, MLA_NOPE, MLA_KV_RANK), lambda i, p: (p, 0, 0)),
                  pl.BlockSpec((ts, LANES), lambda i, p: (i, 0)), pl.BlockSpec((ts, LANES), lambda i, p: (i, 0))],
        out_specs=pl.BlockSpec((2, ts, QK_PAD), lambda i, p: (p, i, 0)),
        out_shape=jax.ShapeDtypeStruct((MLA_HEADS, s, QK_PAD), BF16),
        compiler_params=_params("parallel", "parallel"),
    )(q_nope, q_rope_pre, wuk_hdr, cos, sin_signed)


def _q_bwd(dq, q_nope, wuk_hrd, cos, sin_signed, *, name):
    s = q_nope.shape[0]
    ts = _tile(s, 1024)

    def body(dq_ref, qn_ref, w_ref, c_ref, s_ref, dn_ref, dr_ref, gw_ref):
        @pl.when(pl.program_id(1) == 0)
        def _():
            gw_ref[...] = jnp.zeros_like(gw_ref)

        lane = lax.broadcasted_iota(jnp.int32, (ts, LANES), 1)
        for j in range(2):
            dql = dq_ref[j, :, :MLA_KV_RANK]
            dn_ref[:, j * MLA_NOPE:(j + 1) * MLA_NOPE] = _dot(dql, w_ref[j], NN).astype(BF16)
            gw_ref[j] += _dot(dql, qn_ref[:, j * MLA_NOPE:(j + 1) * MLA_NOPE], TN)
        hi0 = dq_ref[0, :, MLA_KV_RANK:].astype(F32)
        hi1 = dq_ref[1, :, MLA_KV_RANK:].astype(F32)
        d = jnp.where(lane < MLA_ROPE, hi0, pltpu.roll(hi1, MLA_ROPE, 1))
        dr_ref[...] = _rope_bwd(d, c_ref[...], s_ref[...]).astype(BF16)

    return pl.pallas_call(
        body, name=name, grid=(MLA_HEADS // 2, s // ts),
        in_specs=[pl.BlockSpec((2, ts, QK_PAD), lambda p, i: (p, i, 0)), pl.BlockSpec((ts, 2 * MLA_NOPE), lambda p, i: (i, p)),
                  pl.BlockSpec((2, MLA_KV_RANK, MLA_NOPE), lambda p, i: (p, 0, 0)),
                  pl.BlockSpec((ts, LANES), lambda p, i: (i, 0)), pl.BlockSpec((ts, LANES), lambda p, i: (i, 0))],
        out_specs=[pl.BlockSpec((ts, 2 * MLA_NOPE), lambda p, i: (i, p)), pl.BlockSpec((ts, LANES), lambda p, i: (i, p)),
                   pl.BlockSpec((2, MLA_KV_RANK, MLA_NOPE), lambda p, i: (p, 0, 0))],
        out_shape=[jax.ShapeDtypeStruct((s, MLA_HEADS * MLA_NOPE), BF16), jax.ShapeDtypeStruct((s, MLA_HEADS * MLA_ROPE), BF16),
                   jax.ShapeDtypeStruct((MLA_HEADS, MLA_KV_RANK, MLA_NOPE), F32)],
        compiler_params=_params("parallel", "arbitrary"),
    )(dq, q_nope, wuk_hrd, cos, sin_signed)


ATTN_BQ = 128
ATTN_BK = 512


def _diag_mask(rows, bq, bk, q0, k0):
    qc = (q0 + lax.broadcasted_iota(jnp.int32, (rows, bk), 0) % bq) // CHUNK
    kc = (k0 + lax.broadcasted_iota(jnp.int32, (rows, bk), 1)) // CHUNK
    return kc <= qc


def _attn_fwd(q, k, *, name):
    nh, s, dk = q.shape
    bq, bk = _tile(s, ATTN_BQ), _tile(s, ATTN_BK)
    rows = nh * bq

    def body(q_ref, k_ref, o_ref, lse_ref):
        i = pl.program_id(0)
        qb = q_ref[...].reshape(rows, dk)
        n_before = (i * bq) // bk

        def step(j, carry, masked):
            m, l, acc = carry
            k0 = pl.multiple_of(j * bk, bk)
            kb = k_ref[pl.ds(k0, bk), :]
            sc = _dot(qb, kb, NT)
            if masked:
                sc = jnp.where(_diag_mask(rows, bq, bk, i * bq, k0), sc, NEG)
            m_new = jnp.maximum(m, jnp.max(sc, axis=1, keepdims=True))
            p = jnp.exp2(sc - m_new)
            a = jnp.exp2(m - m_new)
            l = a * l + jnp.sum(p, axis=1, keepdims=True)
            acc = a * acc + _dot(p.astype(BF16), kb[:, :MLA_KV_RANK], NN)
            return m_new, l, acc

        init = (jnp.full((rows, 1), NEG, F32), jnp.zeros((rows, 1), F32), jnp.zeros((rows, MLA_KV_RANK), F32))
        carry = lax.fori_loop(0, n_before, lambda j, c: step(j, c, False), init)
        m, l, acc = step(n_before, carry, True)
        o_ref[...] = (acc / l).astype(BF16).reshape(nh, bq, MLA_KV_RANK)
        lse_ref[...] = jnp.broadcast_to(m + jnp.log2(l), (rows, LANES)).reshape(nh, bq, LANES)

    return pl.pallas_call(
        body, name=name, grid=(s // bq,),
        in_specs=[pl.BlockSpec((nh, bq, dk), lambda i: (0, i, 0)), pl.BlockSpec((s, dk), lambda i: (0, 0))],
        out_specs=[pl.BlockSpec((nh, bq, MLA_KV_RANK), lambda i: (0, i, 0)), pl.BlockSpec((nh, bq, LANES), lambda i: (0, i, 0))],
        out_shape=[jax.ShapeDtypeStruct((nh, s, MLA_KV_RANK), BF16), jax.ShapeDtypeStruct((nh, s, LANES), F32)],
        compiler_params=_params("parallel"),
    )(q, k)


def _attn_bwd(q, k, do, o, lse, *, name):
    nh, s, dk = q.shape
    bq, bk = _tile(s, ATTN_BQ), _tile(s, ATTN_BK)
    rows = nh * bq

    def body(q_ref, k_ref, do_ref, o_ref, lse_ref, dq_ref, dkv_ref):
        i = pl.program_id(0)
        n_before = (i * bq) // bk

        @pl.when(i == 0)
        def _():
            dkv_ref[...] = jnp.zeros_like(dkv_ref)

        qb = q_ref[...].reshape(rows, dk)
        dob = do_ref[...].reshape(rows, MLA_KV_RANK)
        lse_b = lse_ref[...].reshape(rows, LANES)[:, :1]
        delta = jnp.sum(dob.astype(F32) * o_ref[...].reshape(rows, MLA_KV_RANK).astype(F32), axis=1, keepdims=True)

        def step(j, dq, masked):
            j0 = pl.multiple_of(j * bk, bk)
            kb = k_ref[pl.ds(j0, bk), :]
            sc = _dot(qb, kb, NT)
            if masked:
                sc = jnp.where(_diag_mask(rows, bq, bk, i * bq, j0), sc, NEG)
            p = jnp.exp2(sc - lse_b)
            dp = _dot(dob, kb[:, :MLA_KV_RANK], NT)
            ds_bf = (p * (dp - delta)).astype(BF16)
            dkv_ref[pl.ds(j0, bk), :] += _dot(ds_bf, qb, TN) * (1.0 / LOG2_E)
            dkv_ref[pl.ds(j0, bk), :MLA_KV_RANK] += _dot(p.astype(BF16), dob, TN)
            return dq + _dot(ds_bf, kb, NN)

        dq = lax.fori_loop(0, n_before, lambda j, c: step(j, c, False), jnp.zeros((rows, dk), F32))
        dq = step(n_before, dq, True) * ATTN_SCALE
        dq_ref[...] = dq.astype(BF16).reshape(nh, bq, dk)

    blk = lambda w: pl.BlockSpec((nh, bq, w), lambda i: (0, i, 0))
    return pl.pallas_call(
        body, name=name, grid=(s // bq,),
        in_specs=[blk(dk), pl.BlockSpec((s, dk), lambda i: (0, 0)), blk(MLA_KV_RANK), blk(MLA_KV_RANK), blk(LANES)],
        out_specs=[blk(dk), pl.BlockSpec((s, dk), lambda i: (0, 0))],
        out_shape=[jax.ShapeDtypeStruct((nh, s, dk), BF16), jax.ShapeDtypeStruct((s, dk), F32)],
        compiler_params=_params("arbitrary"),
    )(q, k, do, o, lse)


HEAD_GROUP = 4


def _o_build(o_lat, wuv_hrv, proj, *, name):
    s = proj.shape[0]
    ts = _tile(s, 1024)
    w = HEAD_GROUP * MLA_V

    def body(ol_ref, w_ref, z_ref, og_ref):
        for j in range(HEAD_GROUP):
            cs = slice(j * MLA_V, (j + 1) * MLA_V)
            z = z_ref[:, cs]
            og_ref[:, cs] = (_dot(ol_ref[j], w_ref[j], NN) * (z * _sigmoid(z))).astype(BF16)

    return pl.pallas_call(
        body, name=name, grid=(s // ts, MLA_HEADS // HEAD_GROUP),
        in_specs=[pl.BlockSpec((HEAD_GROUP, ts, MLA_KV_RANK), lambda i, g: (g, i, 0)),
                  pl.BlockSpec((HEAD_GROUP, MLA_KV_RANK, MLA_V), lambda i, g: (g, 0, 0)),
                  pl.BlockSpec((ts, w), lambda i, g: (i, g + 1))],
        out_specs=pl.BlockSpec((ts, w), lambda i, g: (i, g)),
        out_shape=jax.ShapeDtypeStruct((s, MLA_WIDTH), BF16),
        compiler_params=_params("parallel", "parallel"),
    )(o_lat, wuv_hrv, proj)


def _o_bwd(dg, proj, o_lat, wuv_hrv, wuv_hvr, *, name):
    s = proj.shape[0]
    ts = _tile(s, 1024)
    w = HEAD_GROUP * MLA_V

    def body(dg_ref, z_ref, ol_ref, w_ref, wt_ref, dol_ref, dz_ref, gw_ref):
        @pl.when(pl.program_id(1) == 0)
        def _():
            gw_ref[...] = jnp.zeros_like(gw_ref)

        for j in range(HEAD_GROUP):
            cs = slice(j * MLA_V, (j + 1) * MLA_V)
            z, dgj, ol = z_ref[:, cs], dg_ref[:, cs], ol_ref[j]
            sg = _sigmoid(z)
            o = _dot(ol, w_ref[j], NN)
            dz_ref[:, cs] = (dgj * o * (sg * (1.0 + z * (1.0 - sg)))).astype(BF16)
            do_bf = (dgj * (z * sg)).astype(BF16)
            dol_ref[j] = _dot(do_bf, wt_ref[j], NN).astype(BF16)
            gw_ref[j] += _dot(ol, do_bf, TN)

    hs = lambda a, b: pl.BlockSpec((HEAD_GROUP, a, b), lambda g, i: (g, 0, 0))
    return pl.pallas_call(
        body, name=name, grid=(MLA_HEADS // HEAD_GROUP, s // ts),
        in_specs=[pl.BlockSpec((ts, w), lambda g, i: (i, g)), pl.BlockSpec((ts, w), lambda g, i: (i, g + 1)),
                  pl.BlockSpec((HEAD_GROUP, ts, MLA_KV_RANK), lambda g, i: (g, i, 0)),
                  hs(MLA_KV_RANK, MLA_V), hs(MLA_V, MLA_KV_RANK)],
        out_specs=[pl.BlockSpec((HEAD_GROUP, ts, MLA_KV_RANK), lambda g, i: (g, i, 0)),
                   pl.BlockSpec((ts, w), lambda g, i: (i, g)), hs(MLA_KV_RANK, MLA_V)],
        out_shape=[jax.ShapeDtypeStruct((MLA_HEADS, s, MLA_KV_RANK), BF16), jax.ShapeDtypeStruct((s, MLA_WIDTH), BF16),
                   jax.ShapeDtypeStruct((MLA_HEADS, MLA_KV_RANK, MLA_V), F32)],
        compiler_params=_params("parallel", "arbitrary"),
    )(dg, proj, o_lat, wuv_hrv, wuv_hvr)


def _ada_mod(c_all, ada_w, ada_b_sh, *, name):
    nl, _, cols = ada_w.shape

    def body(c_ref, w_ref, b_ref, o_ref):
        c = c_ref[...]
        cond = (c * _sigmoid(c)).astype(BF16)
        for l in range(nl):
            o_ref[l] = _dot(cond, w_ref[l].astype(BF16), NN) + b_ref[l]

    return pl.pallas_call(
        body, name=name, out_shape=jax.ShapeDtypeStruct((nl, c_all.shape[0], cols), F32),
        compiler_params=_params(),
    )(c_all, ada_w, ada_b_sh)


def _ada_grad(c_all_t, dmod_sh, *, name):
    nl, _, cols = dmod_sh.shape
    d = c_all_t.shape[0]

    def body(c_ref, dm_ref, gw_ref):
        c = c_ref[...]
        cond_t = c * _sigmoid(c)
        for l in range(nl):
            gw_ref[l] = lax.dot_general(cond_t, dm_ref[l], (NN, ((), ())), precision=lax.Precision.HIGHEST,
                                        preferred_element_type=F32)

    return pl.pallas_call(
        body, name=name, out_shape=jax.ShapeDtypeStruct((nl, d, cols), F32), compiler_params=_params(),
    )(c_all_t, dmod_sh)


def _sum_devices(parts, *, name):
    def body(p_ref, o_ref):
        acc = p_ref[0]
        for k in range(1, parts.shape[0]):
            acc = acc + p_ref[k]
        o_ref[...] = acc

    return pl.pallas_call(body, name=name, out_shape=jax.ShapeDtypeStruct(parts.shape[1:], F32), compiler_params=_params())(parts)


def _adamw_math(w, g, m, v):
    c1 = 1.0 - ADAM_B1 ** ADAM_STEP
    c2 = 1.0 - ADAM_B2 ** ADAM_STEP
    nm = ADAM_B1 * m + (1.0 - ADAM_B1) * g
    nv = ADAM_B2 * v + (1.0 - ADAM_B2) * (g * g)
    return -ADAM_LR * ((nm / c1) / (jnp.sqrt(nv / c2) + ADAM_EPS) + ADAM_WD * w), nm, nv


ADAMW_BLOCK_BYTES = 1 << 20


def _adamw(w, g, m, v, *, name, after=None):
    shape = w.shape
    a, b = shape[-2], shape[-1]
    lead = 1
    for dim in shape[:-2]:
        lead *= dim
    row_bytes = 4 * b
    if a * row_bytes <= ADAMW_BLOCK_BYTES:
        ta = a
        tl = max(1, min(lead, ADAMW_BLOCK_BYTES // (a * row_bytes)))
        while lead % tl:
            tl -= 1
    else:
        tl = 1
        ta = _tile(a, 256)
    to3 = lambda t: t.reshape(lead, a, b)

    def body(w_ref, g_ref, m_ref, v_ref, *rest):
        d_ref, nm_ref, nv_ref = rest[-3:]
        d_ref[...], nm_ref[...], nv_ref[...] = _adamw_math(w_ref[...], g_ref[...], m_ref[...], v_ref[...])

    spec = pl.BlockSpec((tl, ta, b), lambda i, j: (i, j, 0))
    out = jax.ShapeDtypeStruct((lead, a, b), F32)
    order = [] if after is None else [after]
    res = pl.pallas_call(
        body, name=name, grid=(lead // tl, a // ta), in_specs=[spec] * 4 + [pl.BlockSpec(memory_space=pl.ANY)] * len(order),
        out_specs=[spec] * 3, out_shape=[out] * 3, compiler_params=_params("parallel", "parallel"),
    )(to3(w), to3(g), to3(m), to3(v), *order)
    return [r.reshape(shape) for r in res]


def _adamw_small(ws, gs, ms, vs, *, name):
    n = len(ws)

    def body(*refs):
        for k in range(n):
            w_ref, g_ref, m_ref, v_ref = (refs[j * n + k] for j in range(4))
            d_ref, nm_ref, nv_ref = (refs[(4 + j) * n + k] for j in range(3))
            d_ref[...], nm_ref[...], nv_ref[...] = _adamw_math(w_ref[...], g_ref[...], m_ref[...], v_ref[...])

    outs = [jax.ShapeDtypeStruct(w.shape, F32) for w in ws]
    res = pl.pallas_call(body, name=name, out_shape=outs * 3, compiler_params=_params())(*ws, *gs, *ms, *vs)
    return res[:n], res[n:2 * n], res[2 * n:]


def _flip(v, bit):
    return 1 - v if bit else v


CHIP_DELTAS = ((1, 0), (0, 1), (1, 1))
SUM_ROWS = 32


def _all_gather_chips(shard, *, name):
    def body(x_ref, o_ref, send_sems, recv_sems, local_sem):
        x, y, c = lax.axis_index("x"), lax.axis_index("y"), lax.axis_index("c")
        mine = pltpu.make_async_copy(x_ref, o_ref.at[2 * x + y], local_sem)
        mine.start()

        def copy(k):
            tx, ty = _flip(x, CHIP_DELTAS[k][0]), _flip(y, CHIP_DELTAS[k][1])
            send = pltpu.make_async_remote_copy(src_ref=x_ref, dst_ref=o_ref.at[2 * x + y], send_sem=send_sems.at[k],
                                                recv_sem=recv_sems.at[k], device_id=(tx, ty, c), device_id_type=MESH)
            recv = pltpu.make_async_remote_copy(src_ref=x_ref, dst_ref=o_ref.at[2 * tx + ty], send_sem=send_sems.at[k],
                                                recv_sem=recv_sems.at[k], device_id=(tx, ty, c), device_id_type=MESH)
            return send, recv

        pairs = [copy(k) for k in range(3)]
        for send, _ in pairs:
            send.start()
        for _, recv in pairs:
            recv.wait_recv()
        for send, _ in pairs:
            send.wait_send()
        mine.wait()

    return pl.pallas_call(
        body, name=name, out_shape=jax.ShapeDtypeStruct((N_CHIPS,) + shard.shape, shard.dtype),
        in_specs=[HBM], out_specs=HBM,
        scratch_shapes=[pltpu.SemaphoreType.DMA((3,)), pltpu.SemaphoreType.DMA((3,)), pltpu.SemaphoreType.DMA(())],
    )(shard)


def _gather_weights(shards, *, name):
    n = len(shards)

    def body(*refs):
        w_refs, o_refs = refs[:n], refs[n:2 * n]
        ici_send, ici_recv, d2d_send, d2d_recv, local_sems = refs[2 * n:]
        x, y, c = lax.axis_index("x"), lax.axis_index("y"), lax.axis_index("c")
        me = 2 * x + y
        peers = [(_flip(x, dx), _flip(y, dy)) for dx, dy in CHIP_DELTAS]
        locals_ = [pltpu.make_async_copy(w_refs[k], o_refs[k].at[me], local_sems.at[k]) for k in range(n)]
        for cp in locals_:
            cp.start()

        def rows(k, which):
            half = shards[k].shape[0] // 2
            return pl.ds(pl.multiple_of(which * half, half), half)

        def over_chips(k, d, slot):
            tx, ty = peers[d]
            return pltpu.make_async_remote_copy(
                src_ref=w_refs[k].at[rows(k, c)], dst_ref=o_refs[k].at[slot, rows(k, c)], send_sem=ici_send.at[k, d],
                recv_sem=ici_recv.at[k, d], device_id=(tx, ty, c), device_id_type=MESH)

        def to_sibling(k, d, which):
            tx, ty = peers[d]
            at = o_refs[k].at[2 * tx + ty, rows(k, which)]
            return pltpu.make_async_remote_copy(src_ref=at, dst_ref=at, send_sem=d2d_send.at[k, d], recv_sem=d2d_recv.at[k, d],
                                                device_id=(x, y, 1 - c), device_id_type=MESH)

        sends = [over_chips(k, d, me) for k in range(n) for d in range(3)]
        for cp in sends:
            cp.start()
        passed = []
        for k in range(n):
            for d in range(3):
                over_chips(k, d, 2 * peers[d][0] + peers[d][1]).wait_recv()
                passed.append(to_sibling(k, d, c))
                passed[-1].start()
        for k in range(n):
            for d in range(3):
                to_sibling(k, d, 1 - c).wait_recv()
        for cp in sends + passed:
            cp.wait_send()
        for cp in locals_:
            cp.wait()

    return pl.pallas_call(
        body, name=name, out_shape=[jax.ShapeDtypeStruct((N_CHIPS,) + w.shape, w.dtype) for w in shards],
        in_specs=[HBM] * n, out_specs=[HBM] * n,
        scratch_shapes=[pltpu.SemaphoreType.DMA((n, 3))] * 4 + [pltpu.SemaphoreType.DMA((n,))],
    )(*shards)


def _add_into(dst_ref, src_ref):
    ns, r, _ = dst_ref.shape
    step = SUM_ROWS if r % SUM_ROWS == 0 else r
    for s in range(ns):
        def tile(t, carry):
            at = pl.ds(pl.multiple_of(t * step, step), step)
            dst_ref[s, at, :] = (dst_ref[s, at, :].astype(F32) + src_ref[s, at, :].astype(F32)).astype(dst_ref.dtype)
            return carry
        lax.fori_loop(0, r // step, tile, 0)


def _reduce_sibling(grads, *, name):
    n = len(grads)

    def body(*refs):
        g_refs, o_refs = refs[:n], refs[n:2 * n]
        mine, got = refs[2 * n:3 * n], refs[3 * n:4 * n]
        send_sems, recv_sems, load_sems, store_sems = refs[4 * n:]
        x, y, c = lax.axis_index("x"), lax.axis_index("y"), lax.axis_index("c")
        loads = [pltpu.make_async_copy(g_refs[k].at[:, c], mine[k], load_sems.at[k]) for k in range(n)]
        swaps = [pltpu.make_async_remote_copy(src_ref=g_refs[k].at[:, 1 - c], dst_ref=got[k], send_sem=send_sems.at[k],
                                              recv_sem=recv_sems.at[k], device_id=(x, y, 1 - c), device_id_type=MESH)
                 for k in range(n)]
        for cp in loads + swaps:
            cp.start()
        stores = []
        for k in range(n):
            loads[k].wait()
            swaps[k].wait_recv()
            _add_into(mine[k], got[k])
            stores.append(pltpu.make_async_copy(mine[k], o_refs[k], store_sems.at[k]))
            stores[-1].start()
        for k in range(n):
            swaps[k].wait_send()
            stores[k].wait()

    half = [jax.ShapeDtypeStruct((g.shape[0],) + g.shape[2:], g.dtype) for g in grads]
    return pl.pallas_call(
        body, name=name, out_shape=half, in_specs=[HBM] * n, out_specs=[HBM] * n,
        scratch_shapes=[pltpu.VMEM(h.shape, h.dtype) for h in half] * 2 + [pltpu.SemaphoreType.DMA((n,))] * 4,
        compiler_params=_params(),
    )(*grads)


def _reduce_chips(parts, landed, *, name):
    n_send = len(parts)
    n = n_send + len(landed)

    def body(*refs):
        p_refs, o_refs = refs[:n], refs[n:2 * n]
        got, total = refs[2 * n:3 * n], refs[3 * n:4 * n]
        send_sems, recv_sems, load_sems, share_send, share_recv, store_sems = refs[4 * n:]
        x, y, c = lax.axis_index("x"), lax.axis_index("y"), lax.axis_index("c")
        me = 2 * x + y
        peers = [(_flip(x, dx), _flip(y, dy)) for dx, dy in CHIP_DELTAS]

        def over_chips(k, d, src_slot, dst_slot):
            tx, ty = peers[d]
            return pltpu.make_async_remote_copy(
                src_ref=p_refs[k].at[src_slot], dst_ref=got[k].at[dst_slot], send_sem=send_sems.at[k, d],
                recv_sem=recv_sems.at[k, d], device_id=(tx, ty, c), device_id_type=MESH)

        loads = [pltpu.make_async_copy(p_refs[k].at[me], got[k].at[me], load_sems.at[k]) for k in range(n_send)]
        loads += [pltpu.make_async_copy(p_refs[k], got[k], load_sems.at[k]) for k in range(n_send, n)]
        sends = [over_chips(k, d, 2 * peers[d][0] + peers[d][1], me) for k in range(n_send) for d in range(3)]
        for cp in loads + sends:
            cp.start()
        shares, stores = [], []
        for k in range(n):
            loads[k].wait()
            for d in range(3 if k < n_send else 0):
                slot = 2 * peers[d][0] + peers[d][1]
                over_chips(k, d, slot, slot).wait_recv()
            r = total[k].shape[0]
            step = SUM_ROWS if r % SUM_ROWS == 0 else r

            def tile(t, carry, k=k, step=step):
                at = pl.ds(pl.multiple_of(t * step, step), step)
                acc = got[k][0, at, :].astype(F32)
                for s in range(1, N_CHIPS):
                    acc = acc + got[k][s, at, :].astype(F32)
                total[k][at, :] = acc
                return carry

            lax.fori_loop(0, r // step, tile, 0)
            stores.append(pltpu.make_async_copy(total[k], o_refs[k].at[c], store_sems.at[k]))
            shares.append(pltpu.make_async_remote_copy(
                src_ref=total[k], dst_ref=o_refs[k].at[c], send_sem=share_send.at[k], recv_sem=share_recv.at[k],
                device_id=(x, y, 1 - c), device_id_type=MESH))
            stores[-1].start()
            shares[-1].start()
        for k in range(n):
            pltpu.make_async_remote_copy(
                src_ref=total[k], dst_ref=o_refs[k].at[1 - c], send_sem=share_send.at[k], recv_sem=share_recv.at[k],
                device_id=(x, y, 1 - c), device_id_type=MESH).wait_recv()
        for cp in sends + shares:
            cp.wait_send()
        for cp in stores:
            cp.wait()

    both = list(parts) + list(landed)
    return pl.pallas_call(
        body, name=name, out_shape=[jax.ShapeDtypeStruct((2,) + p.shape[1:], F32) for p in both],
        in_specs=[HBM] * n, out_specs=[HBM] * n,
        scratch_shapes=[pltpu.VMEM(p.shape, p.dtype) for p in both] + [pltpu.VMEM(p.shape[1:], F32) for p in both]
        + [pltpu.SemaphoreType.DMA((n, 3))] * 2 + [pltpu.SemaphoreType.DMA((n,))] * 4,
        compiler_params=_params(),
    )(*both)


SEM = pl.BlockSpec(memory_space=pltpu.SEMAPHORE)
IN_FLIGHT = pltpu.SideEffectType.DATAFLOW_SIDE_EFFECTING


def _chip_copies(s_refs, l_refs, sems, scatter, theirs):
    x, y, c = lax.axis_index("x"), lax.axis_index("y"), lax.axis_index("c")
    me = 2 * x + y
    copies = []
    for k in range(len(s_refs)):
        for d, (dx, dy) in enumerate(CHIP_DELTAS):
            tx, ty = _flip(x, dx), _flip(y, dy)
            peer = 2 * tx + ty
            send_sem, recv_sem = sems[2 * (3 * k + d)], sems[2 * (3 * k + d) + 1]
            copies.append(pltpu.make_async_remote_copy(
                src_ref=s_refs[k].at[peer] if scatter else s_refs[k], dst_ref=l_refs[k].at[peer if theirs else me],
                send_sem=send_sem, recv_sem=recv_sem, device_id=(tx, ty, c), device_id_type=MESH))
    return copies


def _chips_start(srcs, lands, after, *, scatter, name):
    n = len(srcs)
    n_sem = 2 * 3 * n

    def body(*refs):
        s_refs, l_refs = refs[:n], refs[n:2 * n]
        sems = refs[2 * n + 1:2 * n + 1 + n_sem]
        token = refs[-1]
        for cp in _chip_copies(s_refs, l_refs, sems, scatter, False):
            cp.start()
        token[...] = jnp.zeros_like(token)

    hbm = lambda a: pltpu.HBM(a.shape, a.dtype)
    res = pl.pallas_call(
        body, name=name,
        out_shape=(*[pltpu.SemaphoreType.DMA(())] * n_sem, *[hbm(a) for a in srcs], *[hbm(a) for a in lands],
                   jax.ShapeDtypeStruct((8, LANES), F32)),
        in_specs=[HBM] * (2 * n) + [pl.BlockSpec(memory_space=pl.ANY)],
        out_specs=(*[SEM] * n_sem, *[HBM] * (2 * n), VMEM),
        input_output_aliases={k: n_sem + k for k in range(2 * n)},
        compiler_params=pltpu.CompilerParams(has_side_effects=IN_FLIGHT),
    )(*[pltpu.with_memory_space_constraint(a, pltpu.HBM) for a in list(srcs) + list(lands)], after)
    return res[:n_sem], res[n_sem:n_sem + n], res[n_sem + n:n_sem + 2 * n], res[-1]


def _chips_wait(sems, srcs, lands, after, *, scatter, name):
    n = len(srcs)
    n_sem = len(sems)

    def body(*refs):
        s_refs, l_refs = refs[:n], refs[n:2 * n]
        sem_refs = refs[2 * n:2 * n + n_sem]
        for cp in _chip_copies(s_refs, l_refs, sem_refs, scatter, False):
            cp.wait_send()
        for cp in _chip_copies(s_refs, l_refs, sem_refs, scatter, True):
            cp.wait_recv()

    hbm = lambda a: pltpu.HBM(a.shape, a.dtype)
    res = pl.pallas_call(
        body, name=name, out_shape=tuple(hbm(a) for a in list(srcs) + list(lands)),
        in_specs=[HBM] * (2 * n) + [SEM] * n_sem + [pl.BlockSpec(memory_space=pl.ANY)], out_specs=tuple([HBM] * (2 * n)),
        input_output_aliases={k: k for k in range(2 * n)},
        compiler_params=pltpu.CompilerParams(has_side_effects=IN_FLIGHT),
    )(*srcs, *lands, *sems, after)
    return res[n:]


def _all_gather_devices(rows, *, name):
    deltas = [(dx, dy, dc) for dx in (0, 1) for dy in (0, 1) for dc in (0, 1)][1:]

    def body(x_ref, o_ref, send_sems, recv_sems):
        x, y, c = lax.axis_index("x"), lax.axis_index("y"), lax.axis_index("c")
        me = 4 * x + 2 * y + c
        o_ref[me] = x_ref[...]
        sends, recvs = [], []
        for k, (dx, dy, dc) in enumerate(deltas):
            tx, ty, tc = _flip(x, dx), _flip(y, dy), _flip(c, dc)
            sends.append(pltpu.make_async_remote_copy(src_ref=x_ref, dst_ref=o_ref.at[me], send_sem=send_sems.at[k],
                                                      recv_sem=recv_sems.at[k], device_id=(tx, ty, tc), device_id_type=MESH))
            recvs.append(pltpu.make_async_remote_copy(src_ref=x_ref, dst_ref=o_ref.at[4 * tx + 2 * ty + tc],
                                                      send_sem=send_sems.at[k], recv_sem=recv_sems.at[k],
                                                      device_id=(tx, ty, tc), device_id_type=MESH))
        for cp in sends:
            cp.start()
        for cp in recvs:
            cp.wait_recv()
        for cp in sends:
            cp.wait_send()

    return pl.pallas_call(
        body, name=name, out_shape=jax.ShapeDtypeStruct((N_DEV,) + rows.shape, rows.dtype),
        in_specs=[VMEM], out_specs=VMEM,
        scratch_shapes=[pltpu.SemaphoreType.DMA((N_DEV - 1,)), pltpu.SemaphoreType.DMA((N_DEV - 1,))],
    )(rows)


WEIGHTS = ("ada_w", "ada_b", "ln_g", "ln_b", "e_w_in", "gmlp_norm_g", "gmlp_norm_b", "gmlp_ws", "gmlp_bs", "pool_w",
           "pool_b", "pool_scale", "e_w_out", "o_w_in", "mla_q_norm_g", "mla_kv_norm_g", "mla_w_uq", "mla_w_uk",
           "mla_w_uv", "o_w_out")
SMALL = ("ln_g", "ln_b", "gmlp_norm_g", "gmlp_norm_b", "gmlp_bs", "pool_b", "pool_scale", "mla_kv_norm_g", "mla_q_norm_g")


def _pad_cols(v, n):
    return jnp.concatenate([v, jnp.zeros((v.shape[0], n - v.shape[1]), v.dtype)], axis=1) if n > v.shape[1] else v


def _halves(g):
    return g.reshape(g.shape[0], 2, g.shape[1] // 2, g.shape[2])


def kernel(x, c, positions, ada_w, ada_b, ln_g, ln_b, e_w_in, gmlp_norm_g, gmlp_norm_b, gmlp_ws, gmlp_bs, pool_w, pool_b, pool_scale, e_w_out, o_w_in, mla_q_norm_g, mla_kv_norm_g, mla_w_uq, mla_w_uk, mla_w_uv, o_w_out, loss_target, m_ada_w, m_ada_b, m_ln_g, m_ln_b, m_e_w_in, m_gmlp_norm_g, m_gmlp_norm_b, m_gmlp_ws, m_gmlp_bs, m_pool_w, m_pool_b, m_pool_scale, m_e_w_out, m_o_w_in, m_mla_q_norm_g, m_mla_kv_norm_g, m_mla_w_uq, m_mla_w_uk, m_mla_w_uv, m_o_w_out, v_ada_w, v_ada_b, v_ln_g, v_ln_b, v_e_w_in, v_gmlp_norm_g, v_gmlp_norm_b, v_gmlp_ws, v_gmlp_bs, v_pool_w, v_pool_b, v_pool_scale, v_e_w_out, v_o_w_in, v_mla_q_norm_g, v_mla_kv_norm_g, v_mla_w_uq, v_mla_w_uk, v_mla_w_uv, v_o_w_out):
    args = dict(locals())
    weights = {n: args[n] for n in WEIGHTS}
    mom = {n: args["m_" + n] for n in WEIGHTS}
    var = {n: args["v_" + n] for n in WEIGHTS}
    ax, ay, ac = lax.axis_index("x"), lax.axis_index("y"), lax.axis_index("c")
    chip = 2 * ax + ay
    dev = 2 * chip + ac
    d = D_MODEL
    x2 = x[0]
    target = loss_target[0]
    q_rank_sh = mla_q_norm_g.shape[1]

    empty_zone = lambda w: lax.dynamic_update_slice(lax.empty((N_CHIPS,) + w.shape, w.dtype), w[None], (chip, 0, 0))
    shards0 = [w.astype(BF16) for w in (pool_w[0].reshape(-1, POOL_GROUP_DIM), e_w_out[0])]
    shards1 = [w.astype(BF16) for w in (o_w_in[0], mla_w_uq[0].reshape(q_rank_sh, -1), o_w_out[0])]
    w_in0, = _gather_weights([e_w_in[0].astype(BF16)], name="gather_weights")
    flight0 = _chips_start(shards0, [empty_zone(w) for w in shards0], w_in0, scatter=False, name="gather0_start")
    flight1 = _chips_start(shards1, [empty_zone(w) for w in shards1], flight0[3], scatter=False, name="gather1_start")
    wuk_hrd = jnp.transpose(mla_w_uk[0], (1, 0, 2)).astype(BF16)
    wuk_hdr = jnp.transpose(mla_w_uk[0], (1, 2, 0)).astype(BF16)
    wuv_hrv = jnp.transpose(mla_w_uv[0], (1, 0, 2)).astype(BF16)
    wuv_hvr = jnp.transpose(mla_w_uv[0], (1, 2, 0)).astype(BF16)
    ws = gmlp_ws[0]
    ws_t = jnp.transpose(ws, (0, 2, 1))
    bs_t = _pad_cols(gmlp_bs[0].T, LANES)

    inv = 1.0 / (ROPE_THETA ** (jnp.arange(0, MLA_ROPE, 2, dtype=F32) / MLA_ROPE))
    ang = positions[0].astype(F32)[:, None] * inv
    cos_t = jnp.tile(jnp.cos(ang), (1, 4))
    sin_t = jnp.tile(jnp.concatenate([-jnp.sin(ang), jnp.sin(ang)], axis=1), (1, 2))

    c_all = _all_gather_devices(c.reshape(8, LANES), name="gather_c").reshape(N_DEV, d)
    cols = ada_w.shape[2]
    ada_b_mine = lax.dynamic_slice_in_dim(ada_b, chip * cols, cols, axis=1)[:, None, :]
    mod_sh = _ada_mod(c_all, ada_w, ada_b_mine, name="ada_mod")
    q_norm_rows = jnp.zeros((8, cols), F32).at[0, :q_rank_sh].set(mla_q_norm_g[0])
    mod_all = _all_gather_chips(jnp.concatenate([mod_sh.reshape(2 * N_DEV, cols), q_norm_rows]), name="gather_mod")
    q_norm_g = mod_all[:, 2 * N_DEV, :q_rank_sh].reshape(1, -1)
    mod_all = jnp.transpose(mod_all[:, :2 * N_DEV].reshape(N_CHIPS, 2, N_DEV, cols), (1, 2, 0, 3)).reshape(2, N_DEV, 3 * d)
    mod = lax.dynamic_index_in_dim(mod_all, dev, axis=1, keepdims=False)
    shift = [mod[l:l + 1, :d] for l in range(2)]
    scale = [mod[l:l + 1, d:2 * d] for l in range(2)]
    gate = [mod[l:l + 1, 2 * d:] for l in range(2)]

    scale[0] = scale[0] + flight1[3][:1, :1]
    h0 = _modulate(x2, scale[0], shift[0], name="modulate0")
    proj0 = _matmul(h0, w_in0, b_stacked=True, tm=1024, tn=1280, name="proj0")
    pool_w_g, w_out0 = _chips_wait(*flight0[:3], proj0, scatter=False, name="gather0_wait")
    pool_w_bf = jnp.transpose(pool_w_g.reshape(N_CHIPS, POOL_GROUPS, -1, POOL_GROUP_DIM), (1, 0, 2, 3)).reshape(
        POOL_GROUPS, POOL_GROUP_DIM, POOL_GROUP_DIM)
    w_out0 = w_out0.reshape(-1, d)
    mix0 = _even_fwd(proj0, ws, bs_t, gmlp_norm_g, gmlp_norm_b, pool_w_bf, pool_b, pool_scale, name="even_fwd")
    y0 = _matmul(mix0, w_out0, name="out0")
    x1, h1 = _resid_ln_modulate(x2, y0, gate[0], ln_g[0:1], ln_b[0:1], scale[1], shift[1], name="resid_ln0")

    w_in1_g, w_uq_g, w_out1 = _chips_wait(*flight1[:3], h1, scatter=False, name="gather1_wait")
    w_out1 = w_out1.reshape(-1, d)
    w_in1 = jnp.transpose(w_in1_g, (1, 0, 2)).reshape(d, ODD_IN)
    w_in1 = jnp.concatenate([_pad_cols(w_in1[:, :ODD_SMALL], ODD_SMALL_PAD), w_in1[:, ODD_SMALL:]], axis=1)
    w_uq = w_uq_g.reshape(MLA_Q_RANK, MLA_HEADS, MLA_NOPE + MLA_ROPE)
    w_uq_nope = w_uq[:, :, :MLA_NOPE].reshape(MLA_Q_RANK, -1)
    w_uq_rope = w_uq[:, :, MLA_NOPE:].reshape(MLA_Q_RANK, -1)
    proj1 = _matmul(h1, w_in1, tm=1024, tn=1280, name="proj1")
    q_cn, keys = _mla_prep(proj1, q_norm_g, mla_kv_norm_g, cos_t, sin_t, name="mla_prep")
    q_nope = _matmul(q_cn, w_uq_nope, tm=1024, tn=2048, name="q_nope", out_dtype=BF16)
    q_rope_pre = _matmul(q_cn, w_uq_rope, tm=1024, name="q_rope")
    q = _q_build(q_nope, q_rope_pre, wuk_hdr, cos_t, sin_t, name="q_build")
    o_lat, lse = _attn_fwd(q, keys, name="attn_fwd")
    og = _o_build(o_lat, wuv_hrv, proj1, name="o_build")
    y1 = _matmul(og, w_out1, name="out1")

    dy1, dres1, g_ln_g1, g_ln_b1, dgate1, loss = _loss_ln_bwd(x1, y1, gate[1], ln_g[1:2], ln_b[1:2], target, name="loss_ln1")
    dg1 = _matmul(dy1, w_out1, trans_b=True, tn=2048, name="d_og")
    g_w_out1 = _matmul(og, dy1, trans_a=True, out_dtype=BF16, tm=1024, name="g_out1")
    do_lat, dz, g_uv = _o_bwd(dg1, proj1, o_lat, wuv_hrv, wuv_hvr, name="o_bwd")
    dq, dkeys = _attn_bwd(q, keys, do_lat, o_lat, lse, name="attn_bwd")
    dq_nope, dq_rope, g_uk = _q_bwd(dq, q_nope, wuk_hrd, cos_t, sin_t, name="q_bwd")
    dq_cn = (_matmul(dq_nope, w_uq_nope, trans_b=True, tm=1024, name="d_qcn_nope")
             + _matmul(dq_rope, w_uq_rope, trans_b=True, tm=1024, name="d_qcn_rope"))
    g_uq_nope = _matmul(q_cn, dq_nope, trans_a=True, out_dtype=BF16, tn=2048, name="g_uq_nope")
    g_uq_rope = _matmul(q_cn, dq_rope, trans_a=True, out_dtype=BF16, name="g_uq_rope")
    dsmall, g_qg, g_kvg = _mla_prep_bwd(proj1, dq_cn, dkeys, q_norm_g, mla_kv_norm_g, cos_t, sin_t, name="mla_prep_bwd")
    dproj1 = jnp.concatenate([dsmall, dz], axis=1)
    dh1 = _matmul(dproj1, w_in1, trans_b=True, name="d_h1")
    g_w_in1 = _matmul(h1, dproj1, trans_a=True, out_dtype=BF16, tm=1024, tn=1280, name="g_in1")

    g_uq = jnp.concatenate([g_uq_nope.reshape(MLA_Q_RANK, MLA_HEADS, MLA_NOPE), g_uq_rope.reshape(MLA_Q_RANK, MLA_HEADS, MLA_ROPE)], axis=2)
    g_w_in1 = jnp.concatenate([g_w_in1[:, :ODD_SMALL], g_w_in1[:, ODD_SMALL_PAD:]], axis=1)
    g_w_in1 = jnp.transpose(g_w_in1.reshape(d, N_CHIPS, -1), (1, 0, 2))
    big1 = [
        _halves(g_w_in1),
        _halves(g_uq.reshape(N_CHIPS, q_rank_sh, -1)),
        _halves(g_w_out1.reshape(N_CHIPS, -1, d)),
        _halves(g_uk.astype(BF16).reshape(N_CHIPS, -1, MLA_NOPE)),
        _halves(g_uv.astype(BF16).reshape(N_CHIPS, -1, MLA_V)),
    ]
    parts1 = _reduce_sibling(big1, name="reduce_sibling1")
    lands2 = [lax.dynamic_update_slice(lax.empty(p.shape, BF16), lax.dynamic_slice_in_dim(p, chip, 1, axis=0), (chip, 0, 0))
              for p in parts1]
    flight2 = _chips_start(parts1, lands2, loss, scatter=True, name="reduce1_start")

    gate[0] = gate[0] + flight2[3][:1, :1]
    dy0, dres0, g_ln_g0, g_ln_b0, dgate0, dscale1, dshift1 = _mid_ln_bwd(
        x2, y0, gate[0], ln_g[0:1], ln_b[0:1], dh1, dres1, scale[1], x1, name="mid_ln0")
    dmix0 = _matmul(dy0, w_out0, trans_b=True, tn=2048, name="d_mix0")
    g_w_out0 = _matmul(mix0, dy0, trans_a=True, out_dtype=BF16, tm=1024, name="g_out0")
    dproj0, g_ws, g_bs_t, g_ng, g_nb, g_pw, g_pb, g_ps = _even_bwd(
        proj0, dmix0, ws, ws_t, bs_t, gmlp_norm_g, gmlp_norm_b, pool_w_bf, pool_b, pool_scale, name="even_bwd")
    g_w_in0 = _matmul(h0, dproj0, trans_a=True, out_dtype=BF16, out_stacked=True, tm=1024, tn=1280, name="g_in0")

    g_pw = jnp.transpose(g_pw.astype(BF16).reshape(POOL_GROUPS, N_CHIPS, -1, POOL_GROUP_DIM), (1, 0, 2, 3))
    big0 = [
        _halves(g_w_in0),
        _halves(g_pw.reshape(N_CHIPS, -1, POOL_GROUP_DIM)),
        _halves(g_w_out0.reshape(N_CHIPS, -1, d)),
        _halves(g_ws.astype(BF16)),
    ]
    parts0 = _reduce_sibling(big0, name="reduce_sibling0")
    landed1 = _chips_wait(*flight2[:3], parts0[0], scatter=True, name="reduce1_wait")
    lands3 = [lax.dynamic_update_slice(lax.empty(p.shape, BF16), lax.dynamic_slice_in_dim(p, chip, 1, axis=0), (chip, 0, 0))
              for p in parts0]
    flight3 = _chips_start(parts0, lands3, landed1[0], scatter=True, name="reduce0_start")
    dh0 = _matmul(dproj0, w_in0, trans_b=True, b_stacked=True, tm=1024, after=flight3[3], name="d_h0")
    grad_x, dscale0, dshift0 = _input_bwd(x2, dh0, dres0, scale[0], name="input_bwd")

    small_local = {
        "ln_g": jnp.concatenate([g_ln_g0, g_ln_g1]), "ln_b": jnp.concatenate([g_ln_b0, g_ln_b1]),
        "gmlp_norm_g": g_ng, "gmlp_norm_b": g_nb, "gmlp_bs": g_bs_t[:, :GMLP_HEADS].T, "pool_b": g_pb, "pool_scale": g_ps,
        "mla_kv_norm_g": g_kvg, "mla_q_norm_g": g_qg,
    }
    n_mod = 2 * 3 * d
    vec = jnp.concatenate([dshift0, dscale0, dgate0, dshift1, dscale1, dgate1]
                          + [small_local[n].reshape(1, -1) for n in SMALL], axis=1)
    n_vec = vec.shape[1]
    vec = _pad_cols(vec, -(-n_vec // (8 * LANES)) * 8 * LANES).reshape(-1, LANES)
    vec_all = _all_gather_devices(vec, name="gather_small")
    vec_sum = _sum_devices(vec_all, name="sum_small").reshape(-1)
    dmod_all = vec_all.reshape(N_DEV, -1)[:, :n_mod].reshape(N_DEV, 2, 3 * d)
    dmod_sh = jnp.transpose(lax.dynamic_slice_in_dim(dmod_all, chip * cols, cols, axis=2), (1, 0, 2))
    dmod_sh = jnp.concatenate([dmod_sh, jnp.zeros((2, LANES - N_DEV, cols), F32)], axis=1)
    grads = {"ada_w": _ada_grad(_pad_cols(c_all.T, LANES), dmod_sh, name="ada_grad"), "ada_b": vec_sum[:n_mod].reshape(2, 3 * d)}
    off = n_mod
    for n in SMALL:
        sz = small_local[n].size
        grads[n] = vec_sum[off:off + sz]
        off += sz
    grads["mla_q_norm_g"] = lax.dynamic_slice_in_dim(grads["mla_q_norm_g"], chip * q_rank_sh, q_rank_sh)
    for n in SMALL:
        grads[n] = grads[n].reshape(weights[n].shape)

    landed0 = _chips_wait(*flight3[:3], grads["ada_w"], scatter=True, name="reduce0_wait")
    totals = _reduce_chips([], list(landed0) + list(landed1), name="reduce_chips")
    for n, t in zip(("e_w_in", "pool_w", "e_w_out", "gmlp_ws", "o_w_in", "mla_w_uq", "o_w_out"), totals):
        if n != "gmlp_ws":
            grads[n] = t.reshape(weights[n].shape)
    rep = jnp.concatenate([t.reshape(-1, LANES) for t in (totals[3], totals[7], totals[8])])
    rep_land = lax.dynamic_update_slice(lax.empty((N_CHIPS,) + rep.shape, F32), rep[None], (chip, 0, 0))
    flight4 = _chips_start([rep], [rep_land], totals[0], scatter=False, name="gather_rep_start")

    delta, new_m, new_v = {}, {}, {}
    replicated = ("gmlp_ws", "mla_w_uk", "mla_w_uv")
    large = [n for n in WEIGHTS if n not in SMALL and n != "ada_b"]
    for n in large:
        if n not in replicated:
            delta[n], new_m[n], new_v[n] = _adamw(weights[n], grads[n], mom[n], var[n], after=flight4[3], name="adamw_" + n)
    rep = _chips_wait(*flight4[:3], delta["e_w_in"], scatter=False, name="gather_rep_wait")[0]
    r_ws, r_uk = GMLP_BLOCK, 4 * MLA_KV_RANK
    grads["gmlp_ws"] = rep[:, :r_ws].reshape(weights["gmlp_ws"].shape)
    grads["mla_w_uk"] = jnp.transpose(rep[:, r_ws:r_ws + r_uk].reshape(MLA_HEADS, MLA_KV_RANK, MLA_NOPE), (1, 0, 2))[None]
    grads["mla_w_uv"] = jnp.transpose(rep[:, r_ws + r_uk:].reshape(MLA_HEADS, MLA_KV_RANK, MLA_V), (1, 0, 2))[None]
    for n in replicated:
        delta[n], new_m[n], new_v[n] = _adamw(weights[n], grads[n], mom[n], var[n], name="adamw_" + n)
    small = [n for n in WEIGHTS if n not in large]
    ds, ms, vs = _adamw_small([weights[n] for n in small], [grads[n] for n in small], [mom[n] for n in small],
                              [var[n] for n in small], name="adamw_small")
    for n, dn, mn, vn in zip(small, ds, ms, vs):
        delta[n], new_m[n], new_v[n] = dn, mn, vn

    loss_total = lax.psum(loss[0, 0], ("x", "y", "c"))
    return (loss_total, grad_x[None], *[grads[n] for n in WEIGHTS], *[delta[n] for n in WEIGHTS],
            *[new_m[n] for n in WEIGHTS], *[new_v[n] for n in WEIGHTS])
```

```python
import jax
import jax.numpy as jnp
from jax import lax
from jax.experimental import pallas as pl
from jax.experimental.pallas import tpu as pltpu

F32 = jnp.float32
BF16 = jnp.bfloat16
MESH = pl.DeviceIdType.MESH

D_MODEL = 1024
CHUNK = 64
LN_EPS = 1e-5
GMLP_HEADS = 4
GMLP_HEAD_DIM = 256
GMLP_BLOCK = 128
POOL_WINDOWS = (2, 4, 8, 16)
POOL_GROUPS = 4
POOL_GROUP_DIM = 256
POOL_HALO = 16
EVEN_IN = 5120
MLA_HEADS = 16
MLA_NOPE = 128
MLA_ROPE = 64
MLA_V = 128
MLA_Q_RANK = 256
MLA_KV_RANK = 128
MLA_WIDTH = MLA_HEADS * MLA_V
ODD_IN = 2496
ODD_SMALL = MLA_Q_RANK + MLA_KV_RANK + MLA_ROPE
ODD_SMALL_PAD = 512
QK_PAD = 256
ROPE_THETA = 10000.0
ATTN_SCALE = (MLA_NOPE + MLA_ROPE) ** -0.5
DEEPNORM_ALPHA = (2.0 * 2) ** 0.25
ADAM_LR = 0.001
ADAM_B1 = 0.9
ADAM_B2 = 0.999
ADAM_EPS = 1e-08
ADAM_WD = 0.01
ADAM_STEP = 10
NEG = -1e30
LANES = 128
N_DEV = 8
N_CHIPS = 4
VMEM_LIMIT_BYTES = 56 * 1024 * 1024
HBM = pl.BlockSpec(memory_space=pltpu.HBM)
VMEM = pl.BlockSpec(memory_space=pltpu.VMEM)


def _params(*sem):
    return pltpu.CompilerParams(dimension_semantics=sem if sem else None, vmem_limit_bytes=VMEM_LIMIT_BYTES)


def _tile(dim, pref):
    for t in (pref, 2048, 1280, 1024, 512, 256, 128):
        if t <= min(pref, dim) and dim % t == 0:
            return t
    return dim


def _sigmoid(z):
    return 1.0 / (1.0 + jnp.exp(-z))


def _dot(a, b, dims):
    return lax.dot_general(a, b, (dims, ((), ())), preferred_element_type=F32)


NN = ((1,), (0,))
NT = ((1,), (1,))
TN = ((0,), (0,))


def _matmul(a, b, *, name, trans_a=False, trans_b=False, out_dtype=F32, b_stacked=False, out_stacked=False,
            tm=512, tn=1024, tk=2048, after=None):
    k, m = a.shape if trans_a else a.shape[::-1]
    if b_stacked:
        ns, kb, n_sh = b.shape
        kb, n = (ns * n_sh, kb) if trans_b else (kb, ns * n_sh)
    else:
        n, kb = b.shape if trans_b else b.shape[::-1]
    assert k == kb, (a.shape, b.shape)
    tm = _tile(m, tm)
    if b_stacked and trans_b:
        tn, tk = _tile(n, tn), n_sh
    elif b_stacked or out_stacked:
        tn, tk = _tile(n // N_CHIPS, tn), _tile(k, tk)
    else:
        tn, tk = _tile(n, tn), _tile(k, tk)
    nk = k // tk
    per = max((n // N_CHIPS) // tn, 1)
    dims = ((0 if trans_a else 1,), (1 if trans_b else 0,))

    def body_one(a_ref, b_ref, *rest):
        o_ref = rest[-1]
        o_ref[...] = _dot(a_ref[...].astype(BF16), b_ref[...].astype(BF16), dims).astype(out_dtype)

    def body_acc(a_ref, b_ref, *rest):
        o_ref, acc_ref = rest[-2:]
        kk = pl.program_id(2)

        @pl.when(kk == 0)
        def _():
            acc_ref[...] = jnp.zeros_like(acc_ref)

        acc_ref[...] += _dot(a_ref[...].astype(BF16), b_ref[...].astype(BF16), dims)

        @pl.when(kk == nk - 1)
        def _():
            o_ref[...] = acc_ref[...].astype(out_dtype)

    a_spec = pl.BlockSpec((tk, tm), lambda i, j, kk: (kk, i)) if trans_a else pl.BlockSpec((tm, tk), lambda i, j, kk: (i, kk))
    if b_stacked and trans_b:
        b_spec = pl.BlockSpec((None, tn, tk), lambda i, j, kk: (kk, j, 0))
    elif b_stacked:
        b_spec = pl.BlockSpec((None, tk, tn), lambda i, j, kk: (j // per, kk, j % per))
    elif trans_b:
        b_spec = pl.BlockSpec((tn, tk), lambda i, j, kk: (j, kk))
    else:
        b_spec = pl.BlockSpec((tk, tn), lambda i, j, kk: (kk, j))
    if out_stacked:
        o_spec = pl.BlockSpec((None, tm, tn), lambda i, j, kk: (j // per, i, j % per))
        o_shape = jax.ShapeDtypeStruct((N_CHIPS, m, n // N_CHIPS), out_dtype)
    else:
        o_spec = pl.BlockSpec((tm, tn), lambda i, j, kk: (i, j))
        o_shape = jax.ShapeDtypeStruct((m, n), out_dtype)
    order = [] if after is None else [after]
    return pl.pallas_call(
        body_one if nk == 1 else body_acc, name=name, grid=(m // tm, n // tn, nk),
        in_specs=[a_spec, b_spec] + [pl.BlockSpec(memory_space=pl.ANY)] * len(order),
        out_specs=o_spec, out_shape=o_shape, scratch_shapes=[] if nk == 1 else [pltpu.VMEM((tm, tn), F32)],
        compiler_params=_params("parallel", "parallel", "arbitrary"),
    )(a, b, *order)


def _row_spec(ts, d):
    return pl.BlockSpec((ts, d), lambda i: (i, 0))


def _vec_spec(d):
    return pl.BlockSpec((1, d), lambda i: (0, 0))


def _modulate(x, scale, shift, *, name):
    s, d = x.shape
    ts = _tile(s, 512)

    def body(x_ref, sc_ref, sh_ref, h_ref):
        h_ref[...] = (x_ref[...] * (1.0 + sc_ref[...]) + sh_ref[...]).astype(BF16)

    return pl.pallas_call(
        body, name=name, grid=(s // ts,), in_specs=[_row_spec(ts, d), _vec_spec(d), _vec_spec(d)],
        out_specs=_row_spec(ts, d), out_shape=jax.ShapeDtypeStruct((s, d), BF16), compiler_params=_params("parallel"),
    )(x, scale, shift)


def _ln_stats(pre):
    mu = jnp.mean(pre, axis=-1, keepdims=True)
    xc = pre - mu
    var = jnp.mean(xc * xc, axis=-1, keepdims=True)
    rstd = lax.rsqrt(var + LN_EPS)
    return xc * rstd, rstd


def _ln_bwd_rows(dout, xhat, rstd, g):
    dxh = dout * g
    m1 = jnp.mean(dxh, axis=-1, keepdims=True)
    m2 = jnp.mean(dxh * xhat, axis=-1, keepdims=True)
    return rstd * (dxh - m1 - xhat * m2)


def _colsum(v):
    return jnp.sum(v, axis=0, keepdims=True)


def _resid_ln_modulate(x, y, gate, g, b, scale_next, shift_next, *, name):
    s, d = x.shape
    ts = _tile(s, 512)

    def body(x_ref, y_ref, gate_ref, g_ref, b_ref, sc_ref, sh_ref, xn_ref, h_ref):
        pre = DEEPNORM_ALPHA * x_ref[...] + (1.0 + gate_ref[...]) * y_ref[...]
        xhat, _ = _ln_stats(pre)
        xn = xhat * g_ref[...] + b_ref[...]
        xn_ref[...] = xn
        h_ref[...] = (xn * (1.0 + sc_ref[...]) + sh_ref[...]).astype(BF16)

    return pl.pallas_call(
        body, name=name, grid=(s // ts,),
        in_specs=[_row_spec(ts, d), _row_spec(ts, d)] + [_vec_spec(d)] * 5,
        out_specs=[_row_spec(ts, d), _row_spec(ts, d)],
        out_shape=[jax.ShapeDtypeStruct((s, d), F32), jax.ShapeDtypeStruct((s, d), BF16)],
        compiler_params=_params("parallel"),
    )(x, y, gate, g, b, scale_next, shift_next)


def _loss_ln_bwd(x, y, gate, g, b, target, *, name):
    s, d = x.shape
    ts = _tile(s, 512)

    def body(x_ref, y_ref, gate_ref, g_ref, b_ref, t_ref, dy_ref, dres_ref, dg_ref, db_ref, dgate_ref, loss_ref):
        @pl.when(pl.program_id(0) == 0)
        def _():
            for r in (dg_ref, db_ref, dgate_ref, loss_ref):
                r[...] = jnp.zeros_like(r)

        yv = y_ref[...]
        pre = DEEPNORM_ALPHA * x_ref[...] + (1.0 + gate_ref[...]) * yv
        xhat, rstd = _ln_stats(pre)
        diff = xhat * g_ref[...] + b_ref[...] - t_ref[...]
        loss_ref[...] += (0.5 / d) * jnp.sum(jnp.sum(diff * diff, axis=1, keepdims=True), axis=0, keepdims=True)
        dout = diff * (1.0 / d)
        dpre = _ln_bwd_rows(dout, xhat, rstd, g_ref[...])
        dy_ref[...] = (dpre * (1.0 + gate_ref[...])).astype(BF16)
        dres_ref[...] = DEEPNORM_ALPHA * dpre
        dg_ref[...] += _colsum(dout * xhat)
        db_ref[...] += _colsum(dout)
        dgate_ref[...] += _colsum(dpre * yv)

    vec = jax.ShapeDtypeStruct((1, d), F32)
    return pl.pallas_call(
        body, name=name, grid=(s // ts,),
        in_specs=[_row_spec(ts, d), _row_spec(ts, d), _vec_spec(d), _vec_spec(d), _vec_spec(d), _row_spec(ts, d)],
        out_specs=[_row_spec(ts, d), _row_spec(ts, d), _vec_spec(d), _vec_spec(d), _vec_spec(d), _vec_spec(1)],
        out_shape=[jax.ShapeDtypeStruct((s, d), BF16), jax.ShapeDtypeStruct((s, d), F32), vec, vec, vec,
                   jax.ShapeDtypeStruct((1, 1), F32)],
        compiler_params=_params("arbitrary"),
    )(x, y, gate, g, b, target)


def _mid_ln_bwd(x, y, gate, g, b, dh_next, dres_next, scale_next, x_next, *, name):
    s, d = x.shape
    ts = _tile(s, 512)

    def body(x_ref, y_ref, gate_ref, g_ref, b_ref, dh_ref, dr_ref, sc_ref, xn_ref,
             dy_ref, dres_ref, dg_ref, db_ref, dgate_ref, dscale_ref, dshift_ref):
        @pl.when(pl.program_id(0) == 0)
        def _():
            for r in (dg_ref, db_ref, dgate_ref, dscale_ref, dshift_ref):
                r[...] = jnp.zeros_like(r)

        dh = dh_ref[...]
        dout = dr_ref[...] + dh * (1.0 + sc_ref[...])
        dscale_ref[...] += _colsum(dh * xn_ref[...])
        dshift_ref[...] += _colsum(dh)
        yv = y_ref[...]
        pre = DEEPNORM_ALPHA * x_ref[...] + (1.0 + gate_ref[...]) * yv
        xhat, rstd = _ln_stats(pre)
        dpre = _ln_bwd_rows(dout, xhat, rstd, g_ref[...])
        dy_ref[...] = (dpre * (1.0 + gate_ref[...])).astype(BF16)
        dres_ref[...] = DEEPNORM_ALPHA * dpre
        dg_ref[...] += _colsum(dout * xhat)
        db_ref[...] += _colsum(dout)
        dgate_ref[...] += _colsum(dpre * yv)

    vec = jax.ShapeDtypeStruct((1, d), F32)
    return pl.pallas_call(
        body, name=name, grid=(s // ts,),
        in_specs=[_row_spec(ts, d), _row_spec(ts, d), _vec_spec(d), _vec_spec(d), _vec_spec(d),
                  _row_spec(ts, d), _row_spec(ts, d), _vec_spec(d), _row_spec(ts, d)],
        out_specs=[_row_spec(ts, d), _row_spec(ts, d)] + [_vec_spec(d)] * 5,
        out_shape=[jax.ShapeDtypeStruct((s, d), BF16), jax.ShapeDtypeStruct((s, d), F32)] + [vec] * 5,
        compiler_params=_params("arbitrary"),
    )(x, y, gate, g, b, dh_next, dres_next, scale_next, x_next)


def _input_bwd(x, dh, dres, scale, *, name):
    s, d = x.shape
    ts = _tile(s, 512)

    def body(x_ref, dh_ref, dr_ref, sc_ref, dx_ref, dscale_ref, dshift_ref):
        @pl.when(pl.program_id(0) == 0)
        def _():
            dscale_ref[...] = jnp.zeros_like(dscale_ref)
            dshift_ref[...] = jnp.zeros_like(dshift_ref)

        dh = dh_ref[...]
        dx_ref[...] = dr_ref[...] + dh * (1.0 + sc_ref[...])
        dscale_ref[...] += _colsum(dh * x_ref[...])
        dshift_ref[...] += _colsum(dh)

    vec = jax.ShapeDtypeStruct((1, d), F32)
    return pl.pallas_call(
        body, name=name, grid=(s // ts,),
        in_specs=[_row_spec(ts, d), _row_spec(ts, d), _row_spec(ts, d), _vec_spec(d)],
        out_specs=[_row_spec(ts, d), _vec_spec(d), _vec_spec(d)],
        out_shape=[jax.ShapeDtypeStruct((s, d), F32), vec, vec],
        compiler_params=_params("arbitrary"),
    )(x, dh, dres, scale)


def _chunk_mask(transposed=False):
    r = lax.broadcasted_iota(jnp.int32, (GMLP_BLOCK, GMLP_BLOCK), 0) // CHUNK
    c = lax.broadcasted_iota(jnp.int32, (GMLP_BLOCK, GMLP_BLOCK), 1) // CHUNK
    return (r <= c) if transposed else (c <= r)


def _window_sum(ext, steps, forward):
    rows = ext.shape[0]
    acc = ext
    for k in range(steps):
        shift = 1 << k
        acc = acc + pltpu.roll(acc, (rows - shift) if forward else shift, 0)
    return acc


def _pool_counts(first_row, rows, win):
    t = first_row + lax.broadcasted_iota(jnp.int32, (rows, 1), 0)
    return jnp.minimum(t + 1, win).astype(F32)


def _even_specs(t):
    col = lambda j: pl.BlockSpec((t, D_MODEL), lambda n: (n, j))
    per = t // POOL_HALO
    prev = pl.BlockSpec((POOL_HALO, D_MODEL), lambda n: (jnp.maximum(n * per - 1, 0), 3))
    return col, per, prev


def _full(shape):
    return pl.BlockSpec(shape, lambda n: (0,) * len(shape))


def _gmlp_head(v_h, ng, nb, w_bf):
    xhat, rstd = _ln_stats(v_h)
    vn = (xhat * ng + nb).astype(BF16)
    return xhat, rstd, vn, _dot(w_bf, vn, NN)


def _pool_group(xb_g, prev_g, first_row, grp):
    t = xb_g.shape[0]
    ext = jnp.concatenate([prev_g, xb_g], axis=0)
    tot = _window_sum(ext, grp + 1, False)[POOL_HALO:, :]
    cnt = _pool_counts(first_row, t, POOL_WINDOWS[grp])
    return tot / cnt - xb_g, cnt


def _even_fwd(proj, ws, bs_t, ng, nb, pool_w, pool_b, pool_scale, *, name):
    s = proj.shape[0]
    t = GMLP_BLOCK
    col, per, prev = _even_specs(t)

    def body(u_ref, v_ref, za_ref, xb_ref, zb_ref, xp_ref, ws_ref, bs_ref, ng_ref, nb_ref, pw_ref, pb_ref, ps_ref, o_ref):
        n = pl.program_id(0)
        mask = _chunk_mask()
        for h in range(GMLP_HEADS):
            c0 = h * GMLP_HEAD_DIM
            cs = slice(c0, c0 + GMLP_HEAD_DIM)
            w_bf = jnp.where(mask, ws_ref[h], 0.0).astype(BF16)
            _, _, _, sv = _gmlp_head(v_ref[:, cs], ng_ref[...], nb_ref[...], w_bf)
            sv = sv + bs_ref[:, h:h + 1]
            za = za_ref[:, cs]
            o_ref[:, cs] = (u_ref[:, cs] * sv * (za * _sigmoid(za))).astype(BF16)
        live = (n > 0).astype(F32)
        for grp in range(POOL_GROUPS):
            c0 = grp * POOL_GROUP_DIM
            cs = slice(c0, c0 + POOL_GROUP_DIM)
            pooled, _ = _pool_group(xb_ref[:, cs], xp_ref[:, cs] * live, n * t, grp)
            yb = _dot(pooled.astype(BF16), pw_ref[grp], NN) + pb_ref[:, cs]
            zb = zb_ref[:, cs]
            o_ref[:, D_MODEL + c0:D_MODEL + c0 + POOL_GROUP_DIM] = (yb * ps_ref[:, cs] * (zb * _sigmoid(zb))).astype(BF16)

    return pl.pallas_call(
        body, name=name, grid=(s // t,),
        in_specs=[col(0), col(1), col(2), col(3), col(4), prev,
                  _full((GMLP_HEADS, t, t)), _full((t, LANES)), _full((1, GMLP_HEAD_DIM)), _full((1, GMLP_HEAD_DIM)),
                  _full((POOL_GROUPS, POOL_GROUP_DIM, POOL_GROUP_DIM)), _full((1, D_MODEL)), _full((1, D_MODEL))],
        out_specs=pl.BlockSpec((t, 2 * D_MODEL), lambda n: (n, 0)),
        out_shape=jax.ShapeDtypeStruct((s, 2 * D_MODEL), BF16),
        compiler_params=_params("parallel"),
    )(proj, proj, proj, proj, proj, proj, ws, bs_t, ng, nb, pool_w, pool_b, pool_scale)


def _even_bwd(proj, dmix, ws, ws_t, bs_t, ng, nb, pool_w, pool_b, pool_scale, *, name):
    s = proj.shape[0]
    t = GMLP_BLOCK
    nblk = s // t
    col, per, prev = _even_specs(t)
    nxt = lambda j: pl.BlockSpec((POOL_HALO, D_MODEL), lambda n: (jnp.minimum((n + 1) * per, nblk * per - 1), j))

    def body(u_ref, v_ref, za_ref, xb_ref, zb_ref, xp_ref, zn_ref, da_ref, db_ref, dbn_ref,
             ws_ref, wst_ref, bs_ref, ng_ref, nb_ref, pw_ref, pb_ref, ps_ref,
             dp_ref, gws_ref, gbs_ref, gng_ref, gnb_ref, gpw_ref, gpb_ref, gps_ref):
        n = pl.program_id(0)

        @pl.when(n == 0)
        def _():
            for r in (gws_ref, gbs_ref, gng_ref, gnb_ref, gpw_ref, gpb_ref, gps_ref):
                r[...] = jnp.zeros_like(r)

        mask, mask_t = _chunk_mask(), _chunk_mask(True)
        lane = lax.broadcasted_iota(jnp.int32, (t, LANES), 1)
        ngv, nbv = ng_ref[...], nb_ref[...]
        for h in range(GMLP_HEADS):
            c0 = h * GMLP_HEAD_DIM
            cs = slice(c0, c0 + GMLP_HEAD_DIM)
            w_bf = jnp.where(mask, ws_ref[h], 0.0).astype(BF16)
            wt_bf = jnp.where(mask_t, wst_ref[h], 0.0).astype(BF16)
            xhat, rstd, vn, sv = _gmlp_head(v_ref[:, cs], ngv, nbv, w_bf)
            sv = sv + bs_ref[:, h:h + 1]
            za, u, da = za_ref[:, cs], u_ref[:, cs], da_ref[:, cs]
            sg = _sigmoid(za)
            sl = za * sg
            dp_ref[:, cs] = (da * sv * sl).astype(BF16)
            dp_ref[:, 2 * D_MODEL + c0:2 * D_MODEL + c0 + GMLP_HEAD_DIM] = (
                da * u * sv * (sg * (1.0 + za * (1.0 - sg)))).astype(BF16)
            dsv = da * u * sl
            gbs_ref[...] += jnp.where(lane == h, jnp.sum(dsv, axis=1, keepdims=True), 0.0)
            dsv_bf = dsv.astype(BF16)
            gws_ref[h] += jnp.where(mask, _dot(dsv_bf, vn, NT), 0.0)
            dvn = _dot(wt_bf, dsv_bf, NN)
            dp_ref[:, D_MODEL + c0:D_MODEL + c0 + GMLP_HEAD_DIM] = _ln_bwd_rows(dvn, xhat, rstd, ngv).astype(BF16)
            gng_ref[...] += _colsum(dvn * xhat)
            gnb_ref[...] += _colsum(dvn)
        live_prev = (n > 0).astype(F32)
        live_next = (n < nblk - 1).astype(F32)
        for grp in range(POOL_GROUPS):
            c0 = grp * POOL_GROUP_DIM
            cs = slice(c0, c0 + POOL_GROUP_DIM)
            xb = xb_ref[:, cs]
            pooled, cnt = _pool_group(xb, xp_ref[:, cs] * live_prev, n * t, grp)
            pooled_bf = pooled.astype(BF16)
            pw = pw_ref[grp]
            yb = _dot(pooled_bf, pw, NN) + pb_ref[:, cs]
            ps = ps_ref[:, cs]
            zb, db = zb_ref[:, cs], db_ref[:, cs]
            sg = _sigmoid(zb)
            sl = zb * sg
            dp_ref[:, 4 * D_MODEL + c0:4 * D_MODEL + c0 + POOL_GROUP_DIM] = (
                db * yb * ps * (sg * (1.0 + zb * (1.0 - sg)))).astype(BF16)
            dsl = db * sl
            dy = dsl * ps
            gps_ref[:, cs] += _colsum(dsl * yb)
            gpb_ref[:, cs] += _colsum(dy)
            dy_bf = dy.astype(BF16)
            gpw_ref[grp] += _dot(pooled_bf, dy_bf, TN)
            r = _dot(dy_bf, pw, NT)
            zn = zn_ref[:, cs]
            dyn = (dbn_ref[:, cs] * (zn * _sigmoid(zn)) * ps * live_next).astype(BF16)
            rn = _dot(dyn, pw, NT) / _pool_counts((n + 1) * t, POOL_HALO, POOL_WINDOWS[grp])
            ext = jnp.concatenate([r / cnt, rn], axis=0)
            dxb = _window_sum(ext, grp + 1, True)[:t, :] - r
            dp_ref[:, 3 * D_MODEL + c0:3 * D_MODEL + c0 + POOL_GROUP_DIM] = dxb.astype(BF16)

    out_shape = [
        jax.ShapeDtypeStruct((s, EVEN_IN), BF16),
        jax.ShapeDtypeStruct((GMLP_HEADS, t, t), F32), jax.ShapeDtypeStruct((t, LANES), F32),
        jax.ShapeDtypeStruct((1, GMLP_HEAD_DIM), F32), jax.ShapeDtypeStruct((1, GMLP_HEAD_DIM), F32),
        jax.ShapeDtypeStruct((POOL_GROUPS, POOL_GROUP_DIM, POOL_GROUP_DIM), F32),
        jax.ShapeDtypeStruct((1, D_MODEL), F32), jax.ShapeDtypeStruct((1, D_MODEL), F32),
    ]
    return pl.pallas_call(
        body, name=name, grid=(nblk,),
        in_specs=[col(0), col(1), col(2), col(3), col(4), prev, nxt(4),
                  pl.BlockSpec((t, D_MODEL), lambda n: (n, 0)), pl.BlockSpec((t, D_MODEL), lambda n: (n, 1)), nxt(1),
                  _full((GMLP_HEADS, t, t)), _full((GMLP_HEADS, t, t)), _full((t, LANES)),
                  _full((1, GMLP_HEAD_DIM)), _full((1, GMLP_HEAD_DIM)),
                  _full((POOL_GROUPS, POOL_GROUP_DIM, POOL_GROUP_DIM)), _full((1, D_MODEL)), _full((1, D_MODEL))],
        out_specs=[pl.BlockSpec((t, EVEN_IN), lambda n: (n, 0))] + [_full(o.shape) for o in out_shape[1:]],
        out_shape=out_shape,
        compiler_params=_params("arbitrary"),
    )(proj, proj, proj, proj, proj, proj, proj, dmix, dmix, dmix, ws, ws_t, bs_t, ng, nb, pool_w, pool_b, pool_scale)


def _half_swap(v):
    lane = lax.broadcasted_iota(jnp.int32, v.shape, 1)
    return jnp.where(lane % MLA_ROPE < MLA_ROPE // 2, pltpu.roll(v, LANES - MLA_ROPE // 2, 1), pltpu.roll(v, MLA_ROPE // 2, 1))


def _rope(v, cos, sin_signed):
    return v * cos + _half_swap(v) * sin_signed


def _rope_bwd(d, cos, sin_signed):
    return d * cos + _half_swap(d * sin_signed)


def _rms(v, g):
    r = lax.rsqrt(jnp.mean(v * v, axis=-1, keepdims=True) + LN_EPS)
    return v * r * g, r


def _rms_bwd(dy, v, r, g):
    u = dy * g
    return r * u - v * (r * r * r) * jnp.mean(u * v, axis=-1, keepdims=True)


def _mla_prep(proj, gq, gkv, cos, sin_signed, *, name):
    s = proj.shape[0]
    ts = _tile(s, 512)

    def body(p_ref, gq_ref, gkv_ref, c_ref, s_ref, q_ref, k_ref):
        qcn, _ = _rms(p_ref[:, :MLA_Q_RANK], gq_ref[...])
        kvn, _ = _rms(p_ref[:, MLA_Q_RANK:MLA_Q_RANK + MLA_KV_RANK], gkv_ref[...])
        kr = _rope(p_ref[:, MLA_Q_RANK + MLA_KV_RANK:], c_ref[...], s_ref[...])
        q_ref[...] = qcn.astype(BF16)
        k_ref[...] = jnp.concatenate([kvn, kr], axis=1).astype(BF16)

    return pl.pallas_call(
        body, name=name, grid=(s // ts,),
        in_specs=[_row_spec(ts, ODD_SMALL_PAD), _vec_spec(MLA_Q_RANK), _vec_spec(MLA_KV_RANK), _row_spec(ts, LANES), _row_spec(ts, LANES)],
        out_specs=[_row_spec(ts, MLA_Q_RANK), _row_spec(ts, QK_PAD)],
        out_shape=[jax.ShapeDtypeStruct((s, MLA_Q_RANK), BF16), jax.ShapeDtypeStruct((s, QK_PAD), BF16)],
        compiler_params=_params("parallel"),
    )(proj, gq, gkv, cos, sin_signed)


def _mla_prep_bwd(proj, dqcn, dkv, gq, gkv, cos, sin_signed, *, name):
    s = proj.shape[0]
    ts = _tile(s, 512)

    def body(p_ref, dq_ref, dkv_ref, gq_ref, gkv_ref, c_ref, s_ref, ds_ref, ggq_ref, ggkv_ref):
        @pl.when(pl.program_id(0) == 0)
        def _():
            ggq_ref[...] = jnp.zeros_like(ggq_ref)
            ggkv_ref[...] = jnp.zeros_like(ggkv_ref)

        qc = p_ref[:, :MLA_Q_RANK]
        kvc = p_ref[:, MLA_Q_RANK:MLA_Q_RANK + MLA_KV_RANK]
        _, rq = _rms(qc, gq_ref[...])
        _, rkv = _rms(kvc, gkv_ref[...])
        dq = dq_ref[...]
        dkvn = dkv_ref[:, :MLA_KV_RANK]
        ggq_ref[...] += _colsum(dq * qc * rq)
        ggkv_ref[...] += _colsum(dkvn * kvc * rkv)
        dkr = _rope_bwd(dkv_ref[:, MLA_KV_RANK:], c_ref[...], s_ref[...])
        ds_ref[...] = jnp.concatenate(
            [_rms_bwd(dq, qc, rq, gq_ref[...]), _rms_bwd(dkvn, kvc, rkv, gkv_ref[...]), dkr], axis=1).astype(BF16)

    return pl.pallas_call(
        body, name=name, grid=(s // ts,),
        in_specs=[_row_spec(ts, ODD_SMALL_PAD), _row_spec(ts, MLA_Q_RANK), _row_spec(ts, QK_PAD),
                  _vec_spec(MLA_Q_RANK), _vec_spec(MLA_KV_RANK), _row_spec(ts, LANES), _row_spec(ts, LANES)],
        out_specs=[_row_spec(ts, ODD_SMALL_PAD), _vec_spec(MLA_Q_RANK), _vec_spec(MLA_KV_RANK)],
        out_shape=[jax.ShapeDtypeStruct((s, ODD_SMALL_PAD), BF16), jax.ShapeDtypeStruct((1, MLA_Q_RANK), F32),
                   jax.ShapeDtypeStruct((1, MLA_KV_RANK), F32)],
        compiler_params=_params("arbitrary"),
    )(proj, dqcn, dkv, gq, gkv, cos, sin_signed)


LOG2_E = 1.4426950408889634
Q_PRESCALE = ATTN_SCALE * LOG2_E


def _q_build(q_nope, q_rope_pre, wuk_hdr, cos, sin_signed, *, name):
    s = q_nope.shape[0]
    ts = _tile(s, 1024)

    def body(qn_ref, qr_ref, w_ref, c_ref, s_ref, o_ref):
        r = _rope(qr_ref[...], c_ref[...], s_ref[...])
        lane = lax.broadcasted_iota(jnp.int32, (ts, LANES), 1)
        for j in range(2):
            ql = _dot(qn_ref[:, j * MLA_NOPE:(j + 1) * MLA_NOPE], w_ref[j], NN)
            rr = r if j == 0 else pltpu.roll(r, MLA_ROPE, 1)
            o_ref[j] = (jnp.concatenate([ql, jnp.where(lane < MLA_ROPE, rr, 0.0)], axis=1) * Q_PRESCALE).astype(BF16)

    return pl.pallas_call(
        body, name=name, grid=(s // ts, MLA_HEADS // 2),
        in_specs=[pl.BlockSpec((ts, 2 * MLA_NOPE), lambda i, p: (i, p)), pl.BlockSpec((ts, LANES), lambda i, p: (i, p)),
                  pl.BlockSpec((2, MLA_NOPE, MLA_KV_RANK), lambda i, p: (p, 0, 0)),
                  pl.BlockSpec((ts, LANES), lambda i, p: (i, 0)), pl.BlockSpec((ts, LANES), lambda i, p: (i, 0))],
        out_specs=pl.BlockSpec((2, ts, QK_PAD), lambda i, p: (p, i, 0)),
        out_shape=jax.ShapeDtypeStruct((MLA_HEADS, s, QK_PAD), BF16),
        compiler_params=_params("parallel", "parallel"),
    )(q_nope, q_rope_pre, wuk_hdr, cos, sin_signed)


def _q_bwd(dq, q_nope, wuk_hrd, cos, sin_signed, *, name):
    s = q_nope.shape[0]
    ts = _tile(s, 1024)

    def body(dq_ref, qn_ref, w_ref, c_ref, s_ref, dn_ref, dr_ref, gw_ref):
        @pl.when(pl.program_id(1) == 0)
        def _():
            gw_ref[...] = jnp.zeros_like(gw_ref)

        lane = lax.broadcasted_iota(jnp.int32, (ts, LANES), 1)
        for j in range(2):
            dql = dq_ref[j, :, :MLA_KV_RANK]
            dn_ref[:, j * MLA_NOPE:(j + 1) * MLA_NOPE] = _dot(dql, w_ref[j], NN).astype(BF16)
            gw_ref[j] += _dot(dql, qn_ref[:, j * MLA_NOPE:(j + 1) * MLA_NOPE], TN)
        hi0 = dq_ref[0, :, MLA_KV_RANK:].astype(F32)
        hi1 = dq_ref[1, :, MLA_KV_RANK:].astype(F32)
        d = jnp.where(lane < MLA_ROPE, hi0, pltpu.roll(hi1, MLA_ROPE, 1))
        dr_ref[...] = _rope_bwd(d, c_ref[...], s_ref[...]).astype(BF16)

    return pl.pallas_call(
        body, name=name, grid=(MLA_HEADS // 2, s // ts),
        in_specs=[pl.BlockSpec((2, ts, QK_PAD), lambda p, i: (p, i, 0)), pl.BlockSpec((ts, 2 * MLA_NOPE), lambda p, i: (i, p)),
                  pl.BlockSpec((2, MLA_KV_RANK, MLA_NOPE), lambda p, i: (p, 0, 0)),
                  pl.BlockSpec((ts, LANES), lambda p, i: (i, 0)), pl.BlockSpec((ts, LANES), lambda p, i: (i, 0))],
        out_specs=[pl.BlockSpec((ts, 2 * MLA_NOPE), lambda p, i: (i, p)), pl.BlockSpec((ts, LANES), lambda p, i: (i, p)),
                   pl.BlockSpec((2, MLA_KV_RANK, MLA_NOPE), lambda p, i: (p, 0, 0))],
        out_shape=[jax.ShapeDtypeStruct((s, MLA_HEADS * MLA_NOPE), BF16), jax.ShapeDtypeStruct((s, MLA_HEADS * MLA_ROPE), BF16),
                   jax.ShapeDtypeStruct((MLA_HEADS, MLA_KV_RANK, MLA_NOPE), F32)],
        compiler_params=_params("parallel", "arbitrary"),
    )(dq, q_nope, wuk_hrd, cos, sin_signed)


ATTN_BQ = 128
ATTN_BK = 512


def _diag_mask(rows, bq, bk, q0, k0):
    qc = (q0 + lax.broadcasted_iota(jnp.int32, (rows, bk), 0) % bq) // CHUNK
    kc = (k0 + lax.broadcasted_iota(jnp.int32, (rows, bk), 1)) // CHUNK
    return kc <= qc


def _attn_fwd(q, k, *, name):
    nh, s, dk = q.shape
    bq, bk = _tile(s, ATTN_BQ), _tile(s, ATTN_BK)
    rows = nh * bq

    def body(q_ref, k_ref, o_ref, lse_ref):
        i = pl.program_id(0)
        qb = q_ref[...].reshape(rows, dk)
        n_before = (i * bq) // bk

        def step(j, width, carry, masked):
            m, l, acc = carry
            k0 = pl.multiple_of(j * bk, bk)
            kb = k_ref[pl.ds(k0, width), :]
            sc = _dot(qb, kb, NT)
            if masked:
                sc = jnp.where(_diag_mask(rows, bq, width, i * bq, k0), sc, NEG)
            m_new = jnp.maximum(m, jnp.max(sc, axis=1, keepdims=True))
            p = jnp.exp2(sc - m_new)
            a = jnp.exp2(m - m_new)
            l = a * l + jnp.sum(p, axis=1, keepdims=True)
            acc = a * acc + _dot(p.astype(BF16), kb[:, :MLA_KV_RANK], NN)
            return m_new, l, acc

        init = (jnp.full((rows, 1), NEG, F32), jnp.zeros((rows, 1), F32), jnp.zeros((rows, MLA_KV_RANK), F32))
        carry = lax.fori_loop(0, n_before, lambda j, c: step(j, bk, c, False), init)
        for part in range(bk // bq):
            @pl.when(i % (bk // bq) == part)
            def _(part=part):
                m, l, acc = step(n_before, (part + 1) * bq, carry, True)
                o_ref[...] = (acc / l).astype(BF16).reshape(nh, bq, MLA_KV_RANK)
                lse_ref[...] = jnp.broadcast_to(m + jnp.log2(l), (rows, LANES)).reshape(nh, bq, LANES)

    return pl.pallas_call(
        body, name=name, grid=(s // bq,),
        in_specs=[pl.BlockSpec((nh, bq, dk), lambda i: (0, i, 0)), pl.BlockSpec((s, dk), lambda i: (0, 0))],
        out_specs=[pl.BlockSpec((nh, bq, MLA_KV_RANK), lambda i: (0, i, 0)), pl.BlockSpec((nh, bq, LANES), lambda i: (0, i, 0))],
        out_shape=[jax.ShapeDtypeStruct((nh, s, MLA_KV_RANK), BF16), jax.ShapeDtypeStruct((nh, s, LANES), F32)],
        compiler_params=_params("parallel"),
    )(q, k)


def _attn_bwd(q, k, do, o, lse, *, name):
    nh, s, dk = q.shape
    bq, bk = _tile(s, ATTN_BQ), _tile(s, ATTN_BK)
    rows = nh * bq

    def body(q_ref, k_ref, do_ref, o_ref, lse_ref, dq_ref, dkv_ref):
        i = pl.program_id(0)
        n_before = (i * bq) // bk

        @pl.when(i == 0)
        def _():
            dkv_ref[...] = jnp.zeros_like(dkv_ref)

        qb = q_ref[...].reshape(rows, dk)
        dob = do_ref[...].reshape(rows, MLA_KV_RANK)
        lse_b = lse_ref[...].reshape(rows, LANES)[:, :1]
        delta = jnp.sum(dob.astype(F32) * o_ref[...].reshape(rows, MLA_KV_RANK).astype(F32), axis=1, keepdims=True)

        def step(j, width, dq, masked):
            j0 = pl.multiple_of(j * bk, bk)
            kb = k_ref[pl.ds(j0, width), :]
            sc = _dot(qb, kb, NT)
            if masked:
                sc = jnp.where(_diag_mask(rows, bq, width, i * bq, j0), sc, NEG)
            p = jnp.exp2(sc - lse_b)
            dp = _dot(dob, kb[:, :MLA_KV_RANK], NT)
            ds_bf = (p * (dp - delta)).astype(BF16)
            dkv_ref[pl.ds(j0, width), :] += _dot(ds_bf, qb, TN) * (1.0 / LOG2_E)
            dkv_ref[pl.ds(j0, width), :MLA_KV_RANK] += _dot(p.astype(BF16), dob, TN)
            return dq + _dot(ds_bf, kb, NN)

        dq_before = lax.fori_loop(0, n_before, lambda j, c: step(j, bk, c, False), jnp.zeros((rows, dk), F32))
        for part in range(bk // bq):
            @pl.when(i % (bk // bq) == part)
            def _(part=part):
                dq = step(n_before, (part + 1) * bq, dq_before, True) * ATTN_SCALE
                dq_ref[...] = dq.astype(BF16).reshape(nh, bq, dk)

    blk = lambda w: pl.BlockSpec((nh, bq, w), lambda i: (0, i, 0))
    return pl.pallas_call(
        body, name=name, grid=(s // bq,),
        in_specs=[blk(dk), pl.BlockSpec((s, dk), lambda i: (0, 0)), blk(MLA_KV_RANK), blk(MLA_KV_RANK), blk(LANES)],
        out_specs=[blk(dk), pl.BlockSpec((s, dk), lambda i: (0, 0))],
        out_shape=[jax.ShapeDtypeStruct((nh, s, dk), BF16), jax.ShapeDtypeStruct((s, dk), F32)],
        compiler_params=_params("arbitrary"),
    )(q, k, do, o, lse)


HEAD_GROUP = 4


def _o_build(o_lat, wuv_hrv, proj, *, name):
    s = proj.shape[0]
    ts = _tile(s, 1024)
    w = HEAD_GROUP * MLA_V

    def body(ol_ref, w_ref, z_ref, og_ref):
        for j in range(HEAD_GROUP):
            cs = slice(j * MLA_V, (j + 1) * MLA_V)
            z = z_ref[:, cs]
            og_ref[:, cs] = (_dot(ol_ref[j], w_ref[j], NN) * (z * _sigmoid(z))).astype(BF16)

    return pl.pallas_call(
        body, name=name, grid=(s // ts, MLA_HEADS // HEAD_GROUP),
        in_specs=[pl.BlockSpec((HEAD_GROUP, ts, MLA_KV_RANK), lambda i, g: (g, i, 0)),
                  pl.BlockSpec((HEAD_GROUP, MLA_KV_RANK, MLA_V), lambda i, g: (g, 0, 0)),
                  pl.BlockSpec((ts, w), lambda i, g: (i, g + 1))],
        out_specs=pl.BlockSpec((ts, w), lambda i, g: (i, g)),
        out_shape=jax.ShapeDtypeStruct((s, MLA_WIDTH), BF16),
        compiler_params=_params("parallel", "parallel"),
    )(o_lat, wuv_hrv, proj)


def _o_bwd(dg, proj, o_lat, wuv_hrv, wuv_hvr, *, name):
    s = proj.shape[0]
    ts = _tile(s, 1024)
    w = HEAD_GROUP * MLA_V

    def body(dg_ref, z_ref, ol_ref, w_ref, wt_ref, dol_ref, dz_ref, gw_ref):
        @pl.when(pl.program_id(1) == 0)
        def _():
            gw_ref[...] = jnp.zeros_like(gw_ref)

        for j in range(HEAD_GROUP):
            cs = slice(j * MLA_V, (j + 1) * MLA_V)
            z, dgj, ol = z_ref[:, cs], dg_ref[:, cs], ol_ref[j]
            sg = _sigmoid(z)
            o = _dot(ol, w_ref[j], NN)
            dz_ref[:, cs] = (dgj * o * (sg * (1.0 + z * (1.0 - sg)))).astype(BF16)
            do_bf = (dgj * (z * sg)).astype(BF16)
            dol_ref[j] = _dot(do_bf, wt_ref[j], NN).astype(BF16)
            gw_ref[j] += _dot(ol, do_bf, TN)

    hs = lambda a, b: pl.BlockSpec((HEAD_GROUP, a, b), lambda g, i: (g, 0, 0))
    return pl.pallas_call(
        body, name=name, grid=(MLA_HEADS // HEAD_GROUP, s // ts),
        in_specs=[pl.BlockSpec((ts, w), lambda g, i: (i, g)), pl.BlockSpec((ts, w), lambda g, i: (i, g + 1)),
                  pl.BlockSpec((HEAD_GROUP, ts, MLA_KV_RANK), lambda g, i: (g, i, 0)),
                  hs(MLA_KV_RANK, MLA_V), hs(MLA_V, MLA_KV_RANK)],
        out_specs=[pl.BlockSpec((HEAD_GROUP, ts, MLA_KV_RANK), lambda g, i: (g, i, 0)),
                   pl.BlockSpec((ts, w), lambda g, i: (i, g)), hs(MLA_KV_RANK, MLA_V)],
        out_shape=[jax.ShapeDtypeStruct((MLA_HEADS, s, MLA_KV_RANK), BF16), jax.ShapeDtypeStruct((s, MLA_WIDTH), BF16),
                   jax.ShapeDtypeStruct((MLA_HEADS, MLA_KV_RANK, MLA_V), F32)],
        compiler_params=_params("parallel", "arbitrary"),
    )(dg, proj, o_lat, wuv_hrv, wuv_hvr)


def _ada_mod(c_all, ada_w, ada_b_sh, *, name):
    nl, _, cols = ada_w.shape

    def body(c_ref, w_ref, b_ref, o_ref):
        c = c_ref[...]
        cond = (c * _sigmoid(c)).astype(BF16)
        for l in range(nl):
            o_ref[l] = _dot(cond, w_ref[l].astype(BF16), NN) + b_ref[l]

    return pl.pallas_call(
        body, name=name, out_shape=jax.ShapeDtypeStruct((nl, c_all.shape[0], cols), F32),
        compiler_params=_params(),
    )(c_all, ada_w, ada_b_sh)


def _ada_grad(c_all_t, dmod_sh, *, name):
    nl, _, cols = dmod_sh.shape
    d = c_all_t.shape[0]

    def body(c_ref, dm_ref, gw_ref):
        c = c_ref[...]
        cond_t = c * _sigmoid(c)
        for l in range(nl):
            gw_ref[l] = lax.dot_general(cond_t, dm_ref[l], (NN, ((), ())), precision=lax.Precision.HIGHEST,
                                        preferred_element_type=F32)

    return pl.pallas_call(
        body, name=name, out_shape=jax.ShapeDtypeStruct((nl, d, cols), F32), compiler_params=_params(),
    )(c_all_t, dmod_sh)


def _sum_devices(parts, *, name):
    def body(p_ref, o_ref):
        acc = p_ref[0]
        for k in range(1, parts.shape[0]):
            acc = acc + p_ref[k]
        o_ref[...] = acc

    return pl.pallas_call(body, name=name, out_shape=jax.ShapeDtypeStruct(parts.shape[1:], F32), compiler_params=_params())(parts)


def _adamw_math(w, g, m, v):
    c1 = 1.0 - ADAM_B1 ** ADAM_STEP
    c2 = 1.0 - ADAM_B2 ** ADAM_STEP
    nm = ADAM_B1 * m + (1.0 - ADAM_B1) * g
    nv = ADAM_B2 * v + (1.0 - ADAM_B2) * (g * g)
    return -ADAM_LR * ((nm / c1) / (jnp.sqrt(nv / c2) + ADAM_EPS) + ADAM_WD * w), nm, nv


ADAMW_BLOCK_BYTES = 1 << 20


def _adamw(w, g, m, v, *, name, after=None):
    shape = w.shape
    a, b = shape[-2], shape[-1]
    lead = 1
    for dim in shape[:-2]:
        lead *= dim
    row_bytes = 4 * b
    if a * row_bytes <= ADAMW_BLOCK_BYTES:
        ta = a
        tl = max(1, min(lead, ADAMW_BLOCK_BYTES // (a * row_bytes)))
        while lead % tl:
            tl -= 1
    else:
        tl = 1
        ta = _tile(a, 256)
    to3 = lambda t: t.reshape(lead, a, b)

    def body(w_ref, g_ref, m_ref, v_ref, *rest):
        d_ref, nm_ref, nv_ref = rest[-3:]
        d_ref[...], nm_ref[...], nv_ref[...] = _adamw_math(w_ref[...], g_ref[...], m_ref[...], v_ref[...])

    spec = pl.BlockSpec((tl, ta, b), lambda i, j: (i, j, 0))
    out = jax.ShapeDtypeStruct((lead, a, b), F32)
    order = [] if after is None else [after]
    res = pl.pallas_call(
        body, name=name, grid=(lead // tl, a // ta), in_specs=[spec] * 4 + [pl.BlockSpec(memory_space=pl.ANY)] * len(order),
        out_specs=[spec] * 3, out_shape=[out] * 3, compiler_params=_params("parallel", "parallel"),
    )(to3(w), to3(g), to3(m), to3(v), *order)
    return [r.reshape(shape) for r in res]


def _adamw_small(ws, gs, ms, vs, *, name):
    n = len(ws)

    def body(*refs):
        for k in range(n):
            w_ref, g_ref, m_ref, v_ref = (refs[j * n + k] for j in range(4))
            d_ref, nm_ref, nv_ref = (refs[(4 + j) * n + k] for j in range(3))
            d_ref[...], nm_ref[...], nv_ref[...] = _adamw_math(w_ref[...], g_ref[...], m_ref[...], v_ref[...])

    outs = [jax.ShapeDtypeStruct(w.shape, F32) for w in ws]
    res = pl.pallas_call(body, name=name, out_shape=outs * 3, compiler_params=_params())(*ws, *gs, *ms, *vs)
    return res[:n], res[n:2 * n], res[2 * n:]


def _flip(v, bit):
    return 1 - v if bit else v


CHIP_DELTAS = ((1, 0), (0, 1), (1, 1))
SUM_ROWS = 32


def _all_gather_chips(shard, *, name):
    def body(x_ref, o_ref, send_sems, recv_sems, local_sem):
        x, y, c = lax.axis_index("x"), lax.axis_index("y"), lax.axis_index("c")
        mine = pltpu.make_async_copy(x_ref, o_ref.at[2 * x + y], local_sem)
        mine.start()

        def copy(k):
            tx, ty = _flip(x, CHIP_DELTAS[k][0]), _flip(y, CHIP_DELTAS[k][1])
            send = pltpu.make_async_remote_copy(src_ref=x_ref, dst_ref=o_ref.at[2 * x + y], send_sem=send_sems.at[k],
                                                recv_sem=recv_sems.at[k], device_id=(tx, ty, c), device_id_type=MESH)
            recv = pltpu.make_async_remote_copy(src_ref=x_ref, dst_ref=o_ref.at[2 * tx + ty], send_sem=send_sems.at[k],
                                                recv_sem=recv_sems.at[k], device_id=(tx, ty, c), device_id_type=MESH)
            return send, recv

        pairs = [copy(k) for k in range(3)]
        for send, _ in pairs:
            send.start()
        for _, recv in pairs:
            recv.wait_recv()
        for send, _ in pairs:
            send.wait_send()
        mine.wait()

    return pl.pallas_call(
        body, name=name, out_shape=jax.ShapeDtypeStruct((N_CHIPS,) + shard.shape, shard.dtype),
        in_specs=[HBM], out_specs=HBM,
        scratch_shapes=[pltpu.SemaphoreType.DMA((3,)), pltpu.SemaphoreType.DMA((3,)), pltpu.SemaphoreType.DMA(())],
    )(shard)


def _gather_weights(shards, *, name):
    n = len(shards)

    def body(*refs):
        w_refs, o_refs = refs[:n], refs[n:2 * n]
        ici_send, ici_recv, d2d_send, d2d_recv, local_sems = refs[2 * n:]
        x, y, c = lax.axis_index("x"), lax.axis_index("y"), lax.axis_index("c")
        me = 2 * x + y
        peers = [(_flip(x, dx), _flip(y, dy)) for dx, dy in CHIP_DELTAS]
        locals_ = [pltpu.make_async_copy(w_refs[k], o_refs[k].at[me], local_sems.at[k]) for k in range(n)]
        for cp in locals_:
            cp.start()

        def rows(k, which):
            half = shards[k].shape[0] // 2
            return pl.ds(pl.multiple_of(which * half, half), half)

        def over_chips(k, d, slot):
            tx, ty = peers[d]
            return pltpu.make_async_remote_copy(
                src_ref=w_refs[k].at[rows(k, c)], dst_ref=o_refs[k].at[slot, rows(k, c)], send_sem=ici_send.at[k, d],
                recv_sem=ici_recv.at[k, d], device_id=(tx, ty, c), device_id_type=MESH)

        def to_sibling(k, d, which):
            tx, ty = peers[d]
            at = o_refs[k].at[2 * tx + ty, rows(k, which)]
            return pltpu.make_async_remote_copy(src_ref=at, dst_ref=at, send_sem=d2d_send.at[k, d], recv_sem=d2d_recv.at[k, d],
                                                device_id=(x, y, 1 - c), device_id_type=MESH)

        sends = [over_chips(k, d, me) for k in range(n) for d in range(3)]
        for cp in sends:
            cp.start()
        passed = []
        for k in range(n):
            for d in range(3):
                over_chips(k, d, 2 * peers[d][0] + peers[d][1]).wait_recv()
                passed.append(to_sibling(k, d, c))
                passed[-1].start()
        for k in range(n):
            for d in range(3):
                to_sibling(k, d, 1 - c).wait_recv()
        for cp in sends + passed:
            cp.wait_send()
        for cp in locals_:
            cp.wait()

    return pl.pallas_call(
        body, name=name, out_shape=[jax.ShapeDtypeStruct((N_CHIPS,) + w.shape, w.dtype) for w in shards],
        in_specs=[HBM] * n, out_specs=[HBM] * n,
        scratch_shapes=[pltpu.SemaphoreType.DMA((n, 3))] * 4 + [pltpu.SemaphoreType.DMA((n,))],
    )(*shards)


def _add_into(dst_ref, src_ref):
    ns, r, _ = dst_ref.shape
    step = SUM_ROWS if r % SUM_ROWS == 0 else r
    for s in range(ns):
        def tile(t, carry):
            at = pl.ds(pl.multiple_of(t * step, step), step)
            dst_ref[s, at, :] = (dst_ref[s, at, :].astype(F32) + src_ref[s, at, :].astype(F32)).astype(dst_ref.dtype)
            return carry
        lax.fori_loop(0, r // step, tile, 0)


def _reduce_sibling(grads, *, name):
    n = len(grads)

    def body(*refs):
        g_refs, o_refs = refs[:n], refs[n:2 * n]
        mine, got = refs[2 * n:3 * n], refs[3 * n:4 * n]
        send_sems, recv_sems, load_sems, store_sems = refs[4 * n:]
        x, y, c = lax.axis_index("x"), lax.axis_index("y"), lax.axis_index("c")
        loads = [pltpu.make_async_copy(g_refs[k].at[:, c], mine[k], load_sems.at[k]) for k in range(n)]
        swaps = [pltpu.make_async_remote_copy(src_ref=g_refs[k].at[:, 1 - c], dst_ref=got[k], send_sem=send_sems.at[k],
                                              recv_sem=recv_sems.at[k], device_id=(x, y, 1 - c), device_id_type=MESH)
                 for k in range(n)]
        for cp in loads + swaps:
            cp.start()
        stores = []
        for k in range(n):
            loads[k].wait()
            swaps[k].wait_recv()
            _add_into(mine[k], got[k])
            stores.append(pltpu.make_async_copy(mine[k], o_refs[k], store_sems.at[k]))
            stores[-1].start()
        for k in range(n):
            swaps[k].wait_send()
            stores[k].wait()

    half = [jax.ShapeDtypeStruct((g.shape[0],) + g.shape[2:], g.dtype) for g in grads]
    return pl.pallas_call(
        body, name=name, out_shape=half, in_specs=[HBM] * n, out_specs=[HBM] * n,
        scratch_shapes=[pltpu.VMEM(h.shape, h.dtype) for h in half] * 2 + [pltpu.SemaphoreType.DMA((n,))] * 4,
        compiler_params=_params(),
    )(*grads)


def _reduce_chips(parts, landed, *, name):
    n_send = len(parts)
    n = n_send + len(landed)

    def body(*refs):
        p_refs, o_refs = refs[:n], refs[n:2 * n]
        got, total = refs[2 * n:3 * n], refs[3 * n:4 * n]
        send_sems, recv_sems, load_sems, share_send, share_recv, store_sems = refs[4 * n:]
        x, y, c = lax.axis_index("x"), lax.axis_index("y"), lax.axis_index("c")
        me = 2 * x + y
        peers = [(_flip(x, dx), _flip(y, dy)) for dx, dy in CHIP_DELTAS]

        def over_chips(k, d, src_slot, dst_slot):
            tx, ty = peers[d]
            return pltpu.make_async_remote_copy(
                src_ref=p_refs[k].at[src_slot], dst_ref=got[k].at[dst_slot], send_sem=send_sems.at[k, d],
                recv_sem=recv_sems.at[k, d], device_id=(tx, ty, c), device_id_type=MESH)

        loads = [pltpu.make_async_copy(p_refs[k].at[me], got[k].at[me], load_sems.at[k]) for k in range(n_send)]
        loads += [pltpu.make_async_copy(p_refs[k], got[k], load_sems.at[k]) for k in range(n_send, n)]
        sends = [over_chips(k, d, 2 * peers[d][0] + peers[d][1], me) for k in range(n_send) for d in range(3)]
        for cp in loads + sends:
            cp.start()
        shares, stores = [], []
        for k in range(n):
            loads[k].wait()
            for d in range(3 if k < n_send else 0):
                slot = 2 * peers[d][0] + peers[d][1]
                over_chips(k, d, slot, slot).wait_recv()
            r = total[k].shape[0]
            step = SUM_ROWS if r % SUM_ROWS == 0 else r

            def tile(t, carry, k=k, step=step):
                at = pl.ds(pl.multiple_of(t * step, step), step)
                acc = got[k][0, at, :].astype(F32)
                for s in range(1, N_CHIPS):
                    acc = acc + got[k][s, at, :].astype(F32)
                total[k][at, :] = acc
                return carry

            lax.fori_loop(0, r // step, tile, 0)
            stores.append(pltpu.make_async_copy(total[k], o_refs[k].at[c], store_sems.at[k]))
            shares.append(pltpu.make_async_remote_copy(
                src_ref=total[k], dst_ref=o_refs[k].at[c], send_sem=share_send.at[k], recv_sem=share_recv.at[k],
                device_id=(x, y, 1 - c), device_id_type=MESH))
            stores[-1].start()
            shares[-1].start()
        for k in range(n):
            pltpu.make_async_remote_copy(
                src_ref=total[k], dst_ref=o_refs[k].at[1 - c], send_sem=share_send.at[k], recv_sem=share_recv.at[k],
                device_id=(x, y, 1 - c), device_id_type=MESH).wait_recv()
        for cp in sends + shares:
            cp.wait_send()
        for cp in stores:
            cp.wait()

    both = list(parts) + list(landed)
    return pl.pallas_call(
        body, name=name, out_shape=[jax.ShapeDtypeStruct((2,) + p.shape[1:], F32) for p in both],
        in_specs=[HBM] * n, out_specs=[HBM] * n,
        scratch_shapes=[pltpu.VMEM(p.shape, p.dtype) for p in both] + [pltpu.VMEM(p.shape[1:], F32) for p in both]
        + [pltpu.SemaphoreType.DMA((n, 3))] * 2 + [pltpu.SemaphoreType.DMA((n,))] * 4,
        compiler_params=_params(),
    )(*both)


SEM = pl.BlockSpec(memory_space=pltpu.SEMAPHORE)
IN_FLIGHT = pltpu.SideEffectType.DATAFLOW_SIDE_EFFECTING


def _chip_copies(s_refs, l_refs, sems, scatter, theirs):
    x, y, c = lax.axis_index("x"), lax.axis_index("y"), lax.axis_index("c")
    me = 2 * x + y
    copies = []
    for k in range(len(s_refs)):
        for d, (dx, dy) in enumerate(CHIP_DELTAS):
            tx, ty = _flip(x, dx), _flip(y, dy)
            peer = 2 * tx + ty
            send_sem, recv_sem = sems[2 * (3 * k + d)], sems[2 * (3 * k + d) + 1]
            copies.append(pltpu.make_async_remote_copy(
                src_ref=s_refs[k].at[peer] if scatter else s_refs[k], dst_ref=l_refs[k].at[peer if theirs else me],
                send_sem=send_sem, recv_sem=recv_sem, device_id=(tx, ty, c), device_id_type=MESH))
    return copies


def _chips_start(srcs, lands, after, *, scatter, name):
    n = len(srcs)
    n_sem = 2 * 3 * n

    def body(*refs):
        s_refs, l_refs = refs[:n], refs[n:2 * n]
        sems = refs[2 * n + 1:2 * n + 1 + n_sem]
        token = refs[-1]
        for cp in _chip_copies(s_refs, l_refs, sems, scatter, False):
            cp.start()
        token[...] = jnp.zeros_like(token)

    hbm = lambda a: pltpu.HBM(a.shape, a.dtype)
    res = pl.pallas_call(
        body, name=name,
        out_shape=(*[pltpu.SemaphoreType.DMA(())] * n_sem, *[hbm(a) for a in srcs], *[hbm(a) for a in lands],
                   jax.ShapeDtypeStruct((8, LANES), F32)),
        in_specs=[HBM] * (2 * n) + [pl.BlockSpec(memory_space=pl.ANY)],
        out_specs=(*[SEM] * n_sem, *[HBM] * (2 * n), VMEM),
        input_output_aliases={k: n_sem + k for k in range(2 * n)},
        compiler_params=pltpu.CompilerParams(has_side_effects=IN_FLIGHT),
    )(*[pltpu.with_memory_space_constraint(a, pltpu.HBM) for a in list(srcs) + list(lands)], after)
    return res[:n_sem], res[n_sem:n_sem + n], res[n_sem + n:n_sem + 2 * n], res[-1]


def _chips_wait(sems, srcs, lands, after, *, scatter, name):
    n = len(srcs)
    n_sem = len(sems)

    def body(*refs):
        s_refs, l_refs = refs[:n], refs[n:2 * n]
        sem_refs = refs[2 * n:2 * n + n_sem]
        for cp in _chip_copies(s_refs, l_refs, sem_refs, scatter, False):
            cp.wait_send()
        for cp in _chip_copies(s_refs, l_refs, sem_refs, scatter, True):
            cp.wait_recv()

    hbm = lambda a: pltpu.HBM(a.shape, a.dtype)
    res = pl.pallas_call(
        body, name=name, out_shape=tuple(hbm(a) for a in list(srcs) + list(lands)),
        in_specs=[HBM] * (2 * n) + [SEM] * n_sem + [pl.BlockSpec(memory_space=pl.ANY)], out_specs=tuple([HBM] * (2 * n)),
        input_output_aliases={k: k for k in range(2 * n)},
        compiler_params=pltpu.CompilerParams(has_side_effects=IN_FLIGHT),
    )(*srcs, *lands, *sems, after)
    return res[n:]


def _all_gather_devices(rows, *, name):
    deltas = [(dx, dy, dc) for dx in (0, 1) for dy in (0, 1) for dc in (0, 1)][1:]

    def body(x_ref, o_ref, send_sems, recv_sems):
        x, y, c = lax.axis_index("x"), lax.axis_index("y"), lax.axis_index("c")
        me = 4 * x + 2 * y + c
        o_ref[me] = x_ref[...]
        sends, recvs = [], []
        for k, (dx, dy, dc) in enumerate(deltas):
            tx, ty, tc = _flip(x, dx), _flip(y, dy), _flip(c, dc)
            sends.append(pltpu.make_async_remote_copy(src_ref=x_ref, dst_ref=o_ref.at[me], send_sem=send_sems.at[k],
                                                      recv_sem=recv_sems.at[k], device_id=(tx, ty, tc), device_id_type=MESH))
            recvs.append(pltpu.make_async_remote_copy(src_ref=x_ref, dst_ref=o_ref.at[4 * tx + 2 * ty + tc],
                                                      send_sem=send_sems.at[k], recv_sem=recv_sems.at[k],
                                                      device_id=(tx, ty, tc), device_id_type=MESH))
        for cp in sends:
            cp.start()
        for cp in recvs:
            cp.wait_recv()
        for cp in sends:
            cp.wait_send()

    return pl.pallas_call(
        body, name=name, out_shape=jax.ShapeDtypeStruct((N_DEV,) + rows.shape, rows.dtype),
        in_specs=[VMEM], out_specs=VMEM,
        scratch_shapes=[pltpu.SemaphoreType.DMA((N_DEV - 1,)), pltpu.SemaphoreType.DMA((N_DEV - 1,))],
    )(rows)


WEIGHTS = ("ada_w", "ada_b", "ln_g", "ln_b", "e_w_in", "gmlp_norm_g", "gmlp_norm_b", "gmlp_ws", "gmlp_bs", "pool_w",
           "pool_b", "pool_scale", "e_w_out", "o_w_in", "mla_q_norm_g", "mla_kv_norm_g", "mla_w_uq", "mla_w_uk",
           "mla_w_uv", "o_w_out")
SMALL = ("ln_g", "ln_b", "gmlp_norm_g", "gmlp_norm_b", "gmlp_bs", "pool_b", "pool_scale", "mla_kv_norm_g", "mla_q_norm_g")


def _pad_cols(v, n):
    return jnp.concatenate([v, jnp.zeros((v.shape[0], n - v.shape[1]), v.dtype)], axis=1) if n > v.shape[1] else v


def _halves(g):
    return g.reshape(g.shape[0], 2, g.shape[1] // 2, g.shape[2])


def kernel(x, c, positions, ada_w, ada_b, ln_g, ln_b, e_w_in, gmlp_norm_g, gmlp_norm_b, gmlp_ws, gmlp_bs, pool_w, pool_b, pool_scale, e_w_out, o_w_in, mla_q_norm_g, mla_kv_norm_g, mla_w_uq, mla_w_uk, mla_w_uv, o_w_out, loss_target, m_ada_w, m_ada_b, m_ln_g, m_ln_b, m_e_w_in, m_gmlp_norm_g, m_gmlp_norm_b, m_gmlp_ws, m_gmlp_bs, m_pool_w, m_pool_b, m_pool_scale, m_e_w_out, m_o_w_in, m_mla_q_norm_g, m_mla_kv_norm_g, m_mla_w_uq, m_mla_w_uk, m_mla_w_uv, m_o_w_out, v_ada_w, v_ada_b, v_ln_g, v_ln_b, v_e_w_in, v_gmlp_norm_g, v_gmlp_norm_b, v_gmlp_ws, v_gmlp_bs, v_pool_w, v_pool_b, v_pool_scale, v_e_w_out, v_o_w_in, v_mla_q_norm_g, v_mla_kv_norm_g, v_mla_w_uq, v_mla_w_uk, v_mla_w_uv, v_o_w_out):
    args = dict(locals())
    weights = {n: args[n] for n in WEIGHTS}
    mom = {n: args["m_" + n] for n in WEIGHTS}
    var = {n: args["v_" + n] for n in WEIGHTS}
    ax, ay, ac = lax.axis_index("x"), lax.axis_index("y"), lax.axis_index("c")
    chip = 2 * ax + ay
    dev = 2 * chip + ac
    d = D_MODEL
    x2 = x[0]
    target = loss_target[0]
    q_rank_sh = mla_q_norm_g.shape[1]

    empty_zone = lambda w: lax.dynamic_update_slice(lax.empty((N_CHIPS,) + w.shape, w.dtype), w[None], (chip, 0, 0))
    shards0 = [w.astype(BF16) for w in (pool_w[0].reshape(-1, POOL_GROUP_DIM), e_w_out[0])]
    shards1 = [w.astype(BF16) for w in (o_w_in[0], mla_w_uq[0].reshape(q_rank_sh, -1), o_w_out[0])]
    w_in0, = _gather_weights([e_w_in[0].astype(BF16)], name="gather_weights")
    flight0 = _chips_start(shards0, [empty_zone(w) for w in shards0], w_in0, scatter=False, name="gather0_start")
    flight1 = _chips_start(shards1, [empty_zone(w) for w in shards1], flight0[3], scatter=False, name="gather1_start")
    wuk_hrd = jnp.transpose(mla_w_uk[0], (1, 0, 2)).astype(BF16)
    wuk_hdr = jnp.transpose(mla_w_uk[0], (1, 2, 0)).astype(BF16)
    wuv_hrv = jnp.transpose(mla_w_uv[0], (1, 0, 2)).astype(BF16)
    wuv_hvr = jnp.transpose(mla_w_uv[0], (1, 2, 0)).astype(BF16)
    ws = gmlp_ws[0]
    ws_t = jnp.transpose(ws, (0, 2, 1))
    bs_t = _pad_cols(gmlp_bs[0].T, LANES)

    inv = 1.0 / (ROPE_THETA ** (jnp.arange(0, MLA_ROPE, 2, dtype=F32) / MLA_ROPE))
    ang = positions[0].astype(F32)[:, None] * inv
    cos_t = jnp.tile(jnp.cos(ang), (1, 4))
    sin_t = jnp.tile(jnp.concatenate([-jnp.sin(ang), jnp.sin(ang)], axis=1), (1, 2))

    c_all = _all_gather_devices(c.reshape(8, LANES), name="gather_c").reshape(N_DEV, d)
    cols = ada_w.shape[2]
    ada_b_mine = lax.dynamic_slice_in_dim(ada_b, chip * cols, cols, axis=1)[:, None, :]
    mod_sh = _ada_mod(c_all, ada_w, ada_b_mine, name="ada_mod")
    q_norm_rows = jnp.zeros((8, cols), F32).at[0, :q_rank_sh].set(mla_q_norm_g[0])
    mod_all = _all_gather_chips(jnp.concatenate([mod_sh.reshape(2 * N_DEV, cols), q_norm_rows]), name="gather_mod")
    q_norm_g = mod_all[:, 2 * N_DEV, :q_rank_sh].reshape(1, -1)
    mod_all = jnp.transpose(mod_all[:, :2 * N_DEV].reshape(N_CHIPS, 2, N_DEV, cols), (1, 2, 0, 3)).reshape(2, N_DEV, 3 * d)
    mod = lax.dynamic_index_in_dim(mod_all, dev, axis=1, keepdims=False)
    shift = [mod[l:l + 1, :d] for l in range(2)]
    scale = [mod[l:l + 1, d:2 * d] for l in range(2)]
    gate = [mod[l:l + 1, 2 * d:] for l in range(2)]

    scale[0] = scale[0] + flight1[3][:1, :1]
    h0 = _modulate(x2, scale[0], shift[0], name="modulate0")
    proj0 = _matmul(h0, w_in0, b_stacked=True, tm=1024, tn=1280, name="proj0")
    pool_w_g, w_out0 = _chips_wait(*flight0[:3], proj0, scatter=False, name="gather0_wait")
    pool_w_bf = jnp.transpose(pool_w_g.reshape(N_CHIPS, POOL_GROUPS, -1, POOL_GROUP_DIM), (1, 0, 2, 3)).reshape(
        POOL_GROUPS, POOL_GROUP_DIM, POOL_GROUP_DIM)
    w_out0 = w_out0.reshape(-1, d)
    mix0 = _even_fwd(proj0, ws, bs_t, gmlp_norm_g, gmlp_norm_b, pool_w_bf, pool_b, pool_scale, name="even_fwd")
    y0 = _matmul(mix0, w_out0, name="out0")
    x1, h1 = _resid_ln_modulate(x2, y0, gate[0], ln_g[0:1], ln_b[0:1], scale[1], shift[1], name="resid_ln0")

    w_in1_g, w_uq_g, w_out1 = _chips_wait(*flight1[:3], h1, scatter=False, name="gather1_wait")
    w_out1 = w_out1.reshape(-1, d)
    w_in1 = jnp.transpose(w_in1_g, (1, 0, 2)).reshape(d, ODD_IN)
    w_in1 = jnp.concatenate([_pad_cols(w_in1[:, :ODD_SMALL], ODD_SMALL_PAD), w_in1[:, ODD_SMALL:]], axis=1)
    w_uq = w_uq_g.reshape(MLA_Q_RANK, MLA_HEADS, MLA_NOPE + MLA_ROPE)
    w_uq_nope = w_uq[:, :, :MLA_NOPE].reshape(MLA_Q_RANK, -1)
    w_uq_rope = w_uq[:, :, MLA_NOPE:].reshape(MLA_Q_RANK, -1)
    proj1 = _matmul(h1, w_in1, tm=1024, tn=1280, name="proj1")
    q_cn, keys = _mla_prep(proj1, q_norm_g, mla_kv_norm_g, cos_t, sin_t, name="mla_prep")
    q_nope = _matmul(q_cn, w_uq_nope, tm=1024, tn=2048, name="q_nope", out_dtype=BF16)
    q_rope_pre = _matmul(q_cn, w_uq_rope, tm=1024, name="q_rope")
    q = _q_build(q_nope, q_rope_pre, wuk_hdr, cos_t, sin_t, name="q_build")
    o_lat, lse = _attn_fwd(q, keys, name="attn_fwd")
    og = _o_build(o_lat, wuv_hrv, proj1, name="o_build")
    y1 = _matmul(og, w_out1, name="out1")

    dy1, dres1, g_ln_g1, g_ln_b1, dgate1, loss = _loss_ln_bwd(x1, y1, gate[1], ln_g[1:2], ln_b[1:2], target, name="loss_ln1")
    dg1 = _matmul(dy1, w_out1, trans_b=True, tn=2048, name="d_og")
    g_w_out1 = _matmul(og, dy1, trans_a=True, out_dtype=BF16, tm=1024, name="g_out1")
    do_lat, dz, g_uv = _o_bwd(dg1, proj1, o_lat, wuv_hrv, wuv_hvr, name="o_bwd")
    dq, dkeys = _attn_bwd(q, keys, do_lat, o_lat, lse, name="attn_bwd")
    dq_nope, dq_rope, g_uk = _q_bwd(dq, q_nope, wuk_hrd, cos_t, sin_t, name="q_bwd")
    dq_cn = (_matmul(dq_nope, w_uq_nope, trans_b=True, tm=1024, name="d_qcn_nope")
             + _matmul(dq_rope, w_uq_rope, trans_b=True, tm=1024, name="d_qcn_rope"))
    g_uq_nope = _matmul(q_cn, dq_nope, trans_a=True, out_dtype=BF16, tn=2048, name="g_uq_nope")
    g_uq_rope = _matmul(q_cn, dq_rope, trans_a=True, out_dtype=BF16, name="g_uq_rope")
    dsmall, g_qg, g_kvg = _mla_prep_bwd(proj1, dq_cn, dkeys, q_norm_g, mla_kv_norm_g, cos_t, sin_t, name="mla_prep_bwd")
    dproj1 = jnp.concatenate([dsmall, dz], axis=1)
    dh1 = _matmul(dproj1, w_in1, trans_b=True, name="d_h1")
    g_w_in1 = _matmul(h1, dproj1, trans_a=True, out_dtype=BF16, tm=1024, tn=1280, name="g_in1")

    g_uq = jnp.concatenate([g_uq_nope.reshape(MLA_Q_RANK, MLA_HEADS, MLA_NOPE), g_uq_rope.reshape(MLA_Q_RANK, MLA_HEADS, MLA_ROPE)], axis=2)
    g_w_in1 = jnp.concatenate([g_w_in1[:, :ODD_SMALL], g_w_in1[:, ODD_SMALL_PAD:]], axis=1)
    g_w_in1 = jnp.transpose(g_w_in1.reshape(d, N_CHIPS, -1), (1, 0, 2))
    big1 = [
        _halves(g_w_in1),
        _halves(g_uq.reshape(N_CHIPS, q_rank_sh, -1)),
        _halves(g_w_out1.reshape(N_CHIPS, -1, d)),
        _halves(g_uk.astype(BF16).reshape(N_CHIPS, -1, MLA_NOPE)),
        _halves(g_uv.astype(BF16).reshape(N_CHIPS, -1, MLA_V)),
    ]
    parts1 = _reduce_sibling(big1, name="reduce_sibling1")
    lands2 = [lax.dynamic_update_slice(lax.empty(p.shape, BF16), lax.dynamic_slice_in_dim(p, chip, 1, axis=0), (chip, 0, 0))
              for p in parts1]
    flight2 = _chips_start(parts1, lands2, loss, scatter=True, name="reduce1_start")

    gate[0] = gate[0] + flight2[3][:1, :1]
    dy0, dres0, g_ln_g0, g_ln_b0, dgate0, dscale1, dshift1 = _mid_ln_bwd(
        x2, y0, gate[0], ln_g[0:1], ln_b[0:1], dh1, dres1, scale[1], x1, name="mid_ln0")
    dmix0 = _matmul(dy0, w_out0, trans_b=True, tn=2048, name="d_mix0")
    g_w_out0 = _matmul(mix0, dy0, trans_a=True, out_dtype=BF16, tm=1024, name="g_out0")
    dproj0, g_ws, g_bs_t, g_ng, g_nb, g_pw, g_pb, g_ps = _even_bwd(
        proj0, dmix0, ws, ws_t, bs_t, gmlp_norm_g, gmlp_norm_b, pool_w_bf, pool_b, pool_scale, name="even_bwd")
    g_w_in0 = _matmul(h0, dproj0, trans_a=True, out_dtype=BF16, out_stacked=True, tm=1024, tn=1280, name="g_in0")

    g_pw = jnp.transpose(g_pw.astype(BF16).reshape(POOL_GROUPS, N_CHIPS, -1, POOL_GROUP_DIM), (1, 0, 2, 3))
    big0 = [
        _halves(g_w_in0),
        _halves(g_pw.reshape(N_CHIPS, -1, POOL_GROUP_DIM)),
        _halves(g_w_out0.reshape(N_CHIPS, -1, d)),
        _halves(g_ws.astype(BF16)),
    ]
    parts0 = _reduce_sibling(big0, name="reduce_sibling0")
    landed1 = _chips_wait(*flight2[:3], parts0[0], scatter=True, name="reduce1_wait")
    lands3 = [lax.dynamic_update_slice(lax.empty(p.shape, BF16), lax.dynamic_slice_in_dim(p, chip, 1, axis=0), (chip, 0, 0))
              for p in parts0]
    flight3 = _chips_start(parts0, lands3, landed1[0], scatter=True, name="reduce0_start")
    dh0 = _matmul(dproj0, w_in0, trans_b=True, b_stacked=True, tm=1024, after=flight3[3], name="d_h0")
    grad_x, dscale0, dshift0 = _input_bwd(x2, dh0, dres0, scale[0], name="input_bwd")

    small_local = {
        "ln_g": jnp.concatenate([g_ln_g0, g_ln_g1]), "ln_b": jnp.concatenate([g_ln_b0, g_ln_b1]),
        "gmlp_norm_g": g_ng, "gmlp_norm_b": g_nb, "gmlp_bs": g_bs_t[:, :GMLP_HEADS].T, "pool_b": g_pb, "pool_scale": g_ps,
        "mla_kv_norm_g": g_kvg, "mla_q_norm_g": g_qg,
    }
    n_mod = 2 * 3 * d
    vec = jnp.concatenate([dshift0, dscale0, dgate0, dshift1, dscale1, dgate1]
                          + [small_local[n].reshape(1, -1) for n in SMALL], axis=1)
    n_vec = vec.shape[1]
    vec = _pad_cols(vec, -(-n_vec // (8 * LANES)) * 8 * LANES).reshape(-1, LANES)
    vec_all = _all_gather_devices(vec, name="gather_small")
    vec_sum = _sum_devices(vec_all, name="sum_small").reshape(-1)
    dmod_all = vec_all.reshape(N_DEV, -1)[:, :n_mod].reshape(N_DEV, 2, 3 * d)
    dmod_sh = jnp.transpose(lax.dynamic_slice_in_dim(dmod_all, chip * cols, cols, axis=2), (1, 0, 2))
    dmod_sh = jnp.concatenate([dmod_sh, jnp.zeros((2, LANES - N_DEV, cols), F32)], axis=1)
    grads = {"ada_w": _ada_grad(_pad_cols(c_all.T, LANES), dmod_sh, name="ada_grad"), "ada_b": vec_sum[:n_mod].reshape(2, 3 * d)}
    off = n_mod
    for n in SMALL:
        sz = small_local[n].size
        grads[n] = vec_sum[off:off + sz]
        off += sz
    grads["mla_q_norm_g"] = lax.dynamic_slice_in_dim(grads["mla_q_norm_g"], chip * q_rank_sh, q_rank_sh)
    for n in SMALL:
        grads[n] = grads[n].reshape(weights[n].shape)

    landed0 = _chips_wait(*flight3[:3], grads["ada_w"], scatter=True, name="reduce0_wait")
    totals = _reduce_chips([], list(landed0) + list(landed1), name="reduce_chips")
    for n, t in zip(("e_w_in", "pool_w", "e_w_out", "gmlp_ws", "o_w_in", "mla_w_uq", "o_w_out"), totals):
        if n != "gmlp_ws":
            grads[n] = t.reshape(weights[n].shape)
    rep = jnp.concatenate([t.reshape(-1, LANES) for t in (totals[3], totals[7], totals[8])])
    rep_land = lax.dynamic_update_slice(lax.empty((N_CHIPS,) + rep.shape, F32), rep[None], (chip, 0, 0))
    flight4 = _chips_start([rep], [rep_land], totals[0], scatter=False, name="gather_rep_start")

    delta, new_m, new_v = {}, {}, {}
    replicated = ("gmlp_ws", "mla_w_uk", "mla_w_uv")
    large = [n for n in WEIGHTS if n not in SMALL and n != "ada_b"]
    for n in large:
        if n not in replicated:
            delta[n], new_m[n], new_v[n] = _adamw(weights[n], grads[n], mom[n], var[n], after=flight4[3], name="adamw_" + n)
    rep = _chips_wait(*flight4[:3], delta["e_w_in"], scatter=False, name="gather_rep_wait")[0]
    r_ws, r_uk = GMLP_BLOCK, 4 * MLA_KV_RANK
    grads["gmlp_ws"] = rep[:, :r_ws].reshape(weights["gmlp_ws"].shape)
    grads["mla_w_uk"] = jnp.transpose(rep[:, r_ws:r_ws + r_uk].reshape(MLA_HEADS, MLA_KV_RANK, MLA_NOPE), (1, 0, 2))[None]
    grads["mla_w_uv"] = jnp.transpose(rep[:, r_ws + r_uk:].reshape(MLA_HEADS, MLA_KV_RANK, MLA_V), (1, 0, 2))[None]
    for n in replicated:
        delta[n], new_m[n], new_v[n] = _adamw(weights[n], grads[n], mom[n], var[n], name="adamw_" + n)
    small = [n for n in WEIGHTS if n not in large]
    ds, ms, vs = _adamw_small([weights[n] for n in small], [grads[n] for n in small], [mom[n] for n in small],
                              [var[n] for n in small], name="adamw_small")
    for n, dn, mn, vn in zip(small, ds, ms, vs):
        delta[n], new_m[n], new_v[n] = dn, mn, vn

    loss_total = lax.psum(loss[0, 0], ("x", "y", "c"))
    return (loss_total, grad_x[None], *[grads[n] for n in WEIGHTS], *[delta[n] for n in WEIGHTS],
            *[new_m[n] for n in WEIGHTS], *[new_v[n] for n in WEIGHTS])
```

```python
import jax
import jax.numpy as jnp
from jax import lax
from jax.experimental import pallas as pl
from jax.experimental.pallas import tpu as pltpu

F32 = jnp.float32
BF16 = jnp.bfloat16
MESH = pl.DeviceIdType.MESH

D_MODEL = 1024
CHUNK = 64
LN_EPS = 1e-5
GMLP_HEADS = 4
GMLP_HEAD_DIM = 256
GMLP_BLOCK = 128
POOL_WINDOWS = (2, 4, 8, 16)
POOL_GROUPS = 4
POOL_GROUP_DIM = 256
POOL_HALO = 16
EVEN_IN = 5120
MLA_HEADS = 16
MLA_NOPE = 128
MLA_ROPE = 64
MLA_V = 128
MLA_Q_RANK = 256
MLA_KV_RANK = 128
MLA_WIDTH = MLA_HEADS * MLA_V
ODD_IN = 2496
ODD_SMALL = MLA_Q_RANK + MLA_KV_RANK + MLA_ROPE
ODD_SMALL_PAD = 512
QK_PAD = 256
ROPE_THETA = 10000.0
ATTN_SCALE = (MLA_NOPE + MLA_ROPE) ** -0.5
DEEPNORM_ALPHA = (2.0 * 2) ** 0.25
ADAM_LR = 0.001
ADAM_B1 = 0.9
ADAM_B2 = 0.999
ADAM_EPS = 1e-08
ADAM_WD = 0.01
ADAM_STEP = 10
NEG = -1e30
LANES = 128
N_DEV = 8
N_CHIPS = 4
VMEM_LIMIT_BYTES = 56 * 1024 * 1024
HBM = pl.BlockSpec(memory_space=pltpu.HBM)
VMEM = pl.BlockSpec(memory_space=pltpu.VMEM)


def _params(*sem):
    return pltpu.CompilerParams(dimension_semantics=sem if sem else None, vmem_limit_bytes=VMEM_LIMIT_BYTES)


def _tile(dim, pref):
    for t in (pref, 2048, 1280, 1024, 512, 256, 128):
        if t <= min(pref, dim) and dim % t == 0:
            return t
    return dim


def _sigmoid(z):
    return 1.0 / (1.0 + jnp.exp(-z))


def _dot(a, b, dims):
    return lax.dot_general(a, b, (dims, ((), ())), preferred_element_type=F32)


NN = ((1,), (0,))
NT = ((1,), (1,))
TN = ((0,), (0,))


def _matmul(a, b, *, name, trans_a=False, trans_b=False, out_dtype=F32, b_stacked=False, out_stacked=False,
            tm=512, tn=1024, tk=2048, after=None):
    k, m = a.shape if trans_a else a.shape[::-1]
    if b_stacked:
        ns, kb, n_sh = b.shape
        kb, n = (ns * n_sh, kb) if trans_b else (kb, ns * n_sh)
    else:
        n, kb = b.shape if trans_b else b.shape[::-1]
    assert k == kb, (a.shape, b.shape)
    tm = _tile(m, tm)
    if b_stacked and trans_b:
        tn, tk = _tile(n, tn), n_sh
    elif b_stacked or out_stacked:
        tn, tk = _tile(n // N_CHIPS, tn), _tile(k, tk)
    else:
        tn, tk = _tile(n, tn), _tile(k, tk)
    nk = k // tk
    per = max((n // N_CHIPS) // tn, 1)
    dims = ((0 if trans_a else 1,), (1 if trans_b else 0,))

    def body_one(a_ref, b_ref, *rest):
        o_ref = rest[-1]
        o_ref[...] = _dot(a_ref[...].astype(BF16), b_ref[...].astype(BF16), dims).astype(out_dtype)

    def body_acc(a_ref, b_ref, *rest):
        o_ref, acc_ref = rest[-2:]
        kk = pl.program_id(2)

        @pl.when(kk == 0)
        def _():
            acc_ref[...] = jnp.zeros_like(acc_ref)

        acc_ref[...] += _dot(a_ref[...].astype(BF16), b_ref[...].astype(BF16), dims)

        @pl.when(kk == nk - 1)
        def _():
            o_ref[...] = acc_ref[...].astype(out_dtype)

    a_spec = pl.BlockSpec((tk, tm), lambda i, j, kk: (kk, i)) if trans_a else pl.BlockSpec((tm, tk), lambda i, j, kk: (i, kk))
    if b_stacked and trans_b:
        b_spec = pl.BlockSpec((None, tn, tk), lambda i, j, kk: (kk, j, 0))
    elif b_stacked:
        b_spec = pl.BlockSpec((None, tk, tn), lambda i, j, kk: (j // per, kk, j % per))
    elif trans_b:
        b_spec = pl.BlockSpec((tn, tk), lambda i, j, kk: (j, kk))
    else:
        b_spec = pl.BlockSpec((tk, tn), lambda i, j, kk: (kk, j))
    if out_stacked:
        o_spec = pl.BlockSpec((None, tm, tn), lambda i, j, kk: (j // per, i, j % per))
        o_shape = jax.ShapeDtypeStruct((N_CHIPS, m, n // N_CHIPS), out_dtype)
    else:
        o_spec = pl.BlockSpec((tm, tn), lambda i, j, kk: (i, j))
        o_shape = jax.ShapeDtypeStruct((m, n), out_dtype)
    order = [] if after is None else [after]
    return pl.pallas_call(
        body_one if nk == 1 else body_acc, name=name, grid=(m // tm, n // tn, nk),
        in_specs=[a_spec, b_spec] + [pl.BlockSpec(memory_space=pl.ANY)] * len(order),
        out_specs=o_spec, out_shape=o_shape, scratch_shapes=[] if nk == 1 else [pltpu.VMEM((tm, tn), F32)],
        compiler_params=_params("parallel", "parallel", "arbitrary"),
    )(a, b, *order)


def _row_spec(ts, d):
    return pl.BlockSpec((ts, d), lambda i: (i, 0))


def _vec_spec(d):
    return pl.BlockSpec((1, d), lambda i: (0, 0))


def _modulate(x, scale, shift, *, name):
    s, d = x.shape
    ts = _tile(s, 512)

    def body(x_ref, sc_ref, sh_ref, h_ref):
        h_ref[...] = (x_ref[...] * (1.0 + sc_ref[...]) + sh_ref[...]).astype(BF16)

    return pl.pallas_call(
        body, name=name, grid=(s // ts,), in_specs=[_row_spec(ts, d), _vec_spec(d), _vec_spec(d)],
        out_specs=_row_spec(ts, d), out_shape=jax.ShapeDtypeStruct((s, d), BF16), compiler_params=_params("parallel"),
    )(x, scale, shift)


def _ln_stats(pre):
    mu = jnp.mean(pre, axis=-1, keepdims=True)
    xc = pre - mu
    var = jnp.mean(xc * xc, axis=-1, keepdims=True)
    rstd = lax.rsqrt(var + LN_EPS)
    return xc * rstd, rstd


def _ln_bwd_rows(dout, xhat, rstd, g):
    dxh = dout * g
    m1 = jnp.mean(dxh, axis=-1, keepdims=True)
    m2 = jnp.mean(dxh * xhat, axis=-1, keepdims=True)
    return rstd * (dxh - m1 - xhat * m2)


def _colsum(v):
    return jnp.sum(v, axis=0, keepdims=True)


def _resid_ln_modulate(x, y, gate, g, b, scale_next, shift_next, *, name):
    s, d = x.shape
    ts = _tile(s, 512)

    def body(x_ref, y_ref, gate_ref, g_ref, b_ref, sc_ref, sh_ref, xn_ref, h_ref):
        pre = DEEPNORM_ALPHA * x_ref[...] + (1.0 + gate_ref[...]) * y_ref[...]
        xhat, _ = _ln_stats(pre)
        xn = xhat * g_ref[...] + b_ref[...]
        xn_ref[...] = xn
        h_ref[...] = (xn * (1.0 + sc_ref[...]) + sh_ref[...]).astype(BF16)

    return pl.pallas_call(
        body, name=name, grid=(s // ts,),
        in_specs=[_row_spec(ts, d), _row_spec(ts, d)] + [_vec_spec(d)] * 5,
        out_specs=[_row_spec(ts, d), _row_spec(ts, d)],
        out_shape=[jax.ShapeDtypeStruct((s, d), F32), jax.ShapeDtypeStruct((s, d), BF16)],
        compiler_params=_params("parallel"),
    )(x, y, gate, g, b, scale_next, shift_next)


def _loss_ln_bwd(x, y, gate, g, b, target, *, name):
    s, d = x.shape
    ts = _tile(s, 512)

    def body(x_ref, y_ref, gate_ref, g_ref, b_ref, t_ref, dy_ref, dres_ref, dg_ref, db_ref, dgate_ref, loss_ref):
        @pl.when(pl.program_id(0) == 0)
        def _():
            for r in (dg_ref, db_ref, dgate_ref, loss_ref):
                r[...] = jnp.zeros_like(r)

        yv = y_ref[...]
        pre = DEEPNORM_ALPHA * x_ref[...] + (1.0 + gate_ref[...]) * yv
        xhat, rstd = _ln_stats(pre)
        diff = xhat * g_ref[...] + b_ref[...] - t_ref[...]
        loss_ref[...] += (0.5 / d) * jnp.sum(jnp.sum(diff * diff, axis=1, keepdims=True), axis=0, keepdims=True)
        dout = diff * (1.0 / d)
        dpre = _ln_bwd_rows(dout, xhat, rstd, g_ref[...])
        dy_ref[...] = (dpre * (1.0 + gate_ref[...])).astype(BF16)
        dres_ref[...] = DEEPNORM_ALPHA * dpre
        dg_ref[...] += _colsum(dout * xhat)
        db_ref[...] += _colsum(dout)
        dgate_ref[...] += _colsum(dpre * yv)

    vec = jax.ShapeDtypeStruct((1, d), F32)
    return pl.pallas_call(
        body, name=name, grid=(s // ts,),
        in_specs=[_row_spec(ts, d), _row_spec(ts, d), _vec_spec(d), _vec_spec(d), _vec_spec(d), _row_spec(ts, d)],
        out_specs=[_row_spec(ts, d), _row_spec(ts, d), _vec_spec(d), _vec_spec(d), _vec_spec(d), _vec_spec(1)],
        out_shape=[jax.ShapeDtypeStruct((s, d), BF16), jax.ShapeDtypeStruct((s, d), F32), vec, vec, vec,
                   jax.ShapeDtypeStruct((1, 1), F32)],
        compiler_params=_params("arbitrary"),
    )(x, y, gate, g, b, target)


def _mid_ln_bwd(x, y, gate, g, b, dh_next, dres_next, scale_next, x_next, *, name):
    s, d = x.shape
    ts = _tile(s, 512)

    def body(x_ref, y_ref, gate_ref, g_ref, b_ref, dh_ref, dr_ref, sc_ref, xn_ref,
             dy_ref, dres_ref, dg_ref, db_ref, dgate_ref, dscale_ref, dshift_ref):
        @pl.when(pl.program_id(0) == 0)
        def _():
            for r in (dg_ref, db_ref, dgate_ref, dscale_ref, dshift_ref):
                r[...] = jnp.zeros_like(r)

        dh = dh_ref[...]
        dout = dr_ref[...] + dh * (1.0 + sc_ref[...])
        dscale_ref[...] += _colsum(dh * xn_ref[...])
        dshift_ref[...] += _colsum(dh)
        yv = y_ref[...]
        pre = DEEPNORM_ALPHA * x_ref[...] + (1.0 + gate_ref[...]) * yv
        xhat, rstd = _ln_stats(pre)
        dpre = _ln_bwd_rows(dout, xhat, rstd, g_ref[...])
        dy_ref[...] = (dpre * (1.0 + gate_ref[...])).astype(BF16)
        dres_ref[...] = DEEPNORM_ALPHA * dpre
        dg_ref[...] += _colsum(dout * xhat)
        db_ref[...] += _colsum(dout)
        dgate_ref[...] += _colsum(dpre * yv)

    vec = jax.ShapeDtypeStruct((1, d), F32)
    return pl.pallas_call(
        body, name=name, grid=(s // ts,),
        in_specs=[_row_spec(ts, d), _row_spec(ts, d), _vec_spec(d), _vec_spec(d), _vec_spec(d),
                  _row_spec(ts, d), _row_spec(ts, d), _vec_spec(d), _row_spec(ts, d)],
        out_specs=[_row_spec(ts, d), _row_spec(ts, d)] + [_vec_spec(d)] * 5,
        out_shape=[jax.ShapeDtypeStruct((s, d), BF16), jax.ShapeDtypeStruct((s, d), F32)] + [vec] * 5,
        compiler_params=_params("arbitrary"),
    )(x, y, gate, g, b, dh_next, dres_next, scale_next, x_next)


def _input_bwd(x, dh, dres, scale, *, name):
    s, d = x.shape
    ts = _tile(s, 512)

    def body(x_ref, dh_ref, dr_ref, sc_ref, dx_ref, dscale_ref, dshift_ref):
        @pl.when(pl.program_id(0) == 0)
        def _():
            dscale_ref[...] = jnp.zeros_like(dscale_ref)
            dshift_ref[...] = jnp.zeros_like(dshift_ref)

        dh = dh_ref[...]
        dx_ref[...] = dr_ref[...] + dh * (1.0 + sc_ref[...])
        dscale_ref[...] += _colsum(dh * x_ref[...])
        dshift_ref[...] += _colsum(dh)

    vec = jax.ShapeDtypeStruct((1, d), F32)
    return pl.pallas_call(
        body, name=name, grid=(s // ts,),
        in_specs=[_row_spec(ts, d), _row_spec(ts, d), _row_spec(ts, d), _vec_spec(d)],
        out_specs=[_row_spec(ts, d), _vec_spec(d), _vec_spec(d)],
        out_shape=[jax.ShapeDtypeStruct((s, d), F32), vec, vec],
        compiler_params=_params("arbitrary"),
    )(x, dh, dres, scale)


def _chunk_mask(transposed=False):
    r = lax.broadcasted_iota(jnp.int32, (GMLP_BLOCK, GMLP_BLOCK), 0) // CHUNK
    c = lax.broadcasted_iota(jnp.int32, (GMLP_BLOCK, GMLP_BLOCK), 1) // CHUNK
    return (r <= c) if transposed else (c <= r)


def _window_sum(ext, steps, forward):
    rows = ext.shape[0]
    acc = ext
    for k in range(steps):
        shift = 1 << k
        acc = acc + pltpu.roll(acc, (rows - shift) if forward else shift, 0)
    return acc


def _pool_counts(first_row, rows, win):
    t = first_row + lax.broadcasted_iota(jnp.int32, (rows, 1), 0)
    return jnp.minimum(t + 1, win).astype(F32)


def _even_specs(t):
    col = lambda j: pl.BlockSpec((t, D_MODEL), lambda n: (n, j))
    per = t // POOL_HALO
    prev = pl.BlockSpec((POOL_HALO, D_MODEL), lambda n: (jnp.maximum(n * per - 1, 0), 3))
    return col, per, prev


def _full(shape):
    return pl.BlockSpec(shape, lambda n: (0,) * len(shape))


def _gmlp_head(v_h, ng, nb, w_bf):
    xhat, rstd = _ln_stats(v_h)
    vn = (xhat * ng + nb).astype(BF16)
    return xhat, rstd, vn, _dot(w_bf, vn, NN)


def _pool_group(xb_g, prev_g, first_row, grp):
    t = xb_g.shape[0]
    ext = jnp.concatenate([prev_g, xb_g], axis=0)
    tot = _window_sum(ext, grp + 1, False)[POOL_HALO:, :]
    cnt = _pool_counts(first_row, t, POOL_WINDOWS[grp])
    return tot / cnt - xb_g, cnt


def _even_fwd(proj, ws, bs_t, ng, nb, pool_w, pool_b, pool_scale, *, name):
    s = proj.shape[0]
    t = GMLP_BLOCK
    col, per, prev = _even_specs(t)

    def body(u_ref, v_ref, za_ref, xb_ref, zb_ref, xp_ref, ws_ref, bs_ref, ng_ref, nb_ref, pw_ref, pb_ref, ps_ref, o_ref):
        n = pl.program_id(0)
        mask = _chunk_mask()
        for h in range(GMLP_HEADS):
            c0 = h * GMLP_HEAD_DIM
            cs = slice(c0, c0 + GMLP_HEAD_DIM)
            w_bf = jnp.where(mask, ws_ref[h], 0.0).astype(BF16)
            _, _, _, sv = _gmlp_head(v_ref[:, cs].astype(F32),ng_ref[...], nb_ref[...], w_bf)
            sv = sv + bs_ref[:, h:h + 1]
            za = za_ref[:, cs].astype(F32)
            o_ref[:, cs] = (u_ref[:, cs].astype(F32) * sv * (za * _sigmoid(za))).astype(BF16)
        live = (n > 0).astype(F32)
        for grp in range(POOL_GROUPS):
            c0 = grp * POOL_GROUP_DIM
            cs = slice(c0, c0 + POOL_GROUP_DIM)
            pooled, _ = _pool_group(xb_ref[:, cs].astype(F32), xp_ref[:, cs].astype(F32) * live, n * t, grp)
            yb = _dot(pooled.astype(BF16), pw_ref[grp], NN) + pb_ref[:, cs]
            zb = zb_ref[:, cs].astype(F32)
            o_ref[:, D_MODEL + c0:D_MODEL + c0 + POOL_GROUP_DIM] = (yb * ps_ref[:, cs] * (zb * _sigmoid(zb))).astype(BF16)

    return pl.pallas_call(
        body, name=name, grid=(s // t,),
        in_specs=[col(0), col(1), col(2), col(3), col(4), prev,
                  _full((GMLP_HEADS, t, t)), _full((t, LANES)), _full((1, GMLP_HEAD_DIM)), _full((1, GMLP_HEAD_DIM)),
                  _full((POOL_GROUPS, POOL_GROUP_DIM, POOL_GROUP_DIM)), _full((1, D_MODEL)), _full((1, D_MODEL))],
        out_specs=pl.BlockSpec((t, 2 * D_MODEL), lambda n: (n, 0)),
        out_shape=jax.ShapeDtypeStruct((s, 2 * D_MODEL), BF16),
        compiler_params=_params("parallel"),
    )(proj, proj, proj, proj, proj, proj, ws, bs_t, ng, nb, pool_w, pool_b, pool_scale)


def _even_bwd(proj, dmix, ws, ws_t, bs_t, ng, nb, pool_w, pool_b, pool_scale, *, name):
    s = proj.shape[0]
    t = GMLP_BLOCK
    nblk = s // t
    col, per, prev = _even_specs(t)
    nxt = lambda j: pl.BlockSpec((POOL_HALO, D_MODEL), lambda n: (jnp.minimum((n + 1) * per, nblk * per - 1), j))

    def body(u_ref, v_ref, za_ref, xb_ref, zb_ref, xp_ref, zn_ref, da_ref, db_ref, dbn_ref,
             ws_ref, wst_ref, bs_ref, ng_ref, nb_ref, pw_ref, pb_ref, ps_ref,
             dp_ref, gws_ref, gbs_ref, gng_ref, gnb_ref, gpw_ref, gpb_ref, gps_ref):
        n = pl.program_id(0)

        @pl.when(n == 0)
        def _():
            for r in (gws_ref, gbs_ref, gng_ref, gnb_ref, gpw_ref, gpb_ref, gps_ref):
                r[...] = jnp.zeros_like(r)

        mask, mask_t = _chunk_mask(), _chunk_mask(True)
        lane = lax.broadcasted_iota(jnp.int32, (t, LANES), 1)
        ngv, nbv = ng_ref[...], nb_ref[...]
        for h in range(GMLP_HEADS):
            c0 = h * GMLP_HEAD_DIM
            cs = slice(c0, c0 + GMLP_HEAD_DIM)
            w_bf = jnp.where(mask, ws_ref[h], 0.0).astype(BF16)
            wt_bf = jnp.where(mask_t, wst_ref[h], 0.0).astype(BF16)
            xhat, rstd, vn, sv = _gmlp_head(v_ref[:, cs].astype(F32),ngv, nbv, w_bf)
            sv = sv + bs_ref[:, h:h + 1]
            za, u, da = za_ref[:, cs].astype(F32), u_ref[:, cs].astype(F32), da_ref[:, cs].astype(F32)
            sg = _sigmoid(za)
            sl = za * sg
            dp_ref[:, cs] = (da * sv * sl).astype(BF16)
            dp_ref[:, 2 * D_MODEL + c0:2 * D_MODEL + c0 + GMLP_HEAD_DIM] = (
                da * u * sv * (sg * (1.0 + za * (1.0 - sg)))).astype(BF16)
            dsv = da * u * sl
            gbs_ref[...] += jnp.where(lane == h, jnp.sum(dsv, axis=1, keepdims=True), 0.0)
            dsv_bf = dsv.astype(BF16)
            gws_ref[h] += jnp.where(mask, _dot(dsv_bf, vn, NT), 0.0)
            dvn = _dot(wt_bf, dsv_bf, NN)
            dp_ref[:, D_MODEL + c0:D_MODEL + c0 + GMLP_HEAD_DIM] = _ln_bwd_rows(dvn, xhat, rstd, ngv).astype(BF16)
            gng_ref[...] += _colsum(dvn * xhat)
            gnb_ref[...] += _colsum(dvn)
        live_prev = (n > 0).astype(F32)
        live_next = (n < nblk - 1).astype(F32)
        for grp in range(POOL_GROUPS):
            c0 = grp * POOL_GROUP_DIM
            cs = slice(c0, c0 + POOL_GROUP_DIM)
            xb = xb_ref[:, cs].astype(F32)
            pooled, cnt = _pool_group(xb, xp_ref[:, cs].astype(F32) * live_prev, n * t, grp)
            pooled_bf = pooled.astype(BF16)
            pw = pw_ref[grp]
            yb = _dot(pooled_bf, pw, NN) + pb_ref[:, cs]
            ps = ps_ref[:, cs]
            zb, db = zb_ref[:, cs].astype(F32), db_ref[:, cs].astype(F32)
            sg = _sigmoid(zb)
            sl = zb * sg
            dp_ref[:, 4 * D_MODEL + c0:4 * D_MODEL + c0 + POOL_GROUP_DIM] = (
                db * yb * ps * (sg * (1.0 + zb * (1.0 - sg)))).astype(BF16)
            dsl = db * sl
            dy = dsl * ps
            gps_ref[:, cs] += _colsum(dsl * yb)
            gpb_ref[:, cs] += _colsum(dy)
            dy_bf = dy.astype(BF16)
            gpw_ref[grp] += _dot(pooled_bf, dy_bf, TN)
            r = _dot(dy_bf, pw, NT)
            zn = zn_ref[:, cs].astype(F32)
            dyn = (dbn_ref[:, cs].astype(F32) * (zn * _sigmoid(zn)) * ps * live_next).astype(BF16)
            rn = _dot(dyn, pw, NT) / _pool_counts((n + 1) * t, POOL_HALO, POOL_WINDOWS[grp])
            ext = jnp.concatenate([r / cnt, rn], axis=0)
            dxb = _window_sum(ext, grp + 1, True)[:t, :] - r
            dp_ref[:, 3 * D_MODEL + c0:3 * D_MODEL + c0 + POOL_GROUP_DIM] = dxb.astype(BF16)

    out_shape = [
        jax.ShapeDtypeStruct((s, EVEN_IN), BF16),
        jax.ShapeDtypeStruct((GMLP_HEADS, t, t), F32), jax.ShapeDtypeStruct((t, LANES), F32),
        jax.ShapeDtypeStruct((1, GMLP_HEAD_DIM), F32), jax.ShapeDtypeStruct((1, GMLP_HEAD_DIM), F32),
        jax.ShapeDtypeStruct((POOL_GROUPS, POOL_GROUP_DIM, POOL_GROUP_DIM), F32),
        jax.ShapeDtypeStruct((1, D_MODEL), F32), jax.ShapeDtypeStruct((1, D_MODEL), F32),
    ]
    return pl.pallas_call(
        body, name=name, grid=(nblk,),
        in_specs=[col(0), col(1), col(2), col(3), col(4), prev, nxt(4),
                  pl.BlockSpec((t, D_MODEL), lambda n: (n, 0)), pl.BlockSpec((t, D_MODEL), lambda n: (n, 1)), nxt(1),
                  _full((GMLP_HEADS, t, t)), _full((GMLP_HEADS, t, t)), _full((t, LANES)),
                  _full((1, GMLP_HEAD_DIM)), _full((1, GMLP_HEAD_DIM)),
                  _full((POOL_GROUPS, POOL_GROUP_DIM, POOL_GROUP_DIM)), _full((1, D_MODEL)), _full((1, D_MODEL))],
        out_specs=[pl.BlockSpec((t, EVEN_IN), lambda n: (n, 0))] + [_full(o.shape) for o in out_shape[1:]],
        out_shape=out_shape,
        compiler_params=_params("arbitrary"),
    )(proj, proj, proj, proj, proj, proj, proj, dmix, dmix, dmix, ws, ws_t, bs_t, ng, nb, pool_w, pool_b, pool_scale)


def _half_swap(v):
    lane = lax.broadcasted_iota(jnp.int32, v.shape, 1)
    return jnp.where(lane % MLA_ROPE < MLA_ROPE // 2, pltpu.roll(v, LANES - MLA_ROPE // 2, 1), pltpu.roll(v, MLA_ROPE // 2, 1))


def _rope(v, cos, sin_signed):
    return v * cos + _half_swap(v) * sin_signed


def _rope_bwd(d, cos, sin_signed):
    return d * cos + _half_swap(d * sin_signed)


def _rms(v, g):
    r = lax.rsqrt(jnp.mean(v * v, axis=-1, keepdims=True) + LN_EPS)
    return v * r * g, r


def _rms_bwd(dy, v, r, g):
    u = dy * g
    return r * u - v * (r * r * r) * jnp.mean(u * v, axis=-1, keepdims=True)


def _mla_prep(proj, gq, gkv, cos, sin_signed, *, name):
    s = proj.shape[0]
    ts = _tile(s, 512)

    def body(p_ref, gq_ref, gkv_ref, c_ref, s_ref, q_ref, k_ref):
        qcn, _ = _rms(p_ref[:, :MLA_Q_RANK].astype(F32), gq_ref[...])
        kvn, _ = _rms(p_ref[:, MLA_Q_RANK:MLA_Q_RANK + MLA_KV_RANK].astype(F32), gkv_ref[...])
        kr = _rope(p_ref[:, MLA_Q_RANK + MLA_KV_RANK:].astype(F32), c_ref[...], s_ref[...])
        q_ref[...] = qcn.astype(BF16)
        k_ref[...] = jnp.concatenate([kvn, kr], axis=1).astype(BF16)

    return pl.pallas_call(
        body, name=name, grid=(s // ts,),
        in_specs=[_row_spec(ts, ODD_SMALL_PAD), _vec_spec(MLA_Q_RANK), _vec_spec(MLA_KV_RANK), _row_spec(ts, LANES), _row_spec(ts, LANES)],
        out_specs=[_row_spec(ts, MLA_Q_RANK), _row_spec(ts, QK_PAD)],
        out_shape=[jax.ShapeDtypeStruct((s, MLA_Q_RANK), BF16), jax.ShapeDtypeStruct((s, QK_PAD), BF16)],
        compiler_params=_params("parallel"),
    )(proj, gq, gkv, cos, sin_signed)


def _mla_prep_bwd(proj, dqcn, dkv, gq, gkv, cos, sin_signed, *, name):
    s = proj.shape[0]
    ts = _tile(s, 512)

    def body(p_ref, dq_ref, dkv_ref, gq_ref, gkv_ref, c_ref, s_ref, ds_ref, ggq_ref, ggkv_ref):
        @pl.when(pl.program_id(0) == 0)
        def _():
            ggq_ref[...] = jnp.zeros_like(ggq_ref)
            ggkv_ref[...] = jnp.zeros_like(ggkv_ref)

        qc = p_ref[:, :MLA_Q_RANK].astype(F32)
        kvc = p_ref[:, MLA_Q_RANK:MLA_Q_RANK + MLA_KV_RANK].astype(F32)
        _, rq = _rms(qc, gq_ref[...])
        _, rkv = _rms(kvc, gkv_ref[...])
        dq = dq_ref[...]
        dkvn = dkv_ref[:, :MLA_KV_RANK]
        ggq_ref[...] += _colsum(dq * qc * rq)
        ggkv_ref[...] += _colsum(dkvn * kvc * rkv)
        dkr = _rope_bwd(dkv_ref[:, MLA_KV_RANK:], c_ref[...], s_ref[...])
        ds_ref[...] = jnp.concatenate(
            [_rms_bwd(dq, qc, rq, gq_ref[...]), _rms_bwd(dkvn, kvc, rkv, gkv_ref[...]), dkr], axis=1).astype(BF16)

    return pl.pallas_call(
        body, name=name, grid=(s // ts,),
        in_specs=[_row_spec(ts, ODD_SMALL_PAD), _row_spec(ts, MLA_Q_RANK), _row_spec(ts, QK_PAD),
                  _vec_spec(MLA_Q_RANK), _vec_spec(MLA_KV_RANK), _row_spec(ts, LANES), _row_spec(ts, LANES)],
        out_specs=[_row_spec(ts, ODD_SMALL_PAD), _vec_spec(MLA_Q_RANK), _vec_spec(MLA_KV_RANK)],
        out_shape=[jax.ShapeDtypeStruct((s, ODD_SMALL_PAD), BF16), jax.ShapeDtypeStruct((1, MLA_Q_RANK), F32),
                   jax.ShapeDtypeStruct((1, MLA_KV_RANK), F32)],
        compiler_params=_params("arbitrary"),
    )(proj, dqcn, dkv, gq, gkv, cos, sin_signed)


LOG2_E = 1.4426950408889634
Q_PRESCALE = ATTN_SCALE * LOG2_E


def _q_build(q_nope, q_rope_pre, wuk_hdr, cos, sin_signed, *, name):
    s = q_nope.shape[0]
    ts = _tile(s, 1024)

    def body(qn_ref, qr_ref, w_ref, c_ref, s_ref, o_ref):
        r = _rope(qr_ref[...], c_ref[...], s_ref[...])
        lane = lax.broadcasted_iota(jnp.int32, (ts, LANES), 1)
        for j in range(2):
            ql = _dot(qn_ref[:, j * MLA_NOPE:(j + 1) * MLA_NOPE], w_ref[j], NN)
            rr = r if j == 0 else pltpu.roll(r, MLA_ROPE, 1)
            o_ref[j] = (jnp.concatenate([ql, jnp.where(lane < MLA_ROPE, rr, 0.0)], axis=1) * Q_PRESCALE).astype(BF16)

    return pl.pallas_call(
        body, name=name, grid=(s // ts, MLA_HEADS // 2),
        in_specs=[pl.BlockSpec((ts, 2 * MLA_NOPE), lambda i, p: (i, p)), pl.BlockSpec((ts, LANES), lambda i, p: (i, p)),
                  pl.BlockSpec((2, MLA_NOPE, MLA_KV_RANK), lambda i, p: (p, 0, 0)),
                  pl.BlockSpec((ts, LANES), lambda i, p: (i, 0)), pl.BlockSpec((ts, LANES), lambda i, p: (i, 0))],
        out_specs=pl.BlockSpec((2, ts, QK_PAD), lambda i, p: (p, i, 0)),
        out_shape=jax.ShapeDtypeStruct((MLA_HEADS, s, QK_PAD), BF16),
        compiler_params=_params("parallel", "parallel"),
    )(q_nope, q_rope_pre, wuk_hdr, cos, sin_signed)


def _q_bwd(dq, q_nope, wuk_hrd, cos, sin_signed, *, name):
    s = q_nope.shape[0]
    ts = _tile(s, 1024)

    def body(dq_ref, qn_ref, w_ref, c_ref, s_ref, dn_ref, dr_ref, gw_ref):
        @pl.when(pl.program_id(1) == 0)
        def _():
            gw_ref[...] = jnp.zeros_like(gw_ref)

        lane = lax.broadcasted_iota(jnp.int32, (ts, LANES), 1)
        for j in range(2):
            dql = dq_ref[j, :, :MLA_KV_RANK]
            dn_ref[:, j * MLA_NOPE:(j + 1) * MLA_NOPE] = _dot(dql, w_ref[j], NN).astype(BF16)
            gw_ref[j] += _dot(dql, qn_ref[:, j * MLA_NOPE:(j + 1) * MLA_NOPE], TN)
        hi0 = dq_ref[0, :, MLA_KV_RANK:].astype(F32)
        hi1 = dq_ref[1, :, MLA_KV_RANK:].astype(F32)
        d = jnp.where(lane < MLA_ROPE, hi0, pltpu.roll(hi1, MLA_ROPE, 1))
        dr_ref[...] = _rope_bwd(d, c_ref[...], s_ref[...]).astype(BF16)

    return pl.pallas_call(
        body, name=name, grid=(MLA_HEADS // 2, s // ts),
        in_specs=[pl.BlockSpec((2, ts, QK_PAD), lambda p, i: (p, i, 0)), pl.BlockSpec((ts, 2 * MLA_NOPE), lambda p, i: (i, p)),
                  pl.BlockSpec((2, MLA_KV_RANK, MLA_NOPE), lambda p, i: (p, 0, 0)),
                  pl.BlockSpec((ts, LANES), lambda p, i: (i, 0)), pl.BlockSpec((ts, LANES), lambda p, i: (i, 0))],
        out_specs=[pl.BlockSpec((ts, 2 * MLA_NOPE), lambda p, i: (i, p)), pl.BlockSpec((ts, LANES), lambda p, i: (i, p)),
                   pl.BlockSpec((2, MLA_KV_RANK, MLA_NOPE), lambda p, i: (p, 0, 0))],
        out_shape=[jax.ShapeDtypeStruct((s, MLA_HEADS * MLA_NOPE), BF16), jax.ShapeDtypeStruct((s, MLA_HEADS * MLA_ROPE), BF16),
                   jax.ShapeDtypeStruct((MLA_HEADS, MLA_KV_RANK, MLA_NOPE), F32)],
        compiler_params=_params("parallel", "arbitrary"),
    )(dq, q_nope, wuk_hrd, cos, sin_signed)


ATTN_BQ = 128
ATTN_BK = 512


def _diag_mask(rows, bq, bk, q0, k0):
    qc = (q0 + lax.broadcasted_iota(jnp.int32, (rows, bk), 0) % bq) // CHUNK
    kc = (k0 + lax.broadcasted_iota(jnp.int32, (rows, bk), 1)) // CHUNK
    return kc <= qc


def _attn_fwd(q, k, *, name):
    nh, s, dk = q.shape
    bq, bk = _tile(s, ATTN_BQ), _tile(s, ATTN_BK)
    rows = nh * bq

    def body(q_ref, k_ref, o_ref, lse_ref):
        i = pl.program_id(0)
        qb = q_ref[...].reshape(rows, dk)
        n_before = (i * bq) // bk

        def step(j, width, carry, masked):
            m, l, acc = carry
            k0 = pl.multiple_of(j * bk, bk)
            kb = k_ref[pl.ds(k0, width), :]
            sc = _dot(qb, kb, NT)
            if masked:
                sc = jnp.where(_diag_mask(rows, bq, width, i * bq, k0), sc, NEG)
            m_new = jnp.maximum(m, jnp.max(sc, axis=1, keepdims=True))
            p = jnp.exp2(sc - m_new)
            a = jnp.exp2(m - m_new)
            l = a * l + jnp.sum(p, axis=1, keepdims=True)
            acc = a * acc + _dot(p.astype(BF16), kb[:, :MLA_KV_RANK], NN)
            return m_new, l, acc

        init = (jnp.full((rows, 1), NEG, F32), jnp.zeros((rows, 1), F32), jnp.zeros((rows, MLA_KV_RANK), F32))
        carry = lax.fori_loop(0, n_before, lambda j, c: step(j, bk, c, False), init)
        for part in range(bk // bq):
            @pl.when(i % (bk // bq) == part)
            def _(part=part):
                m, l, acc = step(n_before, (part + 1) * bq, carry, True)
                o_ref[...] = (acc / l).astype(BF16).reshape(nh, bq, MLA_KV_RANK)
                lse_ref[...] = jnp.broadcast_to(m + jnp.log2(l), (rows, LANES)).reshape(nh, bq, LANES)

    return pl.pallas_call(
        body, name=name, grid=(s // bq,),
        in_specs=[pl.BlockSpec((nh, bq, dk), lambda i: (0, i, 0)), pl.BlockSpec((s, dk), lambda i: (0, 0))],
        out_specs=[pl.BlockSpec((nh, bq, MLA_KV_RANK), lambda i: (0, i, 0)), pl.BlockSpec((nh, bq, LANES), lambda i: (0, i, 0))],
        out_shape=[jax.ShapeDtypeStruct((nh, s, MLA_KV_RANK), BF16), jax.ShapeDtypeStruct((nh, s, LANES), F32)],
        compiler_params=_params("parallel"),
    )(q, k)


def _attn_bwd(q, k, do, o, lse, *, name):
    nh, s, dk = q.shape
    bq, bk = _tile(s, ATTN_BQ), _tile(s, ATTN_BK)
    rows = nh * bq

    def body(q_ref, k_ref, do_ref, o_ref, lse_ref, dq_ref, dkv_ref):
        i = pl.program_id(0)
        n_before = (i * bq) // bk

        @pl.when(i == 0)
        def _():
            dkv_ref[...] = jnp.zeros_like(dkv_ref)

        qb = q_ref[...].reshape(rows, dk)
        dob = do_ref[...].reshape(rows, MLA_KV_RANK)
        lse_b = lse_ref[...].reshape(rows, LANES)[:, :1]
        delta = jnp.sum(dob.astype(F32) * o_ref[...].reshape(rows, MLA_KV_RANK).astype(F32), axis=1, keepdims=True)

        def step(j, width, dq, masked):
            j0 = pl.multiple_of(j * bk, bk)
            kb = k_ref[pl.ds(j0, width), :]
            sc = _dot(qb, kb, NT)
            if masked:
                sc = jnp.where(_diag_mask(rows, bq, width, i * bq, j0), sc, NEG)
            p = jnp.exp2(sc - lse_b)
            dp = _dot(dob, kb[:, :MLA_KV_RANK], NT)
            ds_bf = (p * (dp - delta)).astype(BF16)
            dkv_ref[pl.ds(j0, width), :] += _dot(ds_bf, qb, TN) * (1.0 / LOG2_E)
            dkv_ref[pl.ds(j0, width), :MLA_KV_RANK] += _dot(p.astype(BF16), dob, TN)
            return dq + _dot(ds_bf, kb, NN)

        dq_before = lax.fori_loop(0, n_before, lambda j, c: step(j, bk, c, False), jnp.zeros((rows, dk), F32))
        for part in range(bk // bq):
            @pl.when(i % (bk // bq) == part)
            def _(part=part):
                dq = step(n_before, (part + 1) * bq, dq_before, True) * ATTN_SCALE
                dq_ref[...] = dq.astype(BF16).reshape(nh, bq, dk)

    blk = lambda w: pl.BlockSpec((nh, bq, w), lambda i: (0, i, 0))
    return pl.pallas_call(
        body, name=name, grid=(s // bq,),
        in_specs=[blk(dk), pl.BlockSpec((s, dk), lambda i: (0, 0)), blk(MLA_KV_RANK), blk(MLA_KV_RANK), blk(LANES)],
        out_specs=[blk(dk), pl.BlockSpec((s, dk), lambda i: (0, 0))],
        out_shape=[jax.ShapeDtypeStruct((nh, s, dk), BF16), jax.ShapeDtypeStruct((s, dk), F32)],
        compiler_params=_params("arbitrary"),
    )(q, k, do, o, lse)


HEAD_GROUP = 4


def _o_build(o_lat, wuv_hrv, proj, *, name):
    s = proj.shape[0]
    ts = _tile(s, 1024)
    w = HEAD_GROUP * MLA_V

    def body(ol_ref, w_ref, z_ref, og_ref):
        for j in range(HEAD_GROUP):
            cs = slice(j * MLA_V, (j + 1) * MLA_V)
            z = z_ref[:, cs].astype(F32)
            og_ref[:, cs] = (_dot(ol_ref[j], w_ref[j], NN) * (z * _sigmoid(z))).astype(BF16)

    return pl.pallas_call(
        body, name=name, grid=(s // ts, MLA_HEADS // HEAD_GROUP),
        in_specs=[pl.BlockSpec((HEAD_GROUP, ts, MLA_KV_RANK), lambda i, g: (g, i, 0)),
                  pl.BlockSpec((HEAD_GROUP, MLA_KV_RANK, MLA_V), lambda i, g: (g, 0, 0)),
                  pl.BlockSpec((ts, w), lambda i, g: (i, g + 1))],
        out_specs=pl.BlockSpec((ts, w), lambda i, g: (i, g)),
        out_shape=jax.ShapeDtypeStruct((s, MLA_WIDTH), BF16),
        compiler_params=_params("parallel", "parallel"),
    )(o_lat, wuv_hrv, proj)


def _o_bwd(dg, proj, o_lat, wuv_hrv, wuv_hvr, *, name):
    s = proj.shape[0]
    ts = _tile(s, 1024)
    w = HEAD_GROUP * MLA_V

    def body(dg_ref, z_ref, ol_ref, w_ref, wt_ref, dol_ref, dz_ref, gw_ref):
        @pl.when(pl.program_id(1) == 0)
        def _():
            gw_ref[...] = jnp.zeros_like(gw_ref)

        for j in range(HEAD_GROUP):
            cs = slice(j * MLA_V, (j + 1) * MLA_V)
            z, dgj, ol = z_ref[:, cs].astype(F32), dg_ref[:, cs].astype(F32), ol_ref[j]
            sg = _sigmoid(z)
            o = _dot(ol, w_ref[j], NN)
            dz_ref[:, cs] = (dgj * o * (sg * (1.0 + z * (1.0 - sg)))).astype(BF16)
            do_bf = (dgj * (z * sg)).astype(BF16)
            dol_ref[j] = _dot(do_bf, wt_ref[j], NN).astype(BF16)
            gw_ref[j] += _dot(ol, do_bf, TN)

    hs = lambda a, b: pl.BlockSpec((HEAD_GROUP, a, b), lambda g, i: (g, 0, 0))
    return pl.pallas_call(
        body, name=name, grid=(MLA_HEADS // HEAD_GROUP, s // ts),
        in_specs=[pl.BlockSpec((ts, w), lambda g, i: (i, g)), pl.BlockSpec((ts, w), lambda g, i: (i, g + 1)),
                  pl.BlockSpec((HEAD_GROUP, ts, MLA_KV_RANK), lambda g, i: (g, i, 0)),
                  hs(MLA_KV_RANK, MLA_V), hs(MLA_V, MLA_KV_RANK)],
        out_specs=[pl.BlockSpec((HEAD_GROUP, ts, MLA_KV_RANK), lambda g, i: (g, i, 0)),
                   pl.BlockSpec((ts, w), lambda g, i: (i, g)), hs(MLA_KV_RANK, MLA_V)],
        out_shape=[jax.ShapeDtypeStruct((MLA_HEADS, s, MLA_KV_RANK), BF16), jax.ShapeDtypeStruct((s, MLA_WIDTH), BF16),
                   jax.ShapeDtypeStruct((MLA_HEADS, MLA_KV_RANK, MLA_V), F32)],
        compiler_params=_params("parallel", "arbitrary"),
    )(dg, proj, o_lat, wuv_hrv, wuv_hvr)


def _ada_mod(c_all, ada_w, ada_b_sh, *, name):
    nl, _, cols = ada_w.shape

    def body(c_ref, w_ref, b_ref, o_ref):
        c = c_ref[...]
        cond = (c * _sigmoid(c)).astype(BF16)
        for l in range(nl):
            o_ref[l] = _dot(cond, w_ref[l].astype(BF16), NN) + b_ref[l]

    return pl.pallas_call(
        body, name=name, out_shape=jax.ShapeDtypeStruct((nl, c_all.shape[0], cols), F32),
        compiler_params=_params(),
    )(c_all, ada_w, ada_b_sh)


def _ada_grad(c_all_t, dmod_sh, *, name):
    nl, _, cols = dmod_sh.shape
    d = c_all_t.shape[0]

    def body(c_ref, dm_ref, gw_ref):
        c = c_ref[...]
        cond_t = c * _sigmoid(c)
        for l in range(nl):
            gw_ref[l] = lax.dot_general(cond_t, dm_ref[l], (NN, ((), ())), precision=lax.Precision.HIGHEST,
                                        preferred_element_type=F32)

    return pl.pallas_call(
        body, name=name, out_shape=jax.ShapeDtypeStruct((nl, d, cols), F32), compiler_params=_params(),
    )(c_all_t, dmod_sh)


def _sum_devices(parts, *, name):
    def body(p_ref, o_ref):
        acc = p_ref[0]
        for k in range(1, parts.shape[0]):
            acc = acc + p_ref[k]
        o_ref[...] = acc

    return pl.pallas_call(body, name=name, out_shape=jax.ShapeDtypeStruct(parts.shape[1:], F32), compiler_params=_params())(parts)


def _adamw_math(w, g, m, v):
    c1 = 1.0 - ADAM_B1 ** ADAM_STEP
    c2 = 1.0 - ADAM_B2 ** ADAM_STEP
    nm = ADAM_B1 * m + (1.0 - ADAM_B1) * g
    nv = ADAM_B2 * v + (1.0 - ADAM_B2) * (g * g)
    return -ADAM_LR * ((nm / c1) / (jnp.sqrt(nv / c2) + ADAM_EPS) + ADAM_WD * w), nm, nv


ADAMW_BLOCK_BYTES = 1 << 20


def _adamw(w, g, m, v, *, name, after=None):
    shape = w.shape
    a, b = shape[-2], shape[-1]
    lead = 1
    for dim in shape[:-2]:
        lead *= dim
    row_bytes = 4 * b
    if a * row_bytes <= ADAMW_BLOCK_BYTES:
        ta = a
        tl = max(1, min(lead, ADAMW_BLOCK_BYTES // (a * row_bytes)))
        while lead % tl:
            tl -= 1
    else:
        tl = 1
        ta = _tile(a, 256)
    to3 = lambda t: t.reshape(lead, a, b)

    def body(w_ref, g_ref, m_ref, v_ref, *rest):
        d_ref, nm_ref, nv_ref = rest[-3:]
        d_ref[...], nm_ref[...], nv_ref[...] = _adamw_math(w_ref[...], g_ref[...], m_ref[...], v_ref[...])

    spec = pl.BlockSpec((tl, ta, b), lambda i, j: (i, j, 0))
    out = jax.ShapeDtypeStruct((lead, a, b), F32)
    order = [] if after is None else [after]
    res = pl.pallas_call(
        body, name=name, grid=(lead // tl, a // ta), in_specs=[spec] * 4 + [pl.BlockSpec(memory_space=pl.ANY)] * len(order),
        out_specs=[spec] * 3, out_shape=[out] * 3, compiler_params=_params("parallel", "parallel"),
    )(to3(w), to3(g), to3(m), to3(v), *order)
    return [r.reshape(shape) for r in res]


def _adamw_small(ws, gs, ms, vs, *, name):
    n = len(ws)

    def body(*refs):
        for k in range(n):
            w_ref, g_ref, m_ref, v_ref = (refs[j * n + k] for j in range(4))
            d_ref, nm_ref, nv_ref = (refs[(4 + j) * n + k] for j in range(3))
            d_ref[...], nm_ref[...], nv_ref[...] = _adamw_math(w_ref[...], g_ref[...], m_ref[...], v_ref[...])

    outs = [jax.ShapeDtypeStruct(w.shape, F32) for w in ws]
    res = pl.pallas_call(body, name=name, out_shape=outs * 3, compiler_params=_params())(*ws, *gs, *ms, *vs)
    return res[:n], res[n:2 * n], res[2 * n:]


def _flip(v, bit):
    return 1 - v if bit else v


CHIP_DELTAS = ((1, 0), (0, 1), (1, 1))
SUM_ROWS = 32


def _all_gather_chips(shard, *, name):
    def body(x_ref, o_ref, send_sems, recv_sems, local_sem):
        x, y, c = lax.axis_index("x"), lax.axis_index("y"), lax.axis_index("c")
        mine = pltpu.make_async_copy(x_ref, o_ref.at[2 * x + y], local_sem)
        mine.start()

        def copy(k):
            tx, ty = _flip(x, CHIP_DELTAS[k][0]), _flip(y, CHIP_DELTAS[k][1])
            send = pltpu.make_async_remote_copy(src_ref=x_ref, dst_ref=o_ref.at[2 * x + y], send_sem=send_sems.at[k],
                                                recv_sem=recv_sems.at[k], device_id=(tx, ty, c), device_id_type=MESH)
            recv = pltpu.make_async_remote_copy(src_ref=x_ref, dst_ref=o_ref.at[2 * tx + ty], send_sem=send_sems.at[k],
                                                recv_sem=recv_sems.at[k], device_id=(tx, ty, c), device_id_type=MESH)
            return send, recv

        pairs = [copy(k) for k in range(3)]
        for send, _ in pairs:
            send.start()
        for _, recv in pairs:
            recv.wait_recv()
        for send, _ in pairs:
            send.wait_send()
        mine.wait()

    return pl.pallas_call(
        body, name=name, out_shape=jax.ShapeDtypeStruct((N_CHIPS,) + shard.shape, shard.dtype),
        in_specs=[HBM], out_specs=HBM,
        scratch_shapes=[pltpu.SemaphoreType.DMA((3,)), pltpu.SemaphoreType.DMA((3,)), pltpu.SemaphoreType.DMA(())],
    )(shard)


def _gather_weights(shards, *, name):
    n = len(shards)

    def body(*refs):
        w_refs, o_refs = refs[:n], refs[n:2 * n]
        ici_send, ici_recv, d2d_send, d2d_recv, local_sems = refs[2 * n:]
        x, y, c = lax.axis_index("x"), lax.axis_index("y"), lax.axis_index("c")
        me = 2 * x + y
        peers = [(_flip(x, dx), _flip(y, dy)) for dx, dy in CHIP_DELTAS]
        locals_ = [pltpu.make_async_copy(w_refs[k], o_refs[k].at[me], local_sems.at[k]) for k in range(n)]
        for cp in locals_:
            cp.start()

        def rows(k, which):
            half = shards[k].shape[0] // 2
            return pl.ds(pl.multiple_of(which * half, half), half)

        def over_chips(k, d, slot):
            tx, ty = peers[d]
            return pltpu.make_async_remote_copy(
                src_ref=w_refs[k].at[rows(k, c)], dst_ref=o_refs[k].at[slot, rows(k, c)], send_sem=ici_send.at[k, d],
                recv_sem=ici_recv.at[k, d], device_id=(tx, ty, c), device_id_type=MESH)

        def to_sibling(k, d, which):
            tx, ty = peers[d]
            at = o_refs[k].at[2 * tx + ty, rows(k, which)]
            return pltpu.make_async_remote_copy(src_ref=at, dst_ref=at, send_sem=d2d_send.at[k, d], recv_sem=d2d_recv.at[k, d],
                                                device_id=(x, y, 1 - c), device_id_type=MESH)

        sends = [over_chips(k, d, me) for k in range(n) for d in range(3)]
        for cp in sends:
            cp.start()
        passed = []
        for k in range(n):
            for d in range(3):
                over_chips(k, d, 2 * peers[d][0] + peers[d][1]).wait_recv()
                passed.append(to_sibling(k, d, c))
                passed[-1].start()
        for k in range(n):
            for d in range(3):
                to_sibling(k, d, 1 - c).wait_recv()
        for cp in sends + passed:
            cp.wait_send()
        for cp in locals_:
            cp.wait()

    return pl.pallas_call(
        body, name=name, out_shape=[jax.ShapeDtypeStruct((N_CHIPS,) + w.shape, w.dtype) for w in shards],
        in_specs=[HBM] * n, out_specs=[HBM] * n,
        scratch_shapes=[pltpu.SemaphoreType.DMA((n, 3))] * 4 + [pltpu.SemaphoreType.DMA((n,))],
    )(*shards)


def _add_into(dst_ref, src_ref):
    ns, r, _ = dst_ref.shape
    step = SUM_ROWS if r % SUM_ROWS == 0 else r
    for s in range(ns):
        def tile(t, carry):
            at = pl.ds(pl.multiple_of(t * step, step), step)
            dst_ref[s, at, :] = (dst_ref[s, at, :].astype(F32) + src_ref[s, at, :].astype(F32)).astype(dst_ref.dtype)
            return carry
        lax.fori_loop(0, r // step, tile, 0)


def _reduce_sibling(grads, *, name):
    n = len(grads)

    def body(*refs):
        g_refs, o_refs = refs[:n], refs[n:2 * n]
        mine, got = refs[2 * n:3 * n], refs[3 * n:4 * n]
        send_sems, recv_sems, load_sems, store_sems = refs[4 * n:]
        x, y, c = lax.axis_index("x"), lax.axis_index("y"), lax.axis_index("c")
        loads = [pltpu.make_async_copy(g_refs[k].at[:, c], mine[k], load_sems.at[k]) for k in range(n)]
        swaps = [pltpu.make_async_remote_copy(src_ref=g_refs[k].at[:, 1 - c], dst_ref=got[k], send_sem=send_sems.at[k],
                                              recv_sem=recv_sems.at[k], device_id=(x, y, 1 - c), device_id_type=MESH)
                 for k in range(n)]
        for cp in loads + swaps:
            cp.start()
        stores = []
        for k in range(n):
            loads[k].wait()
            swaps[k].wait_recv()
            _add_into(mine[k], got[k])
            stores.append(pltpu.make_async_copy(mine[k], o_refs[k], store_sems.at[k]))
            stores[-1].start()
        for k in range(n):
            swaps[k].wait_send()
            stores[k].wait()

    half = [jax.ShapeDtypeStruct((g.shape[0],) + g.shape[2:], g.dtype) for g in grads]
    return pl.pallas_call(
        body, name=name, out_shape=half, in_specs=[HBM] * n, out_specs=[HBM] * n,
        scratch_shapes=[pltpu.VMEM(h.shape, h.dtype) for h in half] * 2 + [pltpu.SemaphoreType.DMA((n,))] * 4,
        compiler_params=_params(),
    )(*grads)


def _reduce_chips(parts, landed, *, name):
    n_send = len(parts)
    n = n_send + len(landed)

    def body(*refs):
        p_refs, o_refs = refs[:n], refs[n:2 * n]
        got, total = refs[2 * n:3 * n], refs[3 * n:4 * n]
        send_sems, recv_sems, load_sems, share_send, share_recv, store_sems = refs[4 * n:]
        x, y, c = lax.axis_index("x"), lax.axis_index("y"), lax.axis_index("c")
        me = 2 * x + y
        peers = [(_flip(x, dx), _flip(y, dy)) for dx, dy in CHIP_DELTAS]

        def over_chips(k, d, src_slot, dst_slot):
            tx, ty = peers[d]
            return pltpu.make_async_remote_copy(
                src_ref=p_refs[k].at[src_slot], dst_ref=got[k].at[dst_slot], send_sem=send_sems.at[k, d],
                recv_sem=recv_sems.at[k, d], device_id=(tx, ty, c), device_id_type=MESH)

        loads = [pltpu.make_async_copy(p_refs[k].at[me], got[k].at[me], load_sems.at[k]) for k in range(n_send)]
        loads += [pltpu.make_async_copy(p_refs[k], got[k], load_sems.at[k]) for k in range(n_send, n)]
        sends = [over_chips(k, d, 2 * peers[d][0] + peers[d][1], me) for k in range(n_send) for d in range(3)]
        for cp in loads + sends:
            cp.start()
        shares, stores = [], []
        for k in range(n):
            loads[k].wait()
            for d in range(3 if k < n_send else 0):
                slot = 2 * peers[d][0] + peers[d][1]
                over_chips(k, d, slot, slot).wait_recv()
            r = total[k].shape[0]
            step = SUM_ROWS if r % SUM_ROWS == 0 else r

            def tile(t, carry, k=k, step=step):
                at = pl.ds(pl.multiple_of(t * step, step), step)
                acc = got[k][0, at, :].astype(F32)
                for s in range(1, N_CHIPS):
                    acc = acc + got[k][s, at, :].astype(F32)
                total[k][at, :] = acc
                return carry

            lax.fori_loop(0, r // step, tile, 0)
            stores.append(pltpu.make_async_copy(total[k], o_refs[k].at[c], store_sems.at[k]))
            shares.append(pltpu.make_async_remote_copy(
                src_ref=total[k], dst_ref=o_refs[k].at[c], send_sem=share_send.at[k], recv_sem=share_recv.at[k],
                device_id=(x, y, 1 - c), device_id_type=MESH))
            stores[-1].start()
            shares[-1].start()
        for k in range(n):
            pltpu.make_async_remote_copy(
                src_ref=total[k], dst_ref=o_refs[k].at[1 - c], send_sem=share_send.at[k], recv_sem=share_recv.at[k],
                device_id=(x, y, 1 - c), device_id_type=MESH).wait_recv()
        for cp in sends + shares:
            cp.wait_send()
        for cp in stores:
            cp.wait()

    both = list(parts) + list(landed)
    return pl.pallas_call(
        body, name=name, out_shape=[jax.ShapeDtypeStruct((2,) + p.shape[1:], F32) for p in both],
        in_specs=[HBM] * n, out_specs=[HBM] * n,
        scratch_shapes=[pltpu.VMEM(p.shape, p.dtype) for p in both] + [pltpu.VMEM(p.shape[1:], F32) for p in both]
        + [pltpu.SemaphoreType.DMA((n, 3))] * 2 + [pltpu.SemaphoreType.DMA((n,))] * 4,
        compiler_params=_params(),
    )(*both)


SEM = pl.BlockSpec(memory_space=pltpu.SEMAPHORE)
IN_FLIGHT = pltpu.SideEffectType.DATAFLOW_SIDE_EFFECTING


def _chip_copies(s_refs, l_refs, sems, scatter, theirs):
    x, y, c = lax.axis_index("x"), lax.axis_index("y"), lax.axis_index("c")
    me = 2 * x + y
    copies = []
    for k in range(len(s_refs)):
        for d, (dx, dy) in enumerate(CHIP_DELTAS):
            tx, ty = _flip(x, dx), _flip(y, dy)
            peer = 2 * tx + ty
            send_sem, recv_sem = sems[2 * (3 * k + d)], sems[2 * (3 * k + d) + 1]
            copies.append(pltpu.make_async_remote_copy(
                src_ref=s_refs[k].at[peer] if scatter else s_refs[k], dst_ref=l_refs[k].at[peer if theirs else me],
                send_sem=send_sem, recv_sem=recv_sem, device_id=(tx, ty, c), device_id_type=MESH))
    return copies


def _chips_start(srcs, lands, after, *, scatter, name):
    n = len(srcs)
    n_sem = 2 * 3 * n

    def body(*refs):
        s_refs, l_refs = refs[:n], refs[n:2 * n]
        sems = refs[2 * n + 1:2 * n + 1 + n_sem]
        token = refs[-1]
        for cp in _chip_copies(s_refs, l_refs, sems, scatter, False):
            cp.start()
        token[...] = jnp.zeros_like(token)

    hbm = lambda a: pltpu.HBM(a.shape, a.dtype)
    res = pl.pallas_call(
        body, name=name,
        out_shape=(*[pltpu.SemaphoreType.DMA(())] * n_sem, *[hbm(a) for a in srcs], *[hbm(a) for a in lands],
                   jax.ShapeDtypeStruct((8, LANES), F32)),
        in_specs=[HBM] * (2 * n) + [pl.BlockSpec(memory_space=pl.ANY)],
        out_specs=(*[SEM] * n_sem, *[HBM] * (2 * n), VMEM),
        input_output_aliases={k: n_sem + k for k in range(2 * n)},
        compiler_params=pltpu.CompilerParams(has_side_effects=IN_FLIGHT),
    )(*[pltpu.with_memory_space_constraint(a, pltpu.HBM) for a in list(srcs) + list(lands)], after)
    return res[:n_sem], res[n_sem:n_sem + n], res[n_sem + n:n_sem + 2 * n], res[-1]


def _chips_wait(sems, srcs, lands, after, *, scatter, name):
    n = len(srcs)
    n_sem = len(sems)

    def body(*refs):
        s_refs, l_refs = refs[:n], refs[n:2 * n]
        sem_refs = refs[2 * n:2 * n + n_sem]
        for cp in _chip_copies(s_refs, l_refs, sem_refs, scatter, False):
            cp.wait_send()
        for cp in _chip_copies(s_refs, l_refs, sem_refs, scatter, True):
            cp.wait_recv()

    hbm = lambda a: pltpu.HBM(a.shape, a.dtype)
    res = pl.pallas_call(
        body, name=name, out_shape=tuple(hbm(a) for a in list(srcs) + list(lands)),
        in_specs=[HBM] * (2 * n) + [SEM] * n_sem + [pl.BlockSpec(memory_space=pl.ANY)], out_specs=tuple([HBM] * (2 * n)),
        input_output_aliases={k: k for k in range(2 * n)},
        compiler_params=pltpu.CompilerParams(has_side_effects=IN_FLIGHT),
    )(*srcs, *lands, *sems, after)
    return res[n:]


def _all_gather_devices(rows, *, name):
    deltas = [(dx, dy, dc) for dx in (0, 1) for dy in (0, 1) for dc in (0, 1)][1:]

    def body(x_ref, o_ref, send_sems, recv_sems):
        x, y, c = lax.axis_index("x"), lax.axis_index("y"), lax.axis_index("c")
        me = 4 * x + 2 * y + c
        o_ref[me] = x_ref[...]
        sends, recvs = [], []
        for k, (dx, dy, dc) in enumerate(deltas):
            tx, ty, tc = _flip(x, dx), _flip(y, dy), _flip(c, dc)
            sends.append(pltpu.make_async_remote_copy(src_ref=x_ref, dst_ref=o_ref.at[me], send_sem=send_sems.at[k],
                                                      recv_sem=recv_sems.at[k], device_id=(tx, ty, tc), device_id_type=MESH))
            recvs.append(pltpu.make_async_remote_copy(src_ref=x_ref, dst_ref=o_ref.at[4 * tx + 2 * ty + tc],
                                                      send_sem=send_sems.at[k], recv_sem=recv_sems.at[k],
                                                      device_id=(tx, ty, tc), device_id_type=MESH))
        for cp in sends:
            cp.start()
        for cp in recvs:
            cp.wait_recv()
        for cp in sends:
            cp.wait_send()

    return pl.pallas_call(
        body, name=name, out_shape=jax.ShapeDtypeStruct((N_DEV,) + rows.shape, rows.dtype),
        in_specs=[VMEM], out_specs=VMEM,
        scratch_shapes=[pltpu.SemaphoreType.DMA((N_DEV - 1,)), pltpu.SemaphoreType.DMA((N_DEV - 1,))],
    )(rows)


WEIGHTS = ("ada_w", "ada_b", "ln_g", "ln_b", "e_w_in", "gmlp_norm_g", "gmlp_norm_b", "gmlp_ws", "gmlp_bs", "pool_w",
           "pool_b", "pool_scale", "e_w_out", "o_w_in", "mla_q_norm_g", "mla_kv_norm_g", "mla_w_uq", "mla_w_uk",
           "mla_w_uv", "o_w_out")
SMALL = ("ln_g", "ln_b", "gmlp_norm_g", "gmlp_norm_b", "gmlp_bs", "pool_b", "pool_scale", "mla_kv_norm_g", "mla_q_norm_g")


def _pad_cols(v, n):
    return jnp.concatenate([v, jnp.zeros((v.shape[0], n - v.shape[1]), v.dtype)], axis=1) if n > v.shape[1] else v


def _halves(g):
    return g.reshape(g.shape[0], 2, g.shape[1] // 2, g.shape[2])


def kernel(x, c, positions, ada_w, ada_b, ln_g, ln_b, e_w_in, gmlp_norm_g, gmlp_norm_b, gmlp_ws, gmlp_bs, pool_w, pool_b, pool_scale, e_w_out, o_w_in, mla_q_norm_g, mla_kv_norm_g, mla_w_uq, mla_w_uk, mla_w_uv, o_w_out, loss_target, m_ada_w, m_ada_b, m_ln_g, m_ln_b, m_e_w_in, m_gmlp_norm_g, m_gmlp_norm_b, m_gmlp_ws, m_gmlp_bs, m_pool_w, m_pool_b, m_pool_scale, m_e_w_out, m_o_w_in, m_mla_q_norm_g, m_mla_kv_norm_g, m_mla_w_uq, m_mla_w_uk, m_mla_w_uv, m_o_w_out, v_ada_w, v_ada_b, v_ln_g, v_ln_b, v_e_w_in, v_gmlp_norm_g, v_gmlp_norm_b, v_gmlp_ws, v_gmlp_bs, v_pool_w, v_pool_b, v_pool_scale, v_e_w_out, v_o_w_in, v_mla_q_norm_g, v_mla_kv_norm_g, v_mla_w_uq, v_mla_w_uk, v_mla_w_uv, v_o_w_out):
    args = dict(locals())
    weights = {n: args[n] for n in WEIGHTS}
    mom = {n: args["m_" + n] for n in WEIGHTS}
    var = {n: args["v_" + n] for n in WEIGHTS}
    ax, ay, ac = lax.axis_index("x"), lax.axis_index("y"), lax.axis_index("c")
    chip = 2 * ax + ay
    dev = 2 * chip + ac
    d = D_MODEL
    x2 = x[0]
    target = loss_target[0]
    q_rank_sh = mla_q_norm_g.shape[1]

    empty_zone = lambda w: lax.dynamic_update_slice(lax.empty((N_CHIPS,) + w.shape, w.dtype), w[None], (chip, 0, 0))
    shards0 = [w.astype(BF16) for w in (pool_w[0].reshape(-1, POOL_GROUP_DIM), e_w_out[0])]
    shards1 = [w.astype(BF16) for w in (o_w_in[0], mla_w_uq[0].reshape(q_rank_sh, -1), o_w_out[0])]
    w_in0, = _gather_weights([e_w_in[0].astype(BF16)], name="gather_weights")
    flight0 = _chips_start(shards0, [empty_zone(w) for w in shards0], w_in0, scatter=False, name="gather0_start")
    flight1 = _chips_start(shards1, [empty_zone(w) for w in shards1], flight0[3], scatter=False, name="gather1_start")
    wuk_hrd = jnp.transpose(mla_w_uk[0], (1, 0, 2)).astype(BF16)
    wuk_hdr = jnp.transpose(mla_w_uk[0], (1, 2, 0)).astype(BF16)
    wuv_hrv = jnp.transpose(mla_w_uv[0], (1, 0, 2)).astype(BF16)
    wuv_hvr = jnp.transpose(mla_w_uv[0], (1, 2, 0)).astype(BF16)
    ws = gmlp_ws[0]
    ws_t = jnp.transpose(ws, (0, 2, 1))
    bs_t = _pad_cols(gmlp_bs[0].T, LANES)

    inv = 1.0 / (ROPE_THETA ** (jnp.arange(0, MLA_ROPE, 2, dtype=F32) / MLA_ROPE))
    ang = positions[0].astype(F32)[:, None] * inv
    cos_t = jnp.tile(jnp.cos(ang), (1, 4))
    sin_t = jnp.tile(jnp.concatenate([-jnp.sin(ang), jnp.sin(ang)], axis=1), (1, 2))

    c_all = _all_gather_devices(c.reshape(8, LANES), name="gather_c").reshape(N_DEV, d)
    cols = ada_w.shape[2]
    ada_b_mine = lax.dynamic_slice_in_dim(ada_b, chip * cols, cols, axis=1)[:, None, :]
    mod_sh = _ada_mod(c_all, ada_w, ada_b_mine, name="ada_mod")
    q_norm_rows = jnp.zeros((8, cols), F32).at[0, :q_rank_sh].set(mla_q_norm_g[0])
    mod_all = _all_gather_chips(jnp.concatenate([mod_sh.reshape(2 * N_DEV, cols), q_norm_rows]), name="gather_mod")
    q_norm_g = mod_all[:, 2 * N_DEV, :q_rank_sh].reshape(1, -1)
    mod_all = jnp.transpose(mod_all[:, :2 * N_DEV].reshape(N_CHIPS, 2, N_DEV, cols), (1, 2, 0, 3)).reshape(2, N_DEV, 3 * d)
    mod = lax.dynamic_index_in_dim(mod_all, dev, axis=1, keepdims=False)
    shift = [mod[l:l + 1, :d] for l in range(2)]
    scale = [mod[l:l + 1, d:2 * d] for l in range(2)]
    gate = [mod[l:l + 1, 2 * d:] for l in range(2)]

    scale[0] = scale[0] + flight1[3][:1, :1]
    h0 = _modulate(x2, scale[0], shift[0], name="modulate0")
    proj0 = _matmul(h0, w_in0, b_stacked=True, tm=1024, tn=1280, out_dtype=BF16, name="proj0")
    pool_w_g, w_out0 = _chips_wait(*flight0[:3], proj0, scatter=False, name="gather0_wait")
    pool_w_bf = jnp.transpose(pool_w_g.reshape(N_CHIPS, POOL_GROUPS, -1, POOL_GROUP_DIM), (1, 0, 2, 3)).reshape(
        POOL_GROUPS, POOL_GROUP_DIM, POOL_GROUP_DIM)
    w_out0 = w_out0.reshape(-1, d)
    mix0 = _even_fwd(proj0, ws, bs_t, gmlp_norm_g, gmlp_norm_b, pool_w_bf, pool_b, pool_scale, name="even_fwd")
    y0 = _matmul(mix0, w_out0, name="out0")
    x1, h1 = _resid_ln_modulate(x2, y0, gate[0], ln_g[0:1], ln_b[0:1], scale[1], shift[1], name="resid_ln0")

    w_in1_g, w_uq_g, w_out1 = _chips_wait(*flight1[:3], h1, scatter=False, name="gather1_wait")
    w_out1 = w_out1.reshape(-1, d)
    w_in1 = jnp.transpose(w_in1_g, (1, 0, 2)).reshape(d, ODD_IN)
    w_in1 = jnp.concatenate([_pad_cols(w_in1[:, :ODD_SMALL], ODD_SMALL_PAD), w_in1[:, ODD_SMALL:]], axis=1)
    w_uq = w_uq_g.reshape(MLA_Q_RANK, MLA_HEADS, MLA_NOPE + MLA_ROPE)
    w_uq_nope = w_uq[:, :, :MLA_NOPE].reshape(MLA_Q_RANK, -1)
    w_uq_rope = w_uq[:, :, MLA_NOPE:].reshape(MLA_Q_RANK, -1)
    proj1 = _matmul(h1, w_in1, tm=1024, tn=1280, out_dtype=BF16, name="proj1")
    q_cn, keys = _mla_prep(proj1, q_norm_g, mla_kv_norm_g, cos_t, sin_t, name="mla_prep")
    q_nope = _matmul(q_cn, w_uq_nope, tm=1024, tn=2048, name="q_nope", out_dtype=BF16)
    q_rope_pre = _matmul(q_cn, w_uq_rope, tm=1024, name="q_rope")
    q = _q_build(q_nope, q_rope_pre, wuk_hdr, cos_t, sin_t, name="q_build")
    o_lat, lse = _attn_fwd(q, keys, name="attn_fwd")
    og = _o_build(o_lat, wuv_hrv, proj1, name="o_build")
    y1 = _matmul(og, w_out1, name="out1")

    dy1, dres1, g_ln_g1, g_ln_b1, dgate1, loss = _loss_ln_bwd(x1, y1, gate[1], ln_g[1:2], ln_b[1:2], target, name="loss_ln1")
    dg1 = _matmul(dy1, w_out1, trans_b=True, tn=2048, out_dtype=BF16, name="d_og")
    g_w_out1 = _matmul(og, dy1, trans_a=True, out_dtype=BF16, tm=1024, name="g_out1")
    do_lat, dz, g_uv = _o_bwd(dg1, proj1, o_lat, wuv_hrv, wuv_hvr, name="o_bwd")
    dq, dkeys = _attn_bwd(q, keys, do_lat, o_lat, lse, name="attn_bwd")
    dq_nope, dq_rope, g_uk = _q_bwd(dq, q_nope, wuk_hrd, cos_t, sin_t, name="q_bwd")
    dq_cn = (_matmul(dq_nope, w_uq_nope, trans_b=True, tm=1024, name="d_qcn_nope")
             + _matmul(dq_rope, w_uq_rope, trans_b=True, tm=1024, name="d_qcn_rope"))
    g_uq_nope = _matmul(q_cn, dq_nope, trans_a=True, out_dtype=BF16, tn=2048, name="g_uq_nope")
    g_uq_rope = _matmul(q_cn, dq_rope, trans_a=True, out_dtype=BF16, name="g_uq_rope")
    dsmall, g_qg, g_kvg = _mla_prep_bwd(proj1, dq_cn, dkeys, q_norm_g, mla_kv_norm_g, cos_t, sin_t, name="mla_prep_bwd")
    dproj1 = jnp.concatenate([dsmall, dz], axis=1)
    dh1 = _matmul(dproj1, w_in1, trans_b=True, name="d_h1")
    g_w_in1 = _matmul(h1, dproj1, trans_a=True, out_dtype=BF16, tm=1024, tn=1280, name="g_in1")

    g_uq = jnp.concatenate([g_uq_nope.reshape(MLA_Q_RANK, MLA_HEADS, MLA_NOPE), g_uq_rope.reshape(MLA_Q_RANK, MLA_HEADS, MLA_ROPE)], axis=2)
    g_w_in1 = jnp.concatenate([g_w_in1[:, :ODD_SMALL], g_w_in1[:, ODD_SMALL_PAD:]], axis=1)
    g_w_in1 = jnp.transpose(g_w_in1.reshape(d, N_CHIPS, -1), (1, 0, 2))
    big1 = [
        _halves(g_w_in1),
        _halves(g_uq.reshape(N_CHIPS, q_rank_sh, -1)),
        _halves(g_w_out1.reshape(N_CHIPS, -1, d)),
        _halves(g_uk.astype(BF16).reshape(N_CHIPS, -1, MLA_NOPE)),
        _halves(g_uv.astype(BF16).reshape(N_CHIPS, -1, MLA_V)),
    ]
    parts1 = _reduce_sibling(big1, name="reduce_sibling1")
    lands2 = [lax.dynamic_update_slice(lax.empty(p.shape, BF16), lax.dynamic_slice_in_dim(p, chip, 1, axis=0), (chip, 0, 0))
              for p in parts1]
    flight2 = _chips_start(parts1, lands2, loss, scatter=True, name="reduce1_start")

    gate[0] = gate[0] + flight2[3][:1, :1]
    dy0, dres0, g_ln_g0, g_ln_b0, dgate0, dscale1, dshift1 = _mid_ln_bwd(
        x2, y0, gate[0], ln_g[0:1], ln_b[0:1], dh1, dres1, scale[1], x1, name="mid_ln0")
    dmix0 = _matmul(dy0, w_out0, trans_b=True, tn=2048, out_dtype=BF16, name="d_mix0")
    g_w_out0 = _matmul(mix0, dy0, trans_a=True, out_dtype=BF16, tm=1024, name="g_out0")
    dproj0, g_ws, g_bs_t, g_ng, g_nb, g_pw, g_pb, g_ps = _even_bwd(
        proj0, dmix0, ws, ws_t, bs_t, gmlp_norm_g, gmlp_norm_b, pool_w_bf, pool_b, pool_scale, name="even_bwd")
    g_w_in0 = _matmul(h0, dproj0, trans_a=True, out_dtype=BF16, out_stacked=True, tm=1024, tn=1280, name="g_in0")

    g_pw = jnp.transpose(g_pw.astype(BF16).reshape(POOL_GROUPS, N_CHIPS, -1, POOL_GROUP_DIM), (1, 0, 2, 3))
    big0 = [
        _halves(g_w_in0),
        _halves(g_pw.reshape(N_CHIPS, -1, POOL_GROUP_DIM)),
        _halves(g_w_out0.reshape(N_CHIPS, -1, d)),
        _halves(g_ws.astype(BF16)),
    ]
    parts0 = _reduce_sibling(big0, name="reduce_sibling0")
    landed1 = _chips_wait(*flight2[:3], parts0[0], scatter=True, name="reduce1_wait")
    lands3 = [lax.dynamic_update_slice(lax.empty(p.shape, BF16), lax.dynamic_slice_in_dim(p, chip, 1, axis=0), (chip, 0, 0))
              for p in parts0]
    flight3 = _chips_start(parts0, lands3, landed1[0], scatter=True, name="reduce0_start")
    dh0 = _matmul(dproj0, w_in0, trans_b=True, b_stacked=True, tm=1024, after=flight3[3], name="d_h0")
    grad_x, dscale0, dshift0 = _input_bwd(x2, dh0, dres0, scale[0], name="input_bwd")

    small_local = {
        "ln_g": jnp.concatenate([g_ln_g0, g_ln_g1]), "ln_b": jnp.concatenate([g_ln_b0, g_ln_b1]),
        "gmlp_norm_g": g_ng, "gmlp_norm_b": g_nb, "gmlp_bs": g_bs_t[:, :GMLP_HEADS].T, "pool_b": g_pb, "pool_scale": g_ps,
        "mla_kv_norm_g": g_kvg, "mla_q_norm_g": g_qg,
    }
    n_mod = 2 * 3 * d
    vec = jnp.concatenate([dshift0, dscale0, dgate0, dshift1, dscale1, dgate1]
                          + [small_local[n].reshape(1, -1) for n in SMALL] + [loss], axis=1)
    n_vec = vec.shape[1]
    vec = _pad_cols(vec, -(-n_vec // (8 * LANES)) * 8 * LANES).reshape(-1, LANES)
    vec_all = _all_gather_devices(vec, name="gather_small")
    vec_sum = _sum_devices(vec_all, name="sum_small").reshape(-1)
    dmod_all = vec_all.reshape(N_DEV, -1)[:, :n_mod].reshape(N_DEV, 2, 3 * d)
    dmod_sh = jnp.transpose(lax.dynamic_slice_in_dim(dmod_all, chip * cols, cols, axis=2), (1, 0, 2))
    dmod_sh = jnp.concatenate([dmod_sh, jnp.zeros((2, LANES - N_DEV, cols), F32)], axis=1)
    grads = {"ada_w": _ada_grad(_pad_cols(c_all.T, LANES), dmod_sh, name="ada_grad"), "ada_b": vec_sum[:n_mod].reshape(2, 3 * d)}
    off = n_mod
    for n in SMALL:
        sz = small_local[n].size
        grads[n] = vec_sum[off:off + sz]
        off += sz
    grads["mla_q_norm_g"] = lax.dynamic_slice_in_dim(grads["mla_q_norm_g"], chip * q_rank_sh, q_rank_sh)
    for n in SMALL:
        grads[n] = grads[n].reshape(weights[n].shape)

    landed0 = _chips_wait(*flight3[:3], grads["ada_w"], scatter=True, name="reduce0_wait")
    totals = _reduce_chips([], list(landed0) + list(landed1), name="reduce_chips")
    for n, t in zip(("e_w_in", "pool_w", "e_w_out", "gmlp_ws", "o_w_in", "mla_w_uq", "o_w_out"), totals):
        if n != "gmlp_ws":
            grads[n] = t.reshape(weights[n].shape)
    rep = jnp.concatenate([t.reshape(-1, LANES) for t in (totals[3], totals[7], totals[8])])
    rep_land = lax.dynamic_update_slice(lax.empty((N_CHIPS,) + rep.shape, F32), rep[None], (chip, 0, 0))
    flight4 = _chips_start([rep], [rep_land], totals[0], scatter=False, name="gather_rep_start")

    delta, new_m, new_v = {}, {}, {}
    replicated = ("gmlp_ws", "mla_w_uk", "mla_w_uv")
    large = [n for n in WEIGHTS if n not in SMALL and n != "ada_b"]
    for n in large:
        if n not in replicated:
            delta[n], new_m[n], new_v[n] = _adamw(weights[n], grads[n], mom[n], var[n], after=flight4[3], name="adamw_" + n)
    rep = _chips_wait(*flight4[:3], delta["e_w_in"], scatter=False, name="gather_rep_wait")[0]
    r_ws, r_uk = GMLP_BLOCK, 4 * MLA_KV_RANK
    grads["gmlp_ws"] = rep[:, :r_ws].reshape(weights["gmlp_ws"].shape)
    grads["mla_w_uk"] = jnp.transpose(rep[:, r_ws:r_ws + r_uk].reshape(MLA_HEADS, MLA_KV_RANK, MLA_NOPE), (1, 0, 2))[None]
    grads["mla_w_uv"] = jnp.transpose(rep[:, r_ws + r_uk:].reshape(MLA_HEADS, MLA_KV_RANK, MLA_V), (1, 0, 2))[None]
    for n in replicated:
        delta[n], new_m[n], new_v[n] = _adamw(weights[n], grads[n], mom[n], var[n], name="adamw_" + n)
    small = [n for n in WEIGHTS if n not in large]
    ds, ms, vs = _adamw_small([weights[n] for n in small], [grads[n] for n in small], [mom[n] for n in small],
                              [var[n] for n in small], name="adamw_small")
    for n, dn, mn, vn in zip(small, ds, ms, vs):
        delta[n], new_m[n], new_v[n] = dn, mn, vn

    return (vec_sum[n_vec - 1], grad_x[None], *[grads[n] for n in WEIGHTS], *[delta[n] for n in WEIGHTS],
            *[new_m[n] for n in WEIGHTS], *[new_v[n] for n in WEIGHTS])
```

```python
import jax
import jax.numpy as jnp
from jax import lax
from jax.experimental import pallas as pl
from jax.experimental.pallas import tpu as pltpu

F32 = jnp.float32
BF16 = jnp.bfloat16
MESH = pl.DeviceIdType.MESH

D_MODEL = 1024
CHUNK = 64
LN_EPS = 1e-5
GMLP_HEADS = 4
GMLP_HEAD_DIM = 256
GMLP_BLOCK = 128
POOL_WINDOWS = (2, 4, 8, 16)
POOL_GROUPS = 4
POOL_GROUP_DIM = 256
POOL_HALO = 16
EVEN_IN = 5120
MLA_HEADS = 16
MLA_NOPE = 128
MLA_ROPE = 64
MLA_V = 128
MLA_Q_RANK = 256
MLA_KV_RANK = 128
MLA_WIDTH = MLA_HEADS * MLA_V
ODD_IN = 2496
ODD_SMALL = MLA_Q_RANK + MLA_KV_RANK + MLA_ROPE
ODD_SMALL_PAD = 512
QK_PAD = 256
ROPE_THETA = 10000.0
ATTN_SCALE = (MLA_NOPE + MLA_ROPE) ** -0.5
DEEPNORM_ALPHA = (2.0 * 2) ** 0.25
ADAM_LR = 0.001
ADAM_B1 = 0.9
ADAM_B2 = 0.999
ADAM_EPS = 1e-08
ADAM_WD = 0.01
ADAM_STEP = 10
NEG = -1e30
LANES = 128
N_DEV = 8
N_CHIPS = 4
VMEM_LIMIT_BYTES = 56 * 1024 * 1024
HBM = pl.BlockSpec(memory_space=pltpu.HBM)
VMEM = pl.BlockSpec(memory_space=pltpu.VMEM)


def _params(*sem):
    return pltpu.CompilerParams(dimension_semantics=sem if sem else None, vmem_limit_bytes=VMEM_LIMIT_BYTES)


def _tile(dim, pref):
    for t in (pref, 2048, 1280, 1024, 512, 256, 128):
        if t <= min(pref, dim) and dim % t == 0:
            return t
    return dim


def _sigmoid(z):
    return 1.0 / (1.0 + jnp.exp(-z))


def _dot(a, b, dims):
    return lax.dot_general(a, b, (dims, ((), ())), preferred_element_type=F32)


NN = ((1,), (0,))
NT = ((1,), (1,))
TN = ((0,), (0,))


def _matmul(a, b, *, name, trans_a=False, trans_b=False, out_dtype=F32, b_stacked=False, out_stacked=False,
            tm=512, tn=1024, tk=2048, after=None):
    k, m = a.shape if trans_a else a.shape[::-1]
    if b_stacked:
        ns, kb, n_sh = b.shape
        kb, n = (ns * n_sh, kb) if trans_b else (kb, ns * n_sh)
    else:
        n, kb = b.shape if trans_b else b.shape[::-1]
    assert k == kb, (a.shape, b.shape)
    tm = _tile(m, tm)
    if b_stacked and trans_b:
        tn, tk = _tile(n, tn), n_sh
    elif b_stacked or out_stacked:
        tn, tk = _tile(n // N_CHIPS, tn), _tile(k, tk)
    else:
        tn, tk = _tile(n, tn), _tile(k, tk)
    nk = k // tk
    per = max((n // N_CHIPS) // tn, 1)
    dims = ((0 if trans_a else 1,), (1 if trans_b else 0,))

    def body_one(a_ref, b_ref, *rest):
        o_ref = rest[-1]
        o_ref[...] = _dot(a_ref[...].astype(BF16), b_ref[...].astype(BF16), dims).astype(out_dtype)

    def body_acc(a_ref, b_ref, *rest):
        o_ref, acc_ref = rest[-2:]
        kk = pl.program_id(2)

        @pl.when(kk == 0)
        def _():
            acc_ref[...] = jnp.zeros_like(acc_ref)

        acc_ref[...] += _dot(a_ref[...].astype(BF16), b_ref[...].astype(BF16), dims)

        @pl.when(kk == nk - 1)
        def _():
            o_ref[...] = acc_ref[...].astype(out_dtype)

    a_spec = pl.BlockSpec((tk, tm), lambda i, j, kk: (kk, i)) if trans_a else pl.BlockSpec((tm, tk), lambda i, j, kk: (i, kk))
    if b_stacked and trans_b:
        b_spec = pl.BlockSpec((None, tn, tk), lambda i, j, kk: (kk, j, 0))
    elif b_stacked:
        b_spec = pl.BlockSpec((None, tk, tn), lambda i, j, kk: (j // per, kk, j % per))
    elif trans_b:
        b_spec = pl.BlockSpec((tn, tk), lambda i, j, kk: (j, kk))
    else:
        b_spec = pl.BlockSpec((tk, tn), lambda i, j, kk: (kk, j))
    if out_stacked:
        o_spec = pl.BlockSpec((None, tm, tn), lambda i, j, kk: (j // per, i, j % per))
        o_shape = jax.ShapeDtypeStruct((N_CHIPS, m, n // N_CHIPS), out_dtype)
    else:
        o_spec = pl.BlockSpec((tm, tn), lambda i, j, kk: (i, j))
        o_shape = jax.ShapeDtypeStruct((m, n), out_dtype)
    order = [] if after is None else [after]
    return pl.pallas_call(
        body_one if nk == 1 else body_acc, name=name, grid=(m // tm, n // tn, nk),
        in_specs=[a_spec, b_spec] + [pl.BlockSpec(memory_space=pl.ANY)] * len(order),
        out_specs=o_spec, out_shape=o_shape, scratch_shapes=[] if nk == 1 else [pltpu.VMEM((tm, tn), F32)],
        compiler_params=_params("parallel", "parallel", "arbitrary"),
    )(a, b, *order)


def _matmul_rows(a, b, epilogue, row_ins, vec_ins, row_outs, vec_outs, *, name, trans_b=False, b_stacked=False,
                 tm=512, tk=2048, after=None):
    m, k = a.shape
    if b_stacked:
        ns, n, n_sh = b.shape
        assert trans_b and ns * n_sh == k
        tk = n_sh
    else:
        n = b.shape[0] if trans_b else b.shape[1]
        tk = _tile(k, tk)
    tm = _tile(m, tm)
    nk = k // tk
    dims = ((1,), (1 if trans_b else 0,))
    n_ri, n_vi, n_ro, n_vo = len(row_ins), len(vec_ins), len(row_outs), len(vec_outs)
    order = [] if after is None else [after]

    def body(*refs):
        a_ref, b_ref = refs[:2]
        pos = 2
        rin = refs[pos:pos + n_ri]
        pos += n_ri
        vin = refs[pos:pos + n_vi]
        pos += n_vi + len(order)
        rout = refs[pos:pos + n_ro]
        pos += n_ro
        vout = refs[pos:pos + n_vo]
        first = pl.program_id(0) == 0
        part = _dot(a_ref[...].astype(BF16), b_ref[...].astype(BF16), dims)
        if nk == 1:
            epilogue(part, first, rin, vin, rout, vout)
        else:
            acc_ref = refs[-1]
            kk = pl.program_id(1)

            @pl.when(kk == 0)
            def _():
                acc_ref[...] = part

            @pl.when(kk > 0)
            def _():
                acc_ref[...] += part

            @pl.when(kk == nk - 1)
            def _():
                epilogue(acc_ref[...], first, rin, vin, rout, vout)

    a_spec = pl.BlockSpec((tm, tk), lambda i, kk: (i, kk))
    if b_stacked:
        b_spec = pl.BlockSpec((None, n, tk), lambda i, kk: (kk, 0, 0))
    elif trans_b:
        b_spec = pl.BlockSpec((n, tk), lambda i, kk: (0, kk))
    else:
        b_spec = pl.BlockSpec((tk, n), lambda i, kk: (kk, 0))
    row = pl.BlockSpec((tm, n), lambda i, kk: (i, 0))
    vec = lambda w: pl.BlockSpec((1, w), lambda i, kk: (0, 0))
    return pl.pallas_call(
        body, name=name, grid=(m // tm, nk),
        in_specs=[a_spec, b_spec] + [row] * n_ri + [vec(v.shape[1]) for v in vec_ins] + [pl.BlockSpec(memory_space=pl.ANY)] * len(order),
        out_specs=[row] * n_ro + [vec(w) for w in vec_outs],
        out_shape=[jax.ShapeDtypeStruct((m, n), dt) for dt in row_outs] + [jax.ShapeDtypeStruct((1, w), F32) for w in vec_outs],
        scratch_shapes=[] if nk == 1 else [pltpu.VMEM((tm, n), F32)],
        compiler_params=_params("arbitrary", "arbitrary"),
    )(a, b, *row_ins, *vec_ins, *order)


def _row_spec(ts, d):
    return pl.BlockSpec((ts, d), lambda i: (i, 0))


def _vec_spec(d):
    return pl.BlockSpec((1, d), lambda i: (0, 0))


def _modulate(x, scale, shift, *, name):
    s, d = x.shape
    ts = _tile(s, 512)

    def body(x_ref, sc_ref, sh_ref, h_ref):
        h_ref[...] = (x_ref[...] * (1.0 + sc_ref[...]) + sh_ref[...]).astype(BF16)

    return pl.pallas_call(
        body, name=name, grid=(s // ts,), in_specs=[_row_spec(ts, d), _vec_spec(d), _vec_spec(d)],
        out_specs=_row_spec(ts, d), out_shape=jax.ShapeDtypeStruct((s, d), BF16), compiler_params=_params("parallel"),
    )(x, scale, shift)


def _ln_stats(pre):
    mu = jnp.mean(pre, axis=-1, keepdims=True)
    xc = pre - mu
    var = jnp.mean(xc * xc, axis=-1, keepdims=True)
    rstd = lax.rsqrt(var + LN_EPS)
    return xc * rstd, rstd


def _ln_bwd_rows(dout, xhat, rstd, g):
    dxh = dout * g
    m1 = jnp.mean(dxh, axis=-1, keepdims=True)
    m2 = jnp.mean(dxh * xhat, axis=-1, keepdims=True)
    return rstd * (dxh - m1 - xhat * m2)


def _colsum(v):
    return jnp.sum(v, axis=0, keepdims=True)


def _out_resid_ln(mix, w_out, x, gate, g, b, scale_next, shift_next, *, name):
    def epilogue(y, first, rin, vin, rout, vout):
        (x_ref,), (gate_ref, g_ref, b_ref, sc_ref, sh_ref), (y_ref, xn_ref, h_ref) = rin, vin, rout
        y_ref[...] = y
        pre = DEEPNORM_ALPHA * x_ref[...] + (1.0 + gate_ref[...]) * y
        xhat, _ = _ln_stats(pre)
        xn = xhat * g_ref[...] + b_ref[...]
        xn_ref[...] = xn
        h_ref[...] = (xn * (1.0 + sc_ref[...]) + sh_ref[...]).astype(BF16)

    return _matmul_rows(mix, w_out, epilogue, [x], [gate, g, b, scale_next, shift_next], [F32, F32, BF16], [], name=name)


def _out_loss_ln_bwd(og, w_out, x, gate, g, b, target, *, name):
    d = x.shape[1]

    def epilogue(yv, first, rin, vin, rout, vout):
        (x_ref, t_ref), (gate_ref, g_ref, b_ref), (dy_ref, dres_ref), (dg_ref, db_ref, dgate_ref, loss_ref) = rin, vin, rout, vout

        @pl.when(first)
        def _():
            for r in vout:
                r[...] = jnp.zeros_like(r)

        pre = DEEPNORM_ALPHA * x_ref[...] + (1.0 + gate_ref[...]) * yv
        xhat, rstd = _ln_stats(pre)
        diff = xhat * g_ref[...] + b_ref[...] - t_ref[...]
        loss_ref[...] += (0.5 / d) * jnp.sum(jnp.sum(diff * diff, axis=1, keepdims=True), axis=0, keepdims=True)
        dout = diff * (1.0 / d)
        dpre = _ln_bwd_rows(dout, xhat, rstd, g_ref[...])
        dy_ref[...] = (dpre * (1.0 + gate_ref[...])).astype(BF16)
        dres_ref[...] = DEEPNORM_ALPHA * dpre
        dg_ref[...] += _colsum(dout * xhat)
        db_ref[...] += _colsum(dout)
        dgate_ref[...] += _colsum(dpre * yv)

    return _matmul_rows(og, w_out, epilogue, [x, target], [gate, g, b], [BF16, F32], [d, d, d, 1], name=name)


def _dh_mid_ln_bwd(dproj, w_in, x, y, gate, g, b, dres_next, scale_next, x_next, *, name):
    d = x.shape[1]

    def epilogue(dh, first, rin, vin, rout, vout):
        (x_ref, y_ref, dr_ref, xn_ref), (gate_ref, g_ref, b_ref, sc_ref), (dy_ref, dres_ref) = rin, vin, rout
        dg_ref, db_ref, dgate_ref, dscale_ref, dshift_ref = vout

        @pl.when(first)
        def _():
            for r in vout:
                r[...] = jnp.zeros_like(r)

        dout = dr_ref[...] + dh * (1.0 + sc_ref[...])
        dscale_ref[...] += _colsum(dh * xn_ref[...])
        dshift_ref[...] += _colsum(dh)
        yv = y_ref[...]
        pre = DEEPNORM_ALPHA * x_ref[...] + (1.0 + gate_ref[...]) * yv
        xhat, rstd = _ln_stats(pre)
        dpre = _ln_bwd_rows(dout, xhat, rstd, g_ref[...])
        dy_ref[...] = (dpre * (1.0 + gate_ref[...])).astype(BF16)
        dres_ref[...] = DEEPNORM_ALPHA * dpre
        dg_ref[...] += _colsum(dout * xhat)
        db_ref[...] += _colsum(dout)
        dgate_ref[...] += _colsum(dpre * yv)

    return _matmul_rows(dproj, w_in, epilogue, [x, y, dres_next, x_next], [gate, g, b, scale_next], [BF16, F32], [d] * 5,
                        trans_b=True, tk=1280, name=name)


def _dh_input_bwd(dproj, w_in_stacked, x, dres, scale, *, name, after):
    d = x.shape[1]

    def epilogue(dh, first, rin, vin, rout, vout):
        (x_ref, dr_ref), (sc_ref,), (dx_ref,), (dscale_ref, dshift_ref) = rin, vin, rout, vout

        @pl.when(first)
        def _():
            for r in vout:
                r[...] = jnp.zeros_like(r)

        dx_ref[...] = dr_ref[...] + dh * (1.0 + sc_ref[...])
        dscale_ref[...] += _colsum(dh * x_ref[...])
        dshift_ref[...] += _colsum(dh)

    return _matmul_rows(dproj, w_in_stacked, epilogue, [x, dres], [scale], [F32], [d, d], trans_b=True, b_stacked=True,
                        tm=1024, after=after, name=name)


def _chunk_mask(transposed=False):
    r = lax.broadcasted_iota(jnp.int32, (GMLP_BLOCK, GMLP_BLOCK), 0) // CHUNK
    c = lax.broadcasted_iota(jnp.int32, (GMLP_BLOCK, GMLP_BLOCK), 1) // CHUNK
    return (r <= c) if transposed else (c <= r)


def _window_sum(ext, steps, forward):
    rows = ext.shape[0]
    acc = ext
    for k in range(steps):
        shift = 1 << k
        acc = acc + pltpu.roll(acc, (rows - shift) if forward else shift, 0)
    return acc


def _pool_counts(first_row, rows, win):
    t = first_row + lax.broadcasted_iota(jnp.int32, (rows, 1), 0)
    return jnp.minimum(t + 1, win).astype(F32)


def _even_specs(t):
    col = lambda j: pl.BlockSpec((t, D_MODEL), lambda n: (n, j))
    per = t // POOL_HALO
    prev = pl.BlockSpec((POOL_HALO, D_MODEL), lambda n: (jnp.maximum(n * per - 1, 0), 3))
    return col, per, prev


def _full(shape):
    return pl.BlockSpec(shape, lambda n: (0,) * len(shape))


def _gmlp_head(v_h, ng, nb, w_bf):
    xhat, rstd = _ln_stats(v_h)
    vn = (xhat * ng + nb).astype(BF16)
    return xhat, rstd, vn, _dot(w_bf, vn, NN)


def _pool_group(xb_g, prev_g, first_row, grp):
    t = xb_g.shape[0]
    ext = jnp.concatenate([prev_g, xb_g], axis=0)
    tot = _window_sum(ext, grp + 1, False)[POOL_HALO:, :]
    cnt = _pool_counts(first_row, t, POOL_WINDOWS[grp])
    return tot / cnt - xb_g, cnt


def _even_fwd(proj, ws, bs_t, ng, nb, pool_w, pool_b, pool_scale, *, name):
    s = proj.shape[0]
    t = GMLP_BLOCK
    col, per, prev = _even_specs(t)

    def body(u_ref, v_ref, za_ref, xb_ref, zb_ref, xp_ref, ws_ref, bs_ref, ng_ref, nb_ref, pw_ref, pb_ref, ps_ref, o_ref):
        n = pl.program_id(0)
        mask = _chunk_mask()
        for h in range(GMLP_HEADS):
            c0 = h * GMLP_HEAD_DIM
            cs = slice(c0, c0 + GMLP_HEAD_DIM)
            w_bf = jnp.where(mask, ws_ref[h], 0.0).astype(BF16)
            _, _, _, sv = _gmlp_head(v_ref[:, cs].astype(F32),ng_ref[...], nb_ref[...], w_bf)
            sv = sv + bs_ref[:, h:h + 1]
            za = za_ref[:, cs].astype(F32)
            o_ref[:, cs] = (u_ref[:, cs].astype(F32) * sv * (za * _sigmoid(za))).astype(BF16)
        live = (n > 0).astype(F32)
        for grp in range(POOL_GROUPS):
            c0 = grp * POOL_GROUP_DIM
            cs = slice(c0, c0 + POOL_GROUP_DIM)
            pooled, _ = _pool_group(xb_ref[:, cs].astype(F32), xp_ref[:, cs].astype(F32) * live, n * t, grp)
            yb = _dot(pooled.astype(BF16), pw_ref[grp], NN) + pb_ref[:, cs]
            zb = zb_ref[:, cs].astype(F32)
            o_ref[:, D_MODEL + c0:D_MODEL + c0 + POOL_GROUP_DIM] = (yb * ps_ref[:, cs] * (zb * _sigmoid(zb))).astype(BF16)

    return pl.pallas_call(
        body, name=name, grid=(s // t,),
        in_specs=[col(0), col(1), col(2), col(3), col(4), prev,
                  _full((GMLP_HEADS, t, t)), _full((t, LANES)), _full((1, GMLP_HEAD_DIM)), _full((1, GMLP_HEAD_DIM)),
                  _full((POOL_GROUPS, POOL_GROUP_DIM, POOL_GROUP_DIM)), _full((1, D_MODEL)), _full((1, D_MODEL))],
        out_specs=pl.BlockSpec((t, 2 * D_MODEL), lambda n: (n, 0)),
        out_shape=jax.ShapeDtypeStruct((s, 2 * D_MODEL), BF16),
        compiler_params=_params("parallel"),
    )(proj, proj, proj, proj, proj, proj, ws, bs_t, ng, nb, pool_w, pool_b, pool_scale)


def _even_bwd(proj, dmix, ws, ws_t, bs_t, ng, nb, pool_w, pool_b, pool_scale, *, name):
    s = proj.shape[0]
    t = GMLP_BLOCK
    nblk = s // t
    col, per, prev = _even_specs(t)
    nxt = lambda j: pl.BlockSpec((POOL_HALO, D_MODEL), lambda n: (jnp.minimum((n + 1) * per, nblk * per - 1), j))

    def body(u_ref, v_ref, za_ref, xb_ref, zb_ref, xp_ref, zn_ref, da_ref, db_ref, dbn_ref,
             ws_ref, wst_ref, bs_ref, ng_ref, nb_ref, pw_ref, pb_ref, ps_ref,
             dp_ref, gws_ref, gbs_ref, gng_ref, gnb_ref, gpw_ref, gpb_ref, gps_ref):
        n = pl.program_id(0)

        @pl.when(n == 0)
        def _():
            for r in (gws_ref, gbs_ref, gng_ref, gnb_ref, gpw_ref, gpb_ref, gps_ref):
                r[...] = jnp.zeros_like(r)

        mask, mask_t = _chunk_mask(), _chunk_mask(True)
        lane = lax.broadcasted_iota(jnp.int32, (t, LANES), 1)
        ngv, nbv = ng_ref[...], nb_ref[...]
        for h in range(GMLP_HEADS):
            c0 = h * GMLP_HEAD_DIM
            cs = slice(c0, c0 + GMLP_HEAD_DIM)
            w_bf = jnp.where(mask, ws_ref[h], 0.0).astype(BF16)
            wt_bf = jnp.where(mask_t, wst_ref[h], 0.0).astype(BF16)
            xhat, rstd, vn, sv = _gmlp_head(v_ref[:, cs].astype(F32),ngv, nbv, w_bf)
            sv = sv + bs_ref[:, h:h + 1]
            za, u, da = za_ref[:, cs].astype(F32), u_ref[:, cs].astype(F32), da_ref[:, cs].astype(F32)
            sg = _sigmoid(za)
            sl = za * sg
            dp_ref[:, cs] = (da * sv * sl).astype(BF16)
            dp_ref[:, 2 * D_MODEL + c0:2 * D_MODEL + c0 + GMLP_HEAD_DIM] = (
                da * u * sv * (sg * (1.0 + za * (1.0 - sg)))).astype(BF16)
            dsv = da * u * sl
            gbs_ref[...] += jnp.where(lane == h, jnp.sum(dsv, axis=1, keepdims=True), 0.0)
            dsv_bf = dsv.astype(BF16)
            gws_ref[h] += jnp.where(mask, _dot(dsv_bf, vn, NT), 0.0)
            dvn = _dot(wt_bf, dsv_bf, NN)
            dp_ref[:, D_MODEL + c0:D_MODEL + c0 + GMLP_HEAD_DIM] = _ln_bwd_rows(dvn, xhat, rstd, ngv).astype(BF16)
            gng_ref[...] += _colsum(dvn * xhat)
            gnb_ref[...] += _colsum(dvn)
        live_prev = (n > 0).astype(F32)
        live_next = (n < nblk - 1).astype(F32)
        for grp in range(POOL_GROUPS):
            c0 = grp * POOL_GROUP_DIM
            cs = slice(c0, c0 + POOL_GROUP_DIM)
            xb = xb_ref[:, cs].astype(F32)
            pooled, cnt = _pool_group(xb, xp_ref[:, cs].astype(F32) * live_prev, n * t, grp)
            pooled_bf = pooled.astype(BF16)
            pw = pw_ref[grp]
            yb = _dot(pooled_bf, pw, NN) + pb_ref[:, cs]
            ps = ps_ref[:, cs]
            zb, db = zb_ref[:, cs].astype(F32), db_ref[:, cs].astype(F32)
            sg = _sigmoid(zb)
            sl = zb * sg
            dp_ref[:, 4 * D_MODEL + c0:4 * D_MODEL + c0 + POOL_GROUP_DIM] = (
                db * yb * ps * (sg * (1.0 + zb * (1.0 - sg)))).astype(BF16)
            dsl = db * sl
            dy = dsl * ps
            gps_ref[:, cs] += _colsum(dsl * yb)
            gpb_ref[:, cs] += _colsum(dy)
            dy_bf = dy.astype(BF16)
            gpw_ref[grp] += _dot(pooled_bf, dy_bf, TN)
            r = _dot(dy_bf, pw, NT)
            zn = zn_ref[:, cs].astype(F32)
            dyn = (dbn_ref[:, cs].astype(F32) * (zn * _sigmoid(zn)) * ps * live_next).astype(BF16)
            rn = _dot(dyn, pw, NT) / _pool_counts((n + 1) * t, POOL_HALO, POOL_WINDOWS[grp])
            ext = jnp.concatenate([r / cnt, rn], axis=0)
            dxb = _window_sum(ext, grp + 1, True)[:t, :] - r
            dp_ref[:, 3 * D_MODEL + c0:3 * D_MODEL + c0 + POOL_GROUP_DIM] = dxb.astype(BF16)

    out_shape = [
        jax.ShapeDtypeStruct((s, EVEN_IN), BF16),
        jax.ShapeDtypeStruct((GMLP_HEADS, t, t), F32), jax.ShapeDtypeStruct((t, LANES), F32),
        jax.ShapeDtypeStruct((1, GMLP_HEAD_DIM), F32), jax.ShapeDtypeStruct((1, GMLP_HEAD_DIM), F32),
        jax.ShapeDtypeStruct((POOL_GROUPS, POOL_GROUP_DIM, POOL_GROUP_DIM), F32),
        jax.ShapeDtypeStruct((1, D_MODEL), F32), jax.ShapeDtypeStruct((1, D_MODEL), F32),
    ]
    return pl.pallas_call(
        body, name=name, grid=(nblk,),
        in_specs=[col(0), col(1), col(2), col(3), col(4), prev, nxt(4),
                  pl.BlockSpec((t, D_MODEL), lambda n: (n, 0)), pl.BlockSpec((t, D_MODEL), lambda n: (n, 1)), nxt(1),
                  _full((GMLP_HEADS, t, t)), _full((GMLP_HEADS, t, t)), _full((t, LANES)),
                  _full((1, GMLP_HEAD_DIM)), _full((1, GMLP_HEAD_DIM)),
                  _full((POOL_GROUPS, POOL_GROUP_DIM, POOL_GROUP_DIM)), _full((1, D_MODEL)), _full((1, D_MODEL))],
        out_specs=[pl.BlockSpec((t, EVEN_IN), lambda n: (n, 0))] + [_full(o.shape) for o in out_shape[1:]],
        out_shape=out_shape,
        compiler_params=_params("arbitrary"),
    )(proj, proj, proj, proj, proj, proj, proj, dmix, dmix, dmix, ws, ws_t, bs_t, ng, nb, pool_w, pool_b, pool_scale)


def _half_swap(v):
    lane = lax.broadcasted_iota(jnp.int32, v.shape, 1)
    return jnp.where(lane % MLA_ROPE < MLA_ROPE // 2, pltpu.roll(v, LANES - MLA_ROPE // 2, 1), pltpu.roll(v, MLA_ROPE // 2, 1))


def _rope(v, cos, sin_signed):
    return v * cos + _half_swap(v) * sin_signed


def _rope_bwd(d, cos, sin_signed):
    return d * cos + _half_swap(d * sin_signed)


def _rms(v, g):
    r = lax.rsqrt(jnp.mean(v * v, axis=-1, keepdims=True) + LN_EPS)
    return v * r * g, r


def _rms_bwd(dy, v, r, g):
    u = dy * g
    return r * u - v * (r * r * r) * jnp.mean(u * v, axis=-1, keepdims=True)


def _mla_prep(proj, gq, gkv, cos, sin_signed, *, name):
    s = proj.shape[0]
    ts = _tile(s, 512)

    def body(p_ref, gq_ref, gkv_ref, c_ref, s_ref, q_ref, k_ref):
        qcn, _ = _rms(p_ref[:, :MLA_Q_RANK].astype(F32), gq_ref[...])
        kvn, _ = _rms(p_ref[:, MLA_Q_RANK:MLA_Q_RANK + MLA_KV_RANK].astype(F32), gkv_ref[...])
        kr = _rope(p_ref[:, MLA_Q_RANK + MLA_KV_RANK:].astype(F32), c_ref[...], s_ref[...])
        q_ref[...] = qcn.astype(BF16)
        k_ref[...] = jnp.concatenate([kvn, kr], axis=1).astype(BF16)

    return pl.pallas_call(
        body, name=name, grid=(s // ts,),
        in_specs=[_row_spec(ts, ODD_SMALL_PAD), _vec_spec(MLA_Q_RANK), _vec_spec(MLA_KV_RANK), _row_spec(ts, LANES), _row_spec(ts, LANES)],
        out_specs=[_row_spec(ts, MLA_Q_RANK), _row_spec(ts, QK_PAD)],
        out_shape=[jax.ShapeDtypeStruct((s, MLA_Q_RANK), BF16), jax.ShapeDtypeStruct((s, QK_PAD), BF16)],
        compiler_params=_params("parallel"),
    )(proj, gq, gkv, cos, sin_signed)


def _mla_prep_bwd(proj, dqcn, dkv, gq, gkv, cos, sin_signed, *, name):
    s = proj.shape[0]
    ts = _tile(s, 512)

    def body(p_ref, dq_ref, dkv_ref, gq_ref, gkv_ref, c_ref, s_ref, ds_ref, ggq_ref, ggkv_ref):
        @pl.when(pl.program_id(0) == 0)
        def _():
            ggq_ref[...] = jnp.zeros_like(ggq_ref)
            ggkv_ref[...] = jnp.zeros_like(ggkv_ref)

        qc = p_ref[:, :MLA_Q_RANK].astype(F32)
        kvc = p_ref[:, MLA_Q_RANK:MLA_Q_RANK + MLA_KV_RANK].astype(F32)
        _, rq = _rms(qc, gq_ref[...])
        _, rkv = _rms(kvc, gkv_ref[...])
        dq = dq_ref[...]
        dkvn = dkv_ref[:, :MLA_KV_RANK]
        ggq_ref[...] += _colsum(dq * qc * rq)
        ggkv_ref[...] += _colsum(dkvn * kvc * rkv)
        dkr = _rope_bwd(dkv_ref[:, MLA_KV_RANK:], c_ref[...], s_ref[...])
        ds_ref[...] = jnp.concatenate(
            [_rms_bwd(dq, qc, rq, gq_ref[...]), _rms_bwd(dkvn, kvc, rkv, gkv_ref[...]), dkr], axis=1).astype(BF16)

    return pl.pallas_call(
        body, name=name, grid=(s // ts,),
        in_specs=[_row_spec(ts, ODD_SMALL_PAD), _row_spec(ts, MLA_Q_RANK), _row_spec(ts, QK_PAD),
                  _vec_spec(MLA_Q_RANK), _vec_spec(MLA_KV_RANK), _row_spec(ts, LANES), _row_spec(ts, LANES)],
        out_specs=[_row_spec(ts, ODD_SMALL_PAD), _vec_spec(MLA_Q_RANK), _vec_spec(MLA_KV_RANK)],
        out_shape=[jax.ShapeDtypeStruct((s, ODD_SMALL_PAD), BF16), jax.ShapeDtypeStruct((1, MLA_Q_RANK), F32),
                   jax.ShapeDtypeStruct((1, MLA_KV_RANK), F32)],
        compiler_params=_params("arbitrary"),
    )(proj, dqcn, dkv, gq, gkv, cos, sin_signed)


LOG2_E = 1.4426950408889634
Q_PRESCALE = ATTN_SCALE * LOG2_E


def _q_build(q_nope, q_rope_pre, wuk_hdr, cos, sin_signed, *, name):
    s = q_nope.shape[0]
    ts = _tile(s, 1024)

    def body(qn_ref, qr_ref, w_ref, c_ref, s_ref, o_ref):
        r = _rope(qr_ref[...], c_ref[...], s_ref[...])
        lane = lax.broadcasted_iota(jnp.int32, (ts, LANES), 1)
        for j in range(2):
            ql = _dot(qn_ref[:, j * MLA_NOPE:(j + 1) * MLA_NOPE], w_ref[j], NN)
            rr = r if j == 0 else pltpu.roll(r, MLA_ROPE, 1)
            o_ref[j] = (jnp.concatenate([ql, jnp.where(lane < MLA_ROPE, rr, 0.0)], axis=1) * Q_PRESCALE).astype(BF16)

    return pl.pallas_call(
        body, name=name, grid=(s // ts, MLA_HEADS // 2),
        in_specs=[pl.BlockSpec((ts, 2 * MLA_NOPE), lambda i, p: (i, p)), pl.BlockSpec((ts, LANES), lambda i, p: (i, p)),
                  pl.BlockSpec((2, MLA_NOPE, MLA_KV_RANK), lambda i, p: (p, 0, 0)),
                  pl.BlockSpec((ts, LANES), lambda i, p: (i, 0)), pl.BlockSpec((ts, LANES), lambda i, p: (i, 0))],
        out_specs=pl.BlockSpec((2, ts, QK_PAD), lambda i, p: (p, i, 0)),
        out_shape=jax.ShapeDtypeStruct((MLA_HEADS, s, QK_PAD), BF16),
        compiler_params=_params("parallel", "parallel"),
    )(q_nope, q_rope_pre, wuk_hdr, cos, sin_signed)


def _q_bwd(dq, q_nope, wuk_hrd, cos, sin_signed, *, name):
    s = q_nope.shape[0]
    ts = _tile(s, 1024)

    def body(dq_ref, qn_ref, w_ref, c_ref, s_ref, dn_ref, dr_ref, gw_ref):
        @pl.when(pl.program_id(1) == 0)
        def _():
            gw_ref[...] = jnp.zeros_like(gw_ref)

        lane = lax.broadcasted_iota(jnp.int32, (ts, LANES), 1)
        for j in range(2):
            dql = dq_ref[j, :, :MLA_KV_RANK]
            dn_ref[:, j * MLA_NOPE:(j + 1) * MLA_NOPE] = _dot(dql, w_ref[j], NN).astype(BF16)
            gw_ref[j] += _dot(dql, qn_ref[:, j * MLA_NOPE:(j + 1) * MLA_NOPE], TN)
        hi0 = dq_ref[0, :, MLA_KV_RANK:].astype(F32)
        hi1 = dq_ref[1, :, MLA_KV_RANK:].astype(F32)
        d = jnp.where(lane < MLA_ROPE, hi0, pltpu.roll(hi1, MLA_ROPE, 1))
        dr_ref[...] = _rope_bwd(d, c_ref[...], s_ref[...]).astype(BF16)

    return pl.pallas_call(
        body, name=name, grid=(MLA_HEADS // 2, s // ts),
        in_specs=[pl.BlockSpec((2, ts, QK_PAD), lambda p, i: (p, i, 0)), pl.BlockSpec((ts, 2 * MLA_NOPE), lambda p, i: (i, p)),
                  pl.BlockSpec((2, MLA_KV_RANK, MLA_NOPE), lambda p, i: (p, 0, 0)),
                  pl.BlockSpec((ts, LANES), lambda p, i: (i, 0)), pl.BlockSpec((ts, LANES), lambda p, i: (i, 0))],
        out_specs=[pl.BlockSpec((ts, 2 * MLA_NOPE), lambda p, i: (i, p)), pl.BlockSpec((ts, LANES), lambda p, i: (i, p)),
                   pl.BlockSpec((2, MLA_KV_RANK, MLA_NOPE), lambda p, i: (p, 0, 0))],
        out_shape=[jax.ShapeDtypeStruct((s, MLA_HEADS * MLA_NOPE), BF16), jax.ShapeDtypeStruct((s, MLA_HEADS * MLA_ROPE), BF16),
                   jax.ShapeDtypeStruct((MLA_HEADS, MLA_KV_RANK, MLA_NOPE), F32)],
        compiler_params=_params("parallel", "arbitrary"),
    )(dq, q_nope, wuk_hrd, cos, sin_signed)


ATTN_BQ = 128
ATTN_BK = 512


def _diag_mask(rows, bq, bk, q0, k0):
    qc = (q0 + lax.broadcasted_iota(jnp.int32, (rows, bk), 0) % bq) // CHUNK
    kc = (k0 + lax.broadcasted_iota(jnp.int32, (rows, bk), 1)) // CHUNK
    return kc <= qc


def _attn_fwd(q, k, *, name):
    nh, s, dk = q.shape
    bq, bk = _tile(s, ATTN_BQ), _tile(s, ATTN_BK)
    rows = nh * bq

    def body(q_ref, k_ref, o_ref, lse_ref):
        i = pl.program_id(0)
        qb = q_ref[...].reshape(rows, dk)
        n_before = (i * bq) // bk

        def step(j, width, carry, masked):
            m, l, acc = carry
            k0 = pl.multiple_of(j * bk, bk)
            kb = k_ref[pl.ds(k0, width), :]
            sc = _dot(qb, kb, NT)
            if masked:
                sc = jnp.where(_diag_mask(rows, bq, width, i * bq, k0), sc, NEG)
            m_new = jnp.maximum(m, jnp.max(sc, axis=1, keepdims=True))
            p = jnp.exp2(sc - m_new)
            a = jnp.exp2(m - m_new)
            l = a * l + jnp.sum(p, axis=1, keepdims=True)
            acc = a * acc + _dot(p.astype(BF16), kb[:, :MLA_KV_RANK], NN)
            return m_new, l, acc

        init = (jnp.full((rows, 1), NEG, F32), jnp.zeros((rows, 1), F32), jnp.zeros((rows, MLA_KV_RANK), F32))
        carry = lax.fori_loop(0, n_before, lambda j, c: step(j, bk, c, False), init)
        for part in range(bk // bq):
            @pl.when(i % (bk // bq) == part)
            def _(part=part):
                m, l, acc = step(n_before, (part + 1) * bq, carry, True)
                o_ref[...] = (acc / l).astype(BF16).reshape(nh, bq, MLA_KV_RANK)
                lse_ref[...] = jnp.broadcast_to(m + jnp.log2(l), (rows, LANES)).reshape(nh, bq, LANES)

    return pl.pallas_call(
        body, name=name, grid=(s // bq,),
        in_specs=[pl.BlockSpec((nh, bq, dk), lambda i: (0, i, 0)), pl.BlockSpec((s, dk), lambda i: (0, 0))],
        out_specs=[pl.BlockSpec((nh, bq, MLA_KV_RANK), lambda i: (0, i, 0)), pl.BlockSpec((nh, bq, LANES), lambda i: (0, i, 0))],
        out_shape=[jax.ShapeDtypeStruct((nh, s, MLA_KV_RANK), BF16), jax.ShapeDtypeStruct((nh, s, LANES), F32)],
        compiler_params=_params("parallel"),
    )(q, k)


def _attn_bwd(q, k, do, o, lse, *, name):
    nh, s, dk = q.shape
    bq, bk = _tile(s, ATTN_BQ), _tile(s, ATTN_BK)
    rows = nh * bq

    def body(q_ref, k_ref, do_ref, o_ref, lse_ref, dq_ref, dkv_ref):
        i = pl.program_id(0)
        n_before = (i * bq) // bk

        @pl.when(i == 0)
        def _():
            dkv_ref[...] = jnp.zeros_like(dkv_ref)

        qb = q_ref[...].reshape(rows, dk)
        dob = do_ref[...].reshape(rows, MLA_KV_RANK)
        lse_b = lse_ref[...].reshape(rows, LANES)[:, :1]
        delta = jnp.sum(dob.astype(F32) * o_ref[...].reshape(rows, MLA_KV_RANK).astype(F32), axis=1, keepdims=True)

        def step(j, width, dq, masked):
            j0 = pl.multiple_of(j * bk, bk)
            kb = k_ref[pl.ds(j0, width), :]
            sc = _dot(qb, kb, NT)
            if masked:
                sc = jnp.where(_diag_mask(rows, bq, width, i * bq, j0), sc, NEG)
            p = jnp.exp2(sc - lse_b)
            dp = _dot(dob, kb[:, :MLA_KV_RANK], NT)
            ds_bf = (p * (dp - delta)).astype(BF16)
            dkv_ref[pl.ds(j0, width), :] += _dot(ds_bf, qb, TN) * (1.0 / LOG2_E)
            dkv_ref[pl.ds(j0, width), :MLA_KV_RANK] += _dot(p.astype(BF16), dob, TN)
            return dq + _dot(ds_bf, kb, NN)

        dq_before = lax.fori_loop(0, n_before, lambda j, c: step(j, bk, c, False), jnp.zeros((rows, dk), F32))
        for part in range(bk // bq):
            @pl.when(i % (bk // bq) == part)
            def _(part=part):
                dq = step(n_before, (part + 1) * bq, dq_before, True) * ATTN_SCALE
                dq_ref[...] = dq.astype(BF16).reshape(nh, bq, dk)

    blk = lambda w: pl.BlockSpec((nh, bq, w), lambda i: (0, i, 0))
    return pl.pallas_call(
        body, name=name, grid=(s // bq,),
        in_specs=[blk(dk), pl.BlockSpec((s, dk), lambda i: (0, 0)), blk(MLA_KV_RANK), blk(MLA_KV_RANK), blk(LANES)],
        out_specs=[blk(dk), pl.BlockSpec((s, dk), lambda i: (0, 0))],
        out_shape=[jax.ShapeDtypeStruct((nh, s, dk), BF16), jax.ShapeDtypeStruct((s, dk), F32)],
        compiler_params=_params("arbitrary"),
    )(q, k, do, o, lse)


HEAD_GROUP = 4


def _o_build(o_lat, wuv_hrv, proj, *, name):
    s = proj.shape[0]
    ts = _tile(s, 1024)
    w = HEAD_GROUP * MLA_V

    def body(ol_ref, w_ref, z_ref, og_ref):
        for j in range(HEAD_GROUP):
            cs = slice(j * MLA_V, (j + 1) * MLA_V)
            z = z_ref[:, cs].astype(F32)
            og_ref[:, cs] = (_dot(ol_ref[j], w_ref[j], NN) * (z * _sigmoid(z))).astype(BF16)

    return pl.pallas_call(
        body, name=name, grid=(s // ts, MLA_HEADS // HEAD_GROUP),
        in_specs=[pl.BlockSpec((HEAD_GROUP, ts, MLA_KV_RANK), lambda i, g: (g, i, 0)),
                  pl.BlockSpec((HEAD_GROUP, MLA_KV_RANK, MLA_V), lambda i, g: (g, 0, 0)),
                  pl.BlockSpec((ts, w), lambda i, g: (i, g + 1))],
        out_specs=pl.BlockSpec((ts, w), lambda i, g: (i, g)),
        out_shape=jax.ShapeDtypeStruct((s, MLA_WIDTH), BF16),
        compiler_params=_params("parallel", "parallel"),
    )(o_lat, wuv_hrv, proj)


def _o_bwd(dg, proj, o_lat, wuv_hrv, wuv_hvr, *, name):
    s = proj.shape[0]
    ts = _tile(s, 1024)
    w = HEAD_GROUP * MLA_V

    def body(dg_ref, z_ref, ol_ref, w_ref, wt_ref, dol_ref, dz_ref, gw_ref):
        @pl.when(pl.program_id(1) == 0)
        def _():
            gw_ref[...] = jnp.zeros_like(gw_ref)

        for j in range(HEAD_GROUP):
            cs = slice(j * MLA_V, (j + 1) * MLA_V)
            z, dgj, ol = z_ref[:, cs].astype(F32), dg_ref[:, cs].astype(F32), ol_ref[j]
            sg = _sigmoid(z)
            o = _dot(ol, w_ref[j], NN)
            dz_ref[:, cs] = (dgj * o * (sg * (1.0 + z * (1.0 - sg)))).astype(BF16)
            do_bf = (dgj * (z * sg)).astype(BF16)
            dol_ref[j] = _dot(do_bf, wt_ref[j], NN).astype(BF16)
            gw_ref[j] += _dot(ol, do_bf, TN)

    hs = lambda a, b: pl.BlockSpec((HEAD_GROUP, a, b), lambda g, i: (g, 0, 0))
    return pl.pallas_call(
        body, name=name, grid=(MLA_HEADS // HEAD_GROUP, s // ts),
        in_specs=[pl.BlockSpec((ts, w), lambda g, i: (i, g)), pl.BlockSpec((ts, w), lambda g, i: (i, g + 1)),
                  pl.BlockSpec((HEAD_GROUP, ts, MLA_KV_RANK), lambda g, i: (g, i, 0)),
                  hs(MLA_KV_RANK, MLA_V), hs(MLA_V, MLA_KV_RANK)],
        out_specs=[pl.BlockSpec((HEAD_GROUP, ts, MLA_KV_RANK), lambda g, i: (g, i, 0)),
                   pl.BlockSpec((ts, w), lambda g, i: (i, g)), hs(MLA_KV_RANK, MLA_V)],
        out_shape=[jax.ShapeDtypeStruct((MLA_HEADS, s, MLA_KV_RANK), BF16), jax.ShapeDtypeStruct((s, MLA_WIDTH), BF16),
                   jax.ShapeDtypeStruct((MLA_HEADS, MLA_KV_RANK, MLA_V), F32)],
        compiler_params=_params("parallel", "arbitrary"),
    )(dg, proj, o_lat, wuv_hrv, wuv_hvr)


def _ada_mod(c_all, ada_w, ada_b_sh, *, name):
    nl, _, cols = ada_w.shape

    def body(c_ref, w_ref, b_ref, o_ref):
        c = c_ref[...]
        cond = (c * _sigmoid(c)).astype(BF16)
        for l in range(nl):
            o_ref[l] = _dot(cond, w_ref[l].astype(BF16), NN) + b_ref[l]

    return pl.pallas_call(
        body, name=name, out_shape=jax.ShapeDtypeStruct((nl, c_all.shape[0], cols), F32),
        compiler_params=_params(),
    )(c_all, ada_w, ada_b_sh)


def _ada_grad(c_all_t, dmod_sh, *, name):
    nl, _, cols = dmod_sh.shape
    d = c_all_t.shape[0]

    def body(c_ref, dm_ref, gw_ref):
        c = c_ref[...]
        cond_t = c * _sigmoid(c)
        for l in range(nl):
            gw_ref[l] = lax.dot_general(cond_t, dm_ref[l], (NN, ((), ())), precision=lax.Precision.HIGHEST,
                                        preferred_element_type=F32)

    return pl.pallas_call(
        body, name=name, out_shape=jax.ShapeDtypeStruct((nl, d, cols), F32), compiler_params=_params(),
    )(c_all_t, dmod_sh)


def _sum_devices(parts, *, name):
    def body(p_ref, o_ref):
        acc = p_ref[0]
        for k in range(1, parts.shape[0]):
            acc = acc + p_ref[k]
        o_ref[...] = acc

    return pl.pallas_call(body, name=name, out_shape=jax.ShapeDtypeStruct(parts.shape[1:], F32), compiler_params=_params())(parts)


def _adamw_math(w, g, m, v):
    c1 = 1.0 - ADAM_B1 ** ADAM_STEP
    c2 = 1.0 - ADAM_B2 ** ADAM_STEP
    nm = ADAM_B1 * m + (1.0 - ADAM_B1) * g
    nv = ADAM_B2 * v + (1.0 - ADAM_B2) * (g * g)
    return -ADAM_LR * ((nm / c1) / (jnp.sqrt(nv / c2) + ADAM_EPS) + ADAM_WD * w), nm, nv


ADAMW_BLOCK_BYTES = 1 << 20


def _adamw(w, g, m, v, *, name, after=None):
    shape = w.shape
    a, b = shape[-2], shape[-1]
    lead = 1
    for dim in shape[:-2]:
        lead *= dim
    row_bytes = 4 * b
    if a * row_bytes <= ADAMW_BLOCK_BYTES:
        ta = a
        tl = max(1, min(lead, ADAMW_BLOCK_BYTES // (a * row_bytes)))
        while lead % tl:
            tl -= 1
    else:
        tl = 1
        ta = _tile(a, 256)
    to3 = lambda t: t.reshape(lead, a, b)

    def body(w_ref, g_ref, m_ref, v_ref, *rest):
        d_ref, nm_ref, nv_ref = rest[-3:]
        d_ref[...], nm_ref[...], nv_ref[...] = _adamw_math(w_ref[...], g_ref[...], m_ref[...], v_ref[...])

    spec = pl.BlockSpec((tl, ta, b), lambda i, j: (i, j, 0))
    out = jax.ShapeDtypeStruct((lead, a, b), F32)
    order = [] if after is None else [after]
    res = pl.pallas_call(
        body, name=name, grid=(lead // tl, a // ta), in_specs=[spec] * 4 + [pl.BlockSpec(memory_space=pl.ANY)] * len(order),
        out_specs=[spec] * 3, out_shape=[out] * 3, compiler_params=_params("parallel", "parallel"),
    )(to3(w), to3(g), to3(m), to3(v), *order)
    return [r.reshape(shape) for r in res]


def _adamw_small(ws, gs, ms, vs, *, name):
    n = len(ws)

    def body(*refs):
        for k in range(n):
            w_ref, g_ref, m_ref, v_ref = (refs[j * n + k] for j in range(4))
            d_ref, nm_ref, nv_ref = (refs[(4 + j) * n + k] for j in range(3))
            d_ref[...], nm_ref[...], nv_ref[...] = _adamw_math(w_ref[...], g_ref[...], m_ref[...], v_ref[...])

    outs = [jax.ShapeDtypeStruct(w.shape, F32) for w in ws]
    res = pl.pallas_call(body, name=name, out_shape=outs * 3, compiler_params=_params())(*ws, *gs, *ms, *vs)
    return res[:n], res[n:2 * n], res[2 * n:]


def _flip(v, bit):
    return 1 - v if bit else v


CHIP_DELTAS = ((1, 0), (0, 1), (1, 1))
SUM_ROWS = 32


def _all_gather_chips(shard, *, name):
    def body(x_ref, o_ref, send_sems, recv_sems, local_sem):
        x, y, c = lax.axis_index("x"), lax.axis_index("y"), lax.axis_index("c")
        mine = pltpu.make_async_copy(x_ref, o_ref.at[2 * x + y], local_sem)
        mine.start()

        def copy(k):
            tx, ty = _flip(x, CHIP_DELTAS[k][0]), _flip(y, CHIP_DELTAS[k][1])
            send = pltpu.make_async_remote_copy(src_ref=x_ref, dst_ref=o_ref.at[2 * x + y], send_sem=send_sems.at[k],
                                                recv_sem=recv_sems.at[k], device_id=(tx, ty, c), device_id_type=MESH)
            recv = pltpu.make_async_remote_copy(src_ref=x_ref, dst_ref=o_ref.at[2 * tx + ty], send_sem=send_sems.at[k],
                                                recv_sem=recv_sems.at[k], device_id=(tx, ty, c), device_id_type=MESH)
            return send, recv

        pairs = [copy(k) for k in range(3)]
        for send, _ in pairs:
            send.start()
        for _, recv in pairs:
            recv.wait_recv()
        for send, _ in pairs:
            send.wait_send()
        mine.wait()

    return pl.pallas_call(
        body, name=name, out_shape=jax.ShapeDtypeStruct((N_CHIPS,) + shard.shape, shard.dtype),
        in_specs=[HBM], out_specs=HBM,
        scratch_shapes=[pltpu.SemaphoreType.DMA((3,)), pltpu.SemaphoreType.DMA((3,)), pltpu.SemaphoreType.DMA(())],
    )(shard)


def _gather_weights(shards, *, name):
    n = len(shards)

    def body(*refs):
        w_refs, o_refs = refs[:n], refs[n:2 * n]
        ici_send, ici_recv, d2d_send, d2d_recv, local_sems = refs[2 * n:]
        x, y, c = lax.axis_index("x"), lax.axis_index("y"), lax.axis_index("c")
        me = 2 * x + y
        peers = [(_flip(x, dx), _flip(y, dy)) for dx, dy in CHIP_DELTAS]
        locals_ = [pltpu.make_async_copy(w_refs[k], o_refs[k].at[me], local_sems.at[k]) for k in range(n)]
        for cp in locals_:
            cp.start()

        def rows(k, which):
            half = shards[k].shape[0] // 2
            return pl.ds(pl.multiple_of(which * half, half), half)

        def over_chips(k, d, slot):
            tx, ty = peers[d]
            return pltpu.make_async_remote_copy(
                src_ref=w_refs[k].at[rows(k, c)], dst_ref=o_refs[k].at[slot, rows(k, c)], send_sem=ici_send.at[k, d],
                recv_sem=ici_recv.at[k, d], device_id=(tx, ty, c), device_id_type=MESH)

        def to_sibling(k, d, which):
            tx, ty = peers[d]
            at = o_refs[k].at[2 * tx + ty, rows(k, which)]
            return pltpu.make_async_remote_copy(src_ref=at, dst_ref=at, send_sem=d2d_send.at[k, d], recv_sem=d2d_recv.at[k, d],
                                                device_id=(x, y, 1 - c), device_id_type=MESH)

        sends = [over_chips(k, d, me) for k in range(n) for d in range(3)]
        for cp in sends:
            cp.start()
        passed = []
        for k in range(n):
            for d in range(3):
                over_chips(k, d, 2 * peers[d][0] + peers[d][1]).wait_recv()
                passed.append(to_sibling(k, d, c))
                passed[-1].start()
        for k in range(n):
            for d in range(3):
                to_sibling(k, d, 1 - c).wait_recv()
        for cp in sends + passed:
            cp.wait_send()
        for cp in locals_:
            cp.wait()

    return pl.pallas_call(
        body, name=name, out_shape=[jax.ShapeDtypeStruct((N_CHIPS,) + w.shape, w.dtype) for w in shards],
        in_specs=[HBM] * n, out_specs=[HBM] * n,
        scratch_shapes=[pltpu.SemaphoreType.DMA((n, 3))] * 4 + [pltpu.SemaphoreType.DMA((n,))],
    )(*shards)


def _add_into(dst_ref, src_ref):
    ns, r, _ = dst_ref.shape
    step = SUM_ROWS if r % SUM_ROWS == 0 else r
    for s in range(ns):
        def tile(t, carry):
            at = pl.ds(pl.multiple_of(t * step, step), step)
            dst_ref[s, at, :] = (dst_ref[s, at, :].astype(F32) + src_ref[s, at, :].astype(F32)).astype(dst_ref.dtype)
            return carry
        lax.fori_loop(0, r // step, tile, 0)


def _reduce_sibling(grads, *, name):
    n = len(grads)

    def body(*refs):
        g_refs, o_refs = refs[:n], refs[n:2 * n]
        mine, got = refs[2 * n:3 * n], refs[3 * n:4 * n]
        send_sems, recv_sems, load_sems, store_sems = refs[4 * n:]
        x, y, c = lax.axis_index("x"), lax.axis_index("y"), lax.axis_index("c")
        loads = [pltpu.make_async_copy(g_refs[k].at[:, c], mine[k], load_sems.at[k]) for k in range(n)]
        swaps = [pltpu.make_async_remote_copy(src_ref=g_refs[k].at[:, 1 - c], dst_ref=got[k], send_sem=send_sems.at[k],
                                              recv_sem=recv_sems.at[k], device_id=(x, y, 1 - c), device_id_type=MESH)
                 for k in range(n)]
        for cp in loads + swaps:
            cp.start()
        stores = []
        for k in range(n):
            loads[k].wait()
            swaps[k].wait_recv()
            _add_into(mine[k], got[k])
            stores.append(pltpu.make_async_copy(mine[k], o_refs[k], store_sems.at[k]))
            stores[-1].start()
        for k in range(n):
            swaps[k].wait_send()
            stores[k].wait()

    half = [jax.ShapeDtypeStruct((g.shape[0],) + g.shape[2:], g.dtype) for g in grads]
    return pl.pallas_call(
        body, name=name, out_shape=half, in_specs=[HBM] * n, out_specs=[HBM] * n,
        scratch_shapes=[pltpu.VMEM(h.shape, h.dtype) for h in half] * 2 + [pltpu.SemaphoreType.DMA((n,))] * 4,
        compiler_params=_params(),
    )(*grads)


def _reduce_chips(parts, landed, *, name):
    n_send = len(parts)
    n = n_send + len(landed)

    def body(*refs):
        p_refs, o_refs = refs[:n], refs[n:2 * n]
        got, total = refs[2 * n:3 * n], refs[3 * n:4 * n]
        send_sems, recv_sems, load_sems, share_send, share_recv, store_sems = refs[4 * n:]
        x, y, c = lax.axis_index("x"), lax.axis_index("y"), lax.axis_index("c")
        me = 2 * x + y
        peers = [(_flip(x, dx), _flip(y, dy)) for dx, dy in CHIP_DELTAS]

        def over_chips(k, d, src_slot, dst_slot):
            tx, ty = peers[d]
            return pltpu.make_async_remote_copy(
                src_ref=p_refs[k].at[src_slot], dst_ref=got[k].at[dst_slot], send_sem=send_sems.at[k, d],
                recv_sem=recv_sems.at[k, d], device_id=(tx, ty, c), device_id_type=MESH)

        loads = [pltpu.make_async_copy(p_refs[k].at[me], got[k].at[me], load_sems.at[k]) for k in range(n_send)]
        loads += [pltpu.make_async_copy(p_refs[k], got[k], load_sems.at[k]) for k in range(n_send, n)]
        sends = [over_chips(k, d, 2 * peers[d][0] + peers[d][1], me) for k in range(n_send) for d in range(3)]
        for cp in loads + sends:
            cp.start()
        shares, stores = [], []
        for k in range(n):
            loads[k].wait()
            for d in range(3 if k < n_send else 0):
                slot = 2 * peers[d][0] + peers[d][1]
                over_chips(k, d, slot, slot).wait_recv()
            r = total[k].shape[0]
            step = SUM_ROWS if r % SUM_ROWS == 0 else r

            def tile(t, carry, k=k, step=step):
                at = pl.ds(pl.multiple_of(t * step, step), step)
                acc = got[k][0, at, :].astype(F32)
                for s in range(1, N_CHIPS):
                    acc = acc + got[k][s, at, :].astype(F32)
                total[k][at, :] = acc
                return carry

            lax.fori_loop(0, r // step, tile, 0)
            stores.append(pltpu.make_async_copy(total[k], o_refs[k].at[c], store_sems.at[k]))
            shares.append(pltpu.make_async_remote_copy(
                src_ref=total[k], dst_ref=o_refs[k].at[c], send_sem=share_send.at[k], recv_sem=share_recv.at[k],
                device_id=(x, y, 1 - c), device_id_type=MESH))
            stores[-1].start()
            shares[-1].start()
        for k in range(n):
            pltpu.make_async_remote_copy(
                src_ref=total[k], dst_ref=o_refs[k].at[1 - c], send_sem=share_send.at[k], recv_sem=share_recv.at[k],
                device_id=(x, y, 1 - c), device_id_type=MESH).wait_recv()
        for cp in sends + shares:
            cp.wait_send()
        for cp in stores:
            cp.wait()

    both = list(parts) + list(landed)
    return pl.pallas_call(
        body, name=name, out_shape=[jax.ShapeDtypeStruct((2,) + p.shape[1:], F32) for p in both],
        in_specs=[HBM] * n, out_specs=[HBM] * n,
        scratch_shapes=[pltpu.VMEM(p.shape, p.dtype) for p in both] + [pltpu.VMEM(p.shape[1:], F32) for p in both]
        + [pltpu.SemaphoreType.DMA((n, 3))] * 2 + [pltpu.SemaphoreType.DMA((n,))] * 4,
        compiler_params=_params(),
    )(*both)


SEM = pl.BlockSpec(memory_space=pltpu.SEMAPHORE)
IN_FLIGHT = pltpu.SideEffectType.DATAFLOW_SIDE_EFFECTING


def _chip_copies(s_refs, l_refs, sems, scatter, theirs):
    x, y, c = lax.axis_index("x"), lax.axis_index("y"), lax.axis_index("c")
    me = 2 * x + y
    copies = []
    for k in range(len(s_refs)):
        for d, (dx, dy) in enumerate(CHIP_DELTAS):
            tx, ty = _flip(x, dx), _flip(y, dy)
            peer = 2 * tx + ty
            send_sem, recv_sem = sems[2 * (3 * k + d)], sems[2 * (3 * k + d) + 1]
            copies.append(pltpu.make_async_remote_copy(
                src_ref=s_refs[k].at[peer] if scatter else s_refs[k], dst_ref=l_refs[k].at[peer if theirs else me],
                send_sem=send_sem, recv_sem=recv_sem, device_id=(tx, ty, c), device_id_type=MESH))
    return copies


def _chips_start(srcs, lands, after, *, scatter, name):
    n = len(srcs)
    n_sem = 2 * 3 * n

    def body(*refs):
        s_refs, l_refs = refs[:n], refs[n:2 * n]
        sems = refs[2 * n + 1:2 * n + 1 + n_sem]
        token = refs[-1]
        for cp in _chip_copies(s_refs, l_refs, sems, scatter, False):
            cp.start()
        token[...] = jnp.zeros_like(token)

    hbm = lambda a: pltpu.HBM(a.shape, a.dtype)
    res = pl.pallas_call(
        body, name=name,
        out_shape=(*[pltpu.SemaphoreType.DMA(())] * n_sem, *[hbm(a) for a in srcs], *[hbm(a) for a in lands],
                   jax.ShapeDtypeStruct((8, LANES), F32)),
        in_specs=[HBM] * (2 * n) + [pl.BlockSpec(memory_space=pl.ANY)],
        out_specs=(*[SEM] * n_sem, *[HBM] * (2 * n), VMEM),
        input_output_aliases={k: n_sem + k for k in range(2 * n)},
        compiler_params=pltpu.CompilerParams(has_side_effects=IN_FLIGHT),
    )(*[pltpu.with_memory_space_constraint(a, pltpu.HBM) for a in list(srcs) + list(lands)], after)
    return res[:n_sem], res[n_sem:n_sem + n], res[n_sem + n:n_sem + 2 * n], res[-1]


def _chips_wait(sems, srcs, lands, after, *, scatter, name):
    n = len(srcs)
    n_sem = len(sems)

    def body(*refs):
        s_refs, l_refs = refs[:n], refs[n:2 * n]
        sem_refs = refs[2 * n:2 * n + n_sem]
        for cp in _chip_copies(s_refs, l_refs, sem_refs, scatter, False):
            cp.wait_send()
        for cp in _chip_copies(s_refs, l_refs, sem_refs, scatter, True):
            cp.wait_recv()

    hbm = lambda a: pltpu.HBM(a.shape, a.dtype)
    res = pl.pallas_call(
        body, name=name, out_shape=tuple(hbm(a) for a in list(srcs) + list(lands)),
        in_specs=[HBM] * (2 * n) + [SEM] * n_sem + [pl.BlockSpec(memory_space=pl.ANY)], out_specs=tuple([HBM] * (2 * n)),
        input_output_aliases={k: k for k in range(2 * n)},
        compiler_params=pltpu.CompilerParams(has_side_effects=IN_FLIGHT),
    )(*srcs, *lands, *sems, after)
    return res[n:]


def _all_gather_devices(rows, *, name):
    deltas = [(dx, dy, dc) for dx in (0, 1) for dy in (0, 1) for dc in (0, 1)][1:]

    def body(x_ref, o_ref, send_sems, recv_sems):
        x, y, c = lax.axis_index("x"), lax.axis_index("y"), lax.axis_index("c")
        me = 4 * x + 2 * y + c
        o_ref[me] = x_ref[...]
        sends, recvs = [], []
        for k, (dx, dy, dc) in enumerate(deltas):
            tx, ty, tc = _flip(x, dx), _flip(y, dy), _flip(c, dc)
            sends.append(pltpu.make_async_remote_copy(src_ref=x_ref, dst_ref=o_ref.at[me], send_sem=send_sems.at[k],
                                                      recv_sem=recv_sems.at[k], device_id=(tx, ty, tc), device_id_type=MESH))
            recvs.append(pltpu.make_async_remote_copy(src_ref=x_ref, dst_ref=o_ref.at[4 * tx + 2 * ty + tc],
                                                      send_sem=send_sems.at[k], recv_sem=recv_sems.at[k],
                                                      device_id=(tx, ty, tc), device_id_type=MESH))
        for cp in sends:
            cp.start()
        for cp in recvs:
            cp.wait_recv()
        for cp in sends:
            cp.wait_send()

    return pl.pallas_call(
        body, name=name, out_shape=jax.ShapeDtypeStruct((N_DEV,) + rows.shape, rows.dtype),
        in_specs=[VMEM], out_specs=VMEM,
        scratch_shapes=[pltpu.SemaphoreType.DMA((N_DEV - 1,)), pltpu.SemaphoreType.DMA((N_DEV - 1,))],
    )(rows)


WEIGHTS = ("ada_w", "ada_b", "ln_g", "ln_b", "e_w_in", "gmlp_norm_g", "gmlp_norm_b", "gmlp_ws", "gmlp_bs", "pool_w",
           "pool_b", "pool_scale", "e_w_out", "o_w_in", "mla_q_norm_g", "mla_kv_norm_g", "mla_w_uq", "mla_w_uk",
           "mla_w_uv", "o_w_out")
SMALL = ("ln_g", "ln_b", "gmlp_norm_g", "gmlp_norm_b", "gmlp_bs", "pool_b", "pool_scale", "mla_kv_norm_g", "mla_q_norm_g")


def _pad_cols(v, n):
    return jnp.concatenate([v, jnp.zeros((v.shape[0], n - v.shape[1]), v.dtype)], axis=1) if n > v.shape[1] else v


def _halves(g):
    return g.reshape(g.shape[0], 2, g.shape[1] // 2, g.shape[2])


def kernel(x, c, positions, ada_w, ada_b, ln_g, ln_b, e_w_in, gmlp_norm_g, gmlp_norm_b, gmlp_ws, gmlp_bs, pool_w, pool_b, pool_scale, e_w_out, o_w_in, mla_q_norm_g, mla_kv_norm_g, mla_w_uq, mla_w_uk, mla_w_uv, o_w_out, loss_target, m_ada_w, m_ada_b, m_ln_g, m_ln_b, m_e_w_in, m_gmlp_norm_g, m_gmlp_norm_b, m_gmlp_ws, m_gmlp_bs, m_pool_w, m_pool_b, m_pool_scale, m_e_w_out, m_o_w_in, m_mla_q_norm_g, m_mla_kv_norm_g, m_mla_w_uq, m_mla_w_uk, m_mla_w_uv, m_o_w_out, v_ada_w, v_ada_b, v_ln_g, v_ln_b, v_e_w_in, v_gmlp_norm_g, v_gmlp_norm_b, v_gmlp_ws, v_gmlp_bs, v_pool_w, v_pool_b, v_pool_scale, v_e_w_out, v_o_w_in, v_mla_q_norm_g, v_mla_kv_norm_g, v_mla_w_uq, v_mla_w_uk, v_mla_w_uv, v_o_w_out):
    args = dict(locals())
    weights = {n: args[n] for n in WEIGHTS}
    mom = {n: args["m_" + n] for n in WEIGHTS}
    var = {n: args["v_" + n] for n in WEIGHTS}
    ax, ay, ac = lax.axis_index("x"), lax.axis_index("y"), lax.axis_index("c")
    chip = 2 * ax + ay
    dev = 2 * chip + ac
    d = D_MODEL
    x2 = x[0]
    target = loss_target[0]
    q_rank_sh = mla_q_norm_g.shape[1]

    empty_zone = lambda w: lax.dynamic_update_slice(lax.empty((N_CHIPS,) + w.shape, w.dtype), w[None], (chip, 0, 0))
    shards0 = [w.astype(BF16) for w in (pool_w[0].reshape(-1, POOL_GROUP_DIM), e_w_out[0])]
    shards1 = [w.astype(BF16) for w in (o_w_in[0], mla_w_uq[0].reshape(q_rank_sh, -1), o_w_out[0])]
    w_in0, = _gather_weights([e_w_in[0].astype(BF16)], name="gather_weights")
    flight0 = _chips_start(shards0, [empty_zone(w) for w in shards0], w_in0, scatter=False, name="gather0_start")
    flight1 = _chips_start(shards1, [empty_zone(w) for w in shards1], flight0[3], scatter=False, name="gather1_start")
    wuk_hrd = jnp.transpose(mla_w_uk[0], (1, 0, 2)).astype(BF16)
    wuk_hdr = jnp.transpose(mla_w_uk[0], (1, 2, 0)).astype(BF16)
    wuv_hrv = jnp.transpose(mla_w_uv[0], (1, 0, 2)).astype(BF16)
    wuv_hvr = jnp.transpose(mla_w_uv[0], (1, 2, 0)).astype(BF16)
    ws = gmlp_ws[0]
    ws_t = jnp.transpose(ws, (0, 2, 1))
    bs_t = _pad_cols(gmlp_bs[0].T, LANES)

    inv = 1.0 / (ROPE_THETA ** (jnp.arange(0, MLA_ROPE, 2, dtype=F32) / MLA_ROPE))
    ang = positions[0].astype(F32)[:, None] * inv
    cos_t = jnp.tile(jnp.cos(ang), (1, 4))
    sin_t = jnp.tile(jnp.concatenate([-jnp.sin(ang), jnp.sin(ang)], axis=1), (1, 2))

    c_all = _all_gather_devices(c.reshape(8, LANES), name="gather_c").reshape(N_DEV, d)
    cols = ada_w.shape[2]
    ada_b_mine = lax.dynamic_slice_in_dim(ada_b, chip * cols, cols, axis=1)[:, None, :]
    mod_sh = _ada_mod(c_all, ada_w, ada_b_mine, name="ada_mod")
    q_norm_rows = jnp.zeros((8, cols), F32).at[0, :q_rank_sh].set(mla_q_norm_g[0])
    mod_all = _all_gather_chips(jnp.concatenate([mod_sh.reshape(2 * N_DEV, cols), q_norm_rows]), name="gather_mod")
    q_norm_g = mod_all[:, 2 * N_DEV, :q_rank_sh].reshape(1, -1)
    mod_all = jnp.transpose(mod_all[:, :2 * N_DEV].reshape(N_CHIPS, 2, N_DEV, cols), (1, 2, 0, 3)).reshape(2, N_DEV, 3 * d)
    mod = lax.dynamic_index_in_dim(mod_all, dev, axis=1, keepdims=False)
    shift = [mod[l:l + 1, :d] for l in range(2)]
    scale = [mod[l:l + 1, d:2 * d] for l in range(2)]
    gate = [mod[l:l + 1, 2 * d:] for l in range(2)]

    scale[0] = scale[0] + flight1[3][:1, :1]
    h0 = _modulate(x2, scale[0], shift[0], name="modulate0")
    proj0 = _matmul(h0, w_in0, b_stacked=True, tm=1024, tn=1280, out_dtype=BF16, name="proj0")
    pool_w_g, w_out0 = _chips_wait(*flight0[:3], proj0, scatter=False, name="gather0_wait")
    pool_w_bf = jnp.transpose(pool_w_g.reshape(N_CHIPS, POOL_GROUPS, -1, POOL_GROUP_DIM), (1, 0, 2, 3)).reshape(
        POOL_GROUPS, POOL_GROUP_DIM, POOL_GROUP_DIM)
    w_out0 = w_out0.reshape(-1, d)
    mix0 = _even_fwd(proj0, ws, bs_t, gmlp_norm_g, gmlp_norm_b, pool_w_bf, pool_b, pool_scale, name="even_fwd")
    y0, x1, h1 = _out_resid_ln(mix0, w_out0, x2, gate[0], ln_g[0:1], ln_b[0:1], scale[1], shift[1], name="out0_ln")

    w_in1_g, w_uq_g, w_out1 = _chips_wait(*flight1[:3], h1, scatter=False, name="gather1_wait")
    w_out1 = w_out1.reshape(-1, d)
    w_in1 = jnp.transpose(w_in1_g, (1, 0, 2)).reshape(d, ODD_IN)
    w_in1 = jnp.concatenate([_pad_cols(w_in1[:, :ODD_SMALL], ODD_SMALL_PAD), w_in1[:, ODD_SMALL:]], axis=1)
    w_uq = w_uq_g.reshape(MLA_Q_RANK, MLA_HEADS, MLA_NOPE + MLA_ROPE)
    w_uq_nope = w_uq[:, :, :MLA_NOPE].reshape(MLA_Q_RANK, -1)
    w_uq_rope = w_uq[:, :, MLA_NOPE:].reshape(MLA_Q_RANK, -1)
    proj1 = _matmul(h1, w_in1, tm=1024, tn=1280, out_dtype=BF16, name="proj1")
    q_cn, keys = _mla_prep(proj1, q_norm_g, mla_kv_norm_g, cos_t, sin_t, name="mla_prep")
    q_nope = _matmul(q_cn, w_uq_nope, tm=1024, tn=2048, name="q_nope", out_dtype=BF16)
    q_rope_pre = _matmul(q_cn, w_uq_rope, tm=1024, name="q_rope")
    q = _q_build(q_nope, q_rope_pre, wuk_hdr, cos_t, sin_t, name="q_build")
    o_lat, lse = _attn_fwd(q, keys, name="attn_fwd")
    og = _o_build(o_lat, wuv_hrv, proj1, name="o_build")

    dy1, dres1, g_ln_g1, g_ln_b1, dgate1, loss = _out_loss_ln_bwd(
        og, w_out1, x1, gate[1], ln_g[1:2], ln_b[1:2], target, name="out1_loss_ln")
    dg1 = _matmul(dy1, w_out1, trans_b=True, tn=2048, out_dtype=BF16, name="d_og")
    g_w_out1 = _matmul(og, dy1, trans_a=True, out_dtype=BF16, tm=1024, name="g_out1")
    do_lat, dz, g_uv = _o_bwd(dg1, proj1, o_lat, wuv_hrv, wuv_hvr, name="o_bwd")
    dq, dkeys = _attn_bwd(q, keys, do_lat, o_lat, lse, name="attn_bwd")
    dq_nope, dq_rope, g_uk = _q_bwd(dq, q_nope, wuk_hrd, cos_t, sin_t, name="q_bwd")
    dq_cn = (_matmul(dq_nope, w_uq_nope, trans_b=True, tm=1024, name="d_qcn_nope")
             + _matmul(dq_rope, w_uq_rope, trans_b=True, tm=1024, name="d_qcn_rope"))
    g_uq_nope = _matmul(q_cn, dq_nope, trans_a=True, out_dtype=BF16, tn=2048, name="g_uq_nope")
    g_uq_rope = _matmul(q_cn, dq_rope, trans_a=True, out_dtype=BF16, name="g_uq_rope")
    dsmall, g_qg, g_kvg = _mla_prep_bwd(proj1, dq_cn, dkeys, q_norm_g, mla_kv_norm_g, cos_t, sin_t, name="mla_prep_bwd")
    dproj1 = jnp.concatenate([dsmall, dz], axis=1)
    g_w_in1 =_matmul(h1, dproj1, trans_a=True, out_dtype=BF16, tm=1024, tn=1280, name="g_in1")

    g_uq = jnp.concatenate([g_uq_nope.reshape(MLA_Q_RANK, MLA_HEADS, MLA_NOPE), g_uq_rope.reshape(MLA_Q_RANK, MLA_HEADS, MLA_ROPE)], axis=2)
    g_w_in1 = jnp.concatenate([g_w_in1[:, :ODD_SMALL], g_w_in1[:, ODD_SMALL_PAD:]], axis=1)
    g_w_in1 = jnp.transpose(g_w_in1.reshape(d, N_CHIPS, -1), (1, 0, 2))
    big1 = [
        _halves(g_w_in1),
        _halves(g_uq.reshape(N_CHIPS, q_rank_sh, -1)),
        _halves(g_w_out1.reshape(N_CHIPS, -1, d)),
        _halves(g_uk.astype(BF16).reshape(N_CHIPS, -1, MLA_NOPE)),
        _halves(g_uv.astype(BF16).reshape(N_CHIPS, -1, MLA_V)),
    ]
    parts1 = _reduce_sibling(big1, name="reduce_sibling1")
    lands2 = [lax.dynamic_update_slice(lax.empty(p.shape, BF16), lax.dynamic_slice_in_dim(p, chip, 1, axis=0), (chip, 0, 0))
              for p in parts1]
    flight2 = _chips_start(parts1, lands2, loss, scatter=True, name="reduce1_start")

    gate[0] = gate[0] + flight2[3][:1, :1]
    dy0, dres0, g_ln_g0, g_ln_b0, dgate0, dscale1, dshift1 = _dh_mid_ln_bwd(
        dproj1, w_in1, x2, y0, gate[0], ln_g[0:1], ln_b[0:1], dres1, scale[1], x1, name="d_h1_mid_ln")
    dmix0 = _matmul(dy0, w_out0, trans_b=True, tn=2048, out_dtype=BF16, name="d_mix0")
    g_w_out0 = _matmul(mix0, dy0, trans_a=True, out_dtype=BF16, tm=1024, name="g_out0")
    dproj0, g_ws, g_bs_t, g_ng, g_nb, g_pw, g_pb, g_ps = _even_bwd(
        proj0, dmix0, ws, ws_t, bs_t, gmlp_norm_g, gmlp_norm_b, pool_w_bf, pool_b, pool_scale, name="even_bwd")
    g_w_in0 = _matmul(h0, dproj0, trans_a=True, out_dtype=BF16, out_stacked=True, tm=1024, tn=1280, name="g_in0")

    g_pw = jnp.transpose(g_pw.astype(BF16).reshape(POOL_GROUPS, N_CHIPS, -1, POOL_GROUP_DIM), (1, 0, 2, 3))
    big0 = [
        _halves(g_w_in0),
        _halves(g_pw.reshape(N_CHIPS, -1, POOL_GROUP_DIM)),
        _halves(g_w_out0.reshape(N_CHIPS, -1, d)),
        _halves(g_ws.astype(BF16)),
    ]
    parts0 = _reduce_sibling(big0, name="reduce_sibling0")
    landed1 = _chips_wait(*flight2[:3], parts0[0], scatter=True, name="reduce1_wait")
    lands3 = [lax.dynamic_update_slice(lax.empty(p.shape, BF16), lax.dynamic_slice_in_dim(p, chip, 1, axis=0), (chip, 0, 0))
              for p in parts0]
    flight3 = _chips_start(parts0, lands3, landed1[0], scatter=True, name="reduce0_start")
    grad_x, dscale0, dshift0 = _dh_input_bwd(dproj0, w_in0, x2, dres0, scale[0], after=flight3[3], name="d_h0_input")

    small_local = {
        "ln_g": jnp.concatenate([g_ln_g0, g_ln_g1]), "ln_b": jnp.concatenate([g_ln_b0, g_ln_b1]),
        "gmlp_norm_g": g_ng, "gmlp_norm_b": g_nb, "gmlp_bs": g_bs_t[:, :GMLP_HEADS].T, "pool_b": g_pb, "pool_scale": g_ps,
        "mla_kv_norm_g": g_kvg, "mla_q_norm_g": g_qg,
    }
    n_mod = 2 * 3 * d
    vec = jnp.concatenate([dshift0, dscale0, dgate0, dshift1, dscale1, dgate1]
                          + [small_local[n].reshape(1, -1) for n in SMALL] + [loss], axis=1)
    n_vec = vec.shape[1]
    vec = _pad_cols(vec, -(-n_vec // (8 * LANES)) * 8 * LANES).reshape(-1, LANES)
    vec_all = _all_gather_devices(vec, name="gather_small")
    vec_sum = _sum_devices(vec_all, name="sum_small").reshape(-1)
    dmod_all = vec_all.reshape(N_DEV, -1)[:, :n_mod].reshape(N_DEV, 2, 3 * d)
    dmod_sh = jnp.transpose(lax.dynamic_slice_in_dim(dmod_all, chip * cols, cols, axis=2), (1, 0, 2))
    dmod_sh = jnp.concatenate([dmod_sh, jnp.zeros((2, LANES - N_DEV, cols), F32)], axis=1)
    grads = {"ada_w": _ada_grad(_pad_cols(c_all.T, LANES), dmod_sh, name="ada_grad"), "ada_b": vec_sum[:n_mod].reshape(2, 3 * d)}
    off = n_mod
    for n in SMALL:
        sz = small_local[n].size
        grads[n] = vec_sum[off:off + sz]
        off += sz
    grads["mla_q_norm_g"] = lax.dynamic_slice_in_dim(grads["mla_q_norm_g"], chip * q_rank_sh, q_rank_sh)
    for n in SMALL:
        grads[n] = grads[n].reshape(weights[n].shape)

    landed0 = _chips_wait(*flight3[:3], grads["ada_w"], scatter=True, name="reduce0_wait")
    totals = _reduce_chips([], list(landed0) + list(landed1), name="reduce_chips")
    for n, t in zip(("e_w_in", "pool_w", "e_w_out", "gmlp_ws", "o_w_in", "mla_w_uq", "o_w_out"), totals):
        if n != "gmlp_ws":
            grads[n] = t.reshape(weights[n].shape)
    rep = jnp.concatenate([t.reshape(-1, LANES) for t in (totals[3], totals[7], totals[8])])
    rep_land = lax.dynamic_update_slice(lax.empty((N_CHIPS,) + rep.shape, F32), rep[None], (chip, 0, 0))
    flight4 = _chips_start([rep], [rep_land], totals[0], scatter=False, name="gather_rep_start")

    delta, new_m, new_v = {}, {}, {}
    replicated = ("gmlp_ws", "mla_w_uk", "mla_w_uv")
    large = [n for n in WEIGHTS if n not in SMALL and n != "ada_b"]
    for n in large:
        if n not in replicated:
            delta[n], new_m[n], new_v[n] = _adamw(weights[n], grads[n], mom[n], var[n], after=flight4[3], name="adamw_" + n)
    rep = _chips_wait(*flight4[:3], delta["e_w_in"], scatter=False, name="gather_rep_wait")[0]
    r_ws, r_uk = GMLP_BLOCK, 4 * MLA_KV_RANK
    grads["gmlp_ws"] = rep[:, :r_ws].reshape(weights["gmlp_ws"].shape)
    grads["mla_w_uk"] = jnp.transpose(rep[:, r_ws:r_ws + r_uk].reshape(MLA_HEADS, MLA_KV_RANK, MLA_NOPE), (1, 0, 2))[None]
    grads["mla_w_uv"] = jnp.transpose(rep[:, r_ws + r_uk:].reshape(MLA_HEADS, MLA_KV_RANK, MLA_V), (1, 0, 2))[None]
    for n in replicated:
        delta[n], new_m[n], new_v[n] = _adamw(weights[n], grads[n], mom[n], var[n], name="adamw_" + n)
    small = [n for n in WEIGHTS if n not in large]
    ds, ms, vs = _adamw_small([weights[n] for n in small], [grads[n] for n in small], [mom[n] for n in small],
                              [var[n] for n in small], name="adamw_small")
    for n, dn, mn, vn in zip(small, ds, ms, vs):
        delta[n], new_m[n], new_v[n] = dn, mn, vn

    return (vec_sum[n_vec - 1], grad_x[None], *[grads[n] for n in WEIGHTS], *[delta[n] for n in WEIGHTS],
            *[new_m[n] for n in WEIGHTS], *[new_v[n] for n in WEIGHTS])
```

```python
import jax
import jax.numpy as jnp
from jax import lax
from jax.experimental import pallas as pl
from jax.experimental.pallas import tpu as pltpu

F32 = jnp.float32
BF16 = jnp.bfloat16
MESH = pl.DeviceIdType.MESH

D_MODEL = 1024
CHUNK = 64
LN_EPS = 1e-5
GMLP_HEADS = 4
GMLP_HEAD_DIM = 256
GMLP_BLOCK = 128
POOL_WINDOWS = (2, 4, 8, 16)
POOL_GROUPS = 4
POOL_GROUP_DIM = 256
POOL_HALO = 16
EVEN_IN = 5120
MLA_HEADS = 16
MLA_NOPE = 128
MLA_ROPE = 64
MLA_V = 128
MLA_Q_RANK = 256
MLA_KV_RANK = 128
MLA_WIDTH = MLA_HEADS * MLA_V
ODD_IN = 2496
ODD_SMALL = MLA_Q_RANK + MLA_KV_RANK + MLA_ROPE
ODD_SMALL_PAD = 512
QK_PAD = 256
ROPE_THETA = 10000.0
ATTN_SCALE = (MLA_NOPE + MLA_ROPE) ** -0.5
DEEPNORM_ALPHA = (2.0 * 2) ** 0.25
ADAM_LR = 0.001
ADAM_B1 = 0.9
ADAM_B2 = 0.999
ADAM_EPS = 1e-08
ADAM_WD = 0.01
ADAM_STEP = 10
NEG = -1e30
LANES = 128
N_DEV = 8
N_CHIPS = 4
VMEM_LIMIT_BYTES = 56 * 1024 * 1024
HBM = pl.BlockSpec(memory_space=pltpu.HBM)
VMEM = pl.BlockSpec(memory_space=pltpu.VMEM)


def _params(*sem):
    return pltpu.CompilerParams(dimension_semantics=sem if sem else None, vmem_limit_bytes=VMEM_LIMIT_BYTES)


def _tile(dim, pref):
    for t in (pref, 2048, 1280, 1024, 512, 256, 128):
        if t <= min(pref, dim) and dim % t == 0:
            return t
    return dim


def _sigmoid(z):
    return 1.0 / (1.0 + jnp.exp(-z))


def _dot(a, b, dims):
    return lax.dot_general(a, b, (dims, ((), ())), preferred_element_type=F32)


NN = ((1,), (0,))
NT = ((1,), (1,))
TN = ((0,), (0,))


def _matmul(a, b, *, name, trans_a=False, trans_b=False, out_dtype=F32, b_stacked=False, out_stacked=False,
            tm=512, tn=1024, tk=2048, after=None):
    k, m = a.shape if trans_a else a.shape[::-1]
    if b_stacked:
        ns, kb, n_sh = b.shape
        kb, n = (ns * n_sh, kb) if trans_b else (kb, ns * n_sh)
    else:
        n, kb = b.shape if trans_b else b.shape[::-1]
    assert k == kb, (a.shape, b.shape)
    tm = _tile(m, tm)
    if b_stacked and trans_b:
        tn, tk = _tile(n, tn), n_sh
    elif b_stacked or out_stacked:
        tn, tk = _tile(n // N_CHIPS, tn), _tile(k, tk)
    else:
        tn, tk = _tile(n, tn), _tile(k, tk)
    nk = k // tk
    per = max((n // N_CHIPS) // tn, 1)
    dims = ((0 if trans_a else 1,), (1 if trans_b else 0,))

    def body_one(a_ref, b_ref, *rest):
        o_ref = rest[-1]
        o_ref[...] = _dot(a_ref[...].astype(BF16), b_ref[...].astype(BF16), dims).astype(out_dtype)

    def body_acc(a_ref, b_ref, *rest):
        o_ref, acc_ref = rest[-2:]
        kk = pl.program_id(2)

        @pl.when(kk == 0)
        def _():
            acc_ref[...] = jnp.zeros_like(acc_ref)

        acc_ref[...] += _dot(a_ref[...].astype(BF16), b_ref[...].astype(BF16), dims)

        @pl.when(kk == nk - 1)
        def _():
            o_ref[...] = acc_ref[...].astype(out_dtype)

    a_spec = pl.BlockSpec((tk, tm), lambda i, j, kk: (kk, i)) if trans_a else pl.BlockSpec((tm, tk), lambda i, j, kk: (i, kk))
    if b_stacked and trans_b:
        b_spec = pl.BlockSpec((None, tn, tk), lambda i, j, kk: (kk, j, 0))
    elif b_stacked:
        b_spec = pl.BlockSpec((None, tk, tn), lambda i, j, kk: (j // per, kk, j % per))
    elif trans_b:
        b_spec = pl.BlockSpec((tn, tk), lambda i, j, kk: (j, kk))
    else:
        b_spec = pl.BlockSpec((tk, tn), lambda i, j, kk: (kk, j))
    if out_stacked:
        o_spec = pl.BlockSpec((None, tm, tn), lambda i, j, kk: (j // per, i, j % per))
        o_shape = jax.ShapeDtypeStruct((N_CHIPS, m, n // N_CHIPS), out_dtype)
    else:
        o_spec = pl.BlockSpec((tm, tn), lambda i, j, kk: (i, j))
        o_shape = jax.ShapeDtypeStruct((m, n), out_dtype)
    order = [] if after is None else [after]
    return pl.pallas_call(
        body_one if nk == 1 else body_acc, name=name, grid=(m // tm, n // tn, nk),
        in_specs=[a_spec, b_spec] + [pl.BlockSpec(memory_space=pl.ANY)] * len(order),
        out_specs=o_spec, out_shape=o_shape, scratch_shapes=[] if nk == 1 else [pltpu.VMEM((tm, tn), F32)],
        compiler_params=_params("parallel", "parallel", "arbitrary"),
    )(a, b, *order)


def _matmul_rows(a, b, epilogue, row_ins, vec_ins, row_outs, vec_outs, *, name, trans_b=False, b_stacked=False,
                 tm=512, tk=2048, after=None):
    m, k = a.shape
    if b_stacked:
        ns, n, n_sh = b.shape
        assert trans_b and ns * n_sh == k
        tk = n_sh
    else:
        n = b.shape[0] if trans_b else b.shape[1]
        tk = _tile(k, tk)
    tm = _tile(m, tm)
    nk = k // tk
    dims = ((1,), (1 if trans_b else 0,))
    n_ri, n_vi, n_ro, n_vo = len(row_ins), len(vec_ins), len(row_outs), len(vec_outs)
    order = [] if after is None else [after]

    def body(*refs):
        a_ref, b_ref = refs[:2]
        pos = 2
        rin = refs[pos:pos + n_ri]
        pos += n_ri
        vin = refs[pos:pos + n_vi]
        pos += n_vi + len(order)
        rout = refs[pos:pos + n_ro]
        pos += n_ro
        vout = refs[pos:pos + n_vo]
        first = pl.program_id(0) == 0
        part = _dot(a_ref[...].astype(BF16), b_ref[...].astype(BF16), dims)
        if nk == 1:
            epilogue(part, first, rin, vin, rout, vout)
        else:
            acc_ref = refs[-1]
            kk = pl.program_id(1)

            @pl.when(kk == 0)
            def _():
                acc_ref[...] = part

            @pl.when(kk > 0)
            def _():
                acc_ref[...] += part

            @pl.when(kk == nk - 1)
            def _():
                epilogue(acc_ref[...], first, rin, vin, rout, vout)

    a_spec = pl.BlockSpec((tm, tk), lambda i, kk: (i, kk))
    if b_stacked:
        b_spec = pl.BlockSpec((None, n, tk), lambda i, kk: (kk, 0, 0))
    elif trans_b:
        b_spec = pl.BlockSpec((n, tk), lambda i, kk: (0, kk))
    else:
        b_spec = pl.BlockSpec((tk, n), lambda i, kk: (kk, 0))
    row = pl.BlockSpec((tm, n), lambda i, kk: (i, 0))
    vec = lambda w: pl.BlockSpec((1, w), lambda i, kk: (0, 0))
    return pl.pallas_call(
        body, name=name, grid=(m // tm, nk),
        in_specs=[a_spec, b_spec] + [row] * n_ri + [vec(v.shape[1]) for v in vec_ins] + [pl.BlockSpec(memory_space=pl.ANY)] * len(order),
        out_specs=[row] * n_ro + [vec(w) for w in vec_outs],
        out_shape=[jax.ShapeDtypeStruct((m, n), dt) for dt in row_outs] + [jax.ShapeDtypeStruct((1, w), F32) for w in vec_outs],
        scratch_shapes=[] if nk == 1 else [pltpu.VMEM((tm, n), F32)],
        compiler_params=_params("arbitrary", "arbitrary"),
    )(a, b, *row_ins, *vec_ins, *order)


def _row_spec(ts, d):
    return pl.BlockSpec((ts, d), lambda i: (i, 0))


def _vec_spec(d):
    return pl.BlockSpec((1, d), lambda i: (0, 0))


def _modulate(x, scale, shift, *, name):
    s, d = x.shape
    ts = _tile(s, 512)

    def body(x_ref, sc_ref, sh_ref, h_ref):
        h_ref[...] = (x_ref[...] * (1.0 + sc_ref[...]) + sh_ref[...]).astype(BF16)

    return pl.pallas_call(
        body, name=name, grid=(s // ts,), in_specs=[_row_spec(ts, d), _vec_spec(d), _vec_spec(d)],
        out_specs=_row_spec(ts, d), out_shape=jax.ShapeDtypeStruct((s, d), BF16), compiler_params=_params("parallel"),
    )(x, scale, shift)


def _ln_stats(pre):
    mu = jnp.mean(pre, axis=-1, keepdims=True)
    xc = pre - mu
    var = jnp.mean(xc * xc, axis=-1, keepdims=True)
    rstd = lax.rsqrt(var + LN_EPS)
    return xc * rstd, rstd


def _ln_bwd_rows(dout, xhat, rstd, g):
    dxh = dout * g
    m1 = jnp.mean(dxh, axis=-1, keepdims=True)
    m2 = jnp.mean(dxh * xhat, axis=-1, keepdims=True)
    return rstd * (dxh - m1 - xhat * m2)


def _colsum(v):
    return jnp.sum(v, axis=0, keepdims=True)


def _out_resid_ln(mix, w_out, x, gate, g, b, scale_next, shift_next, *, name):
    def epilogue(y, first, rin, vin, rout, vout):
        (x_ref,), (gate_ref, g_ref, b_ref, sc_ref, sh_ref), (y_ref, xn_ref, h_ref) = rin, vin, rout
        y_ref[...] = y
        pre = DEEPNORM_ALPHA * x_ref[...] + (1.0 + gate_ref[...]) * y
        xhat, _ = _ln_stats(pre)
        xn = xhat * g_ref[...] + b_ref[...]
        xn_ref[...] = xn
        h_ref[...] = (xn * (1.0 + sc_ref[...]) + sh_ref[...]).astype(BF16)

    return _matmul_rows(mix, w_out, epilogue, [x], [gate, g, b, scale_next, shift_next], [F32, F32, BF16], [], name=name)


def _out_loss_ln_bwd(og, w_out, x, gate, g, b, target, *, name):
    d = x.shape[1]

    def epilogue(yv, first, rin, vin, rout, vout):
        (x_ref, t_ref), (gate_ref, g_ref, b_ref), (dy_ref, dres_ref), (dg_ref, db_ref, dgate_ref, loss_ref) = rin, vin, rout, vout

        @pl.when(first)
        def _():
            for r in vout:
                r[...] = jnp.zeros_like(r)

        pre = DEEPNORM_ALPHA * x_ref[...] + (1.0 + gate_ref[...]) * yv
        xhat, rstd = _ln_stats(pre)
        diff = xhat * g_ref[...] + b_ref[...] - t_ref[...]
        loss_ref[...] += (0.5 / d) * jnp.sum(jnp.sum(diff * diff, axis=1, keepdims=True), axis=0, keepdims=True)
        dout = diff * (1.0 / d)
        dpre = _ln_bwd_rows(dout, xhat, rstd, g_ref[...])
        dy_ref[...] = (dpre * (1.0 + gate_ref[...])).astype(BF16)
        dres_ref[...] = DEEPNORM_ALPHA * dpre
        dg_ref[...] += _colsum(dout * xhat)
        db_ref[...] += _colsum(dout)
        dgate_ref[...] += _colsum(dpre * yv)

    return _matmul_rows(og, w_out, epilogue, [x, target], [gate, g, b], [BF16, F32], [d, d, d, 1], name=name)


def _dh_mid_ln_bwd(dproj, w_in, x, y, gate, g, b, dres_next, scale_next, x_next, *, name):
    d = x.shape[1]

    def epilogue(dh, first, rin, vin, rout, vout):
        (x_ref, y_ref, dr_ref, xn_ref), (gate_ref, g_ref, b_ref, sc_ref), (dy_ref, dres_ref) = rin, vin, rout
        dg_ref, db_ref, dgate_ref, dscale_ref, dshift_ref = vout

        @pl.when(first)
        def _():
            for r in vout:
                r[...] = jnp.zeros_like(r)

        dout = dr_ref[...] + dh * (1.0 + sc_ref[...])
        dscale_ref[...] += _colsum(dh * xn_ref[...])
        dshift_ref[...] += _colsum(dh)
        yv = y_ref[...]
        pre = DEEPNORM_ALPHA * x_ref[...] + (1.0 + gate_ref[...]) * yv
        xhat, rstd = _ln_stats(pre)
        dpre = _ln_bwd_rows(dout, xhat, rstd, g_ref[...])
        dy_ref[...] = (dpre * (1.0 + gate_ref[...])).astype(BF16)
        dres_ref[...] = DEEPNORM_ALPHA * dpre
        dg_ref[...] += _colsum(dout * xhat)
        db_ref[...] += _colsum(dout)
        dgate_ref[...] += _colsum(dpre * yv)

    return _matmul_rows(dproj, w_in, epilogue, [x, y, dres_next, x_next], [gate, g, b, scale_next], [BF16, F32], [d] * 5,
                        trans_b=True, tk=1280, name=name)


def _dh_input_bwd(dproj, w_in_stacked, x, dres, scale, *, name, after):
    d = x.shape[1]

    def epilogue(dh, first, rin, vin, rout, vout):
        (x_ref, dr_ref), (sc_ref,), (dx_ref,), (dscale_ref, dshift_ref) = rin, vin, rout, vout

        @pl.when(first)
        def _():
            for r in vout:
                r[...] = jnp.zeros_like(r)

        dx_ref[...] = dr_ref[...] + dh * (1.0 + sc_ref[...])
        dscale_ref[...] += _colsum(dh * x_ref[...])
        dshift_ref[...] += _colsum(dh)

    return _matmul_rows(dproj, w_in_stacked, epilogue, [x, dres], [scale], [F32], [d, d], trans_b=True, b_stacked=True,
                        tm=1024, after=after, name=name)


def _chunk_mask(transposed=False):
    r = lax.broadcasted_iota(jnp.int32, (GMLP_BLOCK, GMLP_BLOCK), 0) // CHUNK
    c = lax.broadcasted_iota(jnp.int32, (GMLP_BLOCK, GMLP_BLOCK), 1) // CHUNK
    return (r <= c) if transposed else (c <= r)


def _window_sum(ext, steps, forward):
    rows = ext.shape[0]
    acc = ext
    for k in range(steps):
        shift = 1 << k
        acc = acc + pltpu.roll(acc, (rows - shift) if forward else shift, 0)
    return acc


def _pool_counts(first_row, rows, win):
    t = first_row + lax.broadcasted_iota(jnp.int32, (rows, 1), 0)
    return jnp.minimum(t + 1, win).astype(F32)


def _even_specs(t):
    col = lambda j: pl.BlockSpec((t, D_MODEL), lambda n: (n, j))
    per = t // POOL_HALO
    prev = pl.BlockSpec((POOL_HALO, D_MODEL), lambda n: (jnp.maximum(n * per - 1, 0), 3))
    return col, per, prev


def _full(shape):
    return pl.BlockSpec(shape, lambda n: (0,) * len(shape))


def _gmlp_head(v_h, ng, nb, w_bf):
    xhat, rstd = _ln_stats(v_h)
    vn = (xhat * ng + nb).astype(BF16)
    return xhat, rstd, vn, _dot(w_bf, vn, NN)


def _pool_group(xb_g, prev_g, first_row, grp):
    t = xb_g.shape[0]
    ext = jnp.concatenate([prev_g, xb_g], axis=0)
    tot = _window_sum(ext, grp + 1, False)[POOL_HALO:, :]
    cnt = _pool_counts(first_row, t, POOL_WINDOWS[grp])
    return tot / cnt - xb_g, cnt


def _even_fwd(proj, ws, bs_t, ng, nb, pool_w, pool_b, pool_scale, *, name):
    s = proj.shape[0]
    t = GMLP_BLOCK
    col, per, prev = _even_specs(t)

    def body(u_ref, v_ref, za_ref, xb_ref, zb_ref, xp_ref, ws_ref, bs_ref, ng_ref, nb_ref, pw_ref, pb_ref, ps_ref, o_ref):
        n = pl.program_id(0)
        mask = _chunk_mask()
        for h in range(GMLP_HEADS):
            c0 = h * GMLP_HEAD_DIM
            cs = slice(c0, c0 + GMLP_HEAD_DIM)
            w_bf = jnp.where(mask, ws_ref[h], 0.0).astype(BF16)
            _, _, _, sv = _gmlp_head(v_ref[:, cs].astype(F32),ng_ref[...], nb_ref[...], w_bf)
            sv = sv + bs_ref[:, h:h + 1]
            za = za_ref[:, cs].astype(F32)
            o_ref[:, cs] = (u_ref[:, cs].astype(F32) * sv * (za * _sigmoid(za))).astype(BF16)
        live = (n > 0).astype(F32)
        for grp in range(POOL_GROUPS):
            c0 = grp * POOL_GROUP_DIM
            cs = slice(c0, c0 + POOL_GROUP_DIM)
            pooled, _ = _pool_group(xb_ref[:, cs].astype(F32), xp_ref[:, cs].astype(F32) * live, n * t, grp)
            yb = _dot(pooled.astype(BF16), pw_ref[grp], NN) + pb_ref[:, cs]
            zb = zb_ref[:, cs].astype(F32)
            o_ref[:, D_MODEL + c0:D_MODEL + c0 + POOL_GROUP_DIM] = (yb * ps_ref[:, cs] * (zb * _sigmoid(zb))).astype(BF16)

    return pl.pallas_call(
        body, name=name, grid=(s // t,),
        in_specs=[col(0), col(1), col(2), col(3), col(4), prev,
                  _full((GMLP_HEADS, t, t)), _full((t, LANES)), _full((1, GMLP_HEAD_DIM)), _full((1, GMLP_HEAD_DIM)),
                  _full((POOL_GROUPS, POOL_GROUP_DIM, POOL_GROUP_DIM)), _full((1, D_MODEL)), _full((1, D_MODEL))],
        out_specs=pl.BlockSpec((t, 2 * D_MODEL), lambda n: (n, 0)),
        out_shape=jax.ShapeDtypeStruct((s, 2 * D_MODEL), BF16),
        compiler_params=_params("parallel"),
    )(proj, proj, proj, proj, proj, proj, ws, bs_t, ng, nb, pool_w, pool_b, pool_scale)


def _even_bwd(proj, dmix, ws, ws_t, bs_t, ng, nb, pool_w, pool_b, pool_scale, *, name):
    s = proj.shape[0]
    t = GMLP_BLOCK
    nblk = s // t
    col, per, prev = _even_specs(t)
    nxt = lambda j: pl.BlockSpec((POOL_HALO, D_MODEL), lambda n: (jnp.minimum((n + 1) * per, nblk * per - 1), j))

    def body(u_ref, v_ref, za_ref, xb_ref, zb_ref, xp_ref, zn_ref, da_ref, db_ref, dbn_ref,
             ws_ref, wst_ref, bs_ref, ng_ref, nb_ref, pw_ref, pb_ref, ps_ref,
             dp_ref, gws_ref, gbs_ref, gng_ref, gnb_ref, gpw_ref, gpb_ref, gps_ref):
        n = pl.program_id(0)

        @pl.when(n == 0)
        def _():
            for r in (gws_ref, gbs_ref, gng_ref, gnb_ref, gpw_ref, gpb_ref, gps_ref):
                r[...] = jnp.zeros_like(r)

        mask, mask_t = _chunk_mask(), _chunk_mask(True)
        lane = lax.broadcasted_iota(jnp.int32, (t, LANES), 1)
        ngv, nbv = ng_ref[...], nb_ref[...]
        for h in range(GMLP_HEADS):
            c0 = h * GMLP_HEAD_DIM
            cs = slice(c0, c0 + GMLP_HEAD_DIM)
            w_bf = jnp.where(mask, ws_ref[h], 0.0).astype(BF16)
            wt_bf = jnp.where(mask_t, wst_ref[h], 0.0).astype(BF16)
            xhat, rstd, vn, sv = _gmlp_head(v_ref[:, cs].astype(F32),ngv, nbv, w_bf)
            sv = sv + bs_ref[:, h:h + 1]
            za, u, da = za_ref[:, cs].astype(F32), u_ref[:, cs].astype(F32), da_ref[:, cs].astype(F32)
            sg = _sigmoid(za)
            sl = za * sg
            dp_ref[:, cs] = (da * sv * sl).astype(BF16)
            dp_ref[:, 2 * D_MODEL + c0:2 * D_MODEL + c0 + GMLP_HEAD_DIM] = (
                da * u * sv * (sg * (1.0 + za * (1.0 - sg)))).astype(BF16)
            dsv = da * u * sl
            gbs_ref[...] += jnp.where(lane == h, jnp.sum(dsv, axis=1, keepdims=True), 0.0)
            dsv_bf = dsv.astype(BF16)
            gws_ref[h] += jnp.where(mask, _dot(dsv_bf, vn, NT), 0.0)
            dvn = _dot(wt_bf, dsv_bf, NN)
            dp_ref[:, D_MODEL + c0:D_MODEL + c0 + GMLP_HEAD_DIM] = _ln_bwd_rows(dvn, xhat, rstd, ngv).astype(BF16)
            gng_ref[...] += _colsum(dvn * xhat)
            gnb_ref[...] += _colsum(dvn)
        live_prev = (n > 0).astype(F32)
        live_next = (n < nblk - 1).astype(F32)
        for grp in range(POOL_GROUPS):
            c0 = grp * POOL_GROUP_DIM
            cs = slice(c0, c0 + POOL_GROUP_DIM)
            xb = xb_ref[:, cs].astype(F32)
            pooled, cnt = _pool_group(xb, xp_ref[:, cs].astype(F32) * live_prev, n * t, grp)
            pooled_bf = pooled.astype(BF16)
            pw = pw_ref[grp]
            yb = _dot(pooled_bf, pw, NN) + pb_ref[:, cs]
            ps = ps_ref[:, cs]
            zb, db = zb_ref[:, cs].astype(F32), db_ref[:, cs].astype(F32)
            sg = _sigmoid(zb)
            sl = zb * sg
            dp_ref[:, 4 * D_MODEL + c0:4 * D_MODEL + c0 + POOL_GROUP_DIM] = (
                db * yb * ps * (sg * (1.0 + zb * (1.0 - sg)))).astype(BF16)
            dsl = db * sl
            dy = dsl * ps
            gps_ref[:, cs] += _colsum(dsl * yb)
            gpb_ref[:, cs] += _colsum(dy)
            dy_bf = dy.astype(BF16)
            gpw_ref[grp] += _dot(pooled_bf, dy_bf, TN)
            r = _dot(dy_bf, pw, NT)
            zn = zn_ref[:, cs].astype(F32)
            dyn = (dbn_ref[:, cs].astype(F32) * (zn * _sigmoid(zn)) * ps * live_next).astype(BF16)
            rn = _dot(dyn, pw, NT) / _pool_counts((n + 1) * t, POOL_HALO, POOL_WINDOWS[grp])
            ext = jnp.concatenate([r / cnt, rn], axis=0)
            dxb = _window_sum(ext, grp + 1, True)[:t, :] - r
            dp_ref[:, 3 * D_MODEL + c0:3 * D_MODEL + c0 + POOL_GROUP_DIM] = dxb.astype(BF16)

    out_shape = [
        jax.ShapeDtypeStruct((s, EVEN_IN), BF16),
        jax.ShapeDtypeStruct((GMLP_HEADS, t, t), F32), jax.ShapeDtypeStruct((t, LANES), F32),
        jax.ShapeDtypeStruct((1, GMLP_HEAD_DIM), F32), jax.ShapeDtypeStruct((1, GMLP_HEAD_DIM), F32),
        jax.ShapeDtypeStruct((POOL_GROUPS, POOL_GROUP_DIM, POOL_GROUP_DIM), F32),
        jax.ShapeDtypeStruct((1, D_MODEL), F32), jax.ShapeDtypeStruct((1, D_MODEL), F32),
    ]
    return pl.pallas_call(
        body, name=name, grid=(nblk,),
        in_specs=[col(0), col(1), col(2), col(3), col(4), prev, nxt(4),
                  pl.BlockSpec((t, D_MODEL), lambda n: (n, 0)), pl.BlockSpec((t, D_MODEL), lambda n: (n, 1)), nxt(1),
                  _full((GMLP_HEADS, t, t)), _full((GMLP_HEADS, t, t)), _full((t, LANES)),
                  _full((1, GMLP_HEAD_DIM)), _full((1, GMLP_HEAD_DIM)),
                  _full((POOL_GROUPS, POOL_GROUP_DIM, POOL_GROUP_DIM)), _full((1, D_MODEL)), _full((1, D_MODEL))],
        out_specs=[pl.BlockSpec((t, EVEN_IN), lambda n: (n, 0))] + [_full(o.shape) for o in out_shape[1:]],
        out_shape=out_shape,
        compiler_params=_params("arbitrary"),
    )(proj, proj, proj, proj, proj, proj, proj, dmix, dmix, dmix, ws, ws_t, bs_t, ng, nb, pool_w, pool_b, pool_scale)


def _half_swap(v):
    lane = lax.broadcasted_iota(jnp.int32, v.shape, 1)
    return jnp.where(lane % MLA_ROPE < MLA_ROPE // 2, pltpu.roll(v, LANES - MLA_ROPE // 2, 1), pltpu.roll(v, MLA_ROPE // 2, 1))


def _rope(v, cos, sin_signed):
    return v * cos + _half_swap(v) * sin_signed


def _rope_bwd(d, cos, sin_signed):
    return d * cos + _half_swap(d * sin_signed)


def _rms(v, g):
    r = lax.rsqrt(jnp.mean(v * v, axis=-1, keepdims=True) + LN_EPS)
    return v * r * g, r


def _rms_bwd(dy, v, r, g):
    u = dy * g
    return r * u - v * (r * r * r) * jnp.mean(u * v, axis=-1, keepdims=True)


def _mla_prep(proj, gq, gkv, cos, sin_signed, *, name):
    s = proj.shape[0]
    ts = _tile(s, 512)

    def body(p_ref, gq_ref, gkv_ref, c_ref, s_ref, q_ref, k_ref):
        qcn, _ = _rms(p_ref[:, :MLA_Q_RANK].astype(F32), gq_ref[...])
        kvn, _ = _rms(p_ref[:, MLA_Q_RANK:MLA_Q_RANK + MLA_KV_RANK].astype(F32), gkv_ref[...])
        kr = _rope(p_ref[:, MLA_Q_RANK + MLA_KV_RANK:].astype(F32), c_ref[...], s_ref[...])
        q_ref[...] = qcn.astype(BF16)
        k_ref[...] = jnp.concatenate([kvn, kr], axis=1).astype(BF16)

    return pl.pallas_call(
        body, name=name, grid=(s // ts,),
        in_specs=[_row_spec(ts, ODD_SMALL_PAD), _vec_spec(MLA_Q_RANK), _vec_spec(MLA_KV_RANK), _row_spec(ts, LANES), _row_spec(ts, LANES)],
        out_specs=[_row_spec(ts, MLA_Q_RANK), _row_spec(ts, QK_PAD)],
        out_shape=[jax.ShapeDtypeStruct((s, MLA_Q_RANK), BF16), jax.ShapeDtypeStruct((s, QK_PAD), BF16)],
        compiler_params=_params("parallel"),
    )(proj, gq, gkv, cos, sin_signed)


def _mla_prep_bwd(proj, dqcn, dkv, gq, gkv, cos, sin_signed, *, name):
    s = proj.shape[0]
    ts = _tile(s, 512)

    def body(p_ref, dq_ref, dkv_ref, gq_ref, gkv_ref, c_ref, s_ref, ds_ref, ggq_ref, ggkv_ref):
        @pl.when(pl.program_id(0) == 0)
        def _():
            ggq_ref[...] = jnp.zeros_like(ggq_ref)
            ggkv_ref[...] = jnp.zeros_like(ggkv_ref)

        qc = p_ref[:, :MLA_Q_RANK].astype(F32)
        kvc = p_ref[:, MLA_Q_RANK:MLA_Q_RANK + MLA_KV_RANK].astype(F32)
        _, rq = _rms(qc, gq_ref[...])
        _, rkv = _rms(kvc, gkv_ref[...])
        dq = dq_ref[...]
        dkvn = dkv_ref[:, :MLA_KV_RANK]
        ggq_ref[...] += _colsum(dq * qc * rq)
        ggkv_ref[...] += _colsum(dkvn * kvc * rkv)
        dkr = _rope_bwd(dkv_ref[:, MLA_KV_RANK:], c_ref[...], s_ref[...])
        ds_ref[...] = jnp.concatenate(
            [_rms_bwd(dq, qc, rq, gq_ref[...]), _rms_bwd(dkvn, kvc, rkv, gkv_ref[...]), dkr], axis=1).astype(BF16)

    return pl.pallas_call(
        body, name=name, grid=(s // ts,),
        in_specs=[_row_spec(ts, ODD_SMALL_PAD), _row_spec(ts, MLA_Q_RANK), _row_spec(ts, QK_PAD),
                  _vec_spec(MLA_Q_RANK), _vec_spec(MLA_KV_RANK), _row_spec(ts, LANES), _row_spec(ts, LANES)],
        out_specs=[_row_spec(ts, ODD_SMALL_PAD), _vec_spec(MLA_Q_RANK), _vec_spec(MLA_KV_RANK)],
        out_shape=[jax.ShapeDtypeStruct((s, ODD_SMALL_PAD), BF16), jax.ShapeDtypeStruct((1, MLA_Q_RANK), F32),
                   jax.ShapeDtypeStruct((1, MLA_KV_RANK), F32)],
        compiler_params=_params("arbitrary"),
    )(proj, dqcn, dkv, gq, gkv, cos, sin_signed)


LOG2_E = 1.4426950408889634
Q_PRESCALE = ATTN_SCALE * LOG2_E


def _q_build(q_nope, q_rope_pre, wuk_hdr, cos, sin_signed, *, name):
    s = q_nope.shape[0]
    ts = _tile(s, 1024)

    def body(qn_ref, qr_ref, w_ref, c_ref, s_ref, o_ref):
        r = _rope(qr_ref[...], c_ref[...], s_ref[...])
        lane = lax.broadcasted_iota(jnp.int32, (ts, LANES), 1)
        for j in range(2):
            ql = _dot(qn_ref[:, j * MLA_NOPE:(j + 1) * MLA_NOPE], w_ref[j], NN)
            rr = r if j == 0 else pltpu.roll(r, MLA_ROPE, 1)
            o_ref[j] = (jnp.concatenate([ql, jnp.where(lane < MLA_ROPE, rr, 0.0)], axis=1) * Q_PRESCALE).astype(BF16)

    return pl.pallas_call(
        body, name=name, grid=(s // ts, MLA_HEADS // 2),
        in_specs=[pl.BlockSpec((ts, 2 * MLA_NOPE), lambda i, p: (i, p)), pl.BlockSpec((ts, LANES), lambda i, p: (i, p)),
                  pl.BlockSpec((2, MLA_NOPE, MLA_KV_RANK), lambda i, p: (p, 0, 0)),
                  pl.BlockSpec((ts, LANES), lambda i, p: (i, 0)), pl.BlockSpec((ts, LANES), lambda i, p: (i, 0))],
        out_specs=pl.BlockSpec((2, ts, QK_PAD), lambda i, p: (p, i, 0)),
        out_shape=jax.ShapeDtypeStruct((MLA_HEADS, s, QK_PAD), BF16),
        compiler_params=_params("parallel", "parallel"),
    )(q_nope, q_rope_pre, wuk_hdr, cos, sin_signed)


def _q_bwd(dq, q_nope, wuk_hrd, cos, sin_signed, *, name):
    s = q_nope.shape[0]
    ts = _tile(s, 1024)

    def body(dq_ref, qn_ref, w_ref, c_ref, s_ref, dn_ref, dr_ref, gw_ref):
        @pl.when(pl.program_id(1) == 0)
        def _():
            gw_ref[...] = jnp.zeros_like(gw_ref)

        lane = lax.broadcasted_iota(jnp.int32, (ts, LANES), 1)
        for j in range(2):
            dql = dq_ref[j, :, :MLA_KV_RANK]
            dn_ref[:, j * MLA_NOPE:(j + 1) * MLA_NOPE] = _dot(dql, w_ref[j], NN).astype(BF16)
            gw_ref[j] += _dot(dql, qn_ref[:, j * MLA_NOPE:(j + 1) * MLA_NOPE], TN)
        hi0 = dq_ref[0, :, MLA_KV_RANK:].astype(F32)
        hi1 = dq_ref[1, :, MLA_KV_RANK:].astype(F32)
        d = jnp.where(lane < MLA_ROPE, hi0, pltpu.roll(hi1, MLA_ROPE, 1))
        dr_ref[...] = _rope_bwd(d, c_ref[...], s_ref[...]).astype(BF16)

    return pl.pallas_call(
        body, name=name, grid=(MLA_HEADS // 2, s // ts),
        in_specs=[pl.BlockSpec((2, ts, QK_PAD), lambda p, i: (p, i, 0)), pl.BlockSpec((ts, 2 * MLA_NOPE), lambda p, i: (i, p)),
                  pl.BlockSpec((2, MLA_KV_RANK, MLA_NOPE), lambda p, i: (p, 0, 0)),
                  pl.BlockSpec((ts, LANES), lambda p, i: (i, 0)), pl.BlockSpec((ts, LANES), lambda p, i: (i, 0))],
        out_specs=[pl.BlockSpec((ts, 2 * MLA_NOPE), lambda p, i: (i, p)), pl.BlockSpec((ts, LANES), lambda p, i: (i, p)),
                   pl.BlockSpec((2, MLA_KV_RANK, MLA_NOPE), lambda p, i: (p, 0, 0))],
        out_shape=[jax.ShapeDtypeStruct((s, MLA_HEADS * MLA_NOPE), BF16), jax.ShapeDtypeStruct((s, MLA_HEADS * MLA_ROPE), BF16),
                   jax.ShapeDtypeStruct((MLA_HEADS, MLA_KV_RANK, MLA_NOPE), F32)],
        compiler_params=_params("parallel", "arbitrary"),
    )(dq, q_nope, wuk_hrd, cos, sin_signed)


ATTN_BQ = 128
ATTN_BK = 512


def _diag_mask(rows, bq, bk, q0, k0):
    qc = (q0 + lax.broadcasted_iota(jnp.int32, (rows, bk), 0) % bq) // CHUNK
    kc = (k0 + lax.broadcasted_iota(jnp.int32, (rows, bk), 1)) // CHUNK
    return kc <= qc


def _attn_fwd(q, k, *, name):
    nh, s, dk = q.shape
    bq, bk = _tile(s, ATTN_BQ), _tile(s, ATTN_BK)
    rows = nh * bq

    def body(q_ref, k_ref, o_ref, lse_ref):
        i = pl.program_id(0)
        qb = q_ref[...].reshape(rows, dk)
        n_before = (i * bq) // bk

        def step(j, width, carry, masked):
            m, l, acc = carry
            k0 = pl.multiple_of(j * bk, bk)
            kb = k_ref[pl.ds(k0, width), :]
            sc = _dot(qb, kb, NT)
            if masked:
                sc = jnp.where(_diag_mask(rows, bq, width, i * bq, k0), sc, NEG)
            m_new = jnp.maximum(m, jnp.max(sc, axis=1, keepdims=True))
            p = jnp.exp2(sc - m_new)
            a = jnp.exp2(m - m_new)
            l = a * l + jnp.sum(p, axis=1, keepdims=True)
            acc = a * acc + _dot(p.astype(BF16), kb[:, :MLA_KV_RANK], NN)
            return m_new, l, acc

        init = (jnp.full((rows, 1), NEG, F32), jnp.zeros((rows, 1), F32), jnp.zeros((rows, MLA_KV_RANK), F32))
        carry = lax.fori_loop(0, n_before, lambda j, c: step(j, bk, c, False), init)
        for part in range(bk // bq):
            @pl.when(i % (bk // bq) == part)
            def _(part=part):
                m, l, acc = step(n_before, (part + 1) * bq, carry, True)
                o_ref[...] = (acc / l).astype(BF16).reshape(nh, bq, MLA_KV_RANK)
                lse_ref[...] = jnp.broadcast_to(m + jnp.log2(l), (rows, LANES)).reshape(nh, bq, LANES)

    return pl.pallas_call(
        body, name=name, grid=(s // bq,),
        in_specs=[pl.BlockSpec((nh, bq, dk), lambda i: (0, i, 0)), pl.BlockSpec((s, dk), lambda i: (0, 0))],
        out_specs=[pl.BlockSpec((nh, bq, MLA_KV_RANK), lambda i: (0, i, 0)), pl.BlockSpec((nh, bq, LANES), lambda i: (0, i, 0))],
        out_shape=[jax.ShapeDtypeStruct((nh, s, MLA_KV_RANK), BF16), jax.ShapeDtypeStruct((nh, s, LANES), F32)],
        compiler_params=_params("parallel"),
    )(q, k)


def _attn_bwd(q, k, do, o, lse, *, name):
    nh, s, dk = q.shape
    bq, bk = _tile(s, ATTN_BQ), _tile(s, ATTN_BK)
    rows = nh * bq

    def body(q_ref, k_ref, do_ref, o_ref, lse_ref, dq_ref, dkv_ref):
        i = pl.program_id(0)
        n_before = (i * bq) // bk

        @pl.when(i == 0)
        def _():
            dkv_ref[...] = jnp.zeros_like(dkv_ref)

        qb = q_ref[...].reshape(rows, dk)
        dob = do_ref[...].reshape(rows, MLA_KV_RANK)
        lse_b = lse_ref[...].reshape(rows, LANES)[:, :1]
        delta = jnp.sum(dob.astype(F32) * o_ref[...].reshape(rows, MLA_KV_RANK).astype(F32), axis=1, keepdims=True)

        def step(j, width, dq, masked):
            j0 = pl.multiple_of(j * bk, bk)
            kb = k_ref[pl.ds(j0, width), :]
            sc = _dot(qb, kb, NT)
            if masked:
                sc = jnp.where(_diag_mask(rows, bq, width, i * bq, j0), sc, NEG)
            p = jnp.exp2(sc - lse_b)
            dp = _dot(dob, kb[:, :MLA_KV_RANK], NT)
            ds_bf = (p * (dp - delta)).astype(BF16)
            dkv_ref[pl.ds(j0, width), :] += _dot(ds_bf, qb, TN) * (1.0 / LOG2_E)
            dkv_ref[pl.ds(j0, width), :MLA_KV_RANK] += _dot(p.astype(BF16), dob, TN)
            return dq + _dot(ds_bf, kb, NN)

        dq_before = lax.fori_loop(0, n_before, lambda j, c: step(j, bk, c, False), jnp.zeros((rows, dk), F32))
        for part in range(bk // bq):
            @pl.when(i % (bk // bq) == part)
            def _(part=part):
                dq = step(n_before, (part + 1) * bq, dq_before, True) * ATTN_SCALE
                dq_ref[...] = dq.astype(BF16).reshape(nh, bq, dk)

    blk = lambda w: pl.BlockSpec((nh, bq, w), lambda i: (0, i, 0))
    return pl.pallas_call(
        body, name=name, grid=(s // bq,),
        in_specs=[blk(dk), pl.BlockSpec((s, dk), lambda i: (0, 0)), blk(MLA_KV_RANK), blk(MLA_KV_RANK), blk(LANES)],
        out_specs=[blk(dk), pl.BlockSpec((s, dk), lambda i: (0, 0))],
        out_shape=[jax.ShapeDtypeStruct((nh, s, dk), BF16), jax.ShapeDtypeStruct((s, dk), F32)],
        compiler_params=_params("arbitrary"),
    )(q, k, do, o, lse)


HEAD_GROUP = 4


def _o_build(o_lat, wuv_hrv, proj, *, name):
    s = proj.shape[0]
    ts = _tile(s, 1024)
    w = HEAD_GROUP * MLA_V

    def body(ol_ref, w_ref, z_ref, og_ref):
        for j in range(HEAD_GROUP):
            cs = slice(j * MLA_V, (j + 1) * MLA_V)
            z = z_ref[:, cs].astype(F32)
            og_ref[:, cs] = (_dot(ol_ref[j], w_ref[j], NN) * (z * _sigmoid(z))).astype(BF16)

    return pl.pallas_call(
        body, name=name, grid=(s // ts, MLA_HEADS // HEAD_GROUP),
        in_specs=[pl.BlockSpec((HEAD_GROUP, ts, MLA_KV_RANK), lambda i, g: (g, i, 0)),
                  pl.BlockSpec((HEAD_GROUP, MLA_KV_RANK, MLA_V), lambda i, g: (g, 0, 0)),
                  pl.BlockSpec((ts, w), lambda i, g: (i, g + 1))],
        out_specs=pl.BlockSpec((ts, w), lambda i, g: (i, g)),
        out_shape=jax.ShapeDtypeStruct((s, MLA_WIDTH), BF16),
        compiler_params=_params("parallel", "parallel"),
    )(o_lat, wuv_hrv, proj)


def _o_bwd(dg, proj, o_lat, wuv_hrv, wuv_hvr, *, name):
    s = proj.shape[0]
    ts = _tile(s, 1024)
    w = HEAD_GROUP * MLA_V

    def body(dg_ref, z_ref, ol_ref, w_ref, wt_ref, dol_ref, dz_ref, gw_ref):
        @pl.when(pl.program_id(1) == 0)
        def _():
            gw_ref[...] = jnp.zeros_like(gw_ref)

        for j in range(HEAD_GROUP):
            cs = slice(j * MLA_V, (j + 1) * MLA_V)
            z, dgj, ol = z_ref[:, cs].astype(F32), dg_ref[:, cs].astype(F32), ol_ref[j]
            sg = _sigmoid(z)
            o = _dot(ol, w_ref[j], NN)
            dz_ref[:, cs] = (dgj * o * (sg * (1.0 + z * (1.0 - sg)))).astype(BF16)
            do_bf = (dgj * (z * sg)).astype(BF16)
            dol_ref[j] = _dot(do_bf, wt_ref[j], NN).astype(BF16)
            gw_ref[j] += _dot(ol, do_bf, TN)

    hs = lambda a, b: pl.BlockSpec((HEAD_GROUP, a, b), lambda g, i: (g, 0, 0))
    return pl.pallas_call(
        body, name=name, grid=(MLA_HEADS // HEAD_GROUP, s // ts),
        in_specs=[pl.BlockSpec((ts, w), lambda g, i: (i, g)), pl.BlockSpec((ts, w), lambda g, i: (i, g + 1)),
                  pl.BlockSpec((HEAD_GROUP, ts, MLA_KV_RANK), lambda g, i: (g, i, 0)),
                  hs(MLA_KV_RANK, MLA_V), hs(MLA_V, MLA_KV_RANK)],
        out_specs=[pl.BlockSpec((HEAD_GROUP, ts, MLA_KV_RANK), lambda g, i: (g, i, 0)),
                   pl.BlockSpec((ts, w), lambda g, i: (i, g)), hs(MLA_KV_RANK, MLA_V)],
        out_shape=[jax.ShapeDtypeStruct((MLA_HEADS, s, MLA_KV_RANK), BF16), jax.ShapeDtypeStruct((s, MLA_WIDTH), BF16),
                   jax.ShapeDtypeStruct((MLA_HEADS, MLA_KV_RANK, MLA_V), F32)],
        compiler_params=_params("parallel", "arbitrary"),
    )(dg, proj, o_lat, wuv_hrv, wuv_hvr)


def _ada_mod(c_all, ada_w, ada_b_sh, *, name):
    nl, _, cols = ada_w.shape

    def body(c_ref, w_ref, b_ref, o_ref):
        c = c_ref[...]
        cond = (c * _sigmoid(c)).astype(BF16)
        for l in range(nl):
            o_ref[l] = _dot(cond, w_ref[l].astype(BF16), NN) + b_ref[l]

    return pl.pallas_call(
        body, name=name, out_shape=jax.ShapeDtypeStruct((nl, c_all.shape[0], cols), F32),
        compiler_params=_params(),
    )(c_all, ada_w, ada_b_sh)


def _ada_grad(c_all_t, dmod_sh, *, name):
    nl, _, cols = dmod_sh.shape
    d = c_all_t.shape[0]

    def body(c_ref, dm_ref, gw_ref):
        c = c_ref[...]
        cond_t = c * _sigmoid(c)
        for l in range(nl):
            gw_ref[l] = lax.dot_general(cond_t, dm_ref[l], (NN, ((), ())), precision=lax.Precision.HIGHEST,
                                        preferred_element_type=F32)

    return pl.pallas_call(
        body, name=name, out_shape=jax.ShapeDtypeStruct((nl, d, cols), F32), compiler_params=_params(),
    )(c_all_t, dmod_sh)


def _sum_devices(parts, *, name):
    def body(p_ref, o_ref):
        acc = p_ref[0]
        for k in range(1, parts.shape[0]):
            acc = acc + p_ref[k]
        o_ref[...] = acc

    return pl.pallas_call(body, name=name, out_shape=jax.ShapeDtypeStruct(parts.shape[1:], F32), compiler_params=_params())(parts)


def _adamw_math(w, g, m, v):
    c1 = 1.0 - ADAM_B1 ** ADAM_STEP
    c2 = 1.0 - ADAM_B2 ** ADAM_STEP
    nm = ADAM_B1 * m + (1.0 - ADAM_B1) * g
    nv = ADAM_B2 * v + (1.0 - ADAM_B2) * (g * g)
    return -ADAM_LR * ((nm / c1) / (jnp.sqrt(nv / c2) + ADAM_EPS) + ADAM_WD * w), nm, nv


ADAMW_BLOCK_BYTES = 1 << 20


def _adamw(w, g, m, v, *, name, after=None):
    shape = w.shape
    a, b = shape[-2], shape[-1]
    lead = 1
    for dim in shape[:-2]:
        lead *= dim
    row_bytes = 4 * b
    if a * row_bytes <= ADAMW_BLOCK_BYTES:
        ta = a
        tl = max(1, min(lead, ADAMW_BLOCK_BYTES // (a * row_bytes)))
        while lead % tl:
            tl -= 1
    else:
        tl = 1
        ta = _tile(a, 256)
    to3 = lambda t: t.reshape(lead, a, b)

    def body(w_ref, g_ref, m_ref, v_ref, *rest):
        d_ref, nm_ref, nv_ref = rest[-3:]
        d_ref[...], nm_ref[...], nv_ref[...] = _adamw_math(w_ref[...], g_ref[...], m_ref[...], v_ref[...])

    spec = pl.BlockSpec((tl, ta, b), lambda i, j: (i, j, 0))
    out = jax.ShapeDtypeStruct((lead, a, b), F32)
    order = [] if after is None else [after]
    res = pl.pallas_call(
        body, name=name, grid=(lead // tl, a // ta), in_specs=[spec] * 4 + [pl.BlockSpec(memory_space=pl.ANY)] * len(order),
        out_specs=[spec] * 3, out_shape=[out] * 3, compiler_params=_params("parallel", "parallel"),
    )(to3(w), to3(g), to3(m), to3(v), *order)
    return [r.reshape(shape) for r in res]


def _adamw_small(ws, gs, ms, vs, *, name):
    n = len(ws)

    def body(*refs):
        for k in range(n):
            w_ref, g_ref, m_ref, v_ref = (refs[j * n + k] for j in range(4))
            d_ref, nm_ref, nv_ref = (refs[(4 + j) * n + k] for j in range(3))
            d_ref[...], nm_ref[...], nv_ref[...] = _adamw_math(w_ref[...], g_ref[...], m_ref[...], v_ref[...])

    outs = [jax.ShapeDtypeStruct(w.shape, F32) for w in ws]
    res = pl.pallas_call(body, name=name, out_shape=outs * 3, compiler_params=_params())(*ws, *gs, *ms, *vs)
    return res[:n], res[n:2 * n], res[2 * n:]


def _flip(v, bit):
    return 1 - v if bit else v


CHIP_DELTAS = ((1, 0), (0, 1), (1, 1))
SUM_ROWS = 32


def _all_gather_chips(shard, *, name):
    def body(x_ref, o_ref, send_sems, recv_sems, local_sem):
        x, y, c = lax.axis_index("x"), lax.axis_index("y"), lax.axis_index("c")
        mine = pltpu.make_async_copy(x_ref, o_ref.at[2 * x + y], local_sem)
        mine.start()

        def copy(k):
            tx, ty = _flip(x, CHIP_DELTAS[k][0]), _flip(y, CHIP_DELTAS[k][1])
            send = pltpu.make_async_remote_copy(src_ref=x_ref, dst_ref=o_ref.at[2 * x + y], send_sem=send_sems.at[k],
                                                recv_sem=recv_sems.at[k], device_id=(tx, ty, c), device_id_type=MESH)
            recv = pltpu.make_async_remote_copy(src_ref=x_ref, dst_ref=o_ref.at[2 * tx + ty], send_sem=send_sems.at[k],
                                                recv_sem=recv_sems.at[k], device_id=(tx, ty, c), device_id_type=MESH)
            return send, recv

        pairs = [copy(k) for k in range(3)]
        for send, _ in pairs:
            send.start()
        for _, recv in pairs:
            recv.wait_recv()
        for send, _ in pairs:
            send.wait_send()
        mine.wait()

    return pl.pallas_call(
        body, name=name, out_shape=jax.ShapeDtypeStruct((N_CHIPS,) + shard.shape, shard.dtype),
        in_specs=[HBM], out_specs=HBM,
        scratch_shapes=[pltpu.SemaphoreType.DMA((3,)), pltpu.SemaphoreType.DMA((3,)), pltpu.SemaphoreType.DMA(())],
    )(shard)


def _gather_weights(shards, *, name):
    n = len(shards)

    def body(*refs):
        w_refs, o_refs = refs[:n], refs[n:2 * n]
        ici_send, ici_recv, d2d_send, d2d_recv, local_sems = refs[2 * n:]
        x, y, c = lax.axis_index("x"), lax.axis_index("y"), lax.axis_index("c")
        me = 2 * x + y
        peers = [(_flip(x, dx), _flip(y, dy)) for dx, dy in CHIP_DELTAS]
        locals_ = [pltpu.make_async_copy(w_refs[k], o_refs[k].at[me], local_sems.at[k]) for k in range(n)]
        for cp in locals_:
            cp.start()

        def rows(k, which):
            half = shards[k].shape[0] // 2
            return pl.ds(pl.multiple_of(which * half, half), half)

        def over_chips(k, d, slot):
            tx, ty = peers[d]
            return pltpu.make_async_remote_copy(
                src_ref=w_refs[k].at[rows(k, c)], dst_ref=o_refs[k].at[slot, rows(k, c)], send_sem=ici_send.at[k, d],
                recv_sem=ici_recv.at[k, d], device_id=(tx, ty, c), device_id_type=MESH)

        def to_sibling(k, d, which):
            tx, ty = peers[d]
            at = o_refs[k].at[2 * tx + ty, rows(k, which)]
            return pltpu.make_async_remote_copy(src_ref=at, dst_ref=at, send_sem=d2d_send.at[k, d], recv_sem=d2d_recv.at[k, d],
                                                device_id=(x, y, 1 - c), device_id_type=MESH)

        sends = [over_chips(k, d, me) for k in range(n) for d in range(3)]
        for cp in sends:
            cp.start()
        passed = []
        for k in range(n):
            for d in range(3):
                over_chips(k, d, 2 * peers[d][0] + peers[d][1]).wait_recv()
                passed.append(to_sibling(k, d, c))
                passed[-1].start()
        for k in range(n):
            for d in range(3):
                to_sibling(k, d, 1 - c).wait_recv()
        for cp in sends + passed:
            cp.wait_send()
        for cp in locals_:
            cp.wait()

    return pl.pallas_call(
        body, name=name, out_shape=[jax.ShapeDtypeStruct((N_CHIPS,) + w.shape, w.dtype) for w in shards],
        in_specs=[HBM] * n, out_specs=[HBM] * n,
        scratch_shapes=[pltpu.SemaphoreType.DMA((n, 3))] * 4 + [pltpu.SemaphoreType.DMA((n,))],
    )(*shards)


def _add_into(dst_ref, src_ref):
    ns, r, _ = dst_ref.shape
    step = SUM_ROWS if r % SUM_ROWS == 0 else r
    for s in range(ns):
        def tile(t, carry):
            at = pl.ds(pl.multiple_of(t * step, step), step)
            dst_ref[s, at, :] = (dst_ref[s, at, :].astype(F32) + src_ref[s, at, :].astype(F32)).astype(dst_ref.dtype)
            return carry
        lax.fori_loop(0, r // step, tile, 0)


def _reduce_sibling(grads, *, name):
    n = len(grads)

    def body(*refs):
        g_refs, o_refs = refs[:n], refs[n:2 * n]
        mine, got = refs[2 * n:3 * n], refs[3 * n:4 * n]
        send_sems, recv_sems, load_sems, store_sems = refs[4 * n:]
        x, y, c = lax.axis_index("x"), lax.axis_index("y"), lax.axis_index("c")
        loads = [pltpu.make_async_copy(g_refs[k].at[:, c], mine[k], load_sems.at[k]) for k in range(n)]
        swaps = [pltpu.make_async_remote_copy(src_ref=g_refs[k].at[:, 1 - c], dst_ref=got[k], send_sem=send_sems.at[k],
                                              recv_sem=recv_sems.at[k], device_id=(x, y, 1 - c), device_id_type=MESH)
                 for k in range(n)]
        for cp in loads + swaps:
            cp.start()
        stores = []
        for k in range(n):
            loads[k].wait()
            swaps[k].wait_recv()
            _add_into(mine[k], got[k])
            stores.append(pltpu.make_async_copy(mine[k], o_refs[k], store_sems.at[k]))
            stores[-1].start()
        for k in range(n):
            swaps[k].wait_send()
            stores[k].wait()

    half = [jax.ShapeDtypeStruct((g.shape[0],) + g.shape[2:], g.dtype) for g in grads]
    return pl.pallas_call(
        body, name=name, out_shape=half, in_specs=[HBM] * n, out_specs=[HBM] * n,
        scratch_shapes=[pltpu.VMEM(h.shape, h.dtype) for h in half] * 2 + [pltpu.SemaphoreType.DMA((n,))] * 4,
        compiler_params=_params(),
    )(*grads)


def _reduce_chips(parts, landed, *, name):
    n_send = len(parts)
    n = n_send + len(landed)

    def body(*refs):
        p_refs, o_refs = refs[:n], refs[n:2 * n]
        got, total = refs[2 * n:3 * n], refs[3 * n:4 * n]
        send_sems, recv_sems, load_sems, share_send, share_recv, store_sems = refs[4 * n:]
        x, y, c = lax.axis_index("x"), lax.axis_index("y"), lax.axis_index("c")
        me = 2 * x + y
        peers = [(_flip(x, dx), _flip(y, dy)) for dx, dy in CHIP_DELTAS]

        def over_chips(k, d, src_slot, dst_slot):
            tx, ty = peers[d]
            return pltpu.make_async_remote_copy(
                src_ref=p_refs[k].at[src_slot], dst_ref=got[k].at[dst_slot], send_sem=send_sems.at[k, d],
                recv_sem=recv_sems.at[k, d], device_id=(tx, ty, c), device_id_type=MESH)

        loads = [pltpu.make_async_copy(p_refs[k].at[me], got[k].at[me], load_sems.at[k]) for k in range(n_send)]
        loads += [pltpu.make_async_copy(p_refs[k], got[k], load_sems.at[k]) for k in range(n_send, n)]
        sends = [over_chips(k, d, 2 * peers[d][0] + peers[d][1], me) for k in range(n_send) for d in range(3)]
        for cp in loads + sends:
            cp.start()
        shares, stores = [], []
        for k in range(n):
            loads[k].wait()
            for d in range(3 if k < n_send else 0):
                slot = 2 * peers[d][0] + peers[d][1]
                over_chips(k, d, slot, slot).wait_recv()
            r = total[k].shape[0]
            step = SUM_ROWS if r % SUM_ROWS == 0 else r

            def tile(t, carry, k=k, step=step):
                at = pl.ds(pl.multiple_of(t * step, step), step)
                acc = got[k][0, at, :].astype(F32)
                for s in range(1, N_CHIPS):
                    acc = acc + got[k][s, at, :].astype(F32)
                total[k][at, :] = acc
                return carry

            lax.fori_loop(0, r // step, tile, 0)
            stores.append(pltpu.make_async_copy(total[k], o_refs[k].at[c], store_sems.at[k]))
            shares.append(pltpu.make_async_remote_copy(
                src_ref=total[k], dst_ref=o_refs[k].at[c], send_sem=share_send.at[k], recv_sem=share_recv.at[k],
                device_id=(x, y, 1 - c), device_id_type=MESH))
            stores[-1].start()
            shares[-1].start()
        for k in range(n):
            pltpu.make_async_remote_copy(
                src_ref=total[k], dst_ref=o_refs[k].at[1 - c], send_sem=share_send.at[k], recv_sem=share_recv.at[k],
                device_id=(x, y, 1 - c), device_id_type=MESH).wait_recv()
        for cp in sends + shares:
            cp.wait_send()
        for cp in stores:
            cp.wait()

    both = list(parts) + list(landed)
    return pl.pallas_call(
        body, name=name, out_shape=[jax.ShapeDtypeStruct((2,) + p.shape[1:], F32) for p in both],
        in_specs=[HBM] * n, out_specs=[HBM] * n,
        scratch_shapes=[pltpu.VMEM(p.shape, p.dtype) for p in both] + [pltpu.VMEM(p.shape[1:], F32) for p in both]
        + [pltpu.SemaphoreType.DMA((n, 3))] * 2 + [pltpu.SemaphoreType.DMA((n,))] * 4,
        compiler_params=_params(),
    )(*both)


SEM = pl.BlockSpec(memory_space=pltpu.SEMAPHORE)
IN_FLIGHT = pltpu.SideEffectType.DATAFLOW_SIDE_EFFECTING


def _chip_copies(s_refs, l_refs, sems, scatter, theirs):
    x, y, c = lax.axis_index("x"), lax.axis_index("y"), lax.axis_index("c")
    me = 2 * x + y
    copies = []
    for k in range(len(s_refs)):
        for d, (dx, dy) in enumerate(CHIP_DELTAS):
            tx, ty = _flip(x, dx), _flip(y, dy)
            peer = 2 * tx + ty
            send_sem, recv_sem = sems[2 * (3 * k + d)], sems[2 * (3 * k + d) + 1]
            copies.append(pltpu.make_async_remote_copy(
                src_ref=s_refs[k].at[peer] if scatter else s_refs[k], dst_ref=l_refs[k].at[peer if theirs else me],
                send_sem=send_sem, recv_sem=recv_sem, device_id=(tx, ty, c), device_id_type=MESH))
    return copies


def _chips_start(srcs, lands, after, *, scatter, name):
    n = len(srcs)
    n_sem = 2 * 3 * n

    def body(*refs):
        s_refs, l_refs = refs[:n], refs[n:2 * n]
        sems = refs[2 * n + 1:2 * n + 1 + n_sem]
        token = refs[-1]
        for cp in _chip_copies(s_refs, l_refs, sems, scatter, False):
            cp.start()
        token[...] = jnp.zeros_like(token)

    hbm = lambda a: pltpu.HBM(a.shape, a.dtype)
    res = pl.pallas_call(
        body, name=name,
        out_shape=(*[pltpu.SemaphoreType.DMA(())] * n_sem, *[hbm(a) for a in srcs], *[hbm(a) for a in lands],
                   jax.ShapeDtypeStruct((8, LANES), F32)),
        in_specs=[HBM] * (2 * n) + [pl.BlockSpec(memory_space=pl.ANY)],
        out_specs=(*[SEM] * n_sem, *[HBM] * (2 * n), VMEM),
        input_output_aliases={k: n_sem + k for k in range(2 * n)},
        compiler_params=pltpu.CompilerParams(has_side_effects=IN_FLIGHT),
    )(*[pltpu.with_memory_space_constraint(a, pltpu.HBM) for a in list(srcs) + list(lands)], after)
    return res[:n_sem], res[n_sem:n_sem + n], res[n_sem + n:n_sem + 2 * n], res[-1]


def _chips_wait(sems, srcs, lands, after, *, scatter, name):
    n = len(srcs)
    n_sem = len(sems)

    def body(*refs):
        s_refs, l_refs = refs[:n], refs[n:2 * n]
        sem_refs = refs[2 * n:2 * n + n_sem]
        for cp in _chip_copies(s_refs, l_refs, sem_refs, scatter, False):
            cp.wait_send()
        for cp in _chip_copies(s_refs, l_refs, sem_refs, scatter, True):
            cp.wait_recv()

    hbm = lambda a: pltpu.HBM(a.shape, a.dtype)
    res = pl.pallas_call(
        body, name=name, out_shape=tuple(hbm(a) for a in list(srcs) + list(lands)),
        in_specs=[HBM] * (2 * n) + [SEM] * n_sem + [pl.BlockSpec(memory_space=pl.ANY)], out_specs=tuple([HBM] * (2 * n)),
        input_output_aliases={k: k for k in range(2 * n)},
        compiler_params=pltpu.CompilerParams(has_side_effects=IN_FLIGHT),
    )(*srcs, *lands, *sems, after)
    return res[n:]


def _all_gather_devices(rows, *, name, after=None):
    deltas = [(dx, dy, dc) for dx in (0, 1) for dy in (0, 1) for dc in (0, 1)][1:]
    order = [] if after is None else [after]

    def body(x_ref, *rest):
        o_ref, send_sems, recv_sems = rest[-3:]
        x, y, c = lax.axis_index("x"), lax.axis_index("y"), lax.axis_index("c")
        me = 4 * x + 2 * y + c
        o_ref[me] = x_ref[...]
        sends, recvs = [], []
        for k, (dx, dy, dc) in enumerate(deltas):
            tx, ty, tc = _flip(x, dx), _flip(y, dy), _flip(c, dc)
            sends.append(pltpu.make_async_remote_copy(src_ref=x_ref, dst_ref=o_ref.at[me], send_sem=send_sems.at[k],
                                                      recv_sem=recv_sems.at[k], device_id=(tx, ty, tc), device_id_type=MESH))
            recvs.append(pltpu.make_async_remote_copy(src_ref=x_ref, dst_ref=o_ref.at[4 * tx + 2 * ty + tc],
                                                      send_sem=send_sems.at[k], recv_sem=recv_sems.at[k],
                                                      device_id=(tx, ty, tc), device_id_type=MESH))
        for cp in sends:
            cp.start()
        for cp in recvs:
            cp.wait_recv()
        for cp in sends:
            cp.wait_send()

    return pl.pallas_call(
        body, name=name, out_shape=jax.ShapeDtypeStruct((N_DEV,) + rows.shape, rows.dtype),
        in_specs=[VMEM] + [pl.BlockSpec(memory_space=pl.ANY)] * len(order), out_specs=VMEM,
        scratch_shapes=[pltpu.SemaphoreType.DMA((N_DEV - 1,)), pltpu.SemaphoreType.DMA((N_DEV - 1,))],
    )(rows, *order)


WEIGHTS = ("ada_w", "ada_b", "ln_g", "ln_b", "e_w_in", "gmlp_norm_g", "gmlp_norm_b", "gmlp_ws", "gmlp_bs", "pool_w",
           "pool_b", "pool_scale", "e_w_out", "o_w_in", "mla_q_norm_g", "mla_kv_norm_g", "mla_w_uq", "mla_w_uk",
           "mla_w_uv", "o_w_out")
SMALL = ("ln_g", "ln_b", "gmlp_norm_g", "gmlp_norm_b", "gmlp_bs", "pool_b", "pool_scale", "mla_kv_norm_g", "mla_q_norm_g")


def _pad_cols(v, n):
    return jnp.concatenate([v, jnp.zeros((v.shape[0], n - v.shape[1]), v.dtype)], axis=1) if n > v.shape[1] else v


def _halves(g):
    return g.reshape(g.shape[0], 2, g.shape[1] // 2, g.shape[2])


def kernel(x, c, positions, ada_w, ada_b, ln_g, ln_b, e_w_in, gmlp_norm_g, gmlp_norm_b, gmlp_ws, gmlp_bs, pool_w, pool_b, pool_scale, e_w_out, o_w_in, mla_q_norm_g, mla_kv_norm_g, mla_w_uq, mla_w_uk, mla_w_uv, o_w_out, loss_target, m_ada_w, m_ada_b, m_ln_g, m_ln_b, m_e_w_in, m_gmlp_norm_g, m_gmlp_norm_b, m_gmlp_ws, m_gmlp_bs, m_pool_w, m_pool_b, m_pool_scale, m_e_w_out, m_o_w_in, m_mla_q_norm_g, m_mla_kv_norm_g, m_mla_w_uq, m_mla_w_uk, m_mla_w_uv, m_o_w_out, v_ada_w, v_ada_b, v_ln_g, v_ln_b, v_e_w_in, v_gmlp_norm_g, v_gmlp_norm_b, v_gmlp_ws, v_gmlp_bs, v_pool_w, v_pool_b, v_pool_scale, v_e_w_out, v_o_w_in, v_mla_q_norm_g, v_mla_kv_norm_g, v_mla_w_uq, v_mla_w_uk, v_mla_w_uv, v_o_w_out):
    args = dict(locals())
    weights = {n: args[n] for n in WEIGHTS}
    mom = {n: args["m_" + n] for n in WEIGHTS}
    var = {n: args["v_" + n] for n in WEIGHTS}
    ax, ay, ac = lax.axis_index("x"), lax.axis_index("y"), lax.axis_index("c")
    chip = 2 * ax + ay
    dev = 2 * chip + ac
    d = D_MODEL
    x2 = x[0]
    target = loss_target[0]
    q_rank_sh = mla_q_norm_g.shape[1]

    empty_zone = lambda w: lax.dynamic_update_slice(lax.empty((N_CHIPS,) + w.shape, w.dtype), w[None], (chip, 0, 0))
    shards0 = [w.astype(BF16) for w in (pool_w[0].reshape(-1, POOL_GROUP_DIM), e_w_out[0])]
    shards1 = [w.astype(BF16) for w in (o_w_in[0], mla_w_uq[0].reshape(q_rank_sh, -1), o_w_out[0])]
    w_in0, = _gather_weights([e_w_in[0].astype(BF16)], name="gather_weights")
    wuk_hrd = jnp.transpose(mla_w_uk[0], (1, 0, 2)).astype(BF16)
    wuk_hdr = jnp.transpose(mla_w_uk[0], (1, 2, 0)).astype(BF16)
    wuv_hrv = jnp.transpose(mla_w_uv[0], (1, 0, 2)).astype(BF16)
    wuv_hvr = jnp.transpose(mla_w_uv[0], (1, 2, 0)).astype(BF16)
    ws = gmlp_ws[0]
    ws_t = jnp.transpose(ws, (0, 2, 1))
    bs_t = _pad_cols(gmlp_bs[0].T, LANES)

    inv = 1.0 / (ROPE_THETA ** (jnp.arange(0, MLA_ROPE, 2, dtype=F32) / MLA_ROPE))
    ang = positions[0].astype(F32)[:, None] * inv
    cos_t = jnp.tile(jnp.cos(ang), (1, 4))
    sin_t = jnp.tile(jnp.concatenate([-jnp.sin(ang), jnp.sin(ang)], axis=1), (1, 2))

    c_all = _all_gather_devices(c.reshape(8, LANES), after=w_in0, name="gather_c").reshape(N_DEV, d)
    cols = ada_w.shape[2]
    ada_b_mine = lax.dynamic_slice_in_dim(ada_b, chip * cols, cols, axis=1)[:, None, :]
    mod_sh = _ada_mod(c_all, ada_w, ada_b_mine, name="ada_mod")
    q_norm_rows = jnp.zeros((8, cols), F32).at[0, :q_rank_sh].set(mla_q_norm_g[0])
    mod_all = _all_gather_chips(jnp.concatenate([mod_sh.reshape(2 * N_DEV, cols), q_norm_rows]), name="gather_mod")
    q_norm_g = mod_all[:, 2 * N_DEV, :q_rank_sh].reshape(1, -1)
    mod_all = jnp.transpose(mod_all[:, :2 * N_DEV].reshape(N_CHIPS, 2, N_DEV, cols), (1, 2, 0, 3)).reshape(2, N_DEV, 3 * d)
    mod = lax.dynamic_index_in_dim(mod_all, dev, axis=1, keepdims=False)
    shift = [mod[l:l + 1, :d] for l in range(2)]
    scale = [mod[l:l + 1, d:2 * d] for l in range(2)]
    gate = [mod[l:l + 1, 2 * d:] for l in range(2)]
    flight0 = _chips_start(shards0, [empty_zone(w) for w in shards0], mod, scatter=False, name="gather0_start")
    flight1 = _chips_start(shards1, [empty_zone(w) for w in shards1], flight0[3], scatter=False, name="gather1_start")

    scale[0] = scale[0] + flight1[3][:1, :1]
    h0 = _modulate(x2, scale[0], shift[0], name="modulate0")
    proj0 = _matmul(h0, w_in0, b_stacked=True, tm=1024, tn=1280, out_dtype=BF16, name="proj0")
    pool_w_g, w_out0 = _chips_wait(*flight0[:3], proj0, scatter=False, name="gather0_wait")
    pool_w_bf = jnp.transpose(pool_w_g.reshape(N_CHIPS, POOL_GROUPS, -1, POOL_GROUP_DIM), (1, 0, 2, 3)).reshape(
        POOL_GROUPS, POOL_GROUP_DIM, POOL_GROUP_DIM)
    w_out0 = w_out0.reshape(-1, d)
    mix0 = _even_fwd(proj0, ws, bs_t, gmlp_norm_g, gmlp_norm_b, pool_w_bf, pool_b, pool_scale, name="even_fwd")
    y0, x1, h1 = _out_resid_ln(mix0, w_out0, x2, gate[0], ln_g[0:1], ln_b[0:1], scale[1], shift[1], name="out0_ln")

    w_in1_g, w_uq_g, w_out1 = _chips_wait(*flight1[:3], h1, scatter=False, name="gather1_wait")
    w_out1 = w_out1.reshape(-1, d)
    w_in1 = jnp.transpose(w_in1_g, (1, 0, 2)).reshape(d, ODD_IN)
    w_in1 = jnp.concatenate([_pad_cols(w_in1[:, :ODD_SMALL], ODD_SMALL_PAD), w_in1[:, ODD_SMALL:]], axis=1)
    w_uq = w_uq_g.reshape(MLA_Q_RANK, MLA_HEADS, MLA_NOPE + MLA_ROPE)
    w_uq_nope = w_uq[:, :, :MLA_NOPE].reshape(MLA_Q_RANK, -1)
    w_uq_rope = w_uq[:, :, MLA_NOPE:].reshape(MLA_Q_RANK, -1)
    proj1 = _matmul(h1, w_in1, tm=1024, tn=1280, out_dtype=BF16, name="proj1")
    q_cn, keys = _mla_prep(proj1, q_norm_g, mla_kv_norm_g, cos_t, sin_t, name="mla_prep")
    q_nope = _matmul(q_cn, w_uq_nope, tm=1024, tn=2048, name="q_nope", out_dtype=BF16)
    q_rope_pre = _matmul(q_cn, w_uq_rope, tm=1024, name="q_rope")
    q = _q_build(q_nope, q_rope_pre, wuk_hdr, cos_t, sin_t, name="q_build")
    o_lat, lse = _attn_fwd(q, keys, name="attn_fwd")
    og = _o_build(o_lat, wuv_hrv, proj1, name="o_build")

    dy1, dres1, g_ln_g1, g_ln_b1, dgate1, loss = _out_loss_ln_bwd(
        og, w_out1, x1, gate[1], ln_g[1:2], ln_b[1:2], target, name="out1_loss_ln")
    dg1 = _matmul(dy1, w_out1, trans_b=True, tn=2048, out_dtype=BF16, name="d_og")
    g_w_out1 = _matmul(og, dy1, trans_a=True, out_dtype=BF16, tm=1024, name="g_out1")
    do_lat, dz, g_uv = _o_bwd(dg1, proj1, o_lat, wuv_hrv, wuv_hvr, name="o_bwd")
    dq, dkeys = _attn_bwd(q, keys, do_lat, o_lat, lse, name="attn_bwd")
    dq_nope, dq_rope, g_uk = _q_bwd(dq, q_nope, wuk_hrd, cos_t, sin_t, name="q_bwd")
    dq_cn = (_matmul(dq_nope, w_uq_nope, trans_b=True, tm=1024, name="d_qcn_nope")
             + _matmul(dq_rope, w_uq_rope, trans_b=True, tm=1024, name="d_qcn_rope"))
    g_uq_nope = _matmul(q_cn, dq_nope, trans_a=True, out_dtype=BF16, tn=2048, name="g_uq_nope")
    g_uq_rope = _matmul(q_cn, dq_rope, trans_a=True, out_dtype=BF16, name="g_uq_rope")
    dsmall, g_qg, g_kvg = _mla_prep_bwd(proj1, dq_cn, dkeys, q_norm_g, mla_kv_norm_g, cos_t, sin_t, name="mla_prep_bwd")
    dproj1 = jnp.concatenate([dsmall, dz], axis=1)
    g_w_in1 =_matmul(h1, dproj1, trans_a=True, out_dtype=BF16, tm=1024, tn=1280, name="g_in1")

    g_uq = jnp.concatenate([g_uq_nope.reshape(MLA_Q_RANK, MLA_HEADS, MLA_NOPE), g_uq_rope.reshape(MLA_Q_RANK, MLA_HEADS, MLA_ROPE)], axis=2)
    g_w_in1 = jnp.concatenate([g_w_in1[:, :ODD_SMALL], g_w_in1[:, ODD_SMALL_PAD:]], axis=1)
    g_w_in1 = jnp.transpose(g_w_in1.reshape(d, N_CHIPS, -1), (1, 0, 2))
    big1 = [
        _halves(g_w_in1),
        _halves(g_uq.reshape(N_CHIPS, q_rank_sh, -1)),
        _halves(g_w_out1.reshape(N_CHIPS, -1, d)),
        _halves(g_uk.astype(BF16).reshape(N_CHIPS, -1, MLA_NOPE)),
        _halves(g_uv.astype(BF16).reshape(N_CHIPS, -1, MLA_V)),
    ]
    parts1 = _reduce_sibling(big1, name="reduce_sibling1")
    lands2 = [lax.dynamic_update_slice(lax.empty(p.shape, BF16), lax.dynamic_slice_in_dim(p, chip, 1, axis=0), (chip, 0, 0))
              for p in parts1]
    flight2 = _chips_start(parts1, lands2, loss, scatter=True, name="reduce1_start")

    gate[0] = gate[0] + flight2[3][:1, :1]
    dy0, dres0, g_ln_g0, g_ln_b0, dgate0, dscale1, dshift1 = _dh_mid_ln_bwd(
        dproj1, w_in1, x2, y0, gate[0], ln_g[0:1], ln_b[0:1], dres1, scale[1], x1, name="d_h1_mid_ln")
    dmix0 = _matmul(dy0, w_out0, trans_b=True, tn=2048, out_dtype=BF16, name="d_mix0")
    g_w_out0 = _matmul(mix0, dy0, trans_a=True, out_dtype=BF16, tm=1024, name="g_out0")
    dproj0, g_ws, g_bs_t, g_ng, g_nb, g_pw, g_pb, g_ps = _even_bwd(
        proj0, dmix0, ws, ws_t, bs_t, gmlp_norm_g, gmlp_norm_b, pool_w_bf, pool_b, pool_scale, name="even_bwd")
    g_w_in0 = _matmul(h0, dproj0, trans_a=True, out_dtype=BF16, out_stacked=True, tm=1024, tn=1280, name="g_in0")

    g_pw = jnp.transpose(g_pw.astype(BF16).reshape(POOL_GROUPS, N_CHIPS, -1, POOL_GROUP_DIM), (1, 0, 2, 3))
    big0 = [
        _halves(g_w_in0),
        _halves(g_pw.reshape(N_CHIPS, -1, POOL_GROUP_DIM)),
        _halves(g_w_out0.reshape(N_CHIPS, -1, d)),
        _halves(g_ws.astype(BF16)),
    ]
    parts0 = _reduce_sibling(big0, name="reduce_sibling0")
    landed1 = _chips_wait(*flight2[:3], parts0[0], scatter=True, name="reduce1_wait")
    lands3 = [lax.dynamic_update_slice(lax.empty(p.shape, BF16), lax.dynamic_slice_in_dim(p, chip, 1, axis=0), (chip, 0, 0))
              for p in parts0]
    flight3 = _chips_start(parts0, lands3, landed1[0], scatter=True, name="reduce0_start")
    grad_x, dscale0, dshift0 = _dh_input_bwd(dproj0, w_in0, x2, dres0, scale[0], after=flight3[3], name="d_h0_input")

    small_local = {
        "ln_g": jnp.concatenate([g_ln_g0, g_ln_g1]), "ln_b": jnp.concatenate([g_ln_b0, g_ln_b1]),
        "gmlp_norm_g": g_ng, "gmlp_norm_b": g_nb, "gmlp_bs": g_bs_t[:, :GMLP_HEADS].T, "pool_b": g_pb, "pool_scale": g_ps,
        "mla_kv_norm_g": g_kvg, "mla_q_norm_g": g_qg,
    }
    n_mod = 2 * 3 * d
    vec = jnp.concatenate([dshift0, dscale0, dgate0, dshift1, dscale1, dgate1]
                          + [small_local[n].reshape(1, -1) for n in SMALL] + [loss], axis=1)
    n_vec = vec.shape[1]
    vec = _pad_cols(vec, -(-n_vec // (8 * LANES)) * 8 * LANES).reshape(-1, LANES)
    vec_all = _all_gather_devices(vec, name="gather_small")
    vec_sum = _sum_devices(vec_all, name="sum_small").reshape(-1)
    dmod_all = vec_all.reshape(N_DEV, -1)[:, :n_mod].reshape(N_DEV, 2, 3 * d)
    dmod_sh = jnp.transpose(lax.dynamic_slice_in_dim(dmod_all, chip * cols, cols, axis=2), (1, 0, 2))
    dmod_sh = jnp.concatenate([dmod_sh, jnp.zeros((2, LANES - N_DEV, cols), F32)], axis=1)
    grads = {"ada_w": _ada_grad(_pad_cols(c_all.T, LANES), dmod_sh, name="ada_grad"), "ada_b": vec_sum[:n_mod].reshape(2, 3 * d)}
    off = n_mod
    for n in SMALL:
        sz = small_local[n].size
        grads[n] = vec_sum[off:off + sz]
        off += sz
    grads["mla_q_norm_g"] = lax.dynamic_slice_in_dim(grads["mla_q_norm_g"], chip * q_rank_sh, q_rank_sh)
    for n in SMALL:
        grads[n] = grads[n].reshape(weights[n].shape)

    landed0 = _chips_wait(*flight3[:3], grads["ada_w"], scatter=True, name="reduce0_wait")
    totals = _reduce_chips([], list(landed0) + list(landed1), name="reduce_chips")
    for n, t in zip(("e_w_in", "pool_w", "e_w_out", "gmlp_ws", "o_w_in", "mla_w_uq", "o_w_out"), totals):
        if n != "gmlp_ws":
            grads[n] = t.reshape(weights[n].shape)
    rep = jnp.concatenate([t.reshape(-1, LANES) for t in (totals[3], totals[7], totals[8])])
    rep_land = lax.dynamic_update_slice(lax.empty((N_CHIPS,) + rep.shape, F32), rep[None], (chip, 0, 0))
    flight4 = _chips_start([rep], [rep_land], totals[0], scatter=False, name="gather_rep_start")

    delta, new_m, new_v = {}, {}, {}
    replicated = ("gmlp_ws", "mla_w_uk", "mla_w_uv")
    large = [n for n in WEIGHTS if n not in SMALL and n != "ada_b"]
    for n in large:
        if n not in replicated:
            delta[n], new_m[n], new_v[n] = _adamw(weights[n], grads[n], mom[n], var[n], after=flight4[3], name="adamw_" + n)
    rep = _chips_wait(*flight4[:3], delta["e_w_in"], scatter=False, name="gather_rep_wait")[0]
    r_ws, r_uk = GMLP_BLOCK, 4 * MLA_KV_RANK
    grads["gmlp_ws"] = rep[:, :r_ws].reshape(weights["gmlp_ws"].shape)
    grads["mla_w_uk"] = jnp.transpose(rep[:, r_ws:r_ws + r_uk].reshape(MLA_HEADS, MLA_KV_RANK, MLA_NOPE), (1, 0, 2))[None]
    grads["mla_w_uv"] = jnp.transpose(rep[:, r_ws + r_uk:].reshape(MLA_HEADS, MLA_KV_RANK, MLA_V), (1, 0, 2))[None]
    for n in replicated:
        delta[n], new_m[n], new_v[n] = _adamw(weights[n], grads[n], mom[n], var[n], name="adamw_" + n)
    small = [n for n in WEIGHTS if n not in large]
    ds, ms, vs = _adamw_small([weights[n] for n in small], [grads[n] for n in small], [mom[n] for n in small],
                              [var[n] for n in small], name="adamw_small")
    for n, dn, mn, vn in zip(small, ds, ms, vs):
        delta[n], new_m[n], new_v[n] = dn, mn, vn

    return (vec_sum[n_vec - 1], grad_x[None], *[grads[n] for n in WEIGHTS], *[delta[n] for n in WEIGHTS],
            *[new_m[n] for n in WEIGHTS], *[new_v[n] for n in WEIGHTS])
```

```python
import jax
import jax.numpy as jnp
from jax import lax
from jax.experimental import pallas as pl
from jax.experimental.pallas import tpu as pltpu

F32 = jnp.float32
BF16 = jnp.bfloat16
MESH = pl.DeviceIdType.MESH

D_MODEL = 1024
CHUNK = 64
LN_EPS = 1e-5
GMLP_HEADS = 4
GMLP_HEAD_DIM = 256
GMLP_BLOCK = 128
POOL_WINDOWS = (2, 4, 8, 16)
POOL_GROUPS = 4
POOL_GROUP_DIM = 256
POOL_HALO = 16
EVEN_IN = 5120
MLA_HEADS = 16
MLA_NOPE = 128
MLA_ROPE = 64
MLA_V = 128
MLA_Q_RANK = 256
MLA_KV_RANK = 128
MLA_WIDTH = MLA_HEADS * MLA_V
ODD_IN = 2496
ODD_SMALL = MLA_Q_RANK + MLA_KV_RANK + MLA_ROPE
ODD_SMALL_PAD = 512
QK_PAD = 256
ROPE_THETA = 10000.0
ATTN_SCALE = (MLA_NOPE + MLA_ROPE) ** -0.5
DEEPNORM_ALPHA = (2.0 * 2) ** 0.25
ADAM_LR = 0.001
ADAM_B1 = 0.9
ADAM_B2 = 0.999
ADAM_EPS = 1e-08
ADAM_WD = 0.01
ADAM_STEP = 10
NEG = -1e30
LANES = 128
N_DEV = 8
N_CHIPS = 4
VMEM_LIMIT_BYTES = 56 * 1024 * 1024
HBM = pl.BlockSpec(memory_space=pltpu.HBM)
VMEM = pl.BlockSpec(memory_space=pltpu.VMEM)


def _params(*sem):
    return pltpu.CompilerParams(dimension_semantics=sem if sem else None, vmem_limit_bytes=VMEM_LIMIT_BYTES)


def _tile(dim, pref):
    for t in (pref, 2048, 1280, 1024, 512, 256, 128):
        if t <= min(pref, dim) and dim % t == 0:
            return t
    return dim


def _sigmoid(z):
    return 1.0 / (1.0 + jnp.exp(-z))


def _dot(a, b, dims):
    return lax.dot_general(a, b, (dims, ((), ())), preferred_element_type=F32)


NN = ((1,), (0,))
NT = ((1,), (1,))
TN = ((0,), (0,))


def _matmul(a, b, *, name, trans_a=False, trans_b=False, out_dtype=F32, b_stacked=False, out_stacked=False,
            tm=512, tn=1024, tk=2048, after=None):
    k, m = a.shape if trans_a else a.shape[::-1]
    if b_stacked:
        ns, kb, n_sh = b.shape
        kb, n = (ns * n_sh, kb) if trans_b else (kb, ns * n_sh)
    else:
        n, kb = b.shape if trans_b else b.shape[::-1]
    assert k == kb, (a.shape, b.shape)
    tm = _tile(m, tm)
    if b_stacked and trans_b:
        tn, tk = _tile(n, tn), n_sh
    elif b_stacked or out_stacked:
        tn, tk = _tile(n // N_CHIPS, tn), _tile(k, tk)
    else:
        tn, tk = _tile(n, tn), _tile(k, tk)
    nk = k // tk
    per = max((n // N_CHIPS) // tn, 1)
    dims = ((0 if trans_a else 1,), (1 if trans_b else 0,))

    def body_one(a_ref, b_ref, *rest):
        o_ref = rest[-1]
        o_ref[...] = _dot(a_ref[...].astype(BF16), b_ref[...].astype(BF16), dims).astype(out_dtype)

    def body_acc(a_ref, b_ref, *rest):
        o_ref, acc_ref = rest[-2:]
        kk = pl.program_id(2)

        @pl.when(kk == 0)
        def _():
            acc_ref[...] = jnp.zeros_like(acc_ref)

        acc_ref[...] += _dot(a_ref[...].astype(BF16), b_ref[...].astype(BF16), dims)

        @pl.when(kk == nk - 1)
        def _():
            o_ref[...] = acc_ref[...].astype(out_dtype)

    a_spec = pl.BlockSpec((tk, tm), lambda i, j, kk: (kk, i)) if trans_a else pl.BlockSpec((tm, tk), lambda i, j, kk: (i, kk))
    if b_stacked and trans_b:
        b_spec = pl.BlockSpec((None, tn, tk), lambda i, j, kk: (kk, j, 0))
    elif b_stacked:
        b_spec = pl.BlockSpec((None, tk, tn), lambda i, j, kk: (j // per, kk, j % per))
    elif trans_b:
        b_spec = pl.BlockSpec((tn, tk), lambda i, j, kk: (j, kk))
    else:
        b_spec = pl.BlockSpec((tk, tn), lambda i, j, kk: (kk, j))
    if out_stacked:
        o_spec = pl.BlockSpec((None, tm, tn), lambda i, j, kk: (j // per, i, j % per))
        o_shape = jax.ShapeDtypeStruct((N_CHIPS, m, n // N_CHIPS), out_dtype)
    else:
        o_spec = pl.BlockSpec((tm, tn), lambda i, j, kk: (i, j))
        o_shape = jax.ShapeDtypeStruct((m, n), out_dtype)
    order = [] if after is None else [after]
    return pl.pallas_call(
        body_one if nk == 1 else body_acc, name=name, grid=(m // tm, n // tn, nk),
        in_specs=[a_spec, b_spec] + [pl.BlockSpec(memory_space=pl.ANY)] * len(order),
        out_specs=o_spec, out_shape=o_shape, scratch_shapes=[] if nk == 1 else [pltpu.VMEM((tm, tn), F32)],
        compiler_params=_params("parallel", "parallel", "arbitrary"),
    )(a, b, *order)


def _matmul_rows(a, b, epilogue, row_ins, vec_ins, row_outs, vec_outs, *, name, trans_b=False, b_stacked=False,
                 tm=512, tk=2048, after=None):
    m, k = a.shape
    if b_stacked:
        ns, n, n_sh = b.shape
        assert trans_b and ns * n_sh == k
        tk = n_sh
    else:
        n = b.shape[0] if trans_b else b.shape[1]
        tk = _tile(k, tk)
    tm = _tile(m, tm)
    nk = k // tk
    dims = ((1,), (1 if trans_b else 0,))
    n_ri, n_vi, n_ro, n_vo = len(row_ins), len(vec_ins), len(row_outs), len(vec_outs)
    order = [] if after is None else [after]

    def body(*refs):
        a_ref, b_ref = refs[:2]
        pos = 2
        rin = refs[pos:pos + n_ri]
        pos += n_ri
        vin = refs[pos:pos + n_vi]
        pos += n_vi + len(order)
        rout = refs[pos:pos + n_ro]
        pos += n_ro
        vout = refs[pos:pos + n_vo]
        first = pl.program_id(0) == 0
        part = _dot(a_ref[...].astype(BF16), b_ref[...].astype(BF16), dims)
        if nk == 1:
            epilogue(part, first, rin, vin, rout, vout)
        else:
            acc_ref = refs[-1]
            kk = pl.program_id(1)

            @pl.when(kk == 0)
            def _():
                acc_ref[...] = part

            @pl.when(kk > 0)
            def _():
                acc_ref[...] += part

            @pl.when(kk == nk - 1)
            def _():
                epilogue(acc_ref[...], first, rin, vin, rout, vout)

    a_spec = pl.BlockSpec((tm, tk), lambda i, kk: (i, kk))
    if b_stacked:
        b_spec = pl.BlockSpec((None, n, tk), lambda i, kk: (kk, 0, 0))
    elif trans_b:
        b_spec = pl.BlockSpec((n, tk), lambda i, kk: (0, kk))
    else:
        b_spec = pl.BlockSpec((tk, n), lambda i, kk: (kk, 0))
    row = pl.BlockSpec((tm, n), lambda i, kk: (i, 0))
    vec = lambda w: pl.BlockSpec((1, w), lambda i, kk: (0, 0))
    return pl.pallas_call(
        body, name=name, grid=(m // tm, nk),
        in_specs=[a_spec, b_spec] + [row] * n_ri + [vec(v.shape[1]) for v in vec_ins] + [pl.BlockSpec(memory_space=pl.ANY)] * len(order),
        out_specs=[row] * n_ro + [vec(w) for w in vec_outs],
        out_shape=[jax.ShapeDtypeStruct((m, n), dt) for dt in row_outs] + [jax.ShapeDtypeStruct((1, w), F32) for w in vec_outs],
        scratch_shapes=[] if nk == 1 else [pltpu.VMEM((tm, n), F32)],
        compiler_params=_params("arbitrary", "arbitrary"),
    )(a, b, *row_ins, *vec_ins, *order)


def _row_spec(ts, d):
    return pl.BlockSpec((ts, d), lambda i: (i, 0))


def _vec_spec(d):
    return pl.BlockSpec((1, d), lambda i: (0, 0))


def _modulate(x, scale, shift, *, name):
    s, d = x.shape
    ts = _tile(s, 512)

    def body(x_ref, sc_ref, sh_ref, h_ref):
        h_ref[...] = (x_ref[...] * (1.0 + sc_ref[...]) + sh_ref[...]).astype(BF16)

    return pl.pallas_call(
        body, name=name, grid=(s // ts,), in_specs=[_row_spec(ts, d), _vec_spec(d), _vec_spec(d)],
        out_specs=_row_spec(ts, d), out_shape=jax.ShapeDtypeStruct((s, d), BF16), compiler_params=_params("parallel"),
    )(x, scale, shift)


def _ln_stats(pre):
    mu = jnp.mean(pre, axis=-1, keepdims=True)
    xc = pre - mu
    var = jnp.mean(xc * xc, axis=-1, keepdims=True)
    rstd = lax.rsqrt(var + LN_EPS)
    return xc * rstd, rstd


def _ln_bwd_rows(dout, xhat, rstd, g):
    dxh = dout * g
    m1 = jnp.mean(dxh, axis=-1, keepdims=True)
    m2 = jnp.mean(dxh * xhat, axis=-1, keepdims=True)
    return rstd * (dxh - m1 - xhat * m2)


def _colsum(v):
    return jnp.sum(v, axis=0, keepdims=True)


def _out_resid_ln(mix, w_out, x, gate, g, b, scale_next, shift_next, *, name):
    def epilogue(y, first, rin, vin, rout, vout):
        (x_ref,), (gate_ref, g_ref, b_ref, sc_ref, sh_ref), (y_ref, xn_ref, h_ref) = rin, vin, rout
        y_ref[...] = y
        pre = DEEPNORM_ALPHA * x_ref[...] + (1.0 + gate_ref[...]) * y
        xhat, _ = _ln_stats(pre)
        xn = xhat * g_ref[...] + b_ref[...]
        xn_ref[...] = xn
        h_ref[...] = (xn * (1.0 + sc_ref[...]) + sh_ref[...]).astype(BF16)

    return _matmul_rows(mix, w_out, epilogue, [x], [gate, g, b, scale_next, shift_next], [F32, F32, BF16], [], name=name)


def _out_loss_ln_bwd(og, w_out, x, gate, g, b, target, *, name):
    d = x.shape[1]

    def epilogue(yv, first, rin, vin, rout, vout):
        (x_ref, t_ref), (gate_ref, g_ref, b_ref), (dy_ref, dres_ref), (dg_ref, db_ref, dgate_ref, loss_ref) = rin, vin, rout, vout

        @pl.when(first)
        def _():
            for r in vout:
                r[...] = jnp.zeros_like(r)

        pre = DEEPNORM_ALPHA * x_ref[...] + (1.0 + gate_ref[...]) * yv
        xhat, rstd = _ln_stats(pre)
        diff = xhat * g_ref[...] + b_ref[...] - t_ref[...]
        loss_ref[...] += (0.5 / d) * jnp.sum(jnp.sum(diff * diff, axis=1, keepdims=True), axis=0, keepdims=True)
        dout = diff * (1.0 / d)
        dpre = _ln_bwd_rows(dout, xhat, rstd, g_ref[...])
        dy_ref[...] = (dpre * (1.0 + gate_ref[...])).astype(BF16)
        dres_ref[...] = DEEPNORM_ALPHA * dpre
        dg_ref[...] += _colsum(dout * xhat)
        db_ref[...] += _colsum(dout)
        dgate_ref[...] += _colsum(dpre * yv)

    return _matmul_rows(og, w_out, epilogue, [x, target], [gate, g, b], [BF16, F32], [d, d, d, 1], name=name)


def _dh_mid_ln_bwd(dproj, w_in, x, y, gate, g, b, dres_next, scale_next, x_next, *, name):
    d = x.shape[1]

    def epilogue(dh, first, rin, vin, rout, vout):
        (x_ref, y_ref, dr_ref, xn_ref), (gate_ref, g_ref, b_ref, sc_ref), (dy_ref, dres_ref) = rin, vin, rout
        dg_ref, db_ref, dgate_ref, dscale_ref, dshift_ref = vout

        @pl.when(first)
        def _():
            for r in vout:
                r[...] = jnp.zeros_like(r)

        dout = dr_ref[...] + dh * (1.0 + sc_ref[...])
        dscale_ref[...] += _colsum(dh * xn_ref[...])
        dshift_ref[...] += _colsum(dh)
        yv = y_ref[...]
        pre = DEEPNORM_ALPHA * x_ref[...] + (1.0 + gate_ref[...]) * yv
        xhat, rstd = _ln_stats(pre)
        dpre = _ln_bwd_rows(dout, xhat, rstd, g_ref[...])
        dy_ref[...] = (dpre * (1.0 + gate_ref[...])).astype(BF16)
        dres_ref[...] = DEEPNORM_ALPHA * dpre
        dg_ref[...] += _colsum(dout * xhat)
        db_ref[...] += _colsum(dout)
        dgate_ref[...] += _colsum(dpre * yv)

    return _matmul_rows(dproj, w_in, epilogue, [x, y, dres_next, x_next], [gate, g, b, scale_next], [BF16, F32], [d] * 5,
                        trans_b=True, tk=1280, name=name)


def _dh_input_bwd(dproj, w_in_stacked, x, dres, scale, *, name, after):
    d = x.shape[1]

    def epilogue(dh, first, rin, vin, rout, vout):
        (x_ref, dr_ref), (sc_ref,), (dx_ref,), (dscale_ref, dshift_ref) = rin, vin, rout, vout

        @pl.when(first)
        def _():
            for r in vout:
                r[...] = jnp.zeros_like(r)

        dx_ref[...] = dr_ref[...] + dh * (1.0 + sc_ref[...])
        dscale_ref[...] += _colsum(dh * x_ref[...])
        dshift_ref[...] += _colsum(dh)

    return _matmul_rows(dproj, w_in_stacked, epilogue, [x, dres], [scale], [F32], [d, d], trans_b=True, b_stacked=True,
                        tm=1024, after=after, name=name)


def _chunk_mask(transposed=False):
    r = lax.broadcasted_iota(jnp.int32, (GMLP_BLOCK, GMLP_BLOCK), 0) // CHUNK
    c = lax.broadcasted_iota(jnp.int32, (GMLP_BLOCK, GMLP_BLOCK), 1) // CHUNK
    return (r <= c) if transposed else (c <= r)


def _window_sum(ext, steps, forward):
    rows = ext.shape[0]
    acc = ext
    for k in range(steps):
        shift = 1 << k
        acc = acc + pltpu.roll(acc, (rows - shift) if forward else shift, 0)
    return acc


def _pool_counts(first_row, rows, win):
    t = first_row + lax.broadcasted_iota(jnp.int32, (rows, 1), 0)
    return jnp.minimum(t + 1, win).astype(F32)


def _even_specs(t):
    col = lambda j: pl.BlockSpec((t, D_MODEL), lambda n: (n, j))
    per = t // POOL_HALO
    prev = pl.BlockSpec((POOL_HALO, D_MODEL), lambda n: (jnp.maximum(n * per - 1, 0), 3))
    return col, per, prev


def _full(shape):
    return pl.BlockSpec(shape, lambda n: (0,) * len(shape))


def _gmlp_head(v_h, ng, nb, w_bf):
    xhat, rstd = _ln_stats(v_h)
    vn = (xhat * ng + nb).astype(BF16)
    return xhat, rstd, vn, _dot(w_bf, vn, NN)


def _pool_group(xb_g, prev_g, first_row, grp):
    t = xb_g.shape[0]
    ext = jnp.concatenate([prev_g, xb_g], axis=0)
    tot = _window_sum(ext, grp + 1, False)[POOL_HALO:, :]
    cnt = _pool_counts(first_row, t, POOL_WINDOWS[grp])
    return tot / cnt - xb_g, cnt


def _even_fwd(proj, ws, bs_t, ng, nb, pool_w, pool_b, pool_scale, *, name):
    s = proj.shape[0]
    t = GMLP_BLOCK
    col, per, prev = _even_specs(t)

    def body(u_ref, v_ref, za_ref, xb_ref, zb_ref, xp_ref, ws_ref, bs_ref, ng_ref, nb_ref, pw_ref, pb_ref, ps_ref, o_ref):
        n = pl.program_id(0)
        mask = _chunk_mask()
        for h in range(GMLP_HEADS):
            c0 = h * GMLP_HEAD_DIM
            cs = slice(c0, c0 + GMLP_HEAD_DIM)
            w_bf = jnp.where(mask, ws_ref[h], 0.0).astype(BF16)
            _, _, _, sv = _gmlp_head(v_ref[:, cs].astype(F32),ng_ref[...], nb_ref[...], w_bf)
            sv = sv + bs_ref[:, h:h + 1]
            za = za_ref[:, cs].astype(F32)
            o_ref[:, cs] = (u_ref[:, cs].astype(F32) * sv * (za * _sigmoid(za))).astype(BF16)
        live = (n > 0).astype(F32)
        for grp in range(POOL_GROUPS):
            c0 = grp * POOL_GROUP_DIM
            cs = slice(c0, c0 + POOL_GROUP_DIM)
            pooled, _ = _pool_group(xb_ref[:, cs].astype(F32), xp_ref[:, cs].astype(F32) * live, n * t, grp)
            yb = _dot(pooled.astype(BF16), pw_ref[grp], NN) + pb_ref[:, cs]
            zb = zb_ref[:, cs].astype(F32)
            o_ref[:, D_MODEL + c0:D_MODEL + c0 + POOL_GROUP_DIM] = (yb * ps_ref[:, cs] * (zb * _sigmoid(zb))).astype(BF16)

    return pl.pallas_call(
        body, name=name, grid=(s // t,),
        in_specs=[col(0), col(1), col(2), col(3), col(4), prev,
                  _full((GMLP_HEADS, t, t)), _full((t, LANES)), _full((1, GMLP_HEAD_DIM)), _full((1, GMLP_HEAD_DIM)),
                  _full((POOL_GROUPS, POOL_GROUP_DIM, POOL_GROUP_DIM)), _full((1, D_MODEL)), _full((1, D_MODEL))],
        out_specs=pl.BlockSpec((t, 2 * D_MODEL), lambda n: (n, 0)),
        out_shape=jax.ShapeDtypeStruct((s, 2 * D_MODEL), BF16),
        compiler_params=_params("parallel"),
    )(proj, proj, proj, proj, proj, proj, ws, bs_t, ng, nb, pool_w, pool_b, pool_scale)


def _even_bwd(proj, dmix, ws, ws_t, bs_t, ng, nb, pool_w, pool_b, pool_scale, *, name):
    s = proj.shape[0]
    t = GMLP_BLOCK
    nblk = s // t
    col, per, prev = _even_specs(t)
    nxt = lambda j: pl.BlockSpec((POOL_HALO, D_MODEL), lambda n: (jnp.minimum((n + 1) * per, nblk * per - 1), j))

    def body(u_ref, v_ref, za_ref, xb_ref, zb_ref, xp_ref, zn_ref, da_ref, db_ref, dbn_ref,
             ws_ref, wst_ref, bs_ref, ng_ref, nb_ref, pw_ref, pb_ref, ps_ref,
             dp_ref, gws_ref, gbs_ref, gng_ref, gnb_ref, gpw_ref, gpb_ref, gps_ref):
        n = pl.program_id(0)

        @pl.when(n == 0)
        def _():
            for r in (gws_ref, gbs_ref, gng_ref, gnb_ref, gpw_ref, gpb_ref, gps_ref):
                r[...] = jnp.zeros_like(r)

        mask, mask_t = _chunk_mask(), _chunk_mask(True)
        lane = lax.broadcasted_iota(jnp.int32, (t, LANES), 1)
        ngv, nbv = ng_ref[...], nb_ref[...]
        for h in range(GMLP_HEADS):
            c0 = h * GMLP_HEAD_DIM
            cs = slice(c0, c0 + GMLP_HEAD_DIM)
            w_bf = jnp.where(mask, ws_ref[h], 0.0).astype(BF16)
            wt_bf = jnp.where(mask_t, wst_ref[h], 0.0).astype(BF16)
            xhat, rstd, vn, sv = _gmlp_head(v_ref[:, cs].astype(F32),ngv, nbv, w_bf)
            sv = sv + bs_ref[:, h:h + 1]
            za, u, da = za_ref[:, cs].astype(F32), u_ref[:, cs].astype(F32), da_ref[:, cs].astype(F32)
            sg = _sigmoid(za)
            sl = za * sg
            dp_ref[:, cs] = (da * sv * sl).astype(BF16)
            dp_ref[:, 2 * D_MODEL + c0:2 * D_MODEL + c0 + GMLP_HEAD_DIM] = (
                da * u * sv * (sg * (1.0 + za * (1.0 - sg)))).astype(BF16)
            dsv = da * u * sl
            gbs_ref[...] += jnp.where(lane == h, jnp.sum(dsv, axis=1, keepdims=True), 0.0)
            dsv_bf = dsv.astype(BF16)
            gws_ref[h] += jnp.where(mask, _dot(dsv_bf, vn, NT), 0.0)
            dvn = _dot(wt_bf, dsv_bf, NN)
            dp_ref[:, D_MODEL + c0:D_MODEL + c0 + GMLP_HEAD_DIM] = _ln_bwd_rows(dvn, xhat, rstd, ngv).astype(BF16)
            gng_ref[...] += _colsum(dvn * xhat)
            gnb_ref[...] += _colsum(dvn)
        live_prev = (n > 0).astype(F32)
        live_next = (n < nblk - 1).astype(F32)
        for grp in range(POOL_GROUPS):
            c0 = grp * POOL_GROUP_DIM
            cs = slice(c0, c0 + POOL_GROUP_DIM)
            xb = xb_ref[:, cs].astype(F32)
            pooled, cnt = _pool_group(xb, xp_ref[:, cs].astype(F32) * live_prev, n * t, grp)
            pooled_bf = pooled.astype(BF16)
            pw = pw_ref[grp]
            yb = _dot(pooled_bf, pw, NN) + pb_ref[:, cs]
            ps = ps_ref[:, cs]
            zb, db = zb_ref[:, cs].astype(F32), db_ref[:, cs].astype(F32)
            sg = _sigmoid(zb)
            sl = zb * sg
            dp_ref[:, 4 * D_MODEL + c0:4 * D_MODEL + c0 + POOL_GROUP_DIM] = (
                db * yb * ps * (sg * (1.0 + zb * (1.0 - sg)))).astype(BF16)
            dsl = db * sl
            dy = dsl * ps
            gps_ref[:, cs] += _colsum(dsl * yb)
            gpb_ref[:, cs] += _colsum(dy)
            dy_bf = dy.astype(BF16)
            gpw_ref[grp] += _dot(pooled_bf, dy_bf, TN)
            r = _dot(dy_bf, pw, NT)
            zn = zn_ref[:, cs].astype(F32)
            dyn = (dbn_ref[:, cs].astype(F32) * (zn * _sigmoid(zn)) * ps * live_next).astype(BF16)
            rn = _dot(dyn, pw, NT) / _pool_counts((n + 1) * t, POOL_HALO, POOL_WINDOWS[grp])
            ext = jnp.concatenate([r / cnt, rn], axis=0)
            dxb = _window_sum(ext, grp + 1, True)[:t, :] - r
            dp_ref[:, 3 * D_MODEL + c0:3 * D_MODEL + c0 + POOL_GROUP_DIM] = dxb.astype(BF16)

    out_shape = [
        jax.ShapeDtypeStruct((s, EVEN_IN), BF16),
        jax.ShapeDtypeStruct((GMLP_HEADS, t, t), F32), jax.ShapeDtypeStruct((t, LANES), F32),
        jax.ShapeDtypeStruct((1, GMLP_HEAD_DIM), F32), jax.ShapeDtypeStruct((1, GMLP_HEAD_DIM), F32),
        jax.ShapeDtypeStruct((POOL_GROUPS, POOL_GROUP_DIM, POOL_GROUP_DIM), F32),
        jax.ShapeDtypeStruct((1, D_MODEL), F32), jax.ShapeDtypeStruct((1, D_MODEL), F32),
    ]
    return pl.pallas_call(
        body, name=name, grid=(nblk,),
        in_specs=[col(0), col(1), col(2), col(3), col(4), prev, nxt(4),
                  pl.BlockSpec((t, D_MODEL), lambda n: (n, 0)), pl.BlockSpec((t, D_MODEL), lambda n: (n, 1)), nxt(1),
                  _full((GMLP_HEADS, t, t)), _full((GMLP_HEADS, t, t)), _full((t, LANES)),
                  _full((1, GMLP_HEAD_DIM)), _full((1, GMLP_HEAD_DIM)),
                  _full((POOL_GROUPS, POOL_GROUP_DIM, POOL_GROUP_DIM)), _full((1, D_MODEL)), _full((1, D_MODEL))],
        out_specs=[pl.BlockSpec((t, EVEN_IN), lambda n: (n, 0))] + [_full(o.shape) for o in out_shape[1:]],
        out_shape=out_shape,
        compiler_params=_params("arbitrary"),
    )(proj, proj, proj, proj, proj, proj, proj, dmix, dmix, dmix, ws, ws_t, bs_t, ng, nb, pool_w, pool_b, pool_scale)


ROPE_HALF = MLA_ROPE // 2


def _rope(v, cos, sin_signed):
    return v * cos + pltpu.roll(v, 2 * ROPE_HALF, 1) * sin_signed


def _rope_bwd(d, cos, sin_signed):
    return d * cos + pltpu.roll(d * sin_signed, 2 * ROPE_HALF, 1)


def _slab_lanes(shape, which):
    lane = lax.broadcasted_iota(jnp.int32, shape, 1)
    return (lane // ROPE_HALF) % 2 == which


def _rms(v, g):
    r = lax.rsqrt(jnp.mean(v * v, axis=-1, keepdims=True) + LN_EPS)
    return v * r * g, r


def _rms_bwd(dy, v, r, g):
    u = dy * g
    return r * u - v * (r * r * r) * jnp.mean(u * v, axis=-1, keepdims=True)


def _mla_prep(proj, gq, gkv, cos, sin_signed, *, name):
    s = proj.shape[0]
    ts = _tile(s, 512)

    def body(p_ref, gq_ref, gkv_ref, c_ref, s_ref, q_ref, k_ref):
        qcn, _ = _rms(p_ref[:, :MLA_Q_RANK].astype(F32), gq_ref[...])
        kvn, _ = _rms(p_ref[:, MLA_Q_RANK:MLA_Q_RANK + MLA_KV_RANK].astype(F32), gkv_ref[...])
        kr = p_ref[:, MLA_Q_RANK + MLA_KV_RANK:].astype(F32)
        lane = lax.broadcasted_iota(jnp.int32, kr.shape, 1)
        by1, by2 = pltpu.roll(kr, ROPE_HALF, 1), pltpu.roll(kr, 2 * ROPE_HALF, 1)
        both = jnp.where(lane < ROPE_HALF, kr, jnp.where(lane < 3 * ROPE_HALF, by1, by2))
        kr = _rope(both, c_ref[...], s_ref[...])
        q_ref[...] = qcn.astype(BF16)
        k_ref[...] = jnp.concatenate([kvn, kr], axis=1).astype(BF16)

    return pl.pallas_call(
        body, name=name, grid=(s // ts,),
        in_specs=[_row_spec(ts, ODD_SMALL_PAD), _vec_spec(MLA_Q_RANK), _vec_spec(MLA_KV_RANK), _row_spec(ts, LANES), _row_spec(ts, LANES)],
        out_specs=[_row_spec(ts, MLA_Q_RANK), _row_spec(ts, QK_PAD)],
        out_shape=[jax.ShapeDtypeStruct((s, MLA_Q_RANK), BF16), jax.ShapeDtypeStruct((s, QK_PAD), BF16)],
        compiler_params=_params("parallel"),
    )(proj, gq, gkv, cos, sin_signed)


def _mla_prep_bwd(proj, dqcn, dkv, gq, gkv, cos, sin_signed, *, name):
    s = proj.shape[0]
    ts = _tile(s, 512)

    def body(p_ref, dq_ref, dkv_ref, gq_ref, gkv_ref, c_ref, s_ref, ds_ref, ggq_ref, ggkv_ref):
        @pl.when(pl.program_id(0) == 0)
        def _():
            ggq_ref[...] = jnp.zeros_like(ggq_ref)
            ggkv_ref[...] = jnp.zeros_like(ggkv_ref)

        qc = p_ref[:, :MLA_Q_RANK].astype(F32)
        kvc = p_ref[:, MLA_Q_RANK:MLA_Q_RANK + MLA_KV_RANK].astype(F32)
        _, rq = _rms(qc, gq_ref[...])
        _, rkv = _rms(kvc, gkv_ref[...])
        dq = dq_ref[...]
        dkvn = dkv_ref[:, :MLA_KV_RANK]
        ggq_ref[...] += _colsum(dq * qc * rq)
        ggkv_ref[...] += _colsum(dkvn * kvc * rkv)
        dboth = _rope_bwd(dkv_ref[:, MLA_KV_RANK:], c_ref[...], s_ref[...])
        lane = lax.broadcasted_iota(jnp.int32, dboth.shape, 1)
        pair = dboth + pltpu.roll(dboth, 3 * ROPE_HALF, 1)
        dkr = jnp.where(lane < ROPE_HALF, pair, jnp.where(lane < 2 * ROPE_HALF, pltpu.roll(pair, 3 * ROPE_HALF, 1), 0.0))
        ds_ref[...] = jnp.concatenate(
            [_rms_bwd(dq, qc, rq, gq_ref[...]), _rms_bwd(dkvn, kvc, rkv, gkv_ref[...]), dkr], axis=1).astype(BF16)

    return pl.pallas_call(
        body, name=name, grid=(s // ts,),
        in_specs=[_row_spec(ts, ODD_SMALL_PAD), _row_spec(ts, MLA_Q_RANK), _row_spec(ts, QK_PAD),
                  _vec_spec(MLA_Q_RANK), _vec_spec(MLA_KV_RANK), _row_spec(ts, LANES), _row_spec(ts, LANES)],
        out_specs=[_row_spec(ts, ODD_SMALL_PAD), _vec_spec(MLA_Q_RANK), _vec_spec(MLA_KV_RANK)],
        out_shape=[jax.ShapeDtypeStruct((s, ODD_SMALL_PAD), BF16), jax.ShapeDtypeStruct((1, MLA_Q_RANK), F32),
                   jax.ShapeDtypeStruct((1, MLA_KV_RANK), F32)],
        compiler_params=_params("arbitrary"),
    )(proj, dqcn, dkv, gq, gkv, cos, sin_signed)


LOG2_E = 1.4426950408889634
Q_PRESCALE = ATTN_SCALE * LOG2_E


def _q_build(q_nope, q_rope_pre, wuk_hdr, cos, sin_signed, *, name):
    s = q_nope.shape[0]
    ts = _tile(s, 1024)

    def body(qn_ref, qr_ref, w_ref, c_ref, s_ref, o_ref):
        r = _rope(qr_ref[...], c_ref[...], s_ref[...])
        for j in range(2):
            ql = _dot(qn_ref[:, j * MLA_NOPE:(j + 1) * MLA_NOPE], w_ref[j], NN)
            mine = jnp.where(_slab_lanes(r.shape, j), r, 0.0)
            o_ref[j] = (jnp.concatenate([ql, mine], axis=1) * Q_PRESCALE).astype(BF16)

    return pl.pallas_call(
        body, name=name, grid=(s // ts, MLA_HEADS // 2),
        in_specs=[pl.BlockSpec((ts, 2 * MLA_NOPE), lambda i, p: (i, p)), pl.BlockSpec((ts, LANES), lambda i, p: (i, p)),
                  pl.BlockSpec((2, MLA_NOPE, MLA_KV_RANK), lambda i, p: (p, 0, 0)),
                  pl.BlockSpec((ts, LANES), lambda i, p: (i, 0)), pl.BlockSpec((ts, LANES), lambda i, p: (i, 0))],
        out_specs=pl.BlockSpec((2, ts, QK_PAD), lambda i, p: (p, i, 0)),
        out_shape=jax.ShapeDtypeStruct((MLA_HEADS, s, QK_PAD), BF16),
        compiler_params=_params("parallel", "parallel"),
    )(q_nope, q_rope_pre, wuk_hdr, cos, sin_signed)


def _q_bwd(dq, q_nope, wuk_hrd, cos, sin_signed, *, name):
    s = q_nope.shape[0]
    ts = _tile(s, 1024)

    def body(dq_ref, qn_ref, w_ref, c_ref, s_ref, dn_ref, dr_ref, gw_ref):
        @pl.when(pl.program_id(1) == 0)
        def _():
            gw_ref[...] = jnp.zeros_like(gw_ref)

        for j in range(2):
            dql = dq_ref[j, :, :MLA_KV_RANK]
            dn_ref[:, j * MLA_NOPE:(j + 1) * MLA_NOPE] = _dot(dql, w_ref[j], NN).astype(BF16)
            gw_ref[j] += _dot(dql, qn_ref[:, j * MLA_NOPE:(j + 1) * MLA_NOPE], TN)
        hi0 = dq_ref[0, :, MLA_KV_RANK:].astype(F32)
        hi1 = dq_ref[1, :, MLA_KV_RANK:].astype(F32)
        d = jnp.where(_slab_lanes(hi0.shape, 0), hi0, hi1)
        dr_ref[...] = _rope_bwd(d, c_ref[...], s_ref[...]).astype(BF16)

    return pl.pallas_call(
        body, name=name, grid=(MLA_HEADS // 2, s // ts),
        in_specs=[pl.BlockSpec((2, ts, QK_PAD), lambda p, i: (p, i, 0)), pl.BlockSpec((ts, 2 * MLA_NOPE), lambda p, i: (i, p)),
                  pl.BlockSpec((2, MLA_KV_RANK, MLA_NOPE), lambda p, i: (p, 0, 0)),
                  pl.BlockSpec((ts, LANES), lambda p, i: (i, 0)), pl.BlockSpec((ts, LANES), lambda p, i: (i, 0))],
        out_specs=[pl.BlockSpec((ts, 2 * MLA_NOPE), lambda p, i: (i, p)), pl.BlockSpec((ts, LANES), lambda p, i: (i, p)),
                   pl.BlockSpec((2, MLA_KV_RANK, MLA_NOPE), lambda p, i: (p, 0, 0))],
        out_shape=[jax.ShapeDtypeStruct((s, MLA_HEADS * MLA_NOPE), BF16), jax.ShapeDtypeStruct((s, MLA_HEADS * MLA_ROPE), BF16),
                   jax.ShapeDtypeStruct((MLA_HEADS, MLA_KV_RANK, MLA_NOPE), F32)],
        compiler_params=_params("parallel", "arbitrary"),
    )(dq, q_nope, wuk_hrd, cos, sin_signed)


ATTN_BQ = 128
ATTN_BK = 512


def _diag_mask(rows, bq, bk, q0, k0):
    qc = (q0 + lax.broadcasted_iota(jnp.int32, (rows, bk), 0) % bq) // CHUNK
    kc = (k0 + lax.broadcasted_iota(jnp.int32, (rows, bk), 1)) // CHUNK
    return kc <= qc


def _attn_fwd(q, k, *, name):
    nh, s, dk = q.shape
    bq, bk = _tile(s, ATTN_BQ), _tile(s, ATTN_BK)
    rows = nh * bq

    def body(q_ref, k_ref, o_ref, lse_ref):
        i = pl.program_id(0)
        qb = q_ref[...].reshape(rows, dk)
        n_before = (i * bq) // bk

        def step(j, width, carry, masked):
            m, l, acc = carry
            k0 = pl.multiple_of(j * bk, bk)
            kb = k_ref[pl.ds(k0, width), :]
            sc = _dot(qb, kb, NT)
            if masked:
                sc = jnp.where(_diag_mask(rows, bq, width, i * bq, k0), sc, NEG)
            m_new = jnp.maximum(m, jnp.max(sc, axis=1, keepdims=True))
            p = jnp.exp2(sc - m_new)
            a = jnp.exp2(m - m_new)
            l = a * l + jnp.sum(p, axis=1, keepdims=True)
            acc = a * acc + _dot(p.astype(BF16), kb[:, :MLA_KV_RANK], NN)
            return m_new, l, acc

        init = (jnp.full((rows, 1), NEG, F32), jnp.zeros((rows, 1), F32), jnp.zeros((rows, MLA_KV_RANK), F32))
        carry = lax.fori_loop(0, n_before, lambda j, c: step(j, bk, c, False), init)
        for part in range(bk // bq):
            @pl.when(i % (bk // bq) == part)
            def _(part=part):
                m, l, acc = step(n_before, (part + 1) * bq, carry, True)
                o_ref[...] = (acc / l).astype(BF16).reshape(nh, bq, MLA_KV_RANK)
                lse_ref[...] = jnp.broadcast_to(m + jnp.log2(l), (rows, LANES)).reshape(nh, bq, LANES)

    return pl.pallas_call(
        body, name=name, grid=(s // bq,),
        in_specs=[pl.BlockSpec((nh, bq, dk), lambda i: (0, i, 0)), pl.BlockSpec((s, dk), lambda i: (0, 0))],
        out_specs=[pl.BlockSpec((nh, bq, MLA_KV_RANK), lambda i: (0, i, 0)), pl.BlockSpec((nh, bq, LANES), lambda i: (0, i, 0))],
        out_shape=[jax.ShapeDtypeStruct((nh, s, MLA_KV_RANK), BF16), jax.ShapeDtypeStruct((nh, s, LANES), F32)],
        compiler_params=_params("parallel"),
    )(q, k)


def _attn_bwd(q, k, do, o, lse, *, name):
    nh, s, dk = q.shape
    bq, bk = _tile(s, ATTN_BQ), _tile(s, ATTN_BK)
    rows = nh * bq

    def body(q_ref, k_ref, do_ref, o_ref, lse_ref, dq_ref, dkv_ref):
        i = pl.program_id(0)
        n_before = (i * bq) // bk

        @pl.when(i == 0)
        def _():
            dkv_ref[...] = jnp.zeros_like(dkv_ref)

        qb = q_ref[...].reshape(rows, dk)
        dob = do_ref[...].reshape(rows, MLA_KV_RANK)
        lse_b = lse_ref[...].reshape(rows, LANES)[:, :1]
        delta = jnp.sum(dob.astype(F32) * o_ref[...].reshape(rows, MLA_KV_RANK).astype(F32), axis=1, keepdims=True)

        def step(j, width, dq, masked):
            j0 = pl.multiple_of(j * bk, bk)
            kb = k_ref[pl.ds(j0, width), :]
            sc = _dot(qb, kb, NT)
            if masked:
                sc = jnp.where(_diag_mask(rows, bq, width, i * bq, j0), sc, NEG)
            p = jnp.exp2(sc - lse_b)
            dp = _dot(dob, kb[:, :MLA_KV_RANK], NT)
            ds_bf = (p * (dp - delta)).astype(BF16)
            dkv_ref[pl.ds(j0, width), :] += _dot(ds_bf, qb, TN) * (1.0 / LOG2_E)
            dkv_ref[pl.ds(j0, width), :MLA_KV_RANK] += _dot(p.astype(BF16), dob, TN)
            return dq + _dot(ds_bf, kb, NN)

        dq_before = lax.fori_loop(0, n_before, lambda j, c: step(j, bk, c, False), jnp.zeros((rows, dk), F32))
        for part in range(bk // bq):
            @pl.when(i % (bk // bq) == part)
            def _(part=part):
                dq = step(n_before, (part + 1) * bq, dq_before, True) * ATTN_SCALE
                dq_ref[...] = dq.astype(BF16).reshape(nh, bq, dk)

    blk = lambda w: pl.BlockSpec((nh, bq, w), lambda i: (0, i, 0))
    return pl.pallas_call(
        body, name=name, grid=(s // bq,),
        in_specs=[blk(dk), pl.BlockSpec((s, dk), lambda i: (0, 0)), blk(MLA_KV_RANK), blk(MLA_KV_RANK), blk(LANES)],
        out_specs=[blk(dk), pl.BlockSpec((s, dk), lambda i: (0, 0))],
        out_shape=[jax.ShapeDtypeStruct((nh, s, dk), BF16), jax.ShapeDtypeStruct((s, dk), F32)],
        compiler_params=_params("arbitrary"),
    )(q, k, do, o, lse)


HEAD_GROUP = 4


def _o_build(o_lat, wuv_hrv, proj, *, name):
    s = proj.shape[0]
    ts = _tile(s, 1024)
    w = HEAD_GROUP * MLA_V

    def body(ol_ref, w_ref, z_ref, og_ref):
        for j in range(HEAD_GROUP):
            cs = slice(j * MLA_V, (j + 1) * MLA_V)
            z = z_ref[:, cs].astype(F32)
            og_ref[:, cs] = (_dot(ol_ref[j], w_ref[j], NN) * (z * _sigmoid(z))).astype(BF16)

    return pl.pallas_call(
        body, name=name, grid=(s // ts, MLA_HEADS // HEAD_GROUP),
        in_specs=[pl.BlockSpec((HEAD_GROUP, ts, MLA_KV_RANK), lambda i, g: (g, i, 0)),
                  pl.BlockSpec((HEAD_GROUP, MLA_KV_RANK, MLA_V), lambda i, g: (g, 0, 0)),
                  pl.BlockSpec((ts, w), lambda i, g: (i, g + 1))],
        out_specs=pl.BlockSpec((ts, w), lambda i, g: (i, g)),
        out_shape=jax.ShapeDtypeStruct((s, MLA_WIDTH), BF16),
        compiler_params=_params("parallel", "parallel"),
    )(o_lat, wuv_hrv, proj)


def _o_bwd(dg, proj, o_lat, wuv_hrv, wuv_hvr, *, name):
    s = proj.shape[0]
    ts = _tile(s, 1024)
    w = HEAD_GROUP * MLA_V

    def body(dg_ref, z_ref, ol_ref, w_ref, wt_ref, dol_ref, dz_ref, gw_ref):
        @pl.when(pl.program_id(1) == 0)
        def _():
            gw_ref[...] = jnp.zeros_like(gw_ref)

        for j in range(HEAD_GROUP):
            cs = slice(j * MLA_V, (j + 1) * MLA_V)
            z, dgj, ol = z_ref[:, cs].astype(F32), dg_ref[:, cs].astype(F32), ol_ref[j]
            sg = _sigmoid(z)
            o = _dot(ol, w_ref[j], NN)
            dz_ref[:, cs] = (dgj * o * (sg * (1.0 + z * (1.0 - sg)))).astype(BF16)
            do_bf = (dgj * (z * sg)).astype(BF16)
            dol_ref[j] = _dot(do_bf, wt_ref[j], NN).astype(BF16)
            gw_ref[j] += _dot(ol, do_bf, TN)

    hs = lambda a, b: pl.BlockSpec((HEAD_GROUP, a, b), lambda g, i: (g, 0, 0))
    return pl.pallas_call(
        body, name=name, grid=(MLA_HEADS // HEAD_GROUP, s // ts),
        in_specs=[pl.BlockSpec((ts, w), lambda g, i: (i, g)), pl.BlockSpec((ts, w), lambda g, i: (i, g + 1)),
                  pl.BlockSpec((HEAD_GROUP, ts, MLA_KV_RANK), lambda g, i: (g, i, 0)),
                  hs(MLA_KV_RANK, MLA_V), hs(MLA_V, MLA_KV_RANK)],
        out_specs=[pl.BlockSpec((HEAD_GROUP, ts, MLA_KV_RANK), lambda g, i: (g, i, 0)),
                   pl.BlockSpec((ts, w), lambda g, i: (i, g)), hs(MLA_KV_RANK, MLA_V)],
        out_shape=[jax.ShapeDtypeStruct((MLA_HEADS, s, MLA_KV_RANK), BF16), jax.ShapeDtypeStruct((s, MLA_WIDTH), BF16),
                   jax.ShapeDtypeStruct((MLA_HEADS, MLA_KV_RANK, MLA_V), F32)],
        compiler_params=_params("parallel", "arbitrary"),
    )(dg, proj, o_lat, wuv_hrv, wuv_hvr)


def _ada_mod(c_all, ada_w, ada_b_sh, *, name):
    nl, _, cols = ada_w.shape

    def body(c_ref, w_ref, b_ref, o_ref):
        c = c_ref[...]
        cond = (c * _sigmoid(c)).astype(BF16)
        for l in range(nl):
            o_ref[l] = _dot(cond, w_ref[l].astype(BF16), NN) + b_ref[l]

    return pl.pallas_call(
        body, name=name, out_shape=jax.ShapeDtypeStruct((nl, c_all.shape[0], cols), F32),
        compiler_params=_params(),
    )(c_all, ada_w, ada_b_sh)


def _ada_grad(c_all_t, dmod_sh, *, name):
    nl, _, cols = dmod_sh.shape
    d = c_all_t.shape[0]

    def body(c_ref, dm_ref, gw_ref):
        c = c_ref[...]
        cond_t = c * _sigmoid(c)
        for l in range(nl):
            gw_ref[l] = lax.dot_general(cond_t, dm_ref[l], (NN, ((), ())), precision=lax.Precision.HIGHEST,
                                        preferred_element_type=F32)

    return pl.pallas_call(
        body, name=name, out_shape=jax.ShapeDtypeStruct((nl, d, cols), F32), compiler_params=_params(),
    )(c_all_t, dmod_sh)


def _sum_devices(parts, *, name):
    def body(p_ref, o_ref):
        acc = p_ref[0]
        for k in range(1, parts.shape[0]):
            acc = acc + p_ref[k]
        o_ref[...] = acc

    return pl.pallas_call(body, name=name, out_shape=jax.ShapeDtypeStruct(parts.shape[1:], F32), compiler_params=_params())(parts)


def _adamw_math(w, g, m, v):
    c1 = 1.0 - ADAM_B1 ** ADAM_STEP
    c2 = 1.0 - ADAM_B2 ** ADAM_STEP
    nm = ADAM_B1 * m + (1.0 - ADAM_B1) * g
    nv = ADAM_B2 * v + (1.0 - ADAM_B2) * (g * g)
    return -ADAM_LR * ((nm / c1) / (jnp.sqrt(nv / c2) + ADAM_EPS) + ADAM_WD * w), nm, nv


ADAMW_BLOCK_BYTES = 1 << 20


def _adamw(w, g, m, v, *, name, after=None):
    shape = w.shape
    a, b = shape[-2], shape[-1]
    lead = 1
    for dim in shape[:-2]:
        lead *= dim
    row_bytes = 4 * b
    if a * row_bytes <= ADAMW_BLOCK_BYTES:
        ta = a
        tl = max(1, min(lead, ADAMW_BLOCK_BYTES // (a * row_bytes)))
        while lead % tl:
            tl -= 1
    else:
        tl = 1
        ta = _tile(a, 256)
    to3 = lambda t: t.reshape(lead, a, b)

    def body(w_ref, g_ref, m_ref, v_ref, *rest):
        d_ref, nm_ref, nv_ref = rest[-3:]
        d_ref[...], nm_ref[...], nv_ref[...] = _adamw_math(w_ref[...], g_ref[...], m_ref[...], v_ref[...])

    spec = pl.BlockSpec((tl, ta, b), lambda i, j: (i, j, 0))
    out = jax.ShapeDtypeStruct((lead, a, b), F32)
    order = [] if after is None else [after]
    res = pl.pallas_call(
        body, name=name, grid=(lead // tl, a // ta), in_specs=[spec] * 4 + [pl.BlockSpec(memory_space=pl.ANY)] * len(order),
        out_specs=[spec] * 3, out_shape=[out] * 3, compiler_params=_params("parallel", "parallel"),
    )(to3(w), to3(g), to3(m), to3(v), *order)
    return [r.reshape(shape) for r in res]


def _adamw_small(ws, gs, ms, vs, *, name):
    n = len(ws)

    def body(*refs):
        for k in range(n):
            w_ref, g_ref, m_ref, v_ref = (refs[j * n + k] for j in range(4))
            d_ref, nm_ref, nv_ref = (refs[(4 + j) * n + k] for j in range(3))
            d_ref[...], nm_ref[...], nv_ref[...] = _adamw_math(w_ref[...], g_ref[...], m_ref[...], v_ref[...])

    outs = [jax.ShapeDtypeStruct(w.shape, F32) for w in ws]
    res = pl.pallas_call(body, name=name, out_shape=outs * 3, compiler_params=_params())(*ws, *gs, *ms, *vs)
    return res[:n], res[n:2 * n], res[2 * n:]


def _flip(v, bit):
    return 1 - v if bit else v


CHIP_DELTAS = ((1, 0), (0, 1), (1, 1))
SUM_ROWS = 32


def _all_gather_chips(shard, *, name):
    def body(x_ref, o_ref, send_sems, recv_sems, local_sem):
        x, y, c = lax.axis_index("x"), lax.axis_index("y"), lax.axis_index("c")
        mine = pltpu.make_async_copy(x_ref, o_ref.at[2 * x + y], local_sem)
        mine.start()

        def copy(k):
            tx, ty = _flip(x, CHIP_DELTAS[k][0]), _flip(y, CHIP_DELTAS[k][1])
            send = pltpu.make_async_remote_copy(src_ref=x_ref, dst_ref=o_ref.at[2 * x + y], send_sem=send_sems.at[k],
                                                recv_sem=recv_sems.at[k], device_id=(tx, ty, c), device_id_type=MESH)
            recv = pltpu.make_async_remote_copy(src_ref=x_ref, dst_ref=o_ref.at[2 * tx + ty], send_sem=send_sems.at[k],
                                                recv_sem=recv_sems.at[k], device_id=(tx, ty, c), device_id_type=MESH)
            return send, recv

        pairs = [copy(k) for k in range(3)]
        for send, _ in pairs:
            send.start()
        for _, recv in pairs:
            recv.wait_recv()
        for send, _ in pairs:
            send.wait_send()
        mine.wait()

    return pl.pallas_call(
        body, name=name, out_shape=jax.ShapeDtypeStruct((N_CHIPS,) + shard.shape, shard.dtype),
        in_specs=[HBM], out_specs=HBM,
        scratch_shapes=[pltpu.SemaphoreType.DMA((3,)), pltpu.SemaphoreType.DMA((3,)), pltpu.SemaphoreType.DMA(())],
    )(shard)


def _gather_weights(shards, *, name):
    n = len(shards)

    def body(*refs):
        w_refs, o_refs = refs[:n], refs[n:2 * n]
        ici_send, ici_recv, d2d_send, d2d_recv, local_sems = refs[2 * n:]
        x, y, c = lax.axis_index("x"), lax.axis_index("y"), lax.axis_index("c")
        me = 2 * x + y
        peers = [(_flip(x, dx), _flip(y, dy)) for dx, dy in CHIP_DELTAS]
        locals_ = [pltpu.make_async_copy(w_refs[k], o_refs[k].at[me], local_sems.at[k]) for k in range(n)]
        for cp in locals_:
            cp.start()

        def rows(k, which):
            half = shards[k].shape[0] // 2
            return pl.ds(pl.multiple_of(which * half, half), half)

        def over_chips(k, d, slot):
            tx, ty = peers[d]
            return pltpu.make_async_remote_copy(
                src_ref=w_refs[k].at[rows(k, c)], dst_ref=o_refs[k].at[slot, rows(k, c)], send_sem=ici_send.at[k, d],
                recv_sem=ici_recv.at[k, d], device_id=(tx, ty, c), device_id_type=MESH)

        def to_sibling(k, d, which):
            tx, ty = peers[d]
            at = o_refs[k].at[2 * tx + ty, rows(k, which)]
            return pltpu.make_async_remote_copy(src_ref=at, dst_ref=at, send_sem=d2d_send.at[k, d], recv_sem=d2d_recv.at[k, d],
                                                device_id=(x, y, 1 - c), device_id_type=MESH)

        sends = [over_chips(k, d, me) for k in range(n) for d in range(3)]
        for cp in sends:
            cp.start()
        passed = []
        for k in range(n):
            for d in range(3):
                over_chips(k, d, 2 * peers[d][0] + peers[d][1]).wait_recv()
                passed.append(to_sibling(k, d, c))
                passed[-1].start()
        for k in range(n):
            for d in range(3):
                to_sibling(k, d, 1 - c).wait_recv()
        for cp in sends + passed:
            cp.wait_send()
        for cp in locals_:
            cp.wait()

    return pl.pallas_call(
        body, name=name, out_shape=[jax.ShapeDtypeStruct((N_CHIPS,) + w.shape, w.dtype) for w in shards],
        in_specs=[HBM] * n, out_specs=[HBM] * n,
        scratch_shapes=[pltpu.SemaphoreType.DMA((n, 3))] * 4 + [pltpu.SemaphoreType.DMA((n,))],
    )(*shards)


def _add_into(dst_ref, src_ref):
    ns, r, _ = dst_ref.shape
    step = SUM_ROWS if r % SUM_ROWS == 0 else r
    for s in range(ns):
        def tile(t, carry):
            at = pl.ds(pl.multiple_of(t * step, step), step)
            dst_ref[s, at, :] = (dst_ref[s, at, :].astype(F32) + src_ref[s, at, :].astype(F32)).astype(dst_ref.dtype)
            return carry
        lax.fori_loop(0, r // step, tile, 0)


def _reduce_sibling(grads, *, name):
    n = len(grads)

    def body(*refs):
        g_refs, o_refs = refs[:n], refs[n:2 * n]
        mine, got = refs[2 * n:3 * n], refs[3 * n:4 * n]
        send_sems, recv_sems, load_sems, store_sems = refs[4 * n:]
        x, y, c = lax.axis_index("x"), lax.axis_index("y"), lax.axis_index("c")
        loads = [pltpu.make_async_copy(g_refs[k].at[:, c], mine[k], load_sems.at[k]) for k in range(n)]
        swaps = [pltpu.make_async_remote_copy(src_ref=g_refs[k].at[:, 1 - c], dst_ref=got[k], send_sem=send_sems.at[k],
                                              recv_sem=recv_sems.at[k], device_id=(x, y, 1 - c), device_id_type=MESH)
                 for k in range(n)]
        for cp in loads + swaps:
            cp.start()
        stores = []
        for k in range(n):
            loads[k].wait()
            swaps[k].wait_recv()
            _add_into(mine[k], got[k])
            stores.append(pltpu.make_async_copy(mine[k], o_refs[k], store_sems.at[k]))
            stores[-1].start()
        for k in range(n):
            swaps[k].wait_send()
            stores[k].wait()

    half = [jax.ShapeDtypeStruct((g.shape[0],) + g.shape[2:], g.dtype) for g in grads]
    return pl.pallas_call(
        body, name=name, out_shape=half, in_specs=[HBM] * n, out_specs=[HBM] * n,
        scratch_shapes=[pltpu.VMEM(h.shape, h.dtype) for h in half] * 2 + [pltpu.SemaphoreType.DMA((n,))] * 4,
        compiler_params=_params(),
    )(*grads)


def _reduce_chips(parts, landed, *, name):
    n_send = len(parts)
    n = n_send + len(landed)

    def body(*refs):
        p_refs, o_refs = refs[:n], refs[n:2 * n]
        got, total = refs[2 * n:3 * n], refs[3 * n:4 * n]
        send_sems, recv_sems, load_sems, share_send, share_recv, store_sems = refs[4 * n:]
        x, y, c = lax.axis_index("x"), lax.axis_index("y"), lax.axis_index("c")
        me = 2 * x + y
        peers = [(_flip(x, dx), _flip(y, dy)) for dx, dy in CHIP_DELTAS]

        def over_chips(k, d, src_slot, dst_slot):
            tx, ty = peers[d]
            return pltpu.make_async_remote_copy(
                src_ref=p_refs[k].at[src_slot], dst_ref=got[k].at[dst_slot], send_sem=send_sems.at[k, d],
                recv_sem=recv_sems.at[k, d], device_id=(tx, ty, c), device_id_type=MESH)

        loads = [pltpu.make_async_copy(p_refs[k].at[me], got[k].at[me], load_sems.at[k]) for k in range(n_send)]
        loads += [pltpu.make_async_copy(p_refs[k], got[k], load_sems.at[k]) for k in range(n_send, n)]
        sends = [over_chips(k, d, 2 * peers[d][0] + peers[d][1], me) for k in range(n_send) for d in range(3)]
        for cp in loads + sends:
            cp.start()
        shares, stores = [], []
        for k in range(n):
            loads[k].wait()
            for d in range(3 if k < n_send else 0):
                slot = 2 * peers[d][0] + peers[d][1]
                over_chips(k, d, slot, slot).wait_recv()
            r = total[k].shape[0]
            step = SUM_ROWS if r % SUM_ROWS == 0 else r

            def tile(t, carry, k=k, step=step):
                at = pl.ds(pl.multiple_of(t * step, step), step)
                acc = got[k][0, at, :].astype(F32)
                for s in range(1, N_CHIPS):
                    acc = acc + got[k][s, at, :].astype(F32)
                total[k][at, :] = acc
                return carry

            lax.fori_loop(0, r // step, tile, 0)
            stores.append(pltpu.make_async_copy(total[k], o_refs[k].at[c], store_sems.at[k]))
            shares.append(pltpu.make_async_remote_copy(
                src_ref=total[k], dst_ref=o_refs[k].at[c], send_sem=share_send.at[k], recv_sem=share_recv.at[k],
                device_id=(x, y, 1 - c), device_id_type=MESH))
            stores[-1].start()
            shares[-1].start()
        for k in range(n):
            pltpu.make_async_remote_copy(
                src_ref=total[k], dst_ref=o_refs[k].at[1 - c], send_sem=share_send.at[k], recv_sem=share_recv.at[k],
                device_id=(x, y, 1 - c), device_id_type=MESH).wait_recv()
        for cp in sends + shares:
            cp.wait_send()
        for cp in stores:
            cp.wait()

    both = list(parts) + list(landed)
    return pl.pallas_call(
        body, name=name, out_shape=[jax.ShapeDtypeStruct((2,) + p.shape[1:], F32) for p in both],
        in_specs=[HBM] * n, out_specs=[HBM] * n,
        scratch_shapes=[pltpu.VMEM(p.shape, p.dtype) for p in both] + [pltpu.VMEM(p.shape[1:], F32) for p in both]
        + [pltpu.SemaphoreType.DMA((n, 3))] * 2 + [pltpu.SemaphoreType.DMA((n,))] * 4,
        compiler_params=_params(),
    )(*both)


SEM = pl.BlockSpec(memory_space=pltpu.SEMAPHORE)
IN_FLIGHT = pltpu.SideEffectType.DATAFLOW_SIDE_EFFECTING


def _chip_copies(s_refs, l_refs, sems, scatter, theirs):
    x, y, c = lax.axis_index("x"), lax.axis_index("y"), lax.axis_index("c")
    me = 2 * x + y
    copies = []
    for k in range(len(s_refs)):
        for d, (dx, dy) in enumerate(CHIP_DELTAS):
            tx, ty = _flip(x, dx), _flip(y, dy)
            peer = 2 * tx + ty
            send_sem, recv_sem = sems[2 * (3 * k + d)], sems[2 * (3 * k + d) + 1]
            copies.append(pltpu.make_async_remote_copy(
                src_ref=s_refs[k].at[peer] if scatter else s_refs[k], dst_ref=l_refs[k].at[peer if theirs else me],
                send_sem=send_sem, recv_sem=recv_sem, device_id=(tx, ty, c), device_id_type=MESH))
    return copies


def _chips_start(srcs, lands, after, *, scatter, name):
    n = len(srcs)
    n_sem = 2 * 3 * n

    def body(*refs):
        s_refs, l_refs = refs[:n], refs[n:2 * n]
        sems = refs[2 * n + 1:2 * n + 1 + n_sem]
        token = refs[-1]
        for cp in _chip_copies(s_refs, l_refs, sems, scatter, False):
            cp.start()
        token[...] = jnp.zeros_like(token)

    hbm = lambda a: pltpu.HBM(a.shape, a.dtype)
    res = pl.pallas_call(
        body, name=name,
        out_shape=(*[pltpu.SemaphoreType.DMA(())] * n_sem, *[hbm(a) for a in srcs], *[hbm(a) for a in lands],
                   jax.ShapeDtypeStruct((8, LANES), F32)),
        in_specs=[HBM] * (2 * n) + [pl.BlockSpec(memory_space=pl.ANY)],
        out_specs=(*[SEM] * n_sem, *[HBM] * (2 * n), VMEM),
        input_output_aliases={k: n_sem + k for k in range(2 * n)},
        compiler_params=pltpu.CompilerParams(has_side_effects=IN_FLIGHT),
    )(*[pltpu.with_memory_space_constraint(a, pltpu.HBM) for a in list(srcs) + list(lands)], after)
    return res[:n_sem], res[n_sem:n_sem + n], res[n_sem + n:n_sem + 2 * n], res[-1]


def _chips_wait(sems, srcs, lands, after, *, scatter, name):
    n = len(srcs)
    n_sem = len(sems)

    def body(*refs):
        s_refs, l_refs = refs[:n], refs[n:2 * n]
        sem_refs = refs[2 * n:2 * n + n_sem]
        for cp in _chip_copies(s_refs, l_refs, sem_refs, scatter, False):
            cp.wait_send()
        for cp in _chip_copies(s_refs, l_refs, sem_refs, scatter, True):
            cp.wait_recv()

    hbm = lambda a: pltpu.HBM(a.shape, a.dtype)
    res = pl.pallas_call(
        body, name=name, out_shape=tuple(hbm(a) for a in list(srcs) + list(lands)),
        in_specs=[HBM] * (2 * n) + [SEM] * n_sem + [pl.BlockSpec(memory_space=pl.ANY)], out_specs=tuple([HBM] * (2 * n)),
        input_output_aliases={k: k for k in range(2 * n)},
        compiler_params=pltpu.CompilerParams(has_side_effects=IN_FLIGHT),
    )(*srcs, *lands, *sems, after)
    return res[n:]


def _all_gather_devices(rows, *, name, after=None):
    deltas = [(dx, dy, dc) for dx in (0, 1) for dy in (0, 1) for dc in (0, 1)][1:]
    order = [] if after is None else [after]

    def body(x_ref, *rest):
        o_ref, send_sems, recv_sems = rest[-3:]
        x, y, c = lax.axis_index("x"), lax.axis_index("y"), lax.axis_index("c")
        me = 4 * x + 2 * y + c
        o_ref[me] = x_ref[...]
        sends, recvs = [], []
        for k, (dx, dy, dc) in enumerate(deltas):
            tx, ty, tc = _flip(x, dx), _flip(y, dy), _flip(c, dc)
            sends.append(pltpu.make_async_remote_copy(src_ref=x_ref, dst_ref=o_ref.at[me], send_sem=send_sems.at[k],
                                                      recv_sem=recv_sems.at[k], device_id=(tx, ty, tc), device_id_type=MESH))
            recvs.append(pltpu.make_async_remote_copy(src_ref=x_ref, dst_ref=o_ref.at[4 * tx + 2 * ty + tc],
                                                      send_sem=send_sems.at[k], recv_sem=recv_sems.at[k],
                                                      device_id=(tx, ty, tc), device_id_type=MESH))
        for cp in sends:
            cp.start()
        for cp in recvs:
            cp.wait_recv()
        for cp in sends:
            cp.wait_send()

    return pl.pallas_call(
        body, name=name, out_shape=jax.ShapeDtypeStruct((N_DEV,) + rows.shape, rows.dtype),
        in_specs=[VMEM] + [pl.BlockSpec(memory_space=pl.ANY)] * len(order), out_specs=VMEM,
        scratch_shapes=[pltpu.SemaphoreType.DMA((N_DEV - 1,)), pltpu.SemaphoreType.DMA((N_DEV - 1,))],
    )(rows, *order)


WEIGHTS = ("ada_w", "ada_b", "ln_g", "ln_b", "e_w_in", "gmlp_norm_g", "gmlp_norm_b", "gmlp_ws", "gmlp_bs", "pool_w",
           "pool_b", "pool_scale", "e_w_out", "o_w_in", "mla_q_norm_g", "mla_kv_norm_g", "mla_w_uq", "mla_w_uk",
           "mla_w_uv", "o_w_out")
SMALL = ("ln_g", "ln_b", "gmlp_norm_g", "gmlp_norm_b", "gmlp_bs", "pool_b", "pool_scale", "mla_kv_norm_g", "mla_q_norm_g")


def _pad_cols(v, n):
    return jnp.concatenate([v, jnp.zeros((v.shape[0], n - v.shape[1]), v.dtype)], axis=1) if n > v.shape[1] else v


def _halves(g):
    return g.reshape(g.shape[0], 2, g.shape[1] // 2, g.shape[2])


def kernel(x, c, positions, ada_w, ada_b, ln_g, ln_b, e_w_in, gmlp_norm_g, gmlp_norm_b, gmlp_ws, gmlp_bs, pool_w, pool_b, pool_scale, e_w_out, o_w_in, mla_q_norm_g, mla_kv_norm_g, mla_w_uq, mla_w_uk, mla_w_uv, o_w_out, loss_target, m_ada_w, m_ada_b, m_ln_g, m_ln_b, m_e_w_in, m_gmlp_norm_g, m_gmlp_norm_b, m_gmlp_ws, m_gmlp_bs, m_pool_w, m_pool_b, m_pool_scale, m_e_w_out, m_o_w_in, m_mla_q_norm_g, m_mla_kv_norm_g, m_mla_w_uq, m_mla_w_uk, m_mla_w_uv, m_o_w_out, v_ada_w, v_ada_b, v_ln_g, v_ln_b, v_e_w_in, v_gmlp_norm_g, v_gmlp_norm_b, v_gmlp_ws, v_gmlp_bs, v_pool_w, v_pool_b, v_pool_scale, v_e_w_out, v_o_w_in, v_mla_q_norm_g, v_mla_kv_norm_g, v_mla_w_uq, v_mla_w_uk, v_mla_w_uv, v_o_w_out):
    args = dict(locals())
    weights = {n: args[n] for n in WEIGHTS}
    mom = {n: args["m_" + n] for n in WEIGHTS}
    var = {n: args["v_" + n] for n in WEIGHTS}
    ax, ay, ac = lax.axis_index("x"), lax.axis_index("y"), lax.axis_index("c")
    chip = 2 * ax + ay
    dev = 2 * chip + ac
    d = D_MODEL
    x2 = x[0]
    target = loss_target[0]
    q_rank_sh = mla_q_norm_g.shape[1]

    empty_zone = lambda w: lax.dynamic_update_slice(lax.empty((N_CHIPS,) + w.shape, w.dtype), w[None], (chip, 0, 0))
    shards0 = [w.astype(BF16) for w in (pool_w[0].reshape(-1, POOL_GROUP_DIM), e_w_out[0])]
    shards1 = [w.astype(BF16) for w in (o_w_in[0], mla_w_uq[0].reshape(q_rank_sh, -1), o_w_out[0])]
    w_in0, = _gather_weights([e_w_in[0].astype(BF16)], name="gather_weights")
    wuk_hrd = jnp.transpose(mla_w_uk[0], (1, 0, 2)).astype(BF16)
    wuk_hdr = jnp.transpose(mla_w_uk[0], (1, 2, 0)).astype(BF16)
    wuv_hrv = jnp.transpose(mla_w_uv[0], (1, 0, 2)).astype(BF16)
    wuv_hvr = jnp.transpose(mla_w_uv[0], (1, 2, 0)).astype(BF16)
    ws = gmlp_ws[0]
    ws_t = jnp.transpose(ws, (0, 2, 1))
    bs_t = _pad_cols(gmlp_bs[0].T, LANES)

    inv = 1.0 / (ROPE_THETA ** (jnp.arange(0, MLA_ROPE, 2, dtype=F32) / MLA_ROPE))
    ang = positions[0].astype(F32)[:, None] * inv
    cos_t = jnp.tile(jnp.cos(ang), (1, 4))
    sin_t = jnp.concatenate([-jnp.sin(ang), -jnp.sin(ang), jnp.sin(ang), jnp.sin(ang)], axis=1)

    c_all = _all_gather_devices(c.reshape(8, LANES), after=w_in0, name="gather_c").reshape(N_DEV, d)
    cols = ada_w.shape[2]
    ada_b_mine = lax.dynamic_slice_in_dim(ada_b, chip * cols, cols, axis=1)[:, None, :]
    mod_sh = _ada_mod(c_all, ada_w, ada_b_mine, name="ada_mod")
    q_norm_rows = jnp.zeros((8, cols), F32).at[0, :q_rank_sh].set(mla_q_norm_g[0])
    mod_all = _all_gather_chips(jnp.concatenate([mod_sh.reshape(2 * N_DEV, cols), q_norm_rows]), name="gather_mod")
    q_norm_g = mod_all[:, 2 * N_DEV, :q_rank_sh].reshape(1, -1)
    mod_all = jnp.transpose(mod_all[:, :2 * N_DEV].reshape(N_CHIPS, 2, N_DEV, cols), (1, 2, 0, 3)).reshape(2, N_DEV, 3 * d)
    mod = lax.dynamic_index_in_dim(mod_all, dev, axis=1, keepdims=False)
    shift = [mod[l:l + 1, :d] for l in range(2)]
    scale = [mod[l:l + 1, d:2 * d] for l in range(2)]
    gate = [mod[l:l + 1, 2 * d:] for l in range(2)]
    flight0 = _chips_start(shards0, [empty_zone(w) for w in shards0], mod, scatter=False, name="gather0_start")
    flight1 = _chips_start(shards1, [empty_zone(w) for w in shards1], flight0[3], scatter=False, name="gather1_start")

    scale[0] = scale[0] + flight1[3][:1, :1]
    h0 = _modulate(x2, scale[0], shift[0], name="modulate0")
    proj0 = _matmul(h0, w_in0, b_stacked=True, tm=1024, tn=1280, out_dtype=BF16, name="proj0")
    pool_w_g, w_out0 = _chips_wait(*flight0[:3], proj0, scatter=False, name="gather0_wait")
    pool_w_bf = jnp.transpose(pool_w_g.reshape(N_CHIPS, POOL_GROUPS, -1, POOL_GROUP_DIM), (1, 0, 2, 3)).reshape(
        POOL_GROUPS, POOL_GROUP_DIM, POOL_GROUP_DIM)
    w_out0 = w_out0.reshape(-1, d)
    mix0 = _even_fwd(proj0, ws, bs_t, gmlp_norm_g, gmlp_norm_b, pool_w_bf, pool_b, pool_scale, name="even_fwd")
    y0, x1, h1 = _out_resid_ln(mix0, w_out0, x2, gate[0], ln_g[0:1], ln_b[0:1], scale[1], shift[1], name="out0_ln")

    w_in1_g, w_uq_g, w_out1 = _chips_wait(*flight1[:3], h1, scatter=False, name="gather1_wait")
    w_out1 = w_out1.reshape(-1, d)
    w_in1 = jnp.transpose(w_in1_g, (1, 0, 2)).reshape(d, ODD_IN)
    w_in1 = jnp.concatenate([_pad_cols(w_in1[:, :ODD_SMALL], ODD_SMALL_PAD), w_in1[:, ODD_SMALL:]], axis=1)
    w_uq = w_uq_g.reshape(MLA_Q_RANK, MLA_HEADS, MLA_NOPE + MLA_ROPE)
    w_uq_nope = w_uq[:, :, :MLA_NOPE].reshape(MLA_Q_RANK, -1)
    w_uq_rope = jnp.transpose(w_uq[:, :, MLA_NOPE:].reshape(MLA_Q_RANK, MLA_HEADS // 2, 2, 2, ROPE_HALF),
                              (0, 1, 3, 2, 4)).reshape(MLA_Q_RANK, -1)
    proj1 = _matmul(h1, w_in1, tm=1024, tn=1280, out_dtype=BF16, name="proj1")
    q_cn, keys = _mla_prep(proj1, q_norm_g, mla_kv_norm_g, cos_t, sin_t, name="mla_prep")
    q_nope = _matmul(q_cn, w_uq_nope, tm=1024, tn=2048, name="q_nope", out_dtype=BF16)
    q_rope_pre = _matmul(q_cn, w_uq_rope, tm=1024, name="q_rope")
    q = _q_build(q_nope, q_rope_pre, wuk_hdr, cos_t, sin_t, name="q_build")
    o_lat, lse = _attn_fwd(q, keys, name="attn_fwd")
    og = _o_build(o_lat, wuv_hrv, proj1, name="o_build")

    dy1, dres1, g_ln_g1, g_ln_b1, dgate1, loss = _out_loss_ln_bwd(
        og, w_out1, x1, gate[1], ln_g[1:2], ln_b[1:2], target, name="out1_loss_ln")
    dg1 = _matmul(dy1, w_out1, trans_b=True, tn=2048, out_dtype=BF16, name="d_og")
    g_w_out1 = _matmul(og, dy1, trans_a=True, out_dtype=BF16, tm=1024, name="g_out1")
    do_lat, dz, g_uv = _o_bwd(dg1, proj1, o_lat, wuv_hrv, wuv_hvr, name="o_bwd")
    dq, dkeys = _attn_bwd(q, keys, do_lat, o_lat, lse, name="attn_bwd")
    dq_nope, dq_rope, g_uk = _q_bwd(dq, q_nope, wuk_hrd, cos_t, sin_t, name="q_bwd")
    dq_cn = (_matmul(dq_nope, w_uq_nope, trans_b=True, tm=1024, name="d_qcn_nope")
             + _matmul(dq_rope, w_uq_rope, trans_b=True, tm=1024, name="d_qcn_rope"))
    g_uq_nope = _matmul(q_cn, dq_nope, trans_a=True, out_dtype=BF16, tn=2048, name="g_uq_nope")
    g_uq_rope = _matmul(q_cn, dq_rope, trans_a=True, out_dtype=BF16, name="g_uq_rope")
    dsmall, g_qg, g_kvg = _mla_prep_bwd(proj1, dq_cn, dkeys, q_norm_g, mla_kv_norm_g, cos_t, sin_t, name="mla_prep_bwd")
    dproj1 = jnp.concatenate([dsmall, dz], axis=1)
    g_w_in1 =_matmul(h1, dproj1, trans_a=True, out_dtype=BF16, tm=1024, tn=1280, name="g_in1")

    g_uq_rope = jnp.transpose(g_uq_rope.reshape(MLA_Q_RANK, MLA_HEADS // 2, 2, 2, ROPE_HALF), (0, 1, 3, 2, 4))
    g_uq = jnp.concatenate([g_uq_nope.reshape(MLA_Q_RANK, MLA_HEADS, MLA_NOPE), g_uq_rope.reshape(MLA_Q_RANK, MLA_HEADS, MLA_ROPE)], axis=2)
    g_w_in1 = jnp.concatenate([g_w_in1[:, :ODD_SMALL], g_w_in1[:, ODD_SMALL_PAD:]], axis=1)
    g_w_in1 = jnp.transpose(g_w_in1.reshape(d, N_CHIPS, -1), (1, 0, 2))
    big1 = [
        _halves(g_w_in1),
        _halves(g_uq.reshape(N_CHIPS, q_rank_sh, -1)),
        _halves(g_w_out1.reshape(N_CHIPS, -1, d)),
        _halves(g_uk.astype(BF16).reshape(N_CHIPS, -1, MLA_NOPE)),
        _halves(g_uv.astype(BF16).reshape(N_CHIPS, -1, MLA_V)),
    ]
    parts1 = _reduce_sibling(big1, name="reduce_sibling1")
    lands2 = [lax.dynamic_update_slice(lax.empty(p.shape, BF16), lax.dynamic_slice_in_dim(p, chip, 1, axis=0), (chip, 0, 0))
              for p in parts1]
    flight2 = _chips_start(parts1, lands2, loss, scatter=True, name="reduce1_start")

    gate[0] = gate[0] + flight2[3][:1, :1]
    dy0, dres0, g_ln_g0, g_ln_b0, dgate0, dscale1, dshift1 = _dh_mid_ln_bwd(
        dproj1, w_in1, x2, y0, gate[0], ln_g[0:1], ln_b[0:1], dres1, scale[1], x1, name="d_h1_mid_ln")
    dmix0 = _matmul(dy0, w_out0, trans_b=True, tn=2048, out_dtype=BF16, name="d_mix0")
    g_w_out0 = _matmul(mix0, dy0, trans_a=True, out_dtype=BF16, tm=1024, name="g_out0")
    dproj0, g_ws, g_bs_t, g_ng, g_nb, g_pw, g_pb, g_ps = _even_bwd(
        proj0, dmix0, ws, ws_t, bs_t, gmlp_norm_g, gmlp_norm_b, pool_w_bf, pool_b, pool_scale, name="even_bwd")
    g_w_in0 = _matmul(h0, dproj0, trans_a=True, out_dtype=BF16, out_stacked=True, tm=1024, tn=1280, name="g_in0")

    g_pw = jnp.transpose(g_pw.astype(BF16).reshape(POOL_GROUPS, N_CHIPS, -1, POOL_GROUP_DIM), (1, 0, 2, 3))
    big0 = [
        _halves(g_w_in0),
        _halves(g_pw.reshape(N_CHIPS, -1, POOL_GROUP_DIM)),
        _halves(g_w_out0.reshape(N_CHIPS, -1, d)),
        _halves(g_ws.astype(BF16)),
    ]
    parts0 = _reduce_sibling(big0, name="reduce_sibling0")
    landed1 = _chips_wait(*flight2[:3], parts0[0], scatter=True, name="reduce1_wait")
    lands3 = [lax.dynamic_update_slice(lax.empty(p.shape, BF16), lax.dynamic_slice_in_dim(p, chip, 1, axis=0), (chip, 0, 0))
              for p in parts0]
    flight3 = _chips_start(parts0, lands3, landed1[0], scatter=True, name="reduce0_start")
    grad_x, dscale0, dshift0 = _dh_input_bwd(dproj0, w_in0, x2, dres0, scale[0], after=flight3[3], name="d_h0_input")

    small_local = {
        "ln_g": jnp.concatenate([g_ln_g0, g_ln_g1]), "ln_b": jnp.concatenate([g_ln_b0, g_ln_b1]),
        "gmlp_norm_g": g_ng, "gmlp_norm_b": g_nb, "gmlp_bs": g_bs_t[:, :GMLP_HEADS].T, "pool_b": g_pb, "pool_scale": g_ps,
        "mla_kv_norm_g": g_kvg, "mla_q_norm_g": g_qg,
    }
    n_mod = 2 * 3 * d
    vec = jnp.concatenate([dshift0, dscale0, dgate0, dshift1, dscale1, dgate1]
                          + [small_local[n].reshape(1, -1) for n in SMALL] + [loss], axis=1)
    n_vec = vec.shape[1]
    vec = _pad_cols(vec, -(-n_vec // (8 * LANES)) * 8 * LANES).reshape(-1, LANES)
    vec_all = _all_gather_devices(vec, name="gather_small")
    vec_sum = _sum_devices(vec_all, name="sum_small").reshape(-1)
    dmod_all = vec_all.reshape(N_DEV, -1)[:, :n_mod].reshape(N_DEV, 2, 3 * d)
    dmod_sh = jnp.transpose(lax.dynamic_slice_in_dim(dmod_all, chip * cols, cols, axis=2), (1, 0, 2))
    dmod_sh = jnp.concatenate([dmod_sh, jnp.zeros((2, LANES - N_DEV, cols), F32)], axis=1)
    grads = {"ada_w": _ada_grad(_pad_cols(c_all.T, LANES), dmod_sh, name="ada_grad"), "ada_b": vec_sum[:n_mod].reshape(2, 3 * d)}
    off = n_mod
    for n in SMALL:
        sz = small_local[n].size
        grads[n] = vec_sum[off:off + sz]
        off += sz
    grads["mla_q_norm_g"] = lax.dynamic_slice_in_dim(grads["mla_q_norm_g"], chip * q_rank_sh, q_rank_sh)
    for n in SMALL:
        grads[n] = grads[n].reshape(weights[n].shape)

    landed0 = _chips_wait(*flight3[:3], grads["ada_w"], scatter=True, name="reduce0_wait")
    totals = _reduce_chips([], list(landed0) + list(landed1), name="reduce_chips")
    for n, t in zip(("e_w_in", "pool_w", "e_w_out", "gmlp_ws", "o_w_in", "mla_w_uq", "o_w_out"), totals):
        if n != "gmlp_ws":
            grads[n] = t.reshape(weights[n].shape)
    rep = jnp.concatenate([t.reshape(-1, LANES) for t in (totals[3], totals[7], totals[8])])
    rep_land = lax.dynamic_update_slice(lax.empty((N_CHIPS,) + rep.shape, F32), rep[None], (chip, 0, 0))
    flight4 = _chips_start([rep], [rep_land], totals[0], scatter=False, name="gather_rep_start")

    delta, new_m, new_v = {}, {}, {}
    replicated = ("gmlp_ws", "mla_w_uk", "mla_w_uv")
    large = [n for n in WEIGHTS if n not in SMALL and n != "ada_b"]
    for n in large:
        if n not in replicated:
            delta[n], new_m[n], new_v[n] = _adamw(weights[n], grads[n], mom[n], var[n], after=flight4[3], name="adamw_" + n)
    rep = _chips_wait(*flight4[:3], delta["e_w_in"], scatter=False, name="gather_rep_wait")[0]
    r_ws, r_uk = GMLP_BLOCK, 4 * MLA_KV_RANK
    grads["gmlp_ws"] = rep[:, :r_ws].reshape(weights["gmlp_ws"].shape)
    grads["mla_w_uk"] = jnp.transpose(rep[:, r_ws:r_ws + r_uk].reshape(MLA_HEADS, MLA_KV_RANK, MLA_NOPE), (1, 0, 2))[None]
    grads["mla_w_uv"] = jnp.transpose(rep[:, r_ws + r_uk:].reshape(MLA_HEADS, MLA_KV_RANK, MLA_V), (1, 0, 2))[None]
    for n in replicated:
        delta[n], new_m[n], new_v[n] = _adamw(weights[n], grads[n], mom[n], var[n], name="adamw_" + n)
    small = [n for n in WEIGHTS if n not in large]
    ds, ms, vs = _adamw_small([weights[n] for n in small], [grads[n] for n in small], [mom[n] for n in small],
                              [var[n] for n in small], name="adamw_small")
    for n, dn, mn, vn in zip(small, ds, ms, vs):
        delta[n], new_m[n], new_v[n] = dn, mn, vn

    return (vec_sum[n_vec - 1], grad_x[None], *[grads[n] for n in WEIGHTS], *[delta[n] for n in WEIGHTS],
            *[new_m[n] for n in WEIGHTS], *[new_v[n] for n in WEIGHTS])
```

```python
import jax
import jax.numpy as jnp
from jax import lax
from jax.experimental import pallas as pl
from jax.experimental.pallas import tpu as pltpu

F32 = jnp.float32
BF16 = jnp.bfloat16
MESH = pl.DeviceIdType.MESH

D_MODEL = 1024
CHUNK = 64
LN_EPS = 1e-5
GMLP_HEADS = 4
GMLP_HEAD_DIM = 256
GMLP_BLOCK = 128
POOL_WINDOWS = (2, 4, 8, 16)
POOL_GROUPS = 4
POOL_GROUP_DIM = 256
POOL_HALO = 16
EVEN_IN = 5120
MLA_HEADS = 16
MLA_NOPE = 128
MLA_ROPE = 64
MLA_V = 128
MLA_Q_RANK = 256
MLA_KV_RANK = 128
MLA_WIDTH = MLA_HEADS * MLA_V
ODD_IN = 2496
ODD_SMALL = MLA_Q_RANK + MLA_KV_RANK + MLA_ROPE
ODD_SMALL_PAD = 512
QK_PAD = 256
ROPE_THETA = 10000.0
ATTN_SCALE = (MLA_NOPE + MLA_ROPE) ** -0.5
DEEPNORM_ALPHA = (2.0 * 2) ** 0.25
ADAM_LR = 0.001
ADAM_B1 = 0.9
ADAM_B2 = 0.999
ADAM_EPS = 1e-08
ADAM_WD = 0.01
ADAM_STEP = 10
NEG = -1e30
LANES = 128
N_DEV = 8
N_CHIPS = 4
VMEM_LIMIT_BYTES = 56 * 1024 * 1024
HBM = pl.BlockSpec(memory_space=pltpu.HBM)
VMEM = pl.BlockSpec(memory_space=pltpu.VMEM)


def _params(*sem):
    return pltpu.CompilerParams(dimension_semantics=sem if sem else None, vmem_limit_bytes=VMEM_LIMIT_BYTES)


def _tile(dim, pref):
    for t in (pref, 2048, 1280, 1024, 512, 256, 128):
        if t <= min(pref, dim) and dim % t == 0:
            return t
    return dim


def _sigmoid(z):
    return 1.0 / (1.0 + jnp.exp(-z))


def _dot(a, b, dims):
    return lax.dot_general(a, b, (dims, ((), ())), preferred_element_type=F32)


NN = ((1,), (0,))
NT = ((1,), (1,))
TN = ((0,), (0,))


def _matmul(a, b, *, name, trans_a=False, trans_b=False, out_dtype=F32, b_stacked=False, out_stacked=False,
            tm=512, tn=1024, tk=2048, after=None):
    k, m = a.shape if trans_a else a.shape[::-1]
    if b_stacked:
        ns, kb, n_sh = b.shape
        kb, n = (ns * n_sh, kb) if trans_b else (kb, ns * n_sh)
    else:
        n, kb = b.shape if trans_b else b.shape[::-1]
    assert k == kb, (a.shape, b.shape)
    tm = _tile(m, tm)
    if b_stacked and trans_b:
        tn, tk = _tile(n, tn), n_sh
    elif b_stacked or out_stacked:
        tn, tk = _tile(n // N_CHIPS, tn), _tile(k, tk)
    else:
        tn, tk = _tile(n, tn), _tile(k, tk)
    nk = k // tk
    per = max((n // N_CHIPS) // tn, 1)
    dims = ((0 if trans_a else 1,), (1 if trans_b else 0,))

    def body_one(a_ref, b_ref, *rest):
        o_ref = rest[-1]
        o_ref[...] = _dot(a_ref[...].astype(BF16), b_ref[...].astype(BF16), dims).astype(out_dtype)

    def body_acc(a_ref, b_ref, *rest):
        o_ref, acc_ref = rest[-2:]
        kk = pl.program_id(2)

        @pl.when(kk == 0)
        def _():
            acc_ref[...] = jnp.zeros_like(acc_ref)

        acc_ref[...] += _dot(a_ref[...].astype(BF16), b_ref[...].astype(BF16), dims)

        @pl.when(kk == nk - 1)
        def _():
            o_ref[...] = acc_ref[...].astype(out_dtype)

    a_spec = pl.BlockSpec((tk, tm), lambda i, j, kk: (kk, i)) if trans_a else pl.BlockSpec((tm, tk), lambda i, j, kk: (i, kk))
    if b_stacked and trans_b:
        b_spec = pl.BlockSpec((None, tn, tk), lambda i, j, kk: (kk, j, 0))
    elif b_stacked:
        b_spec = pl.BlockSpec((None, tk, tn), lambda i, j, kk: (j // per, kk, j % per))
    elif trans_b:
        b_spec = pl.BlockSpec((tn, tk), lambda i, j, kk: (j, kk))
    else:
        b_spec = pl.BlockSpec((tk, tn), lambda i, j, kk: (kk, j))
    if out_stacked:
        o_spec = pl.BlockSpec((None, tm, tn), lambda i, j, kk: (j // per, i, j % per))
        o_shape = jax.ShapeDtypeStruct((N_CHIPS, m, n // N_CHIPS), out_dtype)
    else:
        o_spec = pl.BlockSpec((tm, tn), lambda i, j, kk: (i, j))
        o_shape = jax.ShapeDtypeStruct((m, n), out_dtype)
    order = [] if after is None else [after]
    return pl.pallas_call(
        body_one if nk == 1 else body_acc, name=name, grid=(m // tm, n // tn, nk),
        in_specs=[a_spec, b_spec] + [pl.BlockSpec(memory_space=pl.ANY)] * len(order),
        out_specs=o_spec, out_shape=o_shape, scratch_shapes=[] if nk == 1 else [pltpu.VMEM((tm, tn), F32)],
        compiler_params=_params("parallel", "parallel", "arbitrary"),
    )(a, b, *order)


def _matmul_rows(a, b, epilogue, row_ins, vec_ins, row_outs, vec_outs, *, name, trans_b=False, b_stacked=False,
                 tm=512, tk=2048, after=None):
    m, k = a.shape
    if b_stacked:
        ns, n, n_sh = b.shape
        assert trans_b and ns * n_sh == k
        tk = n_sh
    else:
        n = b.shape[0] if trans_b else b.shape[1]
        tk = _tile(k, tk)
    tm = _tile(m, tm)
    nk = k // tk
    dims = ((1,), (1 if trans_b else 0,))
    n_ri, n_vi, n_ro, n_vo = len(row_ins), len(vec_ins), len(row_outs), len(vec_outs)
    order = [] if after is None else [after]

    def body(*refs):
        a_ref, b_ref = refs[:2]
        pos = 2
        rin = refs[pos:pos + n_ri]
        pos += n_ri
        vin = refs[pos:pos + n_vi]
        pos += n_vi + len(order)
        rout = refs[pos:pos + n_ro]
        pos += n_ro
        vout = refs[pos:pos + n_vo]
        first = pl.program_id(0) == 0
        part = _dot(a_ref[...].astype(BF16), b_ref[...].astype(BF16), dims)
        if nk == 1:
            epilogue(part, first, rin, vin, rout, vout)
        else:
            acc_ref = refs[-1]
            kk = pl.program_id(1)

            @pl.when(kk == 0)
            def _():
                acc_ref[...] = part

            @pl.when(kk > 0)
            def _():
                acc_ref[...] += part

            @pl.when(kk == nk - 1)
            def _():
                epilogue(acc_ref[...], first, rin, vin, rout, vout)

    a_spec = pl.BlockSpec((tm, tk), lambda i, kk: (i, kk))
    if b_stacked:
        b_spec = pl.BlockSpec((None, n, tk), lambda i, kk: (kk, 0, 0))
    elif trans_b:
        b_spec = pl.BlockSpec((n, tk), lambda i, kk: (0, kk))
    else:
        b_spec = pl.BlockSpec((tk, n), lambda i, kk: (kk, 0))
    row = pl.BlockSpec((tm, n), lambda i, kk: (i, 0))
    vec = lambda w: pl.BlockSpec((1, w), lambda i, kk: (0, 0))
    return pl.pallas_call(
        body, name=name, grid=(m // tm, nk),
        in_specs=[a_spec, b_spec] + [row] * n_ri + [vec(v.shape[1]) for v in vec_ins] + [pl.BlockSpec(memory_space=pl.ANY)] * len(order),
        out_specs=[row] * n_ro + [vec(w) for w in vec_outs],
        out_shape=[jax.ShapeDtypeStruct((m, n), dt) for dt in row_outs] + [jax.ShapeDtypeStruct((1, w), F32) for w in vec_outs],
        scratch_shapes=[] if nk == 1 else [pltpu.VMEM((tm, n), F32)],
        compiler_params=_params("arbitrary", "arbitrary"),
    )(a, b, *row_ins, *vec_ins, *order)


def _row_spec(ts, d):
    return pl.BlockSpec((ts, d), lambda i: (i, 0))


def _vec_spec(d):
    return pl.BlockSpec((1, d), lambda i: (0, 0))


def _modulate(x, scale, shift, *, name):
    s, d = x.shape
    ts = _tile(s, 512)

    def body(x_ref, sc_ref, sh_ref, h_ref):
        h_ref[...] = (x_ref[...] * (1.0 + sc_ref[...]) + sh_ref[...]).astype(BF16)

    return pl.pallas_call(
        body, name=name, grid=(s // ts,), in_specs=[_row_spec(ts, d), _vec_spec(d), _vec_spec(d)],
        out_specs=_row_spec(ts, d), out_shape=jax.ShapeDtypeStruct((s, d), BF16), compiler_params=_params("parallel"),
    )(x, scale, shift)


def _ln_stats(pre):
    mu = jnp.mean(pre, axis=-1, keepdims=True)
    xc = pre - mu
    var = jnp.mean(xc * xc, axis=-1, keepdims=True)
    rstd = lax.rsqrt(var + LN_EPS)
    return xc * rstd, rstd


def _ln_bwd_rows(dout, xhat, rstd, g):
    dxh = dout * g
    m1 = jnp.mean(dxh, axis=-1, keepdims=True)
    m2 = jnp.mean(dxh * xhat, axis=-1, keepdims=True)
    return rstd * (dxh - m1 - xhat * m2)


def _colsum(v):
    return jnp.sum(v, axis=0, keepdims=True)


def _out_resid_ln(mix, w_out, x, gate, g, b, scale_next, shift_next, *, name):
    def epilogue(y, first, rin, vin, rout, vout):
        (x_ref,), (gate_ref, g_ref, b_ref, sc_ref, sh_ref), (y_ref, xn_ref, h_ref) = rin, vin, rout
        y_ref[...] = y
        pre = DEEPNORM_ALPHA * x_ref[...] + (1.0 + gate_ref[...]) * y
        xhat, _ = _ln_stats(pre)
        xn = xhat * g_ref[...] + b_ref[...]
        xn_ref[...] = xn
        h_ref[...] = (xn * (1.0 + sc_ref[...]) + sh_ref[...]).astype(BF16)

    return _matmul_rows(mix, w_out, epilogue, [x], [gate, g, b, scale_next, shift_next], [F32, F32, BF16], [], name=name)


def _out_loss_ln_bwd(og, w_out, x, gate, g, b, target, *, name):
    d = x.shape[1]

    def epilogue(yv, first, rin, vin, rout, vout):
        (x_ref, t_ref), (gate_ref, g_ref, b_ref), (dy_ref, dres_ref), (dg_ref, db_ref, dgate_ref, loss_ref) = rin, vin, rout, vout

        @pl.when(first)
        def _():
            for r in vout:
                r[...] = jnp.zeros_like(r)

        pre = DEEPNORM_ALPHA * x_ref[...] + (1.0 + gate_ref[...]) * yv
        xhat, rstd = _ln_stats(pre)
        diff = xhat * g_ref[...] + b_ref[...] - t_ref[...]
        loss_ref[...] += (0.5 / d) * jnp.sum(jnp.sum(diff * diff, axis=1, keepdims=True), axis=0, keepdims=True)
        dout = diff * (1.0 / d)
        dpre = _ln_bwd_rows(dout, xhat, rstd, g_ref[...])
        dy_ref[...] = (dpre * (1.0 + gate_ref[...])).astype(BF16)
        dres_ref[...] = DEEPNORM_ALPHA * dpre
        dg_ref[...] += _colsum(dout * xhat)
        db_ref[...] += _colsum(dout)
        dgate_ref[...] += _colsum(dpre * yv)

    return _matmul_rows(og, w_out, epilogue, [x, target], [gate, g, b], [BF16, F32], [d, d, d, 1], name=name)


def _dh_mid_ln_bwd(dproj, w_in, x, y, gate, g, b, dres_next, scale_next, x_next, *, name):
    d = x.shape[1]

    def epilogue(dh, first, rin, vin, rout, vout):
        (x_ref, y_ref, dr_ref, xn_ref), (gate_ref, g_ref, b_ref, sc_ref), (dy_ref, dres_ref) = rin, vin, rout
        dg_ref, db_ref, dgate_ref, dscale_ref, dshift_ref = vout

        @pl.when(first)
        def _():
            for r in vout:
                r[...] = jnp.zeros_like(r)

        dout = dr_ref[...] + dh * (1.0 + sc_ref[...])
        dscale_ref[...] += _colsum(dh * xn_ref[...])
        dshift_ref[...] += _colsum(dh)
        yv = y_ref[...]
        pre = DEEPNORM_ALPHA * x_ref[...] + (1.0 + gate_ref[...]) * yv
        xhat, rstd = _ln_stats(pre)
        dpre = _ln_bwd_rows(dout, xhat, rstd, g_ref[...])
        dy_ref[...] = (dpre * (1.0 + gate_ref[...])).astype(BF16)
        dres_ref[...] = DEEPNORM_ALPHA * dpre
        dg_ref[...] += _colsum(dout * xhat)
        db_ref[...] += _colsum(dout)
        dgate_ref[...] += _colsum(dpre * yv)

    return _matmul_rows(dproj, w_in, epilogue, [x, y, dres_next, x_next], [gate, g, b, scale_next], [BF16, F32], [d] * 5,
                        trans_b=True, tk=1280, name=name)


def _dh_input_bwd(dproj, w_in_stacked, x, dres, scale, *, name, after):
    d = x.shape[1]

    def epilogue(dh, first, rin, vin, rout, vout):
        (x_ref, dr_ref), (sc_ref,), (dx_ref,), (dscale_ref, dshift_ref) = rin, vin, rout, vout

        @pl.when(first)
        def _():
            for r in vout:
                r[...] = jnp.zeros_like(r)

        dx_ref[...] = dr_ref[...] + dh * (1.0 + sc_ref[...])
        dscale_ref[...] += _colsum(dh * x_ref[...])
        dshift_ref[...] += _colsum(dh)

    return _matmul_rows(dproj, w_in_stacked, epilogue, [x, dres], [scale], [F32], [d, d], trans_b=True, b_stacked=True,
                        tm=1024, after=after, name=name)


def _chunk_mask(transposed=False):
    r = lax.broadcasted_iota(jnp.int32, (GMLP_BLOCK, GMLP_BLOCK), 0) // CHUNK
    c = lax.broadcasted_iota(jnp.int32, (GMLP_BLOCK, GMLP_BLOCK), 1) // CHUNK
    return (r <= c) if transposed else (c <= r)


def _window_sum(ext, steps, forward):
    rows = ext.shape[0]
    acc = ext
    for k in range(steps):
        shift = 1 << k
        acc = acc + pltpu.roll(acc, (rows - shift) if forward else shift, 0)
    return acc


def _pool_counts(first_row, rows, win):
    t = first_row + lax.broadcasted_iota(jnp.int32, (rows, 1), 0)
    return jnp.minimum(t + 1, win).astype(F32)


def _even_specs(t):
    col = lambda j: pl.BlockSpec((t, D_MODEL), lambda n: (n, j))
    per = t // POOL_HALO
    prev = pl.BlockSpec((POOL_HALO, D_MODEL), lambda n: (jnp.maximum(n * per - 1, 0), 3))
    return col, per, prev


def _full(shape):
    return pl.BlockSpec(shape, lambda n: (0,) * len(shape))


def _gmlp_head(v_h, ng, nb, w_bf):
    xhat, rstd = _ln_stats(v_h)
    vn = (xhat * ng + nb).astype(BF16)
    return xhat, rstd, vn, _dot(w_bf, vn, NN)


def _pool_group(xb_g, prev_g, first_row, grp):
    t = xb_g.shape[0]
    ext = jnp.concatenate([prev_g, xb_g], axis=0)
    tot = _window_sum(ext, grp + 1, False)[POOL_HALO:, :]
    cnt = _pool_counts(first_row, t, POOL_WINDOWS[grp])
    return tot / cnt - xb_g, cnt


def _even_fwd(proj, ws, bs_t, ng, nb, pool_w, pool_b, pool_scale, *, name):
    s = proj.shape[0]
    t = GMLP_BLOCK
    col, per, prev = _even_specs(t)

    def body(u_ref, v_ref, za_ref, xb_ref, zb_ref, xp_ref, ws_ref, bs_ref, ng_ref, nb_ref, pw_ref, pb_ref, ps_ref, o_ref):
        n = pl.program_id(0)
        mask = _chunk_mask()
        for h in range(GMLP_HEADS):
            c0 = h * GMLP_HEAD_DIM
            cs = slice(c0, c0 + GMLP_HEAD_DIM)
            w_bf = jnp.where(mask, ws_ref[h], 0.0).astype(BF16)
            _, _, _, sv = _gmlp_head(v_ref[:, cs].astype(F32),ng_ref[...], nb_ref[...], w_bf)
            sv = sv + bs_ref[:, h:h + 1]
            za = za_ref[:, cs].astype(F32)
            o_ref[:, cs] = (u_ref[:, cs].astype(F32) * sv * (za * _sigmoid(za))).astype(BF16)
        live = (n > 0).astype(F32)
        for grp in range(POOL_GROUPS):
            c0 = grp * POOL_GROUP_DIM
            cs = slice(c0, c0 + POOL_GROUP_DIM)
            pooled, _ = _pool_group(xb_ref[:, cs].astype(F32), xp_ref[:, cs].astype(F32) * live, n * t, grp)
            yb = _dot(pooled.astype(BF16), pw_ref[grp], NN) + pb_ref[:, cs]
            zb = zb_ref[:, cs].astype(F32)
            o_ref[:, D_MODEL + c0:D_MODEL + c0 + POOL_GROUP_DIM] = (yb * ps_ref[:, cs] * (zb * _sigmoid(zb))).astype(BF16)

    return pl.pallas_call(
        body, name=name, grid=(s // t,),
        in_specs=[col(0), col(1), col(2), col(3), col(4), prev,
                  _full((GMLP_HEADS, t, t)), _full((t, LANES)), _full((1, GMLP_HEAD_DIM)), _full((1, GMLP_HEAD_DIM)),
                  _full((POOL_GROUPS, POOL_GROUP_DIM, POOL_GROUP_DIM)), _full((1, D_MODEL)), _full((1, D_MODEL))],
        out_specs=pl.BlockSpec((t, 2 * D_MODEL), lambda n: (n, 0)),
        out_shape=jax.ShapeDtypeStruct((s, 2 * D_MODEL), BF16),
        compiler_params=_params("parallel"),
    )(proj, proj, proj, proj, proj, proj, ws, bs_t, ng, nb, pool_w, pool_b, pool_scale)


def _even_bwd(proj, dmix, ws, ws_t, bs_t, ng, nb, pool_w, pool_b, pool_scale, *, name):
    s = proj.shape[0]
    t = GMLP_BLOCK
    nblk = s // t
    col, per, prev = _even_specs(t)
    nxt = lambda j: pl.BlockSpec((POOL_HALO, D_MODEL), lambda n: (jnp.minimum((n + 1) * per, nblk * per - 1), j))

    def body(u_ref, v_ref, za_ref, xb_ref, zb_ref, xp_ref, zn_ref, da_ref, db_ref, dbn_ref,
             ws_ref, wst_ref, bs_ref, ng_ref, nb_ref, pw_ref, pb_ref, ps_ref,
             dp_ref, gws_ref, gbs_ref, gng_ref, gnb_ref, gpw_ref, gpb_ref, gps_ref):
        n = pl.program_id(0)

        @pl.when(n == 0)
        def _():
            for r in (gws_ref, gbs_ref, gng_ref, gnb_ref, gpw_ref, gpb_ref, gps_ref):
                r[...] = jnp.zeros_like(r)

        mask, mask_t = _chunk_mask(), _chunk_mask(True)
        lane = lax.broadcasted_iota(jnp.int32, (t, LANES), 1)
        ngv, nbv = ng_ref[...], nb_ref[...]
        for h in range(GMLP_HEADS):
            c0 = h * GMLP_HEAD_DIM
            cs = slice(c0, c0 + GMLP_HEAD_DIM)
            w_bf = jnp.where(mask, ws_ref[h], 0.0).astype(BF16)
            wt_bf = jnp.where(mask_t, wst_ref[h], 0.0).astype(BF16)
            xhat, rstd, vn, sv = _gmlp_head(v_ref[:, cs].astype(F32),ngv, nbv, w_bf)
            sv = sv + bs_ref[:, h:h + 1]
            za, u, da = za_ref[:, cs].astype(F32), u_ref[:, cs].astype(F32), da_ref[:, cs].astype(F32)
            sg = _sigmoid(za)
            sl = za * sg
            dp_ref[:, cs] = (da * sv * sl).astype(BF16)
            dp_ref[:, 2 * D_MODEL + c0:2 * D_MODEL + c0 + GMLP_HEAD_DIM] = (
                da * u * sv * (sg * (1.0 + za * (1.0 - sg)))).astype(BF16)
            dsv = da * u * sl
            gbs_ref[...] += jnp.where(lane == h, jnp.sum(dsv, axis=1, keepdims=True), 0.0)
            dsv_bf = dsv.astype(BF16)
            gws_ref[h] += jnp.where(mask, _dot(dsv_bf, vn, NT), 0.0)
            dvn = _dot(wt_bf, dsv_bf, NN)
            dp_ref[:, D_MODEL + c0:D_MODEL + c0 + GMLP_HEAD_DIM] = _ln_bwd_rows(dvn, xhat, rstd, ngv).astype(BF16)
            gng_ref[...] += _colsum(dvn * xhat)
            gnb_ref[...] += _colsum(dvn)
        live_prev = (n > 0).astype(F32)
        live_next = (n < nblk - 1).astype(F32)
        for grp in range(POOL_GROUPS):
            c0 = grp * POOL_GROUP_DIM
            cs = slice(c0, c0 + POOL_GROUP_DIM)
            xb = xb_ref[:, cs].astype(F32)
            pooled, cnt = _pool_group(xb, xp_ref[:, cs].astype(F32) * live_prev, n * t, grp)
            pooled_bf = pooled.astype(BF16)
            pw = pw_ref[grp]
            yb = _dot(pooled_bf, pw, NN) + pb_ref[:, cs]
            ps = ps_ref[:, cs]
            zb, db = zb_ref[:, cs].astype(F32), db_ref[:, cs].astype(F32)
            sg = _sigmoid(zb)
            sl = zb * sg
            dp_ref[:, 4 * D_MODEL + c0:4 * D_MODEL + c0 + POOL_GROUP_DIM] = (
                db * yb * ps * (sg * (1.0 + zb * (1.0 - sg)))).astype(BF16)
            dsl = db * sl
            dy = dsl * ps
            gps_ref[:, cs] += _colsum(dsl * yb)
            gpb_ref[:, cs] += _colsum(dy)
            dy_bf = dy.astype(BF16)
            gpw_ref[grp] += _dot(pooled_bf, dy_bf, TN)
            r = _dot(dy_bf, pw, NT)
            zn = zn_ref[:, cs].astype(F32)
            dyn = (dbn_ref[:, cs].astype(F32) * (zn * _sigmoid(zn)) * ps * live_next).astype(BF16)
            rn = _dot(dyn, pw, NT) / _pool_counts((n + 1) * t, POOL_HALO, POOL_WINDOWS[grp])
            ext = jnp.concatenate([r / cnt, rn], axis=0)
            dxb = _window_sum(ext, grp + 1, True)[:t, :] - r
            dp_ref[:, 3 * D_MODEL + c0:3 * D_MODEL + c0 + POOL_GROUP_DIM] = dxb.astype(BF16)

    out_shape = [
        jax.ShapeDtypeStruct((s, EVEN_IN), BF16),
        jax.ShapeDtypeStruct((GMLP_HEADS, t, t), F32), jax.ShapeDtypeStruct((t, LANES), F32),
        jax.ShapeDtypeStruct((1, GMLP_HEAD_DIM), F32), jax.ShapeDtypeStruct((1, GMLP_HEAD_DIM), F32),
        jax.ShapeDtypeStruct((POOL_GROUPS, POOL_GROUP_DIM, POOL_GROUP_DIM), F32),
        jax.ShapeDtypeStruct((1, D_MODEL), F32), jax.ShapeDtypeStruct((1, D_MODEL), F32),
    ]
    return pl.pallas_call(
        body, name=name, grid=(nblk,),
        in_specs=[col(0), col(1), col(2), col(3), col(4), prev, nxt(4),
                  pl.BlockSpec((t, D_MODEL), lambda n: (n, 0)), pl.BlockSpec((t, D_MODEL), lambda n: (n, 1)), nxt(1),
                  _full((GMLP_HEADS, t, t)), _full((GMLP_HEADS, t, t)), _full((t, LANES)),
                  _full((1, GMLP_HEAD_DIM)), _full((1, GMLP_HEAD_DIM)),
                  _full((POOL_GROUPS, POOL_GROUP_DIM, POOL_GROUP_DIM)), _full((1, D_MODEL)), _full((1, D_MODEL))],
        out_specs=[pl.BlockSpec((t, EVEN_IN), lambda n: (n, 0))] + [_full(o.shape) for o in out_shape[1:]],
        out_shape=out_shape,
        compiler_params=_params("arbitrary"),
    )(proj, proj, proj, proj, proj, proj, proj, dmix, dmix, dmix, ws, ws_t, bs_t, ng, nb, pool_w, pool_b, pool_scale)


ROPE_HALF = MLA_ROPE // 2


def _rope(v, cos, sin_signed):
    return v * cos + pltpu.roll(v, 2 * ROPE_HALF, 1) * sin_signed


def _rope_bwd(d, cos, sin_signed):
    return d * cos + pltpu.roll(d * sin_signed, 2 * ROPE_HALF, 1)


def _slab_lanes(shape, which):
    lane = lax.broadcasted_iota(jnp.int32, shape, 1)
    return (lane // ROPE_HALF) % 2 == which


def _rms(v, g):
    r = lax.rsqrt(jnp.mean(v * v, axis=-1, keepdims=True) + LN_EPS)
    return v * r * g, r


def _rms_bwd(dy, v, r, g):
    u = dy * g
    return r * u - v * (r * r * r) * jnp.mean(u * v, axis=-1, keepdims=True)


def _mla_prep(proj, gq, gkv, cos, sin_signed, *, name):
    s = proj.shape[0]
    ts = _tile(s, 512)

    def body(p_ref, gq_ref, gkv_ref, c_ref, s_ref, q_ref, k_ref):
        qcn, _ = _rms(p_ref[:, :MLA_Q_RANK].astype(F32), gq_ref[...])
        kvn, _ = _rms(p_ref[:, MLA_Q_RANK:MLA_Q_RANK + MLA_KV_RANK].astype(F32), gkv_ref[...])
        kr = p_ref[:, MLA_Q_RANK + MLA_KV_RANK:].astype(F32)
        lane = lax.broadcasted_iota(jnp.int32, kr.shape, 1)
        by1, by2 = pltpu.roll(kr, ROPE_HALF, 1), pltpu.roll(kr, 2 * ROPE_HALF, 1)
        both = jnp.where(lane < ROPE_HALF, kr, jnp.where(lane < 3 * ROPE_HALF, by1, by2))
        kr = _rope(both, c_ref[...], s_ref[...])
        q_ref[...] = qcn.astype(BF16)
        k_ref[...] = jnp.concatenate([kvn, kr], axis=1).astype(BF16)

    return pl.pallas_call(
        body, name=name, grid=(s // ts,),
        in_specs=[_small_spec(ts), _vec_spec(MLA_Q_RANK), _vec_spec(MLA_KV_RANK), _row_spec(ts, LANES), _row_spec(ts, LANES)],
        out_specs=[_row_spec(ts, MLA_Q_RANK), _row_spec(ts, QK_PAD)],
        out_shape=[jax.ShapeDtypeStruct((s, MLA_Q_RANK), BF16), jax.ShapeDtypeStruct((s, QK_PAD), BF16)],
        compiler_params=_params("parallel"),
    )(proj, gq, gkv, cos, sin_signed)


def _mla_prep_bwd(proj, dqcn, dkv, gq, gkv, cos, sin_signed, *, name):
    s = proj.shape[0]
    ts = _tile(s, 512)

    def body(p_ref, dq_ref, dkv_ref, gq_ref, gkv_ref, c_ref, s_ref, ds_ref, ggq_ref, ggkv_ref):
        @pl.when(pl.program_id(0) == 0)
        def _():
            ggq_ref[...] = jnp.zeros_like(ggq_ref)
            ggkv_ref[...] = jnp.zeros_like(ggkv_ref)

        qc = p_ref[:, :MLA_Q_RANK].astype(F32)
        kvc = p_ref[:, MLA_Q_RANK:MLA_Q_RANK + MLA_KV_RANK].astype(F32)
        _, rq = _rms(qc, gq_ref[...])
        _, rkv = _rms(kvc, gkv_ref[...])
        dq = dq_ref[...]
        dkvn = dkv_ref[:, :MLA_KV_RANK]
        ggq_ref[...] += _colsum(dq * qc * rq)
        ggkv_ref[...] += _colsum(dkvn * kvc * rkv)
        dboth = _rope_bwd(dkv_ref[:, MLA_KV_RANK:], c_ref[...], s_ref[...])
        lane = lax.broadcasted_iota(jnp.int32, dboth.shape, 1)
        pair = dboth + pltpu.roll(dboth, 3 * ROPE_HALF, 1)
        dkr = jnp.where(lane < ROPE_HALF, pair, jnp.where(lane < 2 * ROPE_HALF, pltpu.roll(pair, 3 * ROPE_HALF, 1), 0.0))
        ds_ref[...] = jnp.concatenate(
            [_rms_bwd(dq, qc, rq, gq_ref[...]), _rms_bwd(dkvn, kvc, rkv, gkv_ref[...]), dkr], axis=1).astype(BF16)

    return pl.pallas_call(
        body, name=name, grid=(s // ts,),
        in_specs=[_small_spec(ts), _row_spec(ts, MLA_Q_RANK), _row_spec(ts, QK_PAD),
                  _vec_spec(MLA_Q_RANK), _vec_spec(MLA_KV_RANK), _row_spec(ts, LANES), _row_spec(ts, LANES)],
        out_specs=[_row_spec(ts, ODD_SMALL_PAD), _vec_spec(MLA_Q_RANK), _vec_spec(MLA_KV_RANK)],
        out_shape=[jax.ShapeDtypeStruct((s, ODD_SMALL_PAD), BF16), jax.ShapeDtypeStruct((1, MLA_Q_RANK), F32),
                   jax.ShapeDtypeStruct((1, MLA_KV_RANK), F32)],
        compiler_params=_params("arbitrary"),
    )(proj, dqcn, dkv, gq, gkv, cos, sin_signed)


Q_HEAD_GROUP = 8
LOG2_E = 1.4426950408889634
Q_PRESCALE = ATTN_SCALE * LOG2_E


def _q_build(q_nope, q_rope_pre, wuk_hdr, cos, sin_signed, *, name):
    s = q_nope.shape[0]
    ts = _tile(s, 512)
    hg = Q_HEAD_GROUP

    def body(qn_ref, qr_ref, w_ref, c_ref, s_ref, o_ref):
        for pair in range(hg // 2):
            r = _rope(qr_ref[:, pair * LANES:(pair + 1) * LANES], c_ref[...], s_ref[...])
            for j in range(2):
                h = 2 * pair + j
                ql = _dot(qn_ref[:, h * MLA_NOPE:(h + 1) * MLA_NOPE], w_ref[h], NN)
                mine = jnp.where(_slab_lanes(r.shape, j), r, 0.0)
                o_ref[h] = (jnp.concatenate([ql, mine], axis=1) * Q_PRESCALE).astype(BF16)

    return pl.pallas_call(
        body, name=name, grid=(s // ts, MLA_HEADS // hg),
        in_specs=[pl.BlockSpec((ts, hg * MLA_NOPE), lambda i, p: (i, p)), pl.BlockSpec((ts, hg * MLA_ROPE), lambda i, p: (i, p)),
                  pl.BlockSpec((hg, MLA_NOPE, MLA_KV_RANK), lambda i, p: (p, 0, 0)),
                  pl.BlockSpec((ts, LANES), lambda i, p: (i, 0)), pl.BlockSpec((ts, LANES), lambda i, p: (i, 0))],
        out_specs=pl.BlockSpec((hg, ts, QK_PAD), lambda i, p: (p, i, 0)),
        out_shape=jax.ShapeDtypeStruct((MLA_HEADS, s, QK_PAD), BF16),
        compiler_params=_params("parallel", "parallel"),
    )(q_nope, q_rope_pre, wuk_hdr, cos, sin_signed)


def _q_bwd(dq, q_nope, wuk_hrd, cos, sin_signed, *, name):
    s = q_nope.shape[0]
    ts = _tile(s, 512)
    hg = Q_HEAD_GROUP

    def body(dq_ref, qn_ref, w_ref, c_ref, s_ref, dn_ref, dr_ref, gw_ref):
        @pl.when(pl.program_id(1) == 0)
        def _():
            gw_ref[...] = jnp.zeros_like(gw_ref)

        for h in range(hg):
            dql = dq_ref[h, :, :MLA_KV_RANK]
            dn_ref[:, h * MLA_NOPE:(h + 1) * MLA_NOPE] = _dot(dql, w_ref[h], NN).astype(BF16)
            gw_ref[h] += _dot(dql, qn_ref[:, h * MLA_NOPE:(h + 1) * MLA_NOPE], TN)
        for pair in range(hg // 2):
            hi0 = dq_ref[2 * pair, :, MLA_KV_RANK:].astype(F32)
            hi1 = dq_ref[2 * pair + 1, :, MLA_KV_RANK:].astype(F32)
            d = jnp.where(_slab_lanes(hi0.shape, 0), hi0, hi1)
            dr_ref[:, pair * LANES:(pair + 1) * LANES] = _rope_bwd(d, c_ref[...], s_ref[...]).astype(BF16)

    return pl.pallas_call(
        body, name=name, grid=(MLA_HEADS // hg, s // ts),
        in_specs=[pl.BlockSpec((hg, ts, QK_PAD), lambda p, i: (p, i, 0)), pl.BlockSpec((ts, hg * MLA_NOPE), lambda p, i: (i, p)),
                  pl.BlockSpec((hg, MLA_KV_RANK, MLA_NOPE), lambda p, i: (p, 0, 0)),
                  pl.BlockSpec((ts, LANES), lambda p, i: (i, 0)), pl.BlockSpec((ts, LANES), lambda p, i: (i, 0))],
        out_specs=[pl.BlockSpec((ts, hg * MLA_NOPE), lambda p, i: (i, p)), pl.BlockSpec((ts, hg * MLA_ROPE), lambda p, i: (i, p)),
                   pl.BlockSpec((hg, MLA_KV_RANK, MLA_NOPE), lambda p, i: (p, 0, 0))],
        out_shape=[jax.ShapeDtypeStruct((s, MLA_HEADS * MLA_NOPE), BF16), jax.ShapeDtypeStruct((s, MLA_HEADS * MLA_ROPE), BF16),
                   jax.ShapeDtypeStruct((MLA_HEADS, MLA_KV_RANK, MLA_NOPE), F32)],
        compiler_params=_params("parallel", "arbitrary"),
    )(dq, q_nope, wuk_hrd, cos, sin_signed)


ATTN_BQ = 128
ATTN_BK = 512


def _diag_mask(rows, bq, bk, q0, k0):
    qc = (q0 + lax.broadcasted_iota(jnp.int32, (rows, bk), 0) % bq) // CHUNK
    kc = (k0 + lax.broadcasted_iota(jnp.int32, (rows, bk), 1)) // CHUNK
    return kc <= qc


def _attn_fwd(q, k, *, name):
    nh, s, dk = q.shape
    bq, bk = _tile(s, ATTN_BQ), _tile(s, ATTN_BK)
    rows = nh * bq

    def body(q_ref, k_ref, o_ref, lse_ref):
        i = pl.program_id(0)
        qb = q_ref[...].reshape(rows, dk)
        n_before = (i * bq) // bk

        def step(j, width, carry, masked):
            m, l, acc = carry
            k0 = pl.multiple_of(j * bk, bk)
            kb = k_ref[pl.ds(k0, width), :]
            sc = _dot(qb, kb, NT)
            if masked:
                sc = jnp.where(_diag_mask(rows, bq, width, i * bq, k0), sc, NEG)
            m_new = jnp.maximum(m, jnp.max(sc, axis=1, keepdims=True))
            p = jnp.exp2(sc - m_new)
            a = jnp.exp2(m - m_new)
            l = a * l + jnp.sum(p, axis=1, keepdims=True)
            acc = a * acc + _dot(p.astype(BF16), kb[:, :MLA_KV_RANK], NN)
            return m_new, l, acc

        init = (jnp.full((rows, 1), NEG, F32), jnp.zeros((rows, 1), F32), jnp.zeros((rows, MLA_KV_RANK), F32))
        carry = lax.fori_loop(0, n_before, lambda j, c: step(j, bk, c, False), init)
        for part in range(bk // bq):
            @pl.when(i % (bk // bq) == part)
            def _(part=part):
                m, l, acc = step(n_before, (part + 1) * bq, carry, True)
                o_ref[...] = (acc / l).astype(BF16).reshape(nh, bq, MLA_KV_RANK)
                lse_ref[...] = jnp.broadcast_to(m + jnp.log2(l), (rows, LANES)).reshape(nh, bq, LANES)

    return pl.pallas_call(
        body, name=name, grid=(s // bq,),
        in_specs=[pl.BlockSpec((nh, bq, dk), lambda i: (0, i, 0)), pl.BlockSpec((s, dk), lambda i: (0, 0))],
        out_specs=[pl.BlockSpec((nh, bq, MLA_KV_RANK), lambda i: (0, i, 0)), pl.BlockSpec((nh, bq, LANES), lambda i: (0, i, 0))],
        out_shape=[jax.ShapeDtypeStruct((nh, s, MLA_KV_RANK), BF16), jax.ShapeDtypeStruct((nh, s, LANES), F32)],
        compiler_params=_params("parallel"),
    )(q, k)


def _attn_bwd(q, k, do, o, lse, *, name):
    nh, s, dk = q.shape
    bq, bk = _tile(s, ATTN_BQ), _tile(s, ATTN_BK)
    rows = nh * bq

    def body(q_ref, k_ref, do_ref, o_ref, lse_ref, dq_ref, dkv_ref):
        i = pl.program_id(0)
        n_before = (i * bq) // bk

        @pl.when(i == 0)
        def _():
            dkv_ref[...] = jnp.zeros_like(dkv_ref)

        qb = q_ref[...].reshape(rows, dk)
        dob = do_ref[...].reshape(rows, MLA_KV_RANK)
        lse_b = lse_ref[...].reshape(rows, LANES)[:, :1]
        delta = jnp.sum(dob.astype(F32) * o_ref[...].reshape(rows, MLA_KV_RANK).astype(F32), axis=1, keepdims=True)

        def step(j, width, dq, masked):
            j0 = pl.multiple_of(j * bk, bk)
            kb = k_ref[pl.ds(j0, width), :]
            sc = _dot(qb, kb, NT)
            if masked:
                sc = jnp.where(_diag_mask(rows, bq, width, i * bq, j0), sc, NEG)
            p = jnp.exp2(sc - lse_b)
            dp = _dot(dob, kb[:, :MLA_KV_RANK], NT)
            ds_bf = (p * (dp - delta)).astype(BF16)
            dkv_ref[pl.ds(j0, width), :] += _dot(ds_bf, qb, TN) * (1.0 / LOG2_E)
            dkv_ref[pl.ds(j0, width), :MLA_KV_RANK] += _dot(p.astype(BF16), dob, TN)
            return dq + _dot(ds_bf, kb, NN)

        dq_before = lax.fori_loop(0, n_before, lambda j, c: step(j, bk, c, False), jnp.zeros((rows, dk), F32))
        for part in range(bk // bq):
            @pl.when(i % (bk // bq) == part)
            def _(part=part):
                dq = step(n_before, (part + 1) * bq, dq_before, True) * ATTN_SCALE
                dq_ref[...] = dq.astype(BF16).reshape(nh, bq, dk)

    blk = lambda w: pl.BlockSpec((nh, bq, w), lambda i: (0, i, 0))
    return pl.pallas_call(
        body, name=name, grid=(s // bq,),
        in_specs=[blk(dk), pl.BlockSpec((s, dk), lambda i: (0, 0)), blk(MLA_KV_RANK), blk(MLA_KV_RANK), blk(LANES)],
        out_specs=[blk(dk), pl.BlockSpec((s, dk), lambda i: (0, 0))],
        out_shape=[jax.ShapeDtypeStruct((nh, s, dk), BF16), jax.ShapeDtypeStruct((s, dk), F32)],
        compiler_params=_params("arbitrary"),
    )(q, k, do, o, lse)


HEAD_GROUP = 8
SMALL_BLOCK = MLA_WIDTH // ODD_SMALL_PAD


def _small_spec(ts):
    return pl.BlockSpec((ts, ODD_SMALL_PAD), lambda i: (i, SMALL_BLOCK))


def _o_build(o_lat, wuv_hrv, proj, *, name):
    s = proj.shape[0]
    ts = _tile(s, 512)
    w = HEAD_GROUP * MLA_V

    def body(ol_ref, w_ref, z_ref, og_ref):
        for j in range(HEAD_GROUP):
            cs = slice(j * MLA_V, (j + 1) * MLA_V)
            z = z_ref[:, cs].astype(F32)
            og_ref[:, cs] = (_dot(ol_ref[j], w_ref[j], NN) * (z * _sigmoid(z))).astype(BF16)

    return pl.pallas_call(
        body, name=name, grid=(s // ts, MLA_HEADS // HEAD_GROUP),
        in_specs=[pl.BlockSpec((HEAD_GROUP, ts, MLA_KV_RANK), lambda i, g: (g, i, 0)),
                  pl.BlockSpec((HEAD_GROUP, MLA_KV_RANK, MLA_V), lambda i, g: (g, 0, 0)),
                  pl.BlockSpec((ts, w), lambda i, g: (i, g))],
        out_specs=pl.BlockSpec((ts, w), lambda i, g: (i, g)),
        out_shape=jax.ShapeDtypeStruct((s, MLA_WIDTH), BF16),
        compiler_params=_params("parallel", "parallel"),
    )(o_lat, wuv_hrv, proj)


def _o_bwd(dg, proj, o_lat, wuv_hrv, wuv_hvr, *, name):
    s = proj.shape[0]
    ts = _tile(s, 512)
    w = HEAD_GROUP * MLA_V

    def body(dg_ref, z_ref, ol_ref, w_ref, wt_ref, dol_ref, dz_ref, gw_ref):
        @pl.when(pl.program_id(1) == 0)
        def _():
            gw_ref[...] = jnp.zeros_like(gw_ref)

        for j in range(HEAD_GROUP):
            cs = slice(j * MLA_V, (j + 1) * MLA_V)
            z, dgj, ol = z_ref[:, cs].astype(F32), dg_ref[:, cs].astype(F32), ol_ref[j]
            sg = _sigmoid(z)
            o = _dot(ol, w_ref[j], NN)
            dz_ref[:, cs] = (dgj * o * (sg * (1.0 + z * (1.0 - sg)))).astype(BF16)
            do_bf = (dgj * (z * sg)).astype(BF16)
            dol_ref[j] = _dot(do_bf, wt_ref[j], NN).astype(BF16)
            gw_ref[j] += _dot(ol, do_bf, TN)

    hs = lambda a, b: pl.BlockSpec((HEAD_GROUP, a, b), lambda g, i: (g, 0, 0))
    return pl.pallas_call(
        body, name=name, grid=(MLA_HEADS // HEAD_GROUP, s // ts),
        in_specs=[pl.BlockSpec((ts, w), lambda g, i: (i, g)), pl.BlockSpec((ts, w), lambda g, i: (i, g)),
                  pl.BlockSpec((HEAD_GROUP, ts, MLA_KV_RANK), lambda g, i: (g, i, 0)),
                  hs(MLA_KV_RANK, MLA_V), hs(MLA_V, MLA_KV_RANK)],
        out_specs=[pl.BlockSpec((HEAD_GROUP, ts, MLA_KV_RANK), lambda g, i: (g, i, 0)),
                   pl.BlockSpec((ts, w), lambda g, i: (i, g)), hs(MLA_KV_RANK, MLA_V)],
        out_shape=[jax.ShapeDtypeStruct((MLA_HEADS, s, MLA_KV_RANK), BF16), jax.ShapeDtypeStruct((s, MLA_WIDTH), BF16),
                   jax.ShapeDtypeStruct((MLA_HEADS, MLA_KV_RANK, MLA_V), F32)],
        compiler_params=_params("parallel", "arbitrary"),
    )(dg, proj, o_lat, wuv_hrv, wuv_hvr)


def _ada_mod(c_all, ada_w, ada_b_sh, *, name):
    nl, _, cols = ada_w.shape

    def body(c_ref, w_ref, b_ref, o_ref):
        c = c_ref[...]
        cond = (c * _sigmoid(c)).astype(BF16)
        for l in range(nl):
            o_ref[l] = _dot(cond, w_ref[l].astype(BF16), NN) + b_ref[l]

    return pl.pallas_call(
        body, name=name, out_shape=jax.ShapeDtypeStruct((nl, c_all.shape[0], cols), F32),
        compiler_params=_params(),
    )(c_all, ada_w, ada_b_sh)


def _ada_grad(c_all_t, dmod_sh, *, name):
    nl, _, cols = dmod_sh.shape
    d = c_all_t.shape[0]

    def body(c_ref, dm_ref, gw_ref):
        c = c_ref[...]
        cond_t = c * _sigmoid(c)
        for l in range(nl):
            gw_ref[l] = lax.dot_general(cond_t, dm_ref[l], (NN, ((), ())), precision=lax.Precision.HIGHEST,
                                        preferred_element_type=F32)

    return pl.pallas_call(
        body, name=name, out_shape=jax.ShapeDtypeStruct((nl, d, cols), F32), compiler_params=_params(),
    )(c_all_t, dmod_sh)


def _sum_devices(parts, *, name):
    def body(p_ref, o_ref):
        acc = p_ref[0]
        for k in range(1, parts.shape[0]):
            acc = acc + p_ref[k]
        o_ref[...] = acc

    return pl.pallas_call(body, name=name, out_shape=jax.ShapeDtypeStruct(parts.shape[1:], F32), compiler_params=_params())(parts)


def _adamw_math(w, g, m, v):
    c1 = 1.0 - ADAM_B1 ** ADAM_STEP
    c2 = 1.0 - ADAM_B2 ** ADAM_STEP
    nm = ADAM_B1 * m + (1.0 - ADAM_B1) * g
    nv = ADAM_B2 * v + (1.0 - ADAM_B2) * (g * g)
    return -ADAM_LR * ((nm / c1) / (jnp.sqrt(nv / c2) + ADAM_EPS) + ADAM_WD * w), nm, nv


ADAMW_BLOCK_BYTES = 1 << 20


def _adamw(w, g, m, v, *, name, after=None):
    shape = w.shape
    a, b = shape[-2], shape[-1]
    lead = 1
    for dim in shape[:-2]:
        lead *= dim
    row_bytes = 4 * b
    if a * row_bytes <= ADAMW_BLOCK_BYTES:
        ta = a
        tl = max(1, min(lead, ADAMW_BLOCK_BYTES // (a * row_bytes)))
        while lead % tl:
            tl -= 1
    else:
        tl = 1
        ta = _tile(a, 256)
    to3 = lambda t: t.reshape(lead, a, b)

    def body(w_ref, g_ref, m_ref, v_ref, *rest):
        d_ref, nm_ref, nv_ref = rest[-3:]
        d_ref[...], nm_ref[...], nv_ref[...] = _adamw_math(w_ref[...], g_ref[...], m_ref[...], v_ref[...])

    spec = pl.BlockSpec((tl, ta, b), lambda i, j: (i, j, 0))
    out = jax.ShapeDtypeStruct((lead, a, b), F32)
    order = [] if after is None else [after]
    res = pl.pallas_call(
        body, name=name, grid=(lead // tl, a // ta), in_specs=[spec] * 4 + [pl.BlockSpec(memory_space=pl.ANY)] * len(order),
        out_specs=[spec] * 3, out_shape=[out] * 3, compiler_params=_params("parallel", "parallel"),
    )(to3(w), to3(g), to3(m), to3(v), *order)
    return [r.reshape(shape) for r in res]


def _adamw_small(ws, gs, ms, vs, *, name):
    n = len(ws)

    def body(*refs):
        for k in range(n):
            w_ref, g_ref, m_ref, v_ref = (refs[j * n + k] for j in range(4))
            d_ref, nm_ref, nv_ref = (refs[(4 + j) * n + k] for j in range(3))
            d_ref[...], nm_ref[...], nv_ref[...] = _adamw_math(w_ref[...], g_ref[...], m_ref[...], v_ref[...])

    outs = [jax.ShapeDtypeStruct(w.shape, F32) for w in ws]
    res = pl.pallas_call(body, name=name, out_shape=outs * 3, compiler_params=_params())(*ws, *gs, *ms, *vs)
    return res[:n], res[n:2 * n], res[2 * n:]


def _flip(v, bit):
    return 1 - v if bit else v


CHIP_DELTAS = ((1, 0), (0, 1), (1, 1))
SUM_ROWS = 32


def _all_gather_chips(shard, *, name):
    def body(x_ref, o_ref, send_sems, recv_sems, local_sem):
        x, y, c = lax.axis_index("x"), lax.axis_index("y"), lax.axis_index("c")
        mine = pltpu.make_async_copy(x_ref, o_ref.at[2 * x + y], local_sem)
        mine.start()

        def copy(k):
            tx, ty = _flip(x, CHIP_DELTAS[k][0]), _flip(y, CHIP_DELTAS[k][1])
            send = pltpu.make_async_remote_copy(src_ref=x_ref, dst_ref=o_ref.at[2 * x + y], send_sem=send_sems.at[k],
                                                recv_sem=recv_sems.at[k], device_id=(tx, ty, c), device_id_type=MESH)
            recv = pltpu.make_async_remote_copy(src_ref=x_ref, dst_ref=o_ref.at[2 * tx + ty], send_sem=send_sems.at[k],
                                                recv_sem=recv_sems.at[k], device_id=(tx, ty, c), device_id_type=MESH)
            return send, recv

        pairs = [copy(k) for k in range(3)]
        for send, _ in pairs:
            send.start()
        for _, recv in pairs:
            recv.wait_recv()
        for send, _ in pairs:
            send.wait_send()
        mine.wait()

    return pl.pallas_call(
        body, name=name, out_shape=jax.ShapeDtypeStruct((N_CHIPS,) + shard.shape, shard.dtype),
        in_specs=[HBM], out_specs=HBM,
        scratch_shapes=[pltpu.SemaphoreType.DMA((3,)), pltpu.SemaphoreType.DMA((3,)), pltpu.SemaphoreType.DMA(())],
    )(shard)


def _gather_weights(shards, *, name):
    n = len(shards)

    def body(*refs):
        w_refs, o_refs = refs[:n], refs[n:2 * n]
        ici_send, ici_recv, d2d_send, d2d_recv, local_sems = refs[2 * n:]
        x, y, c = lax.axis_index("x"), lax.axis_index("y"), lax.axis_index("c")
        me = 2 * x + y
        peers = [(_flip(x, dx), _flip(y, dy)) for dx, dy in CHIP_DELTAS]
        locals_ = [pltpu.make_async_copy(w_refs[k], o_refs[k].at[me], local_sems.at[k]) for k in range(n)]
        for cp in locals_:
            cp.start()

        def rows(k, which):
            half = shards[k].shape[0] // 2
            return pl.ds(pl.multiple_of(which * half, half), half)

        def over_chips(k, d, slot):
            tx, ty = peers[d]
            return pltpu.make_async_remote_copy(
                src_ref=w_refs[k].at[rows(k, c)], dst_ref=o_refs[k].at[slot, rows(k, c)], send_sem=ici_send.at[k, d],
                recv_sem=ici_recv.at[k, d], device_id=(tx, ty, c), device_id_type=MESH)

        def to_sibling(k, d, which):
            tx, ty = peers[d]
            at = o_refs[k].at[2 * tx + ty, rows(k, which)]
            return pltpu.make_async_remote_copy(src_ref=at, dst_ref=at, send_sem=d2d_send.at[k, d], recv_sem=d2d_recv.at[k, d],
                                                device_id=(x, y, 1 - c), device_id_type=MESH)

        sends = [over_chips(k, d, me) for k in range(n) for d in range(3)]
        for cp in sends:
            cp.start()
        passed = []
        for k in range(n):
            for d in range(3):
                over_chips(k, d, 2 * peers[d][0] + peers[d][1]).wait_recv()
                passed.append(to_sibling(k, d, c))
                passed[-1].start()
        for k in range(n):
            for d in range(3):
                to_sibling(k, d, 1 - c).wait_recv()
        for cp in sends + passed:
            cp.wait_send()
        for cp in locals_:
            cp.wait()

    return pl.pallas_call(
        body, name=name, out_shape=[jax.ShapeDtypeStruct((N_CHIPS,) + w.shape, w.dtype) for w in shards],
        in_specs=[HBM] * n, out_specs=[HBM] * n,
        scratch_shapes=[pltpu.SemaphoreType.DMA((n, 3))] * 4 + [pltpu.SemaphoreType.DMA((n,))],
    )(*shards)


def _add_into(dst_ref, src_ref):
    ns, r, _ = dst_ref.shape
    step = SUM_ROWS if r % SUM_ROWS == 0 else r
    for s in range(ns):
        def tile(t, carry):
            at = pl.ds(pl.multiple_of(t * step, step), step)
            dst_ref[s, at, :] = (dst_ref[s, at, :].astype(F32) + src_ref[s, at, :].astype(F32)).astype(dst_ref.dtype)
            return carry
        lax.fori_loop(0, r // step, tile, 0)


def _reduce_sibling(grads, *, name):
    n = len(grads)

    def body(*refs):
        g_refs, o_refs = refs[:n], refs[n:2 * n]
        mine, got = refs[2 * n:3 * n], refs[3 * n:4 * n]
        send_sems, recv_sems, load_sems, store_sems = refs[4 * n:]
        x, y, c = lax.axis_index("x"), lax.axis_index("y"), lax.axis_index("c")
        loads = [pltpu.make_async_copy(g_refs[k].at[:, c], mine[k], load_sems.at[k]) for k in range(n)]
        swaps = [pltpu.make_async_remote_copy(src_ref=g_refs[k].at[:, 1 - c], dst_ref=got[k], send_sem=send_sems.at[k],
                                              recv_sem=recv_sems.at[k], device_id=(x, y, 1 - c), device_id_type=MESH)
                 for k in range(n)]
        for cp in loads + swaps:
            cp.start()
        stores = []
        for k in range(n):
            loads[k].wait()
            swaps[k].wait_recv()
            _add_into(mine[k], got[k])
            stores.append(pltpu.make_async_copy(mine[k], o_refs[k], store_sems.at[k]))
            stores[-1].start()
        for k in range(n):
            swaps[k].wait_send()
            stores[k].wait()

    half = [jax.ShapeDtypeStruct((g.shape[0],) + g.shape[2:], g.dtype) for g in grads]
    return pl.pallas_call(
        body, name=name, out_shape=half, in_specs=[HBM] * n, out_specs=[HBM] * n,
        scratch_shapes=[pltpu.VMEM(h.shape, h.dtype) for h in half] * 2 + [pltpu.SemaphoreType.DMA((n,))] * 4,
        compiler_params=_params(),
    )(*grads)


def _reduce_chips(parts, landed, *, name):
    n_send = len(parts)
    n = n_send + len(landed)

    def body(*refs):
        p_refs, o_refs = refs[:n], refs[n:2 * n]
        got, total = refs[2 * n:3 * n], refs[3 * n:4 * n]
        send_sems, recv_sems, load_sems, share_send, share_recv, store_sems = refs[4 * n:]
        x, y, c = lax.axis_index("x"), lax.axis_index("y"), lax.axis_index("c")
        me = 2 * x + y
        peers = [(_flip(x, dx), _flip(y, dy)) for dx, dy in CHIP_DELTAS]

        def over_chips(k, d, src_slot, dst_slot):
            tx, ty = peers[d]
            return pltpu.make_async_remote_copy(
                src_ref=p_refs[k].at[src_slot], dst_ref=got[k].at[dst_slot], send_sem=send_sems.at[k, d],
                recv_sem=recv_sems.at[k, d], device_id=(tx, ty, c), device_id_type=MESH)

        loads = [pltpu.make_async_copy(p_refs[k].at[me], got[k].at[me], load_sems.at[k]) for k in range(n_send)]
        loads += [pltpu.make_async_copy(p_refs[k], got[k], load_sems.at[k]) for k in range(n_send, n)]
        sends = [over_chips(k, d, 2 * peers[d][0] + peers[d][1], me) for k in range(n_send) for d in range(3)]
        for cp in loads + sends:
            cp.start()
        shares, stores = [], []
        for k in range(n):
            loads[k].wait()
            for d in range(3 if k < n_send else 0):
                slot = 2 * peers[d][0] + peers[d][1]
                over_chips(k, d, slot, slot).wait_recv()
            r = total[k].shape[0]
            step = SUM_ROWS if r % SUM_ROWS == 0 else r

            def tile(t, carry, k=k, step=step):
                at = pl.ds(pl.multiple_of(t * step, step), step)
                acc = got[k][0, at, :].astype(F32)
                for s in range(1, N_CHIPS):
                    acc = acc + got[k][s, at, :].astype(F32)
                total[k][at, :] = acc
                return carry

            lax.fori_loop(0, r // step, tile, 0)
            stores.append(pltpu.make_async_copy(total[k], o_refs[k].at[c], store_sems.at[k]))
            shares.append(pltpu.make_async_remote_copy(
                src_ref=total[k], dst_ref=o_refs[k].at[c], send_sem=share_send.at[k], recv_sem=share_recv.at[k],
                device_id=(x, y, 1 - c), device_id_type=MESH))
            stores[-1].start()
            shares[-1].start()
        for k in range(n):
            pltpu.make_async_remote_copy(
                src_ref=total[k], dst_ref=o_refs[k].at[1 - c], send_sem=share_send.at[k], recv_sem=share_recv.at[k],
                device_id=(x, y, 1 - c), device_id_type=MESH).wait_recv()
        for cp in sends + shares:
            cp.wait_send()
        for cp in stores:
            cp.wait()

    both = list(parts) + list(landed)
    return pl.pallas_call(
        body, name=name, out_shape=[jax.ShapeDtypeStruct((2,) + p.shape[1:], F32) for p in both],
        in_specs=[HBM] * n, out_specs=[HBM] * n,
        scratch_shapes=[pltpu.VMEM(p.shape, p.dtype) for p in both] + [pltpu.VMEM(p.shape[1:], F32) for p in both]
        + [pltpu.SemaphoreType.DMA((n, 3))] * 2 + [pltpu.SemaphoreType.DMA((n,))] * 4,
        compiler_params=_params(),
    )(*both)


SEM = pl.BlockSpec(memory_space=pltpu.SEMAPHORE)
IN_FLIGHT = pltpu.SideEffectType.DATAFLOW_SIDE_EFFECTING


def _chip_copies(s_refs, l_refs, sems, scatter, theirs):
    x, y, c = lax.axis_index("x"), lax.axis_index("y"), lax.axis_index("c")
    me = 2 * x + y
    copies = []
    for k in range(len(s_refs)):
        for d, (dx, dy) in enumerate(CHIP_DELTAS):
            tx, ty = _flip(x, dx), _flip(y, dy)
            peer = 2 * tx + ty
            send_sem, recv_sem = sems[2 * (3 * k + d)], sems[2 * (3 * k + d) + 1]
            copies.append(pltpu.make_async_remote_copy(
                src_ref=s_refs[k].at[peer] if scatter else s_refs[k], dst_ref=l_refs[k].at[peer if theirs else me],
                send_sem=send_sem, recv_sem=recv_sem, device_id=(tx, ty, c), device_id_type=MESH))
    return copies


def _chips_start(srcs, lands, after, *, scatter, name):
    n = len(srcs)
    n_sem = 2 * 3 * n

    def body(*refs):
        s_refs, l_refs = refs[:n], refs[n:2 * n]
        sems = refs[2 * n + 1:2 * n + 1 + n_sem]
        token = refs[-1]
        for cp in _chip_copies(s_refs, l_refs, sems, scatter, False):
            cp.start()
        token[...] = jnp.zeros_like(token)

    hbm = lambda a: pltpu.HBM(a.shape, a.dtype)
    res = pl.pallas_call(
        body, name=name,
        out_shape=(*[pltpu.SemaphoreType.DMA(())] * n_sem, *[hbm(a) for a in srcs], *[hbm(a) for a in lands],
                   jax.ShapeDtypeStruct((8, LANES), F32)),
        in_specs=[HBM] * (2 * n) + [pl.BlockSpec(memory_space=pl.ANY)],
        out_specs=(*[SEM] * n_sem, *[HBM] * (2 * n), VMEM),
        input_output_aliases={k: n_sem + k for k in range(2 * n)},
        compiler_params=pltpu.CompilerParams(has_side_effects=IN_FLIGHT),
    )(*[pltpu.with_memory_space_constraint(a, pltpu.HBM) for a in list(srcs) + list(lands)], after)
    return res[:n_sem], res[n_sem:n_sem + n], res[n_sem + n:n_sem + 2 * n], res[-1]


def _chips_wait(sems, srcs, lands, after, *, scatter, name):
    n = len(srcs)
    n_sem = len(sems)

    def body(*refs):
        s_refs, l_refs = refs[:n], refs[n:2 * n]
        sem_refs = refs[2 * n:2 * n + n_sem]
        for cp in _chip_copies(s_refs, l_refs, sem_refs, scatter, False):
            cp.wait_send()
        for cp in _chip_copies(s_refs, l_refs, sem_refs, scatter, True):
            cp.wait_recv()

    hbm = lambda a: pltpu.HBM(a.shape, a.dtype)
    res = pl.pallas_call(
        body, name=name, out_shape=tuple(hbm(a) for a in list(srcs) + list(lands)),
        in_specs=[HBM] * (2 * n) + [SEM] * n_sem + [pl.BlockSpec(memory_space=pl.ANY)], out_specs=tuple([HBM] * (2 * n)),
        input_output_aliases={k: k for k in range(2 * n)},
        compiler_params=pltpu.CompilerParams(has_side_effects=IN_FLIGHT),
    )(*srcs, *lands, *sems, after)
    return res[n:]


def _all_gather_devices(rows, *, name, after=None):
    deltas = [(dx, dy, dc) for dx in (0, 1) for dy in (0, 1) for dc in (0, 1)][1:]
    order = [] if after is None else [after]

    def body(x_ref, *rest):
        o_ref, send_sems, recv_sems = rest[-3:]
        x, y, c = lax.axis_index("x"), lax.axis_index("y"), lax.axis_index("c")
        me = 4 * x + 2 * y + c
        o_ref[me] = x_ref[...]
        sends, recvs = [], []
        for k, (dx, dy, dc) in enumerate(deltas):
            tx, ty, tc = _flip(x, dx), _flip(y, dy), _flip(c, dc)
            sends.append(pltpu.make_async_remote_copy(src_ref=x_ref, dst_ref=o_ref.at[me], send_sem=send_sems.at[k],
                                                      recv_sem=recv_sems.at[k], device_id=(tx, ty, tc), device_id_type=MESH))
            recvs.append(pltpu.make_async_remote_copy(src_ref=x_ref, dst_ref=o_ref.at[4 * tx + 2 * ty + tc],
                                                      send_sem=send_sems.at[k], recv_sem=recv_sems.at[k],
                                                      device_id=(tx, ty, tc), device_id_type=MESH))
        for cp in sends:
            cp.start()
        for cp in recvs:
            cp.wait_recv()
        for cp in sends:
            cp.wait_send()

    return pl.pallas_call(
        body, name=name, out_shape=jax.ShapeDtypeStruct((N_DEV,) + rows.shape, rows.dtype),
        in_specs=[VMEM] + [pl.BlockSpec(memory_space=pl.ANY)] * len(order), out_specs=VMEM,
        scratch_shapes=[pltpu.SemaphoreType.DMA((N_DEV - 1,)), pltpu.SemaphoreType.DMA((N_DEV - 1,))],
    )(rows, *order)


WEIGHTS = ("ada_w", "ada_b", "ln_g", "ln_b", "e_w_in", "gmlp_norm_g", "gmlp_norm_b", "gmlp_ws", "gmlp_bs", "pool_w",
           "pool_b", "pool_scale", "e_w_out", "o_w_in", "mla_q_norm_g", "mla_kv_norm_g", "mla_w_uq", "mla_w_uk",
           "mla_w_uv", "o_w_out")
SMALL = ("ln_g", "ln_b", "gmlp_norm_g", "gmlp_norm_b", "gmlp_bs", "pool_b", "pool_scale", "mla_kv_norm_g", "mla_q_norm_g")


def _pad_cols(v, n):
    return jnp.concatenate([v, jnp.zeros((v.shape[0], n - v.shape[1]), v.dtype)], axis=1) if n > v.shape[1] else v


def _halves(g):
    return g.reshape(g.shape[0], 2, g.shape[1] // 2, g.shape[2])


def kernel(x, c, positions, ada_w, ada_b, ln_g, ln_b, e_w_in, gmlp_norm_g, gmlp_norm_b, gmlp_ws, gmlp_bs, pool_w, pool_b, pool_scale, e_w_out, o_w_in, mla_q_norm_g, mla_kv_norm_g, mla_w_uq, mla_w_uk, mla_w_uv, o_w_out, loss_target, m_ada_w, m_ada_b, m_ln_g, m_ln_b, m_e_w_in, m_gmlp_norm_g, m_gmlp_norm_b, m_gmlp_ws, m_gmlp_bs, m_pool_w, m_pool_b, m_pool_scale, m_e_w_out, m_o_w_in, m_mla_q_norm_g, m_mla_kv_norm_g, m_mla_w_uq, m_mla_w_uk, m_mla_w_uv, m_o_w_out, v_ada_w, v_ada_b, v_ln_g, v_ln_b, v_e_w_in, v_gmlp_norm_g, v_gmlp_norm_b, v_gmlp_ws, v_gmlp_bs, v_pool_w, v_pool_b, v_pool_scale, v_e_w_out, v_o_w_in, v_mla_q_norm_g, v_mla_kv_norm_g, v_mla_w_uq, v_mla_w_uk, v_mla_w_uv, v_o_w_out):
    args = dict(locals())
    weights = {n: args[n] for n in WEIGHTS}
    mom = {n: args["m_" + n] for n in WEIGHTS}
    var = {n: args["v_" + n] for n in WEIGHTS}
    ax, ay, ac = lax.axis_index("x"), lax.axis_index("y"), lax.axis_index("c")
    chip = 2 * ax + ay
    dev = 2 * chip + ac
    d = D_MODEL
    x2 = x[0]
    target = loss_target[0]
    q_rank_sh = mla_q_norm_g.shape[1]

    empty_zone = lambda w: lax.dynamic_update_slice(lax.empty((N_CHIPS,) + w.shape, w.dtype), w[None], (chip, 0, 0))
    shards0 = [w.astype(BF16) for w in (pool_w[0].reshape(-1, POOL_GROUP_DIM), e_w_out[0])]
    shards1 = [w.astype(BF16) for w in (o_w_in[0], mla_w_uq[0].reshape(q_rank_sh, -1), o_w_out[0])]
    w_in0, = _gather_weights([e_w_in[0].astype(BF16)], name="gather_weights")
    wuk_hrd = jnp.transpose(mla_w_uk[0], (1, 0, 2)).astype(BF16)
    wuk_hdr = jnp.transpose(mla_w_uk[0], (1, 2, 0)).astype(BF16)
    wuv_hrv = jnp.transpose(mla_w_uv[0], (1, 0, 2)).astype(BF16)
    wuv_hvr = jnp.transpose(mla_w_uv[0], (1, 2, 0)).astype(BF16)
    ws = gmlp_ws[0]
    ws_t = jnp.transpose(ws, (0, 2, 1))
    bs_t = _pad_cols(gmlp_bs[0].T, LANES)

    inv = 1.0 / (ROPE_THETA ** (jnp.arange(0, MLA_ROPE, 2, dtype=F32) / MLA_ROPE))
    ang = positions[0].astype(F32)[:, None] * inv
    cos_t = jnp.tile(jnp.cos(ang), (1, 4))
    sin_t = jnp.concatenate([-jnp.sin(ang), -jnp.sin(ang), jnp.sin(ang), jnp.sin(ang)], axis=1)

    c_all = _all_gather_devices(c.reshape(8, LANES), after=w_in0, name="gather_c").reshape(N_DEV, d)
    cols = ada_w.shape[2]
    ada_b_mine = lax.dynamic_slice_in_dim(ada_b, chip * cols, cols, axis=1)[:, None, :]
    mod_sh = _ada_mod(c_all, ada_w, ada_b_mine, name="ada_mod")
    q_norm_rows = jnp.zeros((8, cols), F32).at[0, :q_rank_sh].set(mla_q_norm_g[0])
    mod_all = _all_gather_chips(jnp.concatenate([mod_sh.reshape(2 * N_DEV, cols), q_norm_rows]), name="gather_mod")
    q_norm_g = mod_all[:, 2 * N_DEV, :q_rank_sh].reshape(1, -1)
    mod_all = jnp.transpose(mod_all[:, :2 * N_DEV].reshape(N_CHIPS, 2, N_DEV, cols), (1, 2, 0, 3)).reshape(2, N_DEV, 3 * d)
    mod = lax.dynamic_index_in_dim(mod_all, dev, axis=1, keepdims=False)
    shift = [mod[l:l + 1, :d] for l in range(2)]
    scale = [mod[l:l + 1, d:2 * d] for l in range(2)]
    gate = [mod[l:l + 1, 2 * d:] for l in range(2)]
    flight0 = _chips_start(shards0, [empty_zone(w) for w in shards0], mod, scatter=False, name="gather0_start")
    flight1 = _chips_start(shards1, [empty_zone(w) for w in shards1], flight0[3], scatter=False, name="gather1_start")

    scale[0] = scale[0] + flight1[3][:1, :1]
    h0 = _modulate(x2, scale[0], shift[0], name="modulate0")
    proj0 = _matmul(h0, w_in0, b_stacked=True, tm=1024, tn=1280, out_dtype=BF16, name="proj0")
    pool_w_g, w_out0 = _chips_wait(*flight0[:3], proj0, scatter=False, name="gather0_wait")
    pool_w_bf = jnp.transpose(pool_w_g.reshape(N_CHIPS, POOL_GROUPS, -1, POOL_GROUP_DIM), (1, 0, 2, 3)).reshape(
        POOL_GROUPS, POOL_GROUP_DIM, POOL_GROUP_DIM)
    w_out0 = w_out0.reshape(-1, d)
    mix0 = _even_fwd(proj0, ws, bs_t, gmlp_norm_g, gmlp_norm_b, pool_w_bf, pool_b, pool_scale, name="even_fwd")
    y0, x1, h1 = _out_resid_ln(mix0, w_out0, x2, gate[0], ln_g[0:1], ln_b[0:1], scale[1], shift[1], name="out0_ln")

    w_in1_g, w_uq_g, w_out1 = _chips_wait(*flight1[:3], h1, scatter=False, name="gather1_wait")
    w_out1 = w_out1.reshape(-1, d)
    w_in1 = jnp.transpose(w_in1_g, (1, 0, 2)).reshape(d, ODD_IN)
    w_in1 = jnp.concatenate([w_in1[:, ODD_SMALL:], _pad_cols(w_in1[:, :ODD_SMALL], ODD_SMALL_PAD)], axis=1)
    w_uq = w_uq_g.reshape(MLA_Q_RANK, MLA_HEADS, MLA_NOPE + MLA_ROPE)
    w_uq_nope = w_uq[:, :, :MLA_NOPE].reshape(MLA_Q_RANK, -1)
    w_uq_rope = jnp.transpose(w_uq[:, :, MLA_NOPE:].reshape(MLA_Q_RANK, MLA_HEADS // 2, 2, 2, ROPE_HALF),
                              (0, 1, 3, 2, 4)).reshape(MLA_Q_RANK, -1)
    proj1 = _matmul(h1, w_in1, tm=1024, tn=1280, out_dtype=BF16, name="proj1")
    q_cn, keys = _mla_prep(proj1, q_norm_g, mla_kv_norm_g, cos_t, sin_t, name="mla_prep")
    q_nope = _matmul(q_cn, w_uq_nope, tm=1024, tn=2048, name="q_nope", out_dtype=BF16)
    q_rope_pre = _matmul(q_cn, w_uq_rope, tm=1024, name="q_rope")
    q = _q_build(q_nope, q_rope_pre, wuk_hdr, cos_t, sin_t, name="q_build")
    o_lat, lse = _attn_fwd(q, keys, name="attn_fwd")
    og = _o_build(o_lat, wuv_hrv, proj1, name="o_build")

    dy1, dres1, g_ln_g1, g_ln_b1, dgate1, loss = _out_loss_ln_bwd(
        og, w_out1, x1, gate[1], ln_g[1:2], ln_b[1:2], target, name="out1_loss_ln")
    dg1 = _matmul(dy1, w_out1, trans_b=True, tn=2048, out_dtype=BF16, name="d_og")
    g_w_out1 = _matmul(og, dy1, trans_a=True, out_dtype=BF16, tm=1024, name="g_out1")
    do_lat, dz, g_uv = _o_bwd(dg1, proj1, o_lat, wuv_hrv, wuv_hvr, name="o_bwd")
    dq, dkeys = _attn_bwd(q, keys, do_lat, o_lat, lse, name="attn_bwd")
    dq_nope, dq_rope, g_uk = _q_bwd(dq, q_nope, wuk_hrd, cos_t, sin_t, name="q_bwd")
    dq_cn = (_matmul(dq_nope, w_uq_nope, trans_b=True, tm=1024, name="d_qcn_nope")
             + _matmul(dq_rope, w_uq_rope, trans_b=True, tm=1024, name="d_qcn_rope"))
    g_uq_nope = _matmul(q_cn, dq_nope, trans_a=True, out_dtype=BF16, tn=2048, name="g_uq_nope")
    g_uq_rope = _matmul(q_cn, dq_rope, trans_a=True, out_dtype=BF16, name="g_uq_rope")
    dsmall, g_qg, g_kvg = _mla_prep_bwd(proj1, dq_cn, dkeys, q_norm_g, mla_kv_norm_g, cos_t, sin_t, name="mla_prep_bwd")
    dproj1 = jnp.concatenate([dz, dsmall], axis=1)
    g_w_in1 =_matmul(h1, dproj1, trans_a=True, out_dtype=BF16, tm=1024, tn=1280, name="g_in1")

    g_uq_rope = jnp.transpose(g_uq_rope.reshape(MLA_Q_RANK, MLA_HEADS // 2, 2, 2, ROPE_HALF), (0, 1, 3, 2, 4))
    g_uq = jnp.concatenate([g_uq_nope.reshape(MLA_Q_RANK, MLA_HEADS, MLA_NOPE), g_uq_rope.reshape(MLA_Q_RANK, MLA_HEADS, MLA_ROPE)], axis=2)
    g_w_in1 = jnp.concatenate([g_w_in1[:, MLA_WIDTH:MLA_WIDTH + ODD_SMALL], g_w_in1[:, :MLA_WIDTH]], axis=1)
    g_w_in1 = jnp.transpose(g_w_in1.reshape(d, N_CHIPS, -1), (1, 0, 2))
    big1 = [
        _halves(g_w_in1),
        _halves(g_uq.reshape(N_CHIPS, q_rank_sh, -1)),
        _halves(g_w_out1.reshape(N_CHIPS, -1, d)),
        _halves(g_uk.astype(BF16).reshape(N_CHIPS, -1, MLA_NOPE)),
        _halves(g_uv.astype(BF16).reshape(N_CHIPS, -1, MLA_V)),
    ]
    parts1 = _reduce_sibling(big1, name="reduce_sibling1")
    lands2 = [lax.dynamic_update_slice(lax.empty(p.shape, BF16), lax.dynamic_slice_in_dim(p, chip, 1, axis=0), (chip, 0, 0))
              for p in parts1]
    flight2 = _chips_start(parts1, lands2, loss, scatter=True, name="reduce1_start")

    gate[0] = gate[0] + flight2[3][:1, :1]
    dy0, dres0, g_ln_g0, g_ln_b0, dgate0, dscale1, dshift1 = _dh_mid_ln_bwd(
        dproj1, w_in1, x2, y0, gate[0], ln_g[0:1], ln_b[0:1], dres1, scale[1], x1, name="d_h1_mid_ln")
    dmix0 = _matmul(dy0, w_out0, trans_b=True, tn=2048, out_dtype=BF16, name="d_mix0")
    g_w_out0 = _matmul(mix0, dy0, trans_a=True, out_dtype=BF16, tm=1024, name="g_out0")
    dproj0, g_ws, g_bs_t, g_ng, g_nb, g_pw, g_pb, g_ps = _even_bwd(
        proj0, dmix0, ws, ws_t, bs_t, gmlp_norm_g, gmlp_norm_b, pool_w_bf, pool_b, pool_scale, name="even_bwd")
    g_w_in0 = _matmul(h0, dproj0, trans_a=True, out_dtype=BF16, out_stacked=True, tm=1024, tn=1280, name="g_in0")

    g_pw = jnp.transpose(g_pw.astype(BF16).reshape(POOL_GROUPS, N_CHIPS, -1, POOL_GROUP_DIM), (1, 0, 2, 3))
    big0 = [
        _halves(g_w_in0),
        _halves(g_pw.reshape(N_CHIPS, -1, POOL_GROUP_DIM)),
        _halves(g_w_out0.reshape(N_CHIPS, -1, d)),
        _halves(g_ws.astype(BF16)),
    ]
    parts0 = _reduce_sibling(big0, name="reduce_sibling0")
    landed1 = _chips_wait(*flight2[:3], parts0[0], scatter=True, name="reduce1_wait")
    lands3 = [lax.dynamic_update_slice(lax.empty(p.shape, BF16), lax.dynamic_slice_in_dim(p, chip, 1, axis=0), (chip, 0, 0))
              for p in parts0]
    flight3 = _chips_start(parts0, lands3, landed1[0], scatter=True, name="reduce0_start")
    grad_x, dscale0, dshift0 = _dh_input_bwd(dproj0, w_in0, x2, dres0, scale[0], after=flight3[3], name="d_h0_input")

    small_local = {
        "ln_g": jnp.concatenate([g_ln_g0, g_ln_g1]), "ln_b": jnp.concatenate([g_ln_b0, g_ln_b1]),
        "gmlp_norm_g": g_ng, "gmlp_norm_b": g_nb, "gmlp_bs": g_bs_t[:, :GMLP_HEADS].T, "pool_b": g_pb, "pool_scale": g_ps,
        "mla_kv_norm_g": g_kvg, "mla_q_norm_g": g_qg,
    }
    n_mod = 2 * 3 * d
    vec = jnp.concatenate([dshift0, dscale0, dgate0, dshift1, dscale1, dgate1]
                          + [small_local[n].reshape(1, -1) for n in SMALL] + [loss], axis=1)
    n_vec = vec.shape[1]
    vec = _pad_cols(vec, -(-n_vec // (8 * LANES)) * 8 * LANES).reshape(-1, LANES)
    vec_all = _all_gather_devices(vec, name="gather_small")
    vec_sum = _sum_devices(vec_all, name="sum_small").reshape(-1)
    dmod_all = vec_all.reshape(N_DEV, -1)[:, :n_mod].reshape(N_DEV, 2, 3 * d)
    dmod_sh = jnp.transpose(lax.dynamic_slice_in_dim(dmod_all, chip * cols, cols, axis=2), (1, 0, 2))
    dmod_sh = jnp.concatenate([dmod_sh, jnp.zeros((2, LANES - N_DEV, cols), F32)], axis=1)
    grads = {"ada_w": _ada_grad(_pad_cols(c_all.T, LANES), dmod_sh, name="ada_grad"), "ada_b": vec_sum[:n_mod].reshape(2, 3 * d)}
    off = n_mod
    for n in SMALL:
        sz = small_local[n].size
        grads[n] = vec_sum[off:off + sz]
        off += sz
    grads["mla_q_norm_g"] = lax.dynamic_slice_in_dim(grads["mla_q_norm_g"], chip * q_rank_sh, q_rank_sh)
    for n in SMALL:
        grads[n] = grads[n].reshape(weights[n].shape)

    landed0 = _chips_wait(*flight3[:3], grads["ada_w"], scatter=True, name="reduce0_wait")
    totals = _reduce_chips([], list(landed0) + list(landed1), name="reduce_chips")
    for n, t in zip(("e_w_in", "pool_w", "e_w_out", "gmlp_ws", "o_w_in", "mla_w_uq", "o_w_out"), totals):
        if n != "gmlp_ws":
            grads[n] = t.reshape(weights[n].shape)
    rep = jnp.concatenate([t.reshape(-1, LANES) for t in (totals[3], totals[7], totals[8])])
    rep_land = lax.dynamic_update_slice(lax.empty((N_CHIPS,) + rep.shape, F32), rep[None], (chip, 0, 0))
    flight4 = _chips_start([rep], [rep_land], totals[0], scatter=False, name="gather_rep_start")

    delta, new_m, new_v = {}, {}, {}
    replicated = ("gmlp_ws", "mla_w_uk", "mla_w_uv")
    large = [n for n in WEIGHTS if n not in SMALL and n != "ada_b"]
    for n in large:
        if n not in replicated:
            delta[n], new_m[n], new_v[n] = _adamw(weights[n], grads[n], mom[n], var[n], after=flight4[3], name="adamw_" + n)
    rep = _chips_wait(*flight4[:3], delta["e_w_in"], scatter=False, name="gather_rep_wait")[0]
    r_ws, r_uk = GMLP_BLOCK, 4 * MLA_KV_RANK
    grads["gmlp_ws"] = rep[:, :r_ws].reshape(weights["gmlp_ws"].shape)
    grads["mla_w_uk"] = jnp.transpose(rep[:, r_ws:r_ws + r_uk].reshape(MLA_HEADS, MLA_KV_RANK, MLA_NOPE), (1, 0, 2))[None]
    grads["mla_w_uv"] = jnp.transpose(rep[:, r_ws + r_uk:].reshape(MLA_HEADS, MLA_KV_RANK, MLA_V), (1, 0, 2))[None]
    for n in replicated:
        delta[n], new_m[n], new_v[n] = _adamw(weights[n], grads[n], mom[n], var[n], name="adamw_" + n)
    small = [n for n in WEIGHTS if n not in large]
    ds, ms, vs = _adamw_small([weights[n] for n in small], [grads[n] for n in small], [mom[n] for n in small],
                              [var[n] for n in small], name="adamw_small")
    for n, dn, mn, vn in zip(small, ds, ms, vs):
        delta[n], new_m[n], new_v[n] = dn, mn, vn

    return (vec_sum[n_vec - 1], grad_x[None], *[grads[n] for n in WEIGHTS], *[delta[n] for n in WEIGHTS],
            *[new_m[n] for n in WEIGHTS], *[new_v[n] for n in WEIGHTS])
```

```python
import jax
import jax.numpy as jnp
from jax import lax
from jax.experimental import pallas as pl
from jax.experimental.pallas import tpu as pltpu

F32 = jnp.float32
BF16 = jnp.bfloat16
MESH = pl.DeviceIdType.MESH

D_MODEL = 1024
CHUNK = 64
LN_EPS = 1e-5
GMLP_HEADS = 4
GMLP_HEAD_DIM = 256
GMLP_BLOCK = 128
POOL_WINDOWS = (2, 4, 8, 16)
POOL_GROUPS = 4
POOL_GROUP_DIM = 256
POOL_HALO = 16
EVEN_IN = 5120
MLA_HEADS = 16
MLA_NOPE = 128
MLA_ROPE = 64
MLA_V = 128
MLA_Q_RANK = 256
MLA_KV_RANK = 128
MLA_WIDTH = MLA_HEADS * MLA_V
ODD_IN = 2496
ODD_SMALL = MLA_Q_RANK + MLA_KV_RANK + MLA_ROPE
ODD_SMALL_PAD = 512
QK_PAD = 256
ROPE_THETA = 10000.0
ATTN_SCALE = (MLA_NOPE + MLA_ROPE) ** -0.5
DEEPNORM_ALPHA = (2.0 * 2) ** 0.25
ADAM_LR = 0.001
ADAM_B1 = 0.9
ADAM_B2 = 0.999
ADAM_EPS = 1e-08
ADAM_WD = 0.01
ADAM_STEP = 10
NEG = -1e30
LANES = 128
N_DEV = 8
N_CHIPS = 4
VMEM_LIMIT_BYTES = 56 * 1024 * 1024
HBM = pl.BlockSpec(memory_space=pltpu.HBM)
VMEM = pl.BlockSpec(memory_space=pltpu.VMEM)


def _params(*sem):
    return pltpu.CompilerParams(dimension_semantics=sem if sem else None, vmem_limit_bytes=VMEM_LIMIT_BYTES)


def _tile(dim, pref):
    for t in (pref, 2048, 1280, 1024, 512, 256, 128):
        if t <= min(pref, dim) and dim % t == 0:
            return t
    return dim


def _sigmoid(z):
    return 1.0 / (1.0 + jnp.exp(-z))


def _dot(a, b, dims):
    return lax.dot_general(a, b, (dims, ((), ())), preferred_element_type=F32)


NN = ((1,), (0,))
NT = ((1,), (1,))
TN = ((0,), (0,))


def _matmul(a, b, *, name, trans_a=False, trans_b=False, out_dtype=F32, b_stacked=False, out_stacked=False,
            tm=512, tn=1024, tk=2048, after=None):
    k, m = a.shape if trans_a else a.shape[::-1]
    if b_stacked:
        ns, kb, n_sh = b.shape
        kb, n = (ns * n_sh, kb) if trans_b else (kb, ns * n_sh)
    else:
        n, kb = b.shape if trans_b else b.shape[::-1]
    assert k == kb, (a.shape, b.shape)
    tm = _tile(m, tm)
    if b_stacked and trans_b:
        tn, tk = _tile(n, tn), n_sh
    elif b_stacked or out_stacked:
        tn, tk = _tile(n // N_CHIPS, tn), _tile(k, tk)
    else:
        tn, tk = _tile(n, tn), _tile(k, tk)
    nk = k // tk
    per = max((n // N_CHIPS) // tn, 1)
    dims = ((0 if trans_a else 1,), (1 if trans_b else 0,))

    def body_one(a_ref, b_ref, *rest):
        o_ref = rest[-1]
        o_ref[...] = _dot(a_ref[...].astype(BF16), b_ref[...].astype(BF16), dims).astype(out_dtype)

    def body_acc(a_ref, b_ref, *rest):
        o_ref, acc_ref = rest[-2:]
        kk = pl.program_id(2)

        @pl.when(kk == 0)
        def _():
            acc_ref[...] = jnp.zeros_like(acc_ref)

        acc_ref[...] += _dot(a_ref[...].astype(BF16), b_ref[...].astype(BF16), dims)

        @pl.when(kk == nk - 1)
        def _():
            o_ref[...] = acc_ref[...].astype(out_dtype)

    a_spec = pl.BlockSpec((tk, tm), lambda i, j, kk: (kk, i)) if trans_a else pl.BlockSpec((tm, tk), lambda i, j, kk: (i, kk))
    if b_stacked and trans_b:
        b_spec = pl.BlockSpec((None, tn, tk), lambda i, j, kk: (kk, j, 0))
    elif b_stacked:
        b_spec = pl.BlockSpec((None, tk, tn), lambda i, j, kk: (j // per, kk, j % per))
    elif trans_b:
        b_spec = pl.BlockSpec((tn, tk), lambda i, j, kk: (j, kk))
    else:
        b_spec = pl.BlockSpec((tk, tn), lambda i, j, kk: (kk, j))
    if out_stacked:
        o_spec = pl.BlockSpec((None, tm, tn), lambda i, j, kk: (j // per, i, j % per))
        o_shape = jax.ShapeDtypeStruct((N_CHIPS, m, n // N_CHIPS), out_dtype)
    else:
        o_spec = pl.BlockSpec((tm, tn), lambda i, j, kk: (i, j))
        o_shape = jax.ShapeDtypeStruct((m, n), out_dtype)
    order = [] if after is None else [after]
    return pl.pallas_call(
        body_one if nk == 1 else body_acc, name=name, grid=(m // tm, n // tn, nk),
        in_specs=[a_spec, b_spec] + [pl.BlockSpec(memory_space=pl.ANY)] * len(order),
        out_specs=o_spec, out_shape=o_shape, scratch_shapes=[] if nk == 1 else [pltpu.VMEM((tm, tn), F32)],
        compiler_params=_params("parallel", "parallel", "arbitrary"),
    )(a, b, *order)


def _matmul_rows(a, b, epilogue, row_ins, vec_ins, row_outs, vec_outs, *, name, trans_b=False, b_stacked=False,
                 tm=512, tk=2048, after=None):
    m, k = a.shape
    if b_stacked:
        ns, n, n_sh = b.shape
        assert trans_b and ns * n_sh == k
        tk = n_sh
    else:
        n = b.shape[0] if trans_b else b.shape[1]
        tk = _tile(k, tk)
    tm = _tile(m, tm)
    nk = k // tk
    dims = ((1,), (1 if trans_b else 0,))
    n_ri, n_vi, n_ro, n_vo = len(row_ins), len(vec_ins), len(row_outs), len(vec_outs)
    order = [] if after is None else [after]

    def body(*refs):
        a_ref, b_ref = refs[:2]
        pos = 2
        rin = refs[pos:pos + n_ri]
        pos += n_ri
        vin = refs[pos:pos + n_vi]
        pos += n_vi + len(order)
        rout = refs[pos:pos + n_ro]
        pos += n_ro
        vout = refs[pos:pos + n_vo]
        first = pl.program_id(0) == 0
        part = _dot(a_ref[...].astype(BF16), b_ref[...].astype(BF16), dims)
        if nk == 1:
            epilogue(part, first, rin, vin, rout, vout)
        else:
            acc_ref = refs[-1]
            kk = pl.program_id(1)

            @pl.when(kk == 0)
            def _():
                acc_ref[...] = part

            @pl.when(kk > 0)
            def _():
                acc_ref[...] += part

            @pl.when(kk == nk - 1)
            def _():
                epilogue(acc_ref[...], first, rin, vin, rout, vout)

    a_spec = pl.BlockSpec((tm, tk), lambda i, kk: (i, kk))
    if b_stacked:
        b_spec = pl.BlockSpec((None, n, tk), lambda i, kk: (kk, 0, 0))
    elif trans_b:
        b_spec = pl.BlockSpec((n, tk), lambda i, kk: (0, kk))
    else:
        b_spec = pl.BlockSpec((tk, n), lambda i, kk: (kk, 0))
    row = pl.BlockSpec((tm, n), lambda i, kk: (i, 0))
    vec = lambda w: pl.BlockSpec((1, w), lambda i, kk: (0, 0))
    return pl.pallas_call(
        body, name=name, grid=(m // tm, nk),
        in_specs=[a_spec, b_spec] + [row] * n_ri + [vec(v.shape[1]) for v in vec_ins] + [pl.BlockSpec(memory_space=pl.ANY)] * len(order),
        out_specs=[row] * n_ro + [vec(w) for w in vec_outs],
        out_shape=[jax.ShapeDtypeStruct((m, n), dt) for dt in row_outs] + [jax.ShapeDtypeStruct((1, w), F32) for w in vec_outs],
        scratch_shapes=[] if nk == 1 else [pltpu.VMEM((tm, n), F32)],
        compiler_params=_params("arbitrary", "arbitrary"),
    )(a, b, *row_ins, *vec_ins, *order)


def _row_spec(ts, d):
    return pl.BlockSpec((ts, d), lambda i: (i, 0))


def _vec_spec(d):
    return pl.BlockSpec((1, d), lambda i: (0, 0))


def _modulate_matmul(x, scale, shift, w_stacked, *, name):
    s, d = x.shape
    ns, k, n_sh = w_stacked.shape
    assert k == d
    tm = _tile(s, 1024)

    def body(x_ref, sc_ref, sh_ref, w_ref, h_ref, o_ref):
        h = (x_ref[...] * (1.0 + sc_ref[...]) + sh_ref[...]).astype(BF16)

        @pl.when(pl.program_id(1) == 0)
        def _():
            h_ref[...] = h

        o_ref[...] = _dot(h, w_ref[...], NN).astype(BF16)

    vec = pl.BlockSpec((1, d), lambda i, j: (0, 0))
    return pl.pallas_call(
        body, name=name, grid=(s // tm, ns),
        in_specs=[pl.BlockSpec((tm, d), lambda i, j: (i, 0)), vec, vec, pl.BlockSpec((None, d, n_sh), lambda i, j: (j, 0, 0))],
        out_specs=[pl.BlockSpec((tm, d), lambda i, j: (i, 0)), pl.BlockSpec((tm, n_sh), lambda i, j: (i, j))],
        out_shape=[jax.ShapeDtypeStruct((s, d), BF16), jax.ShapeDtypeStruct((s, ns * n_sh), BF16)],
        compiler_params=_params("parallel", "arbitrary"),
    )(x, scale, shift, w_stacked)


def _ln_stats(pre):
    mu = jnp.mean(pre, axis=-1, keepdims=True)
    xc = pre - mu
    var = jnp.mean(xc * xc, axis=-1, keepdims=True)
    rstd = lax.rsqrt(var + LN_EPS)
    return xc * rstd, rstd


def _ln_bwd_rows(dout, xhat, rstd, g):
    dxh = dout * g
    m1 = jnp.mean(dxh, axis=-1, keepdims=True)
    m2 = jnp.mean(dxh * xhat, axis=-1, keepdims=True)
    return rstd * (dxh - m1 - xhat * m2)


def _colsum(v):
    return jnp.sum(v, axis=0, keepdims=True)


def _out_resid_ln(mix, w_out, x, gate, g, b, scale_next, shift_next, *, name):
    def epilogue(y, first, rin, vin, rout, vout):
        (x_ref,), (gate_ref, g_ref, b_ref, sc_ref, sh_ref), (y_ref, xn_ref, h_ref) = rin, vin, rout
        y_ref[...] = y
        pre = DEEPNORM_ALPHA * x_ref[...] + (1.0 + gate_ref[...]) * y
        xhat, _ = _ln_stats(pre)
        xn = xhat * g_ref[...] + b_ref[...]
        xn_ref[...] = xn
        h_ref[...] = (xn * (1.0 + sc_ref[...]) + sh_ref[...]).astype(BF16)

    return _matmul_rows(mix, w_out, epilogue, [x], [gate, g, b, scale_next, shift_next], [F32, F32, BF16], [], name=name)


def _out_loss_ln_bwd(og, w_out, x, gate, g, b, target, *, name):
    d = x.shape[1]

    def epilogue(yv, first, rin, vin, rout, vout):
        (x_ref, t_ref), (gate_ref, g_ref, b_ref), (dy_ref, dres_ref), (dg_ref, db_ref, dgate_ref, loss_ref) = rin, vin, rout, vout

        @pl.when(first)
        def _():
            for r in vout:
                r[...] = jnp.zeros_like(r)

        pre = DEEPNORM_ALPHA * x_ref[...] + (1.0 + gate_ref[...]) * yv
        xhat, rstd = _ln_stats(pre)
        diff = xhat * g_ref[...] + b_ref[...] - t_ref[...]
        loss_ref[...] += (0.5 / d) * jnp.sum(jnp.sum(diff * diff, axis=1, keepdims=True), axis=0, keepdims=True)
        dout = diff * (1.0 / d)
        dpre = _ln_bwd_rows(dout, xhat, rstd, g_ref[...])
        dy_ref[...] = (dpre * (1.0 + gate_ref[...])).astype(BF16)
        dres_ref[...] = DEEPNORM_ALPHA * dpre
        dg_ref[...] += _colsum(dout * xhat)
        db_ref[...] += _colsum(dout)
        dgate_ref[...] += _colsum(dpre * yv)

    return _matmul_rows(og, w_out, epilogue, [x, target], [gate, g, b], [BF16, F32], [d, d, d, 1], name=name)


def _dh_mid_ln_bwd(dproj, w_in, x, y, gate, g, b, dres_next, scale_next, x_next, *, name):
    d = x.shape[1]

    def epilogue(dh, first, rin, vin, rout, vout):
        (x_ref, y_ref, dr_ref, xn_ref), (gate_ref, g_ref, b_ref, sc_ref), (dy_ref, dres_ref) = rin, vin, rout
        dg_ref, db_ref, dgate_ref, dscale_ref, dshift_ref = vout

        @pl.when(first)
        def _():
            for r in vout:
                r[...] = jnp.zeros_like(r)

        dout = dr_ref[...] + dh * (1.0 + sc_ref[...])
        dscale_ref[...] += _colsum(dh * xn_ref[...])
        dshift_ref[...] += _colsum(dh)
        yv = y_ref[...]
        pre = DEEPNORM_ALPHA * x_ref[...] + (1.0 + gate_ref[...]) * yv
        xhat, rstd = _ln_stats(pre)
        dpre = _ln_bwd_rows(dout, xhat, rstd, g_ref[...])
        dy_ref[...] = (dpre * (1.0 + gate_ref[...])).astype(BF16)
        dres_ref[...] = DEEPNORM_ALPHA * dpre
        dg_ref[...] += _colsum(dout * xhat)
        db_ref[...] += _colsum(dout)
        dgate_ref[...] += _colsum(dpre * yv)

    return _matmul_rows(dproj, w_in, epilogue, [x, y, dres_next, x_next], [gate, g, b, scale_next], [BF16, F32], [d] * 5,
                        trans_b=True, tk=1280, name=name)


def _dh_input_bwd(dproj, w_in_stacked, x, dres, scale, *, name, after):
    d = x.shape[1]

    def epilogue(dh, first, rin, vin, rout, vout):
        (x_ref, dr_ref), (sc_ref,), (dx_ref,), (dscale_ref, dshift_ref) = rin, vin, rout, vout

        @pl.when(first)
        def _():
            for r in vout:
                r[...] = jnp.zeros_like(r)

        dx_ref[...] = dr_ref[...] + dh * (1.0 + sc_ref[...])
        dscale_ref[...] += _colsum(dh * x_ref[...])
        dshift_ref[...] += _colsum(dh)

    return _matmul_rows(dproj, w_in_stacked, epilogue, [x, dres], [scale], [F32], [d, d], trans_b=True, b_stacked=True,
                        tm=1024, after=after, name=name)


def _chunk_mask(transposed=False):
    r = lax.broadcasted_iota(jnp.int32, (GMLP_BLOCK, GMLP_BLOCK), 0) // CHUNK
    c = lax.broadcasted_iota(jnp.int32, (GMLP_BLOCK, GMLP_BLOCK), 1) // CHUNK
    return (r <= c) if transposed else (c <= r)


def _window_sum(ext, steps, forward):
    rows = ext.shape[0]
    acc = ext
    for k in range(steps):
        shift = 1 << k
        acc = acc + pltpu.roll(acc, (rows - shift) if forward else shift, 0)
    return acc


def _pool_counts(first_row, rows, win):
    t = first_row + lax.broadcasted_iota(jnp.int32, (rows, 1), 0)
    return jnp.minimum(t + 1, win).astype(F32)


def _even_specs(t):
    col = lambda j: pl.BlockSpec((t, D_MODEL), lambda n: (n, j))
    per = t // POOL_HALO
    prev = pl.BlockSpec((POOL_HALO, D_MODEL), lambda n: (jnp.maximum(n * per - 1, 0), 3))
    return col, per, prev


def _full(shape):
    return pl.BlockSpec(shape, lambda n: (0,) * len(shape))


def _gmlp_head(v_h, ng, nb, w_bf):
    xhat, rstd = _ln_stats(v_h)
    vn = (xhat * ng + nb).astype(BF16)
    return xhat, rstd, vn, _dot(w_bf, vn, NN)


def _pool_group(xb_g, prev_g, first_row, grp):
    t = xb_g.shape[0]
    ext = jnp.concatenate([prev_g, xb_g], axis=0)
    tot = _window_sum(ext, grp + 1, False)[POOL_HALO:, :]
    cnt = _pool_counts(first_row, t, POOL_WINDOWS[grp])
    return tot / cnt - xb_g, cnt


def _even_fwd(proj, ws, bs_t, ng, nb, pool_w, pool_b, pool_scale, *, name):
    s = proj.shape[0]
    t = GMLP_BLOCK
    col, per, prev = _even_specs(t)

    def body(u_ref, v_ref, za_ref, xb_ref, zb_ref, xp_ref, ws_ref, bs_ref, ng_ref, nb_ref, pw_ref, pb_ref, ps_ref, o_ref):
        n = pl.program_id(0)
        mask = _chunk_mask()
        for h in range(GMLP_HEADS):
            c0 = h * GMLP_HEAD_DIM
            cs = slice(c0, c0 + GMLP_HEAD_DIM)
            w_bf = jnp.where(mask, ws_ref[h], 0.0).astype(BF16)
            _, _, _, sv = _gmlp_head(v_ref[:, cs].astype(F32),ng_ref[...], nb_ref[...], w_bf)
            sv = sv + bs_ref[:, h:h + 1]
            za = za_ref[:, cs].astype(F32)
            o_ref[:, cs] = (u_ref[:, cs].astype(F32) * sv * (za * _sigmoid(za))).astype(BF16)
        live = (n > 0).astype(F32)
        for grp in range(POOL_GROUPS):
            c0 = grp * POOL_GROUP_DIM
            cs = slice(c0, c0 + POOL_GROUP_DIM)
            pooled, _ = _pool_group(xb_ref[:, cs].astype(F32), xp_ref[:, cs].astype(F32) * live, n * t, grp)
            yb = _dot(pooled.astype(BF16), pw_ref[grp], NN) + pb_ref[:, cs]
            zb = zb_ref[:, cs].astype(F32)
            o_ref[:, D_MODEL + c0:D_MODEL + c0 + POOL_GROUP_DIM] = (yb * ps_ref[:, cs] * (zb * _sigmoid(zb))).astype(BF16)

    return pl.pallas_call(
        body, name=name, grid=(s // t,),
        in_specs=[col(0), col(1), col(2), col(3), col(4), prev,
                  _full((GMLP_HEADS, t, t)), _full((t, LANES)), _full((1, GMLP_HEAD_DIM)), _full((1, GMLP_HEAD_DIM)),
                  _full((POOL_GROUPS, POOL_GROUP_DIM, POOL_GROUP_DIM)), _full((1, D_MODEL)), _full((1, D_MODEL))],
        out_specs=pl.BlockSpec((t, 2 * D_MODEL), lambda n: (n, 0)),
        out_shape=jax.ShapeDtypeStruct((s, 2 * D_MODEL), BF16),
        compiler_params=_params("parallel"),
    )(proj, proj, proj, proj, proj, proj, ws, bs_t, ng, nb, pool_w, pool_b, pool_scale)


def _even_bwd(proj, dmix, ws, ws_t, bs_t, ng, nb, pool_w, pool_b, pool_scale, *, name):
    s = proj.shape[0]
    t = GMLP_BLOCK
    nblk = s // t
    col, per, prev = _even_specs(t)
    nxt = lambda j: pl.BlockSpec((POOL_HALO, D_MODEL), lambda n: (jnp.minimum((n + 1) * per, nblk * per - 1), j))

    def body(u_ref, v_ref, za_ref, xb_ref, zb_ref, xp_ref, zn_ref, da_ref, db_ref, dbn_ref,
             ws_ref, wst_ref, bs_ref, ng_ref, nb_ref, pw_ref, pb_ref, ps_ref,
             dp_ref, gws_ref, gbs_ref, gng_ref, gnb_ref, gpw_ref, gpb_ref, gps_ref):
        n = pl.program_id(0)

        @pl.when(n == 0)
        def _():
            for r in (gws_ref, gbs_ref, gng_ref, gnb_ref, gpw_ref, gpb_ref, gps_ref):
                r[...] = jnp.zeros_like(r)

        mask, mask_t = _chunk_mask(), _chunk_mask(True)
        lane = lax.broadcasted_iota(jnp.int32, (t, LANES), 1)
        ngv, nbv = ng_ref[...], nb_ref[...]
        for h in range(GMLP_HEADS):
            c0 = h * GMLP_HEAD_DIM
            cs = slice(c0, c0 + GMLP_HEAD_DIM)
            w_bf = jnp.where(mask, ws_ref[h], 0.0).astype(BF16)
            wt_bf = jnp.where(mask_t, wst_ref[h], 0.0).astype(BF16)
            xhat, rstd, vn, sv = _gmlp_head(v_ref[:, cs].astype(F32),ngv, nbv, w_bf)
            sv = sv + bs_ref[:, h:h + 1]
            za, u, da = za_ref[:, cs].astype(F32), u_ref[:, cs].astype(F32), da_ref[:, cs].astype(F32)
            sg = _sigmoid(za)
            sl = za * sg
            dp_ref[:, cs] = (da * sv * sl).astype(BF16)
            dp_ref[:, 2 * D_MODEL + c0:2 * D_MODEL + c0 + GMLP_HEAD_DIM] = (
                da * u * sv * (sg * (1.0 + za * (1.0 - sg)))).astype(BF16)
            dsv = da * u * sl
            gbs_ref[...] += jnp.where(lane == h, jnp.sum(dsv, axis=1, keepdims=True), 0.0)
            dsv_bf = dsv.astype(BF16)
            gws_ref[h] += jnp.where(mask, _dot(dsv_bf, vn, NT), 0.0)
            dvn = _dot(wt_bf, dsv_bf, NN)
            dp_ref[:, D_MODEL + c0:D_MODEL + c0 + GMLP_HEAD_DIM] = _ln_bwd_rows(dvn, xhat, rstd, ngv).astype(BF16)
            gng_ref[...] += _colsum(dvn * xhat)
            gnb_ref[...] += _colsum(dvn)
        live_prev = (n > 0).astype(F32)
        live_next = (n < nblk - 1).astype(F32)
        for grp in range(POOL_GROUPS):
            c0 = grp * POOL_GROUP_DIM
            cs = slice(c0, c0 + POOL_GROUP_DIM)
            xb = xb_ref[:, cs].astype(F32)
            pooled, cnt = _pool_group(xb, xp_ref[:, cs].astype(F32) * live_prev, n * t, grp)
            pooled_bf = pooled.astype(BF16)
            pw = pw_ref[grp]
            yb = _dot(pooled_bf, pw, NN) + pb_ref[:, cs]
            ps = ps_ref[:, cs]
            zb, db = zb_ref[:, cs].astype(F32), db_ref[:, cs].astype(F32)
            sg = _sigmoid(zb)
            sl = zb * sg
            dp_ref[:, 4 * D_MODEL + c0:4 * D_MODEL + c0 + POOL_GROUP_DIM] = (
                db * yb * ps * (sg * (1.0 + zb * (1.0 - sg)))).astype(BF16)
            dsl = db * sl
            dy = dsl * ps
            gps_ref[:, cs] += _colsum(dsl * yb)
            gpb_ref[:, cs] += _colsum(dy)
            dy_bf = dy.astype(BF16)
            gpw_ref[grp] += _dot(pooled_bf, dy_bf, TN)
            r = _dot(dy_bf, pw, NT)
            zn = zn_ref[:, cs].astype(F32)
            dyn = (dbn_ref[:, cs].astype(F32) * (zn * _sigmoid(zn)) * ps * live_next).astype(BF16)
            rn = _dot(dyn, pw, NT) / _pool_counts((n + 1) * t, POOL_HALO, POOL_WINDOWS[grp])
            ext = jnp.concatenate([r / cnt, rn], axis=0)
            dxb = _window_sum(ext, grp + 1, True)[:t, :] - r
            dp_ref[:, 3 * D_MODEL + c0:3 * D_MODEL + c0 + POOL_GROUP_DIM] = dxb.astype(BF16)

    out_shape = [
        jax.ShapeDtypeStruct((s, EVEN_IN), BF16),
        jax.ShapeDtypeStruct((GMLP_HEADS, t, t), F32), jax.ShapeDtypeStruct((t, LANES), F32),
        jax.ShapeDtypeStruct((1, GMLP_HEAD_DIM), F32), jax.ShapeDtypeStruct((1, GMLP_HEAD_DIM), F32),
        jax.ShapeDtypeStruct((POOL_GROUPS, POOL_GROUP_DIM, POOL_GROUP_DIM), F32),
        jax.ShapeDtypeStruct((1, D_MODEL), F32), jax.ShapeDtypeStruct((1, D_MODEL), F32),
    ]
    return pl.pallas_call(
        body, name=name, grid=(nblk,),
        in_specs=[col(0), col(1), col(2), col(3), col(4), prev, nxt(4),
                  pl.BlockSpec((t, D_MODEL), lambda n: (n, 0)), pl.BlockSpec((t, D_MODEL), lambda n: (n, 1)), nxt(1),
                  _full((GMLP_HEADS, t, t)), _full((GMLP_HEADS, t, t)), _full((t, LANES)),
                  _full((1, GMLP_HEAD_DIM)), _full((1, GMLP_HEAD_DIM)),
                  _full((POOL_GROUPS, POOL_GROUP_DIM, POOL_GROUP_DIM)), _full((1, D_MODEL)), _full((1, D_MODEL))],
        out_specs=[pl.BlockSpec((t, EVEN_IN), lambda n: (n, 0))] + [_full(o.shape) for o in out_shape[1:]],
        out_shape=out_shape,
        compiler_params=_params("arbitrary"),
    )(proj, proj, proj, proj, proj, proj, proj, dmix, dmix, dmix, ws, ws_t, bs_t, ng, nb, pool_w, pool_b, pool_scale)


ROPE_HALF = MLA_ROPE // 2


def _rope(v, cos, sin_signed):
    return v * cos + pltpu.roll(v, 2 * ROPE_HALF, 1) * sin_signed


def _rope_bwd(d, cos, sin_signed):
    return d * cos + pltpu.roll(d * sin_signed, 2 * ROPE_HALF, 1)


def _slab_lanes(shape, which):
    lane = lax.broadcasted_iota(jnp.int32, shape, 1)
    return (lane // ROPE_HALF) % 2 == which


def _rms(v, g):
    r = lax.rsqrt(jnp.mean(v * v, axis=-1, keepdims=True) + LN_EPS)
    return v * r * g, r


def _rms_bwd(dy, v, r, g):
    u = dy * g
    return r * u - v * (r * r * r) * jnp.mean(u * v, axis=-1, keepdims=True)


def _mla_prep(proj, gq, gkv, cos, sin_signed, *, name):
    s = proj.shape[0]
    ts = _tile(s, 512)

    def body(p_ref, gq_ref, gkv_ref, c_ref, s_ref, q_ref, k_ref):
        qcn, _ = _rms(p_ref[:, :MLA_Q_RANK].astype(F32), gq_ref[...])
        kvn, _ = _rms(p_ref[:, MLA_Q_RANK:MLA_Q_RANK + MLA_KV_RANK].astype(F32), gkv_ref[...])
        kr = p_ref[:, MLA_Q_RANK + MLA_KV_RANK:].astype(F32)
        lane = lax.broadcasted_iota(jnp.int32, kr.shape, 1)
        by1, by2 = pltpu.roll(kr, ROPE_HALF, 1), pltpu.roll(kr, 2 * ROPE_HALF, 1)
        both = jnp.where(lane < ROPE_HALF, kr, jnp.where(lane < 3 * ROPE_HALF, by1, by2))
        kr = _rope(both, c_ref[...], s_ref[...])
        q_ref[...] = qcn.astype(BF16)
        k_ref[...] = jnp.concatenate([kvn, kr], axis=1).astype(BF16)

    return pl.pallas_call(
        body, name=name, grid=(s // ts,),
        in_specs=[_small_spec(ts), _vec_spec(MLA_Q_RANK), _vec_spec(MLA_KV_RANK), _row_spec(ts, LANES), _row_spec(ts, LANES)],
        out_specs=[_row_spec(ts, MLA_Q_RANK), _row_spec(ts, QK_PAD)],
        out_shape=[jax.ShapeDtypeStruct((s, MLA_Q_RANK), BF16), jax.ShapeDtypeStruct((s, QK_PAD), BF16)],
        compiler_params=_params("parallel"),
    )(proj, gq, gkv, cos, sin_signed)


def _mla_prep_bwd(proj, dqcn, dkv, gq, gkv, cos, sin_signed, *, name):
    s = proj.shape[0]
    ts = _tile(s, 512)

    def body(p_ref, dq_ref, dkv_ref, gq_ref, gkv_ref, c_ref, s_ref, ds_ref, ggq_ref, ggkv_ref):
        @pl.when(pl.program_id(0) == 0)
        def _():
            ggq_ref[...] = jnp.zeros_like(ggq_ref)
            ggkv_ref[...] = jnp.zeros_like(ggkv_ref)

        qc = p_ref[:, :MLA_Q_RANK].astype(F32)
        kvc = p_ref[:, MLA_Q_RANK:MLA_Q_RANK + MLA_KV_RANK].astype(F32)
        _, rq = _rms(qc, gq_ref[...])
        _, rkv = _rms(kvc, gkv_ref[...])
        dq = dq_ref[...]
        dkvn = dkv_ref[:, :MLA_KV_RANK]
        ggq_ref[...] += _colsum(dq * qc * rq)
        ggkv_ref[...] += _colsum(dkvn * kvc * rkv)
        dboth = _rope_bwd(dkv_ref[:, MLA_KV_RANK:], c_ref[...], s_ref[...])
        lane = lax.broadcasted_iota(jnp.int32, dboth.shape, 1)
        pair = dboth + pltpu.roll(dboth, 3 * ROPE_HALF, 1)
        dkr = jnp.where(lane < ROPE_HALF, pair, jnp.where(lane < 2 * ROPE_HALF, pltpu.roll(pair, 3 * ROPE_HALF, 1), 0.0))
        ds_ref[...] = jnp.concatenate(
            [_rms_bwd(dq, qc, rq, gq_ref[...]), _rms_bwd(dkvn, kvc, rkv, gkv_ref[...]), dkr], axis=1).astype(BF16)

    return pl.pallas_call(
        body, name=name, grid=(s // ts,),
        in_specs=[_small_spec(ts), _row_spec(ts, MLA_Q_RANK), _row_spec(ts, QK_PAD),
                  _vec_spec(MLA_Q_RANK), _vec_spec(MLA_KV_RANK), _row_spec(ts, LANES), _row_spec(ts, LANES)],
        out_specs=[_row_spec(ts, ODD_SMALL_PAD), _vec_spec(MLA_Q_RANK), _vec_spec(MLA_KV_RANK)],
        out_shape=[jax.ShapeDtypeStruct((s, ODD_SMALL_PAD), BF16), jax.ShapeDtypeStruct((1, MLA_Q_RANK), F32),
                   jax.ShapeDtypeStruct((1, MLA_KV_RANK), F32)],
        compiler_params=_params("arbitrary"),
    )(proj, dqcn, dkv, gq, gkv, cos, sin_signed)


Q_HEAD_GROUP = 8
LOG2_E = 1.4426950408889634
Q_PRESCALE = ATTN_SCALE * LOG2_E


def _q_build(q_nope, q_rope_pre, wuk_hdr, cos, sin_signed, *, name):
    s = q_nope.shape[0]
    ts = _tile(s, 512)
    hg = Q_HEAD_GROUP

    def body(qn_ref, qr_ref, w_ref, c_ref, s_ref, o_ref):
        for pair in range(hg // 2):
            r = _rope(qr_ref[:, pair * LANES:(pair + 1) * LANES], c_ref[...], s_ref[...])
            for j in range(2):
                h = 2 * pair + j
                ql = _dot(qn_ref[:, h * MLA_NOPE:(h + 1) * MLA_NOPE], w_ref[h], NN)
                mine = jnp.where(_slab_lanes(r.shape, j), r, 0.0)
                o_ref[h] = (jnp.concatenate([ql, mine], axis=1) * Q_PRESCALE).astype(BF16)

    return pl.pallas_call(
        body, name=name, grid=(s // ts, MLA_HEADS // hg),
        in_specs=[pl.BlockSpec((ts, hg * MLA_NOPE), lambda i, p: (i, p)), pl.BlockSpec((ts, hg * MLA_ROPE), lambda i, p: (i, p)),
                  pl.BlockSpec((hg, MLA_NOPE, MLA_KV_RANK), lambda i, p: (p, 0, 0)),
                  pl.BlockSpec((ts, LANES), lambda i, p: (i, 0)), pl.BlockSpec((ts, LANES), lambda i, p: (i, 0))],
        out_specs=pl.BlockSpec((hg, ts, QK_PAD), lambda i, p: (p, i, 0)),
        out_shape=jax.ShapeDtypeStruct((MLA_HEADS, s, QK_PAD), BF16),
        compiler_params=_params("parallel", "parallel"),
    )(q_nope, q_rope_pre, wuk_hdr, cos, sin_signed)


def _q_bwd(dq, q_nope, wuk_hrd, cos, sin_signed, *, name):
    s = q_nope.shape[0]
    ts = _tile(s, 512)
    hg = Q_HEAD_GROUP

    nope_w, all_w = hg * MLA_NOPE, hg * (MLA_NOPE + MLA_ROPE)

    def body(dq_ref, qn_ref, w_ref, c_ref, s_ref, dall_ref, gw_ref):
        @pl.when(pl.program_id(1) == 0)
        def _():
            gw_ref[...] = jnp.zeros_like(gw_ref)

        for h in range(hg):
            dql = dq_ref[h, :, :MLA_KV_RANK]
            dall_ref[:, h * MLA_NOPE:(h + 1) * MLA_NOPE] = _dot(dql, w_ref[h], NN).astype(BF16)
            gw_ref[h] += _dot(dql, qn_ref[:, h * MLA_NOPE:(h + 1) * MLA_NOPE], TN)
        for pair in range(hg // 2):
            hi0 = dq_ref[2 * pair, :, MLA_KV_RANK:].astype(F32)
            hi1 = dq_ref[2 * pair + 1, :, MLA_KV_RANK:].astype(F32)
            d = jnp.where(_slab_lanes(hi0.shape, 0), hi0, hi1)
            dall_ref[:, nope_w + pair * LANES:nope_w + (pair + 1) * LANES] = _rope_bwd(d, c_ref[...], s_ref[...]).astype(BF16)

    return pl.pallas_call(
        body, name=name, grid=(MLA_HEADS // hg, s // ts),
        in_specs=[pl.BlockSpec((hg, ts, QK_PAD), lambda p, i: (p, i, 0)), pl.BlockSpec((ts, nope_w), lambda p, i: (i, p)),
                  pl.BlockSpec((hg, MLA_KV_RANK, MLA_NOPE), lambda p, i: (p, 0, 0)),
                  pl.BlockSpec((ts, LANES), lambda p, i: (i, 0)), pl.BlockSpec((ts, LANES), lambda p, i: (i, 0))],
        out_specs=[pl.BlockSpec((ts, all_w), lambda p, i: (i, p)),
                   pl.BlockSpec((hg, MLA_KV_RANK, MLA_NOPE), lambda p, i: (p, 0, 0))],
        out_shape=[jax.ShapeDtypeStruct((s, MLA_HEADS * (MLA_NOPE + MLA_ROPE)), BF16),
                   jax.ShapeDtypeStruct((MLA_HEADS, MLA_KV_RANK, MLA_NOPE), F32)],
        compiler_params=_params("parallel", "arbitrary"),
    )(dq, q_nope, wuk_hrd, cos, sin_signed)


ATTN_BQ = 128
ATTN_BK = 512


def _diag_mask(rows, bq, bk, q0, k0):
    qc = (q0 + lax.broadcasted_iota(jnp.int32, (rows, bk), 0) % bq) // CHUNK
    kc = (k0 + lax.broadcasted_iota(jnp.int32, (rows, bk), 1)) // CHUNK
    return kc <= qc


def _attn_fwd(q, k, *, name):
    nh, s, dk = q.shape
    bq, bk = _tile(s, ATTN_BQ), _tile(s, ATTN_BK)
    rows = nh * bq

    def body(q_ref, k_ref, o_ref, lse_ref):
        i = pl.program_id(0)
        qb = q_ref[...].reshape(rows, dk)
        n_before = (i * bq) // bk

        def step(j, width, carry, masked):
            m, l, acc = carry
            k0 = pl.multiple_of(j * bk, bk)
            kb = k_ref[pl.ds(k0, width), :]
            sc = _dot(qb, kb, NT)
            if masked:
                sc = jnp.where(_diag_mask(rows, bq, width, i * bq, k0), sc, NEG)
            m_new = jnp.maximum(m, jnp.max(sc, axis=1, keepdims=True))
            p = jnp.exp2(sc - m_new)
            a = jnp.exp2(m - m_new)
            l = a * l + jnp.sum(p, axis=1, keepdims=True)
            acc = a * acc + _dot(p.astype(BF16), kb[:, :MLA_KV_RANK], NN)
            return m_new, l, acc

        init = (jnp.full((rows, 1), NEG, F32), jnp.zeros((rows, 1), F32), jnp.zeros((rows, MLA_KV_RANK), F32))
        carry = lax.fori_loop(0, n_before, lambda j, c: step(j, bk, c, False), init)
        for part in range(bk // bq):
            @pl.when(i % (bk // bq) == part)
            def _(part=part):
                m, l, acc = step(n_before, (part + 1) * bq, carry, True)
                o_ref[...] = (acc / l).astype(BF16).reshape(nh, bq, MLA_KV_RANK)
                lse_ref[...] = jnp.broadcast_to(m + jnp.log2(l), (rows, LANES)).reshape(nh, bq, LANES)

    return pl.pallas_call(
        body, name=name, grid=(s // bq,),
        in_specs=[pl.BlockSpec((nh, bq, dk), lambda i: (0, i, 0)), pl.BlockSpec((s, dk), lambda i: (0, 0))],
        out_specs=[pl.BlockSpec((nh, bq, MLA_KV_RANK), lambda i: (0, i, 0)), pl.BlockSpec((nh, bq, LANES), lambda i: (0, i, 0))],
        out_shape=[jax.ShapeDtypeStruct((nh, s, MLA_KV_RANK), BF16), jax.ShapeDtypeStruct((nh, s, LANES), F32)],
        compiler_params=_params("parallel"),
    )(q, k)


def _attn_bwd(q, k, do, o, lse, *, name):
    nh, s, dk = q.shape
    bq, bk = _tile(s, ATTN_BQ), _tile(s, ATTN_BK)
    rows = nh * bq

    def body(q_ref, k_ref, do_ref, o_ref, lse_ref, dq_ref, dkv_ref):
        i = pl.program_id(0)
        n_before = (i * bq) // bk

        @pl.when(i == 0)
        def _():
            dkv_ref[...] = jnp.zeros_like(dkv_ref)

        qb = q_ref[...].reshape(rows, dk)
        dob = do_ref[...].reshape(rows, MLA_KV_RANK)
        lse_b = lse_ref[...].reshape(rows, LANES)[:, :1]
        delta = jnp.sum(dob.astype(F32) * o_ref[...].reshape(rows, MLA_KV_RANK).astype(F32), axis=1, keepdims=True)

        def step(j, width, dq, masked):
            j0 = pl.multiple_of(j * bk, bk)
            kb = k_ref[pl.ds(j0, width), :]
            sc = _dot(qb, kb, NT)
            if masked:
                sc = jnp.where(_diag_mask(rows, bq, width, i * bq, j0), sc, NEG)
            p = jnp.exp2(sc - lse_b)
            dp = _dot(dob, kb[:, :MLA_KV_RANK], NT)
            ds_bf = (p * (dp - delta)).astype(BF16)
            dkv_ref[pl.ds(j0, width), :] += _dot(ds_bf, qb, TN) * (1.0 / LOG2_E)
            dkv_ref[pl.ds(j0, width), :MLA_KV_RANK] += _dot(p.astype(BF16), dob, TN)
            return dq + _dot(ds_bf, kb, NN)

        dq_before = lax.fori_loop(0, n_before, lambda j, c: step(j, bk, c, False), jnp.zeros((rows, dk), F32))
        for part in range(bk // bq):
            @pl.when(i % (bk // bq) == part)
            def _(part=part):
                dq = step(n_before, (part + 1) * bq, dq_before, True) * ATTN_SCALE
                dq_ref[...] = dq.astype(BF16).reshape(nh, bq, dk)

    blk = lambda w: pl.BlockSpec((nh, bq, w), lambda i: (0, i, 0))
    return pl.pallas_call(
        body, name=name, grid=(s // bq,),
        in_specs=[blk(dk), pl.BlockSpec((s, dk), lambda i: (0, 0)), blk(MLA_KV_RANK), blk(MLA_KV_RANK), blk(LANES)],
        out_specs=[blk(dk), pl.BlockSpec((s, dk), lambda i: (0, 0))],
        out_shape=[jax.ShapeDtypeStruct((nh, s, dk), BF16), jax.ShapeDtypeStruct((s, dk), F32)],
        compiler_params=_params("arbitrary"),
    )(q, k, do, o, lse)


HEAD_GROUP = 4
SMALL_BLOCK = MLA_WIDTH // ODD_SMALL_PAD


def _small_spec(ts):
    return pl.BlockSpec((ts, ODD_SMALL_PAD), lambda i: (i, SMALL_BLOCK))


def _o_build(o_lat, wuv_hrv, proj, *, name):
    s = proj.shape[0]
    ts = _tile(s, 1024)
    w = HEAD_GROUP * MLA_V

    def body(ol_ref, w_ref, z_ref, og_ref):
        for j in range(HEAD_GROUP):
            cs = slice(j * MLA_V, (j + 1) * MLA_V)
            z = z_ref[:, cs].astype(F32)
            og_ref[:, cs] = (_dot(ol_ref[j], w_ref[j], NN) * (z * _sigmoid(z))).astype(BF16)

    return pl.pallas_call(
        body, name=name, grid=(s // ts, MLA_HEADS // HEAD_GROUP),
        in_specs=[pl.BlockSpec((HEAD_GROUP, ts, MLA_KV_RANK), lambda i, g: (g, i, 0)),
                  pl.BlockSpec((HEAD_GROUP, MLA_KV_RANK, MLA_V), lambda i, g: (g, 0, 0)),
                  pl.BlockSpec((ts, w), lambda i, g: (i, g))],
        out_specs=pl.BlockSpec((ts, w), lambda i, g: (i, g)),
        out_shape=jax.ShapeDtypeStruct((s, MLA_WIDTH), BF16),
        compiler_params=_params("parallel", "parallel"),
    )(o_lat, wuv_hrv, proj)


def _o_bwd(dg, proj, o_lat, wuv_hrv, wuv_hvr, *, name):
    s = proj.shape[0]
    ts = _tile(s, 1024)
    w = HEAD_GROUP * MLA_V

    def body(dg_ref, z_ref, ol_ref, w_ref, wt_ref, dol_ref, dz_ref, gw_ref):
        @pl.when(pl.program_id(1) == 0)
        def _():
            gw_ref[...] = jnp.zeros_like(gw_ref)

        for j in range(HEAD_GROUP):
            cs = slice(j * MLA_V, (j + 1) * MLA_V)
            z, dgj, ol = z_ref[:, cs].astype(F32), dg_ref[:, cs].astype(F32), ol_ref[j]
            sg = _sigmoid(z)
            o = _dot(ol, w_ref[j], NN)
            dz_ref[:, cs] = (dgj * o * (sg * (1.0 + z * (1.0 - sg)))).astype(BF16)
            do_bf = (dgj * (z * sg)).astype(BF16)
            dol_ref[j] = _dot(do_bf, wt_ref[j], NN).astype(BF16)
            gw_ref[j] += _dot(ol, do_bf, TN)

    hs = lambda a, b: pl.BlockSpec((HEAD_GROUP, a, b), lambda g, i: (g, 0, 0))
    return pl.pallas_call(
        body, name=name, grid=(MLA_HEADS // HEAD_GROUP, s // ts),
        in_specs=[pl.BlockSpec((ts, w), lambda g, i: (i, g)), pl.BlockSpec((ts, w), lambda g, i: (i, g)),
                  pl.BlockSpec((HEAD_GROUP, ts, MLA_KV_RANK), lambda g, i: (g, i, 0)),
                  hs(MLA_KV_RANK, MLA_V), hs(MLA_V, MLA_KV_RANK)],
        out_specs=[pl.BlockSpec((HEAD_GROUP, ts, MLA_KV_RANK), lambda g, i: (g, i, 0)),
                   pl.BlockSpec((ts, w), lambda g, i: (i, g)), hs(MLA_KV_RANK, MLA_V)],
        out_shape=[jax.ShapeDtypeStruct((MLA_HEADS, s, MLA_KV_RANK), BF16), jax.ShapeDtypeStruct((s, MLA_WIDTH), BF16),
                   jax.ShapeDtypeStruct((MLA_HEADS, MLA_KV_RANK, MLA_V), F32)],
        compiler_params=_params("parallel", "arbitrary"),
    )(dg, proj, o_lat, wuv_hrv, wuv_hvr)


def _ada_mod(c_all, ada_w, ada_b_sh, *, name):
    nl, _, cols = ada_w.shape

    def body(c_ref, w_ref, b_ref, o_ref):
        c = c_ref[...]
        cond = (c * _sigmoid(c)).astype(BF16)
        for l in range(nl):
            o_ref[l] = _dot(cond, w_ref[l].astype(BF16), NN) + b_ref[l]

    return pl.pallas_call(
        body, name=name, out_shape=jax.ShapeDtypeStruct((nl, c_all.shape[0], cols), F32),
        compiler_params=_params(),
    )(c_all, ada_w, ada_b_sh)


def _ada_grad(c_all_t, dmod_sh, *, name):
    nl, _, cols = dmod_sh.shape
    d = c_all_t.shape[0]

    def body(c_ref, dm_ref, gw_ref):
        c = c_ref[...]
        cond_t = c * _sigmoid(c)
        for l in range(nl):
            gw_ref[l] = lax.dot_general(cond_t, dm_ref[l], (NN, ((), ())), precision=lax.Precision.HIGHEST,
                                        preferred_element_type=F32)

    return pl.pallas_call(
        body, name=name, out_shape=jax.ShapeDtypeStruct((nl, d, cols), F32), compiler_params=_params(),
    )(c_all_t, dmod_sh)


def _sum_devices(parts, *, name):
    def body(p_ref, o_ref):
        acc = p_ref[0]
        for k in range(1, parts.shape[0]):
            acc = acc + p_ref[k]
        o_ref[...] = acc

    return pl.pallas_call(body, name=name, out_shape=jax.ShapeDtypeStruct(parts.shape[1:], F32), compiler_params=_params())(parts)


def _adamw_math(w, g, m, v):
    c1 = 1.0 - ADAM_B1 ** ADAM_STEP
    c2 = 1.0 - ADAM_B2 ** ADAM_STEP
    nm = ADAM_B1 * m + (1.0 - ADAM_B1) * g
    nv = ADAM_B2 * v + (1.0 - ADAM_B2) * (g * g)
    return -ADAM_LR * ((nm / c1) / (jnp.sqrt(nv / c2) + ADAM_EPS) + ADAM_WD * w), nm, nv


ADAMW_BLOCK_BYTES = 1 << 20


def _adamw(w, g, m, v, *, name, after=None):
    shape = w.shape
    a, b = shape[-2], shape[-1]
    lead = 1
    for dim in shape[:-2]:
        lead *= dim
    row_bytes = 4 * b
    if a * row_bytes <= ADAMW_BLOCK_BYTES:
        ta = a
        tl = max(1, min(lead, ADAMW_BLOCK_BYTES // (a * row_bytes)))
        while lead % tl:
            tl -= 1
    else:
        tl = 1
        ta = _tile(a, 256)
    to3 = lambda t: t.reshape(lead, a, b)

    def body(w_ref, g_ref, m_ref, v_ref, *rest):
        d_ref, nm_ref, nv_ref = rest[-3:]
        d_ref[...], nm_ref[...], nv_ref[...] = _adamw_math(w_ref[...], g_ref[...], m_ref[...], v_ref[...])

    spec = pl.BlockSpec((tl, ta, b), lambda i, j: (i, j, 0))
    out = jax.ShapeDtypeStruct((lead, a, b), F32)
    order = [] if after is None else [after]
    res = pl.pallas_call(
        body, name=name, grid=(lead // tl, a // ta), in_specs=[spec] * 4 + [pl.BlockSpec(memory_space=pl.ANY)] * len(order),
        out_specs=[spec] * 3, out_shape=[out] * 3, compiler_params=_params("parallel", "parallel"),
    )(to3(w), to3(g), to3(m), to3(v), *order)
    return [r.reshape(shape) for r in res]


def _adamw_small(ws, gs, ms, vs, *, name):
    n = len(ws)

    def body(*refs):
        for k in range(n):
            w_ref, g_ref, m_ref, v_ref = (refs[j * n + k] for j in range(4))
            d_ref, nm_ref, nv_ref = (refs[(4 + j) * n + k] for j in range(3))
            d_ref[...], nm_ref[...], nv_ref[...] = _adamw_math(w_ref[...], g_ref[...], m_ref[...], v_ref[...])

    outs = [jax.ShapeDtypeStruct(w.shape, F32) for w in ws]
    res = pl.pallas_call(body, name=name, out_shape=outs * 3, compiler_params=_params())(*ws, *gs, *ms, *vs)
    return res[:n], res[n:2 * n], res[2 * n:]


def _flip(v, bit):
    return 1 - v if bit else v


CHIP_DELTAS = ((1, 0), (0, 1), (1, 1))
SUM_ROWS = 32


def _all_gather_chips(shard, *, name):
    def body(x_ref, o_ref, send_sems, recv_sems, local_sem):
        x, y, c = lax.axis_index("x"), lax.axis_index("y"), lax.axis_index("c")
        mine = pltpu.make_async_copy(x_ref, o_ref.at[2 * x + y], local_sem)
        mine.start()

        def copy(k):
            tx, ty = _flip(x, CHIP_DELTAS[k][0]), _flip(y, CHIP_DELTAS[k][1])
            send = pltpu.make_async_remote_copy(src_ref=x_ref, dst_ref=o_ref.at[2 * x + y], send_sem=send_sems.at[k],
                                                recv_sem=recv_sems.at[k], device_id=(tx, ty, c), device_id_type=MESH)
            recv = pltpu.make_async_remote_copy(src_ref=x_ref, dst_ref=o_ref.at[2 * tx + ty], send_sem=send_sems.at[k],
                                                recv_sem=recv_sems.at[k], device_id=(tx, ty, c), device_id_type=MESH)
            return send, recv

        pairs = [copy(k) for k in range(3)]
        for send, _ in pairs:
            send.start()
        for _, recv in pairs:
            recv.wait_recv()
        for send, _ in pairs:
            send.wait_send()
        mine.wait()

    return pl.pallas_call(
        body, name=name, out_shape=jax.ShapeDtypeStruct((N_CHIPS,) + shard.shape, shard.dtype),
        in_specs=[HBM], out_specs=HBM,
        scratch_shapes=[pltpu.SemaphoreType.DMA((3,)), pltpu.SemaphoreType.DMA((3,)), pltpu.SemaphoreType.DMA(())],
    )(shard)


def _gather_weights(shards, *, name):
    n = len(shards)

    def body(*refs):
        w_refs, o_refs = refs[:n], refs[n:2 * n]
        ici_send, ici_recv, d2d_send, d2d_recv, local_sems = refs[2 * n:]
        x, y, c = lax.axis_index("x"), lax.axis_index("y"), lax.axis_index("c")
        me = 2 * x + y
        peers = [(_flip(x, dx), _flip(y, dy)) for dx, dy in CHIP_DELTAS]
        locals_ = [pltpu.make_async_copy(w_refs[k], o_refs[k].at[me], local_sems.at[k]) for k in range(n)]
        for cp in locals_:
            cp.start()

        def rows(k, which):
            half = shards[k].shape[0] // 2
            return pl.ds(pl.multiple_of(which * half, half), half)

        def over_chips(k, d, slot):
            tx, ty = peers[d]
            return pltpu.make_async_remote_copy(
                src_ref=w_refs[k].at[rows(k, c)], dst_ref=o_refs[k].at[slot, rows(k, c)], send_sem=ici_send.at[k, d],
                recv_sem=ici_recv.at[k, d], device_id=(tx, ty, c), device_id_type=MESH)

        def to_sibling(k, d, which):
            tx, ty = peers[d]
            at = o_refs[k].at[2 * tx + ty, rows(k, which)]
            return pltpu.make_async_remote_copy(src_ref=at, dst_ref=at, send_sem=d2d_send.at[k, d], recv_sem=d2d_recv.at[k, d],
                                                device_id=(x, y, 1 - c), device_id_type=MESH)

        sends = [over_chips(k, d, me) for k in range(n) for d in range(3)]
        for cp in sends:
            cp.start()
        passed = []
        for k in range(n):
            for d in range(3):
                over_chips(k, d, 2 * peers[d][0] + peers[d][1]).wait_recv()
                passed.append(to_sibling(k, d, c))
                passed[-1].start()
        for k in range(n):
            for d in range(3):
                to_sibling(k, d, 1 - c).wait_recv()
        for cp in sends + passed:
            cp.wait_send()
        for cp in locals_:
            cp.wait()

    return pl.pallas_call(
        body, name=name, out_shape=[jax.ShapeDtypeStruct((N_CHIPS,) + w.shape, w.dtype) for w in shards],
        in_specs=[HBM] * n, out_specs=[HBM] * n,
        scratch_shapes=[pltpu.SemaphoreType.DMA((n, 3))] * 4 + [pltpu.SemaphoreType.DMA((n,))],
    )(*shards)


def _add_into(dst_ref, src_ref):
    ns, r, _ = dst_ref.shape
    step = SUM_ROWS if r % SUM_ROWS == 0 else r
    for s in range(ns):
        def tile(t, carry):
            at = pl.ds(pl.multiple_of(t * step, step), step)
            dst_ref[s, at, :] = (dst_ref[s, at, :].astype(F32) + src_ref[s, at, :].astype(F32)).astype(dst_ref.dtype)
            return carry
        lax.fori_loop(0, r // step, tile, 0)


def _reduce_sibling(grads, *, name):
    n = len(grads)

    def body(*refs):
        g_refs, o_refs = refs[:n], refs[n:2 * n]
        mine, got = refs[2 * n:3 * n], refs[3 * n:4 * n]
        send_sems, recv_sems, load_sems, store_sems = refs[4 * n:]
        x, y, c = lax.axis_index("x"), lax.axis_index("y"), lax.axis_index("c")
        loads = [pltpu.make_async_copy(g_refs[k].at[:, c], mine[k], load_sems.at[k]) for k in range(n)]
        swaps = [pltpu.make_async_remote_copy(src_ref=g_refs[k].at[:, 1 - c], dst_ref=got[k], send_sem=send_sems.at[k],
                                              recv_sem=recv_sems.at[k], device_id=(x, y, 1 - c), device_id_type=MESH)
                 for k in range(n)]
        for cp in loads + swaps:
            cp.start()
        stores = []
        for k in range(n):
            loads[k].wait()
            swaps[k].wait_recv()
            _add_into(mine[k], got[k])
            stores.append(pltpu.make_async_copy(mine[k], o_refs[k], store_sems.at[k]))
            stores[-1].start()
        for k in range(n):
            swaps[k].wait_send()
            stores[k].wait()

    half = [jax.ShapeDtypeStruct((g.shape[0],) + g.shape[2:], g.dtype) for g in grads]
    return pl.pallas_call(
        body, name=name, out_shape=half, in_specs=[HBM] * n, out_specs=[HBM] * n,
        scratch_shapes=[pltpu.VMEM(h.shape, h.dtype) for h in half] * 2 + [pltpu.SemaphoreType.DMA((n,))] * 4,
        compiler_params=_params(),
    )(*grads)


def _reduce_chips(parts, landed, *, name):
    n_send = len(parts)
    n = n_send + len(landed)

    def body(*refs):
        p_refs, o_refs = refs[:n], refs[n:2 * n]
        got, total = refs[2 * n:3 * n], refs[3 * n:4 * n]
        send_sems, recv_sems, load_sems, share_send, share_recv, store_sems = refs[4 * n:]
        x, y, c = lax.axis_index("x"), lax.axis_index("y"), lax.axis_index("c")
        me = 2 * x + y
        peers = [(_flip(x, dx), _flip(y, dy)) for dx, dy in CHIP_DELTAS]

        def over_chips(k, d, src_slot, dst_slot):
            tx, ty = peers[d]
            return pltpu.make_async_remote_copy(
                src_ref=p_refs[k].at[src_slot], dst_ref=got[k].at[dst_slot], send_sem=send_sems.at[k, d],
                recv_sem=recv_sems.at[k, d], device_id=(tx, ty, c), device_id_type=MESH)

        loads = [pltpu.make_async_copy(p_refs[k].at[me], got[k].at[me], load_sems.at[k]) for k in range(n_send)]
        loads += [pltpu.make_async_copy(p_refs[k], got[k], load_sems.at[k]) for k in range(n_send, n)]
        sends = [over_chips(k, d, 2 * peers[d][0] + peers[d][1], me) for k in range(n_send) for d in range(3)]
        for cp in loads + sends:
            cp.start()
        shares, stores = [], []
        for k in range(n):
            loads[k].wait()
            for d in range(3 if k < n_send else 0):
                slot = 2 * peers[d][0] + peers[d][1]
                over_chips(k, d, slot, slot).wait_recv()
            r = total[k].shape[0]
            step = SUM_ROWS if r % SUM_ROWS == 0 else r

            def tile(t, carry, k=k, step=step):
                at = pl.ds(pl.multiple_of(t * step, step), step)
                acc = got[k][0, at, :].astype(F32)
                for s in range(1, N_CHIPS):
                    acc = acc + got[k][s, at, :].astype(F32)
                total[k][at, :] = acc
                return carry

            lax.fori_loop(0, r // step, tile, 0)
            stores.append(pltpu.make_async_copy(total[k], o_refs[k].at[c], store_sems.at[k]))
            shares.append(pltpu.make_async_remote_copy(
                src_ref=total[k], dst_ref=o_refs[k].at[c], send_sem=share_send.at[k], recv_sem=share_recv.at[k],
                device_id=(x, y, 1 - c), device_id_type=MESH))
            stores[-1].start()
            shares[-1].start()
        for k in range(n):
            pltpu.make_async_remote_copy(
                src_ref=total[k], dst_ref=o_refs[k].at[1 - c], send_sem=share_send.at[k], recv_sem=share_recv.at[k],
                device_id=(x, y, 1 - c), device_id_type=MESH).wait_recv()
        for cp in sends + shares:
            cp.wait_send()
        for cp in stores:
            cp.wait()

    both = list(parts) + list(landed)
    return pl.pallas_call(
        body, name=name, out_shape=[jax.ShapeDtypeStruct((2,) + p.shape[1:], F32) for p in both],
        in_specs=[HBM] * n, out_specs=[HBM] * n,
        scratch_shapes=[pltpu.VMEM(p.shape, p.dtype) for p in both] + [pltpu.VMEM(p.shape[1:], F32) for p in both]
        + [pltpu.SemaphoreType.DMA((n, 3))] * 2 + [pltpu.SemaphoreType.DMA((n,))] * 4,
        compiler_params=_params(),
    )(*both)


SEM = pl.BlockSpec(memory_space=pltpu.SEMAPHORE)
IN_FLIGHT = pltpu.SideEffectType.DATAFLOW_SIDE_EFFECTING


def _chip_copies(s_refs, l_refs, sems, scatter, theirs):
    x, y, c = lax.axis_index("x"), lax.axis_index("y"), lax.axis_index("c")
    me = 2 * x + y
    copies = []
    for k in range(len(s_refs)):
        for d, (dx, dy) in enumerate(CHIP_DELTAS):
            tx, ty = _flip(x, dx), _flip(y, dy)
            peer = 2 * tx + ty
            send_sem, recv_sem = sems[2 * (3 * k + d)], sems[2 * (3 * k + d) + 1]
            copies.append(pltpu.make_async_remote_copy(
                src_ref=s_refs[k].at[peer] if scatter else s_refs[k], dst_ref=l_refs[k].at[peer if theirs else me],
                send_sem=send_sem, recv_sem=recv_sem, device_id=(tx, ty, c), device_id_type=MESH))
    return copies


def _chips_start(srcs, lands, after, *, scatter, name):
    n = len(srcs)
    n_sem = 2 * 3 * n

    def body(*refs):
        s_refs, l_refs = refs[:n], refs[n:2 * n]
        sems = refs[2 * n + 1:2 * n + 1 + n_sem]
        token = refs[-1]
        for cp in _chip_copies(s_refs, l_refs, sems, scatter, False):
            cp.start()
        token[...] = jnp.zeros_like(token)

    hbm = lambda a: pltpu.HBM(a.shape, a.dtype)
    res = pl.pallas_call(
        body, name=name,
        out_shape=(*[pltpu.SemaphoreType.DMA(())] * n_sem, *[hbm(a) for a in srcs], *[hbm(a) for a in lands],
                   jax.ShapeDtypeStruct((8, LANES), F32)),
        in_specs=[HBM] * (2 * n) + [pl.BlockSpec(memory_space=pl.ANY)],
        out_specs=(*[SEM] * n_sem, *[HBM] * (2 * n), VMEM),
        input_output_aliases={k: n_sem + k for k in range(2 * n)},
        compiler_params=pltpu.CompilerParams(has_side_effects=IN_FLIGHT),
    )(*[pltpu.with_memory_space_constraint(a, pltpu.HBM) for a in list(srcs) + list(lands)], after)
    return res[:n_sem], res[n_sem:n_sem + n], res[n_sem + n:n_sem + 2 * n], res[-1]


def _chips_wait(sems, srcs, lands, after, *, scatter, name):
    n = len(srcs)
    n_sem = len(sems)

    def body(*refs):
        s_refs, l_refs = refs[:n], refs[n:2 * n]
        sem_refs = refs[2 * n:2 * n + n_sem]
        for cp in _chip_copies(s_refs, l_refs, sem_refs, scatter, False):
            cp.wait_send()
        for cp in _chip_copies(s_refs, l_refs, sem_refs, scatter, True):
            cp.wait_recv()

    hbm = lambda a: pltpu.HBM(a.shape, a.dtype)
    res = pl.pallas_call(
        body, name=name, out_shape=tuple(hbm(a) for a in list(srcs) + list(lands)),
        in_specs=[HBM] * (2 * n) + [SEM] * n_sem + [pl.BlockSpec(memory_space=pl.ANY)], out_specs=tuple([HBM] * (2 * n)),
        input_output_aliases={k: k for k in range(2 * n)},
        compiler_params=pltpu.CompilerParams(has_side_effects=IN_FLIGHT),
    )(*srcs, *lands, *sems, after)
    return res[n:]


def _all_gather_devices(rows, *, name, after=None):
    deltas = [(dx, dy, dc) for dx in (0, 1) for dy in (0, 1) for dc in (0, 1)][1:]
    order = [] if after is None else [after]

    def body(x_ref, *rest):
        o_ref, send_sems, recv_sems = rest[-3:]
        x, y, c = lax.axis_index("x"), lax.axis_index("y"), lax.axis_index("c")
        me = 4 * x + 2 * y + c
        o_ref[me] = x_ref[...]
        sends, recvs = [], []
        for k, (dx, dy, dc) in enumerate(deltas):
            tx, ty, tc = _flip(x, dx), _flip(y, dy), _flip(c, dc)
            sends.append(pltpu.make_async_remote_copy(src_ref=x_ref, dst_ref=o_ref.at[me], send_sem=send_sems.at[k],
                                                      recv_sem=recv_sems.at[k], device_id=(tx, ty, tc), device_id_type=MESH))
            recvs.append(pltpu.make_async_remote_copy(src_ref=x_ref, dst_ref=o_ref.at[4 * tx + 2 * ty + tc],
                                                      send_sem=send_sems.at[k], recv_sem=recv_sems.at[k],
                                                      device_id=(tx, ty, tc), device_id_type=MESH))
        for cp in sends:
            cp.start()
        for cp in recvs:
            cp.wait_recv()
        for cp in sends:
            cp.wait_send()

    return pl.pallas_call(
        body, name=name, out_shape=jax.ShapeDtypeStruct((N_DEV,) + rows.shape, rows.dtype),
        in_specs=[VMEM] + [pl.BlockSpec(memory_space=pl.ANY)] * len(order), out_specs=VMEM,
        scratch_shapes=[pltpu.SemaphoreType.DMA((N_DEV - 1,)), pltpu.SemaphoreType.DMA((N_DEV - 1,))],
    )(rows, *order)


WEIGHTS = ("ada_w", "ada_b", "ln_g", "ln_b", "e_w_in", "gmlp_norm_g", "gmlp_norm_b", "gmlp_ws", "gmlp_bs", "pool_w",
           "pool_b", "pool_scale", "e_w_out", "o_w_in", "mla_q_norm_g", "mla_kv_norm_g", "mla_w_uq", "mla_w_uk",
           "mla_w_uv", "o_w_out")
SMALL = ("ln_g", "ln_b", "gmlp_norm_g", "gmlp_norm_b", "gmlp_bs", "pool_b", "pool_scale", "mla_kv_norm_g", "mla_q_norm_g")


def _pad_cols(v, n):
    return jnp.concatenate([v, jnp.zeros((v.shape[0], n - v.shape[1]), v.dtype)], axis=1) if n > v.shape[1] else v


def _halves(g):
    return g.reshape(g.shape[0], 2, g.shape[1] // 2, g.shape[2])


def kernel(x, c, positions, ada_w, ada_b, ln_g, ln_b, e_w_in, gmlp_norm_g, gmlp_norm_b, gmlp_ws, gmlp_bs, pool_w, pool_b, pool_scale, e_w_out, o_w_in, mla_q_norm_g, mla_kv_norm_g, mla_w_uq, mla_w_uk, mla_w_uv, o_w_out, loss_target, m_ada_w, m_ada_b, m_ln_g, m_ln_b, m_e_w_in, m_gmlp_norm_g, m_gmlp_norm_b, m_gmlp_ws, m_gmlp_bs, m_pool_w, m_pool_b, m_pool_scale, m_e_w_out, m_o_w_in, m_mla_q_norm_g, m_mla_kv_norm_g, m_mla_w_uq, m_mla_w_uk, m_mla_w_uv, m_o_w_out, v_ada_w, v_ada_b, v_ln_g, v_ln_b, v_e_w_in, v_gmlp_norm_g, v_gmlp_norm_b, v_gmlp_ws, v_gmlp_bs, v_pool_w, v_pool_b, v_pool_scale, v_e_w_out, v_o_w_in, v_mla_q_norm_g, v_mla_kv_norm_g, v_mla_w_uq, v_mla_w_uk, v_mla_w_uv, v_o_w_out):
    args = dict(locals())
    weights = {n: args[n] for n in WEIGHTS}
    mom = {n: args["m_" + n] for n in WEIGHTS}
    var = {n: args["v_" + n] for n in WEIGHTS}
    ax, ay, ac = lax.axis_index("x"), lax.axis_index("y"), lax.axis_index("c")
    chip = 2 * ax + ay
    dev = 2 * chip + ac
    d = D_MODEL
    x2 = x[0]
    target = loss_target[0]
    q_rank_sh = mla_q_norm_g.shape[1]

    empty_zone = lambda w: lax.dynamic_update_slice(lax.empty((N_CHIPS,) + w.shape, w.dtype), w[None], (chip, 0, 0))
    shards0 = [w.astype(BF16) for w in (pool_w[0].reshape(-1, POOL_GROUP_DIM), e_w_out[0])]
    shards1 = [w.astype(BF16) for w in (o_w_in[0], mla_w_uq[0].reshape(q_rank_sh, -1), o_w_out[0])]
    w_in0, = _gather_weights([e_w_in[0].astype(BF16)], name="gather_weights")
    wuk_hrd = jnp.transpose(mla_w_uk[0], (1, 0, 2)).astype(BF16)
    wuk_hdr = jnp.transpose(mla_w_uk[0], (1, 2, 0)).astype(BF16)
    wuv_hrv = jnp.transpose(mla_w_uv[0], (1, 0, 2)).astype(BF16)
    wuv_hvr = jnp.transpose(mla_w_uv[0], (1, 2, 0)).astype(BF16)
    ws = gmlp_ws[0]
    ws_t = jnp.transpose(ws, (0, 2, 1))
    bs_t = _pad_cols(gmlp_bs[0].T, LANES)

    inv = 1.0 / (ROPE_THETA ** (jnp.arange(0, MLA_ROPE, 2, dtype=F32) / MLA_ROPE))
    ang = positions[0].astype(F32)[:, None] * inv
    cos_t = jnp.tile(jnp.cos(ang), (1, 4))
    sin_t = jnp.concatenate([-jnp.sin(ang), -jnp.sin(ang), jnp.sin(ang), jnp.sin(ang)], axis=1)

    c_all = _all_gather_devices(c.reshape(8, LANES), after=w_in0, name="gather_c").reshape(N_DEV, d)
    cols = ada_w.shape[2]
    ada_b_mine = lax.dynamic_slice_in_dim(ada_b, chip * cols, cols, axis=1)[:, None, :]
    mod_sh = _ada_mod(c_all, ada_w, ada_b_mine, name="ada_mod")
    q_norm_rows = jnp.zeros((8, cols), F32).at[0, :q_rank_sh].set(mla_q_norm_g[0])
    mod_all = _all_gather_chips(jnp.concatenate([mod_sh.reshape(2 * N_DEV, cols), q_norm_rows]), name="gather_mod")
    q_norm_g = mod_all[:, 2 * N_DEV, :q_rank_sh].reshape(1, -1)
    mod_all = jnp.transpose(mod_all[:, :2 * N_DEV].reshape(N_CHIPS, 2, N_DEV, cols), (1, 2, 0, 3)).reshape(2, N_DEV, 3 * d)
    mod = lax.dynamic_index_in_dim(mod_all, dev, axis=1, keepdims=False)
    shift = [mod[l:l + 1, :d] for l in range(2)]
    scale = [mod[l:l + 1, d:2 * d] for l in range(2)]
    gate = [mod[l:l + 1, 2 * d:] for l in range(2)]
    flight0 = _chips_start(shards0, [empty_zone(w) for w in shards0], mod, scatter=False, name="gather0_start")
    flight1 = _chips_start(shards1, [empty_zone(w) for w in shards1], flight0[3], scatter=False, name="gather1_start")

    scale[0] = scale[0] + flight1[3][:1, :1]
    h0, proj0 = _modulate_matmul(x2, scale[0], shift[0], w_in0, name="proj0")
    pool_w_g, w_out0 = _chips_wait(*flight0[:3], proj0, scatter=False, name="gather0_wait")
    pool_w_bf = jnp.transpose(pool_w_g.reshape(N_CHIPS, POOL_GROUPS, -1, POOL_GROUP_DIM), (1, 0, 2, 3)).reshape(
        POOL_GROUPS, POOL_GROUP_DIM, POOL_GROUP_DIM)
    w_out0 = w_out0.reshape(-1, d)
    mix0 = _even_fwd(proj0, ws, bs_t, gmlp_norm_g, gmlp_norm_b, pool_w_bf, pool_b, pool_scale, name="even_fwd")
    y0, x1, h1 = _out_resid_ln(mix0, w_out0, x2, gate[0], ln_g[0:1], ln_b[0:1], scale[1], shift[1], name="out0_ln")

    w_in1_g, w_uq_g, w_out1 = _chips_wait(*flight1[:3], h1, scatter=False, name="gather1_wait")
    w_out1 = w_out1.reshape(-1, d)
    w_in1 = jnp.transpose(w_in1_g, (1, 0, 2)).reshape(d, ODD_IN)
    w_in1 = jnp.concatenate([w_in1[:, ODD_SMALL:], _pad_cols(w_in1[:, :ODD_SMALL], ODD_SMALL_PAD)], axis=1)
    w_uq = w_uq_g.reshape(MLA_Q_RANK, MLA_HEADS, MLA_NOPE + MLA_ROPE)
    w_uq_nope = w_uq[:, :, :MLA_NOPE].reshape(MLA_Q_RANK, -1)
    w_uq_rope = jnp.transpose(w_uq[:, :, MLA_NOPE:].reshape(MLA_Q_RANK, MLA_HEADS // 2, 2, 2, ROPE_HALF),
                              (0, 1, 3, 2, 4)).reshape(MLA_Q_RANK, -1)
    proj1 = _matmul(h1, w_in1, tm=1024, tn=1280, out_dtype=BF16, name="proj1")
    q_cn, keys = _mla_prep(proj1, q_norm_g, mla_kv_norm_g, cos_t, sin_t, name="mla_prep")
    q_nope = _matmul(q_cn, w_uq_nope, tm=1024, tn=2048, name="q_nope", out_dtype=BF16)
    q_rope_pre = _matmul(q_cn, w_uq_rope, tm=1024, name="q_rope")
    q = _q_build(q_nope, q_rope_pre, wuk_hdr, cos_t, sin_t, name="q_build")
    o_lat, lse = _attn_fwd(q, keys, name="attn_fwd")
    og = _o_build(o_lat, wuv_hrv, proj1, name="o_build")

    dy1, dres1, g_ln_g1, g_ln_b1, dgate1, loss = _out_loss_ln_bwd(
        og, w_out1, x1, gate[1], ln_g[1:2], ln_b[1:2], target, name="out1_loss_ln")
    dg1 = _matmul(dy1, w_out1, trans_b=True, tn=2048, out_dtype=BF16, name="d_og")
    g_w_out1 = _matmul(og, dy1, trans_a=True, out_dtype=BF16, tm=1024, name="g_out1")
    do_lat, dz, g_uv = _o_bwd(dg1, proj1, o_lat, wuv_hrv, wuv_hvr, name="o_bwd")
    dq, dkeys = _attn_bwd(q, keys, do_lat, o_lat, lse, name="attn_bwd")
    dq_all, g_uk = _q_bwd(dq, q_nope, wuk_hrd, cos_t, sin_t, name="q_bwd")
    n_grp = MLA_HEADS // Q_HEAD_GROUP
    w_uq_all = jnp.concatenate([w_uq_nope.reshape(MLA_Q_RANK, n_grp, -1), w_uq_rope.reshape(MLA_Q_RANK, n_grp, -1)],
                               axis=2).reshape(MLA_Q_RANK, -1)
    dq_cn = _matmul(dq_all, w_uq_all, trans_b=True, tm=1024, name="d_qcn")
    g_uq_all = _matmul(q_cn, dq_all, trans_a=True, out_dtype=BF16, name="g_uq").reshape(MLA_Q_RANK, n_grp, -1)
    g_uq_nope = g_uq_all[:, :, :Q_HEAD_GROUP * MLA_NOPE].reshape(MLA_Q_RANK, -1)
    g_uq_rope = g_uq_all[:, :, Q_HEAD_GROUP * MLA_NOPE:].reshape(MLA_Q_RANK, -1)
    dsmall, g_qg, g_kvg = _mla_prep_bwd(proj1, dq_cn, dkeys, q_norm_g, mla_kv_norm_g, cos_t, sin_t, name="mla_prep_bwd")
    dproj1 = jnp.concatenate([dz, dsmall], axis=1)
    g_w_in1 =_matmul(h1, dproj1, trans_a=True, out_dtype=BF16, tm=1024, tn=1280, name="g_in1")

    g_uq_rope = jnp.transpose(g_uq_rope.reshape(MLA_Q_RANK, MLA_HEADS // 2, 2, 2, ROPE_HALF), (0, 1, 3, 2, 4))
    g_uq = jnp.concatenate([g_uq_nope.reshape(MLA_Q_RANK, MLA_HEADS, MLA_NOPE), g_uq_rope.reshape(MLA_Q_RANK, MLA_HEADS, MLA_ROPE)], axis=2)
    g_w_in1 = jnp.concatenate([g_w_in1[:, MLA_WIDTH:MLA_WIDTH + ODD_SMALL], g_w_in1[:, :MLA_WIDTH]], axis=1)
    g_w_in1 = jnp.transpose(g_w_in1.reshape(d, N_CHIPS, -1), (1, 0, 2))
    big1 = [
        _halves(g_w_in1),
        _halves(g_uq.reshape(N_CHIPS, q_rank_sh, -1)),
        _halves(g_w_out1.reshape(N_CHIPS, -1, d)),
        _halves(g_uk.astype(BF16).reshape(N_CHIPS, -1, MLA_NOPE)),
        _halves(g_uv.astype(BF16).reshape(N_CHIPS, -1, MLA_V)),
    ]
    parts1 = _reduce_sibling(big1, name="reduce_sibling1")
    lands2 = [lax.dynamic_update_slice(lax.empty(p.shape, BF16), lax.dynamic_slice_in_dim(p, chip, 1, axis=0), (chip, 0, 0))
              for p in parts1]
    flight2 = _chips_start(parts1, lands2, loss, scatter=True, name="reduce1_start")

    gate[0] = gate[0] + flight2[3][:1, :1]
    dy0, dres0, g_ln_g0, g_ln_b0, dgate0, dscale1, dshift1 = _dh_mid_ln_bwd(
        dproj1, w_in1, x2, y0, gate[0], ln_g[0:1], ln_b[0:1], dres1, scale[1], x1, name="d_h1_mid_ln")
    dmix0 = _matmul(dy0, w_out0, trans_b=True, tn=2048, out_dtype=BF16, name="d_mix0")
    g_w_out0 = _matmul(mix0, dy0, trans_a=True, out_dtype=BF16, tm=1024, name="g_out0")
    dproj0, g_ws, g_bs_t, g_ng, g_nb, g_pw, g_pb, g_ps = _even_bwd(
        proj0, dmix0, ws, ws_t, bs_t, gmlp_norm_g, gmlp_norm_b, pool_w_bf, pool_b, pool_scale, name="even_bwd")
    g_w_in0 = _matmul(h0, dproj0, trans_a=True, out_dtype=BF16, out_stacked=True, tm=1024, tn=1280, name="g_in0")

    g_pw = jnp.transpose(g_pw.astype(BF16).reshape(POOL_GROUPS, N_CHIPS, -1, POOL_GROUP_DIM), (1, 0, 2, 3))
    big0 = [
        _halves(g_w_in0),
        _halves(g_pw.reshape(N_CHIPS, -1, POOL_GROUP_DIM)),
        _halves(g_w_out0.reshape(N_CHIPS, -1, d)),
        _halves(g_ws.astype(BF16)),
    ]
    parts0 = _reduce_sibling(big0, name="reduce_sibling0")
    landed1 = _chips_wait(*flight2[:3], parts0[0], scatter=True, name="reduce1_wait")
    lands3 = [lax.dynamic_update_slice(lax.empty(p.shape, BF16), lax.dynamic_slice_in_dim(p, chip, 1, axis=0), (chip, 0, 0))
              for p in parts0]
    flight3 = _chips_start(parts0, lands3, landed1[0], scatter=True, name="reduce0_start")
    grad_x, dscale0, dshift0 = _dh_input_bwd(dproj0, w_in0, x2, dres0, scale[0], after=flight3[3], name="d_h0_input")

    small_local = {
        "ln_g": jnp.concatenate([g_ln_g0, g_ln_g1]), "ln_b": jnp.concatenate([g_ln_b0, g_ln_b1]),
        "gmlp_norm_g": g_ng, "gmlp_norm_b": g_nb, "gmlp_bs": g_bs_t[:, :GMLP_HEADS].T, "pool_b": g_pb, "pool_scale": g_ps,
        "mla_kv_norm_g": g_kvg, "mla_q_norm_g": g_qg,
    }
    n_mod = 2 * 3 * d
    vec = jnp.concatenate([dshift0, dscale0, dgate0, dshift1, dscale1, dgate1]
                          + [small_local[n].reshape(1, -1) for n in SMALL] + [loss], axis=1)
    n_vec = vec.shape[1]
    vec = _pad_cols(vec, -(-n_vec // (8 * LANES)) * 8 * LANES).reshape(-1, LANES)
    vec_all = _all_gather_devices(vec, name="gather_small")
    vec_sum = _sum_devices(vec_all, name="sum_small").reshape(-1)
    dmod_all = vec_all.reshape(N_DEV, -1)[:, :n_mod].reshape(N_DEV, 2, 3 * d)
    dmod_sh = jnp.transpose(lax.dynamic_slice_in_dim(dmod_all, chip * cols, cols, axis=2), (1, 0, 2))
    dmod_sh = jnp.concatenate([dmod_sh, jnp.zeros((2, LANES - N_DEV, cols), F32)], axis=1)
    grads = {"ada_w": _ada_grad(_pad_cols(c_all.T, LANES), dmod_sh, name="ada_grad"), "ada_b": vec_sum[:n_mod].reshape(2, 3 * d)}
    off = n_mod
    for n in SMALL:
        sz = small_local[n].size
        grads[n] = vec_sum[off:off + sz]
        off += sz
    grads["mla_q_norm_g"] = lax.dynamic_slice_in_dim(grads["mla_q_norm_g"], chip * q_rank_sh, q_rank_sh)
    for n in SMALL:
        grads[n] = grads[n].reshape(weights[n].shape)

    landed0 = _chips_wait(*flight3[:3], grads["ada_w"], scatter=True, name="reduce0_wait")
    totals = _reduce_chips([], list(landed0) + list(landed1), name="reduce_chips")
    for n, t in zip(("e_w_in", "pool_w", "e_w_out", "gmlp_ws", "o_w_in", "mla_w_uq", "o_w_out"), totals):
        if n != "gmlp_ws":
            grads[n] = t.reshape(weights[n].shape)
    rep = jnp.concatenate([t.reshape(-1, LANES) for t in (totals[3], totals[7], totals[8])])
    rep_land = lax.dynamic_update_slice(lax.empty((N_CHIPS,) + rep.shape, F32), rep[None], (chip, 0, 0))
    flight4 = _chips_start([rep], [rep_land], totals[0], scatter=False, name="gather_rep_start")

    delta, new_m, new_v = {}, {}, {}
    replicated = ("gmlp_ws", "mla_w_uk", "mla_w_uv")
    large = [n for n in WEIGHTS if n not in SMALL and n != "ada_b"]
    for n in large:
        if n not in replicated:
            delta[n], new_m[n], new_v[n] = _adamw(weights[n], grads[n], mom[n], var[n], after=flight4[3], name="adamw_" + n)
    rep = _chips_wait(*flight4[:3], delta["e_w_in"], scatter=False, name="gather_rep_wait")[0]
    r_ws, r_uk = GMLP_BLOCK, 4 * MLA_KV_RANK
    grads["gmlp_ws"] = rep[:, :r_ws].reshape(weights["gmlp_ws"].shape)
    grads["mla_w_uk"] = jnp.transpose(rep[:, r_ws:r_ws + r_uk].reshape(MLA_HEADS, MLA_KV_RANK, MLA_NOPE), (1, 0, 2))[None]
    grads["mla_w_uv"] = jnp.transpose(rep[:, r_ws + r_uk:].reshape(MLA_HEADS, MLA_KV_RANK, MLA_V), (1, 0, 2))[None]
    for n in replicated:
        delta[n], new_m[n], new_v[n] = _adamw(weights[n], grads[n], mom[n], var[n], name="adamw_" + n)
    small = [n for n in WEIGHTS if n not in large]
    ds, ms, vs = _adamw_small([weights[n] for n in small], [grads[n] for n in small], [mom[n] for n in small],
                              [var[n] for n in small], name="adamw_small")
    for n, dn, mn, vn in zip(small, ds, ms, vs):
        delta[n], new_m[n], new_v[n] = dn, mn, vn

    return (vec_sum[n_vec - 1], grad_x[None], *[grads[n] for n in WEIGHTS], *[delta[n] for n in WEIGHTS],
            *[new_m[n] for n in WEIGHTS], *[new_v[n] for n in WEIGHTS])
```

```python
import jax
import jax.numpy as jnp
from jax import lax
from jax.experimental import pallas as pl
from jax.experimental.pallas import tpu as pltpu

F32 = jnp.float32
BF16 = jnp.bfloat16
MESH = pl.DeviceIdType.MESH

D_MODEL = 1024
CHUNK = 64
LN_EPS = 1e-5
GMLP_HEADS = 4
GMLP_HEAD_DIM = 256
GMLP_BLOCK = 128
POOL_WINDOWS = (2, 4, 8, 16)
POOL_GROUPS = 4
POOL_GROUP_DIM = 256
POOL_HALO = 16
EVEN_IN = 5120
MLA_HEADS = 16
MLA_NOPE = 128
MLA_ROPE = 64
MLA_V = 128
MLA_Q_RANK = 256
MLA_KV_RANK = 128
MLA_WIDTH = MLA_HEADS * MLA_V
ODD_IN = 2496
ODD_SMALL = MLA_Q_RANK + MLA_KV_RANK + MLA_ROPE
ODD_SMALL_PAD = 512
QK_PAD = 256
ROPE_THETA = 10000.0
ATTN_SCALE = (MLA_NOPE + MLA_ROPE) ** -0.5
DEEPNORM_ALPHA = (2.0 * 2) ** 0.25
ADAM_LR = 0.001
ADAM_B1 = 0.9
ADAM_B2 = 0.999
ADAM_EPS = 1e-08
ADAM_WD = 0.01
ADAM_STEP = 10
NEG = -1e30
LANES = 128
N_DEV = 8
N_CHIPS = 4
VMEM_LIMIT_BYTES = 56 * 1024 * 1024
HBM = pl.BlockSpec(memory_space=pltpu.HBM)
VMEM = pl.BlockSpec(memory_space=pltpu.VMEM)


def _params(*sem):
    return pltpu.CompilerParams(dimension_semantics=sem if sem else None, vmem_limit_bytes=VMEM_LIMIT_BYTES)


def _tile(dim, pref):
    for t in (pref, 2048, 1280, 1024, 512, 256, 128):
        if t <= min(pref, dim) and dim % t == 0:
            return t
    return dim


def _sigmoid(z):
    return 1.0 / (1.0 + jnp.exp(-z))


def _dot(a, b, dims):
    return lax.dot_general(a, b, (dims, ((), ())), preferred_element_type=F32)


NN = ((1,), (0,))
NT = ((1,), (1,))
TN = ((0,), (0,))


def _matmul(a, b, *, name, trans_a=False, trans_b=False, out_dtype=F32, b_stacked=False, out_stacked=False,
            tm=512, tn=1024, tk=2048, after=None):
    k, m = a.shape if trans_a else a.shape[::-1]
    if b_stacked:
        ns, kb, n_sh = b.shape
        kb, n = (ns * n_sh, kb) if trans_b else (kb, ns * n_sh)
    else:
        n, kb = b.shape if trans_b else b.shape[::-1]
    assert k == kb, (a.shape, b.shape)
    tm = _tile(m, tm)
    if b_stacked and trans_b:
        tn, tk = _tile(n, tn), n_sh
    elif b_stacked or out_stacked:
        tn, tk = _tile(n // N_CHIPS, tn), _tile(k, tk)
    else:
        tn, tk = _tile(n, tn), _tile(k, tk)
    nk = k // tk
    per = max((n // N_CHIPS) // tn, 1)
    dims = ((0 if trans_a else 1,), (1 if trans_b else 0,))

    def body_one(a_ref, b_ref, *rest):
        o_ref = rest[-1]
        o_ref[...] = _dot(a_ref[...].astype(BF16), b_ref[...].astype(BF16), dims).astype(out_dtype)

    def body_acc(a_ref, b_ref, *rest):
        o_ref, acc_ref = rest[-2:]
        kk = pl.program_id(2)

        @pl.when(kk == 0)
        def _():
            acc_ref[...] = jnp.zeros_like(acc_ref)

        acc_ref[...] += _dot(a_ref[...].astype(BF16), b_ref[...].astype(BF16), dims)

        @pl.when(kk == nk - 1)
        def _():
            o_ref[...] = acc_ref[...].astype(out_dtype)

    a_spec = pl.BlockSpec((tk, tm), lambda i, j, kk: (kk, i)) if trans_a else pl.BlockSpec((tm, tk), lambda i, j, kk: (i, kk))
    if b_stacked and trans_b:
        b_spec = pl.BlockSpec((None, tn, tk), lambda i, j, kk: (kk, j, 0))
    elif b_stacked:
        b_spec = pl.BlockSpec((None, tk, tn), lambda i, j, kk: (j // per, kk, j % per))
    elif trans_b:
        b_spec = pl.BlockSpec((tn, tk), lambda i, j, kk: (j, kk))
    else:
        b_spec = pl.BlockSpec((tk, tn), lambda i, j, kk: (kk, j))
    if out_stacked:
        o_spec = pl.BlockSpec((None, tm, tn), lambda i, j, kk: (j // per, i, j % per))
        o_shape = jax.ShapeDtypeStruct((N_CHIPS, m, n // N_CHIPS), out_dtype)
    else:
        o_spec = pl.BlockSpec((tm, tn), lambda i, j, kk: (i, j))
        o_shape = jax.ShapeDtypeStruct((m, n), out_dtype)
    order = [] if after is None else [after]
    return pl.pallas_call(
        body_one if nk == 1 else body_acc, name=name, grid=(m // tm, n // tn, nk),
        in_specs=[a_spec, b_spec] + [pl.BlockSpec(memory_space=pl.ANY)] * len(order),
        out_specs=o_spec, out_shape=o_shape, scratch_shapes=[] if nk == 1 else [pltpu.VMEM((tm, tn), F32)],
        compiler_params=_params("parallel", "parallel", "arbitrary"),
    )(a, b, *order)


def _matmul_rows(a, b, epilogue, row_ins, vec_ins, row_outs, vec_outs, *, name, trans_b=False, b_stacked=False,
                 tm=512, tk=2048, after=None):
    m, k = a.shape
    if b_stacked:
        ns, n, n_sh = b.shape
        assert trans_b and ns * n_sh == k
        tk = n_sh
    else:
        n = b.shape[0] if trans_b else b.shape[1]
        tk = _tile(k, tk)
    tm = _tile(m, tm)
    nk = k // tk
    dims = ((1,), (1 if trans_b else 0,))
    n_ri, n_vi, n_ro, n_vo = len(row_ins), len(vec_ins), len(row_outs), len(vec_outs)
    order = [] if after is None else [after]

    def body(*refs):
        a_ref, b_ref = refs[:2]
        pos = 2
        rin = refs[pos:pos + n_ri]
        pos += n_ri
        vin = refs[pos:pos + n_vi]
        pos += n_vi + len(order)
        rout = refs[pos:pos + n_ro]
        pos += n_ro
        vout = refs[pos:pos + n_vo]
        first = pl.program_id(0) == 0
        part = _dot(a_ref[...].astype(BF16), b_ref[...].astype(BF16), dims)
        if nk == 1:
            epilogue(part, first, rin, vin, rout, vout)
        else:
            acc_ref = refs[-1]
            kk = pl.program_id(1)

            @pl.when(kk == 0)
            def _():
                acc_ref[...] = part

            @pl.when(kk > 0)
            def _():
                acc_ref[...] += part

            @pl.when(kk == nk - 1)
            def _():
                epilogue(acc_ref[...], first, rin, vin, rout, vout)

    a_spec = pl.BlockSpec((tm, tk), lambda i, kk: (i, kk))
    if b_stacked:
        b_spec = pl.BlockSpec((None, n, tk), lambda i, kk: (kk, 0, 0))
    elif trans_b:
        b_spec = pl.BlockSpec((n, tk), lambda i, kk: (0, kk))
    else:
        b_spec = pl.BlockSpec((tk, n), lambda i, kk: (kk, 0))
    row = pl.BlockSpec((tm, n), lambda i, kk: (i, 0))
    vec = lambda w: pl.BlockSpec((1, w), lambda i, kk: (0, 0))
    return pl.pallas_call(
        body, name=name, grid=(m // tm, nk),
        in_specs=[a_spec, b_spec] + [row] * n_ri + [vec(v.shape[1]) for v in vec_ins] + [pl.BlockSpec(memory_space=pl.ANY)] * len(order),
        out_specs=[row] * n_ro + [vec(w) for w in vec_outs],
        out_shape=[jax.ShapeDtypeStruct((m, n), dt) for dt in row_outs] + [jax.ShapeDtypeStruct((1, w), F32) for w in vec_outs],
        scratch_shapes=[] if nk == 1 else [pltpu.VMEM((tm, n), F32)],
        compiler_params=_params("arbitrary", "arbitrary"),
    )(a, b, *row_ins, *vec_ins, *order)


def _row_spec(ts, d):
    return pl.BlockSpec((ts, d), lambda i: (i, 0))


def _vec_spec(d):
    return pl.BlockSpec((1, d), lambda i: (0, 0))


def _modulate(x, scale, shift, *, name):
    s, d = x.shape
    ts = _tile(s, 512)

    def body(x_ref, sc_ref, sh_ref, h_ref):
        h_ref[...] = (x_ref[...] * (1.0 + sc_ref[...]) + sh_ref[...]).astype(BF16)

    return pl.pallas_call(
        body, name=name, grid=(s // ts,), in_specs=[_row_spec(ts, d), _vec_spec(d), _vec_spec(d)],
        out_specs=_row_spec(ts, d), out_shape=jax.ShapeDtypeStruct((s, d), BF16), compiler_params=_params("parallel"),
    )(x, scale, shift)


def _ln_stats(pre):
    mu = jnp.mean(pre, axis=-1, keepdims=True)
    xc = pre - mu
    var = jnp.mean(xc * xc, axis=-1, keepdims=True)
    rstd = lax.rsqrt(var + LN_EPS)
    return xc * rstd, rstd


def _ln_bwd_rows(dout, xhat, rstd, g):
    dxh = dout * g
    m1 = jnp.mean(dxh, axis=-1, keepdims=True)
    m2 = jnp.mean(dxh * xhat, axis=-1, keepdims=True)
    return rstd * (dxh - m1 - xhat * m2)


def _colsum(v):
    return jnp.sum(v, axis=0, keepdims=True)


def _out_resid_ln(mix, w_out, x, gate, g, b, scale_next, shift_next, *, name):
    def epilogue(y, first, rin, vin, rout, vout):
        (x_ref,), (gate_ref, g_ref, b_ref, sc_ref, sh_ref), (y_ref, xn_ref, h_ref) = rin, vin, rout
        y_ref[...] = y
        pre = DEEPNORM_ALPHA * x_ref[...] + (1.0 + gate_ref[...]) * y
        xhat, _ = _ln_stats(pre)
        xn = xhat * g_ref[...] + b_ref[...]
        xn_ref[...] = xn
        h_ref[...] = (xn * (1.0 + sc_ref[...]) + sh_ref[...]).astype(BF16)

    return _matmul_rows(mix, w_out, epilogue, [x], [gate, g, b, scale_next, shift_next], [F32, F32, BF16], [], name=name)


def _out_loss_ln_bwd(og, w_out, x, gate, g, b, target, *, name):
    d = x.shape[1]

    def epilogue(yv, first, rin, vin, rout, vout):
        (x_ref, t_ref), (gate_ref, g_ref, b_ref), (dy_ref, dres_ref), (dg_ref, db_ref, dgate_ref, loss_ref) = rin, vin, rout, vout

        @pl.when(first)
        def _():
            for r in vout:
                r[...] = jnp.zeros_like(r)

        pre = DEEPNORM_ALPHA * x_ref[...] + (1.0 + gate_ref[...]) * yv
        xhat, rstd = _ln_stats(pre)
        diff = xhat * g_ref[...] + b_ref[...] - t_ref[...]
        loss_ref[...] += (0.5 / d) * jnp.sum(jnp.sum(diff * diff, axis=1, keepdims=True), axis=0, keepdims=True)
        dout = diff * (1.0 / d)
        dpre = _ln_bwd_rows(dout, xhat, rstd, g_ref[...])
        dy_ref[...] = (dpre * (1.0 + gate_ref[...])).astype(BF16)
        dres_ref[...] = DEEPNORM_ALPHA * dpre
        dg_ref[...] += _colsum(dout * xhat)
        db_ref[...] += _colsum(dout)
        dgate_ref[...] += _colsum(dpre * yv)

    return _matmul_rows(og, w_out, epilogue, [x, target], [gate, g, b], [BF16, F32], [d, d, d, 1], name=name)


def _dh_mid_ln_bwd(dproj, w_in, x, y, gate, g, b, dres_next, scale_next, x_next, *, name):
    d = x.shape[1]

    def epilogue(dh, first, rin, vin, rout, vout):
        (x_ref, y_ref, dr_ref, xn_ref), (gate_ref, g_ref, b_ref, sc_ref), (dy_ref, dres_ref) = rin, vin, rout
        dg_ref, db_ref, dgate_ref, dscale_ref, dshift_ref = vout

        @pl.when(first)
        def _():
            for r in vout:
                r[...] = jnp.zeros_like(r)

        dout = dr_ref[...] + dh * (1.0 + sc_ref[...])
        dscale_ref[...] += _colsum(dh * xn_ref[...])
        dshift_ref[...] += _colsum(dh)
        yv = y_ref[...]
        pre = DEEPNORM_ALPHA * x_ref[...] + (1.0 + gate_ref[...]) * yv
        xhat, rstd = _ln_stats(pre)
        dpre = _ln_bwd_rows(dout, xhat, rstd, g_ref[...])
        dy_ref[...] = (dpre * (1.0 + gate_ref[...])).astype(BF16)
        dres_ref[...] = DEEPNORM_ALPHA * dpre
        dg_ref[...] += _colsum(dout * xhat)
        db_ref[...] += _colsum(dout)
        dgate_ref[...] += _colsum(dpre * yv)

    return _matmul_rows(dproj, w_in, epilogue, [x, y, dres_next, x_next], [gate, g, b, scale_next], [BF16, F32], [d] * 5,
                        trans_b=True, tk=1280, name=name)


def _dh_input_bwd(dproj, w_in_stacked, x, dres, scale, *, name, after):
    d = x.shape[1]

    def epilogue(dh, first, rin, vin, rout, vout):
        (x_ref, dr_ref), (sc_ref,), (dx_ref,), (dscale_ref, dshift_ref) = rin, vin, rout, vout

        @pl.when(first)
        def _():
            for r in vout:
                r[...] = jnp.zeros_like(r)

        dx_ref[...] = dr_ref[...] + dh * (1.0 + sc_ref[...])
        dscale_ref[...] += _colsum(dh * x_ref[...])
        dshift_ref[...] += _colsum(dh)

    return _matmul_rows(dproj, w_in_stacked, epilogue, [x, dres], [scale], [F32], [d, d], trans_b=True, b_stacked=True,
                        tm=1024, after=after, name=name)


def _chunk_mask(transposed=False):
    r = lax.broadcasted_iota(jnp.int32, (GMLP_BLOCK, GMLP_BLOCK), 0) // CHUNK
    c = lax.broadcasted_iota(jnp.int32, (GMLP_BLOCK, GMLP_BLOCK), 1) // CHUNK
    return (r <= c) if transposed else (c <= r)


def _window_sum(ext, steps, forward):
    rows = ext.shape[0]
    acc = ext
    for k in range(steps):
        shift = 1 << k
        acc = acc + pltpu.roll(acc, (rows - shift) if forward else shift, 0)
    return acc


def _pool_counts(first_row, rows, win):
    t = first_row + lax.broadcasted_iota(jnp.int32, (rows, 1), 0)
    return jnp.minimum(t + 1, win).astype(F32)


def _even_specs(t):
    col = lambda j: pl.BlockSpec((t, D_MODEL), lambda n: (n, j))
    per = t // POOL_HALO
    prev = pl.BlockSpec((POOL_HALO, D_MODEL), lambda n: (jnp.maximum(n * per - 1, 0), 3))
    return col, per, prev


def _full(shape):
    return pl.BlockSpec(shape, lambda n: (0,) * len(shape))


def _gmlp_head(v_h, ng, nb, w_bf):
    xhat, rstd = _ln_stats(v_h)
    vn = (xhat * ng + nb).astype(BF16)
    return xhat, rstd, vn, _dot(w_bf, vn, NN)


def _pool_group(xb_g, prev_g, first_row, grp):
    t = xb_g.shape[0]
    ext = jnp.concatenate([prev_g, xb_g], axis=0)
    tot = _window_sum(ext, grp + 1, False)[POOL_HALO:, :]
    cnt = _pool_counts(first_row, t, POOL_WINDOWS[grp])
    return tot / cnt - xb_g, cnt


def _even_fwd(proj, ws, bs_t, ng, nb, pool_w, pool_b, pool_scale, *, name):
    s = proj.shape[0]
    t = GMLP_BLOCK
    col, per, prev = _even_specs(t)

    def body(u_ref, v_ref, za_ref, xb_ref, zb_ref, xp_ref, ws_ref, bs_ref, ng_ref, nb_ref, pw_ref, pb_ref, ps_ref, o_ref):
        n = pl.program_id(0)
        mask = _chunk_mask()
        for h in range(GMLP_HEADS):
            c0 = h * GMLP_HEAD_DIM
            cs = slice(c0, c0 + GMLP_HEAD_DIM)
            w_bf = jnp.where(mask, ws_ref[h], 0.0).astype(BF16)
            _, _, _, sv = _gmlp_head(v_ref[:, cs].astype(F32),ng_ref[...], nb_ref[...], w_bf)
            sv = sv + bs_ref[:, h:h + 1]
            za = za_ref[:, cs].astype(F32)
            o_ref[:, cs] = (u_ref[:, cs].astype(F32) * sv * (za * _sigmoid(za))).astype(BF16)
        live = (n > 0).astype(F32)
        for grp in range(POOL_GROUPS):
            c0 = grp * POOL_GROUP_DIM
            cs = slice(c0, c0 + POOL_GROUP_DIM)
            pooled, _ = _pool_group(xb_ref[:, cs].astype(F32), xp_ref[:, cs].astype(F32) * live, n * t, grp)
            yb = _dot(pooled.astype(BF16), pw_ref[grp], NN) + pb_ref[:, cs]
            zb = zb_ref[:, cs].astype(F32)
            o_ref[:, D_MODEL + c0:D_MODEL + c0 + POOL_GROUP_DIM] = (yb * ps_ref[:, cs] * (zb * _sigmoid(zb))).astype(BF16)

    return pl.pallas_call(
        body, name=name, grid=(s // t,),
        in_specs=[col(0), col(1), col(2), col(3), col(4), prev,
                  _full((GMLP_HEADS, t, t)), _full((t, LANES)), _full((1, GMLP_HEAD_DIM)), _full((1, GMLP_HEAD_DIM)),
                  _full((POOL_GROUPS, POOL_GROUP_DIM, POOL_GROUP_DIM)), _full((1, D_MODEL)), _full((1, D_MODEL))],
        out_specs=pl.BlockSpec((t, 2 * D_MODEL), lambda n: (n, 0)),
        out_shape=jax.ShapeDtypeStruct((s, 2 * D_MODEL), BF16),
        compiler_params=_params("parallel"),
    )(proj, proj, proj, proj, proj, proj, ws, bs_t, ng, nb, pool_w, pool_b, pool_scale)


def _even_bwd(proj, dmix, ws, ws_t, bs_t, ng, nb, pool_w, pool_b, pool_scale, *, name):
    s = proj.shape[0]
    t = GMLP_BLOCK
    nblk = s // t
    col, per, prev = _even_specs(t)
    nxt = lambda j: pl.BlockSpec((POOL_HALO, D_MODEL), lambda n: (jnp.minimum((n + 1) * per, nblk * per - 1), j))

    def body(u_ref, v_ref, za_ref, xb_ref, zb_ref, xp_ref, zn_ref, da_ref, db_ref, dbn_ref,
             ws_ref, wst_ref, bs_ref, ng_ref, nb_ref, pw_ref, pb_ref, ps_ref,
             dp_ref, gws_ref, gbs_ref, gng_ref, gnb_ref, gpw_ref, gpb_ref, gps_ref):
        n = pl.program_id(0)

        @pl.when(n == 0)
        def _():
            for r in (gws_ref, gbs_ref, gng_ref, gnb_ref, gpw_ref, gpb_ref, gps_ref):
                r[...] = jnp.zeros_like(r)

        mask, mask_t = _chunk_mask(), _chunk_mask(True)
        lane = lax.broadcasted_iota(jnp.int32, (t, LANES), 1)
        ngv, nbv = ng_ref[...], nb_ref[...]
        for h in range(GMLP_HEADS):
            c0 = h * GMLP_HEAD_DIM
            cs = slice(c0, c0 + GMLP_HEAD_DIM)
            w_bf = jnp.where(mask, ws_ref[h], 0.0).astype(BF16)
            wt_bf = jnp.where(mask_t, wst_ref[h], 0.0).astype(BF16)
            xhat, rstd, vn, sv = _gmlp_head(v_ref[:, cs].astype(F32),ngv, nbv, w_bf)
            sv = sv + bs_ref[:, h:h + 1]
            za, u, da = za_ref[:, cs].astype(F32), u_ref[:, cs].astype(F32), da_ref[:, cs].astype(F32)
            sg = _sigmoid(za)
            sl = za * sg
            dp_ref[:, cs] = (da * sv * sl).astype(BF16)
            dp_ref[:, 2 * D_MODEL + c0:2 * D_MODEL + c0 + GMLP_HEAD_DIM] = (
                da * u * sv * (sg * (1.0 + za * (1.0 - sg)))).astype(BF16)
            dsv = da * u * sl
            gbs_ref[...] += jnp.where(lane == h, jnp.sum(dsv, axis=1, keepdims=True), 0.0)
            dsv_bf = dsv.astype(BF16)
            gws_ref[h] += jnp.where(mask, _dot(dsv_bf, vn, NT), 0.0)
            dvn = _dot(wt_bf, dsv_bf, NN)
            dp_ref[:, D_MODEL + c0:D_MODEL + c0 + GMLP_HEAD_DIM] = _ln_bwd_rows(dvn, xhat, rstd, ngv).astype(BF16)
            gng_ref[...] += _colsum(dvn * xhat)
            gnb_ref[...] += _colsum(dvn)
        live_prev = (n > 0).astype(F32)
        live_next = (n < nblk - 1).astype(F32)
        for grp in range(POOL_GROUPS):
            c0 = grp * POOL_GROUP_DIM
            cs = slice(c0, c0 + POOL_GROUP_DIM)
            xb = xb_ref[:, cs].astype(F32)
            pooled, cnt = _pool_group(xb, xp_ref[:, cs].astype(F32) * live_prev, n * t, grp)
            pooled_bf = pooled.astype(BF16)
            pw = pw_ref[grp]
            yb = _dot(pooled_bf, pw, NN) + pb_ref[:, cs]
            ps = ps_ref[:, cs]
            zb, db = zb_ref[:, cs].astype(F32), db_ref[:, cs].astype(F32)
            sg = _sigmoid(zb)
            sl = zb * sg
            dp_ref[:, 4 * D_MODEL + c0:4 * D_MODEL + c0 + POOL_GROUP_DIM] = (
                db * yb * ps * (sg * (1.0 + zb * (1.0 - sg)))).astype(BF16)
            dsl = db * sl
            dy = dsl * ps
            gps_ref[:, cs] += _colsum(dsl * yb)
            gpb_ref[:, cs] += _colsum(dy)
            dy_bf = dy.astype(BF16)
            gpw_ref[grp] += _dot(pooled_bf, dy_bf, TN)
            r = _dot(dy_bf, pw, NT)
            zn = zn_ref[:, cs].astype(F32)
            dyn = (dbn_ref[:, cs].astype(F32) * (zn * _sigmoid(zn)) * ps * live_next).astype(BF16)
            rn = _dot(dyn, pw, NT) / _pool_counts((n + 1) * t, POOL_HALO, POOL_WINDOWS[grp])
            ext = jnp.concatenate([r / cnt, rn], axis=0)
            dxb = _window_sum(ext, grp + 1, True)[:t, :] - r
            dp_ref[:, 3 * D_MODEL + c0:3 * D_MODEL + c0 + POOL_GROUP_DIM] = dxb.astype(BF16)

    out_shape = [
        jax.ShapeDtypeStruct((s, EVEN_IN), BF16),
        jax.ShapeDtypeStruct((GMLP_HEADS, t, t), F32), jax.ShapeDtypeStruct((t, LANES), F32),
        jax.ShapeDtypeStruct((1, GMLP_HEAD_DIM), F32), jax.ShapeDtypeStruct((1, GMLP_HEAD_DIM), F32),
        jax.ShapeDtypeStruct((POOL_GROUPS, POOL_GROUP_DIM, POOL_GROUP_DIM), F32),
        jax.ShapeDtypeStruct((1, D_MODEL), F32), jax.ShapeDtypeStruct((1, D_MODEL), F32),
    ]
    return pl.pallas_call(
        body, name=name, grid=(nblk,),
        in_specs=[col(0), col(1), col(2), col(3), col(4), prev, nxt(4),
                  pl.BlockSpec((t, D_MODEL), lambda n: (n, 0)), pl.BlockSpec((t, D_MODEL), lambda n: (n, 1)), nxt(1),
                  _full((GMLP_HEADS, t, t)), _full((GMLP_HEADS, t, t)), _full((t, LANES)),
                  _full((1, GMLP_HEAD_DIM)), _full((1, GMLP_HEAD_DIM)),
                  _full((POOL_GROUPS, POOL_GROUP_DIM, POOL_GROUP_DIM)), _full((1, D_MODEL)), _full((1, D_MODEL))],
        out_specs=[pl.BlockSpec((t, EVEN_IN), lambda n: (n, 0))] + [_full(o.shape) for o in out_shape[1:]],
        out_shape=out_shape,
        compiler_params=_params("arbitrary"),
    )(proj, proj, proj, proj, proj, proj, proj, dmix, dmix, dmix, ws, ws_t, bs_t, ng, nb, pool_w, pool_b, pool_scale)


ROPE_HALF = MLA_ROPE // 2


def _rope(v, cos, sin_signed):
    return v * cos + pltpu.roll(v, 2 * ROPE_HALF, 1) * sin_signed


def _rope_bwd(d, cos, sin_signed):
    return d * cos + pltpu.roll(d * sin_signed, 2 * ROPE_HALF, 1)


def _slab_lanes(shape, which):
    lane = lax.broadcasted_iota(jnp.int32, shape, 1)
    return (lane // ROPE_HALF) % 2 == which


def _rms(v, g):
    r = lax.rsqrt(jnp.mean(v * v, axis=-1, keepdims=True) + LN_EPS)
    return v * r * g, r


def _rms_bwd(dy, v, r, g):
    u = dy * g
    return r * u - v * (r * r * r) * jnp.mean(u * v, axis=-1, keepdims=True)


def _mla_prep(proj, gq, gkv, cos, sin_signed, *, name):
    s = proj.shape[0]
    ts = _tile(s, 512)

    def body(p_ref, gq_ref, gkv_ref, c_ref, s_ref, q_ref, k_ref):
        qcn, _ = _rms(p_ref[:, :MLA_Q_RANK].astype(F32), gq_ref[...])
        kvn, _ = _rms(p_ref[:, MLA_Q_RANK:MLA_Q_RANK + MLA_KV_RANK].astype(F32), gkv_ref[...])
        kr = p_ref[:, MLA_Q_RANK + MLA_KV_RANK:].astype(F32)
        lane = lax.broadcasted_iota(jnp.int32, kr.shape, 1)
        by1, by2 = pltpu.roll(kr, ROPE_HALF, 1), pltpu.roll(kr, 2 * ROPE_HALF, 1)
        both = jnp.where(lane < ROPE_HALF, kr, jnp.where(lane < 3 * ROPE_HALF, by1, by2))
        kr = _rope(both, c_ref[...], s_ref[...])
        q_ref[...] = qcn.astype(BF16)
        k_ref[...] = jnp.concatenate([kvn, kr], axis=1).astype(BF16)

    return pl.pallas_call(
        body, name=name, grid=(s // ts,),
        in_specs=[_small_spec(ts), _vec_spec(MLA_Q_RANK), _vec_spec(MLA_KV_RANK), _row_spec(ts, LANES), _row_spec(ts, LANES)],
        out_specs=[_row_spec(ts, MLA_Q_RANK), _row_spec(ts, QK_PAD)],
        out_shape=[jax.ShapeDtypeStruct((s, MLA_Q_RANK), BF16), jax.ShapeDtypeStruct((s, QK_PAD), BF16)],
        compiler_params=_params("parallel"),
    )(proj, gq, gkv, cos, sin_signed)


def _mla_prep_bwd(proj, dqcn, dkv, gq, gkv, cos, sin_signed, *, name):
    s = proj.shape[0]
    ts = _tile(s, 512)

    def body(p_ref, dq_ref, dkv_ref, gq_ref, gkv_ref, c_ref, s_ref, ds_ref, ggq_ref, ggkv_ref):
        @pl.when(pl.program_id(0) == 0)
        def _():
            ggq_ref[...] = jnp.zeros_like(ggq_ref)
            ggkv_ref[...] = jnp.zeros_like(ggkv_ref)

        qc = p_ref[:, :MLA_Q_RANK].astype(F32)
        kvc = p_ref[:, MLA_Q_RANK:MLA_Q_RANK + MLA_KV_RANK].astype(F32)
        _, rq = _rms(qc, gq_ref[...])
        _, rkv = _rms(kvc, gkv_ref[...])
        dq = dq_ref[...]
        dkvn = dkv_ref[:, :MLA_KV_RANK]
        ggq_ref[...] += _colsum(dq * qc * rq)
        ggkv_ref[...] += _colsum(dkvn * kvc * rkv)
        dboth = _rope_bwd(dkv_ref[:, MLA_KV_RANK:], c_ref[...], s_ref[...])
        lane = lax.broadcasted_iota(jnp.int32, dboth.shape, 1)
        pair = dboth + pltpu.roll(dboth, 3 * ROPE_HALF, 1)
        dkr = jnp.where(lane < ROPE_HALF, pair, jnp.where(lane < 2 * ROPE_HALF, pltpu.roll(pair, 3 * ROPE_HALF, 1), 0.0))
        ds_ref[...] = jnp.concatenate(
            [_rms_bwd(dq, qc, rq, gq_ref[...]), _rms_bwd(dkvn, kvc, rkv, gkv_ref[...]), dkr], axis=1).astype(BF16)

    return pl.pallas_call(
        body, name=name, grid=(s // ts,),
        in_specs=[_small_spec(ts), _row_spec(ts, MLA_Q_RANK), _row_spec(ts, QK_PAD),
                  _vec_spec(MLA_Q_RANK), _vec_spec(MLA_KV_RANK), _row_spec(ts, LANES), _row_spec(ts, LANES)],
        out_specs=[_row_spec(ts, ODD_SMALL_PAD), _vec_spec(MLA_Q_RANK), _vec_spec(MLA_KV_RANK)],
        out_shape=[jax.ShapeDtypeStruct((s, ODD_SMALL_PAD), BF16), jax.ShapeDtypeStruct((1, MLA_Q_RANK), F32),
                   jax.ShapeDtypeStruct((1, MLA_KV_RANK), F32)],
        compiler_params=_params("arbitrary"),
    )(proj, dqcn, dkv, gq, gkv, cos, sin_signed)


Q_HEAD_GROUP = 8
LOG2_E = 1.4426950408889634
Q_PRESCALE = ATTN_SCALE * LOG2_E


def _q_build(q_nope, q_rope_pre, wuk_hdr, cos, sin_signed, *, name):
    s = q_nope.shape[0]
    ts = _tile(s, 512)
    hg = Q_HEAD_GROUP

    def body(qn_ref, qr_ref, w_ref, c_ref, s_ref, o_ref):
        for pair in range(hg // 2):
            r = _rope(qr_ref[:, pair * LANES:(pair + 1) * LANES], c_ref[...], s_ref[...])
            for j in range(2):
                h = 2 * pair + j
                ql = _dot(qn_ref[:, h * MLA_NOPE:(h + 1) * MLA_NOPE], w_ref[h], NN)
                mine = jnp.where(_slab_lanes(r.shape, j), r, 0.0)
                o_ref[h] = (jnp.concatenate([ql, mine], axis=1) * Q_PRESCALE).astype(BF16)

    return pl.pallas_call(
        body, name=name, grid=(s // ts, MLA_HEADS // hg),
        in_specs=[pl.BlockSpec((ts, hg * MLA_NOPE), lambda i, p: (i, p)), pl.BlockSpec((ts, hg * MLA_ROPE), lambda i, p: (i, p)),
                  pl.BlockSpec((hg, MLA_NOPE, MLA_KV_RANK), lambda i, p: (p, 0, 0)),
                  pl.BlockSpec((ts, LANES), lambda i, p: (i, 0)), pl.BlockSpec((ts, LANES), lambda i, p: (i, 0))],
        out_specs=pl.BlockSpec((hg, ts, QK_PAD), lambda i, p: (p, i, 0)),
        out_shape=jax.ShapeDtypeStruct((MLA_HEADS, s, QK_PAD), BF16),
        compiler_params=_params("parallel", "parallel"),
    )(q_nope, q_rope_pre, wuk_hdr, cos, sin_signed)


def _q_bwd(dq, q_nope, wuk_hrd, cos, sin_signed, *, name):
    s = q_nope.shape[0]
    ts = _tile(s, 512)
    hg = Q_HEAD_GROUP

    nope_w, all_w = hg * MLA_NOPE, hg * (MLA_NOPE + MLA_ROPE)

    def body(dq_ref, qn_ref, w_ref, c_ref, s_ref, dall_ref, gw_ref):
        @pl.when(pl.program_id(1) == 0)
        def _():
            gw_ref[...] = jnp.zeros_like(gw_ref)

        for h in range(hg):
            dql = dq_ref[h, :, :MLA_KV_RANK]
            dall_ref[:, h * MLA_NOPE:(h + 1) * MLA_NOPE] = _dot(dql, w_ref[h], NN).astype(BF16)
            gw_ref[h] += _dot(dql, qn_ref[:, h * MLA_NOPE:(h + 1) * MLA_NOPE], TN)
        for pair in range(hg // 2):
            hi0 = dq_ref[2 * pair, :, MLA_KV_RANK:].astype(F32)
            hi1 = dq_ref[2 * pair + 1, :, MLA_KV_RANK:].astype(F32)
            d = jnp.where(_slab_lanes(hi0.shape, 0), hi0, hi1)
            dall_ref[:, nope_w + pair * LANES:nope_w + (pair + 1) * LANES] = _rope_bwd(d, c_ref[...], s_ref[...]).astype(BF16)

    return pl.pallas_call(
        body, name=name, grid=(MLA_HEADS // hg, s // ts),
        in_specs=[pl.BlockSpec((hg, ts, QK_PAD), lambda p, i: (p, i, 0)), pl.BlockSpec((ts, nope_w), lambda p, i: (i, p)),
                  pl.BlockSpec((hg, MLA_KV_RANK, MLA_NOPE), lambda p, i: (p, 0, 0)),
                  pl.BlockSpec((ts, LANES), lambda p, i: (i, 0)), pl.BlockSpec((ts, LANES), lambda p, i: (i, 0))],
        out_specs=[pl.BlockSpec((ts, all_w), lambda p, i: (i, p)),
                   pl.BlockSpec((hg, MLA_KV_RANK, MLA_NOPE), lambda p, i: (p, 0, 0))],
        out_shape=[jax.ShapeDtypeStruct((s, MLA_HEADS * (MLA_NOPE + MLA_ROPE)), BF16),
                   jax.ShapeDtypeStruct((MLA_HEADS, MLA_KV_RANK, MLA_NOPE), F32)],
        compiler_params=_params("parallel", "arbitrary"),
    )(dq, q_nope, wuk_hrd, cos, sin_signed)


ATTN_BQ = 128
ATTN_BK = 512


def _diag_mask(rows, bq, bk, q0, k0):
    qc = (q0 + lax.broadcasted_iota(jnp.int32, (rows, bk), 0) % bq) // CHUNK
    kc = (k0 + lax.broadcasted_iota(jnp.int32, (rows, bk), 1)) // CHUNK
    return kc <= qc


def _attn_fwd(q, k, *, name):
    nh, s, dk = q.shape
    bq, bk = _tile(s, ATTN_BQ), _tile(s, ATTN_BK)
    rows = nh * bq

    def body(q_ref, k_ref, o_ref, lse_ref):
        i = pl.program_id(0)
        qb = q_ref[...].reshape(rows, dk)
        n_before = (i * bq) // bk

        def step(j, width, carry, masked):
            m, l, acc = carry
            k0 = pl.multiple_of(j * bk, bk)
            kb = k_ref[pl.ds(k0, width), :]
            sc = _dot(qb, kb, NT)
            if masked:
                sc = jnp.where(_diag_mask(rows, bq, width, i * bq, k0), sc, NEG)
            m_new = jnp.maximum(m, jnp.max(sc, axis=1, keepdims=True))
            p = jnp.exp2(sc - m_new)
            a = jnp.exp2(m - m_new)
            l = a * l + jnp.sum(p, axis=1, keepdims=True)
            acc = a * acc + _dot(p.astype(BF16), kb[:, :MLA_KV_RANK], NN)
            return m_new, l, acc

        init = (jnp.full((rows, 1), NEG, F32), jnp.zeros((rows, 1), F32), jnp.zeros((rows, MLA_KV_RANK), F32))
        carry = lax.fori_loop(0, n_before, lambda j, c: step(j, bk, c, False), init)
        for part in range(bk // bq):
            @pl.when(i % (bk // bq) == part)
            def _(part=part):
                m, l, acc = step(n_before, (part + 1) * bq, carry, True)
                o_ref[...] = (acc / l).astype(BF16).reshape(nh, bq, MLA_KV_RANK)
                lse_ref[...] = jnp.broadcast_to(m + jnp.log2(l), (rows, LANES)).reshape(nh, bq, LANES)

    return pl.pallas_call(
        body, name=name, grid=(s // bq,),
        in_specs=[pl.BlockSpec((nh, bq, dk), lambda i: (0, i, 0)), pl.BlockSpec((s, dk), lambda i: (0, 0))],
        out_specs=[pl.BlockSpec((nh, bq, MLA_KV_RANK), lambda i: (0, i, 0)), pl.BlockSpec((nh, bq, LANES), lambda i: (0, i, 0))],
        out_shape=[jax.ShapeDtypeStruct((nh, s, MLA_KV_RANK), BF16), jax.ShapeDtypeStruct((nh, s, LANES), F32)],
        compiler_params=_params("parallel"),
    )(q, k)


def _attn_bwd(q, k, do, o, lse, *, name):
    nh, s, dk = q.shape
    bq, bk = _tile(s, ATTN_BQ), _tile(s, ATTN_BK)
    rows = nh * bq

    def body(q_ref, k_ref, do_ref, o_ref, lse_ref, dq_ref, dkv_ref):
        i = pl.program_id(0)
        n_before = (i * bq) // bk

        @pl.when(i == 0)
        def _():
            dkv_ref[...] = jnp.zeros_like(dkv_ref)

        qb = q_ref[...].reshape(rows, dk)
        dob = do_ref[...].reshape(rows, MLA_KV_RANK)
        lse_b = lse_ref[...].reshape(rows, LANES)[:, :1]
        delta = jnp.sum(dob.astype(F32) * o_ref[...].reshape(rows, MLA_KV_RANK).astype(F32), axis=1, keepdims=True)

        def step(j, width, dq, masked):
            j0 = pl.multiple_of(j * bk, bk)
            kb = k_ref[pl.ds(j0, width), :]
            sc = _dot(qb, kb, NT)
            if masked:
                sc = jnp.where(_diag_mask(rows, bq, width, i * bq, j0), sc, NEG)
            p = jnp.exp2(sc - lse_b)
            dp = _dot(dob, kb[:, :MLA_KV_RANK], NT)
            ds_bf = (p * (dp - delta)).astype(BF16)
            dkv_ref[pl.ds(j0, width), :] += _dot(ds_bf, qb, TN) * (1.0 / LOG2_E)
            dkv_ref[pl.ds(j0, width), :MLA_KV_RANK] += _dot(p.astype(BF16), dob, TN)
            return dq + _dot(ds_bf, kb, NN)

        dq_before = lax.fori_loop(0, n_before, lambda j, c: step(j, bk, c, False), jnp.zeros((rows, dk), F32))
        for part in range(bk // bq):
            @pl.when(i % (bk // bq) == part)
            def _(part=part):
                dq = step(n_before, (part + 1) * bq, dq_before, True) * ATTN_SCALE
                dq_ref[...] = dq.astype(BF16).reshape(nh, bq, dk)

    blk = lambda w: pl.BlockSpec((nh, bq, w), lambda i: (0, i, 0))
    return pl.pallas_call(
        body, name=name, grid=(s // bq,),
        in_specs=[blk(dk), pl.BlockSpec((s, dk), lambda i: (0, 0)), blk(MLA_KV_RANK), blk(MLA_KV_RANK), blk(LANES)],
        out_specs=[blk(dk), pl.BlockSpec((s, dk), lambda i: (0, 0))],
        out_shape=[jax.ShapeDtypeStruct((nh, s, dk), BF16), jax.ShapeDtypeStruct((s, dk), F32)],
        compiler_params=_params("arbitrary"),
    )(q, k, do, o, lse)


HEAD_GROUP = 4
SMALL_BLOCK = MLA_WIDTH // ODD_SMALL_PAD


def _small_spec(ts):
    return pl.BlockSpec((ts, ODD_SMALL_PAD), lambda i: (i, SMALL_BLOCK))


def _o_build(o_lat, wuv_hrv, proj, *, name):
    s = proj.shape[0]
    ts = _tile(s, 1024)
    w = HEAD_GROUP * MLA_V

    def body(ol_ref, w_ref, z_ref, og_ref):
        for j in range(HEAD_GROUP):
            cs = slice(j * MLA_V, (j + 1) * MLA_V)
            z = z_ref[:, cs].astype(F32)
            og_ref[:, cs] = (_dot(ol_ref[j], w_ref[j], NN) * (z * _sigmoid(z))).astype(BF16)

    return pl.pallas_call(
        body, name=name, grid=(s // ts, MLA_HEADS // HEAD_GROUP),
        in_specs=[pl.BlockSpec((HEAD_GROUP, ts, MLA_KV_RANK), lambda i, g: (g, i, 0)),
                  pl.BlockSpec((HEAD_GROUP, MLA_KV_RANK, MLA_V), lambda i, g: (g, 0, 0)),
                  pl.BlockSpec((ts, w), lambda i, g: (i, g))],
        out_specs=pl.BlockSpec((ts, w), lambda i, g: (i, g)),
        out_shape=jax.ShapeDtypeStruct((s, MLA_WIDTH), BF16),
        compiler_params=_params("parallel", "parallel"),
    )(o_lat, wuv_hrv, proj)


def _o_bwd(dg, proj, o_lat, wuv_hrv, wuv_hvr, *, name):
    s = proj.shape[0]
    ts = _tile(s, 1024)
    w = HEAD_GROUP * MLA_V

    def body(dg_ref, z_ref, ol_ref, w_ref, wt_ref, dol_ref, dz_ref, gw_ref):
        @pl.when(pl.program_id(1) == 0)
        def _():
            gw_ref[...] = jnp.zeros_like(gw_ref)

        for j in range(HEAD_GROUP):
            cs = slice(j * MLA_V, (j + 1) * MLA_V)
            z, dgj, ol = z_ref[:, cs].astype(F32), dg_ref[:, cs].astype(F32), ol_ref[j]
            sg = _sigmoid(z)
            o = _dot(ol, w_ref[j], NN)
            dz_ref[:, cs] = (dgj * o * (sg * (1.0 + z * (1.0 - sg)))).astype(BF16)
            do_bf = (dgj * (z * sg)).astype(BF16)
            dol_ref[j] = _dot(do_bf, wt_ref[j], NN).astype(BF16)
            gw_ref[j] += _dot(ol, do_bf, TN)

    hs = lambda a, b: pl.BlockSpec((HEAD_GROUP, a, b), lambda g, i: (g, 0, 0))
    return pl.pallas_call(
        body, name=name, grid=(MLA_HEADS // HEAD_GROUP, s // ts),
        in_specs=[pl.BlockSpec((ts, w), lambda g, i: (i, g)), pl.BlockSpec((ts, w), lambda g, i: (i, g)),
                  pl.BlockSpec((HEAD_GROUP, ts, MLA_KV_RANK), lambda g, i: (g, i, 0)),
                  hs(MLA_KV_RANK, MLA_V), hs(MLA_V, MLA_KV_RANK)],
        out_specs=[pl.BlockSpec((HEAD_GROUP, ts, MLA_KV_RANK), lambda g, i: (g, i, 0)),
                   pl.BlockSpec((ts, w), lambda g, i: (i, g)), hs(MLA_KV_RANK, MLA_V)],
        out_shape=[jax.ShapeDtypeStruct((MLA_HEADS, s, MLA_KV_RANK), BF16), jax.ShapeDtypeStruct((s, MLA_WIDTH), BF16),
                   jax.ShapeDtypeStruct((MLA_HEADS, MLA_KV_RANK, MLA_V), F32)],
        compiler_params=_params("parallel", "arbitrary"),
    )(dg, proj, o_lat, wuv_hrv, wuv_hvr)


def _ada_mod(c_all, ada_w, ada_b_sh, *, name):
    nl, _, cols = ada_w.shape

    def body(c_ref, w_ref, b_ref, o_ref):
        c = c_ref[...]
        cond = (c * _sigmoid(c)).astype(BF16)
        for l in range(nl):
            o_ref[l] = _dot(cond, w_ref[l].astype(BF16), NN) + b_ref[l]

    return pl.pallas_call(
        body, name=name, out_shape=jax.ShapeDtypeStruct((nl, c_all.shape[0], cols), F32),
        compiler_params=_params(),
    )(c_all, ada_w, ada_b_sh)


def _ada_grad(c_all_t, dmod_sh, *, name):
    nl, _, cols = dmod_sh.shape
    d = c_all_t.shape[0]

    def body(c_ref, dm_ref, gw_ref):
        c = c_ref[...]
        cond_t = c * _sigmoid(c)
        for l in range(nl):
            gw_ref[l] = lax.dot_general(cond_t, dm_ref[l], (NN, ((), ())), precision=lax.Precision.HIGHEST,
                                        preferred_element_type=F32)

    return pl.pallas_call(
        body, name=name, out_shape=jax.ShapeDtypeStruct((nl, d, cols), F32), compiler_params=_params(),
    )(c_all_t, dmod_sh)


def _sum_devices(parts, *, name):
    def body(p_ref, o_ref):
        acc = p_ref[0]
        for k in range(1, parts.shape[0]):
            acc = acc + p_ref[k]
        o_ref[...] = acc

    return pl.pallas_call(body, name=name, out_shape=jax.ShapeDtypeStruct(parts.shape[1:], F32), compiler_params=_params())(parts)


def _adamw_math(w, g, m, v):
    c1 = 1.0 - ADAM_B1 ** ADAM_STEP
    c2 = 1.0 - ADAM_B2 ** ADAM_STEP
    nm = ADAM_B1 * m + (1.0 - ADAM_B1) * g
    nv = ADAM_B2 * v + (1.0 - ADAM_B2) * (g * g)
    return -ADAM_LR * ((nm / c1) / (jnp.sqrt(nv / c2) + ADAM_EPS) + ADAM_WD * w), nm, nv


ADAMW_BLOCK_BYTES = 1 << 20


def _adamw(w, g, m, v, *, name, after=None):
    shape = w.shape
    a, b = shape[-2], shape[-1]
    lead = 1
    for dim in shape[:-2]:
        lead *= dim
    row_bytes = 4 * b
    if a * row_bytes <= ADAMW_BLOCK_BYTES:
        ta = a
        tl = max(1, min(lead, ADAMW_BLOCK_BYTES // (a * row_bytes)))
        while lead % tl:
            tl -= 1
    else:
        tl = 1
        ta = _tile(a, 256)
    to3 = lambda t: t.reshape(lead, a, b)

    def body(w_ref, g_ref, m_ref, v_ref, *rest):
        d_ref, nm_ref, nv_ref = rest[-3:]
        d_ref[...], nm_ref[...], nv_ref[...] = _adamw_math(w_ref[...], g_ref[...], m_ref[...], v_ref[...])

    spec = pl.BlockSpec((tl, ta, b), lambda i, j: (i, j, 0))
    out = jax.ShapeDtypeStruct((lead, a, b), F32)
    order = [] if after is None else [after]
    res = pl.pallas_call(
        body, name=name, grid=(lead // tl, a // ta), in_specs=[spec] * 4 + [pl.BlockSpec(memory_space=pl.ANY)] * len(order),
        out_specs=[spec] * 3, out_shape=[out] * 3, compiler_params=_params("parallel", "parallel"),
    )(to3(w), to3(g), to3(m), to3(v), *order)
    return [r.reshape(shape) for r in res]


def _adamw_small(ws, gs, ms, vs, *, name):
    n = len(ws)

    def body(*refs):
        for k in range(n):
            w_ref, g_ref, m_ref, v_ref = (refs[j * n + k] for j in range(4))
            d_ref, nm_ref, nv_ref = (refs[(4 + j) * n + k] for j in range(3))
            d_ref[...], nm_ref[...], nv_ref[...] = _adamw_math(w_ref[...], g_ref[...], m_ref[...], v_ref[...])

    outs = [jax.ShapeDtypeStruct(w.shape, F32) for w in ws]
    res = pl.pallas_call(body, name=name, out_shape=outs * 3, compiler_params=_params())(*ws, *gs, *ms, *vs)
    return res[:n], res[n:2 * n], res[2 * n:]


def _flip(v, bit):
    return 1 - v if bit else v


CHIP_DELTAS = ((1, 0), (0, 1), (1, 1))
SUM_ROWS = 32


def _all_gather_chips(shard, *, name):
    def body(x_ref, o_ref, send_sems, recv_sems, local_sem):
        x, y, c = lax.axis_index("x"), lax.axis_index("y"), lax.axis_index("c")
        mine = pltpu.make_async_copy(x_ref, o_ref.at[2 * x + y], local_sem)
        mine.start()

        def copy(k):
            tx, ty = _flip(x, CHIP_DELTAS[k][0]), _flip(y, CHIP_DELTAS[k][1])
            send = pltpu.make_async_remote_copy(src_ref=x_ref, dst_ref=o_ref.at[2 * x + y], send_sem=send_sems.at[k],
                                                recv_sem=recv_sems.at[k], device_id=(tx, ty, c), device_id_type=MESH)
            recv = pltpu.make_async_remote_copy(src_ref=x_ref, dst_ref=o_ref.at[2 * tx + ty], send_sem=send_sems.at[k],
                                                recv_sem=recv_sems.at[k], device_id=(tx, ty, c), device_id_type=MESH)
            return send, recv

        pairs = [copy(k) for k in range(3)]
        for send, _ in pairs:
            send.start()
        for _, recv in pairs:
            recv.wait_recv()
        for send, _ in pairs:
            send.wait_send()
        mine.wait()

    return pl.pallas_call(
        body, name=name, out_shape=jax.ShapeDtypeStruct((N_CHIPS,) + shard.shape, shard.dtype),
        in_specs=[HBM], out_specs=HBM,
        scratch_shapes=[pltpu.SemaphoreType.DMA((3,)), pltpu.SemaphoreType.DMA((3,)), pltpu.SemaphoreType.DMA(())],
    )(shard)


def _gather_weights(shards, *, name):
    n = len(shards)

    def body(*refs):
        w_refs, o_refs = refs[:n], refs[n:2 * n]
        ici_send, ici_recv, d2d_send, d2d_recv, local_sems = refs[2 * n:]
        x, y, c = lax.axis_index("x"), lax.axis_index("y"), lax.axis_index("c")
        me = 2 * x + y
        peers = [(_flip(x, dx), _flip(y, dy)) for dx, dy in CHIP_DELTAS]
        locals_ = [pltpu.make_async_copy(w_refs[k], o_refs[k].at[me], local_sems.at[k]) for k in range(n)]
        for cp in locals_:
            cp.start()

        def rows(k, which):
            half = shards[k].shape[0] // 2
            return pl.ds(pl.multiple_of(which * half, half), half)

        def over_chips(k, d, slot):
            tx, ty = peers[d]
            return pltpu.make_async_remote_copy(
                src_ref=w_refs[k].at[rows(k, c)], dst_ref=o_refs[k].at[slot, rows(k, c)], send_sem=ici_send.at[k, d],
                recv_sem=ici_recv.at[k, d], device_id=(tx, ty, c), device_id_type=MESH)

        def to_sibling(k, d, which):
            tx, ty = peers[d]
            at = o_refs[k].at[2 * tx + ty, rows(k, which)]
            return pltpu.make_async_remote_copy(src_ref=at, dst_ref=at, send_sem=d2d_send.at[k, d], recv_sem=d2d_recv.at[k, d],
                                                device_id=(x, y, 1 - c), device_id_type=MESH)

        sends = [over_chips(k, d, me) for k in range(n) for d in range(3)]
        for cp in sends:
            cp.start()
        passed = []
        for k in range(n):
            for d in range(3):
                over_chips(k, d, 2 * peers[d][0] + peers[d][1]).wait_recv()
                passed.append(to_sibling(k, d, c))
                passed[-1].start()
        for k in range(n):
            for d in range(3):
                to_sibling(k, d, 1 - c).wait_recv()
        for cp in sends + passed:
            cp.wait_send()
        for cp in locals_:
            cp.wait()

    return pl.pallas_call(
        body, name=name, out_shape=[jax.ShapeDtypeStruct((N_CHIPS,) + w.shape, w.dtype) for w in shards],
        in_specs=[HBM] * n, out_specs=[HBM] * n,
        scratch_shapes=[pltpu.SemaphoreType.DMA((n, 3))] * 4 + [pltpu.SemaphoreType.DMA((n,))],
    )(*shards)


def _add_into(dst_ref, src_ref):
    ns, r, _ = dst_ref.shape
    step = SUM_ROWS if r % SUM_ROWS == 0 else r
    for s in range(ns):
        def tile(t, carry):
            at = pl.ds(pl.multiple_of(t * step, step), step)
            dst_ref[s, at, :] = (dst_ref[s, at, :].astype(F32) + src_ref[s, at, :].astype(F32)).astype(dst_ref.dtype)
            return carry
        lax.fori_loop(0, r // step, tile, 0)


def _reduce_sibling(grads, *, name):
    n = len(grads)

    def body(*refs):
        g_refs, o_refs = refs[:n], refs[n:2 * n]
        mine, got = refs[2 * n:3 * n], refs[3 * n:4 * n]
        send_sems, recv_sems, load_sems, store_sems = refs[4 * n:]
        x, y, c = lax.axis_index("x"), lax.axis_index("y"), lax.axis_index("c")
        loads = [pltpu.make_async_copy(g_refs[k].at[:, c], mine[k], load_sems.at[k]) for k in range(n)]
        swaps = [pltpu.make_async_remote_copy(src_ref=g_refs[k].at[:, 1 - c], dst_ref=got[k], send_sem=send_sems.at[k],
                                              recv_sem=recv_sems.at[k], device_id=(x, y, 1 - c), device_id_type=MESH)
                 for k in range(n)]
        for cp in loads + swaps:
            cp.start()
        stores = []
        for k in range(n):
            loads[k].wait()
            swaps[k].wait_recv()
            _add_into(mine[k], got[k])
            stores.append(pltpu.make_async_copy(mine[k], o_refs[k], store_sems.at[k]))
            stores[-1].start()
        for k in range(n):
            swaps[k].wait_send()
            stores[k].wait()

    half = [jax.ShapeDtypeStruct((g.shape[0],) + g.shape[2:], g.dtype) for g in grads]
    return pl.pallas_call(
        body, name=name, out_shape=half, in_specs=[HBM] * n, out_specs=[HBM] * n,
        scratch_shapes=[pltpu.VMEM(h.shape, h.dtype) for h in half] * 2 + [pltpu.SemaphoreType.DMA((n,))] * 4,
        compiler_params=_params(),
    )(*grads)


def _reduce_chips(parts, landed, *, name):
    n_send = len(parts)
    n = n_send + len(landed)

    def body(*refs):
        p_refs, o_refs = refs[:n], refs[n:2 * n]
        got, total = refs[2 * n:3 * n], refs[3 * n:4 * n]
        send_sems, recv_sems, load_sems, share_send, share_recv, store_sems = refs[4 * n:]
        x, y, c = lax.axis_index("x"), lax.axis_index("y"), lax.axis_index("c")
        me = 2 * x + y
        peers = [(_flip(x, dx), _flip(y, dy)) for dx, dy in CHIP_DELTAS]

        def over_chips(k, d, src_slot, dst_slot):
            tx, ty = peers[d]
            return pltpu.make_async_remote_copy(
                src_ref=p_refs[k].at[src_slot], dst_ref=got[k].at[dst_slot], send_sem=send_sems.at[k, d],
                recv_sem=recv_sems.at[k, d], device_id=(tx, ty, c), device_id_type=MESH)

        loads = [pltpu.make_async_copy(p_refs[k].at[me], got[k].at[me], load_sems.at[k]) for k in range(n_send)]
        loads += [pltpu.make_async_copy(p_refs[k], got[k], load_sems.at[k]) for k in range(n_send, n)]
        sends = [over_chips(k, d, 2 * peers[d][0] + peers[d][1], me) for k in range(n_send) for d in range(3)]
        for cp in loads + sends:
            cp.start()
        shares, stores = [], []
        for k in range(n):
            loads[k].wait()
            for d in range(3 if k < n_send else 0):
                slot = 2 * peers[d][0] + peers[d][1]
                over_chips(k, d, slot, slot).wait_recv()
            r = total[k].shape[0]
            step = SUM_ROWS if r % SUM_ROWS == 0 else r

            def tile(t, carry, k=k, step=step):
                at = pl.ds(pl.multiple_of(t * step, step), step)
                acc = got[k][0, at, :].astype(F32)
                for s in range(1, N_CHIPS):
                    acc = acc + got[k][s, at, :].astype(F32)
                total[k][at, :] = acc
                return carry

            lax.fori_loop(0, r // step, tile, 0)
            stores.append(pltpu.make_async_copy(total[k], o_refs[k].at[c], store_sems.at[k]))
            shares.append(pltpu.make_async_remote_copy(
                src_ref=total[k], dst_ref=o_refs[k].at[c], send_sem=share_send.at[k], recv_sem=share_recv.at[k],
                device_id=(x, y, 1 - c), device_id_type=MESH))
            stores[-1].start()
            shares[-1].start()
        for k in range(n):
            pltpu.make_async_remote_copy(
                src_ref=total[k], dst_ref=o_refs[k].at[1 - c], send_sem=share_send.at[k], recv_sem=share_recv.at[k],
                device_id=(x, y, 1 - c), device_id_type=MESH).wait_recv()
        for cp in sends + shares:
            cp.wait_send()
        for cp in stores:
            cp.wait()

    both = list(parts) + list(landed)
    return pl.pallas_call(
        body, name=name, out_shape=[jax.ShapeDtypeStruct((2,) + p.shape[1:], F32) for p in both],
        in_specs=[HBM] * n, out_specs=[HBM] * n,
        scratch_shapes=[pltpu.VMEM(p.shape, p.dtype) for p in both] + [pltpu.VMEM(p.shape[1:], F32) for p in both]
        + [pltpu.SemaphoreType.DMA((n, 3))] * 2 + [pltpu.SemaphoreType.DMA((n,))] * 4,
        compiler_params=_params(),
    )(*both)


SEM = pl.BlockSpec(memory_space=pltpu.SEMAPHORE)
IN_FLIGHT = pltpu.SideEffectType.DATAFLOW_SIDE_EFFECTING


def _chip_copies(s_refs, l_refs, sems, scatter, theirs):
    x, y, c = lax.axis_index("x"), lax.axis_index("y"), lax.axis_index("c")
    me = 2 * x + y
    copies = []
    for k in range(len(s_refs)):
        for d, (dx, dy) in enumerate(CHIP_DELTAS):
            tx, ty = _flip(x, dx), _flip(y, dy)
            peer = 2 * tx + ty
            send_sem, recv_sem = sems[2 * (3 * k + d)], sems[2 * (3 * k + d) + 1]
            copies.append(pltpu.make_async_remote_copy(
                src_ref=s_refs[k].at[peer] if scatter else s_refs[k], dst_ref=l_refs[k].at[peer if theirs else me],
                send_sem=send_sem, recv_sem=recv_sem, device_id=(tx, ty, c), device_id_type=MESH))
    return copies


def _chips_start(srcs, lands, after, *, scatter, name):
    n = len(srcs)
    n_sem = 2 * 3 * n

    def body(*refs):
        s_refs, l_refs = refs[:n], refs[n:2 * n]
        sems = refs[2 * n + 1:2 * n + 1 + n_sem]
        token = refs[-1]
        for cp in _chip_copies(s_refs, l_refs, sems, scatter, False):
            cp.start()
        token[...] = jnp.zeros_like(token)

    hbm = lambda a: pltpu.HBM(a.shape, a.dtype)
    res = pl.pallas_call(
        body, name=name,
        out_shape=(*[pltpu.SemaphoreType.DMA(())] * n_sem, *[hbm(a) for a in srcs], *[hbm(a) for a in lands],
                   jax.ShapeDtypeStruct((8, LANES), F32)),
        in_specs=[HBM] * (2 * n) + [pl.BlockSpec(memory_space=pl.ANY)],
        out_specs=(*[SEM] * n_sem, *[HBM] * (2 * n), VMEM),
        input_output_aliases={k: n_sem + k for k in range(2 * n)},
        compiler_params=pltpu.CompilerParams(has_side_effects=IN_FLIGHT),
    )(*[pltpu.with_memory_space_constraint(a, pltpu.HBM) for a in list(srcs) + list(lands)], after)
    return res[:n_sem], res[n_sem:n_sem + n], res[n_sem + n:n_sem + 2 * n], res[-1]


def _chips_wait(sems, srcs, lands, after, *, scatter, name):
    n = len(srcs)
    n_sem = len(sems)

    def body(*refs):
        s_refs, l_refs = refs[:n], refs[n:2 * n]
        sem_refs = refs[2 * n:2 * n + n_sem]
        for cp in _chip_copies(s_refs, l_refs, sem_refs, scatter, False):
            cp.wait_send()
        for cp in _chip_copies(s_refs, l_refs, sem_refs, scatter, True):
            cp.wait_recv()

    hbm = lambda a: pltpu.HBM(a.shape, a.dtype)
    res = pl.pallas_call(
        body, name=name, out_shape=tuple(hbm(a) for a in list(srcs) + list(lands)),
        in_specs=[HBM] * (2 * n) + [SEM] * n_sem + [pl.BlockSpec(memory_space=pl.ANY)], out_specs=tuple([HBM] * (2 * n)),
        input_output_aliases={k: k for k in range(2 * n)},
        compiler_params=pltpu.CompilerParams(has_side_effects=IN_FLIGHT),
    )(*srcs, *lands, *sems, after)
    return res[n:]


def _all_gather_devices(rows, *, name, after=None):
    deltas = [(dx, dy, dc) for dx in (0, 1) for dy in (0, 1) for dc in (0, 1)][1:]
    order = [] if after is None else [after]

    def body(x_ref, *rest):
        o_ref, send_sems, recv_sems = rest[-3:]
        x, y, c = lax.axis_index("x"), lax.axis_index("y"), lax.axis_index("c")
        me = 4 * x + 2 * y + c
        o_ref[me] = x_ref[...]
        sends, recvs = [], []
        for k, (dx, dy, dc) in enumerate(deltas):
            tx, ty, tc = _flip(x, dx), _flip(y, dy), _flip(c, dc)
            sends.append(pltpu.make_async_remote_copy(src_ref=x_ref, dst_ref=o_ref.at[me], send_sem=send_sems.at[k],
                                                      recv_sem=recv_sems.at[k], device_id=(tx, ty, tc), device_id_type=MESH))
            recvs.append(pltpu.make_async_remote_copy(src_ref=x_ref, dst_ref=o_ref.at[4 * tx + 2 * ty + tc],
                                                      send_sem=send_sems.at[k], recv_sem=recv_sems.at[k],
                                                      device_id=(tx, ty, tc), device_id_type=MESH))
        for cp in sends:
            cp.start()
        for cp in recvs:
            cp.wait_recv()
        for cp in sends:
            cp.wait_send()

    return pl.pallas_call(
        body, name=name, out_shape=jax.ShapeDtypeStruct((N_DEV,) + rows.shape, rows.dtype),
        in_specs=[VMEM] + [pl.BlockSpec(memory_space=pl.ANY)] * len(order), out_specs=VMEM,
        scratch_shapes=[pltpu.SemaphoreType.DMA((N_DEV - 1,)), pltpu.SemaphoreType.DMA((N_DEV - 1,))],
    )(rows, *order)


WEIGHTS = ("ada_w", "ada_b", "ln_g", "ln_b", "e_w_in", "gmlp_norm_g", "gmlp_norm_b", "gmlp_ws", "gmlp_bs", "pool_w",
           "pool_b", "pool_scale", "e_w_out", "o_w_in", "mla_q_norm_g", "mla_kv_norm_g", "mla_w_uq", "mla_w_uk",
           "mla_w_uv", "o_w_out")
SMALL = ("ln_g", "ln_b", "gmlp_norm_g", "gmlp_norm_b", "gmlp_bs", "pool_b", "pool_scale", "mla_kv_norm_g", "mla_q_norm_g")


def _pad_cols(v, n):
    return jnp.concatenate([v, jnp.zeros((v.shape[0], n - v.shape[1]), v.dtype)], axis=1) if n > v.shape[1] else v


def _halves(g):
    return g.reshape(g.shape[0], 2, g.shape[1] // 2, g.shape[2])


def kernel(x, c, positions, ada_w, ada_b, ln_g, ln_b, e_w_in, gmlp_norm_g, gmlp_norm_b, gmlp_ws, gmlp_bs, pool_w, pool_b, pool_scale, e_w_out, o_w_in, mla_q_norm_g, mla_kv_norm_g, mla_w_uq, mla_w_uk, mla_w_uv, o_w_out, loss_target, m_ada_w, m_ada_b, m_ln_g, m_ln_b, m_e_w_in, m_gmlp_norm_g, m_gmlp_norm_b, m_gmlp_ws, m_gmlp_bs, m_pool_w, m_pool_b, m_pool_scale, m_e_w_out, m_o_w_in, m_mla_q_norm_g, m_mla_kv_norm_g, m_mla_w_uq, m_mla_w_uk, m_mla_w_uv, m_o_w_out, v_ada_w, v_ada_b, v_ln_g, v_ln_b, v_e_w_in, v_gmlp_norm_g, v_gmlp_norm_b, v_gmlp_ws, v_gmlp_bs, v_pool_w, v_pool_b, v_pool_scale, v_e_w_out, v_o_w_in, v_mla_q_norm_g, v_mla_kv_norm_g, v_mla_w_uq, v_mla_w_uk, v_mla_w_uv, v_o_w_out):
    args = dict(locals())
    weights = {n: args[n] for n in WEIGHTS}
    mom = {n: args["m_" + n] for n in WEIGHTS}
    var = {n: args["v_" + n] for n in WEIGHTS}
    ax, ay, ac = lax.axis_index("x"), lax.axis_index("y"), lax.axis_index("c")
    chip = 2 * ax + ay
    dev = 2 * chip + ac
    d = D_MODEL
    x2 = x[0]
    target = loss_target[0]
    q_rank_sh = mla_q_norm_g.shape[1]

    empty_zone = lambda w: lax.dynamic_update_slice(lax.empty((N_CHIPS,) + w.shape, w.dtype), w[None], (chip, 0, 0))
    shards0 = [w.astype(BF16) for w in (pool_w[0].reshape(-1, POOL_GROUP_DIM), e_w_out[0])]
    shards1 = [w.astype(BF16) for w in (o_w_in[0], mla_w_uq[0].reshape(q_rank_sh, -1), o_w_out[0])]
    w_in0, = _gather_weights([e_w_in[0].astype(BF16)], name="gather_weights")
    wuk_hrd = jnp.transpose(mla_w_uk[0], (1, 0, 2)).astype(BF16)
    wuk_hdr = jnp.transpose(mla_w_uk[0], (1, 2, 0)).astype(BF16)
    wuv_hrv = jnp.transpose(mla_w_uv[0], (1, 0, 2)).astype(BF16)
    wuv_hvr = jnp.transpose(mla_w_uv[0], (1, 2, 0)).astype(BF16)
    ws = gmlp_ws[0]
    ws_t = jnp.transpose(ws, (0, 2, 1))
    bs_t = _pad_cols(gmlp_bs[0].T, LANES)

    inv = 1.0 / (ROPE_THETA ** (jnp.arange(0, MLA_ROPE, 2, dtype=F32) / MLA_ROPE))
    ang = positions[0].astype(F32)[:, None] * inv
    cos_t = jnp.tile(jnp.cos(ang), (1, 4))
    sin_t = jnp.concatenate([-jnp.sin(ang), -jnp.sin(ang), jnp.sin(ang), jnp.sin(ang)], axis=1)

    c_all = _all_gather_devices(c.reshape(8, LANES), after=w_in0, name="gather_c").reshape(N_DEV, d)
    cols = ada_w.shape[2]
    ada_b_mine = lax.dynamic_slice_in_dim(ada_b, chip * cols, cols, axis=1)[:, None, :]
    mod_sh = _ada_mod(c_all, ada_w, ada_b_mine, name="ada_mod")
    q_norm_rows = jnp.zeros((8, cols), F32).at[0, :q_rank_sh].set(mla_q_norm_g[0])
    mod_all = _all_gather_chips(jnp.concatenate([mod_sh.reshape(2 * N_DEV, cols), q_norm_rows]), name="gather_mod")
    q_norm_g = mod_all[:, 2 * N_DEV, :q_rank_sh].reshape(1, -1)
    mod_all = jnp.transpose(mod_all[:, :2 * N_DEV].reshape(N_CHIPS, 2, N_DEV, cols), (1, 2, 0, 3)).reshape(2, N_DEV, 3 * d)
    mod = lax.dynamic_index_in_dim(mod_all, dev, axis=1, keepdims=False)
    shift = [mod[l:l + 1, :d] for l in range(2)]
    scale = [mod[l:l + 1, d:2 * d] for l in range(2)]
    gate = [mod[l:l + 1, 2 * d:] for l in range(2)]
    flight0 = _chips_start(shards0, [empty_zone(w) for w in shards0], mod, scatter=False, name="gather0_start")
    flight1 = _chips_start(shards1, [empty_zone(w) for w in shards1], flight0[3], scatter=False, name="gather1_start")

    scale[0] = scale[0] + flight1[3][:1, :1]
    h0 = _modulate(x2, scale[0], shift[0], name="modulate0")
    proj0 = _matmul(h0, w_in0, b_stacked=True, tm=1024, tn=1280, out_dtype=BF16, name="proj0")
    pool_w_g, w_out0 = _chips_wait(*flight0[:3], proj0, scatter=False, name="gather0_wait")
    pool_w_bf = jnp.transpose(pool_w_g.reshape(N_CHIPS, POOL_GROUPS, -1, POOL_GROUP_DIM), (1, 0, 2, 3)).reshape(
        POOL_GROUPS, POOL_GROUP_DIM, POOL_GROUP_DIM)
    w_out0 = w_out0.reshape(-1, d)
    mix0 = _even_fwd(proj0, ws, bs_t, gmlp_norm_g, gmlp_norm_b, pool_w_bf, pool_b, pool_scale, name="even_fwd")
    y0, x1, h1 = _out_resid_ln(mix0, w_out0, x2, gate[0], ln_g[0:1], ln_b[0:1], scale[1], shift[1], name="out0_ln")

    w_in1_g, w_uq_g, w_out1 = _chips_wait(*flight1[:3], h1, scatter=False, name="gather1_wait")
    w_out1 = w_out1.reshape(-1, d)
    w_in1 = jnp.transpose(w_in1_g, (1, 0, 2)).reshape(d, ODD_IN)
    w_in1 = jnp.concatenate([w_in1[:, ODD_SMALL:], _pad_cols(w_in1[:, :ODD_SMALL], ODD_SMALL_PAD)], axis=1)
    w_uq = w_uq_g.reshape(MLA_Q_RANK, MLA_HEADS, MLA_NOPE + MLA_ROPE)
    w_uq_nope = w_uq[:, :, :MLA_NOPE].reshape(MLA_Q_RANK, -1)
    w_uq_rope = jnp.transpose(w_uq[:, :, MLA_NOPE:].reshape(MLA_Q_RANK, MLA_HEADS // 2, 2, 2, ROPE_HALF),
                              (0, 1, 3, 2, 4)).reshape(MLA_Q_RANK, -1)
    proj1 = _matmul(h1, w_in1, tm=1024, tn=1280, out_dtype=BF16, name="proj1")
    q_cn, keys = _mla_prep(proj1, q_norm_g, mla_kv_norm_g, cos_t, sin_t, name="mla_prep")
    q_nope = _matmul(q_cn, w_uq_nope, tm=1024, tn=2048, name="q_nope", out_dtype=BF16)
    q_rope_pre = _matmul(q_cn, w_uq_rope, tm=1024, name="q_rope")
    q = _q_build(q_nope, q_rope_pre, wuk_hdr, cos_t, sin_t, name="q_build")
    o_lat, lse = _attn_fwd(q, keys, name="attn_fwd")
    og = _o_build(o_lat, wuv_hrv, proj1, name="o_build")

    dy1, dres1, g_ln_g1, g_ln_b1, dgate1, loss = _out_loss_ln_bwd(
        og, w_out1, x1, gate[1], ln_g[1:2], ln_b[1:2], target, name="out1_loss_ln")
    dg1 = _matmul(dy1, w_out1, trans_b=True, tn=2048, out_dtype=BF16, name="d_og")
    g_w_out1 = _matmul(og, dy1, trans_a=True, out_dtype=BF16, tm=1024, name="g_out1")
    do_lat, dz, g_uv = _o_bwd(dg1, proj1, o_lat, wuv_hrv, wuv_hvr, name="o_bwd")
    dq, dkeys = _attn_bwd(q, keys, do_lat, o_lat, lse, name="attn_bwd")
    dq_all, g_uk = _q_bwd(dq, q_nope, wuk_hrd, cos_t, sin_t, name="q_bwd")
    n_grp = MLA_HEADS // Q_HEAD_GROUP
    w_uq_all = jnp.concatenate([w_uq_nope.reshape(MLA_Q_RANK, n_grp, -1), w_uq_rope.reshape(MLA_Q_RANK, n_grp, -1)],
                               axis=2).reshape(MLA_Q_RANK, -1)
    dq_cn = _matmul(dq_all, w_uq_all, trans_b=True, tm=1024, name="d_qcn")
    g_uq_all = _matmul(q_cn, dq_all, trans_a=True, out_dtype=BF16, name="g_uq").reshape(MLA_Q_RANK, n_grp, -1)
    g_uq_nope = g_uq_all[:, :, :Q_HEAD_GROUP * MLA_NOPE].reshape(MLA_Q_RANK, -1)
    g_uq_rope = g_uq_all[:, :, Q_HEAD_GROUP * MLA_NOPE:].reshape(MLA_Q_RANK, -1)
    dsmall, g_qg, g_kvg = _mla_prep_bwd(proj1, dq_cn, dkeys, q_norm_g, mla_kv_norm_g, cos_t, sin_t, name="mla_prep_bwd")
    dproj1 = jnp.concatenate([dz, dsmall], axis=1)
    g_w_in1 =_matmul(h1, dproj1, trans_a=True, out_dtype=BF16, tm=1024, tn=1280, name="g_in1")

    g_uq_rope = jnp.transpose(g_uq_rope.reshape(MLA_Q_RANK, MLA_HEADS // 2, 2, 2, ROPE_HALF), (0, 1, 3, 2, 4))
    g_uq = jnp.concatenate([g_uq_nope.reshape(MLA_Q_RANK, MLA_HEADS, MLA_NOPE), g_uq_rope.reshape(MLA_Q_RANK, MLA_HEADS, MLA_ROPE)], axis=2)
    g_w_in1 = jnp.concatenate([g_w_in1[:, MLA_WIDTH:MLA_WIDTH + ODD_SMALL], g_w_in1[:, :MLA_WIDTH]], axis=1)
    g_w_in1 = jnp.transpose(g_w_in1.reshape(d, N_CHIPS, -1), (1, 0, 2))
    big1 = [
        _halves(g_w_in1),
        _halves(g_uq.reshape(N_CHIPS, q_rank_sh, -1)),
        _halves(g_w_out1.reshape(N_CHIPS, -1, d)),
        _halves(g_uk.astype(BF16).reshape(N_CHIPS, -1, MLA_NOPE)),
        _halves(g_uv.astype(BF16).reshape(N_CHIPS, -1, MLA_V)),
    ]
    parts1 = _reduce_sibling(big1, name="reduce_sibling1")
    lands2 = [lax.dynamic_update_slice(lax.empty(p.shape, BF16), lax.dynamic_slice_in_dim(p, chip, 1, axis=0), (chip, 0, 0))
              for p in parts1]
    flight2 = _chips_start(parts1, lands2, loss, scatter=True, name="reduce1_start")

    gate[0] = gate[0] + flight2[3][:1, :1]
    dy0, dres0, g_ln_g0, g_ln_b0, dgate0, dscale1, dshift1 = _dh_mid_ln_bwd(
        dproj1, w_in1, x2, y0, gate[0], ln_g[0:1], ln_b[0:1], dres1, scale[1], x1, name="d_h1_mid_ln")
    dmix0 = _matmul(dy0, w_out0, trans_b=True, tn=2048, out_dtype=BF16, name="d_mix0")
    g_w_out0 = _matmul(mix0, dy0, trans_a=True, out_dtype=BF16, tm=1024, name="g_out0")
    dproj0, g_ws, g_bs_t, g_ng, g_nb, g_pw, g_pb, g_ps = _even_bwd(
        proj0, dmix0, ws, ws_t, bs_t, gmlp_norm_g, gmlp_norm_b, pool_w_bf, pool_b, pool_scale, name="even_bwd")
    g_w_in0 = _matmul(h0, dproj0, trans_a=True, out_dtype=BF16, out_stacked=True, tm=1024, tn=1280, name="g_in0")

    g_pw = jnp.transpose(g_pw.astype(BF16).reshape(POOL_GROUPS, N_CHIPS, -1, POOL_GROUP_DIM), (1, 0, 2, 3))
    big0 = [
        _halves(g_w_in0),
        _halves(g_pw.reshape(N_CHIPS, -1, POOL_GROUP_DIM)),
        _halves(g_w_out0.reshape(N_CHIPS, -1, d)),
        _halves(g_ws.astype(BF16)),
    ]
    parts0 = _reduce_sibling(big0, name="reduce_sibling0")
    landed1 = _chips_wait(*flight2[:3], parts0[0], scatter=True, name="reduce1_wait")
    lands3 = [lax.dynamic_update_slice(lax.empty(p.shape, BF16), lax.dynamic_slice_in_dim(p, chip, 1, axis=0), (chip, 0, 0))
              for p in parts0]
    flight3 = _chips_start(parts0, lands3, landed1[0], scatter=True, name="reduce0_start")
    grad_x, dscale0, dshift0 = _dh_input_bwd(dproj0, w_in0, x2, dres0, scale[0], after=flight3[3], name="d_h0_input")

    small_local = {
        "ln_g": jnp.concatenate([g_ln_g0, g_ln_g1]), "ln_b": jnp.concatenate([g_ln_b0, g_ln_b1]),
        "gmlp_norm_g": g_ng, "gmlp_norm_b": g_nb, "gmlp_bs": g_bs_t[:, :GMLP_HEADS].T, "pool_b": g_pb, "pool_scale": g_ps,
        "mla_kv_norm_g": g_kvg, "mla_q_norm_g": g_qg,
    }
    n_mod = 2 * 3 * d
    vec = jnp.concatenate([dshift0, dscale0, dgate0, dshift1, dscale1, dgate1]
                          + [small_local[n].reshape(1, -1) for n in SMALL] + [loss], axis=1)
    n_vec = vec.shape[1]
    vec = _pad_cols(vec, -(-n_vec // (8 * LANES)) * 8 * LANES).reshape(-1, LANES)
    vec_all = _all_gather_devices(vec, name="gather_small")
    vec_sum = _sum_devices(vec_all, name="sum_small").reshape(-1)
    dmod_all = vec_all.reshape(N_DEV, -1)[:, :n_mod].reshape(N_DEV, 2, 3 * d)
    dmod_sh = jnp.transpose(lax.dynamic_slice_in_dim(dmod_all, chip * cols, cols, axis=2), (1, 0, 2))
    dmod_sh = jnp.concatenate([dmod_sh, jnp.zeros((2, LANES - N_DEV, cols), F32)], axis=1)
    grads = {"ada_w": _ada_grad(_pad_cols(c_all.T, LANES), dmod_sh, name="ada_grad"), "ada_b": vec_sum[:n_mod].reshape(2, 3 * d)}
    off = n_mod
    for n in SMALL:
        sz = small_local[n].size
        grads[n] = vec_sum[off:off + sz]
        off += sz
    grads["mla_q_norm_g"] = lax.dynamic_slice_in_dim(grads["mla_q_norm_g"], chip * q_rank_sh, q_rank_sh)
    for n in SMALL:
        grads[n] = grads[n].reshape(weights[n].shape)

    landed0 = _chips_wait(*flight3[:3], grads["ada_w"], scatter=True, name="reduce0_wait")
    totals = _reduce_chips([], list(landed0) + list(landed1), name="reduce_chips")
    for n, t in zip(("e_w_in", "pool_w", "e_w_out", "gmlp_ws", "o_w_in", "mla_w_uq", "o_w_out"), totals):
        if n != "gmlp_ws":
            grads[n] = t.reshape(weights[n].shape)
    rep = jnp.concatenate([t.reshape(-1, LANES) for t in (totals[3], totals[7], totals[8])])
    rep_land = lax.dynamic_update_slice(lax.empty((N_CHIPS,) + rep.shape, F32), rep[None], (chip, 0, 0))
    flight4 = _chips_start([rep], [rep_land], totals[0], scatter=False, name="gather_rep_start")

    delta, new_m, new_v = {}, {}, {}
    replicated = ("gmlp_ws", "mla_w_uk", "mla_w_uv")
    large = [n for n in WEIGHTS if n not in SMALL and n != "ada_b"]
    for n in large:
        if n not in replicated:
            delta[n], new_m[n], new_v[n] = _adamw(weights[n], grads[n], mom[n], var[n], after=flight4[3], name="adamw_" + n)
    rep = _chips_wait(*flight4[:3], delta["e_w_in"], scatter=False, name="gather_rep_wait")[0]
    r_ws, r_uk = GMLP_BLOCK, 4 * MLA_KV_RANK
    grads["gmlp_ws"] = rep[:, :r_ws].reshape(weights["gmlp_ws"].shape)
    grads["mla_w_uk"] = jnp.transpose(rep[:, r_ws:r_ws + r_uk].reshape(MLA_HEADS, MLA_KV_RANK, MLA_NOPE), (1, 0, 2))[None]
    grads["mla_w_uv"] = jnp.transpose(rep[:, r_ws + r_uk:].reshape(MLA_HEADS, MLA_KV_RANK, MLA_V), (1, 0, 2))[None]
    for n in replicated:
        delta[n], new_m[n], new_v[n] = _adamw(weights[n], grads[n], mom[n], var[n], name="adamw_" + n)
    small = [n for n in WEIGHTS if n not in large]
    ds, ms, vs = _adamw_small([weights[n] for n in small], [grads[n] for n in small], [mom[n] for n in small],
                              [var[n] for n in small], name="adamw_small")
    for n, dn, mn, vn in zip(small, ds, ms, vs):
        delta[n], new_m[n], new_v[n] = dn, mn, vn

    return (vec_sum[n_vec - 1], grad_x[None], *[grads[n] for n in WEIGHTS], *[delta[n] for n in WEIGHTS],
            *[new_m[n] for n in WEIGHTS], *[new_v[n] for n in WEIGHTS])
```

```python
import jax
import jax.numpy as jnp
from jax import lax
from jax.experimental import pallas as pl
from jax.experimental.pallas import tpu as pltpu

F32 = jnp.float32
BF16 = jnp.bfloat16
MESH = pl.DeviceIdType.MESH

D_MODEL = 1024
CHUNK = 64
LN_EPS = 1e-5
GMLP_HEADS = 4
GMLP_HEAD_DIM = 256
GMLP_BLOCK = 128
POOL_WINDOWS = (2, 4, 8, 16)
POOL_GROUPS = 4
POOL_GROUP_DIM = 256
POOL_HALO = 16
EVEN_IN = 5120
MLA_HEADS = 16
MLA_NOPE = 128
MLA_ROPE = 64
MLA_V = 128
MLA_Q_RANK = 256
MLA_KV_RANK = 128
MLA_WIDTH = MLA_HEADS * MLA_V
ODD_IN = 2496
ODD_SMALL = MLA_Q_RANK + MLA_KV_RANK + MLA_ROPE
ODD_SMALL_PAD = 512
QK_PAD = 256
ROPE_THETA = 10000.0
ATTN_SCALE = (MLA_NOPE + MLA_ROPE) ** -0.5
DEEPNORM_ALPHA = (2.0 * 2) ** 0.25
ADAM_LR = 0.001
ADAM_B1 = 0.9
ADAM_B2 = 0.999
ADAM_EPS = 1e-08
ADAM_WD = 0.01
ADAM_STEP = 10
NEG = -1e30
LANES = 128
N_DEV = 8
N_CHIPS = 4
VMEM_LIMIT_BYTES = 56 * 1024 * 1024
HBM = pl.BlockSpec(memory_space=pltpu.HBM)
VMEM = pl.BlockSpec(memory_space=pltpu.VMEM)


def _params(*sem):
    return pltpu.CompilerParams(dimension_semantics=sem if sem else None, vmem_limit_bytes=VMEM_LIMIT_BYTES)


def _tile(dim, pref):
    for t in (pref, 2048, 1280, 1024, 512, 256, 128):
        if t <= min(pref, dim) and dim % t == 0:
            return t
    return dim


def _sigmoid(z):
    return 1.0 / (1.0 + jnp.exp(-z))


def _dot(a, b, dims):
    return lax.dot_general(a, b, (dims, ((), ())), preferred_element_type=F32)


NN = ((1,), (0,))
NT = ((1,), (1,))
TN = ((0,), (0,))


def _matmul(a, b, *, name, trans_a=False, trans_b=False, out_dtype=F32, b_stacked=False, out_stacked=False,
            tm=512, tn=1024, tk=2048, after=None):
    k, m = a.shape if trans_a else a.shape[::-1]
    if b_stacked:
        ns, kb, n_sh = b.shape
        kb, n = (ns * n_sh, kb) if trans_b else (kb, ns * n_sh)
    else:
        n, kb = b.shape if trans_b else b.shape[::-1]
    assert k == kb, (a.shape, b.shape)
    tm = _tile(m, tm)
    if b_stacked and trans_b:
        tn, tk = _tile(n, tn), n_sh
    elif b_stacked or out_stacked:
        tn, tk = _tile(n // N_CHIPS, tn), _tile(k, tk)
    else:
        tn, tk = _tile(n, tn), _tile(k, tk)
    nk = k // tk
    per = max((n // N_CHIPS) // tn, 1)
    dims = ((0 if trans_a else 1,), (1 if trans_b else 0,))

    def body_one(a_ref, b_ref, *rest):
        o_ref = rest[-1]
        o_ref[...] = _dot(a_ref[...].astype(BF16), b_ref[...].astype(BF16), dims).astype(out_dtype)

    def body_acc(a_ref, b_ref, *rest):
        o_ref, acc_ref = rest[-2:]
        kk = pl.program_id(2)

        @pl.when(kk == 0)
        def _():
            acc_ref[...] = jnp.zeros_like(acc_ref)

        acc_ref[...] += _dot(a_ref[...].astype(BF16), b_ref[...].astype(BF16), dims)

        @pl.when(kk == nk - 1)
        def _():
            o_ref[...] = acc_ref[...].astype(out_dtype)

    a_spec = pl.BlockSpec((tk, tm), lambda i, j, kk: (kk, i)) if trans_a else pl.BlockSpec((tm, tk), lambda i, j, kk: (i, kk))
    if b_stacked and trans_b:
        b_spec = pl.BlockSpec((None, tn, tk), lambda i, j, kk: (kk, j, 0))
    elif b_stacked:
        b_spec = pl.BlockSpec((None, tk, tn), lambda i, j, kk: (j // per, kk, j % per))
    elif trans_b:
        b_spec = pl.BlockSpec((tn, tk), lambda i, j, kk: (j, kk))
    else:
        b_spec = pl.BlockSpec((tk, tn), lambda i, j, kk: (kk, j))
    if out_stacked:
        o_spec = pl.BlockSpec((None, tm, tn), lambda i, j, kk: (j // per, i, j % per))
        o_shape = jax.ShapeDtypeStruct((N_CHIPS, m, n // N_CHIPS), out_dtype)
    else:
        o_spec = pl.BlockSpec((tm, tn), lambda i, j, kk: (i, j))
        o_shape = jax.ShapeDtypeStruct((m, n), out_dtype)
    order = [] if after is None else [after]
    return pl.pallas_call(
        body_one if nk == 1 else body_acc, name=name, grid=(m // tm, n // tn, nk),
        in_specs=[a_spec, b_spec] + [pl.BlockSpec(memory_space=pl.ANY)] * len(order),
        out_specs=o_spec, out_shape=o_shape, scratch_shapes=[] if nk == 1 else [pltpu.VMEM((tm, tn), F32)],
        compiler_params=_params("parallel", "parallel", "arbitrary"),
    )(a, b, *order)


def _matmul_rows(a, b, epilogue, row_ins, vec_ins, row_outs, vec_outs, *, name, trans_b=False, b_stacked=False,
                 tm=512, tk=2048, after=None):
    m, k = a.shape
    if b_stacked:
        ns, n, n_sh = b.shape
        assert trans_b and ns * n_sh == k
        tk = n_sh
    else:
        n = b.shape[0] if trans_b else b.shape[1]
        tk = _tile(k, tk)
    tm = _tile(m, tm)
    nk = k // tk
    dims = ((1,), (1 if trans_b else 0,))
    n_ri, n_vi, n_ro, n_vo = len(row_ins), len(vec_ins), len(row_outs), len(vec_outs)
    order = [] if after is None else [after]

    def body(*refs):
        a_ref, b_ref = refs[:2]
        pos = 2
        rin = refs[pos:pos + n_ri]
        pos += n_ri
        vin = refs[pos:pos + n_vi]
        pos += n_vi + len(order)
        rout = refs[pos:pos + n_ro]
        pos += n_ro
        vout = refs[pos:pos + n_vo]
        first = pl.program_id(0) == 0
        part = _dot(a_ref[...].astype(BF16), b_ref[...].astype(BF16), dims)
        if nk == 1:
            epilogue(part, first, rin, vin, rout, vout)
        else:
            acc_ref = refs[-1]
            kk = pl.program_id(1)

            @pl.when(kk == 0)
            def _():
                acc_ref[...] = part

            @pl.when(kk > 0)
            def _():
                acc_ref[...] += part

            @pl.when(kk == nk - 1)
            def _():
                epilogue(acc_ref[...], first, rin, vin, rout, vout)

    a_spec = pl.BlockSpec((tm, tk), lambda i, kk: (i, kk))
    if b_stacked:
        b_spec = pl.BlockSpec((None, n, tk), lambda i, kk: (kk, 0, 0))
    elif trans_b:
        b_spec = pl.BlockSpec((n, tk), lambda i, kk: (0, kk))
    else:
        b_spec = pl.BlockSpec((tk, n), lambda i, kk: (kk, 0))
    row = pl.BlockSpec((tm, n), lambda i, kk: (i, 0))
    vec = lambda w: pl.BlockSpec((1, w), lambda i, kk: (0, 0))
    return pl.pallas_call(
        body, name=name, grid=(m // tm, nk),
        in_specs=[a_spec, b_spec] + [row] * n_ri + [vec(v.shape[1]) for v in vec_ins] + [pl.BlockSpec(memory_space=pl.ANY)] * len(order),
        out_specs=[row] * n_ro + [vec(w) for w in vec_outs],
        out_shape=[jax.ShapeDtypeStruct((m, n), dt) for dt in row_outs] + [jax.ShapeDtypeStruct((1, w), F32) for w in vec_outs],
        scratch_shapes=[] if nk == 1 else [pltpu.VMEM((tm, n), F32)],
        compiler_params=_params("arbitrary", "arbitrary"),
    )(a, b, *row_ins, *vec_ins, *order)


def _row_spec(ts, d):
    return pl.BlockSpec((ts, d), lambda i: (i, 0))


def _vec_spec(d):
    return pl.BlockSpec((1, d), lambda i: (0, 0))


def _modulate(x, scale, shift, *, name):
    s, d = x.shape
    ts = _tile(s, 512)

    def body(x_ref, sc_ref, sh_ref, h_ref):
        h_ref[...] = (x_ref[...] * (1.0 + sc_ref[...]) + sh_ref[...]).astype(BF16)

    return pl.pallas_call(
        body, name=name, grid=(s // ts,), in_specs=[_row_spec(ts, d), _vec_spec(d), _vec_spec(d)],
        out_specs=_row_spec(ts, d), out_shape=jax.ShapeDtypeStruct((s, d), BF16), compiler_params=_params("parallel"),
    )(x, scale, shift)


def _ln_stats(pre):
    mu = jnp.mean(pre, axis=-1, keepdims=True)
    xc = pre - mu
    var = jnp.mean(xc * xc, axis=-1, keepdims=True)
    rstd = lax.rsqrt(var + LN_EPS)
    return xc * rstd, rstd


def _ln_bwd_rows(dout, xhat, rstd, g):
    dxh = dout * g
    m1 = jnp.mean(dxh, axis=-1, keepdims=True)
    m2 = jnp.mean(dxh * xhat, axis=-1, keepdims=True)
    return rstd * (dxh - m1 - xhat * m2)


def _colsum(v):
    return jnp.sum(v, axis=0, keepdims=True)


def _out_resid_ln(mix, w_out, x, gate, g, b, scale_next, shift_next, *, name):
    def epilogue(y, first, rin, vin, rout, vout):
        (x_ref,), (gate_ref, g_ref, b_ref, sc_ref, sh_ref), (y_ref, xn_ref, h_ref) = rin, vin, rout
        y_ref[...] = y
        pre = DEEPNORM_ALPHA * x_ref[...] + (1.0 + gate_ref[...]) * y
        xhat, _ = _ln_stats(pre)
        xn = xhat * g_ref[...] + b_ref[...]
        xn_ref[...] = xn
        h_ref[...] = (xn * (1.0 + sc_ref[...]) + sh_ref[...]).astype(BF16)

    return _matmul_rows(mix, w_out, epilogue, [x], [gate, g, b, scale_next, shift_next], [F32, F32, BF16], [], name=name)


def _out_loss_ln_bwd(og, w_out, x, gate, g, b, target, *, name):
    d = x.shape[1]

    def epilogue(yv, first, rin, vin, rout, vout):
        (x_ref, t_ref), (gate_ref, g_ref, b_ref), (dy_ref, dres_ref), (dg_ref, db_ref, dgate_ref, loss_ref) = rin, vin, rout, vout

        @pl.when(first)
        def _():
            for r in vout:
                r[...] = jnp.zeros_like(r)

        pre = DEEPNORM_ALPHA * x_ref[...] + (1.0 + gate_ref[...]) * yv
        xhat, rstd = _ln_stats(pre)
        diff = xhat * g_ref[...] + b_ref[...] - t_ref[...]
        loss_ref[...] += (0.5 / d) * jnp.sum(jnp.sum(diff * diff, axis=1, keepdims=True), axis=0, keepdims=True)
        dout = diff * (1.0 / d)
        dpre = _ln_bwd_rows(dout, xhat, rstd, g_ref[...])
        dy_ref[...] = (dpre * (1.0 + gate_ref[...])).astype(BF16)
        dres_ref[...] = DEEPNORM_ALPHA * dpre
        dg_ref[...] += _colsum(dout * xhat)
        db_ref[...] += _colsum(dout)
        dgate_ref[...] += _colsum(dpre * yv)

    return _matmul_rows(og, w_out, epilogue, [x, target], [gate, g, b], [BF16, F32], [d, d, d, 1], name=name)


def _dh_mid_ln_bwd(dproj, w_in, x, y, gate, g, b, dres_next, scale_next, x_next, *, name):
    d = x.shape[1]

    def epilogue(dh, first, rin, vin, rout, vout):
        (x_ref, y_ref, dr_ref, xn_ref), (gate_ref, g_ref, b_ref, sc_ref), (dy_ref, dres_ref) = rin, vin, rout
        dg_ref, db_ref, dgate_ref, dscale_ref, dshift_ref = vout

        @pl.when(first)
        def _():
            for r in vout:
                r[...] = jnp.zeros_like(r)

        dout = dr_ref[...] + dh * (1.0 + sc_ref[...])
        dscale_ref[...] += _colsum(dh * xn_ref[...])
        dshift_ref[...] += _colsum(dh)
        yv = y_ref[...]
        pre = DEEPNORM_ALPHA * x_ref[...] + (1.0 + gate_ref[...]) * yv
        xhat, rstd = _ln_stats(pre)
        dpre = _ln_bwd_rows(dout, xhat, rstd, g_ref[...])
        dy_ref[...] = (dpre * (1.0 + gate_ref[...])).astype(BF16)
        dres_ref[...] = DEEPNORM_ALPHA * dpre
        dg_ref[...] += _colsum(dout * xhat)
        db_ref[...] += _colsum(dout)
        dgate_ref[...] += _colsum(dpre * yv)

    return _matmul_rows(dproj, w_in, epilogue, [x, y, dres_next, x_next], [gate, g, b, scale_next], [BF16, F32], [d] * 5,
                        trans_b=True, tk=1280, name=name)


def _dh_input_bwd(dproj, w_in_stacked, x, dres, scale, *, name, after):
    d = x.shape[1]

    def epilogue(dh, first, rin, vin, rout, vout):
        (x_ref, dr_ref), (sc_ref,), (dx_ref,), (dscale_ref, dshift_ref) = rin, vin, rout, vout

        @pl.when(first)
        def _():
            for r in vout:
                r[...] = jnp.zeros_like(r)

        dx_ref[...] = dr_ref[...] + dh * (1.0 + sc_ref[...])
        dscale_ref[...] += _colsum(dh * x_ref[...])
        dshift_ref[...] += _colsum(dh)

    return _matmul_rows(dproj, w_in_stacked, epilogue, [x, dres], [scale], [F32], [d, d], trans_b=True, b_stacked=True,
                        tm=1024, after=after, name=name)


def _chunk_mask(transposed=False):
    r = lax.broadcasted_iota(jnp.int32, (GMLP_BLOCK, GMLP_BLOCK), 0) // CHUNK
    c = lax.broadcasted_iota(jnp.int32, (GMLP_BLOCK, GMLP_BLOCK), 1) // CHUNK
    return (r <= c) if transposed else (c <= r)


def _window_sum(ext, steps, forward):
    rows = ext.shape[0]
    acc = ext
    for k in range(steps):
        shift = 1 << k
        acc = acc + pltpu.roll(acc, (rows - shift) if forward else shift, 0)
    return acc


def _pool_counts(first_row, rows, win):
    t = first_row + lax.broadcasted_iota(jnp.int32, (rows, 1), 0)
    return jnp.minimum(t + 1, win).astype(F32)


def _even_specs(t):
    col = lambda j: pl.BlockSpec((t, D_MODEL), lambda n: (n, j))
    per = t // POOL_HALO
    prev = pl.BlockSpec((POOL_HALO, D_MODEL), lambda n: (jnp.maximum(n * per - 1, 0), 3))
    return col, per, prev


def _full(shape):
    return pl.BlockSpec(shape, lambda n: (0,) * len(shape))


def _gmlp_head(v_h, ng, nb, w_bf):
    xhat, rstd = _ln_stats(v_h)
    vn = (xhat * ng + nb).astype(BF16)
    return xhat, rstd, vn, _dot(w_bf, vn, NN)


def _pool_group(xb_g, prev_g, first_row, grp):
    t = xb_g.shape[0]
    ext = jnp.concatenate([prev_g, xb_g], axis=0)
    tot = _window_sum(ext, grp + 1, False)[POOL_HALO:, :]
    cnt = _pool_counts(first_row, t, POOL_WINDOWS[grp])
    return tot / cnt - xb_g, cnt


def _even_fwd(proj, ws, bs_t, ng, nb, pool_w, pool_b, pool_scale, *, name):
    s = proj.shape[0]
    t = GMLP_BLOCK
    col, per, prev = _even_specs(t)

    def body(u_ref, v_ref, za_ref, xb_ref, zb_ref, xp_ref, ws_ref, bs_ref, ng_ref, nb_ref, pw_ref, pb_ref, ps_ref, o_ref):
        n = pl.program_id(0)
        mask = _chunk_mask()
        for h in range(GMLP_HEADS):
            c0 = h * GMLP_HEAD_DIM
            cs = slice(c0, c0 + GMLP_HEAD_DIM)
            w_bf = jnp.where(mask, ws_ref[h], 0.0).astype(BF16)
            _, _, _, sv = _gmlp_head(v_ref[:, cs].astype(F32),ng_ref[...], nb_ref[...], w_bf)
            sv = sv + bs_ref[:, h:h + 1]
            za = za_ref[:, cs].astype(F32)
            o_ref[:, cs] = (u_ref[:, cs].astype(F32) * sv * (za * _sigmoid(za))).astype(BF16)
        live = (n > 0).astype(F32)
        for grp in range(POOL_GROUPS):
            c0 = grp * POOL_GROUP_DIM
            cs = slice(c0, c0 + POOL_GROUP_DIM)
            pooled, _ = _pool_group(xb_ref[:, cs].astype(F32), xp_ref[:, cs].astype(F32) * live, n * t, grp)
            yb = _dot(pooled.astype(BF16), pw_ref[grp], NN) + pb_ref[:, cs]
            zb = zb_ref[:, cs].astype(F32)
            o_ref[:, D_MODEL + c0:D_MODEL + c0 + POOL_GROUP_DIM] = (yb * ps_ref[:, cs] * (zb * _sigmoid(zb))).astype(BF16)

    return pl.pallas_call(
        body, name=name, grid=(s // t,),
        in_specs=[col(0), col(1), col(2), col(3), col(4), prev,
                  _full((GMLP_HEADS, t, t)), _full((t, LANES)), _full((1, GMLP_HEAD_DIM)), _full((1, GMLP_HEAD_DIM)),
                  _full((POOL_GROUPS, POOL_GROUP_DIM, POOL_GROUP_DIM)), _full((1, D_MODEL)), _full((1, D_MODEL))],
        out_specs=pl.BlockSpec((t, 2 * D_MODEL), lambda n: (n, 0)),
        out_shape=jax.ShapeDtypeStruct((s, 2 * D_MODEL), BF16),
        compiler_params=_params("parallel"),
    )(proj, proj, proj, proj, proj, proj, ws, bs_t, ng, nb, pool_w, pool_b, pool_scale)


def _even_bwd(proj, dmix, ws, ws_t, bs_t, ng, nb, pool_w, pool_b, pool_scale, *, name):
    s = proj.shape[0]
    t = GMLP_BLOCK
    nblk = s // t
    col, per, prev = _even_specs(t)
    nxt = lambda j: pl.BlockSpec((POOL_HALO, D_MODEL), lambda n: (jnp.minimum((n + 1) * per, nblk * per - 1), j))

    def body(u_ref, v_ref, za_ref, xb_ref, zb_ref, xp_ref, zn_ref, da_ref, db_ref, dbn_ref,
             ws_ref, wst_ref, bs_ref, ng_ref, nb_ref, pw_ref, pb_ref, ps_ref,
             dp_ref, gws_ref, gbs_ref, gng_ref, gnb_ref, gpw_ref, gpb_ref, gps_ref):
        n = pl.program_id(0)

        @pl.when(n == 0)
        def _():
            for r in (gws_ref, gbs_ref, gng_ref, gnb_ref, gpw_ref, gpb_ref, gps_ref):
                r[...] = jnp.zeros_like(r)

        mask, mask_t = _chunk_mask(), _chunk_mask(True)
        lane = lax.broadcasted_iota(jnp.int32, (t, LANES), 1)
        ngv, nbv = ng_ref[...], nb_ref[...]
        for h in range(GMLP_HEADS):
            c0 = h * GMLP_HEAD_DIM
            cs = slice(c0, c0 + GMLP_HEAD_DIM)
            w_bf = jnp.where(mask, ws_ref[h], 0.0).astype(BF16)
            wt_bf = jnp.where(mask_t, wst_ref[h], 0.0).astype(BF16)
            xhat, rstd, vn, sv = _gmlp_head(v_ref[:, cs].astype(F32),ngv, nbv, w_bf)
            sv = sv + bs_ref[:, h:h + 1]
            za, u, da = za_ref[:, cs].astype(F32), u_ref[:, cs].astype(F32), da_ref[:, cs].astype(F32)
            sg = _sigmoid(za)
            sl = za * sg
            dp_ref[:, cs] = (da * sv * sl).astype(BF16)
            dp_ref[:, 2 * D_MODEL + c0:2 * D_MODEL + c0 + GMLP_HEAD_DIM] = (
                da * u * sv * (sg * (1.0 + za * (1.0 - sg)))).astype(BF16)
            dsv = da * u * sl
            gbs_ref[...] += jnp.where(lane == h, jnp.sum(dsv, axis=1, keepdims=True), 0.0)
            dsv_bf = dsv.astype(BF16)
            gws_ref[h] += jnp.where(mask, _dot(dsv_bf, vn, NT), 0.0)
            dvn = _dot(wt_bf, dsv_bf, NN)
            dp_ref[:, D_MODEL + c0:D_MODEL + c0 + GMLP_HEAD_DIM] = _ln_bwd_rows(dvn, xhat, rstd, ngv).astype(BF16)
            gng_ref[...] += _colsum(dvn * xhat)
            gnb_ref[...] += _colsum(dvn)
        live_prev = (n > 0).astype(F32)
        live_next = (n < nblk - 1).astype(F32)
        for grp in range(POOL_GROUPS):
            c0 = grp * POOL_GROUP_DIM
            cs = slice(c0, c0 + POOL_GROUP_DIM)
            xb = xb_ref[:, cs].astype(F32)
            pooled, cnt = _pool_group(xb, xp_ref[:, cs].astype(F32) * live_prev, n * t, grp)
            pooled_bf = pooled.astype(BF16)
            pw = pw_ref[grp]
            yb = _dot(pooled_bf, pw, NN) + pb_ref[:, cs]
            ps = ps_ref[:, cs]
            zb, db = zb_ref[:, cs].astype(F32), db_ref[:, cs].astype(F32)
            sg = _sigmoid(zb)
            sl = zb * sg
            dp_ref[:, 4 * D_MODEL + c0:4 * D_MODEL + c0 + POOL_GROUP_DIM] = (
                db * yb * ps * (sg * (1.0 + zb * (1.0 - sg)))).astype(BF16)
            dsl = db * sl
            dy = dsl * ps
            gps_ref[:, cs] += _colsum(dsl * yb)
            gpb_ref[:, cs] += _colsum(dy)
            dy_bf = dy.astype(BF16)
            gpw_ref[grp] += _dot(pooled_bf, dy_bf, TN)
            r = _dot(dy_bf, pw, NT)
            zn = zn_ref[:, cs].astype(F32)
            dyn = (dbn_ref[:, cs].astype(F32) * (zn * _sigmoid(zn)) * ps * live_next).astype(BF16)
            rn = _dot(dyn, pw, NT) / _pool_counts((n + 1) * t, POOL_HALO, POOL_WINDOWS[grp])
            ext = jnp.concatenate([r / cnt, rn], axis=0)
            dxb = _window_sum(ext, grp + 1, True)[:t, :] - r
            dp_ref[:, 3 * D_MODEL + c0:3 * D_MODEL + c0 + POOL_GROUP_DIM] = dxb.astype(BF16)

    out_shape = [
        jax.ShapeDtypeStruct((s, EVEN_IN), BF16),
        jax.ShapeDtypeStruct((GMLP_HEADS, t, t), F32), jax.ShapeDtypeStruct((t, LANES), F32),
        jax.ShapeDtypeStruct((1, GMLP_HEAD_DIM), F32), jax.ShapeDtypeStruct((1, GMLP_HEAD_DIM), F32),
        jax.ShapeDtypeStruct((POOL_GROUPS, POOL_GROUP_DIM, POOL_GROUP_DIM), F32),
        jax.ShapeDtypeStruct((1, D_MODEL), F32), jax.ShapeDtypeStruct((1, D_MODEL), F32),
    ]
    return pl.pallas_call(
        body, name=name, grid=(nblk,),
        in_specs=[col(0), col(1), col(2), col(3), col(4), prev, nxt(4),
                  pl.BlockSpec((t, D_MODEL), lambda n: (n, 0)), pl.BlockSpec((t, D_MODEL), lambda n: (n, 1)), nxt(1),
                  _full((GMLP_HEADS, t, t)), _full((GMLP_HEADS, t, t)), _full((t, LANES)),
                  _full((1, GMLP_HEAD_DIM)), _full((1, GMLP_HEAD_DIM)),
                  _full((POOL_GROUPS, POOL_GROUP_DIM, POOL_GROUP_DIM)), _full((1, D_MODEL)), _full((1, D_MODEL))],
        out_specs=[pl.BlockSpec((t, EVEN_IN), lambda n: (n, 0))] + [_full(o.shape) for o in out_shape[1:]],
        out_shape=out_shape,
        compiler_params=_params("arbitrary"),
    )(proj, proj, proj, proj, proj, proj, proj, dmix, dmix, dmix, ws, ws_t, bs_t, ng, nb, pool_w, pool_b, pool_scale)


ROPE_HALF = MLA_ROPE // 2


def _rope(v, cos, sin_signed):
    return v * cos + pltpu.roll(v, 2 * ROPE_HALF, 1) * sin_signed


def _rope_bwd(d, cos, sin_signed):
    return d * cos + pltpu.roll(d * sin_signed, 2 * ROPE_HALF, 1)


def _slab_lanes(shape, which):
    lane = lax.broadcasted_iota(jnp.int32, shape, 1)
    return (lane // ROPE_HALF) % 2 == which


def _rms(v, g):
    r = lax.rsqrt(jnp.mean(v * v, axis=-1, keepdims=True) + LN_EPS)
    return v * r * g, r


def _rms_bwd(dy, v, r, g):
    u = dy * g
    return r * u - v * (r * r * r) * jnp.mean(u * v, axis=-1, keepdims=True)


def _mla_prep(proj, gq, gkv, cos, sin_signed, *, name):
    s = proj.shape[0]
    ts = _tile(s, 512)

    def body(p_ref, gq_ref, gkv_ref, c_ref, s_ref, q_ref, k_ref):
        qcn, _ = _rms(p_ref[:, :MLA_Q_RANK].astype(F32), gq_ref[...])
        kvn, _ = _rms(p_ref[:, MLA_Q_RANK:MLA_Q_RANK + MLA_KV_RANK].astype(F32), gkv_ref[...])
        kr = p_ref[:, MLA_Q_RANK + MLA_KV_RANK:].astype(F32)
        lane = lax.broadcasted_iota(jnp.int32, kr.shape, 1)
        by1, by2 = pltpu.roll(kr, ROPE_HALF, 1), pltpu.roll(kr, 2 * ROPE_HALF, 1)
        both = jnp.where(lane < ROPE_HALF, kr, jnp.where(lane < 3 * ROPE_HALF, by1, by2))
        kr = _rope(both, c_ref[...], s_ref[...])
        q_ref[...] = qcn.astype(BF16)
        k_ref[...] = jnp.concatenate([kvn, kr], axis=1).astype(BF16)

    return pl.pallas_call(
        body, name=name, grid=(s // ts,),
        in_specs=[_small_spec(ts), _vec_spec(MLA_Q_RANK), _vec_spec(MLA_KV_RANK), _row_spec(ts, LANES), _row_spec(ts, LANES)],
        out_specs=[_row_spec(ts, MLA_Q_RANK), _row_spec(ts, QK_PAD)],
        out_shape=[jax.ShapeDtypeStruct((s, MLA_Q_RANK), BF16), jax.ShapeDtypeStruct((s, QK_PAD), BF16)],
        compiler_params=_params("parallel"),
    )(proj, gq, gkv, cos, sin_signed)


def _mla_prep_bwd(proj, dqcn, dkv, gq, gkv, cos, sin_signed, *, name):
    s = proj.shape[0]
    ts = _tile(s, 512)

    def body(p_ref, dq_ref, dkv_ref, gq_ref, gkv_ref, c_ref, s_ref, ds_ref, ggq_ref, ggkv_ref):
        @pl.when(pl.program_id(0) == 0)
        def _():
            ggq_ref[...] = jnp.zeros_like(ggq_ref)
            ggkv_ref[...] = jnp.zeros_like(ggkv_ref)

        qc = p_ref[:, :MLA_Q_RANK].astype(F32)
        kvc = p_ref[:, MLA_Q_RANK:MLA_Q_RANK + MLA_KV_RANK].astype(F32)
        _, rq = _rms(qc, gq_ref[...])
        _, rkv = _rms(kvc, gkv_ref[...])
        dq = dq_ref[...]
        dkvn = dkv_ref[:, :MLA_KV_RANK]
        ggq_ref[...] += _colsum(dq * qc * rq)
        ggkv_ref[...] += _colsum(dkvn * kvc * rkv)
        dboth = _rope_bwd(dkv_ref[:, MLA_KV_RANK:], c_ref[...], s_ref[...])
        lane = lax.broadcasted_iota(jnp.int32, dboth.shape, 1)
        pair = dboth + pltpu.roll(dboth, 3 * ROPE_HALF, 1)
        dkr = jnp.where(lane < ROPE_HALF, pair, jnp.where(lane < 2 * ROPE_HALF, pltpu.roll(pair, 3 * ROPE_HALF, 1), 0.0))
        ds_ref[...] = jnp.concatenate(
            [_rms_bwd(dq, qc, rq, gq_ref[...]), _rms_bwd(dkvn, kvc, rkv, gkv_ref[...]), dkr], axis=1).astype(BF16)

    return pl.pallas_call(
        body, name=name, grid=(s // ts,),
        in_specs=[_small_spec(ts), _row_spec(ts, MLA_Q_RANK), _row_spec(ts, QK_PAD),
                  _vec_spec(MLA_Q_RANK), _vec_spec(MLA_KV_RANK), _row_spec(ts, LANES), _row_spec(ts, LANES)],
        out_specs=[_row_spec(ts, ODD_SMALL_PAD), _vec_spec(MLA_Q_RANK), _vec_spec(MLA_KV_RANK)],
        out_shape=[jax.ShapeDtypeStruct((s, ODD_SMALL_PAD), BF16), jax.ShapeDtypeStruct((1, MLA_Q_RANK), F32),
                   jax.ShapeDtypeStruct((1, MLA_KV_RANK), F32)],
        compiler_params=_params("arbitrary"),
    )(proj, dqcn, dkv, gq, gkv, cos, sin_signed)


Q_HEAD_GROUP = 8
LOG2_E = 1.4426950408889634
Q_PRESCALE = ATTN_SCALE * LOG2_E


def _q_build(q_nope, q_rope_pre, wuk_hdr, cos, sin_signed, *, name):
    s = q_nope.shape[0]
    ts = _tile(s, 512)
    hg = Q_HEAD_GROUP

    def body(qn_ref, qr_ref, w_ref, c_ref, s_ref, o_ref):
        for pair in range(hg // 2):
            r = _rope(qr_ref[:, pair * LANES:(pair + 1) * LANES], c_ref[...], s_ref[...])
            for j in range(2):
                h = 2 * pair + j
                ql = _dot(qn_ref[:, h * MLA_NOPE:(h + 1) * MLA_NOPE], w_ref[h], NN)
                mine = jnp.where(_slab_lanes(r.shape, j), r, 0.0)
                o_ref[h] = (jnp.concatenate([ql, mine], axis=1) * Q_PRESCALE).astype(BF16)

    return pl.pallas_call(
        body, name=name, grid=(s // ts, MLA_HEADS // hg),
        in_specs=[pl.BlockSpec((ts, hg * MLA_NOPE), lambda i, p: (i, p)), pl.BlockSpec((ts, hg * MLA_ROPE), lambda i, p: (i, p)),
                  pl.BlockSpec((hg, MLA_NOPE, MLA_KV_RANK), lambda i, p: (p, 0, 0)),
                  pl.BlockSpec((ts, LANES), lambda i, p: (i, 0)), pl.BlockSpec((ts, LANES), lambda i, p: (i, 0))],
        out_specs=pl.BlockSpec((hg, ts, QK_PAD), lambda i, p: (p, i, 0)),
        out_shape=jax.ShapeDtypeStruct((MLA_HEADS, s, QK_PAD), BF16),
        compiler_params=_params("parallel", "parallel"),
    )(q_nope, q_rope_pre, wuk_hdr, cos, sin_signed)


def _q_bwd(dq, q_nope, wuk_hrd, cos, sin_signed, *, name):
    s = q_nope.shape[0]
    ts = _tile(s, 512)
    hg = Q_HEAD_GROUP

    nope_w, all_w = hg * MLA_NOPE, hg * (MLA_NOPE + MLA_ROPE)

    def body(dq_ref, qn_ref, w_ref, c_ref, s_ref, dall_ref, gw_ref):
        @pl.when(pl.program_id(1) == 0)
        def _():
            gw_ref[...] = jnp.zeros_like(gw_ref)

        for h in range(hg):
            dql = dq_ref[h, :, :MLA_KV_RANK]
            dall_ref[:, h * MLA_NOPE:(h + 1) * MLA_NOPE] = _dot(dql, w_ref[h], NN).astype(BF16)
            gw_ref[h] += _dot(dql, qn_ref[:, h * MLA_NOPE:(h + 1) * MLA_NOPE], TN)
        for pair in range(hg // 2):
            hi0 = dq_ref[2 * pair, :, MLA_KV_RANK:].astype(F32)
            hi1 = dq_ref[2 * pair + 1, :, MLA_KV_RANK:].astype(F32)
            d = jnp.where(_slab_lanes(hi0.shape, 0), hi0, hi1)
            dall_ref[:, nope_w + pair * LANES:nope_w + (pair + 1) * LANES] = _rope_bwd(d, c_ref[...], s_ref[...]).astype(BF16)

    return pl.pallas_call(
        body, name=name, grid=(MLA_HEADS // hg, s // ts),
        in_specs=[pl.BlockSpec((hg, ts, QK_PAD), lambda p, i: (p, i, 0)), pl.BlockSpec((ts, nope_w), lambda p, i: (i, p)),
                  pl.BlockSpec((hg, MLA_KV_RANK, MLA_NOPE), lambda p, i: (p, 0, 0)),
                  pl.BlockSpec((ts, LANES), lambda p, i: (i, 0)), pl.BlockSpec((ts, LANES), lambda p, i: (i, 0))],
        out_specs=[pl.BlockSpec((ts, all_w), lambda p, i: (i, p)),
                   pl.BlockSpec((hg, MLA_KV_RANK, MLA_NOPE), lambda p, i: (p, 0, 0))],
        out_shape=[jax.ShapeDtypeStruct((s, MLA_HEADS * (MLA_NOPE + MLA_ROPE)), BF16),
                   jax.ShapeDtypeStruct((MLA_HEADS, MLA_KV_RANK, MLA_NOPE), F32)],
        compiler_params=_params("parallel", "arbitrary"),
    )(dq, q_nope, wuk_hrd, cos, sin_signed)


ATTN_BQ = 128
ATTN_BK = 512
ATTN_BK_FWD = 1024


def _diag_mask(rows, bq, bk, q0, k0):
    qc = (q0 + lax.broadcasted_iota(jnp.int32, (rows, bk), 0) % bq) // CHUNK
    kc = (k0 + lax.broadcasted_iota(jnp.int32, (rows, bk), 1)) // CHUNK
    return kc <= qc


def _attn_fwd(q, k, *, name):
    nh, s, dk = q.shape
    bq, bk = _tile(s, ATTN_BQ), _tile(s, ATTN_BK_FWD)
    rows = nh * bq

    def body(q_ref, k_ref, o_ref, lse_ref):
        i = pl.program_id(0)
        qb = q_ref[...].reshape(rows, dk)
        n_before = (i * bq) // bk

        def step(j, width, carry, masked):
            m, l, acc = carry
            k0 = pl.multiple_of(j * bk, bk)
            kb = k_ref[pl.ds(k0, width), :]
            sc = _dot(qb, kb, NT)
            if masked:
                sc = jnp.where(_diag_mask(rows, bq, width, i * bq, k0), sc, NEG)
            m_new = jnp.maximum(m, jnp.max(sc, axis=1, keepdims=True))
            p = jnp.exp2(sc - m_new)
            a = jnp.exp2(m - m_new)
            l = a * l + jnp.sum(p, axis=1, keepdims=True)
            acc = a * acc + _dot(p.astype(BF16), kb[:, :MLA_KV_RANK], NN)
            return m_new, l, acc

        init = (jnp.full((rows, 1), NEG, F32), jnp.zeros((rows, 1), F32), jnp.zeros((rows, MLA_KV_RANK), F32))
        carry = lax.fori_loop(0, n_before, lambda j, c: step(j, bk, c, False), init)
        for part in range(bk // bq):
            @pl.when(i % (bk // bq) == part)
            def _(part=part):
                m, l, acc = step(n_before, (part + 1) * bq, carry, True)
                o_ref[...] = (acc / l).astype(BF16).reshape(nh, bq, MLA_KV_RANK)
                lse_ref[...] = jnp.broadcast_to(m + jnp.log2(l), (rows, LANES)).reshape(nh, bq, LANES)

    return pl.pallas_call(
        body, name=name, grid=(s // bq,),
        in_specs=[pl.BlockSpec((nh, bq, dk), lambda i: (0, i, 0)), pl.BlockSpec((s, dk), lambda i: (0, 0))],
        out_specs=[pl.BlockSpec((nh, bq, MLA_KV_RANK), lambda i: (0, i, 0)), pl.BlockSpec((nh, bq, LANES), lambda i: (0, i, 0))],
        out_shape=[jax.ShapeDtypeStruct((nh, s, MLA_KV_RANK), BF16), jax.ShapeDtypeStruct((nh, s, LANES), F32)],
        compiler_params=_params("parallel"),
    )(q, k)


def _attn_bwd(q, k, do, o, lse, *, name):
    nh, s, dk = q.shape
    bq, bk = _tile(s, ATTN_BQ), _tile(s, ATTN_BK)
    rows = nh * bq

    def body(q_ref, k_ref, do_ref, o_ref, lse_ref, dq_ref, dkv_ref):
        i = pl.program_id(0)
        n_before = (i * bq) // bk

        @pl.when(i == 0)
        def _():
            dkv_ref[...] = jnp.zeros_like(dkv_ref)

        qb = q_ref[...].reshape(rows, dk)
        dob = do_ref[...].reshape(rows, MLA_KV_RANK)
        lse_b = lse_ref[...].reshape(rows, LANES)[:, :1]
        delta = jnp.sum(dob.astype(F32) * o_ref[...].reshape(rows, MLA_KV_RANK).astype(F32), axis=1, keepdims=True)

        def step(j, width, dq, masked):
            j0 = pl.multiple_of(j * bk, bk)
            kb = k_ref[pl.ds(j0, width), :]
            sc = _dot(qb, kb, NT)
            if masked:
                sc = jnp.where(_diag_mask(rows, bq, width, i * bq, j0), sc, NEG)
            p = jnp.exp2(sc - lse_b)
            dp = _dot(dob, kb[:, :MLA_KV_RANK], NT)
            ds_bf = (p * (dp - delta)).astype(BF16)
            dkv_ref[pl.ds(j0, width), :] += _dot(ds_bf, qb, TN) * (1.0 / LOG2_E)
            dkv_ref[pl.ds(j0, width), :MLA_KV_RANK] += _dot(p.astype(BF16), dob, TN)
            return dq + _dot(ds_bf, kb, NN)

        dq_before = lax.fori_loop(0, n_before, lambda j, c: step(j, bk, c, False), jnp.zeros((rows, dk), F32))
        for part in range(bk // bq):
            @pl.when(i % (bk // bq) == part)
            def _(part=part):
                dq = step(n_before, (part + 1) * bq, dq_before, True) * ATTN_SCALE
                dq_ref[...] = dq.astype(BF16).reshape(nh, bq, dk)

    blk = lambda w: pl.BlockSpec((nh, bq, w), lambda i: (0, i, 0))
    return pl.pallas_call(
        body, name=name, grid=(s // bq,),
        in_specs=[blk(dk), pl.BlockSpec((s, dk), lambda i: (0, 0)), blk(MLA_KV_RANK), blk(MLA_KV_RANK), blk(LANES)],
        out_specs=[blk(dk), pl.BlockSpec((s, dk), lambda i: (0, 0))],
        out_shape=[jax.ShapeDtypeStruct((nh, s, dk), BF16), jax.ShapeDtypeStruct((s, dk), F32)],
        compiler_params=_params("arbitrary"),
    )(q, k, do, o, lse)


HEAD_GROUP = 4
SMALL_BLOCK = MLA_WIDTH // ODD_SMALL_PAD


def _small_spec(ts):
    return pl.BlockSpec((ts, ODD_SMALL_PAD), lambda i: (i, SMALL_BLOCK))


def _o_build(o_lat, wuv_hrv, proj, *, name):
    s = proj.shape[0]
    ts = _tile(s, 1024)
    w = HEAD_GROUP * MLA_V

    def body(ol_ref, w_ref, z_ref, og_ref):
        for j in range(HEAD_GROUP):
            cs = slice(j * MLA_V, (j + 1) * MLA_V)
            z = z_ref[:, cs].astype(F32)
            og_ref[:, cs] = (_dot(ol_ref[j], w_ref[j], NN) * (z * _sigmoid(z))).astype(BF16)

    return pl.pallas_call(
        body, name=name, grid=(s // ts, MLA_HEADS // HEAD_GROUP),
        in_specs=[pl.BlockSpec((HEAD_GROUP, ts, MLA_KV_RANK), lambda i, g: (g, i, 0)),
                  pl.BlockSpec((HEAD_GROUP, MLA_KV_RANK, MLA_V), lambda i, g: (g, 0, 0)),
                  pl.BlockSpec((ts, w), lambda i, g: (i, g))],
        out_specs=pl.BlockSpec((ts, w), lambda i, g: (i, g)),
        out_shape=jax.ShapeDtypeStruct((s, MLA_WIDTH), BF16),
        compiler_params=_params("parallel", "parallel"),
    )(o_lat, wuv_hrv, proj)


def _o_bwd(dg, proj, o_lat, wuv_hrv, wuv_hvr, *, name):
    s = proj.shape[0]
    ts = _tile(s, 1024)
    w = HEAD_GROUP * MLA_V

    def body(dg_ref, z_ref, ol_ref, w_ref, wt_ref, dol_ref, dz_ref, gw_ref):
        @pl.when(pl.program_id(1) == 0)
        def _():
            gw_ref[...] = jnp.zeros_like(gw_ref)

        for j in range(HEAD_GROUP):
            cs = slice(j * MLA_V, (j + 1) * MLA_V)
            z, dgj, ol = z_ref[:, cs].astype(F32), dg_ref[:, cs].astype(F32), ol_ref[j]
            sg = _sigmoid(z)
            o = _dot(ol, w_ref[j], NN)
            dz_ref[:, cs] = (dgj * o * (sg * (1.0 + z * (1.0 - sg)))).astype(BF16)
            do_bf = (dgj * (z * sg)).astype(BF16)
            dol_ref[j] = _dot(do_bf, wt_ref[j], NN).astype(BF16)
            gw_ref[j] += _dot(ol, do_bf, TN)

    hs = lambda a, b: pl.BlockSpec((HEAD_GROUP, a, b), lambda g, i: (g, 0, 0))
    return pl.pallas_call(
        body, name=name, grid=(MLA_HEADS // HEAD_GROUP, s // ts),
        in_specs=[pl.BlockSpec((ts, w), lambda g, i: (i, g)), pl.BlockSpec((ts, w), lambda g, i: (i, g)),
                  pl.BlockSpec((HEAD_GROUP, ts, MLA_KV_RANK), lambda g, i: (g, i, 0)),
                  hs(MLA_KV_RANK, MLA_V), hs(MLA_V, MLA_KV_RANK)],
        out_specs=[pl.BlockSpec((HEAD_GROUP, ts, MLA_KV_RANK), lambda g, i: (g, i, 0)),
                   pl.BlockSpec((ts, w), lambda g, i: (i, g)), hs(MLA_KV_RANK, MLA_V)],
        out_shape=[jax.ShapeDtypeStruct((MLA_HEADS, s, MLA_KV_RANK), BF16), jax.ShapeDtypeStruct((s, MLA_WIDTH), BF16),
                   jax.ShapeDtypeStruct((MLA_HEADS, MLA_KV_RANK, MLA_V), F32)],
        compiler_params=_params("parallel", "arbitrary"),
    )(dg, proj, o_lat, wuv_hrv, wuv_hvr)


def _ada_mod(c_all, ada_w, ada_b_sh, *, name):
    nl, _, cols = ada_w.shape

    def body(c_ref, w_ref, b_ref, o_ref):
        c = c_ref[...]
        cond = (c * _sigmoid(c)).astype(BF16)
        for l in range(nl):
            o_ref[l] = _dot(cond, w_ref[l].astype(BF16), NN) + b_ref[l]

    return pl.pallas_call(
        body, name=name, out_shape=jax.ShapeDtypeStruct((nl, c_all.shape[0], cols), F32),
        compiler_params=_params(),
    )(c_all, ada_w, ada_b_sh)


def _ada_grad(c_all_t, dmod_sh, *, name):
    nl, _, cols = dmod_sh.shape
    d = c_all_t.shape[0]

    def body(c_ref, dm_ref, gw_ref):
        c = c_ref[...]
        cond_t = c * _sigmoid(c)
        for l in range(nl):
            gw_ref[l] = lax.dot_general(cond_t, dm_ref[l], (NN, ((), ())), precision=lax.Precision.HIGHEST,
                                        preferred_element_type=F32)

    return pl.pallas_call(
        body, name=name, out_shape=jax.ShapeDtypeStruct((nl, d, cols), F32), compiler_params=_params(),
    )(c_all_t, dmod_sh)


def _sum_devices(parts, *, name):
    def body(p_ref, o_ref):
        acc = p_ref[0]
        for k in range(1, parts.shape[0]):
            acc = acc + p_ref[k]
        o_ref[...] = acc

    return pl.pallas_call(body, name=name, out_shape=jax.ShapeDtypeStruct(parts.shape[1:], F32), compiler_params=_params())(parts)


def _adamw_math(w, g, m, v):
    c1 = 1.0 - ADAM_B1 ** ADAM_STEP
    c2 = 1.0 - ADAM_B2 ** ADAM_STEP
    nm = ADAM_B1 * m + (1.0 - ADAM_B1) * g
    nv = ADAM_B2 * v + (1.0 - ADAM_B2) * (g * g)
    return -ADAM_LR * ((nm / c1) / (jnp.sqrt(nv / c2) + ADAM_EPS) + ADAM_WD * w), nm, nv


ADAMW_BLOCK_BYTES = 1 << 20


def _adamw(w, g, m, v, *, name, after=None):
    shape = w.shape
    a, b = shape[-2], shape[-1]
    lead = 1
    for dim in shape[:-2]:
        lead *= dim
    row_bytes = 4 * b
    if a * row_bytes <= ADAMW_BLOCK_BYTES:
        ta = a
        tl = max(1, min(lead, ADAMW_BLOCK_BYTES // (a * row_bytes)))
        while lead % tl:
            tl -= 1
    else:
        tl = 1
        ta = _tile(a, 256)
    to3 = lambda t: t.reshape(lead, a, b)

    def body(w_ref, g_ref, m_ref, v_ref, *rest):
        d_ref, nm_ref, nv_ref = rest[-3:]
        d_ref[...], nm_ref[...], nv_ref[...] = _adamw_math(w_ref[...], g_ref[...], m_ref[...], v_ref[...])

    spec = pl.BlockSpec((tl, ta, b), lambda i, j: (i, j, 0))
    out = jax.ShapeDtypeStruct((lead, a, b), F32)
    order = [] if after is None else [after]
    res = pl.pallas_call(
        body, name=name, grid=(lead // tl, a // ta), in_specs=[spec] * 4 + [pl.BlockSpec(memory_space=pl.ANY)] * len(order),
        out_specs=[spec] * 3, out_shape=[out] * 3, compiler_params=_params("parallel", "parallel"),
    )(to3(w), to3(g), to3(m), to3(v), *order)
    return [r.reshape(shape) for r in res]


def _adamw_small(ws, gs, ms, vs, *, name):
    n = len(ws)

    def body(*refs):
        for k in range(n):
            w_ref, g_ref, m_ref, v_ref = (refs[j * n + k] for j in range(4))
            d_ref, nm_ref, nv_ref = (refs[(4 + j) * n + k] for j in range(3))
            d_ref[...], nm_ref[...], nv_ref[...] = _adamw_math(w_ref[...], g_ref[...], m_ref[...], v_ref[...])

    outs = [jax.ShapeDtypeStruct(w.shape, F32) for w in ws]
    res = pl.pallas_call(body, name=name, out_shape=outs * 3, compiler_params=_params())(*ws, *gs, *ms, *vs)
    return res[:n], res[n:2 * n], res[2 * n:]


def _flip(v, bit):
    return 1 - v if bit else v


CHIP_DELTAS = ((1, 0), (0, 1), (1, 1))
SUM_ROWS = 32


def _all_gather_chips(shard, *, name):
    def body(x_ref, o_ref, send_sems, recv_sems, local_sem):
        x, y, c = lax.axis_index("x"), lax.axis_index("y"), lax.axis_index("c")
        mine = pltpu.make_async_copy(x_ref, o_ref.at[2 * x + y], local_sem)
        mine.start()

        def copy(k):
            tx, ty = _flip(x, CHIP_DELTAS[k][0]), _flip(y, CHIP_DELTAS[k][1])
            send = pltpu.make_async_remote_copy(src_ref=x_ref, dst_ref=o_ref.at[2 * x + y], send_sem=send_sems.at[k],
                                                recv_sem=recv_sems.at[k], device_id=(tx, ty, c), device_id_type=MESH)
            recv = pltpu.make_async_remote_copy(src_ref=x_ref, dst_ref=o_ref.at[2 * tx + ty], send_sem=send_sems.at[k],
                                                recv_sem=recv_sems.at[k], device_id=(tx, ty, c), device_id_type=MESH)
            return send, recv

        pairs = [copy(k) for k in range(3)]
        for send, _ in pairs:
            send.start()
        for _, recv in pairs:
            recv.wait_recv()
        for send, _ in pairs:
            send.wait_send()
        mine.wait()

    return pl.pallas_call(
        body, name=name, out_shape=jax.ShapeDtypeStruct((N_CHIPS,) + shard.shape, shard.dtype),
        in_specs=[HBM], out_specs=HBM,
        scratch_shapes=[pltpu.SemaphoreType.DMA((3,)), pltpu.SemaphoreType.DMA((3,)), pltpu.SemaphoreType.DMA(())],
    )(shard)


def _gather_weights(shards, *, name):
    n = len(shards)

    def body(*refs):
        w_refs, o_refs = refs[:n], refs[n:2 * n]
        ici_send, ici_recv, d2d_send, d2d_recv, local_sems = refs[2 * n:]
        x, y, c = lax.axis_index("x"), lax.axis_index("y"), lax.axis_index("c")
        me = 2 * x + y
        peers = [(_flip(x, dx), _flip(y, dy)) for dx, dy in CHIP_DELTAS]
        locals_ = [pltpu.make_async_copy(w_refs[k], o_refs[k].at[me], local_sems.at[k]) for k in range(n)]
        for cp in locals_:
            cp.start()

        def rows(k, which):
            half = shards[k].shape[0] // 2
            return pl.ds(pl.multiple_of(which * half, half), half)

        def over_chips(k, d, slot):
            tx, ty = peers[d]
            return pltpu.make_async_remote_copy(
                src_ref=w_refs[k].at[rows(k, c)], dst_ref=o_refs[k].at[slot, rows(k, c)], send_sem=ici_send.at[k, d],
                recv_sem=ici_recv.at[k, d], device_id=(tx, ty, c), device_id_type=MESH)

        def to_sibling(k, d, which):
            tx, ty = peers[d]
            at = o_refs[k].at[2 * tx + ty, rows(k, which)]
            return pltpu.make_async_remote_copy(src_ref=at, dst_ref=at, send_sem=d2d_send.at[k, d], recv_sem=d2d_recv.at[k, d],
                                                device_id=(x, y, 1 - c), device_id_type=MESH)

        sends = [over_chips(k, d, me) for k in range(n) for d in range(3)]
        for cp in sends:
            cp.start()
        passed = []
        for k in range(n):
            for d in range(3):
                over_chips(k, d, 2 * peers[d][0] + peers[d][1]).wait_recv()
                passed.append(to_sibling(k, d, c))
                passed[-1].start()
        for k in range(n):
            for d in range(3):
                to_sibling(k, d, 1 - c).wait_recv()
        for cp in sends + passed:
            cp.wait_send()
        for cp in locals_:
            cp.wait()

    return pl.pallas_call(
        body, name=name, out_shape=[jax.ShapeDtypeStruct((N_CHIPS,) + w.shape, w.dtype) for w in shards],
        in_specs=[HBM] * n, out_specs=[HBM] * n,
        scratch_shapes=[pltpu.SemaphoreType.DMA((n, 3))] * 4 + [pltpu.SemaphoreType.DMA((n,))],
    )(*shards)


def _add_into(dst_ref, src_ref):
    ns, r, _ = dst_ref.shape
    step = SUM_ROWS if r % SUM_ROWS == 0 else r
    for s in range(ns):
        def tile(t, carry):
            at = pl.ds(pl.multiple_of(t * step, step), step)
            dst_ref[s, at, :] = (dst_ref[s, at, :].astype(F32) + src_ref[s, at, :].astype(F32)).astype(dst_ref.dtype)
            return carry
        lax.fori_loop(0, r // step, tile, 0)


def _reduce_sibling(grads, *, name):
    n = len(grads)

    def body(*refs):
        g_refs, o_refs = refs[:n], refs[n:2 * n]
        mine, got = refs[2 * n:3 * n], refs[3 * n:4 * n]
        send_sems, recv_sems, load_sems, store_sems = refs[4 * n:]
        x, y, c = lax.axis_index("x"), lax.axis_index("y"), lax.axis_index("c")
        loads = [pltpu.make_async_copy(g_refs[k].at[:, c], mine[k], load_sems.at[k]) for k in range(n)]
        swaps = [pltpu.make_async_remote_copy(src_ref=g_refs[k].at[:, 1 - c], dst_ref=got[k], send_sem=send_sems.at[k],
                                              recv_sem=recv_sems.at[k], device_id=(x, y, 1 - c), device_id_type=MESH)
                 for k in range(n)]
        for cp in loads + swaps:
            cp.start()
        stores = []
        for k in range(n):
            loads[k].wait()
            swaps[k].wait_recv()
            _add_into(mine[k], got[k])
            stores.append(pltpu.make_async_copy(mine[k], o_refs[k], store_sems.at[k]))
            stores[-1].start()
        for k in range(n):
            swaps[k].wait_send()
            stores[k].wait()

    half = [jax.ShapeDtypeStruct((g.shape[0],) + g.shape[2:], g.dtype) for g in grads]
    return pl.pallas_call(
        body, name=name, out_shape=half, in_specs=[HBM] * n, out_specs=[HBM] * n,
        scratch_shapes=[pltpu.VMEM(h.shape, h.dtype) for h in half] * 2 + [pltpu.SemaphoreType.DMA((n,))] * 4,
        compiler_params=_params(),
    )(*grads)


def _reduce_chips(parts, landed, *, name):
    n_send = len(parts)
    n = n_send + len(landed)

    def body(*refs):
        p_refs, o_refs = refs[:n], refs[n:2 * n]
        got, total = refs[2 * n:3 * n], refs[3 * n:4 * n]
        send_sems, recv_sems, load_sems, share_send, share_recv, store_sems = refs[4 * n:]
        x, y, c = lax.axis_index("x"), lax.axis_index("y"), lax.axis_index("c")
        me = 2 * x + y
        peers = [(_flip(x, dx), _flip(y, dy)) for dx, dy in CHIP_DELTAS]

        def over_chips(k, d, src_slot, dst_slot):
            tx, ty = peers[d]
            return pltpu.make_async_remote_copy(
                src_ref=p_refs[k].at[src_slot], dst_ref=got[k].at[dst_slot], send_sem=send_sems.at[k, d],
                recv_sem=recv_sems.at[k, d], device_id=(tx, ty, c), device_id_type=MESH)

        loads = [pltpu.make_async_copy(p_refs[k].at[me], got[k].at[me], load_sems.at[k]) for k in range(n_send)]
        loads += [pltpu.make_async_copy(p_refs[k], got[k], load_sems.at[k]) for k in range(n_send, n)]
        sends = [over_chips(k, d, 2 * peers[d][0] + peers[d][1], me) for k in range(n_send) for d in range(3)]
        for cp in loads + sends:
            cp.start()
        shares, stores = [], []
        for k in range(n):
            loads[k].wait()
            for d in range(3 if k < n_send else 0):
                slot = 2 * peers[d][0] + peers[d][1]
                over_chips(k, d, slot, slot).wait_recv()
            r = total[k].shape[0]
            step = SUM_ROWS if r % SUM_ROWS == 0 else r

            def tile(t, carry, k=k, step=step):
                at = pl.ds(pl.multiple_of(t * step, step), step)
                acc = got[k][0, at, :].astype(F32)
                for s in range(1, N_CHIPS):
                    acc = acc + got[k][s, at, :].astype(F32)
                total[k][at, :] = acc
                return carry

            lax.fori_loop(0, r // step, tile, 0)
            stores.append(pltpu.make_async_copy(total[k], o_refs[k].at[c], store_sems.at[k]))
            shares.append(pltpu.make_async_remote_copy(
                src_ref=total[k], dst_ref=o_refs[k].at[c], send_sem=share_send.at[k], recv_sem=share_recv.at[k],
                device_id=(x, y, 1 - c), device_id_type=MESH))
            stores[-1].start()
            shares[-1].start()
        for k in range(n):
            pltpu.make_async_remote_copy(
                src_ref=total[k], dst_ref=o_refs[k].at[1 - c], send_sem=share_send.at[k], recv_sem=share_recv.at[k],
                device_id=(x, y, 1 - c), device_id_type=MESH).wait_recv()
        for cp in sends + shares:
            cp.wait_send()
        for cp in stores:
            cp.wait()

    both = list(parts) + list(landed)
    return pl.pallas_call(
        body, name=name, out_shape=[jax.ShapeDtypeStruct((2,) + p.shape[1:], F32) for p in both],
        in_specs=[HBM] * n, out_specs=[HBM] * n,
        scratch_shapes=[pltpu.VMEM(p.shape, p.dtype) for p in both] + [pltpu.VMEM(p.shape[1:], F32) for p in both]
        + [pltpu.SemaphoreType.DMA((n, 3))] * 2 + [pltpu.SemaphoreType.DMA((n,))] * 4,
        compiler_params=_params(),
    )(*both)


SEM = pl.BlockSpec(memory_space=pltpu.SEMAPHORE)
IN_FLIGHT = pltpu.SideEffectType.DATAFLOW_SIDE_EFFECTING


def _chip_copies(s_refs, l_refs, sems, scatter, theirs):
    x, y, c = lax.axis_index("x"), lax.axis_index("y"), lax.axis_index("c")
    me = 2 * x + y
    copies = []
    for k in range(len(s_refs)):
        for d, (dx, dy) in enumerate(CHIP_DELTAS):
            tx, ty = _flip(x, dx), _flip(y, dy)
            peer = 2 * tx + ty
            send_sem, recv_sem = sems[2 * (3 * k + d)], sems[2 * (3 * k + d) + 1]
            copies.append(pltpu.make_async_remote_copy(
                src_ref=s_refs[k].at[peer] if scatter else s_refs[k], dst_ref=l_refs[k].at[peer if theirs else me],
                send_sem=send_sem, recv_sem=recv_sem, device_id=(tx, ty, c), device_id_type=MESH))
    return copies


def _chips_start(srcs, lands, after, *, scatter, name):
    n = len(srcs)
    n_sem = 2 * 3 * n

    def body(*refs):
        s_refs, l_refs = refs[:n], refs[n:2 * n]
        sems = refs[2 * n + 1:2 * n + 1 + n_sem]
        token = refs[-1]
        for cp in _chip_copies(s_refs, l_refs, sems, scatter, False):
            cp.start()
        token[...] = jnp.zeros_like(token)

    hbm = lambda a: pltpu.HBM(a.shape, a.dtype)
    res = pl.pallas_call(
        body, name=name,
        out_shape=(*[pltpu.SemaphoreType.DMA(())] * n_sem, *[hbm(a) for a in srcs], *[hbm(a) for a in lands],
                   jax.ShapeDtypeStruct((8, LANES), F32)),
        in_specs=[HBM] * (2 * n) + [pl.BlockSpec(memory_space=pl.ANY)],
        out_specs=(*[SEM] * n_sem, *[HBM] * (2 * n), VMEM),
        input_output_aliases={k: n_sem + k for k in range(2 * n)},
        compiler_params=pltpu.CompilerParams(has_side_effects=IN_FLIGHT),
    )(*[pltpu.with_memory_space_constraint(a, pltpu.HBM) for a in list(srcs) + list(lands)], after)
    return res[:n_sem], res[n_sem:n_sem + n], res[n_sem + n:n_sem + 2 * n], res[-1]


def _chips_wait(sems, srcs, lands, after, *, scatter, name):
    n = len(srcs)
    n_sem = len(sems)

    def body(*refs):
        s_refs, l_refs = refs[:n], refs[n:2 * n]
        sem_refs = refs[2 * n:2 * n + n_sem]
        for cp in _chip_copies(s_refs, l_refs, sem_refs, scatter, False):
            cp.wait_send()
        for cp in _chip_copies(s_refs, l_refs, sem_refs, scatter, True):
            cp.wait_recv()

    hbm = lambda a: pltpu.HBM(a.shape, a.dtype)
    res = pl.pallas_call(
        body, name=name, out_shape=tuple(hbm(a) for a in list(srcs) + list(lands)),
        in_specs=[HBM] * (2 * n) + [SEM] * n_sem + [pl.BlockSpec(memory_space=pl.ANY)], out_specs=tuple([HBM] * (2 * n)),
        input_output_aliases={k: k for k in range(2 * n)},
        compiler_params=pltpu.CompilerParams(has_side_effects=IN_FLIGHT),
    )(*srcs, *lands, *sems, after)
    return res[n:]


def _all_gather_devices(rows, *, name, after=None):
    deltas = [(dx, dy, dc) for dx in (0, 1) for dy in (0, 1) for dc in (0, 1)][1:]
    order = [] if after is None else [after]

    def body(x_ref, *rest):
        o_ref, send_sems, recv_sems = rest[-3:]
        x, y, c = lax.axis_index("x"), lax.axis_index("y"), lax.axis_index("c")
        me = 4 * x + 2 * y + c
        o_ref[me] = x_ref[...]
        sends, recvs = [], []
        for k, (dx, dy, dc) in enumerate(deltas):
            tx, ty, tc = _flip(x, dx), _flip(y, dy), _flip(c, dc)
            sends.append(pltpu.make_async_remote_copy(src_ref=x_ref, dst_ref=o_ref.at[me], send_sem=send_sems.at[k],
                                                      recv_sem=recv_sems.at[k], device_id=(tx, ty, tc), device_id_type=MESH))
            recvs.append(pltpu.make_async_remote_copy(src_ref=x_ref, dst_ref=o_ref.at[4 * tx + 2 * ty + tc],
                                                      send_sem=send_sems.at[k], recv_sem=recv_sems.at[k],
                                                      device_id=(tx, ty, tc), device_id_type=MESH))
        for cp in sends:
            cp.start()
        for cp in recvs:
            cp.wait_recv()
        for cp in sends:
            cp.wait_send()

    return pl.pallas_call(
        body, name=name, out_shape=jax.ShapeDtypeStruct((N_DEV,) + rows.shape, rows.dtype),
        in_specs=[VMEM] + [pl.BlockSpec(memory_space=pl.ANY)] * len(order), out_specs=VMEM,
        scratch_shapes=[pltpu.SemaphoreType.DMA((N_DEV - 1,)), pltpu.SemaphoreType.DMA((N_DEV - 1,))],
    )(rows, *order)


WEIGHTS = ("ada_w", "ada_b", "ln_g", "ln_b", "e_w_in", "gmlp_norm_g", "gmlp_norm_b", "gmlp_ws", "gmlp_bs", "pool_w",
           "pool_b", "pool_scale", "e_w_out", "o_w_in", "mla_q_norm_g", "mla_kv_norm_g", "mla_w_uq", "mla_w_uk",
           "mla_w_uv", "o_w_out")
SMALL = ("ln_g", "ln_b", "gmlp_norm_g", "gmlp_norm_b", "gmlp_bs", "pool_b", "pool_scale", "mla_kv_norm_g", "mla_q_norm_g")


def _pad_cols(v, n):
    return jnp.concatenate([v, jnp.zeros((v.shape[0], n - v.shape[1]), v.dtype)], axis=1) if n > v.shape[1] else v


def _halves(g):
    return g.reshape(g.shape[0], 2, g.shape[1] // 2, g.shape[2])


def kernel(x, c, positions, ada_w, ada_b, ln_g, ln_b, e_w_in, gmlp_norm_g, gmlp_norm_b, gmlp_ws, gmlp_bs, pool_w, pool_b, pool_scale, e_w_out, o_w_in, mla_q_norm_g, mla_kv_norm_g, mla_w_uq, mla_w_uk, mla_w_uv, o_w_out, loss_target, m_ada_w, m_ada_b, m_ln_g, m_ln_b, m_e_w_in, m_gmlp_norm_g, m_gmlp_norm_b, m_gmlp_ws, m_gmlp_bs, m_pool_w, m_pool_b, m_pool_scale, m_e_w_out, m_o_w_in, m_mla_q_norm_g, m_mla_kv_norm_g, m_mla_w_uq, m_mla_w_uk, m_mla_w_uv, m_o_w_out, v_ada_w, v_ada_b, v_ln_g, v_ln_b, v_e_w_in, v_gmlp_norm_g, v_gmlp_norm_b, v_gmlp_ws, v_gmlp_bs, v_pool_w, v_pool_b, v_pool_scale, v_e_w_out, v_o_w_in, v_mla_q_norm_g, v_mla_kv_norm_g, v_mla_w_uq, v_mla_w_uk, v_mla_w_uv, v_o_w_out):
    args = dict(locals())
    weights = {n: args[n] for n in WEIGHTS}
    mom = {n: args["m_" + n] for n in WEIGHTS}
    var = {n: args["v_" + n] for n in WEIGHTS}
    ax, ay, ac = lax.axis_index("x"), lax.axis_index("y"), lax.axis_index("c")
    chip = 2 * ax + ay
    dev = 2 * chip + ac
    d = D_MODEL
    x2 = x[0]
    target = loss_target[0]
    q_rank_sh = mla_q_norm_g.shape[1]

    empty_zone = lambda w: lax.dynamic_update_slice(lax.empty((N_CHIPS,) + w.shape, w.dtype), w[None], (chip, 0, 0))
    shards0 = [w.astype(BF16) for w in (pool_w[0].reshape(-1, POOL_GROUP_DIM), e_w_out[0])]
    shards1 = [w.astype(BF16) for w in (o_w_in[0], mla_w_uq[0].reshape(q_rank_sh, -1), o_w_out[0])]
    w_in0, = _gather_weights([e_w_in[0].astype(BF16)], name="gather_weights")
    wuk_hrd = jnp.transpose(mla_w_uk[0], (1, 0, 2)).astype(BF16)
    wuk_hdr = jnp.transpose(mla_w_uk[0], (1, 2, 0)).astype(BF16)
    wuv_hrv = jnp.transpose(mla_w_uv[0], (1, 0, 2)).astype(BF16)
    wuv_hvr = jnp.transpose(mla_w_uv[0], (1, 2, 0)).astype(BF16)
    ws = gmlp_ws[0]
    ws_t = jnp.transpose(ws, (0, 2, 1))
    bs_t = _pad_cols(gmlp_bs[0].T, LANES)

    inv = 1.0 / (ROPE_THETA ** (jnp.arange(0, MLA_ROPE, 2, dtype=F32) / MLA_ROPE))
    ang = positions[0].astype(F32)[:, None] * inv
    cos_t = jnp.tile(jnp.cos(ang), (1, 4))
    sin_t = jnp.concatenate([-jnp.sin(ang), -jnp.sin(ang), jnp.sin(ang), jnp.sin(ang)], axis=1)

    c_all = _all_gather_devices(c.reshape(8, LANES), after=w_in0, name="gather_c").reshape(N_DEV, d)
    cols = ada_w.shape[2]
    ada_b_mine = lax.dynamic_slice_in_dim(ada_b, chip * cols, cols, axis=1)[:, None, :]
    mod_sh = _ada_mod(c_all, ada_w, ada_b_mine, name="ada_mod")
    q_norm_rows = jnp.zeros((8, cols), F32).at[0, :q_rank_sh].set(mla_q_norm_g[0])
    mod_all = _all_gather_chips(jnp.concatenate([mod_sh.reshape(2 * N_DEV, cols), q_norm_rows]), name="gather_mod")
    q_norm_g = mod_all[:, 2 * N_DEV, :q_rank_sh].reshape(1, -1)
    mod_all = jnp.transpose(mod_all[:, :2 * N_DEV].reshape(N_CHIPS, 2, N_DEV, cols), (1, 2, 0, 3)).reshape(2, N_DEV, 3 * d)
    mod = lax.dynamic_index_in_dim(mod_all, dev, axis=1, keepdims=False)
    shift = [mod[l:l + 1, :d] for l in range(2)]
    scale = [mod[l:l + 1, d:2 * d] for l in range(2)]
    gate = [mod[l:l + 1, 2 * d:] for l in range(2)]
    flight0 = _chips_start(shards0, [empty_zone(w) for w in shards0], mod, scatter=False, name="gather0_start")
    flight1 = _chips_start(shards1, [empty_zone(w) for w in shards1], flight0[3], scatter=False, name="gather1_start")

    scale[0] = scale[0] + flight1[3][:1, :1]
    h0 = _modulate(x2, scale[0], shift[0], name="modulate0")
    proj0 = _matmul(h0, w_in0, b_stacked=True, tm=1024, tn=1280, out_dtype=BF16, name="proj0")
    pool_w_g, w_out0 = _chips_wait(*flight0[:3], proj0, scatter=False, name="gather0_wait")
    pool_w_bf = jnp.transpose(pool_w_g.reshape(N_CHIPS, POOL_GROUPS, -1, POOL_GROUP_DIM), (1, 0, 2, 3)).reshape(
        POOL_GROUPS, POOL_GROUP_DIM, POOL_GROUP_DIM)
    w_out0 = w_out0.reshape(-1, d)
    mix0 = _even_fwd(proj0, ws, bs_t, gmlp_norm_g, gmlp_norm_b, pool_w_bf, pool_b, pool_scale, name="even_fwd")
    y0, x1, h1 = _out_resid_ln(mix0, w_out0, x2, gate[0], ln_g[0:1], ln_b[0:1], scale[1], shift[1], name="out0_ln")

    w_in1_g, w_uq_g, w_out1 = _chips_wait(*flight1[:3], h1, scatter=False, name="gather1_wait")
    w_out1 = w_out1.reshape(-1, d)
    w_in1 = jnp.transpose(w_in1_g, (1, 0, 2)).reshape(d, ODD_IN)
    w_in1 = jnp.concatenate([w_in1[:, ODD_SMALL:], _pad_cols(w_in1[:, :ODD_SMALL], ODD_SMALL_PAD)], axis=1)
    w_uq = w_uq_g.reshape(MLA_Q_RANK, MLA_HEADS, MLA_NOPE + MLA_ROPE)
    w_uq_nope = w_uq[:, :, :MLA_NOPE].reshape(MLA_Q_RANK, -1)
    w_uq_rope = jnp.transpose(w_uq[:, :, MLA_NOPE:].reshape(MLA_Q_RANK, MLA_HEADS // 2, 2, 2, ROPE_HALF),
                              (0, 1, 3, 2, 4)).reshape(MLA_Q_RANK, -1)
    proj1 = _matmul(h1, w_in1, tm=1024, tn=1280, out_dtype=BF16, name="proj1")
    q_cn, keys = _mla_prep(proj1, q_norm_g, mla_kv_norm_g, cos_t, sin_t, name="mla_prep")
    q_nope = _matmul(q_cn, w_uq_nope, tm=1024, tn=2048, name="q_nope", out_dtype=BF16)
    q_rope_pre = _matmul(q_cn, w_uq_rope, tm=1024, name="q_rope")
    q = _q_build(q_nope, q_rope_pre, wuk_hdr, cos_t, sin_t, name="q_build")
    o_lat, lse = _attn_fwd(q, keys, name="attn_fwd")
    og = _o_build(o_lat, wuv_hrv, proj1, name="o_build")

    dy1, dres1, g_ln_g1, g_ln_b1, dgate1, loss = _out_loss_ln_bwd(
        og, w_out1, x1, gate[1], ln_g[1:2], ln_b[1:2], target, name="out1_loss_ln")
    dg1 = _matmul(dy1, w_out1, trans_b=True, tn=2048, out_dtype=BF16, name="d_og")
    g_w_out1 = _matmul(og, dy1, trans_a=True, out_dtype=BF16, tm=1024, name="g_out1")
    do_lat, dz, g_uv = _o_bwd(dg1, proj1, o_lat, wuv_hrv, wuv_hvr, name="o_bwd")
    dq, dkeys = _attn_bwd(q, keys, do_lat, o_lat, lse, name="attn_bwd")
    dq_all, g_uk = _q_bwd(dq, q_nope, wuk_hrd, cos_t, sin_t, name="q_bwd")
    n_grp = MLA_HEADS // Q_HEAD_GROUP
    w_uq_all = jnp.concatenate([w_uq_nope.reshape(MLA_Q_RANK, n_grp, -1), w_uq_rope.reshape(MLA_Q_RANK, n_grp, -1)],
                               axis=2).reshape(MLA_Q_RANK, -1)
    dq_cn = _matmul(dq_all, w_uq_all, trans_b=True, tm=1024, name="d_qcn")
    g_uq_all = _matmul(q_cn, dq_all, trans_a=True, out_dtype=BF16, name="g_uq").reshape(MLA_Q_RANK, n_grp, -1)
    g_uq_nope = g_uq_all[:, :, :Q_HEAD_GROUP * MLA_NOPE].reshape(MLA_Q_RANK, -1)
    g_uq_rope = g_uq_all[:, :, Q_HEAD_GROUP * MLA_NOPE:].reshape(MLA_Q_RANK, -1)
    dsmall, g_qg, g_kvg = _mla_prep_bwd(proj1, dq_cn, dkeys, q_norm_g, mla_kv_norm_g, cos_t, sin_t, name="mla_prep_bwd")
    dproj1 = jnp.concatenate([dz, dsmall], axis=1)
    g_w_in1 =_matmul(h1, dproj1, trans_a=True, out_dtype=BF16, tm=1024, tn=1280, name="g_in1")

    g_uq_rope = jnp.transpose(g_uq_rope.reshape(MLA_Q_RANK, MLA_HEADS // 2, 2, 2, ROPE_HALF), (0, 1, 3, 2, 4))
    g_uq = jnp.concatenate([g_uq_nope.reshape(MLA_Q_RANK, MLA_HEADS, MLA_NOPE), g_uq_rope.reshape(MLA_Q_RANK, MLA_HEADS, MLA_ROPE)], axis=2)
    g_w_in1 = jnp.concatenate([g_w_in1[:, MLA_WIDTH:MLA_WIDTH + ODD_SMALL], g_w_in1[:, :MLA_WIDTH]], axis=1)
    g_w_in1 = jnp.transpose(g_w_in1.reshape(d, N_CHIPS, -1), (1, 0, 2))
    big1 = [
        _halves(g_w_in1),
        _halves(g_uq.reshape(N_CHIPS, q_rank_sh, -1)),
        _halves(g_w_out1.reshape(N_CHIPS, -1, d)),
        _halves(g_uk.astype(BF16).reshape(N_CHIPS, -1, MLA_NOPE)),
        _halves(g_uv.astype(BF16).reshape(N_CHIPS, -1, MLA_V)),
    ]
    parts1 = _reduce_sibling(big1, name="reduce_sibling1")
    lands2 = [lax.dynamic_update_slice(lax.empty(p.shape, BF16), lax.dynamic_slice_in_dim(p, chip, 1, axis=0), (chip, 0, 0))
              for p in parts1]
    flight2 = _chips_start(parts1, lands2, loss, scatter=True, name="reduce1_start")

    gate[0] = gate[0] + flight2[3][:1, :1]
    dy0, dres0, g_ln_g0, g_ln_b0, dgate0, dscale1, dshift1 = _dh_mid_ln_bwd(
        dproj1, w_in1, x2, y0, gate[0], ln_g[0:1], ln_b[0:1], dres1, scale[1], x1, name="d_h1_mid_ln")
    dmix0 = _matmul(dy0, w_out0, trans_b=True, tn=2048, out_dtype=BF16, name="d_mix0")
    g_w_out0 = _matmul(mix0, dy0, trans_a=True, out_dtype=BF16, tm=1024, name="g_out0")
    dproj0, g_ws, g_bs_t, g_ng, g_nb, g_pw, g_pb, g_ps = _even_bwd(
        proj0, dmix0, ws, ws_t, bs_t, gmlp_norm_g, gmlp_norm_b, pool_w_bf, pool_b, pool_scale, name="even_bwd")
    g_w_in0 = _matmul(h0, dproj0, trans_a=True, out_dtype=BF16, out_stacked=True, tm=1024, tn=1280, name="g_in0")

    g_pw = jnp.transpose(g_pw.astype(BF16).reshape(POOL_GROUPS, N_CHIPS, -1, POOL_GROUP_DIM), (1, 0, 2, 3))
    big0 = [
        _halves(g_w_in0),
        _halves(g_pw.reshape(N_CHIPS, -1, POOL_GROUP_DIM)),
        _halves(g_w_out0.reshape(N_CHIPS, -1, d)),
        _halves(g_ws.astype(BF16)),
    ]
    parts0 = _reduce_sibling(big0, name="reduce_sibling0")
    landed1 = _chips_wait(*flight2[:3], parts0[0], scatter=True, name="reduce1_wait")
    lands3 = [lax.dynamic_update_slice(lax.empty(p.shape, BF16), lax.dynamic_slice_in_dim(p, chip, 1, axis=0), (chip, 0, 0))
              for p in parts0]
    flight3 = _chips_start(parts0, lands3, landed1[0], scatter=True, name="reduce0_start")
    grad_x, dscale0, dshift0 = _dh_input_bwd(dproj0, w_in0, x2, dres0, scale[0], after=flight3[3], name="d_h0_input")

    small_local = {
        "ln_g": jnp.concatenate([g_ln_g0, g_ln_g1]), "ln_b": jnp.concatenate([g_ln_b0, g_ln_b1]),
        "gmlp_norm_g": g_ng, "gmlp_norm_b": g_nb, "gmlp_bs": g_bs_t[:, :GMLP_HEADS].T, "pool_b": g_pb, "pool_scale": g_ps,
        "mla_kv_norm_g": g_kvg, "mla_q_norm_g": g_qg,
    }
    n_mod = 2 * 3 * d
    vec = jnp.concatenate([dshift0, dscale0, dgate0, dshift1, dscale1, dgate1]
                          + [small_local[n].reshape(1, -1) for n in SMALL] + [loss], axis=1)
    n_vec = vec.shape[1]
    vec = _pad_cols(vec, -(-n_vec // (8 * LANES)) * 8 * LANES).reshape(-1, LANES)
    vec_all = _all_gather_devices(vec, name="gather_small")
    vec_sum = _sum_devices(vec_all, name="sum_small").reshape(-1)
    dmod_all = vec_all.reshape(N_DEV, -1)[:, :n_mod].reshape(N_DEV, 2, 3 * d)
    dmod_sh = jnp.transpose(lax.dynamic_slice_in_dim(dmod_all, chip * cols, cols, axis=2), (1, 0, 2))
    dmod_sh = jnp.concatenate([dmod_sh, jnp.zeros((2, LANES - N_DEV, cols), F32)], axis=1)
    grads = {"ada_w": _ada_grad(_pad_cols(c_all.T, LANES), dmod_sh, name="ada_grad"), "ada_b": vec_sum[:n_mod].reshape(2, 3 * d)}
    off = n_mod
    for n in SMALL:
        sz = small_local[n].size
        grads[n] = vec_sum[off:off + sz]
        off += sz
    grads["mla_q_norm_g"] = lax.dynamic_slice_in_dim(grads["mla_q_norm_g"], chip * q_rank_sh, q_rank_sh)
    for n in SMALL:
        grads[n] = grads[n].reshape(weights[n].shape)

    landed0 = _chips_wait(*flight3[:3], grads["ada_w"], scatter=True, name="reduce0_wait")
    totals = _reduce_chips([], list(landed0) + list(landed1), name="reduce_chips")
    for n, t in zip(("e_w_in", "pool_w", "e_w_out", "gmlp_ws", "o_w_in", "mla_w_uq", "o_w_out"), totals):
        if n != "gmlp_ws":
            grads[n] = t.reshape(weights[n].shape)
    rep = jnp.concatenate([t.reshape(-1, LANES) for t in (totals[3], totals[7], totals[8])])
    rep_land = lax.dynamic_update_slice(lax.empty((N_CHIPS,) + rep.shape, F32), rep[None], (chip, 0, 0))
    flight4 = _chips_start([rep], [rep_land], totals[0], scatter=False, name="gather_rep_start")

    delta, new_m, new_v = {}, {}, {}
    replicated = ("gmlp_ws", "mla_w_uk", "mla_w_uv")
    large = [n for n in WEIGHTS if n not in SMALL and n != "ada_b"]
    for n in large:
        if n not in replicated:
            delta[n], new_m[n], new_v[n] = _adamw(weights[n], grads[n], mom[n], var[n], after=flight4[3], name="adamw_" + n)
    rep = _chips_wait(*flight4[:3], delta["e_w_in"], scatter=False, name="gather_rep_wait")[0]
    r_ws, r_uk = GMLP_BLOCK, 4 * MLA_KV_RANK
    grads["gmlp_ws"] = rep[:, :r_ws].reshape(weights["gmlp_ws"].shape)
    grads["mla_w_uk"] = jnp.transpose(rep[:, r_ws:r_ws + r_uk].reshape(MLA_HEADS, MLA_KV_RANK, MLA_NOPE), (1, 0, 2))[None]
    grads["mla_w_uv"] = jnp.transpose(rep[:, r_ws + r_uk:].reshape(MLA_HEADS, MLA_KV_RANK, MLA_V), (1, 0, 2))[None]
    for n in replicated:
        delta[n], new_m[n], new_v[n] = _adamw(weights[n], grads[n], mom[n], var[n], name="adamw_" + n)
    small = [n for n in WEIGHTS if n not in large]
    ds, ms, vs = _adamw_small([weights[n] for n in small], [grads[n] for n in small], [mom[n] for n in small],
                              [var[n] for n in small], name="adamw_small")
    for n, dn, mn, vn in zip(small, ds, ms, vs):
        delta[n], new_m[n], new_v[n] = dn, mn, vn

    return (vec_sum[n_vec - 1], grad_x[None], *[grads[n] for n in WEIGHTS], *[delta[n] for n in WEIGHTS],
            *[new_m[n] for n in WEIGHTS], *[new_v[n] for n in WEIGHTS])
```

```python
import jax
import jax.numpy as jnp
from jax import lax
from jax.experimental import pallas as pl
from jax.experimental.pallas import tpu as pltpu

F32 = jnp.float32
BF16 = jnp.bfloat16
MESH = pl.DeviceIdType.MESH

D_MODEL = 1024
CHUNK = 64
LN_EPS = 1e-5
GMLP_HEADS = 4
GMLP_HEAD_DIM = 256
GMLP_BLOCK = 128
POOL_WINDOWS = (2, 4, 8, 16)
POOL_GROUPS = 4
POOL_GROUP_DIM = 256
POOL_HALO = 16
EVEN_IN = 5120
MLA_HEADS = 16
MLA_NOPE = 128
MLA_ROPE = 64
MLA_V = 128
MLA_Q_RANK = 256
MLA_KV_RANK = 128
MLA_WIDTH = MLA_HEADS * MLA_V
ODD_IN = 2496
ODD_SMALL = MLA_Q_RANK + MLA_KV_RANK + MLA_ROPE
ODD_SMALL_PAD = 512
QK_PAD = 256
ROPE_THETA = 10000.0
ATTN_SCALE = (MLA_NOPE + MLA_ROPE) ** -0.5
DEEPNORM_ALPHA = (2.0 * 2) ** 0.25
ADAM_LR = 0.001
ADAM_B1 = 0.9
ADAM_B2 = 0.999
ADAM_EPS = 1e-08
ADAM_WD = 0.01
ADAM_STEP = 10
NEG = -1e30
LANES = 128
N_DEV = 8
N_CHIPS = 4
VMEM_LIMIT_BYTES = 56 * 1024 * 1024
HBM = pl.BlockSpec(memory_space=pltpu.HBM)
VMEM = pl.BlockSpec(memory_space=pltpu.VMEM)


def _params(*sem):
    return pltpu.CompilerParams(dimension_semantics=sem if sem else None, vmem_limit_bytes=VMEM_LIMIT_BYTES)


def _tile(dim, pref):
    for t in (pref, 2048, 1280, 1024, 512, 256, 128):
        if t <= min(pref, dim) and dim % t == 0:
            return t
    return dim


def _sigmoid(z):
    return 1.0 / (1.0 + jnp.exp(-z))


def _dot(a, b, dims):
    return lax.dot_general(a, b, (dims, ((), ())), preferred_element_type=F32)


NN = ((1,), (0,))
NT = ((1,), (1,))
TN = ((0,), (0,))


def _matmul(a, b, *, name, trans_a=False, trans_b=False, out_dtype=F32, b_stacked=False, out_stacked=False,
            tm=512, tn=1024, tk=2048, after=None):
    k, m = a.shape if trans_a else a.shape[::-1]
    if b_stacked:
        ns, kb, n_sh = b.shape
        kb, n = (ns * n_sh, kb) if trans_b else (kb, ns * n_sh)
    else:
        n, kb = b.shape if trans_b else b.shape[::-1]
    assert k == kb, (a.shape, b.shape)
    tm = _tile(m, tm)
    if b_stacked and trans_b:
        tn, tk = _tile(n, tn), n_sh
    elif b_stacked or out_stacked:
        tn, tk = _tile(n // N_CHIPS, tn), _tile(k, tk)
    else:
        tn, tk = _tile(n, tn), _tile(k, tk)
    nk = k // tk
    per = max((n // N_CHIPS) // tn, 1)
    dims = ((0 if trans_a else 1,), (1 if trans_b else 0,))

    def body_one(a_ref, b_ref, *rest):
        o_ref = rest[-1]
        o_ref[...] = _dot(a_ref[...].astype(BF16), b_ref[...].astype(BF16), dims).astype(out_dtype)

    def body_acc(a_ref, b_ref, *rest):
        o_ref, acc_ref = rest[-2:]
        kk = pl.program_id(2)

        @pl.when(kk == 0)
        def _():
            acc_ref[...] = jnp.zeros_like(acc_ref)

        acc_ref[...] += _dot(a_ref[...].astype(BF16), b_ref[...].astype(BF16), dims)

        @pl.when(kk == nk - 1)
        def _():
            o_ref[...] = acc_ref[...].astype(out_dtype)

    a_spec = pl.BlockSpec((tk, tm), lambda i, j, kk: (kk, i)) if trans_a else pl.BlockSpec((tm, tk), lambda i, j, kk: (i, kk))
    if b_stacked and trans_b:
        b_spec = pl.BlockSpec((None, tn, tk), lambda i, j, kk: (kk, j, 0))
    elif b_stacked:
        b_spec = pl.BlockSpec((None, tk, tn), lambda i, j, kk: (j // per, kk, j % per))
    elif trans_b:
        b_spec = pl.BlockSpec((tn, tk), lambda i, j, kk: (j, kk))
    else:
        b_spec = pl.BlockSpec((tk, tn), lambda i, j, kk: (kk, j))
    if out_stacked:
        o_spec = pl.BlockSpec((None, tm, tn), lambda i, j, kk: (j // per, i, j % per))
        o_shape = jax.ShapeDtypeStruct((N_CHIPS, m, n // N_CHIPS), out_dtype)
    else:
        o_spec = pl.BlockSpec((tm, tn), lambda i, j, kk: (i, j))
        o_shape = jax.ShapeDtypeStruct((m, n), out_dtype)
    order = [] if after is None else [after]
    return pl.pallas_call(
        body_one if nk == 1 else body_acc, name=name, grid=(m // tm, n // tn, nk),
        in_specs=[a_spec, b_spec] + [pl.BlockSpec(memory_space=pl.ANY)] * len(order),
        out_specs=o_spec, out_shape=o_shape, scratch_shapes=[] if nk == 1 else [pltpu.VMEM((tm, tn), F32)],
        compiler_params=_params("parallel", "parallel", "arbitrary"),
    )(a, b, *order)


def _matmul_rows(a, b, epilogue, row_ins, vec_ins, row_outs, vec_outs, *, name, trans_b=False, b_stacked=False,
                 tm=512, tk=2048, after=None):
    m, k = a.shape
    if b_stacked:
        ns, n, n_sh = b.shape
        assert trans_b and ns * n_sh == k
        tk = n_sh
    else:
        n = b.shape[0] if trans_b else b.shape[1]
        tk = _tile(k, tk)
    tm = _tile(m, tm)
    nk = k // tk
    dims = ((1,), (1 if trans_b else 0,))
    n_ri, n_vi, n_ro, n_vo = len(row_ins), len(vec_ins), len(row_outs), len(vec_outs)
    order = [] if after is None else [after]

    def body(*refs):
        a_ref, b_ref = refs[:2]
        pos = 2
        rin = refs[pos:pos + n_ri]
        pos += n_ri
        vin = refs[pos:pos + n_vi]
        pos += n_vi + len(order)
        rout = refs[pos:pos + n_ro]
        pos += n_ro
        vout = refs[pos:pos + n_vo]
        first = pl.program_id(0) == 0
        part = _dot(a_ref[...].astype(BF16), b_ref[...].astype(BF16), dims)
        if nk == 1:
            epilogue(part, first, rin, vin, rout, vout)
        else:
            acc_ref = refs[-1]
            kk = pl.program_id(1)

            @pl.when(kk == 0)
            def _():
                acc_ref[...] = part

            @pl.when(kk > 0)
            def _():
                acc_ref[...] += part

            @pl.when(kk == nk - 1)
            def _():
                epilogue(acc_ref[...], first, rin, vin, rout, vout)

    a_spec = pl.BlockSpec((tm, tk), lambda i, kk: (i, kk))
    if b_stacked:
        b_spec = pl.BlockSpec((None, n, tk), lambda i, kk: (kk, 0, 0))
    elif trans_b:
        b_spec = pl.BlockSpec((n, tk), lambda i, kk: (0, kk))
    else:
        b_spec = pl.BlockSpec((tk, n), lambda i, kk: (kk, 0))
    row = pl.BlockSpec((tm, n), lambda i, kk: (i, 0))
    vec = lambda w: pl.BlockSpec((1, w), lambda i, kk: (0, 0))
    return pl.pallas_call(
        body, name=name, grid=(m // tm, nk),
        in_specs=[a_spec, b_spec] + [row] * n_ri + [vec(v.shape[1]) for v in vec_ins] + [pl.BlockSpec(memory_space=pl.ANY)] * len(order),
        out_specs=[row] * n_ro + [vec(w) for w in vec_outs],
        out_shape=[jax.ShapeDtypeStruct((m, n), dt) for dt in row_outs] + [jax.ShapeDtypeStruct((1, w), F32) for w in vec_outs],
        scratch_shapes=[] if nk == 1 else [pltpu.VMEM((tm, n), F32)],
        compiler_params=_params("arbitrary", "arbitrary"),
    )(a, b, *row_ins, *vec_ins, *order)


def _row_spec(ts, d):
    return pl.BlockSpec((ts, d), lambda i: (i, 0))


def _vec_spec(d):
    return pl.BlockSpec((1, d), lambda i: (0, 0))


def _modulate(x, scale, shift, *, name):
    s, d = x.shape
    ts = _tile(s, 512)

    def body(x_ref, sc_ref, sh_ref, h_ref):
        h_ref[...] = (x_ref[...] * (1.0 + sc_ref[...]) + sh_ref[...]).astype(BF16)

    return pl.pallas_call(
        body, name=name, grid=(s // ts,), in_specs=[_row_spec(ts, d), _vec_spec(d), _vec_spec(d)],
        out_specs=_row_spec(ts, d), out_shape=jax.ShapeDtypeStruct((s, d), BF16), compiler_params=_params("parallel"),
    )(x, scale, shift)


def _ln_stats(pre):
    mu = jnp.mean(pre, axis=-1, keepdims=True)
    xc = pre - mu
    var = jnp.mean(xc * xc, axis=-1, keepdims=True)
    rstd = lax.rsqrt(var + LN_EPS)
    return xc * rstd, rstd


def _ln_bwd_rows(dout, xhat, rstd, g):
    dxh = dout * g
    m1 = jnp.mean(dxh, axis=-1, keepdims=True)
    m2 = jnp.mean(dxh * xhat, axis=-1, keepdims=True)
    return rstd * (dxh - m1 - xhat * m2)


def _colsum(v):
    return jnp.sum(v, axis=0, keepdims=True)


def _out_resid_ln(mix, w_out, x, gate, g, b, scale_next, shift_next, *, name):
    def epilogue(y, first, rin, vin, rout, vout):
        (x_ref,), (gate_ref, g_ref, b_ref, sc_ref, sh_ref), (y_ref, xn_ref, h_ref) = rin, vin, rout
        y_ref[...] = y
        pre = DEEPNORM_ALPHA * x_ref[...] + (1.0 + gate_ref[...]) * y
        xhat, _ = _ln_stats(pre)
        xn = xhat * g_ref[...] + b_ref[...]
        xn_ref[...] = xn
        h_ref[...] = (xn * (1.0 + sc_ref[...]) + sh_ref[...]).astype(BF16)

    return _matmul_rows(mix, w_out, epilogue, [x], [gate, g, b, scale_next, shift_next], [F32, F32, BF16], [], name=name)


def _out_loss_ln_bwd(og, w_out, x, gate, g, b, target, *, name):
    d = x.shape[1]

    def epilogue(yv, first, rin, vin, rout, vout):
        (x_ref, t_ref), (gate_ref, g_ref, b_ref), (dy_ref, dres_ref), (dg_ref, db_ref, dgate_ref, loss_ref) = rin, vin, rout, vout

        @pl.when(first)
        def _():
            for r in vout:
                r[...] = jnp.zeros_like(r)

        pre = DEEPNORM_ALPHA * x_ref[...] + (1.0 + gate_ref[...]) * yv
        xhat, rstd = _ln_stats(pre)
        diff = xhat * g_ref[...] + b_ref[...] - t_ref[...]
        loss_ref[...] += (0.5 / d) * jnp.sum(jnp.sum(diff * diff, axis=1, keepdims=True), axis=0, keepdims=True)
        dout = diff * (1.0 / d)
        dpre = _ln_bwd_rows(dout, xhat, rstd, g_ref[...])
        dy_ref[...] = (dpre * (1.0 + gate_ref[...])).astype(BF16)
        dres_ref[...] = DEEPNORM_ALPHA * dpre
        dg_ref[...] += _colsum(dout * xhat)
        db_ref[...] += _colsum(dout)
        dgate_ref[...] += _colsum(dpre * yv)

    return _matmul_rows(og, w_out, epilogue, [x, target], [gate, g, b], [BF16, F32], [d, d, d, 1], name=name)


def _dh_mid_ln_bwd(dproj, w_in, x, y, gate, g, b, dres_next, scale_next, x_next, *, name):
    d = x.shape[1]

    def epilogue(dh, first, rin, vin, rout, vout):
        (x_ref, y_ref, dr_ref, xn_ref), (gate_ref, g_ref, b_ref, sc_ref), (dy_ref, dres_ref) = rin, vin, rout
        dg_ref, db_ref, dgate_ref, dscale_ref, dshift_ref = vout

        @pl.when(first)
        def _():
            for r in vout:
                r[...] = jnp.zeros_like(r)

        dout = dr_ref[...] + dh * (1.0 + sc_ref[...])
        dscale_ref[...] += _colsum(dh * xn_ref[...])
        dshift_ref[...] += _colsum(dh)
        yv = y_ref[...]
        pre = DEEPNORM_ALPHA * x_ref[...] + (1.0 + gate_ref[...]) * yv
        xhat, rstd = _ln_stats(pre)
        dpre = _ln_bwd_rows(dout, xhat, rstd, g_ref[...])
        dy_ref[...] = (dpre * (1.0 + gate_ref[...])).astype(BF16)
        dres_ref[...] = DEEPNORM_ALPHA * dpre
        dg_ref[...] += _colsum(dout * xhat)
        db_ref[...] += _colsum(dout)
        dgate_ref[...] += _colsum(dpre * yv)

    return _matmul_rows(dproj, w_in, epilogue, [x, y, dres_next, x_next], [gate, g, b, scale_next], [BF16, F32], [d] * 5,
                        trans_b=True, tk=1280, name=name)


def _dh_input_bwd(dproj, w_in_stacked, x, dres, scale, *, name, after):
    d = x.shape[1]

    def epilogue(dh, first, rin, vin, rout, vout):
        (x_ref, dr_ref), (sc_ref,), (dx_ref,), (dscale_ref, dshift_ref) = rin, vin, rout, vout

        @pl.when(first)
        def _():
            for r in vout:
                r[...] = jnp.zeros_like(r)

        dx_ref[...] = dr_ref[...] + dh * (1.0 + sc_ref[...])
        dscale_ref[...] += _colsum(dh * x_ref[...])
        dshift_ref[...] += _colsum(dh)

    return _matmul_rows(dproj, w_in_stacked, epilogue, [x, dres], [scale], [F32], [d, d], trans_b=True, b_stacked=True,
                        tm=1024, after=after, name=name)


def _chunk_mask(transposed=False):
    r = lax.broadcasted_iota(jnp.int32, (GMLP_BLOCK, GMLP_BLOCK), 0) // CHUNK
    c = lax.broadcasted_iota(jnp.int32, (GMLP_BLOCK, GMLP_BLOCK), 1) // CHUNK
    return (r <= c) if transposed else (c <= r)


def _window_sum(ext, steps, forward):
    rows = ext.shape[0]
    acc = ext
    for k in range(steps):
        shift = 1 << k
        acc = acc + pltpu.roll(acc, (rows - shift) if forward else shift, 0)
    return acc


def _pool_counts(first_row, rows, win):
    t = first_row + lax.broadcasted_iota(jnp.int32, (rows, 1), 0)
    return jnp.minimum(t + 1, win).astype(F32)


def _even_specs(t):
    col = lambda j: pl.BlockSpec((t, D_MODEL), lambda n: (n, j))
    per = t // POOL_HALO
    prev = pl.BlockSpec((POOL_HALO, D_MODEL), lambda n: (jnp.maximum(n * per - 1, 0), 3))
    return col, per, prev


def _full(shape):
    return pl.BlockSpec(shape, lambda n: (0,) * len(shape))


def _gmlp_head(v_h, ng, nb, w_bf):
    xhat, rstd = _ln_stats(v_h)
    vn = (xhat * ng + nb).astype(BF16)
    return xhat, rstd, vn, _dot(w_bf, vn, NN)


def _pool_group(xb_g, prev_g, first_row, grp):
    t = xb_g.shape[0]
    ext = jnp.concatenate([prev_g, xb_g], axis=0)
    tot = _window_sum(ext, grp + 1, False)[POOL_HALO:, :]
    cnt = _pool_counts(first_row, t, POOL_WINDOWS[grp])
    return tot / cnt - xb_g, cnt


def _even_fwd(proj, ws, bs_t, ng, nb, pool_w, pool_b, pool_scale, *, name):
    s = proj.shape[0]
    t = GMLP_BLOCK
    col, per, prev = _even_specs(t)

    def body(u_ref, v_ref, za_ref, xb_ref, zb_ref, xp_ref, ws_ref, bs_ref, ng_ref, nb_ref, pw_ref, pb_ref, ps_ref, o_ref):
        n = pl.program_id(0)
        mask = _chunk_mask()
        for h in range(GMLP_HEADS):
            c0 = h * GMLP_HEAD_DIM
            cs = slice(c0, c0 + GMLP_HEAD_DIM)
            w_bf = jnp.where(mask, ws_ref[h], 0.0).astype(BF16)
            _, _, _, sv = _gmlp_head(v_ref[:, cs].astype(F32),ng_ref[...], nb_ref[...], w_bf)
            sv = sv + bs_ref[:, h:h + 1]
            za = za_ref[:, cs].astype(F32)
            o_ref[:, cs] = (u_ref[:, cs].astype(F32) * sv * (za * _sigmoid(za))).astype(BF16)
        live = (n > 0).astype(F32)
        for grp in range(POOL_GROUPS):
            c0 = grp * POOL_GROUP_DIM
            cs = slice(c0, c0 + POOL_GROUP_DIM)
            pooled, _ = _pool_group(xb_ref[:, cs].astype(F32), xp_ref[:, cs].astype(F32) * live, n * t, grp)
            yb = _dot(pooled.astype(BF16), pw_ref[grp], NN) + pb_ref[:, cs]
            zb = zb_ref[:, cs].astype(F32)
            o_ref[:, D_MODEL + c0:D_MODEL + c0 + POOL_GROUP_DIM] = (yb * ps_ref[:, cs] * (zb * _sigmoid(zb))).astype(BF16)

    return pl.pallas_call(
        body, name=name, grid=(s // t,),
        in_specs=[col(0), col(1), col(2), col(3), col(4), prev,
                  _full((GMLP_HEADS, t, t)), _full((t, LANES)), _full((1, GMLP_HEAD_DIM)), _full((1, GMLP_HEAD_DIM)),
                  _full((POOL_GROUPS, POOL_GROUP_DIM, POOL_GROUP_DIM)), _full((1, D_MODEL)), _full((1, D_MODEL))],
        out_specs=pl.BlockSpec((t, 2 * D_MODEL), lambda n: (n, 0)),
        out_shape=jax.ShapeDtypeStruct((s, 2 * D_MODEL), BF16),
        compiler_params=_params("parallel"),
    )(proj, proj, proj, proj, proj, proj, ws, bs_t, ng, nb, pool_w, pool_b, pool_scale)


def _even_bwd(proj, dmix, ws, ws_t, bs_t, ng, nb, pool_w, pool_b, pool_scale, *, name):
    s = proj.shape[0]
    t = GMLP_BLOCK
    nblk = s // t
    col, per, prev = _even_specs(t)
    nxt = lambda j: pl.BlockSpec((POOL_HALO, D_MODEL), lambda n: (jnp.minimum((n + 1) * per, nblk * per - 1), j))

    def body(u_ref, v_ref, za_ref, xb_ref, zb_ref, xp_ref, zn_ref, da_ref, db_ref, dbn_ref,
             ws_ref, wst_ref, bs_ref, ng_ref, nb_ref, pw_ref, pb_ref, ps_ref,
             dp_ref, gws_ref, gbs_ref, gng_ref, gnb_ref, gpw_ref, gpb_ref, gps_ref):
        n = pl.program_id(0)

        @pl.when(n == 0)
        def _():
            for r in (gws_ref, gbs_ref, gng_ref, gnb_ref, gpw_ref, gpb_ref, gps_ref):
                r[...] = jnp.zeros_like(r)

        mask, mask_t = _chunk_mask(), _chunk_mask(True)
        lane = lax.broadcasted_iota(jnp.int32, (t, LANES), 1)
        ngv, nbv = ng_ref[...], nb_ref[...]
        for h in range(GMLP_HEADS):
            c0 = h * GMLP_HEAD_DIM
            cs = slice(c0, c0 + GMLP_HEAD_DIM)
            w_bf = jnp.where(mask, ws_ref[h], 0.0).astype(BF16)
            wt_bf = jnp.where(mask_t, wst_ref[h], 0.0).astype(BF16)
            xhat, rstd, vn, sv = _gmlp_head(v_ref[:, cs].astype(F32),ngv, nbv, w_bf)
            sv = sv + bs_ref[:, h:h + 1]
            za, u, da = za_ref[:, cs].astype(F32), u_ref[:, cs].astype(F32), da_ref[:, cs].astype(F32)
            sg = _sigmoid(za)
            sl = za * sg
            dp_ref[:, cs] = (da * sv * sl).astype(BF16)
            dp_ref[:, 2 * D_MODEL + c0:2 * D_MODEL + c0 + GMLP_HEAD_DIM] = (
                da * u * sv * (sg * (1.0 + za * (1.0 - sg)))).astype(BF16)
            dsv = da * u * sl
            gbs_ref[...] += jnp.where(lane == h, jnp.sum(dsv, axis=1, keepdims=True), 0.0)
            dsv_bf = dsv.astype(BF16)
            gws_ref[h] += jnp.where(mask, _dot(dsv_bf, vn, NT), 0.0)
            dvn = _dot(wt_bf, dsv_bf, NN)
            dp_ref[:, D_MODEL + c0:D_MODEL + c0 + GMLP_HEAD_DIM] = _ln_bwd_rows(dvn, xhat, rstd, ngv).astype(BF16)
            gng_ref[...] += _colsum(dvn * xhat)
            gnb_ref[...] += _colsum(dvn)
        live_prev = (n > 0).astype(F32)
        live_next = (n < nblk - 1).astype(F32)
        for grp in range(POOL_GROUPS):
            c0 = grp * POOL_GROUP_DIM
            cs = slice(c0, c0 + POOL_GROUP_DIM)
            xb = xb_ref[:, cs].astype(F32)
            pooled, cnt = _pool_group(xb, xp_ref[:, cs].astype(F32) * live_prev, n * t, grp)
            pooled_bf = pooled.astype(BF16)
            pw = pw_ref[grp]
            yb = _dot(pooled_bf, pw, NN) + pb_ref[:, cs]
            ps = ps_ref[:, cs]
            zb, db = zb_ref[:, cs].astype(F32), db_ref[:, cs].astype(F32)
            sg = _sigmoid(zb)
            sl = zb * sg
            dp_ref[:, 4 * D_MODEL + c0:4 * D_MODEL + c0 + POOL_GROUP_DIM] = (
                db * yb * ps * (sg * (1.0 + zb * (1.0 - sg)))).astype(BF16)
            dsl = db * sl
            dy = dsl * ps
            gps_ref[:, cs] += _colsum(dsl * yb)
            gpb_ref[:, cs] += _colsum(dy)
            dy_bf = dy.astype(BF16)
            gpw_ref[grp] += _dot(pooled_bf, dy_bf, TN)
            r = _dot(dy_bf, pw, NT)
            zn = zn_ref[:, cs].astype(F32)
            dyn = (dbn_ref[:, cs].astype(F32) * (zn * _sigmoid(zn)) * ps * live_next).astype(BF16)
            rn = _dot(dyn, pw, NT) / _pool_counts((n + 1) * t, POOL_HALO, POOL_WINDOWS[grp])
            ext = jnp.concatenate([r / cnt, rn], axis=0)
            dxb = _window_sum(ext, grp + 1, True)[:t, :] - r
            dp_ref[:, 3 * D_MODEL + c0:3 * D_MODEL + c0 + POOL_GROUP_DIM] = dxb.astype(BF16)

    out_shape = [
        jax.ShapeDtypeStruct((s, EVEN_IN), BF16),
        jax.ShapeDtypeStruct((GMLP_HEADS, t, t), F32), jax.ShapeDtypeStruct((t, LANES), F32),
        jax.ShapeDtypeStruct((1, GMLP_HEAD_DIM), F32), jax.ShapeDtypeStruct((1, GMLP_HEAD_DIM), F32),
        jax.ShapeDtypeStruct((POOL_GROUPS, POOL_GROUP_DIM, POOL_GROUP_DIM), F32),
        jax.ShapeDtypeStruct((1, D_MODEL), F32), jax.ShapeDtypeStruct((1, D_MODEL), F32),
    ]
    return pl.pallas_call(
        body, name=name, grid=(nblk,),
        in_specs=[col(0), col(1), col(2), col(3), col(4), prev, nxt(4),
                  pl.BlockSpec((t, D_MODEL), lambda n: (n, 0)), pl.BlockSpec((t, D_MODEL), lambda n: (n, 1)), nxt(1),
                  _full((GMLP_HEADS, t, t)), _full((GMLP_HEADS, t, t)), _full((t, LANES)),
                  _full((1, GMLP_HEAD_DIM)), _full((1, GMLP_HEAD_DIM)),
                  _full((POOL_GROUPS, POOL_GROUP_DIM, POOL_GROUP_DIM)), _full((1, D_MODEL)), _full((1, D_MODEL))],
        out_specs=[pl.BlockSpec((t, EVEN_IN), lambda n: (n, 0))] + [_full(o.shape) for o in out_shape[1:]],
        out_shape=out_shape,
        compiler_params=_params("arbitrary"),
    )(proj, proj, proj, proj, proj, proj, proj, dmix, dmix, dmix, ws, ws_t, bs_t, ng, nb, pool_w, pool_b, pool_scale)


ROPE_HALF = MLA_ROPE // 2


def _rope(v, cos, sin_signed):
    return v * cos + pltpu.roll(v, 2 * ROPE_HALF, 1) * sin_signed


def _rope_bwd(d, cos, sin_signed):
    return d * cos + pltpu.roll(d * sin_signed, 2 * ROPE_HALF, 1)


def _slab_lanes(shape, which):
    lane = lax.broadcasted_iota(jnp.int32, shape, 1)
    return (lane // ROPE_HALF) % 2 == which


def _rms(v, g):
    r = lax.rsqrt(jnp.mean(v * v, axis=-1, keepdims=True) + LN_EPS)
    return v * r * g, r


def _rms_bwd(dy, v, r, g):
    u = dy * g
    return r * u - v * (r * r * r) * jnp.mean(u * v, axis=-1, keepdims=True)


def _mla_prep(proj, gq, gkv, cos, sin_signed, *, name):
    s = proj.shape[0]
    ts = _tile(s, 512)

    def body(p_ref, gq_ref, gkv_ref, c_ref, s_ref, q_ref, k_ref):
        qcn, _ = _rms(p_ref[:, :MLA_Q_RANK].astype(F32), gq_ref[...])
        kvn, _ = _rms(p_ref[:, MLA_Q_RANK:MLA_Q_RANK + MLA_KV_RANK].astype(F32), gkv_ref[...])
        kr = p_ref[:, MLA_Q_RANK + MLA_KV_RANK:].astype(F32)
        lane = lax.broadcasted_iota(jnp.int32, kr.shape, 1)
        by1, by2 = pltpu.roll(kr, ROPE_HALF, 1), pltpu.roll(kr, 2 * ROPE_HALF, 1)
        both = jnp.where(lane < ROPE_HALF, kr, jnp.where(lane < 3 * ROPE_HALF, by1, by2))
        kr = _rope(both, c_ref[...], s_ref[...])
        q_ref[...] = qcn.astype(BF16)
        k_ref[...] = jnp.concatenate([kvn, kr], axis=1).astype(BF16)

    return pl.pallas_call(
        body, name=name, grid=(s // ts,),
        in_specs=[_small_spec(ts), _vec_spec(MLA_Q_RANK), _vec_spec(MLA_KV_RANK), _row_spec(ts, LANES), _row_spec(ts, LANES)],
        out_specs=[_row_spec(ts, MLA_Q_RANK), _row_spec(ts, QK_PAD)],
        out_shape=[jax.ShapeDtypeStruct((s, MLA_Q_RANK), BF16), jax.ShapeDtypeStruct((s, QK_PAD), BF16)],
        compiler_params=_params("parallel"),
    )(proj, gq, gkv, cos, sin_signed)


def _mla_prep_bwd(proj, dqcn, dkv, gq, gkv, cos, sin_signed, *, name):
    s = proj.shape[0]
    ts = _tile(s, 512)

    def body(p_ref, dq_ref, dkv_ref, gq_ref, gkv_ref, c_ref, s_ref, ds_ref, ggq_ref, ggkv_ref):
        @pl.when(pl.program_id(0) == 0)
        def _():
            ggq_ref[...] = jnp.zeros_like(ggq_ref)
            ggkv_ref[...] = jnp.zeros_like(ggkv_ref)

        qc = p_ref[:, :MLA_Q_RANK].astype(F32)
        kvc = p_ref[:, MLA_Q_RANK:MLA_Q_RANK + MLA_KV_RANK].astype(F32)
        _, rq = _rms(qc, gq_ref[...])
        _, rkv = _rms(kvc, gkv_ref[...])
        dq = dq_ref[...]
        dkvn = dkv_ref[:, :MLA_KV_RANK]
        ggq_ref[...] += _colsum(dq * qc * rq)
        ggkv_ref[...] += _colsum(dkvn * kvc * rkv)
        dboth = _rope_bwd(dkv_ref[:, MLA_KV_RANK:], c_ref[...], s_ref[...])
        lane = lax.broadcasted_iota(jnp.int32, dboth.shape, 1)
        pair = dboth + pltpu.roll(dboth, 3 * ROPE_HALF, 1)
        dkr = jnp.where(lane < ROPE_HALF, pair, jnp.where(lane < 2 * ROPE_HALF, pltpu.roll(pair, 3 * ROPE_HALF, 1), 0.0))
        ds_ref[...] = jnp.concatenate(
            [_rms_bwd(dq, qc, rq, gq_ref[...]), _rms_bwd(dkvn, kvc, rkv, gkv_ref[...]), dkr], axis=1).astype(BF16)

    return pl.pallas_call(
        body, name=name, grid=(s // ts,),
        in_specs=[_small_spec(ts), _row_spec(ts, MLA_Q_RANK), _row_spec(ts, QK_PAD),
                  _vec_spec(MLA_Q_RANK), _vec_spec(MLA_KV_RANK), _row_spec(ts, LANES), _row_spec(ts, LANES)],
        out_specs=[_row_spec(ts, ODD_SMALL_PAD), _vec_spec(MLA_Q_RANK), _vec_spec(MLA_KV_RANK)],
        out_shape=[jax.ShapeDtypeStruct((s, ODD_SMALL_PAD), BF16), jax.ShapeDtypeStruct((1, MLA_Q_RANK), F32),
                   jax.ShapeDtypeStruct((1, MLA_KV_RANK), F32)],
        compiler_params=_params("arbitrary"),
    )(proj, dqcn, dkv, gq, gkv, cos, sin_signed)


Q_HEAD_GROUP = 8
LOG2_E = 1.4426950408889634
Q_PRESCALE = ATTN_SCALE * LOG2_E


def _q_build(q_nope, q_rope_pre, wuk_hdr, cos, sin_signed, *, name):
    s = q_nope.shape[0]
    ts = _tile(s, 512)
    hg = Q_HEAD_GROUP

    def body(qn_ref, qr_ref, w_ref, c_ref, s_ref, o_ref):
        for pair in range(hg // 2):
            r = _rope(qr_ref[:, pair * LANES:(pair + 1) * LANES], c_ref[...], s_ref[...])
            for j in range(2):
                h = 2 * pair + j
                ql = _dot(qn_ref[:, h * MLA_NOPE:(h + 1) * MLA_NOPE], w_ref[h], NN)
                mine = jnp.where(_slab_lanes(r.shape, j), r, 0.0)
                o_ref[h] = (jnp.concatenate([ql, mine], axis=1) * Q_PRESCALE).astype(BF16)

    return pl.pallas_call(
        body, name=name, grid=(s // ts, MLA_HEADS // hg),
        in_specs=[pl.BlockSpec((ts, hg * MLA_NOPE), lambda i, p: (i, p)), pl.BlockSpec((ts, hg * MLA_ROPE), lambda i, p: (i, p)),
                  pl.BlockSpec((hg, MLA_NOPE, MLA_KV_RANK), lambda i, p: (p, 0, 0)),
                  pl.BlockSpec((ts, LANES), lambda i, p: (i, 0)), pl.BlockSpec((ts, LANES), lambda i, p: (i, 0))],
        out_specs=pl.BlockSpec((hg, ts, QK_PAD), lambda i, p: (p, i, 0)),
        out_shape=jax.ShapeDtypeStruct((MLA_HEADS, s, QK_PAD), BF16),
        compiler_params=_params("parallel", "parallel"),
    )(q_nope, q_rope_pre, wuk_hdr, cos, sin_signed)


def _q_bwd(dq, q_nope, wuk_hrd, cos, sin_signed, *, name):
    s = q_nope.shape[0]
    ts = _tile(s, 512)
    hg = Q_HEAD_GROUP

    nope_w, all_w = hg * MLA_NOPE, hg * (MLA_NOPE + MLA_ROPE)

    def body(dq_ref, qn_ref, w_ref, c_ref, s_ref, dall_ref, gw_ref):
        @pl.when(pl.program_id(1) == 0)
        def _():
            gw_ref[...] = jnp.zeros_like(gw_ref)

        for h in range(hg):
            dql = dq_ref[h, :, :MLA_KV_RANK]
            dall_ref[:, h * MLA_NOPE:(h + 1) * MLA_NOPE] = _dot(dql, w_ref[h], NN).astype(BF16)
            gw_ref[h] += _dot(dql, qn_ref[:, h * MLA_NOPE:(h + 1) * MLA_NOPE], TN)
        for pair in range(hg // 2):
            hi0 = dq_ref[2 * pair, :, MLA_KV_RANK:].astype(F32)
            hi1 = dq_ref[2 * pair + 1, :, MLA_KV_RANK:].astype(F32)
            d = jnp.where(_slab_lanes(hi0.shape, 0), hi0, hi1)
            dall_ref[:, nope_w + pair * LANES:nope_w + (pair + 1) * LANES] = _rope_bwd(d, c_ref[...], s_ref[...]).astype(BF16)

    return pl.pallas_call(
        body, name=name, grid=(MLA_HEADS // hg, s // ts),
        in_specs=[pl.BlockSpec((hg, ts, QK_PAD), lambda p, i: (p, i, 0)), pl.BlockSpec((ts, nope_w), lambda p, i: (i, p)),
                  pl.BlockSpec((hg, MLA_KV_RANK, MLA_NOPE), lambda p, i: (p, 0, 0)),
                  pl.BlockSpec((ts, LANES), lambda p, i: (i, 0)), pl.BlockSpec((ts, LANES), lambda p, i: (i, 0))],
        out_specs=[pl.BlockSpec((ts, all_w), lambda p, i: (i, p)),
                   pl.BlockSpec((hg, MLA_KV_RANK, MLA_NOPE), lambda p, i: (p, 0, 0))],
        out_shape=[jax.ShapeDtypeStruct((s, MLA_HEADS * (MLA_NOPE + MLA_ROPE)), BF16),
                   jax.ShapeDtypeStruct((MLA_HEADS, MLA_KV_RANK, MLA_NOPE), F32)],
        compiler_params=_params("parallel", "arbitrary"),
    )(dq, q_nope, wuk_hrd, cos, sin_signed)


ATTN_BQ = 128
ATTN_BK = 512
ATTN_BK_FWD = 1024


def _diag_mask(rows, bq, bk, q0, k0):
    qc = (q0 + lax.broadcasted_iota(jnp.int32, (rows, bk), 0) % bq) // CHUNK
    kc = (k0 + lax.broadcasted_iota(jnp.int32, (rows, bk), 1)) // CHUNK
    return kc <= qc


def _attn_fwd(q, k, *, name):
    nh, s, dk = q.shape
    bq, bk = _tile(s, ATTN_BQ), _tile(s, ATTN_BK_FWD)
    rows = nh * bq

    def body(q_ref, k_ref, o_ref, lse_ref):
        i = pl.program_id(0)
        qb = q_ref[...].reshape(rows, dk)
        n_before = (i * bq) // bk

        def step(j, width, carry, masked):
            m, l, acc = carry
            k0 = pl.multiple_of(j * bk, bk)
            kb = k_ref[pl.ds(k0, width), :]
            sc = _dot(qb, kb, NT)
            if masked:
                sc = jnp.where(_diag_mask(rows, bq, width, i * bq, k0), sc, NEG)
            m_new = jnp.maximum(m, jnp.max(sc, axis=1, keepdims=True))
            p = jnp.exp2(sc - m_new)
            a = jnp.exp2(m - m_new)
            l = a * l + jnp.sum(p, axis=1, keepdims=True)
            acc = a * acc + _dot(p.astype(BF16), kb[:, :MLA_KV_RANK], NN)
            return m_new, l, acc

        init = (jnp.full((rows, 1), NEG, F32), jnp.zeros((rows, 1), F32), jnp.zeros((rows, MLA_KV_RANK), F32))
        carry = lax.fori_loop(0, n_before, lambda j, c: step(j, bk, c, False), init)
        for part in range(bk // bq):
            @pl.when(i % (bk // bq) == part)
            def _(part=part):
                m, l, acc = step(n_before, (part + 1) * bq, carry, True)
                o_ref[...] = (acc / l).astype(BF16).reshape(nh, bq, MLA_KV_RANK)
                lse_ref[...] = jnp.broadcast_to(m + jnp.log2(l), (rows, LANES)).reshape(nh, bq, LANES)

    return pl.pallas_call(
        body, name=name, grid=(s // bq,),
        in_specs=[pl.BlockSpec((nh, bq, dk), lambda i: (0, i, 0)), pl.BlockSpec((s, dk), lambda i: (0, 0))],
        out_specs=[pl.BlockSpec((nh, bq, MLA_KV_RANK), lambda i: (0, i, 0)), pl.BlockSpec((nh, bq, LANES), lambda i: (0, i, 0))],
        out_shape=[jax.ShapeDtypeStruct((nh, s, MLA_KV_RANK), BF16), jax.ShapeDtypeStruct((nh, s, LANES), F32)],
        compiler_params=_params("parallel"),
    )(q, k)


def _attn_bwd(q, k, do, o, lse, *, name):
    nh, s, dk = q.shape
    bq, bk = _tile(s, ATTN_BQ), _tile(s, ATTN_BK)
    rows = nh * bq

    def body(q_ref, k_ref, do_ref, o_ref, lse_ref, dq_ref, dkv_ref):
        i = pl.program_id(0)
        n_before = (i * bq) // bk

        @pl.when(i == 0)
        def _():
            dkv_ref[...] = jnp.zeros_like(dkv_ref)

        qb = q_ref[...].reshape(rows, dk)
        dob = do_ref[...].reshape(rows, MLA_KV_RANK)
        lse_b = lse_ref[...].reshape(rows, LANES)[:, :1]
        delta = jnp.sum(dob.astype(F32) * o_ref[...].reshape(rows, MLA_KV_RANK).astype(F32), axis=1, keepdims=True)

        def step(j, width, dq, masked):
            j0 = pl.multiple_of(j * bk, bk)
            kb = k_ref[pl.ds(j0, width), :]
            sc = _dot(qb, kb, NT)
            if masked:
                sc = jnp.where(_diag_mask(rows, bq, width, i * bq, j0), sc, NEG)
            p = jnp.exp2(sc - lse_b)
            dp = _dot(dob, kb[:, :MLA_KV_RANK], NT)
            ds_bf = (p * (dp - delta)).astype(BF16)
            dkv_ref[pl.ds(j0, width), :] += _dot(ds_bf, qb, TN) * (1.0 / LOG2_E)
            dkv_ref[pl.ds(j0, width), :MLA_KV_RANK] += _dot(p.astype(BF16), dob, TN)
            return dq + _dot(ds_bf, kb, NN)

        dq_before = lax.fori_loop(0, n_before, lambda j, c: step(j, bk, c, False), jnp.zeros((rows, dk), F32))
        for part in range(bk // bq):
            @pl.when(i % (bk // bq) == part)
            def _(part=part):
                dq = step(n_before, (part + 1) * bq, dq_before, True) * ATTN_SCALE
                dq_ref[...] = dq.astype(BF16).reshape(nh, bq, dk)

    blk = lambda w: pl.BlockSpec((nh, bq, w), lambda i: (0, i, 0))
    return pl.pallas_call(
        body, name=name, grid=(s // bq,),
        in_specs=[blk(dk), pl.BlockSpec((s, dk), lambda i: (0, 0)), blk(MLA_KV_RANK), blk(MLA_KV_RANK), blk(LANES)],
        out_specs=[blk(dk), pl.BlockSpec((s, dk), lambda i: (0, 0))],
        out_shape=[jax.ShapeDtypeStruct((nh, s, dk), BF16), jax.ShapeDtypeStruct((s, dk), F32)],
        compiler_params=_params("arbitrary"),
    )(q, k, do, o, lse)


HEAD_GROUP = 4
SMALL_BLOCK = MLA_WIDTH // ODD_SMALL_PAD


def _small_spec(ts):
    return pl.BlockSpec((ts, ODD_SMALL_PAD), lambda i: (i, SMALL_BLOCK))


def _o_build(o_lat, wuv_hrv, proj, *, name):
    s = proj.shape[0]
    ts = _tile(s, 1024)
    w = HEAD_GROUP * MLA_V

    def body(ol_ref, w_ref, z_ref, og_ref):
        for j in range(HEAD_GROUP):
            cs = slice(j * MLA_V, (j + 1) * MLA_V)
            z = z_ref[:, cs].astype(F32)
            og_ref[:, cs] = (_dot(ol_ref[j], w_ref[j], NN) * (z * _sigmoid(z))).astype(BF16)

    return pl.pallas_call(
        body, name=name, grid=(s // ts, MLA_HEADS // HEAD_GROUP),
        in_specs=[pl.BlockSpec((HEAD_GROUP, ts, MLA_KV_RANK), lambda i, g: (g, i, 0)),
                  pl.BlockSpec((HEAD_GROUP, MLA_KV_RANK, MLA_V), lambda i, g: (g, 0, 0)),
                  pl.BlockSpec((ts, w), lambda i, g: (i, g))],
        out_specs=pl.BlockSpec((ts, w), lambda i, g: (i, g)),
        out_shape=jax.ShapeDtypeStruct((s, MLA_WIDTH), BF16),
        compiler_params=_params("parallel", "parallel"),
    )(o_lat, wuv_hrv, proj)


def _o_bwd(dg, proj, o_lat, wuv_hrv, wuv_hvr, *, name):
    s = proj.shape[0]
    ts = _tile(s, 1024)
    w = HEAD_GROUP * MLA_V

    def body(dg_ref, z_ref, ol_ref, w_ref, wt_ref, dol_ref, dz_ref, gw_ref):
        @pl.when(pl.program_id(1) == 0)
        def _():
            gw_ref[...] = jnp.zeros_like(gw_ref)

        for j in range(HEAD_GROUP):
            cs = slice(j * MLA_V, (j + 1) * MLA_V)
            z, dgj, ol = z_ref[:, cs].astype(F32), dg_ref[:, cs].astype(F32), ol_ref[j]
            sg = _sigmoid(z)
            o = _dot(ol, w_ref[j], NN)
            dz_ref[:, cs] = (dgj * o * (sg * (1.0 + z * (1.0 - sg)))).astype(BF16)
            do_bf = (dgj * (z * sg)).astype(BF16)
            dol_ref[j] = _dot(do_bf, wt_ref[j], NN).astype(BF16)
            gw_ref[j] += _dot(ol, do_bf, TN)

    hs = lambda a, b: pl.BlockSpec((HEAD_GROUP, a, b), lambda g, i: (g, 0, 0))
    return pl.pallas_call(
        body, name=name, grid=(MLA_HEADS // HEAD_GROUP, s // ts),
        in_specs=[pl.BlockSpec((ts, w), lambda g, i: (i, g)), pl.BlockSpec((ts, w), lambda g, i: (i, g)),
                  pl.BlockSpec((HEAD_GROUP, ts, MLA_KV_RANK), lambda g, i: (g, i, 0)),
                  hs(MLA_KV_RANK, MLA_V), hs(MLA_V, MLA_KV_RANK)],
        out_specs=[pl.BlockSpec((HEAD_GROUP, ts, MLA_KV_RANK), lambda g, i: (g, i, 0)),
                   pl.BlockSpec((ts, w), lambda g, i: (i, g)), hs(MLA_KV_RANK, MLA_V)],
        out_shape=[jax.ShapeDtypeStruct((MLA_HEADS, s, MLA_KV_RANK), BF16), jax.ShapeDtypeStruct((s, MLA_WIDTH), BF16),
                   jax.ShapeDtypeStruct((MLA_HEADS, MLA_KV_RANK, MLA_V), F32)],
        compiler_params=_params("parallel", "arbitrary"),
    )(dg, proj, o_lat, wuv_hrv, wuv_hvr)


def _ada_mod(c_all, ada_w, ada_b_sh, *, name):
    nl, _, cols = ada_w.shape

    def body(c_ref, w_ref, b_ref, o_ref):
        c = c_ref[...]
        cond = (c * _sigmoid(c)).astype(BF16)
        for l in range(nl):
            o_ref[l] = _dot(cond, w_ref[l].astype(BF16), NN) + b_ref[l]

    return pl.pallas_call(
        body, name=name, out_shape=jax.ShapeDtypeStruct((nl, c_all.shape[0], cols), F32),
        compiler_params=_params(),
    )(c_all, ada_w, ada_b_sh)


def _ada_grad(c_all_t, dmod_sh, *, name):
    nl, _, cols = dmod_sh.shape
    d = c_all_t.shape[0]

    def body(c_ref, dm_ref, gw_ref):
        c = c_ref[...]
        cond_t = c * _sigmoid(c)
        for l in range(nl):
            gw_ref[l] = lax.dot_general(cond_t, dm_ref[l], (NN, ((), ())), precision=lax.Precision.HIGHEST,
                                        preferred_element_type=F32)

    return pl.pallas_call(
        body, name=name, out_shape=jax.ShapeDtypeStruct((nl, d, cols), F32), compiler_params=_params(),
    )(c_all_t, dmod_sh)


def _sum_devices(parts, *, name):
    def body(p_ref, o_ref):
        acc = p_ref[0]
        for k in range(1, parts.shape[0]):
            acc = acc + p_ref[k]
        o_ref[...] = acc

    return pl.pallas_call(body, name=name, out_shape=jax.ShapeDtypeStruct(parts.shape[1:], F32), compiler_params=_params())(parts)


def _adamw_math(w, g, m, v):
    c1 = 1.0 - ADAM_B1 ** ADAM_STEP
    c2 = 1.0 - ADAM_B2 ** ADAM_STEP
    nm = ADAM_B1 * m + (1.0 - ADAM_B1) * g
    nv = ADAM_B2 * v + (1.0 - ADAM_B2) * (g * g)
    return -ADAM_LR * ((nm / c1) / (jnp.sqrt(nv / c2) + ADAM_EPS) + ADAM_WD * w), nm, nv


ADAMW_BLOCK_BYTES = 1 << 20


def _adamw(w, g, m, v, *, name, after=None):
    shape = w.shape
    a, b = shape[-2], shape[-1]
    lead = 1
    for dim in shape[:-2]:
        lead *= dim
    row_bytes = 4 * b
    if a * row_bytes <= ADAMW_BLOCK_BYTES:
        ta = a
        tl = max(1, min(lead, ADAMW_BLOCK_BYTES // (a * row_bytes)))
        while lead % tl:
            tl -= 1
    else:
        tl = 1
        ta = _tile(a, 256)
    to3 = lambda t: t.reshape(lead, a, b)

    def body(w_ref, g_ref, m_ref, v_ref, *rest):
        d_ref, nm_ref, nv_ref = rest[-3:]
        d_ref[...], nm_ref[...], nv_ref[...] = _adamw_math(w_ref[...], g_ref[...], m_ref[...], v_ref[...])

    spec = pl.BlockSpec((tl, ta, b), lambda i, j: (i, j, 0))
    out = jax.ShapeDtypeStruct((lead, a, b), F32)
    order = [] if after is None else [after]
    res = pl.pallas_call(
        body, name=name, grid=(lead // tl, a // ta), in_specs=[spec] * 4 + [pl.BlockSpec(memory_space=pl.ANY)] * len(order),
        out_specs=[spec] * 3, out_shape=[out] * 3, compiler_params=_params("parallel", "parallel"),
    )(to3(w), to3(g), to3(m), to3(v), *order)
    return [r.reshape(shape) for r in res]


def _adamw_small(ws, gs, ms, vs, *, name):
    n = len(ws)

    def body(*refs):
        for k in range(n):
            w_ref, g_ref, m_ref, v_ref = (refs[j * n + k] for j in range(4))
            d_ref, nm_ref, nv_ref = (refs[(4 + j) * n + k] for j in range(3))
            d_ref[...], nm_ref[...], nv_ref[...] = _adamw_math(w_ref[...], g_ref[...], m_ref[...], v_ref[...])

    outs = [jax.ShapeDtypeStruct(w.shape, F32) for w in ws]
    res = pl.pallas_call(body, name=name, out_shape=outs * 3, compiler_params=_params())(*ws, *gs, *ms, *vs)
    return res[:n], res[n:2 * n], res[2 * n:]


def _flip(v, bit):
    return 1 - v if bit else v


CHIP_DELTAS = ((1, 0), (0, 1), (1, 1))
SUM_ROWS = 32


def _all_gather_chips(shard, *, name):
    def body(x_ref, o_ref, send_sems, recv_sems, local_sem):
        x, y, c = lax.axis_index("x"), lax.axis_index("y"), lax.axis_index("c")
        mine = pltpu.make_async_copy(x_ref, o_ref.at[2 * x + y], local_sem)
        mine.start()

        def copy(k):
            tx, ty = _flip(x, CHIP_DELTAS[k][0]), _flip(y, CHIP_DELTAS[k][1])
            send = pltpu.make_async_remote_copy(src_ref=x_ref, dst_ref=o_ref.at[2 * x + y], send_sem=send_sems.at[k],
                                                recv_sem=recv_sems.at[k], device_id=(tx, ty, c), device_id_type=MESH)
            recv = pltpu.make_async_remote_copy(src_ref=x_ref, dst_ref=o_ref.at[2 * tx + ty], send_sem=send_sems.at[k],
                                                recv_sem=recv_sems.at[k], device_id=(tx, ty, c), device_id_type=MESH)
            return send, recv

        pairs = [copy(k) for k in range(3)]
        for send, _ in pairs:
            send.start()
        for _, recv in pairs:
            recv.wait_recv()
        for send, _ in pairs:
            send.wait_send()
        mine.wait()

    return pl.pallas_call(
        body, name=name, out_shape=jax.ShapeDtypeStruct((N_CHIPS,) + shard.shape, shard.dtype),
        in_specs=[HBM], out_specs=HBM,
        scratch_shapes=[pltpu.SemaphoreType.DMA((3,)), pltpu.SemaphoreType.DMA((3,)), pltpu.SemaphoreType.DMA(())],
    )(shard)


def _gather_weights(shards, *, name):
    n = len(shards)

    def body(*refs):
        w_refs, o_refs = refs[:n], refs[n:2 * n]
        ici_send, ici_recv, d2d_send, d2d_recv, local_sems = refs[2 * n:]
        x, y, c = lax.axis_index("x"), lax.axis_index("y"), lax.axis_index("c")
        me = 2 * x + y
        peers = [(_flip(x, dx), _flip(y, dy)) for dx, dy in CHIP_DELTAS]
        locals_ = [pltpu.make_async_copy(w_refs[k], o_refs[k].at[me], local_sems.at[k]) for k in range(n)]
        for cp in locals_:
            cp.start()

        def rows(k, which):
            half = shards[k].shape[0] // 2
            return pl.ds(pl.multiple_of(which * half, half), half)

        def over_chips(k, d, slot):
            tx, ty = peers[d]
            return pltpu.make_async_remote_copy(
                src_ref=w_refs[k].at[rows(k, c)], dst_ref=o_refs[k].at[slot, rows(k, c)], send_sem=ici_send.at[k, d],
                recv_sem=ici_recv.at[k, d], device_id=(tx, ty, c), device_id_type=MESH)

        def to_sibling(k, d, which):
            tx, ty = peers[d]
            at = o_refs[k].at[2 * tx + ty, rows(k, which)]
            return pltpu.make_async_remote_copy(src_ref=at, dst_ref=at, send_sem=d2d_send.at[k, d], recv_sem=d2d_recv.at[k, d],
                                                device_id=(x, y, 1 - c), device_id_type=MESH)

        sends = [over_chips(k, d, me) for k in range(n) for d in range(3)]
        for cp in sends:
            cp.start()
        passed = []
        for k in range(n):
            for d in range(3):
                over_chips(k, d, 2 * peers[d][0] + peers[d][1]).wait_recv()
                passed.append(to_sibling(k, d, c))
                passed[-1].start()
        for k in range(n):
            for d in range(3):
                to_sibling(k, d, 1 - c).wait_recv()
        for cp in sends + passed:
            cp.wait_send()
        for cp in locals_:
            cp.wait()

    return pl.pallas_call(
        body, name=name, out_shape=[jax.ShapeDtypeStruct((N_CHIPS,) + w.shape, w.dtype) for w in shards],
        in_specs=[HBM] * n, out_specs=[HBM] * n,
        scratch_shapes=[pltpu.SemaphoreType.DMA((n, 3))] * 4 + [pltpu.SemaphoreType.DMA((n,))],
    )(*shards)


def _add_into(dst_ref, src_ref):
    ns, r, _ = dst_ref.shape
    step = SUM_ROWS if r % SUM_ROWS == 0 else r
    for s in range(ns):
        def tile(t, carry):
            at = pl.ds(pl.multiple_of(t * step, step), step)
            dst_ref[s, at, :] = (dst_ref[s, at, :].astype(F32) + src_ref[s, at, :].astype(F32)).astype(dst_ref.dtype)
            return carry
        lax.fori_loop(0, r // step, tile, 0)


def _reduce_sibling(grads, *, name):
    n = len(grads)

    def body(*refs):
        g_refs, o_refs = refs[:n], refs[n:2 * n]
        mine, got = refs[2 * n:3 * n], refs[3 * n:4 * n]
        send_sems, recv_sems, load_sems, store_sems = refs[4 * n:]
        x, y, c = lax.axis_index("x"), lax.axis_index("y"), lax.axis_index("c")
        loads = [pltpu.make_async_copy(g_refs[k].at[:, c], mine[k], load_sems.at[k]) for k in range(n)]
        swaps = [pltpu.make_async_remote_copy(src_ref=g_refs[k].at[:, 1 - c], dst_ref=got[k], send_sem=send_sems.at[k],
                                              recv_sem=recv_sems.at[k], device_id=(x, y, 1 - c), device_id_type=MESH)
                 for k in range(n)]
        for cp in loads + swaps:
            cp.start()
        stores = []
        for k in range(n):
            loads[k].wait()
            swaps[k].wait_recv()
            _add_into(mine[k], got[k])
            stores.append(pltpu.make_async_copy(mine[k], o_refs[k], store_sems.at[k]))
            stores[-1].start()
        for k in range(n):
            swaps[k].wait_send()
            stores[k].wait()

    half = [jax.ShapeDtypeStruct((g.shape[0],) + g.shape[2:], g.dtype) for g in grads]
    return pl.pallas_call(
        body, name=name, out_shape=half, in_specs=[HBM] * n, out_specs=[HBM] * n,
        scratch_shapes=[pltpu.VMEM(h.shape, h.dtype) for h in half] * 2 + [pltpu.SemaphoreType.DMA((n,))] * 4,
        compiler_params=_params(),
    )(*grads)


def _reduce_chips(parts, landed, *, name):
    n = len(parts)

    def body(*refs):
        p_refs, l_refs, o_refs = refs[:n], refs[n:2 * n], refs[2 * n:3 * n]
        got, total = refs[3 * n:4 * n], refs[4 * n:5 * n]
        load_sems, share_send, share_recv, store_sems = refs[5 * n:]
        x, y, c = lax.axis_index("x"), lax.axis_index("y"), lax.axis_index("c")
        me = 2 * x + y
        slots = [me] + [2 * _flip(x, dx) + _flip(y, dy) for dx, dy in CHIP_DELTAS]
        loads = [[pltpu.make_async_copy((p_refs if j == 0 else l_refs)[k].at[slot], got[k].at[slot], load_sems.at[k, j])
                  for j, slot in enumerate(slots)] for k in range(n)]
        for per_array in loads:
            for cp in per_array:
                cp.start()
        shares, stores = [], []
        for k in range(n):
            for cp in loads[k]:
                cp.wait()
            r = total[k].shape[0]
            step = SUM_ROWS if r % SUM_ROWS == 0 else r

            def tile(t, carry, k=k, step=step):
                at = pl.ds(pl.multiple_of(t * step, step), step)
                acc = got[k][0, at, :].astype(F32)
                for s in range(1, N_CHIPS):
                    acc = acc + got[k][s, at, :].astype(F32)
                total[k][at, :] = acc
                return carry

            lax.fori_loop(0, r // step, tile, 0)
            stores.append(pltpu.make_async_copy(total[k], o_refs[k].at[c], store_sems.at[k]))
            shares.append(pltpu.make_async_remote_copy(
                src_ref=total[k], dst_ref=o_refs[k].at[c], send_sem=share_send.at[k], recv_sem=share_recv.at[k],
                device_id=(x, y, 1 - c), device_id_type=MESH))
            stores[-1].start()
            shares[-1].start()
        for k in range(n):
            pltpu.make_async_remote_copy(
                src_ref=total[k], dst_ref=o_refs[k].at[1 - c], send_sem=share_send.at[k], recv_sem=share_recv.at[k],
                device_id=(x, y, 1 - c), device_id_type=MESH).wait_recv()
        for cp in shares:
            cp.wait_send()
        for cp in stores:
            cp.wait()

    return pl.pallas_call(
        body, name=name, out_shape=[jax.ShapeDtypeStruct((2,) + p.shape[1:], F32) for p in parts],
        in_specs=[HBM] * (2 * n), out_specs=[HBM] * n,
        scratch_shapes=[pltpu.VMEM(p.shape, p.dtype) for p in parts] + [pltpu.VMEM(p.shape[1:], F32) for p in parts]
        + [pltpu.SemaphoreType.DMA((n, N_CHIPS))] + [pltpu.SemaphoreType.DMA((n,))] * 3,
        compiler_params=_params(),
    )(*parts, *landed)


SEM = pl.BlockSpec(memory_space=pltpu.SEMAPHORE)
IN_FLIGHT = pltpu.SideEffectType.DATAFLOW_SIDE_EFFECTING


def _chip_copies(s_refs, l_refs, sems, scatter, theirs):
    x, y, c = lax.axis_index("x"), lax.axis_index("y"), lax.axis_index("c")
    me = 2 * x + y
    copies = []
    for k in range(len(s_refs)):
        for d, (dx, dy) in enumerate(CHIP_DELTAS):
            tx, ty = _flip(x, dx), _flip(y, dy)
            peer = 2 * tx + ty
            send_sem, recv_sem = sems[2 * (3 * k + d)], sems[2 * (3 * k + d) + 1]
            copies.append(pltpu.make_async_remote_copy(
                src_ref=s_refs[k].at[peer] if scatter else s_refs[k], dst_ref=l_refs[k].at[peer if theirs else me],
                send_sem=send_sem, recv_sem=recv_sem, device_id=(tx, ty, c), device_id_type=MESH))
    return copies


def _chips_start(srcs, lands, after, *, scatter, name):
    n = len(srcs)
    n_sem = 2 * 3 * n

    def body(*refs):
        s_refs, l_refs = refs[:n], refs[n:2 * n]
        sems = refs[2 * n + 1:2 * n + 1 + n_sem]
        token = refs[-1]
        for cp in _chip_copies(s_refs, l_refs, sems, scatter, False):
            cp.start()
        token[...] = jnp.zeros_like(token)

    hbm = lambda a: pltpu.HBM(a.shape, a.dtype)
    res = pl.pallas_call(
        body, name=name,
        out_shape=(*[pltpu.SemaphoreType.DMA(())] * n_sem, *[hbm(a) for a in srcs], *[hbm(a) for a in lands],
                   jax.ShapeDtypeStruct((8, LANES), F32)),
        in_specs=[HBM] * (2 * n) + [pl.BlockSpec(memory_space=pl.ANY)],
        out_specs=(*[SEM] * n_sem, *[HBM] * (2 * n), VMEM),
        input_output_aliases={k: n_sem + k for k in range(2 * n)},
        compiler_params=pltpu.CompilerParams(has_side_effects=IN_FLIGHT),
    )(*[pltpu.with_memory_space_constraint(a, pltpu.HBM) for a in list(srcs) + list(lands)], after)
    return res[:n_sem], res[n_sem:n_sem + n], res[n_sem + n:n_sem + 2 * n], res[-1]


def _chips_wait(sems, srcs, lands, after, *, scatter, name):
    n = len(srcs)
    n_sem = len(sems)

    def body(*refs):
        s_refs, l_refs = refs[:n], refs[n:2 * n]
        sem_refs = refs[2 * n:2 * n + n_sem]
        for cp in _chip_copies(s_refs, l_refs, sem_refs, scatter, False):
            cp.wait_send()
        for cp in _chip_copies(s_refs, l_refs, sem_refs, scatter, True):
            cp.wait_recv()

    hbm = lambda a: pltpu.HBM(a.shape, a.dtype)
    res = pl.pallas_call(
        body, name=name, out_shape=tuple(hbm(a) for a in list(srcs) + list(lands)),
        in_specs=[HBM] * (2 * n) + [SEM] * n_sem + [pl.BlockSpec(memory_space=pl.ANY)], out_specs=tuple([HBM] * (2 * n)),
        input_output_aliases={k: k for k in range(2 * n)},
        compiler_params=pltpu.CompilerParams(has_side_effects=IN_FLIGHT),
    )(*srcs, *lands, *sems, after)
    return res[:n], res[n:]


def _all_gather_devices(rows, *, name, after=None):
    deltas = [(dx, dy, dc) for dx in (0, 1) for dy in (0, 1) for dc in (0, 1)][1:]
    order = [] if after is None else [after]

    def body(x_ref, *rest):
        o_ref, send_sems, recv_sems = rest[-3:]
        x, y, c = lax.axis_index("x"), lax.axis_index("y"), lax.axis_index("c")
        me = 4 * x + 2 * y + c
        o_ref[me] = x_ref[...]
        sends, recvs = [], []
        for k, (dx, dy, dc) in enumerate(deltas):
            tx, ty, tc = _flip(x, dx), _flip(y, dy), _flip(c, dc)
            sends.append(pltpu.make_async_remote_copy(src_ref=x_ref, dst_ref=o_ref.at[me], send_sem=send_sems.at[k],
                                                      recv_sem=recv_sems.at[k], device_id=(tx, ty, tc), device_id_type=MESH))
            recvs.append(pltpu.make_async_remote_copy(src_ref=x_ref, dst_ref=o_ref.at[4 * tx + 2 * ty + tc],
                                                      send_sem=send_sems.at[k], recv_sem=recv_sems.at[k],
                                                      device_id=(tx, ty, tc), device_id_type=MESH))
        for cp in sends:
            cp.start()
        for cp in recvs:
            cp.wait_recv()
        for cp in sends:
            cp.wait_send()

    return pl.pallas_call(
        body, name=name, out_shape=jax.ShapeDtypeStruct((N_DEV,) + rows.shape, rows.dtype),
        in_specs=[VMEM] + [pl.BlockSpec(memory_space=pl.ANY)] * len(order), out_specs=VMEM,
        scratch_shapes=[pltpu.SemaphoreType.DMA((N_DEV - 1,)), pltpu.SemaphoreType.DMA((N_DEV - 1,))],
    )(rows, *order)


WEIGHTS = ("ada_w", "ada_b", "ln_g", "ln_b", "e_w_in", "gmlp_norm_g", "gmlp_norm_b", "gmlp_ws", "gmlp_bs", "pool_w",
           "pool_b", "pool_scale", "e_w_out", "o_w_in", "mla_q_norm_g", "mla_kv_norm_g", "mla_w_uq", "mla_w_uk",
           "mla_w_uv", "o_w_out")
SMALL = ("ln_g", "ln_b", "gmlp_norm_g", "gmlp_norm_b", "gmlp_bs", "pool_b", "pool_scale", "mla_kv_norm_g", "mla_q_norm_g")


def _pad_cols(v, n):
    return jnp.concatenate([v, jnp.zeros((v.shape[0], n - v.shape[1]), v.dtype)], axis=1) if n > v.shape[1] else v


def _halves(g):
    return g.reshape(g.shape[0], 2, g.shape[1] // 2, g.shape[2])


def kernel(x, c, positions, ada_w, ada_b, ln_g, ln_b, e_w_in, gmlp_norm_g, gmlp_norm_b, gmlp_ws, gmlp_bs, pool_w, pool_b, pool_scale, e_w_out, o_w_in, mla_q_norm_g, mla_kv_norm_g, mla_w_uq, mla_w_uk, mla_w_uv, o_w_out, loss_target, m_ada_w, m_ada_b, m_ln_g, m_ln_b, m_e_w_in, m_gmlp_norm_g, m_gmlp_norm_b, m_gmlp_ws, m_gmlp_bs, m_pool_w, m_pool_b, m_pool_scale, m_e_w_out, m_o_w_in, m_mla_q_norm_g, m_mla_kv_norm_g, m_mla_w_uq, m_mla_w_uk, m_mla_w_uv, m_o_w_out, v_ada_w, v_ada_b, v_ln_g, v_ln_b, v_e_w_in, v_gmlp_norm_g, v_gmlp_norm_b, v_gmlp_ws, v_gmlp_bs, v_pool_w, v_pool_b, v_pool_scale, v_e_w_out, v_o_w_in, v_mla_q_norm_g, v_mla_kv_norm_g, v_mla_w_uq, v_mla_w_uk, v_mla_w_uv, v_o_w_out):
    args = dict(locals())
    weights = {n: args[n] for n in WEIGHTS}
    mom = {n: args["m_" + n] for n in WEIGHTS}
    var = {n: args["v_" + n] for n in WEIGHTS}
    ax, ay, ac = lax.axis_index("x"), lax.axis_index("y"), lax.axis_index("c")
    chip = 2 * ax + ay
    dev = 2 * chip + ac
    d = D_MODEL
    x2 = x[0]
    target = loss_target[0]
    q_rank_sh = mla_q_norm_g.shape[1]

    empty_zone = lambda w: lax.dynamic_update_slice(lax.empty((N_CHIPS,) + w.shape, w.dtype), w[None], (chip, 0, 0))
    shards0 = [w.astype(BF16) for w in (pool_w[0].reshape(-1, POOL_GROUP_DIM), e_w_out[0])]
    shards1 = [w.astype(BF16) for w in (o_w_in[0], mla_w_uq[0].reshape(q_rank_sh, -1), o_w_out[0])]
    w_in0, = _gather_weights([e_w_in[0].astype(BF16)], name="gather_weights")
    wuk_hrd = jnp.transpose(mla_w_uk[0], (1, 0, 2)).astype(BF16)
    wuk_hdr = jnp.transpose(mla_w_uk[0], (1, 2, 0)).astype(BF16)
    wuv_hrv = jnp.transpose(mla_w_uv[0], (1, 0, 2)).astype(BF16)
    wuv_hvr = jnp.transpose(mla_w_uv[0], (1, 2, 0)).astype(BF16)
    ws = gmlp_ws[0]
    ws_t = jnp.transpose(ws, (0, 2, 1))
    bs_t = _pad_cols(gmlp_bs[0].T, LANES)

    inv = 1.0 / (ROPE_THETA ** (jnp.arange(0, MLA_ROPE, 2, dtype=F32) / MLA_ROPE))
    ang = positions[0].astype(F32)[:, None] * inv
    cos_t = jnp.tile(jnp.cos(ang), (1, 4))
    sin_t = jnp.concatenate([-jnp.sin(ang), -jnp.sin(ang), jnp.sin(ang), jnp.sin(ang)], axis=1)

    c_all = _all_gather_devices(c.reshape(8, LANES), after=w_in0, name="gather_c").reshape(N_DEV, d)
    cols = ada_w.shape[2]
    ada_b_mine = lax.dynamic_slice_in_dim(ada_b, chip * cols, cols, axis=1)[:, None, :]
    mod_sh = _ada_mod(c_all, ada_w, ada_b_mine, name="ada_mod")
    q_norm_rows = jnp.zeros((8, cols), F32).at[0, :q_rank_sh].set(mla_q_norm_g[0])
    mod_all = _all_gather_chips(jnp.concatenate([mod_sh.reshape(2 * N_DEV, cols), q_norm_rows]), name="gather_mod")
    q_norm_g = mod_all[:, 2 * N_DEV, :q_rank_sh].reshape(1, -1)
    mod_all = jnp.transpose(mod_all[:, :2 * N_DEV].reshape(N_CHIPS, 2, N_DEV, cols), (1, 2, 0, 3)).reshape(2, N_DEV, 3 * d)
    mod = lax.dynamic_index_in_dim(mod_all, dev, axis=1, keepdims=False)
    shift = [mod[l:l + 1, :d] for l in range(2)]
    scale = [mod[l:l + 1, d:2 * d] for l in range(2)]
    gate = [mod[l:l + 1, 2 * d:] for l in range(2)]
    flight0 = _chips_start(shards0, [empty_zone(w) for w in shards0], mod, scatter=False, name="gather0_start")
    flight1 = _chips_start(shards1, [empty_zone(w) for w in shards1], flight0[3], scatter=False, name="gather1_start")

    scale[0] = scale[0] + flight1[3][:1, :1]
    h0 = _modulate(x2, scale[0], shift[0], name="modulate0")
    proj0 = _matmul(h0, w_in0, b_stacked=True, tm=1024, tn=1280, out_dtype=BF16, name="proj0")
    pool_w_g, w_out0 = _chips_wait(*flight0[:3], proj0, scatter=False, name="gather0_wait")[1]
    pool_w_bf = jnp.transpose(pool_w_g.reshape(N_CHIPS, POOL_GROUPS, -1, POOL_GROUP_DIM), (1, 0, 2, 3)).reshape(
        POOL_GROUPS, POOL_GROUP_DIM, POOL_GROUP_DIM)
    w_out0 = w_out0.reshape(-1, d)
    mix0 = _even_fwd(proj0, ws, bs_t, gmlp_norm_g, gmlp_norm_b, pool_w_bf, pool_b, pool_scale, name="even_fwd")
    y0, x1, h1 = _out_resid_ln(mix0, w_out0, x2, gate[0], ln_g[0:1], ln_b[0:1], scale[1], shift[1], name="out0_ln")

    w_in1_g, w_uq_g, w_out1 = _chips_wait(*flight1[:3], h1, scatter=False, name="gather1_wait")[1]
    w_out1 = w_out1.reshape(-1, d)
    w_in1 = jnp.transpose(w_in1_g, (1, 0, 2)).reshape(d, ODD_IN)
    w_in1 = jnp.concatenate([w_in1[:, ODD_SMALL:], _pad_cols(w_in1[:, :ODD_SMALL], ODD_SMALL_PAD)], axis=1)
    w_uq = w_uq_g.reshape(MLA_Q_RANK, MLA_HEADS, MLA_NOPE + MLA_ROPE)
    w_uq_nope = w_uq[:, :, :MLA_NOPE].reshape(MLA_Q_RANK, -1)
    w_uq_rope = jnp.transpose(w_uq[:, :, MLA_NOPE:].reshape(MLA_Q_RANK, MLA_HEADS // 2, 2, 2, ROPE_HALF),
                              (0, 1, 3, 2, 4)).reshape(MLA_Q_RANK, -1)
    proj1 = _matmul(h1, w_in1, tm=1024, tn=1280, out_dtype=BF16, name="proj1")
    q_cn, keys = _mla_prep(proj1, q_norm_g, mla_kv_norm_g, cos_t, sin_t, name="mla_prep")
    q_nope = _matmul(q_cn, w_uq_nope, tm=1024, tn=2048, name="q_nope", out_dtype=BF16)
    q_rope_pre = _matmul(q_cn, w_uq_rope, tm=1024, name="q_rope")
    q = _q_build(q_nope, q_rope_pre, wuk_hdr, cos_t, sin_t, name="q_build")
    o_lat, lse = _attn_fwd(q, keys, name="attn_fwd")
    og = _o_build(o_lat, wuv_hrv, proj1, name="o_build")

    dy1, dres1, g_ln_g1, g_ln_b1, dgate1, loss = _out_loss_ln_bwd(
        og, w_out1, x1, gate[1], ln_g[1:2], ln_b[1:2], target, name="out1_loss_ln")
    dg1 = _matmul(dy1, w_out1, trans_b=True, tn=2048, out_dtype=BF16, name="d_og")
    g_w_out1 = _matmul(og, dy1, trans_a=True, out_dtype=BF16, tm=1024, name="g_out1")
    do_lat, dz, g_uv = _o_bwd(dg1, proj1, o_lat, wuv_hrv, wuv_hvr, name="o_bwd")
    dq, dkeys = _attn_bwd(q, keys, do_lat, o_lat, lse, name="attn_bwd")
    dq_all, g_uk = _q_bwd(dq, q_nope, wuk_hrd, cos_t, sin_t, name="q_bwd")
    n_grp = MLA_HEADS // Q_HEAD_GROUP
    w_uq_all = jnp.concatenate([w_uq_nope.reshape(MLA_Q_RANK, n_grp, -1), w_uq_rope.reshape(MLA_Q_RANK, n_grp, -1)],
                               axis=2).reshape(MLA_Q_RANK, -1)
    dq_cn = _matmul(dq_all, w_uq_all, trans_b=True, tm=1024, name="d_qcn")
    g_uq_all = _matmul(q_cn, dq_all, trans_a=True, out_dtype=BF16, name="g_uq").reshape(MLA_Q_RANK, n_grp, -1)
    g_uq_nope = g_uq_all[:, :, :Q_HEAD_GROUP * MLA_NOPE].reshape(MLA_Q_RANK, -1)
    g_uq_rope = g_uq_all[:, :, Q_HEAD_GROUP * MLA_NOPE:].reshape(MLA_Q_RANK, -1)
    dsmall, g_qg, g_kvg = _mla_prep_bwd(proj1, dq_cn, dkeys, q_norm_g, mla_kv_norm_g, cos_t, sin_t, name="mla_prep_bwd")
    dproj1 = jnp.concatenate([dz, dsmall], axis=1)
    g_w_in1 =_matmul(h1, dproj1, trans_a=True, out_dtype=BF16, tm=1024, tn=1280, name="g_in1")

    g_uq_rope = jnp.transpose(g_uq_rope.reshape(MLA_Q_RANK, MLA_HEADS // 2, 2, 2, ROPE_HALF), (0, 1, 3, 2, 4))
    g_uq = jnp.concatenate([g_uq_nope.reshape(MLA_Q_RANK, MLA_HEADS, MLA_NOPE), g_uq_rope.reshape(MLA_Q_RANK, MLA_HEADS, MLA_ROPE)], axis=2)
    g_w_in1 = jnp.concatenate([g_w_in1[:, MLA_WIDTH:MLA_WIDTH + ODD_SMALL], g_w_in1[:, :MLA_WIDTH]], axis=1)
    g_w_in1 = jnp.transpose(g_w_in1.reshape(d, N_CHIPS, -1), (1, 0, 2))
    big1 = [
        _halves(g_w_in1),
        _halves(g_uq.reshape(N_CHIPS, q_rank_sh, -1)),
        _halves(g_w_out1.reshape(N_CHIPS, -1, d)),
        _halves(g_uk.astype(BF16).reshape(N_CHIPS, -1, MLA_NOPE)),
        _halves(g_uv.astype(BF16).reshape(N_CHIPS, -1, MLA_V)),
    ]
    parts1 = _reduce_sibling(big1, name="reduce_sibling1")
    flight2 = _chips_start(parts1, [lax.empty(p.shape, BF16) for p in parts1], loss, scatter=True, name="reduce1_start")

    gate[0] = gate[0] + flight2[3][:1, :1]
    dy0, dres0, g_ln_g0, g_ln_b0, dgate0, dscale1, dshift1 = _dh_mid_ln_bwd(
        dproj1, w_in1, x2, y0, gate[0], ln_g[0:1], ln_b[0:1], dres1, scale[1], x1, name="d_h1_mid_ln")
    dmix0 = _matmul(dy0, w_out0, trans_b=True, tn=2048, out_dtype=BF16, name="d_mix0")
    g_w_out0 = _matmul(mix0, dy0, trans_a=True, out_dtype=BF16, tm=1024, name="g_out0")
    dproj0, g_ws, g_bs_t, g_ng, g_nb, g_pw, g_pb, g_ps = _even_bwd(
        proj0, dmix0, ws, ws_t, bs_t, gmlp_norm_g, gmlp_norm_b, pool_w_bf, pool_b, pool_scale, name="even_bwd")
    g_w_in0 = _matmul(h0, dproj0, trans_a=True, out_dtype=BF16, out_stacked=True, tm=1024, tn=1280, name="g_in0")

    g_pw = jnp.transpose(g_pw.astype(BF16).reshape(POOL_GROUPS, N_CHIPS, -1, POOL_GROUP_DIM), (1, 0, 2, 3))
    big0 = [
        _halves(g_w_in0),
        _halves(g_pw.reshape(N_CHIPS, -1, POOL_GROUP_DIM)),
        _halves(g_w_out0.reshape(N_CHIPS, -1, d)),
        _halves(g_ws.astype(BF16)),
    ]
    parts0 = _reduce_sibling(big0, name="reduce_sibling0")
    parts1, landed1 = _chips_wait(*flight2[:3], parts0[0], scatter=True, name="reduce1_wait")
    flight3 = _chips_start(parts0, [lax.empty(p.shape, BF16) for p in parts0], landed1[0], scatter=True, name="reduce0_start")
    grad_x, dscale0, dshift0 = _dh_input_bwd(dproj0, w_in0, x2, dres0, scale[0], after=flight3[3], name="d_h0_input")

    small_local = {
        "ln_g": jnp.concatenate([g_ln_g0, g_ln_g1]), "ln_b": jnp.concatenate([g_ln_b0, g_ln_b1]),
        "gmlp_norm_g": g_ng, "gmlp_norm_b": g_nb, "gmlp_bs": g_bs_t[:, :GMLP_HEADS].T, "pool_b": g_pb, "pool_scale": g_ps,
        "mla_kv_norm_g": g_kvg, "mla_q_norm_g": g_qg,
    }
    n_mod = 2 * 3 * d
    vec = jnp.concatenate([dshift0, dscale0, dgate0, dshift1, dscale1, dgate1]
                          + [small_local[n].reshape(1, -1) for n in SMALL] + [loss], axis=1)
    n_vec = vec.shape[1]
    vec = _pad_cols(vec, -(-n_vec // (8 * LANES)) * 8 * LANES).reshape(-1, LANES)
    vec_all = _all_gather_devices(vec, name="gather_small")
    vec_sum = _sum_devices(vec_all, name="sum_small").reshape(-1)
    dmod_all = vec_all.reshape(N_DEV, -1)[:, :n_mod].reshape(N_DEV, 2, 3 * d)
    dmod_sh = jnp.transpose(lax.dynamic_slice_in_dim(dmod_all, chip * cols, cols, axis=2), (1, 0, 2))
    dmod_sh = jnp.concatenate([dmod_sh, jnp.zeros((2, LANES - N_DEV, cols), F32)], axis=1)
    grads = {"ada_w": _ada_grad(_pad_cols(c_all.T, LANES), dmod_sh, name="ada_grad"), "ada_b": vec_sum[:n_mod].reshape(2, 3 * d)}
    off = n_mod
    for n in SMALL:
        sz = small_local[n].size
        grads[n] = vec_sum[off:off + sz]
        off += sz
    grads["mla_q_norm_g"] = lax.dynamic_slice_in_dim(grads["mla_q_norm_g"], chip * q_rank_sh, q_rank_sh)
    for n in SMALL:
        grads[n] = grads[n].reshape(weights[n].shape)

    parts0, landed0 = _chips_wait(*flight3[:3], grads["ada_w"], scatter=True, name="reduce0_wait")
    totals = _reduce_chips(list(parts0) + list(parts1), list(landed0) + list(landed1), name="reduce_chips")
    for n, t in zip(("e_w_in", "pool_w", "e_w_out", "gmlp_ws", "o_w_in", "mla_w_uq", "o_w_out"), totals):
        if n != "gmlp_ws":
            grads[n] = t.reshape(weights[n].shape)
    rep = jnp.concatenate([t.reshape(-1, LANES) for t in (totals[3], totals[7], totals[8])])
    rep_land = lax.dynamic_update_slice(lax.empty((N_CHIPS,) + rep.shape, F32), rep[None], (chip, 0, 0))
    flight4 = _chips_start([rep], [rep_land], totals[0], scatter=False, name="gather_rep_start")

    delta, new_m, new_v = {}, {}, {}
    replicated = ("gmlp_ws", "mla_w_uk", "mla_w_uv")
    large = [n for n in WEIGHTS if n not in SMALL and n != "ada_b"]
    for n in large:
        if n not in replicated:
            delta[n], new_m[n], new_v[n] = _adamw(weights[n], grads[n], mom[n], var[n], after=flight4[3], name="adamw_" + n)
    rep = _chips_wait(*flight4[:3], delta["e_w_in"], scatter=False, name="gather_rep_wait")[1][0]
    r_ws, r_uk = GMLP_BLOCK, 4 * MLA_KV_RANK
    grads["gmlp_ws"] = rep[:, :r_ws].reshape(weights["gmlp_ws"].shape)
    grads["mla_w_uk"] = jnp.transpose(rep[:, r_ws:r_ws + r_uk].reshape(MLA_HEADS, MLA_KV_RANK, MLA_NOPE), (1, 0, 2))[None]
    grads["mla_w_uv"] = jnp.transpose(rep[:, r_ws + r_uk:].reshape(MLA_HEADS, MLA_KV_RANK, MLA_V), (1, 0, 2))[None]
    for n in replicated:
        delta[n], new_m[n], new_v[n] = _adamw(weights[n], grads[n], mom[n], var[n], name="adamw_" + n)
    small = [n for n in WEIGHTS if n not in large]
    ds, ms, vs = _adamw_small([weights[n] for n in small], [grads[n] for n in small], [mom[n] for n in small],
                              [var[n] for n in small], name="adamw_small")
    for n, dn, mn, vn in zip(small, ds, ms, vs):
        delta[n], new_m[n], new_v[n] = dn, mn, vn

    return (vec_sum[n_vec - 1], grad_x[None], *[grads[n] for n in WEIGHTS], *[delta[n] for n in WEIGHTS],
            *[new_m[n] for n in WEIGHTS], *[new_v[n] for n in WEIGHTS])
```

```python
import jax
import jax.numpy as jnp
from jax import lax
from jax.experimental import pallas as pl
from jax.experimental.pallas import tpu as pltpu

F32 = jnp.float32
BF16 = jnp.bfloat16
MESH = pl.DeviceIdType.MESH

D_MODEL = 1024
CHUNK = 64
LN_EPS = 1e-5
GMLP_HEADS = 4
GMLP_HEAD_DIM = 256
GMLP_BLOCK = 128
POOL_WINDOWS = (2, 4, 8, 16)
POOL_GROUPS = 4
POOL_GROUP_DIM = 256
POOL_HALO = 16
EVEN_IN = 5120
MLA_HEADS = 16
MLA_NOPE = 128
MLA_ROPE = 64
MLA_V = 128
MLA_Q_RANK = 256
MLA_KV_RANK = 128
MLA_WIDTH = MLA_HEADS * MLA_V
ODD_IN = 2496
ODD_SMALL = MLA_Q_RANK + MLA_KV_RANK + MLA_ROPE
ODD_SMALL_PAD = 512
QK_PAD = 256
ROPE_THETA = 10000.0
ATTN_SCALE = (MLA_NOPE + MLA_ROPE) ** -0.5
DEEPNORM_ALPHA = (2.0 * 2) ** 0.25
ADAM_LR = 0.001
ADAM_B1 = 0.9
ADAM_B2 = 0.999
ADAM_EPS = 1e-08
ADAM_WD = 0.01
ADAM_STEP = 10
NEG = -1e30
LANES = 128
N_DEV = 8
N_CHIPS = 4
VMEM_LIMIT_BYTES = 56 * 1024 * 1024
HBM = pl.BlockSpec(memory_space=pltpu.HBM)
VMEM = pl.BlockSpec(memory_space=pltpu.VMEM)


def _params(*sem):
    return pltpu.CompilerParams(dimension_semantics=sem if sem else None, vmem_limit_bytes=VMEM_LIMIT_BYTES)


def _tile(dim, pref):
    for t in (pref, 2048, 1280, 1024, 512, 256, 128):
        if t <= min(pref, dim) and dim % t == 0:
            return t
    return dim


def _sigmoid(z):
    return 1.0 / (1.0 + jnp.exp(-z))


def _dot(a, b, dims):
    return lax.dot_general(a, b, (dims, ((), ())), preferred_element_type=F32)


NN = ((1,), (0,))
NT = ((1,), (1,))
TN = ((0,), (0,))


def _matmul(a, b, *, name, trans_a=False, trans_b=False, out_dtype=F32, b_stacked=False, out_stacked=False,
            tm=512, tn=1024, tk=2048, after=None):
    k, m = a.shape if trans_a else a.shape[::-1]
    if b_stacked:
        assert not trans_b
        ns, kb, n_sh = b.shape
        n = ns * n_sh
    else:
        n, kb = b.shape if trans_b else b.shape[::-1]
    assert k == kb, (a.shape, b.shape)
    tm = _tile(m, tm)
    tn, tk = _tile(n // N_CHIPS if b_stacked or out_stacked else n, tn), _tile(k, tk)
    nk = k // tk
    per = max((n // N_CHIPS) // tn, 1)
    dims = ((0 if trans_a else 1,), (1 if trans_b else 0,))

    def body_one(a_ref, b_ref, *rest):
        o_ref = rest[-1]
        o_ref[...] = _dot(a_ref[...].astype(BF16), b_ref[...].astype(BF16), dims).astype(out_dtype)

    def body_acc(a_ref, b_ref, *rest):
        o_ref, acc_ref = rest[-2:]
        kk = pl.program_id(2)

        @pl.when(kk == 0)
        def _():
            acc_ref[...] = jnp.zeros_like(acc_ref)

        acc_ref[...] += _dot(a_ref[...].astype(BF16), b_ref[...].astype(BF16), dims)

        @pl.when(kk == nk - 1)
        def _():
            o_ref[...] = acc_ref[...].astype(out_dtype)

    a_spec = pl.BlockSpec((tk, tm), lambda i, j, kk: (kk, i)) if trans_a else pl.BlockSpec((tm, tk), lambda i, j, kk: (i, kk))
    if b_stacked:
        b_spec = pl.BlockSpec((None, tk, tn), lambda i, j, kk: (j // per, kk, j % per))
    elif trans_b:
        b_spec = pl.BlockSpec((tn, tk), lambda i, j, kk: (j, kk))
    else:
        b_spec = pl.BlockSpec((tk, tn), lambda i, j, kk: (kk, j))
    if out_stacked:
        o_spec = pl.BlockSpec((None, tm, tn), lambda i, j, kk: (j // per, i, j % per))
        o_shape = jax.ShapeDtypeStruct((N_CHIPS, m, n // N_CHIPS), out_dtype)
    else:
        o_spec = pl.BlockSpec((tm, tn), lambda i, j, kk: (i, j))
        o_shape = jax.ShapeDtypeStruct((m, n), out_dtype)
    order = [] if after is None else [after]
    return pl.pallas_call(
        body_one if nk == 1 else body_acc, name=name, grid=(m // tm, n // tn, nk),
        in_specs=[a_spec, b_spec] + [pl.BlockSpec(memory_space=pl.ANY)] * len(order),
        out_specs=o_spec, out_shape=o_shape, scratch_shapes=[] if nk == 1 else [pltpu.VMEM((tm, tn), F32)],
        compiler_params=_params("parallel", "parallel", "arbitrary"),
    )(a, b, *order)


def _matmul_rows(a, b, epilogue, row_ins, vec_ins, row_outs, vec_outs, *, name, trans_b=False, b_stacked=False,
                 tm=512, tk=2048, after=None):
    m, k = a.shape
    if b_stacked:
        ns, n, n_sh = b.shape
        assert trans_b and ns * n_sh == k
        tk = n_sh
    else:
        n = b.shape[0] if trans_b else b.shape[1]
        tk = _tile(k, tk)
    tm = _tile(m, tm)
    nk = k // tk
    dims = ((1,), (1 if trans_b else 0,))
    n_ri, n_vi, n_ro, n_vo = len(row_ins), len(vec_ins), len(row_outs), len(vec_outs)
    order = [] if after is None else [after]

    def body(*refs):
        a_ref, b_ref = refs[:2]
        pos = 2
        rin = refs[pos:pos + n_ri]
        pos += n_ri
        vin = refs[pos:pos + n_vi]
        pos += n_vi + len(order)
        rout = refs[pos:pos + n_ro]
        pos += n_ro
        vout = refs[pos:pos + n_vo]
        first = pl.program_id(0) == 0
        part = _dot(a_ref[...].astype(BF16), b_ref[...].astype(BF16), dims)
        if nk == 1:
            epilogue(part, first, rin, vin, rout, vout)
        else:
            acc_ref = refs[-1]
            kk = pl.program_id(1)

            @pl.when(kk == 0)
            def _():
                acc_ref[...] = part

            @pl.when(kk > 0)
            def _():
                acc_ref[...] += part

            @pl.when(kk == nk - 1)
            def _():
                epilogue(acc_ref[...], first, rin, vin, rout, vout)

    a_spec = pl.BlockSpec((tm, tk), lambda i, kk: (i, kk))
    if b_stacked:
        b_spec = pl.BlockSpec((None, n, tk), lambda i, kk: (kk, 0, 0))
    elif trans_b:
        b_spec = pl.BlockSpec((n, tk), lambda i, kk: (0, kk))
    else:
        b_spec = pl.BlockSpec((tk, n), lambda i, kk: (kk, 0))
    row = pl.BlockSpec((tm, n), lambda i, kk: (i, 0))
    vec = lambda w: pl.BlockSpec((1, w), lambda i, kk: (0, 0))
    return pl.pallas_call(
        body, name=name, grid=(m // tm, nk),
        in_specs=[a_spec, b_spec] + [row] * n_ri + [vec(v.shape[1]) for v in vec_ins] + [pl.BlockSpec(memory_space=pl.ANY)] * len(order),
        out_specs=[row] * n_ro + [vec(w) for w in vec_outs],
        out_shape=[jax.ShapeDtypeStruct((m, n), dt) for dt in row_outs] + [jax.ShapeDtypeStruct((1, w), F32) for w in vec_outs],
        scratch_shapes=[] if nk == 1 else [pltpu.VMEM((tm, n), F32)],
        compiler_params=_params("arbitrary", "arbitrary"),
    )(a, b, *row_ins, *vec_ins, *order)


def _row_spec(ts, d):
    return pl.BlockSpec((ts, d), lambda i: (i, 0))


def _vec_spec(d):
    return pl.BlockSpec((1, d), lambda i: (0, 0))


def _modulate(x, scale, shift, *, name):
    s, d = x.shape
    ts = _tile(s, 512)

    def body(x_ref, sc_ref, sh_ref, h_ref):
        h_ref[...] = (x_ref[...] * (1.0 + sc_ref[...]) + sh_ref[...]).astype(BF16)

    return pl.pallas_call(
        body, name=name, grid=(s // ts,), in_specs=[_row_spec(ts, d), _vec_spec(d), _vec_spec(d)],
        out_specs=_row_spec(ts, d), out_shape=jax.ShapeDtypeStruct((s, d), BF16), compiler_params=_params("parallel"),
    )(x, scale, shift)


def _ln_stats(pre):
    mu = jnp.mean(pre, axis=-1, keepdims=True)
    xc = pre - mu
    var = jnp.mean(xc * xc, axis=-1, keepdims=True)
    rstd = lax.rsqrt(var + LN_EPS)
    return xc * rstd, rstd


def _ln_bwd_rows(dout, xhat, rstd, g):
    dxh = dout * g
    m1 = jnp.mean(dxh, axis=-1, keepdims=True)
    m2 = jnp.mean(dxh * xhat, axis=-1, keepdims=True)
    return rstd * (dxh - m1 - xhat * m2)


def _colsum(v):
    return jnp.sum(v, axis=0, keepdims=True)


def _out_resid_ln(mix, w_out, x, gate, g, b, scale_next, shift_next, *, name):
    def epilogue(y, first, rin, vin, rout, vout):
        (x_ref,), (gate_ref, g_ref, b_ref, sc_ref, sh_ref), (y_ref, xn_ref, h_ref) = rin, vin, rout
        y_ref[...] = y
        pre = DEEPNORM_ALPHA * x_ref[...] + (1.0 + gate_ref[...]) * y
        xhat, _ = _ln_stats(pre)
        xn = xhat * g_ref[...] + b_ref[...]
        xn_ref[...] = xn
        h_ref[...] = (xn * (1.0 + sc_ref[...]) + sh_ref[...]).astype(BF16)

    return _matmul_rows(mix, w_out, epilogue, [x], [gate, g, b, scale_next, shift_next], [F32, F32, BF16], [], name=name)


def _out_loss_ln_bwd(og, w_out, x, gate, g, b, target, *, name):
    d = x.shape[1]

    def epilogue(yv, first, rin, vin, rout, vout):
        (x_ref, t_ref), (gate_ref, g_ref, b_ref), (dy_ref, dres_ref), (dg_ref, db_ref, dgate_ref, loss_ref) = rin, vin, rout, vout

        @pl.when(first)
        def _():
            for r in vout:
                r[...] = jnp.zeros_like(r)

        pre = DEEPNORM_ALPHA * x_ref[...] + (1.0 + gate_ref[...]) * yv
        xhat, rstd = _ln_stats(pre)
        diff = xhat * g_ref[...] + b_ref[...] - t_ref[...]
        loss_ref[...] += (0.5 / d) * jnp.sum(jnp.sum(diff * diff, axis=1, keepdims=True), axis=0, keepdims=True)
        dout = diff * (1.0 / d)
        dpre = _ln_bwd_rows(dout, xhat, rstd, g_ref[...])
        dy_ref[...] = (dpre * (1.0 + gate_ref[...])).astype(BF16)
        dres_ref[...] = DEEPNORM_ALPHA * dpre
        dg_ref[...] += _colsum(dout * xhat)
        db_ref[...] += _colsum(dout)
        dgate_ref[...] += _colsum(dpre * yv)

    return _matmul_rows(og, w_out, epilogue, [x, target], [gate, g, b], [BF16, F32], [d, d, d, 1], name=name)


def _dh_mid_ln_bwd(dproj, w_in, x, y, gate, g, b, dres_next, scale_next, x_next, *, name):
    d = x.shape[1]

    def epilogue(dh, first, rin, vin, rout, vout):
        (x_ref, y_ref, dr_ref, xn_ref), (gate_ref, g_ref, b_ref, sc_ref), (dy_ref, dres_ref) = rin, vin, rout
        dg_ref, db_ref, dgate_ref, dscale_ref, dshift_ref = vout

        @pl.when(first)
        def _():
            for r in vout:
                r[...] = jnp.zeros_like(r)

        dout = dr_ref[...] + dh * (1.0 + sc_ref[...])
        dscale_ref[...] += _colsum(dh * xn_ref[...])
        dshift_ref[...] += _colsum(dh)
        yv = y_ref[...]
        pre = DEEPNORM_ALPHA * x_ref[...] + (1.0 + gate_ref[...]) * yv
        xhat, rstd = _ln_stats(pre)
        dpre = _ln_bwd_rows(dout, xhat, rstd, g_ref[...])
        dy_ref[...] = (dpre * (1.0 + gate_ref[...])).astype(BF16)
        dres_ref[...] = DEEPNORM_ALPHA * dpre
        dg_ref[...] += _colsum(dout * xhat)
        db_ref[...] += _colsum(dout)
        dgate_ref[...] += _colsum(dpre * yv)

    return _matmul_rows(dproj, w_in, epilogue, [x, y, dres_next, x_next], [gate, g, b, scale_next], [BF16, F32], [d] * 5,
                        trans_b=True, tk=2560, name=name)


def _dh_input_bwd(dproj, w_in_stacked, x, dres, scale, *, name, after):
    d = x.shape[1]

    def epilogue(dh, first, rin, vin, rout, vout):
        (x_ref, dr_ref), (sc_ref,), (dx_ref,), (dscale_ref, dshift_ref) = rin, vin, rout, vout

        @pl.when(first)
        def _():
            for r in vout:
                r[...] = jnp.zeros_like(r)

        dx_ref[...] = dr_ref[...] + dh * (1.0 + sc_ref[...])
        dscale_ref[...] += _colsum(dh * x_ref[...])
        dshift_ref[...] += _colsum(dh)

    return _matmul_rows(dproj, w_in_stacked, epilogue, [x, dres], [scale], [F32], [d, d], trans_b=True, b_stacked=True,
                        tm=1024, after=after, name=name)


def _chunk_mask(transposed=False):
    r = lax.broadcasted_iota(jnp.int32, (GMLP_BLOCK, GMLP_BLOCK), 0) // CHUNK
    c = lax.broadcasted_iota(jnp.int32, (GMLP_BLOCK, GMLP_BLOCK), 1) // CHUNK
    return (r <= c) if transposed else (c <= r)


def _window_sum(ext, steps, forward):
    rows = ext.shape[0]
    acc = ext
    for k in range(steps):
        shift = 1 << k
        acc = acc + pltpu.roll(acc, (rows - shift) if forward else shift, 0)
    return acc


def _pool_counts(first_row, rows, win):
    t = first_row + lax.broadcasted_iota(jnp.int32, (rows, 1), 0)
    return jnp.minimum(t + 1, win).astype(F32)


def _even_specs(t):
    col = lambda j: pl.BlockSpec((t, D_MODEL), lambda n: (n, j))
    per = t // POOL_HALO
    prev = pl.BlockSpec((POOL_HALO, D_MODEL), lambda n: (jnp.maximum(n * per - 1, 0), 3))
    return col, per, prev


def _full(shape):
    return pl.BlockSpec(shape, lambda n: (0,) * len(shape))


def _gmlp_head(v_h, ng, nb, w_bf):
    xhat, rstd = _ln_stats(v_h)
    vn = (xhat * ng + nb).astype(BF16)
    return xhat, rstd, vn, _dot(w_bf, vn, NN)


def _pool_group(xb_g, prev_g, first_row, grp):
    t = xb_g.shape[0]
    ext = jnp.concatenate([prev_g, xb_g], axis=0)
    tot = _window_sum(ext, grp + 1, False)[POOL_HALO:, :]
    cnt = _pool_counts(first_row, t, POOL_WINDOWS[grp])
    return tot / cnt - xb_g, cnt


def _even_fwd(proj, ws, bs_t, ng, nb, pool_w, pool_b, pool_scale, *, name):
    s = proj.shape[0]
    t = GMLP_BLOCK
    col, per, prev = _even_specs(t)

    def body(u_ref, v_ref, za_ref, xb_ref, zb_ref, xp_ref, ws_ref, bs_ref, ng_ref, nb_ref, pw_ref, pb_ref, ps_ref, o_ref):
        n = pl.program_id(0)
        mask = _chunk_mask()
        for h in range(GMLP_HEADS):
            c0 = h * GMLP_HEAD_DIM
            cs = slice(c0, c0 + GMLP_HEAD_DIM)
            w_bf = jnp.where(mask, ws_ref[h], 0.0).astype(BF16)
            _, _, _, sv = _gmlp_head(v_ref[:, cs].astype(F32),ng_ref[...], nb_ref[...], w_bf)
            sv = sv + bs_ref[:, h:h + 1]
            za = za_ref[:, cs].astype(F32)
            o_ref[:, cs] = (u_ref[:, cs].astype(F32) * sv * (za * _sigmoid(za))).astype(BF16)
        live = (n > 0).astype(F32)
        for grp in range(POOL_GROUPS):
            c0 = grp * POOL_GROUP_DIM
            cs = slice(c0, c0 + POOL_GROUP_DIM)
            pooled, _ = _pool_group(xb_ref[:, cs].astype(F32), xp_ref[:, cs].astype(F32) * live, n * t, grp)
            yb = _dot(pooled.astype(BF16), pw_ref[grp], NN) + pb_ref[:, cs]
            zb = zb_ref[:, cs].astype(F32)
            o_ref[:, D_MODEL + c0:D_MODEL + c0 + POOL_GROUP_DIM] = (yb * ps_ref[:, cs] * (zb * _sigmoid(zb))).astype(BF16)

    return pl.pallas_call(
        body, name=name, grid=(s // t,),
        in_specs=[col(0), col(1), col(2), col(3), col(4), prev,
                  _full((GMLP_HEADS, t, t)), _full((t, LANES)), _full((1, GMLP_HEAD_DIM)), _full((1, GMLP_HEAD_DIM)),
                  _full((POOL_GROUPS, POOL_GROUP_DIM, POOL_GROUP_DIM)), _full((1, D_MODEL)), _full((1, D_MODEL))],
        out_specs=pl.BlockSpec((t, 2 * D_MODEL), lambda n: (n, 0)),
        out_shape=jax.ShapeDtypeStruct((s, 2 * D_MODEL), BF16),
        compiler_params=_params("parallel"),
    )(proj, proj, proj, proj, proj, proj, ws, bs_t, ng, nb, pool_w, pool_b, pool_scale)


def _even_bwd(proj, dmix, ws, ws_t, bs_t, ng, nb, pool_w, pool_b, pool_scale, *, name):
    s = proj.shape[0]
    t = GMLP_BLOCK
    nblk = s // t
    col, per, prev = _even_specs(t)
    nxt = lambda j: pl.BlockSpec((POOL_HALO, D_MODEL), lambda n: (jnp.minimum((n + 1) * per, nblk * per - 1), j))

    def body(u_ref, v_ref, za_ref, xb_ref, zb_ref, xp_ref, zn_ref, da_ref, db_ref, dbn_ref,
             ws_ref, wst_ref, bs_ref, ng_ref, nb_ref, pw_ref, pb_ref, ps_ref,
             dp_ref, gws_ref, gbs_ref, gng_ref, gnb_ref, gpw_ref, gpb_ref, gps_ref):
        n = pl.program_id(0)

        @pl.when(n == 0)
        def _():
            for r in (gws_ref, gbs_ref, gng_ref, gnb_ref, gpw_ref, gpb_ref, gps_ref):
                r[...] = jnp.zeros_like(r)

        mask, mask_t = _chunk_mask(), _chunk_mask(True)
        lane = lax.broadcasted_iota(jnp.int32, (t, LANES), 1)
        ngv, nbv = ng_ref[...], nb_ref[...]
        for h in range(GMLP_HEADS):
            c0 = h * GMLP_HEAD_DIM
            cs = slice(c0, c0 + GMLP_HEAD_DIM)
            w_bf = jnp.where(mask, ws_ref[h], 0.0).astype(BF16)
            wt_bf = jnp.where(mask_t, wst_ref[h], 0.0).astype(BF16)
            xhat, rstd, vn, sv = _gmlp_head(v_ref[:, cs].astype(F32),ngv, nbv, w_bf)
            sv = sv + bs_ref[:, h:h + 1]
            za, u, da = za_ref[:, cs].astype(F32), u_ref[:, cs].astype(F32), da_ref[:, cs].astype(F32)
            sg = _sigmoid(za)
            sl = za * sg
            dp_ref[:, cs] = (da * sv * sl).astype(BF16)
            dp_ref[:, 2 * D_MODEL + c0:2 * D_MODEL + c0 + GMLP_HEAD_DIM] = (
                da * u * sv * (sg * (1.0 + za * (1.0 - sg)))).astype(BF16)
            dsv = da * u * sl
            gbs_ref[...] += jnp.where(lane == h, jnp.sum(dsv, axis=1, keepdims=True), 0.0)
            dsv_bf = dsv.astype(BF16)
            gws_ref[h] += jnp.where(mask, _dot(dsv_bf, vn, NT), 0.0)
            dvn = _dot(wt_bf, dsv_bf, NN)
            dp_ref[:, D_MODEL + c0:D_MODEL + c0 + GMLP_HEAD_DIM] = _ln_bwd_rows(dvn, xhat, rstd, ngv).astype(BF16)
            gng_ref[...] += _colsum(dvn * xhat)
            gnb_ref[...] += _colsum(dvn)
        live_prev = (n > 0).astype(F32)
        live_next = (n < nblk - 1).astype(F32)
        for grp in range(POOL_GROUPS):
            c0 = grp * POOL_GROUP_DIM
            cs = slice(c0, c0 + POOL_GROUP_DIM)
            xb = xb_ref[:, cs].astype(F32)
            pooled, cnt = _pool_group(xb, xp_ref[:, cs].astype(F32) * live_prev, n * t, grp)
            pooled_bf = pooled.astype(BF16)
            pw = pw_ref[grp]
            yb = _dot(pooled_bf, pw, NN) + pb_ref[:, cs]
            ps = ps_ref[:, cs]
            zb, db = zb_ref[:, cs].astype(F32), db_ref[:, cs].astype(F32)
            sg = _sigmoid(zb)
            sl = zb * sg
            dp_ref[:, 4 * D_MODEL + c0:4 * D_MODEL + c0 + POOL_GROUP_DIM] = (
                db * yb * ps * (sg * (1.0 + zb * (1.0 - sg)))).astype(BF16)
            dsl = db * sl
            dy = dsl * ps
            gps_ref[:, cs] += _colsum(dsl * yb)
            gpb_ref[:, cs] += _colsum(dy)
            dy_bf = dy.astype(BF16)
            gpw_ref[grp] += _dot(pooled_bf, dy_bf, TN)
            r = _dot(dy_bf, pw, NT)
            zn = zn_ref[:, cs].astype(F32)
            dyn = (dbn_ref[:, cs].astype(F32) * (zn * _sigmoid(zn)) * ps * live_next).astype(BF16)
            rn = _dot(dyn, pw, NT) / _pool_counts((n + 1) * t, POOL_HALO, POOL_WINDOWS[grp])
            ext = jnp.concatenate([r / cnt, rn], axis=0)
            dxb = _window_sum(ext, grp + 1, True)[:t, :] - r
            dp_ref[:, 3 * D_MODEL + c0:3 * D_MODEL + c0 + POOL_GROUP_DIM] = dxb.astype(BF16)

    out_shape = [
        jax.ShapeDtypeStruct((s, EVEN_IN), BF16),
        jax.ShapeDtypeStruct((GMLP_HEADS, t, t), F32), jax.ShapeDtypeStruct((t, LANES), F32),
        jax.ShapeDtypeStruct((1, GMLP_HEAD_DIM), F32), jax.ShapeDtypeStruct((1, GMLP_HEAD_DIM), F32),
        jax.ShapeDtypeStruct((POOL_GROUPS, POOL_GROUP_DIM, POOL_GROUP_DIM), F32),
        jax.ShapeDtypeStruct((1, D_MODEL), F32), jax.ShapeDtypeStruct((1, D_MODEL), F32),
    ]
    return pl.pallas_call(
        body, name=name, grid=(nblk,),
        in_specs=[col(0), col(1), col(2), col(3), col(4), prev, nxt(4),
                  pl.BlockSpec((t, D_MODEL), lambda n: (n, 0)), pl.BlockSpec((t, D_MODEL), lambda n: (n, 1)), nxt(1),
                  _full((GMLP_HEADS, t, t)), _full((GMLP_HEADS, t, t)), _full((t, LANES)),
                  _full((1, GMLP_HEAD_DIM)), _full((1, GMLP_HEAD_DIM)),
                  _full((POOL_GROUPS, POOL_GROUP_DIM, POOL_GROUP_DIM)), _full((1, D_MODEL)), _full((1, D_MODEL))],
        out_specs=[pl.BlockSpec((t, EVEN_IN), lambda n: (n, 0))] + [_full(o.shape) for o in out_shape[1:]],
        out_shape=out_shape,
        compiler_params=_params("arbitrary"),
    )(proj, proj, proj, proj, proj, proj, proj, dmix, dmix, dmix, ws, ws_t, bs_t, ng, nb, pool_w, pool_b, pool_scale)


ROPE_HALF = MLA_ROPE // 2


def _rope(v, cos, sin_signed):
    return v * cos + pltpu.roll(v, 2 * ROPE_HALF, 1) * sin_signed


def _rope_bwd(d, cos, sin_signed):
    return d * cos + pltpu.roll(d * sin_signed, 2 * ROPE_HALF, 1)


def _slab_lanes(shape, which):
    lane = lax.broadcasted_iota(jnp.int32, shape, 1)
    return (lane // ROPE_HALF) % 2 == which


def _rms(v, g):
    r = lax.rsqrt(jnp.mean(v * v, axis=-1, keepdims=True) + LN_EPS)
    return v * r * g, r


def _rms_bwd(dy, v, r, g):
    u = dy * g
    return r * u - v * (r * r * r) * jnp.mean(u * v, axis=-1, keepdims=True)


def _mla_prep(proj, gq, gkv, cos, sin_signed, *, name):
    s = proj.shape[0]
    ts = _tile(s, 512)

    def body(p_ref, gq_ref, gkv_ref, c_ref, s_ref, q_ref, k_ref):
        qcn, _ = _rms(p_ref[:, :MLA_Q_RANK].astype(F32), gq_ref[...])
        kvn, _ = _rms(p_ref[:, MLA_Q_RANK:MLA_Q_RANK + MLA_KV_RANK].astype(F32), gkv_ref[...])
        kr = p_ref[:, MLA_Q_RANK + MLA_KV_RANK:].astype(F32)
        lane = lax.broadcasted_iota(jnp.int32, kr.shape, 1)
        by1, by2 = pltpu.roll(kr, ROPE_HALF, 1), pltpu.roll(kr, 2 * ROPE_HALF, 1)
        both = jnp.where(lane < ROPE_HALF, kr, jnp.where(lane < 3 * ROPE_HALF, by1, by2))
        kr = _rope(both, c_ref[...], s_ref[...])
        q_ref[...] = qcn.astype(BF16)
        k_ref[...] = jnp.concatenate([kvn, kr], axis=1).astype(BF16)

    return pl.pallas_call(
        body, name=name, grid=(s // ts,),
        in_specs=[_small_spec(ts), _vec_spec(MLA_Q_RANK), _vec_spec(MLA_KV_RANK), _row_spec(ts, LANES), _row_spec(ts, LANES)],
        out_specs=[_row_spec(ts, MLA_Q_RANK), _row_spec(ts, QK_PAD)],
        out_shape=[jax.ShapeDtypeStruct((s, MLA_Q_RANK), BF16), jax.ShapeDtypeStruct((s, QK_PAD), BF16)],
        compiler_params=_params("parallel"),
    )(proj, gq, gkv, cos, sin_signed)


def _mla_prep_bwd(proj, dqcn, dkv, gq, gkv, cos, sin_signed, *, name):
    s = proj.shape[0]
    ts = _tile(s, 512)

    def body(p_ref, dq_ref, dkv_ref, gq_ref, gkv_ref, c_ref, s_ref, ds_ref, ggq_ref, ggkv_ref):
        @pl.when(pl.program_id(0) == 0)
        def _():
            ggq_ref[...] = jnp.zeros_like(ggq_ref)
            ggkv_ref[...] = jnp.zeros_like(ggkv_ref)

        qc = p_ref[:, :MLA_Q_RANK].astype(F32)
        kvc = p_ref[:, MLA_Q_RANK:MLA_Q_RANK + MLA_KV_RANK].astype(F32)
        _, rq = _rms(qc, gq_ref[...])
        _, rkv = _rms(kvc, gkv_ref[...])
        dq = dq_ref[...]
        dkvn = dkv_ref[:, :MLA_KV_RANK]
        ggq_ref[...] += _colsum(dq * qc * rq)
        ggkv_ref[...] += _colsum(dkvn * kvc * rkv)
        dboth = _rope_bwd(dkv_ref[:, MLA_KV_RANK:], c_ref[...], s_ref[...])
        lane = lax.broadcasted_iota(jnp.int32, dboth.shape, 1)
        pair = dboth + pltpu.roll(dboth, 3 * ROPE_HALF, 1)
        dkr = jnp.where(lane < ROPE_HALF, pair, jnp.where(lane < 2 * ROPE_HALF, pltpu.roll(pair, 3 * ROPE_HALF, 1), 0.0))
        ds_ref[...] = jnp.concatenate(
            [_rms_bwd(dq, qc, rq, gq_ref[...]), _rms_bwd(dkvn, kvc, rkv, gkv_ref[...]), dkr], axis=1).astype(BF16)

    return pl.pallas_call(
        body, name=name, grid=(s // ts,),
        in_specs=[_small_spec(ts), _row_spec(ts, MLA_Q_RANK), _row_spec(ts, QK_PAD),
                  _vec_spec(MLA_Q_RANK), _vec_spec(MLA_KV_RANK), _row_spec(ts, LANES), _row_spec(ts, LANES)],
        out_specs=[_row_spec(ts, ODD_SMALL_PAD), _vec_spec(MLA_Q_RANK), _vec_spec(MLA_KV_RANK)],
        out_shape=[jax.ShapeDtypeStruct((s, ODD_SMALL_PAD), BF16), jax.ShapeDtypeStruct((1, MLA_Q_RANK), F32),
                   jax.ShapeDtypeStruct((1, MLA_KV_RANK), F32)],
        compiler_params=_params("arbitrary"),
    )(proj, dqcn, dkv, gq, gkv, cos, sin_signed)


Q_HEAD_GROUP = 8
LOG2_E = 1.4426950408889634
Q_PRESCALE = ATTN_SCALE * LOG2_E


def _q_build(q_nope, q_rope_pre, wuk_hdr, cos, sin_signed, *, name):
    s = q_nope.shape[0]
    ts = _tile(s, 512)
    hg = Q_HEAD_GROUP

    def body(qn_ref, qr_ref, w_ref, c_ref, s_ref, o_ref):
        for pair in range(hg // 2):
            r = _rope(qr_ref[:, pair * LANES:(pair + 1) * LANES], c_ref[...], s_ref[...])
            for j in range(2):
                h = 2 * pair + j
                ql = _dot(qn_ref[:, h * MLA_NOPE:(h + 1) * MLA_NOPE], w_ref[h], NN)
                mine = jnp.where(_slab_lanes(r.shape, j), r, 0.0)
                o_ref[h] = (jnp.concatenate([ql, mine], axis=1) * Q_PRESCALE).astype(BF16)

    return pl.pallas_call(
        body, name=name, grid=(s // ts, MLA_HEADS // hg),
        in_specs=[pl.BlockSpec((ts, hg * MLA_NOPE), lambda i, p: (i, p)), pl.BlockSpec((ts, hg * MLA_ROPE), lambda i, p: (i, p)),
                  pl.BlockSpec((hg, MLA_NOPE, MLA_KV_RANK), lambda i, p: (p, 0, 0)),
                  pl.BlockSpec((ts, LANES), lambda i, p: (i, 0)), pl.BlockSpec((ts, LANES), lambda i, p: (i, 0))],
        out_specs=pl.BlockSpec((hg, ts, QK_PAD), lambda i, p: (p, i, 0)),
        out_shape=jax.ShapeDtypeStruct((MLA_HEADS, s, QK_PAD), BF16),
        compiler_params=_params("parallel", "parallel"),
    )(q_nope, q_rope_pre, wuk_hdr, cos, sin_signed)


def _q_bwd(dq, q_nope, wuk_hrd, cos, sin_signed, *, name):
    s = q_nope.shape[0]
    ts = _tile(s, 512)
    hg = Q_HEAD_GROUP

    nope_w, all_w = hg * MLA_NOPE, hg * (MLA_NOPE + MLA_ROPE)

    def body(dq_ref, qn_ref, w_ref, c_ref, s_ref, dall_ref, gw_ref):
        @pl.when(pl.program_id(1) == 0)
        def _():
            gw_ref[...] = jnp.zeros_like(gw_ref)

        for h in range(hg):
            dql = dq_ref[h, :, :MLA_KV_RANK]
            dall_ref[:, h * MLA_NOPE:(h + 1) * MLA_NOPE] = _dot(dql, w_ref[h], NN).astype(BF16)
            gw_ref[h] += _dot(dql, qn_ref[:, h * MLA_NOPE:(h + 1) * MLA_NOPE], TN)
        for pair in range(hg // 2):
            hi0 = dq_ref[2 * pair, :, MLA_KV_RANK:].astype(F32)
            hi1 = dq_ref[2 * pair + 1, :, MLA_KV_RANK:].astype(F32)
            d = jnp.where(_slab_lanes(hi0.shape, 0), hi0, hi1)
            dall_ref[:, nope_w + pair * LANES:nope_w + (pair + 1) * LANES] = _rope_bwd(d, c_ref[...], s_ref[...]).astype(BF16)

    return pl.pallas_call(
        body, name=name, grid=(MLA_HEADS // hg, s // ts),
        in_specs=[pl.BlockSpec((hg, ts, QK_PAD), lambda p, i: (p, i, 0)), pl.BlockSpec((ts, nope_w), lambda p, i: (i, p)),
                  pl.BlockSpec((hg, MLA_KV_RANK, MLA_NOPE), lambda p, i: (p, 0, 0)),
                  pl.BlockSpec((ts, LANES), lambda p, i: (i, 0)), pl.BlockSpec((ts, LANES), lambda p, i: (i, 0))],
        out_specs=[pl.BlockSpec((ts, all_w), lambda p, i: (i, p)),
                   pl.BlockSpec((hg, MLA_KV_RANK, MLA_NOPE), lambda p, i: (p, 0, 0))],
        out_shape=[jax.ShapeDtypeStruct((s, MLA_HEADS * (MLA_NOPE + MLA_ROPE)), BF16),
                   jax.ShapeDtypeStruct((MLA_HEADS, MLA_KV_RANK, MLA_NOPE), F32)],
        compiler_params=_params("parallel", "arbitrary"),
    )(dq, q_nope, wuk_hrd, cos, sin_signed)


ATTN_BQ = 128
ATTN_BK = 512
ATTN_BK_FWD = 1024


def _diag_mask(rows, bq, bk, q0, k0):
    qc = (q0 + lax.broadcasted_iota(jnp.int32, (rows, bk), 0) % bq) // CHUNK
    kc = (k0 + lax.broadcasted_iota(jnp.int32, (rows, bk), 1)) // CHUNK
    return kc <= qc


def _attn_fwd(q, k, *, name):
    nh, s, dk = q.shape
    bq, bk = _tile(s, ATTN_BQ), _tile(s, ATTN_BK_FWD)
    rows = nh * bq

    def body(q_ref, k_ref, o_ref, lse_ref):
        i = pl.program_id(0)
        qb = q_ref[...].reshape(rows, dk)
        n_before = (i * bq) // bk

        def step(j, width, carry, masked):
            m, l, acc = carry
            k0 = pl.multiple_of(j * bk, bk)
            kb = k_ref[pl.ds(k0, width), :]
            sc = _dot(qb, kb, NT)
            if masked:
                sc = jnp.where(_diag_mask(rows, bq, width, i * bq, k0), sc, NEG)
            m_new = jnp.maximum(m, jnp.max(sc, axis=1, keepdims=True))
            p = jnp.exp2(sc - m_new)
            a = jnp.exp2(m - m_new)
            l = a * l + jnp.sum(p, axis=1, keepdims=True)
            acc = a * acc + _dot(p.astype(BF16), kb[:, :MLA_KV_RANK], NN)
            return m_new, l, acc

        init = (jnp.full((rows, 1), NEG, F32), jnp.zeros((rows, 1), F32), jnp.zeros((rows, MLA_KV_RANK), F32))
        carry = lax.fori_loop(0, n_before, lambda j, c: step(j, bk, c, False), init)
        for part in range(bk // bq):
            @pl.when(i % (bk // bq) == part)
            def _(part=part):
                m, l, acc = step(n_before, (part + 1) * bq, carry, True)
                o_ref[...] = (acc / l).astype(BF16).reshape(nh, bq, MLA_KV_RANK)
                lse_ref[...] = jnp.broadcast_to(m + jnp.log2(l), (rows, LANES)).reshape(nh, bq, LANES)

    return pl.pallas_call(
        body, name=name, grid=(s // bq,),
        in_specs=[pl.BlockSpec((nh, bq, dk), lambda i: (0, i, 0)), pl.BlockSpec((s, dk), lambda i: (0, 0))],
        out_specs=[pl.BlockSpec((nh, bq, MLA_KV_RANK), lambda i: (0, i, 0)), pl.BlockSpec((nh, bq, LANES), lambda i: (0, i, 0))],
        out_shape=[jax.ShapeDtypeStruct((nh, s, MLA_KV_RANK), BF16), jax.ShapeDtypeStruct((nh, s, LANES), F32)],
        compiler_params=_params("parallel"),
    )(q, k)


def _attn_bwd(q, k, do, o, lse, *, name):
    nh, s, dk = q.shape
    bq, bk = _tile(s, ATTN_BQ), _tile(s, ATTN_BK)
    rows = nh * bq

    def body(q_ref, k_ref, do_ref, o_ref, lse_ref, dq_ref, dkv_ref):
        i = pl.program_id(0)
        n_before = (i * bq) // bk

        @pl.when(i == 0)
        def _():
            dkv_ref[...] = jnp.zeros_like(dkv_ref)

        qb = q_ref[...].reshape(rows, dk)
        dob = do_ref[...].reshape(rows, MLA_KV_RANK)
        lse_b = lse_ref[...].reshape(rows, LANES)[:, :1]
        delta = jnp.sum(dob.astype(F32) * o_ref[...].reshape(rows, MLA_KV_RANK).astype(F32), axis=1, keepdims=True)

        def step(j, width, dq, masked):
            j0 = pl.multiple_of(j * bk, bk)
            kb = k_ref[pl.ds(j0, width), :]
            sc = _dot(qb, kb, NT)
            if masked:
                sc = jnp.where(_diag_mask(rows, bq, width, i * bq, j0), sc, NEG)
            p = jnp.exp2(sc - lse_b)
            dp = _dot(dob, kb[:, :MLA_KV_RANK], NT)
            ds_bf = (p * (dp - delta)).astype(BF16)
            dkv_ref[pl.ds(j0, width), :] += _dot(ds_bf, qb, TN) * (1.0 / LOG2_E)
            dkv_ref[pl.ds(j0, width), :MLA_KV_RANK] += _dot(p.astype(BF16), dob, TN)
            return dq + _dot(ds_bf, kb, NN)

        dq_before = lax.fori_loop(0, n_before, lambda j, c: step(j, bk, c, False), jnp.zeros((rows, dk), F32))
        for part in range(bk // bq):
            @pl.when(i % (bk // bq) == part)
            def _(part=part):
                dq = step(n_before, (part + 1) * bq, dq_before, True) * ATTN_SCALE
                dq_ref[...] = dq.astype(BF16).reshape(nh, bq, dk)

    blk = lambda w: pl.BlockSpec((nh, bq, w), lambda i: (0, i, 0))
    return pl.pallas_call(
        body, name=name, grid=(s // bq,),
        in_specs=[blk(dk), pl.BlockSpec((s, dk), lambda i: (0, 0)), blk(MLA_KV_RANK), blk(MLA_KV_RANK), blk(LANES)],
        out_specs=[blk(dk), pl.BlockSpec((s, dk), lambda i: (0, 0))],
        out_shape=[jax.ShapeDtypeStruct((nh, s, dk), BF16), jax.ShapeDtypeStruct((s, dk), F32)],
        compiler_params=_params("arbitrary"),
    )(q, k, do, o, lse)


HEAD_GROUP = 4
SMALL_BLOCK = MLA_WIDTH // ODD_SMALL_PAD


def _small_spec(ts):
    return pl.BlockSpec((ts, ODD_SMALL_PAD), lambda i: (i, SMALL_BLOCK))


def _o_build(o_lat, wuv_hrv, proj, *, name):
    s = proj.shape[0]
    ts = _tile(s, 1024)
    w = HEAD_GROUP * MLA_V

    def body(ol_ref, w_ref, z_ref, og_ref):
        for j in range(HEAD_GROUP):
            cs = slice(j * MLA_V, (j + 1) * MLA_V)
            z = z_ref[:, cs].astype(F32)
            og_ref[:, cs] = (_dot(ol_ref[j], w_ref[j], NN) * (z * _sigmoid(z))).astype(BF16)

    return pl.pallas_call(
        body, name=name, grid=(s // ts, MLA_HEADS // HEAD_GROUP),
        in_specs=[pl.BlockSpec((HEAD_GROUP, ts, MLA_KV_RANK), lambda i, g: (g, i, 0)),
                  pl.BlockSpec((HEAD_GROUP, MLA_KV_RANK, MLA_V), lambda i, g: (g, 0, 0)),
                  pl.BlockSpec((ts, w), lambda i, g: (i, g))],
        out_specs=pl.BlockSpec((ts, w), lambda i, g: (i, g)),
        out_shape=jax.ShapeDtypeStruct((s, MLA_WIDTH), BF16),
        compiler_params=_params("parallel", "parallel"),
    )(o_lat, wuv_hrv, proj)


def _o_bwd(dg, proj, o_lat, wuv_hrv, wuv_hvr, *, name):
    s = proj.shape[0]
    ts = _tile(s, 1024)
    w = HEAD_GROUP * MLA_V

    def body(dg_ref, z_ref, ol_ref, w_ref, wt_ref, dol_ref, dz_ref, gw_ref):
        @pl.when(pl.program_id(1) == 0)
        def _():
            gw_ref[...] = jnp.zeros_like(gw_ref)

        for j in range(HEAD_GROUP):
            cs = slice(j * MLA_V, (j + 1) * MLA_V)
            z, dgj, ol = z_ref[:, cs].astype(F32), dg_ref[:, cs].astype(F32), ol_ref[j]
            sg = _sigmoid(z)
            o = _dot(ol, w_ref[j], NN)
            dz_ref[:, cs] = (dgj * o * (sg * (1.0 + z * (1.0 - sg)))).astype(BF16)
            do_bf = (dgj * (z * sg)).astype(BF16)
            dol_ref[j] = _dot(do_bf, wt_ref[j], NN).astype(BF16)
            gw_ref[j] += _dot(ol, do_bf, TN)

    hs = lambda a, b: pl.BlockSpec((HEAD_GROUP, a, b), lambda g, i: (g, 0, 0))
    return pl.pallas_call(
        body, name=name, grid=(MLA_HEADS // HEAD_GROUP, s // ts),
        in_specs=[pl.BlockSpec((ts, w), lambda g, i: (i, g)), pl.BlockSpec((ts, w), lambda g, i: (i, g)),
                  pl.BlockSpec((HEAD_GROUP, ts, MLA_KV_RANK), lambda g, i: (g, i, 0)),
                  hs(MLA_KV_RANK, MLA_V), hs(MLA_V, MLA_KV_RANK)],
        out_specs=[pl.BlockSpec((HEAD_GROUP, ts, MLA_KV_RANK), lambda g, i: (g, i, 0)),
                   pl.BlockSpec((ts, w), lambda g, i: (i, g)), hs(MLA_KV_RANK, MLA_V)],
        out_shape=[jax.ShapeDtypeStruct((MLA_HEADS, s, MLA_KV_RANK), BF16), jax.ShapeDtypeStruct((s, MLA_WIDTH), BF16),
                   jax.ShapeDtypeStruct((MLA_HEADS, MLA_KV_RANK, MLA_V), F32)],
        compiler_params=_params("parallel", "arbitrary"),
    )(dg, proj, o_lat, wuv_hrv, wuv_hvr)


def _ada_mod(c_all, ada_w, ada_b_sh, *, name):
    nl, _, cols = ada_w.shape

    def body(c_ref, w_ref, b_ref, o_ref):
        c = c_ref[...]
        cond = (c * _sigmoid(c)).astype(BF16)
        for l in range(nl):
            o_ref[l] = _dot(cond, w_ref[l].astype(BF16), NN) + b_ref[l]

    return pl.pallas_call(
        body, name=name, out_shape=jax.ShapeDtypeStruct((nl, c_all.shape[0], cols), F32),
        compiler_params=_params(),
    )(c_all, ada_w, ada_b_sh)


def _ada_grad(c_all_t, dmod_sh, *, name):
    nl, _, cols = dmod_sh.shape
    d = c_all_t.shape[0]

    def body(c_ref, dm_ref, gw_ref):
        c = c_ref[...]
        cond_t = c * _sigmoid(c)
        for l in range(nl):
            gw_ref[l] = lax.dot_general(cond_t, dm_ref[l], (NN, ((), ())), precision=lax.Precision.HIGHEST,
                                        preferred_element_type=F32)

    return pl.pallas_call(
        body, name=name, out_shape=jax.ShapeDtypeStruct((nl, d, cols), F32), compiler_params=_params(),
    )(c_all_t, dmod_sh)


def _sum_devices(parts, *, name):
    def body(p_ref, o_ref):
        acc = p_ref[0]
        for k in range(1, parts.shape[0]):
            acc = acc + p_ref[k]
        o_ref[...] = acc

    return pl.pallas_call(body, name=name, out_shape=jax.ShapeDtypeStruct(parts.shape[1:], F32), compiler_params=_params())(parts)


def _adamw_math(w, g, m, v):
    c1 = 1.0 - ADAM_B1 ** ADAM_STEP
    c2 = 1.0 - ADAM_B2 ** ADAM_STEP
    nm = ADAM_B1 * m + (1.0 - ADAM_B1) * g
    nv = ADAM_B2 * v + (1.0 - ADAM_B2) * (g * g)
    return -ADAM_LR * ((nm / c1) / (jnp.sqrt(nv / c2) + ADAM_EPS) + ADAM_WD * w), nm, nv


ADAMW_BLOCK_BYTES = 1 << 20


def _adamw(w, g, m, v, *, name, after=None):
    shape = w.shape
    a, b = shape[-2], shape[-1]
    lead = 1
    for dim in shape[:-2]:
        lead *= dim
    row_bytes = 4 * b
    if a * row_bytes <= ADAMW_BLOCK_BYTES:
        ta = a
        tl = max(1, min(lead, ADAMW_BLOCK_BYTES // (a * row_bytes)))
        while lead % tl:
            tl -= 1
    else:
        tl = 1
        ta = _tile(a, 256)
    to3 = lambda t: t.reshape(lead, a, b)

    def body(w_ref, g_ref, m_ref, v_ref, *rest):
        d_ref, nm_ref, nv_ref = rest[-3:]
        d_ref[...], nm_ref[...], nv_ref[...] = _adamw_math(w_ref[...], g_ref[...], m_ref[...], v_ref[...])

    spec = pl.BlockSpec((tl, ta, b), lambda i, j: (i, j, 0))
    out = jax.ShapeDtypeStruct((lead, a, b), F32)
    order = [] if after is None else [after]
    res = pl.pallas_call(
        body, name=name, grid=(lead // tl, a // ta), in_specs=[spec] * 4 + [pl.BlockSpec(memory_space=pl.ANY)] * len(order),
        out_specs=[spec] * 3, out_shape=[out] * 3, compiler_params=_params("parallel", "parallel"),
    )(to3(w), to3(g), to3(m), to3(v), *order)
    return [r.reshape(shape) for r in res]


def _adamw_small(ws, gs, ms, vs, *, name):
    n = len(ws)

    def body(*refs):
        for k in range(n):
            w_ref, g_ref, m_ref, v_ref = (refs[j * n + k] for j in range(4))
            d_ref, nm_ref, nv_ref = (refs[(4 + j) * n + k] for j in range(3))
            d_ref[...], nm_ref[...], nv_ref[...] = _adamw_math(w_ref[...], g_ref[...], m_ref[...], v_ref[...])

    outs = [jax.ShapeDtypeStruct(w.shape, F32) for w in ws]
    res = pl.pallas_call(body, name=name, out_shape=outs * 3, compiler_params=_params())(*ws, *gs, *ms, *vs)
    return res[:n], res[n:2 * n], res[2 * n:]


def _flip(v, bit):
    return 1 - v if bit else v


CHIP_DELTAS = ((1, 0), (0, 1), (1, 1))
SUM_ROWS = 32


def _all_gather_chips(shard, *, name):
    def body(x_ref, o_ref, send_sems, recv_sems, local_sem):
        x, y, c = lax.axis_index("x"), lax.axis_index("y"), lax.axis_index("c")
        mine = pltpu.make_async_copy(x_ref, o_ref.at[2 * x + y], local_sem)
        mine.start()

        def copy(k):
            tx, ty = _flip(x, CHIP_DELTAS[k][0]), _flip(y, CHIP_DELTAS[k][1])
            send = pltpu.make_async_remote_copy(src_ref=x_ref, dst_ref=o_ref.at[2 * x + y], send_sem=send_sems.at[k],
                                                recv_sem=recv_sems.at[k], device_id=(tx, ty, c), device_id_type=MESH)
            recv = pltpu.make_async_remote_copy(src_ref=x_ref, dst_ref=o_ref.at[2 * tx + ty], send_sem=send_sems.at[k],
                                                recv_sem=recv_sems.at[k], device_id=(tx, ty, c), device_id_type=MESH)
            return send, recv

        pairs = [copy(k) for k in range(3)]
        for send, _ in pairs:
            send.start()
        for _, recv in pairs:
            recv.wait_recv()
        for send, _ in pairs:
            send.wait_send()
        mine.wait()

    return pl.pallas_call(
        body, name=name, out_shape=jax.ShapeDtypeStruct((N_CHIPS,) + shard.shape, shard.dtype),
        in_specs=[HBM], out_specs=HBM,
        scratch_shapes=[pltpu.SemaphoreType.DMA((3,)), pltpu.SemaphoreType.DMA((3,)), pltpu.SemaphoreType.DMA(())],
    )(shard)


def _gather_weights(shards, *, name):
    n = len(shards)

    def body(*refs):
        w_refs, o_refs = refs[:n], refs[n:2 * n]
        ici_send, ici_recv, d2d_send, d2d_recv, local_sems = refs[2 * n:]
        x, y, c = lax.axis_index("x"), lax.axis_index("y"), lax.axis_index("c")
        me = 2 * x + y
        peers = [(_flip(x, dx), _flip(y, dy)) for dx, dy in CHIP_DELTAS]
        locals_ = [pltpu.make_async_copy(w_refs[k], o_refs[k].at[me], local_sems.at[k]) for k in range(n)]
        for cp in locals_:
            cp.start()

        def rows(k, which):
            half = shards[k].shape[0] // 2
            return pl.ds(pl.multiple_of(which * half, half), half)

        def over_chips(k, d, slot):
            tx, ty = peers[d]
            return pltpu.make_async_remote_copy(
                src_ref=w_refs[k].at[rows(k, c)], dst_ref=o_refs[k].at[slot, rows(k, c)], send_sem=ici_send.at[k, d],
                recv_sem=ici_recv.at[k, d], device_id=(tx, ty, c), device_id_type=MESH)

        def to_sibling(k, d, which):
            tx, ty = peers[d]
            at = o_refs[k].at[2 * tx + ty, rows(k, which)]
            return pltpu.make_async_remote_copy(src_ref=at, dst_ref=at, send_sem=d2d_send.at[k, d], recv_sem=d2d_recv.at[k, d],
                                                device_id=(x, y, 1 - c), device_id_type=MESH)

        sends = [over_chips(k, d, me) for k in range(n) for d in range(3)]
        for cp in sends:
            cp.start()
        passed = []
        for k in range(n):
            for d in range(3):
                over_chips(k, d, 2 * peers[d][0] + peers[d][1]).wait_recv()
                passed.append(to_sibling(k, d, c))
                passed[-1].start()
        for k in range(n):
            for d in range(3):
                to_sibling(k, d, 1 - c).wait_recv()
        for cp in sends + passed:
            cp.wait_send()
        for cp in locals_:
            cp.wait()

    return pl.pallas_call(
        body, name=name, out_shape=[jax.ShapeDtypeStruct((N_CHIPS,) + w.shape, w.dtype) for w in shards],
        in_specs=[HBM] * n, out_specs=[HBM] * n,
        scratch_shapes=[pltpu.SemaphoreType.DMA((n, 3))] * 4 + [pltpu.SemaphoreType.DMA((n,))],
    )(*shards)


def _add_into(dst_ref, src_ref):
    ns, r, _ = dst_ref.shape
    step = SUM_ROWS if r % SUM_ROWS == 0 else r
    for s in range(ns):
        def tile(t, carry):
            at = pl.ds(pl.multiple_of(t * step, step), step)
            dst_ref[s, at, :] = (dst_ref[s, at, :].astype(F32) + src_ref[s, at, :].astype(F32)).astype(dst_ref.dtype)
            return carry
        lax.fori_loop(0, r // step, tile, 0)


def _reduce_sibling(grads, *, name):
    n = len(grads)

    def body(*refs):
        g_refs, o_refs = refs[:n], refs[n:2 * n]
        mine, got = refs[2 * n:3 * n], refs[3 * n:4 * n]
        send_sems, recv_sems, load_sems, store_sems = refs[4 * n:]
        x, y, c = lax.axis_index("x"), lax.axis_index("y"), lax.axis_index("c")
        loads = [pltpu.make_async_copy(g_refs[k].at[:, c], mine[k], load_sems.at[k]) for k in range(n)]
        swaps = [pltpu.make_async_remote_copy(src_ref=g_refs[k].at[:, 1 - c], dst_ref=got[k], send_sem=send_sems.at[k],
                                              recv_sem=recv_sems.at[k], device_id=(x, y, 1 - c), device_id_type=MESH)
                 for k in range(n)]
        for cp in loads + swaps:
            cp.start()
        stores = []
        for k in range(n):
            loads[k].wait()
            swaps[k].wait_recv()
            _add_into(mine[k], got[k])
            stores.append(pltpu.make_async_copy(mine[k], o_refs[k], store_sems.at[k]))
            stores[-1].start()
        for k in range(n):
            swaps[k].wait_send()
            stores[k].wait()

    half = [jax.ShapeDtypeStruct((g.shape[0],) + g.shape[2:], g.dtype) for g in grads]
    return pl.pallas_call(
        body, name=name, out_shape=half, in_specs=[HBM] * n, out_specs=[HBM] * n,
        scratch_shapes=[pltpu.VMEM(h.shape, h.dtype) for h in half] * 2 + [pltpu.SemaphoreType.DMA((n,))] * 4,
        compiler_params=_params(),
    )(*grads)


def _reduce_chips(parts, landed, *, name):
    n = len(parts)

    def body(*refs):
        p_refs, l_refs, o_refs = refs[:n], refs[n:2 * n], refs[2 * n:3 * n]
        got, total = refs[3 * n:4 * n], refs[4 * n:5 * n]
        load_sems, share_send, share_recv, store_sems = refs[5 * n:]
        x, y, c = lax.axis_index("x"), lax.axis_index("y"), lax.axis_index("c")
        me = 2 * x + y
        slots = [me] + [2 * _flip(x, dx) + _flip(y, dy) for dx, dy in CHIP_DELTAS]
        loads = [[pltpu.make_async_copy((p_refs if j == 0 else l_refs)[k].at[slot], got[k].at[slot], load_sems.at[k, j])
                  for j, slot in enumerate(slots)] for k in range(n)]
        for per_array in loads:
            for cp in per_array:
                cp.start()
        shares, stores = [], []
        for k in range(n):
            for cp in loads[k]:
                cp.wait()
            r = total[k].shape[0]
            step = SUM_ROWS if r % SUM_ROWS == 0 else r

            def tile(t, carry, k=k, step=step):
                at = pl.ds(pl.multiple_of(t * step, step), step)
                acc = got[k][0, at, :].astype(F32)
                for s in range(1, N_CHIPS):
                    acc = acc + got[k][s, at, :].astype(F32)
                total[k][at, :] = acc
                return carry

            lax.fori_loop(0, r // step, tile, 0)
            stores.append(pltpu.make_async_copy(total[k], o_refs[k].at[c], store_sems.at[k]))
            shares.append(pltpu.make_async_remote_copy(
                src_ref=total[k], dst_ref=o_refs[k].at[c], send_sem=share_send.at[k], recv_sem=share_recv.at[k],
                device_id=(x, y, 1 - c), device_id_type=MESH))
            stores[-1].start()
            shares[-1].start()
        for k in range(n):
            pltpu.make_async_remote_copy(
                src_ref=total[k], dst_ref=o_refs[k].at[1 - c], send_sem=share_send.at[k], recv_sem=share_recv.at[k],
                device_id=(x, y, 1 - c), device_id_type=MESH).wait_recv()
        for cp in shares:
            cp.wait_send()
        for cp in stores:
            cp.wait()

    return pl.pallas_call(
        body, name=name, out_shape=[jax.ShapeDtypeStruct((2,) + p.shape[1:], F32) for p in parts],
        in_specs=[HBM] * (2 * n), out_specs=[HBM] * n,
        scratch_shapes=[pltpu.VMEM(p.shape, p.dtype) for p in parts] + [pltpu.VMEM(p.shape[1:], F32) for p in parts]
        + [pltpu.SemaphoreType.DMA((n, N_CHIPS))] + [pltpu.SemaphoreType.DMA((n,))] * 3,
        compiler_params=_params(),
    )(*parts, *landed)


SEM = pl.BlockSpec(memory_space=pltpu.SEMAPHORE)
IN_FLIGHT = pltpu.SideEffectType.DATAFLOW_SIDE_EFFECTING


def _chip_copies(s_refs, l_refs, sems, scatter, theirs):
    x, y, c = lax.axis_index("x"), lax.axis_index("y"), lax.axis_index("c")
    me = 2 * x + y
    copies = []
    for k in range(len(s_refs)):
        for d, (dx, dy) in enumerate(CHIP_DELTAS):
            tx, ty = _flip(x, dx), _flip(y, dy)
            peer = 2 * tx + ty
            send_sem, recv_sem = sems[2 * (3 * k + d)], sems[2 * (3 * k + d) + 1]
            copies.append(pltpu.make_async_remote_copy(
                src_ref=s_refs[k].at[peer] if scatter else s_refs[k], dst_ref=l_refs[k].at[peer if theirs else me],
                send_sem=send_sem, recv_sem=recv_sem, device_id=(tx, ty, c), device_id_type=MESH))
    return copies


def _chips_start(srcs, lands, after, *, scatter, name):
    n = len(srcs)
    n_sem = 2 * 3 * n

    def body(*refs):
        s_refs, l_refs = refs[:n], refs[n:2 * n]
        sems = refs[2 * n + 1:2 * n + 1 + n_sem]
        token = refs[-1]
        for cp in _chip_copies(s_refs, l_refs, sems, scatter, False):
            cp.start()
        token[...] = jnp.zeros_like(token)

    hbm = lambda a: pltpu.HBM(a.shape, a.dtype)
    res = pl.pallas_call(
        body, name=name,
        out_shape=(*[pltpu.SemaphoreType.DMA(())] * n_sem, *[hbm(a) for a in srcs], *[hbm(a) for a in lands],
                   jax.ShapeDtypeStruct((8, LANES), F32)),
        in_specs=[HBM] * (2 * n) + [pl.BlockSpec(memory_space=pl.ANY)],
        out_specs=(*[SEM] * n_sem, *[HBM] * (2 * n), VMEM),
        input_output_aliases={k: n_sem + k for k in range(2 * n)},
        compiler_params=pltpu.CompilerParams(has_side_effects=IN_FLIGHT),
    )(*[pltpu.with_memory_space_constraint(a, pltpu.HBM) for a in list(srcs) + list(lands)], after)
    return res[:n_sem], res[n_sem:n_sem + n], res[n_sem + n:n_sem + 2 * n], res[-1]


def _chips_wait(sems, srcs, lands, after, *, scatter, name):
    n = len(srcs)
    n_sem = len(sems)

    def body(*refs):
        s_refs, l_refs = refs[:n], refs[n:2 * n]
        sem_refs = refs[2 * n:2 * n + n_sem]
        for cp in _chip_copies(s_refs, l_refs, sem_refs, scatter, False):
            cp.wait_send()
        for cp in _chip_copies(s_refs, l_refs, sem_refs, scatter, True):
            cp.wait_recv()

    hbm = lambda a: pltpu.HBM(a.shape, a.dtype)
    res = pl.pallas_call(
        body, name=name, out_shape=tuple(hbm(a) for a in list(srcs) + list(lands)),
        in_specs=[HBM] * (2 * n) + [SEM] * n_sem + [pl.BlockSpec(memory_space=pl.ANY)], out_specs=tuple([HBM] * (2 * n)),
        input_output_aliases={k: k for k in range(2 * n)},
        compiler_params=pltpu.CompilerParams(has_side_effects=IN_FLIGHT),
    )(*srcs, *lands, *sems, after)
    return res[:n], res[n:]


def _all_gather_devices(rows, *, name, after=None):
    deltas = [(dx, dy, dc) for dx in (0, 1) for dy in (0, 1) for dc in (0, 1)][1:]
    order = [] if after is None else [after]

    def body(x_ref, *rest):
        o_ref, send_sems, recv_sems = rest[-3:]
        x, y, c = lax.axis_index("x"), lax.axis_index("y"), lax.axis_index("c")
        me = 4 * x + 2 * y + c
        o_ref[me] = x_ref[...]
        sends, recvs = [], []
        for k, (dx, dy, dc) in enumerate(deltas):
            tx, ty, tc = _flip(x, dx), _flip(y, dy), _flip(c, dc)
            sends.append(pltpu.make_async_remote_copy(src_ref=x_ref, dst_ref=o_ref.at[me], send_sem=send_sems.at[k],
                                                      recv_sem=recv_sems.at[k], device_id=(tx, ty, tc), device_id_type=MESH))
            recvs.append(pltpu.make_async_remote_copy(src_ref=x_ref, dst_ref=o_ref.at[4 * tx + 2 * ty + tc],
                                                      send_sem=send_sems.at[k], recv_sem=recv_sems.at[k],
                                                      device_id=(tx, ty, tc), device_id_type=MESH))
        for cp in sends:
            cp.start()
        for cp in recvs:
            cp.wait_recv()
        for cp in sends:
            cp.wait_send()

    return pl.pallas_call(
        body, name=name, out_shape=jax.ShapeDtypeStruct((N_DEV,) + rows.shape, rows.dtype),
        in_specs=[VMEM] + [pl.BlockSpec(memory_space=pl.ANY)] * len(order), out_specs=VMEM,
        scratch_shapes=[pltpu.SemaphoreType.DMA((N_DEV - 1,)), pltpu.SemaphoreType.DMA((N_DEV - 1,))],
    )(rows, *order)


WEIGHTS = ("ada_w", "ada_b", "ln_g", "ln_b", "e_w_in", "gmlp_norm_g", "gmlp_norm_b", "gmlp_ws", "gmlp_bs", "pool_w",
           "pool_b", "pool_scale", "e_w_out", "o_w_in", "mla_q_norm_g", "mla_kv_norm_g", "mla_w_uq", "mla_w_uk",
           "mla_w_uv", "o_w_out")
SMALL = ("ln_g", "ln_b", "gmlp_norm_g", "gmlp_norm_b", "gmlp_bs", "pool_b", "pool_scale", "mla_kv_norm_g", "mla_q_norm_g")


def _pad_cols(v, n):
    return jnp.concatenate([v, jnp.zeros((v.shape[0], n - v.shape[1]), v.dtype)], axis=1) if n > v.shape[1] else v


def _halves(g):
    return g.reshape(g.shape[0], 2, g.shape[1] // 2, g.shape[2])


def kernel(x, c, positions, ada_w, ada_b, ln_g, ln_b, e_w_in, gmlp_norm_g, gmlp_norm_b, gmlp_ws, gmlp_bs, pool_w, pool_b, pool_scale, e_w_out, o_w_in, mla_q_norm_g, mla_kv_norm_g, mla_w_uq, mla_w_uk, mla_w_uv, o_w_out, loss_target, m_ada_w, m_ada_b, m_ln_g, m_ln_b, m_e_w_in, m_gmlp_norm_g, m_gmlp_norm_b, m_gmlp_ws, m_gmlp_bs, m_pool_w, m_pool_b, m_pool_scale, m_e_w_out, m_o_w_in, m_mla_q_norm_g, m_mla_kv_norm_g, m_mla_w_uq, m_mla_w_uk, m_mla_w_uv, m_o_w_out, v_ada_w, v_ada_b, v_ln_g, v_ln_b, v_e_w_in, v_gmlp_norm_g, v_gmlp_norm_b, v_gmlp_ws, v_gmlp_bs, v_pool_w, v_pool_b, v_pool_scale, v_e_w_out, v_o_w_in, v_mla_q_norm_g, v_mla_kv_norm_g, v_mla_w_uq, v_mla_w_uk, v_mla_w_uv, v_o_w_out):
    args = dict(locals())
    weights = {n: args[n] for n in WEIGHTS}
    mom = {n: args["m_" + n] for n in WEIGHTS}
    var = {n: args["v_" + n] for n in WEIGHTS}
    ax, ay, ac = lax.axis_index("x"), lax.axis_index("y"), lax.axis_index("c")
    chip = 2 * ax + ay
    dev = 2 * chip + ac
    d = D_MODEL
    x2 = x[0]
    target = loss_target[0]
    q_rank_sh = mla_q_norm_g.shape[1]

    empty_zone = lambda w: lax.dynamic_update_slice(lax.empty((N_CHIPS,) + w.shape, w.dtype), w[None], (chip, 0, 0))
    shards0 = [w.astype(BF16) for w in (pool_w[0].reshape(-1, POOL_GROUP_DIM), e_w_out[0])]
    shards1 = [w.astype(BF16) for w in (o_w_in[0], mla_w_uq[0].reshape(q_rank_sh, -1), o_w_out[0])]
    w_in0, = _gather_weights([e_w_in[0].astype(BF16)], name="gather_weights")
    wuk_hrd = jnp.transpose(mla_w_uk[0], (1, 0, 2)).astype(BF16)
    wuk_hdr = jnp.transpose(mla_w_uk[0], (1, 2, 0)).astype(BF16)
    wuv_hrv = jnp.transpose(mla_w_uv[0], (1, 0, 2)).astype(BF16)
    wuv_hvr = jnp.transpose(mla_w_uv[0], (1, 2, 0)).astype(BF16)
    ws = gmlp_ws[0]
    ws_t = jnp.transpose(ws, (0, 2, 1))
    bs_t = _pad_cols(gmlp_bs[0].T, LANES)

    inv = 1.0 / (ROPE_THETA ** (jnp.arange(0, MLA_ROPE, 2, dtype=F32) / MLA_ROPE))
    ang = positions[0].astype(F32)[:, None] * inv
    cos_t = jnp.tile(jnp.cos(ang), (1, 4))
    sin_t = jnp.concatenate([-jnp.sin(ang), -jnp.sin(ang), jnp.sin(ang), jnp.sin(ang)], axis=1)

    c_all = _all_gather_devices(c.reshape(8, LANES), after=w_in0, name="gather_c").reshape(N_DEV, d)
    cols = ada_w.shape[2]
    ada_b_mine = lax.dynamic_slice_in_dim(ada_b, chip * cols, cols, axis=1)[:, None, :]
    mod_sh = _ada_mod(c_all, ada_w, ada_b_mine, name="ada_mod")
    q_norm_rows = jnp.zeros((8, cols), F32).at[0, :q_rank_sh].set(mla_q_norm_g[0])
    mod_all = _all_gather_chips(jnp.concatenate([mod_sh.reshape(2 * N_DEV, cols), q_norm_rows]), name="gather_mod")
    q_norm_g = mod_all[:, 2 * N_DEV, :q_rank_sh].reshape(1, -1)
    mod_all = jnp.transpose(mod_all[:, :2 * N_DEV].reshape(N_CHIPS, 2, N_DEV, cols), (1, 2, 0, 3)).reshape(2, N_DEV, 3 * d)
    mod = lax.dynamic_index_in_dim(mod_all, dev, axis=1, keepdims=False)
    shift = [mod[l:l + 1, :d] for l in range(2)]
    scale = [mod[l:l + 1, d:2 * d] for l in range(2)]
    gate = [mod[l:l + 1, 2 * d:] for l in range(2)]
    flight0 = _chips_start(shards0, [empty_zone(w) for w in shards0], mod, scatter=False, name="gather0_start")
    flight1 = _chips_start(shards1, [empty_zone(w) for w in shards1], flight0[3], scatter=False, name="gather1_start")

    scale[0] = scale[0] + flight1[3][:1, :1]
    h0 = _modulate(x2, scale[0], shift[0], name="modulate0")
    proj0 = _matmul(h0, w_in0, b_stacked=True, tm=1024, tn=1280, out_dtype=BF16, name="proj0")
    pool_w_g, w_out0 = _chips_wait(*flight0[:3], proj0, scatter=False, name="gather0_wait")[1]
    pool_w_bf = jnp.transpose(pool_w_g.reshape(N_CHIPS, POOL_GROUPS, -1, POOL_GROUP_DIM), (1, 0, 2, 3)).reshape(
        POOL_GROUPS, POOL_GROUP_DIM, POOL_GROUP_DIM)
    w_out0 = w_out0.reshape(-1, d)
    mix0 = _even_fwd(proj0, ws, bs_t, gmlp_norm_g, gmlp_norm_b, pool_w_bf, pool_b, pool_scale, name="even_fwd")
    y0, x1, h1 = _out_resid_ln(mix0, w_out0, x2, gate[0], ln_g[0:1], ln_b[0:1], scale[1], shift[1], name="out0_ln")

    w_in1_g, w_uq_g, w_out1 = _chips_wait(*flight1[:3], h1, scatter=False, name="gather1_wait")[1]
    w_out1 = w_out1.reshape(-1, d)
    w_in1 = jnp.transpose(w_in1_g, (1, 0, 2)).reshape(d, ODD_IN)
    w_in1 = jnp.concatenate([w_in1[:, ODD_SMALL:], _pad_cols(w_in1[:, :ODD_SMALL], ODD_SMALL_PAD)], axis=1)
    w_uq = w_uq_g.reshape(MLA_Q_RANK, MLA_HEADS, MLA_NOPE + MLA_ROPE)
    w_uq_nope = w_uq[:, :, :MLA_NOPE].reshape(MLA_Q_RANK, -1)
    w_uq_rope = jnp.transpose(w_uq[:, :, MLA_NOPE:].reshape(MLA_Q_RANK, MLA_HEADS // 2, 2, 2, ROPE_HALF),
                              (0, 1, 3, 2, 4)).reshape(MLA_Q_RANK, -1)
    proj1 = _matmul(h1, w_in1, tm=1024, tn=1280, out_dtype=BF16, name="proj1")
    q_cn, keys = _mla_prep(proj1, q_norm_g, mla_kv_norm_g, cos_t, sin_t, name="mla_prep")
    q_nope = _matmul(q_cn, w_uq_nope, tm=1024, tn=2048, name="q_nope", out_dtype=BF16)
    q_rope_pre = _matmul(q_cn, w_uq_rope, tm=1024, name="q_rope")
    q = _q_build(q_nope, q_rope_pre, wuk_hdr, cos_t, sin_t, name="q_build")
    o_lat, lse = _attn_fwd(q, keys, name="attn_fwd")
    og = _o_build(o_lat, wuv_hrv, proj1, name="o_build")

    dy1, dres1, g_ln_g1, g_ln_b1, dgate1, loss = _out_loss_ln_bwd(
        og, w_out1, x1, gate[1], ln_g[1:2], ln_b[1:2], target, name="out1_loss_ln")
    dg1 = _matmul(dy1, w_out1, trans_b=True, tn=2048, out_dtype=BF16, name="d_og")
    g_w_out1 = _matmul(og, dy1, trans_a=True, out_dtype=BF16, tm=1024, name="g_out1")
    do_lat, dz, g_uv = _o_bwd(dg1, proj1, o_lat, wuv_hrv, wuv_hvr, name="o_bwd")
    dq, dkeys = _attn_bwd(q, keys, do_lat, o_lat, lse, name="attn_bwd")
    dq_all, g_uk = _q_bwd(dq, q_nope, wuk_hrd, cos_t, sin_t, name="q_bwd")
    n_grp = MLA_HEADS // Q_HEAD_GROUP
    w_uq_all = jnp.concatenate([w_uq_nope.reshape(MLA_Q_RANK, n_grp, -1), w_uq_rope.reshape(MLA_Q_RANK, n_grp, -1)],
                               axis=2).reshape(MLA_Q_RANK, -1)
    dq_cn = _matmul(dq_all, w_uq_all, trans_b=True, tm=1024, name="d_qcn")
    g_uq_all = _matmul(q_cn, dq_all, trans_a=True, out_dtype=BF16, name="g_uq").reshape(MLA_Q_RANK, n_grp, -1)
    g_uq_nope = g_uq_all[:, :, :Q_HEAD_GROUP * MLA_NOPE].reshape(MLA_Q_RANK, -1)
    g_uq_rope = g_uq_all[:, :, Q_HEAD_GROUP * MLA_NOPE:].reshape(MLA_Q_RANK, -1)
    dsmall, g_qg, g_kvg = _mla_prep_bwd(proj1, dq_cn, dkeys, q_norm_g, mla_kv_norm_g, cos_t, sin_t, name="mla_prep_bwd")
    dproj1 = jnp.concatenate([dz, dsmall], axis=1)
    g_w_in1 =_matmul(h1, dproj1, trans_a=True, out_dtype=BF16, tm=1024, tn=1280, name="g_in1")

    g_uq_rope = jnp.transpose(g_uq_rope.reshape(MLA_Q_RANK, MLA_HEADS // 2, 2, 2, ROPE_HALF), (0, 1, 3, 2, 4))
    g_uq = jnp.concatenate([g_uq_nope.reshape(MLA_Q_RANK, MLA_HEADS, MLA_NOPE), g_uq_rope.reshape(MLA_Q_RANK, MLA_HEADS, MLA_ROPE)], axis=2)
    g_w_in1 = jnp.concatenate([g_w_in1[:, MLA_WIDTH:MLA_WIDTH + ODD_SMALL], g_w_in1[:, :MLA_WIDTH]], axis=1)
    g_w_in1 = jnp.transpose(g_w_in1.reshape(d, N_CHIPS, -1), (1, 0, 2))
    big1 = [
        _halves(g_w_in1),
        _halves(g_uq.reshape(N_CHIPS, q_rank_sh, -1)),
        _halves(g_w_out1.reshape(N_CHIPS, -1, d)),
        _halves(g_uk.astype(BF16).reshape(N_CHIPS, -1, MLA_NOPE)),
        _halves(g_uv.astype(BF16).reshape(N_CHIPS, -1, MLA_V)),
    ]
    parts1 = _reduce_sibling(big1, name="reduce_sibling1")
    flight2 = _chips_start(parts1, [lax.empty(p.shape, BF16) for p in parts1], loss, scatter=True, name="reduce1_start")

    gate[0] = gate[0] + flight2[3][:1, :1]
    dy0, dres0, g_ln_g0, g_ln_b0, dgate0, dscale1, dshift1 = _dh_mid_ln_bwd(
        dproj1, w_in1, x2, y0, gate[0], ln_g[0:1], ln_b[0:1], dres1, scale[1], x1, name="d_h1_mid_ln")
    dmix0 = _matmul(dy0, w_out0, trans_b=True, tn=2048, out_dtype=BF16, name="d_mix0")
    g_w_out0 = _matmul(mix0, dy0, trans_a=True, out_dtype=BF16, tm=1024, name="g_out0")
    dproj0, g_ws, g_bs_t, g_ng, g_nb, g_pw, g_pb, g_ps = _even_bwd(
        proj0, dmix0, ws, ws_t, bs_t, gmlp_norm_g, gmlp_norm_b, pool_w_bf, pool_b, pool_scale, name="even_bwd")
    g_w_in0 = _matmul(h0, dproj0, trans_a=True, out_dtype=BF16, out_stacked=True, tm=1024, tn=1280, name="g_in0")

    g_pw = jnp.transpose(g_pw.astype(BF16).reshape(POOL_GROUPS, N_CHIPS, -1, POOL_GROUP_DIM), (1, 0, 2, 3))
    big0 = [
        _halves(g_w_in0),
        _halves(g_pw.reshape(N_CHIPS, -1, POOL_GROUP_DIM)),
        _halves(g_w_out0.reshape(N_CHIPS, -1, d)),
        _halves(g_ws.astype(BF16)),
    ]
    parts0 = _reduce_sibling(big0, name="reduce_sibling0")
    parts1, landed1 = _chips_wait(*flight2[:3], parts0[0], scatter=True, name="reduce1_wait")
    flight3 = _chips_start(parts0, [lax.empty(p.shape, BF16) for p in parts0], landed1[0], scatter=True, name="reduce0_start")
    grad_x, dscale0, dshift0 = _dh_input_bwd(dproj0, w_in0, x2, dres0, scale[0], after=flight3[3], name="d_h0_input")

    small_local = {
        "ln_g": jnp.concatenate([g_ln_g0, g_ln_g1]), "ln_b": jnp.concatenate([g_ln_b0, g_ln_b1]),
        "gmlp_norm_g": g_ng, "gmlp_norm_b": g_nb, "gmlp_bs": g_bs_t[:, :GMLP_HEADS].T, "pool_b": g_pb, "pool_scale": g_ps,
        "mla_kv_norm_g": g_kvg, "mla_q_norm_g": g_qg,
    }
    n_mod = 2 * 3 * d
    vec = jnp.concatenate([dshift0, dscale0, dgate0, dshift1, dscale1, dgate1]
                          + [small_local[n].reshape(1, -1) for n in SMALL] + [loss], axis=1)
    n_vec = vec.shape[1]
    vec = _pad_cols(vec, -(-n_vec // (8 * LANES)) * 8 * LANES).reshape(-1, LANES)
    vec_all = _all_gather_devices(vec, name="gather_small")
    vec_sum = _sum_devices(vec_all, name="sum_small").reshape(-1)
    dmod_all = vec_all.reshape(N_DEV, -1)[:, :n_mod].reshape(N_DEV, 2, 3 * d)
    dmod_sh = jnp.transpose(lax.dynamic_slice_in_dim(dmod_all, chip * cols, cols, axis=2), (1, 0, 2))
    dmod_sh = jnp.concatenate([dmod_sh, jnp.zeros((2, LANES - N_DEV, cols), F32)], axis=1)
    grads = {"ada_w": _ada_grad(_pad_cols(c_all.T, LANES), dmod_sh, name="ada_grad"), "ada_b": vec_sum[:n_mod].reshape(2, 3 * d)}
    off = n_mod
    for n in SMALL:
        sz = small_local[n].size
        grads[n] = vec_sum[off:off + sz]
        off += sz
    grads["mla_q_norm_g"] = lax.dynamic_slice_in_dim(grads["mla_q_norm_g"], chip * q_rank_sh, q_rank_sh)
    for n in SMALL:
        grads[n] = grads[n].reshape(weights[n].shape)

    parts0, landed0 = _chips_wait(*flight3[:3], grads["ada_w"], scatter=True, name="reduce0_wait")
    totals = _reduce_chips(list(parts0) + list(parts1), list(landed0) + list(landed1), name="reduce_chips")
    for n, t in zip(("e_w_in", "pool_w", "e_w_out", "gmlp_ws", "o_w_in", "mla_w_uq", "o_w_out"), totals):
        if n != "gmlp_ws":
            grads[n] = t.reshape(weights[n].shape)
    rep = jnp.concatenate([t.reshape(-1, LANES) for t in (totals[3], totals[7], totals[8])])
    rep_land = lax.dynamic_update_slice(lax.empty((N_CHIPS,) + rep.shape, F32), rep[None], (chip, 0, 0))
    flight4 = _chips_start([rep], [rep_land], totals[0], scatter=False, name="gather_rep_start")

    delta, new_m, new_v = {}, {}, {}
    replicated = ("gmlp_ws", "mla_w_uk", "mla_w_uv")
    large = [n for n in WEIGHTS if n not in SMALL and n != "ada_b"]
    for n in large:
        if n not in replicated:
            delta[n], new_m[n], new_v[n] = _adamw(weights[n], grads[n], mom[n], var[n], after=flight4[3], name="adamw_" + n)
    rep = _chips_wait(*flight4[:3], delta["e_w_in"], scatter=False, name="gather_rep_wait")[1][0]
    r_ws, r_uk = GMLP_BLOCK, 4 * MLA_KV_RANK
    grads["gmlp_ws"] = rep[:, :r_ws].reshape(weights["gmlp_ws"].shape)
    grads["mla_w_uk"] = jnp.transpose(rep[:, r_ws:r_ws + r_uk].reshape(MLA_HEADS, MLA_KV_RANK, MLA_NOPE), (1, 0, 2))[None]
    grads["mla_w_uv"] = jnp.transpose(rep[:, r_ws + r_uk:].reshape(MLA_HEADS, MLA_KV_RANK, MLA_V), (1, 0, 2))[None]
    for n in replicated:
        delta[n], new_m[n], new_v[n] = _adamw(weights[n], grads[n], mom[n], var[n], name="adamw_" + n)
    small = [n for n in WEIGHTS if n not in large]
    ds, ms, vs = _adamw_small([weights[n] for n in small], [grads[n] for n in small], [mom[n] for n in small],
                              [var[n] for n in small], name="adamw_small")
    for n, dn, mn, vn in zip(small, ds, ms, vs):
        delta[n], new_m[n], new_v[n] = dn, mn, vn

    return (vec_sum[n_vec - 1], grad_x[None], *[grads[n] for n in WEIGHTS], *[delta[n] for n in WEIGHTS],
            *[new_m[n] for n in WEIGHTS], *[new_v[n] for n in WEIGHTS])
```

```python
import jax
import jax.numpy as jnp
from jax import lax
from jax.experimental import pallas as pl
from jax.experimental.pallas import tpu as pltpu

F32 = jnp.float32
BF16 = jnp.bfloat16
MESH = pl.DeviceIdType.MESH

D_MODEL = 1024
CHUNK = 64
LN_EPS = 1e-5
GMLP_HEADS = 4
GMLP_HEAD_DIM = 256
GMLP_BLOCK = 128
POOL_WINDOWS = (2, 4, 8, 16)
POOL_GROUPS = 4
POOL_GROUP_DIM = 256
POOL_HALO = 16
EVEN_IN = 5120
MLA_HEADS = 16
MLA_NOPE = 128
MLA_ROPE = 64
MLA_V = 128
MLA_Q_RANK = 256
MLA_KV_RANK = 128
MLA_WIDTH = MLA_HEADS * MLA_V
ODD_IN = 2496
ODD_SMALL = MLA_Q_RANK + MLA_KV_RANK + MLA_ROPE
ODD_SMALL_PAD = 512
QK_PAD = 256
ROPE_THETA = 10000.0
ATTN_SCALE = (MLA_NOPE + MLA_ROPE) ** -0.5
DEEPNORM_ALPHA = (2.0 * 2) ** 0.25
ADAM_LR = 0.001
ADAM_B1 = 0.9
ADAM_B2 = 0.999
ADAM_EPS = 1e-08
ADAM_WD = 0.01
ADAM_STEP = 10
NEG = -1e30
LANES = 128
N_DEV = 8
N_CHIPS = 4
VMEM_LIMIT_BYTES = 56 * 1024 * 1024
HBM = pl.BlockSpec(memory_space=pltpu.HBM)
VMEM = pl.BlockSpec(memory_space=pltpu.VMEM)


def _params(*sem):
    return pltpu.CompilerParams(dimension_semantics=sem if sem else None, vmem_limit_bytes=VMEM_LIMIT_BYTES)


def _tile(dim, pref):
    for t in (pref, 2048, 1280, 1024, 512, 256, 128):
        if t <= min(pref, dim) and dim % t == 0:
            return t
    return dim


def _sigmoid(z):
    return 1.0 / (1.0 + jnp.exp(-z))


def _dot(a, b, dims):
    return lax.dot_general(a, b, (dims, ((), ())), preferred_element_type=F32)


NN = ((1,), (0,))
NT = ((1,), (1,))
TN = ((0,), (0,))


def _matmul(a, b, *, name, trans_a=False, trans_b=False, out_dtype=F32, b_stacked=False, out_stacked=False,
            tm=512, tn=1024, tk=2048, after=None):
    k, m = a.shape if trans_a else a.shape[::-1]
    if b_stacked:
        assert not trans_b
        ns, kb, n_sh = b.shape
        n = ns * n_sh
    else:
        n, kb = b.shape if trans_b else b.shape[::-1]
    assert k == kb, (a.shape, b.shape)
    tm = _tile(m, tm)
    tn, tk = _tile(n // N_CHIPS if b_stacked or out_stacked else n, tn), _tile(k, tk)
    nk = k // tk
    per = max((n // N_CHIPS) // tn, 1)
    dims = ((0 if trans_a else 1,), (1 if trans_b else 0,))

    def body_one(a_ref, b_ref, *rest):
        o_ref = rest[-1]
        o_ref[...] = _dot(a_ref[...].astype(BF16), b_ref[...].astype(BF16), dims).astype(out_dtype)

    def body_acc(a_ref, b_ref, *rest):
        o_ref, acc_ref = rest[-2:]
        kk = pl.program_id(2)

        @pl.when(kk == 0)
        def _():
            acc_ref[...] = jnp.zeros_like(acc_ref)

        acc_ref[...] += _dot(a_ref[...].astype(BF16), b_ref[...].astype(BF16), dims)

        @pl.when(kk == nk - 1)
        def _():
            o_ref[...] = acc_ref[...].astype(out_dtype)

    a_spec = pl.BlockSpec((tk, tm), lambda i, j, kk: (kk, i)) if trans_a else pl.BlockSpec((tm, tk), lambda i, j, kk: (i, kk))
    if b_stacked:
        b_spec = pl.BlockSpec((None, tk, tn), lambda i, j, kk: (j // per, kk, j % per))
    elif trans_b:
        b_spec = pl.BlockSpec((tn, tk), lambda i, j, kk: (j, kk))
    else:
        b_spec = pl.BlockSpec((tk, tn), lambda i, j, kk: (kk, j))
    if out_stacked:
        o_spec = pl.BlockSpec((None, tm, tn), lambda i, j, kk: (j // per, i, j % per))
        o_shape = jax.ShapeDtypeStruct((N_CHIPS, m, n // N_CHIPS), out_dtype)
    else:
        o_spec = pl.BlockSpec((tm, tn), lambda i, j, kk: (i, j))
        o_shape = jax.ShapeDtypeStruct((m, n), out_dtype)
    order = [] if after is None else [after]
    return pl.pallas_call(
        body_one if nk == 1 else body_acc, name=name, grid=(m // tm, n // tn, nk),
        in_specs=[a_spec, b_spec] + [pl.BlockSpec(memory_space=pl.ANY)] * len(order),
        out_specs=o_spec, out_shape=o_shape, scratch_shapes=[] if nk == 1 else [pltpu.VMEM((tm, tn), F32)],
        compiler_params=_params("parallel", "parallel", "arbitrary"),
    )(a, b, *order)


def _matmul_rows(a, b, epilogue, row_ins, vec_ins, row_outs, vec_outs, *, name, trans_b=False, b_stacked=False,
                 tm=512, tk=2048, after=None):
    m, k = a.shape
    if b_stacked:
        ns, n, n_sh = b.shape
        assert trans_b and ns * n_sh == k
        tk = n_sh
    else:
        n = b.shape[0] if trans_b else b.shape[1]
        tk = _tile(k, tk)
    tm = _tile(m, tm)
    nk = k // tk
    dims = ((1,), (1 if trans_b else 0,))
    n_ri, n_vi, n_ro, n_vo = len(row_ins), len(vec_ins), len(row_outs), len(vec_outs)
    order = [] if after is None else [after]

    def body(*refs):
        a_ref, b_ref = refs[:2]
        pos = 2
        rin = refs[pos:pos + n_ri]
        pos += n_ri
        vin = refs[pos:pos + n_vi]
        pos += n_vi + len(order)
        rout = refs[pos:pos + n_ro]
        pos += n_ro
        vout = refs[pos:pos + n_vo]
        first = pl.program_id(0) == 0
        part = _dot(a_ref[...].astype(BF16), b_ref[...].astype(BF16), dims)
        if nk == 1:
            epilogue(part, first, rin, vin, rout, vout)
        else:
            acc_ref = refs[-1]
            kk = pl.program_id(1)

            @pl.when(kk == 0)
            def _():
                acc_ref[...] = part

            @pl.when(kk > 0)
            def _():
                acc_ref[...] += part

            @pl.when(kk == nk - 1)
            def _():
                epilogue(acc_ref[...], first, rin, vin, rout, vout)

    a_spec = pl.BlockSpec((tm, tk), lambda i, kk: (i, kk))
    if b_stacked:
        b_spec = pl.BlockSpec((None, n, tk), lambda i, kk: (kk, 0, 0))
    elif trans_b:
        b_spec = pl.BlockSpec((n, tk), lambda i, kk: (0, kk))
    else:
        b_spec = pl.BlockSpec((tk, n), lambda i, kk: (kk, 0))
    row = pl.BlockSpec((tm, n), lambda i, kk: (i, 0))
    vec = lambda w: pl.BlockSpec((1, w), lambda i, kk: (0, 0))
    return pl.pallas_call(
        body, name=name, grid=(m // tm, nk),
        in_specs=[a_spec, b_spec] + [row] * n_ri + [vec(v.shape[1]) for v in vec_ins] + [pl.BlockSpec(memory_space=pl.ANY)] * len(order),
        out_specs=[row] * n_ro + [vec(w) for w in vec_outs],
        out_shape=[jax.ShapeDtypeStruct((m, n), dt) for dt in row_outs] + [jax.ShapeDtypeStruct((1, w), F32) for w in vec_outs],
        scratch_shapes=[] if nk == 1 else [pltpu.VMEM((tm, n), F32)],
        compiler_params=_params("arbitrary", "arbitrary"),
    )(a, b, *row_ins, *vec_ins, *order)


def _row_spec(ts, d):
    return pl.BlockSpec((ts, d), lambda i: (i, 0))


def _vec_spec(d):
    return pl.BlockSpec((1, d), lambda i: (0, 0))


def _modulate(x, scale, shift, *, name):
    s, d = x.shape
    ts = _tile(s, 512)

    def body(x_ref, sc_ref, sh_ref, h_ref):
        h_ref[...] = (x_ref[...] * (1.0 + sc_ref[...]) + sh_ref[...]).astype(BF16)

    return pl.pallas_call(
        body, name=name, grid=(s // ts,), in_specs=[_row_spec(ts, d), _vec_spec(d), _vec_spec(d)],
        out_specs=_row_spec(ts, d), out_shape=jax.ShapeDtypeStruct((s, d), BF16), compiler_params=_params("parallel"),
    )(x, scale, shift)


def _ln_stats(pre):
    mu = jnp.mean(pre, axis=-1, keepdims=True)
    xc = pre - mu
    var = jnp.mean(xc * xc, axis=-1, keepdims=True)
    rstd = lax.rsqrt(var + LN_EPS)
    return xc * rstd, rstd


def _ln_bwd_rows(dout, xhat, rstd, g):
    dxh = dout * g
    m1 = jnp.mean(dxh, axis=-1, keepdims=True)
    m2 = jnp.mean(dxh * xhat, axis=-1, keepdims=True)
    return rstd * (dxh - m1 - xhat * m2)


def _colsum(v):
    return jnp.sum(v, axis=0, keepdims=True)


def _out_resid_ln(mix, w_out, x, gate, g, b, scale_next, shift_next, *, name):
    def epilogue(y, first, rin, vin, rout, vout):
        (x_ref,), (gate_ref, g_ref, b_ref, sc_ref, sh_ref), (y_ref, xn_ref, h_ref) = rin, vin, rout
        y_ref[...] = y
        pre = DEEPNORM_ALPHA * x_ref[...] + (1.0 + gate_ref[...]) * y
        xhat, _ = _ln_stats(pre)
        xn = xhat * g_ref[...] + b_ref[...]
        xn_ref[...] = xn
        h_ref[...] = (xn * (1.0 + sc_ref[...]) + sh_ref[...]).astype(BF16)

    return _matmul_rows(mix, w_out, epilogue, [x], [gate, g, b, scale_next, shift_next], [F32, F32, BF16], [], name=name)


def _out_loss_ln_bwd(og, w_out, x, gate, g, b, target, *, name):
    d = x.shape[1]

    def epilogue(yv, first, rin, vin, rout, vout):
        (x_ref, t_ref), (gate_ref, g_ref, b_ref), (dy_ref, dres_ref), (dg_ref, db_ref, dgate_ref, loss_ref) = rin, vin, rout, vout

        @pl.when(first)
        def _():
            for r in vout:
                r[...] = jnp.zeros_like(r)

        pre = DEEPNORM_ALPHA * x_ref[...] + (1.0 + gate_ref[...]) * yv
        xhat, rstd = _ln_stats(pre)
        diff = xhat * g_ref[...] + b_ref[...] - t_ref[...]
        loss_ref[...] += (0.5 / d) * jnp.sum(jnp.sum(diff * diff, axis=1, keepdims=True), axis=0, keepdims=True)
        dout = diff * (1.0 / d)
        dpre = _ln_bwd_rows(dout, xhat, rstd, g_ref[...])
        dy_ref[...] = (dpre * (1.0 + gate_ref[...])).astype(BF16)
        dres_ref[...] = DEEPNORM_ALPHA * dpre
        dg_ref[...] += _colsum(dout * xhat)
        db_ref[...] += _colsum(dout)
        dgate_ref[...] += _colsum(dpre * yv)

    return _matmul_rows(og, w_out, epilogue, [x, target], [gate, g, b], [BF16, F32], [d, d, d, 1], name=name)


def _dh_mid_ln_bwd(dproj, w_in, x, y, gate, g, b, dres_next, scale_next, x_next, *, name):
    d = x.shape[1]

    def epilogue(dh, first, rin, vin, rout, vout):
        (x_ref, y_ref, dr_ref, xn_ref), (gate_ref, g_ref, b_ref, sc_ref), (dy_ref, dres_ref) = rin, vin, rout
        dg_ref, db_ref, dgate_ref, dscale_ref, dshift_ref = vout

        @pl.when(first)
        def _():
            for r in vout:
                r[...] = jnp.zeros_like(r)

        dout = dr_ref[...] + dh * (1.0 + sc_ref[...])
        dscale_ref[...] += _colsum(dh * xn_ref[...])
        dshift_ref[...] += _colsum(dh)
        yv = y_ref[...]
        pre = DEEPNORM_ALPHA * x_ref[...] + (1.0 + gate_ref[...]) * yv
        xhat, rstd = _ln_stats(pre)
        dpre = _ln_bwd_rows(dout, xhat, rstd, g_ref[...])
        dy_ref[...] = (dpre * (1.0 + gate_ref[...])).astype(BF16)
        dres_ref[...] = DEEPNORM_ALPHA * dpre
        dg_ref[...] += _colsum(dout * xhat)
        db_ref[...] += _colsum(dout)
        dgate_ref[...] += _colsum(dpre * yv)

    return _matmul_rows(dproj, w_in, epilogue, [x, y, dres_next, x_next], [gate, g, b, scale_next], [BF16, F32], [d] * 5,
                        trans_b=True, tk=2560, name=name)


def _dh_input_bwd(dproj, w_in_stacked, x, dres, scale, *, name, after):
    d = x.shape[1]

    def epilogue(dh, first, rin, vin, rout, vout):
        (x_ref, dr_ref), (sc_ref,), (dx_ref,), (dscale_ref, dshift_ref) = rin, vin, rout, vout

        @pl.when(first)
        def _():
            for r in vout:
                r[...] = jnp.zeros_like(r)

        dx_ref[...] = dr_ref[...] + dh * (1.0 + sc_ref[...])
        dscale_ref[...] += _colsum(dh * x_ref[...])
        dshift_ref[...] += _colsum(dh)

    return _matmul_rows(dproj, w_in_stacked, epilogue, [x, dres], [scale], [F32], [d, d], trans_b=True, b_stacked=True,
                        tm=1024, after=after, name=name)


def _chunk_mask(transposed=False):
    r = lax.broadcasted_iota(jnp.int32, (GMLP_BLOCK, GMLP_BLOCK), 0) // CHUNK
    c = lax.broadcasted_iota(jnp.int32, (GMLP_BLOCK, GMLP_BLOCK), 1) // CHUNK
    return (r <= c) if transposed else (c <= r)


def _window_sum(ext, steps, forward):
    rows = ext.shape[0]
    acc = ext
    for k in range(steps):
        shift = 1 << k
        acc = acc + pltpu.roll(acc, (rows - shift) if forward else shift, 0)
    return acc


def _pool_counts(first_row, rows, win):
    t = first_row + lax.broadcasted_iota(jnp.int32, (rows, 1), 0)
    return jnp.minimum(t + 1, win).astype(F32)


def _even_specs(t):
    col = lambda j: pl.BlockSpec((t, D_MODEL), lambda n: (n, j))
    per = t // POOL_HALO
    prev = pl.BlockSpec((POOL_HALO, D_MODEL), lambda n: (jnp.maximum(n * per - 1, 0), 3))
    return col, per, prev


def _full(shape):
    return pl.BlockSpec(shape, lambda n: (0,) * len(shape))


def _gmlp_head(v_h, ng, nb, w_bf):
    xhat, rstd = _ln_stats(v_h)
    vn = (xhat * ng + nb).astype(BF16)
    return xhat, rstd, vn, _dot(w_bf, vn, NN)


def _pool_group(xb_g, prev_g, first_row, grp):
    t = xb_g.shape[0]
    ext = jnp.concatenate([prev_g, xb_g], axis=0)
    tot = _window_sum(ext, grp + 1, False)[POOL_HALO:, :]
    cnt = _pool_counts(first_row, t, POOL_WINDOWS[grp])
    return tot / cnt - xb_g, cnt


def _even_fwd(proj, ws, bs_t, ng, nb, pool_w, pool_b, pool_scale, *, name):
    s = proj.shape[0]
    t = GMLP_BLOCK
    col, per, prev = _even_specs(t)

    def body(u_ref, v_ref, za_ref, xb_ref, zb_ref, xp_ref, ws_ref, bs_ref, ng_ref, nb_ref, pw_ref, pb_ref, ps_ref, o_ref):
        n = pl.program_id(0)
        mask = _chunk_mask()
        for h in range(GMLP_HEADS):
            c0 = h * GMLP_HEAD_DIM
            cs = slice(c0, c0 + GMLP_HEAD_DIM)
            w_bf = jnp.where(mask, ws_ref[h], 0.0).astype(BF16)
            _, _, _, sv = _gmlp_head(v_ref[:, cs].astype(F32),ng_ref[...], nb_ref[...], w_bf)
            sv = sv + bs_ref[:, h:h + 1]
            za = za_ref[:, cs].astype(F32)
            o_ref[:, cs] = (u_ref[:, cs].astype(F32) * sv * (za * _sigmoid(za))).astype(BF16)
        live = (n > 0).astype(F32)
        for grp in range(POOL_GROUPS):
            c0 = grp * POOL_GROUP_DIM
            cs = slice(c0, c0 + POOL_GROUP_DIM)
            pooled, _ = _pool_group(xb_ref[:, cs].astype(F32), xp_ref[:, cs].astype(F32) * live, n * t, grp)
            yb = _dot(pooled.astype(BF16), pw_ref[grp], NN) + pb_ref[:, cs]
            zb = zb_ref[:, cs].astype(F32)
            o_ref[:, D_MODEL + c0:D_MODEL + c0 + POOL_GROUP_DIM] = (yb * ps_ref[:, cs] * (zb * _sigmoid(zb))).astype(BF16)

    return pl.pallas_call(
        body, name=name, grid=(s // t,),
        in_specs=[col(0), col(1), col(2), col(3), col(4), prev,
                  _full((GMLP_HEADS, t, t)), _full((t, LANES)), _full((1, GMLP_HEAD_DIM)), _full((1, GMLP_HEAD_DIM)),
                  _full((POOL_GROUPS, POOL_GROUP_DIM, POOL_GROUP_DIM)), _full((1, D_MODEL)), _full((1, D_MODEL))],
        out_specs=pl.BlockSpec((t, 2 * D_MODEL), lambda n: (n, 0)),
        out_shape=jax.ShapeDtypeStruct((s, 2 * D_MODEL), BF16),
        compiler_params=_params("parallel"),
    )(proj, proj, proj, proj, proj, proj, ws, bs_t, ng, nb, pool_w, pool_b, pool_scale)


def _even_bwd(proj, dmix, ws, ws_t, bs_t, ng, nb, pool_w, pool_b, pool_scale, *, name):
    s = proj.shape[0]
    t = GMLP_BLOCK
    nblk = s // t
    col, per, prev = _even_specs(t)
    nxt = lambda j: pl.BlockSpec((POOL_HALO, D_MODEL), lambda n: (jnp.minimum((n + 1) * per, nblk * per - 1), j))

    def body(u_ref, v_ref, za_ref, xb_ref, zb_ref, xp_ref, zn_ref, da_ref, db_ref, dbn_ref,
             ws_ref, wst_ref, bs_ref, ng_ref, nb_ref, pw_ref, pb_ref, ps_ref,
             dp_ref, gws_ref, gbs_ref, gng_ref, gnb_ref, gpw_ref, gpb_ref, gps_ref):
        n = pl.program_id(0)

        @pl.when(n == 0)
        def _():
            for r in (gws_ref, gbs_ref, gng_ref, gnb_ref, gpw_ref, gpb_ref, gps_ref):
                r[...] = jnp.zeros_like(r)

        mask, mask_t = _chunk_mask(), _chunk_mask(True)
        lane = lax.broadcasted_iota(jnp.int32, (t, LANES), 1)
        ngv, nbv = ng_ref[...], nb_ref[...]
        for h in range(GMLP_HEADS):
            c0 = h * GMLP_HEAD_DIM
            cs = slice(c0, c0 + GMLP_HEAD_DIM)
            w_bf = jnp.where(mask, ws_ref[h], 0.0).astype(BF16)
            wt_bf = jnp.where(mask_t, wst_ref[h], 0.0).astype(BF16)
            xhat, rstd, vn, sv = _gmlp_head(v_ref[:, cs].astype(F32),ngv, nbv, w_bf)
            sv = sv + bs_ref[:, h:h + 1]
            za, u, da = za_ref[:, cs].astype(F32), u_ref[:, cs].astype(F32), da_ref[:, cs].astype(F32)
            sg = _sigmoid(za)
            sl = za * sg
            dp_ref[:, cs] = (da * sv * sl).astype(BF16)
            dp_ref[:, 2 * D_MODEL + c0:2 * D_MODEL + c0 + GMLP_HEAD_DIM] = (
                da * u * sv * (sg * (1.0 + za * (1.0 - sg)))).astype(BF16)
            dsv = da * u * sl
            gbs_ref[...] += jnp.where(lane == h, jnp.sum(dsv, axis=1, keepdims=True), 0.0)
            dsv_bf = dsv.astype(BF16)
            gws_ref[h] += jnp.where(mask, _dot(dsv_bf, vn, NT), 0.0)
            dvn = _dot(wt_bf, dsv_bf, NN)
            dp_ref[:, D_MODEL + c0:D_MODEL + c0 + GMLP_HEAD_DIM] = _ln_bwd_rows(dvn, xhat, rstd, ngv).astype(BF16)
            gng_ref[...] += _colsum(dvn * xhat)
            gnb_ref[...] += _colsum(dvn)
        live_prev = (n > 0).astype(F32)
        live_next = (n < nblk - 1).astype(F32)
        for grp in range(POOL_GROUPS):
            c0 = grp * POOL_GROUP_DIM
            cs = slice(c0, c0 + POOL_GROUP_DIM)
            xb = xb_ref[:, cs].astype(F32)
            pooled, cnt = _pool_group(xb, xp_ref[:, cs].astype(F32) * live_prev, n * t, grp)
            pooled_bf = pooled.astype(BF16)
            pw = pw_ref[grp]
            yb = _dot(pooled_bf, pw, NN) + pb_ref[:, cs]
            ps = ps_ref[:, cs]
            zb, db = zb_ref[:, cs].astype(F32), db_ref[:, cs].astype(F32)
            sg = _sigmoid(zb)
            sl = zb * sg
            dp_ref[:, 4 * D_MODEL + c0:4 * D_MODEL + c0 + POOL_GROUP_DIM] = (
                db * yb * ps * (sg * (1.0 + zb * (1.0 - sg)))).astype(BF16)
            dsl = db * sl
            dy = dsl * ps
            gps_ref[:, cs] += _colsum(dsl * yb)
            gpb_ref[:, cs] += _colsum(dy)
            dy_bf = dy.astype(BF16)
            gpw_ref[grp] += _dot(pooled_bf, dy_bf, TN)
            r = _dot(dy_bf, pw, NT)
            zn = zn_ref[:, cs].astype(F32)
            dyn = (dbn_ref[:, cs].astype(F32) * (zn * _sigmoid(zn)) * ps * live_next).astype(BF16)
            rn = _dot(dyn, pw, NT) / _pool_counts((n + 1) * t, POOL_HALO, POOL_WINDOWS[grp])
            ext = jnp.concatenate([r / cnt, rn], axis=0)
            dxb = _window_sum(ext, grp + 1, True)[:t, :] - r
            dp_ref[:, 3 * D_MODEL + c0:3 * D_MODEL + c0 + POOL_GROUP_DIM] = dxb.astype(BF16)

    out_shape = [
        jax.ShapeDtypeStruct((s, EVEN_IN), BF16),
        jax.ShapeDtypeStruct((GMLP_HEADS, t, t), F32), jax.ShapeDtypeStruct((t, LANES), F32),
        jax.ShapeDtypeStruct((1, GMLP_HEAD_DIM), F32), jax.ShapeDtypeStruct((1, GMLP_HEAD_DIM), F32),
        jax.ShapeDtypeStruct((POOL_GROUPS, POOL_GROUP_DIM, POOL_GROUP_DIM), F32),
        jax.ShapeDtypeStruct((1, D_MODEL), F32), jax.ShapeDtypeStruct((1, D_MODEL), F32),
    ]
    return pl.pallas_call(
        body, name=name, grid=(nblk,),
        in_specs=[col(0), col(1), col(2), col(3), col(4), prev, nxt(4),
                  pl.BlockSpec((t, D_MODEL), lambda n: (n, 0)), pl.BlockSpec((t, D_MODEL), lambda n: (n, 1)), nxt(1),
                  _full((GMLP_HEADS, t, t)), _full((GMLP_HEADS, t, t)), _full((t, LANES)),
                  _full((1, GMLP_HEAD_DIM)), _full((1, GMLP_HEAD_DIM)),
                  _full((POOL_GROUPS, POOL_GROUP_DIM, POOL_GROUP_DIM)), _full((1, D_MODEL)), _full((1, D_MODEL))],
        out_specs=[pl.BlockSpec((t, EVEN_IN), lambda n: (n, 0))] + [_full(o.shape) for o in out_shape[1:]],
        out_shape=out_shape,
        compiler_params=_params("arbitrary"),
    )(proj, proj, proj, proj, proj, proj, proj, dmix, dmix, dmix, ws, ws_t, bs_t, ng, nb, pool_w, pool_b, pool_scale)


ROPE_HALF = MLA_ROPE // 2


def _rope(v, cos, sin_signed):
    return v * cos + pltpu.roll(v, 2 * ROPE_HALF, 1) * sin_signed


def _rope_bwd(d, cos, sin_signed):
    return d * cos + pltpu.roll(d * sin_signed, 2 * ROPE_HALF, 1)


def _slab_lanes(shape, which):
    lane = lax.broadcasted_iota(jnp.int32, shape, 1)
    return (lane // ROPE_HALF) % 2 == which


def _rms(v, g):
    r = lax.rsqrt(jnp.mean(v * v, axis=-1, keepdims=True) + LN_EPS)
    return v * r * g, r


def _rms_bwd(dy, v, r, g):
    u = dy * g
    return r * u - v * (r * r * r) * jnp.mean(u * v, axis=-1, keepdims=True)


def _mla_prep(proj, gq, gkv, cos, sin_signed, *, name):
    s = proj.shape[0]
    ts = _tile(s, 512)

    def body(p_ref, gq_ref, gkv_ref, c_ref, s_ref, q_ref, k_ref):
        qcn, _ = _rms(p_ref[:, :MLA_Q_RANK].astype(F32), gq_ref[...])
        kvn, _ = _rms(p_ref[:, MLA_Q_RANK:MLA_Q_RANK + MLA_KV_RANK].astype(F32), gkv_ref[...])
        kr = p_ref[:, MLA_Q_RANK + MLA_KV_RANK:].astype(F32)
        lane = lax.broadcasted_iota(jnp.int32, kr.shape, 1)
        by1, by2 = pltpu.roll(kr, ROPE_HALF, 1), pltpu.roll(kr, 2 * ROPE_HALF, 1)
        both = jnp.where(lane < ROPE_HALF, kr, jnp.where(lane < 3 * ROPE_HALF, by1, by2))
        kr = _rope(both, c_ref[...], s_ref[...])
        q_ref[...] = qcn.astype(BF16)
        k_ref[...] = jnp.concatenate([kvn, kr], axis=1).astype(BF16)

    return pl.pallas_call(
        body, name=name, grid=(s // ts,),
        in_specs=[_small_spec(ts), _vec_spec(MLA_Q_RANK), _vec_spec(MLA_KV_RANK), _row_spec(ts, LANES), _row_spec(ts, LANES)],
        out_specs=[_row_spec(ts, MLA_Q_RANK), _row_spec(ts, QK_PAD)],
        out_shape=[jax.ShapeDtypeStruct((s, MLA_Q_RANK), BF16), jax.ShapeDtypeStruct((s, QK_PAD), BF16)],
        compiler_params=_params("parallel"),
    )(proj, gq, gkv, cos, sin_signed)


def _mla_prep_bwd(proj, dqcn, dkv, gq, gkv, cos, sin_signed, *, name):
    s = proj.shape[0]
    ts = _tile(s, 512)

    def body(p_ref, dq_ref, dkv_ref, gq_ref, gkv_ref, c_ref, s_ref, ds_ref, ggq_ref, ggkv_ref):
        @pl.when(pl.program_id(0) == 0)
        def _():
            ggq_ref[...] = jnp.zeros_like(ggq_ref)
            ggkv_ref[...] = jnp.zeros_like(ggkv_ref)

        qc = p_ref[:, :MLA_Q_RANK].astype(F32)
        kvc = p_ref[:, MLA_Q_RANK:MLA_Q_RANK + MLA_KV_RANK].astype(F32)
        _, rq = _rms(qc, gq_ref[...])
        _, rkv = _rms(kvc, gkv_ref[...])
        dq = dq_ref[...]
        dkvn = dkv_ref[:, :MLA_KV_RANK]
        ggq_ref[...] += _colsum(dq * qc * rq)
        ggkv_ref[...] += _colsum(dkvn * kvc * rkv)
        dboth = _rope_bwd(dkv_ref[:, MLA_KV_RANK:], c_ref[...], s_ref[...])
        lane = lax.broadcasted_iota(jnp.int32, dboth.shape, 1)
        pair = dboth + pltpu.roll(dboth, 3 * ROPE_HALF, 1)
        dkr = jnp.where(lane < ROPE_HALF, pair, jnp.where(lane < 2 * ROPE_HALF, pltpu.roll(pair, 3 * ROPE_HALF, 1), 0.0))
        ds_ref[...] = jnp.concatenate(
            [_rms_bwd(dq, qc, rq, gq_ref[...]), _rms_bwd(dkvn, kvc, rkv, gkv_ref[...]), dkr], axis=1).astype(BF16)

    return pl.pallas_call(
        body, name=name, grid=(s // ts,),
        in_specs=[_small_spec(ts), _row_spec(ts, MLA_Q_RANK), _row_spec(ts, QK_PAD),
                  _vec_spec(MLA_Q_RANK), _vec_spec(MLA_KV_RANK), _row_spec(ts, LANES), _row_spec(ts, LANES)],
        out_specs=[_row_spec(ts, ODD_SMALL_PAD), _vec_spec(MLA_Q_RANK), _vec_spec(MLA_KV_RANK)],
        out_shape=[jax.ShapeDtypeStruct((s, ODD_SMALL_PAD), BF16), jax.ShapeDtypeStruct((1, MLA_Q_RANK), F32),
                   jax.ShapeDtypeStruct((1, MLA_KV_RANK), F32)],
        compiler_params=_params("arbitrary"),
    )(proj, dqcn, dkv, gq, gkv, cos, sin_signed)


Q_HEAD_GROUP = 8
LOG2_E = 1.4426950408889634
Q_PRESCALE = ATTN_SCALE * LOG2_E


def _q_build(q_nope, q_rope_pre, wuk_hdr, cos, sin_signed, *, name):
    s = q_nope.shape[0]
    ts = _tile(s, 512)
    hg = Q_HEAD_GROUP

    def body(qn_ref, qr_ref, w_ref, c_ref, s_ref, o_ref):
        for pair in range(hg // 2):
            r = _rope(qr_ref[:, pair * LANES:(pair + 1) * LANES], c_ref[...], s_ref[...])
            for j in range(2):
                h = 2 * pair + j
                ql = _dot(qn_ref[:, h * MLA_NOPE:(h + 1) * MLA_NOPE], w_ref[h], NN)
                mine = jnp.where(_slab_lanes(r.shape, j), r, 0.0)
                o_ref[h] = (jnp.concatenate([ql, mine], axis=1) * Q_PRESCALE).astype(BF16)

    return pl.pallas_call(
        body, name=name, grid=(s // ts, MLA_HEADS // hg),
        in_specs=[pl.BlockSpec((ts, hg * MLA_NOPE), lambda i, p: (i, p)), pl.BlockSpec((ts, hg * MLA_ROPE), lambda i, p: (i, p)),
                  pl.BlockSpec((hg, MLA_NOPE, MLA_KV_RANK), lambda i, p: (p, 0, 0)),
                  pl.BlockSpec((ts, LANES), lambda i, p: (i, 0)), pl.BlockSpec((ts, LANES), lambda i, p: (i, 0))],
        out_specs=pl.BlockSpec((hg, ts, QK_PAD), lambda i, p: (p, i, 0)),
        out_shape=jax.ShapeDtypeStruct((MLA_HEADS, s, QK_PAD), BF16),
        compiler_params=_params("parallel", "parallel"),
    )(q_nope, q_rope_pre, wuk_hdr, cos, sin_signed)


def _q_bwd(dq, q_nope, wuk_hrd, cos, sin_signed, *, name):
    s = q_nope.shape[0]
    ts = _tile(s, 512)
    hg = Q_HEAD_GROUP

    nope_w, all_w = hg * MLA_NOPE, hg * (MLA_NOPE + MLA_ROPE)

    def body(dq_ref, qn_ref, w_ref, c_ref, s_ref, dall_ref, gw_ref):
        @pl.when(pl.program_id(1) == 0)
        def _():
            gw_ref[...] = jnp.zeros_like(gw_ref)

        for h in range(hg):
            dql = dq_ref[h, :, :MLA_KV_RANK]
            dall_ref[:, h * MLA_NOPE:(h + 1) * MLA_NOPE] = _dot(dql, w_ref[h], NN).astype(BF16)
            gw_ref[h] += _dot(dql, qn_ref[:, h * MLA_NOPE:(h + 1) * MLA_NOPE], TN)
        for pair in range(hg // 2):
            hi0 = dq_ref[2 * pair, :, MLA_KV_RANK:].astype(F32)
            hi1 = dq_ref[2 * pair + 1, :, MLA_KV_RANK:].astype(F32)
            d = jnp.where(_slab_lanes(hi0.shape, 0), hi0, hi1)
            dall_ref[:, nope_w + pair * LANES:nope_w + (pair + 1) * LANES] = _rope_bwd(d, c_ref[...], s_ref[...]).astype(BF16)

    return pl.pallas_call(
        body, name=name, grid=(MLA_HEADS // hg, s // ts),
        in_specs=[pl.BlockSpec((hg, ts, QK_PAD), lambda p, i: (p, i, 0)), pl.BlockSpec((ts, nope_w), lambda p, i: (i, p)),
                  pl.BlockSpec((hg, MLA_KV_RANK, MLA_NOPE), lambda p, i: (p, 0, 0)),
                  pl.BlockSpec((ts, LANES), lambda p, i: (i, 0)), pl.BlockSpec((ts, LANES), lambda p, i: (i, 0))],
        out_specs=[pl.BlockSpec((ts, all_w), lambda p, i: (i, p)),
                   pl.BlockSpec((hg, MLA_KV_RANK, MLA_NOPE), lambda p, i: (p, 0, 0))],
        out_shape=[jax.ShapeDtypeStruct((s, MLA_HEADS * (MLA_NOPE + MLA_ROPE)), BF16),
                   jax.ShapeDtypeStruct((MLA_HEADS, MLA_KV_RANK, MLA_NOPE), F32)],
        compiler_params=_params("parallel", "arbitrary"),
    )(dq, q_nope, wuk_hrd, cos, sin_signed)


ATTN_BQ = 128
ATTN_BK = 512
ATTN_BK_FWD = 1024


def _diag_mask(rows, bq, bk, q0, k0):
    qc = (q0 + lax.broadcasted_iota(jnp.int32, (rows, bk), 0) % bq) // CHUNK
    kc = (k0 + lax.broadcasted_iota(jnp.int32, (rows, bk), 1)) // CHUNK
    return kc <= qc


def _attn_fwd(q, k, *, name):
    nh, s, dk = q.shape
    bq, bk = _tile(s, ATTN_BQ), _tile(s, ATTN_BK_FWD)
    rows = nh * bq

    def body(q_ref, k_ref, o_ref, lse_ref):
        i = pl.program_id(0)
        qb = q_ref[...].reshape(rows, dk)
        n_before = (i * bq) // bk

        def step(j, width, carry, masked):
            m, l, acc = carry
            k0 = pl.multiple_of(j * bk, bk)
            kb = k_ref[pl.ds(k0, width), :]
            sc = _dot(qb, kb, NT)
            if masked:
                sc = jnp.where(_diag_mask(rows, bq, width, i * bq, k0), sc, NEG)
            m_new = jnp.maximum(m, jnp.max(sc, axis=1, keepdims=True))
            p = jnp.exp2(sc - m_new)
            a = jnp.exp2(m - m_new)
            l = a * l + jnp.sum(p, axis=1, keepdims=True)
            acc = a * acc + _dot(p.astype(BF16), kb[:, :MLA_KV_RANK], NN)
            return m_new, l, acc

        init = (jnp.full((rows, 1), NEG, F32), jnp.zeros((rows, 1), F32), jnp.zeros((rows, MLA_KV_RANK), F32))
        carry = lax.fori_loop(0, n_before, lambda j, c: step(j, bk, c, False), init)
        for part in range(bk // bq):
            @pl.when(i % (bk // bq) == part)
            def _(part=part):
                m, l, acc = step(n_before, (part + 1) * bq, carry, True)
                o_ref[...] = (acc / l).astype(BF16).reshape(nh, bq, MLA_KV_RANK)
                lse_ref[...] = jnp.broadcast_to(m + jnp.log2(l), (rows, LANES)).reshape(nh, bq, LANES)

    return pl.pallas_call(
        body, name=name, grid=(s // bq,),
        in_specs=[pl.BlockSpec((nh, bq, dk), lambda i: (0, i, 0)), pl.BlockSpec((s, dk), lambda i: (0, 0))],
        out_specs=[pl.BlockSpec((nh, bq, MLA_KV_RANK), lambda i: (0, i, 0)), pl.BlockSpec((nh, bq, LANES), lambda i: (0, i, 0))],
        out_shape=[jax.ShapeDtypeStruct((nh, s, MLA_KV_RANK), BF16), jax.ShapeDtypeStruct((nh, s, LANES), F32)],
        compiler_params=_params("parallel"),
    )(q, k)


def _attn_bwd(q, k, do, o, lse, *, name):
    nh, s, dk = q.shape
    bq, bk = _tile(s, ATTN_BQ), _tile(s, ATTN_BK)
    rows = nh * bq

    def body(q_ref, k_ref, do_ref, o_ref, lse_ref, dq_ref, dkv_ref):
        i = pl.program_id(0)
        n_before = (i * bq) // bk

        @pl.when(i == 0)
        def _():
            dkv_ref[...] = jnp.zeros_like(dkv_ref)

        qb = q_ref[...].reshape(rows, dk)
        dob = do_ref[...].reshape(rows, MLA_KV_RANK)
        lse_b = lse_ref[...].reshape(rows, LANES)[:, :1]
        delta = jnp.sum(dob.astype(F32) * o_ref[...].reshape(rows, MLA_KV_RANK).astype(F32), axis=1, keepdims=True)

        def step(j, width, dq, masked):
            j0 = pl.multiple_of(j * bk, bk)
            kb = k_ref[pl.ds(j0, width), :]
            sc = _dot(qb, kb, NT)
            if masked:
                sc = jnp.where(_diag_mask(rows, bq, width, i * bq, j0), sc, NEG)
            p = jnp.exp2(sc - lse_b)
            dp = _dot(dob, kb[:, :MLA_KV_RANK], NT)
            ds_bf = (p * (dp - delta)).astype(BF16)
            dkv_ref[pl.ds(j0, width), :] += _dot(ds_bf, qb, TN) * (1.0 / LOG2_E)
            dkv_ref[pl.ds(j0, width), :MLA_KV_RANK] += _dot(p.astype(BF16), dob, TN)
            return dq + _dot(ds_bf, kb, NN)

        dq_before = lax.fori_loop(0, n_before, lambda j, c: step(j, bk, c, False), jnp.zeros((rows, dk), F32))
        for part in range(bk // bq):
            @pl.when(i % (bk // bq) == part)
            def _(part=part):
                dq = step(n_before, (part + 1) * bq, dq_before, True) * ATTN_SCALE
                dq_ref[...] = dq.astype(BF16).reshape(nh, bq, dk)

    blk = lambda w: pl.BlockSpec((nh, bq, w), lambda i: (0, i, 0))
    return pl.pallas_call(
        body, name=name, grid=(s // bq,),
        in_specs=[blk(dk), pl.BlockSpec((s, dk), lambda i: (0, 0)), blk(MLA_KV_RANK), blk(MLA_KV_RANK), blk(LANES)],
        out_specs=[blk(dk), pl.BlockSpec((s, dk), lambda i: (0, 0))],
        out_shape=[jax.ShapeDtypeStruct((nh, s, dk), BF16), jax.ShapeDtypeStruct((s, dk), F32)],
        compiler_params=_params("arbitrary"),
    )(q, k, do, o, lse)


HEAD_GROUP = 4
SMALL_BLOCK = MLA_WIDTH // ODD_SMALL_PAD


def _small_spec(ts):
    return pl.BlockSpec((ts, ODD_SMALL_PAD), lambda i: (i, SMALL_BLOCK))


def _o_build(o_lat, wuv_hrv, proj, *, name):
    s = proj.shape[0]
    ts = _tile(s, 1024)
    w = HEAD_GROUP * MLA_V

    def body(ol_ref, w_ref, z_ref, og_ref):
        for j in range(HEAD_GROUP):
            cs = slice(j * MLA_V, (j + 1) * MLA_V)
            z = z_ref[:, cs].astype(F32)
            og_ref[:, cs] = (_dot(ol_ref[j], w_ref[j], NN) * (z * _sigmoid(z))).astype(BF16)

    return pl.pallas_call(
        body, name=name, grid=(s // ts, MLA_HEADS // HEAD_GROUP),
        in_specs=[pl.BlockSpec((HEAD_GROUP, ts, MLA_KV_RANK), lambda i, g: (g, i, 0)),
                  pl.BlockSpec((HEAD_GROUP, MLA_KV_RANK, MLA_V), lambda i, g: (g, 0, 0)),
                  pl.BlockSpec((ts, w), lambda i, g: (i, g))],
        out_specs=pl.BlockSpec((ts, w), lambda i, g: (i, g)),
        out_shape=jax.ShapeDtypeStruct((s, MLA_WIDTH), BF16),
        compiler_params=_params("parallel", "parallel"),
    )(o_lat, wuv_hrv, proj)


def _o_bwd(dg, proj, o_lat, wuv_hrv, wuv_hvr, *, name):
    s = proj.shape[0]
    ts = _tile(s, 1024)
    w = HEAD_GROUP * MLA_V

    def body(dg_ref, z_ref, ol_ref, w_ref, wt_ref, dol_ref, dz_ref, gw_ref):
        @pl.when(pl.program_id(1) == 0)
        def _():
            gw_ref[...] = jnp.zeros_like(gw_ref)

        for j in range(HEAD_GROUP):
            cs = slice(j * MLA_V, (j + 1) * MLA_V)
            z, dgj, ol = z_ref[:, cs].astype(F32), dg_ref[:, cs].astype(F32), ol_ref[j]
            sg = _sigmoid(z)
            o = _dot(ol, w_ref[j], NN)
            dz_ref[:, cs] = (dgj * o * (sg * (1.0 + z * (1.0 - sg)))).astype(BF16)
            do_bf = (dgj * (z * sg)).astype(BF16)
            dol_ref[j] = _dot(do_bf, wt_ref[j], NN).astype(BF16)
            gw_ref[j] += _dot(ol, do_bf, TN)

    hs = lambda a, b: pl.BlockSpec((HEAD_GROUP, a, b), lambda g, i: (g, 0, 0))
    return pl.pallas_call(
        body, name=name, grid=(MLA_HEADS // HEAD_GROUP, s // ts),
        in_specs=[pl.BlockSpec((ts, w), lambda g, i: (i, g)), pl.BlockSpec((ts, w), lambda g, i: (i, g)),
                  pl.BlockSpec((HEAD_GROUP, ts, MLA_KV_RANK), lambda g, i: (g, i, 0)),
                  hs(MLA_KV_RANK, MLA_V), hs(MLA_V, MLA_KV_RANK)],
        out_specs=[pl.BlockSpec((HEAD_GROUP, ts, MLA_KV_RANK), lambda g, i: (g, i, 0)),
                   pl.BlockSpec((ts, w), lambda g, i: (i, g)), hs(MLA_KV_RANK, MLA_V)],
        out_shape=[jax.ShapeDtypeStruct((MLA_HEADS, s, MLA_KV_RANK), BF16), jax.ShapeDtypeStruct((s, MLA_WIDTH), BF16),
                   jax.ShapeDtypeStruct((MLA_HEADS, MLA_KV_RANK, MLA_V), F32)],
        compiler_params=_params("parallel", "arbitrary"),
    )(dg, proj, o_lat, wuv_hrv, wuv_hvr)


def _ada_mod(c_all, ada_w, ada_b_sh, *, name):
    nl, _, cols = ada_w.shape

    def body(c_ref, w_ref, b_ref, o_ref):
        c = c_ref[...]
        cond = (c * _sigmoid(c)).astype(BF16)
        for l in range(nl):
            o_ref[l] = _dot(cond, w_ref[l].astype(BF16), NN) + b_ref[l]

    return pl.pallas_call(
        body, name=name, out_shape=jax.ShapeDtypeStruct((nl, c_all.shape[0], cols), F32),
        compiler_params=_params(),
    )(c_all, ada_w, ada_b_sh)


def _ada_grad(c_all_t, dmod_sh, *, name):
    nl, _, cols = dmod_sh.shape
    d = c_all_t.shape[0]

    def body(c_ref, dm_ref, gw_ref):
        c = c_ref[...]
        cond_t = c * _sigmoid(c)
        for l in range(nl):
            gw_ref[l] = lax.dot_general(cond_t, dm_ref[l], (NN, ((), ())), precision=lax.Precision.HIGHEST,
                                        preferred_element_type=F32)

    return pl.pallas_call(
        body, name=name, out_shape=jax.ShapeDtypeStruct((nl, d, cols), F32), compiler_params=_params(),
    )(c_all_t, dmod_sh)


def _sum_devices(parts, *, name):
    def body(p_ref, o_ref):
        acc = p_ref[0]
        for k in range(1, parts.shape[0]):
            acc = acc + p_ref[k]
        o_ref[...] = acc

    return pl.pallas_call(body, name=name, out_shape=jax.ShapeDtypeStruct(parts.shape[1:], F32), compiler_params=_params())(parts)


def _adamw_math(w, g, m, v):
    c1 = 1.0 - ADAM_B1 ** ADAM_STEP
    c2 = 1.0 - ADAM_B2 ** ADAM_STEP
    nm = ADAM_B1 * m + (1.0 - ADAM_B1) * g
    nv = ADAM_B2 * v + (1.0 - ADAM_B2) * (g * g)
    return -ADAM_LR * ((nm / c1) / (jnp.sqrt(nv / c2) + ADAM_EPS) + ADAM_WD * w), nm, nv


ADAMW_BLOCK_BYTES = 1 << 20


def _adamw(w, g, m, v, *, name, after=None):
    shape = w.shape
    a, b = shape[-2], shape[-1]
    lead = 1
    for dim in shape[:-2]:
        lead *= dim
    row_bytes = 4 * b
    if a * row_bytes <= ADAMW_BLOCK_BYTES:
        ta = a
        tl = max(1, min(lead, ADAMW_BLOCK_BYTES // (a * row_bytes)))
        while lead % tl:
            tl -= 1
    else:
        tl = 1
        ta = _tile(a, 256)
    to3 = lambda t: t.reshape(lead, a, b)

    def body(w_ref, g_ref, m_ref, v_ref, *rest):
        d_ref, nm_ref, nv_ref = rest[-3:]
        d_ref[...], nm_ref[...], nv_ref[...] = _adamw_math(w_ref[...], g_ref[...], m_ref[...], v_ref[...])

    spec = pl.BlockSpec((tl, ta, b), lambda i, j: (i, j, 0))
    out = jax.ShapeDtypeStruct((lead, a, b), F32)
    order = [] if after is None else [after]
    res = pl.pallas_call(
        body, name=name, grid=(lead // tl, a // ta), in_specs=[spec] * 4 + [pl.BlockSpec(memory_space=pl.ANY)] * len(order),
        out_specs=[spec] * 3, out_shape=[out] * 3, compiler_params=_params("parallel", "parallel"),
    )(to3(w), to3(g), to3(m), to3(v), *order)
    return [r.reshape(shape) for r in res]


def _adamw_small(ws, gs, ms, vs, *, name):
    n = len(ws)

    def body(*refs):
        for k in range(n):
            w_ref, g_ref, m_ref, v_ref = (refs[j * n + k] for j in range(4))
            d_ref, nm_ref, nv_ref = (refs[(4 + j) * n + k] for j in range(3))
            d_ref[...], nm_ref[...], nv_ref[...] = _adamw_math(w_ref[...], g_ref[...], m_ref[...], v_ref[...])

    outs = [jax.ShapeDtypeStruct(w.shape, F32) for w in ws]
    res = pl.pallas_call(body, name=name, out_shape=outs * 3, compiler_params=_params())(*ws, *gs, *ms, *vs)
    return res[:n], res[n:2 * n], res[2 * n:]


def _flip(v, bit):
    return 1 - v if bit else v


CHIP_DELTAS = ((1, 0), (0, 1), (1, 1))
SUM_ROWS = 32


def _all_gather_chips(shard, *, name):
    def body(x_ref, o_ref, send_sems, recv_sems, local_sem):
        x, y, c = lax.axis_index("x"), lax.axis_index("y"), lax.axis_index("c")
        mine = pltpu.make_async_copy(x_ref, o_ref.at[2 * x + y], local_sem)
        mine.start()

        def copy(k):
            tx, ty = _flip(x, CHIP_DELTAS[k][0]), _flip(y, CHIP_DELTAS[k][1])
            send = pltpu.make_async_remote_copy(src_ref=x_ref, dst_ref=o_ref.at[2 * x + y], send_sem=send_sems.at[k],
                                                recv_sem=recv_sems.at[k], device_id=(tx, ty, c), device_id_type=MESH)
            recv = pltpu.make_async_remote_copy(src_ref=x_ref, dst_ref=o_ref.at[2 * tx + ty], send_sem=send_sems.at[k],
                                                recv_sem=recv_sems.at[k], device_id=(tx, ty, c), device_id_type=MESH)
            return send, recv

        pairs = [copy(k) for k in range(3)]
        for send, _ in pairs:
            send.start()
        for _, recv in pairs:
            recv.wait_recv()
        for send, _ in pairs:
            send.wait_send()
        mine.wait()

    return pl.pallas_call(
        body, name=name, out_shape=jax.ShapeDtypeStruct((N_CHIPS,) + shard.shape, shard.dtype),
        in_specs=[HBM], out_specs=HBM,
        scratch_shapes=[pltpu.SemaphoreType.DMA((3,)), pltpu.SemaphoreType.DMA((3,)), pltpu.SemaphoreType.DMA(())],
    )(shard)


def _gather_weights(shards, *, name):
    n = len(shards)

    def body(*refs):
        w_refs, o_refs = refs[:n], refs[n:2 * n]
        ici_send, ici_recv, d2d_send, d2d_recv, local_sems = refs[2 * n:]
        x, y, c = lax.axis_index("x"), lax.axis_index("y"), lax.axis_index("c")
        me = 2 * x + y
        peers = [(_flip(x, dx), _flip(y, dy)) for dx, dy in CHIP_DELTAS]
        locals_ = [pltpu.make_async_copy(w_refs[k], o_refs[k].at[me], local_sems.at[k]) for k in range(n)]
        for cp in locals_:
            cp.start()

        def rows(k, which):
            half = shards[k].shape[0] // 2
            return pl.ds(pl.multiple_of(which * half, half), half)

        def over_chips(k, d, slot):
            tx, ty = peers[d]
            return pltpu.make_async_remote_copy(
                src_ref=w_refs[k].at[rows(k, c)], dst_ref=o_refs[k].at[slot, rows(k, c)], send_sem=ici_send.at[k, d],
                recv_sem=ici_recv.at[k, d], device_id=(tx, ty, c), device_id_type=MESH)

        def to_sibling(k, d, which):
            tx, ty = peers[d]
            at = o_refs[k].at[2 * tx + ty, rows(k, which)]
            return pltpu.make_async_remote_copy(src_ref=at, dst_ref=at, send_sem=d2d_send.at[k, d], recv_sem=d2d_recv.at[k, d],
                                                device_id=(x, y, 1 - c), device_id_type=MESH)

        sends = [over_chips(k, d, me) for k in range(n) for d in range(3)]
        for cp in sends:
            cp.start()
        passed = []
        for k in range(n):
            for d in range(3):
                over_chips(k, d, 2 * peers[d][0] + peers[d][1]).wait_recv()
                passed.append(to_sibling(k, d, c))
                passed[-1].start()
        for k in range(n):
            for d in range(3):
                to_sibling(k, d, 1 - c).wait_recv()
        for cp in sends + passed:
            cp.wait_send()
        for cp in locals_:
            cp.wait()

    return pl.pallas_call(
        body, name=name, out_shape=[jax.ShapeDtypeStruct((N_CHIPS,) + w.shape, w.dtype) for w in shards],
        in_specs=[HBM] * n, out_specs=[HBM] * n,
        scratch_shapes=[pltpu.SemaphoreType.DMA((n, 3))] * 4 + [pltpu.SemaphoreType.DMA((n,))],
    )(*shards)


def _add_into(dst_ref, src_ref):
    ns, r, _ = dst_ref.shape
    step = SUM_ROWS if r % SUM_ROWS == 0 else r
    for s in range(ns):
        def tile(t, carry):
            at = pl.ds(pl.multiple_of(t * step, step), step)
            dst_ref[s, at, :] = (dst_ref[s, at, :].astype(F32) + src_ref[s, at, :].astype(F32)).astype(dst_ref.dtype)
            return carry
        lax.fori_loop(0, r // step, tile, 0)


def _reduce_sibling(grads, *, name):
    n = len(grads)

    def body(*refs):
        g_refs, o_refs = refs[:n], refs[n:2 * n]
        mine, got = refs[2 * n:3 * n], refs[3 * n:4 * n]
        send_sems, recv_sems, load_sems, store_sems = refs[4 * n:]
        x, y, c = lax.axis_index("x"), lax.axis_index("y"), lax.axis_index("c")
        loads = [pltpu.make_async_copy(g_refs[k].at[:, c], mine[k], load_sems.at[k]) for k in range(n)]
        swaps = [pltpu.make_async_remote_copy(src_ref=g_refs[k].at[:, 1 - c], dst_ref=got[k], send_sem=send_sems.at[k],
                                              recv_sem=recv_sems.at[k], device_id=(x, y, 1 - c), device_id_type=MESH)
                 for k in range(n)]
        for cp in loads + swaps:
            cp.start()
        stores = []
        for k in range(n):
            loads[k].wait()
            swaps[k].wait_recv()
            _add_into(mine[k], got[k])
            stores.append(pltpu.make_async_copy(mine[k], o_refs[k], store_sems.at[k]))
            stores[-1].start()
        for k in range(n):
            swaps[k].wait_send()
            stores[k].wait()

    half = [jax.ShapeDtypeStruct((g.shape[0],) + g.shape[2:], g.dtype) for g in grads]
    return pl.pallas_call(
        body, name=name, out_shape=half, in_specs=[HBM] * n, out_specs=[HBM] * n,
        scratch_shapes=[pltpu.VMEM(h.shape, h.dtype) for h in half] * 2 + [pltpu.SemaphoreType.DMA((n,))] * 4,
        compiler_params=_params(),
    )(*grads)


def _reduce_chips(parts, landed, *, name):
    n = len(parts)

    def body(*refs):
        p_refs, l_refs, o_refs = refs[:n], refs[n:2 * n], refs[2 * n:3 * n]
        got, total = refs[3 * n:4 * n], refs[4 * n:5 * n]
        load_sems, share_send, share_recv, store_sems = refs[5 * n:]
        x, y, c = lax.axis_index("x"), lax.axis_index("y"), lax.axis_index("c")
        me = 2 * x + y
        slots = [me] + [2 * _flip(x, dx) + _flip(y, dy) for dx, dy in CHIP_DELTAS]
        loads = [[pltpu.make_async_copy((p_refs if j == 0 else l_refs)[k].at[slot], got[k].at[slot], load_sems.at[k, j])
                  for j, slot in enumerate(slots)] for k in range(n)]
        for per_array in loads:
            for cp in per_array:
                cp.start()
        shares, stores = [], []
        for k in range(n):
            for cp in loads[k]:
                cp.wait()
            r = total[k].shape[0]
            step = SUM_ROWS if r % SUM_ROWS == 0 else r

            def tile(t, carry, k=k, step=step):
                at = pl.ds(pl.multiple_of(t * step, step), step)
                acc = got[k][0, at, :].astype(F32)
                for s in range(1, N_CHIPS):
                    acc = acc + got[k][s, at, :].astype(F32)
                total[k][at, :] = acc
                return carry

            lax.fori_loop(0, r // step, tile, 0)
            stores.append(pltpu.make_async_copy(total[k], o_refs[k].at[c], store_sems.at[k]))
            shares.append(pltpu.make_async_remote_copy(
                src_ref=total[k], dst_ref=o_refs[k].at[c], send_sem=share_send.at[k], recv_sem=share_recv.at[k],
                device_id=(x, y, 1 - c), device_id_type=MESH))
            stores[-1].start()
            shares[-1].start()
        for k in range(n):
            pltpu.make_async_remote_copy(
                src_ref=total[k], dst_ref=o_refs[k].at[1 - c], send_sem=share_send.at[k], recv_sem=share_recv.at[k],
                device_id=(x, y, 1 - c), device_id_type=MESH).wait_recv()
        for cp in shares:
            cp.wait_send()
        for cp in stores:
            cp.wait()

    return pl.pallas_call(
        body, name=name, out_shape=[jax.ShapeDtypeStruct((2,) + p.shape[1:], F32) for p in parts],
        in_specs=[HBM] * (2 * n), out_specs=[HBM] * n,
        scratch_shapes=[pltpu.VMEM(p.shape, p.dtype) for p in parts] + [pltpu.VMEM(p.shape[1:], F32) for p in parts]
        + [pltpu.SemaphoreType.DMA((n, N_CHIPS))] + [pltpu.SemaphoreType.DMA((n,))] * 3,
        compiler_params=_params(),
    )(*parts, *landed)


SEM = pl.BlockSpec(memory_space=pltpu.SEMAPHORE)
IN_FLIGHT = pltpu.SideEffectType.DATAFLOW_SIDE_EFFECTING


def _chip_copies(s_refs, l_refs, sems, scatter, theirs):
    x, y, c = lax.axis_index("x"), lax.axis_index("y"), lax.axis_index("c")
    me = 2 * x + y
    copies = []
    for k in range(len(s_refs)):
        for d, (dx, dy) in enumerate(CHIP_DELTAS):
            tx, ty = _flip(x, dx), _flip(y, dy)
            peer = 2 * tx + ty
            send_sem, recv_sem = sems[2 * (3 * k + d)], sems[2 * (3 * k + d) + 1]
            copies.append(pltpu.make_async_remote_copy(
                src_ref=s_refs[k].at[peer] if scatter else s_refs[k], dst_ref=l_refs[k].at[peer if theirs else me],
                send_sem=send_sem, recv_sem=recv_sem, device_id=(tx, ty, c), device_id_type=MESH))
    return copies


def _chips_start(srcs, lands, after, *, scatter, name):
    n = len(srcs)
    n_sem = 2 * 3 * n

    def body(*refs):
        s_refs, l_refs = refs[:n], refs[n:2 * n]
        sems = refs[2 * n + 1:2 * n + 1 + n_sem]
        token = refs[-1]
        for cp in _chip_copies(s_refs, l_refs, sems, scatter, False):
            cp.start()
        token[...] = jnp.zeros_like(token)

    hbm = lambda a: pltpu.HBM(a.shape, a.dtype)
    res = pl.pallas_call(
        body, name=name,
        out_shape=(*[pltpu.SemaphoreType.DMA(())] * n_sem, *[hbm(a) for a in srcs], *[hbm(a) for a in lands],
                   jax.ShapeDtypeStruct((8, LANES), F32)),
        in_specs=[HBM] * (2 * n) + [pl.BlockSpec(memory_space=pl.ANY)],
        out_specs=(*[SEM] * n_sem, *[HBM] * (2 * n), VMEM),
        input_output_aliases={k: n_sem + k for k in range(2 * n)},
        compiler_params=pltpu.CompilerParams(has_side_effects=IN_FLIGHT),
    )(*[pltpu.with_memory_space_constraint(a, pltpu.HBM) for a in list(srcs) + list(lands)], after)
    return res[:n_sem], res[n_sem:n_sem + n], res[n_sem + n:n_sem + 2 * n], res[-1]


def _chips_wait(sems, srcs, lands, after, *, scatter, name):
    n = len(srcs)
    n_sem = len(sems)

    def body(*refs):
        s_refs, l_refs = refs[:n], refs[n:2 * n]
        sem_refs = refs[2 * n:2 * n + n_sem]
        for cp in _chip_copies(s_refs, l_refs, sem_refs, scatter, False):
            cp.wait_send()
        for cp in _chip_copies(s_refs, l_refs, sem_refs, scatter, True):
            cp.wait_recv()

    hbm = lambda a: pltpu.HBM(a.shape, a.dtype)
    res = pl.pallas_call(
        body, name=name, out_shape=tuple(hbm(a) for a in list(srcs) + list(lands)),
        in_specs=[HBM] * (2 * n) + [SEM] * n_sem + [pl.BlockSpec(memory_space=pl.ANY)], out_specs=tuple([HBM] * (2 * n)),
        input_output_aliases={k: k for k in range(2 * n)},
        compiler_params=pltpu.CompilerParams(has_side_effects=IN_FLIGHT),
    )(*srcs, *lands, *sems, after)
    return res[:n], res[n:]


def _all_gather_devices(rows, *, name, after=None):
    deltas = [(dx, dy, dc) for dx in (0, 1) for dy in (0, 1) for dc in (0, 1)][1:]
    order = [] if after is None else [after]

    def body(x_ref, *rest):
        o_ref, send_sems, recv_sems = rest[-3:]
        x, y, c = lax.axis_index("x"), lax.axis_index("y"), lax.axis_index("c")
        me = 4 * x + 2 * y + c
        o_ref[me] = x_ref[...]
        sends, recvs = [], []
        for k, (dx, dy, dc) in enumerate(deltas):
            tx, ty, tc = _flip(x, dx), _flip(y, dy), _flip(c, dc)
            sends.append(pltpu.make_async_remote_copy(src_ref=x_ref, dst_ref=o_ref.at[me], send_sem=send_sems.at[k],
                                                      recv_sem=recv_sems.at[k], device_id=(tx, ty, tc), device_id_type=MESH))
            recvs.append(pltpu.make_async_remote_copy(src_ref=x_ref, dst_ref=o_ref.at[4 * tx + 2 * ty + tc],
                                                      send_sem=send_sems.at[k], recv_sem=recv_sems.at[k],
                                                      device_id=(tx, ty, tc), device_id_type=MESH))
        for cp in sends:
            cp.start()
        for cp in recvs:
            cp.wait_recv()
        for cp in sends:
            cp.wait_send()

    return pl.pallas_call(
        body, name=name, out_shape=jax.ShapeDtypeStruct((N_DEV,) + rows.shape, rows.dtype),
        in_specs=[VMEM] + [pl.BlockSpec(memory_space=pl.ANY)] * len(order), out_specs=VMEM,
        scratch_shapes=[pltpu.SemaphoreType.DMA((N_DEV - 1,)), pltpu.SemaphoreType.DMA((N_DEV - 1,))],
    )(rows, *order)


WEIGHTS = ("ada_w", "ada_b", "ln_g", "ln_b", "e_w_in", "gmlp_norm_g", "gmlp_norm_b", "gmlp_ws", "gmlp_bs", "pool_w",
           "pool_b", "pool_scale", "e_w_out", "o_w_in", "mla_q_norm_g", "mla_kv_norm_g", "mla_w_uq", "mla_w_uk",
           "mla_w_uv", "o_w_out")
SMALL = ("ln_g", "ln_b", "gmlp_norm_g", "gmlp_norm_b", "gmlp_bs", "pool_b", "pool_scale", "mla_kv_norm_g", "mla_q_norm_g")


def _pad_cols(v, n):
    return jnp.concatenate([v, jnp.zeros((v.shape[0], n - v.shape[1]), v.dtype)], axis=1) if n > v.shape[1] else v


def _halves(g):
    return g.reshape(g.shape[0], 2, g.shape[1] // 2, g.shape[2])


def kernel(x, c, positions, ada_w, ada_b, ln_g, ln_b, e_w_in, gmlp_norm_g, gmlp_norm_b, gmlp_ws, gmlp_bs, pool_w, pool_b, pool_scale, e_w_out, o_w_in, mla_q_norm_g, mla_kv_norm_g, mla_w_uq, mla_w_uk, mla_w_uv, o_w_out, loss_target, m_ada_w, m_ada_b, m_ln_g, m_ln_b, m_e_w_in, m_gmlp_norm_g, m_gmlp_norm_b, m_gmlp_ws, m_gmlp_bs, m_pool_w, m_pool_b, m_pool_scale, m_e_w_out, m_o_w_in, m_mla_q_norm_g, m_mla_kv_norm_g, m_mla_w_uq, m_mla_w_uk, m_mla_w_uv, m_o_w_out, v_ada_w, v_ada_b, v_ln_g, v_ln_b, v_e_w_in, v_gmlp_norm_g, v_gmlp_norm_b, v_gmlp_ws, v_gmlp_bs, v_pool_w, v_pool_b, v_pool_scale, v_e_w_out, v_o_w_in, v_mla_q_norm_g, v_mla_kv_norm_g, v_mla_w_uq, v_mla_w_uk, v_mla_w_uv, v_o_w_out):
    args = dict(locals())
    weights = {n: args[n] for n in WEIGHTS}
    mom = {n: args["m_" + n] for n in WEIGHTS}
    var = {n: args["v_" + n] for n in WEIGHTS}
    ax, ay, ac = lax.axis_index("x"), lax.axis_index("y"), lax.axis_index("c")
    chip = 2 * ax + ay
    dev = 2 * chip + ac
    d = D_MODEL
    x2 = x[0]
    target = loss_target[0]
    q_rank_sh = mla_q_norm_g.shape[1]

    empty_zone = lambda w: lax.dynamic_update_slice(lax.empty((N_CHIPS,) + w.shape, w.dtype), w[None], (chip, 0, 0))
    shards0 = [w.astype(BF16) for w in (pool_w[0].reshape(-1, POOL_GROUP_DIM), e_w_out[0])]
    shards1 = [w.astype(BF16) for w in (o_w_in[0], mla_w_uq[0].reshape(q_rank_sh, -1), o_w_out[0])]
    w_in0, = _gather_weights([e_w_in[0].astype(BF16)], name="gather_weights")
    wuk_hrd = jnp.transpose(mla_w_uk[0], (1, 0, 2)).astype(BF16)
    wuk_hdr = jnp.transpose(mla_w_uk[0], (1, 2, 0)).astype(BF16)
    wuv_hrv = jnp.transpose(mla_w_uv[0], (1, 0, 2)).astype(BF16)
    wuv_hvr = jnp.transpose(mla_w_uv[0], (1, 2, 0)).astype(BF16)
    ws = gmlp_ws[0]
    ws_t = jnp.transpose(ws, (0, 2, 1))
    bs_t = _pad_cols(gmlp_bs[0].T, LANES)

    inv = 1.0 / (ROPE_THETA ** (jnp.arange(0, MLA_ROPE, 2, dtype=F32) / MLA_ROPE))
    ang = positions[0].astype(F32)[:, None] * inv
    cos_t = jnp.tile(jnp.cos(ang), (1, 4))
    sin_t = jnp.concatenate([-jnp.sin(ang), -jnp.sin(ang), jnp.sin(ang), jnp.sin(ang)], axis=1)

    c_all = _all_gather_devices(c.reshape(8, LANES), after=w_in0, name="gather_c").reshape(N_DEV, d)
    cols = ada_w.shape[2]
    ada_b_mine = lax.dynamic_slice_in_dim(ada_b, chip * cols, cols, axis=1)[:, None, :]
    mod_sh = _ada_mod(c_all, ada_w, ada_b_mine, name="ada_mod")
    q_norm_rows = jnp.zeros((8, cols), F32).at[0, :q_rank_sh].set(mla_q_norm_g[0])
    mod_all = _all_gather_chips(jnp.concatenate([mod_sh.reshape(2 * N_DEV, cols), q_norm_rows]), name="gather_mod")
    q_norm_g = mod_all[:, 2 * N_DEV, :q_rank_sh].reshape(1, -1)
    mod_all = jnp.transpose(mod_all[:, :2 * N_DEV].reshape(N_CHIPS, 2, N_DEV, cols), (1, 2, 0, 3)).reshape(2, N_DEV, 3 * d)
    mod = lax.dynamic_index_in_dim(mod_all, dev, axis=1, keepdims=False)
    shift = [mod[l:l + 1, :d] for l in range(2)]
    scale = [mod[l:l + 1, d:2 * d] for l in range(2)]
    gate = [mod[l:l + 1, 2 * d:] for l in range(2)]
    flight0 = _chips_start(shards0, [empty_zone(w) for w in shards0], mod, scatter=False, name="gather0_start")
    flight1 = _chips_start(shards1, [empty_zone(w) for w in shards1], flight0[3], scatter=False, name="gather1_start")

    scale[0] = scale[0] + flight1[3][:1, :1]
    h0 = _modulate(x2, scale[0], shift[0], name="modulate0")
    proj0 = _matmul(h0, w_in0, b_stacked=True, tm=1024, tn=1280, out_dtype=BF16, name="proj0")
    pool_w_g, w_out0 = _chips_wait(*flight0[:3], proj0, scatter=False, name="gather0_wait")[1]
    pool_w_bf = jnp.transpose(pool_w_g.reshape(N_CHIPS, POOL_GROUPS, -1, POOL_GROUP_DIM), (1, 0, 2, 3)).reshape(
        POOL_GROUPS, POOL_GROUP_DIM, POOL_GROUP_DIM)
    w_out0 = w_out0.reshape(-1, d)
    mix0 = _even_fwd(proj0, ws, bs_t, gmlp_norm_g, gmlp_norm_b, pool_w_bf, pool_b, pool_scale, name="even_fwd")
    y0, x1, h1 = _out_resid_ln(mix0, w_out0, x2, gate[0], ln_g[0:1], ln_b[0:1], scale[1], shift[1], name="out0_ln")

    w_in1_g, w_uq_g, w_out1 = _chips_wait(*flight1[:3], h1, scatter=False, name="gather1_wait")[1]
    w_out1 = w_out1.reshape(-1, d)
    w_in1 = jnp.transpose(w_in1_g, (1, 0, 2)).reshape(d, ODD_IN)
    w_in1 = jnp.concatenate([w_in1[:, ODD_SMALL:], _pad_cols(w_in1[:, :ODD_SMALL], ODD_SMALL_PAD)], axis=1)
    w_uq = w_uq_g.reshape(MLA_Q_RANK, MLA_HEADS, MLA_NOPE + MLA_ROPE)
    w_uq_nope = w_uq[:, :, :MLA_NOPE].reshape(MLA_Q_RANK, -1)
    w_uq_rope = jnp.transpose(w_uq[:, :, MLA_NOPE:].reshape(MLA_Q_RANK, MLA_HEADS // 2, 2, 2, ROPE_HALF),
                              (0, 1, 3, 2, 4)).reshape(MLA_Q_RANK, -1)
    proj1 = _matmul(h1, w_in1, tm=1024, tn=1280, out_dtype=BF16, name="proj1")
    q_cn, keys = _mla_prep(proj1, q_norm_g, mla_kv_norm_g, cos_t, sin_t, name="mla_prep")
    q_nope = _matmul(q_cn, w_uq_nope, tm=1024, tn=2048, name="q_nope", out_dtype=BF16)
    q_rope_pre = _matmul(q_cn, w_uq_rope, tm=1024, name="q_rope")
    q = _q_build(q_nope, q_rope_pre, wuk_hdr, cos_t, sin_t, name="q_build")
    o_lat, lse = _attn_fwd(q, keys, name="attn_fwd")
    og = _o_build(o_lat, wuv_hrv, proj1, name="o_build")

    dy1, dres1, g_ln_g1, g_ln_b1, dgate1, loss = _out_loss_ln_bwd(
        og, w_out1, x1, gate[1], ln_g[1:2], ln_b[1:2], target, name="out1_loss_ln")
    dg1 = _matmul(dy1, w_out1, trans_b=True, tn=2048, out_dtype=BF16, name="d_og")
    g_w_out1 = _matmul(og, dy1, trans_a=True, out_dtype=BF16, tm=1024, tk=4096, name="g_out1")
    do_lat, dz, g_uv = _o_bwd(dg1, proj1, o_lat, wuv_hrv, wuv_hvr, name="o_bwd")
    dq, dkeys = _attn_bwd(q, keys, do_lat, o_lat, lse, name="attn_bwd")
    dq_all, g_uk = _q_bwd(dq, q_nope, wuk_hrd, cos_t, sin_t, name="q_bwd")
    n_grp = MLA_HEADS // Q_HEAD_GROUP
    w_uq_all = jnp.concatenate([w_uq_nope.reshape(MLA_Q_RANK, n_grp, -1), w_uq_rope.reshape(MLA_Q_RANK, n_grp, -1)],
                               axis=2).reshape(MLA_Q_RANK, -1)
    dq_cn = _matmul(dq_all, w_uq_all, trans_b=True, tm=1024, tk=3072, name="d_qcn")
    g_uq_all = _matmul(q_cn, dq_all, trans_a=True, out_dtype=BF16, name="g_uq").reshape(MLA_Q_RANK, n_grp, -1)
    g_uq_nope = g_uq_all[:, :, :Q_HEAD_GROUP * MLA_NOPE].reshape(MLA_Q_RANK, -1)
    g_uq_rope = g_uq_all[:, :, Q_HEAD_GROUP * MLA_NOPE:].reshape(MLA_Q_RANK, -1)
    dsmall, g_qg, g_kvg = _mla_prep_bwd(proj1, dq_cn, dkeys, q_norm_g, mla_kv_norm_g, cos_t, sin_t, name="mla_prep_bwd")
    dproj1 = jnp.concatenate([dz, dsmall], axis=1)
    g_w_in1 =_matmul(h1, dproj1, trans_a=True, out_dtype=BF16, tm=1024, tn=1280, name="g_in1")

    g_uq_rope = jnp.transpose(g_uq_rope.reshape(MLA_Q_RANK, MLA_HEADS // 2, 2, 2, ROPE_HALF), (0, 1, 3, 2, 4))
    g_uq = jnp.concatenate([g_uq_nope.reshape(MLA_Q_RANK, MLA_HEADS, MLA_NOPE), g_uq_rope.reshape(MLA_Q_RANK, MLA_HEADS, MLA_ROPE)], axis=2)
    g_w_in1 = jnp.concatenate([g_w_in1[:, MLA_WIDTH:MLA_WIDTH + ODD_SMALL], g_w_in1[:, :MLA_WIDTH]], axis=1)
    g_w_in1 = jnp.transpose(g_w_in1.reshape(d, N_CHIPS, -1), (1, 0, 2))
    big1 = [
        _halves(g_w_in1),
        _halves(g_uq.reshape(N_CHIPS, q_rank_sh, -1)),
        _halves(g_w_out1.reshape(N_CHIPS, -1, d)),
        _halves(g_uk.astype(BF16).reshape(N_CHIPS, -1, MLA_NOPE)),
        _halves(g_uv.astype(BF16).reshape(N_CHIPS, -1, MLA_V)),
    ]
    parts1 = _reduce_sibling(big1, name="reduce_sibling1")
    flight2 = _chips_start(parts1, [lax.empty(p.shape, BF16) for p in parts1], loss, scatter=True, name="reduce1_start")

    gate[0] = gate[0] + flight2[3][:1, :1]
    dy0, dres0, g_ln_g0, g_ln_b0, dgate0, dscale1, dshift1 = _dh_mid_ln_bwd(
        dproj1, w_in1, x2, y0, gate[0], ln_g[0:1], ln_b[0:1], dres1, scale[1], x1, name="d_h1_mid_ln")
    dmix0 = _matmul(dy0, w_out0, trans_b=True, tn=2048, out_dtype=BF16, name="d_mix0")
    g_w_out0 = _matmul(mix0, dy0, trans_a=True, out_dtype=BF16, tm=1024, tk=4096, name="g_out0")
    dproj0, g_ws, g_bs_t, g_ng, g_nb, g_pw, g_pb, g_ps = _even_bwd(
        proj0, dmix0, ws, ws_t, bs_t, gmlp_norm_g, gmlp_norm_b, pool_w_bf, pool_b, pool_scale, name="even_bwd")
    g_w_in0 = _matmul(h0, dproj0, trans_a=True, out_dtype=BF16, out_stacked=True, tm=1024, tn=1280, name="g_in0")

    g_pw = jnp.transpose(g_pw.astype(BF16).reshape(POOL_GROUPS, N_CHIPS, -1, POOL_GROUP_DIM), (1, 0, 2, 3))
    big0 = [
        _halves(g_w_in0),
        _halves(g_pw.reshape(N_CHIPS, -1, POOL_GROUP_DIM)),
        _halves(g_w_out0.reshape(N_CHIPS, -1, d)),
        _halves(g_ws.astype(BF16)),
    ]
    parts0 = _reduce_sibling(big0, name="reduce_sibling0")
    parts1, landed1 = _chips_wait(*flight2[:3], parts0[0], scatter=True, name="reduce1_wait")
    flight3 = _chips_start(parts0, [lax.empty(p.shape, BF16) for p in parts0], landed1[0], scatter=True, name="reduce0_start")
    grad_x, dscale0, dshift0 = _dh_input_bwd(dproj0, w_in0, x2, dres0, scale[0], after=flight3[3], name="d_h0_input")

    small_local = {
        "ln_g": jnp.concatenate([g_ln_g0, g_ln_g1]), "ln_b": jnp.concatenate([g_ln_b0, g_ln_b1]),
        "gmlp_norm_g": g_ng, "gmlp_norm_b": g_nb, "gmlp_bs": g_bs_t[:, :GMLP_HEADS].T, "pool_b": g_pb, "pool_scale": g_ps,
        "mla_kv_norm_g": g_kvg, "mla_q_norm_g": g_qg,
    }
    n_mod = 2 * 3 * d
    vec = jnp.concatenate([dshift0, dscale0, dgate0, dshift1, dscale1, dgate1]
                          + [small_local[n].reshape(1, -1) for n in SMALL] + [loss], axis=1)
    n_vec = vec.shape[1]
    vec = _pad_cols(vec, -(-n_vec // (8 * LANES)) * 8 * LANES).reshape(-1, LANES)
    vec_all = _all_gather_devices(vec, name="gather_small")
    vec_sum = _sum_devices(vec_all, name="sum_small").reshape(-1)
    dmod_all = vec_all.reshape(N_DEV, -1)[:, :n_mod].reshape(N_DEV, 2, 3 * d)
    dmod_sh = jnp.transpose(lax.dynamic_slice_in_dim(dmod_all, chip * cols, cols, axis=2), (1, 0, 2))
    dmod_sh = jnp.concatenate([dmod_sh, jnp.zeros((2, LANES - N_DEV, cols), F32)], axis=1)
    grads = {"ada_w": _ada_grad(_pad_cols(c_all.T, LANES), dmod_sh, name="ada_grad"), "ada_b": vec_sum[:n_mod].reshape(2, 3 * d)}
    off = n_mod
    for n in SMALL:
        sz = small_local[n].size
        grads[n] = vec_sum[off:off + sz]
        off += sz
    grads["mla_q_norm_g"] = lax.dynamic_slice_in_dim(grads["mla_q_norm_g"], chip * q_rank_sh, q_rank_sh)
    for n in SMALL:
        grads[n] = grads[n].reshape(weights[n].shape)

    parts0, landed0 = _chips_wait(*flight3[:3], grads["ada_w"], scatter=True, name="reduce0_wait")
    totals = _reduce_chips(list(parts0) + list(parts1), list(landed0) + list(landed1), name="reduce_chips")
    for n, t in zip(("e_w_in", "pool_w", "e_w_out", "gmlp_ws", "o_w_in", "mla_w_uq", "o_w_out"), totals):
        if n != "gmlp_ws":
            grads[n] = t.reshape(weights[n].shape)
    rep = jnp.concatenate([t.reshape(-1, LANES) for t in (totals[3], totals[7], totals[8])])
    rep_land = lax.dynamic_update_slice(lax.empty((N_CHIPS,) + rep.shape, F32), rep[None], (chip, 0, 0))
    flight4 = _chips_start([rep], [rep_land], totals[0], scatter=False, name="gather_rep_start")

    delta, new_m, new_v = {}, {}, {}
    replicated = ("gmlp_ws", "mla_w_uk", "mla_w_uv")
    large = [n for n in WEIGHTS if n not in SMALL and n != "ada_b"]
    for n in large:
        if n not in replicated:
            delta[n], new_m[n], new_v[n] = _adamw(weights[n], grads[n], mom[n], var[n], after=flight4[3], name="adamw_" + n)
    rep = _chips_wait(*flight4[:3], delta["e_w_in"], scatter=False, name="gather_rep_wait")[1][0]
    r_ws, r_uk = GMLP_BLOCK, 4 * MLA_KV_RANK
    grads["gmlp_ws"] = rep[:, :r_ws].reshape(weights["gmlp_ws"].shape)
    grads["mla_w_uk"] = jnp.transpose(rep[:, r_ws:r_ws + r_uk].reshape(MLA_HEADS, MLA_KV_RANK, MLA_NOPE), (1, 0, 2))[None]
    grads["mla_w_uv"] = jnp.transpose(rep[:, r_ws + r_uk:].reshape(MLA_HEADS, MLA_KV_RANK, MLA_V), (1, 0, 2))[None]
    for n in replicated:
        delta[n], new_m[n], new_v[n] = _adamw(weights[n], grads[n], mom[n], var[n], name="adamw_" + n)
    small = [n for n in WEIGHTS if n not in large]
    ds, ms, vs = _adamw_small([weights[n] for n in small], [grads[n] for n in small], [mom[n] for n in small],
                              [var[n] for n in small], name="adamw_small")
    for n, dn, mn, vn in zip(small, ds, ms, vs):
        delta[n], new_m[n], new_v[n] = dn, mn, vn

    return (vec_sum[n_vec - 1], grad_x[None], *[grads[n] for n in WEIGHTS], *[delta[n] for n in WEIGHTS],
            *[new_m[n] for n in WEIGHTS], *[new_v[n] for n in WEIGHTS])
```

```python
import jax
import jax.numpy as jnp
from jax import lax
from jax.experimental import pallas as pl
from jax.experimental.pallas import tpu as pltpu

F32 = jnp.float32
BF16 = jnp.bfloat16
MESH = pl.DeviceIdType.MESH

D_MODEL = 1024
CHUNK = 64
LN_EPS = 1e-5
GMLP_HEADS = 4
GMLP_HEAD_DIM = 256
GMLP_BLOCK = 128
POOL_WINDOWS = (2, 4, 8, 16)
POOL_GROUPS = 4
POOL_GROUP_DIM = 256
POOL_HALO = 16
EVEN_IN = 5120
MLA_HEADS = 16
MLA_NOPE = 128
MLA_ROPE = 64
MLA_V = 128
MLA_Q_RANK = 256
MLA_KV_RANK = 128
MLA_WIDTH = MLA_HEADS * MLA_V
ODD_IN = 2496
ODD_SMALL = MLA_Q_RANK + MLA_KV_RANK + MLA_ROPE
ODD_SMALL_PAD = 512
QK_PAD = 256
ROPE_THETA = 10000.0
ATTN_SCALE = (MLA_NOPE + MLA_ROPE) ** -0.5
DEEPNORM_ALPHA = (2.0 * 2) ** 0.25
ADAM_LR = 0.001
ADAM_B1 = 0.9
ADAM_B2 = 0.999
ADAM_EPS = 1e-08
ADAM_WD = 0.01
ADAM_STEP = 10
NEG = -1e30
LANES = 128
N_DEV = 8
N_CHIPS = 4
VMEM_LIMIT_BYTES = 56 * 1024 * 1024
HBM = pl.BlockSpec(memory_space=pltpu.HBM)
VMEM = pl.BlockSpec(memory_space=pltpu.VMEM)


def _params(*sem):
    return pltpu.CompilerParams(dimension_semantics=sem if sem else None, vmem_limit_bytes=VMEM_LIMIT_BYTES)


def _tile(dim, pref):
    for t in (pref, 2048, 1280, 1024, 512, 256, 128):
        if t <= min(pref, dim) and dim % t == 0:
            return t
    return dim


def _sigmoid(z):
    return 1.0 / (1.0 + jnp.exp(-z))


def _dot(a, b, dims):
    return lax.dot_general(a, b, (dims, ((), ())), preferred_element_type=F32)


NN = ((1,), (0,))
NT = ((1,), (1,))
TN = ((0,), (0,))


def _matmul(a, b, *, name, trans_a=False, trans_b=False, out_dtype=F32, b_stacked=False, out_stacked=False,
            tm=512, tn=1024, tk=2048, after=None):
    k, m = a.shape if trans_a else a.shape[::-1]
    if b_stacked:
        assert not trans_b
        ns, kb, n_sh = b.shape
        n = ns * n_sh
    else:
        n, kb = b.shape if trans_b else b.shape[::-1]
    assert k == kb, (a.shape, b.shape)
    tm = _tile(m, tm)
    tn, tk = _tile(n // N_CHIPS if b_stacked or out_stacked else n, tn), _tile(k, tk)
    nk = k // tk
    per = max((n // N_CHIPS) // tn, 1)
    dims = ((0 if trans_a else 1,), (1 if trans_b else 0,))

    def body_one(a_ref, b_ref, *rest):
        o_ref = rest[-1]
        o_ref[...] = _dot(a_ref[...].astype(BF16), b_ref[...].astype(BF16), dims).astype(out_dtype)

    def body_acc(a_ref, b_ref, *rest):
        o_ref, acc_ref = rest[-2:]
        kk = pl.program_id(2)

        @pl.when(kk == 0)
        def _():
            acc_ref[...] = jnp.zeros_like(acc_ref)

        acc_ref[...] += _dot(a_ref[...].astype(BF16), b_ref[...].astype(BF16), dims)

        @pl.when(kk == nk - 1)
        def _():
            o_ref[...] = acc_ref[...].astype(out_dtype)

    a_spec = pl.BlockSpec((tk, tm), lambda i, j, kk: (kk, i)) if trans_a else pl.BlockSpec((tm, tk), lambda i, j, kk: (i, kk))
    if b_stacked:
        b_spec = pl.BlockSpec((None, tk, tn), lambda i, j, kk: (j // per, kk, j % per))
    elif trans_b:
        b_spec = pl.BlockSpec((tn, tk), lambda i, j, kk: (j, kk))
    else:
        b_spec = pl.BlockSpec((tk, tn), lambda i, j, kk: (kk, j))
    if out_stacked:
        o_spec = pl.BlockSpec((None, tm, tn), lambda i, j, kk: (j // per, i, j % per))
        o_shape = jax.ShapeDtypeStruct((N_CHIPS, m, n // N_CHIPS), out_dtype)
    else:
        o_spec = pl.BlockSpec((tm, tn), lambda i, j, kk: (i, j))
        o_shape = jax.ShapeDtypeStruct((m, n), out_dtype)
    order = [] if after is None else [after]
    return pl.pallas_call(
        body_one if nk == 1 else body_acc, name=name, grid=(m // tm, n // tn, nk),
        in_specs=[a_spec, b_spec] + [pl.BlockSpec(memory_space=pl.ANY)] * len(order),
        out_specs=o_spec, out_shape=o_shape, scratch_shapes=[] if nk == 1 else [pltpu.VMEM((tm, tn), F32)],
        compiler_params=_params("parallel", "parallel", "arbitrary"),
    )(a, b, *order)


def _matmul_rows(a, b, epilogue, row_ins, vec_ins, row_outs, vec_outs, *, name, trans_b=False, b_stacked=False,
                 tm=512, tk=2048, after=None):
    m, k = a.shape
    if b_stacked:
        ns, n, n_sh = b.shape
        assert trans_b and ns * n_sh == k
        tk = k
    else:
        n = b.shape[0] if trans_b else b.shape[1]
        tk = _tile(k, tk)
    tm = _tile(m, tm)
    nk = k // tk
    dims = ((1,), (1 if trans_b else 0,))
    n_ri, n_vi, n_ro, n_vo = len(row_ins), len(vec_ins), len(row_outs), len(vec_outs)
    order = [] if after is None else [after]

    def body(*refs):
        a_ref, b_ref = refs[:2]
        pos = 2
        rin = refs[pos:pos + n_ri]
        pos += n_ri
        vin = refs[pos:pos + n_vi]
        pos += n_vi + len(order)
        rout = refs[pos:pos + n_ro]
        pos += n_ro
        vout = refs[pos:pos + n_vo]
        first = pl.program_id(0) == 0
        if b_stacked:
            part = _dot(a_ref[:, :n_sh].astype(BF16), b_ref[0].astype(BF16), dims)
            for sh in range(1, ns):
                part = part + _dot(a_ref[:, sh * n_sh:(sh + 1) * n_sh].astype(BF16), b_ref[sh].astype(BF16), dims)
        else:
            part = _dot(a_ref[...].astype(BF16), b_ref[...].astype(BF16), dims)
        if nk == 1:
            epilogue(part, first, rin, vin, rout, vout)
        else:
            acc_ref = refs[-1]
            kk = pl.program_id(1)

            @pl.when(kk == 0)
            def _():
                acc_ref[...] = part

            @pl.when(kk > 0)
            def _():
                acc_ref[...] += part

            @pl.when(kk == nk - 1)
            def _():
                epilogue(acc_ref[...], first, rin, vin, rout, vout)

    a_spec = pl.BlockSpec((tm, tk), lambda i, kk: (i, kk))
    if b_stacked:
        b_spec = pl.BlockSpec((ns, n, n_sh), lambda i, kk: (0, 0, 0))
    elif trans_b:
        b_spec = pl.BlockSpec((n, tk), lambda i, kk: (0, kk))
    else:
        b_spec = pl.BlockSpec((tk, n), lambda i, kk: (kk, 0))
    row = pl.BlockSpec((tm, n), lambda i, kk: (i, 0))
    vec = lambda w: pl.BlockSpec((1, w), lambda i, kk: (0, 0))
    return pl.pallas_call(
        body, name=name, grid=(m // tm, nk),
        in_specs=[a_spec, b_spec] + [row] * n_ri + [vec(v.shape[1]) for v in vec_ins] + [pl.BlockSpec(memory_space=pl.ANY)] * len(order),
        out_specs=[row] * n_ro + [vec(w) for w in vec_outs],
        out_shape=[jax.ShapeDtypeStruct((m, n), dt) for dt in row_outs] + [jax.ShapeDtypeStruct((1, w), F32) for w in vec_outs],
        scratch_shapes=[] if nk == 1 else [pltpu.VMEM((tm, n), F32)],
        compiler_params=_params("arbitrary", "arbitrary"),
    )(a, b, *row_ins, *vec_ins, *order)


def _row_spec(ts, d):
    return pl.BlockSpec((ts, d), lambda i: (i, 0))


def _vec_spec(d):
    return pl.BlockSpec((1, d), lambda i: (0, 0))


def _modulate(x, scale, shift, *, name):
    s, d = x.shape
    ts = _tile(s, 512)

    def body(x_ref, sc_ref, sh_ref, h_ref):
        h_ref[...] = (x_ref[...] * (1.0 + sc_ref[...]) + sh_ref[...]).astype(BF16)

    return pl.pallas_call(
        body, name=name, grid=(s // ts,), in_specs=[_row_spec(ts, d), _vec_spec(d), _vec_spec(d)],
        out_specs=_row_spec(ts, d), out_shape=jax.ShapeDtypeStruct((s, d), BF16), compiler_params=_params("parallel"),
    )(x, scale, shift)


def _ln_stats(pre):
    mu = jnp.mean(pre, axis=-1, keepdims=True)
    xc = pre - mu
    var = jnp.mean(xc * xc, axis=-1, keepdims=True)
    rstd = lax.rsqrt(var + LN_EPS)
    return xc * rstd, rstd


def _ln_bwd_rows(dout, xhat, rstd, g):
    dxh = dout * g
    m1 = jnp.mean(dxh, axis=-1, keepdims=True)
    m2 = jnp.mean(dxh * xhat, axis=-1, keepdims=True)
    return rstd * (dxh - m1 - xhat * m2)


def _colsum(v):
    return jnp.sum(v, axis=0, keepdims=True)


def _out_resid_ln(mix, w_out, x, gate, g, b, scale_next, shift_next, *, name):
    def epilogue(y, first, rin, vin, rout, vout):
        (x_ref,), (gate_ref, g_ref, b_ref, sc_ref, sh_ref), (y_ref, xn_ref, h_ref) = rin, vin, rout
        y_ref[...] = y
        pre = DEEPNORM_ALPHA * x_ref[...] + (1.0 + gate_ref[...]) * y
        xhat, _ = _ln_stats(pre)
        xn = xhat * g_ref[...] + b_ref[...]
        xn_ref[...] = xn
        h_ref[...] = (xn * (1.0 + sc_ref[...]) + sh_ref[...]).astype(BF16)

    return _matmul_rows(mix, w_out, epilogue, [x], [gate, g, b, scale_next, shift_next], [F32, F32, BF16], [], name=name)


def _out_loss_ln_bwd(og, w_out, x, gate, g, b, target, *, name):
    d = x.shape[1]

    def epilogue(yv, first, rin, vin, rout, vout):
        (x_ref, t_ref), (gate_ref, g_ref, b_ref), (dy_ref, dres_ref), (dg_ref, db_ref, dgate_ref, loss_ref) = rin, vin, rout, vout

        @pl.when(first)
        def _():
            for r in vout:
                r[...] = jnp.zeros_like(r)

        pre = DEEPNORM_ALPHA * x_ref[...] + (1.0 + gate_ref[...]) * yv
        xhat, rstd = _ln_stats(pre)
        diff = xhat * g_ref[...] + b_ref[...] - t_ref[...]
        loss_ref[...] += (0.5 / d) * jnp.sum(jnp.sum(diff * diff, axis=1, keepdims=True), axis=0, keepdims=True)
        dout = diff * (1.0 / d)
        dpre = _ln_bwd_rows(dout, xhat, rstd, g_ref[...])
        dy_ref[...] = (dpre * (1.0 + gate_ref[...])).astype(BF16)
        dres_ref[...] = DEEPNORM_ALPHA * dpre
        dg_ref[...] += _colsum(dout * xhat)
        db_ref[...] += _colsum(dout)
        dgate_ref[...] += _colsum(dpre * yv)

    return _matmul_rows(og, w_out, epilogue, [x, target], [gate, g, b], [BF16, F32], [d, d, d, 1], name=name)


def _dh_mid_ln_bwd(dproj, w_in, x, y, gate, g, b, dres_next, scale_next, x_next, *, name):
    d = x.shape[1]

    def epilogue(dh, first, rin, vin, rout, vout):
        (x_ref, y_ref, dr_ref, xn_ref), (gate_ref, g_ref, b_ref, sc_ref), (dy_ref, dres_ref) = rin, vin, rout
        dg_ref, db_ref, dgate_ref, dscale_ref, dshift_ref = vout

        @pl.when(first)
        def _():
            for r in vout:
                r[...] = jnp.zeros_like(r)

        dout = dr_ref[...] + dh * (1.0 + sc_ref[...])
        dscale_ref[...] += _colsum(dh * xn_ref[...])
        dshift_ref[...] += _colsum(dh)
        yv = y_ref[...]
        pre = DEEPNORM_ALPHA * x_ref[...] + (1.0 + gate_ref[...]) * yv
        xhat, rstd = _ln_stats(pre)
        dpre = _ln_bwd_rows(dout, xhat, rstd, g_ref[...])
        dy_ref[...] = (dpre * (1.0 + gate_ref[...])).astype(BF16)
        dres_ref[...] = DEEPNORM_ALPHA * dpre
        dg_ref[...] += _colsum(dout * xhat)
        db_ref[...] += _colsum(dout)
        dgate_ref[...] += _colsum(dpre * yv)

    return _matmul_rows(dproj, w_in, epilogue, [x, y, dres_next, x_next], [gate, g, b, scale_next], [BF16, F32], [d] * 5,
                        trans_b=True, tk=2560, name=name)


def _dh_input_bwd(dproj, w_in_stacked, x, dres, scale, *, name, after):
    d = x.shape[1]

    def epilogue(dh, first, rin, vin, rout, vout):
        (x_ref, dr_ref), (sc_ref,), (dx_ref,), (dscale_ref, dshift_ref) = rin, vin, rout, vout

        @pl.when(first)
        def _():
            for r in vout:
                r[...] = jnp.zeros_like(r)

        dx_ref[...] = dr_ref[...] + dh * (1.0 + sc_ref[...])
        dscale_ref[...] += _colsum(dh * x_ref[...])
        dshift_ref[...] += _colsum(dh)

    return _matmul_rows(dproj, w_in_stacked, epilogue, [x, dres], [scale], [F32], [d, d], trans_b=True, b_stacked=True,
                        tm=512, after=after, name=name)


def _chunk_mask(transposed=False):
    r = lax.broadcasted_iota(jnp.int32, (GMLP_BLOCK, GMLP_BLOCK), 0) // CHUNK
    c = lax.broadcasted_iota(jnp.int32, (GMLP_BLOCK, GMLP_BLOCK), 1) // CHUNK
    return (r <= c) if transposed else (c <= r)


def _window_sum(ext, steps, forward):
    rows = ext.shape[0]
    acc = ext
    for k in range(steps):
        shift = 1 << k
        acc = acc + pltpu.roll(acc, (rows - shift) if forward else shift, 0)
    return acc


def _pool_counts(first_row, rows, win):
    t = first_row + lax.broadcasted_iota(jnp.int32, (rows, 1), 0)
    return jnp.minimum(t + 1, win).astype(F32)


def _even_specs(t):
    col = lambda j: pl.BlockSpec((t, D_MODEL), lambda n: (n, j))
    per = t // POOL_HALO
    prev = pl.BlockSpec((POOL_HALO, D_MODEL), lambda n: (jnp.maximum(n * per - 1, 0), 3))
    return col, per, prev


def _full(shape):
    return pl.BlockSpec(shape, lambda n: (0,) * len(shape))


def _gmlp_head(v_h, ng, nb, w_bf):
    xhat, rstd = _ln_stats(v_h)
    vn = (xhat * ng + nb).astype(BF16)
    return xhat, rstd, vn, _dot(w_bf, vn, NN)


def _pool_group(xb_g, prev_g, first_row, grp):
    t = xb_g.shape[0]
    ext = jnp.concatenate([prev_g, xb_g], axis=0)
    tot = _window_sum(ext, grp + 1, False)[POOL_HALO:, :]
    cnt = _pool_counts(first_row, t, POOL_WINDOWS[grp])
    return tot / cnt - xb_g, cnt


def _even_fwd(proj, ws, bs_t, ng, nb, pool_w, pool_b, pool_scale, *, name):
    s = proj.shape[0]
    t = GMLP_BLOCK
    col, per, prev = _even_specs(t)

    def body(u_ref, v_ref, za_ref, xb_ref, zb_ref, xp_ref, ws_ref, bs_ref, ng_ref, nb_ref, pw_ref, pb_ref, ps_ref, o_ref):
        n = pl.program_id(0)
        mask = _chunk_mask()
        for h in range(GMLP_HEADS):
            c0 = h * GMLP_HEAD_DIM
            cs = slice(c0, c0 + GMLP_HEAD_DIM)
            w_bf = jnp.where(mask, ws_ref[h], 0.0).astype(BF16)
            _, _, _, sv = _gmlp_head(v_ref[:, cs].astype(F32),ng_ref[...], nb_ref[...], w_bf)
            sv = sv + bs_ref[:, h:h + 1]
            za = za_ref[:, cs].astype(F32)
            o_ref[:, cs] = (u_ref[:, cs].astype(F32) * sv * (za * _sigmoid(za))).astype(BF16)
        live = (n > 0).astype(F32)
        for grp in range(POOL_GROUPS):
            c0 = grp * POOL_GROUP_DIM
            cs = slice(c0, c0 + POOL_GROUP_DIM)
            pooled, _ = _pool_group(xb_ref[:, cs].astype(F32), xp_ref[:, cs].astype(F32) * live, n * t, grp)
            yb = _dot(pooled.astype(BF16), pw_ref[grp], NN) + pb_ref[:, cs]
            zb = zb_ref[:, cs].astype(F32)
            o_ref[:, D_MODEL + c0:D_MODEL + c0 + POOL_GROUP_DIM] = (yb * ps_ref[:, cs] * (zb * _sigmoid(zb))).astype(BF16)

    return pl.pallas_call(
        body, name=name, grid=(s // t,),
        in_specs=[col(0), col(1), col(2), col(3), col(4), prev,
                  _full((GMLP_HEADS, t, t)), _full((t, LANES)), _full((1, GMLP_HEAD_DIM)), _full((1, GMLP_HEAD_DIM)),
                  _full((POOL_GROUPS, POOL_GROUP_DIM, POOL_GROUP_DIM)), _full((1, D_MODEL)), _full((1, D_MODEL))],
        out_specs=pl.BlockSpec((t, 2 * D_MODEL), lambda n: (n, 0)),
        out_shape=jax.ShapeDtypeStruct((s, 2 * D_MODEL), BF16),
        compiler_params=_params("parallel"),
    )(proj, proj, proj, proj, proj, proj, ws, bs_t, ng, nb, pool_w, pool_b, pool_scale)


def _even_bwd(proj, dmix, ws, ws_t, bs_t, ng, nb, pool_w, pool_b, pool_scale, *, name):
    s = proj.shape[0]
    t = GMLP_BLOCK
    nblk = s // t
    col, per, prev = _even_specs(t)
    nxt = lambda j: pl.BlockSpec((POOL_HALO, D_MODEL), lambda n: (jnp.minimum((n + 1) * per, nblk * per - 1), j))

    def body(u_ref, v_ref, za_ref, xb_ref, zb_ref, xp_ref, zn_ref, da_ref, db_ref, dbn_ref,
             ws_ref, wst_ref, bs_ref, ng_ref, nb_ref, pw_ref, pb_ref, ps_ref,
             dp_ref, gws_ref, gbs_ref, gng_ref, gnb_ref, gpw_ref, gpb_ref, gps_ref):
        n = pl.program_id(0)

        @pl.when(n == 0)
        def _():
            for r in (gws_ref, gbs_ref, gng_ref, gnb_ref, gpw_ref, gpb_ref, gps_ref):
                r[...] = jnp.zeros_like(r)

        mask, mask_t = _chunk_mask(), _chunk_mask(True)
        lane = lax.broadcasted_iota(jnp.int32, (t, LANES), 1)
        ngv, nbv = ng_ref[...], nb_ref[...]
        for h in range(GMLP_HEADS):
            c0 = h * GMLP_HEAD_DIM
            cs = slice(c0, c0 + GMLP_HEAD_DIM)
            w_bf = jnp.where(mask, ws_ref[h], 0.0).astype(BF16)
            wt_bf = jnp.where(mask_t, wst_ref[h], 0.0).astype(BF16)
            xhat, rstd, vn, sv = _gmlp_head(v_ref[:, cs].astype(F32),ngv, nbv, w_bf)
            sv = sv + bs_ref[:, h:h + 1]
            za, u, da = za_ref[:, cs].astype(F32), u_ref[:, cs].astype(F32), da_ref[:, cs].astype(F32)
            sg = _sigmoid(za)
            sl = za * sg
            dp_ref[:, cs] = (da * sv * sl).astype(BF16)
            dp_ref[:, 2 * D_MODEL + c0:2 * D_MODEL + c0 + GMLP_HEAD_DIM] = (
                da * u * sv * (sg * (1.0 + za * (1.0 - sg)))).astype(BF16)
            dsv = da * u * sl
            gbs_ref[...] += jnp.where(lane == h, jnp.sum(dsv, axis=1, keepdims=True), 0.0)
            dsv_bf = dsv.astype(BF16)
            gws_ref[h] += jnp.where(mask, _dot(dsv_bf, vn, NT), 0.0)
            dvn = _dot(wt_bf, dsv_bf, NN)
            dp_ref[:, D_MODEL + c0:D_MODEL + c0 + GMLP_HEAD_DIM] = _ln_bwd_rows(dvn, xhat, rstd, ngv).astype(BF16)
            gng_ref[...] += _colsum(dvn * xhat)
            gnb_ref[...] += _colsum(dvn)
        live_prev = (n > 0).astype(F32)
        live_next = (n < nblk - 1).astype(F32)
        for grp in range(POOL_GROUPS):
            c0 = grp * POOL_GROUP_DIM
            cs = slice(c0, c0 + POOL_GROUP_DIM)
            xb = xb_ref[:, cs].astype(F32)
            pooled, cnt = _pool_group(xb, xp_ref[:, cs].astype(F32) * live_prev, n * t, grp)
            pooled_bf = pooled.astype(BF16)
            pw = pw_ref[grp]
            yb = _dot(pooled_bf, pw, NN) + pb_ref[:, cs]
            ps = ps_ref[:, cs]
            zb, db = zb_ref[:, cs].astype(F32), db_ref[:, cs].astype(F32)
            sg = _sigmoid(zb)
            sl = zb * sg
            dp_ref[:, 4 * D_MODEL + c0:4 * D_MODEL + c0 + POOL_GROUP_DIM] = (
                db * yb * ps * (sg * (1.0 + zb * (1.0 - sg)))).astype(BF16)
            dsl = db * sl
            dy = dsl * ps
            gps_ref[:, cs] += _colsum(dsl * yb)
            gpb_ref[:, cs] += _colsum(dy)
            dy_bf = dy.astype(BF16)
            gpw_ref[grp] += _dot(pooled_bf, dy_bf, TN)
            r = _dot(dy_bf, pw, NT)
            zn = zn_ref[:, cs].astype(F32)
            dyn = (dbn_ref[:, cs].astype(F32) * (zn * _sigmoid(zn)) * ps * live_next).astype(BF16)
            rn = _dot(dyn, pw, NT) / _pool_counts((n + 1) * t, POOL_HALO, POOL_WINDOWS[grp])
            ext = jnp.concatenate([r / cnt, rn], axis=0)
            dxb = _window_sum(ext, grp + 1, True)[:t, :] - r
            dp_ref[:, 3 * D_MODEL + c0:3 * D_MODEL + c0 + POOL_GROUP_DIM] = dxb.astype(BF16)

    out_shape = [
        jax.ShapeDtypeStruct((s, EVEN_IN), BF16),
        jax.ShapeDtypeStruct((GMLP_HEADS, t, t), F32), jax.ShapeDtypeStruct((t, LANES), F32),
        jax.ShapeDtypeStruct((1, GMLP_HEAD_DIM), F32), jax.ShapeDtypeStruct((1, GMLP_HEAD_DIM), F32),
        jax.ShapeDtypeStruct((POOL_GROUPS, POOL_GROUP_DIM, POOL_GROUP_DIM), F32),
        jax.ShapeDtypeStruct((1, D_MODEL), F32), jax.ShapeDtypeStruct((1, D_MODEL), F32),
    ]
    return pl.pallas_call(
        body, name=name, grid=(nblk,),
        in_specs=[col(0), col(1), col(2), col(3), col(4), prev, nxt(4),
                  pl.BlockSpec((t, D_MODEL), lambda n: (n, 0)), pl.BlockSpec((t, D_MODEL), lambda n: (n, 1)), nxt(1),
                  _full((GMLP_HEADS, t, t)), _full((GMLP_HEADS, t, t)), _full((t, LANES)),
                  _full((1, GMLP_HEAD_DIM)), _full((1, GMLP_HEAD_DIM)),
                  _full((POOL_GROUPS, POOL_GROUP_DIM, POOL_GROUP_DIM)), _full((1, D_MODEL)), _full((1, D_MODEL))],
        out_specs=[pl.BlockSpec((t, EVEN_IN), lambda n: (n, 0))] + [_full(o.shape) for o in out_shape[1:]],
        out_shape=out_shape,
        compiler_params=_params("arbitrary"),
    )(proj, proj, proj, proj, proj, proj, proj, dmix, dmix, dmix, ws, ws_t, bs_t, ng, nb, pool_w, pool_b, pool_scale)


ROPE_HALF = MLA_ROPE // 2


def _rope(v, cos, sin_signed):
    return v * cos + pltpu.roll(v, 2 * ROPE_HALF, 1) * sin_signed


def _rope_bwd(d, cos, sin_signed):
    return d * cos + pltpu.roll(d * sin_signed, 2 * ROPE_HALF, 1)


def _slab_lanes(shape, which):
    lane = lax.broadcasted_iota(jnp.int32, shape, 1)
    return (lane // ROPE_HALF) % 2 == which


def _rms(v, g):
    r = lax.rsqrt(jnp.mean(v * v, axis=-1, keepdims=True) + LN_EPS)
    return v * r * g, r


def _rms_bwd(dy, v, r, g):
    u = dy * g
    return r * u - v * (r * r * r) * jnp.mean(u * v, axis=-1, keepdims=True)


def _mla_prep(proj, gq, gkv, cos, sin_signed, *, name):
    s = proj.shape[0]
    ts = _tile(s, 512)

    def body(p_ref, gq_ref, gkv_ref, c_ref, s_ref, q_ref, k_ref):
        qcn, _ = _rms(p_ref[:, :MLA_Q_RANK].astype(F32), gq_ref[...])
        kvn, _ = _rms(p_ref[:, MLA_Q_RANK:MLA_Q_RANK + MLA_KV_RANK].astype(F32), gkv_ref[...])
        kr = p_ref[:, MLA_Q_RANK + MLA_KV_RANK:].astype(F32)
        lane = lax.broadcasted_iota(jnp.int32, kr.shape, 1)
        by1, by2 = pltpu.roll(kr, ROPE_HALF, 1), pltpu.roll(kr, 2 * ROPE_HALF, 1)
        both = jnp.where(lane < ROPE_HALF, kr, jnp.where(lane < 3 * ROPE_HALF, by1, by2))
        kr = _rope(both, c_ref[...], s_ref[...])
        q_ref[...] = qcn.astype(BF16)
        k_ref[...] = jnp.concatenate([kvn, kr], axis=1).astype(BF16)

    return pl.pallas_call(
        body, name=name, grid=(s // ts,),
        in_specs=[_small_spec(ts), _vec_spec(MLA_Q_RANK), _vec_spec(MLA_KV_RANK), _row_spec(ts, LANES), _row_spec(ts, LANES)],
        out_specs=[_row_spec(ts, MLA_Q_RANK), _row_spec(ts, QK_PAD)],
        out_shape=[jax.ShapeDtypeStruct((s, MLA_Q_RANK), BF16), jax.ShapeDtypeStruct((s, QK_PAD), BF16)],
        compiler_params=_params("parallel"),
    )(proj, gq, gkv, cos, sin_signed)


def _mla_prep_bwd(proj, dqcn, dkv, gq, gkv, cos, sin_signed, *, name):
    s = proj.shape[0]
    ts = _tile(s, 512)

    def body(p_ref, dq_ref, dkv_ref, gq_ref, gkv_ref, c_ref, s_ref, ds_ref, ggq_ref, ggkv_ref):
        @pl.when(pl.program_id(0) == 0)
        def _():
            ggq_ref[...] = jnp.zeros_like(ggq_ref)
            ggkv_ref[...] = jnp.zeros_like(ggkv_ref)

        qc = p_ref[:, :MLA_Q_RANK].astype(F32)
        kvc = p_ref[:, MLA_Q_RANK:MLA_Q_RANK + MLA_KV_RANK].astype(F32)
        _, rq = _rms(qc, gq_ref[...])
        _, rkv = _rms(kvc, gkv_ref[...])
        dq = dq_ref[...]
        dkvn = dkv_ref[:, :MLA_KV_RANK]
        ggq_ref[...] += _colsum(dq * qc * rq)
        ggkv_ref[...] += _colsum(dkvn * kvc * rkv)
        dboth = _rope_bwd(dkv_ref[:, MLA_KV_RANK:], c_ref[...], s_ref[...])
        lane = lax.broadcasted_iota(jnp.int32, dboth.shape, 1)
        pair = dboth + pltpu.roll(dboth, 3 * ROPE_HALF, 1)
        dkr = jnp.where(lane < ROPE_HALF, pair, jnp.where(lane < 2 * ROPE_HALF, pltpu.roll(pair, 3 * ROPE_HALF, 1), 0.0))
        ds_ref[...] = jnp.concatenate(
            [_rms_bwd(dq, qc, rq, gq_ref[...]), _rms_bwd(dkvn, kvc, rkv, gkv_ref[...]), dkr], axis=1).astype(BF16)

    return pl.pallas_call(
        body, name=name, grid=(s // ts,),
        in_specs=[_small_spec(ts), _row_spec(ts, MLA_Q_RANK), _row_spec(ts, QK_PAD),
                  _vec_spec(MLA_Q_RANK), _vec_spec(MLA_KV_RANK), _row_spec(ts, LANES), _row_spec(ts, LANES)],
        out_specs=[_row_spec(ts, ODD_SMALL_PAD), _vec_spec(MLA_Q_RANK), _vec_spec(MLA_KV_RANK)],
        out_shape=[jax.ShapeDtypeStruct((s, ODD_SMALL_PAD), BF16), jax.ShapeDtypeStruct((1, MLA_Q_RANK), F32),
                   jax.ShapeDtypeStruct((1, MLA_KV_RANK), F32)],
        compiler_params=_params("arbitrary"),
    )(proj, dqcn, dkv, gq, gkv, cos, sin_signed)


Q_HEAD_GROUP = 8
LOG2_E = 1.4426950408889634
Q_PRESCALE = ATTN_SCALE * LOG2_E


def _q_build(q_nope, q_rope_pre, wuk_hdr, cos, sin_signed, *, name):
    s = q_nope.shape[0]
    ts = _tile(s, 512)
    hg = Q_HEAD_GROUP

    def body(qn_ref, qr_ref, w_ref, c_ref, s_ref, o_ref):
        for pair in range(hg // 2):
            r = _rope(qr_ref[:, pair * LANES:(pair + 1) * LANES], c_ref[...], s_ref[...])
            for j in range(2):
                h = 2 * pair + j
                ql = _dot(qn_ref[:, h * MLA_NOPE:(h + 1) * MLA_NOPE], w_ref[h], NN)
                mine = jnp.where(_slab_lanes(r.shape, j), r, 0.0)
                o_ref[h] = (jnp.concatenate([ql, mine], axis=1) * Q_PRESCALE).astype(BF16)

    return pl.pallas_call(
        body, name=name, grid=(s // ts, MLA_HEADS // hg),
        in_specs=[pl.BlockSpec((ts, hg * MLA_NOPE), lambda i, p: (i, p)), pl.BlockSpec((ts, hg * MLA_ROPE), lambda i, p: (i, p)),
                  pl.BlockSpec((hg, MLA_NOPE, MLA_KV_RANK), lambda i, p: (p, 0, 0)),
                  pl.BlockSpec((ts, LANES), lambda i, p: (i, 0)), pl.BlockSpec((ts, LANES), lambda i, p: (i, 0))],
        out_specs=pl.BlockSpec((hg, ts, QK_PAD), lambda i, p: (p, i, 0)),
        out_shape=jax.ShapeDtypeStruct((MLA_HEADS, s, QK_PAD), BF16),
        compiler_params=_params("parallel", "parallel"),
    )(q_nope, q_rope_pre, wuk_hdr, cos, sin_signed)


def _q_bwd(dq, q_nope, wuk_hrd, cos, sin_signed, *, name):
    s = q_nope.shape[0]
    ts = _tile(s, 512)
    hg = Q_HEAD_GROUP

    nope_w, all_w = hg * MLA_NOPE, hg * (MLA_NOPE + MLA_ROPE)

    def body(dq_ref, qn_ref, w_ref, c_ref, s_ref, dall_ref, gw_ref):
        @pl.when(pl.program_id(1) == 0)
        def _():
            gw_ref[...] = jnp.zeros_like(gw_ref)

        for h in range(hg):
            dql = dq_ref[h, :, :MLA_KV_RANK]
            dall_ref[:, h * MLA_NOPE:(h + 1) * MLA_NOPE] = _dot(dql, w_ref[h], NN).astype(BF16)
            gw_ref[h] += _dot(dql, qn_ref[:, h * MLA_NOPE:(h + 1) * MLA_NOPE], TN)
        for pair in range(hg // 2):
            hi0 = dq_ref[2 * pair, :, MLA_KV_RANK:].astype(F32)
            hi1 = dq_ref[2 * pair + 1, :, MLA_KV_RANK:].astype(F32)
            d = jnp.where(_slab_lanes(hi0.shape, 0), hi0, hi1)
            dall_ref[:, nope_w + pair * LANES:nope_w + (pair + 1) * LANES] = _rope_bwd(d, c_ref[...], s_ref[...]).astype(BF16)

    return pl.pallas_call(
        body, name=name, grid=(MLA_HEADS // hg, s // ts),
        in_specs=[pl.BlockSpec((hg, ts, QK_PAD), lambda p, i: (p, i, 0)), pl.BlockSpec((ts, nope_w), lambda p, i: (i, p)),
                  pl.BlockSpec((hg, MLA_KV_RANK, MLA_NOPE), lambda p, i: (p, 0, 0)),
                  pl.BlockSpec((ts, LANES), lambda p, i: (i, 0)), pl.BlockSpec((ts, LANES), lambda p, i: (i, 0))],
        out_specs=[pl.BlockSpec((ts, all_w), lambda p, i: (i, p)),
                   pl.BlockSpec((hg, MLA_KV_RANK, MLA_NOPE), lambda p, i: (p, 0, 0))],
        out_shape=[jax.ShapeDtypeStruct((s, MLA_HEADS * (MLA_NOPE + MLA_ROPE)), BF16),
                   jax.ShapeDtypeStruct((MLA_HEADS, MLA_KV_RANK, MLA_NOPE), F32)],
        compiler_params=_params("parallel", "arbitrary"),
    )(dq, q_nope, wuk_hrd, cos, sin_signed)


ATTN_BQ = 128
ATTN_BK = 512
ATTN_BK_FWD = 1024


def _diag_mask(rows, bq, bk, q0, k0):
    qc = (q0 + lax.broadcasted_iota(jnp.int32, (rows, bk), 0) % bq) // CHUNK
    kc = (k0 + lax.broadcasted_iota(jnp.int32, (rows, bk), 1)) // CHUNK
    return kc <= qc


def _attn_fwd(q, k, *, name):
    nh, s, dk = q.shape
    bq, bk = _tile(s, ATTN_BQ), _tile(s, ATTN_BK_FWD)
    rows = nh * bq

    def body(q_ref, k_ref, o_ref, lse_ref):
        i = pl.program_id(0)
        qb = q_ref[...].reshape(rows, dk)
        n_before = (i * bq) // bk

        def step(j, width, carry, masked):
            m, l, acc = carry
            k0 = pl.multiple_of(j * bk, bk)
            kb = k_ref[pl.ds(k0, width), :]
            sc = _dot(qb, kb, NT)
            if masked:
                sc = jnp.where(_diag_mask(rows, bq, width, i * bq, k0), sc, NEG)
            m_new = jnp.maximum(m, jnp.max(sc, axis=1, keepdims=True))
            p = jnp.exp2(sc - m_new)
            a = jnp.exp2(m - m_new)
            l = a * l + jnp.sum(p, axis=1, keepdims=True)
            acc = a * acc + _dot(p.astype(BF16), kb[:, :MLA_KV_RANK], NN)
            return m_new, l, acc

        init = (jnp.full((rows, 1), NEG, F32), jnp.zeros((rows, 1), F32), jnp.zeros((rows, MLA_KV_RANK), F32))
        carry = lax.fori_loop(0, n_before, lambda j, c: step(j, bk, c, False), init)
        for part in range(bk // bq):
            @pl.when(i % (bk // bq) == part)
            def _(part=part):
                m, l, acc = step(n_before, (part + 1) * bq, carry, True)
                o_ref[...] = (acc / l).astype(BF16).reshape(nh, bq, MLA_KV_RANK)
                lse_ref[...] = jnp.broadcast_to(m + jnp.log2(l), (rows, LANES)).reshape(nh, bq, LANES)

    return pl.pallas_call(
        body, name=name, grid=(s // bq,),
        in_specs=[pl.BlockSpec((nh, bq, dk), lambda i: (0, i, 0)), pl.BlockSpec((s, dk), lambda i: (0, 0))],
        out_specs=[pl.BlockSpec((nh, bq, MLA_KV_RANK), lambda i: (0, i, 0)), pl.BlockSpec((nh, bq, LANES), lambda i: (0, i, 0))],
        out_shape=[jax.ShapeDtypeStruct((nh, s, MLA_KV_RANK), BF16), jax.ShapeDtypeStruct((nh, s, LANES), F32)],
        compiler_params=_params("parallel"),
    )(q, k)


def _attn_bwd(q, k, do, o, lse, *, name):
    nh, s, dk = q.shape
    bq, bk = _tile(s, ATTN_BQ), _tile(s, ATTN_BK)
    rows = nh * bq

    def body(q_ref, k_ref, do_ref, o_ref, lse_ref, dq_ref, dkv_ref):
        i = pl.program_id(0)
        n_before = (i * bq) // bk

        @pl.when(i == 0)
        def _():
            dkv_ref[...] = jnp.zeros_like(dkv_ref)

        qb = q_ref[...].reshape(rows, dk)
        dob = do_ref[...].reshape(rows, MLA_KV_RANK)
        lse_b = lse_ref[...].reshape(rows, LANES)[:, :1]
        delta = jnp.sum(dob.astype(F32) * o_ref[...].reshape(rows, MLA_KV_RANK).astype(F32), axis=1, keepdims=True)

        def step(j, width, dq, masked):
            j0 = pl.multiple_of(j * bk, bk)
            kb = k_ref[pl.ds(j0, width), :]
            sc = _dot(qb, kb, NT)
            if masked:
                sc = jnp.where(_diag_mask(rows, bq, width, i * bq, j0), sc, NEG)
            p = jnp.exp2(sc - lse_b)
            dp = _dot(dob, kb[:, :MLA_KV_RANK], NT)
            ds_bf = (p * (dp - delta)).astype(BF16)
            dkv_ref[pl.ds(j0, width), :] += _dot(ds_bf, qb, TN) * (1.0 / LOG2_E)
            dkv_ref[pl.ds(j0, width), :MLA_KV_RANK] += _dot(p.astype(BF16), dob, TN)
            return dq + _dot(ds_bf, kb, NN)

        dq_before = lax.fori_loop(0, n_before, lambda j, c: step(j, bk, c, False), jnp.zeros((rows, dk), F32))
        for part in range(bk // bq):
            @pl.when(i % (bk // bq) == part)
            def _(part=part):
                dq = step(n_before, (part + 1) * bq, dq_before, True) * ATTN_SCALE
                dq_ref[...] = dq.astype(BF16).reshape(nh, bq, dk)

    blk = lambda w: pl.BlockSpec((nh, bq, w), lambda i: (0, i, 0))
    return pl.pallas_call(
        body, name=name, grid=(s // bq,),
        in_specs=[blk(dk), pl.BlockSpec((s, dk), lambda i: (0, 0)), blk(MLA_KV_RANK), blk(MLA_KV_RANK), blk(LANES)],
        out_specs=[blk(dk), pl.BlockSpec((s, dk), lambda i: (0, 0))],
        out_shape=[jax.ShapeDtypeStruct((nh, s, dk), BF16), jax.ShapeDtypeStruct((s, dk), F32)],
        compiler_params=_params("arbitrary"),
    )(q, k, do, o, lse)


HEAD_GROUP = 4
SMALL_BLOCK = MLA_WIDTH // ODD_SMALL_PAD


def _small_spec(ts):
    return pl.BlockSpec((ts, ODD_SMALL_PAD), lambda i: (i, SMALL_BLOCK))


def _o_build(o_lat, wuv_hrv, proj, *, name):
    s = proj.shape[0]
    ts = _tile(s, 1024)
    w = HEAD_GROUP * MLA_V

    def body(ol_ref, w_ref, z_ref, og_ref):
        for j in range(HEAD_GROUP):
            cs = slice(j * MLA_V, (j + 1) * MLA_V)
            z = z_ref[:, cs].astype(F32)
            og_ref[:, cs] = (_dot(ol_ref[j], w_ref[j], NN) * (z * _sigmoid(z))).astype(BF16)

    return pl.pallas_call(
        body, name=name, grid=(s // ts, MLA_HEADS // HEAD_GROUP),
        in_specs=[pl.BlockSpec((HEAD_GROUP, ts, MLA_KV_RANK), lambda i, g: (g, i, 0)),
                  pl.BlockSpec((HEAD_GROUP, MLA_KV_RANK, MLA_V), lambda i, g: (g, 0, 0)),
                  pl.BlockSpec((ts, w), lambda i, g: (i, g))],
        out_specs=pl.BlockSpec((ts, w), lambda i, g: (i, g)),
        out_shape=jax.ShapeDtypeStruct((s, MLA_WIDTH), BF16),
        compiler_params=_params("parallel", "parallel"),
    )(o_lat, wuv_hrv, proj)


def _o_bwd(dg, proj, o_lat, wuv_hrv, wuv_hvr, *, name):
    s = proj.shape[0]
    ts = _tile(s, 1024)
    w = HEAD_GROUP * MLA_V

    def body(dg_ref, z_ref, ol_ref, w_ref, wt_ref, dol_ref, dz_ref, gw_ref):
        @pl.when(pl.program_id(1) == 0)
        def _():
            gw_ref[...] = jnp.zeros_like(gw_ref)

        for j in range(HEAD_GROUP):
            cs = slice(j * MLA_V, (j + 1) * MLA_V)
            z, dgj, ol = z_ref[:, cs].astype(F32), dg_ref[:, cs].astype(F32), ol_ref[j]
            sg = _sigmoid(z)
            o = _dot(ol, w_ref[j], NN)
            dz_ref[:, cs] = (dgj * o * (sg * (1.0 + z * (1.0 - sg)))).astype(BF16)
            do_bf = (dgj * (z * sg)).astype(BF16)
            dol_ref[j] = _dot(do_bf, wt_ref[j], NN).astype(BF16)
            gw_ref[j] += _dot(ol, do_bf, TN)

    hs = lambda a, b: pl.BlockSpec((HEAD_GROUP, a, b), lambda g, i: (g, 0, 0))
    return pl.pallas_call(
        body, name=name, grid=(MLA_HEADS // HEAD_GROUP, s // ts),
        in_specs=[pl.BlockSpec((ts, w), lambda g, i: (i, g)), pl.BlockSpec((ts, w), lambda g, i: (i, g)),
                  pl.BlockSpec((HEAD_GROUP, ts, MLA_KV_RANK), lambda g, i: (g, i, 0)),
                  hs(MLA_KV_RANK, MLA_V), hs(MLA_V, MLA_KV_RANK)],
        out_specs=[pl.BlockSpec((HEAD_GROUP, ts, MLA_KV_RANK), lambda g, i: (g, i, 0)),
                   pl.BlockSpec((ts, w), lambda g, i: (i, g)), hs(MLA_KV_RANK, MLA_V)],
        out_shape=[jax.ShapeDtypeStruct((MLA_HEADS, s, MLA_KV_RANK), BF16), jax.ShapeDtypeStruct((s, MLA_WIDTH), BF16),
                   jax.ShapeDtypeStruct((MLA_HEADS, MLA_KV_RANK, MLA_V), F32)],
        compiler_params=_params("parallel", "arbitrary"),
    )(dg, proj, o_lat, wuv_hrv, wuv_hvr)


def _ada_mod(c_all, ada_w, ada_b_sh, *, name):
    nl, _, cols = ada_w.shape

    def body(c_ref, w_ref, b_ref, o_ref):
        c = c_ref[...]
        cond = (c * _sigmoid(c)).astype(BF16)
        for l in range(nl):
            o_ref[l] = _dot(cond, w_ref[l].astype(BF16), NN) + b_ref[l]

    return pl.pallas_call(
        body, name=name, out_shape=jax.ShapeDtypeStruct((nl, c_all.shape[0], cols), F32),
        compiler_params=_params(),
    )(c_all, ada_w, ada_b_sh)


def _ada_grad(c_all_t, dmod_sh, *, name):
    nl, _, cols = dmod_sh.shape
    d = c_all_t.shape[0]

    def body(c_ref, dm_ref, gw_ref):
        c = c_ref[...]
        cond_t = c * _sigmoid(c)
        for l in range(nl):
            gw_ref[l] = lax.dot_general(cond_t, dm_ref[l], (NN, ((), ())), precision=lax.Precision.HIGHEST,
                                        preferred_element_type=F32)

    return pl.pallas_call(
        body, name=name, out_shape=jax.ShapeDtypeStruct((nl, d, cols), F32), compiler_params=_params(),
    )(c_all_t, dmod_sh)


def _sum_devices(parts, *, name):
    def body(p_ref, o_ref):
        acc = p_ref[0]
        for k in range(1, parts.shape[0]):
            acc = acc + p_ref[k]
        o_ref[...] = acc

    return pl.pallas_call(body, name=name, out_shape=jax.ShapeDtypeStruct(parts.shape[1:], F32), compiler_params=_params())(parts)


def _adamw_math(w, g, m, v):
    c1 = 1.0 - ADAM_B1 ** ADAM_STEP
    c2 = 1.0 - ADAM_B2 ** ADAM_STEP
    nm = ADAM_B1 * m + (1.0 - ADAM_B1) * g
    nv = ADAM_B2 * v + (1.0 - ADAM_B2) * (g * g)
    return -ADAM_LR * ((nm / c1) / (jnp.sqrt(nv / c2) + ADAM_EPS) + ADAM_WD * w), nm, nv


ADAMW_BLOCK_BYTES = 1 << 20


def _adamw(w, g, m, v, *, name, after=None):
    shape = w.shape
    a, b = shape[-2], shape[-1]
    lead = 1
    for dim in shape[:-2]:
        lead *= dim
    row_bytes = 4 * b
    if a * row_bytes <= ADAMW_BLOCK_BYTES:
        ta = a
        tl = max(1, min(lead, ADAMW_BLOCK_BYTES // (a * row_bytes)))
        while lead % tl:
            tl -= 1
    else:
        tl = 1
        ta = _tile(a, 256)
    to3 = lambda t: t.reshape(lead, a, b)

    def body(w_ref, g_ref, m_ref, v_ref, *rest):
        d_ref, nm_ref, nv_ref = rest[-3:]
        d_ref[...], nm_ref[...], nv_ref[...] = _adamw_math(w_ref[...], g_ref[...], m_ref[...], v_ref[...])

    spec = pl.BlockSpec((tl, ta, b), lambda i, j: (i, j, 0))
    out = jax.ShapeDtypeStruct((lead, a, b), F32)
    order = [] if after is None else [after]
    res = pl.pallas_call(
        body, name=name, grid=(lead // tl, a // ta), in_specs=[spec] * 4 + [pl.BlockSpec(memory_space=pl.ANY)] * len(order),
        out_specs=[spec] * 3, out_shape=[out] * 3, compiler_params=_params("parallel", "parallel"),
    )(to3(w), to3(g), to3(m), to3(v), *order)
    return [r.reshape(shape) for r in res]


def _adamw_small(ws, gs, ms, vs, *, name):
    n = len(ws)

    def body(*refs):
        for k in range(n):
            w_ref, g_ref, m_ref, v_ref = (refs[j * n + k] for j in range(4))
            d_ref, nm_ref, nv_ref = (refs[(4 + j) * n + k] for j in range(3))
            d_ref[...], nm_ref[...], nv_ref[...] = _adamw_math(w_ref[...], g_ref[...], m_ref[...], v_ref[...])

    outs = [jax.ShapeDtypeStruct(w.shape, F32) for w in ws]
    res = pl.pallas_call(body, name=name, out_shape=outs * 3, compiler_params=_params())(*ws, *gs, *ms, *vs)
    return res[:n], res[n:2 * n], res[2 * n:]


def _flip(v, bit):
    return 1 - v if bit else v


CHIP_DELTAS = ((1, 0), (0, 1), (1, 1))
SUM_ROWS = 32


def _all_gather_chips(shard, *, name):
    def body(x_ref, o_ref, send_sems, recv_sems, local_sem):
        x, y, c = lax.axis_index("x"), lax.axis_index("y"), lax.axis_index("c")
        mine = pltpu.make_async_copy(x_ref, o_ref.at[2 * x + y], local_sem)
        mine.start()

        def copy(k):
            tx, ty = _flip(x, CHIP_DELTAS[k][0]), _flip(y, CHIP_DELTAS[k][1])
            send = pltpu.make_async_remote_copy(src_ref=x_ref, dst_ref=o_ref.at[2 * x + y], send_sem=send_sems.at[k],
                                                recv_sem=recv_sems.at[k], device_id=(tx, ty, c), device_id_type=MESH)
            recv = pltpu.make_async_remote_copy(src_ref=x_ref, dst_ref=o_ref.at[2 * tx + ty], send_sem=send_sems.at[k],
                                                recv_sem=recv_sems.at[k], device_id=(tx, ty, c), device_id_type=MESH)
            return send, recv

        pairs = [copy(k) for k in range(3)]
        for send, _ in pairs:
            send.start()
        for _, recv in pairs:
            recv.wait_recv()
        for send, _ in pairs:
            send.wait_send()
        mine.wait()

    return pl.pallas_call(
        body, name=name, out_shape=jax.ShapeDtypeStruct((N_CHIPS,) + shard.shape, shard.dtype),
        in_specs=[HBM], out_specs=HBM,
        scratch_shapes=[pltpu.SemaphoreType.DMA((3,)), pltpu.SemaphoreType.DMA((3,)), pltpu.SemaphoreType.DMA(())],
    )(shard)


def _gather_weights(shards, *, name):
    n = len(shards)

    def body(*refs):
        w_refs, o_refs = refs[:n], refs[n:2 * n]
        ici_send, ici_recv, d2d_send, d2d_recv, local_sems = refs[2 * n:]
        x, y, c = lax.axis_index("x"), lax.axis_index("y"), lax.axis_index("c")
        me = 2 * x + y
        peers = [(_flip(x, dx), _flip(y, dy)) for dx, dy in CHIP_DELTAS]
        locals_ = [pltpu.make_async_copy(w_refs[k], o_refs[k].at[me], local_sems.at[k]) for k in range(n)]
        for cp in locals_:
            cp.start()

        def rows(k, which):
            half = shards[k].shape[0] // 2
            return pl.ds(pl.multiple_of(which * half, half), half)

        def over_chips(k, d, slot):
            tx, ty = peers[d]
            return pltpu.make_async_remote_copy(
                src_ref=w_refs[k].at[rows(k, c)], dst_ref=o_refs[k].at[slot, rows(k, c)], send_sem=ici_send.at[k, d],
                recv_sem=ici_recv.at[k, d], device_id=(tx, ty, c), device_id_type=MESH)

        def to_sibling(k, d, which):
            tx, ty = peers[d]
            at = o_refs[k].at[2 * tx + ty, rows(k, which)]
            return pltpu.make_async_remote_copy(src_ref=at, dst_ref=at, send_sem=d2d_send.at[k, d], recv_sem=d2d_recv.at[k, d],
                                                device_id=(x, y, 1 - c), device_id_type=MESH)

        sends = [over_chips(k, d, me) for k in range(n) for d in range(3)]
        for cp in sends:
            cp.start()
        passed = []
        for k in range(n):
            for d in range(3):
                over_chips(k, d, 2 * peers[d][0] + peers[d][1]).wait_recv()
                passed.append(to_sibling(k, d, c))
                passed[-1].start()
        for k in range(n):
            for d in range(3):
                to_sibling(k, d, 1 - c).wait_recv()
        for cp in sends + passed:
            cp.wait_send()
        for cp in locals_:
            cp.wait()

    return pl.pallas_call(
        body, name=name, out_shape=[jax.ShapeDtypeStruct((N_CHIPS,) + w.shape, w.dtype) for w in shards],
        in_specs=[HBM] * n, out_specs=[HBM] * n,
        scratch_shapes=[pltpu.SemaphoreType.DMA((n, 3))] * 4 + [pltpu.SemaphoreType.DMA((n,))],
    )(*shards)


def _add_into(dst_ref, src_ref):
    ns, r, _ = dst_ref.shape
    step = SUM_ROWS if r % SUM_ROWS == 0 else r
    for s in range(ns):
        def tile(t, carry):
            at = pl.ds(pl.multiple_of(t * step, step), step)
            dst_ref[s, at, :] = (dst_ref[s, at, :].astype(F32) + src_ref[s, at, :].astype(F32)).astype(dst_ref.dtype)
            return carry
        lax.fori_loop(0, r // step, tile, 0)


def _reduce_sibling(grads, *, name):
    n = len(grads)

    def body(*refs):
        g_refs, o_refs = refs[:n], refs[n:2 * n]
        mine, got = refs[2 * n:3 * n], refs[3 * n:4 * n]
        send_sems, recv_sems, load_sems, store_sems = refs[4 * n:]
        x, y, c = lax.axis_index("x"), lax.axis_index("y"), lax.axis_index("c")
        loads = [pltpu.make_async_copy(g_refs[k].at[:, c], mine[k], load_sems.at[k]) for k in range(n)]
        swaps = [pltpu.make_async_remote_copy(src_ref=g_refs[k].at[:, 1 - c], dst_ref=got[k], send_sem=send_sems.at[k],
                                              recv_sem=recv_sems.at[k], device_id=(x, y, 1 - c), device_id_type=MESH)
                 for k in range(n)]
        for cp in loads + swaps:
            cp.start()
        stores = []
        for k in range(n):
            loads[k].wait()
            swaps[k].wait_recv()
            _add_into(mine[k], got[k])
            stores.append(pltpu.make_async_copy(mine[k], o_refs[k], store_sems.at[k]))
            stores[-1].start()
        for k in range(n):
            swaps[k].wait_send()
            stores[k].wait()

    half = [jax.ShapeDtypeStruct((g.shape[0],) + g.shape[2:], g.dtype) for g in grads]
    return pl.pallas_call(
        body, name=name, out_shape=half, in_specs=[HBM] * n, out_specs=[HBM] * n,
        scratch_shapes=[pltpu.VMEM(h.shape, h.dtype) for h in half] * 2 + [pltpu.SemaphoreType.DMA((n,))] * 4,
        compiler_params=_params(),
    )(*grads)


def _reduce_chips(parts, landed, *, name):
    n = len(parts)

    def body(*refs):
        p_refs, l_refs, o_refs = refs[:n], refs[n:2 * n], refs[2 * n:3 * n]
        got, total = refs[3 * n:4 * n], refs[4 * n:5 * n]
        load_sems, share_send, share_recv, store_sems = refs[5 * n:]
        x, y, c = lax.axis_index("x"), lax.axis_index("y"), lax.axis_index("c")
        me = 2 * x + y
        slots = [me] + [2 * _flip(x, dx) + _flip(y, dy) for dx, dy in CHIP_DELTAS]
        loads = [[pltpu.make_async_copy((p_refs if j == 0 else l_refs)[k].at[slot], got[k].at[slot], load_sems.at[k, j])
                  for j, slot in enumerate(slots)] for k in range(n)]
        for per_array in loads:
            for cp in per_array:
                cp.start()
        shares, stores = [], []
        for k in range(n):
            for cp in loads[k]:
                cp.wait()
            r = total[k].shape[0]
            step = SUM_ROWS if r % SUM_ROWS == 0 else r

            def tile(t, carry, k=k, step=step):
                at = pl.ds(pl.multiple_of(t * step, step), step)
                acc = got[k][0, at, :].astype(F32)
                for s in range(1, N_CHIPS):
                    acc = acc + got[k][s, at, :].astype(F32)
                total[k][at, :] = acc
                return carry

            lax.fori_loop(0, r // step, tile, 0)
            stores.append(pltpu.make_async_copy(total[k], o_refs[k].at[c], store_sems.at[k]))
            shares.append(pltpu.make_async_remote_copy(
                src_ref=total[k], dst_ref=o_refs[k].at[c], send_sem=share_send.at[k], recv_sem=share_recv.at[k],
                device_id=(x, y, 1 - c), device_id_type=MESH))
            stores[-1].start()
            shares[-1].start()
        for k in range(n):
            pltpu.make_async_remote_copy(
                src_ref=total[k], dst_ref=o_refs[k].at[1 - c], send_sem=share_send.at[k], recv_sem=share_recv.at[k],
                device_id=(x, y, 1 - c), device_id_type=MESH).wait_recv()
        for cp in shares:
            cp.wait_send()
        for cp in stores:
            cp.wait()

    return pl.pallas_call(
        body, name=name, out_shape=[jax.ShapeDtypeStruct((2,) + p.shape[1:], F32) for p in parts],
        in_specs=[HBM] * (2 * n), out_specs=[HBM] * n,
        scratch_shapes=[pltpu.VMEM(p.shape, p.dtype) for p in parts] + [pltpu.VMEM(p.shape[1:], F32) for p in parts]
        + [pltpu.SemaphoreType.DMA((n, N_CHIPS))] + [pltpu.SemaphoreType.DMA((n,))] * 3,
        compiler_params=_params(),
    )(*parts, *landed)


SEM = pl.BlockSpec(memory_space=pltpu.SEMAPHORE)
IN_FLIGHT = pltpu.SideEffectType.DATAFLOW_SIDE_EFFECTING


def _chip_copies(s_refs, l_refs, sems, scatter, theirs):
    x, y, c = lax.axis_index("x"), lax.axis_index("y"), lax.axis_index("c")
    me = 2 * x + y
    copies = []
    for k in range(len(s_refs)):
        for d, (dx, dy) in enumerate(CHIP_DELTAS):
            tx, ty = _flip(x, dx), _flip(y, dy)
            peer = 2 * tx + ty
            send_sem, recv_sem = sems[2 * (3 * k + d)], sems[2 * (3 * k + d) + 1]
            copies.append(pltpu.make_async_remote_copy(
                src_ref=s_refs[k].at[peer] if scatter else s_refs[k], dst_ref=l_refs[k].at[peer if theirs else me],
                send_sem=send_sem, recv_sem=recv_sem, device_id=(tx, ty, c), device_id_type=MESH))
    return copies


def _chips_start(srcs, lands, after, *, scatter, name):
    n = len(srcs)
    n_sem = 2 * 3 * n

    def body(*refs):
        s_refs, l_refs = refs[:n], refs[n:2 * n]
        sems = refs[2 * n + 1:2 * n + 1 + n_sem]
        token = refs[-1]
        for cp in _chip_copies(s_refs, l_refs, sems, scatter, False):
            cp.start()
        token[...] = jnp.zeros_like(token)

    hbm = lambda a: pltpu.HBM(a.shape, a.dtype)
    res = pl.pallas_call(
        body, name=name,
        out_shape=(*[pltpu.SemaphoreType.DMA(())] * n_sem, *[hbm(a) for a in srcs], *[hbm(a) for a in lands],
                   jax.ShapeDtypeStruct((8, LANES), F32)),
        in_specs=[HBM] * (2 * n) + [pl.BlockSpec(memory_space=pl.ANY)],
        out_specs=(*[SEM] * n_sem, *[HBM] * (2 * n), VMEM),
        input_output_aliases={k: n_sem + k for k in range(2 * n)},
        compiler_params=pltpu.CompilerParams(has_side_effects=IN_FLIGHT),
    )(*[pltpu.with_memory_space_constraint(a, pltpu.HBM) for a in list(srcs) + list(lands)], after)
    return res[:n_sem], res[n_sem:n_sem + n], res[n_sem + n:n_sem + 2 * n], res[-1]


def _chips_wait(sems, srcs, lands, after, *, scatter, name):
    n = len(srcs)
    n_sem = len(sems)

    def body(*refs):
        s_refs, l_refs = refs[:n], refs[n:2 * n]
        sem_refs = refs[2 * n:2 * n + n_sem]
        for cp in _chip_copies(s_refs, l_refs, sem_refs, scatter, False):
            cp.wait_send()
        for cp in _chip_copies(s_refs, l_refs, sem_refs, scatter, True):
            cp.wait_recv()

    hbm = lambda a: pltpu.HBM(a.shape, a.dtype)
    res = pl.pallas_call(
        body, name=name, out_shape=tuple(hbm(a) for a in list(srcs) + list(lands)),
        in_specs=[HBM] * (2 * n) + [SEM] * n_sem + [pl.BlockSpec(memory_space=pl.ANY)], out_specs=tuple([HBM] * (2 * n)),
        input_output_aliases={k: k for k in range(2 * n)},
        compiler_params=pltpu.CompilerParams(has_side_effects=IN_FLIGHT),
    )(*srcs, *lands, *sems, after)
    return res[:n], res[n:]


def _all_gather_devices(rows, *, name, after=None):
    deltas = [(dx, dy, dc) for dx in (0, 1) for dy in (0, 1) for dc in (0, 1)][1:]
    order = [] if after is None else [after]

    def body(x_ref, *rest):
        o_ref, send_sems, recv_sems = rest[-3:]
        x, y, c = lax.axis_index("x"), lax.axis_index("y"), lax.axis_index("c")
        me = 4 * x + 2 * y + c
        o_ref[me] = x_ref[...]
        sends, recvs = [], []
        for k, (dx, dy, dc) in enumerate(deltas):
            tx, ty, tc = _flip(x, dx), _flip(y, dy), _flip(c, dc)
            sends.append(pltpu.make_async_remote_copy(src_ref=x_ref, dst_ref=o_ref.at[me], send_sem=send_sems.at[k],
                                                      recv_sem=recv_sems.at[k], device_id=(tx, ty, tc), device_id_type=MESH))
            recvs.append(pltpu.make_async_remote_copy(src_ref=x_ref, dst_ref=o_ref.at[4 * tx + 2 * ty + tc],
                                                      send_sem=send_sems.at[k], recv_sem=recv_sems.at[k],
                                                      device_id=(tx, ty, tc), device_id_type=MESH))
        for cp in sends:
            cp.start()
        for cp in recvs:
            cp.wait_recv()
        for cp in sends:
            cp.wait_send()

    return pl.pallas_call(
        body, name=name, out_shape=jax.ShapeDtypeStruct((N_DEV,) + rows.shape, rows.dtype),
        in_specs=[VMEM] + [pl.BlockSpec(memory_space=pl.ANY)] * len(order), out_specs=VMEM,
        scratch_shapes=[pltpu.SemaphoreType.DMA((N_DEV - 1,)), pltpu.SemaphoreType.DMA((N_DEV - 1,))],
    )(rows, *order)


WEIGHTS = ("ada_w", "ada_b", "ln_g", "ln_b", "e_w_in", "gmlp_norm_g", "gmlp_norm_b", "gmlp_ws", "gmlp_bs", "pool_w",
           "pool_b", "pool_scale", "e_w_out", "o_w_in", "mla_q_norm_g", "mla_kv_norm_g", "mla_w_uq", "mla_w_uk",
           "mla_w_uv", "o_w_out")
SMALL = ("ln_g", "ln_b", "gmlp_norm_g", "gmlp_norm_b", "gmlp_bs", "pool_b", "pool_scale", "mla_kv_norm_g", "mla_q_norm_g")


def _pad_cols(v, n):
    return jnp.concatenate([v, jnp.zeros((v.shape[0], n - v.shape[1]), v.dtype)], axis=1) if n > v.shape[1] else v


def _halves(g):
    return g.reshape(g.shape[0], 2, g.shape[1] // 2, g.shape[2])


def kernel(x, c, positions, ada_w, ada_b, ln_g, ln_b, e_w_in, gmlp_norm_g, gmlp_norm_b, gmlp_ws, gmlp_bs, pool_w, pool_b, pool_scale, e_w_out, o_w_in, mla_q_norm_g, mla_kv_norm_g, mla_w_uq, mla_w_uk, mla_w_uv, o_w_out, loss_target, m_ada_w, m_ada_b, m_ln_g, m_ln_b, m_e_w_in, m_gmlp_norm_g, m_gmlp_norm_b, m_gmlp_ws, m_gmlp_bs, m_pool_w, m_pool_b, m_pool_scale, m_e_w_out, m_o_w_in, m_mla_q_norm_g, m_mla_kv_norm_g, m_mla_w_uq, m_mla_w_uk, m_mla_w_uv, m_o_w_out, v_ada_w, v_ada_b, v_ln_g, v_ln_b, v_e_w_in, v_gmlp_norm_g, v_gmlp_norm_b, v_gmlp_ws, v_gmlp_bs, v_pool_w, v_pool_b, v_pool_scale, v_e_w_out, v_o_w_in, v_mla_q_norm_g, v_mla_kv_norm_g, v_mla_w_uq, v_mla_w_uk, v_mla_w_uv, v_o_w_out):
    args = dict(locals())
    weights = {n: args[n] for n in WEIGHTS}
    mom = {n: args["m_" + n] for n in WEIGHTS}
    var = {n: args["v_" + n] for n in WEIGHTS}
    ax, ay, ac = lax.axis_index("x"), lax.axis_index("y"), lax.axis_index("c")
    chip = 2 * ax + ay
    dev = 2 * chip + ac
    d = D_MODEL
    x2 = x[0]
    target = loss_target[0]
    q_rank_sh = mla_q_norm_g.shape[1]

    empty_zone = lambda w: lax.dynamic_update_slice(lax.empty((N_CHIPS,) + w.shape, w.dtype), w[None], (chip, 0, 0))
    shards0 = [w.astype(BF16) for w in (pool_w[0].reshape(-1, POOL_GROUP_DIM), e_w_out[0])]
    shards1 = [w.astype(BF16) for w in (o_w_in[0], mla_w_uq[0].reshape(q_rank_sh, -1), o_w_out[0])]
    w_in0, = _gather_weights([e_w_in[0].astype(BF16)], name="gather_weights")
    wuk_hrd = jnp.transpose(mla_w_uk[0], (1, 0, 2)).astype(BF16)
    wuk_hdr = jnp.transpose(mla_w_uk[0], (1, 2, 0)).astype(BF16)
    wuv_hrv = jnp.transpose(mla_w_uv[0], (1, 0, 2)).astype(BF16)
    wuv_hvr = jnp.transpose(mla_w_uv[0], (1, 2, 0)).astype(BF16)
    ws = gmlp_ws[0]
    ws_t = jnp.transpose(ws, (0, 2, 1))
    bs_t = _pad_cols(gmlp_bs[0].T, LANES)

    inv = 1.0 / (ROPE_THETA ** (jnp.arange(0, MLA_ROPE, 2, dtype=F32) / MLA_ROPE))
    ang = positions[0].astype(F32)[:, None] * inv
    cos_t = jnp.tile(jnp.cos(ang), (1, 4))
    sin_t = jnp.concatenate([-jnp.sin(ang), -jnp.sin(ang), jnp.sin(ang), jnp.sin(ang)], axis=1)

    c_all = _all_gather_devices(c.reshape(8, LANES), after=w_in0, name="gather_c").reshape(N_DEV, d)
    cols = ada_w.shape[2]
    ada_b_mine = lax.dynamic_slice_in_dim(ada_b, chip * cols, cols, axis=1)[:, None, :]
    mod_sh = _ada_mod(c_all, ada_w, ada_b_mine, name="ada_mod")
    q_norm_rows = jnp.zeros((8, cols), F32).at[0, :q_rank_sh].set(mla_q_norm_g[0])
    mod_all = _all_gather_chips(jnp.concatenate([mod_sh.reshape(2 * N_DEV, cols), q_norm_rows]), name="gather_mod")
    q_norm_g = mod_all[:, 2 * N_DEV, :q_rank_sh].reshape(1, -1)
    mod_all = jnp.transpose(mod_all[:, :2 * N_DEV].reshape(N_CHIPS, 2, N_DEV, cols), (1, 2, 0, 3)).reshape(2, N_DEV, 3 * d)
    mod = lax.dynamic_index_in_dim(mod_all, dev, axis=1, keepdims=False)
    shift = [mod[l:l + 1, :d] for l in range(2)]
    scale = [mod[l:l + 1, d:2 * d] for l in range(2)]
    gate = [mod[l:l + 1, 2 * d:] for l in range(2)]
    flight0 = _chips_start(shards0, [empty_zone(w) for w in shards0], mod, scatter=False, name="gather0_start")
    flight1 = _chips_start(shards1, [empty_zone(w) for w in shards1], flight0[3], scatter=False, name="gather1_start")

    scale[0] = scale[0] + flight1[3][:1, :1]
    h0 = _modulate(x2, scale[0], shift[0], name="modulate0")
    proj0 = _matmul(h0, w_in0, b_stacked=True, tm=1024, tn=1280, out_dtype=BF16, name="proj0")
    pool_w_g, w_out0 = _chips_wait(*flight0[:3], proj0, scatter=False, name="gather0_wait")[1]
    pool_w_bf = jnp.transpose(pool_w_g.reshape(N_CHIPS, POOL_GROUPS, -1, POOL_GROUP_DIM), (1, 0, 2, 3)).reshape(
        POOL_GROUPS, POOL_GROUP_DIM, POOL_GROUP_DIM)
    w_out0 = w_out0.reshape(-1, d)
    mix0 = _even_fwd(proj0, ws, bs_t, gmlp_norm_g, gmlp_norm_b, pool_w_bf, pool_b, pool_scale, name="even_fwd")
    y0, x1, h1 = _out_resid_ln(mix0, w_out0, x2, gate[0], ln_g[0:1], ln_b[0:1], scale[1], shift[1], name="out0_ln")

    w_in1_g, w_uq_g, w_out1 = _chips_wait(*flight1[:3], h1, scatter=False, name="gather1_wait")[1]
    w_out1 = w_out1.reshape(-1, d)
    w_in1 = jnp.transpose(w_in1_g, (1, 0, 2)).reshape(d, ODD_IN)
    w_in1 = jnp.concatenate([w_in1[:, ODD_SMALL:], _pad_cols(w_in1[:, :ODD_SMALL], ODD_SMALL_PAD)], axis=1)
    w_uq = w_uq_g.reshape(MLA_Q_RANK, MLA_HEADS, MLA_NOPE + MLA_ROPE)
    w_uq_nope = w_uq[:, :, :MLA_NOPE].reshape(MLA_Q_RANK, -1)
    w_uq_rope = jnp.transpose(w_uq[:, :, MLA_NOPE:].reshape(MLA_Q_RANK, MLA_HEADS // 2, 2, 2, ROPE_HALF),
                              (0, 1, 3, 2, 4)).reshape(MLA_Q_RANK, -1)
    proj1 = _matmul(h1, w_in1, tm=1024, tn=1280, out_dtype=BF16, name="proj1")
    q_cn, keys = _mla_prep(proj1, q_norm_g, mla_kv_norm_g, cos_t, sin_t, name="mla_prep")
    q_nope = _matmul(q_cn, w_uq_nope, tm=1024, tn=2048, name="q_nope", out_dtype=BF16)
    q_rope_pre = _matmul(q_cn, w_uq_rope, tm=1024, name="q_rope")
    q = _q_build(q_nope, q_rope_pre, wuk_hdr, cos_t, sin_t, name="q_build")
    o_lat, lse = _attn_fwd(q, keys, name="attn_fwd")
    og = _o_build(o_lat, wuv_hrv, proj1, name="o_build")

    dy1, dres1, g_ln_g1, g_ln_b1, dgate1, loss = _out_loss_ln_bwd(
        og, w_out1, x1, gate[1], ln_g[1:2], ln_b[1:2], target, name="out1_loss_ln")
    dg1 = _matmul(dy1, w_out1, trans_b=True, tn=2048, out_dtype=BF16, name="d_og")
    g_w_out1 = _matmul(og, dy1, trans_a=True, out_dtype=BF16, tm=1024, tk=4096, name="g_out1")
    do_lat, dz, g_uv = _o_bwd(dg1, proj1, o_lat, wuv_hrv, wuv_hvr, name="o_bwd")
    dq, dkeys = _attn_bwd(q, keys, do_lat, o_lat, lse, name="attn_bwd")
    dq_all, g_uk = _q_bwd(dq, q_nope, wuk_hrd, cos_t, sin_t, name="q_bwd")
    n_grp = MLA_HEADS // Q_HEAD_GROUP
    w_uq_all = jnp.concatenate([w_uq_nope.reshape(MLA_Q_RANK, n_grp, -1), w_uq_rope.reshape(MLA_Q_RANK, n_grp, -1)],
                               axis=2).reshape(MLA_Q_RANK, -1)
    dq_cn = _matmul(dq_all, w_uq_all, trans_b=True, tm=1024, tk=3072, name="d_qcn")
    g_uq_all = _matmul(q_cn, dq_all, trans_a=True, out_dtype=BF16, name="g_uq").reshape(MLA_Q_RANK, n_grp, -1)
    g_uq_nope = g_uq_all[:, :, :Q_HEAD_GROUP * MLA_NOPE].reshape(MLA_Q_RANK, -1)
    g_uq_rope = g_uq_all[:, :, Q_HEAD_GROUP * MLA_NOPE:].reshape(MLA_Q_RANK, -1)
    dsmall, g_qg, g_kvg = _mla_prep_bwd(proj1, dq_cn, dkeys, q_norm_g, mla_kv_norm_g, cos_t, sin_t, name="mla_prep_bwd")
    dproj1 = jnp.concatenate([dz, dsmall], axis=1)
    g_w_in1 =_matmul(h1, dproj1, trans_a=True, out_dtype=BF16, tm=1024, tn=1280, name="g_in1")

    g_uq_rope = jnp.transpose(g_uq_rope.reshape(MLA_Q_RANK, MLA_HEADS // 2, 2, 2, ROPE_HALF), (0, 1, 3, 2, 4))
    g_uq = jnp.concatenate([g_uq_nope.reshape(MLA_Q_RANK, MLA_HEADS, MLA_NOPE), g_uq_rope.reshape(MLA_Q_RANK, MLA_HEADS, MLA_ROPE)], axis=2)
    g_w_in1 = jnp.concatenate([g_w_in1[:, MLA_WIDTH:MLA_WIDTH + ODD_SMALL], g_w_in1[:, :MLA_WIDTH]], axis=1)
    g_w_in1 = jnp.transpose(g_w_in1.reshape(d, N_CHIPS, -1), (1, 0, 2))
    big1 = [
        _halves(g_w_in1),
        _halves(g_uq.reshape(N_CHIPS, q_rank_sh, -1)),
        _halves(g_w_out1.reshape(N_CHIPS, -1, d)),
        _halves(g_uk.astype(BF16).reshape(N_CHIPS, -1, MLA_NOPE)),
        _halves(g_uv.astype(BF16).reshape(N_CHIPS, -1, MLA_V)),
    ]
    parts1 = _reduce_sibling(big1, name="reduce_sibling1")
    flight2 = _chips_start(parts1, [lax.empty(p.shape, BF16) for p in parts1], loss, scatter=True, name="reduce1_start")

    gate[0] = gate[0] + flight2[3][:1, :1]
    dy0, dres0, g_ln_g0, g_ln_b0, dgate0, dscale1, dshift1 = _dh_mid_ln_bwd(
        dproj1, w_in1, x2, y0, gate[0], ln_g[0:1], ln_b[0:1], dres1, scale[1], x1, name="d_h1_mid_ln")
    dmix0 = _matmul(dy0, w_out0, trans_b=True, tn=2048, out_dtype=BF16, name="d_mix0")
    g_w_out0 = _matmul(mix0, dy0, trans_a=True, out_dtype=BF16, tm=1024, tk=4096, name="g_out0")
    dproj0, g_ws, g_bs_t, g_ng, g_nb, g_pw, g_pb, g_ps = _even_bwd(
        proj0, dmix0, ws, ws_t, bs_t, gmlp_norm_g, gmlp_norm_b, pool_w_bf, pool_b, pool_scale, name="even_bwd")
    g_w_in0 = _matmul(h0, dproj0, trans_a=True, out_dtype=BF16, out_stacked=True, tm=1024, tn=1280, name="g_in0")

    g_pw = jnp.transpose(g_pw.astype(BF16).reshape(POOL_GROUPS, N_CHIPS, -1, POOL_GROUP_DIM), (1, 0, 2, 3))
    big0 = [
        _halves(g_w_in0),
        _halves(g_pw.reshape(N_CHIPS, -1, POOL_GROUP_DIM)),
        _halves(g_w_out0.reshape(N_CHIPS, -1, d)),
        _halves(g_ws.astype(BF16)),
    ]
    parts0 = _reduce_sibling(big0, name="reduce_sibling0")
    parts1, landed1 = _chips_wait(*flight2[:3], parts0[0], scatter=True, name="reduce1_wait")
    flight3 = _chips_start(parts0, [lax.empty(p.shape, BF16) for p in parts0], landed1[0], scatter=True, name="reduce0_start")
    grad_x, dscale0, dshift0 = _dh_input_bwd(dproj0, w_in0, x2, dres0, scale[0], after=flight3[3], name="d_h0_input")

    small_local = {
        "ln_g": jnp.concatenate([g_ln_g0, g_ln_g1]), "ln_b": jnp.concatenate([g_ln_b0, g_ln_b1]),
        "gmlp_norm_g": g_ng, "gmlp_norm_b": g_nb, "gmlp_bs": g_bs_t[:, :GMLP_HEADS].T, "pool_b": g_pb, "pool_scale": g_ps,
        "mla_kv_norm_g": g_kvg, "mla_q_norm_g": g_qg,
    }
    n_mod = 2 * 3 * d
    vec = jnp.concatenate([dshift0, dscale0, dgate0, dshift1, dscale1, dgate1]
                          + [small_local[n].reshape(1, -1) for n in SMALL] + [loss], axis=1)
    n_vec = vec.shape[1]
    vec = _pad_cols(vec, -(-n_vec // (8 * LANES)) * 8 * LANES).reshape(-1, LANES)
    vec_all = _all_gather_devices(vec, name="gather_small")
    vec_sum = _sum_devices(vec_all, name="sum_small").reshape(-1)
    dmod_all = vec_all.reshape(N_DEV, -1)[:, :n_mod].reshape(N_DEV, 2, 3 * d)
    dmod_sh = jnp.transpose(lax.dynamic_slice_in_dim(dmod_all, chip * cols, cols, axis=2), (1, 0, 2))
    dmod_sh = jnp.concatenate([dmod_sh, jnp.zeros((2, LANES - N_DEV, cols), F32)], axis=1)
    grads = {"ada_w": _ada_grad(_pad_cols(c_all.T, LANES), dmod_sh, name="ada_grad"), "ada_b": vec_sum[:n_mod].reshape(2, 3 * d)}
    off = n_mod
    for n in SMALL:
        sz = small_local[n].size
        grads[n] = vec_sum[off:off + sz]
        off += sz
    grads["mla_q_norm_g"] = lax.dynamic_slice_in_dim(grads["mla_q_norm_g"], chip * q_rank_sh, q_rank_sh)
    for n in SMALL:
        grads[n] = grads[n].reshape(weights[n].shape)

    parts0, landed0 = _chips_wait(*flight3[:3], grads["ada_w"], scatter=True, name="reduce0_wait")
    totals = _reduce_chips(list(parts0) + list(parts1), list(landed0) + list(landed1), name="reduce_chips")
    for n, t in zip(("e_w_in", "pool_w", "e_w_out", "gmlp_ws", "o_w_in", "mla_w_uq", "o_w_out"), totals):
        if n != "gmlp_ws":
            grads[n] = t.reshape(weights[n].shape)
    rep = jnp.concatenate([t.reshape(-1, LANES) for t in (totals[3], totals[7], totals[8])])
    rep_land = lax.dynamic_update_slice(lax.empty((N_CHIPS,) + rep.shape, F32), rep[None], (chip, 0, 0))
    flight4 = _chips_start([rep], [rep_land], totals[0], scatter=False, name="gather_rep_start")

    delta, new_m, new_v = {}, {}, {}
    replicated = ("gmlp_ws", "mla_w_uk", "mla_w_uv")
    large = [n for n in WEIGHTS if n not in SMALL and n != "ada_b"]
    for n in large:
        if n not in replicated:
            delta[n], new_m[n], new_v[n] = _adamw(weights[n], grads[n], mom[n], var[n], after=flight4[3], name="adamw_" + n)
    rep = _chips_wait(*flight4[:3], delta["e_w_in"], scatter=False, name="gather_rep_wait")[1][0]
    r_ws, r_uk = GMLP_BLOCK, 4 * MLA_KV_RANK
    grads["gmlp_ws"] = rep[:, :r_ws].reshape(weights["gmlp_ws"].shape)
    grads["mla_w_uk"] = jnp.transpose(rep[:, r_ws:r_ws + r_uk].reshape(MLA_HEADS, MLA_KV_RANK, MLA_NOPE), (1, 0, 2))[None]
    grads["mla_w_uv"] = jnp.transpose(rep[:, r_ws + r_uk:].reshape(MLA_HEADS, MLA_KV_RANK, MLA_V), (1, 0, 2))[None]
    for n in replicated:
        delta[n], new_m[n], new_v[n] = _adamw(weights[n], grads[n], mom[n], var[n], name="adamw_" + n)
    small = [n for n in WEIGHTS if n not in large]
    ds, ms, vs = _adamw_small([weights[n] for n in small], [grads[n] for n in small], [mom[n] for n in small],
                              [var[n] for n in small], name="adamw_small")
    for n, dn, mn, vn in zip(small, ds, ms, vs):
        delta[n], new_m[n], new_v[n] = dn, mn, vn

    return (vec_sum[n_vec - 1], grad_x[None], *[grads[n] for n in WEIGHTS], *[delta[n] for n in WEIGHTS],
            *[new_m[n] for n in WEIGHTS], *[new_v[n] for n in WEIGHTS])
```

```python
import jax
import jax.numpy as jnp
from jax import lax
from jax.experimental import pallas as pl
from jax.experimental.pallas import tpu as pltpu

F32 = jnp.float32
BF16 = jnp.bfloat16
MESH = pl.DeviceIdType.MESH

D_MODEL = 1024
CHUNK = 64
LN_EPS = 1e-5
GMLP_HEADS = 4
GMLP_HEAD_DIM = 256
GMLP_BLOCK = 128
POOL_WINDOWS = (2, 4, 8, 16)
POOL_GROUPS = 4
POOL_GROUP_DIM = 256
POOL_HALO = 16
EVEN_IN = 5120
MLA_HEADS = 16
MLA_NOPE = 128
MLA_ROPE = 64
MLA_V = 128
MLA_Q_RANK = 256
MLA_KV_RANK = 128
MLA_WIDTH = MLA_HEADS * MLA_V
ODD_IN = 2496
ODD_SMALL = MLA_Q_RANK + MLA_KV_RANK + MLA_ROPE
ODD_SMALL_PAD = 512
QK_PAD = 256
ROPE_THETA = 10000.0
ATTN_SCALE = (MLA_NOPE + MLA_ROPE) ** -0.5
DEEPNORM_ALPHA = (2.0 * 2) ** 0.25
ADAM_LR = 0.001
ADAM_B1 = 0.9
ADAM_B2 = 0.999
ADAM_EPS = 1e-08
ADAM_WD = 0.01
ADAM_STEP = 10
NEG = -1e30
LANES = 128
N_DEV = 8
N_CHIPS = 4
VMEM_LIMIT_BYTES = 56 * 1024 * 1024
HBM = pl.BlockSpec(memory_space=pltpu.HBM)
VMEM = pl.BlockSpec(memory_space=pltpu.VMEM)


def _params(*sem):
    return pltpu.CompilerParams(dimension_semantics=sem if sem else None, vmem_limit_bytes=VMEM_LIMIT_BYTES)


def _tile(dim, pref):
    for t in (pref, 2048, 1280, 1024, 512, 256, 128):
        if t <= min(pref, dim) and dim % t == 0:
            return t
    return dim


def _sigmoid(z):
    return 1.0 / (1.0 + jnp.exp(-z))


def _dot(a, b, dims):
    return lax.dot_general(a, b, (dims, ((), ())), preferred_element_type=F32)


NN = ((1,), (0,))
NT = ((1,), (1,))
TN = ((0,), (0,))


def _matmul(a, b, *, name, trans_a=False, trans_b=False, out_dtype=F32, b_stacked=False, out_stacked=False,
            tm=512, tn=1024, tk=2048, after=None):
    k, m = a.shape if trans_a else a.shape[::-1]
    if b_stacked:
        assert not trans_b
        ns, kb, n_sh = b.shape
        n = ns * n_sh
    else:
        n, kb = b.shape if trans_b else b.shape[::-1]
    assert k == kb, (a.shape, b.shape)
    tm = _tile(m, tm)
    tn, tk = _tile(n // N_CHIPS if b_stacked or out_stacked else n, tn), _tile(k, tk)
    nk = k // tk
    per = max((n // N_CHIPS) // tn, 1)
    dims = ((0 if trans_a else 1,), (1 if trans_b else 0,))

    def body_one(a_ref, b_ref, *rest):
        o_ref = rest[-1]
        o_ref[...] = _dot(a_ref[...].astype(BF16), b_ref[...].astype(BF16), dims).astype(out_dtype)

    def body_acc(a_ref, b_ref, *rest):
        o_ref, acc_ref = rest[-2:]
        kk = pl.program_id(2)

        @pl.when(kk == 0)
        def _():
            acc_ref[...] = jnp.zeros_like(acc_ref)

        acc_ref[...] += _dot(a_ref[...].astype(BF16), b_ref[...].astype(BF16), dims)

        @pl.when(kk == nk - 1)
        def _():
            o_ref[...] = acc_ref[...].astype(out_dtype)

    a_spec = pl.BlockSpec((tk, tm), lambda i, j, kk: (kk, i)) if trans_a else pl.BlockSpec((tm, tk), lambda i, j, kk: (i, kk))
    if b_stacked:
        b_spec = pl.BlockSpec((None, tk, tn), lambda i, j, kk: (j // per, kk, j % per))
    elif trans_b:
        b_spec = pl.BlockSpec((tn, tk), lambda i, j, kk: (j, kk))
    else:
        b_spec = pl.BlockSpec((tk, tn), lambda i, j, kk: (kk, j))
    if out_stacked:
        o_spec = pl.BlockSpec((None, tm, tn), lambda i, j, kk: (j // per, i, j % per))
        o_shape = jax.ShapeDtypeStruct((N_CHIPS, m, n // N_CHIPS), out_dtype)
    else:
        o_spec = pl.BlockSpec((tm, tn), lambda i, j, kk: (i, j))
        o_shape = jax.ShapeDtypeStruct((m, n), out_dtype)
    order = [] if after is None else [after]
    return pl.pallas_call(
        body_one if nk == 1 else body_acc, name=name, grid=(m // tm, n // tn, nk),
        in_specs=[a_spec, b_spec] + [pl.BlockSpec(memory_space=pl.ANY)] * len(order),
        out_specs=o_spec, out_shape=o_shape, scratch_shapes=[] if nk == 1 else [pltpu.VMEM((tm, tn), F32)],
        compiler_params=_params("parallel", "parallel", "arbitrary"),
    )(a, b, *order)


def _matmul_rows(a, b, epilogue, row_ins, vec_ins, row_outs, vec_outs, *, name, trans_b=False, b_stacked=False,
                 tm=512, tk=2048, after=None):
    m, k = a.shape
    if b_stacked:
        ns, n, n_sh = b.shape
        assert trans_b and ns * n_sh == k
        tk = k
    else:
        n = b.shape[0] if trans_b else b.shape[1]
        tk = _tile(k, tk)
    tm = _tile(m, tm)
    nk = k // tk
    dims = ((1,), (1 if trans_b else 0,))
    n_ri, n_vi, n_ro, n_vo = len(row_ins), len(vec_ins), len(row_outs), len(vec_outs)
    order = [] if after is None else [after]

    def body(*refs):
        a_ref, b_ref = refs[:2]
        pos = 2
        rin = refs[pos:pos + n_ri]
        pos += n_ri
        vin = refs[pos:pos + n_vi]
        pos += n_vi + len(order)
        rout = refs[pos:pos + n_ro]
        pos += n_ro
        vout = refs[pos:pos + n_vo]
        first = pl.program_id(0) == 0
        if b_stacked:
            part = _dot(a_ref[:, :n_sh].astype(BF16), b_ref[0].astype(BF16), dims)
            for sh in range(1, ns):
                part = part + _dot(a_ref[:, sh * n_sh:(sh + 1) * n_sh].astype(BF16), b_ref[sh].astype(BF16), dims)
        else:
            part = _dot(a_ref[...].astype(BF16), b_ref[...].astype(BF16), dims)
        if nk == 1:
            epilogue(part, first, rin, vin, rout, vout)
        else:
            acc_ref = refs[-1]
            kk = pl.program_id(1)

            @pl.when(kk == 0)
            def _():
                acc_ref[...] = part

            @pl.when(kk > 0)
            def _():
                acc_ref[...] += part

            @pl.when(kk == nk - 1)
            def _():
                epilogue(acc_ref[...], first, rin, vin, rout, vout)

    a_spec = pl.BlockSpec((tm, tk), lambda i, kk: (i, kk))
    if b_stacked:
        b_spec = pl.BlockSpec((ns, n, n_sh), lambda i, kk: (0, 0, 0))
    elif trans_b:
        b_spec = pl.BlockSpec((n, tk), lambda i, kk: (0, kk))
    else:
        b_spec = pl.BlockSpec((tk, n), lambda i, kk: (kk, 0))
    row = pl.BlockSpec((tm, n), lambda i, kk: (i, 0))
    vec = lambda w: pl.BlockSpec((1, w), lambda i, kk: (0, 0))
    return pl.pallas_call(
        body, name=name, grid=(m // tm, nk),
        in_specs=[a_spec, b_spec] + [row] * n_ri + [vec(v.shape[1]) for v in vec_ins] + [pl.BlockSpec(memory_space=pl.ANY)] * len(order),
        out_specs=[row] * n_ro + [vec(w) for w in vec_outs],
        out_shape=[jax.ShapeDtypeStruct((m, n), dt) for dt in row_outs] + [jax.ShapeDtypeStruct((1, w), F32) for w in vec_outs],
        scratch_shapes=[] if nk == 1 else [pltpu.VMEM((tm, n), F32)],
        compiler_params=_params("arbitrary", "arbitrary"),
    )(a, b, *row_ins, *vec_ins, *order)


def _row_spec(ts, d):
    return pl.BlockSpec((ts, d), lambda i: (i, 0))


def _vec_spec(d):
    return pl.BlockSpec((1, d), lambda i: (0, 0))


def _modulate(x, scale, shift, *, name):
    s, d = x.shape
    ts = _tile(s, 512)

    def body(x_ref, sc_ref, sh_ref, h_ref):
        h_ref[...] = (x_ref[...] * (1.0 + sc_ref[...]) + sh_ref[...]).astype(BF16)

    return pl.pallas_call(
        body, name=name, grid=(s // ts,), in_specs=[_row_spec(ts, d), _vec_spec(d), _vec_spec(d)],
        out_specs=_row_spec(ts, d), out_shape=jax.ShapeDtypeStruct((s, d), BF16), compiler_params=_params("parallel"),
    )(x, scale, shift)


def _ln_stats(pre):
    mu = jnp.mean(pre, axis=-1, keepdims=True)
    xc = pre - mu
    var = jnp.mean(xc * xc, axis=-1, keepdims=True)
    rstd = lax.rsqrt(var + LN_EPS)
    return xc * rstd, rstd


def _ln_bwd_rows(dout, xhat, rstd, g):
    dxh = dout * g
    m1 = jnp.mean(dxh, axis=-1, keepdims=True)
    m2 = jnp.mean(dxh * xhat, axis=-1, keepdims=True)
    return rstd * (dxh - m1 - xhat * m2)


def _colsum(v):
    return jnp.sum(v, axis=0, keepdims=True)


def _out_resid_ln(mix, w_out, x, gate, g, b, scale_next, shift_next, *, name):
    def epilogue(y, first, rin, vin, rout, vout):
        (x_ref,), (gate_ref, g_ref, b_ref, sc_ref, sh_ref), (y_ref, xn_ref, h_ref) = rin, vin, rout
        y_ref[...] = y
        pre = DEEPNORM_ALPHA * x_ref[...] + (1.0 + gate_ref[...]) * y
        xhat, _ = _ln_stats(pre)
        xn = xhat * g_ref[...] + b_ref[...]
        xn_ref[...] = xn
        h_ref[...] = (xn * (1.0 + sc_ref[...]) + sh_ref[...]).astype(BF16)

    return _matmul_rows(mix, w_out, epilogue, [x], [gate, g, b, scale_next, shift_next], [F32, F32, BF16], [], name=name)


def _out_loss_ln_bwd(og, w_out, x, gate, g, b, target, *, name):
    d = x.shape[1]

    def epilogue(yv, first, rin, vin, rout, vout):
        (x_ref, t_ref), (gate_ref, g_ref, b_ref), (dy_ref, dres_ref), (dg_ref, db_ref, dgate_ref, loss_ref) = rin, vin, rout, vout

        @pl.when(first)
        def _():
            for r in vout:
                r[...] = jnp.zeros_like(r)

        pre = DEEPNORM_ALPHA * x_ref[...] + (1.0 + gate_ref[...]) * yv
        xhat, rstd = _ln_stats(pre)
        diff = xhat * g_ref[...] + b_ref[...] - t_ref[...]
        loss_ref[...] += (0.5 / d) * jnp.sum(jnp.sum(diff * diff, axis=1, keepdims=True), axis=0, keepdims=True)
        dout = diff * (1.0 / d)
        dpre = _ln_bwd_rows(dout, xhat, rstd, g_ref[...])
        dy_ref[...] = (dpre * (1.0 + gate_ref[...])).astype(BF16)
        dres_ref[...] = DEEPNORM_ALPHA * dpre
        dg_ref[...] += _colsum(dout * xhat)
        db_ref[...] += _colsum(dout)
        dgate_ref[...] += _colsum(dpre * yv)

    return _matmul_rows(og, w_out, epilogue, [x, target], [gate, g, b], [BF16, F32], [d, d, d, 1], name=name)


def _dh_mid_ln_bwd(dproj, w_in, x, y, gate, g, b, dres_next, scale_next, x_next, *, name):
    d = x.shape[1]

    def epilogue(dh, first, rin, vin, rout, vout):
        (x_ref, y_ref, dr_ref, xn_ref), (gate_ref, g_ref, b_ref, sc_ref), (dy_ref, dres_ref) = rin, vin, rout
        dg_ref, db_ref, dgate_ref, dscale_ref, dshift_ref = vout

        @pl.when(first)
        def _():
            for r in vout:
                r[...] = jnp.zeros_like(r)

        dout = dr_ref[...] + dh * (1.0 + sc_ref[...])
        dscale_ref[...] += _colsum(dh * xn_ref[...])
        dshift_ref[...] += _colsum(dh)
        yv = y_ref[...]
        pre = DEEPNORM_ALPHA * x_ref[...] + (1.0 + gate_ref[...]) * yv
        xhat, rstd = _ln_stats(pre)
        dpre = _ln_bwd_rows(dout, xhat, rstd, g_ref[...])
        dy_ref[...] = (dpre * (1.0 + gate_ref[...])).astype(BF16)
        dres_ref[...] = DEEPNORM_ALPHA * dpre
        dg_ref[...] += _colsum(dout * xhat)
        db_ref[...] += _colsum(dout)
        dgate_ref[...] += _colsum(dpre * yv)

    return _matmul_rows(dproj, w_in, epilogue, [x, y, dres_next, x_next], [gate, g, b, scale_next], [BF16, F32], [d] * 5,
                        trans_b=True, tk=2560, name=name)


def _dh_input_bwd(dproj, w_in_stacked, x, dres, scale, *, name, after):
    d = x.shape[1]

    def epilogue(dh, first, rin, vin, rout, vout):
        (x_ref, dr_ref), (sc_ref,), (dx_ref,), (dscale_ref, dshift_ref) = rin, vin, rout, vout

        @pl.when(first)
        def _():
            for r in vout:
                r[...] = jnp.zeros_like(r)

        dx_ref[...] = dr_ref[...] + dh * (1.0 + sc_ref[...])
        dscale_ref[...] += _colsum(dh * x_ref[...])
        dshift_ref[...] += _colsum(dh)

    return _matmul_rows(dproj, w_in_stacked, epilogue, [x, dres], [scale], [F32], [d, d], trans_b=True, b_stacked=True,
                        tm=512, after=after, name=name)


def _chunk_mask(transposed=False):
    r = lax.broadcasted_iota(jnp.int32, (GMLP_BLOCK, GMLP_BLOCK), 0) // CHUNK
    c = lax.broadcasted_iota(jnp.int32, (GMLP_BLOCK, GMLP_BLOCK), 1) // CHUNK
    return (r <= c) if transposed else (c <= r)


def _window_sum(ext, steps, forward):
    rows = ext.shape[0]
    acc = ext
    for k in range(steps):
        shift = 1 << k
        acc = acc + pltpu.roll(acc, (rows - shift) if forward else shift, 0)
    return acc


def _pool_counts(first_row, rows, win):
    t = first_row + lax.broadcasted_iota(jnp.int32, (rows, 1), 0)
    return jnp.minimum(t + 1, win).astype(F32)


def _even_specs(t):
    col = lambda j: pl.BlockSpec((t, D_MODEL), lambda n: (n, j))
    per = t // POOL_HALO
    prev = pl.BlockSpec((POOL_HALO, D_MODEL), lambda n: (jnp.maximum(n * per - 1, 0), 3))
    return col, per, prev


def _full(shape):
    return pl.BlockSpec(shape, lambda n: (0,) * len(shape))


def _gmlp_head(v_h, ng, nb, w_bf):
    xhat, rstd = _ln_stats(v_h)
    vn = (xhat * ng + nb).astype(BF16)
    return xhat, rstd, vn, _dot(w_bf, vn, NN)


def _pool_group(xb_g, prev_g, first_row, grp):
    t = xb_g.shape[0]
    ext = jnp.concatenate([prev_g, xb_g], axis=0)
    tot = _window_sum(ext, grp + 1, False)[POOL_HALO:, :]
    cnt = _pool_counts(first_row, t, POOL_WINDOWS[grp])
    return tot / cnt - xb_g, cnt


def _even_fwd(proj, ws, bs_t, ng, nb, pool_w, pool_b, pool_scale, *, name):
    s = proj.shape[0]
    t = GMLP_BLOCK
    col, per, prev = _even_specs(t)

    def body(u_ref, v_ref, za_ref, xb_ref, zb_ref, xp_ref, ws_ref, bs_ref, ng_ref, nb_ref, pw_ref, pb_ref, ps_ref, o_ref):
        n = pl.program_id(0)
        mask = _chunk_mask()
        for h in range(GMLP_HEADS):
            c0 = h * GMLP_HEAD_DIM
            cs = slice(c0, c0 + GMLP_HEAD_DIM)
            w_bf = jnp.where(mask, ws_ref[h], 0.0).astype(BF16)
            _, _, _, sv = _gmlp_head(v_ref[:, cs].astype(F32),ng_ref[...], nb_ref[...], w_bf)
            sv = sv + bs_ref[:, h:h + 1]
            za = za_ref[:, cs].astype(F32)
            o_ref[:, cs] = (u_ref[:, cs].astype(F32) * sv * (za * _sigmoid(za))).astype(BF16)
        live = (n > 0).astype(F32)
        for grp in range(POOL_GROUPS):
            c0 = grp * POOL_GROUP_DIM
            cs = slice(c0, c0 + POOL_GROUP_DIM)
            pooled, _ = _pool_group(xb_ref[:, cs].astype(F32), xp_ref[:, cs].astype(F32) * live, n * t, grp)
            yb = _dot(pooled.astype(BF16), pw_ref[grp], NN) + pb_ref[:, cs]
            zb = zb_ref[:, cs].astype(F32)
            o_ref[:, D_MODEL + c0:D_MODEL + c0 + POOL_GROUP_DIM] = (yb * ps_ref[:, cs] * (zb * _sigmoid(zb))).astype(BF16)

    return pl.pallas_call(
        body, name=name, grid=(s // t,),
        in_specs=[col(0), col(1), col(2), col(3), col(4), prev,
                  _full((GMLP_HEADS, t, t)), _full((t, LANES)), _full((1, GMLP_HEAD_DIM)), _full((1, GMLP_HEAD_DIM)),
                  _full((POOL_GROUPS, POOL_GROUP_DIM, POOL_GROUP_DIM)), _full((1, D_MODEL)), _full((1, D_MODEL))],
        out_specs=pl.BlockSpec((t, 2 * D_MODEL), lambda n: (n, 0)),
        out_shape=jax.ShapeDtypeStruct((s, 2 * D_MODEL), BF16),
        compiler_params=_params("parallel"),
    )(proj, proj, proj, proj, proj, proj, ws, bs_t, ng, nb, pool_w, pool_b, pool_scale)


def _even_bwd(proj, dmix, ws, ws_t, bs_t, ng, nb, pool_w, pool_b, pool_scale, *, name):
    s = proj.shape[0]
    t = GMLP_BLOCK
    nblk = s // t
    col, per, prev = _even_specs(t)
    nxt = lambda j: pl.BlockSpec((POOL_HALO, D_MODEL), lambda n: (jnp.minimum((n + 1) * per, nblk * per - 1), j))

    def body(u_ref, v_ref, za_ref, xb_ref, zb_ref, xp_ref, zn_ref, da_ref, db_ref, dbn_ref,
             ws_ref, wst_ref, bs_ref, ng_ref, nb_ref, pw_ref, pb_ref, ps_ref,
             dp_ref, gws_ref, gbs_ref, gng_ref, gnb_ref, gpw_ref, gpb_ref, gps_ref):
        n = pl.program_id(0)

        @pl.when(n == 0)
        def _():
            for r in (gws_ref, gbs_ref, gng_ref, gnb_ref, gpw_ref, gpb_ref, gps_ref):
                r[...] = jnp.zeros_like(r)

        mask, mask_t = _chunk_mask(), _chunk_mask(True)
        lane = lax.broadcasted_iota(jnp.int32, (t, LANES), 1)
        ngv, nbv = ng_ref[...], nb_ref[...]
        for h in range(GMLP_HEADS):
            c0 = h * GMLP_HEAD_DIM
            cs = slice(c0, c0 + GMLP_HEAD_DIM)
            w_bf = jnp.where(mask, ws_ref[h], 0.0).astype(BF16)
            wt_bf = jnp.where(mask_t, wst_ref[h], 0.0).astype(BF16)
            xhat, rstd, vn, sv = _gmlp_head(v_ref[:, cs].astype(F32),ngv, nbv, w_bf)
            sv = sv + bs_ref[:, h:h + 1]
            za, u, da = za_ref[:, cs].astype(F32), u_ref[:, cs].astype(F32), da_ref[:, cs].astype(F32)
            sg = _sigmoid(za)
            sl = za * sg
            dp_ref[:, cs] = (da * sv * sl).astype(BF16)
            dp_ref[:, 2 * D_MODEL + c0:2 * D_MODEL + c0 + GMLP_HEAD_DIM] = (
                da * u * sv * (sg * (1.0 + za * (1.0 - sg)))).astype(BF16)
            dsv = da * u * sl
            gbs_ref[...] += jnp.where(lane == h, jnp.sum(dsv, axis=1, keepdims=True), 0.0)
            dsv_bf = dsv.astype(BF16)
            gws_ref[h] += jnp.where(mask, _dot(dsv_bf, vn, NT), 0.0)
            dvn = _dot(wt_bf, dsv_bf, NN)
            dp_ref[:, D_MODEL + c0:D_MODEL + c0 + GMLP_HEAD_DIM] = _ln_bwd_rows(dvn, xhat, rstd, ngv).astype(BF16)
            gng_ref[...] += _colsum(dvn * xhat)
            gnb_ref[...] += _colsum(dvn)
        live_prev = (n > 0).astype(F32)
        live_next = (n < nblk - 1).astype(F32)
        for grp in range(POOL_GROUPS):
            c0 = grp * POOL_GROUP_DIM
            cs = slice(c0, c0 + POOL_GROUP_DIM)
            xb = xb_ref[:, cs].astype(F32)
            pooled, cnt = _pool_group(xb, xp_ref[:, cs].astype(F32) * live_prev, n * t, grp)
            pooled_bf = pooled.astype(BF16)
            pw = pw_ref[grp]
            yb = _dot(pooled_bf, pw, NN) + pb_ref[:, cs]
            ps = ps_ref[:, cs]
            zb, db = zb_ref[:, cs].astype(F32), db_ref[:, cs].astype(F32)
            sg = _sigmoid(zb)
            sl = zb * sg
            dp_ref[:, 4 * D_MODEL + c0:4 * D_MODEL + c0 + POOL_GROUP_DIM] = (
                db * yb * ps * (sg * (1.0 + zb * (1.0 - sg)))).astype(BF16)
            dsl = db * sl
            dy = dsl * ps
            gps_ref[:, cs] += _colsum(dsl * yb)
            gpb_ref[:, cs] += _colsum(dy)
            dy_bf = dy.astype(BF16)
            gpw_ref[grp] += _dot(pooled_bf, dy_bf, TN)
            r = _dot(dy_bf, pw, NT)
            zn = zn_ref[:, cs].astype(F32)
            dyn = (dbn_ref[:, cs].astype(F32) * (zn * _sigmoid(zn)) * ps * live_next).astype(BF16)
            rn = _dot(dyn, pw, NT) / _pool_counts((n + 1) * t, POOL_HALO, POOL_WINDOWS[grp])
            ext = jnp.concatenate([r / cnt, rn], axis=0)
            dxb = _window_sum(ext, grp + 1, True)[:t, :] - r
            dp_ref[:, 3 * D_MODEL + c0:3 * D_MODEL + c0 + POOL_GROUP_DIM] = dxb.astype(BF16)

    out_shape = [
        jax.ShapeDtypeStruct((s, EVEN_IN), BF16),
        jax.ShapeDtypeStruct((GMLP_HEADS, t, t), F32), jax.ShapeDtypeStruct((t, LANES), F32),
        jax.ShapeDtypeStruct((1, GMLP_HEAD_DIM), F32), jax.ShapeDtypeStruct((1, GMLP_HEAD_DIM), F32),
        jax.ShapeDtypeStruct((POOL_GROUPS, POOL_GROUP_DIM, POOL_GROUP_DIM), F32),
        jax.ShapeDtypeStruct((1, D_MODEL), F32), jax.ShapeDtypeStruct((1, D_MODEL), F32),
    ]
    return pl.pallas_call(
        body, name=name, grid=(nblk,),
        in_specs=[col(0), col(1), col(2), col(3), col(4), prev, nxt(4),
                  pl.BlockSpec((t, D_MODEL), lambda n: (n, 0)), pl.BlockSpec((t, D_MODEL), lambda n: (n, 1)), nxt(1),
                  _full((GMLP_HEADS, t, t)), _full((GMLP_HEADS, t, t)), _full((t, LANES)),
                  _full((1, GMLP_HEAD_DIM)), _full((1, GMLP_HEAD_DIM)),
                  _full((POOL_GROUPS, POOL_GROUP_DIM, POOL_GROUP_DIM)), _full((1, D_MODEL)), _full((1, D_MODEL))],
        out_specs=[pl.BlockSpec((t, EVEN_IN), lambda n: (n, 0))] + [_full(o.shape) for o in out_shape[1:]],
        out_shape=out_shape,
        compiler_params=_params("arbitrary"),
    )(proj, proj, proj, proj, proj, proj, proj, dmix, dmix, dmix, ws, ws_t, bs_t, ng, nb, pool_w, pool_b, pool_scale)


ROPE_HALF = MLA_ROPE // 2


def _rope(v, cos, sin_signed):
    return v * cos + pltpu.roll(v, 2 * ROPE_HALF, 1) * sin_signed


def _rope_bwd(d, cos, sin_signed):
    return d * cos + pltpu.roll(d * sin_signed, 2 * ROPE_HALF, 1)


def _slab_lanes(shape, which):
    lane = lax.broadcasted_iota(jnp.int32, shape, 1)
    return (lane // ROPE_HALF) % 2 == which


def _rms(v, g):
    r = lax.rsqrt(jnp.mean(v * v, axis=-1, keepdims=True) + LN_EPS)
    return v * r * g, r


def _rms_bwd(dy, v, r, g):
    u = dy * g
    return r * u - v * (r * r * r) * jnp.mean(u * v, axis=-1, keepdims=True)


def _mla_prep(proj, gq, gkv, cos, sin_signed, *, name):
    s = proj.shape[0]
    ts = _tile(s, 512)

    def body(p_ref, gq_ref, gkv_ref, c_ref, s_ref, q_ref, k_ref):
        qcn, _ = _rms(p_ref[:, :MLA_Q_RANK].astype(F32), gq_ref[...])
        kvn, _ = _rms(p_ref[:, MLA_Q_RANK:MLA_Q_RANK + MLA_KV_RANK].astype(F32), gkv_ref[...])
        kr = p_ref[:, MLA_Q_RANK + MLA_KV_RANK:].astype(F32)
        lane = lax.broadcasted_iota(jnp.int32, kr.shape, 1)
        by1, by2 = pltpu.roll(kr, ROPE_HALF, 1), pltpu.roll(kr, 2 * ROPE_HALF, 1)
        both = jnp.where(lane < ROPE_HALF, kr, jnp.where(lane < 3 * ROPE_HALF, by1, by2))
        kr = _rope(both, c_ref[...], s_ref[...])
        q_ref[...] = qcn.astype(BF16)
        k_ref[...] = jnp.concatenate([kvn, kr], axis=1).astype(BF16)

    return pl.pallas_call(
        body, name=name, grid=(s // ts,),
        in_specs=[_small_spec(ts), _vec_spec(MLA_Q_RANK), _vec_spec(MLA_KV_RANK), _row_spec(ts, LANES), _row_spec(ts, LANES)],
        out_specs=[_row_spec(ts, MLA_Q_RANK), _row_spec(ts, QK_PAD)],
        out_shape=[jax.ShapeDtypeStruct((s, MLA_Q_RANK), BF16), jax.ShapeDtypeStruct((s, QK_PAD), BF16)],
        compiler_params=_params("parallel"),
    )(proj, gq, gkv, cos, sin_signed)


def _mla_prep_bwd(proj, dqcn, dkv, gq, gkv, cos, sin_signed, *, name):
    s = proj.shape[0]
    ts = _tile(s, 512)

    def body(p_ref, dq_ref, dkv_ref, gq_ref, gkv_ref, c_ref, s_ref, ds_ref, ggq_ref, ggkv_ref):
        @pl.when(pl.program_id(0) == 0)
        def _():
            ggq_ref[...] = jnp.zeros_like(ggq_ref)
            ggkv_ref[...] = jnp.zeros_like(ggkv_ref)

        qc = p_ref[:, :MLA_Q_RANK].astype(F32)
        kvc = p_ref[:, MLA_Q_RANK:MLA_Q_RANK + MLA_KV_RANK].astype(F32)
        _, rq = _rms(qc, gq_ref[...])
        _, rkv = _rms(kvc, gkv_ref[...])
        dq = dq_ref[...]
        dkvn = dkv_ref[:, :MLA_KV_RANK]
        ggq_ref[...] += _colsum(dq * qc * rq)
        ggkv_ref[...] += _colsum(dkvn * kvc * rkv)
        dboth = _rope_bwd(dkv_ref[:, MLA_KV_RANK:], c_ref[...], s_ref[...])
        lane = lax.broadcasted_iota(jnp.int32, dboth.shape, 1)
        pair = dboth + pltpu.roll(dboth, 3 * ROPE_HALF, 1)
        dkr = jnp.where(lane < ROPE_HALF, pair, jnp.where(lane < 2 * ROPE_HALF, pltpu.roll(pair, 3 * ROPE_HALF, 1), 0.0))
        ds_ref[...] = jnp.concatenate(
            [_rms_bwd(dq, qc, rq, gq_ref[...]), _rms_bwd(dkvn, kvc, rkv, gkv_ref[...]), dkr], axis=1).astype(BF16)

    return pl.pallas_call(
        body, name=name, grid=(s // ts,),
        in_specs=[_small_spec(ts), _row_spec(ts, MLA_Q_RANK), _row_spec(ts, QK_PAD),
                  _vec_spec(MLA_Q_RANK), _vec_spec(MLA_KV_RANK), _row_spec(ts, LANES), _row_spec(ts, LANES)],
        out_specs=[_row_spec(ts, ODD_SMALL_PAD), _vec_spec(MLA_Q_RANK), _vec_spec(MLA_KV_RANK)],
        out_shape=[jax.ShapeDtypeStruct((s, ODD_SMALL_PAD), BF16), jax.ShapeDtypeStruct((1, MLA_Q_RANK), F32),
                   jax.ShapeDtypeStruct((1, MLA_KV_RANK), F32)],
        compiler_params=_params("arbitrary"),
    )(proj, dqcn, dkv, gq, gkv, cos, sin_signed)


Q_HEAD_GROUP = 8
LOG2_E = 1.4426950408889634
Q_PRESCALE = ATTN_SCALE * LOG2_E


def _q_build(q_nope, q_rope_pre, wuk_hdr, cos, sin_signed, *, name):
    s = q_nope.shape[0]
    ts = _tile(s, 512)
    hg = Q_HEAD_GROUP

    def body(qn_ref, qr_ref, w_ref, c_ref, s_ref, o_ref):
        for pair in range(hg // 2):
            r = _rope(qr_ref[:, pair * LANES:(pair + 1) * LANES], c_ref[...], s_ref[...])
            for j in range(2):
                h = 2 * pair + j
                ql = _dot(qn_ref[:, h * MLA_NOPE:(h + 1) * MLA_NOPE], w_ref[h], NN)
                mine = jnp.where(_slab_lanes(r.shape, j), r, 0.0)
                o_ref[h] = (jnp.concatenate([ql, mine], axis=1) * Q_PRESCALE).astype(BF16)

    return pl.pallas_call(
        body, name=name, grid=(s // ts, MLA_HEADS // hg),
        in_specs=[pl.BlockSpec((ts, hg * MLA_NOPE), lambda i, p: (i, p)), pl.BlockSpec((ts, hg * MLA_ROPE), lambda i, p: (i, p)),
                  pl.BlockSpec((hg, MLA_NOPE, MLA_KV_RANK), lambda i, p: (p, 0, 0)),
                  pl.BlockSpec((ts, LANES), lambda i, p: (i, 0)), pl.BlockSpec((ts, LANES), lambda i, p: (i, 0))],
        out_specs=pl.BlockSpec((hg, ts, QK_PAD), lambda i, p: (p, i, 0)),
        out_shape=jax.ShapeDtypeStruct((MLA_HEADS, s, QK_PAD), BF16),
        compiler_params=_params("parallel", "parallel"),
    )(q_nope, q_rope_pre, wuk_hdr, cos, sin_signed)


def _q_bwd(dq, q_nope, wuk_hrd, cos, sin_signed, *, name):
    s = q_nope.shape[0]
    ts = _tile(s, 512)
    hg = Q_HEAD_GROUP

    nope_w, all_w = hg * MLA_NOPE, hg * (MLA_NOPE + MLA_ROPE)

    def body(dq_ref, qn_ref, w_ref, c_ref, s_ref, dall_ref, gw_ref):
        @pl.when(pl.program_id(1) == 0)
        def _():
            gw_ref[...] = jnp.zeros_like(gw_ref)

        for h in range(hg):
            dql = dq_ref[h, :, :MLA_KV_RANK]
            dall_ref[:, h * MLA_NOPE:(h + 1) * MLA_NOPE] = _dot(dql, w_ref[h], NN).astype(BF16)
            gw_ref[h] += _dot(dql, qn_ref[:, h * MLA_NOPE:(h + 1) * MLA_NOPE], TN)
        for pair in range(hg // 2):
            hi0 = dq_ref[2 * pair, :, MLA_KV_RANK:].astype(F32)
            hi1 = dq_ref[2 * pair + 1, :, MLA_KV_RANK:].astype(F32)
            d = jnp.where(_slab_lanes(hi0.shape, 0), hi0, hi1)
            dall_ref[:, nope_w + pair * LANES:nope_w + (pair + 1) * LANES] = _rope_bwd(d, c_ref[...], s_ref[...]).astype(BF16)

    return pl.pallas_call(
        body, name=name, grid=(MLA_HEADS // hg, s // ts),
        in_specs=[pl.BlockSpec((hg, ts, QK_PAD), lambda p, i: (p, i, 0)), pl.BlockSpec((ts, nope_w), lambda p, i: (i, p)),
                  pl.BlockSpec((hg, MLA_KV_RANK, MLA_NOPE), lambda p, i: (p, 0, 0)),
                  pl.BlockSpec((ts, LANES), lambda p, i: (i, 0)), pl.BlockSpec((ts, LANES), lambda p, i: (i, 0))],
        out_specs=[pl.BlockSpec((ts, all_w), lambda p, i: (i, p)),
                   pl.BlockSpec((hg, MLA_KV_RANK, MLA_NOPE), lambda p, i: (p, 0, 0))],
        out_shape=[jax.ShapeDtypeStruct((s, MLA_HEADS * (MLA_NOPE + MLA_ROPE)), BF16),
                   jax.ShapeDtypeStruct((MLA_HEADS, MLA_KV_RANK, MLA_NOPE), F32)],
        compiler_params=_params("parallel", "arbitrary"),
    )(dq, q_nope, wuk_hrd, cos, sin_signed)


ATTN_BQ = 128
ATTN_BK = 512
ATTN_BK_FWD = 1024


def _diag_mask(rows, bq, bk, q0, k0):
    qc = (q0 + lax.broadcasted_iota(jnp.int32, (rows, bk), 0) % bq) // CHUNK
    kc = (k0 + lax.broadcasted_iota(jnp.int32, (rows, bk), 1)) // CHUNK
    return kc <= qc


def _attn_fwd(q, k, *, name):
    nh, s, dk = q.shape
    bq, bk = _tile(s, ATTN_BQ), _tile(s, ATTN_BK_FWD)
    rows = nh * bq

    def body(q_ref, k_ref, o_ref, lse_ref):
        i = pl.program_id(0)
        qb = q_ref[...].reshape(rows, dk)
        n_before = (i * bq) // bk

        def step(j, width, carry, masked):
            m, l, acc = carry
            k0 = pl.multiple_of(j * bk, bk)
            kb = k_ref[pl.ds(k0, width), :]
            sc = _dot(qb, kb, NT)
            if masked:
                sc = jnp.where(_diag_mask(rows, bq, width, i * bq, k0), sc, NEG)
            m_new = jnp.maximum(m, jnp.max(sc, axis=1, keepdims=True))
            p = jnp.exp2(sc - m_new)
            a = jnp.exp2(m - m_new)
            l = a * l + jnp.sum(p, axis=1, keepdims=True)
            acc = a * acc + _dot(p.astype(BF16), kb[:, :MLA_KV_RANK], NN)
            return m_new, l, acc

        init = (jnp.full((rows, 1), NEG, F32), jnp.zeros((rows, 1), F32), jnp.zeros((rows, MLA_KV_RANK), F32))
        carry = lax.fori_loop(0, n_before, lambda j, c: step(j, bk, c, False), init)
        for part in range(bk // bq):
            @pl.when(i % (bk // bq) == part)
            def _(part=part):
                m, l, acc = step(n_before, (part + 1) * bq, carry, True)
                o_ref[...] = (acc / l).astype(BF16).reshape(nh, bq, MLA_KV_RANK)
                lse_ref[...] = jnp.broadcast_to(m + jnp.log2(l), (rows, LANES)).reshape(nh, bq, LANES)

    return pl.pallas_call(
        body, name=name, grid=(s // bq,),
        in_specs=[pl.BlockSpec((nh, bq, dk), lambda i: (0, i, 0)), pl.BlockSpec((s, dk), lambda i: (0, 0))],
        out_specs=[pl.BlockSpec((nh, bq, MLA_KV_RANK), lambda i: (0, i, 0)), pl.BlockSpec((nh, bq, LANES), lambda i: (0, i, 0))],
        out_shape=[jax.ShapeDtypeStruct((nh, s, MLA_KV_RANK), BF16), jax.ShapeDtypeStruct((nh, s, LANES), F32)],
        compiler_params=_params("parallel"),
    )(q, k)


def _attn_bwd(q, k, do, o, lse, *, name):
    nh, s, dk = q.shape
    bq, bk = _tile(s, ATTN_BQ), _tile(s, ATTN_BK)
    rows = nh * bq

    def body(q_ref, k_ref, do_ref, o_ref, lse_ref, dq_ref, dkv_ref):
        i = pl.program_id(0)
        n_before = (i * bq) // bk

        @pl.when(i == 0)
        def _():
            dkv_ref[...] = jnp.zeros_like(dkv_ref)

        qb = q_ref[...].reshape(rows, dk)
        dob = do_ref[...].reshape(rows, MLA_KV_RANK)
        lse_b = lse_ref[...].reshape(rows, LANES)[:, :1]
        delta = jnp.sum(dob.astype(F32) * o_ref[...].reshape(rows, MLA_KV_RANK).astype(F32), axis=1, keepdims=True)

        def step(j, width, dq, masked):
            j0 = pl.multiple_of(j * bk, bk)
            kb = k_ref[pl.ds(j0, width), :]
            sc = _dot(qb, kb, NT)
            if masked:
                sc = jnp.where(_diag_mask(rows, bq, width, i * bq, j0), sc, NEG)
            p = jnp.exp2(sc - lse_b)
            dp = _dot(dob, kb[:, :MLA_KV_RANK], NT)
            ds_bf = (p * (dp - delta)).astype(BF16)
            dkv_ref[pl.ds(j0, width), :] += _dot(ds_bf, qb, TN) * (1.0 / LOG2_E)
            dkv_ref[pl.ds(j0, width), :MLA_KV_RANK] += _dot(p.astype(BF16), dob, TN)
            return dq + _dot(ds_bf, kb, NN)

        dq_before = lax.fori_loop(0, n_before, lambda j, c: step(j, bk, c, False), jnp.zeros((rows, dk), F32))
        for part in range(bk // bq):
            @pl.when(i % (bk // bq) == part)
            def _(part=part):
                dq = step(n_before, (part + 1) * bq, dq_before, True) * ATTN_SCALE
                dq_ref[...] = dq.astype(BF16).reshape(nh, bq, dk)

    blk = lambda w: pl.BlockSpec((nh, bq, w), lambda i: (0, i, 0))
    return pl.pallas_call(
        body, name=name, grid=(s // bq,),
        in_specs=[blk(dk), pl.BlockSpec((s, dk), lambda i: (0, 0)), blk(MLA_KV_RANK), blk(MLA_KV_RANK), blk(LANES)],
        out_specs=[blk(dk), pl.BlockSpec((s, dk), lambda i: (0, 0))],
        out_shape=[jax.ShapeDtypeStruct((nh, s, dk), BF16), jax.ShapeDtypeStruct((s, dk), F32)],
        compiler_params=_params("arbitrary"),
    )(q, k, do, o, lse)


HEAD_GROUP = 4
SMALL_BLOCK = MLA_WIDTH // ODD_SMALL_PAD


def _small_spec(ts):
    return pl.BlockSpec((ts, ODD_SMALL_PAD), lambda i: (i, SMALL_BLOCK))


def _o_build(o_lat, wuv_hrv, proj, *, name):
    s = proj.shape[0]
    ts = _tile(s, 1024)
    w = HEAD_GROUP * MLA_V

    def body(ol_ref, w_ref, z_ref, og_ref):
        for j in range(HEAD_GROUP):
            cs = slice(j * MLA_V, (j + 1) * MLA_V)
            z = z_ref[:, cs].astype(F32)
            og_ref[:, cs] = (_dot(ol_ref[j], w_ref[j], NN) * (z * _sigmoid(z))).astype(BF16)

    return pl.pallas_call(
        body, name=name, grid=(s // ts, MLA_HEADS // HEAD_GROUP),
        in_specs=[pl.BlockSpec((HEAD_GROUP, ts, MLA_KV_RANK), lambda i, g: (g, i, 0)),
                  pl.BlockSpec((HEAD_GROUP, MLA_KV_RANK, MLA_V), lambda i, g: (g, 0, 0)),
                  pl.BlockSpec((ts, w), lambda i, g: (i, g))],
        out_specs=pl.BlockSpec((ts, w), lambda i, g: (i, g)),
        out_shape=jax.ShapeDtypeStruct((s, MLA_WIDTH), BF16),
        compiler_params=_params("parallel", "parallel"),
    )(o_lat, wuv_hrv, proj)


def _o_bwd(dg, proj, o_lat, wuv_hrv, wuv_hvr, *, name):
    s = proj.shape[0]
    ts = _tile(s, 1024)
    w = HEAD_GROUP * MLA_V

    def body(dg_ref, z_ref, ol_ref, w_ref, wt_ref, dol_ref, dz_ref, gw_ref):
        @pl.when(pl.program_id(1) == 0)
        def _():
            gw_ref[...] = jnp.zeros_like(gw_ref)

        for j in range(HEAD_GROUP):
            cs = slice(j * MLA_V, (j + 1) * MLA_V)
            z, dgj, ol = z_ref[:, cs].astype(F32), dg_ref[:, cs].astype(F32), ol_ref[j]
            sg = _sigmoid(z)
            o = _dot(ol, w_ref[j], NN)
            dz_ref[:, cs] = (dgj * o * (sg * (1.0 + z * (1.0 - sg)))).astype(BF16)
            do_bf = (dgj * (z * sg)).astype(BF16)
            dol_ref[j] = _dot(do_bf, wt_ref[j], NN).astype(BF16)
            gw_ref[j] += _dot(ol, do_bf, TN)

    hs = lambda a, b: pl.BlockSpec((HEAD_GROUP, a, b), lambda g, i: (g, 0, 0))
    return pl.pallas_call(
        body, name=name, grid=(MLA_HEADS // HEAD_GROUP, s // ts),
        in_specs=[pl.BlockSpec((ts, w), lambda g, i: (i, g)), pl.BlockSpec((ts, w), lambda g, i: (i, g)),
                  pl.BlockSpec((HEAD_GROUP, ts, MLA_KV_RANK), lambda g, i: (g, i, 0)),
                  hs(MLA_KV_RANK, MLA_V), hs(MLA_V, MLA_KV_RANK)],
        out_specs=[pl.BlockSpec((HEAD_GROUP, ts, MLA_KV_RANK), lambda g, i: (g, i, 0)),
                   pl.BlockSpec((ts, w), lambda g, i: (i, g)), hs(MLA_KV_RANK, MLA_V)],
        out_shape=[jax.ShapeDtypeStruct((MLA_HEADS, s, MLA_KV_RANK), BF16), jax.ShapeDtypeStruct((s, MLA_WIDTH), BF16),
                   jax.ShapeDtypeStruct((MLA_HEADS, MLA_KV_RANK, MLA_V), F32)],
        compiler_params=_params("parallel", "arbitrary"),
    )(dg, proj, o_lat, wuv_hrv, wuv_hvr)


def _ada_mod(c_all, ada_w, ada_b_sh, *, name):
    nl, _, cols = ada_w.shape

    def body(c_ref, w_ref, b_ref, o_ref):
        c = c_ref[...]
        cond = (c * _sigmoid(c)).astype(BF16)
        for l in range(nl):
            o_ref[l] = _dot(cond, w_ref[l].astype(BF16), NN) + b_ref[l]

    return pl.pallas_call(
        body, name=name, out_shape=jax.ShapeDtypeStruct((nl, c_all.shape[0], cols), F32),
        compiler_params=_params(),
    )(c_all, ada_w, ada_b_sh)


def _ada_grad(c_all_t, dmod_sh, *, name):
    nl, _, cols = dmod_sh.shape
    d = c_all_t.shape[0]

    def body(c_ref, dm_ref, gw_ref):
        c = c_ref[...]
        cond_t = c * _sigmoid(c)
        for l in range(nl):
            gw_ref[l] = lax.dot_general(cond_t, dm_ref[l], (NN, ((), ())), precision=lax.Precision.HIGHEST,
                                        preferred_element_type=F32)

    return pl.pallas_call(
        body, name=name, out_shape=jax.ShapeDtypeStruct((nl, d, cols), F32), compiler_params=_params(),
    )(c_all_t, dmod_sh)


def _sum_devices(parts, *, name):
    def body(p_ref, o_ref):
        acc = p_ref[0]
        for k in range(1, parts.shape[0]):
            acc = acc + p_ref[k]
        o_ref[...] = acc

    return pl.pallas_call(body, name=name, out_shape=jax.ShapeDtypeStruct(parts.shape[1:], F32), compiler_params=_params())(parts)


def _adamw_math(w, g, m, v):
    c1 = 1.0 - ADAM_B1 ** ADAM_STEP
    c2 = 1.0 - ADAM_B2 ** ADAM_STEP
    nm = ADAM_B1 * m + (1.0 - ADAM_B1) * g
    nv = ADAM_B2 * v + (1.0 - ADAM_B2) * (g * g)
    return -ADAM_LR * ((nm / c1) / (jnp.sqrt(nv / c2) + ADAM_EPS) + ADAM_WD * w), nm, nv


ADAMW_BLOCK_BYTES = 1 << 20


def _adamw(w, g, m, v, *, name, after=None):
    shape = w.shape
    a, b = shape[-2], shape[-1]
    lead = 1
    for dim in shape[:-2]:
        lead *= dim
    row_bytes = 4 * b
    if a * row_bytes <= ADAMW_BLOCK_BYTES:
        ta = a
        tl = max(1, min(lead, ADAMW_BLOCK_BYTES // (a * row_bytes)))
        while lead % tl:
            tl -= 1
    else:
        tl = 1
        ta = _tile(a, 256)
    to3 = lambda t: t.reshape(lead, a, b)

    def body(w_ref, g_ref, m_ref, v_ref, *rest):
        d_ref, nm_ref, nv_ref = rest[-3:]
        d_ref[...], nm_ref[...], nv_ref[...] = _adamw_math(w_ref[...], g_ref[...], m_ref[...], v_ref[...])

    spec = pl.BlockSpec((tl, ta, b), lambda i, j: (i, j, 0))
    out = jax.ShapeDtypeStruct((lead, a, b), F32)
    order = [] if after is None else [after]
    res = pl.pallas_call(
        body, name=name, grid=(lead // tl, a // ta), in_specs=[spec] * 4 + [pl.BlockSpec(memory_space=pl.ANY)] * len(order),
        out_specs=[spec] * 3, out_shape=[out] * 3, compiler_params=_params("parallel", "parallel"),
    )(to3(w), to3(g), to3(m), to3(v), *order)
    return [r.reshape(shape) for r in res]


def _adamw_small(ws, gs, ms, vs, *, name):
    n = len(ws)

    def body(*refs):
        for k in range(n):
            w_ref, g_ref, m_ref, v_ref = (refs[j * n + k] for j in range(4))
            d_ref, nm_ref, nv_ref = (refs[(4 + j) * n + k] for j in range(3))
            d_ref[...], nm_ref[...], nv_ref[...] = _adamw_math(w_ref[...], g_ref[...], m_ref[...], v_ref[...])

    outs = [jax.ShapeDtypeStruct(w.shape, F32) for w in ws]
    res = pl.pallas_call(body, name=name, out_shape=outs * 3, compiler_params=_params())(*ws, *gs, *ms, *vs)
    return res[:n], res[n:2 * n], res[2 * n:]


def _flip(v, bit):
    return 1 - v if bit else v


CHIP_DELTAS = ((1, 0), (0, 1), (1, 1))
SUM_ROWS = 32


def _all_gather_chips(shard, *, name):
    def body(x_ref, o_ref, send_sems, recv_sems, local_sem):
        x, y, c = lax.axis_index("x"), lax.axis_index("y"), lax.axis_index("c")
        mine = pltpu.make_async_copy(x_ref, o_ref.at[2 * x + y], local_sem)
        mine.start()

        def copy(k):
            tx, ty = _flip(x, CHIP_DELTAS[k][0]), _flip(y, CHIP_DELTAS[k][1])
            send = pltpu.make_async_remote_copy(src_ref=x_ref, dst_ref=o_ref.at[2 * x + y], send_sem=send_sems.at[k],
                                                recv_sem=recv_sems.at[k], device_id=(tx, ty, c), device_id_type=MESH)
            recv = pltpu.make_async_remote_copy(src_ref=x_ref, dst_ref=o_ref.at[2 * tx + ty], send_sem=send_sems.at[k],
                                                recv_sem=recv_sems.at[k], device_id=(tx, ty, c), device_id_type=MESH)
            return send, recv

        pairs = [copy(k) for k in range(3)]
        for send, _ in pairs:
            send.start()
        for _, recv in pairs:
            recv.wait_recv()
        for send, _ in pairs:
            send.wait_send()
        mine.wait()

    return pl.pallas_call(
        body, name=name, out_shape=jax.ShapeDtypeStruct((N_CHIPS,) + shard.shape, shard.dtype),
        in_specs=[HBM], out_specs=HBM,
        scratch_shapes=[pltpu.SemaphoreType.DMA((3,)), pltpu.SemaphoreType.DMA((3,)), pltpu.SemaphoreType.DMA(())],
    )(shard)


def _gather_weights(shards, *, name):
    n = len(shards)

    def body(*refs):
        w_refs, o_refs = refs[:n], refs[n:2 * n]
        ici_send, ici_recv, d2d_send, d2d_recv, local_sems = refs[2 * n:]
        x, y, c = lax.axis_index("x"), lax.axis_index("y"), lax.axis_index("c")
        me = 2 * x + y
        peers = [(_flip(x, dx), _flip(y, dy)) for dx, dy in CHIP_DELTAS]
        locals_ = [pltpu.make_async_copy(w_refs[k], o_refs[k].at[me], local_sems.at[k]) for k in range(n)]
        for cp in locals_:
            cp.start()

        def rows(k, which):
            half = shards[k].shape[0] // 2
            return pl.ds(pl.multiple_of(which * half, half), half)

        def over_chips(k, d, slot):
            tx, ty = peers[d]
            return pltpu.make_async_remote_copy(
                src_ref=w_refs[k].at[rows(k, c)], dst_ref=o_refs[k].at[slot, rows(k, c)], send_sem=ici_send.at[k, d],
                recv_sem=ici_recv.at[k, d], device_id=(tx, ty, c), device_id_type=MESH)

        def to_sibling(k, d, which):
            tx, ty = peers[d]
            at = o_refs[k].at[2 * tx + ty, rows(k, which)]
            return pltpu.make_async_remote_copy(src_ref=at, dst_ref=at, send_sem=d2d_send.at[k, d], recv_sem=d2d_recv.at[k, d],
                                                device_id=(x, y, 1 - c), device_id_type=MESH)

        sends = [over_chips(k, d, me) for k in range(n) for d in range(3)]
        for cp in sends:
            cp.start()
        passed = []
        for k in range(n):
            for d in range(3):
                over_chips(k, d, 2 * peers[d][0] + peers[d][1]).wait_recv()
                passed.append(to_sibling(k, d, c))
                passed[-1].start()
        for k in range(n):
            for d in range(3):
                to_sibling(k, d, 1 - c).wait_recv()
        for cp in sends + passed:
            cp.wait_send()
        for cp in locals_:
            cp.wait()

    return pl.pallas_call(
        body, name=name, out_shape=[jax.ShapeDtypeStruct((N_CHIPS,) + w.shape, w.dtype) for w in shards],
        in_specs=[HBM] * n, out_specs=[HBM] * n,
        scratch_shapes=[pltpu.SemaphoreType.DMA((n, 3))] * 4 + [pltpu.SemaphoreType.DMA((n,))],
    )(*shards)


def _add_into(dst_ref, src_ref):
    ns, r, _ = dst_ref.shape
    step = SUM_ROWS if r % SUM_ROWS == 0 else r
    for s in range(ns):
        def tile(t, carry):
            at = pl.ds(pl.multiple_of(t * step, step), step)
            dst_ref[s, at, :] = (dst_ref[s, at, :].astype(F32) + src_ref[s, at, :].astype(F32)).astype(dst_ref.dtype)
            return carry
        lax.fori_loop(0, r // step, tile, 0)


def _reduce_sibling(grads, *, name):
    n = len(grads)

    def body(*refs):
        g_refs, o_refs = refs[:n], refs[n:2 * n]
        mine, got = refs[2 * n:3 * n], refs[3 * n:4 * n]
        send_sems, recv_sems, load_sems, store_sems = refs[4 * n:]
        x, y, c = lax.axis_index("x"), lax.axis_index("y"), lax.axis_index("c")
        loads = [pltpu.make_async_copy(g_refs[k].at[:, c], mine[k], load_sems.at[k]) for k in range(n)]
        swaps = [pltpu.make_async_remote_copy(src_ref=g_refs[k].at[:, 1 - c], dst_ref=got[k], send_sem=send_sems.at[k],
                                              recv_sem=recv_sems.at[k], device_id=(x, y, 1 - c), device_id_type=MESH)
                 for k in range(n)]
        for cp in loads + swaps:
            cp.start()
        stores = []
        for k in range(n):
            loads[k].wait()
            swaps[k].wait_recv()
            _add_into(mine[k], got[k])
            stores.append(pltpu.make_async_copy(mine[k], o_refs[k], store_sems.at[k]))
            stores[-1].start()
        for k in range(n):
            swaps[k].wait_send()
            stores[k].wait()

    half = [jax.ShapeDtypeStruct((g.shape[0],) + g.shape[2:], g.dtype) for g in grads]
    return pl.pallas_call(
        body, name=name, out_shape=half, in_specs=[HBM] * n, out_specs=[HBM] * n,
        scratch_shapes=[pltpu.VMEM(h.shape, h.dtype) for h in half] * 2 + [pltpu.SemaphoreType.DMA((n,))] * 4,
        compiler_params=_params(),
    )(*grads)


def _reduce_chips(parts, landed, *, name):
    n = len(parts)

    def body(*refs):
        p_refs, l_refs, o_refs = refs[:n], refs[n:2 * n], refs[2 * n:3 * n]
        got, total = refs[3 * n:4 * n], refs[4 * n:5 * n]
        load_sems, share_send, share_recv, store_sems = refs[5 * n:]
        x, y, c = lax.axis_index("x"), lax.axis_index("y"), lax.axis_index("c")
        me = 2 * x + y
        slots = [me] + [2 * _flip(x, dx) + _flip(y, dy) for dx, dy in CHIP_DELTAS]
        loads = [[pltpu.make_async_copy((p_refs if j == 0 else l_refs)[k].at[slot], got[k].at[slot], load_sems.at[k, j])
                  for j, slot in enumerate(slots)] for k in range(n)]
        for per_array in loads:
            for cp in per_array:
                cp.start()
        shares, stores = [], []
        for k in range(n):
            for cp in loads[k]:
                cp.wait()
            r = total[k].shape[0]
            step = SUM_ROWS if r % SUM_ROWS == 0 else r

            def tile(t, carry, k=k, step=step):
                at = pl.ds(pl.multiple_of(t * step, step), step)
                acc = got[k][0, at, :].astype(F32)
                for s in range(1, N_CHIPS):
                    acc = acc + got[k][s, at, :].astype(F32)
                total[k][at, :] = acc
                return carry

            lax.fori_loop(0, r // step, tile, 0)
            stores.append(pltpu.make_async_copy(total[k], o_refs[k].at[c], store_sems.at[k]))
            shares.append(pltpu.make_async_remote_copy(
                src_ref=total[k], dst_ref=o_refs[k].at[c], send_sem=share_send.at[k], recv_sem=share_recv.at[k],
                device_id=(x, y, 1 - c), device_id_type=MESH))
            stores[-1].start()
            shares[-1].start()
        for k in range(n):
            pltpu.make_async_remote_copy(
                src_ref=total[k], dst_ref=o_refs[k].at[1 - c], send_sem=share_send.at[k], recv_sem=share_recv.at[k],
                device_id=(x, y, 1 - c), device_id_type=MESH).wait_recv()
        for cp in shares:
            cp.wait_send()
        for cp in stores:
            cp.wait()

    return pl.pallas_call(
        body, name=name, out_shape=[jax.ShapeDtypeStruct((2,) + p.shape[1:], F32) for p in parts],
        in_specs=[HBM] * (2 * n), out_specs=[HBM] * n,
        scratch_shapes=[pltpu.VMEM(p.shape, p.dtype) for p in parts] + [pltpu.VMEM(p.shape[1:], F32) for p in parts]
        + [pltpu.SemaphoreType.DMA((n, N_CHIPS))] + [pltpu.SemaphoreType.DMA((n,))] * 3,
        compiler_params=_params(),
    )(*parts, *landed)


SEM = pl.BlockSpec(memory_space=pltpu.SEMAPHORE)
IN_FLIGHT = pltpu.SideEffectType.DATAFLOW_SIDE_EFFECTING


def _chip_copies(s_refs, l_refs, sems, scatter, theirs):
    x, y, c = lax.axis_index("x"), lax.axis_index("y"), lax.axis_index("c")
    me = 2 * x + y
    copies = []
    for k in range(len(s_refs)):
        for d, (dx, dy) in enumerate(CHIP_DELTAS):
            tx, ty = _flip(x, dx), _flip(y, dy)
            peer = 2 * tx + ty
            send_sem, recv_sem = sems[2 * (3 * k + d)], sems[2 * (3 * k + d) + 1]
            copies.append(pltpu.make_async_remote_copy(
                src_ref=s_refs[k].at[peer] if scatter else s_refs[k], dst_ref=l_refs[k].at[peer if theirs else me],
                send_sem=send_sem, recv_sem=recv_sem, device_id=(tx, ty, c), device_id_type=MESH))
    return copies


def _chips_start(srcs, lands, after, *, scatter, name):
    n = len(srcs)
    n_sem = 2 * 3 * n

    def body(*refs):
        s_refs, l_refs = refs[:n], refs[n:2 * n]
        sems = refs[2 * n + 1:2 * n + 1 + n_sem]
        token = refs[-1]
        for cp in _chip_copies(s_refs, l_refs, sems, scatter, False):
            cp.start()
        token[...] = jnp.zeros_like(token)

    hbm = lambda a: pltpu.HBM(a.shape, a.dtype)
    res = pl.pallas_call(
        body, name=name,
        out_shape=(*[pltpu.SemaphoreType.DMA(())] * n_sem, *[hbm(a) for a in srcs], *[hbm(a) for a in lands],
                   jax.ShapeDtypeStruct((8, LANES), F32)),
        in_specs=[HBM] * (2 * n) + [pl.BlockSpec(memory_space=pl.ANY)],
        out_specs=(*[SEM] * n_sem, *[HBM] * (2 * n), VMEM),
        input_output_aliases={k: n_sem + k for k in range(2 * n)},
        compiler_params=pltpu.CompilerParams(has_side_effects=IN_FLIGHT),
    )(*[pltpu.with_memory_space_constraint(a, pltpu.HBM) for a in list(srcs) + list(lands)], after)
    return res[:n_sem], res[n_sem:n_sem + n], res[n_sem + n:n_sem + 2 * n], res[-1]


def _chips_wait(sems, srcs, lands, after, *, scatter, name):
    n = len(srcs)
    n_sem = len(sems)

    def body(*refs):
        s_refs, l_refs = refs[:n], refs[n:2 * n]
        sem_refs = refs[2 * n:2 * n + n_sem]
        for cp in _chip_copies(s_refs, l_refs, sem_refs, scatter, False):
            cp.wait_send()
        for cp in _chip_copies(s_refs, l_refs, sem_refs, scatter, True):
            cp.wait_recv()

    hbm = lambda a: pltpu.HBM(a.shape, a.dtype)
    res = pl.pallas_call(
        body, name=name, out_shape=tuple(hbm(a) for a in list(srcs) + list(lands)),
        in_specs=[HBM] * (2 * n) + [SEM] * n_sem + [pl.BlockSpec(memory_space=pl.ANY)], out_specs=tuple([HBM] * (2 * n)),
        input_output_aliases={k: k for k in range(2 * n)},
        compiler_params=pltpu.CompilerParams(has_side_effects=IN_FLIGHT),
    )(*srcs, *lands, *sems, after)
    return res[:n], res[n:]


def _all_gather_devices(rows, *, name, after=None):
    deltas = [(dx, dy, dc) for dx in (0, 1) for dy in (0, 1) for dc in (0, 1)][1:]
    order = [] if after is None else [after]

    def body(x_ref, *rest):
        o_ref, send_sems, recv_sems = rest[-3:]
        x, y, c = lax.axis_index("x"), lax.axis_index("y"), lax.axis_index("c")
        me = 4 * x + 2 * y + c
        o_ref[me] = x_ref[...]
        sends, recvs = [], []
        for k, (dx, dy, dc) in enumerate(deltas):
            tx, ty, tc = _flip(x, dx), _flip(y, dy), _flip(c, dc)
            sends.append(pltpu.make_async_remote_copy(src_ref=x_ref, dst_ref=o_ref.at[me], send_sem=send_sems.at[k],
                                                      recv_sem=recv_sems.at[k], device_id=(tx, ty, tc), device_id_type=MESH))
            recvs.append(pltpu.make_async_remote_copy(src_ref=x_ref, dst_ref=o_ref.at[4 * tx + 2 * ty + tc],
                                                      send_sem=send_sems.at[k], recv_sem=recv_sems.at[k],
                                                      device_id=(tx, ty, tc), device_id_type=MESH))
        for cp in sends:
            cp.start()
        for cp in recvs:
            cp.wait_recv()
        for cp in sends:
            cp.wait_send()

    return pl.pallas_call(
        body, name=name, out_shape=jax.ShapeDtypeStruct((N_DEV,) + rows.shape, rows.dtype),
        in_specs=[VMEM] + [pl.BlockSpec(memory_space=pl.ANY)] * len(order), out_specs=VMEM,
        scratch_shapes=[pltpu.SemaphoreType.DMA((N_DEV - 1,)), pltpu.SemaphoreType.DMA((N_DEV - 1,))],
    )(rows, *order)


WEIGHTS = ("ada_w", "ada_b", "ln_g", "ln_b", "e_w_in", "gmlp_norm_g", "gmlp_norm_b", "gmlp_ws", "gmlp_bs", "pool_w",
           "pool_b", "pool_scale", "e_w_out", "o_w_in", "mla_q_norm_g", "mla_kv_norm_g", "mla_w_uq", "mla_w_uk",
           "mla_w_uv", "o_w_out")
SMALL = ("ln_g", "ln_b", "gmlp_norm_g", "gmlp_norm_b", "gmlp_bs", "pool_b", "pool_scale", "mla_kv_norm_g", "mla_q_norm_g")


def _pad_cols(v, n):
    return jnp.concatenate([v, jnp.zeros((v.shape[0], n - v.shape[1]), v.dtype)], axis=1) if n > v.shape[1] else v


def _halves(g):
    return g.reshape(g.shape[0], 2, g.shape[1] // 2, g.shape[2])


def kernel(x, c, positions, ada_w, ada_b, ln_g, ln_b, e_w_in, gmlp_norm_g, gmlp_norm_b, gmlp_ws, gmlp_bs, pool_w, pool_b, pool_scale, e_w_out, o_w_in, mla_q_norm_g, mla_kv_norm_g, mla_w_uq, mla_w_uk, mla_w_uv, o_w_out, loss_target, m_ada_w, m_ada_b, m_ln_g, m_ln_b, m_e_w_in, m_gmlp_norm_g, m_gmlp_norm_b, m_gmlp_ws, m_gmlp_bs, m_pool_w, m_pool_b, m_pool_scale, m_e_w_out, m_o_w_in, m_mla_q_norm_g, m_mla_kv_norm_g, m_mla_w_uq, m_mla_w_uk, m_mla_w_uv, m_o_w_out, v_ada_w, v_ada_b, v_ln_g, v_ln_b, v_e_w_in, v_gmlp_norm_g, v_gmlp_norm_b, v_gmlp_ws, v_gmlp_bs, v_pool_w, v_pool_b, v_pool_scale, v_e_w_out, v_o_w_in, v_mla_q_norm_g, v_mla_kv_norm_g, v_mla_w_uq, v_mla_w_uk, v_mla_w_uv, v_o_w_out):
    args = dict(locals())
    weights = {n: args[n] for n in WEIGHTS}
    mom = {n: args["m_" + n] for n in WEIGHTS}
    var = {n: args["v_" + n] for n in WEIGHTS}
    ax, ay, ac = lax.axis_index("x"), lax.axis_index("y"), lax.axis_index("c")
    chip = 2 * ax + ay
    dev = 2 * chip + ac
    d = D_MODEL
    x2 = x[0]
    target = loss_target[0]
    q_rank_sh = mla_q_norm_g.shape[1]

    empty_zone = lambda w: lax.dynamic_update_slice(lax.empty((N_CHIPS,) + w.shape, w.dtype), w[None], (chip, 0, 0))
    shards0 = [w.astype(BF16) for w in (pool_w[0].reshape(-1, POOL_GROUP_DIM), e_w_out[0])]
    shards1 = [w.astype(BF16) for w in (o_w_in[0], mla_w_uq[0].reshape(q_rank_sh, -1), o_w_out[0])]
    w_in0, = _gather_weights([e_w_in[0].astype(BF16)], name="gather_weights")
    wuk_hrd = jnp.transpose(mla_w_uk[0], (1, 0, 2)).astype(BF16)
    wuk_hdr = jnp.transpose(mla_w_uk[0], (1, 2, 0)).astype(BF16)
    wuv_hrv = jnp.transpose(mla_w_uv[0], (1, 0, 2)).astype(BF16)
    wuv_hvr = jnp.transpose(mla_w_uv[0], (1, 2, 0)).astype(BF16)
    ws = gmlp_ws[0]
    ws_t = jnp.transpose(ws, (0, 2, 1))
    bs_t = _pad_cols(gmlp_bs[0].T, LANES)

    inv = 1.0 / (ROPE_THETA ** (jnp.arange(0, MLA_ROPE, 2, dtype=F32) / MLA_ROPE))
    ang = positions[0].astype(F32)[:, None] * inv
    cos_t = jnp.tile(jnp.cos(ang), (1, 4))
    sin_t = jnp.concatenate([-jnp.sin(ang), -jnp.sin(ang), jnp.sin(ang), jnp.sin(ang)], axis=1)

    c_all = _all_gather_devices(c.reshape(8, LANES), after=w_in0, name="gather_c").reshape(N_DEV, d)
    cols = ada_w.shape[2]
    ada_b_mine = lax.dynamic_slice_in_dim(ada_b, chip * cols, cols, axis=1)[:, None, :]
    mod_sh = _ada_mod(c_all, ada_w, ada_b_mine, name="ada_mod")
    q_norm_rows = jnp.zeros((8, cols), F32).at[0, :q_rank_sh].set(mla_q_norm_g[0])
    mod_all = _all_gather_chips(jnp.concatenate([mod_sh.reshape(2 * N_DEV, cols), q_norm_rows]), name="gather_mod")
    q_norm_g = mod_all[:, 2 * N_DEV, :q_rank_sh].reshape(1, -1)
    mod_all = jnp.transpose(mod_all[:, :2 * N_DEV].reshape(N_CHIPS, 2, N_DEV, cols), (1, 2, 0, 3)).reshape(2, N_DEV, 3 * d)
    mod = lax.dynamic_index_in_dim(mod_all, dev, axis=1, keepdims=False)
    shift = [mod[l:l + 1, :d] for l in range(2)]
    scale = [mod[l:l + 1, d:2 * d] for l in range(2)]
    gate = [mod[l:l + 1, 2 * d:] for l in range(2)]
    flight0 = _chips_start(shards0, [empty_zone(w) for w in shards0], mod, scatter=False, name="gather0_start")
    flight1 = _chips_start(shards1, [empty_zone(w) for w in shards1], flight0[3], scatter=False, name="gather1_start")

    scale[0] = scale[0] + flight1[3][:1, :1]
    h0 = _modulate(x2, scale[0], shift[0], name="modulate0")
    proj0 = _matmul(h0, w_in0, b_stacked=True, tm=1024, tn=1280, out_dtype=BF16, name="proj0")
    pool_w_g, w_out0 = _chips_wait(*flight0[:3], proj0, scatter=False, name="gather0_wait")[1]
    pool_w_bf = jnp.transpose(pool_w_g.reshape(N_CHIPS, POOL_GROUPS, -1, POOL_GROUP_DIM), (1, 0, 2, 3)).reshape(
        POOL_GROUPS, POOL_GROUP_DIM, POOL_GROUP_DIM)
    w_out0 = w_out0.reshape(-1, d)
    mix0 = _even_fwd(proj0, ws, bs_t, gmlp_norm_g, gmlp_norm_b, pool_w_bf, pool_b, pool_scale, name="even_fwd")
    y0, x1, h1 = _out_resid_ln(mix0, w_out0, x2, gate[0], ln_g[0:1], ln_b[0:1], scale[1], shift[1], name="out0_ln")

    w_in1_g, w_uq_g, w_out1 = _chips_wait(*flight1[:3], h1, scatter=False, name="gather1_wait")[1]
    w_out1 = w_out1.reshape(-1, d)
    w_in1 = jnp.transpose(w_in1_g, (1, 0, 2)).reshape(d, ODD_IN)
    w_in1 = jnp.concatenate([w_in1[:, ODD_SMALL:], _pad_cols(w_in1[:, :ODD_SMALL], ODD_SMALL_PAD)], axis=1)
    w_uq = w_uq_g.reshape(MLA_Q_RANK, MLA_HEADS, MLA_NOPE + MLA_ROPE)
    w_uq_nope = w_uq[:, :, :MLA_NOPE].reshape(MLA_Q_RANK, -1)
    w_uq_rope = jnp.transpose(w_uq[:, :, MLA_NOPE:].reshape(MLA_Q_RANK, MLA_HEADS // 2, 2, 2, ROPE_HALF),
                              (0, 1, 3, 2, 4)).reshape(MLA_Q_RANK, -1)
    proj1 = _matmul(h1, w_in1, tm=1024, tn=1280, out_dtype=BF16, name="proj1")
    q_cn, keys = _mla_prep(proj1, q_norm_g, mla_kv_norm_g, cos_t, sin_t, name="mla_prep")
    q_nope = _matmul(q_cn, w_uq_nope, tm=1024, tn=2048, name="q_nope", out_dtype=BF16)
    q_rope_pre = _matmul(q_cn, w_uq_rope, tm=1024, name="q_rope")
    q = _q_build(q_nope, q_rope_pre, wuk_hdr, cos_t, sin_t, name="q_build")
    o_lat, lse = _attn_fwd(q, keys, name="attn_fwd")
    og = _o_build(o_lat, wuv_hrv, proj1, name="o_build")

    dy1, dres1, g_ln_g1, g_ln_b1, dgate1, loss = _out_loss_ln_bwd(
        og, w_out1, x1, gate[1], ln_g[1:2], ln_b[1:2], target, name="out1_loss_ln")
    dg1 = _matmul(dy1, w_out1, trans_b=True, tn=2048, out_dtype=BF16, name="d_og")
    g_w_out1 = _matmul(og, dy1, trans_a=True, out_dtype=BF16, tm=1024, tk=4096, name="g_out1")
    do_lat, dz, g_uv = _o_bwd(dg1, proj1, o_lat, wuv_hrv, wuv_hvr, name="o_bwd")
    dq, dkeys = _attn_bwd(q, keys, do_lat, o_lat, lse, name="attn_bwd")
    dq_all, g_uk = _q_bwd(dq, q_nope, wuk_hrd, cos_t, sin_t, name="q_bwd")
    n_grp = MLA_HEADS // Q_HEAD_GROUP
    w_uq_all = jnp.concatenate([w_uq_nope.reshape(MLA_Q_RANK, n_grp, -1), w_uq_rope.reshape(MLA_Q_RANK, n_grp, -1)],
                               axis=2).reshape(MLA_Q_RANK, -1)
    dq_cn = _matmul(dq_all, w_uq_all, trans_b=True, tm=1024, tk=3072, name="d_qcn")
    g_uq_all = _matmul(q_cn, dq_all, trans_a=True, out_dtype=BF16, name="g_uq").reshape(MLA_Q_RANK, n_grp, -1)
    g_uq_nope = g_uq_all[:, :, :Q_HEAD_GROUP * MLA_NOPE].reshape(MLA_Q_RANK, -1)
    g_uq_rope = g_uq_all[:, :, Q_HEAD_GROUP * MLA_NOPE:].reshape(MLA_Q_RANK, -1)
    dsmall, g_qg, g_kvg = _mla_prep_bwd(proj1, dq_cn, dkeys, q_norm_g, mla_kv_norm_g, cos_t, sin_t, name="mla_prep_bwd")
    dproj1 = jnp.concatenate([dz, dsmall], axis=1)
    g_w_in1 = _matmul(h1, dproj1, trans_a=True, out_dtype=BF16, tm=1024, tn=1280, tk=4096, name="g_in1")

    g_uq_rope = jnp.transpose(g_uq_rope.reshape(MLA_Q_RANK, MLA_HEADS // 2, 2, 2, ROPE_HALF), (0, 1, 3, 2, 4))
    g_uq = jnp.concatenate([g_uq_nope.reshape(MLA_Q_RANK, MLA_HEADS, MLA_NOPE), g_uq_rope.reshape(MLA_Q_RANK, MLA_HEADS, MLA_ROPE)], axis=2)
    g_w_in1 = jnp.concatenate([g_w_in1[:, MLA_WIDTH:MLA_WIDTH + ODD_SMALL], g_w_in1[:, :MLA_WIDTH]], axis=1)
    g_w_in1 = jnp.transpose(g_w_in1.reshape(d, N_CHIPS, -1), (1, 0, 2))
    big1 = [
        _halves(g_w_in1),
        _halves(g_uq.reshape(N_CHIPS, q_rank_sh, -1)),
        _halves(g_w_out1.reshape(N_CHIPS, -1, d)),
        _halves(g_uk.astype(BF16).reshape(N_CHIPS, -1, MLA_NOPE)),
        _halves(g_uv.astype(BF16).reshape(N_CHIPS, -1, MLA_V)),
    ]
    parts1 = _reduce_sibling(big1, name="reduce_sibling1")
    flight2 = _chips_start(parts1, [lax.empty(p.shape, BF16) for p in parts1], loss, scatter=True, name="reduce1_start")

    gate[0] = gate[0] + flight2[3][:1, :1]
    dy0, dres0, g_ln_g0, g_ln_b0, dgate0, dscale1, dshift1 = _dh_mid_ln_bwd(
        dproj1, w_in1, x2, y0, gate[0], ln_g[0:1], ln_b[0:1], dres1, scale[1], x1, name="d_h1_mid_ln")
    dmix0 = _matmul(dy0, w_out0, trans_b=True, tn=2048, out_dtype=BF16, name="d_mix0")
    g_w_out0 = _matmul(mix0, dy0, trans_a=True, out_dtype=BF16, tm=1024, tk=4096, name="g_out0")
    dproj0, g_ws, g_bs_t, g_ng, g_nb, g_pw, g_pb, g_ps = _even_bwd(
        proj0, dmix0, ws, ws_t, bs_t, gmlp_norm_g, gmlp_norm_b, pool_w_bf, pool_b, pool_scale, name="even_bwd")
    g_w_in0 = _matmul(h0, dproj0, trans_a=True, out_dtype=BF16, out_stacked=True, tm=1024, tn=1280, tk=4096, name="g_in0")

    g_pw = jnp.transpose(g_pw.astype(BF16).reshape(POOL_GROUPS, N_CHIPS, -1, POOL_GROUP_DIM), (1, 0, 2, 3))
    big0 = [
        _halves(g_w_in0),
        _halves(g_pw.reshape(N_CHIPS, -1, POOL_GROUP_DIM)),
        _halves(g_w_out0.reshape(N_CHIPS, -1, d)),
        _halves(g_ws.astype(BF16)),
    ]
    parts0 = _reduce_sibling(big0, name="reduce_sibling0")
    parts1, landed1 = _chips_wait(*flight2[:3], parts0[0], scatter=True, name="reduce1_wait")
    flight3 = _chips_start(parts0, [lax.empty(p.shape, BF16) for p in parts0], landed1[0], scatter=True, name="reduce0_start")
    grad_x, dscale0, dshift0 = _dh_input_bwd(dproj0, w_in0, x2, dres0, scale[0], after=flight3[3], name="d_h0_input")

    small_local = {
        "ln_g": jnp.concatenate([g_ln_g0, g_ln_g1]), "ln_b": jnp.concatenate([g_ln_b0, g_ln_b1]),
        "gmlp_norm_g": g_ng, "gmlp_norm_b": g_nb, "gmlp_bs": g_bs_t[:, :GMLP_HEADS].T, "pool_b": g_pb, "pool_scale": g_ps,
        "mla_kv_norm_g": g_kvg, "mla_q_norm_g": g_qg,
    }
    n_mod = 2 * 3 * d
    vec = jnp.concatenate([dshift0, dscale0, dgate0, dshift1, dscale1, dgate1]
                          + [small_local[n].reshape(1, -1) for n in SMALL] + [loss], axis=1)
    n_vec = vec.shape[1]
    vec = _pad_cols(vec, -(-n_vec // (8 * LANES)) * 8 * LANES).reshape(-1, LANES)
    vec_all = _all_gather_devices(vec, name="gather_small")
    vec_sum = _sum_devices(vec_all, name="sum_small").reshape(-1)
    dmod_all = vec_all.reshape(N_DEV, -1)[:, :n_mod].reshape(N_DEV, 2, 3 * d)
    dmod_sh = jnp.transpose(lax.dynamic_slice_in_dim(dmod_all, chip * cols, cols, axis=2), (1, 0, 2))
    dmod_sh = jnp.concatenate([dmod_sh, jnp.zeros((2, LANES - N_DEV, cols), F32)], axis=1)
    grads = {"ada_w": _ada_grad(_pad_cols(c_all.T, LANES), dmod_sh, name="ada_grad"), "ada_b": vec_sum[:n_mod].reshape(2, 3 * d)}
    off = n_mod
    for n in SMALL:
        sz = small_local[n].size
        grads[n] = vec_sum[off:off + sz]
        off += sz
    grads["mla_q_norm_g"] = lax.dynamic_slice_in_dim(grads["mla_q_norm_g"], chip * q_rank_sh, q_rank_sh)
    for n in SMALL:
        grads[n] = grads[n].reshape(weights[n].shape)

    parts0, landed0 = _chips_wait(*flight3[:3], grads["ada_w"], scatter=True, name="reduce0_wait")
    totals = _reduce_chips(list(parts0) + list(parts1), list(landed0) + list(landed1), name="reduce_chips")
    for n, t in zip(("e_w_in", "pool_w", "e_w_out", "gmlp_ws", "o_w_in", "mla_w_uq", "o_w_out"), totals):
        if n != "gmlp_ws":
            grads[n] = t.reshape(weights[n].shape)
    rep = jnp.concatenate([t.reshape(-1, LANES) for t in (totals[3], totals[7], totals[8])])
    rep_land = lax.dynamic_update_slice(lax.empty((N_CHIPS,) + rep.shape, F32), rep[None], (chip, 0, 0))
    flight4 = _chips_start([rep], [rep_land], totals[0], scatter=False, name="gather_rep_start")

    delta, new_m, new_v = {}, {}, {}
    replicated = ("gmlp_ws", "mla_w_uk", "mla_w_uv")
    large = [n for n in WEIGHTS if n not in SMALL and n != "ada_b"]
    for n in large:
        if n not in replicated:
            delta[n], new_m[n], new_v[n] = _adamw(weights[n], grads[n], mom[n], var[n], after=flight4[3], name="adamw_" + n)
    rep = _chips_wait(*flight4[:3], delta["e_w_in"], scatter=False, name="gather_rep_wait")[1][0]
    r_ws, r_uk = GMLP_BLOCK, 4 * MLA_KV_RANK
    grads["gmlp_ws"] = rep[:, :r_ws].reshape(weights["gmlp_ws"].shape)
    grads["mla_w_uk"] = jnp.transpose(rep[:, r_ws:r_ws + r_uk].reshape(MLA_HEADS, MLA_KV_RANK, MLA_NOPE), (1, 0, 2))[None]
    grads["mla_w_uv"] = jnp.transpose(rep[:, r_ws + r_uk:].reshape(MLA_HEADS, MLA_KV_RANK, MLA_V), (1, 0, 2))[None]
    for n in replicated:
        delta[n], new_m[n], new_v[n] = _adamw(weights[n], grads[n], mom[n], var[n], name="adamw_" + n)
    small = [n for n in WEIGHTS if n not in large]
    ds, ms, vs = _adamw_small([weights[n] for n in small], [grads[n] for n in small], [mom[n] for n in small],
                              [var[n] for n in small], name="adamw_small")
    for n, dn, mn, vn in zip(small, ds, ms, vs):
        delta[n], new_m[n], new_v[n] = dn, mn, vn

    return (vec_sum[n_vec - 1], grad_x[None], *[grads[n] for n in WEIGHTS], *[delta[n] for n in WEIGHTS],
            *[new_m[n] for n in WEIGHTS], *[new_v[n] for n in WEIGHTS])
```

```python
import jax
import jax.numpy as jnp
from jax import lax
from jax.experimental import pallas as pl
from jax.experimental.pallas import tpu as pltpu

F32 = jnp.float32
BF16 = jnp.bfloat16
MESH = pl.DeviceIdType.MESH

D_MODEL = 1024
CHUNK = 64
LN_EPS = 1e-5
GMLP_HEADS = 4
GMLP_HEAD_DIM = 256
GMLP_BLOCK = 128
POOL_WINDOWS = (2, 4, 8, 16)
POOL_GROUPS = 4
POOL_GROUP_DIM = 256
POOL_HALO = 16
EVEN_IN = 5120
MLA_HEADS = 16
MLA_NOPE = 128
MLA_ROPE = 64
MLA_V = 128
MLA_Q_RANK = 256
MLA_KV_RANK = 128
MLA_WIDTH = MLA_HEADS * MLA_V
ODD_IN = 2496
ODD_SMALL = MLA_Q_RANK + MLA_KV_RANK + MLA_ROPE
ODD_SMALL_PAD = 512
QK_PAD = 256
ROPE_THETA = 10000.0
ATTN_SCALE = (MLA_NOPE + MLA_ROPE) ** -0.5
DEEPNORM_ALPHA = (2.0 * 2) ** 0.25
ADAM_LR = 0.001
ADAM_B1 = 0.9
ADAM_B2 = 0.999
ADAM_EPS = 1e-08
ADAM_WD = 0.01
ADAM_STEP = 10
NEG = -1e30
LANES = 128
N_DEV = 8
N_CHIPS = 4
VMEM_LIMIT_BYTES = 56 * 1024 * 1024
HBM = pl.BlockSpec(memory_space=pltpu.HBM)
VMEM = pl.BlockSpec(memory_space=pltpu.VMEM)


def _params(*sem):
    return pltpu.CompilerParams(dimension_semantics=sem if sem else None, vmem_limit_bytes=VMEM_LIMIT_BYTES)


def _tile(dim, pref):
    for t in (pref, 2048, 1280, 1024, 512, 256, 128):
        if t <= min(pref, dim) and dim % t == 0:
            return t
    return dim


def _sigmoid(z):
    return 1.0 / (1.0 + jnp.exp(-z))


def _dot(a, b, dims):
    return lax.dot_general(a, b, (dims, ((), ())), preferred_element_type=F32)


NN = ((1,), (0,))
NT = ((1,), (1,))
TN = ((0,), (0,))


def _matmul(a, b, *, name, trans_a=False, trans_b=False, out_dtype=F32, b_stacked=False, out_stacked=False,
            tm=512, tn=1024, tk=2048, after=None):
    k, m = a.shape if trans_a else a.shape[::-1]
    if b_stacked:
        assert not trans_b
        ns, kb, n_sh = b.shape
        n = ns * n_sh
    else:
        n, kb = b.shape if trans_b else b.shape[::-1]
    assert k == kb, (a.shape, b.shape)
    tm = _tile(m, tm)
    tn, tk = _tile(n // N_CHIPS if b_stacked or out_stacked else n, tn), _tile(k, tk)
    nk = k // tk
    per = max((n // N_CHIPS) // tn, 1)
    dims = ((0 if trans_a else 1,), (1 if trans_b else 0,))

    def body_one(a_ref, b_ref, *rest):
        o_ref = rest[-1]
        o_ref[...] = _dot(a_ref[...].astype(BF16), b_ref[...].astype(BF16), dims).astype(out_dtype)

    def body_acc(a_ref, b_ref, *rest):
        o_ref, acc_ref = rest[-2:]
        kk = pl.program_id(2)

        @pl.when(kk == 0)
        def _():
            acc_ref[...] = jnp.zeros_like(acc_ref)

        acc_ref[...] += _dot(a_ref[...].astype(BF16), b_ref[...].astype(BF16), dims)

        @pl.when(kk == nk - 1)
        def _():
            o_ref[...] = acc_ref[...].astype(out_dtype)

    a_spec = pl.BlockSpec((tk, tm), lambda i, j, kk: (kk, i)) if trans_a else pl.BlockSpec((tm, tk), lambda i, j, kk: (i, kk))
    if b_stacked:
        b_spec = pl.BlockSpec((None, tk, tn), lambda i, j, kk: (j // per, kk, j % per))
    elif trans_b:
        b_spec = pl.BlockSpec((tn, tk), lambda i, j, kk: (j, kk))
    else:
        b_spec = pl.BlockSpec((tk, tn), lambda i, j, kk: (kk, j))
    if out_stacked:
        o_spec = pl.BlockSpec((None, tm, tn), lambda i, j, kk: (j // per, i, j % per))
        o_shape = jax.ShapeDtypeStruct((N_CHIPS, m, n // N_CHIPS), out_dtype)
    else:
        o_spec = pl.BlockSpec((tm, tn), lambda i, j, kk: (i, j))
        o_shape = jax.ShapeDtypeStruct((m, n), out_dtype)
    order = [] if after is None else [after]
    return pl.pallas_call(
        body_one if nk == 1 else body_acc, name=name, grid=(m // tm, n // tn, nk),
        in_specs=[a_spec, b_spec] + [pl.BlockSpec(memory_space=pl.ANY)] * len(order),
        out_specs=o_spec, out_shape=o_shape, scratch_shapes=[] if nk == 1 else [pltpu.VMEM((tm, tn), F32)],
        compiler_params=_params("parallel", "parallel", "arbitrary"),
    )(a, b, *order)


def _matmul_rows(a, b, epilogue, row_ins, vec_ins, row_outs, vec_outs, *, name, trans_b=False, b_stacked=False,
                 tm=512, tk=2048, after=None):
    m, k = a.shape
    if b_stacked:
        ns, n, n_sh = b.shape
        assert trans_b and ns * n_sh == k
        tk = k
    else:
        n = b.shape[0] if trans_b else b.shape[1]
        tk = _tile(k, tk)
    tm = _tile(m, tm)
    nk = k // tk
    dims = ((1,), (1 if trans_b else 0,))
    n_ri, n_vi, n_ro, n_vo = len(row_ins), len(vec_ins), len(row_outs), len(vec_outs)
    order = [] if after is None else [after]

    def body(*refs):
        a_ref, b_ref = refs[:2]
        pos = 2
        rin = refs[pos:pos + n_ri]
        pos += n_ri
        vin = refs[pos:pos + n_vi]
        pos += n_vi + len(order)
        rout = refs[pos:pos + n_ro]
        pos += n_ro
        vout = refs[pos:pos + n_vo]
        first = pl.program_id(0) == 0
        if b_stacked:
            part = _dot(a_ref[:, :n_sh].astype(BF16), b_ref[0].astype(BF16), dims)
            for sh in range(1, ns):
                part = part + _dot(a_ref[:, sh * n_sh:(sh + 1) * n_sh].astype(BF16), b_ref[sh].astype(BF16), dims)
        else:
            part = _dot(a_ref[...].astype(BF16), b_ref[...].astype(BF16), dims)
        if nk == 1:
            epilogue(part, first, rin, vin, rout, vout)
        else:
            acc_ref = refs[-1]
            kk = pl.program_id(1)

            @pl.when(kk == 0)
            def _():
                acc_ref[...] = part

            @pl.when(kk > 0)
            def _():
                acc_ref[...] += part

            @pl.when(kk == nk - 1)
            def _():
                epilogue(acc_ref[...], first, rin, vin, rout, vout)

    a_spec = pl.BlockSpec((tm, tk), lambda i, kk: (i, kk))
    if b_stacked:
        b_spec = pl.BlockSpec((ns, n, n_sh), lambda i, kk: (0, 0, 0))
    elif trans_b:
        b_spec = pl.BlockSpec((n, tk), lambda i, kk: (0, kk))
    else:
        b_spec = pl.BlockSpec((tk, n), lambda i, kk: (kk, 0))
    row = pl.BlockSpec((tm, n), lambda i, kk: (i, 0))
    vec = lambda w: pl.BlockSpec((1, w), lambda i, kk: (0, 0))
    return pl.pallas_call(
        body, name=name, grid=(m // tm, nk),
        in_specs=[a_spec, b_spec] + [row] * n_ri + [vec(v.shape[1]) for v in vec_ins] + [pl.BlockSpec(memory_space=pl.ANY)] * len(order),
        out_specs=[row] * n_ro + [vec(w) for w in vec_outs],
        out_shape=[jax.ShapeDtypeStruct((m, n), dt) for dt in row_outs] + [jax.ShapeDtypeStruct((1, w), F32) for w in vec_outs],
        scratch_shapes=[] if nk == 1 else [pltpu.VMEM((tm, n), F32)],
        compiler_params=_params("arbitrary", "arbitrary"),
    )(a, b, *row_ins, *vec_ins, *order)


def _row_spec(ts, d):
    return pl.BlockSpec((ts, d), lambda i: (i, 0))


def _vec_spec(d):
    return pl.BlockSpec((1, d), lambda i: (0, 0))


def _modulate(x, scale, shift, *, name):
    s, d = x.shape
    ts = _tile(s, 512)

    def body(x_ref, sc_ref, sh_ref, h_ref):
        h_ref[...] = (x_ref[...] * (1.0 + sc_ref[...]) + sh_ref[...]).astype(BF16)

    return pl.pallas_call(
        body, name=name, grid=(s // ts,), in_specs=[_row_spec(ts, d), _vec_spec(d), _vec_spec(d)],
        out_specs=_row_spec(ts, d), out_shape=jax.ShapeDtypeStruct((s, d), BF16), compiler_params=_params("parallel"),
    )(x, scale, shift)


def _ln_stats(pre):
    mu = jnp.mean(pre, axis=-1, keepdims=True)
    xc = pre - mu
    var = jnp.mean(xc * xc, axis=-1, keepdims=True)
    rstd = lax.rsqrt(var + LN_EPS)
    return xc * rstd, rstd


def _ln_bwd_rows(dout, xhat, rstd, g):
    dxh = dout * g
    m1 = jnp.mean(dxh, axis=-1, keepdims=True)
    m2 = jnp.mean(dxh * xhat, axis=-1, keepdims=True)
    return rstd * (dxh - m1 - xhat * m2)


def _colsum(v):
    return jnp.sum(v, axis=0, keepdims=True)


def _out_resid_ln(mix, w_out, x, gate, g, b, scale_next, shift_next, *, name):
    def epilogue(y, first, rin, vin, rout, vout):
        (x_ref,), (gate_ref, g_ref, b_ref, sc_ref, sh_ref), (y_ref, xn_ref, h_ref) = rin, vin, rout
        y_ref[...] = y
        pre = DEEPNORM_ALPHA * x_ref[...] + (1.0 + gate_ref[...]) * y
        xhat, _ = _ln_stats(pre)
        xn = xhat * g_ref[...] + b_ref[...]
        xn_ref[...] = xn
        h_ref[...] = (xn * (1.0 + sc_ref[...]) + sh_ref[...]).astype(BF16)

    return _matmul_rows(mix, w_out, epilogue, [x], [gate, g, b, scale_next, shift_next], [F32, F32, BF16], [], name=name)


def _out_loss_ln_bwd(og, w_out, x, gate, g, b, target, *, name):
    d = x.shape[1]

    def epilogue(yv, first, rin, vin, rout, vout):
        (x_ref, t_ref), (gate_ref, g_ref, b_ref), (dy_ref, dres_ref), (dg_ref, db_ref, dgate_ref, loss_ref) = rin, vin, rout, vout

        @pl.when(first)
        def _():
            for r in vout:
                r[...] = jnp.zeros_like(r)

        pre = DEEPNORM_ALPHA * x_ref[...] + (1.0 + gate_ref[...]) * yv
        xhat, rstd = _ln_stats(pre)
        diff = xhat * g_ref[...] + b_ref[...] - t_ref[...]
        loss_ref[...] += (0.5 / d) * jnp.sum(jnp.sum(diff * diff, axis=1, keepdims=True), axis=0, keepdims=True)
        dout = diff * (1.0 / d)
        dpre = _ln_bwd_rows(dout, xhat, rstd, g_ref[...])
        dy_ref[...] = (dpre * (1.0 + gate_ref[...])).astype(BF16)
        dres_ref[...] = DEEPNORM_ALPHA * dpre
        dg_ref[...] += _colsum(dout * xhat)
        db_ref[...] += _colsum(dout)
        dgate_ref[...] += _colsum(dpre * yv)

    return _matmul_rows(og, w_out, epilogue, [x, target], [gate, g, b], [BF16, F32], [d, d, d, 1], name=name)


def _dh_mid_ln_bwd(dproj, w_in, x, y, gate, g, b, dres_next, scale_next, x_next, *, name):
    d = x.shape[1]

    def epilogue(dh, first, rin, vin, rout, vout):
        (x_ref, y_ref, dr_ref, xn_ref), (gate_ref, g_ref, b_ref, sc_ref), (dy_ref, dres_ref) = rin, vin, rout
        dg_ref, db_ref, dgate_ref, dscale_ref, dshift_ref = vout

        @pl.when(first)
        def _():
            for r in vout:
                r[...] = jnp.zeros_like(r)

        dout = dr_ref[...] + dh * (1.0 + sc_ref[...])
        dscale_ref[...] += _colsum(dh * xn_ref[...])
        dshift_ref[...] += _colsum(dh)
        yv = y_ref[...]
        pre = DEEPNORM_ALPHA * x_ref[...] + (1.0 + gate_ref[...]) * yv
        xhat, rstd = _ln_stats(pre)
        dpre = _ln_bwd_rows(dout, xhat, rstd, g_ref[...])
        dy_ref[...] = (dpre * (1.0 + gate_ref[...])).astype(BF16)
        dres_ref[...] = DEEPNORM_ALPHA * dpre
        dg_ref[...] += _colsum(dout * xhat)
        db_ref[...] += _colsum(dout)
        dgate_ref[...] += _colsum(dpre * yv)

    return _matmul_rows(dproj, w_in, epilogue, [x, y, dres_next, x_next], [gate, g, b, scale_next], [BF16, F32], [d] * 5,
                        trans_b=True, tk=2560, name=name)


def _dh_input_bwd(dproj, w_in_stacked, x, dres, scale, *, name, after):
    d = x.shape[1]

    def epilogue(dh, first, rin, vin, rout, vout):
        (x_ref, dr_ref), (sc_ref,), (dx_ref,), (dscale_ref, dshift_ref) = rin, vin, rout, vout

        @pl.when(first)
        def _():
            for r in vout:
                r[...] = jnp.zeros_like(r)

        dx_ref[...] = dr_ref[...] + dh * (1.0 + sc_ref[...])
        dscale_ref[...] += _colsum(dh * x_ref[...])
        dshift_ref[...] += _colsum(dh)

    return _matmul_rows(dproj, w_in_stacked, epilogue, [x, dres], [scale], [F32], [d, d], trans_b=True, b_stacked=True,
                        tm=512, after=after, name=name)


def _chunk_mask(transposed=False):
    r = lax.broadcasted_iota(jnp.int32, (GMLP_BLOCK, GMLP_BLOCK), 0) // CHUNK
    c = lax.broadcasted_iota(jnp.int32, (GMLP_BLOCK, GMLP_BLOCK), 1) // CHUNK
    return (r <= c) if transposed else (c <= r)


def _window_sum(ext, steps, forward):
    rows = ext.shape[0]
    acc = ext
    for k in range(steps):
        shift = 1 << k
        acc = acc + pltpu.roll(acc, (rows - shift) if forward else shift, 0)
    return acc


def _pool_counts(first_row, rows, win):
    t = first_row + lax.broadcasted_iota(jnp.int32, (rows, 1), 0)
    return jnp.minimum(t + 1, win).astype(F32)


def _even_specs(t):
    col = lambda j: pl.BlockSpec((t, D_MODEL), lambda n: (n, j))
    per = t // POOL_HALO
    prev = pl.BlockSpec((POOL_HALO, D_MODEL), lambda n: (jnp.maximum(n * per - 1, 0), 3))
    return col, per, prev


def _full(shape):
    return pl.BlockSpec(shape, lambda n: (0,) * len(shape))


def _gmlp_head(v_h, ng, nb, w_bf):
    xhat, rstd = _ln_stats(v_h)
    vn = (xhat * ng + nb).astype(BF16)
    return xhat, rstd, vn, _dot(w_bf, vn, NN)


def _pool_group(xb_g, prev_g, first_row, grp):
    t = xb_g.shape[0]
    ext = jnp.concatenate([prev_g, xb_g], axis=0)
    tot = _window_sum(ext, grp + 1, False)[POOL_HALO:, :]
    cnt = _pool_counts(first_row, t, POOL_WINDOWS[grp])
    return tot / cnt - xb_g, cnt


def _even_fwd(proj, ws, bs_t, ng, nb, pool_w, pool_b, pool_scale, *, name):
    s = proj.shape[0]
    t = GMLP_BLOCK
    col, per, prev = _even_specs(t)

    def body(u_ref, v_ref, za_ref, xb_ref, zb_ref, xp_ref, ws_ref, bs_ref, ng_ref, nb_ref, pw_ref, pb_ref, ps_ref, o_ref):
        n = pl.program_id(0)
        mask = _chunk_mask()
        for h in range(GMLP_HEADS):
            c0 = h * GMLP_HEAD_DIM
            cs = slice(c0, c0 + GMLP_HEAD_DIM)
            w_bf = jnp.where(mask, ws_ref[h], 0.0).astype(BF16)
            _, _, _, sv = _gmlp_head(v_ref[:, cs].astype(F32),ng_ref[...], nb_ref[...], w_bf)
            sv = sv + bs_ref[:, h:h + 1]
            za = za_ref[:, cs].astype(F32)
            o_ref[:, cs] = (u_ref[:, cs].astype(F32) * sv * (za * _sigmoid(za))).astype(BF16)
        live = (n > 0).astype(F32)
        for grp in range(POOL_GROUPS):
            c0 = grp * POOL_GROUP_DIM
            cs = slice(c0, c0 + POOL_GROUP_DIM)
            pooled, _ = _pool_group(xb_ref[:, cs].astype(F32), xp_ref[:, cs].astype(F32) * live, n * t, grp)
            yb = _dot(pooled.astype(BF16), pw_ref[grp], NN) + pb_ref[:, cs]
            zb = zb_ref[:, cs].astype(F32)
            o_ref[:, D_MODEL + c0:D_MODEL + c0 + POOL_GROUP_DIM] = (yb * ps_ref[:, cs] * (zb * _sigmoid(zb))).astype(BF16)

    return pl.pallas_call(
        body, name=name, grid=(s // t,),
        in_specs=[col(0), col(1), col(2), col(3), col(4), prev,
                  _full((GMLP_HEADS, t, t)), _full((t, LANES)), _full((1, GMLP_HEAD_DIM)), _full((1, GMLP_HEAD_DIM)),
                  _full((POOL_GROUPS, POOL_GROUP_DIM, POOL_GROUP_DIM)), _full((1, D_MODEL)), _full((1, D_MODEL))],
        out_specs=pl.BlockSpec((t, 2 * D_MODEL), lambda n: (n, 0)),
        out_shape=jax.ShapeDtypeStruct((s, 2 * D_MODEL), BF16),
        compiler_params=_params("parallel"),
    )(proj, proj, proj, proj, proj, proj, ws, bs_t, ng, nb, pool_w, pool_b, pool_scale)


def _even_bwd(proj, dmix, ws, ws_t, bs_t, ng, nb, pool_w, pool_b, pool_scale, *, name):
    s = proj.shape[0]
    t = GMLP_BLOCK
    nblk = s // t
    col, per, prev = _even_specs(t)
    nxt = lambda j: pl.BlockSpec((POOL_HALO, D_MODEL), lambda n: (jnp.minimum((n + 1) * per, nblk * per - 1), j))

    def body(u_ref, v_ref, za_ref, xb_ref, zb_ref, xp_ref, zn_ref, da_ref, db_ref, dbn_ref,
             ws_ref, wst_ref, bs_ref, ng_ref, nb_ref, pw_ref, pb_ref, ps_ref,
             dp_ref, gws_ref, gbs_ref, gng_ref, gnb_ref, gpw_ref, gpb_ref, gps_ref):
        n = pl.program_id(0)

        @pl.when(n == 0)
        def _():
            for r in (gws_ref, gbs_ref, gng_ref, gnb_ref, gpw_ref, gpb_ref, gps_ref):
                r[...] = jnp.zeros_like(r)

        mask, mask_t = _chunk_mask(), _chunk_mask(True)
        lane = lax.broadcasted_iota(jnp.int32, (t, LANES), 1)
        ngv, nbv = ng_ref[...], nb_ref[...]
        for h in range(GMLP_HEADS):
            c0 = h * GMLP_HEAD_DIM
            cs = slice(c0, c0 + GMLP_HEAD_DIM)
            w_bf = jnp.where(mask, ws_ref[h], 0.0).astype(BF16)
            wt_bf = jnp.where(mask_t, wst_ref[h], 0.0).astype(BF16)
            xhat, rstd, vn, sv = _gmlp_head(v_ref[:, cs].astype(F32),ngv, nbv, w_bf)
            sv = sv + bs_ref[:, h:h + 1]
            za, u, da = za_ref[:, cs].astype(F32), u_ref[:, cs].astype(F32), da_ref[:, cs].astype(F32)
            sg = _sigmoid(za)
            sl = za * sg
            dp_ref[:, cs] = (da * sv * sl).astype(BF16)
            dp_ref[:, 2 * D_MODEL + c0:2 * D_MODEL + c0 + GMLP_HEAD_DIM] = (
                da * u * sv * (sg * (1.0 + za * (1.0 - sg)))).astype(BF16)
            dsv = da * u * sl
            gbs_ref[...] += jnp.where(lane == h, jnp.sum(dsv, axis=1, keepdims=True), 0.0)
            dsv_bf = dsv.astype(BF16)
            gws_ref[h] += jnp.where(mask, _dot(dsv_bf, vn, NT), 0.0)
            dvn = _dot(wt_bf, dsv_bf, NN)
            dp_ref[:, D_MODEL + c0:D_MODEL + c0 + GMLP_HEAD_DIM] = _ln_bwd_rows(dvn, xhat, rstd, ngv).astype(BF16)
            gng_ref[...] += _colsum(dvn * xhat)
            gnb_ref[...] += _colsum(dvn)
        live_prev = (n > 0).astype(F32)
        live_next = (n < nblk - 1).astype(F32)
        for grp in range(POOL_GROUPS):
            c0 = grp * POOL_GROUP_DIM
            cs = slice(c0, c0 + POOL_GROUP_DIM)
            xb = xb_ref[:, cs].astype(F32)
            pooled, cnt = _pool_group(xb, xp_ref[:, cs].astype(F32) * live_prev, n * t, grp)
            pooled_bf = pooled.astype(BF16)
            pw = pw_ref[grp]
            yb = _dot(pooled_bf, pw, NN) + pb_ref[:, cs]
            ps = ps_ref[:, cs]
            zb, db = zb_ref[:, cs].astype(F32), db_ref[:, cs].astype(F32)
            sg = _sigmoid(zb)
            sl = zb * sg
            dp_ref[:, 4 * D_MODEL + c0:4 * D_MODEL + c0 + POOL_GROUP_DIM] = (
                db * yb * ps * (sg * (1.0 + zb * (1.0 - sg)))).astype(BF16)
            dsl = db * sl
            dy = dsl * ps
            gps_ref[:, cs] += _colsum(dsl * yb)
            gpb_ref[:, cs] += _colsum(dy)
            dy_bf = dy.astype(BF16)
            gpw_ref[grp] += _dot(pooled_bf, dy_bf, TN)
            r = _dot(dy_bf, pw, NT)
            zn = zn_ref[:, cs].astype(F32)
            dyn = (dbn_ref[:, cs].astype(F32) * (zn * _sigmoid(zn)) * ps * live_next).astype(BF16)
            rn = _dot(dyn, pw, NT) / _pool_counts((n + 1) * t, POOL_HALO, POOL_WINDOWS[grp])
            ext = jnp.concatenate([r / cnt, rn], axis=0)
            dxb = _window_sum(ext, grp + 1, True)[:t, :] - r
            dp_ref[:, 3 * D_MODEL + c0:3 * D_MODEL + c0 + POOL_GROUP_DIM] = dxb.astype(BF16)

    out_shape = [
        jax.ShapeDtypeStruct((s, EVEN_IN), BF16),
        jax.ShapeDtypeStruct((GMLP_HEADS, t, t), F32), jax.ShapeDtypeStruct((t, LANES), F32),
        jax.ShapeDtypeStruct((1, GMLP_HEAD_DIM), F32), jax.ShapeDtypeStruct((1, GMLP_HEAD_DIM), F32),
        jax.ShapeDtypeStruct((POOL_GROUPS, POOL_GROUP_DIM, POOL_GROUP_DIM), F32),
        jax.ShapeDtypeStruct((1, D_MODEL), F32), jax.ShapeDtypeStruct((1, D_MODEL), F32),
    ]
    return pl.pallas_call(
        body, name=name, grid=(nblk,),
        in_specs=[col(0), col(1), col(2), col(3), col(4), prev, nxt(4),
                  pl.BlockSpec((t, D_MODEL), lambda n: (n, 0)), pl.BlockSpec((t, D_MODEL), lambda n: (n, 1)), nxt(1),
                  _full((GMLP_HEADS, t, t)), _full((GMLP_HEADS, t, t)), _full((t, LANES)),
                  _full((1, GMLP_HEAD_DIM)), _full((1, GMLP_HEAD_DIM)),
                  _full((POOL_GROUPS, POOL_GROUP_DIM, POOL_GROUP_DIM)), _full((1, D_MODEL)), _full((1, D_MODEL))],
        out_specs=[pl.BlockSpec((t, EVEN_IN), lambda n: (n, 0))] + [_full(o.shape) for o in out_shape[1:]],
        out_shape=out_shape,
        compiler_params=_params("arbitrary"),
    )(proj, proj, proj, proj, proj, proj, proj, dmix, dmix, dmix, ws, ws_t, bs_t, ng, nb, pool_w, pool_b, pool_scale)


ROPE_HALF = MLA_ROPE // 2


def _rope(v, cos, sin_signed):
    return v * cos + pltpu.roll(v, 2 * ROPE_HALF, 1) * sin_signed


def _rope_bwd(d, cos, sin_signed):
    return d * cos + pltpu.roll(d * sin_signed, 2 * ROPE_HALF, 1)


def _slab_lanes(shape, which):
    lane = lax.broadcasted_iota(jnp.int32, shape, 1)
    return (lane // ROPE_HALF) % 2 == which


def _rms(v, g):
    r = lax.rsqrt(jnp.mean(v * v, axis=-1, keepdims=True) + LN_EPS)
    return v * r * g, r


def _rms_bwd(dy, v, r, g):
    u = dy * g
    return r * u - v * (r * r * r) * jnp.mean(u * v, axis=-1, keepdims=True)


def _mla_prep(proj, gq, gkv, cos, sin_signed, *, name):
    s = proj.shape[0]
    ts = _tile(s, 512)

    def body(p_ref, gq_ref, gkv_ref, c_ref, s_ref, q_ref, k_ref):
        qcn, _ = _rms(p_ref[:, :MLA_Q_RANK].astype(F32), gq_ref[...])
        kvn, _ = _rms(p_ref[:, MLA_Q_RANK:MLA_Q_RANK + MLA_KV_RANK].astype(F32), gkv_ref[...])
        kr = p_ref[:, MLA_Q_RANK + MLA_KV_RANK:].astype(F32)
        lane = lax.broadcasted_iota(jnp.int32, kr.shape, 1)
        by1, by2 = pltpu.roll(kr, ROPE_HALF, 1), pltpu.roll(kr, 2 * ROPE_HALF, 1)
        both = jnp.where(lane < ROPE_HALF, kr, jnp.where(lane < 3 * ROPE_HALF, by1, by2))
        kr = _rope(both, c_ref[...], s_ref[...])
        q_ref[...] = qcn.astype(BF16)
        k_ref[...] = jnp.concatenate([kvn, kr], axis=1).astype(BF16)

    return pl.pallas_call(
        body, name=name, grid=(s // ts,),
        in_specs=[_small_spec(ts), _vec_spec(MLA_Q_RANK), _vec_spec(MLA_KV_RANK), _row_spec(ts, LANES), _row_spec(ts, LANES)],
        out_specs=[_row_spec(ts, MLA_Q_RANK), _row_spec(ts, QK_PAD)],
        out_shape=[jax.ShapeDtypeStruct((s, MLA_Q_RANK), BF16), jax.ShapeDtypeStruct((s, QK_PAD), BF16)],
        compiler_params=_params("parallel"),
    )(proj, gq, gkv, cos, sin_signed)


def _mla_prep_bwd(proj, dqcn, dkv, gq, gkv, cos, sin_signed, *, name):
    s = proj.shape[0]
    ts = _tile(s, 512)

    def body(p_ref, dq_ref, dkv_ref, gq_ref, gkv_ref, c_ref, s_ref, ds_ref, ggq_ref, ggkv_ref):
        @pl.when(pl.program_id(0) == 0)
        def _():
            ggq_ref[...] = jnp.zeros_like(ggq_ref)
            ggkv_ref[...] = jnp.zeros_like(ggkv_ref)

        qc = p_ref[:, :MLA_Q_RANK].astype(F32)
        kvc = p_ref[:, MLA_Q_RANK:MLA_Q_RANK + MLA_KV_RANK].astype(F32)
        _, rq = _rms(qc, gq_ref[...])
        _, rkv = _rms(kvc, gkv_ref[...])
        dq = dq_ref[...]
        dkvn = dkv_ref[:, :MLA_KV_RANK]
        ggq_ref[...] += _colsum(dq * qc * rq)
        ggkv_ref[...] += _colsum(dkvn * kvc * rkv)
        dboth = _rope_bwd(dkv_ref[:, MLA_KV_RANK:], c_ref[...], s_ref[...])
        lane = lax.broadcasted_iota(jnp.int32, dboth.shape, 1)
        pair = dboth + pltpu.roll(dboth, 3 * ROPE_HALF, 1)
        dkr = jnp.where(lane < ROPE_HALF, pair, jnp.where(lane < 2 * ROPE_HALF, pltpu.roll(pair, 3 * ROPE_HALF, 1), 0.0))
        ds_ref[...] = jnp.concatenate(
            [_rms_bwd(dq, qc, rq, gq_ref[...]), _rms_bwd(dkvn, kvc, rkv, gkv_ref[...]), dkr], axis=1).astype(BF16)

    return pl.pallas_call(
        body, name=name, grid=(s // ts,),
        in_specs=[_small_spec(ts), _row_spec(ts, MLA_Q_RANK), _row_spec(ts, QK_PAD),
                  _vec_spec(MLA_Q_RANK), _vec_spec(MLA_KV_RANK), _row_spec(ts, LANES), _row_spec(ts, LANES)],
        out_specs=[_row_spec(ts, ODD_SMALL_PAD), _vec_spec(MLA_Q_RANK), _vec_spec(MLA_KV_RANK)],
        out_shape=[jax.ShapeDtypeStruct((s, ODD_SMALL_PAD), BF16), jax.ShapeDtypeStruct((1, MLA_Q_RANK), F32),
                   jax.ShapeDtypeStruct((1, MLA_KV_RANK), F32)],
        compiler_params=_params("arbitrary"),
    )(proj, dqcn, dkv, gq, gkv, cos, sin_signed)


Q_HEAD_GROUP = 16
LOG2_E = 1.4426950408889634
Q_PRESCALE = ATTN_SCALE * LOG2_E


def _q_build(q_nope, q_rope_pre, wuk_hdr, cos, sin_signed, *, name):
    s = q_nope.shape[0]
    ts = _tile(s, 512)
    hg = Q_HEAD_GROUP

    def body(qn_ref, qr_ref, w_ref, c_ref, s_ref, o_ref):
        for pair in range(hg // 2):
            r = _rope(qr_ref[:, pair * LANES:(pair + 1) * LANES], c_ref[...], s_ref[...])
            for j in range(2):
                h = 2 * pair + j
                ql = _dot(qn_ref[:, h * MLA_NOPE:(h + 1) * MLA_NOPE], w_ref[h], NN)
                mine = jnp.where(_slab_lanes(r.shape, j), r, 0.0)
                o_ref[h] = (jnp.concatenate([ql, mine], axis=1) * Q_PRESCALE).astype(BF16)

    return pl.pallas_call(
        body, name=name, grid=(s // ts, MLA_HEADS // hg),
        in_specs=[pl.BlockSpec((ts, hg * MLA_NOPE), lambda i, p: (i, p)), pl.BlockSpec((ts, hg * MLA_ROPE), lambda i, p: (i, p)),
                  pl.BlockSpec((hg, MLA_NOPE, MLA_KV_RANK), lambda i, p: (p, 0, 0)),
                  pl.BlockSpec((ts, LANES), lambda i, p: (i, 0)), pl.BlockSpec((ts, LANES), lambda i, p: (i, 0))],
        out_specs=pl.BlockSpec((hg, ts, QK_PAD), lambda i, p: (p, i, 0)),
        out_shape=jax.ShapeDtypeStruct((MLA_HEADS, s, QK_PAD), BF16),
        compiler_params=_params("parallel", "parallel"),
    )(q_nope, q_rope_pre, wuk_hdr, cos, sin_signed)


def _q_bwd(dq, q_nope, wuk_hrd, cos, sin_signed, *, name):
    s = q_nope.shape[0]
    ts = _tile(s, 512)
    hg = Q_HEAD_GROUP

    nope_w, all_w = hg * MLA_NOPE, hg * (MLA_NOPE + MLA_ROPE)

    def body(dq_ref, qn_ref, w_ref, c_ref, s_ref, dall_ref, gw_ref):
        @pl.when(pl.program_id(1) == 0)
        def _():
            gw_ref[...] = jnp.zeros_like(gw_ref)

        for h in range(hg):
            dql = dq_ref[h, :, :MLA_KV_RANK]
            dall_ref[:, h * MLA_NOPE:(h + 1) * MLA_NOPE] = _dot(dql, w_ref[h], NN).astype(BF16)
            gw_ref[h] += _dot(dql, qn_ref[:, h * MLA_NOPE:(h + 1) * MLA_NOPE], TN)
        for pair in range(hg // 2):
            hi0 = dq_ref[2 * pair, :, MLA_KV_RANK:].astype(F32)
            hi1 = dq_ref[2 * pair + 1, :, MLA_KV_RANK:].astype(F32)
            d = jnp.where(_slab_lanes(hi0.shape, 0), hi0, hi1)
            dall_ref[:, nope_w + pair * LANES:nope_w + (pair + 1) * LANES] = _rope_bwd(d, c_ref[...], s_ref[...]).astype(BF16)

    return pl.pallas_call(
        body, name=name, grid=(MLA_HEADS // hg, s // ts),
        in_specs=[pl.BlockSpec((hg, ts, QK_PAD), lambda p, i: (p, i, 0)), pl.BlockSpec((ts, nope_w), lambda p, i: (i, p)),
                  pl.BlockSpec((hg, MLA_KV_RANK, MLA_NOPE), lambda p, i: (p, 0, 0)),
                  pl.BlockSpec((ts, LANES), lambda p, i: (i, 0)), pl.BlockSpec((ts, LANES), lambda p, i: (i, 0))],
        out_specs=[pl.BlockSpec((ts, all_w), lambda p, i: (i, p)),
                   pl.BlockSpec((hg, MLA_KV_RANK, MLA_NOPE), lambda p, i: (p, 0, 0))],
        out_shape=[jax.ShapeDtypeStruct((s, MLA_HEADS * (MLA_NOPE + MLA_ROPE)), BF16),
                   jax.ShapeDtypeStruct((MLA_HEADS, MLA_KV_RANK, MLA_NOPE), F32)],
        compiler_params=_params("parallel", "arbitrary"),
    )(dq, q_nope, wuk_hrd, cos, sin_signed)


ATTN_BQ = 128
ATTN_BK = 512
ATTN_BK_FWD = 1024


def _diag_mask(rows, bq, bk, q0, k0):
    qc = (q0 + lax.broadcasted_iota(jnp.int32, (rows, bk), 0) % bq) // CHUNK
    kc = (k0 + lax.broadcasted_iota(jnp.int32, (rows, bk), 1)) // CHUNK
    return kc <= qc


def _attn_fwd(q, k, *, name):
    nh, s, dk = q.shape
    bq, bk = _tile(s, ATTN_BQ), _tile(s, ATTN_BK_FWD)
    rows = nh * bq

    def body(q_ref, k_ref, o_ref, lse_ref):
        i = pl.program_id(0)
        qb = q_ref[...].reshape(rows, dk)
        n_before = (i * bq) // bk

        def step(j, width, carry, masked):
            m, l, acc = carry
            k0 = pl.multiple_of(j * bk, bk)
            kb = k_ref[pl.ds(k0, width), :]
            sc = _dot(qb, kb, NT)
            if masked:
                sc = jnp.where(_diag_mask(rows, bq, width, i * bq, k0), sc, NEG)
            m_new = jnp.maximum(m, jnp.max(sc, axis=1, keepdims=True))
            p = jnp.exp2(sc - m_new)
            a = jnp.exp2(m - m_new)
            l = a * l + jnp.sum(p, axis=1, keepdims=True)
            acc = a * acc + _dot(p.astype(BF16), kb[:, :MLA_KV_RANK], NN)
            return m_new, l, acc

        init = (jnp.full((rows, 1), NEG, F32), jnp.zeros((rows, 1), F32), jnp.zeros((rows, MLA_KV_RANK), F32))
        carry = lax.fori_loop(0, n_before, lambda j, c: step(j, bk, c, False), init)
        for part in range(bk // bq):
            @pl.when(i % (bk // bq) == part)
            def _(part=part):
                m, l, acc = step(n_before, (part + 1) * bq, carry, True)
                o_ref[...] = (acc / l).astype(BF16).reshape(nh, bq, MLA_KV_RANK)
                lse_ref[...] = jnp.broadcast_to(m + jnp.log2(l), (rows, LANES)).reshape(nh, bq, LANES)

    return pl.pallas_call(
        body, name=name, grid=(s // bq,),
        in_specs=[pl.BlockSpec((nh, bq, dk), lambda i: (0, i, 0)), pl.BlockSpec((s, dk), lambda i: (0, 0))],
        out_specs=[pl.BlockSpec((nh, bq, MLA_KV_RANK), lambda i: (0, i, 0)), pl.BlockSpec((nh, bq, LANES), lambda i: (0, i, 0))],
        out_shape=[jax.ShapeDtypeStruct((nh, s, MLA_KV_RANK), BF16), jax.ShapeDtypeStruct((nh, s, LANES), F32)],
        compiler_params=_params("parallel"),
    )(q, k)


def _attn_bwd(q, k, do, o, lse, *, name):
    nh, s, dk = q.shape
    bq, bk = _tile(s, ATTN_BQ), _tile(s, ATTN_BK)
    rows = nh * bq

    def body(q_ref, k_ref, do_ref, o_ref, lse_ref, dq_ref, dkv_ref):
        i = pl.program_id(0)
        n_before = (i * bq) // bk

        @pl.when(i == 0)
        def _():
            dkv_ref[...] = jnp.zeros_like(dkv_ref)

        qb = q_ref[...].reshape(rows, dk)
        dob = do_ref[...].reshape(rows, MLA_KV_RANK)
        lse_b = lse_ref[...].reshape(rows, LANES)[:, :1]
        delta = jnp.sum(dob.astype(F32) * o_ref[...].reshape(rows, MLA_KV_RANK).astype(F32), axis=1, keepdims=True)

        def step(j, width, dq, masked):
            j0 = pl.multiple_of(j * bk, bk)
            kb = k_ref[pl.ds(j0, width), :]
            sc = _dot(qb, kb, NT)
            if masked:
                sc = jnp.where(_diag_mask(rows, bq, width, i * bq, j0), sc, NEG)
            p = jnp.exp2(sc - lse_b)
            dp = _dot(dob, kb[:, :MLA_KV_RANK], NT)
            ds_bf = (p * (dp - delta)).astype(BF16)
            dkv_ref[pl.ds(j0, width), :] += _dot(ds_bf, qb, TN) * (1.0 / LOG2_E)
            dkv_ref[pl.ds(j0, width), :MLA_KV_RANK] += _dot(p.astype(BF16), dob, TN)
            return dq + _dot(ds_bf, kb, NN)

        dq_before = lax.fori_loop(0, n_before, lambda j, c: step(j, bk, c, False), jnp.zeros((rows, dk), F32))
        for part in range(bk // bq):
            @pl.when(i % (bk // bq) == part)
            def _(part=part):
                dq = step(n_before, (part + 1) * bq, dq_before, True) * ATTN_SCALE
                dq_ref[...] = dq.astype(BF16).reshape(nh, bq, dk)

    blk = lambda w: pl.BlockSpec((nh, bq, w), lambda i: (0, i, 0))
    return pl.pallas_call(
        body, name=name, grid=(s // bq,),
        in_specs=[blk(dk), pl.BlockSpec((s, dk), lambda i: (0, 0)), blk(MLA_KV_RANK), blk(MLA_KV_RANK), blk(LANES)],
        out_specs=[blk(dk), pl.BlockSpec((s, dk), lambda i: (0, 0))],
        out_shape=[jax.ShapeDtypeStruct((nh, s, dk), BF16), jax.ShapeDtypeStruct((s, dk), F32)],
        compiler_params=_params("arbitrary"),
    )(q, k, do, o, lse)


HEAD_GROUP = 4
SMALL_BLOCK = MLA_WIDTH // ODD_SMALL_PAD


def _small_spec(ts):
    return pl.BlockSpec((ts, ODD_SMALL_PAD), lambda i: (i, SMALL_BLOCK))


def _o_build(o_lat, wuv_hrv, proj, *, name):
    s = proj.shape[0]
    ts = _tile(s, 1024)
    w = HEAD_GROUP * MLA_V

    def body(ol_ref, w_ref, z_ref, og_ref):
        for j in range(HEAD_GROUP):
            cs = slice(j * MLA_V, (j + 1) * MLA_V)
            z = z_ref[:, cs].astype(F32)
            og_ref[:, cs] = (_dot(ol_ref[j], w_ref[j], NN) * (z * _sigmoid(z))).astype(BF16)

    return pl.pallas_call(
        body, name=name, grid=(s // ts, MLA_HEADS // HEAD_GROUP),
        in_specs=[pl.BlockSpec((HEAD_GROUP, ts, MLA_KV_RANK), lambda i, g: (g, i, 0)),
                  pl.BlockSpec((HEAD_GROUP, MLA_KV_RANK, MLA_V), lambda i, g: (g, 0, 0)),
                  pl.BlockSpec((ts, w), lambda i, g: (i, g))],
        out_specs=pl.BlockSpec((ts, w), lambda i, g: (i, g)),
        out_shape=jax.ShapeDtypeStruct((s, MLA_WIDTH), BF16),
        compiler_params=_params("parallel", "parallel"),
    )(o_lat, wuv_hrv, proj)


def _o_bwd(dg, proj, o_lat, wuv_hrv, wuv_hvr, *, name):
    s = proj.shape[0]
    ts = _tile(s, 1024)
    w = HEAD_GROUP * MLA_V

    def body(dg_ref, z_ref, ol_ref, w_ref, wt_ref, dol_ref, dz_ref, gw_ref):
        @pl.when(pl.program_id(1) == 0)
        def _():
            gw_ref[...] = jnp.zeros_like(gw_ref)

        for j in range(HEAD_GROUP):
            cs = slice(j * MLA_V, (j + 1) * MLA_V)
            z, dgj, ol = z_ref[:, cs].astype(F32), dg_ref[:, cs].astype(F32), ol_ref[j]
            sg = _sigmoid(z)
            o = _dot(ol, w_ref[j], NN)
            dz_ref[:, cs] = (dgj * o * (sg * (1.0 + z * (1.0 - sg)))).astype(BF16)
            do_bf = (dgj * (z * sg)).astype(BF16)
            dol_ref[j] = _dot(do_bf, wt_ref[j], NN).astype(BF16)
            gw_ref[j] += _dot(ol, do_bf, TN)

    hs = lambda a, b: pl.BlockSpec((HEAD_GROUP, a, b), lambda g, i: (g, 0, 0))
    return pl.pallas_call(
        body, name=name, grid=(MLA_HEADS // HEAD_GROUP, s // ts),
        in_specs=[pl.BlockSpec((ts, w), lambda g, i: (i, g)), pl.BlockSpec((ts, w), lambda g, i: (i, g)),
                  pl.BlockSpec((HEAD_GROUP, ts, MLA_KV_RANK), lambda g, i: (g, i, 0)),
                  hs(MLA_KV_RANK, MLA_V), hs(MLA_V, MLA_KV_RANK)],
        out_specs=[pl.BlockSpec((HEAD_GROUP, ts, MLA_KV_RANK), lambda g, i: (g, i, 0)),
                   pl.BlockSpec((ts, w), lambda g, i: (i, g)), hs(MLA_KV_RANK, MLA_V)],
        out_shape=[jax.ShapeDtypeStruct((MLA_HEADS, s, MLA_KV_RANK), BF16), jax.ShapeDtypeStruct((s, MLA_WIDTH), BF16),
                   jax.ShapeDtypeStruct((MLA_HEADS, MLA_KV_RANK, MLA_V), F32)],
        compiler_params=_params("parallel", "arbitrary"),
    )(dg, proj, o_lat, wuv_hrv, wuv_hvr)


def _ada_mod(c_all, ada_w, ada_b_sh, *, name):
    nl, _, cols = ada_w.shape

    def body(c_ref, w_ref, b_ref, o_ref):
        c = c_ref[...]
        cond = (c * _sigmoid(c)).astype(BF16)
        for l in range(nl):
            o_ref[l] = _dot(cond, w_ref[l].astype(BF16), NN) + b_ref[l]

    return pl.pallas_call(
        body, name=name, out_shape=jax.ShapeDtypeStruct((nl, c_all.shape[0], cols), F32),
        compiler_params=_params(),
    )(c_all, ada_w, ada_b_sh)


def _ada_grad(c_all_t, dmod_sh, *, name):
    nl, _, cols = dmod_sh.shape
    d = c_all_t.shape[0]

    def body(c_ref, dm_ref, gw_ref):
        c = c_ref[...]
        cond_t = c * _sigmoid(c)
        for l in range(nl):
            gw_ref[l] = lax.dot_general(cond_t, dm_ref[l], (NN, ((), ())), precision=lax.Precision.HIGHEST,
                                        preferred_element_type=F32)

    return pl.pallas_call(
        body, name=name, out_shape=jax.ShapeDtypeStruct((nl, d, cols), F32), compiler_params=_params(),
    )(c_all_t, dmod_sh)


def _sum_devices(parts, *, name):
    def body(p_ref, o_ref):
        acc = p_ref[0]
        for k in range(1, parts.shape[0]):
            acc = acc + p_ref[k]
        o_ref[...] = acc

    return pl.pallas_call(body, name=name, out_shape=jax.ShapeDtypeStruct(parts.shape[1:], F32), compiler_params=_params())(parts)


def _adamw_math(w, g, m, v):
    c1 = 1.0 - ADAM_B1 ** ADAM_STEP
    c2 = 1.0 - ADAM_B2 ** ADAM_STEP
    nm = ADAM_B1 * m + (1.0 - ADAM_B1) * g
    nv = ADAM_B2 * v + (1.0 - ADAM_B2) * (g * g)
    return -ADAM_LR * ((nm / c1) / (jnp.sqrt(nv / c2) + ADAM_EPS) + ADAM_WD * w), nm, nv


ADAMW_BLOCK_BYTES = 1 << 20


def _adamw(w, g, m, v, *, name, after=None):
    shape = w.shape
    a, b = shape[-2], shape[-1]
    lead = 1
    for dim in shape[:-2]:
        lead *= dim
    row_bytes = 4 * b
    if a * row_bytes <= ADAMW_BLOCK_BYTES:
        ta = a
        tl = max(1, min(lead, ADAMW_BLOCK_BYTES // (a * row_bytes)))
        while lead % tl:
            tl -= 1
    else:
        tl = 1
        ta = _tile(a, 256)
    to3 = lambda t: t.reshape(lead, a, b)

    def body(w_ref, g_ref, m_ref, v_ref, *rest):
        d_ref, nm_ref, nv_ref = rest[-3:]
        d_ref[...], nm_ref[...], nv_ref[...] = _adamw_math(w_ref[...], g_ref[...], m_ref[...], v_ref[...])

    spec = pl.BlockSpec((tl, ta, b), lambda i, j: (i, j, 0))
    out = jax.ShapeDtypeStruct((lead, a, b), F32)
    order = [] if after is None else [after]
    res = pl.pallas_call(
        body, name=name, grid=(lead // tl, a // ta), in_specs=[spec] * 4 + [pl.BlockSpec(memory_space=pl.ANY)] * len(order),
        out_specs=[spec] * 3, out_shape=[out] * 3, compiler_params=_params("parallel", "parallel"),
    )(to3(w), to3(g), to3(m), to3(v), *order)
    return [r.reshape(shape) for r in res]


def _adamw_small(ws, gs, ms, vs, *, name):
    n = len(ws)

    def body(*refs):
        for k in range(n):
            w_ref, g_ref, m_ref, v_ref = (refs[j * n + k] for j in range(4))
            d_ref, nm_ref, nv_ref = (refs[(4 + j) * n + k] for j in range(3))
            d_ref[...], nm_ref[...], nv_ref[...] = _adamw_math(w_ref[...], g_ref[...], m_ref[...], v_ref[...])

    outs = [jax.ShapeDtypeStruct(w.shape, F32) for w in ws]
    res = pl.pallas_call(body, name=name, out_shape=outs * 3, compiler_params=_params())(*ws, *gs, *ms, *vs)
    return res[:n], res[n:2 * n], res[2 * n:]


def _flip(v, bit):
    return 1 - v if bit else v


CHIP_DELTAS = ((1, 0), (0, 1), (1, 1))
SUM_ROWS = 32


def _all_gather_chips(shard, *, name):
    def body(x_ref, o_ref, send_sems, recv_sems, local_sem):
        x, y, c = lax.axis_index("x"), lax.axis_index("y"), lax.axis_index("c")
        mine = pltpu.make_async_copy(x_ref, o_ref.at[2 * x + y], local_sem)
        mine.start()

        def copy(k):
            tx, ty = _flip(x, CHIP_DELTAS[k][0]), _flip(y, CHIP_DELTAS[k][1])
            send = pltpu.make_async_remote_copy(src_ref=x_ref, dst_ref=o_ref.at[2 * x + y], send_sem=send_sems.at[k],
                                                recv_sem=recv_sems.at[k], device_id=(tx, ty, c), device_id_type=MESH)
            recv = pltpu.make_async_remote_copy(src_ref=x_ref, dst_ref=o_ref.at[2 * tx + ty], send_sem=send_sems.at[k],
                                                recv_sem=recv_sems.at[k], device_id=(tx, ty, c), device_id_type=MESH)
            return send, recv

        pairs = [copy(k) for k in range(3)]
        for send, _ in pairs:
            send.start()
        for _, recv in pairs:
            recv.wait_recv()
        for send, _ in pairs:
            send.wait_send()
        mine.wait()

    return pl.pallas_call(
        body, name=name, out_shape=jax.ShapeDtypeStruct((N_CHIPS,) + shard.shape, shard.dtype),
        in_specs=[HBM], out_specs=HBM,
        scratch_shapes=[pltpu.SemaphoreType.DMA((3,)), pltpu.SemaphoreType.DMA((3,)), pltpu.SemaphoreType.DMA(())],
    )(shard)


def _gather_weights(shards, *, name):
    n = len(shards)

    def body(*refs):
        w_refs, o_refs = refs[:n], refs[n:2 * n]
        ici_send, ici_recv, d2d_send, d2d_recv, local_sems = refs[2 * n:]
        x, y, c = lax.axis_index("x"), lax.axis_index("y"), lax.axis_index("c")
        me = 2 * x + y
        peers = [(_flip(x, dx), _flip(y, dy)) for dx, dy in CHIP_DELTAS]
        locals_ = [pltpu.make_async_copy(w_refs[k], o_refs[k].at[me], local_sems.at[k]) for k in range(n)]
        for cp in locals_:
            cp.start()

        def rows(k, which):
            half = shards[k].shape[0] // 2
            return pl.ds(pl.multiple_of(which * half, half), half)

        def over_chips(k, d, slot):
            tx, ty = peers[d]
            return pltpu.make_async_remote_copy(
                src_ref=w_refs[k].at[rows(k, c)], dst_ref=o_refs[k].at[slot, rows(k, c)], send_sem=ici_send.at[k, d],
                recv_sem=ici_recv.at[k, d], device_id=(tx, ty, c), device_id_type=MESH)

        def to_sibling(k, d, which):
            tx, ty = peers[d]
            at = o_refs[k].at[2 * tx + ty, rows(k, which)]
            return pltpu.make_async_remote_copy(src_ref=at, dst_ref=at, send_sem=d2d_send.at[k, d], recv_sem=d2d_recv.at[k, d],
                                                device_id=(x, y, 1 - c), device_id_type=MESH)

        sends = [over_chips(k, d, me) for k in range(n) for d in range(3)]
        for cp in sends:
            cp.start()
        passed = []
        for k in range(n):
            for d in range(3):
                over_chips(k, d, 2 * peers[d][0] + peers[d][1]).wait_recv()
                passed.append(to_sibling(k, d, c))
                passed[-1].start()
        for k in range(n):
            for d in range(3):
                to_sibling(k, d, 1 - c).wait_recv()
        for cp in sends + passed:
            cp.wait_send()
        for cp in locals_:
            cp.wait()

    return pl.pallas_call(
        body, name=name, out_shape=[jax.ShapeDtypeStruct((N_CHIPS,) + w.shape, w.dtype) for w in shards],
        in_specs=[HBM] * n, out_specs=[HBM] * n,
        scratch_shapes=[pltpu.SemaphoreType.DMA((n, 3))] * 4 + [pltpu.SemaphoreType.DMA((n,))],
    )(*shards)


def _add_into(dst_ref, src_ref):
    ns, r, _ = dst_ref.shape
    step = SUM_ROWS if r % SUM_ROWS == 0 else r
    for s in range(ns):
        def tile(t, carry):
            at = pl.ds(pl.multiple_of(t * step, step), step)
            dst_ref[s, at, :] = (dst_ref[s, at, :].astype(F32) + src_ref[s, at, :].astype(F32)).astype(dst_ref.dtype)
            return carry
        lax.fori_loop(0, r // step, tile, 0)


def _reduce_sibling(grads, *, name):
    n = len(grads)

    def body(*refs):
        g_refs, o_refs = refs[:n], refs[n:2 * n]
        mine, got = refs[2 * n:3 * n], refs[3 * n:4 * n]
        send_sems, recv_sems, load_sems, store_sems = refs[4 * n:]
        x, y, c = lax.axis_index("x"), lax.axis_index("y"), lax.axis_index("c")
        loads = [pltpu.make_async_copy(g_refs[k].at[:, c], mine[k], load_sems.at[k]) for k in range(n)]
        swaps = [pltpu.make_async_remote_copy(src_ref=g_refs[k].at[:, 1 - c], dst_ref=got[k], send_sem=send_sems.at[k],
                                              recv_sem=recv_sems.at[k], device_id=(x, y, 1 - c), device_id_type=MESH)
                 for k in range(n)]
        for cp in loads + swaps:
            cp.start()
        stores = []
        for k in range(n):
            loads[k].wait()
            swaps[k].wait_recv()
            _add_into(mine[k], got[k])
            stores.append(pltpu.make_async_copy(mine[k], o_refs[k], store_sems.at[k]))
            stores[-1].start()
        for k in range(n):
            swaps[k].wait_send()
            stores[k].wait()

    half = [jax.ShapeDtypeStruct((g.shape[0],) + g.shape[2:], g.dtype) for g in grads]
    return pl.pallas_call(
        body, name=name, out_shape=half, in_specs=[HBM] * n, out_specs=[HBM] * n,
        scratch_shapes=[pltpu.VMEM(h.shape, h.dtype) for h in half] * 2 + [pltpu.SemaphoreType.DMA((n,))] * 4,
        compiler_params=_params(),
    )(*grads)


def _reduce_chips(parts, landed, *, name):
    n = len(parts)

    def body(*refs):
        p_refs, l_refs, o_refs = refs[:n], refs[n:2 * n], refs[2 * n:3 * n]
        got, total = refs[3 * n:4 * n], refs[4 * n:5 * n]
        load_sems, share_send, share_recv, store_sems = refs[5 * n:]
        x, y, c = lax.axis_index("x"), lax.axis_index("y"), lax.axis_index("c")
        me = 2 * x + y
        slots = [me] + [2 * _flip(x, dx) + _flip(y, dy) for dx, dy in CHIP_DELTAS]
        loads = [[pltpu.make_async_copy((p_refs if j == 0 else l_refs)[k].at[slot], got[k].at[slot], load_sems.at[k, j])
                  for j, slot in enumerate(slots)] for k in range(n)]
        for per_array in loads:
            for cp in per_array:
                cp.start()
        shares, stores = [], []
        for k in range(n):
            for cp in loads[k]:
                cp.wait()
            r = total[k].shape[0]
            step = SUM_ROWS if r % SUM_ROWS == 0 else r

            def tile(t, carry, k=k, step=step):
                at = pl.ds(pl.multiple_of(t * step, step), step)
                acc = got[k][0, at, :].astype(F32)
                for s in range(1, N_CHIPS):
                    acc = acc + got[k][s, at, :].astype(F32)
                total[k][at, :] = acc
                return carry

            lax.fori_loop(0, r // step, tile, 0)
            stores.append(pltpu.make_async_copy(total[k], o_refs[k].at[c], store_sems.at[k]))
            shares.append(pltpu.make_async_remote_copy(
                src_ref=total[k], dst_ref=o_refs[k].at[c], send_sem=share_send.at[k], recv_sem=share_recv.at[k],
                device_id=(x, y, 1 - c), device_id_type=MESH))
            stores[-1].start()
            shares[-1].start()
        for k in range(n):
            pltpu.make_async_remote_copy(
                src_ref=total[k], dst_ref=o_refs[k].at[1 - c], send_sem=share_send.at[k], recv_sem=share_recv.at[k],
                device_id=(x, y, 1 - c), device_id_type=MESH).wait_recv()
        for cp in shares:
            cp.wait_send()
        for cp in stores:
            cp.wait()

    return pl.pallas_call(
        body, name=name, out_shape=[jax.ShapeDtypeStruct((2,) + p.shape[1:], F32) for p in parts],
        in_specs=[HBM] * (2 * n), out_specs=[HBM] * n,
        scratch_shapes=[pltpu.VMEM(p.shape, p.dtype) for p in parts] + [pltpu.VMEM(p.shape[1:], F32) for p in parts]
        + [pltpu.SemaphoreType.DMA((n, N_CHIPS))] + [pltpu.SemaphoreType.DMA((n,))] * 3,
        compiler_params=_params(),
    )(*parts, *landed)


SEM = pl.BlockSpec(memory_space=pltpu.SEMAPHORE)
IN_FLIGHT = pltpu.SideEffectType.DATAFLOW_SIDE_EFFECTING


def _chip_copies(s_refs, l_refs, sems, scatter, theirs):
    x, y, c = lax.axis_index("x"), lax.axis_index("y"), lax.axis_index("c")
    me = 2 * x + y
    copies = []
    for k in range(len(s_refs)):
        for d, (dx, dy) in enumerate(CHIP_DELTAS):
            tx, ty = _flip(x, dx), _flip(y, dy)
            peer = 2 * tx + ty
            send_sem, recv_sem = sems[2 * (3 * k + d)], sems[2 * (3 * k + d) + 1]
            copies.append(pltpu.make_async_remote_copy(
                src_ref=s_refs[k].at[peer] if scatter else s_refs[k], dst_ref=l_refs[k].at[peer if theirs else me],
                send_sem=send_sem, recv_sem=recv_sem, device_id=(tx, ty, c), device_id_type=MESH))
    return copies


def _chips_start(srcs, lands, after, *, scatter, name):
    n = len(srcs)
    n_sem = 2 * 3 * n

    def body(*refs):
        s_refs, l_refs = refs[:n], refs[n:2 * n]
        sems = refs[2 * n + 1:2 * n + 1 + n_sem]
        token = refs[-1]
        for cp in _chip_copies(s_refs, l_refs, sems, scatter, False):
            cp.start()
        token[...] = jnp.zeros_like(token)

    hbm = lambda a: pltpu.HBM(a.shape, a.dtype)
    res = pl.pallas_call(
        body, name=name,
        out_shape=(*[pltpu.SemaphoreType.DMA(())] * n_sem, *[hbm(a) for a in srcs], *[hbm(a) for a in lands],
                   jax.ShapeDtypeStruct((8, LANES), F32)),
        in_specs=[HBM] * (2 * n) + [pl.BlockSpec(memory_space=pl.ANY)],
        out_specs=(*[SEM] * n_sem, *[HBM] * (2 * n), VMEM),
        input_output_aliases={k: n_sem + k for k in range(2 * n)},
        compiler_params=pltpu.CompilerParams(has_side_effects=IN_FLIGHT),
    )(*[pltpu.with_memory_space_constraint(a, pltpu.HBM) for a in list(srcs) + list(lands)], after)
    return res[:n_sem], res[n_sem:n_sem + n], res[n_sem + n:n_sem + 2 * n], res[-1]


def _chips_wait(sems, srcs, lands, after, *, scatter, name):
    n = len(srcs)
    n_sem = len(sems)

    def body(*refs):
        s_refs, l_refs = refs[:n], refs[n:2 * n]
        sem_refs = refs[2 * n:2 * n + n_sem]
        for cp in _chip_copies(s_refs, l_refs, sem_refs, scatter, False):
            cp.wait_send()
        for cp in _chip_copies(s_refs, l_refs, sem_refs, scatter, True):
            cp.wait_recv()

    hbm = lambda a: pltpu.HBM(a.shape, a.dtype)
    res = pl.pallas_call(
        body, name=name, out_shape=tuple(hbm(a) for a in list(srcs) + list(lands)),
        in_specs=[HBM] * (2 * n) + [SEM] * n_sem + [pl.BlockSpec(memory_space=pl.ANY)], out_specs=tuple([HBM] * (2 * n)),
        input_output_aliases={k: k for k in range(2 * n)},
        compiler_params=pltpu.CompilerParams(has_side_effects=IN_FLIGHT),
    )(*srcs, *lands, *sems, after)
    return res[:n], res[n:]


def _all_gather_devices(rows, *, name, after=None):
    deltas = [(dx, dy, dc) for dx in (0, 1) for dy in (0, 1) for dc in (0, 1)][1:]
    order = [] if after is None else [after]

    def body(x_ref, *rest):
        o_ref, send_sems, recv_sems = rest[-3:]
        x, y, c = lax.axis_index("x"), lax.axis_index("y"), lax.axis_index("c")
        me = 4 * x + 2 * y + c
        o_ref[me] = x_ref[...]
        sends, recvs = [], []
        for k, (dx, dy, dc) in enumerate(deltas):
            tx, ty, tc = _flip(x, dx), _flip(y, dy), _flip(c, dc)
            sends.append(pltpu.make_async_remote_copy(src_ref=x_ref, dst_ref=o_ref.at[me], send_sem=send_sems.at[k],
                                                      recv_sem=recv_sems.at[k], device_id=(tx, ty, tc), device_id_type=MESH))
            recvs.append(pltpu.make_async_remote_copy(src_ref=x_ref, dst_ref=o_ref.at[4 * tx + 2 * ty + tc],
                                                      send_sem=send_sems.at[k], recv_sem=recv_sems.at[k],
                                                      device_id=(tx, ty, tc), device_id_type=MESH))
        for cp in sends:
            cp.start()
        for cp in recvs:
            cp.wait_recv()
        for cp in sends:
            cp.wait_send()

    return pl.pallas_call(
        body, name=name, out_shape=jax.ShapeDtypeStruct((N_DEV,) + rows.shape, rows.dtype),
        in_specs=[VMEM] + [pl.BlockSpec(memory_space=pl.ANY)] * len(order), out_specs=VMEM,
        scratch_shapes=[pltpu.SemaphoreType.DMA((N_DEV - 1,)), pltpu.SemaphoreType.DMA((N_DEV - 1,))],
    )(rows, *order)


WEIGHTS = ("ada_w", "ada_b", "ln_g", "ln_b", "e_w_in", "gmlp_norm_g", "gmlp_norm_b", "gmlp_ws", "gmlp_bs", "pool_w",
           "pool_b", "pool_scale", "e_w_out", "o_w_in", "mla_q_norm_g", "mla_kv_norm_g", "mla_w_uq", "mla_w_uk",
           "mla_w_uv", "o_w_out")
SMALL = ("ln_g", "ln_b", "gmlp_norm_g", "gmlp_norm_b", "gmlp_bs", "pool_b", "pool_scale", "mla_kv_norm_g", "mla_q_norm_g")


def _pad_cols(v, n):
    return jnp.concatenate([v, jnp.zeros((v.shape[0], n - v.shape[1]), v.dtype)], axis=1) if n > v.shape[1] else v


def _halves(g):
    return g.reshape(g.shape[0], 2, g.shape[1] // 2, g.shape[2])


def kernel(x, c, positions, ada_w, ada_b, ln_g, ln_b, e_w_in, gmlp_norm_g, gmlp_norm_b, gmlp_ws, gmlp_bs, pool_w, pool_b, pool_scale, e_w_out, o_w_in, mla_q_norm_g, mla_kv_norm_g, mla_w_uq, mla_w_uk, mla_w_uv, o_w_out, loss_target, m_ada_w, m_ada_b, m_ln_g, m_ln_b, m_e_w_in, m_gmlp_norm_g, m_gmlp_norm_b, m_gmlp_ws, m_gmlp_bs, m_pool_w, m_pool_b, m_pool_scale, m_e_w_out, m_o_w_in, m_mla_q_norm_g, m_mla_kv_norm_g, m_mla_w_uq, m_mla_w_uk, m_mla_w_uv, m_o_w_out, v_ada_w, v_ada_b, v_ln_g, v_ln_b, v_e_w_in, v_gmlp_norm_g, v_gmlp_norm_b, v_gmlp_ws, v_gmlp_bs, v_pool_w, v_pool_b, v_pool_scale, v_e_w_out, v_o_w_in, v_mla_q_norm_g, v_mla_kv_norm_g, v_mla_w_uq, v_mla_w_uk, v_mla_w_uv, v_o_w_out):
    args = dict(locals())
    weights = {n: args[n] for n in WEIGHTS}
    mom = {n: args["m_" + n] for n in WEIGHTS}
    var = {n: args["v_" + n] for n in WEIGHTS}
    ax, ay, ac = lax.axis_index("x"), lax.axis_index("y"), lax.axis_index("c")
    chip = 2 * ax + ay
    dev = 2 * chip + ac
    d = D_MODEL
    x2 = x[0]
    target = loss_target[0]
    q_rank_sh = mla_q_norm_g.shape[1]

    empty_zone = lambda w: lax.dynamic_update_slice(lax.empty((N_CHIPS,) + w.shape, w.dtype), w[None], (chip, 0, 0))
    shards0 = [w.astype(BF16) for w in (pool_w[0].reshape(-1, POOL_GROUP_DIM), e_w_out[0])]
    shards1 = [w.astype(BF16) for w in (o_w_in[0], mla_w_uq[0].reshape(q_rank_sh, -1), o_w_out[0])]
    w_in0, = _gather_weights([e_w_in[0].astype(BF16)], name="gather_weights")
    wuk_hrd = jnp.transpose(mla_w_uk[0], (1, 0, 2)).astype(BF16)
    wuk_hdr = jnp.transpose(mla_w_uk[0], (1, 2, 0)).astype(BF16)
    wuv_hrv = jnp.transpose(mla_w_uv[0], (1, 0, 2)).astype(BF16)
    wuv_hvr = jnp.transpose(mla_w_uv[0], (1, 2, 0)).astype(BF16)
    ws = gmlp_ws[0]
    ws_t = jnp.transpose(ws, (0, 2, 1))
    bs_t = _pad_cols(gmlp_bs[0].T, LANES)

    inv = 1.0 / (ROPE_THETA ** (jnp.arange(0, MLA_ROPE, 2, dtype=F32) / MLA_ROPE))
    ang = positions[0].astype(F32)[:, None] * inv
    cos_t = jnp.tile(jnp.cos(ang), (1, 4))
    sin_t = jnp.concatenate([-jnp.sin(ang), -jnp.sin(ang), jnp.sin(ang), jnp.sin(ang)], axis=1)

    c_all = _all_gather_devices(c.reshape(8, LANES), after=w_in0, name="gather_c").reshape(N_DEV, d)
    cols = ada_w.shape[2]
    ada_b_mine = lax.dynamic_slice_in_dim(ada_b, chip * cols, cols, axis=1)[:, None, :]
    mod_sh = _ada_mod(c_all, ada_w, ada_b_mine, name="ada_mod")
    q_norm_rows = jnp.zeros((8, cols), F32).at[0, :q_rank_sh].set(mla_q_norm_g[0])
    mod_all = _all_gather_chips(jnp.concatenate([mod_sh.reshape(2 * N_DEV, cols), q_norm_rows]), name="gather_mod")
    q_norm_g = mod_all[:, 2 * N_DEV, :q_rank_sh].reshape(1, -1)
    mod_all = jnp.transpose(mod_all[:, :2 * N_DEV].reshape(N_CHIPS, 2, N_DEV, cols), (1, 2, 0, 3)).reshape(2, N_DEV, 3 * d)
    mod = lax.dynamic_index_in_dim(mod_all, dev, axis=1, keepdims=False)
    shift = [mod[l:l + 1, :d] for l in range(2)]
    scale = [mod[l:l + 1, d:2 * d] for l in range(2)]
    gate = [mod[l:l + 1, 2 * d:] for l in range(2)]
    flight0 = _chips_start(shards0, [empty_zone(w) for w in shards0], mod, scatter=False, name="gather0_start")
    flight1 = _chips_start(shards1, [empty_zone(w) for w in shards1], flight0[3], scatter=False, name="gather1_start")

    scale[0] = scale[0] + flight1[3][:1, :1]
    h0 = _modulate(x2, scale[0], shift[0], name="modulate0")
    proj0 = _matmul(h0, w_in0, b_stacked=True, tm=1024, tn=1280, out_dtype=BF16, name="proj0")
    pool_w_g, w_out0 = _chips_wait(*flight0[:3], proj0, scatter=False, name="gather0_wait")[1]
    pool_w_bf = jnp.transpose(pool_w_g.reshape(N_CHIPS, POOL_GROUPS, -1, POOL_GROUP_DIM), (1, 0, 2, 3)).reshape(
        POOL_GROUPS, POOL_GROUP_DIM, POOL_GROUP_DIM)
    w_out0 = w_out0.reshape(-1, d)
    mix0 = _even_fwd(proj0, ws, bs_t, gmlp_norm_g, gmlp_norm_b, pool_w_bf, pool_b, pool_scale, name="even_fwd")
    y0, x1, h1 = _out_resid_ln(mix0, w_out0, x2, gate[0], ln_g[0:1], ln_b[0:1], scale[1], shift[1], name="out0_ln")

    w_in1_g, w_uq_g, w_out1 = _chips_wait(*flight1[:3], h1, scatter=False, name="gather1_wait")[1]
    w_out1 = w_out1.reshape(-1, d)
    w_in1 = jnp.transpose(w_in1_g, (1, 0, 2)).reshape(d, ODD_IN)
    w_in1 = jnp.concatenate([w_in1[:, ODD_SMALL:], _pad_cols(w_in1[:, :ODD_SMALL], ODD_SMALL_PAD)], axis=1)
    w_uq = w_uq_g.reshape(MLA_Q_RANK, MLA_HEADS, MLA_NOPE + MLA_ROPE)
    w_uq_nope = w_uq[:, :, :MLA_NOPE].reshape(MLA_Q_RANK, -1)
    w_uq_rope = jnp.transpose(w_uq[:, :, MLA_NOPE:].reshape(MLA_Q_RANK, MLA_HEADS // 2, 2, 2, ROPE_HALF),
                              (0, 1, 3, 2, 4)).reshape(MLA_Q_RANK, -1)
    proj1 = _matmul(h1, w_in1, tm=1024, tn=1280, out_dtype=BF16, name="proj1")
    q_cn, keys = _mla_prep(proj1, q_norm_g, mla_kv_norm_g, cos_t, sin_t, name="mla_prep")
    q_nope = _matmul(q_cn, w_uq_nope, tm=1024, tn=2048, name="q_nope", out_dtype=BF16)
    q_rope_pre = _matmul(q_cn, w_uq_rope, tm=1024, name="q_rope")
    q = _q_build(q_nope, q_rope_pre, wuk_hdr, cos_t, sin_t, name="q_build")
    o_lat, lse = _attn_fwd(q, keys, name="attn_fwd")
    og = _o_build(o_lat, wuv_hrv, proj1, name="o_build")

    dy1, dres1, g_ln_g1, g_ln_b1, dgate1, loss = _out_loss_ln_bwd(
        og, w_out1, x1, gate[1], ln_g[1:2], ln_b[1:2], target, name="out1_loss_ln")
    dg1 = _matmul(dy1, w_out1, trans_b=True, tn=2048, out_dtype=BF16, name="d_og")
    g_w_out1 = _matmul(og, dy1, trans_a=True, out_dtype=BF16, tm=1024, tk=4096, name="g_out1")
    do_lat, dz, g_uv = _o_bwd(dg1, proj1, o_lat, wuv_hrv, wuv_hvr, name="o_bwd")
    dq, dkeys = _attn_bwd(q, keys, do_lat, o_lat, lse, name="attn_bwd")
    dq_all, g_uk = _q_bwd(dq, q_nope, wuk_hrd, cos_t, sin_t, name="q_bwd")
    n_grp = MLA_HEADS // Q_HEAD_GROUP
    w_uq_all = jnp.concatenate([w_uq_nope.reshape(MLA_Q_RANK, n_grp, -1), w_uq_rope.reshape(MLA_Q_RANK, n_grp, -1)],
                               axis=2).reshape(MLA_Q_RANK, -1)
    dq_cn = _matmul(dq_all, w_uq_all, trans_b=True, tm=1024, tk=3072, name="d_qcn")
    g_uq_all = _matmul(q_cn, dq_all, trans_a=True, out_dtype=BF16, name="g_uq").reshape(MLA_Q_RANK, n_grp, -1)
    g_uq_nope = g_uq_all[:, :, :Q_HEAD_GROUP * MLA_NOPE].reshape(MLA_Q_RANK, -1)
    g_uq_rope = g_uq_all[:, :, Q_HEAD_GROUP * MLA_NOPE:].reshape(MLA_Q_RANK, -1)
    dsmall, g_qg, g_kvg = _mla_prep_bwd(proj1, dq_cn, dkeys, q_norm_g, mla_kv_norm_g, cos_t, sin_t, name="mla_prep_bwd")
    dproj1 = jnp.concatenate([dz, dsmall], axis=1)
    g_w_in1 =_matmul(h1, dproj1, trans_a=True, out_dtype=BF16, tm=1024, tn=1280, name="g_in1")

    g_uq_rope = jnp.transpose(g_uq_rope.reshape(MLA_Q_RANK, MLA_HEADS // 2, 2, 2, ROPE_HALF), (0, 1, 3, 2, 4))
    g_uq = jnp.concatenate([g_uq_nope.reshape(MLA_Q_RANK, MLA_HEADS, MLA_NOPE), g_uq_rope.reshape(MLA_Q_RANK, MLA_HEADS, MLA_ROPE)], axis=2)
    g_w_in1 = jnp.concatenate([g_w_in1[:, MLA_WIDTH:MLA_WIDTH + ODD_SMALL], g_w_in1[:, :MLA_WIDTH]], axis=1)
    g_w_in1 = jnp.transpose(g_w_in1.reshape(d, N_CHIPS, -1), (1, 0, 2))
    big1 = [
        _halves(g_w_in1),
        _halves(g_uq.reshape(N_CHIPS, q_rank_sh, -1)),
        _halves(g_w_out1.reshape(N_CHIPS, -1, d)),
        _halves(g_uk.astype(BF16).reshape(N_CHIPS, -1, MLA_NOPE)),
        _halves(g_uv.astype(BF16).reshape(N_CHIPS, -1, MLA_V)),
    ]
    parts1 = _reduce_sibling(big1, name="reduce_sibling1")
    flight2 = _chips_start(parts1, [lax.empty(p.shape, BF16) for p in parts1], loss, scatter=True, name="reduce1_start")

    gate[0] = gate[0] + flight2[3][:1, :1]
    dy0, dres0, g_ln_g0, g_ln_b0, dgate0, dscale1, dshift1 = _dh_mid_ln_bwd(
        dproj1, w_in1, x2, y0, gate[0], ln_g[0:1], ln_b[0:1], dres1, scale[1], x1, name="d_h1_mid_ln")
    dmix0 = _matmul(dy0, w_out0, trans_b=True, tn=2048, out_dtype=BF16, name="d_mix0")
    g_w_out0 = _matmul(mix0, dy0, trans_a=True, out_dtype=BF16, tm=1024, tk=4096, name="g_out0")
    dproj0, g_ws, g_bs_t, g_ng, g_nb, g_pw, g_pb, g_ps = _even_bwd(
        proj0, dmix0, ws, ws_t, bs_t, gmlp_norm_g, gmlp_norm_b, pool_w_bf, pool_b, pool_scale, name="even_bwd")
    g_w_in0 = _matmul(h0, dproj0, trans_a=True, out_dtype=BF16, out_stacked=True, tm=1024, tn=1280, name="g_in0")

    g_pw = jnp.transpose(g_pw.astype(BF16).reshape(POOL_GROUPS, N_CHIPS, -1, POOL_GROUP_DIM), (1, 0, 2, 3))
    big0 = [
        _halves(g_w_in0),
        _halves(g_pw.reshape(N_CHIPS, -1, POOL_GROUP_DIM)),
        _halves(g_w_out0.reshape(N_CHIPS, -1, d)),
        _halves(g_ws.astype(BF16)),
    ]
    parts0 = _reduce_sibling(big0, name="reduce_sibling0")
    parts1, landed1 = _chips_wait(*flight2[:3], parts0[0], scatter=True, name="reduce1_wait")
    flight3 = _chips_start(parts0, [lax.empty(p.shape, BF16) for p in parts0], landed1[0], scatter=True, name="reduce0_start")
    grad_x, dscale0, dshift0 = _dh_input_bwd(dproj0, w_in0, x2, dres0, scale[0], after=flight3[3], name="d_h0_input")

    small_local = {
        "ln_g": jnp.concatenate([g_ln_g0, g_ln_g1]), "ln_b": jnp.concatenate([g_ln_b0, g_ln_b1]),
        "gmlp_norm_g": g_ng, "gmlp_norm_b": g_nb, "gmlp_bs": g_bs_t[:, :GMLP_HEADS].T, "pool_b": g_pb, "pool_scale": g_ps,
        "mla_kv_norm_g": g_kvg, "mla_q_norm_g": g_qg,
    }
    n_mod = 2 * 3 * d
    vec = jnp.concatenate([dshift0, dscale0, dgate0, dshift1, dscale1, dgate1]
                          + [small_local[n].reshape(1, -1) for n in SMALL] + [loss], axis=1)
    n_vec = vec.shape[1]
    vec = _pad_cols(vec, -(-n_vec // (8 * LANES)) * 8 * LANES).reshape(-1, LANES)
    vec_all = _all_gather_devices(vec, name="gather_small")
    vec_sum = _sum_devices(vec_all, name="sum_small").reshape(-1)
    dmod_all = vec_all.reshape(N_DEV, -1)[:, :n_mod].reshape(N_DEV, 2, 3 * d)
    dmod_sh = jnp.transpose(lax.dynamic_slice_in_dim(dmod_all, chip * cols, cols, axis=2), (1, 0, 2))
    dmod_sh = jnp.concatenate([dmod_sh, jnp.zeros((2, LANES - N_DEV, cols), F32)], axis=1)
    grads = {"ada_w": _ada_grad(_pad_cols(c_all.T, LANES), dmod_sh, name="ada_grad"), "ada_b": vec_sum[:n_mod].reshape(2, 3 * d)}
    off = n_mod
    for n in SMALL:
        sz = small_local[n].size
        grads[n] = vec_sum[off:off + sz]
        off += sz
    grads["mla_q_norm_g"] = lax.dynamic_slice_in_dim(grads["mla_q_norm_g"], chip * q_rank_sh, q_rank_sh)
    for n in SMALL:
        grads[n] = grads[n].reshape(weights[n].shape)

    parts0, landed0 = _chips_wait(*flight3[:3], grads["ada_w"], scatter=True, name="reduce0_wait")
    totals = _reduce_chips(list(parts0) + list(parts1), list(landed0) + list(landed1), name="reduce_chips")
    for n, t in zip(("e_w_in", "pool_w", "e_w_out", "gmlp_ws", "o_w_in", "mla_w_uq", "o_w_out"), totals):
        if n != "gmlp_ws":
            grads[n] = t.reshape(weights[n].shape)
    rep = jnp.concatenate([t.reshape(-1, LANES) for t in (totals[3], totals[7], totals[8])])
    rep_land = lax.dynamic_update_slice(lax.empty((N_CHIPS,) + rep.shape, F32), rep[None], (chip, 0, 0))
    flight4 = _chips_start([rep], [rep_land], totals[0], scatter=False, name="gather_rep_start")

    delta, new_m, new_v = {}, {}, {}
    replicated = ("gmlp_ws", "mla_w_uk", "mla_w_uv")
    large = [n for n in WEIGHTS if n not in SMALL and n != "ada_b"]
    for n in large:
        if n not in replicated:
            delta[n], new_m[n], new_v[n] = _adamw(weights[n], grads[n], mom[n], var[n], after=flight4[3], name="adamw_" + n)
    rep = _chips_wait(*flight4[:3], delta["e_w_in"], scatter=False, name="gather_rep_wait")[1][0]
    r_ws, r_uk = GMLP_BLOCK, 4 * MLA_KV_RANK
    grads["gmlp_ws"] = rep[:, :r_ws].reshape(weights["gmlp_ws"].shape)
    grads["mla_w_uk"] = jnp.transpose(rep[:, r_ws:r_ws + r_uk].reshape(MLA_HEADS, MLA_KV_RANK, MLA_NOPE), (1, 0, 2))[None]
    grads["mla_w_uv"] = jnp.transpose(rep[:, r_ws + r_uk:].reshape(MLA_HEADS, MLA_KV_RANK, MLA_V), (1, 0, 2))[None]
    for n in replicated:
        delta[n], new_m[n], new_v[n] = _adamw(weights[n], grads[n], mom[n], var[n], name="adamw_" + n)
    small = [n for n in WEIGHTS if n not in large]
    ds, ms, vs = _adamw_small([weights[n] for n in small], [grads[n] for n in small], [mom[n] for n in small],
                              [var[n] for n in small], name="adamw_small")
    for n, dn, mn, vn in zip(small, ds, ms, vs):
        delta[n], new_m[n], new_v[n] = dn, mn, vn

    return (vec_sum[n_vec - 1], grad_x[None], *[grads[n] for n in WEIGHTS], *[delta[n] for n in WEIGHTS],
            *[new_m[n] for n in WEIGHTS], *[new_v[n] for n in WEIGHTS])
```

```python
import jax
import jax.numpy as jnp
from jax import lax
from jax.experimental import pallas as pl
from jax.experimental.pallas import tpu as pltpu

F32 = jnp.float32
BF16 = jnp.bfloat16
MESH = pl.DeviceIdType.MESH

D_MODEL = 1024
CHUNK = 64
LN_EPS = 1e-5
GMLP_HEADS = 4
GMLP_HEAD_DIM = 256
GMLP_BLOCK = 128
POOL_WINDOWS = (2, 4, 8, 16)
POOL_GROUPS = 4
POOL_GROUP_DIM = 256
POOL_HALO = 16
EVEN_IN = 5120
MLA_HEADS = 16
MLA_NOPE = 128
MLA_ROPE = 64
MLA_V = 128
MLA_Q_RANK = 256
MLA_KV_RANK = 128
MLA_WIDTH = MLA_HEADS * MLA_V
ODD_IN = 2496
ODD_SMALL = MLA_Q_RANK + MLA_KV_RANK + MLA_ROPE
ODD_SMALL_PAD = 512
QK_PAD = 256
ROPE_THETA = 10000.0
ATTN_SCALE = (MLA_NOPE + MLA_ROPE) ** -0.5
DEEPNORM_ALPHA = (2.0 * 2) ** 0.25
ADAM_LR = 0.001
ADAM_B1 = 0.9
ADAM_B2 = 0.999
ADAM_EPS = 1e-08
ADAM_WD = 0.01
ADAM_STEP = 10
NEG = -1e30
LANES = 128
N_DEV = 8
N_CHIPS = 4
VMEM_LIMIT_BYTES = 56 * 1024 * 1024
HBM = pl.BlockSpec(memory_space=pltpu.HBM)
VMEM = pl.BlockSpec(memory_space=pltpu.VMEM)


def _params(*sem):
    return pltpu.CompilerParams(dimension_semantics=sem if sem else None, vmem_limit_bytes=VMEM_LIMIT_BYTES)


def _tile(dim, pref):
    for t in (pref, 2048, 1280, 1024, 512, 256, 128):
        if t <= min(pref, dim) and dim % t == 0:
            return t
    return dim


def _sigmoid(z):
    return 1.0 / (1.0 + jnp.exp(-z))


def _dot(a, b, dims):
    return lax.dot_general(a, b, (dims, ((), ())), preferred_element_type=F32)


NN = ((1,), (0,))
NT = ((1,), (1,))
TN = ((0,), (0,))


def _matmul(a, b, *, name, trans_a=False, trans_b=False, out_dtype=F32, b_stacked=False, out_stacked=False,
            tm=512, tn=1024, tk=2048, after=None):
    k, m = a.shape if trans_a else a.shape[::-1]
    if b_stacked:
        assert not trans_b
        ns, kb, n_sh = b.shape
        n = ns * n_sh
    else:
        n, kb = b.shape if trans_b else b.shape[::-1]
    assert k == kb, (a.shape, b.shape)
    tm = _tile(m, tm)
    tn, tk = _tile(n // N_CHIPS if b_stacked or out_stacked else n, tn), _tile(k, tk)
    nk = k // tk
    per = max((n // N_CHIPS) // tn, 1)
    dims = ((0 if trans_a else 1,), (1 if trans_b else 0,))

    def body_one(a_ref, b_ref, *rest):
        o_ref = rest[-1]
        o_ref[...] = _dot(a_ref[...].astype(BF16), b_ref[...].astype(BF16), dims).astype(out_dtype)

    def body_acc(a_ref, b_ref, *rest):
        o_ref, acc_ref = rest[-2:]
        kk = pl.program_id(2)

        @pl.when(kk == 0)
        def _():
            acc_ref[...] = jnp.zeros_like(acc_ref)

        acc_ref[...] += _dot(a_ref[...].astype(BF16), b_ref[...].astype(BF16), dims)

        @pl.when(kk == nk - 1)
        def _():
            o_ref[...] = acc_ref[...].astype(out_dtype)

    a_spec = pl.BlockSpec((tk, tm), lambda i, j, kk: (kk, i)) if trans_a else pl.BlockSpec((tm, tk), lambda i, j, kk: (i, kk))
    if b_stacked:
        b_spec = pl.BlockSpec((None, tk, tn), lambda i, j, kk: (j // per, kk, j % per))
    elif trans_b:
        b_spec = pl.BlockSpec((tn, tk), lambda i, j, kk: (j, kk))
    else:
        b_spec = pl.BlockSpec((tk, tn), lambda i, j, kk: (kk, j))
    if out_stacked:
        o_spec = pl.BlockSpec((None, tm, tn), lambda i, j, kk: (j // per, i, j % per))
        o_shape = jax.ShapeDtypeStruct((N_CHIPS, m, n // N_CHIPS), out_dtype)
    else:
        o_spec = pl.BlockSpec((tm, tn), lambda i, j, kk: (i, j))
        o_shape = jax.ShapeDtypeStruct((m, n), out_dtype)
    order = [] if after is None else [after]
    return pl.pallas_call(
        body_one if nk == 1 else body_acc, name=name, grid=(m // tm, n // tn, nk),
        in_specs=[a_spec, b_spec] + [pl.BlockSpec(memory_space=pl.ANY)] * len(order),
        out_specs=o_spec, out_shape=o_shape, scratch_shapes=[] if nk == 1 else [pltpu.VMEM((tm, tn), F32)],
        compiler_params=_params("parallel", "parallel", "arbitrary"),
    )(a, b, *order)


def _matmul_rows(a, b, epilogue, row_ins, vec_ins, row_outs, vec_outs, *, name, trans_b=False, b_stacked=False,
                 tm=512, tk=2048, after=None):
    m, k = a.shape
    if b_stacked:
        ns, n, n_sh = b.shape
        assert trans_b and ns * n_sh == k
        tk = k
    else:
        n = b.shape[0] if trans_b else b.shape[1]
        tk = _tile(k, tk)
    tm = _tile(m, tm)
    nk = k // tk
    dims = ((1,), (1 if trans_b else 0,))
    n_ri, n_vi, n_ro, n_vo = len(row_ins), len(vec_ins), len(row_outs), len(vec_outs)
    order = [] if after is None else [after]

    def body(*refs):
        a_ref, b_ref = refs[:2]
        pos = 2
        rin = refs[pos:pos + n_ri]
        pos += n_ri
        vin = refs[pos:pos + n_vi]
        pos += n_vi + len(order)
        rout = refs[pos:pos + n_ro]
        pos += n_ro
        vout = refs[pos:pos + n_vo]
        first = pl.program_id(0) == 0
        if b_stacked:
            part = _dot(a_ref[:, :n_sh].astype(BF16), b_ref[0].astype(BF16), dims)
            for sh in range(1, ns):
                part = part + _dot(a_ref[:, sh * n_sh:(sh + 1) * n_sh].astype(BF16), b_ref[sh].astype(BF16), dims)
        else:
            part = _dot(a_ref[...].astype(BF16), b_ref[...].astype(BF16), dims)
        if nk == 1:
            epilogue(part, first, rin, vin, rout, vout)
        else:
            acc_ref = refs[-1]
            kk = pl.program_id(1)

            @pl.when(kk == 0)
            def _():
                acc_ref[...] = part

            @pl.when(kk > 0)
            def _():
                acc_ref[...] += part

            @pl.when(kk == nk - 1)
            def _():
                epilogue(acc_ref[...], first, rin, vin, rout, vout)

    a_spec = pl.BlockSpec((tm, tk), lambda i, kk: (i, kk))
    if b_stacked:
        b_spec = pl.BlockSpec((ns, n, n_sh), lambda i, kk: (0, 0, 0))
    elif trans_b:
        b_spec = pl.BlockSpec((n, tk), lambda i, kk: (0, kk))
    else:
        b_spec = pl.BlockSpec((tk, n), lambda i, kk: (kk, 0))
    row = pl.BlockSpec((tm, n), lambda i, kk: (i, 0))
    vec = lambda w: pl.BlockSpec((1, w), lambda i, kk: (0, 0))
    return pl.pallas_call(
        body, name=name, grid=(m // tm, nk),
        in_specs=[a_spec, b_spec] + [row] * n_ri + [vec(v.shape[1]) for v in vec_ins] + [pl.BlockSpec(memory_space=pl.ANY)] * len(order),
        out_specs=[row] * n_ro + [vec(w) for w in vec_outs],
        out_shape=[jax.ShapeDtypeStruct((m, n), dt) for dt in row_outs] + [jax.ShapeDtypeStruct((1, w), F32) for w in vec_outs],
        scratch_shapes=[] if nk == 1 else [pltpu.VMEM((tm, n), F32)],
        compiler_params=_params("arbitrary", "arbitrary"),
    )(a, b, *row_ins, *vec_ins, *order)


def _row_spec(ts, d):
    return pl.BlockSpec((ts, d), lambda i: (i, 0))


def _vec_spec(d):
    return pl.BlockSpec((1, d), lambda i: (0, 0))


def _modulate(x, scale, shift, *, name):
    s, d = x.shape
    ts = _tile(s, 512)

    def body(x_ref, sc_ref, sh_ref, h_ref):
        h_ref[...] = (x_ref[...] * (1.0 + sc_ref[...]) + sh_ref[...]).astype(BF16)

    return pl.pallas_call(
        body, name=name, grid=(s // ts,), in_specs=[_row_spec(ts, d), _vec_spec(d), _vec_spec(d)],
        out_specs=_row_spec(ts, d), out_shape=jax.ShapeDtypeStruct((s, d), BF16), compiler_params=_params("parallel"),
    )(x, scale, shift)


def _ln_stats(pre):
    mu = jnp.mean(pre, axis=-1, keepdims=True)
    xc = pre - mu
    var = jnp.mean(xc * xc, axis=-1, keepdims=True)
    rstd = lax.rsqrt(var + LN_EPS)
    return xc * rstd, rstd


def _ln_bwd_rows(dout, xhat, rstd, g):
    dxh = dout * g
    m1 = jnp.mean(dxh, axis=-1, keepdims=True)
    m2 = jnp.mean(dxh * xhat, axis=-1, keepdims=True)
    return rstd * (dxh - m1 - xhat * m2)


def _colsum(v):
    return jnp.sum(v, axis=0, keepdims=True)


def _out_resid_ln(mix, w_out, x, gate, g, b, scale_next, shift_next, *, name):
    def epilogue(y, first, rin, vin, rout, vout):
        (x_ref,), (gate_ref, g_ref, b_ref, sc_ref, sh_ref), (y_ref, xn_ref, h_ref) = rin, vin, rout
        y_ref[...] = y
        pre = DEEPNORM_ALPHA * x_ref[...] + (1.0 + gate_ref[...]) * y
        xhat, _ = _ln_stats(pre)
        xn = xhat * g_ref[...] + b_ref[...]
        xn_ref[...] = xn
        h_ref[...] = (xn * (1.0 + sc_ref[...]) + sh_ref[...]).astype(BF16)

    return _matmul_rows(mix, w_out, epilogue, [x], [gate, g, b, scale_next, shift_next], [F32, F32, BF16], [], name=name)


def _out_loss_ln_bwd(og, w_out, x, gate, g, b, target, *, name):
    d = x.shape[1]

    def epilogue(yv, first, rin, vin, rout, vout):
        (x_ref, t_ref), (gate_ref, g_ref, b_ref), (dy_ref, dres_ref), (dg_ref, db_ref, dgate_ref, loss_ref) = rin, vin, rout, vout

        @pl.when(first)
        def _():
            for r in vout:
                r[...] = jnp.zeros_like(r)

        pre = DEEPNORM_ALPHA * x_ref[...] + (1.0 + gate_ref[...]) * yv
        xhat, rstd = _ln_stats(pre)
        diff = xhat * g_ref[...] + b_ref[...] - t_ref[...]
        loss_ref[...] += (0.5 / d) * jnp.sum(jnp.sum(diff * diff, axis=1, keepdims=True), axis=0, keepdims=True)
        dout = diff * (1.0 / d)
        dpre = _ln_bwd_rows(dout, xhat, rstd, g_ref[...])
        dy_ref[...] = (dpre * (1.0 + gate_ref[...])).astype(BF16)
        dres_ref[...] = DEEPNORM_ALPHA * dpre
        dg_ref[...] += _colsum(dout * xhat)
        db_ref[...] += _colsum(dout)
        dgate_ref[...] += _colsum(dpre * yv)

    return _matmul_rows(og, w_out, epilogue, [x, target], [gate, g, b], [BF16, F32], [d, d, d, 1], name=name)


def _dh_mid_ln_bwd(dproj, w_in, x, y, gate, g, b, dres_next, scale_next, x_next, *, name):
    d = x.shape[1]

    def epilogue(dh, first, rin, vin, rout, vout):
        (x_ref, y_ref, dr_ref, xn_ref), (gate_ref, g_ref, b_ref, sc_ref), (dy_ref, dres_ref) = rin, vin, rout
        dg_ref, db_ref, dgate_ref, dscale_ref, dshift_ref = vout

        @pl.when(first)
        def _():
            for r in vout:
                r[...] = jnp.zeros_like(r)

        dout = dr_ref[...] + dh * (1.0 + sc_ref[...])
        dscale_ref[...] += _colsum(dh * xn_ref[...])
        dshift_ref[...] += _colsum(dh)
        yv = y_ref[...]
        pre = DEEPNORM_ALPHA * x_ref[...] + (1.0 + gate_ref[...]) * yv
        xhat, rstd = _ln_stats(pre)
        dpre = _ln_bwd_rows(dout, xhat, rstd, g_ref[...])
        dy_ref[...] = (dpre * (1.0 + gate_ref[...])).astype(BF16)
        dres_ref[...] = DEEPNORM_ALPHA * dpre
        dg_ref[...] += _colsum(dout * xhat)
        db_ref[...] += _colsum(dout)
        dgate_ref[...] += _colsum(dpre * yv)

    return _matmul_rows(dproj, w_in, epilogue, [x, y, dres_next, x_next], [gate, g, b, scale_next], [BF16, F32], [d] * 5,
                        trans_b=True, tk=2560, name=name)


def _dh_input_bwd(dproj, w_in_stacked, x, dres, scale, *, name, after):
    d = x.shape[1]

    def epilogue(dh, first, rin, vin, rout, vout):
        (x_ref, dr_ref), (sc_ref,), (dx_ref,), (dscale_ref, dshift_ref) = rin, vin, rout, vout

        @pl.when(first)
        def _():
            for r in vout:
                r[...] = jnp.zeros_like(r)

        dx_ref[...] = dr_ref[...] + dh * (1.0 + sc_ref[...])
        dscale_ref[...] += _colsum(dh * x_ref[...])
        dshift_ref[...] += _colsum(dh)

    return _matmul_rows(dproj, w_in_stacked, epilogue, [x, dres], [scale], [F32], [d, d], trans_b=True, b_stacked=True,
                        tm=512, after=after, name=name)


def _chunk_mask(transposed=False):
    r = lax.broadcasted_iota(jnp.int32, (GMLP_BLOCK, GMLP_BLOCK), 0) // CHUNK
    c = lax.broadcasted_iota(jnp.int32, (GMLP_BLOCK, GMLP_BLOCK), 1) // CHUNK
    return (r <= c) if transposed else (c <= r)


def _window_sum(ext, steps, forward):
    rows = ext.shape[0]
    acc = ext
    for k in range(steps):
        shift = 1 << k
        acc = acc + pltpu.roll(acc, (rows - shift) if forward else shift, 0)
    return acc


def _pool_counts(first_row, rows, win):
    t = first_row + lax.broadcasted_iota(jnp.int32, (rows, 1), 0)
    return jnp.minimum(t + 1, win).astype(F32)


def _even_specs(t):
    col = lambda j: pl.BlockSpec((t, D_MODEL), lambda n: (n, j))
    per = t // POOL_HALO
    prev = pl.BlockSpec((POOL_HALO, D_MODEL), lambda n: (jnp.maximum(n * per - 1, 0), 3))
    return col, per, prev


def _full(shape):
    return pl.BlockSpec(shape, lambda n: (0,) * len(shape))


def _gmlp_head(v_h, ng, nb, w_bf):
    xhat, rstd = _ln_stats(v_h)
    vn = (xhat * ng + nb).astype(BF16)
    return xhat, rstd, vn, _dot(w_bf, vn, NN)


def _pool_group(xb_g, prev_g, first_row, grp):
    t = xb_g.shape[0]
    ext = jnp.concatenate([prev_g, xb_g], axis=0)
    tot = _window_sum(ext, grp + 1, False)[POOL_HALO:, :]
    cnt = _pool_counts(first_row, t, POOL_WINDOWS[grp])
    return tot / cnt - xb_g, cnt


def _even_fwd(proj, ws, bs_t, ng, nb, pool_w, pool_b, pool_scale, *, name):
    s = proj.shape[0]
    t = GMLP_BLOCK
    col, per, prev = _even_specs(t)

    def body(u_ref, v_ref, za_ref, xb_ref, zb_ref, xp_ref, ws_ref, bs_ref, ng_ref, nb_ref, pw_ref, pb_ref, ps_ref, o_ref):
        n = pl.program_id(0)
        mask = _chunk_mask()
        for h in range(GMLP_HEADS):
            c0 = h * GMLP_HEAD_DIM
            cs = slice(c0, c0 + GMLP_HEAD_DIM)
            w_bf = jnp.where(mask, ws_ref[h], 0.0).astype(BF16)
            _, _, _, sv = _gmlp_head(v_ref[:, cs].astype(F32),ng_ref[...], nb_ref[...], w_bf)
            sv = sv + bs_ref[:, h:h + 1]
            za = za_ref[:, cs].astype(F32)
            o_ref[:, cs] = (u_ref[:, cs].astype(F32) * sv * (za * _sigmoid(za))).astype(BF16)
        live = (n > 0).astype(F32)
        for grp in range(POOL_GROUPS):
            c0 = grp * POOL_GROUP_DIM
            cs = slice(c0, c0 + POOL_GROUP_DIM)
            pooled, _ = _pool_group(xb_ref[:, cs].astype(F32), xp_ref[:, cs].astype(F32) * live, n * t, grp)
            yb = _dot(pooled.astype(BF16), pw_ref[grp], NN) + pb_ref[:, cs]
            zb = zb_ref[:, cs].astype(F32)
            o_ref[:, D_MODEL + c0:D_MODEL + c0 + POOL_GROUP_DIM] = (yb * ps_ref[:, cs] * (zb * _sigmoid(zb))).astype(BF16)

    return pl.pallas_call(
        body, name=name, grid=(s // t,),
        in_specs=[col(0), col(1), col(2), col(3), col(4), prev,
                  _full((GMLP_HEADS, t, t)), _full((t, LANES)), _full((1, GMLP_HEAD_DIM)), _full((1, GMLP_HEAD_DIM)),
                  _full((POOL_GROUPS, POOL_GROUP_DIM, POOL_GROUP_DIM)), _full((1, D_MODEL)), _full((1, D_MODEL))],
        out_specs=pl.BlockSpec((t, 2 * D_MODEL), lambda n: (n, 0)),
        out_shape=jax.ShapeDtypeStruct((s, 2 * D_MODEL), BF16),
        compiler_params=_params("parallel"),
    )(proj, proj, proj, proj, proj, proj, ws, bs_t, ng, nb, pool_w, pool_b, pool_scale)


def _even_bwd(proj, dmix, ws, ws_t, bs_t, ng, nb, pool_w, pool_b, pool_scale, *, name):
    s = proj.shape[0]
    t = GMLP_BLOCK
    nblk = s // t
    col, per, prev = _even_specs(t)
    nxt = lambda j: pl.BlockSpec((POOL_HALO, D_MODEL), lambda n: (jnp.minimum((n + 1) * per, nblk * per - 1), j))

    def body(u_ref, v_ref, za_ref, xb_ref, zb_ref, xp_ref, zn_ref, da_ref, db_ref, dbn_ref,
             ws_ref, wst_ref, bs_ref, ng_ref, nb_ref, pw_ref, pb_ref, ps_ref,
             dp_ref, gws_ref, gbs_ref, gng_ref, gnb_ref, gpw_ref, gpb_ref, gps_ref):
        n = pl.program_id(0)

        @pl.when(n == 0)
        def _():
            for r in (gws_ref, gbs_ref, gng_ref, gnb_ref, gpw_ref, gpb_ref, gps_ref):
                r[...] = jnp.zeros_like(r)

        mask, mask_t = _chunk_mask(), _chunk_mask(True)
        lane = lax.broadcasted_iota(jnp.int32, (t, LANES), 1)
        ngv, nbv = ng_ref[...], nb_ref[...]
        for h in range(GMLP_HEADS):
            c0 = h * GMLP_HEAD_DIM
            cs = slice(c0, c0 + GMLP_HEAD_DIM)
            w_bf = jnp.where(mask, ws_ref[h], 0.0).astype(BF16)
            wt_bf = jnp.where(mask_t, wst_ref[h], 0.0).astype(BF16)
            xhat, rstd, vn, sv = _gmlp_head(v_ref[:, cs].astype(F32),ngv, nbv, w_bf)
            sv = sv + bs_ref[:, h:h + 1]
            za, u, da = za_ref[:, cs].astype(F32), u_ref[:, cs].astype(F32), da_ref[:, cs].astype(F32)
            sg = _sigmoid(za)
            sl = za * sg
            dp_ref[:, cs] = (da * sv * sl).astype(BF16)
            dp_ref[:, 2 * D_MODEL + c0:2 * D_MODEL + c0 + GMLP_HEAD_DIM] = (
                da * u * sv * (sg * (1.0 + za * (1.0 - sg)))).astype(BF16)
            dsv = da * u * sl
            gbs_ref[...] += jnp.where(lane == h, jnp.sum(dsv, axis=1, keepdims=True), 0.0)
            dsv_bf = dsv.astype(BF16)
            gws_ref[h] += jnp.where(mask, _dot(dsv_bf, vn, NT), 0.0)
            dvn = _dot(wt_bf, dsv_bf, NN)
            dp_ref[:, D_MODEL + c0:D_MODEL + c0 + GMLP_HEAD_DIM] = _ln_bwd_rows(dvn, xhat, rstd, ngv).astype(BF16)
            gng_ref[...] += _colsum(dvn * xhat)
            gnb_ref[...] += _colsum(dvn)
        live_prev = (n > 0).astype(F32)
        live_next = (n < nblk - 1).astype(F32)
        for grp in range(POOL_GROUPS):
            c0 = grp * POOL_GROUP_DIM
            cs = slice(c0, c0 + POOL_GROUP_DIM)
            xb = xb_ref[:, cs].astype(F32)
            pooled, cnt = _pool_group(xb, xp_ref[:, cs].astype(F32) * live_prev, n * t, grp)
            pooled_bf = pooled.astype(BF16)
            pw = pw_ref[grp]
            yb = _dot(pooled_bf, pw, NN) + pb_ref[:, cs]
            ps = ps_ref[:, cs]
            zb, db = zb_ref[:, cs].astype(F32), db_ref[:, cs].astype(F32)
            sg = _sigmoid(zb)
            sl = zb * sg
            dp_ref[:, 4 * D_MODEL + c0:4 * D_MODEL + c0 + POOL_GROUP_DIM] = (
                db * yb * ps * (sg * (1.0 + zb * (1.0 - sg)))).astype(BF16)
            dsl = db * sl
            dy = dsl * ps
            gps_ref[:, cs] += _colsum(dsl * yb)
            gpb_ref[:, cs] += _colsum(dy)
            dy_bf = dy.astype(BF16)
            gpw_ref[grp] += _dot(pooled_bf, dy_bf, TN)
            r = _dot(dy_bf, pw, NT)
            zn = zn_ref[:, cs].astype(F32)
            dyn = (dbn_ref[:, cs].astype(F32) * (zn * _sigmoid(zn)) * ps * live_next).astype(BF16)
            rn = _dot(dyn, pw, NT) / _pool_counts((n + 1) * t, POOL_HALO, POOL_WINDOWS[grp])
            ext = jnp.concatenate([r / cnt, rn], axis=0)
            dxb = _window_sum(ext, grp + 1, True)[:t, :] - r
            dp_ref[:, 3 * D_MODEL + c0:3 * D_MODEL + c0 + POOL_GROUP_DIM] = dxb.astype(BF16)

    out_shape = [
        jax.ShapeDtypeStruct((s, EVEN_IN), BF16),
        jax.ShapeDtypeStruct((GMLP_HEADS, t, t), F32), jax.ShapeDtypeStruct((t, LANES), F32),
        jax.ShapeDtypeStruct((1, GMLP_HEAD_DIM), F32), jax.ShapeDtypeStruct((1, GMLP_HEAD_DIM), F32),
        jax.ShapeDtypeStruct((POOL_GROUPS, POOL_GROUP_DIM, POOL_GROUP_DIM), F32),
        jax.ShapeDtypeStruct((1, D_MODEL), F32), jax.ShapeDtypeStruct((1, D_MODEL), F32),
    ]
    return pl.pallas_call(
        body, name=name, grid=(nblk,),
        in_specs=[col(0), col(1), col(2), col(3), col(4), prev, nxt(4),
                  pl.BlockSpec((t, D_MODEL), lambda n: (n, 0)), pl.BlockSpec((t, D_MODEL), lambda n: (n, 1)), nxt(1),
                  _full((GMLP_HEADS, t, t)), _full((GMLP_HEADS, t, t)), _full((t, LANES)),
                  _full((1, GMLP_HEAD_DIM)), _full((1, GMLP_HEAD_DIM)),
                  _full((POOL_GROUPS, POOL_GROUP_DIM, POOL_GROUP_DIM)), _full((1, D_MODEL)), _full((1, D_MODEL))],
        out_specs=[pl.BlockSpec((t, EVEN_IN), lambda n: (n, 0))] + [_full(o.shape) for o in out_shape[1:]],
        out_shape=out_shape,
        compiler_params=_params("arbitrary"),
    )(proj, proj, proj, proj, proj, proj, proj, dmix, dmix, dmix, ws, ws_t, bs_t, ng, nb, pool_w, pool_b, pool_scale)


ROPE_HALF = MLA_ROPE // 2


def _rope(v, cos, sin_signed):
    return v * cos + pltpu.roll(v, 2 * ROPE_HALF, 1) * sin_signed


def _rope_bwd(d, cos, sin_signed):
    return d * cos + pltpu.roll(d * sin_signed, 2 * ROPE_HALF, 1)


def _slab_lanes(shape, which):
    lane = lax.broadcasted_iota(jnp.int32, shape, 1)
    return (lane // ROPE_HALF) % 2 == which


def _rms(v, g):
    r = lax.rsqrt(jnp.mean(v * v, axis=-1, keepdims=True) + LN_EPS)
    return v * r * g, r


def _rms_bwd(dy, v, r, g):
    u = dy * g
    return r * u - v * (r * r * r) * jnp.mean(u * v, axis=-1, keepdims=True)


def _mla_prep(proj, gq, gkv, cos, sin_signed, *, name):
    s = proj.shape[0]
    ts = _tile(s, 512)

    def body(p_ref, gq_ref, gkv_ref, c_ref, s_ref, q_ref, k_ref):
        qcn, _ = _rms(p_ref[:, :MLA_Q_RANK].astype(F32), gq_ref[...])
        kvn, _ = _rms(p_ref[:, MLA_Q_RANK:MLA_Q_RANK + MLA_KV_RANK].astype(F32), gkv_ref[...])
        kr = p_ref[:, MLA_Q_RANK + MLA_KV_RANK:].astype(F32)
        lane = lax.broadcasted_iota(jnp.int32, kr.shape, 1)
        by1, by2 = pltpu.roll(kr, ROPE_HALF, 1), pltpu.roll(kr, 2 * ROPE_HALF, 1)
        both = jnp.where(lane < ROPE_HALF, kr, jnp.where(lane < 3 * ROPE_HALF, by1, by2))
        kr = _rope(both, c_ref[...], s_ref[...])
        q_ref[...] = qcn.astype(BF16)
        k_ref[...] = jnp.concatenate([kvn, kr], axis=1).astype(BF16)

    return pl.pallas_call(
        body, name=name, grid=(s // ts,),
        in_specs=[_small_spec(ts), _vec_spec(MLA_Q_RANK), _vec_spec(MLA_KV_RANK), _row_spec(ts, LANES), _row_spec(ts, LANES)],
        out_specs=[_row_spec(ts, MLA_Q_RANK), _row_spec(ts, QK_PAD)],
        out_shape=[jax.ShapeDtypeStruct((s, MLA_Q_RANK), BF16), jax.ShapeDtypeStruct((s, QK_PAD), BF16)],
        compiler_params=_params("parallel"),
    )(proj, gq, gkv, cos, sin_signed)


def _mla_prep_bwd(proj, dqcn, dkv, gq, gkv, cos, sin_signed, *, name):
    s = proj.shape[0]
    ts = _tile(s, 512)

    def body(p_ref, dq_ref, dkv_ref, gq_ref, gkv_ref, c_ref, s_ref, ds_ref, ggq_ref, ggkv_ref):
        @pl.when(pl.program_id(0) == 0)
        def _():
            ggq_ref[...] = jnp.zeros_like(ggq_ref)
            ggkv_ref[...] = jnp.zeros_like(ggkv_ref)

        qc = p_ref[:, :MLA_Q_RANK].astype(F32)
        kvc = p_ref[:, MLA_Q_RANK:MLA_Q_RANK + MLA_KV_RANK].astype(F32)
        _, rq = _rms(qc, gq_ref[...])
        _, rkv = _rms(kvc, gkv_ref[...])
        dq = dq_ref[...]
        dkvn = dkv_ref[:, :MLA_KV_RANK]
        ggq_ref[...] += _colsum(dq * qc * rq)
        ggkv_ref[...] += _colsum(dkvn * kvc * rkv)
        dboth = _rope_bwd(dkv_ref[:, MLA_KV_RANK:], c_ref[...], s_ref[...])
        lane = lax.broadcasted_iota(jnp.int32, dboth.shape, 1)
        pair = dboth + pltpu.roll(dboth, 3 * ROPE_HALF, 1)
        dkr = jnp.where(lane < ROPE_HALF, pair, jnp.where(lane < 2 * ROPE_HALF, pltpu.roll(pair, 3 * ROPE_HALF, 1), 0.0))
        ds_ref[...] = jnp.concatenate(
            [_rms_bwd(dq, qc, rq, gq_ref[...]), _rms_bwd(dkvn, kvc, rkv, gkv_ref[...]), dkr], axis=1).astype(BF16)

    return pl.pallas_call(
        body, name=name, grid=(s // ts,),
        in_specs=[_small_spec(ts), _row_spec(ts, MLA_Q_RANK), _row_spec(ts, QK_PAD),
                  _vec_spec(MLA_Q_RANK), _vec_spec(MLA_KV_RANK), _row_spec(ts, LANES), _row_spec(ts, LANES)],
        out_specs=[_row_spec(ts, ODD_SMALL_PAD), _vec_spec(MLA_Q_RANK), _vec_spec(MLA_KV_RANK)],
        out_shape=[jax.ShapeDtypeStruct((s, ODD_SMALL_PAD), BF16), jax.ShapeDtypeStruct((1, MLA_Q_RANK), F32),
                   jax.ShapeDtypeStruct((1, MLA_KV_RANK), F32)],
        compiler_params=_params("arbitrary"),
    )(proj, dqcn, dkv, gq, gkv, cos, sin_signed)


Q_HEAD_GROUP = 16
LOG2_E = 1.4426950408889634
Q_PRESCALE = ATTN_SCALE * LOG2_E


def _q_build(q_nope, q_rope_pre, wuk_hdr, cos, sin_signed, *, name):
    s = q_nope.shape[0]
    ts = _tile(s, 512)
    hg = Q_HEAD_GROUP

    def body(qn_ref, qr_ref, w_ref, c_ref, s_ref, o_ref):
        for pair in range(hg // 2):
            r = _rope(qr_ref[:, pair * LANES:(pair + 1) * LANES], c_ref[...], s_ref[...])
            for j in range(2):
                h = 2 * pair + j
                ql = _dot(qn_ref[:, h * MLA_NOPE:(h + 1) * MLA_NOPE], w_ref[h], NN)
                mine = jnp.where(_slab_lanes(r.shape, j), r, 0.0)
                o_ref[h] = (jnp.concatenate([ql, mine], axis=1) * Q_PRESCALE).astype(BF16)

    return pl.pallas_call(
        body, name=name, grid=(s // ts, MLA_HEADS // hg),
        in_specs=[pl.BlockSpec((ts, hg * MLA_NOPE), lambda i, p: (i, p)), pl.BlockSpec((ts, hg * MLA_ROPE), lambda i, p: (i, p)),
                  pl.BlockSpec((hg, MLA_NOPE, MLA_KV_RANK), lambda i, p: (p, 0, 0)),
                  pl.BlockSpec((ts, LANES), lambda i, p: (i, 0)), pl.BlockSpec((ts, LANES), lambda i, p: (i, 0))],
        out_specs=pl.BlockSpec((hg, ts, QK_PAD), lambda i, p: (p, i, 0)),
        out_shape=jax.ShapeDtypeStruct((MLA_HEADS, s, QK_PAD), BF16),
        compiler_params=_params("parallel", "parallel"),
    )(q_nope, q_rope_pre, wuk_hdr, cos, sin_signed)


def _q_bwd(dq, q_nope, wuk_hrd, cos, sin_signed, *, name):
    s = q_nope.shape[0]
    ts = _tile(s, 512)
    hg = Q_HEAD_GROUP

    nope_w, all_w = hg * MLA_NOPE, hg * (MLA_NOPE + MLA_ROPE)

    def body(dq_ref, qn_ref, w_ref, c_ref, s_ref, dall_ref, gw_ref):
        @pl.when(pl.program_id(1) == 0)
        def _():
            gw_ref[...] = jnp.zeros_like(gw_ref)

        for h in range(hg):
            dql = dq_ref[h, :, :MLA_KV_RANK]
            dall_ref[:, h * MLA_NOPE:(h + 1) * MLA_NOPE] = _dot(dql, w_ref[h], NN).astype(BF16)
            gw_ref[h] += _dot(dql, qn_ref[:, h * MLA_NOPE:(h + 1) * MLA_NOPE], TN)
        for pair in range(hg // 2):
            hi0 = dq_ref[2 * pair, :, MLA_KV_RANK:].astype(F32)
            hi1 = dq_ref[2 * pair + 1, :, MLA_KV_RANK:].astype(F32)
            d = jnp.where(_slab_lanes(hi0.shape, 0), hi0, hi1)
            dall_ref[:, nope_w + pair * LANES:nope_w + (pair + 1) * LANES] = _rope_bwd(d, c_ref[...], s_ref[...]).astype(BF16)

    return pl.pallas_call(
        body, name=name, grid=(MLA_HEADS // hg, s // ts),
        in_specs=[pl.BlockSpec((hg, ts, QK_PAD), lambda p, i: (p, i, 0)), pl.BlockSpec((ts, nope_w), lambda p, i: (i, p)),
                  pl.BlockSpec((hg, MLA_KV_RANK, MLA_NOPE), lambda p, i: (p, 0, 0)),
                  pl.BlockSpec((ts, LANES), lambda p, i: (i, 0)), pl.BlockSpec((ts, LANES), lambda p, i: (i, 0))],
        out_specs=[pl.BlockSpec((ts, all_w), lambda p, i: (i, p)),
                   pl.BlockSpec((hg, MLA_KV_RANK, MLA_NOPE), lambda p, i: (p, 0, 0))],
        out_shape=[jax.ShapeDtypeStruct((s, MLA_HEADS * (MLA_NOPE + MLA_ROPE)), BF16),
                   jax.ShapeDtypeStruct((MLA_HEADS, MLA_KV_RANK, MLA_NOPE), F32)],
        compiler_params=_params("parallel", "arbitrary"),
    )(dq, q_nope, wuk_hrd, cos, sin_signed)


ATTN_BQ = 128
ATTN_BK = 512
ATTN_BK_FWD = 1024


def _diag_mask(rows, bq, bk, q0, k0):
    qc = (q0 + lax.broadcasted_iota(jnp.int32, (rows, bk), 0) % bq) // CHUNK
    kc = (k0 + lax.broadcasted_iota(jnp.int32, (rows, bk), 1)) // CHUNK
    return kc <= qc


def _attn_fwd(q, k, *, name):
    nh, s, dk = q.shape
    bq, bk = _tile(s, ATTN_BQ), _tile(s, ATTN_BK_FWD)
    rows = nh * bq

    def body(q_ref, k_ref, o_ref, lse_ref):
        i = pl.program_id(0)
        qb = q_ref[...].reshape(rows, dk)
        n_before = (i * bq) // bk

        def step(j, width, carry, masked):
            m, l, acc = carry
            k0 = pl.multiple_of(j * bk, bk)
            kb = k_ref[pl.ds(k0, width), :]
            sc = _dot(qb, kb, NT)
            if masked:
                sc = jnp.where(_diag_mask(rows, bq, width, i * bq, k0), sc, NEG)
            m_new = jnp.maximum(m, jnp.max(sc, axis=1, keepdims=True))
            p = jnp.exp2(sc - m_new)
            a = jnp.exp2(m - m_new)
            l = a * l + jnp.sum(p, axis=1, keepdims=True)
            acc = a * acc + _dot(p.astype(BF16), kb[:, :MLA_KV_RANK], NN)
            return m_new, l, acc

        init = (jnp.full((rows, 1), NEG, F32), jnp.zeros((rows, 1), F32), jnp.zeros((rows, MLA_KV_RANK), F32))
        carry = lax.fori_loop(0, n_before, lambda j, c: step(j, bk, c, False), init)
        for part in range(bk // bq):
            @pl.when(i % (bk // bq) == part)
            def _(part=part):
                m, l, acc = step(n_before, (part + 1) * bq, carry, True)
                o_ref[...] = (acc / l).astype(BF16).reshape(nh, bq, MLA_KV_RANK)
                lse_ref[...] = jnp.broadcast_to(m + jnp.log2(l), (rows, LANES)).reshape(nh, bq, LANES)

    return pl.pallas_call(
        body, name=name, grid=(s // bq,),
        in_specs=[pl.BlockSpec((nh, bq, dk), lambda i: (0, i, 0)), pl.BlockSpec((s, dk), lambda i: (0, 0))],
        out_specs=[pl.BlockSpec((nh, bq, MLA_KV_RANK), lambda i: (0, i, 0)), pl.BlockSpec((nh, bq, LANES), lambda i: (0, i, 0))],
        out_shape=[jax.ShapeDtypeStruct((nh, s, MLA_KV_RANK), BF16), jax.ShapeDtypeStruct((nh, s, LANES), F32)],
        compiler_params=_params("parallel"),
    )(q, k)


def _attn_bwd(q, k, do, o, lse, *, name):
    nh, s, dk = q.shape
    bq, bk = _tile(s, ATTN_BQ), _tile(s, ATTN_BK)
    rows = nh * bq

    def body(q_ref, k_ref, do_ref, o_ref, lse_ref, dq_ref, dkv_ref):
        i = pl.program_id(0)
        n_before = (i * bq) // bk

        @pl.when(i == 0)
        def _():
            dkv_ref[...] = jnp.zeros_like(dkv_ref)

        qb = q_ref[...].reshape(rows, dk)
        dob = do_ref[...].reshape(rows, MLA_KV_RANK)
        lse_b = lse_ref[...].reshape(rows, LANES)[:, :1]
        delta = jnp.sum(dob.astype(F32) * o_ref[...].reshape(rows, MLA_KV_RANK).astype(F32), axis=1, keepdims=True)

        def step(j, width, dq, masked):
            j0 = pl.multiple_of(j * bk, bk)
            kb = k_ref[pl.ds(j0, width), :]
            halves = []
            for part in range(2):
                at = slice(part * (rows // 2), (part + 1) * (rows // 2))
                qh, doh = qb[at], dob[at]
                sc = _dot(qh, kb, NT)
                if masked:
                    sc = jnp.where(_diag_mask(rows // 2, bq, width, i * bq, j0), sc, NEG)
                p = jnp.exp2(sc - lse_b[at])
                dp = _dot(doh, kb[:, :MLA_KV_RANK], NT)
                ds_bf = (p * (dp - delta[at])).astype(BF16)
                dkv_ref[pl.ds(j0, width), :] += _dot(ds_bf, qh, TN) * (1.0 / LOG2_E)
                dkv_ref[pl.ds(j0, width), :MLA_KV_RANK] += _dot(p.astype(BF16), doh, TN)
                halves.append(dq[at] + _dot(ds_bf, kb, NN))
            return jnp.concatenate(halves, axis=0)

        dq_before = lax.fori_loop(0, n_before, lambda j, c: step(j, bk, c, False), jnp.zeros((rows, dk), F32))
        for part in range(bk // bq):
            @pl.when(i % (bk // bq) == part)
            def _(part=part):
                dq = step(n_before, (part + 1) * bq, dq_before, True) * ATTN_SCALE
                dq_ref[...] = dq.astype(BF16).reshape(nh, bq, dk)

    blk = lambda w: pl.BlockSpec((nh, bq, w), lambda i: (0, i, 0))
    return pl.pallas_call(
        body, name=name, grid=(s // bq,),
        in_specs=[blk(dk), pl.BlockSpec((s, dk), lambda i: (0, 0)), blk(MLA_KV_RANK), blk(MLA_KV_RANK), blk(LANES)],
        out_specs=[blk(dk), pl.BlockSpec((s, dk), lambda i: (0, 0))],
        out_shape=[jax.ShapeDtypeStruct((nh, s, dk), BF16), jax.ShapeDtypeStruct((s, dk), F32)],
        compiler_params=_params("arbitrary"),
    )(q, k, do, o, lse)


HEAD_GROUP = 4
SMALL_BLOCK = MLA_WIDTH // ODD_SMALL_PAD


def _small_spec(ts):
    return pl.BlockSpec((ts, ODD_SMALL_PAD), lambda i: (i, SMALL_BLOCK))


def _o_build(o_lat, wuv_hrv, proj, *, name):
    s = proj.shape[0]
    ts = _tile(s, 1024)
    w = HEAD_GROUP * MLA_V

    def body(ol_ref, w_ref, z_ref, og_ref):
        for j in range(HEAD_GROUP):
            cs = slice(j * MLA_V, (j + 1) * MLA_V)
            z = z_ref[:, cs].astype(F32)
            og_ref[:, cs] = (_dot(ol_ref[j], w_ref[j], NN) * (z * _sigmoid(z))).astype(BF16)

    return pl.pallas_call(
        body, name=name, grid=(s // ts, MLA_HEADS // HEAD_GROUP),
        in_specs=[pl.BlockSpec((HEAD_GROUP, ts, MLA_KV_RANK), lambda i, g: (g, i, 0)),
                  pl.BlockSpec((HEAD_GROUP, MLA_KV_RANK, MLA_V), lambda i, g: (g, 0, 0)),
                  pl.BlockSpec((ts, w), lambda i, g: (i, g))],
        out_specs=pl.BlockSpec((ts, w), lambda i, g: (i, g)),
        out_shape=jax.ShapeDtypeStruct((s, MLA_WIDTH), BF16),
        compiler_params=_params("parallel", "parallel"),
    )(o_lat, wuv_hrv, proj)


def _o_bwd(dg, proj, o_lat, wuv_hrv, wuv_hvr, *, name):
    s = proj.shape[0]
    ts = _tile(s, 1024)
    w = HEAD_GROUP * MLA_V

    def body(dg_ref, z_ref, ol_ref, w_ref, wt_ref, dol_ref, dz_ref, gw_ref):
        @pl.when(pl.program_id(1) == 0)
        def _():
            gw_ref[...] = jnp.zeros_like(gw_ref)

        for j in range(HEAD_GROUP):
            cs = slice(j * MLA_V, (j + 1) * MLA_V)
            z, dgj, ol = z_ref[:, cs].astype(F32), dg_ref[:, cs].astype(F32), ol_ref[j]
            sg = _sigmoid(z)
            o = _dot(ol, w_ref[j], NN)
            dz_ref[:, cs] = (dgj * o * (sg * (1.0 + z * (1.0 - sg)))).astype(BF16)
            do_bf = (dgj * (z * sg)).astype(BF16)
            dol_ref[j] = _dot(do_bf, wt_ref[j], NN).astype(BF16)
            gw_ref[j] += _dot(ol, do_bf, TN)

    hs = lambda a, b: pl.BlockSpec((HEAD_GROUP, a, b), lambda g, i: (g, 0, 0))
    return pl.pallas_call(
        body, name=name, grid=(MLA_HEADS // HEAD_GROUP, s // ts),
        in_specs=[pl.BlockSpec((ts, w), lambda g, i: (i, g)), pl.BlockSpec((ts, w), lambda g, i: (i, g)),
                  pl.BlockSpec((HEAD_GROUP, ts, MLA_KV_RANK), lambda g, i: (g, i, 0)),
                  hs(MLA_KV_RANK, MLA_V), hs(MLA_V, MLA_KV_RANK)],
        out_specs=[pl.BlockSpec((HEAD_GROUP, ts, MLA_KV_RANK), lambda g, i: (g, i, 0)),
                   pl.BlockSpec((ts, w), lambda g, i: (i, g)), hs(MLA_KV_RANK, MLA_V)],
        out_shape=[jax.ShapeDtypeStruct((MLA_HEADS, s, MLA_KV_RANK), BF16), jax.ShapeDtypeStruct((s, MLA_WIDTH), BF16),
                   jax.ShapeDtypeStruct((MLA_HEADS, MLA_KV_RANK, MLA_V), F32)],
        compiler_params=_params("parallel", "arbitrary"),
    )(dg, proj, o_lat, wuv_hrv, wuv_hvr)


def _ada_mod(c_all, ada_w, ada_b_sh, *, name):
    nl, _, cols = ada_w.shape

    def body(c_ref, w_ref, b_ref, o_ref):
        c = c_ref[...]
        cond = (c * _sigmoid(c)).astype(BF16)
        for l in range(nl):
            o_ref[l] = _dot(cond, w_ref[l].astype(BF16), NN) + b_ref[l]

    return pl.pallas_call(
        body, name=name, out_shape=jax.ShapeDtypeStruct((nl, c_all.shape[0], cols), F32),
        compiler_params=_params(),
    )(c_all, ada_w, ada_b_sh)


def _ada_grad(c_all_t, dmod_sh, *, name):
    nl, _, cols = dmod_sh.shape
    d = c_all_t.shape[0]

    def body(c_ref, dm_ref, gw_ref):
        c = c_ref[...]
        cond_t = c * _sigmoid(c)
        for l in range(nl):
            gw_ref[l] = lax.dot_general(cond_t, dm_ref[l], (NN, ((), ())), precision=lax.Precision.HIGHEST,
                                        preferred_element_type=F32)

    return pl.pallas_call(
        body, name=name, out_shape=jax.ShapeDtypeStruct((nl, d, cols), F32), compiler_params=_params(),
    )(c_all_t, dmod_sh)


def _sum_devices(parts, *, name):
    def body(p_ref, o_ref):
        acc = p_ref[0]
        for k in range(1, parts.shape[0]):
            acc = acc + p_ref[k]
        o_ref[...] = acc

    return pl.pallas_call(body, name=name, out_shape=jax.ShapeDtypeStruct(parts.shape[1:], F32), compiler_params=_params())(parts)


def _adamw_math(w, g, m, v):
    c1 = 1.0 - ADAM_B1 ** ADAM_STEP
    c2 = 1.0 - ADAM_B2 ** ADAM_STEP
    nm = ADAM_B1 * m + (1.0 - ADAM_B1) * g
    nv = ADAM_B2 * v + (1.0 - ADAM_B2) * (g * g)
    return -ADAM_LR * ((nm / c1) / (jnp.sqrt(nv / c2) + ADAM_EPS) + ADAM_WD * w), nm, nv


ADAMW_BLOCK_BYTES = 1 << 20


def _adamw(w, g, m, v, *, name, after=None):
    shape = w.shape
    a, b = shape[-2], shape[-1]
    lead = 1
    for dim in shape[:-2]:
        lead *= dim
    row_bytes = 4 * b
    if a * row_bytes <= ADAMW_BLOCK_BYTES:
        ta = a
        tl = max(1, min(lead, ADAMW_BLOCK_BYTES // (a * row_bytes)))
        while lead % tl:
            tl -= 1
    else:
        tl = 1
        ta = _tile(a, 256)
    to3 = lambda t: t.reshape(lead, a, b)

    def body(w_ref, g_ref, m_ref, v_ref, *rest):
        d_ref, nm_ref, nv_ref = rest[-3:]
        d_ref[...], nm_ref[...], nv_ref[...] = _adamw_math(w_ref[...], g_ref[...], m_ref[...], v_ref[...])

    spec = pl.BlockSpec((tl, ta, b), lambda i, j: (i, j, 0))
    out = jax.ShapeDtypeStruct((lead, a, b), F32)
    order = [] if after is None else [after]
    res = pl.pallas_call(
        body, name=name, grid=(lead // tl, a // ta), in_specs=[spec] * 4 + [pl.BlockSpec(memory_space=pl.ANY)] * len(order),
        out_specs=[spec] * 3, out_shape=[out] * 3, compiler_params=_params("parallel", "parallel"),
    )(to3(w), to3(g), to3(m), to3(v), *order)
    return [r.reshape(shape) for r in res]


def _adamw_small(ws, gs, ms, vs, *, name):
    n = len(ws)

    def body(*refs):
        for k in range(n):
            w_ref, g_ref, m_ref, v_ref = (refs[j * n + k] for j in range(4))
            d_ref, nm_ref, nv_ref = (refs[(4 + j) * n + k] for j in range(3))
            d_ref[...], nm_ref[...], nv_ref[...] = _adamw_math(w_ref[...], g_ref[...], m_ref[...], v_ref[...])

    outs = [jax.ShapeDtypeStruct(w.shape, F32) for w in ws]
    res = pl.pallas_call(body, name=name, out_shape=outs * 3, compiler_params=_params())(*ws, *gs, *ms, *vs)
    return res[:n], res[n:2 * n], res[2 * n:]


def _flip(v, bit):
    return 1 - v if bit else v


CHIP_DELTAS = ((1, 0), (0, 1), (1, 1))
SUM_ROWS = 32


def _all_gather_chips(shard, *, name):
    def body(x_ref, o_ref, send_sems, recv_sems, local_sem):
        x, y, c = lax.axis_index("x"), lax.axis_index("y"), lax.axis_index("c")
        mine = pltpu.make_async_copy(x_ref, o_ref.at[2 * x + y], local_sem)
        mine.start()

        def copy(k):
            tx, ty = _flip(x, CHIP_DELTAS[k][0]), _flip(y, CHIP_DELTAS[k][1])
            send = pltpu.make_async_remote_copy(src_ref=x_ref, dst_ref=o_ref.at[2 * x + y], send_sem=send_sems.at[k],
                                                recv_sem=recv_sems.at[k], device_id=(tx, ty, c), device_id_type=MESH)
            recv = pltpu.make_async_remote_copy(src_ref=x_ref, dst_ref=o_ref.at[2 * tx + ty], send_sem=send_sems.at[k],
                                                recv_sem=recv_sems.at[k], device_id=(tx, ty, c), device_id_type=MESH)
            return send, recv

        pairs = [copy(k) for k in range(3)]
        for send, _ in pairs:
            send.start()
        for _, recv in pairs:
            recv.wait_recv()
        for send, _ in pairs:
            send.wait_send()
        mine.wait()

    return pl.pallas_call(
        body, name=name, out_shape=jax.ShapeDtypeStruct((N_CHIPS,) + shard.shape, shard.dtype),
        in_specs=[HBM], out_specs=HBM,
        scratch_shapes=[pltpu.SemaphoreType.DMA((3,)), pltpu.SemaphoreType.DMA((3,)), pltpu.SemaphoreType.DMA(())],
    )(shard)


def _gather_weights(shards, *, name):
    n = len(shards)

    def body(*refs):
        w_refs, o_refs = refs[:n], refs[n:2 * n]
        ici_send, ici_recv, d2d_send, d2d_recv, local_sems = refs[2 * n:]
        x, y, c = lax.axis_index("x"), lax.axis_index("y"), lax.axis_index("c")
        me = 2 * x + y
        peers = [(_flip(x, dx), _flip(y, dy)) for dx, dy in CHIP_DELTAS]
        locals_ = [pltpu.make_async_copy(w_refs[k], o_refs[k].at[me], local_sems.at[k]) for k in range(n)]
        for cp in locals_:
            cp.start()

        def rows(k, which):
            half = shards[k].shape[0] // 2
            return pl.ds(pl.multiple_of(which * half, half), half)

        def over_chips(k, d, slot):
            tx, ty = peers[d]
            return pltpu.make_async_remote_copy(
                src_ref=w_refs[k].at[rows(k, c)], dst_ref=o_refs[k].at[slot, rows(k, c)], send_sem=ici_send.at[k, d],
                recv_sem=ici_recv.at[k, d], device_id=(tx, ty, c), device_id_type=MESH)

        def to_sibling(k, d, which):
            tx, ty = peers[d]
            at = o_refs[k].at[2 * tx + ty, rows(k, which)]
            return pltpu.make_async_remote_copy(src_ref=at, dst_ref=at, send_sem=d2d_send.at[k, d], recv_sem=d2d_recv.at[k, d],
                                                device_id=(x, y, 1 - c), device_id_type=MESH)

        sends = [over_chips(k, d, me) for k in range(n) for d in range(3)]
        for cp in sends:
            cp.start()
        passed = []
        for k in range(n):
            for d in range(3):
                over_chips(k, d, 2 * peers[d][0] + peers[d][1]).wait_recv()
                passed.append(to_sibling(k, d, c))
                passed[-1].start()
        for k in range(n):
            for d in range(3):
                to_sibling(k, d, 1 - c).wait_recv()
        for cp in sends + passed:
            cp.wait_send()
        for cp in locals_:
            cp.wait()

    return pl.pallas_call(
        body, name=name, out_shape=[jax.ShapeDtypeStruct((N_CHIPS,) + w.shape, w.dtype) for w in shards],
        in_specs=[HBM] * n, out_specs=[HBM] * n,
        scratch_shapes=[pltpu.SemaphoreType.DMA((n, 3))] * 4 + [pltpu.SemaphoreType.DMA((n,))],
    )(*shards)


def _add_into(dst_ref, src_ref):
    ns, r, _ = dst_ref.shape
    step = SUM_ROWS if r % SUM_ROWS == 0 else r
    for s in range(ns):
        def tile(t, carry):
            at = pl.ds(pl.multiple_of(t * step, step), step)
            dst_ref[s, at, :] = (dst_ref[s, at, :].astype(F32) + src_ref[s, at, :].astype(F32)).astype(dst_ref.dtype)
            return carry
        lax.fori_loop(0, r // step, tile, 0)


def _reduce_sibling(grads, *, name):
    n = len(grads)

    def body(*refs):
        g_refs, o_refs = refs[:n], refs[n:2 * n]
        mine, got = refs[2 * n:3 * n], refs[3 * n:4 * n]
        send_sems, recv_sems, load_sems, store_sems = refs[4 * n:]
        x, y, c = lax.axis_index("x"), lax.axis_index("y"), lax.axis_index("c")
        loads = [pltpu.make_async_copy(g_refs[k].at[:, c], mine[k], load_sems.at[k]) for k in range(n)]
        swaps = [pltpu.make_async_remote_copy(src_ref=g_refs[k].at[:, 1 - c], dst_ref=got[k], send_sem=send_sems.at[k],
                                              recv_sem=recv_sems.at[k], device_id=(x, y, 1 - c), device_id_type=MESH)
                 for k in range(n)]
        for cp in loads + swaps:
            cp.start()
        stores = []
        for k in range(n):
            loads[k].wait()
            swaps[k].wait_recv()
            _add_into(mine[k], got[k])
            stores.append(pltpu.make_async_copy(mine[k], o_refs[k], store_sems.at[k]))
            stores[-1].start()
        for k in range(n):
            swaps[k].wait_send()
            stores[k].wait()

    half = [jax.ShapeDtypeStruct((g.shape[0],) + g.shape[2:], g.dtype) for g in grads]
    return pl.pallas_call(
        body, name=name, out_shape=half, in_specs=[HBM] * n, out_specs=[HBM] * n,
        scratch_shapes=[pltpu.VMEM(h.shape, h.dtype) for h in half] * 2 + [pltpu.SemaphoreType.DMA((n,))] * 4,
        compiler_params=_params(),
    )(*grads)


def _reduce_chips(parts, landed, *, name):
    n = len(parts)

    def body(*refs):
        p_refs, l_refs, o_refs = refs[:n], refs[n:2 * n], refs[2 * n:3 * n]
        got, total = refs[3 * n:4 * n], refs[4 * n:5 * n]
        load_sems, share_send, share_recv, store_sems = refs[5 * n:]
        x, y, c = lax.axis_index("x"), lax.axis_index("y"), lax.axis_index("c")
        me = 2 * x + y
        slots = [me] + [2 * _flip(x, dx) + _flip(y, dy) for dx, dy in CHIP_DELTAS]
        loads = [[pltpu.make_async_copy((p_refs if j == 0 else l_refs)[k].at[slot], got[k].at[slot], load_sems.at[k, j])
                  for j, slot in enumerate(slots)] for k in range(n)]
        for per_array in loads:
            for cp in per_array:
                cp.start()
        shares, stores = [], []
        for k in range(n):
            for cp in loads[k]:
                cp.wait()
            r = total[k].shape[0]
            step = SUM_ROWS if r % SUM_ROWS == 0 else r

            def tile(t, carry, k=k, step=step):
                at = pl.ds(pl.multiple_of(t * step, step), step)
                acc = got[k][0, at, :].astype(F32)
                for s in range(1, N_CHIPS):
                    acc = acc + got[k][s, at, :].astype(F32)
                total[k][at, :] = acc
                return carry

            lax.fori_loop(0, r // step, tile, 0)
            stores.append(pltpu.make_async_copy(total[k], o_refs[k].at[c], store_sems.at[k]))
            shares.append(pltpu.make_async_remote_copy(
                src_ref=total[k], dst_ref=o_refs[k].at[c], send_sem=share_send.at[k], recv_sem=share_recv.at[k],
                device_id=(x, y, 1 - c), device_id_type=MESH))
            stores[-1].start()
            shares[-1].start()
        for k in range(n):
            pltpu.make_async_remote_copy(
                src_ref=total[k], dst_ref=o_refs[k].at[1 - c], send_sem=share_send.at[k], recv_sem=share_recv.at[k],
                device_id=(x, y, 1 - c), device_id_type=MESH).wait_recv()
        for cp in shares:
            cp.wait_send()
        for cp in stores:
            cp.wait()

    return pl.pallas_call(
        body, name=name, out_shape=[jax.ShapeDtypeStruct((2,) + p.shape[1:], F32) for p in parts],
        in_specs=[HBM] * (2 * n), out_specs=[HBM] * n,
        scratch_shapes=[pltpu.VMEM(p.shape, p.dtype) for p in parts] + [pltpu.VMEM(p.shape[1:], F32) for p in parts]
        + [pltpu.SemaphoreType.DMA((n, N_CHIPS))] + [pltpu.SemaphoreType.DMA((n,))] * 3,
        compiler_params=_params(),
    )(*parts, *landed)


SEM = pl.BlockSpec(memory_space=pltpu.SEMAPHORE)
IN_FLIGHT = pltpu.SideEffectType.DATAFLOW_SIDE_EFFECTING


def _chip_copies(s_refs, l_refs, sems, scatter, theirs):
    x, y, c = lax.axis_index("x"), lax.axis_index("y"), lax.axis_index("c")
    me = 2 * x + y
    copies = []
    for k in range(len(s_refs)):
        for d, (dx, dy) in enumerate(CHIP_DELTAS):
            tx, ty = _flip(x, dx), _flip(y, dy)
            peer = 2 * tx + ty
            send_sem, recv_sem = sems[2 * (3 * k + d)], sems[2 * (3 * k + d) + 1]
            copies.append(pltpu.make_async_remote_copy(
                src_ref=s_refs[k].at[peer] if scatter else s_refs[k], dst_ref=l_refs[k].at[peer if theirs else me],
                send_sem=send_sem, recv_sem=recv_sem, device_id=(tx, ty, c), device_id_type=MESH))
    return copies


def _chips_start(srcs, lands, after, *, scatter, name):
    n = len(srcs)
    n_sem = 2 * 3 * n

    def body(*refs):
        s_refs, l_refs = refs[:n], refs[n:2 * n]
        sems = refs[2 * n + 1:2 * n + 1 + n_sem]
        token = refs[-1]
        for cp in _chip_copies(s_refs, l_refs, sems, scatter, False):
            cp.start()
        token[...] = jnp.zeros_like(token)

    hbm = lambda a: pltpu.HBM(a.shape, a.dtype)
    res = pl.pallas_call(
        body, name=name,
        out_shape=(*[pltpu.SemaphoreType.DMA(())] * n_sem, *[hbm(a) for a in srcs], *[hbm(a) for a in lands],
                   jax.ShapeDtypeStruct((8, LANES), F32)),
        in_specs=[HBM] * (2 * n) + [pl.BlockSpec(memory_space=pl.ANY)],
        out_specs=(*[SEM] * n_sem, *[HBM] * (2 * n), VMEM),
        input_output_aliases={k: n_sem + k for k in range(2 * n)},
        compiler_params=pltpu.CompilerParams(has_side_effects=IN_FLIGHT),
    )(*[pltpu.with_memory_space_constraint(a, pltpu.HBM) for a in list(srcs) + list(lands)], after)
    return res[:n_sem], res[n_sem:n_sem + n], res[n_sem + n:n_sem + 2 * n], res[-1]


def _chips_wait(sems, srcs, lands, after, *, scatter, name):
    n = len(srcs)
    n_sem = len(sems)

    def body(*refs):
        s_refs, l_refs = refs[:n], refs[n:2 * n]
        sem_refs = refs[2 * n:2 * n + n_sem]
        for cp in _chip_copies(s_refs, l_refs, sem_refs, scatter, False):
            cp.wait_send()
        for cp in _chip_copies(s_refs, l_refs, sem_refs, scatter, True):
            cp.wait_recv()

    hbm = lambda a: pltpu.HBM(a.shape, a.dtype)
    res = pl.pallas_call(
        body, name=name, out_shape=tuple(hbm(a) for a in list(srcs) + list(lands)),
        in_specs=[HBM] * (2 * n) + [SEM] * n_sem + [pl.BlockSpec(memory_space=pl.ANY)], out_specs=tuple([HBM] * (2 * n)),
        input_output_aliases={k: k for k in range(2 * n)},
        compiler_params=pltpu.CompilerParams(has_side_effects=IN_FLIGHT),
    )(*srcs, *lands, *sems, after)
    return res[:n], res[n:]


def _all_gather_devices(rows, *, name, after=None):
    deltas = [(dx, dy, dc) for dx in (0, 1) for dy in (0, 1) for dc in (0, 1)][1:]
    order = [] if after is None else [after]

    def body(x_ref, *rest):
        o_ref, send_sems, recv_sems = rest[-3:]
        x, y, c = lax.axis_index("x"), lax.axis_index("y"), lax.axis_index("c")
        me = 4 * x + 2 * y + c
        o_ref[me] = x_ref[...]
        sends, recvs = [], []
        for k, (dx, dy, dc) in enumerate(deltas):
            tx, ty, tc = _flip(x, dx), _flip(y, dy), _flip(c, dc)
            sends.append(pltpu.make_async_remote_copy(src_ref=x_ref, dst_ref=o_ref.at[me], send_sem=send_sems.at[k],
                                                      recv_sem=recv_sems.at[k], device_id=(tx, ty, tc), device_id_type=MESH))
            recvs.append(pltpu.make_async_remote_copy(src_ref=x_ref, dst_ref=o_ref.at[4 * tx + 2 * ty + tc],
                                                      send_sem=send_sems.at[k], recv_sem=recv_sems.at[k],
                                                      device_id=(tx, ty, tc), device_id_type=MESH))
        for cp in sends:
            cp.start()
        for cp in recvs:
            cp.wait_recv()
        for cp in sends:
            cp.wait_send()

    return pl.pallas_call(
        body, name=name, out_shape=jax.ShapeDtypeStruct((N_DEV,) + rows.shape, rows.dtype),
        in_specs=[VMEM] + [pl.BlockSpec(memory_space=pl.ANY)] * len(order), out_specs=VMEM,
        scratch_shapes=[pltpu.SemaphoreType.DMA((N_DEV - 1,)), pltpu.SemaphoreType.DMA((N_DEV - 1,))],
    )(rows, *order)


WEIGHTS = ("ada_w", "ada_b", "ln_g", "ln_b", "e_w_in", "gmlp_norm_g", "gmlp_norm_b", "gmlp_ws", "gmlp_bs", "pool_w",
           "pool_b", "pool_scale", "e_w_out", "o_w_in", "mla_q_norm_g", "mla_kv_norm_g", "mla_w_uq", "mla_w_uk",
           "mla_w_uv", "o_w_out")
SMALL = ("ln_g", "ln_b", "gmlp_norm_g", "gmlp_norm_b", "gmlp_bs", "pool_b", "pool_scale", "mla_kv_norm_g", "mla_q_norm_g")


def _pad_cols(v, n):
    return jnp.concatenate([v, jnp.zeros((v.shape[0], n - v.shape[1]), v.dtype)], axis=1) if n > v.shape[1] else v


def _halves(g):
    return g.reshape(g.shape[0], 2, g.shape[1] // 2, g.shape[2])


def kernel(x, c, positions, ada_w, ada_b, ln_g, ln_b, e_w_in, gmlp_norm_g, gmlp_norm_b, gmlp_ws, gmlp_bs, pool_w, pool_b, pool_scale, e_w_out, o_w_in, mla_q_norm_g, mla_kv_norm_g, mla_w_uq, mla_w_uk, mla_w_uv, o_w_out, loss_target, m_ada_w, m_ada_b, m_ln_g, m_ln_b, m_e_w_in, m_gmlp_norm_g, m_gmlp_norm_b, m_gmlp_ws, m_gmlp_bs, m_pool_w, m_pool_b, m_pool_scale, m_e_w_out, m_o_w_in, m_mla_q_norm_g, m_mla_kv_norm_g, m_mla_w_uq, m_mla_w_uk, m_mla_w_uv, m_o_w_out, v_ada_w, v_ada_b, v_ln_g, v_ln_b, v_e_w_in, v_gmlp_norm_g, v_gmlp_norm_b, v_gmlp_ws, v_gmlp_bs, v_pool_w, v_pool_b, v_pool_scale, v_e_w_out, v_o_w_in, v_mla_q_norm_g, v_mla_kv_norm_g, v_mla_w_uq, v_mla_w_uk, v_mla_w_uv, v_o_w_out):
    args = dict(locals())
    weights = {n: args[n] for n in WEIGHTS}
    mom = {n: args["m_" + n] for n in WEIGHTS}
    var = {n: args["v_" + n] for n in WEIGHTS}
    ax, ay, ac = lax.axis_index("x"), lax.axis_index("y"), lax.axis_index("c")
    chip = 2 * ax + ay
    dev = 2 * chip + ac
    d = D_MODEL
    x2 = x[0]
    target = loss_target[0]
    q_rank_sh = mla_q_norm_g.shape[1]

    empty_zone = lambda w: lax.dynamic_update_slice(lax.empty((N_CHIPS,) + w.shape, w.dtype), w[None], (chip, 0, 0))
    shards0 = [w.astype(BF16) for w in (pool_w[0].reshape(-1, POOL_GROUP_DIM), e_w_out[0])]
    shards1 = [w.astype(BF16) for w in (o_w_in[0], mla_w_uq[0].reshape(q_rank_sh, -1), o_w_out[0])]
    w_in0, = _gather_weights([e_w_in[0].astype(BF16)], name="gather_weights")
    wuk_hrd = jnp.transpose(mla_w_uk[0], (1, 0, 2)).astype(BF16)
    wuk_hdr = jnp.transpose(mla_w_uk[0], (1, 2, 0)).astype(BF16)
    wuv_hrv = jnp.transpose(mla_w_uv[0], (1, 0, 2)).astype(BF16)
    wuv_hvr = jnp.transpose(mla_w_uv[0], (1, 2, 0)).astype(BF16)
    ws = gmlp_ws[0]
    ws_t = jnp.transpose(ws, (0, 2, 1))
    bs_t = _pad_cols(gmlp_bs[0].T, LANES)

    inv = 1.0 / (ROPE_THETA ** (jnp.arange(0, MLA_ROPE, 2, dtype=F32) / MLA_ROPE))
    ang = positions[0].astype(F32)[:, None] * inv
    cos_t = jnp.tile(jnp.cos(ang), (1, 4))
    sin_t = jnp.concatenate([-jnp.sin(ang), -jnp.sin(ang), jnp.sin(ang), jnp.sin(ang)], axis=1)

    c_all = _all_gather_devices(c.reshape(8, LANES), after=w_in0, name="gather_c").reshape(N_DEV, d)
    cols = ada_w.shape[2]
    ada_b_mine = lax.dynamic_slice_in_dim(ada_b, chip * cols, cols, axis=1)[:, None, :]
    mod_sh = _ada_mod(c_all, ada_w, ada_b_mine, name="ada_mod")
    q_norm_rows = jnp.zeros((8, cols), F32).at[0, :q_rank_sh].set(mla_q_norm_g[0])
    mod_all = _all_gather_chips(jnp.concatenate([mod_sh.reshape(2 * N_DEV, cols), q_norm_rows]), name="gather_mod")
    q_norm_g = mod_all[:, 2 * N_DEV, :q_rank_sh].reshape(1, -1)
    mod_all = jnp.transpose(mod_all[:, :2 * N_DEV].reshape(N_CHIPS, 2, N_DEV, cols), (1, 2, 0, 3)).reshape(2, N_DEV, 3 * d)
    mod = lax.dynamic_index_in_dim(mod_all, dev, axis=1, keepdims=False)
    shift = [mod[l:l + 1, :d] for l in range(2)]
    scale = [mod[l:l + 1, d:2 * d] for l in range(2)]
    gate = [mod[l:l + 1, 2 * d:] for l in range(2)]
    flight0 = _chips_start(shards0, [empty_zone(w) for w in shards0], mod, scatter=False, name="gather0_start")
    flight1 = _chips_start(shards1, [empty_zone(w) for w in shards1], flight0[3], scatter=False, name="gather1_start")

    scale[0] = scale[0] + flight1[3][:1, :1]
    h0 = _modulate(x2, scale[0], shift[0], name="modulate0")
    proj0 = _matmul(h0, w_in0, b_stacked=True, tm=1024, tn=1280, out_dtype=BF16, name="proj0")
    pool_w_g, w_out0 = _chips_wait(*flight0[:3], proj0, scatter=False, name="gather0_wait")[1]
    pool_w_bf = jnp.transpose(pool_w_g.reshape(N_CHIPS, POOL_GROUPS, -1, POOL_GROUP_DIM), (1, 0, 2, 3)).reshape(
        POOL_GROUPS, POOL_GROUP_DIM, POOL_GROUP_DIM)
    w_out0 = w_out0.reshape(-1, d)
    mix0 = _even_fwd(proj0, ws, bs_t, gmlp_norm_g, gmlp_norm_b, pool_w_bf, pool_b, pool_scale, name="even_fwd")
    y0, x1, h1 = _out_resid_ln(mix0, w_out0, x2, gate[0], ln_g[0:1], ln_b[0:1], scale[1], shift[1], name="out0_ln")

    w_in1_g, w_uq_g, w_out1 = _chips_wait(*flight1[:3], h1, scatter=False, name="gather1_wait")[1]
    w_out1 = w_out1.reshape(-1, d)
    w_in1 = jnp.transpose(w_in1_g, (1, 0, 2)).reshape(d, ODD_IN)
    w_in1 = jnp.concatenate([w_in1[:, ODD_SMALL:], _pad_cols(w_in1[:, :ODD_SMALL], ODD_SMALL_PAD)], axis=1)
    w_uq = w_uq_g.reshape(MLA_Q_RANK, MLA_HEADS, MLA_NOPE + MLA_ROPE)
    w_uq_nope = w_uq[:, :, :MLA_NOPE].reshape(MLA_Q_RANK, -1)
    w_uq_rope = jnp.transpose(w_uq[:, :, MLA_NOPE:].reshape(MLA_Q_RANK, MLA_HEADS // 2, 2, 2, ROPE_HALF),
                              (0, 1, 3, 2, 4)).reshape(MLA_Q_RANK, -1)
    proj1 = _matmul(h1, w_in1, tm=1024, tn=1280, out_dtype=BF16, name="proj1")
    q_cn, keys = _mla_prep(proj1, q_norm_g, mla_kv_norm_g, cos_t, sin_t, name="mla_prep")
    q_nope = _matmul(q_cn, w_uq_nope, tm=1024, tn=2048, name="q_nope", out_dtype=BF16)
    q_rope_pre = _matmul(q_cn, w_uq_rope, tm=1024, name="q_rope")
    q = _q_build(q_nope, q_rope_pre, wuk_hdr, cos_t, sin_t, name="q_build")
    o_lat, lse = _attn_fwd(q, keys, name="attn_fwd")
    og = _o_build(o_lat, wuv_hrv, proj1, name="o_build")

    dy1, dres1, g_ln_g1, g_ln_b1, dgate1, loss = _out_loss_ln_bwd(
        og, w_out1, x1, gate[1], ln_g[1:2], ln_b[1:2], target, name="out1_loss_ln")
    dg1 = _matmul(dy1, w_out1, trans_b=True, tn=2048, out_dtype=BF16, name="d_og")
    g_w_out1 = _matmul(og, dy1, trans_a=True, out_dtype=BF16, tm=1024, tk=4096, name="g_out1")
    do_lat, dz, g_uv = _o_bwd(dg1, proj1, o_lat, wuv_hrv, wuv_hvr, name="o_bwd")
    dq, dkeys = _attn_bwd(q, keys, do_lat, o_lat, lse, name="attn_bwd")
    dq_all, g_uk = _q_bwd(dq, q_nope, wuk_hrd, cos_t, sin_t, name="q_bwd")
    n_grp = MLA_HEADS // Q_HEAD_GROUP
    w_uq_all = jnp.concatenate([w_uq_nope.reshape(MLA_Q_RANK, n_grp, -1), w_uq_rope.reshape(MLA_Q_RANK, n_grp, -1)],
                               axis=2).reshape(MLA_Q_RANK, -1)
    dq_cn = _matmul(dq_all, w_uq_all, trans_b=True, tm=1024, tk=3072, name="d_qcn")
    g_uq_all = _matmul(q_cn, dq_all, trans_a=True, out_dtype=BF16, name="g_uq").reshape(MLA_Q_RANK, n_grp, -1)
    g_uq_nope = g_uq_all[:, :, :Q_HEAD_GROUP * MLA_NOPE].reshape(MLA_Q_RANK, -1)
    g_uq_rope = g_uq_all[:, :, Q_HEAD_GROUP * MLA_NOPE:].reshape(MLA_Q_RANK, -1)
    dsmall, g_qg, g_kvg = _mla_prep_bwd(proj1, dq_cn, dkeys, q_norm_g, mla_kv_norm_g, cos_t, sin_t, name="mla_prep_bwd")
    dproj1 = jnp.concatenate([dz, dsmall], axis=1)
    g_w_in1 =_matmul(h1, dproj1, trans_a=True, out_dtype=BF16, tm=1024, tn=1280, name="g_in1")

    g_uq_rope = jnp.transpose(g_uq_rope.reshape(MLA_Q_RANK, MLA_HEADS // 2, 2, 2, ROPE_HALF), (0, 1, 3, 2, 4))
    g_uq = jnp.concatenate([g_uq_nope.reshape(MLA_Q_RANK, MLA_HEADS, MLA_NOPE), g_uq_rope.reshape(MLA_Q_RANK, MLA_HEADS, MLA_ROPE)], axis=2)
    g_w_in1 = jnp.concatenate([g_w_in1[:, MLA_WIDTH:MLA_WIDTH + ODD_SMALL], g_w_in1[:, :MLA_WIDTH]], axis=1)
    g_w_in1 = jnp.transpose(g_w_in1.reshape(d, N_CHIPS, -1), (1, 0, 2))
    big1 = [
        _halves(g_w_in1),
        _halves(g_uq.reshape(N_CHIPS, q_rank_sh, -1)),
        _halves(g_w_out1.reshape(N_CHIPS, -1, d)),
        _halves(g_uk.astype(BF16).reshape(N_CHIPS, -1, MLA_NOPE)),
        _halves(g_uv.astype(BF16).reshape(N_CHIPS, -1, MLA_V)),
    ]
    parts1 = _reduce_sibling(big1, name="reduce_sibling1")
    flight2 = _chips_start(parts1, [lax.empty(p.shape, BF16) for p in parts1], loss, scatter=True, name="reduce1_start")

    gate[0] = gate[0] + flight2[3][:1, :1]
    dy0, dres0, g_ln_g0, g_ln_b0, dgate0, dscale1, dshift1 = _dh_mid_ln_bwd(
        dproj1, w_in1, x2, y0, gate[0], ln_g[0:1], ln_b[0:1], dres1, scale[1], x1, name="d_h1_mid_ln")
    dmix0 = _matmul(dy0, w_out0, trans_b=True, tn=2048, out_dtype=BF16, name="d_mix0")
    g_w_out0 = _matmul(mix0, dy0, trans_a=True, out_dtype=BF16, tm=1024, tk=4096, name="g_out0")
    dproj0, g_ws, g_bs_t, g_ng, g_nb, g_pw, g_pb, g_ps = _even_bwd(
        proj0, dmix0, ws, ws_t, bs_t, gmlp_norm_g, gmlp_norm_b, pool_w_bf, pool_b, pool_scale, name="even_bwd")
    g_w_in0 = _matmul(h0, dproj0, trans_a=True, out_dtype=BF16, out_stacked=True, tm=1024, tn=1280, name="g_in0")

    g_pw = jnp.transpose(g_pw.astype(BF16).reshape(POOL_GROUPS, N_CHIPS, -1, POOL_GROUP_DIM), (1, 0, 2, 3))
    big0 = [
        _halves(g_w_in0),
        _halves(g_pw.reshape(N_CHIPS, -1, POOL_GROUP_DIM)),
        _halves(g_w_out0.reshape(N_CHIPS, -1, d)),
        _halves(g_ws.astype(BF16)),
    ]
    parts0 = _reduce_sibling(big0, name="reduce_sibling0")
    parts1, landed1 = _chips_wait(*flight2[:3], parts0[0], scatter=True, name="reduce1_wait")
    flight3 = _chips_start(parts0, [lax.empty(p.shape, BF16) for p in parts0], landed1[0], scatter=True, name="reduce0_start")
    grad_x, dscale0, dshift0 = _dh_input_bwd(dproj0, w_in0, x2, dres0, scale[0], after=flight3[3], name="d_h0_input")

    small_local = {
        "ln_g": jnp.concatenate([g_ln_g0, g_ln_g1]), "ln_b": jnp.concatenate([g_ln_b0, g_ln_b1]),
        "gmlp_norm_g": g_ng, "gmlp_norm_b": g_nb, "gmlp_bs": g_bs_t[:, :GMLP_HEADS].T, "pool_b": g_pb, "pool_scale": g_ps,
        "mla_kv_norm_g": g_kvg, "mla_q_norm_g": g_qg,
    }
    n_mod = 2 * 3 * d
    vec = jnp.concatenate([dshift0, dscale0, dgate0, dshift1, dscale1, dgate1]
                          + [small_local[n].reshape(1, -1) for n in SMALL] + [loss], axis=1)
    n_vec = vec.shape[1]
    vec = _pad_cols(vec, -(-n_vec // (8 * LANES)) * 8 * LANES).reshape(-1, LANES)
    vec_all = _all_gather_devices(vec, name="gather_small")
    vec_sum = _sum_devices(vec_all, name="sum_small").reshape(-1)
    dmod_all = vec_all.reshape(N_DEV, -1)[:, :n_mod].reshape(N_DEV, 2, 3 * d)
    dmod_sh = jnp.transpose(lax.dynamic_slice_in_dim(dmod_all, chip * cols, cols, axis=2), (1, 0, 2))
    dmod_sh = jnp.concatenate([dmod_sh, jnp.zeros((2, LANES - N_DEV, cols), F32)], axis=1)
    grads = {"ada_w": _ada_grad(_pad_cols(c_all.T, LANES), dmod_sh, name="ada_grad"), "ada_b": vec_sum[:n_mod].reshape(2, 3 * d)}
    off = n_mod
    for n in SMALL:
        sz = small_local[n].size
        grads[n] = vec_sum[off:off + sz]
        off += sz
    grads["mla_q_norm_g"] = lax.dynamic_slice_in_dim(grads["mla_q_norm_g"], chip * q_rank_sh, q_rank_sh)
    for n in SMALL:
        grads[n] = grads[n].reshape(weights[n].shape)

    parts0, landed0 = _chips_wait(*flight3[:3], grads["ada_w"], scatter=True, name="reduce0_wait")
    totals = _reduce_chips(list(parts0) + list(parts1), list(landed0) + list(landed1), name="reduce_chips")
    for n, t in zip(("e_w_in", "pool_w", "e_w_out", "gmlp_ws", "o_w_in", "mla_w_uq", "o_w_out"), totals):
        if n != "gmlp_ws":
            grads[n] = t.reshape(weights[n].shape)
    rep = jnp.concatenate([t.reshape(-1, LANES) for t in (totals[3], totals[7], totals[8])])
    rep_land = lax.dynamic_update_slice(lax.empty((N_CHIPS,) + rep.shape, F32), rep[None], (chip, 0, 0))
    flight4 = _chips_start([rep], [rep_land], totals[0], scatter=False, name="gather_rep_start")

    delta, new_m, new_v = {}, {}, {}
    replicated = ("gmlp_ws", "mla_w_uk", "mla_w_uv")
    large = [n for n in WEIGHTS if n not in SMALL and n != "ada_b"]
    for n in large:
        if n not in replicated:
            delta[n], new_m[n], new_v[n] = _adamw(weights[n], grads[n], mom[n], var[n], after=flight4[3], name="adamw_" + n)
    rep = _chips_wait(*flight4[:3], delta["e_w_in"], scatter=False, name="gather_rep_wait")[1][0]
    r_ws, r_uk = GMLP_BLOCK, 4 * MLA_KV_RANK
    grads["gmlp_ws"] = rep[:, :r_ws].reshape(weights["gmlp_ws"].shape)
    grads["mla_w_uk"] = jnp.transpose(rep[:, r_ws:r_ws + r_uk].reshape(MLA_HEADS, MLA_KV_RANK, MLA_NOPE), (1, 0, 2))[None]
    grads["mla_w_uv"] = jnp.transpose(rep[:, r_ws + r_uk:].reshape(MLA_HEADS, MLA_KV_RANK, MLA_V), (1, 0, 2))[None]
    for n in replicated:
        delta[n], new_m[n], new_v[n] = _adamw(weights[n], grads[n], mom[n], var[n], name="adamw_" + n)
    small = [n for n in WEIGHTS if n not in large]
    ds, ms, vs = _adamw_small([weights[n] for n in small], [grads[n] for n in small], [mom[n] for n in small],
                              [var[n] for n in small], name="adamw_small")
    for n, dn, mn, vn in zip(small, ds, ms, vs):
        delta[n], new_m[n], new_v[n] = dn, mn, vn

    return (vec_sum[n_vec - 1], grad_x[None], *[grads[n] for n in WEIGHTS], *[delta[n] for n in WEIGHTS],
            *[new_m[n] for n in WEIGHTS], *[new_v[n] for n in WEIGHTS])
```
